```python
import math
import jax
import jax.numpy as jnp
from jax import lax
import numpy as np

D_MODEL = 1024
BATCH = 8
SEQ = 2048
DEPTH = 1

CTX_LEN = 256
GRID_W = 64

ATTN_HEADS = 8
ATTN_KV_HEADS = 2
ATTN_GROUP = ATTN_HEADS // ATTN_KV_HEADS
HEAD_DIM = 128
ROPE_AXIS_DIM = HEAD_DIM // 2
ROPE_THETA = 10000.0
Q_BLOCK = 128

GDN_HEADS = 8
GDN_DK = 128
GDN_DV = 128
GDN_CHUNK = 64
SHORT_CONV = 3

D_FF = 2816
FFN_CONV = 3

NORM_EPS = 1e-6

ATTN_Q_W = ATTN_HEADS * HEAD_DIM
ATTN_KV_W = ATTN_KV_HEADS * HEAD_DIM
GDN_QK_W = GDN_HEADS * GDN_DK
GDN_V_W = GDN_HEADS * GDN_DV
GDN_CONV_W = 2 * GDN_QK_W + GDN_V_W
IN_SPLITS = (ATTN_KV_W, ATTN_KV_W, GDN_CONV_W, 2 * GDN_HEADS, 2 * GDN_HEADS, ATTN_Q_W, GDN_V_W, 2 * D_MODEL)
CTX_COLS = 2 * ATTN_KV_W + GDN_CONV_W + 4 * GDN_HEADS
IN_COLS = CTX_COLS + ATTN_Q_W + GDN_V_W + 2 * D_MODEL

kernel_name = 'hybrid_gqa_gdn_convffn_prefix_block'


def rms_norm(x, gain=None):
    x32 = x.astype(jnp.float32)
    y = x32 * lax.rsqrt(jnp.mean(x32 * x32, axis=-1, keepdims=True) + NORM_EPS)
    if gain is not None:
        y = y * gain.astype(jnp.float32)
    return y.astype(x.dtype)


def l2_normalize(x):
    return x * lax.rsqrt(jnp.sum(x * x, axis=-1, keepdims=True) + NORM_EPS)


def modulate(h, shift, scale):
    return h * (1.0 + scale) + shift


def split_cols(p):
    parts, off = [], 0
    for size in IN_SPLITS:
        if off >= p.shape[-1]:
            break
        parts.append(p[..., off:off + size])
        off += size
    return parts


def dwconv_centred(x, w, b=None):
    width = w.shape[0]
    pad = width // 2
    n = x.shape[1]
    xp = jnp.pad(x, ((0, 0), (pad, pad), (0, 0)))
    y = xp[:, 0:n] * w[0]
    for j in range(1, width):
        y = y + xp[:, j:j + n] * w[j]
    return y if b is None else y + b


def axial_rope_tables(n):
    rows = n // GRID_W
    row_ids = jnp.broadcast_to(jnp.arange(rows, dtype=jnp.float32)[:, None], (rows, GRID_W)).reshape(n)
    col_ids = jnp.broadcast_to(jnp.arange(GRID_W, dtype=jnp.float32)[None, :], (rows, GRID_W)).reshape(n)
    inv_freq = ROPE_THETA ** (-jnp.arange(0, ROPE_AXIS_DIM, 2, dtype=jnp.float32) / ROPE_AXIS_DIM)
    ang = jnp.concatenate([row_ids[:, None] * inv_freq, col_ids[:, None] * inv_freq], axis=-1)
    ang = ang.reshape(n, 2, ROPE_AXIS_DIM // 2)
    return jnp.cos(ang), jnp.sin(ang)


def apply_axial_rope(x, cos, sin):
    x32 = x.astype(jnp.float32).reshape(*x.shape[:-1], 2, 2, ROPE_AXIS_DIM // 2)
    x1, x2 = x32[..., 0, :], x32[..., 1, :]
    cos = cos[None, :, None]
    sin = sin[None, :, None]
    out = jnp.stack([x1 * cos - x2 * sin, x2 * cos + x1 * sin], axis=-2)
    return out.reshape(x.shape).astype(x.dtype)


def sdpa_block(q, k, v):
    s = jnp.einsum('bqhgd,bkhd->bhgqk', q, k).astype(jnp.float32) * (HEAD_DIM ** -0.5)
    p = jax.nn.softmax(s, axis=-1).astype(v.dtype)
    return jnp.einsum('bhgqk,bkhd->bqhgd', p, v)


def latent_attention(q, k_all, v_all):
    b, n = q.shape[:2]
    nb = n // Q_BLOCK
    qb = jnp.moveaxis(q.reshape(b, nb, Q_BLOCK, *q.shape[2:]), 1, 0)
    o = lax.map(lambda q_blk: sdpa_block(q_blk, k_all, v_all), qb)
    return jnp.moveaxis(o, 0, 1).reshape(b, n, ATTN_Q_W)


def gdn_chunked(q, k, v, log_a, beta, s0, with_output):
    b, h, n, dk = q.shape
    dv = v.shape[-1]
    nc = n // GDN_CHUNK
    q = q.reshape(b, h, nc, GDN_CHUNK, dk)
    k = k.reshape(b, h, nc, GDN_CHUNK, dk)
    v = v.reshape(b, h, nc, GDN_CHUNK, dv)
    log_a = log_a.reshape(b, h, nc, GDN_CHUNK)
    beta = beta.reshape(b, h, nc, GDN_CHUNK)
    gam = jnp.cumsum(log_a, axis=-1)
    idx = jnp.arange(GDN_CHUNK)
    strict = idx[:, None] > idx[None, :]
    incl = idx[:, None] >= idx[None, :]
    dec = jnp.exp(jnp.where(incl, gam[..., :, None] - gam[..., None, :], -jnp.inf))
    kk = jnp.einsum('bhncd,bhnsd->bhncs', k, k)
    a_mat = jnp.where(strict, beta[..., :, None] * dec * kk, 0.0) + jnp.eye(GDN_CHUNK, dtype=jnp.float32)
    rhs = jnp.concatenate([beta[..., None] * v, (beta * jnp.exp(gam))[..., None] * k], axis=-1)
    uw = lax.linalg.triangular_solve(a_mat, rhs, left_side=True, lower=True, unit_diagonal=True)
    u, w = uw[..., :dv], uw[..., dv:]
    k_dec = k * jnp.exp(gam[..., -1:] - gam)[..., None]
    g_last = jnp.exp(gam[..., -1])
    xs = [u, w, k_dec, g_last]
    if with_output:
        q_dec = q * jnp.exp(gam)[..., None]
        p = dec * jnp.einsum('bhncd,bhnsd->bhncs', q, k)
        xs = xs + [q_dec, p]
    xs = tuple(jnp.moveaxis(t, 2, 0) for t in xs)

    def step(s, inp):
        u_c, w_c, kd_c, gl_c = inp[:4]
        delta = u_c - jnp.einsum('bhcd,bhde->bhce', w_c, s)
        s_new = gl_c[..., None, None] * s + jnp.einsum('bhcd,bhce->bhde', kd_c, delta)
        if with_output:
            qd_c, p_c = inp[4], inp[5]
            o = jnp.einsum('bhcd,bhde->bhce', qd_c, s) + jnp.einsum('bhcs,bhse->bhce', p_c, delta)
            return s_new, o
        return s_new, None

    s_fin, o = lax.scan(step, s0, xs)
    if with_output:
        o = jnp.moveaxis(o, 0, 2).reshape(b, h, n, dv)
    return o, s_fin


def gdn_prepare(qkv, db, da, conv_w, a_log, dt_bias):
    b, n, _ = qkv.shape
    qkv = jax.nn.silu(dwconv_centred(qkv, conv_w)).astype(jnp.float32)
    q, k, v = jnp.split(qkv, [GDN_QK_W, 2 * GDN_QK_W], axis=-1)
    q = l2_normalize(q.reshape(b, n, GDN_HEADS, GDN_DK)) * (GDN_DK ** -0.5)
    k = l2_normalize(k.reshape(b, n, GDN_HEADS, GDN_DK))
    v = v.reshape(b, n, GDN_HEADS, GDN_DV)
    q, k, v = (t.transpose(0, 2, 1, 3) for t in (q, k, v))
    beta = jax.nn.sigmoid(db.astype(jnp.float32)).reshape(b, n, 2, GDN_HEADS).transpose(2, 0, 3, 1)
    da = da.astype(jnp.float32).reshape(b, n, 2, GDN_HEADS).transpose(2, 0, 3, 1)
    log_a = -jnp.exp(a_log.astype(jnp.float32))[:, None, :, None] * jax.nn.softplus(
        da + dt_bias.astype(jnp.float32)[:, None, :, None])
    return q, k, v, log_a, beta


def gdn_bidirectional(ctx_in, lat_in, with_ctx_output):
    qc, kc, vc, lac, bc = ctx_in
    ql, kl, vl, lal, bl = lat_in
    s0 = jnp.zeros((ql.shape[0], GDN_HEADS, GDN_DK, GDN_DV), jnp.float32)
    o_lat, o_ctx = None, None
    for d in range(2):
        rev = (lambda t: jnp.flip(t, axis=2)) if d == 1 else (lambda t: t)
        oc, sc = gdn_chunked(rev(qc), rev(kc), rev(vc), rev(lac[d]), rev(bc[d]), s0, with_ctx_output)
        ol, _ = gdn_chunked(rev(ql), rev(kl), rev(vl), rev(lal[d]), rev(bl[d]), sc, True)
        o_lat = rev(ol) if o_lat is None else o_lat + rev(ol)
        if with_ctx_output:
            o_ctx = rev(oc) if o_ctx is None else o_ctx + rev(oc)
    return o_lat, o_ctx


def gdn_output(o, z, norm_w):
    b, n = z.shape[:2]
    o = rms_norm(o.transpose(0, 2, 1, 3), norm_w)
    y = o * jax.nn.silu(z.astype(jnp.float32).reshape(b, n, GDN_HEADS, GDN_DV))
    return y.reshape(b, n, GDN_V_W).astype(z.dtype)


def merge_branches(attn, gdn, gates, w_pa, w_pd, w_out):
    g_a, g_d = jnp.split(gates, 2, axis=-1)
    y = jax.nn.sigmoid(g_a) * (attn @ w_pa) + jax.nn.sigmoid(g_d) * (gdn @ w_pd)
    return y @ w_out


def conv_ffn(h, w_up, conv_w, conv_b, w_down):
    u = dwconv_centred(h @ w_up, conv_w, conv_b)
    g, val = jnp.split(u, 2, axis=-1)
    return (jax.nn.silu(g) * val) @ w_down


def hybrid_layer(x, ctx, c, c_ctx, w_mod, b_mod, w_in, q_norm_w, k_norm_w, conv_qkv_w, a_log, dt_bias,
                 gdn_norm_w, w_pa, w_pd, w_out, w_up, ffn_conv_w, ffn_conv_b, w_down, update_ctx):
    b, n, _ = x.shape
    cl = ctx.shape[1]
    mod_lat = jax.nn.silu(c) @ w_mod + b_mod
    mod_ctx = jax.nn.silu(c_ctx) @ w_mod + b_mod
    sh1, sc1, g1, sh2, sc2, g2 = [m[:, None, :] for m in jnp.split(mod_lat, 6, axis=-1)]
    csh1, csc1, cg1, csh2, csc2, cg2 = jnp.split(mod_ctx, 6, axis=-1)

    hx = modulate(rms_norm(x), sh1, sc1)
    hc = modulate(rms_norm(ctx), csh1, csc1)
    ak_x, av_x, qkv_x, db_x, da_x, aq_x, z_x, gate_x = split_cols(hx @ w_in)
    ak_c, av_c, qkv_c, db_c, da_c, *rest_c = split_cols(hc @ (w_in if update_ctx else w_in[:, :CTX_COLS]))

    cos, sin = axial_rope_tables(n)
    q_x = apply_axial_rope(rms_norm(aq_x.reshape(b, n, ATTN_HEADS, HEAD_DIM), q_norm_w), cos, sin)
    q_x = q_x.reshape(b, n, ATTN_KV_HEADS, ATTN_GROUP, HEAD_DIM)
    k_x = apply_axial_rope(rms_norm(ak_x.reshape(b, n, ATTN_KV_HEADS, HEAD_DIM), k_norm_w), cos, sin)
    v_x = av_x.reshape(b, n, ATTN_KV_HEADS, HEAD_DIM)
    k_c = rms_norm(ak_c.reshape(b, cl, ATTN_KV_HEADS, HEAD_DIM), k_norm_w)
    v_c = av_c.reshape(b, cl, ATTN_KV_HEADS, HEAD_DIM)
    attn_x = latent_attention(q_x, jnp.concatenate([k_c, k_x], axis=1), jnp.concatenate([v_c, v_x], axis=1))

    gdn_x_in = gdn_prepare(qkv_x, db_x, da_x, conv_qkv_w, a_log, dt_bias)
    gdn_c_in = gdn_prepare(qkv_c, db_c, da_c, conv_qkv_w, a_log, dt_bias)
    o_x, o_c = gdn_bidirectional(gdn_c_in, gdn_x_in, update_ctx)
    gdn_x = gdn_output(o_x, z_x, gdn_norm_w)

    x = x + g1 * merge_branches(attn_x, gdn_x, gate_x, w_pa, w_pd, w_out)
    x = x + g2 * conv_ffn(modulate(rms_norm(x), sh2, sc2), w_up, ffn_conv_w, ffn_conv_b, w_down)

    if update_ctx:
        aq_c, z_c, gate_c = rest_c
        q_c = rms_norm(aq_c.reshape(b, cl, ATTN_HEADS, HEAD_DIM), q_norm_w)
        q_c = q_c.reshape(b, cl, ATTN_KV_HEADS, ATTN_GROUP, HEAD_DIM)
        attn_c = sdpa_block(q_c, k_c, v_c).reshape(b, cl, ATTN_Q_W)
        gdn_c = gdn_output(o_c, z_c, gdn_norm_w)
        ctx = ctx + cg1 * merge_branches(attn_c, gdn_c, gate_c, w_pa, w_pd, w_out)
        ctx = ctx + cg2 * conv_ffn(modulate(rms_norm(ctx), csh2, csc2), w_up, ffn_conv_w, ffn_conv_b, w_down)
    return x, ctx


def _fwd_setup_inputs(seed: int = 0) -> dict:
    key = jax.random.key(seed)
    ks = jax.random.split(key, 22)
    f32 = jnp.float32

    def dense(k, shape, fan_in, s=1.0):
        return s * (fan_in ** -0.5) * jax.random.normal(k, shape, f32)

    dt = jnp.exp(jax.random.uniform(ks[10], (DEPTH, 2, GDN_HEADS), f32,
                                    minval=math.log(1e-3), maxval=math.log(1e-1)))
    return {
        'x': jax.random.normal(ks[0], (BATCH, SEQ, D_MODEL), f32),
        'c': jax.random.normal(ks[1], (BATCH, D_MODEL), f32),
        'ctx': jax.random.normal(ks[2], (BATCH, CTX_LEN, D_MODEL), f32),
        'c_ctx': jax.random.normal(ks[3], (D_MODEL,), f32),
        'w_mod': dense(ks[4], (DEPTH, D_MODEL, 6 * D_MODEL), D_MODEL, 0.5),
        'b_mod': 0.02 * jax.random.normal(ks[5], (DEPTH, 6 * D_MODEL), f32),
        'w_in': dense(ks[6], (DEPTH, D_MODEL, IN_COLS), D_MODEL),
        'q_norm_w': 1.0 + 0.05 * jax.random.normal(ks[7], (DEPTH, HEAD_DIM), f32),
        'k_norm_w': 1.0 + 0.05 * jax.random.normal(ks[8], (DEPTH, HEAD_DIM), f32),
        'conv_qkv_w': dense(ks[9], (DEPTH, SHORT_CONV, GDN_CONV_W), SHORT_CONV),
        'a_log': jnp.log(jax.random.uniform(ks[11], (DEPTH, 2, GDN_HEADS), f32, minval=1.0, maxval=16.0)),
        'dt_bias': dt + jnp.log(-jnp.expm1(-dt)),
        'gdn_norm_w': 1.0 + 0.05 * jax.random.normal(ks[12], (DEPTH, GDN_DV), f32),
        'w_pa': dense(ks[13], (DEPTH, ATTN_Q_W, D_MODEL), ATTN_Q_W),
        'w_pd': dense(ks[14], (DEPTH, GDN_V_W, D_MODEL), GDN_V_W),
        'w_out': dense(ks[15], (DEPTH, D_MODEL, D_MODEL), D_MODEL),
        'w_up': dense(ks[16], (DEPTH, D_MODEL, 2 * D_FF), D_MODEL),
        'ffn_conv_w': dense(ks[17], (DEPTH, FFN_CONV, 2 * D_FF), FFN_CONV),
        'ffn_conv_b': 0.02 * jax.random.normal(ks[18], (DEPTH, 2 * D_FF), f32),
        'w_down': dense(ks[19], (DEPTH, D_FF, D_MODEL), D_FF),
        'final_norm_w': 1.0 + 0.05 * jax.random.normal(ks[20], (D_MODEL,), f32),
    }


def _fwd_reference(x, c, ctx, c_ctx, w_mod, b_mod, w_in, q_norm_w, k_norm_w, conv_qkv_w, a_log, dt_bias,
              gdn_norm_w, w_pa, w_pd, w_out, w_up, ffn_conv_w, ffn_conv_b, w_down, final_norm_w):
    for layer in range(DEPTH):
        x, ctx = hybrid_layer(
            x, ctx, c, c_ctx, w_mod[layer], b_mod[layer], w_in[layer], q_norm_w[layer], k_norm_w[layer],
            conv_qkv_w[layer], a_log[layer], dt_bias[layer], gdn_norm_w[layer], w_pa[layer], w_pd[layer],
            w_out[layer], w_up[layer], ffn_conv_w[layer], ffn_conv_b[layer], w_down[layer],
            update_ctx=layer < DEPTH - 1)
    return rms_norm(x, final_norm_w)


import jax as _jax
import jax.numpy as _jnp

TWIN_FORMAT = 'train_step'
FWD_PARAMS = ['x', 'c', 'ctx', 'c_ctx', 'w_mod', 'b_mod', 'w_in', 'q_norm_w', 'k_norm_w', 'conv_qkv_w', 'a_log', 'dt_bias', 'gdn_norm_w', 'w_pa', 'w_pd', 'w_out', 'w_up', 'ffn_conv_w', 'ffn_conv_b', 'w_down', 'final_norm_w']
TWIN_WEIGHTS = ['c_ctx', 'w_mod', 'b_mod', 'w_in', 'q_norm_w', 'k_norm_w', 'conv_qkv_w', 'a_log', 'dt_bias', 'gdn_norm_w', 'w_pa', 'w_pd', 'w_out', 'w_up', 'ffn_conv_w', 'ffn_conv_b', 'w_down', 'final_norm_w']
TWIN_DIFF_INPUT = 'x'
TWIN_INPUTS = ['x', 'c', 'ctx', 'c_ctx', 'w_mod', 'b_mod', 'w_in', 'q_norm_w', 'k_norm_w', 'conv_qkv_w', 'a_log', 'dt_bias', 'gdn_norm_w', 'w_pa', 'w_pd', 'w_out', 'w_up', 'ffn_conv_w', 'ffn_conv_b', 'w_down', 'final_norm_w', 'loss_target', 'm_c_ctx', 'm_w_mod', 'm_b_mod', 'm_w_in', 'm_q_norm_w', 'm_k_norm_w', 'm_conv_qkv_w', 'm_a_log', 'm_dt_bias', 'm_gdn_norm_w', 'm_w_pa', 'm_w_pd', 'm_w_out', 'm_w_up', 'm_ffn_conv_w', 'm_ffn_conv_b', 'm_w_down', 'm_final_norm_w', 'v_c_ctx', 'v_w_mod', 'v_b_mod', 'v_w_in', 'v_q_norm_w', 'v_k_norm_w', 'v_conv_qkv_w', 'v_a_log', 'v_dt_bias', 'v_gdn_norm_w', 'v_w_pa', 'v_w_pd', 'v_w_out', 'v_w_up', 'v_ffn_conv_w', 'v_ffn_conv_b', 'v_w_down', 'v_final_norm_w']
TWIN_OUTPUTS = ['loss', 'grad_x', 'grad_c_ctx', 'grad_w_mod', 'grad_b_mod', 'grad_w_in', 'grad_q_norm_w', 'grad_k_norm_w', 'grad_conv_qkv_w', 'grad_a_log', 'grad_dt_bias', 'grad_gdn_norm_w', 'grad_w_pa', 'grad_w_pd', 'grad_w_out', 'grad_w_up', 'grad_ffn_conv_w', 'grad_ffn_conv_b', 'grad_w_down', 'grad_final_norm_w', 'delta_c_ctx', 'delta_w_mod', 'delta_b_mod', 'delta_w_in', 'delta_q_norm_w', 'delta_k_norm_w', 'delta_conv_qkv_w', 'delta_a_log', 'delta_dt_bias', 'delta_gdn_norm_w', 'delta_w_pa', 'delta_w_pd', 'delta_w_out', 'delta_w_up', 'delta_ffn_conv_w', 'delta_ffn_conv_b', 'delta_w_down', 'delta_final_norm_w', 'new_m_c_ctx', 'new_m_w_mod', 'new_m_b_mod', 'new_m_w_in', 'new_m_q_norm_w', 'new_m_k_norm_w', 'new_m_conv_qkv_w', 'new_m_a_log', 'new_m_dt_bias', 'new_m_gdn_norm_w', 'new_m_w_pa', 'new_m_w_pd', 'new_m_w_out', 'new_m_w_up', 'new_m_ffn_conv_w', 'new_m_ffn_conv_b', 'new_m_w_down', 'new_m_final_norm_w', 'new_v_c_ctx', 'new_v_w_mod', 'new_v_b_mod', 'new_v_w_in', 'new_v_q_norm_w', 'new_v_k_norm_w', 'new_v_conv_qkv_w', 'new_v_a_log', 'new_v_dt_bias', 'new_v_gdn_norm_w', 'new_v_w_pa', 'new_v_w_pd', 'new_v_w_out', 'new_v_w_up', 'new_v_ffn_conv_w', 'new_v_ffn_conv_b', 'new_v_w_down', 'new_v_final_norm_w']
TWIN_LEAF_KINDS = {'loss': 'loss', 'grad_x': 'grad_x', 'grad_c_ctx': 'grad_w', 'grad_w_mod': 'grad_w', 'grad_b_mod': 'grad_w', 'grad_w_in': 'grad_w', 'grad_q_norm_w': 'grad_w', 'grad_k_norm_w': 'grad_w', 'grad_conv_qkv_w': 'grad_w', 'grad_a_log': 'grad_w', 'grad_dt_bias': 'grad_w', 'grad_gdn_norm_w': 'grad_w', 'grad_w_pa': 'grad_w', 'grad_w_pd': 'grad_w', 'grad_w_out': 'grad_w', 'grad_w_up': 'grad_w', 'grad_ffn_conv_w': 'grad_w', 'grad_ffn_conv_b': 'grad_w', 'grad_w_down': 'grad_w', 'grad_final_norm_w': 'grad_w', 'delta_c_ctx': 'delta_w', 'delta_w_mod': 'delta_w', 'delta_b_mod': 'delta_w', 'delta_w_in': 'delta_w', 'delta_q_norm_w': 'delta_w', 'delta_k_norm_w': 'delta_w', 'delta_conv_qkv_w': 'delta_w', 'delta_a_log': 'delta_w', 'delta_dt_bias': 'delta_w', 'delta_gdn_norm_w': 'delta_w', 'delta_w_pa': 'delta_w', 'delta_w_pd': 'delta_w', 'delta_w_out': 'delta_w', 'delta_w_up': 'delta_w', 'delta_ffn_conv_w': 'delta_w', 'delta_ffn_conv_b': 'delta_w', 'delta_w_down': 'delta_w', 'delta_final_norm_w': 'delta_w', 'new_m_c_ctx': 'new_m', 'new_m_w_mod': 'new_m', 'new_m_b_mod': 'new_m', 'new_m_w_in': 'new_m', 'new_m_q_norm_w': 'new_m', 'new_m_k_norm_w': 'new_m', 'new_m_conv_qkv_w': 'new_m', 'new_m_a_log': 'new_m', 'new_m_dt_bias': 'new_m', 'new_m_gdn_norm_w': 'new_m', 'new_m_w_pa': 'new_m', 'new_m_w_pd': 'new_m', 'new_m_w_out': 'new_m', 'new_m_w_up': 'new_m', 'new_m_ffn_conv_w': 'new_m', 'new_m_ffn_conv_b': 'new_m', 'new_m_w_down': 'new_m', 'new_m_final_norm_w': 'new_m', 'new_v_c_ctx': 'new_v', 'new_v_w_mod': 'new_v', 'new_v_b_mod': 'new_v', 'new_v_w_in': 'new_v', 'new_v_q_norm_w': 'new_v', 'new_v_k_norm_w': 'new_v', 'new_v_conv_qkv_w': 'new_v', 'new_v_a_log': 'new_v', 'new_v_dt_bias': 'new_v', 'new_v_gdn_norm_w': 'new_v', 'new_v_w_pa': 'new_v', 'new_v_w_pd': 'new_v', 'new_v_w_out': 'new_v', 'new_v_w_up': 'new_v', 'new_v_ffn_conv_w': 'new_v', 'new_v_ffn_conv_b': 'new_v', 'new_v_w_down': 'new_v', 'new_v_final_norm_w': 'new_v'}


def _forward(args):
    return _fwd_reference(*[args[k] for k in FWD_PARAMS])


def _output_shape():
    out = _jax.eval_shape(lambda: _forward(_fwd_setup_inputs(0)))
    return out.shape, out.dtype

N_MICROBATCH = 1
ADAM_LR = 0.001
ADAM_B1 = 0.9
ADAM_B2 = 0.999
ADAM_EPS = 1e-08
ADAM_WD = 0.01
ADAM_STEP = 10
PER_EXAMPLE_BATCH_AXIS = {'x': 0, 'c': 0, 'ctx': 0, 'loss_target': 0}
SHARED_INPUTS = []
_WEIGHT_DTYPES = {'c_ctx': _jnp.float32, 'w_mod': _jnp.float32, 'b_mod': _jnp.float32, 'w_in': _jnp.float32, 'q_norm_w': _jnp.float32, 'k_norm_w': _jnp.float32, 'conv_qkv_w': _jnp.float32, 'a_log': _jnp.float32, 'dt_bias': _jnp.float32, 'gdn_norm_w': _jnp.float32, 'w_pa': _jnp.float32, 'w_pd': _jnp.float32, 'w_out': _jnp.float32, 'w_up': _jnp.float32, 'ffn_conv_w': _jnp.float32, 'ffn_conv_b': _jnp.float32, 'w_down': _jnp.float32, 'final_norm_w': _jnp.float32}
MOMENT_SCALE = {'c_ctx': 3.473190e-03, 'w_mod': 3.883999e-02, 'b_mod': 6.701305e-02, 'w_in': 9.061312e-03, 'q_norm_w': 5.448927e-03, 'k_norm_w': 5.463688e-03, 'conv_qkv_w': 1.076402e-02, 'a_log': 3.435875e-02, 'dt_bias': 3.265505e-02, 'gdn_norm_w': 5.235156e-02, 'w_pa': 6.098525e-03, 'w_pd': 1.446503e-02, 'w_out': 1.569235e-02, 'w_up': 1.740682e-02, 'ffn_conv_w': 1.720822e-02, 'ffn_conv_b': 1.572280e-02, 'w_down': 2.848334e-02, 'final_norm_w': 1.606866e+01}


def _to_microbatches(a, axis):
    t = _jnp.moveaxis(a, axis, 0)
    t = t.reshape((N_MICROBATCH, t.shape[0] // N_MICROBATCH) + t.shape[1:])
    return _jnp.moveaxis(t, 1, axis + 1)


def setup_inputs(seed: int = 0) -> dict:
    inp = _fwd_setup_inputs(seed)
    key = _jax.random.fold_in(_jax.random.key(seed), 7919)
    shape, _ = _output_shape()
    out = dict(inp)
    out["loss_target"] = _jax.random.normal(_jax.random.fold_in(key, 0), shape, _jnp.float32)
    for i, name in enumerate(TWIN_WEIGHTS):
        w = inp[name].astype(_jnp.float32)
        if MOMENT_SCALE is None:
            s = _jnp.sqrt(_jnp.mean(_jnp.square(w)) + 1e-30)
        else:
            s = MOMENT_SCALE[name]
        km, kv = _jax.random.split(_jax.random.fold_in(key, i + 1))
        out[name] = w
        out["m_" + name] = s * _jax.random.normal(km, w.shape, _jnp.float32)
        out["v_" + name] = (s * s) * _jax.random.uniform(kv, w.shape, _jnp.float32, 0.5, 1.5)
    if N_MICROBATCH > 1:
        for name, axis in PER_EXAMPLE_BATCH_AXIS.items():
            out[name] = _to_microbatches(out[name], axis)
    return {'x': out['x'], 'c': out['c'], 'ctx': out['ctx'], 'c_ctx': out['c_ctx'], 'w_mod': out['w_mod'], 'b_mod': out['b_mod'], 'w_in': out['w_in'], 'q_norm_w': out['q_norm_w'], 'k_norm_w': out['k_norm_w'], 'conv_qkv_w': out['conv_qkv_w'], 'a_log': out['a_log'], 'dt_bias': out['dt_bias'], 'gdn_norm_w': out['gdn_norm_w'], 'w_pa': out['w_pa'], 'w_pd': out['w_pd'], 'w_out': out['w_out'], 'w_up': out['w_up'], 'ffn_conv_w': out['ffn_conv_w'], 'ffn_conv_b': out['ffn_conv_b'], 'w_down': out['w_down'], 'final_norm_w': out['final_norm_w'], 'loss_target': out['loss_target'], 'm_c_ctx': out['m_c_ctx'], 'm_w_mod': out['m_w_mod'], 'm_b_mod': out['m_b_mod'], 'm_w_in': out['m_w_in'], 'm_q_norm_w': out['m_q_norm_w'], 'm_k_norm_w': out['m_k_norm_w'], 'm_conv_qkv_w': out['m_conv_qkv_w'], 'm_a_log': out['m_a_log'], 'm_dt_bias': out['m_dt_bias'], 'm_gdn_norm_w': out['m_gdn_norm_w'], 'm_w_pa': out['m_w_pa'], 'm_w_pd': out['m_w_pd'], 'm_w_out': out['m_w_out'], 'm_w_up': out['m_w_up'], 'm_ffn_conv_w': out['m_ffn_conv_w'], 'm_ffn_conv_b': out['m_ffn_conv_b'], 'm_w_down': out['m_w_down'], 'm_final_norm_w': out['m_final_norm_w'], 'v_c_ctx': out['v_c_ctx'], 'v_w_mod': out['v_w_mod'], 'v_b_mod': out['v_b_mod'], 'v_w_in': out['v_w_in'], 'v_q_norm_w': out['v_q_norm_w'], 'v_k_norm_w': out['v_k_norm_w'], 'v_conv_qkv_w': out['v_conv_qkv_w'], 'v_a_log': out['v_a_log'], 'v_dt_bias': out['v_dt_bias'], 'v_gdn_norm_w': out['v_gdn_norm_w'], 'v_w_pa': out['v_w_pa'], 'v_w_pd': out['v_w_pd'], 'v_w_out': out['v_w_out'], 'v_w_up': out['v_w_up'], 'v_ffn_conv_w': out['v_ffn_conv_w'], 'v_ffn_conv_b': out['v_ffn_conv_b'], 'v_w_down': out['v_w_down'], 'v_final_norm_w': out['v_final_norm_w']}


def _loss(weights, diff, rest, loss_target):
    with _jax.named_scope("forward"):
        args = {**rest, TWIN_DIFF_INPUT: diff, **{k: w.astype(_WEIGHT_DTYPES[k]) for k, w in weights.items()}}
        y = _forward(args)
    with _jax.named_scope("loss_head"):
        err = _jnp.square(y.astype(_jnp.float32) - loss_target)
        return 0.5 * _jnp.sum(_jnp.mean(err, axis=-1)) if err.ndim else 0.5 * err


def _adamw(w, g, m, v):
    m = ADAM_B1 * m + (1.0 - ADAM_B1) * g
    v = ADAM_B2 * v + (1.0 - ADAM_B2) * _jnp.square(g)
    m_hat = m / (1.0 - ADAM_B1 ** ADAM_STEP)
    v_hat = v / (1.0 - ADAM_B2 ** ADAM_STEP)
    delta = -ADAM_LR * (m_hat / (_jnp.sqrt(v_hat) + ADAM_EPS) + ADAM_WD * w)
    return delta, m, v


def reference(x, c, ctx, c_ctx, w_mod, b_mod, w_in, q_norm_w, k_norm_w, conv_qkv_w, a_log, dt_bias, gdn_norm_w, w_pa, w_pd, w_out, w_up, ffn_conv_w, ffn_conv_b, w_down, final_norm_w, loss_target, m_c_ctx, m_w_mod, m_b_mod, m_w_in, m_q_norm_w, m_k_norm_w, m_conv_qkv_w, m_a_log, m_dt_bias, m_gdn_norm_w, m_w_pa, m_w_pd, m_w_out, m_w_up, m_ffn_conv_w, m_ffn_conv_b, m_w_down, m_final_norm_w, v_c_ctx, v_w_mod, v_b_mod, v_w_in, v_q_norm_w, v_k_norm_w, v_conv_qkv_w, v_a_log, v_dt_bias, v_gdn_norm_w, v_w_pa, v_w_pd, v_w_out, v_w_up, v_ffn_conv_w, v_ffn_conv_b, v_w_down, v_final_norm_w):
    given = dict(x=x, c=c, ctx=ctx, c_ctx=c_ctx, w_mod=w_mod, b_mod=b_mod, w_in=w_in, q_norm_w=q_norm_w, k_norm_w=k_norm_w, conv_qkv_w=conv_qkv_w, a_log=a_log, dt_bias=dt_bias, gdn_norm_w=gdn_norm_w, w_pa=w_pa, w_pd=w_pd, w_out=w_out, w_up=w_up, ffn_conv_w=ffn_conv_w, ffn_conv_b=ffn_conv_b, w_down=w_down, final_norm_w=final_norm_w, loss_target=loss_target, m_c_ctx=m_c_ctx, m_w_mod=m_w_mod, m_b_mod=m_b_mod, m_w_in=m_w_in, m_q_norm_w=m_q_norm_w, m_k_norm_w=m_k_norm_w, m_conv_qkv_w=m_conv_qkv_w, m_a_log=m_a_log, m_dt_bias=m_dt_bias, m_gdn_norm_w=m_gdn_norm_w, m_w_pa=m_w_pa, m_w_pd=m_w_pd, m_w_out=m_w_out, m_w_up=m_w_up, m_ffn_conv_w=m_ffn_conv_w, m_ffn_conv_b=m_ffn_conv_b, m_w_down=m_w_down, m_final_norm_w=m_final_norm_w, v_c_ctx=v_c_ctx, v_w_mod=v_w_mod, v_b_mod=v_b_mod, v_w_in=v_w_in, v_q_norm_w=v_q_norm_w, v_k_norm_w=v_k_norm_w, v_conv_qkv_w=v_conv_qkv_w, v_a_log=v_a_log, v_dt_bias=v_dt_bias, v_gdn_norm_w=v_gdn_norm_w, v_w_pa=v_w_pa, v_w_pd=v_w_pd, v_w_out=v_w_out, v_w_up=v_w_up, v_ffn_conv_w=v_ffn_conv_w, v_ffn_conv_b=v_ffn_conv_b, v_w_down=v_w_down, v_final_norm_w=v_final_norm_w)
    weights = {n: given[n] for n in TWIN_WEIGHTS}
    shared = {n: given[n] for n in SHARED_INPUTS}
    per_example = {n: given[n] for n in ['x', 'c', 'ctx']}
    grad_fn = _jax.value_and_grad(_loss, argnums=(0, 1))

    def one_microbatch(ex, loss_target):
        ex = dict(ex)
        diff = ex.pop(TWIN_DIFF_INPUT)
        return grad_fn(weights, diff, {**shared, **ex}, loss_target)

    if N_MICROBATCH == 1:
        loss, (grad_w, grad_x) = one_microbatch(per_example, given["loss_target"])
    else:
        def body(carry, xs):
            loss_sum, grad_sum = carry
            l_k, (gw_k, gx_k) = one_microbatch(xs[0], xs[1])
            with _jax.named_scope("update"):
                return (loss_sum + l_k, _jax.tree.map(_jnp.add, grad_sum, gw_k)), gx_k

        init = (_jnp.zeros((), _jnp.float32), _jax.tree.map(_jnp.zeros_like, weights))
        (loss, grad_w), grad_x = _jax.lax.scan(body, init, (per_example, given["loss_target"]))
    with _jax.named_scope("update"):
        delta_w, new_m, new_v = {}, {}, {}
        for n in TWIN_WEIGHTS:
            delta_w[n], new_m[n], new_v[n] = _adamw(weights[n], grad_w[n], given["m_" + n], given["v_" + n])
    return (loss, grad_x, *[grad_w[n] for n in TWIN_WEIGHTS], *[delta_w[n] for n in TWIN_WEIGHTS],
            *[new_m[n] for n in TWIN_WEIGHTS], *[new_v[n] for n in TWIN_WEIGHTS])
```

```python
import functools
import math

import jax
import jax.numpy as jnp
from jax import lax
from jax.experimental import pallas as pl
from jax.experimental.pallas import tpu as pltpu

F32 = jnp.float32
BF16 = jnp.bfloat16
HI = lax.Precision.HIGHEST
MESH = pl.DeviceIdType.MESH

NDEV = 8
D = 1024
HD = 128
AH, AKV, GRP = 8, 2, 4
GH = 8
CH = 64
DFF = 2816
GRID_W = 64
EPS = 1e-6
ROPE_THETA = 10000.0
C_KV, C_QKV, C_BL, C_AQ, C_Z, C_GATE, C_END = 0, 512, 3584, 4096, 5120, 6144, 8192
W_BL, W_AQ, W_END = 3584, 3616, 7712
LR, B1, B2, AEPS, WD, STEP = 0.001, 0.9, 0.999, 1e-08, 0.01, 10
VMEM_BIG = 56 * 1024 * 1024


def _call(body, *, name, out_shape, grid=None, in_specs=None, out_specs=None, scratch=(), sem=None,
          vmem=None, aliases=None):
    params = {}
    if sem is not None:
        params["dimension_semantics"] = sem
    if vmem is not None:
        params["vmem_limit_bytes"] = vmem
    kw = {}
    if grid is not None:
        kw["grid"] = grid
    if in_specs is not None:
        kw["in_specs"] = in_specs
    if out_specs is not None:
        kw["out_specs"] = out_specs
    if aliases:
        kw["input_output_aliases"] = aliases
    return pl.pallas_call(body, name=name, out_shape=out_shape, scratch_shapes=list(scratch),
                          compiler_params=pltpu.CompilerParams(**params), **kw)


def _sds(shape, dtype=F32):
    return jax.ShapeDtypeStruct(tuple(shape), dtype)


def _dot(a, b, ca, cb):
    return lax.dot_general(a.astype(BF16), b.astype(BF16), (((ca,), (cb,)), ((), ())),
                           preferred_element_type=F32)


@jax.custom_vjp
def _nn(a, b):
    return _dot(a, b, 1, 0)


@jax.custom_vjp
def _nt(a, b):
    return _dot(a, b, 1, 1)


@jax.custom_vjp
def _tn(a, b):
    return _dot(a, b, 0, 0)


_nn.defvjp(lambda a, b: (_nn(a, b), (a, b)), lambda r, g: (_nt(g, r[1]), _tn(r[0], g)))
_nt.defvjp(lambda a, b: (_nt(a, b), (a, b)), lambda r, g: (_nn(g, r[1]), _tn(g, r[0])))
_tn.defvjp(lambda a, b: (_tn(a, b), (a, b)), lambda r, g: (_nt(r[1], g), _nn(r[0], g)))


def _hdot(a, b):
    return jnp.dot(a, b, precision=HI, preferred_element_type=F32)


def _row_ids(shape):
    return lax.broadcasted_iota(jnp.int32, shape, 0)


def _shift_rows(x, down, bounds):
    n = x.shape[0]
    rows = _row_ids(x.shape)
    y = pltpu.roll(x, 1 if down else n - 1, 0)
    edge = functools.reduce(jnp.logical_or, [rows == (s if down else e - 1) for s, e in bounds])
    return jnp.where(edge, 0.0, y)


def _make_shift(bounds):
    @jax.custom_vjp
    def down(x):
        return _shift_rows(x, True, bounds)

    @jax.custom_vjp
    def up(x):
        return _shift_rows(x, False, bounds)

    down.defvjp(lambda x: (down(x), None), lambda _, g: (up(g),))
    up.defvjp(lambda x: (up(x), None), lambda _, g: (down(g),))
    return down, up


@jax.custom_vjp
def _swap32(x):
    lane = lax.broadcasted_iota(jnp.int32, x.shape, x.ndim - 1)
    return jnp.where((lane % 64) < 32, pltpu.roll(x, HD - 32, x.ndim - 1), pltpu.roll(x, 32, x.ndim - 1))


_swap32.defvjp(lambda x: (_swap32(x), None), lambda _, g: (_swap32(g),))


def _rms(x):
    return x * lax.rsqrt(jnp.mean(x * x, axis=-1, keepdims=True) + EPS)


def _silu(x):
    return x * jax.nn.sigmoid(x)


def _mm(a, b, *, name, M, N, K, ta=False, tb=False, out_dtype=F32, bm=None, bn=None, bk=None,
        a_off=(0, 0), b_off=(0, 0)):
    bm, bn, bk = bm or M, bn or N, bk or K
    assert M % bm == 0 and N % bn == 0 and K % bk == 0, (name, M, N, K, bm, bn, bk)
    nk = K // bk
    ca, cb = (0 if ta else 1), (1 if tb else 0)

    def body(a_ref, b_ref, o_ref, *acc):
        r = _dot(a_ref[...], b_ref[...], ca, cb)
        if nk == 1:
            o_ref[...] = r.astype(out_dtype)
        else:
            acc_ref, = acc
            k = pl.program_id(2)

            @pl.when(k == 0)
            def _():
                acc_ref[...] = r

            @pl.when(k > 0)
            def _():
                acc_ref[...] += r

            @pl.when(k == nk - 1)
            def _():
                o_ref[...] = acc_ref[...].astype(out_dtype)

    def blk(off, bshape):
        assert off[0] % bshape[0] == 0 and off[1] % bshape[1] == 0, (name, off, bshape)
        return off[0] // bshape[0], off[1] // bshape[1]

    if ta:
        ao = blk(a_off, (bk, bm))
        a_spec = pl.BlockSpec((bk, bm), lambda i, j, k: (k + ao[0], i + ao[1]))
    else:
        ao = blk(a_off, (bm, bk))
        a_spec = pl.BlockSpec((bm, bk), lambda i, j, k: (i + ao[0], k + ao[1]))
    if tb:
        bo = blk(b_off, (bn, bk))
        b_spec = pl.BlockSpec((bn, bk), lambda i, j, k: (j + bo[0], k + bo[1]))
    else:
        bo = blk(b_off, (bk, bn))
        b_spec = pl.BlockSpec((bk, bn), lambda i, j, k: (k + bo[0], j + bo[1]))
    return _call(body, name=name, out_shape=_sds((M, N), out_dtype), grid=(M // bm, N // bn, nk),
                 in_specs=[a_spec, b_spec], out_specs=pl.BlockSpec((bm, bn), lambda i, j, k: (i, j)),
                 scratch=[pltpu.VMEM((bm, bn), F32)] if nk > 1 else [],
                 sem=("parallel", "parallel", "arbitrary"), vmem=VMEM_BIG)(a, b)


def _normmod_fn(x, sh, sc):
    return _rms(x) * (1.0 + sc) + sh


def _normmod_fwd(x, mod, i_sh, i_sc, *, name, br=256):
    R = x.shape[0]

    def body(x_ref, mod_ref, o_ref):
        o_ref[...] = _normmod_fn(x_ref[...], mod_ref[i_sh:i_sh + 1, :], mod_ref[i_sc:i_sc + 1, :]).astype(BF16)

    return _call(body, name=name, out_shape=_sds((R, D), BF16), grid=(R // br,),
                 in_specs=[pl.BlockSpec((br, D), lambda i: (i, 0)), pl.BlockSpec((6, D), lambda i: (0, 0))],
                 out_specs=pl.BlockSpec((br, D), lambda i: (i, 0)), sem=("parallel",))(x, mod)


def _normmod_bwd(x, mod, i_sh, i_sc, dh, dh_off, res, *, name, br=256):
    R = x.shape[0]
    ob = dh_off // br
    has_res = res is not None

    def body(x_ref, mod_ref, dh_ref, *rest):
        if has_res:
            res_ref, dx_ref, dsh_ref, dsc_ref = rest
        else:
            dx_ref, dsh_ref, dsc_ref = rest
        sh, sc = mod_ref[i_sh:i_sh + 1, :], mod_ref[i_sc:i_sc + 1, :]
        _, vjp = jax.vjp(_normmod_fn, x_ref[...], sh, sc)
        dx, dsh, dsc = vjp(dh_ref[...])
        dx_ref[...] = dx + res_ref[...] if has_res else dx

        @pl.when(pl.program_id(0) == 0)
        def _():
            dsh_ref[...] = jnp.zeros_like(dsh_ref)
            dsc_ref[...] = jnp.zeros_like(dsc_ref)

        dsh_ref[...] += dsh
        dsc_ref[...] += dsc

    row = pl.BlockSpec((br, D), lambda i: (i, 0))
    vec = pl.BlockSpec((1, D), lambda i: (0, 0))
    ins = [row, pl.BlockSpec((6, D), lambda i: (0, 0)), pl.BlockSpec((br, D), lambda i: (i + ob, 0))]
    args = [x, mod, dh]
    if has_res:
        ins.append(row)
        args.append(res)
    return _call(body, name=name, out_shape=(_sds((R, D)), _sds((1, D)), _sds((1, D))), grid=(R // br,),
                 in_specs=ins, out_specs=(row, vec, vec), sem=("arbitrary",))(*args)


def _rope(x, cos, sin):
    return x * cos + _swap32(x) * sin


def _aprep_fn(qs, ks, cos, sin, qw, kw):
    return ([_rope(_rms(q) * qw, cos, sin) for q in qs], [_rope(_rms(k) * kw, cos, sin) for k in ks])


def _aprep_fwd(proj, cos, sin, qw, kw, *, br=256):
    T = proj.shape[0]

    def body(aq_ref, kv_ref, cos_ref, sin_ref, qw_ref, kw_ref, q_ref, k_ref, v_ref):
        qs = [aq_ref[:, h * HD:(h + 1) * HD] for h in range(AH)]
        ks = [kv_ref[:, h * HD:(h + 1) * HD] for h in range(AKV)]
        qo, ko = _aprep_fn(qs, ks, cos_ref[...], sin_ref[...], qw_ref[...], kw_ref[...])
        for h in range(AH):
            q_ref[h] = qo[h].astype(BF16)
        for h in range(AKV):
            k_ref[h] = ko[h].astype(BF16)
            v_ref[h] = kv_ref[:, (AKV + h) * HD:(AKV + h + 1) * HD].astype(BF16)

    tab = pl.BlockSpec((br, HD), lambda i: (i, 0))
    vec = pl.BlockSpec((1, HD), lambda i: (0, 0))
    return _call(body, name="aprep_fwd",
                 out_shape=(_sds((AH, T, HD), BF16), _sds((AKV, T, HD), BF16), _sds((AKV, T, HD), BF16)),
                 grid=(T // br,),
                 in_specs=[pl.BlockSpec((br, AH * HD), lambda i: (i, C_AQ // (AH * HD))),
                           pl.BlockSpec((br, 2 * AKV * HD), lambda i: (i, 0)), tab, tab, vec, vec],
                 out_specs=(pl.BlockSpec((AH, br, HD), lambda i: (0, i, 0)),
                            pl.BlockSpec((AKV, br, HD), lambda i: (0, i, 0)),
                            pl.BlockSpec((AKV, br, HD), lambda i: (0, i, 0))),
                 sem=("parallel",))(proj, proj, cos, sin, qw, kw)


def _aprep_bwd(proj, cos, sin, qw, kw, dq, dk, dv, L, *, br=256):
    T = proj.shape[0]
    lb = L // br

    def body(aq_ref, kv_ref, cos_ref, sin_ref, qw_ref, kw_ref, dq_ref, dk_ref, dv_ref,
             daq_ref, dkv_ref, dqw_ref, dkw_ref):
        i = pl.program_id(0)
        qs = [aq_ref[:, h * HD:(h + 1) * HD] for h in range(AH)]
        ks = [kv_ref[:, h * HD:(h + 1) * HD] for h in range(AKV)]
        _, vjp = jax.vjp(_aprep_fn, qs, ks, cos_ref[...], sin_ref[...], qw_ref[...], kw_ref[...])
        is_lat = i >= lb
        dqs = [jnp.where(is_lat, dq_ref[h], 0.0) for h in range(AH)]
        dks = [dk_ref[h] for h in range(AKV)]
        gq, gk, _, _, gqw, gkw = vjp((dqs, dks))
        for h in range(AH):
            daq_ref[:, h * HD:(h + 1) * HD] = gq[h].astype(BF16)
        for h in range(AKV):
            dkv_ref[:, h * HD:(h + 1) * HD] = gk[h].astype(BF16)
            dkv_ref[:, (AKV + h) * HD:(AKV + h + 1) * HD] = dv_ref[h].astype(BF16)

        @pl.when(i == 0)
        def _():
            dqw_ref[...] = jnp.zeros_like(dqw_ref)
            dkw_ref[...] = jnp.zeros_like(dkw_ref)

        dqw_ref[...] += gqw
        dkw_ref[...] += gkw

    tab = pl.BlockSpec((br, HD), lambda i: (i, 0))
    vec = pl.BlockSpec((1, HD), lambda i: (0, 0))
    kvb = pl.BlockSpec((AKV, br, HD), lambda i: (0, i, 0))
    return _call(body, name="aprep_bwd",
                 out_shape=(_sds((T, AH * HD), BF16), _sds((T, 2 * AKV * HD), BF16), _sds((1, HD)), _sds((1, HD))),
                 grid=(T // br,),
                 in_specs=[pl.BlockSpec((br, AH * HD), lambda i: (i, C_AQ // (AH * HD))),
                           pl.BlockSpec((br, 2 * AKV * HD), lambda i: (i, 0)), tab, tab, vec, vec,
                           pl.BlockSpec((AH, br, HD), lambda i: (0, jnp.maximum(i - lb, 0), 0)), kvb, kvb],
                 out_specs=(pl.BlockSpec((br, AH * HD), lambda i: (i, 0)),
                            pl.BlockSpec((br, 2 * AKV * HD), lambda i: (i, 0)), vec, vec),
                 sem=("arbitrary",))(proj, proj, cos, sin, qw, kw, dq, dk, dv)


def _attn_fn(q, k, v):
    s = _nt(q, k) * (HD ** -0.5)
    m = lax.stop_gradient(jnp.max(s, axis=-1, keepdims=True))
    e = jnp.exp(s - m)
    p = e / jnp.sum(e, axis=-1, keepdims=True)
    return _nn(p, v)


def _attn_fwd(q, k, v, L, *, bq=128):
    T = q.shape[1]
    N = T - L
    lb = L // bq

    def body(q_ref, k_ref, v_ref, o_ref):
        qv = q_ref[...].reshape(GRP * bq, HD).astype(F32)
        o = _attn_fn(qv, k_ref[...].astype(F32), v_ref[...].astype(F32))
        for g in range(GRP):
            o_ref[:, g * HD:(g + 1) * HD] = o[g * bq:(g + 1) * bq].astype(BF16)

    kvb = pl.BlockSpec((None, T, HD), lambda g, i: (g, 0, 0))
    return _call(body, name="attn_fwd", out_shape=_sds((N, AH * HD), BF16), grid=(AKV, N // bq),
                 in_specs=[pl.BlockSpec((GRP, bq, HD), lambda g, i: (g, i + lb, 0)), kvb, kvb],
                 out_specs=pl.BlockSpec((bq, GRP * HD), lambda g, i: (i, g)),
                 sem=("parallel", "parallel"), vmem=VMEM_BIG)(q, k, v)


def _attn_bwd(q, k, v, do, L, *, bq=128):
    T = q.shape[1]
    N = T - L
    lb = L // bq

    def body(q_ref, k_ref, v_ref, do_ref, dq_ref, dk_ref, dv_ref):
        qv = q_ref[...].reshape(GRP * bq, HD).astype(F32)
        _, vjp = jax.vjp(_attn_fn, qv, k_ref[...].astype(F32), v_ref[...].astype(F32))
        dov = jnp.concatenate([do_ref[:, g * HD:(g + 1) * HD] for g in range(GRP)], axis=0)
        dq, dk, dv = vjp(dov)
        dq_ref[...] = dq.reshape(GRP, bq, HD)

        @pl.when(pl.program_id(1) == 0)
        def _():
            dk_ref[...] = jnp.zeros_like(dk_ref)
            dv_ref[...] = jnp.zeros_like(dv_ref)

        dk_ref[...] += dk
        dv_ref[...] += dv

    kvb = pl.BlockSpec((None, T, HD), lambda g, i: (g, 0, 0))
    return _call(body, name="attn_bwd",
                 out_shape=(_sds((AH, N, HD)), _sds((AKV, T, HD)), _sds((AKV, T, HD))), grid=(AKV, N // bq),
                 in_specs=[pl.BlockSpec((GRP, bq, HD), lambda g, i: (g, i + lb, 0)), kvb, kvb,
                           pl.BlockSpec((bq, GRP * HD), lambda g, i: (i, g))],
                 out_specs=(pl.BlockSpec((GRP, bq, HD), lambda g, i: (g, i, 0)), kvb, kvb),
                 sem=("parallel", "arbitrary"), vmem=VMEM_BIG)(q, k, v, do)


def _gprep_fn(kind, shifts, x, w):
    down, up = shifts
    y = down(x) * w[0:1, :] + x * w[1:2, :] + up(x) * w[2:3, :]
    a = _silu(y)
    if kind == 2:
        return a
    a = a * lax.rsqrt(jnp.sum(a * a, axis=-1, keepdims=True) + EPS)
    return a * (HD ** -0.5) if kind == 0 else a


def _gprep_fwd(proj, conv_w, kind, bounds):
    T = proj.shape[0]
    shifts = _make_shift(bounds)
    cb = C_QKV // HD + kind * GH

    def body(x_ref, w_ref, o_ref):
        o_ref[...] = _gprep_fn(kind, shifts, x_ref[...], w_ref[...])

    return _call(body, name=f"gprep_fwd{kind}", out_shape=_sds((GH, T, HD)), grid=(GH,),
                 in_specs=[pl.BlockSpec((T, HD), lambda h: (0, cb + h)),
                           pl.BlockSpec((3, HD), lambda h: (0, kind * GH + h))],
                 out_specs=pl.BlockSpec((None, T, HD), lambda h: (h, 0, 0)), sem=("parallel",))(proj, conv_w)


def _gprep_bwd(proj, conv_w, kind, bounds, dy):
    T = proj.shape[0]
    shifts = _make_shift(bounds)
    cb = C_QKV // HD + kind * GH

    def body(x_ref, w_ref, dy_ref, dx_ref, dw_ref):
        _, vjp = jax.vjp(functools.partial(_gprep_fn, kind, shifts), x_ref[...], w_ref[...])
        dx, dw = vjp(dy_ref[0] + dy_ref[1])
        dx_ref[...] = dx.astype(BF16)
        dw_ref[...] = dw

    return _call(body, name=f"gprep_bwd{kind}", out_shape=(_sds((T, GH * HD), BF16), _sds((3, GH * HD))), grid=(GH,),
                 in_specs=[pl.BlockSpec((T, HD), lambda h: (0, cb + h)),
                           pl.BlockSpec((3, HD), lambda h: (0, kind * GH + h)),
                           pl.BlockSpec((2, None, T, HD), lambda h: (0, h, 0, 0))],
                 out_specs=(pl.BlockSpec((T, HD), lambda h: (0, h)), pl.BlockSpec((3, HD), lambda h: (0, h))),
                 sem=("parallel",))(proj, conv_w, dy)


def _bl_fn(x, alog, dtb):
    lane = lax.broadcasted_iota(jnp.int32, x.shape, 1)
    beta = jax.nn.sigmoid(x)
    z = x + dtb
    sp = jnp.maximum(z, 0.0) + jnp.log1p(jnp.exp(-jnp.abs(z)))
    la = -jnp.exp(alog) * sp
    return jnp.where(lane < 2 * GH, beta, jnp.where(lane < 4 * GH, la, 0.0))


def _bl_fwd(proj, alog, dtb, *, br=256):
    T = proj.shape[0]

    def body(x_ref, a_ref, d_ref, o_ref):
        o_ref[...] = _bl_fn(x_ref[...], a_ref[...], d_ref[...])

    vec = pl.BlockSpec((1, HD), lambda i: (0, 0))
    return _call(body, name="bl_fwd", out_shape=_sds((T, HD)), grid=(T // br,),
                 in_specs=[pl.BlockSpec((br, HD), lambda i: (i, C_BL // HD)), vec, vec],
                 out_specs=pl.BlockSpec((br, HD), lambda i: (i, 0)), sem=("parallel",))(proj, alog, dtb)


def _bl_bwd(proj, alog, dtb, dbl, *, br=256):
    T = proj.shape[0]

    def body(x_ref, a_ref, d_ref, g_ref, dx_ref, da_ref, dd_ref):
        g = g_ref[0, 0]
        for d in range(2):
            for h in range(GH):
                if d or h:
                    g = g + g_ref[d, h]
        _, vjp = jax.vjp(_bl_fn, x_ref[...], a_ref[...], d_ref[...])
        dx, da, dd = vjp(g)
        dx_ref[...] = dx.astype(BF16)

        @pl.when(pl.program_id(0) == 0)
        def _():
            da_ref[...] = jnp.zeros_like(da_ref)
            dd_ref[...] = jnp.zeros_like(dd_ref)

        da_ref[...] += da
        dd_ref[...] += dd

    vec = pl.BlockSpec((1, HD), lambda i: (0, 0))
    return _call(body, name="bl_bwd", out_shape=(_sds((T, HD), BF16), _sds((1, HD)), _sds((1, HD))), grid=(T // br,),
                 in_specs=[pl.BlockSpec((br, HD), lambda i: (i, C_BL // HD)), vec, vec,
                           pl.BlockSpec((2, GH, br, HD), lambda i: (0, 0, i, 0))],
                 out_specs=(pl.BlockSpec((br, HD), lambda i: (i, 0)), vec, vec), sem=("arbitrary",))(proj, alog, dtb, dbl)


def _chunk_masks(d):
    ii = lax.broadcasted_iota(jnp.int32, (CH, CH), 0)
    jj = lax.broadcasted_iota(jnp.int32, (CH, CH), 1)
    eye = (ii == jj).astype(F32)
    before = jnp.where(d == 0, (jj < ii).astype(F32), (jj > ii).astype(F32))
    return before, before + eye, eye


def _intra_fn(masks, sel_b, sel_l, q, k, v, bl):
    before, ateq, eye = masks
    beta = jnp.sum(bl * sel_b, axis=-1, keepdims=True)
    la = jnp.sum(bl * sel_l, axis=-1, keepdims=True)
    gam = _hdot(ateq, jnp.broadcast_to(la, (CH, HD)))
    gi = _hdot(ateq, jnp.broadcast_to(la, (CH, CH)))
    gj = _hdot(jnp.ones((CH, CH), F32), eye * gi)
    inc = ateq > 0.0
    dec = jnp.where(inc, jnp.exp(jnp.where(inc, gi - gj, 0.0)), 0.0)
    lmat = before * (beta * dec * _nt(k, k))
    x = eye - lmat
    p2 = _hdot(lmat, lmat)
    for it in range(5):
        x = x + _hdot(x, p2)
        if it < 4:
            p2 = _hdot(p2, p2)
    eg = jnp.exp(gam)
    u = _hdot(x, beta * v)
    w = _hdot(x, (beta * eg) * k)
    tot = jnp.sum(la, axis=0, keepdims=True)
    kd = k * jnp.exp(tot - gam)
    gl = jnp.broadcast_to(jnp.exp(tot), (1, HD))
    qd = q * eg
    p = dec * _nt(q, k)
    return u, w, kd, qd, p, gl


def _dir_head_sel(d, h):
    lane = lax.broadcasted_iota(jnp.int32, (1, HD), 1)
    return (lane == d * GH + h).astype(F32), (lane == 2 * GH + d * GH + h).astype(F32)


def _intra_specs(T):
    nc = T // CH
    qkv = pl.BlockSpec((None, CH, HD), lambda d, h, c: (h, c, 0))
    bl = pl.BlockSpec((CH, HD), lambda d, h, c: (c, 0))
    big = pl.BlockSpec((None, None, CH, HD), lambda d, h, c: (d, h, c, 0))
    pm = pl.BlockSpec((None, None, CH, CH), lambda d, h, c: (d, h, c, 0))
    gl = pl.BlockSpec((None, None, None, 1, HD), lambda d, h, c: (d, h, c, 0, 0))
    shapes = (_sds((2, GH, T, HD)),) * 4 + (_sds((2, GH, T, CH)), _sds((2, GH, nc, 1, HD)))
    return nc, qkv, bl, big, pm, gl, shapes


def _intra_fwd(q, k, v, bl):
    T = q.shape[1]
    nc, qkv_s, bl_s, big, pm, gl_s, shapes = _intra_specs(T)

    def body(q_ref, k_ref, v_ref, bl_ref, u_ref, w_ref, kd_ref, qd_ref, p_ref, gl_ref):
        d, h = pl.program_id(0), pl.program_id(1)
        sb, sl = _dir_head_sel(d, h)
        outs = _intra_fn(_chunk_masks(d), sb, sl, q_ref[...], k_ref[...], v_ref[...], bl_ref[...])
        for r, o in zip((u_ref, w_ref, kd_ref, qd_ref, p_ref, gl_ref), outs):
            r[...] = o

    return _call(body, name="gdn_intra_fwd", out_shape=shapes, grid=(2, GH, nc),
                 in_specs=[qkv_s, qkv_s, qkv_s, bl_s], out_specs=(big, big, big, big, pm, gl_s),
                 sem=("parallel", "parallel", "parallel"))(q, k, v, bl)


def _intra_bwd(q, k, v, bl, cts):
    T = q.shape[1]
    nc, qkv_s, bl_s, big, pm, gl_s, _ = _intra_specs(T)

    def body(q_ref, k_ref, v_ref, bl_ref, du, dw, dkd, dqd, dp, dgl, dq_ref, dk_ref, dv_ref, dbl_ref):
        d, h = pl.program_id(0), pl.program_id(1)
        sb, sl = _dir_head_sel(d, h)
        fn = functools.partial(_intra_fn, _chunk_masks(d), sb, sl)
        _, vjp = jax.vjp(fn, q_ref[...], k_ref[...], v_ref[...], bl_ref[...])
        gq, gk, gv, gbl = vjp((du[...], dw[...], dkd[...], dqd[...], dp[...], dgl[...]))
        dq_ref[...] = gq
        dk_ref[...] = gk
        dv_ref[...] = gv
        dbl_ref[...] = gbl

    return _call(body, name="gdn_intra_bwd", out_shape=(_sds((2, GH, T, HD)),) * 4, grid=(2, GH, nc),
                 in_specs=[qkv_s, qkv_s, qkv_s, bl_s, big, big, big, big, pm, gl_s], out_specs=(big,) * 4,
                 sem=("parallel", "parallel", "parallel"))(q, k, v, bl, *cts)


def _scan_fn(s, u, w, kd, qd, p, gl):
    delta = u - _nn(w, s)
    s_new = gl * s + _tn(kd, delta)
    o = _nn(qd, s) + _nn(p, delta)
    return o, s_new


def _scan_chunk(step, d, nc, ncc):
    fwd = step
    rev = jnp.where(step < ncc, ncc - 1 - step, nc - 1 - (step - ncc))
    return jnp.where(d == 0, fwd, rev)


def _scan_specs(T):
    nc = T // CH
    big = pl.BlockSpec((None, None, T, HD), lambda d, h: (d, h, 0, 0))
    pm = pl.BlockSpec((None, None, T, CH), lambda d, h: (d, h, 0, 0))
    gl = pl.BlockSpec((None, None, nc, 1, HD), lambda d, h: (d, h, 0, 0, 0))
    st = pl.BlockSpec((None, None, nc, HD, HD), lambda d, h: (d, h, 0, 0, 0))
    return nc, big, pm, gl, st


def _scan_fwd(u, w, kd, qd, p, gl, L):
    T = u.shape[2]
    nc, big, pm, gl_s, st = _scan_specs(T)
    ncc = L // CH

    def body(u_ref, w_ref, kd_ref, qd_ref, p_ref, gl_ref, o_ref, st_ref):
        d = pl.program_id(0)

        def step(i, s):
            c = _scan_chunk(i, d, nc, ncc)
            rows = pl.ds(pl.multiple_of(c * CH, CH), CH)
            st_ref[c] = s
            o, s_new = _scan_fn(s, u_ref[rows, :], w_ref[rows, :], kd_ref[rows, :], qd_ref[rows, :],
                                p_ref[rows, :], gl_ref[c])
            o_ref[rows, :] = o
            return s_new

        lax.fori_loop(0, nc, step, jnp.zeros((HD, HD), F32))

    return _call(body, name="gdn_scan_fwd", out_shape=(_sds((2, GH, T, HD)), _sds((2, GH, nc, HD, HD))),
                 grid=(2, GH), in_specs=[big, big, big, big, pm, gl_s], out_specs=(big, st),
                 sem=("parallel", "parallel"))(u, w, kd, qd, p, gl)


def _scan_bwd(u, w, kd, qd, p, gl, states, do, L):
    T = u.shape[2]
    nc, big, pm, gl_s, st = _scan_specs(T)
    ncc = L // CH

    def body(u_ref, w_ref, kd_ref, qd_ref, p_ref, gl_ref, st_ref, do_ref,
             du_ref, dw_ref, dkd_ref, dqd_ref, dp_ref, dgl_ref):
        d = pl.program_id(0)

        def step(j, ds):
            c = _scan_chunk(nc - 1 - j, d, nc, ncc)
            rows = pl.ds(pl.multiple_of(c * CH, CH), CH)
            _, vjp = jax.vjp(_scan_fn, st_ref[c], u_ref[rows, :], w_ref[rows, :], kd_ref[rows, :],
                             qd_ref[rows, :], p_ref[rows, :], gl_ref[c])
            ds_prev, gu, gw, gkd, gqd, gp, ggl = vjp((do_ref[rows, :], ds))
            du_ref[rows, :] = gu
            dw_ref[rows, :] = gw
            dkd_ref[rows, :] = gkd
            dqd_ref[rows, :] = gqd
            dp_ref[rows, :] = gp
            dgl_ref[c] = ggl
            return ds_prev

        lax.fori_loop(0, nc, step, jnp.zeros((HD, HD), F32))

    return _call(body, name="gdn_scan_bwd",
                 out_shape=(_sds((2, GH, T, HD)),) * 4 + (_sds((2, GH, T, CH)), _sds((2, GH, nc, 1, HD))),
                 grid=(2, GH),
                 in_specs=[big, big, big, big, pm, gl_s, st, pl.BlockSpec((None, T, HD), lambda d, h: (h, 0, 0))],
                 out_specs=(big, big, big, big, pm, gl_s),
                 sem=("parallel", "parallel"))(u, w, kd, qd, p, gl, states, do)


def _gout_fn(o0, o1, z, gw):
    return _rms(o0 + o1) * gw * _silu(z)


def _gout_fwd(o, proj, gw, L, *, br=256):
    T = o.shape[2]
    N = T - L
    lb = L // br
    ob = pl.BlockSpec((None, None, br, HD), lambda i, h: (0, h, i + lb, 0))
    ob1 = pl.BlockSpec((None, None, br, HD), lambda i, h: (1, h, i + lb, 0))

    def body(o0_ref, o1_ref, z_ref, gw_ref, y_ref):
        y_ref[...] = _gout_fn(o0_ref[...], o1_ref[...], z_ref[...], gw_ref[...]).astype(BF16)

    return _call(body, name="gout_fwd", out_shape=_sds((N, GH * HD), BF16), grid=(N // br, GH),
                 in_specs=[ob, ob1, pl.BlockSpec((br, HD), lambda i, h: (i + lb, C_Z // HD + h)),
                           pl.BlockSpec((1, HD), lambda i, h: (0, 0))],
                 out_specs=pl.BlockSpec((br, HD), lambda i, h: (i, h)), sem=("parallel", "parallel"))(o, o, proj, gw)


def _gout_bwd(o, proj, gw, dy, L, *, br=256):
    T = o.shape[2]
    lb = L // br
    ob = pl.BlockSpec((None, None, br, HD), lambda i, h: (0, h, i, 0))
    ob1 = pl.BlockSpec((None, None, br, HD), lambda i, h: (1, h, i, 0))

    def body(o0_ref, o1_ref, z_ref, gw_ref, dy_ref, do_ref, dz_ref, dgw_ref):
        i, h = pl.program_id(0), pl.program_id(1)
        _, vjp = jax.vjp(_gout_fn, o0_ref[...], o1_ref[...], z_ref[...], gw_ref[...])
        g0, _, gz, ggw = vjp(dy_ref[...])
        lat = i >= lb
        do_ref[...] = jnp.where(lat, g0, 0.0)
        dz_ref[...] = jnp.where(lat, gz, 0.0).astype(BF16)

        @pl.when(jnp.logical_and(i == 0, h == 0))
        def _():
            dgw_ref[...] = jnp.zeros_like(dgw_ref)

        dgw_ref[...] += jnp.where(lat, ggw, 0.0)

    return _call(body, name="gout_bwd", out_shape=(_sds((GH, T, HD)), _sds((T, GH * HD), BF16), _sds((1, HD))),
                 grid=(T // br, GH),
                 in_specs=[ob, ob1, pl.BlockSpec((br, HD), lambda i, h: (i, C_Z // HD + h)),
                           pl.BlockSpec((1, HD), lambda i, h: (0, 0)),
                           pl.BlockSpec((br, HD), lambda i, h: (jnp.maximum(i - lb, 0), h))],
                 out_specs=(pl.BlockSpec((None, br, HD), lambda i, h: (h, i, 0)),
                            pl.BlockSpec((br, HD), lambda i, h: (i, h)),
                            pl.BlockSpec((1, HD), lambda i, h: (0, 0))),
                 sem=("arbitrary", "arbitrary"))(o, o, proj, gw, dy)


def _merge_fn(pa, pd, ga, gd):
    return jax.nn.sigmoid(ga) * pa + jax.nn.sigmoid(gd) * pd


def _merge_fwd(pa, pd, proj, L, *, br=256):
    N = pa.shape[0]
    lb = L // br
    row = pl.BlockSpec((br, D), lambda i: (i, 0))

    def body(pa_ref, pd_ref, ga_ref, gd_ref, y_ref):
        y_ref[...] = _merge_fn(pa_ref[...], pd_ref[...], ga_ref[...], gd_ref[...]).astype(BF16)

    return _call(body, name="merge_fwd", out_shape=_sds((N, D), BF16), grid=(N // br,),
                 in_specs=[row, row, pl.BlockSpec((br, D), lambda i: (i + lb, C_GATE // D)),
                           pl.BlockSpec((br, D), lambda i: (i + lb, C_GATE // D + 1))],
                 out_specs=row, sem=("parallel",))(pa, pd, proj, proj)


def _merge_bwd(pa, pd, proj, dy, L, *, br=256):
    N = pa.shape[0]
    T = N + L
    lb = L // br
    lrow = pl.BlockSpec((br, D), lambda i: (jnp.maximum(i - lb, 0), 0))

    def body(pa_ref, pd_ref, ga_ref, gd_ref, dy_ref, dpa_ref, dpd_ref, dg_ref):
        lat = pl.program_id(0) >= lb
        _, vjp = jax.vjp(_merge_fn, pa_ref[...], pd_ref[...], ga_ref[...], gd_ref[...])
        gpa, gpd, gga, ggd = vjp(dy_ref[...])
        dpa_ref[...] = gpa.astype(BF16)
        dpd_ref[...] = gpd.astype(BF16)
        dg_ref[:, :D] = jnp.where(lat, gga, 0.0).astype(BF16)
        dg_ref[:, D:] = jnp.where(lat, ggd, 0.0).astype(BF16)

    return _call(body, name="merge_bwd", out_shape=(_sds((N, D), BF16), _sds((N, D), BF16), _sds((T, 2 * D), BF16)),
                 grid=(T // br,),
                 in_specs=[lrow, lrow, pl.BlockSpec((br, D), lambda i: (i, C_GATE // D)),
                           pl.BlockSpec((br, D), lambda i: (i, C_GATE // D + 1)), lrow],
                 out_specs=(lrow, lrow, pl.BlockSpec((br, 2 * D), lambda i: (i, 0))),
                 sem=("arbitrary",))(pa, pd, proj, proj, dy)


def _resid_fwd(x, m, mod, i_g, *, name, br=256):
    R = x.shape[0]
    row = pl.BlockSpec((br, D), lambda i: (i, 0))

    def body(x_ref, m_ref, mod_ref, o_ref):
        o_ref[...] = x_ref[...] + mod_ref[i_g:i_g + 1, :] * m_ref[...]

    return _call(body, name=name, out_shape=_sds((R, D)), grid=(R // br,),
                 in_specs=[row, row, pl.BlockSpec((6, D), lambda i: (0, 0))], out_specs=row,
                 sem=("parallel",))(x, m, mod)


def _resid_bwd(dx, m, mod, i_g, *, name, br=256):
    R = dx.shape[0]
    row = pl.BlockSpec((br, D), lambda i: (i, 0))
    vec = pl.BlockSpec((1, D), lambda i: (0, 0))

    def body(dx_ref, m_ref, mod_ref, dm_ref, dg_ref):
        dxv = dx_ref[...]
        dm_ref[...] = (dxv * mod_ref[i_g:i_g + 1, :]).astype(BF16)

        @pl.when(pl.program_id(0) == 0)
        def _():
            dg_ref[...] = jnp.zeros_like(dg_ref)

        dg_ref[...] += jnp.sum(dxv * m_ref[...], axis=0, keepdims=True)

    return _call(body, name=name, out_shape=(_sds((R, D), BF16), _sds((1, D))), grid=(R // br,),
                 in_specs=[row, row, pl.BlockSpec((6, D), lambda i: (0, 0))], out_specs=(row, vec),
                 sem=("arbitrary",))(dx, m, mod)


def _ffn_fn(shifts, ug, uv, wg, wv, bg, bv):
    down, up = shifts

    def conv(x, w, b):
        return down(x) * w[0:1, :] + x * w[1:2, :] + up(x) * w[2:3, :] + b

    return _silu(conv(ug, wg, bg)) * conv(uv, wv, bv)


def _ffn_fwd(up, cw, cb, *, bw=256):
    N = up.shape[0]
    shifts = _make_shift(((0, N),))
    nb = DFF // bw

    def body(ug, uv, wg, wv, bg, bv, a_ref):
        a_ref[...] = _ffn_fn(shifts, ug[...], uv[...], wg[...], wv[...], bg[...], bv[...]).astype(BF16)

    def col(rows, off):
        return pl.BlockSpec((rows, bw), lambda j: (0, j + off))

    return _call(body, name="ffn_fwd", out_shape=_sds((N, DFF), BF16), grid=(nb,),
                 in_specs=[col(N, 0), col(N, nb), col(3, 0), col(3, nb), col(1, 0), col(1, nb)],
                 out_specs=col(N, 0), sem=("parallel",), vmem=VMEM_BIG)(up, up, cw, cw, cb, cb)


def _ffn_bwd(up, cw, cb, da, *, bw=256):
    N = up.shape[0]
    shifts = _make_shift(((0, N),))
    nb = DFF // bw

    def body(ug, uv, wg, wv, bg, bv, da_ref, dug, duv, dwg, dwv, dbg, dbv):
        _, vjp = jax.vjp(functools.partial(_ffn_fn, shifts), ug[...], uv[...], wg[...], wv[...], bg[...], bv[...])
        g = vjp(da_ref[...])
        dug[...] = g[0].astype(BF16)
        duv[...] = g[1].astype(BF16)
        dwg[...], dwv[...], dbg[...], dbv[...] = g[2], g[3], g[4], g[5]

    def col(rows, off):
        return pl.BlockSpec((rows, bw), lambda j: (0, j + off))

    half = (_sds((N, DFF), BF16), _sds((N, DFF), BF16), _sds((3, DFF)), _sds((3, DFF)), _sds((1, DFF)), _sds((1, DFF)))
    dug, duv, dwg, dwv, dbg, dbv = _call(
        body, name="ffn_bwd", out_shape=half, grid=(nb,),
        in_specs=[col(N, 0), col(N, nb), col(3, 0), col(3, nb), col(1, 0), col(1, nb), col(N, 0)],
        out_specs=(col(N, 0), col(N, 0), col(3, 0), col(3, 0), col(1, 0), col(1, 0)),
        sem=("parallel",), vmem=VMEM_BIG)(up, up, cw, cw, cb, cb, da)
    return (jnp.concatenate([dug, duv], axis=1), jnp.concatenate([dwg, dwv], axis=1),
            jnp.concatenate([dbg, dbv], axis=1))


def _head_fn(x1, dn, g2, fw, tgt):
    y = _rms(x1 + g2 * dn) * fw
    err = y - tgt
    return 0.5 * jnp.sum(jnp.mean(err * err, axis=-1))


def _head(x1, dn, mod, fw, tgt, *, br=256):
    N = x1.shape[0]
    row = pl.BlockSpec((br, D), lambda i: (i, 0))
    vec = pl.BlockSpec((1, D), lambda i: (0, 0))
    one = pl.BlockSpec((1, HD), lambda i: (0, 0))

    def body(x1_ref, dn_ref, mod_ref, fw_ref, tgt_ref, loss_ref, dx_ref, ddn_ref, dg_ref, dfw_ref):
        loss, (gx, gdn, gg, gfw) = jax.value_and_grad(_head_fn, argnums=(0, 1, 2, 3))(
            x1_ref[...], dn_ref[...], mod_ref[5:6, :], fw_ref[...], tgt_ref[...])
        dx_ref[...] = gx
        ddn_ref[...] = gdn.astype(BF16)

        @pl.when(pl.program_id(0) == 0)
        def _():
            loss_ref[...] = jnp.zeros_like(loss_ref)
            dg_ref[...] = jnp.zeros_like(dg_ref)
            dfw_ref[...] = jnp.zeros_like(dfw_ref)

        loss_ref[...] += jnp.broadcast_to(loss, (1, HD))
        dg_ref[...] += gg
        dfw_ref[...] += gfw

    return _call(body, name="head", out_shape=(_sds((1, HD)), _sds((N, D)), _sds((N, D), BF16), _sds((1, D)), _sds((1, D))),
                 grid=(N // br,), in_specs=[row, row, pl.BlockSpec((6, D), lambda i: (0, 0)), vec, row],
                 out_specs=(one, row, row, vec, vec), sem=("arbitrary",))(x1, dn, mod, fw, tgt)


def _adamw(w, g, m, v, *, name):
    shape = w.shape
    cols = shape[-1]
    rows = max(1, math.prod(shape[:-1]))
    w2, g2, m2, v2 = (t.reshape(rows, cols) for t in (w, g, m, v))
    br = 256 if rows % 256 == 0 else (128 if rows % 128 == 0 else (8 if rows % 8 == 0 and rows > 64 else rows))
    if rows % 352 == 0:
        br = 352
    c1 = 1.0 - B1 ** STEP
    c2 = 1.0 - B2 ** STEP

    def body(w_ref, g_ref, m_ref, v_ref, d_ref, nm_ref, nv_ref):
        gv = g_ref[...]
        nm = B1 * m_ref[...] + (1.0 - B1) * gv
        nv = B2 * v_ref[...] + (1.0 - B2) * (gv * gv)
        d_ref[...] = -LR * ((nm / c1) / (jnp.sqrt(nv / c2) + AEPS) + WD * w_ref[...])
        nm_ref[...] = nm
        nv_ref[...] = nv

    blk = pl.BlockSpec((br, cols), lambda i: (i, 0))
    outs = _call(body, name=name, out_shape=(_sds((rows, cols)),) * 3, grid=(rows // br,),
                 in_specs=[blk] * 4, out_specs=(blk,) * 3, sem=("parallel",))(w2, g2, m2, v2)
    return tuple(t.reshape(shape) for t in outs)


def _rope_tables(N, L):
    t = jnp.arange(N)
    pos = jnp.stack([(t // GRID_W).astype(F32), (t % GRID_W).astype(F32)], axis=1)
    inv = ROPE_THETA ** (-jnp.arange(0, HD // 2, 2, dtype=F32) / (HD // 2))
    ang = pos[:, :, None] * inv[None, None, :]
    cos = jnp.broadcast_to(jnp.cos(ang)[:, :, None, :], (N, 2, 2, HD // 4)).reshape(N, HD)
    sin = jnp.broadcast_to(jnp.sin(ang)[:, :, None, :], (N, 2, 2, HD // 4))
    sin = (sin * jnp.array([-1.0, 1.0], F32)[None, None, :, None]).reshape(N, HD)
    cos = jnp.concatenate([jnp.ones((L, HD), F32), cos], axis=0)
    sin = jnp.concatenate([jnp.zeros((L, HD), F32), sin], axis=0)
    return cos, sin


def _pad_lanes(v, off=0):
    return jnp.zeros((1, HD), F32).at[0, off:off + v.shape[0]].set(v)


def _local_step(x, ctx, tgt, mod_lat, mod_ctx, wts, small):
    N, L = x.shape[0], ctx.shape[0]
    T = N + L
    bounds = ((0, L), (L, T))
    w_in, w_pa, w_pd, w_out, w_up, w_down = wts
    qw, kw, gw = small["q_norm_w"], small["k_norm_w"], small["gdn_norm_w"]
    conv_w, ffn_w, ffn_b, fnw = small["conv_qkv_w"], small["ffn_conv_w"], small["ffn_conv_b"], small["final_norm_w"]
    alog = _pad_lanes(small["a_log"].reshape(-1), 2 * GH)
    dtb = _pad_lanes(small["dt_bias"].reshape(-1), 2 * GH)
    cos, sin = _rope_tables(N, L)
    bt = 256 if T % 768 else 768
    bnl = 256 if N % 1024 else 1024

    hc = _normmod_fwd(ctx, mod_ctx, 0, 1, name="normmod_ctx")
    hx = _normmod_fwd(x, mod_lat, 0, 1, name="normmod_x")
    h1 = jnp.concatenate([hc, hx], axis=0)
    proj = _mm(h1, w_in, name="mm_in", M=T, N=C_END, K=D, bm=bt, bn=1024)
    aq, ak, av = _aprep_fwd(proj, cos, sin, qw, kw)
    attn = _attn_fwd(aq, ak, av, L)
    gq = _gprep_fwd(proj, conv_w, 0, bounds)
    gk = _gprep_fwd(proj, conv_w, 1, bounds)
    gv = _gprep_fwd(proj, conv_w, 2, bounds)
    bl = _bl_fwd(proj, alog, dtb)
    intra = _intra_fwd(gq, gk, gv, bl)
    o, states = _scan_fwd(*intra, L)
    gdn = _gout_fwd(o, proj, gw, L)
    pa = _mm(attn, w_pa, name="mm_pa", M=N, N=D, K=D, bm=bnl)
    pd = _mm(gdn, w_pd, name="mm_pd", M=N, N=D, K=D, bm=bnl)
    y = _merge_fwd(pa, pd, proj, L)
    m = _mm(y, w_out, name="mm_out", M=N, N=D, K=D, bm=bnl)
    x1 = _resid_fwd(x, m, mod_lat, 2, name="resid1")
    h2 = _normmod_fwd(x1, mod_lat, 3, 4, name="normmod_x1")
    up = _mm(h2, w_up, name="mm_up", M=N, N=2 * DFF, K=D, bm=bnl, bn=2 * DFF // 4)
    a = _ffn_fwd(up, ffn_w, ffn_b)
    dn = _mm(a, w_down, name="mm_down", M=N, N=D, K=DFF, bm=bnl)
    loss, dx2, ddn, dg2, dfnw = _head(x1, dn, mod_lat, fnw, tgt)

    da = _mm(ddn, w_down, name="mm_down_dx", M=N, N=DFF, K=D, tb=True, bm=bnl, bn=DFF // 2)
    g_down = _mm(a, ddn, name="mm_down_dw", M=DFF, N=D, K=N, ta=True, bm=DFF // 2)
    dup, d_ffn_w, d_ffn_b = _ffn_bwd(up, ffn_w, ffn_b, da)
    dh2 = _mm(dup, w_up, name="mm_up_dx", M=N, N=D, K=2 * DFF, tb=True, bm=bnl, bk=2 * DFF // 4)
    g_up = _mm(h2, dup, name="mm_up_dw", M=D, N=2 * DFF, K=N, ta=True, bn=2 * DFF // 4)
    dx1, dsh2, dsc2 = _normmod_bwd(x1, mod_lat, 3, 4, dh2, 0, dx2, name="normmod_x1_bwd")
    dm, dg1 = _resid_bwd(dx1, m, mod_lat, 2, name="resid1_bwd")
    dy = _mm(dm, w_out, name="mm_out_dx", M=N, N=D, K=D, tb=True, bm=bnl)
    g_out = _mm(y, dm, name="mm_out_dw", M=D, N=D, K=N, ta=True)
    dpa, dpd, dgate = _merge_bwd(pa, pd, proj, dy, L)
    dattn = _mm(dpa, w_pa, name="mm_pa_dx", M=N, N=D, K=D, tb=True, bm=bnl)
    g_pa = _mm(attn, dpa, name="mm_pa_dw", M=D, N=D, K=N, ta=True)
    dgdn = _mm(dpd, w_pd, name="mm_pd_dx", M=N, N=D, K=D, tb=True, bm=bnl)
    g_pd = _mm(gdn, dpd, name="mm_pd_dw", M=D, N=D, K=N, ta=True)
    do, dz, dgw = _gout_bwd(o, proj, gw, dgdn, L)
    cts = _scan_bwd(*intra, states, do, L)
    dgq, dgk, dgv, dbl = _intra_bwd(gq, gk, gv, bl, cts)
    dxq, dwq = _gprep_bwd(proj, conv_w, 0, bounds, dgq)
    dxk, dwk = _gprep_bwd(proj, conv_w, 1, bounds, dgk)
    dxv, dwv = _gprep_bwd(proj, conv_w, 2, bounds, dgv)
    dxbl, dalog, ddtb = _bl_bwd(proj, alog, dtb, dbl)
    daq_h, dak_h, dav_h = _attn_bwd(aq, ak, av, dattn, L)
    daq, dkv, dqw, dkw = _aprep_bwd(proj, cos, sin, qw, kw, daq_h, dak_h, dav_h, L)
    dproj = jnp.concatenate([dkv, dxq, dxk, dxv, dxbl, jnp.zeros((T, C_AQ - C_BL - HD), BF16), daq, dz, dgate], axis=1)
    dh1 = _mm(dproj, w_in, name="mm_in_dx", M=T, N=D, K=C_END, tb=True, bm=bt, bk=1024)
    g_in = _mm(h1, dproj, name="mm_in_dw", M=D, N=C_END, K=T, ta=True, bn=1024)
    grad_x, dsh1, dsc1 = _normmod_bwd(x, mod_lat, 0, 1, dh1, L, dx1, name="normmod_x_bwd")
    _, dcsh1, dcsc1 = _normmod_bwd(ctx, mod_ctx, 0, 1, dh1, 0, None, name="normmod_ctx_bwd")

    z1 = jnp.zeros((1, D), F32)
    dmod_lat = jnp.concatenate([dsh1, dsc1, dg1, dsh2, dsc2, dg2], axis=0)
    dmod_ctx = jnp.concatenate([dcsh1, dcsc1, z1, z1, z1, z1], axis=0)
    gsmall = {
        "q_norm_w": dqw, "k_norm_w": dkw, "gdn_norm_w": dgw,
        "conv_qkv_w": jnp.concatenate([dwq, dwk, dwv], axis=1),
        "a_log": dalog[0, 2 * GH:4 * GH], "dt_bias": ddtb[0, 2 * GH:4 * GH],
        "ffn_conv_w": d_ffn_w, "ffn_conv_b": d_ffn_b, "final_norm_w": dfnw,
    }
    return loss[0, 0], grad_x, (g_in, g_pa, g_pd, g_out, g_up, g_down), dmod_lat, dmod_ctx, gsmall


HBM = pl.BlockSpec(memory_space=pltpu.HBM)


def _position():
    x, y, c = lax.axis_index("x"), lax.axis_index("y"), lax.axis_index("c")
    return x, y, c, 4 * x + 2 * y + c


def _peer(x, y, c, k):
    px = 1 - x if k & 4 else x
    py = 1 - y if k & 2 else y
    pc = 1 - c if k & 1 else c
    return (px, py, pc), 4 * px + 2 * py + pc


def _exchange(arrs, *, name, scatter):
    n = len(arrs)

    def body(*refs):
        ins, outs = refs[:n], refs[n:2 * n]
        send, recv, loc = refs[2 * n:]
        x, y, c, me = _position()
        local = []
        for a in range(n):
            src = ins[a].at[me] if scatter else ins[a]
            cp = pltpu.make_async_copy(src, outs[a].at[me], loc.at[a])
            cp.start()
            local.append(cp)
        remote = []
        for k in range(1, NDEV):
            peer, pid = _peer(x, y, c, k)
            for a in range(n):
                src = ins[a].at[pid] if scatter else ins[a]
                cp = pltpu.make_async_remote_copy(src_ref=src, dst_ref=outs[a].at[me], send_sem=send.at[a, k - 1],
                                                  recv_sem=recv.at[a, k - 1], device_id=peer, device_id_type=MESH)
                cp.start()
                remote.append(cp)
        for cp in remote:
            cp.wait()
        for cp in local:
            cp.wait()

    out_shape = tuple(_sds(a.shape if scatter else (NDEV,) + a.shape, a.dtype) for a in arrs)
    outs = pl.pallas_call(
        body, name=name, out_shape=out_shape, in_specs=[HBM] * n, out_specs=(HBM,) * n,
        scratch_shapes=[pltpu.SemaphoreType.DMA((n, NDEV - 1)), pltpu.SemaphoreType.DMA((n, NDEV - 1)),
                        pltpu.SemaphoreType.DMA((n,))],
        compiler_params=pltpu.CompilerParams(has_side_effects=True))(*arrs)
    return list(outs)


def _cast_bf16(w, *, name):
    rows, cols = w.shape
    br = 128 if rows % 128 == 0 else rows

    def body(w_ref, o_ref):
        o_ref[...] = w_ref[...].astype(BF16)

    blk = pl.BlockSpec((br, cols), lambda i: (i, 0))
    return _call(body, name=name, out_shape=_sds((rows, cols), BF16), grid=(rows // br,), in_specs=[blk],
                 out_specs=blk, sem=("parallel",))(w)


def _sum_slots(a, *, name):
    _, R, C = a.shape

    def body(a_ref, o_ref):
        s = a_ref[0]
        for d in range(1, NDEV):
            s = s + a_ref[d]
        o_ref[...] = s

    return _call(body, name=name, out_shape=_sds((R, C)))(a)


MODROWS = 16


def _mod_fwd(c9, w, b):
    cols = w.shape[1]

    def body(c_ref, w_ref, b_ref, o_ref):
        o_ref[...] = _nn(_silu(c_ref[...]), w_ref[...]) + b_ref[...]

    return _call(body, name="mod_fwd", out_shape=_sds((MODROWS, cols)))(c9, w, b)


def _mod_bwd(c9, dmy, dall, w):
    cols = w.shape[1]

    def body(c_ref, dmy_ref, dall_ref, w_ref, gw_ref, gb_ref, cp_ref):
        sc = _silu(c_ref[...])
        rows = lax.broadcasted_iota(jnp.int32, (MODROWS, 1), 0)
        d = dmy_ref[...]
        d_ctx = jnp.where(rows == NDEV, d, 0.0)
        sc_ctx = jnp.where(rows == NDEV, sc, 0.0)
        outer = lax.dot_general(sc_ctx, d_ctx, (((0,), (0,)), ((), ())), precision=HI, preferred_element_type=F32)
        gw_ref[...] = _tn(jnp.where(rows < NDEV, sc, 0.0), jnp.where(rows < NDEV, d, 0.0)) + outer
        gb_ref[...] = jnp.sum(dall_ref[...], axis=0, keepdims=True)
        cp_ref[...] = jnp.sum(_nt(d_ctx, w_ref[...]), axis=0, keepdims=True)

    return _call(body, name="mod_bwd", out_shape=(_sds((D, cols)), _sds((1, 6 * D)), _sds((1, D))),
                 vmem=VMEM_BIG)(c9, dmy, dall, w)


def _cctx_finish(parts, c_ctx):
    def body(p_ref, c_ref, o_ref):
        s = p_ref[0]
        for d in range(1, NDEV):
            s = s + p_ref[d]
        _, vjp = jax.vjp(_silu, c_ref[...])
        o_ref[...] = vjp(s)[0]

    return _call(body, name="cctx_finish", out_shape=_sds((1, D)))(parts, c_ctx)


def _adamw_recv(w, recv, m, v, *, name):
    rows, cols = w.shape
    br = 128 if rows % 128 == 0 else rows
    c1 = 1.0 - B1 ** STEP
    c2 = 1.0 - B2 ** STEP

    def body(w_ref, r_ref, m_ref, v_ref, g_ref, d_ref, nm_ref, nv_ref):
        gv = r_ref[0].astype(F32)
        for d in range(1, NDEV):
            gv = gv + r_ref[d].astype(F32)
        nm = B1 * m_ref[...] + (1.0 - B1) * gv
        nv = B2 * v_ref[...] + (1.0 - B2) * (gv * gv)
        g_ref[...] = gv
        d_ref[...] = -LR * ((nm / c1) / (jnp.sqrt(nv / c2) + AEPS) + WD * w_ref[...])
        nm_ref[...] = nm
        nv_ref[...] = nv

    blk = pl.BlockSpec((br, cols), lambda i: (i, 0))
    return _call(body, name=name, out_shape=(_sds((rows, cols)),) * 4, grid=(rows // br,),
                 in_specs=[blk, pl.BlockSpec((NDEV, br, cols), lambda i: (0, i, 0)), blk, blk], out_specs=(blk,) * 4,
                 sem=("parallel",))(w, recv, m, v)


P_LAT, P_CTX, P_FNW, P_FFNB, P_CONV, P_FFNW, P_MISC, P_ROWS = 0, 6, 12, 13, 19, 28, 45, 48


def _rows_of(v, nrows):
    flat = v.reshape(-1)
    return jnp.pad(flat, (0, nrows * D - flat.shape[0])).reshape(nrows, D)


def _by_columns(g):
    n, r, c = g.shape
    return jnp.transpose(g, (1, 0, 2)).reshape(r, n * c)


def _to_columns(a):
    r, nc = a.shape
    return jnp.transpose(a.reshape(r, NDEV, nc // NDEV), (1, 0, 2))


def kernel(x, c, ctx, c_ctx, w_mod, b_mod, w_in, q_norm_w, k_norm_w, conv_qkv_w, a_log, dt_bias, gdn_norm_w, w_pa, w_pd, w_out, w_up, ffn_conv_w, ffn_conv_b, w_down, final_norm_w, loss_target, m_c_ctx, m_w_mod, m_b_mod, m_w_in, m_q_norm_w, m_k_norm_w, m_conv_qkv_w, m_a_log, m_dt_bias, m_gdn_norm_w, m_w_pa, m_w_pd, m_w_out, m_w_up, m_ffn_conv_w, m_ffn_conv_b, m_w_down, m_final_norm_w, v_c_ctx, v_w_mod, v_b_mod, v_w_in, v_q_norm_w, v_k_norm_w, v_conv_qkv_w, v_a_log, v_dt_bias, v_gdn_norm_w, v_w_pa, v_w_pd, v_w_out, v_w_up, v_ffn_conv_w, v_ffn_conv_b, v_w_down, v_final_norm_w):
    _, _, _, me = _position()
    mcols = w_mod.shape[2]

    big = {"w_in": w_in[0], "w_pa": w_pa[0], "w_pd": w_pd[0], "w_out": w_out[0], "w_up": w_up[0], "w_down": w_down[0]}
    names = list(big)
    gathered = _exchange([_cast_bf16(big[n], name="cast_" + n) for n in names], name="gather_weights", scatter=False)
    gw = dict(zip(names, gathered))
    c_all, conv_g, ffnw_g = _exchange([c, conv_qkv_w[0], ffn_conv_w[0]], name="gather_small", scatter=False)
    w_in_full = _by_columns(gw["w_in"])
    w_in_pad = jnp.concatenate([w_in_full[:, :W_AQ], jnp.zeros((D, C_AQ - W_AQ), BF16), w_in_full[:, W_AQ:]], axis=1)
    wts = (w_in_pad, gw["w_pa"].reshape(D, D), gw["w_pd"].reshape(D, D), gw["w_out"].reshape(D, D),
           _by_columns(gw["w_up"]), gw["w_down"].reshape(DFF, D))

    c9 = jnp.concatenate([c_all.reshape(NDEV, D), c_ctx[None], jnp.zeros((MODROWS - NDEV - 1, D), F32)], axis=0)
    b_loc = lax.dynamic_slice(b_mod, (0, me * mcols), (1, mcols))
    mod_all, = _exchange([_mod_fwd(c9, w_mod[0], b_loc)], name="gather_mod", scatter=False)
    mod_lat = lax.dynamic_index_in_dim(mod_all, me, axis=1, keepdims=False).reshape(6, D)
    mod_ctx = mod_all[:, NDEV, :].reshape(6, D)

    small = {"q_norm_w": q_norm_w, "k_norm_w": k_norm_w, "gdn_norm_w": gdn_norm_w, "a_log": a_log, "dt_bias": dt_bias,
             "conv_qkv_w": _by_columns(conv_g), "ffn_conv_w": _by_columns(ffnw_g), "ffn_conv_b": ffn_conv_b,
             "final_norm_w": final_norm_w[None]}
    loss_me, grad_x, gfull, dmod_lat, dmod_ctx, gs = _local_step(x[0], ctx[0], loss_target[0], mod_lat, mod_ctx, wts, small)

    g_in, g_pa, g_pd, g_out, g_up, g_down = gfull
    g_in = jnp.concatenate([g_in[:, :W_AQ], g_in[:, C_AQ:]], axis=1)
    parts = [_to_columns(g_in), g_pa.reshape(NDEV, D // NDEV, D), g_pd.reshape(NDEV, D // NDEV, D),
             g_out.reshape(NDEV, D // NDEV, D), _to_columns(g_up), g_down.reshape(NDEV, DFF // NDEV, D)]
    recv = _exchange([p.astype(BF16) for p in parts], name="scatter_grads", scatter=True)
    moments = {"w_in": (m_w_in, v_w_in), "w_pa": (m_w_pa, v_w_pa), "w_pd": (m_w_pd, v_w_pd),
               "w_out": (m_w_out, v_w_out), "w_up": (m_w_up, v_w_up), "w_down": (m_w_down, v_w_down)}
    res = {}
    for n, r in zip(names, recv):
        outs = _adamw_recv(big[n], r, moments[n][0][0], moments[n][1][0], name="adamw_" + n)
        res[n] = tuple(t[None] for t in outs)

    misc = jnp.concatenate([gs["q_norm_w"][0], gs["k_norm_w"][0], gs["gdn_norm_w"][0], gs["a_log"], gs["dt_bias"],
                            loss_me[None]])
    pack = jnp.concatenate([dmod_lat, dmod_ctx, gs["final_norm_w"], _rows_of(gs["ffn_conv_b"], P_CONV - P_FFNB),
                            _rows_of(gs["conv_qkv_w"], P_FFNW - P_CONV), _rows_of(gs["ffn_conv_w"], P_MISC - P_FFNW),
                            _rows_of(misc, P_ROWS - P_MISC)], axis=0)
    pack_all, = _exchange([pack], name="gather_pack", scatter=False)
    tot = _sum_slots(pack_all, name="sum_pack")
    dall = jnp.concatenate([pack_all[:, P_LAT:P_CTX, :].reshape(NDEV, 6 * D), tot[P_CTX:P_FNW].reshape(1, 6 * D),
                            jnp.zeros((MODROWS - NDEV - 1, 6 * D), F32)], axis=0)
    dmy = lax.dynamic_slice(dall, (0, me * mcols), (MODROWS, mcols))
    g_w_mod, g_b_mod, cpart = _mod_bwd(c9, dmy, dall, w_mod[0])
    cparts, = _exchange([cpart], name="gather_cctx", scatter=False)
    g_c_ctx = _cctx_finish(cparts, c_ctx[None])[0]

    nconv, nffn = 3 * GH * HD, 2 * DFF
    conv_tot = tot[P_CONV:P_FFNW].reshape(-1)[:3 * nconv].reshape(3, nconv)
    ffnw_tot = tot[P_FFNW:P_MISC].reshape(-1)[:3 * nffn].reshape(3, nffn)
    mrow = tot[P_MISC]
    grads = {
        "c_ctx": g_c_ctx, "w_mod": g_w_mod[None], "b_mod": g_b_mod,
        "q_norm_w": mrow[None, 0:HD], "k_norm_w": mrow[None, HD:2 * HD], "gdn_norm_w": mrow[None, 2 * HD:3 * HD],
        "conv_qkv_w": lax.dynamic_slice(conv_tot, (0, me * (nconv // NDEV)), (3, nconv // NDEV))[None],
        "a_log": mrow[3 * HD:3 * HD + 2 * GH].reshape(1, 2, GH),
        "dt_bias": mrow[3 * HD + 2 * GH:3 * HD + 4 * GH].reshape(1, 2, GH),
        "ffn_conv_w": lax.dynamic_slice(ffnw_tot, (0, me * (nffn // NDEV)), (3, nffn // NDEV))[None],
        "ffn_conv_b": tot[P_FFNB:P_CONV].reshape(-1)[:nffn][None],
        "final_norm_w": tot[P_FNW],
    }
    loss = mrow[3 * HD + 4 * GH]
    given = {"c_ctx": (c_ctx, m_c_ctx, v_c_ctx), "w_mod": (w_mod, m_w_mod, v_w_mod), "b_mod": (b_mod, m_b_mod, v_b_mod),
             "q_norm_w": (q_norm_w, m_q_norm_w, v_q_norm_w), "k_norm_w": (k_norm_w, m_k_norm_w, v_k_norm_w),
             "conv_qkv_w": (conv_qkv_w, m_conv_qkv_w, v_conv_qkv_w), "a_log": (a_log, m_a_log, v_a_log),
             "dt_bias": (dt_bias, m_dt_bias, v_dt_bias), "gdn_norm_w": (gdn_norm_w, m_gdn_norm_w, v_gdn_norm_w),
             "ffn_conv_w": (ffn_conv_w, m_ffn_conv_w, v_ffn_conv_w), "ffn_conv_b": (ffn_conv_b, m_ffn_conv_b, v_ffn_conv_b),
             "final_norm_w": (final_norm_w, m_final_norm_w, v_final_norm_w)}
    for n, (w, m, v) in given.items():
        res[n] = (grads[n],) + _adamw(w, grads[n], m, v, name="adamw_" + n)

    order = ["c_ctx", "w_mod", "b_mod", "w_in", "q_norm_w", "k_norm_w", "conv_qkv_w", "a_log", "dt_bias", "gdn_norm_w",
             "w_pa", "w_pd", "w_out", "w_up", "ffn_conv_w", "ffn_conv_b", "w_down", "final_norm_w"]
    return (loss, grad_x[None], *[res[n][0] for n in order], *[res[n][1] for n in order],
            *[res[n][2] for n in order], *[res[n][3] for n in order])
```

```python
import functools
import math

import jax
import jax.numpy as jnp
from jax import lax
from jax.experimental import pallas as pl
from jax.experimental.pallas import tpu as pltpu

F32 = jnp.float32
BF16 = jnp.bfloat16
HI = lax.Precision.HIGHEST
MESH = pl.DeviceIdType.MESH

NDEV = 8
D = 1024
HD = 128
AH, AKV, GRP = 8, 2, 4
GH = 8
CH = 64
DFF = 2816
GRID_W = 64
EPS = 1e-6
ROPE_THETA = 10000.0
C_KV, C_QKV, C_BL, C_AQ, C_Z, C_GATE, C_END = 0, 512, 3584, 4096, 5120, 6144, 8192
W_BL, W_AQ, W_END = 3584, 3616, 7712
LR, B1, B2, AEPS, WD, STEP = 0.001, 0.9, 0.999, 1e-08, 0.01, 10
VMEM_BIG = 56 * 1024 * 1024
INTRA_FWD_CHUNKS = 18
INTRA_BWD_CHUNKS = 12


def _call(body, *, name, out_shape, grid=None, in_specs=None, out_specs=None, scratch=(), sem=None,
          vmem=None, aliases=None):
    params = {}
    if sem is not None:
        params["dimension_semantics"] = sem
    if vmem is not None:
        params["vmem_limit_bytes"] = vmem
    kw = {}
    if grid is not None:
        kw["grid"] = grid
    if in_specs is not None:
        kw["in_specs"] = in_specs
    if out_specs is not None:
        kw["out_specs"] = out_specs
    if aliases:
        kw["input_output_aliases"] = aliases
    return pl.pallas_call(body, name=name, out_shape=out_shape, scratch_shapes=list(scratch),
                          compiler_params=pltpu.CompilerParams(**params), **kw)


def _sds(shape, dtype=F32):
    return jax.ShapeDtypeStruct(tuple(shape), dtype)


def _dot(a, b, ca, cb):
    return lax.dot_general(a.astype(BF16), b.astype(BF16), (((ca,), (cb,)), ((), ())),
                           preferred_element_type=F32)


@jax.custom_vjp
def _nn(a, b):
    return _dot(a, b, 1, 0)


@jax.custom_vjp
def _nt(a, b):
    return _dot(a, b, 1, 1)


@jax.custom_vjp
def _tn(a, b):
    return _dot(a, b, 0, 0)


_nn.defvjp(lambda a, b: (_nn(a, b), (a, b)), lambda r, g: (_nt(g, r[1]), _tn(r[0], g)))
_nt.defvjp(lambda a, b: (_nt(a, b), (a, b)), lambda r, g: (_nn(g, r[1]), _tn(g, r[0])))
_tn.defvjp(lambda a, b: (_tn(a, b), (a, b)), lambda r, g: (_nt(r[1], g), _nn(r[0], g)))


def _hdot(a, b):
    return jnp.dot(a, b, precision=HI, preferred_element_type=F32)


def _mdot(a, b):
    return jnp.dot(a, b, precision=lax.Precision.HIGH, preferred_element_type=F32)


def _row_ids(shape):
    return lax.broadcasted_iota(jnp.int32, shape, 0)


def _shift_rows(x, down, bounds):
    n = x.shape[0]
    rows = _row_ids(x.shape)
    y = pltpu.roll(x, 1 if down else n - 1, 0)
    edge = functools.reduce(jnp.logical_or, [rows == (s if down else e - 1) for s, e in bounds])
    return jnp.where(edge, 0.0, y)


def _make_shift(bounds):
    @jax.custom_vjp
    def down(x):
        return _shift_rows(x, True, bounds)

    @jax.custom_vjp
    def up(x):
        return _shift_rows(x, False, bounds)

    down.defvjp(lambda x: (down(x), None), lambda _, g: (up(g),))
    up.defvjp(lambda x: (up(x), None), lambda _, g: (down(g),))
    return down, up


@jax.custom_vjp
def _swap32(x):
    lane = lax.broadcasted_iota(jnp.int32, x.shape, x.ndim - 1)
    return jnp.where((lane % 64) < 32, pltpu.roll(x, HD - 32, x.ndim - 1), pltpu.roll(x, 32, x.ndim - 1))


_swap32.defvjp(lambda x: (_swap32(x), None), lambda _, g: (_swap32(g),))


def _rms(x):
    return x * lax.rsqrt(jnp.mean(x * x, axis=-1, keepdims=True) + EPS)


def _silu(x):
    return x * jax.nn.sigmoid(x)


def _mm(a, b, *, name, M, N, K, ta=False, tb=False, out_dtype=F32, bm=None, bn=None, bk=None,
        a_off=(0, 0), b_off=(0, 0)):
    bm, bn, bk = bm or M, bn or N, bk or K
    assert M % bm == 0 and N % bn == 0 and K % bk == 0, (name, M, N, K, bm, bn, bk)
    nk = K // bk
    ca, cb = (0 if ta else 1), (1 if tb else 0)

    def body(a_ref, b_ref, o_ref, *acc):
        r = _dot(a_ref[...], b_ref[...], ca, cb)
        if nk == 1:
            o_ref[...] = r.astype(out_dtype)
        else:
            acc_ref, = acc
            k = pl.program_id(2)

            @pl.when(k == 0)
            def _():
                acc_ref[...] = r

            @pl.when(k > 0)
            def _():
                acc_ref[...] += r

            @pl.when(k == nk - 1)
            def _():
                o_ref[...] = acc_ref[...].astype(out_dtype)

    def blk(off, bshape):
        assert off[0] % bshape[0] == 0 and off[1] % bshape[1] == 0, (name, off, bshape)
        return off[0] // bshape[0], off[1] // bshape[1]

    if ta:
        ao = blk(a_off, (bk, bm))
        a_spec = pl.BlockSpec((bk, bm), lambda i, j, k: (k + ao[0], i + ao[1]))
    else:
        ao = blk(a_off, (bm, bk))
        a_spec = pl.BlockSpec((bm, bk), lambda i, j, k: (i + ao[0], k + ao[1]))
    if tb:
        bo = blk(b_off, (bn, bk))
        b_spec = pl.BlockSpec((bn, bk), lambda i, j, k: (j + bo[0], k + bo[1]))
    else:
        bo = blk(b_off, (bk, bn))
        b_spec = pl.BlockSpec((bk, bn), lambda i, j, k: (k + bo[0], j + bo[1]))
    return _call(body, name=name, out_shape=_sds((M, N), out_dtype), grid=(M // bm, N // bn, nk),
                 in_specs=[a_spec, b_spec], out_specs=pl.BlockSpec((bm, bn), lambda i, j, k: (i, j)),
                 scratch=[pltpu.VMEM((bm, bn), F32)] if nk > 1 else [],
                 sem=("parallel", "parallel", "arbitrary"), vmem=VMEM_BIG)(a, b)


def _normmod_fn(x, sh, sc):
    return _rms(x) * (1.0 + sc) + sh


def _normmod_fwd(x, mod, i_sh, i_sc, *, name, br=256):
    R = x.shape[0]

    def body(x_ref, mod_ref, o_ref):
        o_ref[...] = _normmod_fn(x_ref[...], mod_ref[i_sh:i_sh + 1, :], mod_ref[i_sc:i_sc + 1, :]).astype(BF16)

    return _call(body, name=name, out_shape=_sds((R, D), BF16), grid=(R // br,),
                 in_specs=[pl.BlockSpec((br, D), lambda i: (i, 0)), pl.BlockSpec((6, D), lambda i: (0, 0))],
                 out_specs=pl.BlockSpec((br, D), lambda i: (i, 0)), sem=("parallel",))(x, mod)


def _normmod_bwd(x, mod, i_sh, i_sc, dh, dh_off, res, *, name, br=256):
    R = x.shape[0]
    ob = dh_off // br
    has_res = res is not None

    def body(x_ref, mod_ref, dh_ref, *rest):
        if has_res:
            res_ref, dx_ref, dsh_ref, dsc_ref = rest
        else:
            dx_ref, dsh_ref, dsc_ref = rest
        sh, sc = mod_ref[i_sh:i_sh + 1, :], mod_ref[i_sc:i_sc + 1, :]
        _, vjp = jax.vjp(_normmod_fn, x_ref[...], sh, sc)
        dx, dsh, dsc = vjp(dh_ref[...])
        dx_ref[...] = dx + res_ref[...] if has_res else dx

        @pl.when(pl.program_id(0) == 0)
        def _():
            dsh_ref[...] = jnp.zeros_like(dsh_ref)
            dsc_ref[...] = jnp.zeros_like(dsc_ref)

        dsh_ref[...] += dsh
        dsc_ref[...] += dsc

    row = pl.BlockSpec((br, D), lambda i: (i, 0))
    vec = pl.BlockSpec((1, D), lambda i: (0, 0))
    ins = [row, pl.BlockSpec((6, D), lambda i: (0, 0)), pl.BlockSpec((br, D), lambda i: (i + ob, 0))]
    args = [x, mod, dh]
    if has_res:
        ins.append(row)
        args.append(res)
    return _call(body, name=name, out_shape=(_sds((R, D)), _sds((1, D)), _sds((1, D))), grid=(R // br,),
                 in_specs=ins, out_specs=(row, vec, vec), sem=("arbitrary",))(*args)


def _rope(x, cos, sin):
    return x * cos + _swap32(x) * sin


def _aprep_fn(qs, ks, cos, sin, qw, kw):
    return ([_rope(_rms(q) * qw, cos, sin) for q in qs], [_rope(_rms(k) * kw, cos, sin) for k in ks])


def _aprep_fwd(proj, cos, sin, qw, kw, *, br=256):
    T = proj.shape[0]

    def body(aq_ref, kv_ref, cos_ref, sin_ref, qw_ref, kw_ref, q_ref, k_ref, v_ref):
        qs = [aq_ref[:, h * HD:(h + 1) * HD] for h in range(AH)]
        ks = [kv_ref[:, h * HD:(h + 1) * HD] for h in range(AKV)]
        qo, ko = _aprep_fn(qs, ks, cos_ref[...], sin_ref[...], qw_ref[...], kw_ref[...])
        for h in range(AH):
            q_ref[h] = qo[h].astype(BF16)
        for h in range(AKV):
            k_ref[h] = ko[h].astype(BF16)
            v_ref[h] = kv_ref[:, (AKV + h) * HD:(AKV + h + 1) * HD].astype(BF16)

    tab = pl.BlockSpec((br, HD), lambda i: (i, 0))
    vec = pl.BlockSpec((1, HD), lambda i: (0, 0))
    return _call(body, name="aprep_fwd",
                 out_shape=(_sds((AH, T, HD), BF16), _sds((AKV, T, HD), BF16), _sds((AKV, T, HD), BF16)),
                 grid=(T // br,),
                 in_specs=[pl.BlockSpec((br, AH * HD), lambda i: (i, C_AQ // (AH * HD))),
                           pl.BlockSpec((br, 2 * AKV * HD), lambda i: (i, 0)), tab, tab, vec, vec],
                 out_specs=(pl.BlockSpec((AH, br, HD), lambda i: (0, i, 0)),
                            pl.BlockSpec((AKV, br, HD), lambda i: (0, i, 0)),
                            pl.BlockSpec((AKV, br, HD), lambda i: (0, i, 0))),
                 sem=("parallel",))(proj, proj, cos, sin, qw, kw)


def _aprep_bwd(proj, cos, sin, qw, kw, dq, dk, dv, L, *, br=256):
    T = proj.shape[0]
    lb = L // br

    def body(aq_ref, kv_ref, cos_ref, sin_ref, qw_ref, kw_ref, dq_ref, dk_ref, dv_ref,
             daq_ref, dkv_ref, dqw_ref, dkw_ref):
        i = pl.program_id(0)
        qs = [aq_ref[:, h * HD:(h + 1) * HD] for h in range(AH)]
        ks = [kv_ref[:, h * HD:(h + 1) * HD] for h in range(AKV)]
        _, vjp = jax.vjp(_aprep_fn, qs, ks, cos_ref[...], sin_ref[...], qw_ref[...], kw_ref[...])
        is_lat = i >= lb
        dqs = [jnp.where(is_lat, dq_ref[h], 0.0) for h in range(AH)]
        dks = [dk_ref[h] for h in range(AKV)]
        gq, gk, _, _, gqw, gkw = vjp((dqs, dks))
        for h in range(AH):
            daq_ref[:, h * HD:(h + 1) * HD] = gq[h].astype(BF16)
        for h in range(AKV):
            dkv_ref[:, h * HD:(h + 1) * HD] = gk[h].astype(BF16)
            dkv_ref[:, (AKV + h) * HD:(AKV + h + 1) * HD] = dv_ref[h].astype(BF16)

        @pl.when(i == 0)
        def _():
            dqw_ref[...] = jnp.zeros_like(dqw_ref)
            dkw_ref[...] = jnp.zeros_like(dkw_ref)

        dqw_ref[...] += gqw
        dkw_ref[...] += gkw

    tab = pl.BlockSpec((br, HD), lambda i: (i, 0))
    vec = pl.BlockSpec((1, HD), lambda i: (0, 0))
    kvb = pl.BlockSpec((AKV, br, HD), lambda i: (0, i, 0))
    return _call(body, name="aprep_bwd",
                 out_shape=(_sds((T, AH * HD), BF16), _sds((T, 2 * AKV * HD), BF16), _sds((1, HD)), _sds((1, HD))),
                 grid=(T // br,),
                 in_specs=[pl.BlockSpec((br, AH * HD), lambda i: (i, C_AQ // (AH * HD))),
                           pl.BlockSpec((br, 2 * AKV * HD), lambda i: (i, 0)), tab, tab, vec, vec,
                           pl.BlockSpec((AH, br, HD), lambda i: (0, jnp.maximum(i - lb, 0), 0)), kvb, kvb],
                 out_specs=(pl.BlockSpec((br, AH * HD), lambda i: (i, 0)),
                            pl.BlockSpec((br, 2 * AKV * HD), lambda i: (i, 0)), vec, vec),
                 sem=("arbitrary",))(proj, proj, cos, sin, qw, kw, dq, dk, dv)


def _attn_fn(q, k, v):
    s = _nt(q, k) * (HD ** -0.5)
    m = lax.stop_gradient(jnp.max(s, axis=-1, keepdims=True))
    e = jnp.exp(s - m)
    p = e / jnp.sum(e, axis=-1, keepdims=True)
    return _nn(p, v)


def _attn_fwd(q, k, v, L, *, bq=128):
    T = q.shape[1]
    N = T - L
    lb = L // bq

    def body(q_ref, k_ref, v_ref, o_ref):
        qv = q_ref[...].reshape(GRP * bq, HD).astype(F32)
        o = _attn_fn(qv, k_ref[...].astype(F32), v_ref[...].astype(F32))
        for g in range(GRP):
            o_ref[:, g * HD:(g + 1) * HD] = o[g * bq:(g + 1) * bq].astype(BF16)

    kvb = pl.BlockSpec((None, T, HD), lambda g, i: (g, 0, 0))
    return _call(body, name="attn_fwd", out_shape=_sds((N, AH * HD), BF16), grid=(AKV, N // bq),
                 in_specs=[pl.BlockSpec((GRP, bq, HD), lambda g, i: (g, i + lb, 0)), kvb, kvb],
                 out_specs=pl.BlockSpec((bq, GRP * HD), lambda g, i: (i, g)),
                 sem=("parallel", "parallel"), vmem=VMEM_BIG)(q, k, v)


def _attn_bwd(q, k, v, do, L, *, bq=128):
    T = q.shape[1]
    N = T - L
    lb = L // bq

    def body(q_ref, k_ref, v_ref, do_ref, dq_ref, dk_ref, dv_ref):
        qv = q_ref[...].reshape(GRP * bq, HD).astype(F32)
        _, vjp = jax.vjp(_attn_fn, qv, k_ref[...].astype(F32), v_ref[...].astype(F32))
        dov = jnp.concatenate([do_ref[:, g * HD:(g + 1) * HD] for g in range(GRP)], axis=0)
        dq, dk, dv = vjp(dov)
        dq_ref[...] = dq.reshape(GRP, bq, HD)

        @pl.when(pl.program_id(1) == 0)
        def _():
            dk_ref[...] = jnp.zeros_like(dk_ref)
            dv_ref[...] = jnp.zeros_like(dv_ref)

        dk_ref[...] += dk
        dv_ref[...] += dv

    kvb = pl.BlockSpec((None, T, HD), lambda g, i: (g, 0, 0))
    return _call(body, name="attn_bwd",
                 out_shape=(_sds((AH, N, HD)), _sds((AKV, T, HD)), _sds((AKV, T, HD))), grid=(AKV, N // bq),
                 in_specs=[pl.BlockSpec((GRP, bq, HD), lambda g, i: (g, i + lb, 0)), kvb, kvb,
                           pl.BlockSpec((bq, GRP * HD), lambda g, i: (i, g))],
                 out_specs=(pl.BlockSpec((GRP, bq, HD), lambda g, i: (g, i, 0)), kvb, kvb),
                 sem=("parallel", "arbitrary"), vmem=VMEM_BIG)(q, k, v, do)


def _gprep_fn(kind, shifts, x, w):
    down, up = shifts
    y = down(x) * w[0:1, :] + x * w[1:2, :] + up(x) * w[2:3, :]
    a = _silu(y)
    if kind == 2:
        return a
    a = a * lax.rsqrt(jnp.sum(a * a, axis=-1, keepdims=True) + EPS)
    return a * (HD ** -0.5) if kind == 0 else a


def _gprep_fwd(proj, conv_w, kind, bounds):
    T = proj.shape[0]
    shifts = _make_shift(bounds)
    cb = C_QKV // HD + kind * GH

    def body(x_ref, w_ref, o_ref):
        o_ref[...] = _gprep_fn(kind, shifts, x_ref[...], w_ref[...])

    return _call(body, name=f"gprep_fwd{kind}", out_shape=_sds((GH, T, HD)), grid=(GH,),
                 in_specs=[pl.BlockSpec((T, HD), lambda h: (0, cb + h)),
                           pl.BlockSpec((3, HD), lambda h: (0, kind * GH + h))],
                 out_specs=pl.BlockSpec((None, T, HD), lambda h: (h, 0, 0)), sem=("parallel",))(proj, conv_w)


def _gprep_bwd(proj, conv_w, kind, bounds, dy):
    T = proj.shape[0]
    shifts = _make_shift(bounds)
    cb = C_QKV // HD + kind * GH

    def body(x_ref, w_ref, dy_ref, dx_ref, dw_ref):
        _, vjp = jax.vjp(functools.partial(_gprep_fn, kind, shifts), x_ref[...], w_ref[...])
        dx, dw = vjp(dy_ref[0] + dy_ref[1])
        dx_ref[...] = dx.astype(BF16)
        dw_ref[...] = dw

    return _call(body, name=f"gprep_bwd{kind}", out_shape=(_sds((T, GH * HD), BF16), _sds((3, GH * HD))), grid=(GH,),
                 in_specs=[pl.BlockSpec((T, HD), lambda h: (0, cb + h)),
                           pl.BlockSpec((3, HD), lambda h: (0, kind * GH + h)),
                           pl.BlockSpec((2, None, T, HD), lambda h: (0, h, 0, 0))],
                 out_specs=(pl.BlockSpec((T, HD), lambda h: (0, h)), pl.BlockSpec((3, HD), lambda h: (0, h))),
                 sem=("parallel",))(proj, conv_w, dy)


def _bl_fn(x, alog, dtb):
    lane = lax.broadcasted_iota(jnp.int32, x.shape, 1)
    beta = jax.nn.sigmoid(x)
    z = x + dtb
    sp = jnp.maximum(z, 0.0) + jnp.log1p(jnp.exp(-jnp.abs(z)))
    la = -jnp.exp(alog) * sp
    return jnp.where(lane < 2 * GH, beta, jnp.where(lane < 4 * GH, la, 0.0))


def _bl_fwd(proj, alog, dtb, *, br=256):
    T = proj.shape[0]

    def body(x_ref, a_ref, d_ref, o_ref):
        o_ref[...] = _bl_fn(x_ref[...], a_ref[...], d_ref[...])

    vec = pl.BlockSpec((1, HD), lambda i: (0, 0))
    return _call(body, name="bl_fwd", out_shape=_sds((T, HD)), grid=(T // br,),
                 in_specs=[pl.BlockSpec((br, HD), lambda i: (i, C_BL // HD)), vec, vec],
                 out_specs=pl.BlockSpec((br, HD), lambda i: (i, 0)), sem=("parallel",))(proj, alog, dtb)


def _bl_bwd(proj, alog, dtb, dbl, *, br=256):
    T = proj.shape[0]

    def body(x_ref, a_ref, d_ref, g_ref, dx_ref, da_ref, dd_ref):
        g = g_ref[0, 0]
        for d in range(2):
            for h in range(GH):
                if d or h:
                    g = g + g_ref[d, h]
        _, vjp = jax.vjp(_bl_fn, x_ref[...], a_ref[...], d_ref[...])
        dx, da, dd = vjp(g)
        dx_ref[...] = dx.astype(BF16)

        @pl.when(pl.program_id(0) == 0)
        def _():
            da_ref[...] = jnp.zeros_like(da_ref)
            dd_ref[...] = jnp.zeros_like(dd_ref)

        da_ref[...] += da
        dd_ref[...] += dd

    vec = pl.BlockSpec((1, HD), lambda i: (0, 0))
    return _call(body, name="bl_bwd", out_shape=(_sds((T, HD), BF16), _sds((1, HD)), _sds((1, HD))), grid=(T // br,),
                 in_specs=[pl.BlockSpec((br, HD), lambda i: (i, C_BL // HD)), vec, vec,
                           pl.BlockSpec((2, GH, br, HD), lambda i: (0, 0, i, 0))],
                 out_specs=(pl.BlockSpec((br, HD), lambda i: (i, 0)), vec, vec), sem=("arbitrary",))(proj, alog, dtb, dbl)


def _chunk_masks(d):
    ii = lax.broadcasted_iota(jnp.int32, (CH, CH), 0)
    jj = lax.broadcasted_iota(jnp.int32, (CH, CH), 1)
    eye = (ii == jj).astype(F32)
    before = jnp.where(d == 0, (jj < ii).astype(F32), (jj > ii).astype(F32))
    return before, before + eye, eye


def _intra_fn(masks, sel_b, sel_l, qs, ks, vs, bls):
    before, ateq, eye = masks
    ones = jnp.ones((CH, CH), F32)
    inc = ateq > 0.0
    each = lambda f, *ls: [f(*t) for t in zip(*ls)]
    beta = each(lambda bl: jnp.sum(bl * sel_b, axis=-1, keepdims=True), bls)
    la = each(lambda bl: jnp.sum(bl * sel_l, axis=-1, keepdims=True), bls)
    gam = each(lambda a: _hdot(ateq, jnp.broadcast_to(a, (CH, HD))), la)
    gi = each(lambda a: _hdot(ateq, jnp.broadcast_to(a, (CH, CH))), la)
    gj = each(lambda g: _hdot(ones, eye * g), gi)
    kk = each(lambda k: _nt(k, k), ks)
    qk = each(_nt, qs, ks)
    dec = each(lambda a, b: jnp.where(inc, jnp.exp(jnp.where(inc, a - b, 0.0)), 0.0), gi, gj)
    lmat = each(lambda b, d, m: before * (b * d * m), beta, dec, kk)
    x = each(lambda m: eye - m, lmat)
    p2 = each(lambda m: _mdot(m, m), lmat)
    for it in range(5):
        x = each(lambda a, b: a + _mdot(a, b), x, p2)
        if it < 4:
            p2 = each(lambda m: _mdot(m, m), p2)
    eg = each(jnp.exp, gam)
    u = each(lambda a, b, v: _mdot(a, b * v), x, beta, vs)
    w = each(lambda a, b, e, k: _mdot(a, (b * e) * k), x, beta, eg, ks)
    tot = each(lambda a: jnp.sum(a, axis=0, keepdims=True), la)
    kd = each(lambda k, t, g: k * jnp.exp(t - g), ks, tot, gam)
    gl = each(lambda t: jnp.broadcast_to(jnp.exp(t), (1, HD)), tot)
    qd = each(lambda q, e: q * e, qs, eg)
    p = each(lambda d, m: d * m, dec, qk)
    return u, w, kd, qd, p, gl


def _dir_head_sel(d, h):
    lane = lax.broadcasted_iota(jnp.int32, (1, HD), 1)
    return (lane == d * GH + h).astype(F32), (lane == 2 * GH + d * GH + h).astype(F32)


def _intra_specs(T, G):
    nc = T // CH
    assert nc % G == 0
    qkv = pl.BlockSpec((None, G * CH, HD), lambda d, h, c: (h, c, 0))
    bl = pl.BlockSpec((G * CH, HD), lambda d, h, c: (c, 0))
    big = pl.BlockSpec((None, None, G * CH, HD), lambda d, h, c: (d, h, c, 0))
    pm = pl.BlockSpec((None, None, G * CH, CH), lambda d, h, c: (d, h, c, 0))
    gl = pl.BlockSpec((None, None, G, 1, HD), lambda d, h, c: (d, h, c, 0, 0))
    shapes = (_sds((2, GH, T, HD)),) + (_sds((2, GH, T, HD), BF16),) * 3 + (_sds((2, GH, T, CH), BF16),
                                                                           _sds((2, GH, nc, 1, HD)))
    return nc, qkv, bl, big, pm, gl, shapes


def _chunks_per_step(T, most):
    nc = T // CH
    return max(g for g in range(1, most + 1) if nc % g == 0)


def _intra_fwd(q, k, v, bl):
    T = q.shape[1]
    G = _chunks_per_step(T, INTRA_FWD_CHUNKS)
    nc, qkv_s, bl_s, big, pm, gl_s, shapes = _intra_specs(T, G)

    def body(q_ref, k_ref, v_ref, bl_ref, u_ref, w_ref, kd_ref, qd_ref, p_ref, gl_ref):
        d, h = pl.program_id(0), pl.program_id(1)
        sb, sl = _dir_head_sel(d, h)
        rows = [slice(g * CH, (g + 1) * CH) for g in range(G)]
        outs = _intra_fn(_chunk_masks(d), sb, sl, *[[r[s, :] for s in rows] for r in (q_ref, k_ref, v_ref, bl_ref)])
        for g in range(G):
            for r, o in zip((u_ref, w_ref, kd_ref, qd_ref, p_ref), outs[:5]):
                r[rows[g], :] = o[g].astype(r.dtype)
            gl_ref[g] = outs[5][g]

    return _call(body, name="gdn_intra_fwd", out_shape=shapes, grid=(2, GH, nc // G),
                 in_specs=[qkv_s, qkv_s, qkv_s, bl_s], out_specs=(big, big, big, big, pm, gl_s),
                 sem=("parallel", "parallel", "parallel"))(q, k, v, bl)


def _intra_bwd(q, k, v, bl, cts):
    T = q.shape[1]
    G = _chunks_per_step(T, INTRA_BWD_CHUNKS)
    nc, qkv_s, bl_s, big, pm, gl_s, _ = _intra_specs(T, G)

    def body(q_ref, k_ref, v_ref, bl_ref, du, dw, dkd, dqd, dp, dgl, dq_ref, dk_ref, dv_ref, dbl_ref):
        d, h = pl.program_id(0), pl.program_id(1)
        sb, sl = _dir_head_sel(d, h)
        fn = functools.partial(_intra_fn, _chunk_masks(d), sb, sl)
        rows = [slice(g * CH, (g + 1) * CH) for g in range(G)]
        _, vjp = jax.vjp(fn, *[[r[s, :] for s in rows] for r in (q_ref, k_ref, v_ref, bl_ref)])
        cts = tuple([r[s, :] for s in rows] for r in (du, dw, dkd, dqd, dp)) + ([dgl[g] for g in range(G)],)
        grads = vjp(cts)
        for g in range(G):
            for r, o in zip((dq_ref, dk_ref, dv_ref, dbl_ref), grads):
                r[rows[g], :] = o[g]

    return _call(body, name="gdn_intra_bwd", out_shape=(_sds((2, GH, T, HD)),) * 4, grid=(2, GH, nc // G),
                 in_specs=[qkv_s, qkv_s, qkv_s, bl_s, big, big, big, big, pm, gl_s], out_specs=(big,) * 4,
                 sem=("parallel", "parallel", "parallel"))(q, k, v, bl, *cts)


def _scan_fn(s, u, w, kd, qd, p, gl):
    each = lambda f, *ls: [f(*t) for t in zip(*ls)]
    ws = each(_nn, w, s)
    delta = each(lambda a, b: a - b, u, ws)
    kdd = each(_tn, kd, delta)
    s_new = each(lambda g, a, b: g * a + b, gl, s, kdd)
    qs = each(_nn, qd, s)
    pd = each(_nn, p, delta)
    return each(lambda a, b: a + b, qs, pd), s_new


SCAN_BLOCK = 4


def _scan_visit(t, d, nb, ncb):
    rev = jnp.where(t < ncb, ncb - 1 - t, nb - 1 - (t - ncb))
    return jnp.where(d == 0, t, rev)


def _scan_specs(T, L, back):
    tb = SCAN_BLOCK * CH
    assert T % tb == 0 and L % tb == 0
    nb, ncb = T // tb, L // tb

    def at(d, t):
        return _scan_visit(nb - 1 - t if back else t, d, nb, ncb)

    big = pl.BlockSpec((None, GH, tb, HD), lambda d, t: (d, 0, at(d, t), 0))
    pm = pl.BlockSpec((None, GH, tb, CH), lambda d, t: (d, 0, at(d, t), 0))
    gl = pl.BlockSpec((None, GH, SCAN_BLOCK, 1, HD), lambda d, t: (d, 0, at(d, t), 0, 0))
    st = pl.BlockSpec((None, GH, SCAN_BLOCK, HD, HD), lambda d, t: (d, 0, at(d, t), 0, 0))
    do = pl.BlockSpec((GH, tb, HD), lambda d, t: (0, at(d, t), 0))
    return nb, big, pm, gl, st, do


def _scan_fwd(u, w, kd, qd, p, gl, L):
    T = u.shape[2]
    nb, big, pm, gl_s, st, _ = _scan_specs(T, L, False)
    heads = range(GH)

    def body(u_ref, w_ref, kd_ref, qd_ref, p_ref, gl_ref, o_ref, st_ref, s_scr):
        d = pl.program_id(0)

        @pl.when(pl.program_id(1) == 0)
        def _():
            s_scr[...] = jnp.zeros_like(s_scr)

        s = [s_scr[h] for h in heads]
        for i in range(SCAN_BLOCK):
            c = jnp.where(d == 0, i, SCAN_BLOCK - 1 - i)
            rows = pl.ds(pl.multiple_of(c * CH, CH), CH)
            for h in heads:
                st_ref[h, c] = s[h]
            o, s = _scan_fn(s, *[[r[h, rows, :].astype(F32) for h in heads] for r in (u_ref, w_ref, kd_ref, qd_ref, p_ref)],
                            [gl_ref[h, c] for h in heads])
            for h in heads:
                o_ref[h, rows, :] = o[h]
        for h in heads:
            s_scr[h] = s[h]

    return _call(body, name="gdn_scan_fwd", out_shape=(_sds((2, GH, T, HD)), _sds((2, GH, T // CH, HD, HD))),
                 grid=(2, nb), in_specs=[big, big, big, big, pm, gl_s], out_specs=(big, st),
                 scratch=[pltpu.VMEM((GH, HD, HD), F32)], sem=("parallel", "arbitrary"))(u, w, kd, qd, p, gl)


def _scan_bwd(u, w, kd, qd, p, gl, states, do, L):
    T = u.shape[2]
    nb, big, pm, gl_s, st, do_s = _scan_specs(T, L, True)
    heads = range(GH)

    def body(u_ref, w_ref, kd_ref, qd_ref, p_ref, gl_ref, st_ref, do_ref,
             du_ref, dw_ref, dkd_ref, dqd_ref, dp_ref, dgl_ref, ds_scr):
        d = pl.program_id(0)

        @pl.when(pl.program_id(1) == 0)
        def _():
            ds_scr[...] = jnp.zeros_like(ds_scr)

        ds = [ds_scr[h] for h in heads]
        for i in range(SCAN_BLOCK):
            c = jnp.where(d == 0, SCAN_BLOCK - 1 - i, i)
            rows = pl.ds(pl.multiple_of(c * CH, CH), CH)
            _, vjp = jax.vjp(_scan_fn, [st_ref[h, c] for h in heads],
                             *[[r[h, rows, :].astype(F32) for h in heads] for r in (u_ref, w_ref, kd_ref, qd_ref, p_ref)],
                             [gl_ref[h, c] for h in heads])
            ds, gu, gw, gkd, gqd, gp, ggl = vjp(([do_ref[h, rows, :] for h in heads], ds))
            for h in heads:
                du_ref[h, rows, :] = gu[h]
                dw_ref[h, rows, :] = gw[h]
                dkd_ref[h, rows, :] = gkd[h]
                dqd_ref[h, rows, :] = gqd[h]
                dp_ref[h, rows, :] = gp[h]
                dgl_ref[h, c] = ggl[h]
        for h in heads:
            ds_scr[h] = ds[h]

    return _call(body, name="gdn_scan_bwd",
                 out_shape=(_sds((2, GH, T, HD)),) * 4 + (_sds((2, GH, T, CH)), _sds((2, GH, T // CH, 1, HD))),
                 grid=(2, nb), in_specs=[big, big, big, big, pm, gl_s, st, do_s],
                 out_specs=(big, big, big, big, pm, gl_s), scratch=[pltpu.VMEM((GH, HD, HD), F32)],
                 sem=("parallel", "arbitrary"))(u, w, kd, qd, p, gl, states, do)


def _gout_fn(o0, o1, z, gw):
    return _rms(o0 + o1) * gw * _silu(z)


def _gout_fwd(o, proj, gw, L, *, br=256):
    T = o.shape[2]
    N = T - L
    lb = L // br
    ob = pl.BlockSpec((None, None, br, HD), lambda i, h: (0, h, i + lb, 0))
    ob1 = pl.BlockSpec((None, None, br, HD), lambda i, h: (1, h, i + lb, 0))

    def body(o0_ref, o1_ref, z_ref, gw_ref, y_ref):
        y_ref[...] = _gout_fn(o0_ref[...], o1_ref[...], z_ref[...], gw_ref[...]).astype(BF16)

    return _call(body, name="gout_fwd", out_shape=_sds((N, GH * HD), BF16), grid=(N // br, GH),
                 in_specs=[ob, ob1, pl.BlockSpec((br, HD), lambda i, h: (i + lb, C_Z // HD + h)),
                           pl.BlockSpec((1, HD), lambda i, h: (0, 0))],
                 out_specs=pl.BlockSpec((br, HD), lambda i, h: (i, h)), sem=("parallel", "parallel"))(o, o, proj, gw)


def _gout_bwd(o, proj, gw, dy, L, *, br=256):
    T = o.shape[2]
    lb = L // br
    ob = pl.BlockSpec((None, None, br, HD), lambda i, h: (0, h, i, 0))
    ob1 = pl.BlockSpec((None, None, br, HD), lambda i, h: (1, h, i, 0))

    def body(o0_ref, o1_ref, z_ref, gw_ref, dy_ref, do_ref, dz_ref, dgw_ref):
        i, h = pl.program_id(0), pl.program_id(1)
        _, vjp = jax.vjp(_gout_fn, o0_ref[...], o1_ref[...], z_ref[...], gw_ref[...])
        g0, _, gz, ggw = vjp(dy_ref[...])
        lat = i >= lb
        do_ref[...] = jnp.where(lat, g0, 0.0)
        dz_ref[...] = jnp.where(lat, gz, 0.0).astype(BF16)

        @pl.when(jnp.logical_and(i == 0, h == 0))
        def _():
            dgw_ref[...] = jnp.zeros_like(dgw_ref)

        dgw_ref[...] += jnp.where(lat, ggw, 0.0)

    return _call(body, name="gout_bwd", out_shape=(_sds((GH, T, HD)), _sds((T, GH * HD), BF16), _sds((1, HD))),
                 grid=(T // br, GH),
                 in_specs=[ob, ob1, pl.BlockSpec((br, HD), lambda i, h: (i, C_Z // HD + h)),
                           pl.BlockSpec((1, HD), lambda i, h: (0, 0)),
                           pl.BlockSpec((br, HD), lambda i, h: (jnp.maximum(i - lb, 0), h))],
                 out_specs=(pl.BlockSpec((None, br, HD), lambda i, h: (h, i, 0)),
                            pl.BlockSpec((br, HD), lambda i, h: (i, h)),
                            pl.BlockSpec((1, HD), lambda i, h: (0, 0))),
                 sem=("arbitrary", "arbitrary"))(o, o, proj, gw, dy)


def _merge_fn(pa, pd, ga, gd):
    return jax.nn.sigmoid(ga) * pa + jax.nn.sigmoid(gd) * pd


def _merge_fwd(pa, pd, proj, L, *, br=256):
    N = pa.shape[0]
    lb = L // br
    row = pl.BlockSpec((br, D), lambda i: (i, 0))

    def body(pa_ref, pd_ref, ga_ref, gd_ref, y_ref):
        y_ref[...] = _merge_fn(pa_ref[...], pd_ref[...], ga_ref[...], gd_ref[...]).astype(BF16)

    return _call(body, name="merge_fwd", out_shape=_sds((N, D), BF16), grid=(N // br,),
                 in_specs=[row, row, pl.BlockSpec((br, D), lambda i: (i + lb, C_GATE // D)),
                           pl.BlockSpec((br, D), lambda i: (i + lb, C_GATE // D + 1))],
                 out_specs=row, sem=("parallel",))(pa, pd, proj, proj)


def _merge_bwd(pa, pd, proj, dy, L, *, br=256):
    N = pa.shape[0]
    T = N + L
    lb = L // br
    lrow = pl.BlockSpec((br, D), lambda i: (jnp.maximum(i - lb, 0), 0))

    def body(pa_ref, pd_ref, ga_ref, gd_ref, dy_ref, dpa_ref, dpd_ref, dg_ref):
        lat = pl.program_id(0) >= lb
        _, vjp = jax.vjp(_merge_fn, pa_ref[...], pd_ref[...], ga_ref[...], gd_ref[...])
        gpa, gpd, gga, ggd = vjp(dy_ref[...])
        dpa_ref[...] = gpa.astype(BF16)
        dpd_ref[...] = gpd.astype(BF16)
        dg_ref[:, :D] = jnp.where(lat, gga, 0.0).astype(BF16)
        dg_ref[:, D:] = jnp.where(lat, ggd, 0.0).astype(BF16)

    return _call(body, name="merge_bwd", out_shape=(_sds((N, D), BF16), _sds((N, D), BF16), _sds((T, 2 * D), BF16)),
                 grid=(T // br,),
                 in_specs=[lrow, lrow, pl.BlockSpec((br, D), lambda i: (i, C_GATE // D)),
                           pl.BlockSpec((br, D), lambda i: (i, C_GATE // D + 1)), lrow],
                 out_specs=(lrow, lrow, pl.BlockSpec((br, 2 * D), lambda i: (i, 0))),
                 sem=("arbitrary",))(pa, pd, proj, proj, dy)


def _resid_fwd(x, m, mod, i_g, *, name, br=256):
    R = x.shape[0]
    row = pl.BlockSpec((br, D), lambda i: (i, 0))

    def body(x_ref, m_ref, mod_ref, o_ref):
        o_ref[...] = x_ref[...] + mod_ref[i_g:i_g + 1, :] * m_ref[...]

    return _call(body, name=name, out_shape=_sds((R, D)), grid=(R // br,),
                 in_specs=[row, row, pl.BlockSpec((6, D), lambda i: (0, 0))], out_specs=row,
                 sem=("parallel",))(x, m, mod)


def _resid_bwd(dx, m, mod, i_g, *, name, br=256):
    R = dx.shape[0]
    row = pl.BlockSpec((br, D), lambda i: (i, 0))
    vec = pl.BlockSpec((1, D), lambda i: (0, 0))

    def body(dx_ref, m_ref, mod_ref, dm_ref, dg_ref):
        dxv = dx_ref[...]
        dm_ref[...] = (dxv * mod_ref[i_g:i_g + 1, :]).astype(BF16)

        @pl.when(pl.program_id(0) == 0)
        def _():
            dg_ref[...] = jnp.zeros_like(dg_ref)

        dg_ref[...] += jnp.sum(dxv * m_ref[...], axis=0, keepdims=True)

    return _call(body, name=name, out_shape=(_sds((R, D), BF16), _sds((1, D))), grid=(R // br,),
                 in_specs=[row, row, pl.BlockSpec((6, D), lambda i: (0, 0))], out_specs=(row, vec),
                 sem=("arbitrary",))(dx, m, mod)


def _ffn_fn(shifts, ug, uv, wg, wv, bg, bv):
    down, up = shifts

    def conv(x, w, b):
        return down(x) * w[0:1, :] + x * w[1:2, :] + up(x) * w[2:3, :] + b

    return _silu(conv(ug, wg, bg)) * conv(uv, wv, bv)


def _ffn_fwd(up, cw, cb, *, bw=256):
    N = up.shape[0]
    shifts = _make_shift(((0, N),))
    nb = DFF // bw

    def body(ug, uv, wg, wv, bg, bv, a_ref):
        a_ref[...] = _ffn_fn(shifts, ug[...], uv[...], wg[...], wv[...], bg[...], bv[...]).astype(BF16)

    def col(rows, off):
        return pl.BlockSpec((rows, bw), lambda j: (0, j + off))

    return _call(body, name="ffn_fwd", out_shape=_sds((N, DFF), BF16), grid=(nb,),
                 in_specs=[col(N, 0), col(N, nb), col(3, 0), col(3, nb), col(1, 0), col(1, nb)],
                 out_specs=col(N, 0), sem=("parallel",), vmem=VMEM_BIG)(up, up, cw, cw, cb, cb)


def _ffn_bwd(up, cw, cb, da, *, bw=256):
    N = up.shape[0]
    shifts = _make_shift(((0, N),))
    nb = DFF // bw

    def body(ug, uv, wg, wv, bg, bv, da_ref, dug, duv, dwg, dwv, dbg, dbv):
        _, vjp = jax.vjp(functools.partial(_ffn_fn, shifts), ug[...], uv[...], wg[...], wv[...], bg[...], bv[...])
        g = vjp(da_ref[...])
        dug[...] = g[0].astype(BF16)
        duv[...] = g[1].astype(BF16)
        dwg[...], dwv[...], dbg[...], dbv[...] = g[2], g[3], g[4], g[5]

    def col(rows, off):
        return pl.BlockSpec((rows, bw), lambda j: (0, j + off))

    half = (_sds((N, DFF), BF16), _sds((N, DFF), BF16), _sds((3, DFF)), _sds((3, DFF)), _sds((1, DFF)), _sds((1, DFF)))
    dug, duv, dwg, dwv, dbg, dbv = _call(
        body, name="ffn_bwd", out_shape=half, grid=(nb,),
        in_specs=[col(N, 0), col(N, nb), col(3, 0), col(3, nb), col(1, 0), col(1, nb), col(N, 0)],
        out_specs=(col(N, 0), col(N, 0), col(3, 0), col(3, 0), col(1, 0), col(1, 0)),
        sem=("parallel",), vmem=VMEM_BIG)(up, up, cw, cw, cb, cb, da)
    return (jnp.concatenate([dug, duv], axis=1), jnp.concatenate([dwg, dwv], axis=1),
            jnp.concatenate([dbg, dbv], axis=1))


def _head_fn(x1, dn, g2, fw, tgt):
    y = _rms(x1 + g2 * dn) * fw
    err = y - tgt
    return 0.5 * jnp.sum(jnp.mean(err * err, axis=-1))


def _head(x1, dn, mod, fw, tgt, *, br=256):
    N = x1.shape[0]
    row = pl.BlockSpec((br, D), lambda i: (i, 0))
    vec = pl.BlockSpec((1, D), lambda i: (0, 0))
    one = pl.BlockSpec((1, HD), lambda i: (0, 0))

    def body(x1_ref, dn_ref, mod_ref, fw_ref, tgt_ref, loss_ref, dx_ref, ddn_ref, dg_ref, dfw_ref):
        loss, (gx, gdn, gg, gfw) = jax.value_and_grad(_head_fn, argnums=(0, 1, 2, 3))(
            x1_ref[...], dn_ref[...], mod_ref[5:6, :], fw_ref[...], tgt_ref[...])
        dx_ref[...] = gx
        ddn_ref[...] = gdn.astype(BF16)

        @pl.when(pl.program_id(0) == 0)
        def _():
            loss_ref[...] = jnp.zeros_like(loss_ref)
            dg_ref[...] = jnp.zeros_like(dg_ref)
            dfw_ref[...] = jnp.zeros_like(dfw_ref)

        loss_ref[...] += jnp.broadcast_to(loss, (1, HD))
        dg_ref[...] += gg
        dfw_ref[...] += gfw

    return _call(body, name="head", out_shape=(_sds((1, HD)), _sds((N, D)), _sds((N, D), BF16), _sds((1, D)), _sds((1, D))),
                 grid=(N // br,), in_specs=[row, row, pl.BlockSpec((6, D), lambda i: (0, 0)), vec, row],
                 out_specs=(one, row, row, vec, vec), sem=("arbitrary",))(x1, dn, mod, fw, tgt)


def _adamw(w, g, m, v, *, name):
    shape = w.shape
    cols = shape[-1]
    rows = max(1, math.prod(shape[:-1]))
    w2, g2, m2, v2 = (t.reshape(rows, cols) for t in (w, g, m, v))
    br = 256 if rows % 256 == 0 else (128 if rows % 128 == 0 else (8 if rows % 8 == 0 and rows > 64 else rows))
    if rows % 352 == 0:
        br = 352
    c1 = 1.0 - B1 ** STEP
    c2 = 1.0 - B2 ** STEP

    def body(w_ref, g_ref, m_ref, v_ref, d_ref, nm_ref, nv_ref):
        gv = g_ref[...]
        nm = B1 * m_ref[...] + (1.0 - B1) * gv
        nv = B2 * v_ref[...] + (1.0 - B2) * (gv * gv)
        d_ref[...] = -LR * ((nm / c1) / (jnp.sqrt(nv / c2) + AEPS) + WD * w_ref[...])
        nm_ref[...] = nm
        nv_ref[...] = nv

    blk = pl.BlockSpec((br, cols), lambda i: (i, 0))
    outs = _call(body, name=name, out_shape=(_sds((rows, cols)),) * 3, grid=(rows // br,),
                 in_specs=[blk] * 4, out_specs=(blk,) * 3, sem=("parallel",))(w2, g2, m2, v2)
    return tuple(t.reshape(shape) for t in outs)


def _rope_tables(N, L):
    t = jnp.arange(N)
    pos = jnp.stack([(t // GRID_W).astype(F32), (t % GRID_W).astype(F32)], axis=1)
    inv = ROPE_THETA ** (-jnp.arange(0, HD // 2, 2, dtype=F32) / (HD // 2))
    ang = pos[:, :, None] * inv[None, None, :]
    cos = jnp.broadcast_to(jnp.cos(ang)[:, :, None, :], (N, 2, 2, HD // 4)).reshape(N, HD)
    sin = jnp.broadcast_to(jnp.sin(ang)[:, :, None, :], (N, 2, 2, HD // 4))
    sin = (sin * jnp.array([-1.0, 1.0], F32)[None, None, :, None]).reshape(N, HD)
    cos = jnp.concatenate([jnp.ones((L, HD), F32), cos], axis=0)
    sin = jnp.concatenate([jnp.zeros((L, HD), F32), sin], axis=0)
    return cos, sin


def _pad_lanes(v, off=0):
    return jnp.zeros((1, HD), F32).at[0, off:off + v.shape[0]].set(v)


def _local_step(x, ctx, tgt, mod_lat, mod_ctx, wts, small):
    N, L = x.shape[0], ctx.shape[0]
    T = N + L
    bounds = ((0, L), (L, T))
    w_in, w_pa, w_pd, w_out, w_up, w_down = wts
    qw, kw, gw = small["q_norm_w"], small["k_norm_w"], small["gdn_norm_w"]
    conv_w, ffn_w, ffn_b, fnw = small["conv_qkv_w"], small["ffn_conv_w"], small["ffn_conv_b"], small["final_norm_w"]
    alog = _pad_lanes(small["a_log"].reshape(-1), 2 * GH)
    dtb = _pad_lanes(small["dt_bias"].reshape(-1), 2 * GH)
    cos, sin = _rope_tables(N, L)
    bt = 256 if T % 768 else 768
    bnl = 256 if N % 1024 else 1024

    hc = _normmod_fwd(ctx, mod_ctx, 0, 1, name="normmod_ctx")
    hx = _normmod_fwd(x, mod_lat, 0, 1, name="normmod_x")
    h1 = jnp.concatenate([hc, hx], axis=0)
    proj = _mm(h1, w_in, name="mm_in", M=T, N=C_END, K=D, bm=bt, bn=1024)
    aq, ak, av = _aprep_fwd(proj, cos, sin, qw, kw)
    attn = _attn_fwd(aq, ak, av, L)
    gq = _gprep_fwd(proj, conv_w, 0, bounds)
    gk = _gprep_fwd(proj, conv_w, 1, bounds)
    gv = _gprep_fwd(proj, conv_w, 2, bounds)
    bl = _bl_fwd(proj, alog, dtb)
    intra = _intra_fwd(gq, gk, gv, bl)
    o, states = _scan_fwd(*intra, L)
    gdn = _gout_fwd(o, proj, gw, L)
    pa = _mm(attn, w_pa, name="mm_pa", M=N, N=D, K=D, bm=bnl)
    pd = _mm(gdn, w_pd, name="mm_pd", M=N, N=D, K=D, bm=bnl)
    y = _merge_fwd(pa, pd, proj, L)
    m = _mm(y, w_out, name="mm_out", M=N, N=D, K=D, bm=bnl)
    x1 = _resid_fwd(x, m, mod_lat, 2, name="resid1")
    h2 = _normmod_fwd(x1, mod_lat, 3, 4, name="normmod_x1")
    up = _mm(h2, w_up, name="mm_up", M=N, N=2 * DFF, K=D, bm=bnl, bn=2 * DFF // 4)
    a = _ffn_fwd(up, ffn_w, ffn_b)
    dn = _mm(a, w_down, name="mm_down", M=N, N=D, K=DFF, bm=bnl)
    loss, dx2, ddn, dg2, dfnw = _head(x1, dn, mod_lat, fnw, tgt)

    da = _mm(ddn, w_down, name="mm_down_dx", M=N, N=DFF, K=D, tb=True, bm=bnl, bn=DFF // 2)
    g_down = _mm(a, ddn, name="mm_down_dw", M=DFF, N=D, K=N, ta=True, bm=DFF // 2)
    dup, d_ffn_w, d_ffn_b = _ffn_bwd(up, ffn_w, ffn_b, da)
    dh2 = _mm(dup, w_up, name="mm_up_dx", M=N, N=D, K=2 * DFF, tb=True, bm=bnl, bk=2 * DFF // 4)
    g_up = _mm(h2, dup, name="mm_up_dw", M=D, N=2 * DFF, K=N, ta=True, bn=2 * DFF // 4)
    dx1, dsh2, dsc2 = _normmod_bwd(x1, mod_lat, 3, 4, dh2, 0, dx2, name="normmod_x1_bwd")
    dm, dg1 = _resid_bwd(dx1, m, mod_lat, 2, name="resid1_bwd")
    dy = _mm(dm, w_out, name="mm_out_dx", M=N, N=D, K=D, tb=True, bm=bnl)
    g_out = _mm(y, dm, name="mm_out_dw", M=D, N=D, K=N, ta=True)
    dpa, dpd, dgate = _merge_bwd(pa, pd, proj, dy, L)
    dattn = _mm(dpa, w_pa, name="mm_pa_dx", M=N, N=D, K=D, tb=True, bm=bnl)
    g_pa = _mm(attn, dpa, name="mm_pa_dw", M=D, N=D, K=N, ta=True)
    dgdn = _mm(dpd, w_pd, name="mm_pd_dx", M=N, N=D, K=D, tb=True, bm=bnl)
    g_pd = _mm(gdn, dpd, name="mm_pd_dw", M=D, N=D, K=N, ta=True)
    do, dz, dgw = _gout_bwd(o, proj, gw, dgdn, L)
    cts = _scan_bwd(*intra, states, do, L)
    dgq, dgk, dgv, dbl = _intra_bwd(gq, gk, gv, bl, cts)
    dxq, dwq = _gprep_bwd(proj, conv_w, 0, bounds, dgq)
    dxk, dwk = _gprep_bwd(proj, conv_w, 1, bounds, dgk)
    dxv, dwv = _gprep_bwd(proj, conv_w, 2, bounds, dgv)
    dxbl, dalog, ddtb = _bl_bwd(proj, alog, dtb, dbl)
    daq_h, dak_h, dav_h = _attn_bwd(aq, ak, av, dattn, L)
    daq, dkv, dqw, dkw = _aprep_bwd(proj, cos, sin, qw, kw, daq_h, dak_h, dav_h, L)
    dproj = jnp.concatenate([dkv, dxq, dxk, dxv, dxbl, jnp.zeros((T, C_AQ - C_BL - HD), BF16), daq, dz, dgate], axis=1)
    dh1 = _mm(dproj, w_in, name="mm_in_dx", M=T, N=D, K=C_END, tb=True, bm=bt, bk=1024)
    g_in = _mm(h1, dproj, name="mm_in_dw", M=D, N=C_END, K=T, ta=True, bn=1024)
    grad_x, dsh1, dsc1 = _normmod_bwd(x, mod_lat, 0, 1, dh1, L, dx1, name="normmod_x_bwd")
    _, dcsh1, dcsc1 = _normmod_bwd(ctx, mod_ctx, 0, 1, dh1, 0, None, name="normmod_ctx_bwd")

    z1 = jnp.zeros((1, D), F32)
    dmod_lat = jnp.concatenate([dsh1, dsc1, dg1, dsh2, dsc2, dg2], axis=0)
    dmod_ctx = jnp.concatenate([dcsh1, dcsc1, z1, z1, z1, z1], axis=0)
    gsmall = {
        "q_norm_w": dqw, "k_norm_w": dkw, "gdn_norm_w": dgw,
        "conv_qkv_w": jnp.concatenate([dwq, dwk, dwv], axis=1),
        "a_log": dalog[0, 2 * GH:4 * GH], "dt_bias": ddtb[0, 2 * GH:4 * GH],
        "ffn_conv_w": d_ffn_w, "ffn_conv_b": d_ffn_b, "final_norm_w": dfnw,
    }
    return loss[0, 0], grad_x, (g_in, g_pa, g_pd, g_out, g_up, g_down), dmod_lat, dmod_ctx, gsmall


HBM = pl.BlockSpec(memory_space=pltpu.HBM)


def _position():
    x, y, c = lax.axis_index("x"), lax.axis_index("y"), lax.axis_index("c")
    return x, y, c, 4 * x + 2 * y + c


def _peer(x, y, c, k):
    px = 1 - x if k & 4 else x
    py = 1 - y if k & 2 else y
    pc = 1 - c if k & 1 else c
    return (px, py, pc), 4 * px + 2 * py + pc


def _exchange(arrs, *, name, scatter):
    n = len(arrs)

    def body(*refs):
        ins, outs = refs[:n], refs[n:2 * n]
        send, recv, loc = refs[2 * n:]
        x, y, c, me = _position()
        local = []
        for a in range(n):
            src = ins[a].at[me] if scatter else ins[a]
            cp = pltpu.make_async_copy(src, outs[a].at[me], loc.at[a])
            cp.start()
            local.append(cp)
        remote = []
        for k in range(1, NDEV):
            peer, pid = _peer(x, y, c, k)
            for a in range(n):
                src = ins[a].at[pid] if scatter else ins[a]
                cp = pltpu.make_async_remote_copy(src_ref=src, dst_ref=outs[a].at[me], send_sem=send.at[a, k - 1],
                                                  recv_sem=recv.at[a, k - 1], device_id=peer, device_id_type=MESH)
                cp.start()
                remote.append(cp)
        for cp in remote:
            cp.wait()
        for cp in local:
            cp.wait()

    out_shape = tuple(_sds(a.shape if scatter else (NDEV,) + a.shape, a.dtype) for a in arrs)
    outs = pl.pallas_call(
        body, name=name, out_shape=out_shape, in_specs=[HBM] * n, out_specs=(HBM,) * n,
        scratch_shapes=[pltpu.SemaphoreType.DMA((n, NDEV - 1)), pltpu.SemaphoreType.DMA((n, NDEV - 1)),
                        pltpu.SemaphoreType.DMA((n,))],
        compiler_params=pltpu.CompilerParams(has_side_effects=True))(*arrs)
    return list(outs)


def _cast_bf16(w, *, name):
    rows, cols = w.shape
    br = 128 if rows % 128 == 0 else rows

    def body(w_ref, o_ref):
        o_ref[...] = w_ref[...].astype(BF16)

    blk = pl.BlockSpec((br, cols), lambda i: (i, 0))
    return _call(body, name=name, out_shape=_sds((rows, cols), BF16), grid=(rows // br,), in_specs=[blk],
                 out_specs=blk, sem=("parallel",))(w)


def _sum_slots(a, *, name):
    _, R, C = a.shape

    def body(a_ref, o_ref):
        s = a_ref[0]
        for d in range(1, NDEV):
            s = s + a_ref[d]
        o_ref[...] = s

    return _call(body, name=name, out_shape=_sds((R, C)))(a)


MODROWS = 16


def _mod_fwd(c9, w, b):
    cols = w.shape[1]

    def body(c_ref, w_ref, b_ref, o_ref):
        o_ref[...] = _nn(_silu(c_ref[...]), w_ref[...]) + b_ref[...]

    return _call(body, name="mod_fwd", out_shape=_sds((MODROWS, cols)))(c9, w, b)


def _mod_bwd(c9, dmy, dall, w):
    cols = w.shape[1]

    def body(c_ref, dmy_ref, dall_ref, w_ref, gw_ref, gb_ref, cp_ref):
        sc = _silu(c_ref[...])
        rows = lax.broadcasted_iota(jnp.int32, (MODROWS, 1), 0)
        d = dmy_ref[...]
        d_ctx = jnp.where(rows == NDEV, d, 0.0)
        sc_ctx = jnp.where(rows == NDEV, sc, 0.0)
        outer = lax.dot_general(sc_ctx, d_ctx, (((0,), (0,)), ((), ())), precision=HI, preferred_element_type=F32)
        gw_ref[...] = _tn(jnp.where(rows < NDEV, sc, 0.0), jnp.where(rows < NDEV, d, 0.0)) + outer
        gb_ref[...] = jnp.sum(dall_ref[...], axis=0, keepdims=True)
        cp_ref[...] = jnp.sum(_nt(d_ctx, w_ref[...]), axis=0, keepdims=True)

    return _call(body, name="mod_bwd", out_shape=(_sds((D, cols)), _sds((1, 6 * D)), _sds((1, D))),
                 vmem=VMEM_BIG)(c9, dmy, dall, w)


def _cctx_finish(parts, c_ctx):
    def body(p_ref, c_ref, o_ref):
        s = p_ref[0]
        for d in range(1, NDEV):
            s = s + p_ref[d]
        _, vjp = jax.vjp(_silu, c_ref[...])
        o_ref[...] = vjp(s)[0]

    return _call(body, name="cctx_finish", out_shape=_sds((1, D)))(parts, c_ctx)


def _adamw_recv(w, recv, m, v, *, name):
    rows, cols = w.shape
    br = 128 if rows % 128 == 0 else rows
    c1 = 1.0 - B1 ** STEP
    c2 = 1.0 - B2 ** STEP

    def body(w_ref, r_ref, m_ref, v_ref, g_ref, d_ref, nm_ref, nv_ref):
        gv = r_ref[0].astype(F32)
        for d in range(1, NDEV):
            gv = gv + r_ref[d].astype(F32)
        nm = B1 * m_ref[...] + (1.0 - B1) * gv
        nv = B2 * v_ref[...] + (1.0 - B2) * (gv * gv)
        g_ref[...] = gv
        d_ref[...] = -LR * ((nm / c1) / (jnp.sqrt(nv / c2) + AEPS) + WD * w_ref[...])
        nm_ref[...] = nm
        nv_ref[...] = nv

    blk = pl.BlockSpec((br, cols), lambda i: (i, 0))
    return _call(body, name=name, out_shape=(_sds((rows, cols)),) * 4, grid=(rows // br,),
                 in_specs=[blk, pl.BlockSpec((NDEV, br, cols), lambda i: (0, i, 0)), blk, blk], out_specs=(blk,) * 4,
                 sem=("parallel",))(w, recv, m, v)


P_LAT, P_CTX, P_FNW, P_FFNB, P_CONV, P_FFNW, P_MISC, P_ROWS = 0, 8, 16, 24, 32, 48, 72, 80


def _rows_of(v, nrows):
    flat = v.reshape(-1)
    return jnp.pad(flat, (0, nrows * D - flat.shape[0])).reshape(nrows, D)


def _by_columns(g):
    n, r, c = g.shape
    return jnp.transpose(g, (1, 0, 2)).reshape(r, n * c)


def _to_columns(a):
    r, nc = a.shape
    return jnp.transpose(a.reshape(r, NDEV, nc // NDEV), (1, 0, 2))


def kernel(x, c, ctx, c_ctx, w_mod, b_mod, w_in, q_norm_w, k_norm_w, conv_qkv_w, a_log, dt_bias, gdn_norm_w, w_pa, w_pd, w_out, w_up, ffn_conv_w, ffn_conv_b, w_down, final_norm_w, loss_target, m_c_ctx, m_w_mod, m_b_mod, m_w_in, m_q_norm_w, m_k_norm_w, m_conv_qkv_w, m_a_log, m_dt_bias, m_gdn_norm_w, m_w_pa, m_w_pd, m_w_out, m_w_up, m_ffn_conv_w, m_ffn_conv_b, m_w_down, m_final_norm_w, v_c_ctx, v_w_mod, v_b_mod, v_w_in, v_q_norm_w, v_k_norm_w, v_conv_qkv_w, v_a_log, v_dt_bias, v_gdn_norm_w, v_w_pa, v_w_pd, v_w_out, v_w_up, v_ffn_conv_w, v_ffn_conv_b, v_w_down, v_final_norm_w):
    _, _, _, me = _position()
    mcols = w_mod.shape[2]

    big = {"w_in": w_in[0], "w_pa": w_pa[0], "w_pd": w_pd[0], "w_out": w_out[0], "w_up": w_up[0], "w_down": w_down[0]}
    names = list(big)
    gathered = _exchange([_cast_bf16(big[n], name="cast_" + n) for n in names], name="gather_weights", scatter=False)
    gw = dict(zip(names, gathered))
    c_all, conv_g, ffnw_g = _exchange([c, conv_qkv_w[0], ffn_conv_w[0]], name="gather_small", scatter=False)
    w_in_full = _by_columns(gw["w_in"])
    w_in_pad = jnp.concatenate([w_in_full[:, :W_AQ], jnp.zeros((D, C_AQ - W_AQ), BF16), w_in_full[:, W_AQ:]], axis=1)
    wts = (w_in_pad, gw["w_pa"].reshape(D, D), gw["w_pd"].reshape(D, D), gw["w_out"].reshape(D, D),
           _by_columns(gw["w_up"]), gw["w_down"].reshape(DFF, D))

    c9 = jnp.concatenate([c_all.reshape(NDEV, D), jnp.pad(c_ctx[None], ((0, MODROWS - NDEV - 1), (0, 0)))], axis=0)
    b_loc = lax.dynamic_slice(b_mod, (0, me * mcols), (1, mcols))
    mod_all, = _exchange([_mod_fwd(c9, w_mod[0], b_loc)], name="gather_mod", scatter=False)
    mod_lat = lax.dynamic_index_in_dim(mod_all, me, axis=1, keepdims=False).reshape(6, D)
    mod_ctx = mod_all[:, NDEV, :].reshape(6, D)

    small = {"q_norm_w": q_norm_w, "k_norm_w": k_norm_w, "gdn_norm_w": gdn_norm_w, "a_log": a_log, "dt_bias": dt_bias,
             "conv_qkv_w": _by_columns(conv_g), "ffn_conv_w": _by_columns(ffnw_g), "ffn_conv_b": ffn_conv_b,
             "final_norm_w": final_norm_w[None]}
    loss_me, grad_x, gfull, dmod_lat, dmod_ctx, gs = _local_step(x[0], ctx[0], loss_target[0], mod_lat, mod_ctx, wts, small)

    g_in, g_pa, g_pd, g_out, g_up, g_down = gfull
    g_in = jnp.concatenate([g_in[:, :W_AQ], g_in[:, C_AQ:]], axis=1)
    parts = [_to_columns(g_in), g_pa.reshape(NDEV, D // NDEV, D), g_pd.reshape(NDEV, D // NDEV, D),
             g_out.reshape(NDEV, D // NDEV, D), _to_columns(g_up), g_down.reshape(NDEV, DFF // NDEV, D)]
    recv = _exchange([p.astype(BF16) for p in parts], name="scatter_grads", scatter=True)
    moments = {"w_in": (m_w_in, v_w_in), "w_pa": (m_w_pa, v_w_pa), "w_pd": (m_w_pd, v_w_pd),
               "w_out": (m_w_out, v_w_out), "w_up": (m_w_up, v_w_up), "w_down": (m_w_down, v_w_down)}
    res = {}
    for n, r in zip(names, recv):
        outs = _adamw_recv(big[n], r, moments[n][0][0], moments[n][1][0], name="adamw_" + n)
        res[n] = tuple(t[None] for t in outs)

    misc = jnp.concatenate([gs["q_norm_w"][0], gs["k_norm_w"][0], gs["gdn_norm_w"][0], gs["a_log"], gs["dt_bias"],
                            loss_me[None]])
    pack = jnp.concatenate([_rows_of(dmod_lat, P_CTX - P_LAT), _rows_of(dmod_ctx, P_FNW - P_CTX),
                            _rows_of(gs["final_norm_w"], P_FFNB - P_FNW), _rows_of(gs["ffn_conv_b"], P_CONV - P_FFNB),
                            _rows_of(gs["conv_qkv_w"], P_FFNW - P_CONV), _rows_of(gs["ffn_conv_w"], P_MISC - P_FFNW),
                            _rows_of(misc, P_ROWS - P_MISC)], axis=0)
    pack_all, = _exchange([pack], name="gather_pack", scatter=False)
    tot = _sum_slots(pack_all, name="sum_pack")
    dall = jnp.concatenate([pack_all[:, P_LAT:P_LAT + 6, :].reshape(NDEV, 6 * D),
                            jnp.pad(tot[P_CTX:P_CTX + 6].reshape(1, 6 * D), ((0, MODROWS - NDEV - 1), (0, 0)))], axis=0)
    dmy = lax.dynamic_slice(dall, (0, me * mcols), (MODROWS, mcols))
    g_w_mod, g_b_mod, cpart = _mod_bwd(c9, dmy, dall, w_mod[0])
    cparts, = _exchange([cpart], name="gather_cctx", scatter=False)
    g_c_ctx = _cctx_finish(cparts, c_ctx[None])[0]

    nconv, nffn = 3 * GH * HD, 2 * DFF
    conv_tot = tot[P_CONV:P_FFNW].reshape(-1)[:3 * nconv].reshape(3, nconv)
    ffnw_tot = tot[P_FFNW:P_MISC].reshape(-1)[:3 * nffn].reshape(3, nffn)
    mrow = tot[P_MISC]
    grads = {
        "c_ctx": g_c_ctx, "w_mod": g_w_mod[None], "b_mod": g_b_mod,
        "q_norm_w": mrow[None, 0:HD], "k_norm_w": mrow[None, HD:2 * HD], "gdn_norm_w": mrow[None, 2 * HD:3 * HD],
        "conv_qkv_w": lax.dynamic_slice(conv_tot, (0, me * (nconv // NDEV)), (3, nconv // NDEV))[None],
        "a_log": mrow[3 * HD:3 * HD + 2 * GH].reshape(1, 2, GH),
        "dt_bias": mrow[3 * HD + 2 * GH:3 * HD + 4 * GH].reshape(1, 2, GH),
        "ffn_conv_w": lax.dynamic_slice(ffnw_tot, (0, me * (nffn // NDEV)), (3, nffn // NDEV))[None],
        "ffn_conv_b": tot[P_FFNB:P_CONV].reshape(-1)[:nffn][None],
        "final_norm_w": tot[P_FNW],
    }
    loss = mrow[3 * HD + 4 * GH]
    given = {"c_ctx": (c_ctx, m_c_ctx, v_c_ctx), "w_mod": (w_mod, m_w_mod, v_w_mod), "b_mod": (b_mod, m_b_mod, v_b_mod),
             "q_norm_w": (q_norm_w, m_q_norm_w, v_q_norm_w), "k_norm_w": (k_norm_w, m_k_norm_w, v_k_norm_w),
             "conv_qkv_w": (conv_qkv_w, m_conv_qkv_w, v_conv_qkv_w), "a_log": (a_log, m_a_log, v_a_log),
             "dt_bias": (dt_bias, m_dt_bias, v_dt_bias), "gdn_norm_w": (gdn_norm_w, m_gdn_norm_w, v_gdn_norm_w),
             "ffn_conv_w": (ffn_conv_w, m_ffn_conv_w, v_ffn_conv_w), "ffn_conv_b": (ffn_conv_b, m_ffn_conv_b, v_ffn_conv_b),
             "final_norm_w": (final_norm_w, m_final_norm_w, v_final_norm_w)}
    for n, (w, m, v) in given.items():
        res[n] = (grads[n],) + _adamw(w, grads[n], m, v, name="adamw_" + n)

    order = ["c_ctx", "w_mod", "b_mod", "w_in", "q_norm_w", "k_norm_w", "conv_qkv_w", "a_log", "dt_bias", "gdn_norm_w",
             "w_pa", "w_pd", "w_out", "w_up", "ffn_conv_w", "ffn_conv_b", "w_down", "final_norm_w"]
    return (loss, grad_x[None], *[res[n][0] for n in order], *[res[n][1] for n in order],
            *[res[n][2] for n in order], *[res[n][3] for n in order])
```

```python
import functools
import math

import jax
import jax.numpy as jnp
from jax import lax
from jax.experimental import pallas as pl
from jax.experimental.pallas import tpu as pltpu

F32 = jnp.float32
BF16 = jnp.bfloat16
HI = lax.Precision.HIGHEST
MESH = pl.DeviceIdType.MESH

NDEV = 8
D = 1024
HD = 128
AH, AKV, GRP = 8, 2, 4
GH = 8
CH = 64
DFF = 2816
GRID_W = 64
EPS = 1e-6
ROPE_THETA = 10000.0
C_KV, C_QKV, C_BL, C_AQ, C_Z, C_GATE, C_END = 0, 512, 3584, 4096, 5120, 6144, 8192
W_BL, W_AQ, W_END = 3584, 3616, 7712
LR, B1, B2, AEPS, WD, STEP = 0.001, 0.9, 0.999, 1e-08, 0.01, 10
VMEM_BIG = 56 * 1024 * 1024
INTRA_FWD_CHUNKS = 18
INTRA_BWD_CHUNKS = 12


def _call(body, *, name, out_shape, grid=None, in_specs=None, out_specs=None, scratch=(), sem=None,
          vmem=None, aliases=None):
    params = {}
    if sem is not None:
        params["dimension_semantics"] = sem
    if vmem is not None:
        params["vmem_limit_bytes"] = vmem
    kw = {}
    if grid is not None:
        kw["grid"] = grid
    if in_specs is not None:
        kw["in_specs"] = in_specs
    if out_specs is not None:
        kw["out_specs"] = out_specs
    if aliases:
        kw["input_output_aliases"] = aliases
    return pl.pallas_call(body, name=name, out_shape=out_shape, scratch_shapes=list(scratch),
                          compiler_params=pltpu.CompilerParams(**params), **kw)


def _call_carrying(body, exch, *, name, out_shape, grid, in_specs, out_specs, scratch=(), vmem=None):
    n, nin, nout, nscr = exch.n, len(in_specs), len(out_shape), len(scratch)

    def wrapped(*refs):
        ins, cins = refs[:nin], refs[nin:nin + n]
        outs, couts = refs[nin + n:nin + n + nout], refs[nin + n + nout:nin + 2 * n + nout]
        scr, sems = refs[nin + 2 * n + nout:nin + 2 * n + nout + nscr], refs[nin + 2 * n + nout + nscr:]
        ids = [pl.program_id(i) for i in range(len(grid))]
        first = functools.reduce(jnp.logical_and, [i == 0 for i in ids])
        last = functools.reduce(jnp.logical_and, [i == g - 1 for i, g in zip(ids, grid)])

        @pl.when(first)
        def _():
            exch.start(cins, couts, sems)

        body(*ins, *outs, *scr)

        @pl.when(last)
        def _():
            exch.finish(cins, couts, sems)

    params = {"dimension_semantics": ("arbitrary",) * len(grid)}
    if vmem is not None:
        params["vmem_limit_bytes"] = vmem
    fn = pl.pallas_call(wrapped, name=name, out_shape=tuple(out_shape) + exch.out_shape, grid=grid,
                        in_specs=list(in_specs) + [HBM] * n, out_specs=tuple(out_specs) + (HBM,) * n,
                        scratch_shapes=list(scratch) + exch.scratch, compiler_params=pltpu.CompilerParams(**params))

    def run(*args):
        res = fn(*args, *exch.arrs)
        return res[:nout], list(res[nout:])

    return run


def _sds(shape, dtype=F32):
    return jax.ShapeDtypeStruct(tuple(shape), dtype)


def _dot(a, b, ca, cb):
    return lax.dot_general(a.astype(BF16), b.astype(BF16), (((ca,), (cb,)), ((), ())),
                           preferred_element_type=F32)


@jax.custom_vjp
def _nn(a, b):
    return _dot(a, b, 1, 0)


@jax.custom_vjp
def _nt(a, b):
    return _dot(a, b, 1, 1)


@jax.custom_vjp
def _tn(a, b):
    return _dot(a, b, 0, 0)


_nn.defvjp(lambda a, b: (_nn(a, b), (a, b)), lambda r, g: (_nt(g, r[1]), _tn(r[0], g)))
_nt.defvjp(lambda a, b: (_nt(a, b), (a, b)), lambda r, g: (_nn(g, r[1]), _tn(g, r[0])))
_tn.defvjp(lambda a, b: (_tn(a, b), (a, b)), lambda r, g: (_nt(r[1], g), _nn(r[0], g)))


def _hdot(a, b):
    return jnp.dot(a, b, precision=HI, preferred_element_type=F32)


def _mdot(a, b):
    return jnp.dot(a, b, precision=lax.Precision.HIGH, preferred_element_type=F32)


def _row_ids(shape):
    return lax.broadcasted_iota(jnp.int32, shape, 0)


def _shift_rows(x, down, bounds):
    n = x.shape[0]
    rows = _row_ids(x.shape)
    y = pltpu.roll(x, 1 if down else n - 1, 0)
    edge = functools.reduce(jnp.logical_or, [rows == (s if down else e - 1) for s, e in bounds])
    return jnp.where(edge, 0.0, y)


def _make_shift(bounds):
    @jax.custom_vjp
    def down(x):
        return _shift_rows(x, True, bounds)

    @jax.custom_vjp
    def up(x):
        return _shift_rows(x, False, bounds)

    down.defvjp(lambda x: (down(x), None), lambda _, g: (up(g),))
    up.defvjp(lambda x: (up(x), None), lambda _, g: (down(g),))
    return down, up


@jax.custom_vjp
def _swap32(x):
    lane = lax.broadcasted_iota(jnp.int32, x.shape, x.ndim - 1)
    return jnp.where((lane % 64) < 32, pltpu.roll(x, HD - 32, x.ndim - 1), pltpu.roll(x, 32, x.ndim - 1))


_swap32.defvjp(lambda x: (_swap32(x), None), lambda _, g: (_swap32(g),))


def _rms(x):
    return x * lax.rsqrt(jnp.mean(x * x, axis=-1, keepdims=True) + EPS)


def _silu(x):
    return x * jax.nn.sigmoid(x)


def _mm(a, b, *, name, M, N, K, ta=False, tb=False, out_dtype=F32, bm=None, bn=None, bk=None,
        a_off=(0, 0), b_off=(0, 0)):
    bm, bn, bk = bm or M, bn or N, bk or K
    assert M % bm == 0 and N % bn == 0 and K % bk == 0, (name, M, N, K, bm, bn, bk)
    nk = K // bk
    ca, cb = (0 if ta else 1), (1 if tb else 0)

    def body(a_ref, b_ref, o_ref, *acc):
        r = _dot(a_ref[...], b_ref[...], ca, cb)
        if nk == 1:
            o_ref[...] = r.astype(out_dtype)
        else:
            acc_ref, = acc
            k = pl.program_id(2)

            @pl.when(k == 0)
            def _():
                acc_ref[...] = r

            @pl.when(k > 0)
            def _():
                acc_ref[...] += r

            @pl.when(k == nk - 1)
            def _():
                o_ref[...] = acc_ref[...].astype(out_dtype)

    def blk(off, bshape):
        assert off[0] % bshape[0] == 0 and off[1] % bshape[1] == 0, (name, off, bshape)
        return off[0] // bshape[0], off[1] // bshape[1]

    if ta:
        ao = blk(a_off, (bk, bm))
        a_spec = pl.BlockSpec((bk, bm), lambda i, j, k: (k + ao[0], i + ao[1]))
    else:
        ao = blk(a_off, (bm, bk))
        a_spec = pl.BlockSpec((bm, bk), lambda i, j, k: (i + ao[0], k + ao[1]))
    if tb:
        bo = blk(b_off, (bn, bk))
        b_spec = pl.BlockSpec((bn, bk), lambda i, j, k: (j + bo[0], k + bo[1]))
    else:
        bo = blk(b_off, (bk, bn))
        b_spec = pl.BlockSpec((bk, bn), lambda i, j, k: (k + bo[0], j + bo[1]))
    return _call(body, name=name, out_shape=_sds((M, N), out_dtype), grid=(M // bm, N // bn, nk),
                 in_specs=[a_spec, b_spec], out_specs=pl.BlockSpec((bm, bn), lambda i, j, k: (i, j)),
                 scratch=[pltpu.VMEM((bm, bn), F32)] if nk > 1 else [],
                 sem=("parallel", "parallel", "arbitrary"), vmem=VMEM_BIG)(a, b)


def _normmod_fn(x, sh, sc):
    return _rms(x) * (1.0 + sc) + sh


def _normmod_fwd(x, mod, i_sh, i_sc, *, name, br=256):
    R = x.shape[0]

    def body(x_ref, mod_ref, o_ref):
        o_ref[...] = _normmod_fn(x_ref[...], mod_ref[i_sh:i_sh + 1, :], mod_ref[i_sc:i_sc + 1, :]).astype(BF16)

    return _call(body, name=name, out_shape=_sds((R, D), BF16), grid=(R // br,),
                 in_specs=[pl.BlockSpec((br, D), lambda i: (i, 0)), pl.BlockSpec((6, D), lambda i: (0, 0))],
                 out_specs=pl.BlockSpec((br, D), lambda i: (i, 0)), sem=("parallel",))(x, mod)


def _normmod_bwd(x, mod, i_sh, i_sc, dh, dh_off, res, *, name, br=256):
    R = x.shape[0]
    ob = dh_off // br
    has_res = res is not None

    def body(x_ref, mod_ref, dh_ref, *rest):
        if has_res:
            res_ref, dx_ref, dsh_ref, dsc_ref = rest
        else:
            dx_ref, dsh_ref, dsc_ref = rest
        sh, sc = mod_ref[i_sh:i_sh + 1, :], mod_ref[i_sc:i_sc + 1, :]
        _, vjp = jax.vjp(_normmod_fn, x_ref[...], sh, sc)
        dx, dsh, dsc = vjp(dh_ref[...])
        dx_ref[...] = dx + res_ref[...] if has_res else dx

        @pl.when(pl.program_id(0) == 0)
        def _():
            dsh_ref[...] = jnp.zeros_like(dsh_ref)
            dsc_ref[...] = jnp.zeros_like(dsc_ref)

        dsh_ref[...] += dsh
        dsc_ref[...] += dsc

    row = pl.BlockSpec((br, D), lambda i: (i, 0))
    vec = pl.BlockSpec((1, D), lambda i: (0, 0))
    ins = [row, pl.BlockSpec((6, D), lambda i: (0, 0)), pl.BlockSpec((br, D), lambda i: (i + ob, 0))]
    args = [x, mod, dh]
    if has_res:
        ins.append(row)
        args.append(res)
    return _call(body, name=name, out_shape=(_sds((R, D)), _sds((1, D)), _sds((1, D))), grid=(R // br,),
                 in_specs=ins, out_specs=(row, vec, vec), sem=("arbitrary",))(*args)


def _rope(x, cos, sin):
    return x * cos + _swap32(x) * sin


def _aprep_fn(qs, ks, cos, sin, qw, kw):
    return ([_rope(_rms(q) * qw, cos, sin) for q in qs], [_rope(_rms(k) * kw, cos, sin) for k in ks])


def _aprep_fwd(proj, cos, sin, qw, kw, *, br=256):
    T = proj.shape[0]

    def body(aq_ref, kv_ref, cos_ref, sin_ref, qw_ref, kw_ref, q_ref, k_ref, v_ref):
        qs = [aq_ref[:, h * HD:(h + 1) * HD] for h in range(AH)]
        ks = [kv_ref[:, h * HD:(h + 1) * HD] for h in range(AKV)]
        qo, ko = _aprep_fn(qs, ks, cos_ref[...], sin_ref[...], qw_ref[...], kw_ref[...])
        for h in range(AH):
            q_ref[h] = qo[h].astype(BF16)
        for h in range(AKV):
            k_ref[h] = ko[h].astype(BF16)
            v_ref[h] = kv_ref[:, (AKV + h) * HD:(AKV + h + 1) * HD].astype(BF16)

    tab = pl.BlockSpec((br, HD), lambda i: (i, 0))
    vec = pl.BlockSpec((1, HD), lambda i: (0, 0))
    return _call(body, name="aprep_fwd",
                 out_shape=(_sds((AH, T, HD), BF16), _sds((AKV, T, HD), BF16), _sds((AKV, T, HD), BF16)),
                 grid=(T // br,),
                 in_specs=[pl.BlockSpec((br, AH * HD), lambda i: (i, C_AQ // (AH * HD))),
                           pl.BlockSpec((br, 2 * AKV * HD), lambda i: (i, 0)), tab, tab, vec, vec],
                 out_specs=(pl.BlockSpec((AH, br, HD), lambda i: (0, i, 0)),
                            pl.BlockSpec((AKV, br, HD), lambda i: (0, i, 0)),
                            pl.BlockSpec((AKV, br, HD), lambda i: (0, i, 0))),
                 sem=("parallel",))(proj, proj, cos, sin, qw, kw)


def _aprep_bwd(proj, cos, sin, qw, kw, dq, dk, dv, L, *, br=256):
    T = proj.shape[0]
    lb = L // br

    def body(aq_ref, kv_ref, cos_ref, sin_ref, qw_ref, kw_ref, dq_ref, dk_ref, dv_ref,
             daq_ref, dkv_ref, dqw_ref, dkw_ref):
        i = pl.program_id(0)
        qs = [aq_ref[:, h * HD:(h + 1) * HD] for h in range(AH)]
        ks = [kv_ref[:, h * HD:(h + 1) * HD] for h in range(AKV)]
        _, vjp = jax.vjp(_aprep_fn, qs, ks, cos_ref[...], sin_ref[...], qw_ref[...], kw_ref[...])
        is_lat = i >= lb
        dqs = [jnp.where(is_lat, dq_ref[h], 0.0) for h in range(AH)]
        dks = [dk_ref[h] for h in range(AKV)]
        gq, gk, _, _, gqw, gkw = vjp((dqs, dks))
        for h in range(AH):
            daq_ref[:, h * HD:(h + 1) * HD] = gq[h].astype(BF16)
        for h in range(AKV):
            dkv_ref[:, h * HD:(h + 1) * HD] = gk[h].astype(BF16)
            dkv_ref[:, (AKV + h) * HD:(AKV + h + 1) * HD] = dv_ref[h].astype(BF16)

        @pl.when(i == 0)
        def _():
            dqw_ref[...] = jnp.zeros_like(dqw_ref)
            dkw_ref[...] = jnp.zeros_like(dkw_ref)

        dqw_ref[...] += gqw
        dkw_ref[...] += gkw

    tab = pl.BlockSpec((br, HD), lambda i: (i, 0))
    vec = pl.BlockSpec((1, HD), lambda i: (0, 0))
    kvb = pl.BlockSpec((AKV, br, HD), lambda i: (0, i, 0))
    return _call(body, name="aprep_bwd",
                 out_shape=(_sds((T, AH * HD), BF16), _sds((T, 2 * AKV * HD), BF16), _sds((1, HD)), _sds((1, HD))),
                 grid=(T // br,),
                 in_specs=[pl.BlockSpec((br, AH * HD), lambda i: (i, C_AQ // (AH * HD))),
                           pl.BlockSpec((br, 2 * AKV * HD), lambda i: (i, 0)), tab, tab, vec, vec,
                           pl.BlockSpec((AH, br, HD), lambda i: (0, jnp.maximum(i - lb, 0), 0)), kvb, kvb],
                 out_specs=(pl.BlockSpec((br, AH * HD), lambda i: (i, 0)),
                            pl.BlockSpec((br, 2 * AKV * HD), lambda i: (i, 0)), vec, vec),
                 sem=("arbitrary",))(proj, proj, cos, sin, qw, kw, dq, dk, dv)


def _attn_fn(q, k, v):
    s = _nt(q, k) * (HD ** -0.5)
    m = lax.stop_gradient(jnp.max(s, axis=-1, keepdims=True))
    e = jnp.exp(s - m)
    p = e / jnp.sum(e, axis=-1, keepdims=True)
    return _nn(p, v)


def _attn_fwd(q, k, v, L, exch, *, bq=128):
    T = q.shape[1]
    N = T - L
    lb = L // bq

    def body(q_ref, k_ref, v_ref, o_ref):
        qv = q_ref[...].reshape(GRP * bq, HD).astype(F32)
        o = _attn_fn(qv, k_ref[...].astype(F32), v_ref[...].astype(F32))
        for g in range(GRP):
            o_ref[:, g * HD:(g + 1) * HD] = o[g * bq:(g + 1) * bq].astype(BF16)

    kvb = pl.BlockSpec((None, T, HD), lambda g, i: (g, 0, 0))
    (attn,), moved = _call_carrying(
        body, exch, name="attn_fwd", out_shape=(_sds((N, AH * HD), BF16),), grid=(AKV, N // bq),
        in_specs=[pl.BlockSpec((GRP, bq, HD), lambda g, i: (g, i + lb, 0)), kvb, kvb],
        out_specs=(pl.BlockSpec((bq, GRP * HD), lambda g, i: (i, g)),), vmem=VMEM_BIG)(q, k, v)
    return attn, moved


def _attn_bwd(q, k, v, do, L, *, bq=128):
    T = q.shape[1]
    N = T - L
    lb = L // bq

    def body(q_ref, k_ref, v_ref, do_ref, dq_ref, dk_ref, dv_ref):
        qv = q_ref[...].reshape(GRP * bq, HD).astype(F32)
        _, vjp = jax.vjp(_attn_fn, qv, k_ref[...].astype(F32), v_ref[...].astype(F32))
        dov = jnp.concatenate([do_ref[:, g * HD:(g + 1) * HD] for g in range(GRP)], axis=0)
        dq, dk, dv = vjp(dov)
        dq_ref[...] = dq.reshape(GRP, bq, HD)

        @pl.when(pl.program_id(1) == 0)
        def _():
            dk_ref[...] = jnp.zeros_like(dk_ref)
            dv_ref[...] = jnp.zeros_like(dv_ref)

        dk_ref[...] += dk
        dv_ref[...] += dv

    kvb = pl.BlockSpec((None, T, HD), lambda g, i: (g, 0, 0))
    return _call(body, name="attn_bwd",
                 out_shape=(_sds((AH, N, HD)), _sds((AKV, T, HD)), _sds((AKV, T, HD))), grid=(AKV, N // bq),
                 in_specs=[pl.BlockSpec((GRP, bq, HD), lambda g, i: (g, i + lb, 0)), kvb, kvb,
                           pl.BlockSpec((bq, GRP * HD), lambda g, i: (i, g))],
                 out_specs=(pl.BlockSpec((GRP, bq, HD), lambda g, i: (g, i, 0)), kvb, kvb),
                 sem=("parallel", "arbitrary"), vmem=VMEM_BIG)(q, k, v, do)


def _gprep_fn(kind, shifts, x, w):
    down, up = shifts
    y = down(x) * w[0:1, :] + x * w[1:2, :] + up(x) * w[2:3, :]
    a = _silu(y)
    if kind == 2:
        return a
    a = a * lax.rsqrt(jnp.sum(a * a, axis=-1, keepdims=True) + EPS)
    return a * (HD ** -0.5) if kind == 0 else a


def _gprep_fwd(proj, conv_w, kind, bounds):
    T = proj.shape[0]
    shifts = _make_shift(bounds)
    cb = C_QKV // HD + kind * GH

    def body(x_ref, w_ref, o_ref):
        o_ref[...] = _gprep_fn(kind, shifts, x_ref[...], w_ref[...])

    return _call(body, name=f"gprep_fwd{kind}", out_shape=_sds((GH, T, HD)), grid=(GH,),
                 in_specs=[pl.BlockSpec((T, HD), lambda h: (0, cb + h)),
                           pl.BlockSpec((3, HD), lambda h: (0, kind * GH + h))],
                 out_specs=pl.BlockSpec((None, T, HD), lambda h: (h, 0, 0)), sem=("parallel",))(proj, conv_w)


def _gprep_bwd(proj, conv_w, kind, bounds, dy):
    T = proj.shape[0]
    shifts = _make_shift(bounds)
    cb = C_QKV // HD + kind * GH

    def body(x_ref, w_ref, dy_ref, dx_ref, dw_ref):
        _, vjp = jax.vjp(functools.partial(_gprep_fn, kind, shifts), x_ref[...], w_ref[...])
        dx, dw = vjp(dy_ref[0] + dy_ref[1])
        dx_ref[...] = dx.astype(BF16)
        dw_ref[...] = dw

    return _call(body, name=f"gprep_bwd{kind}", out_shape=(_sds((T, GH * HD), BF16), _sds((3, GH * HD))), grid=(GH,),
                 in_specs=[pl.BlockSpec((T, HD), lambda h: (0, cb + h)),
                           pl.BlockSpec((3, HD), lambda h: (0, kind * GH + h)),
                           pl.BlockSpec((2, None, T, HD), lambda h: (0, h, 0, 0))],
                 out_specs=(pl.BlockSpec((T, HD), lambda h: (0, h)), pl.BlockSpec((3, HD), lambda h: (0, h))),
                 sem=("parallel",))(proj, conv_w, dy)


def _bl_fn(x, alog, dtb):
    lane = lax.broadcasted_iota(jnp.int32, x.shape, 1)
    beta = jax.nn.sigmoid(x)
    z = x + dtb
    sp = jnp.maximum(z, 0.0) + jnp.log1p(jnp.exp(-jnp.abs(z)))
    la = -jnp.exp(alog) * sp
    return jnp.where(lane < 2 * GH, beta, jnp.where(lane < 4 * GH, la, 0.0))


def _bl_fwd(proj, alog, dtb, *, br=256):
    T = proj.shape[0]

    def body(x_ref, a_ref, d_ref, o_ref):
        o_ref[...] = _bl_fn(x_ref[...], a_ref[...], d_ref[...])

    vec = pl.BlockSpec((1, HD), lambda i: (0, 0))
    return _call(body, name="bl_fwd", out_shape=_sds((T, HD)), grid=(T // br,),
                 in_specs=[pl.BlockSpec((br, HD), lambda i: (i, C_BL // HD)), vec, vec],
                 out_specs=pl.BlockSpec((br, HD), lambda i: (i, 0)), sem=("parallel",))(proj, alog, dtb)


def _bl_bwd(proj, alog, dtb, dbl, *, br=256):
    T = proj.shape[0]

    def body(x_ref, a_ref, d_ref, g_ref, dx_ref, da_ref, dd_ref):
        g = g_ref[0, 0]
        for d in range(2):
            for h in range(GH):
                if d or h:
                    g = g + g_ref[d, h]
        _, vjp = jax.vjp(_bl_fn, x_ref[...], a_ref[...], d_ref[...])
        dx, da, dd = vjp(g)
        dx_ref[...] = dx.astype(BF16)

        @pl.when(pl.program_id(0) == 0)
        def _():
            da_ref[...] = jnp.zeros_like(da_ref)
            dd_ref[...] = jnp.zeros_like(dd_ref)

        da_ref[...] += da
        dd_ref[...] += dd

    vec = pl.BlockSpec((1, HD), lambda i: (0, 0))
    return _call(body, name="bl_bwd", out_shape=(_sds((T, HD), BF16), _sds((1, HD)), _sds((1, HD))), grid=(T // br,),
                 in_specs=[pl.BlockSpec((br, HD), lambda i: (i, C_BL // HD)), vec, vec,
                           pl.BlockSpec((2, GH, br, HD), lambda i: (0, 0, i, 0))],
                 out_specs=(pl.BlockSpec((br, HD), lambda i: (i, 0)), vec, vec), sem=("arbitrary",))(proj, alog, dtb, dbl)


def _chunk_masks(d):
    ii = lax.broadcasted_iota(jnp.int32, (CH, CH), 0)
    jj = lax.broadcasted_iota(jnp.int32, (CH, CH), 1)
    eye = (ii == jj).astype(F32)
    before = jnp.where(d == 0, (jj < ii).astype(F32), (jj > ii).astype(F32))
    return before, before + eye, eye


def _intra_fn(masks, sel_b, sel_l, qs, ks, vs, bls):
    before, ateq, eye = masks
    ones = jnp.ones((CH, CH), F32)
    inc = ateq > 0.0
    each = lambda f, *ls: [f(*t) for t in zip(*ls)]
    beta = each(lambda bl: jnp.sum(bl * sel_b, axis=-1, keepdims=True), bls)
    la = each(lambda bl: jnp.sum(bl * sel_l, axis=-1, keepdims=True), bls)
    gam = each(lambda a: _hdot(ateq, jnp.broadcast_to(a, (CH, HD))), la)
    gi = each(lambda a: _hdot(ateq, jnp.broadcast_to(a, (CH, CH))), la)
    gj = each(lambda g: _hdot(ones, eye * g), gi)
    kk = each(lambda k: _nt(k, k), ks)
    qk = each(_nt, qs, ks)
    dec = each(lambda a, b: jnp.where(inc, jnp.exp(jnp.where(inc, a - b, 0.0)), 0.0), gi, gj)
    lmat = each(lambda b, d, m: before * (b * d * m), beta, dec, kk)
    x = each(lambda m: eye - m, lmat)
    p2 = each(lambda m: _mdot(m, m), lmat)
    for it in range(5):
        x = each(lambda a, b: a + _mdot(a, b), x, p2)
        if it < 4:
            p2 = each(lambda m: _mdot(m, m), p2)
    eg = each(jnp.exp, gam)
    u = each(lambda a, b, v: _mdot(a, b * v), x, beta, vs)
    w = each(lambda a, b, e, k: _mdot(a, (b * e) * k), x, beta, eg, ks)
    tot = each(lambda a: jnp.sum(a, axis=0, keepdims=True), la)
    kd = each(lambda k, t, g: k * jnp.exp(t - g), ks, tot, gam)
    gl = each(lambda t: jnp.broadcast_to(jnp.exp(t), (1, HD)), tot)
    qd = each(lambda q, e: q * e, qs, eg)
    p = each(lambda d, m: d * m, dec, qk)
    return u, w, kd, qd, p, gl


def _dir_head_sel(d, h):
    lane = lax.broadcasted_iota(jnp.int32, (1, HD), 1)
    return (lane == d * GH + h).astype(F32), (lane == 2 * GH + d * GH + h).astype(F32)


def _intra_specs(T, G):
    nc = T // CH
    assert nc % G == 0
    qkv = pl.BlockSpec((None, G * CH, HD), lambda d, h, c: (h, c, 0))
    bl = pl.BlockSpec((G * CH, HD), lambda d, h, c: (c, 0))
    big = pl.BlockSpec((None, None, G * CH, HD), lambda d, h, c: (d, h, c, 0))
    pm = pl.BlockSpec((None, None, G * CH, CH), lambda d, h, c: (d, h, c, 0))
    gl = pl.BlockSpec((None, None, G, 1, HD), lambda d, h, c: (d, h, c, 0, 0))
    shapes = (_sds((2, GH, T, HD)),) + (_sds((2, GH, T, HD), BF16),) * 3 + (_sds((2, GH, T, CH), BF16),
                                                                           _sds((2, GH, nc, 1, HD)))
    return nc, qkv, bl, big, pm, gl, shapes


def _chunks_per_step(T, most):
    nc = T // CH
    return max(g for g in range(1, most + 1) if nc % g == 0)


def _intra_fwd(q, k, v, bl, exch):
    T = q.shape[1]
    G = _chunks_per_step(T, INTRA_FWD_CHUNKS)
    nc, qkv_s, bl_s, big, pm, gl_s, shapes = _intra_specs(T, G)

    def body(q_ref, k_ref, v_ref, bl_ref, u_ref, w_ref, kd_ref, qd_ref, p_ref, gl_ref):
        d, h = pl.program_id(0), pl.program_id(1)
        sb, sl = _dir_head_sel(d, h)
        rows = [slice(g * CH, (g + 1) * CH) for g in range(G)]
        outs = _intra_fn(_chunk_masks(d), sb, sl, *[[r[s, :] for s in rows] for r in (q_ref, k_ref, v_ref, bl_ref)])
        for g in range(G):
            for r, o in zip((u_ref, w_ref, kd_ref, qd_ref, p_ref), outs[:5]):
                r[rows[g], :] = o[g].astype(r.dtype)
            gl_ref[g] = outs[5][g]

    return _call_carrying(body, exch, name="gdn_intra_fwd", out_shape=shapes, grid=(2, GH, nc // G),
                          in_specs=[qkv_s, qkv_s, qkv_s, bl_s], out_specs=(big, big, big, big, pm, gl_s))(q, k, v, bl)


def _intra_bwd(q, k, v, bl, cts, exch):
    T = q.shape[1]
    G = _chunks_per_step(T, INTRA_BWD_CHUNKS)
    nc, qkv_s, bl_s, big, pm, gl_s, _ = _intra_specs(T, G)

    def body(q_ref, k_ref, v_ref, bl_ref, du, dw, dkd, dqd, dp, dgl, dq_ref, dk_ref, dv_ref, dbl_ref):
        d, h = pl.program_id(0), pl.program_id(1)
        sb, sl = _dir_head_sel(d, h)
        fn = functools.partial(_intra_fn, _chunk_masks(d), sb, sl)
        rows = [slice(g * CH, (g + 1) * CH) for g in range(G)]
        _, vjp = jax.vjp(fn, *[[r[s, :] for s in rows] for r in (q_ref, k_ref, v_ref, bl_ref)])
        cts = tuple([r[s, :] for s in rows] for r in (du, dw, dkd, dqd, dp)) + ([dgl[g] for g in range(G)],)
        grads = vjp(cts)
        for g in range(G):
            for r, o in zip((dq_ref, dk_ref, dv_ref, dbl_ref), grads):
                r[rows[g], :] = o[g]

    return _call_carrying(body, exch, name="gdn_intra_bwd", out_shape=(_sds((2, GH, T, HD)),) * 4,
                          grid=(2, GH, nc // G), in_specs=[qkv_s, qkv_s, qkv_s, bl_s, big, big, big, big, pm, gl_s],
                          out_specs=(big,) * 4)(q, k, v, bl, *cts)


def _scan_fn(s, u, w, kd, qd, p, gl):
    each = lambda f, *ls: [f(*t) for t in zip(*ls)]
    ws = each(_nn, w, s)
    delta = each(lambda a, b: a - b, u, ws)
    kdd = each(_tn, kd, delta)
    s_new = each(lambda g, a, b: g * a + b, gl, s, kdd)
    qs = each(_nn, qd, s)
    pd = each(_nn, p, delta)
    return each(lambda a, b: a + b, qs, pd), s_new


SCAN_BLOCK = 4


def _scan_visit(t, d, nb, ncb):
    rev = jnp.where(t < ncb, ncb - 1 - t, nb - 1 - (t - ncb))
    return jnp.where(d == 0, t, rev)


def _scan_specs(T, L, back):
    tb = SCAN_BLOCK * CH
    assert T % tb == 0 and L % tb == 0
    nb, ncb = T // tb, L // tb

    def at(d, t):
        return _scan_visit(nb - 1 - t if back else t, d, nb, ncb)

    big = pl.BlockSpec((None, GH, tb, HD), lambda d, t: (d, 0, at(d, t), 0))
    pm = pl.BlockSpec((None, GH, tb, CH), lambda d, t: (d, 0, at(d, t), 0))
    gl = pl.BlockSpec((None, GH, SCAN_BLOCK, 1, HD), lambda d, t: (d, 0, at(d, t), 0, 0))
    st = pl.BlockSpec((None, GH, SCAN_BLOCK, HD, HD), lambda d, t: (d, 0, at(d, t), 0, 0))
    do = pl.BlockSpec((GH, tb, HD), lambda d, t: (0, at(d, t), 0))
    return nb, big, pm, gl, st, do


def _scan_fwd(u, w, kd, qd, p, gl, L):
    T = u.shape[2]
    nb, big, pm, gl_s, st, _ = _scan_specs(T, L, False)
    heads = range(GH)

    def body(u_ref, w_ref, kd_ref, qd_ref, p_ref, gl_ref, o_ref, st_ref, s_scr):
        d = pl.program_id(0)

        @pl.when(pl.program_id(1) == 0)
        def _():
            s_scr[...] = jnp.zeros_like(s_scr)

        s = [s_scr[h] for h in heads]
        for i in range(SCAN_BLOCK):
            c = jnp.where(d == 0, i, SCAN_BLOCK - 1 - i)
            rows = pl.ds(pl.multiple_of(c * CH, CH), CH)
            for h in heads:
                st_ref[h, c] = s[h]
            o, s = _scan_fn(s, *[[r[h, rows, :].astype(F32) for h in heads] for r in (u_ref, w_ref, kd_ref, qd_ref, p_ref)],
                            [gl_ref[h, c] for h in heads])
            for h in heads:
                o_ref[h, rows, :] = o[h]
        for h in heads:
            s_scr[h] = s[h]

    return _call(body, name="gdn_scan_fwd", out_shape=(_sds((2, GH, T, HD)), _sds((2, GH, T // CH, HD, HD))),
                 grid=(2, nb), in_specs=[big, big, big, big, pm, gl_s], out_specs=(big, st),
                 scratch=[pltpu.VMEM((GH, HD, HD), F32)], sem=("parallel", "arbitrary"))(u, w, kd, qd, p, gl)


def _scan_bwd(u, w, kd, qd, p, gl, states, do, L):
    T = u.shape[2]
    nb, big, pm, gl_s, st, do_s = _scan_specs(T, L, True)
    heads = range(GH)

    def body(u_ref, w_ref, kd_ref, qd_ref, p_ref, gl_ref, st_ref, do_ref,
             du_ref, dw_ref, dkd_ref, dqd_ref, dp_ref, dgl_ref, ds_scr):
        d = pl.program_id(0)

        @pl.when(pl.program_id(1) == 0)
        def _():
            ds_scr[...] = jnp.zeros_like(ds_scr)

        ds = [ds_scr[h] for h in heads]
        for i in range(SCAN_BLOCK):
            c = jnp.where(d == 0, SCAN_BLOCK - 1 - i, i)
            rows = pl.ds(pl.multiple_of(c * CH, CH), CH)
            _, vjp = jax.vjp(_scan_fn, [st_ref[h, c] for h in heads],
                             *[[r[h, rows, :].astype(F32) for h in heads] for r in (u_ref, w_ref, kd_ref, qd_ref, p_ref)],
                             [gl_ref[h, c] for h in heads])
            ds, gu, gw, gkd, gqd, gp, ggl = vjp(([do_ref[h, rows, :] for h in heads], ds))
            for h in heads:
                du_ref[h, rows, :] = gu[h]
                dw_ref[h, rows, :] = gw[h]
                dkd_ref[h, rows, :] = gkd[h]
                dqd_ref[h, rows, :] = gqd[h]
                dp_ref[h, rows, :] = gp[h]
                dgl_ref[h, c] = ggl[h]
        for h in heads:
            ds_scr[h] = ds[h]

    return _call(body, name="gdn_scan_bwd",
                 out_shape=(_sds((2, GH, T, HD)),) * 4 + (_sds((2, GH, T, CH)), _sds((2, GH, T // CH, 1, HD))),
                 grid=(2, nb), in_specs=[big, big, big, big, pm, gl_s, st, do_s],
                 out_specs=(big, big, big, big, pm, gl_s), scratch=[pltpu.VMEM((GH, HD, HD), F32)],
                 sem=("parallel", "arbitrary"))(u, w, kd, qd, p, gl, states, do)


def _gout_fn(o0, o1, z, gw):
    return _rms(o0 + o1) * gw * _silu(z)


def _gout_fwd(o, proj, gw, L, *, br=256):
    T = o.shape[2]
    N = T - L
    lb = L // br
    ob = pl.BlockSpec((None, None, br, HD), lambda i, h: (0, h, i + lb, 0))
    ob1 = pl.BlockSpec((None, None, br, HD), lambda i, h: (1, h, i + lb, 0))

    def body(o0_ref, o1_ref, z_ref, gw_ref, y_ref):
        y_ref[...] = _gout_fn(o0_ref[...], o1_ref[...], z_ref[...], gw_ref[...]).astype(BF16)

    return _call(body, name="gout_fwd", out_shape=_sds((N, GH * HD), BF16), grid=(N // br, GH),
                 in_specs=[ob, ob1, pl.BlockSpec((br, HD), lambda i, h: (i + lb, C_Z // HD + h)),
                           pl.BlockSpec((1, HD), lambda i, h: (0, 0))],
                 out_specs=pl.BlockSpec((br, HD), lambda i, h: (i, h)), sem=("parallel", "parallel"))(o, o, proj, gw)


def _gout_bwd(o, proj, gw, dy, L, *, br=256):
    T = o.shape[2]
    lb = L // br
    ob = pl.BlockSpec((None, None, br, HD), lambda i, h: (0, h, i, 0))
    ob1 = pl.BlockSpec((None, None, br, HD), lambda i, h: (1, h, i, 0))

    def body(o0_ref, o1_ref, z_ref, gw_ref, dy_ref, do_ref, dz_ref, dgw_ref):
        i, h = pl.program_id(0), pl.program_id(1)
        _, vjp = jax.vjp(_gout_fn, o0_ref[...], o1_ref[...], z_ref[...], gw_ref[...])
        g0, _, gz, ggw = vjp(dy_ref[...])
        lat = i >= lb
        do_ref[...] = jnp.where(lat, g0, 0.0)
        dz_ref[...] = jnp.where(lat, gz, 0.0).astype(BF16)

        @pl.when(jnp.logical_and(i == 0, h == 0))
        def _():
            dgw_ref[...] = jnp.zeros_like(dgw_ref)

        dgw_ref[...] += jnp.where(lat, ggw, 0.0)

    return _call(body, name="gout_bwd", out_shape=(_sds((GH, T, HD)), _sds((T, GH * HD), BF16), _sds((1, HD))),
                 grid=(T // br, GH),
                 in_specs=[ob, ob1, pl.BlockSpec((br, HD), lambda i, h: (i, C_Z // HD + h)),
                           pl.BlockSpec((1, HD), lambda i, h: (0, 0)),
                           pl.BlockSpec((br, HD), lambda i, h: (jnp.maximum(i - lb, 0), h))],
                 out_specs=(pl.BlockSpec((None, br, HD), lambda i, h: (h, i, 0)),
                            pl.BlockSpec((br, HD), lambda i, h: (i, h)),
                            pl.BlockSpec((1, HD), lambda i, h: (0, 0))),
                 sem=("arbitrary", "arbitrary"))(o, o, proj, gw, dy)


def _merge_fn(pa, pd, ga, gd):
    return jax.nn.sigmoid(ga) * pa + jax.nn.sigmoid(gd) * pd


def _merge_fwd(pa, pd, proj, L, *, br=256):
    N = pa.shape[0]
    lb = L // br
    row = pl.BlockSpec((br, D), lambda i: (i, 0))

    def body(pa_ref, pd_ref, ga_ref, gd_ref, y_ref):
        y_ref[...] = _merge_fn(pa_ref[...], pd_ref[...], ga_ref[...], gd_ref[...]).astype(BF16)

    return _call(body, name="merge_fwd", out_shape=_sds((N, D), BF16), grid=(N // br,),
                 in_specs=[row, row, pl.BlockSpec((br, D), lambda i: (i + lb, C_GATE // D)),
                           pl.BlockSpec((br, D), lambda i: (i + lb, C_GATE // D + 1))],
                 out_specs=row, sem=("parallel",))(pa, pd, proj, proj)


def _merge_bwd(pa, pd, proj, dy, L, *, br=256):
    N = pa.shape[0]
    T = N + L
    lb = L // br
    lrow = pl.BlockSpec((br, D), lambda i: (jnp.maximum(i - lb, 0), 0))

    def body(pa_ref, pd_ref, ga_ref, gd_ref, dy_ref, dpa_ref, dpd_ref, dg_ref):
        lat = pl.program_id(0) >= lb
        _, vjp = jax.vjp(_merge_fn, pa_ref[...], pd_ref[...], ga_ref[...], gd_ref[...])
        gpa, gpd, gga, ggd = vjp(dy_ref[...])
        dpa_ref[...] = gpa.astype(BF16)
        dpd_ref[...] = gpd.astype(BF16)
        dg_ref[:, :D] = jnp.where(lat, gga, 0.0).astype(BF16)
        dg_ref[:, D:] = jnp.where(lat, ggd, 0.0).astype(BF16)

    return _call(body, name="merge_bwd", out_shape=(_sds((N, D), BF16), _sds((N, D), BF16), _sds((T, 2 * D), BF16)),
                 grid=(T // br,),
                 in_specs=[lrow, lrow, pl.BlockSpec((br, D), lambda i: (i, C_GATE // D)),
                           pl.BlockSpec((br, D), lambda i: (i, C_GATE // D + 1)), lrow],
                 out_specs=(lrow, lrow, pl.BlockSpec((br, 2 * D), lambda i: (i, 0))),
                 sem=("arbitrary",))(pa, pd, proj, proj, dy)


def _resid_fwd(x, m, mod, i_g, *, name, br=256):
    R = x.shape[0]
    row = pl.BlockSpec((br, D), lambda i: (i, 0))

    def body(x_ref, m_ref, mod_ref, o_ref):
        o_ref[...] = x_ref[...] + mod_ref[i_g:i_g + 1, :] * m_ref[...]

    return _call(body, name=name, out_shape=_sds((R, D)), grid=(R // br,),
                 in_specs=[row, row, pl.BlockSpec((6, D), lambda i: (0, 0))], out_specs=row,
                 sem=("parallel",))(x, m, mod)


def _resid_bwd(dx, m, mod, i_g, *, name, br=256):
    R = dx.shape[0]
    row = pl.BlockSpec((br, D), lambda i: (i, 0))
    vec = pl.BlockSpec((1, D), lambda i: (0, 0))

    def body(dx_ref, m_ref, mod_ref, dm_ref, dg_ref):
        dxv = dx_ref[...]
        dm_ref[...] = (dxv * mod_ref[i_g:i_g + 1, :]).astype(BF16)

        @pl.when(pl.program_id(0) == 0)
        def _():
            dg_ref[...] = jnp.zeros_like(dg_ref)

        dg_ref[...] += jnp.sum(dxv * m_ref[...], axis=0, keepdims=True)

    return _call(body, name=name, out_shape=(_sds((R, D), BF16), _sds((1, D))), grid=(R // br,),
                 in_specs=[row, row, pl.BlockSpec((6, D), lambda i: (0, 0))], out_specs=(row, vec),
                 sem=("arbitrary",))(dx, m, mod)


def _ffn_fn(shifts, ug, uv, wg, wv, bg, bv):
    down, up = shifts

    def conv(x, w, b):
        return down(x) * w[0:1, :] + x * w[1:2, :] + up(x) * w[2:3, :] + b

    return _silu(conv(ug, wg, bg)) * conv(uv, wv, bv)


def _ffn_fwd(up, cw, cb, *, bw=256):
    N = up.shape[0]
    shifts = _make_shift(((0, N),))
    nb = DFF // bw

    def body(ug, uv, wg, wv, bg, bv, a_ref):
        a_ref[...] = _ffn_fn(shifts, ug[...], uv[...], wg[...], wv[...], bg[...], bv[...]).astype(BF16)

    def col(rows, off):
        return pl.BlockSpec((rows, bw), lambda j: (0, j + off))

    return _call(body, name="ffn_fwd", out_shape=_sds((N, DFF), BF16), grid=(nb,),
                 in_specs=[col(N, 0), col(N, nb), col(3, 0), col(3, nb), col(1, 0), col(1, nb)],
                 out_specs=col(N, 0), sem=("parallel",), vmem=VMEM_BIG)(up, up, cw, cw, cb, cb)


def _ffn_bwd(up, cw, cb, da, *, bw=256):
    N = up.shape[0]
    shifts = _make_shift(((0, N),))
    nb = DFF // bw

    def body(ug, uv, wg, wv, bg, bv, da_ref, dug, duv, dwg, dwv, dbg, dbv):
        _, vjp = jax.vjp(functools.partial(_ffn_fn, shifts), ug[...], uv[...], wg[...], wv[...], bg[...], bv[...])
        g = vjp(da_ref[...])
        dug[...] = g[0].astype(BF16)
        duv[...] = g[1].astype(BF16)
        dwg[...], dwv[...], dbg[...], dbv[...] = g[2], g[3], g[4], g[5]

    def col(rows, off):
        return pl.BlockSpec((rows, bw), lambda j: (0, j + off))

    half = (_sds((N, DFF), BF16), _sds((N, DFF), BF16), _sds((3, DFF)), _sds((3, DFF)), _sds((1, DFF)), _sds((1, DFF)))
    dug, duv, dwg, dwv, dbg, dbv = _call(
        body, name="ffn_bwd", out_shape=half, grid=(nb,),
        in_specs=[col(N, 0), col(N, nb), col(3, 0), col(3, nb), col(1, 0), col(1, nb), col(N, 0)],
        out_specs=(col(N, 0), col(N, 0), col(3, 0), col(3, 0), col(1, 0), col(1, 0)),
        sem=("parallel",), vmem=VMEM_BIG)(up, up, cw, cw, cb, cb, da)
    return (jnp.concatenate([dug, duv], axis=1), jnp.concatenate([dwg, dwv], axis=1),
            jnp.concatenate([dbg, dbv], axis=1))


def _head_fn(x1, dn, g2, fw, tgt):
    y = _rms(x1 + g2 * dn) * fw
    err = y - tgt
    return 0.5 * jnp.sum(jnp.mean(err * err, axis=-1))


def _head(x1, dn, mod, fw, tgt, *, br=256):
    N = x1.shape[0]
    row = pl.BlockSpec((br, D), lambda i: (i, 0))
    vec = pl.BlockSpec((1, D), lambda i: (0, 0))
    one = pl.BlockSpec((1, HD), lambda i: (0, 0))

    def body(x1_ref, dn_ref, mod_ref, fw_ref, tgt_ref, loss_ref, dx_ref, ddn_ref, dg_ref, dfw_ref):
        loss, (gx, gdn, gg, gfw) = jax.value_and_grad(_head_fn, argnums=(0, 1, 2, 3))(
            x1_ref[...], dn_ref[...], mod_ref[5:6, :], fw_ref[...], tgt_ref[...])
        dx_ref[...] = gx
        ddn_ref[...] = gdn.astype(BF16)

        @pl.when(pl.program_id(0) == 0)
        def _():
            loss_ref[...] = jnp.zeros_like(loss_ref)
            dg_ref[...] = jnp.zeros_like(dg_ref)
            dfw_ref[...] = jnp.zeros_like(dfw_ref)

        loss_ref[...] += jnp.broadcast_to(loss, (1, HD))
        dg_ref[...] += gg
        dfw_ref[...] += gfw

    return _call(body, name="head", out_shape=(_sds((1, HD)), _sds((N, D)), _sds((N, D), BF16), _sds((1, D)), _sds((1, D))),
                 grid=(N // br,), in_specs=[row, row, pl.BlockSpec((6, D), lambda i: (0, 0)), vec, row],
                 out_specs=(one, row, row, vec, vec), sem=("arbitrary",))(x1, dn, mod, fw, tgt)


def _adamw(w, g, m, v, *, name):
    shape = w.shape
    cols = shape[-1]
    rows = max(1, math.prod(shape[:-1]))
    w2, g2, m2, v2 = (t.reshape(rows, cols) for t in (w, g, m, v))
    br = 256 if rows % 256 == 0 else (128 if rows % 128 == 0 else (8 if rows % 8 == 0 and rows > 64 else rows))
    if rows % 352 == 0:
        br = 352
    c1 = 1.0 - B1 ** STEP
    c2 = 1.0 - B2 ** STEP

    def body(w_ref, g_ref, m_ref, v_ref, d_ref, nm_ref, nv_ref):
        gv = g_ref[...]
        nm = B1 * m_ref[...] + (1.0 - B1) * gv
        nv = B2 * v_ref[...] + (1.0 - B2) * (gv * gv)
        d_ref[...] = -LR * ((nm / c1) / (jnp.sqrt(nv / c2) + AEPS) + WD * w_ref[...])
        nm_ref[...] = nm
        nv_ref[...] = nv

    blk = pl.BlockSpec((br, cols), lambda i: (i, 0))
    outs = _call(body, name=name, out_shape=(_sds((rows, cols)),) * 3, grid=(rows // br,),
                 in_specs=[blk] * 4, out_specs=(blk,) * 3, sem=("parallel",))(w2, g2, m2, v2)
    return tuple(t.reshape(shape) for t in outs)


def _rope_tables(N, L):
    t = jnp.arange(N)
    pos = jnp.stack([(t // GRID_W).astype(F32), (t % GRID_W).astype(F32)], axis=1)
    inv = ROPE_THETA ** (-jnp.arange(0, HD // 2, 2, dtype=F32) / (HD // 2))
    ang = pos[:, :, None] * inv[None, None, :]
    cos = jnp.broadcast_to(jnp.cos(ang)[:, :, None, :], (N, 2, 2, HD // 4)).reshape(N, HD)
    sin = jnp.broadcast_to(jnp.sin(ang)[:, :, None, :], (N, 2, 2, HD // 4))
    sin = (sin * jnp.array([-1.0, 1.0], F32)[None, None, :, None]).reshape(N, HD)
    cos = jnp.concatenate([jnp.ones((L, HD), F32), cos], axis=0)
    sin = jnp.concatenate([jnp.zeros((L, HD), F32), sin], axis=0)
    return cos, sin


def _pad_lanes(v, off=0):
    return jnp.zeros((1, HD), F32).at[0, off:off + v.shape[0]].set(v)


def _local_step(x, ctx, tgt, mod_lat, mod_ctx, w_in, shards, small):
    N, L = x.shape[0], ctx.shape[0]
    T = N + L
    bounds = ((0, L), (L, T))
    qw, kw, gw = small["q_norm_w"], small["k_norm_w"], small["gdn_norm_w"]
    conv_w, ffn_w, ffn_b, fnw = small["conv_qkv_w"], small["ffn_conv_w"], small["ffn_conv_b"], small["final_norm_w"]
    alog = _pad_lanes(small["a_log"].reshape(-1), 2 * GH)
    dtb = _pad_lanes(small["dt_bias"].reshape(-1), 2 * GH)
    cos, sin = _rope_tables(N, L)
    bt = 256 if T % 768 else 768
    bnl = 256 if N % 1024 else 1024

    hc = _normmod_fwd(ctx, mod_ctx, 0, 1, name="normmod_ctx")
    hx = _normmod_fwd(x, mod_lat, 0, 1, name="normmod_x")
    h1 = jnp.concatenate([hc, hx], axis=0)
    proj = _mm(h1, w_in, name="mm_in", M=T, N=C_END, K=D, bm=bt, bn=1024)
    aq, ak, av = _aprep_fwd(proj, cos, sin, qw, kw)
    attn, (up_g,) = _attn_fwd(aq, ak, av, L, _Exchange([shards["w_up"]], False))
    gq = _gprep_fwd(proj, conv_w, 0, bounds)
    gk = _gprep_fwd(proj, conv_w, 1, bounds)
    gv = _gprep_fwd(proj, conv_w, 2, bounds)
    bl = _bl_fwd(proj, alog, dtb)
    intra, (down_g, pa_g, pd_g, out_g) = _intra_fwd(
        gq, gk, gv, bl, _Exchange([shards[n] for n in ("w_down", "w_pa", "w_pd", "w_out")], False))
    w_up, w_down = _by_columns(up_g), down_g.reshape(DFF, D)
    w_pa, w_pd, w_out = pa_g.reshape(D, D), pd_g.reshape(D, D), out_g.reshape(D, D)
    o, states = _scan_fwd(*intra, L)
    gdn = _gout_fwd(o, proj, gw, L)
    pa = _mm(attn, w_pa, name="mm_pa", M=N, N=D, K=D, bm=bnl)
    pd = _mm(gdn, w_pd, name="mm_pd", M=N, N=D, K=D, bm=bnl)
    y = _merge_fwd(pa, pd, proj, L)
    m = _mm(y, w_out, name="mm_out", M=N, N=D, K=D, bm=bnl)
    x1 = _resid_fwd(x, m, mod_lat, 2, name="resid1")
    h2 = _normmod_fwd(x1, mod_lat, 3, 4, name="normmod_x1")
    up = _mm(h2, w_up, name="mm_up", M=N, N=2 * DFF, K=D, bm=bnl, bn=2 * DFF // 4)
    a = _ffn_fwd(up, ffn_w, ffn_b)
    dn = _mm(a, w_down, name="mm_down", M=N, N=D, K=DFF, bm=bnl)
    loss, dx2, ddn, dg2, dfnw = _head(x1, dn, mod_lat, fnw, tgt)

    da = _mm(ddn, w_down, name="mm_down_dx", M=N, N=DFF, K=D, tb=True, bm=bnl, bn=DFF // 2)
    g_down = _mm(a, ddn, name="mm_down_dw", M=DFF, N=D, K=N, ta=True, bm=DFF // 2)
    dup, d_ffn_w, d_ffn_b = _ffn_bwd(up, ffn_w, ffn_b, da)
    dh2 = _mm(dup, w_up, name="mm_up_dx", M=N, N=D, K=2 * DFF, tb=True, bm=bnl, bk=2 * DFF // 4)
    g_up = _mm(h2, dup, name="mm_up_dw", M=D, N=2 * DFF, K=N, ta=True, bn=2 * DFF // 4)
    dx1, dsh2, dsc2 = _normmod_bwd(x1, mod_lat, 3, 4, dh2, 0, dx2, name="normmod_x1_bwd")
    dm, dg1 = _resid_bwd(dx1, m, mod_lat, 2, name="resid1_bwd")
    dy = _mm(dm, w_out, name="mm_out_dx", M=N, N=D, K=D, tb=True, bm=bnl)
    g_out = _mm(y, dm, name="mm_out_dw", M=D, N=D, K=N, ta=True)
    dpa, dpd, dgate = _merge_bwd(pa, pd, proj, dy, L)
    dattn = _mm(dpa, w_pa, name="mm_pa_dx", M=N, N=D, K=D, tb=True, bm=bnl)
    g_pa = _mm(attn, dpa, name="mm_pa_dw", M=D, N=D, K=N, ta=True)
    dgdn = _mm(dpd, w_pd, name="mm_pd_dx", M=N, N=D, K=D, tb=True, bm=bnl)
    g_pd = _mm(gdn, dpd, name="mm_pd_dw", M=D, N=D, K=N, ta=True)
    do, dz, dgw = _gout_bwd(o, proj, gw, dgdn, L)
    cts = _scan_bwd(*intra, states, do, L)
    parts = [g_pa.reshape(NDEV, D // NDEV, D), g_pd.reshape(NDEV, D // NDEV, D), g_out.reshape(NDEV, D // NDEV, D),
             _to_columns(g_up), g_down.reshape(NDEV, DFF // NDEV, D)]
    (dgq, dgk, dgv, dbl), recv = _intra_bwd(gq, gk, gv, bl, cts, _Exchange([p.astype(BF16) for p in parts], True))
    dxq, dwq = _gprep_bwd(proj, conv_w, 0, bounds, dgq)
    dxk, dwk = _gprep_bwd(proj, conv_w, 1, bounds, dgk)
    dxv, dwv = _gprep_bwd(proj, conv_w, 2, bounds, dgv)
    dxbl, dalog, ddtb = _bl_bwd(proj, alog, dtb, dbl)
    daq_h, dak_h, dav_h = _attn_bwd(aq, ak, av, dattn, L)
    daq, dkv, dqw, dkw = _aprep_bwd(proj, cos, sin, qw, kw, daq_h, dak_h, dav_h, L)
    dproj = jnp.concatenate([dkv, dxq, dxk, dxv, dxbl, jnp.zeros((T, C_AQ - C_BL - HD), BF16), daq, dz, dgate], axis=1)
    dh1 = _mm(dproj, w_in, name="mm_in_dx", M=T, N=D, K=C_END, tb=True, bm=bt, bk=1024)
    g_in = _mm(h1, dproj, name="mm_in_dw", M=D, N=C_END, K=T, ta=True, bn=1024)
    grad_x, dsh1, dsc1 = _normmod_bwd(x, mod_lat, 0, 1, dh1, L, dx1, name="normmod_x_bwd")
    _, dcsh1, dcsc1 = _normmod_bwd(ctx, mod_ctx, 0, 1, dh1, 0, None, name="normmod_ctx_bwd")

    z1 = jnp.zeros((1, D), F32)
    dmod_lat = jnp.concatenate([dsh1, dsc1, dg1, dsh2, dsc2, dg2], axis=0)
    dmod_ctx = jnp.concatenate([dcsh1, dcsc1, z1, z1, z1, z1], axis=0)
    gsmall = {
        "q_norm_w": dqw, "k_norm_w": dkw, "gdn_norm_w": dgw,
        "conv_qkv_w": jnp.concatenate([dwq, dwk, dwv], axis=1),
        "a_log": dalog[0, 2 * GH:4 * GH], "dt_bias": ddtb[0, 2 * GH:4 * GH],
        "ffn_conv_w": d_ffn_w, "ffn_conv_b": d_ffn_b, "final_norm_w": dfnw,
    }
    return loss[0, 0], grad_x, g_in, dict(zip(("w_pa", "w_pd", "w_out", "w_up", "w_down"), recv)), dmod_lat, dmod_ctx, gsmall


HBM = pl.BlockSpec(memory_space=pltpu.HBM)


def _position():
    x, y, c = lax.axis_index("x"), lax.axis_index("y"), lax.axis_index("c")
    return x, y, c, 4 * x + 2 * y + c


def _peer(x, y, c, k):
    px = 1 - x if k & 4 else x
    py = 1 - y if k & 2 else y
    pc = 1 - c if k & 1 else c
    return (px, py, pc), 4 * px + 2 * py + pc


def _exchange(arrs, *, name, scatter):
    exch = _Exchange(arrs, scatter)
    n = exch.n

    def body(*refs):
        ins, outs, sems = refs[:n], refs[n:2 * n], refs[2 * n:]
        exch.start(ins, outs, sems)
        exch.finish(ins, outs, sems)

    outs = pl.pallas_call(body, name=name, out_shape=exch.out_shape, in_specs=[HBM] * n, out_specs=(HBM,) * n,
                          scratch_shapes=exch.scratch,
                          compiler_params=pltpu.CompilerParams(has_side_effects=True))(*arrs)
    return list(outs)


class _Exchange:
    def __init__(self, arrs, scatter):
        self.arrs, self.scatter, self.n = list(arrs), scatter, len(arrs)
        self.out_shape = tuple(_sds(a.shape if scatter else (NDEV,) + a.shape, a.dtype) for a in arrs)
        self.scratch = [pltpu.SemaphoreType.DMA((self.n, NDEV - 1)), pltpu.SemaphoreType.DMA((self.n, NDEV - 1)),
                        pltpu.SemaphoreType.DMA((self.n,))]

    def _copies(self, ins, outs, sems):
        send, recv, loc = sems
        x, y, c, me = _position()
        local = [pltpu.make_async_copy(ins[a].at[me] if self.scatter else ins[a], outs[a].at[me], loc.at[a])
                 for a in range(self.n)]
        remote = []
        for k in range(1, NDEV):
            peer, pid = _peer(x, y, c, k)
            for a in range(self.n):
                src = ins[a].at[pid] if self.scatter else ins[a]
                remote.append(pltpu.make_async_remote_copy(
                    src_ref=src, dst_ref=outs[a].at[me], send_sem=send.at[a, k - 1], recv_sem=recv.at[a, k - 1],
                    device_id=peer, device_id_type=MESH))
        return local, remote

    def start(self, ins, outs, sems):
        local, remote = self._copies(ins, outs, sems)
        for cp in local + remote:
            cp.start()

    def finish(self, ins, outs, sems):
        local, remote = self._copies(ins, outs, sems)
        for cp in remote:
            cp.wait()
        for cp in local:
            cp.wait()


def _cast_bf16(w, *, name):
    rows, cols = w.shape
    br = 128 if rows % 128 == 0 else rows

    def body(w_ref, o_ref):
        o_ref[...] = w_ref[...].astype(BF16)

    blk = pl.BlockSpec((br, cols), lambda i: (i, 0))
    return _call(body, name=name, out_shape=_sds((rows, cols), BF16), grid=(rows // br,), in_specs=[blk],
                 out_specs=blk, sem=("parallel",))(w)


def _sum_slots(a, *, name):
    _, R, C = a.shape

    def body(a_ref, o_ref):
        s = a_ref[0]
        for d in range(1, NDEV):
            s = s + a_ref[d]
        o_ref[...] = s

    return _call(body, name=name, out_shape=_sds((R, C)))(a)


MODROWS = 16


def _mod_fwd(c9, w, b):
    cols = w.shape[1]

    def body(c_ref, w_ref, b_ref, o_ref):
        o_ref[...] = _nn(_silu(c_ref[...]), w_ref[...]) + b_ref[...]

    return _call(body, name="mod_fwd", out_shape=_sds((MODROWS, cols)))(c9, w, b)


def _mod_bwd(c9, dmy, dall, w):
    cols = w.shape[1]

    def body(c_ref, dmy_ref, dall_ref, w_ref, gw_ref, gb_ref, cp_ref):
        sc = _silu(c_ref[...])
        rows = lax.broadcasted_iota(jnp.int32, (MODROWS, 1), 0)
        d = dmy_ref[...]
        d_ctx = jnp.where(rows == NDEV, d, 0.0)
        sc_ctx = jnp.where(rows == NDEV, sc, 0.0)
        outer = lax.dot_general(sc_ctx, d_ctx, (((0,), (0,)), ((), ())), precision=HI, preferred_element_type=F32)
        gw_ref[...] = _tn(jnp.where(rows < NDEV, sc, 0.0), jnp.where(rows < NDEV, d, 0.0)) + outer
        gb_ref[...] = jnp.sum(dall_ref[...], axis=0, keepdims=True)
        cp_ref[...] = jnp.sum(_nt(d_ctx, w_ref[...]), axis=0, keepdims=True)

    return _call(body, name="mod_bwd", out_shape=(_sds((D, cols)), _sds((1, 6 * D)), _sds((1, D))),
                 vmem=VMEM_BIG)(c9, dmy, dall, w)


def _cctx_finish(parts, c_ctx):
    def body(p_ref, c_ref, o_ref):
        s = p_ref[0]
        for d in range(1, NDEV):
            s = s + p_ref[d]
        _, vjp = jax.vjp(_silu, c_ref[...])
        o_ref[...] = vjp(s)[0]

    return _call(body, name="cctx_finish", out_shape=_sds((1, D)))(parts, c_ctx)


def _adamw_recv(w, recv, m, v, *, name):
    rows, cols = w.shape
    br = 128 if rows % 128 == 0 else rows
    c1 = 1.0 - B1 ** STEP
    c2 = 1.0 - B2 ** STEP

    def body(w_ref, r_ref, m_ref, v_ref, g_ref, d_ref, nm_ref, nv_ref):
        gv = r_ref[0].astype(F32)
        for d in range(1, NDEV):
            gv = gv + r_ref[d].astype(F32)
        nm = B1 * m_ref[...] + (1.0 - B1) * gv
        nv = B2 * v_ref[...] + (1.0 - B2) * (gv * gv)
        g_ref[...] = gv
        d_ref[...] = -LR * ((nm / c1) / (jnp.sqrt(nv / c2) + AEPS) + WD * w_ref[...])
        nm_ref[...] = nm
        nv_ref[...] = nv

    blk = pl.BlockSpec((br, cols), lambda i: (i, 0))
    return _call(body, name=name, out_shape=(_sds((rows, cols)),) * 4, grid=(rows // br,),
                 in_specs=[blk, pl.BlockSpec((NDEV, br, cols), lambda i: (0, i, 0)), blk, blk], out_specs=(blk,) * 4,
                 sem=("parallel",))(w, recv, m, v)


P_LAT, P_CTX, P_FNW, P_FFNB, P_CONV, P_FFNW, P_MISC, P_ROWS = 0, 8, 16, 24, 32, 48, 72, 80


def _rows_of(v, nrows):
    flat = v.reshape(-1)
    return jnp.pad(flat, (0, nrows * D - flat.shape[0])).reshape(nrows, D)


def _by_columns(g):
    n, r, c = g.shape
    return jnp.transpose(g, (1, 0, 2)).reshape(r, n * c)


def _to_columns(a):
    r, nc = a.shape
    return jnp.transpose(a.reshape(r, NDEV, nc // NDEV), (1, 0, 2))


def kernel(x, c, ctx, c_ctx, w_mod, b_mod, w_in, q_norm_w, k_norm_w, conv_qkv_w, a_log, dt_bias, gdn_norm_w, w_pa, w_pd, w_out, w_up, ffn_conv_w, ffn_conv_b, w_down, final_norm_w, loss_target, m_c_ctx, m_w_mod, m_b_mod, m_w_in, m_q_norm_w, m_k_norm_w, m_conv_qkv_w, m_a_log, m_dt_bias, m_gdn_norm_w, m_w_pa, m_w_pd, m_w_out, m_w_up, m_ffn_conv_w, m_ffn_conv_b, m_w_down, m_final_norm_w, v_c_ctx, v_w_mod, v_b_mod, v_w_in, v_q_norm_w, v_k_norm_w, v_conv_qkv_w, v_a_log, v_dt_bias, v_gdn_norm_w, v_w_pa, v_w_pd, v_w_out, v_w_up, v_ffn_conv_w, v_ffn_conv_b, v_w_down, v_final_norm_w):
    _, _, _, me = _position()
    mcols = w_mod.shape[2]

    big = {"w_in": w_in[0], "w_pa": w_pa[0], "w_pd": w_pd[0], "w_out": w_out[0], "w_up": w_up[0], "w_down": w_down[0]}
    names = list(big)
    shards = {n: _cast_bf16(big[n], name="cast_" + n) for n in names}
    w_in_g, = _exchange([shards["w_in"]], name="gather_w_in", scatter=False)
    c_all, conv_g, ffnw_g = _exchange([c, conv_qkv_w[0], ffn_conv_w[0]], name="gather_small", scatter=False)
    w_in_full = _by_columns(w_in_g)
    w_in_pad = jnp.concatenate([w_in_full[:, :W_AQ], jnp.zeros((D, C_AQ - W_AQ), BF16), w_in_full[:, W_AQ:]], axis=1)

    c9 = jnp.concatenate([c_all.reshape(NDEV, D), jnp.pad(c_ctx[None], ((0, MODROWS - NDEV - 1), (0, 0)))], axis=0)
    b_loc = lax.dynamic_slice(b_mod, (0, me * mcols), (1, mcols))
    mod_all, = _exchange([_mod_fwd(c9, w_mod[0], b_loc)], name="gather_mod", scatter=False)
    mod_lat = lax.dynamic_index_in_dim(mod_all, me, axis=1, keepdims=False).reshape(6, D)
    mod_ctx = mod_all[:, NDEV, :].reshape(6, D)

    small = {"q_norm_w": q_norm_w, "k_norm_w": k_norm_w, "gdn_norm_w": gdn_norm_w, "a_log": a_log, "dt_bias": dt_bias,
             "conv_qkv_w": _by_columns(conv_g), "ffn_conv_w": _by_columns(ffnw_g), "ffn_conv_b": ffn_conv_b,
             "final_norm_w": final_norm_w[None]}
    loss_me, grad_x, g_in, recv, dmod_lat, dmod_ctx, gs = _local_step(x[0], ctx[0], loss_target[0], mod_lat, mod_ctx,
                                                                      w_in_pad, shards, small)

    g_in = jnp.concatenate([g_in[:, :W_AQ], g_in[:, C_AQ:]], axis=1)
    recv["w_in"], = _exchange([_to_columns(g_in).astype(BF16)], name="scatter_g_in", scatter=True)
    moments = {"w_in": (m_w_in, v_w_in), "w_pa": (m_w_pa, v_w_pa), "w_pd": (m_w_pd, v_w_pd),
               "w_out": (m_w_out, v_w_out), "w_up": (m_w_up, v_w_up), "w_down": (m_w_down, v_w_down)}
    res = {}
    for n in names:
        outs = _adamw_recv(big[n], recv[n], moments[n][0][0], moments[n][1][0], name="adamw_" + n)
        res[n] = tuple(t[None] for t in outs)

    misc = jnp.concatenate([gs["q_norm_w"][0], gs["k_norm_w"][0], gs["gdn_norm_w"][0], gs["a_log"], gs["dt_bias"],
                            loss_me[None]])
    pack = jnp.concatenate([_rows_of(dmod_lat, P_CTX - P_LAT), _rows_of(dmod_ctx, P_FNW - P_CTX),
                            _rows_of(gs["final_norm_w"], P_FFNB - P_FNW), _rows_of(gs["ffn_conv_b"], P_CONV - P_FFNB),
                            _rows_of(gs["conv_qkv_w"], P_FFNW - P_CONV), _rows_of(gs["ffn_conv_w"], P_MISC - P_FFNW),
                            _rows_of(misc, P_ROWS - P_MISC)], axis=0)
    pack_all, = _exchange([pack], name="gather_pack", scatter=False)
    tot = _sum_slots(pack_all, name="sum_pack")
    dall = jnp.concatenate([pack_all[:, P_LAT:P_LAT + 6, :].reshape(NDEV, 6 * D),
                            jnp.pad(tot[P_CTX:P_CTX + 6].reshape(1, 6 * D), ((0, MODROWS - NDEV - 1), (0, 0)))], axis=0)
    dmy = lax.dynamic_slice(dall, (0, me * mcols), (MODROWS, mcols))
    g_w_mod, g_b_mod, cpart = _mod_bwd(c9, dmy, dall, w_mod[0])
    cparts, = _exchange([cpart], name="gather_cctx", scatter=False)
    g_c_ctx = _cctx_finish(cparts, c_ctx[None])[0]

    nconv, nffn = 3 * GH * HD, 2 * DFF
    conv_tot = tot[P_CONV:P_FFNW].reshape(-1)[:3 * nconv].reshape(3, nconv)
    ffnw_tot = tot[P_FFNW:P_MISC].reshape(-1)[:3 * nffn].reshape(3, nffn)
    mrow = tot[P_MISC]
    grads = {
        "c_ctx": g_c_ctx, "w_mod": g_w_mod[None], "b_mod": g_b_mod,
        "q_norm_w": mrow[None, 0:HD], "k_norm_w": mrow[None, HD:2 * HD], "gdn_norm_w": mrow[None, 2 * HD:3 * HD],
        "conv_qkv_w": lax.dynamic_slice(conv_tot, (0, me * (nconv // NDEV)), (3, nconv // NDEV))[None],
        "a_log": mrow[3 * HD:3 * HD + 2 * GH].reshape(1, 2, GH),
        "dt_bias": mrow[3 * HD + 2 * GH:3 * HD + 4 * GH].reshape(1, 2, GH),
        "ffn_conv_w": lax.dynamic_slice(ffnw_tot, (0, me * (nffn // NDEV)), (3, nffn // NDEV))[None],
        "ffn_conv_b": tot[P_FFNB:P_CONV].reshape(-1)[:nffn][None],
        "final_norm_w": tot[P_FNW],
    }
    loss = mrow[3 * HD + 4 * GH]
    given = {"c_ctx": (c_ctx, m_c_ctx, v_c_ctx), "w_mod": (w_mod, m_w_mod, v_w_mod), "b_mod": (b_mod, m_b_mod, v_b_mod),
             "q_norm_w": (q_norm_w, m_q_norm_w, v_q_norm_w), "k_norm_w": (k_norm_w, m_k_norm_w, v_k_norm_w),
             "conv_qkv_w": (conv_qkv_w, m_conv_qkv_w, v_conv_qkv_w), "a_log": (a_log, m_a_log, v_a_log),
             "dt_bias": (dt_bias, m_dt_bias, v_dt_bias), "gdn_norm_w": (gdn_norm_w, m_gdn_norm_w, v_gdn_norm_w),
             "ffn_conv_w": (ffn_conv_w, m_ffn_conv_w, v_ffn_conv_w), "ffn_conv_b": (ffn_conv_b, m_ffn_conv_b, v_ffn_conv_b),
             "final_norm_w": (final_norm_w, m_final_norm_w, v_final_norm_w)}
    for n, (w, m, v) in given.items():
        res[n] = (grads[n],) + _adamw(w, grads[n], m, v, name="adamw_" + n)

    order = ["c_ctx", "w_mod", "b_mod", "w_in", "q_norm_w", "k_norm_w", "conv_qkv_w", "a_log", "dt_bias", "gdn_norm_w",
             "w_pa", "w_pd", "w_out", "w_up", "ffn_conv_w", "ffn_conv_b", "w_down", "final_norm_w"]
    return (loss, grad_x[None], *[res[n][0] for n in order], *[res[n][1] for n in order],
            *[res[n][2] for n in order], *[res[n][3] for n in order])
```

```python
import functools
import math

import jax
import jax.numpy as jnp
from jax import lax
from jax.experimental import pallas as pl
from jax.experimental.pallas import tpu as pltpu

F32 = jnp.float32
BF16 = jnp.bfloat16
HI = lax.Precision.HIGHEST
MESH = pl.DeviceIdType.MESH

NDEV = 8
D = 1024
HD = 128
AH, AKV, GRP = 8, 2, 4
GH = 8
CH = 64
DFF = 2816
GRID_W = 64
EPS = 1e-6
ROPE_THETA = 10000.0
C_KV, C_QKV, C_BL, C_AQ, C_Z, C_GATE, C_END = 0, 512, 3584, 4096, 5120, 6144, 8192
W_BL, W_AQ, W_END = 3584, 3616, 7712
LR, B1, B2, AEPS, WD, STEP = 0.001, 0.9, 0.999, 1e-08, 0.01, 10
VMEM_BIG = 56 * 1024 * 1024
INTRA_FWD_CHUNKS = 18
INTRA_BWD_CHUNKS = 12


def _call(body, *, name, out_shape, grid=None, in_specs=None, out_specs=None, scratch=(), sem=None,
          vmem=None, aliases=None):
    params = {}
    if sem is not None:
        params["dimension_semantics"] = sem
    if vmem is not None:
        params["vmem_limit_bytes"] = vmem
    kw = {}
    if grid is not None:
        kw["grid"] = grid
    if in_specs is not None:
        kw["in_specs"] = in_specs
    if out_specs is not None:
        kw["out_specs"] = out_specs
    if aliases:
        kw["input_output_aliases"] = aliases
    return pl.pallas_call(body, name=name, out_shape=out_shape, scratch_shapes=list(scratch),
                          compiler_params=pltpu.CompilerParams(**params), **kw)


def _call_carrying(body, exch, *, name, out_shape, grid, in_specs, out_specs, scratch=(), vmem=None):
    n, nin, nout, nscr = exch.n, len(in_specs), len(out_shape), len(scratch)

    def wrapped(*refs):
        ins, cins = refs[:nin], refs[nin:nin + n]
        outs, couts = refs[nin + n:nin + n + nout], refs[nin + n + nout:nin + 2 * n + nout]
        scr, sems = refs[nin + 2 * n + nout:nin + 2 * n + nout + nscr], refs[nin + 2 * n + nout + nscr:]
        ids = [pl.program_id(i) for i in range(len(grid))]
        first = functools.reduce(jnp.logical_and, [i == 0 for i in ids])
        last = functools.reduce(jnp.logical_and, [i == g - 1 for i, g in zip(ids, grid)])

        @pl.when(first)
        def _():
            exch.start(cins, couts, sems)

        body(*ins, *outs, *scr)

        @pl.when(last)
        def _():
            exch.finish(cins, couts, sems)

    params = {"dimension_semantics": ("arbitrary",) * len(grid)}
    if vmem is not None:
        params["vmem_limit_bytes"] = vmem
    fn = pl.pallas_call(wrapped, name=name, out_shape=tuple(out_shape) + exch.out_shape, grid=grid,
                        in_specs=list(in_specs) + [HBM] * n, out_specs=tuple(out_specs) + (HBM,) * n,
                        scratch_shapes=list(scratch) + exch.scratch, compiler_params=pltpu.CompilerParams(**params))

    def run(*args):
        res = fn(*args, *exch.arrs)
        return res[:nout], list(res[nout:])

    return run


def _sds(shape, dtype=F32):
    return jax.ShapeDtypeStruct(tuple(shape), dtype)


def _dot(a, b, ca, cb):
    return lax.dot_general(a.astype(BF16), b.astype(BF16), (((ca,), (cb,)), ((), ())),
                           preferred_element_type=F32)


@jax.custom_vjp
def _nn(a, b):
    return _dot(a, b, 1, 0)


@jax.custom_vjp
def _nt(a, b):
    return _dot(a, b, 1, 1)


@jax.custom_vjp
def _tn(a, b):
    return _dot(a, b, 0, 0)


_nn.defvjp(lambda a, b: (_nn(a, b), (a, b)), lambda r, g: (_nt(g, r[1]), _tn(r[0], g)))
_nt.defvjp(lambda a, b: (_nt(a, b), (a, b)), lambda r, g: (_nn(g, r[1]), _tn(g, r[0])))
_tn.defvjp(lambda a, b: (_tn(a, b), (a, b)), lambda r, g: (_nt(r[1], g), _nn(r[0], g)))


def _hdot(a, b):
    return jnp.dot(a, b, precision=HI, preferred_element_type=F32)


def _mdot(a, b):
    return jnp.dot(a, b, precision=lax.Precision.HIGH, preferred_element_type=F32)


def _maskdot(mask, a, cm):
    hi = a.astype(BF16)
    r = a - hi.astype(F32)
    mid = r.astype(BF16)
    lo = (r - mid.astype(F32)).astype(BF16)
    mb = mask.astype(BF16)
    dims = (((cm,), (0,)), ((), ()))
    return (lax.dot_general(mb, hi, dims, preferred_element_type=F32)
            + lax.dot_general(mb, mid, dims, preferred_element_type=F32)
            + lax.dot_general(mb, lo, dims, preferred_element_type=F32))


@jax.custom_vjp
def _mask_nn(mask, a):
    return _maskdot(mask, a, 1)


_mask_nn.defvjp(lambda mask, a: (_maskdot(mask, a, 1), mask),
                lambda mask, g: (jnp.zeros_like(mask), _maskdot(mask, g, 0)))


@jax.custom_vjp
def _saved_inverse(lmat, x):
    return x


def _saved_inverse_bwd(x, g):
    t = lax.dot_general(x, g, (((0,), (0,)), ((), ())), precision=lax.Precision.HIGH, preferred_element_type=F32)
    dl = lax.dot_general(t, x, (((1,), (1,)), ((), ())), precision=lax.Precision.HIGH, preferred_element_type=F32)
    return -dl, jnp.zeros_like(x)


_saved_inverse.defvjp(lambda lmat, x: (x, x), _saved_inverse_bwd)


def _row_ids(shape):
    return lax.broadcasted_iota(jnp.int32, shape, 0)


def _shift_rows(x, down, bounds):
    n = x.shape[0]
    rows = _row_ids(x.shape)
    y = pltpu.roll(x, 1 if down else n - 1, 0)
    edge = functools.reduce(jnp.logical_or, [rows == (s if down else e - 1) for s, e in bounds])
    return jnp.where(edge, 0.0, y)


def _make_shift(bounds):
    @jax.custom_vjp
    def down(x):
        return _shift_rows(x, True, bounds)

    @jax.custom_vjp
    def up(x):
        return _shift_rows(x, False, bounds)

    down.defvjp(lambda x: (down(x), None), lambda _, g: (up(g),))
    up.defvjp(lambda x: (up(x), None), lambda _, g: (down(g),))
    return down, up


@jax.custom_vjp
def _swap32(x):
    lane = lax.broadcasted_iota(jnp.int32, x.shape, x.ndim - 1)
    return jnp.where((lane % 64) < 32, pltpu.roll(x, HD - 32, x.ndim - 1), pltpu.roll(x, 32, x.ndim - 1))


_swap32.defvjp(lambda x: (_swap32(x), None), lambda _, g: (_swap32(g),))


def _rms(x):
    return x * lax.rsqrt(jnp.mean(x * x, axis=-1, keepdims=True) + EPS)


def _silu(x):
    return x * jax.nn.sigmoid(x)


def _mm(a, b, *, name, M, N, K, ta=False, tb=False, out_dtype=F32, bm=None, bn=None, bk=None,
        a_off=(0, 0), b_off=(0, 0)):
    bm, bn, bk = bm or M, bn or N, bk or K
    assert M % bm == 0 and N % bn == 0 and K % bk == 0, (name, M, N, K, bm, bn, bk)
    nk = K // bk
    ca, cb = (0 if ta else 1), (1 if tb else 0)

    def body(a_ref, b_ref, o_ref, *acc):
        r = _dot(a_ref[...], b_ref[...], ca, cb)
        if nk == 1:
            o_ref[...] = r.astype(out_dtype)
        else:
            acc_ref, = acc
            k = pl.program_id(2)

            @pl.when(k == 0)
            def _():
                acc_ref[...] = r

            @pl.when(k > 0)
            def _():
                acc_ref[...] += r

            @pl.when(k == nk - 1)
            def _():
                o_ref[...] = acc_ref[...].astype(out_dtype)

    def blk(off, bshape):
        assert off[0] % bshape[0] == 0 and off[1] % bshape[1] == 0, (name, off, bshape)
        return off[0] // bshape[0], off[1] // bshape[1]

    if ta:
        ao = blk(a_off, (bk, bm))
        a_spec = pl.BlockSpec((bk, bm), lambda i, j, k: (k + ao[0], i + ao[1]))
    else:
        ao = blk(a_off, (bm, bk))
        a_spec = pl.BlockSpec((bm, bk), lambda i, j, k: (i + ao[0], k + ao[1]))
    if tb:
        bo = blk(b_off, (bn, bk))
        b_spec = pl.BlockSpec((bn, bk), lambda i, j, k: (j + bo[0], k + bo[1]))
    else:
        bo = blk(b_off, (bk, bn))
        b_spec = pl.BlockSpec((bk, bn), lambda i, j, k: (k + bo[0], j + bo[1]))
    return _call(body, name=name, out_shape=_sds((M, N), out_dtype), grid=(M // bm, N // bn, nk),
                 in_specs=[a_spec, b_spec], out_specs=pl.BlockSpec((bm, bn), lambda i, j, k: (i, j)),
                 scratch=[pltpu.VMEM((bm, bn), F32)] if nk > 1 else [],
                 sem=("parallel", "parallel", "arbitrary"), vmem=VMEM_BIG)(a, b)


def _normmod_fn(x, sh, sc):
    return _rms(x) * (1.0 + sc) + sh


def _normmod_fwd(x, mod, i_sh, i_sc, *, name, br=256):
    R = x.shape[0]

    def body(x_ref, mod_ref, o_ref):
        o_ref[...] = _normmod_fn(x_ref[...], mod_ref[i_sh:i_sh + 1, :], mod_ref[i_sc:i_sc + 1, :]).astype(BF16)

    return _call(body, name=name, out_shape=_sds((R, D), BF16), grid=(R // br,),
                 in_specs=[pl.BlockSpec((br, D), lambda i: (i, 0)), pl.BlockSpec((6, D), lambda i: (0, 0))],
                 out_specs=pl.BlockSpec((br, D), lambda i: (i, 0)), sem=("parallel",))(x, mod)


def _normmod_bwd(x, mod, i_sh, i_sc, dh, dh_off, res, *, name, br=256):
    R = x.shape[0]
    ob = dh_off // br
    has_res = res is not None

    def body(x_ref, mod_ref, dh_ref, *rest):
        if has_res:
            res_ref, dx_ref, dsh_ref, dsc_ref = rest
        else:
            dx_ref, dsh_ref, dsc_ref = rest
        sh, sc = mod_ref[i_sh:i_sh + 1, :], mod_ref[i_sc:i_sc + 1, :]
        _, vjp = jax.vjp(_normmod_fn, x_ref[...], sh, sc)
        dx, dsh, dsc = vjp(dh_ref[...])
        dx_ref[...] = dx + res_ref[...] if has_res else dx

        @pl.when(pl.program_id(0) == 0)
        def _():
            dsh_ref[...] = jnp.zeros_like(dsh_ref)
            dsc_ref[...] = jnp.zeros_like(dsc_ref)

        dsh_ref[...] += dsh
        dsc_ref[...] += dsc

    row = pl.BlockSpec((br, D), lambda i: (i, 0))
    vec = pl.BlockSpec((1, D), lambda i: (0, 0))
    ins = [row, pl.BlockSpec((6, D), lambda i: (0, 0)), pl.BlockSpec((br, D), lambda i: (i + ob, 0))]
    args = [x, mod, dh]
    if has_res:
        ins.append(row)
        args.append(res)
    return _call(body, name=name, out_shape=(_sds((R, D)), _sds((1, D)), _sds((1, D))), grid=(R // br,),
                 in_specs=ins, out_specs=(row, vec, vec), sem=("arbitrary",))(*args)


def _rope(x, cos, sin):
    return x * cos + _swap32(x) * sin


def _aprep_fn(qs, ks, cos, sin, qw, kw):
    return ([_rope(_rms(q) * qw, cos, sin) for q in qs], [_rope(_rms(k) * kw, cos, sin) for k in ks])


def _aprep_fwd(proj, cos, sin, qw, kw, *, br=256):
    T = proj.shape[0]

    def body(aq_ref, kv_ref, cos_ref, sin_ref, qw_ref, kw_ref, q_ref, k_ref, v_ref):
        qs = [aq_ref[:, h * HD:(h + 1) * HD] for h in range(AH)]
        ks = [kv_ref[:, h * HD:(h + 1) * HD] for h in range(AKV)]
        qo, ko = _aprep_fn(qs, ks, cos_ref[...], sin_ref[...], qw_ref[...], kw_ref[...])
        for h in range(AH):
            q_ref[h] = qo[h].astype(BF16)
        for h in range(AKV):
            k_ref[h] = ko[h].astype(BF16)
            v_ref[h] = kv_ref[:, (AKV + h) * HD:(AKV + h + 1) * HD].astype(BF16)

    tab = pl.BlockSpec((br, HD), lambda i: (i, 0))
    vec = pl.BlockSpec((1, HD), lambda i: (0, 0))
    return _call(body, name="aprep_fwd",
                 out_shape=(_sds((AH, T, HD), BF16), _sds((AKV, T, HD), BF16), _sds((AKV, T, HD), BF16)),
                 grid=(T // br,),
                 in_specs=[pl.BlockSpec((br, AH * HD), lambda i: (i, C_AQ // (AH * HD))),
                           pl.BlockSpec((br, 2 * AKV * HD), lambda i: (i, 0)), tab, tab, vec, vec],
                 out_specs=(pl.BlockSpec((AH, br, HD), lambda i: (0, i, 0)),
                            pl.BlockSpec((AKV, br, HD), lambda i: (0, i, 0)),
                            pl.BlockSpec((AKV, br, HD), lambda i: (0, i, 0))),
                 sem=("parallel",))(proj, proj, cos, sin, qw, kw)


def _aprep_bwd(proj, cos, sin, qw, kw, dq, dk, dv, L, *, br=256):
    T = proj.shape[0]
    lb = L // br

    def body(aq_ref, kv_ref, cos_ref, sin_ref, qw_ref, kw_ref, dq_ref, dk_ref, dv_ref,
             daq_ref, dkv_ref, dqw_ref, dkw_ref):
        i = pl.program_id(0)
        qs = [aq_ref[:, h * HD:(h + 1) * HD] for h in range(AH)]
        ks = [kv_ref[:, h * HD:(h + 1) * HD] for h in range(AKV)]
        _, vjp = jax.vjp(_aprep_fn, qs, ks, cos_ref[...], sin_ref[...], qw_ref[...], kw_ref[...])
        is_lat = i >= lb
        dqs = [jnp.where(is_lat, dq_ref[h], 0.0) for h in range(AH)]
        dks = [dk_ref[h] for h in range(AKV)]
        gq, gk, _, _, gqw, gkw = vjp((dqs, dks))
        for h in range(AH):
            daq_ref[:, h * HD:(h + 1) * HD] = gq[h].astype(BF16)
        for h in range(AKV):
            dkv_ref[:, h * HD:(h + 1) * HD] = gk[h].astype(BF16)
            dkv_ref[:, (AKV + h) * HD:(AKV + h + 1) * HD] = dv_ref[h].astype(BF16)

        @pl.when(i == 0)
        def _():
            dqw_ref[...] = jnp.zeros_like(dqw_ref)
            dkw_ref[...] = jnp.zeros_like(dkw_ref)

        dqw_ref[...] += gqw
        dkw_ref[...] += gkw

    tab = pl.BlockSpec((br, HD), lambda i: (i, 0))
    vec = pl.BlockSpec((1, HD), lambda i: (0, 0))
    kvb = pl.BlockSpec((AKV, br, HD), lambda i: (0, i, 0))
    return _call(body, name="aprep_bwd",
                 out_shape=(_sds((T, AH * HD), BF16), _sds((T, 2 * AKV * HD), BF16), _sds((1, HD)), _sds((1, HD))),
                 grid=(T // br,),
                 in_specs=[pl.BlockSpec((br, AH * HD), lambda i: (i, C_AQ // (AH * HD))),
                           pl.BlockSpec((br, 2 * AKV * HD), lambda i: (i, 0)), tab, tab, vec, vec,
                           pl.BlockSpec((AH, br, HD), lambda i: (0, jnp.maximum(i - lb, 0), 0)), kvb, kvb],
                 out_specs=(pl.BlockSpec((br, AH * HD), lambda i: (i, 0)),
                            pl.BlockSpec((br, 2 * AKV * HD), lambda i: (i, 0)), vec, vec),
                 sem=("arbitrary",))(proj, proj, cos, sin, qw, kw, dq, dk, dv)


def _attn_fn(q, k, v):
    s = _nt(q, k) * (HD ** -0.5)
    m = lax.stop_gradient(jnp.max(s, axis=-1, keepdims=True))
    e = jnp.exp(s - m)
    p = e / jnp.sum(e, axis=-1, keepdims=True)
    return _nn(p, v)


def _attn_fwd(q, k, v, L, exch, *, bq=128):
    T = q.shape[1]
    N = T - L
    lb = L // bq

    def body(q_ref, k_ref, v_ref, o_ref):
        qv = q_ref[...].reshape(GRP * bq, HD).astype(F32)
        o = _attn_fn(qv, k_ref[...].astype(F32), v_ref[...].astype(F32))
        for g in range(GRP):
            o_ref[:, g * HD:(g + 1) * HD] = o[g * bq:(g + 1) * bq].astype(BF16)

    kvb = pl.BlockSpec((None, T, HD), lambda g, i: (g, 0, 0))
    (attn,), moved = _call_carrying(
        body, exch, name="attn_fwd", out_shape=(_sds((N, AH * HD), BF16),), grid=(AKV, N // bq),
        in_specs=[pl.BlockSpec((GRP, bq, HD), lambda g, i: (g, i + lb, 0)), kvb, kvb],
        out_specs=(pl.BlockSpec((bq, GRP * HD), lambda g, i: (i, g)),), vmem=VMEM_BIG)(q, k, v)
    return attn, moved


def _attn_bwd(q, k, v, do, L, *, bq=128):
    T = q.shape[1]
    N = T - L
    lb = L // bq

    def body(q_ref, k_ref, v_ref, do_ref, dq_ref, dk_ref, dv_ref):
        qv = q_ref[...].reshape(GRP * bq, HD).astype(F32)
        _, vjp = jax.vjp(_attn_fn, qv, k_ref[...].astype(F32), v_ref[...].astype(F32))
        dov = jnp.concatenate([do_ref[:, g * HD:(g + 1) * HD] for g in range(GRP)], axis=0)
        dq, dk, dv = vjp(dov)
        dq_ref[...] = dq.reshape(GRP, bq, HD)

        @pl.when(pl.program_id(1) == 0)
        def _():
            dk_ref[...] = jnp.zeros_like(dk_ref)
            dv_ref[...] = jnp.zeros_like(dv_ref)

        dk_ref[...] += dk
        dv_ref[...] += dv

    kvb = pl.BlockSpec((None, T, HD), lambda g, i: (g, 0, 0))
    return _call(body, name="attn_bwd",
                 out_shape=(_sds((AH, N, HD)), _sds((AKV, T, HD)), _sds((AKV, T, HD))), grid=(AKV, N // bq),
                 in_specs=[pl.BlockSpec((GRP, bq, HD), lambda g, i: (g, i + lb, 0)), kvb, kvb,
                           pl.BlockSpec((bq, GRP * HD), lambda g, i: (i, g))],
                 out_specs=(pl.BlockSpec((GRP, bq, HD), lambda g, i: (g, i, 0)), kvb, kvb),
                 sem=("parallel", "arbitrary"), vmem=VMEM_BIG)(q, k, v, do)


def _gprep_fn(kind, shifts, x, w):
    down, up = shifts
    y = down(x) * w[0:1, :] + x * w[1:2, :] + up(x) * w[2:3, :]
    a = _silu(y)
    if kind == 2:
        return a
    a = a * lax.rsqrt(jnp.sum(a * a, axis=-1, keepdims=True) + EPS)
    return a * (HD ** -0.5) if kind == 0 else a


def _gprep_fwd(proj, conv_w, kind, bounds):
    T = proj.shape[0]
    shifts = _make_shift(bounds)
    cb = C_QKV // HD + kind * GH

    def body(x_ref, w_ref, o_ref):
        o_ref[...] = _gprep_fn(kind, shifts, x_ref[...], w_ref[...])

    return _call(body, name=f"gprep_fwd{kind}", out_shape=_sds((GH, T, HD)), grid=(GH,),
                 in_specs=[pl.BlockSpec((T, HD), lambda h: (0, cb + h)),
                           pl.BlockSpec((3, HD), lambda h: (0, kind * GH + h))],
                 out_specs=pl.BlockSpec((None, T, HD), lambda h: (h, 0, 0)), sem=("parallel",))(proj, conv_w)


def _gprep_bwd(proj, conv_w, kind, bounds, dy):
    T = proj.shape[0]
    shifts = _make_shift(bounds)
    cb = C_QKV // HD + kind * GH

    def body(x_ref, w_ref, dy_ref, dx_ref, dw_ref):
        _, vjp = jax.vjp(functools.partial(_gprep_fn, kind, shifts), x_ref[...], w_ref[...])
        dx, dw = vjp(dy_ref[0] + dy_ref[1])
        dx_ref[...] = dx.astype(BF16)
        dw_ref[...] = dw

    return _call(body, name=f"gprep_bwd{kind}", out_shape=(_sds((T, GH * HD), BF16), _sds((3, GH * HD))), grid=(GH,),
                 in_specs=[pl.BlockSpec((T, HD), lambda h: (0, cb + h)),
                           pl.BlockSpec((3, HD), lambda h: (0, kind * GH + h)),
                           pl.BlockSpec((2, None, T, HD), lambda h: (0, h, 0, 0))],
                 out_specs=(pl.BlockSpec((T, HD), lambda h: (0, h)), pl.BlockSpec((3, HD), lambda h: (0, h))),
                 sem=("parallel",))(proj, conv_w, dy)


def _bl_fn(x, alog, dtb):
    lane = lax.broadcasted_iota(jnp.int32, x.shape, 1)
    beta = jax.nn.sigmoid(x)
    z = x + dtb
    sp = jnp.maximum(z, 0.0) + jnp.log1p(jnp.exp(-jnp.abs(z)))
    la = -jnp.exp(alog) * sp
    return jnp.where(lane < 2 * GH, beta, jnp.where(lane < 4 * GH, la, 0.0))


def _bl_fwd(proj, alog, dtb, *, br=256):
    T = proj.shape[0]

    def body(x_ref, a_ref, d_ref, o_ref):
        o_ref[...] = _bl_fn(x_ref[...], a_ref[...], d_ref[...])

    vec = pl.BlockSpec((1, HD), lambda i: (0, 0))
    return _call(body, name="bl_fwd", out_shape=_sds((T, HD)), grid=(T // br,),
                 in_specs=[pl.BlockSpec((br, HD), lambda i: (i, C_BL // HD)), vec, vec],
                 out_specs=pl.BlockSpec((br, HD), lambda i: (i, 0)), sem=("parallel",))(proj, alog, dtb)


def _bl_bwd(proj, alog, dtb, dbl, *, br=256):
    T = proj.shape[0]

    def body(x_ref, a_ref, d_ref, g_ref, dx_ref, da_ref, dd_ref):
        g = g_ref[0, 0]
        for d in range(2):
            for h in range(GH):
                if d or h:
                    g = g + g_ref[d, h]
        _, vjp = jax.vjp(_bl_fn, x_ref[...], a_ref[...], d_ref[...])
        dx, da, dd = vjp(g)
        dx_ref[...] = dx.astype(BF16)

        @pl.when(pl.program_id(0) == 0)
        def _():
            da_ref[...] = jnp.zeros_like(da_ref)
            dd_ref[...] = jnp.zeros_like(dd_ref)

        da_ref[...] += da
        dd_ref[...] += dd

    vec = pl.BlockSpec((1, HD), lambda i: (0, 0))
    return _call(body, name="bl_bwd", out_shape=(_sds((T, HD), BF16), _sds((1, HD)), _sds((1, HD))), grid=(T // br,),
                 in_specs=[pl.BlockSpec((br, HD), lambda i: (i, C_BL // HD)), vec, vec,
                           pl.BlockSpec((2, GH, br, HD), lambda i: (0, 0, i, 0))],
                 out_specs=(pl.BlockSpec((br, HD), lambda i: (i, 0)), vec, vec), sem=("arbitrary",))(proj, alog, dtb, dbl)


def _chunk_masks(d):
    ii = lax.broadcasted_iota(jnp.int32, (CH, CH), 0)
    jj = lax.broadcasted_iota(jnp.int32, (CH, CH), 1)
    eye = (ii == jj).astype(F32)
    before = jnp.where(d == 0, (jj < ii).astype(F32), (jj > ii).astype(F32))
    return before, before + eye, eye


def _intra_fn(masks, sel_b, sel_l, qs, ks, vs, bls, xs=None):
    before, ateq, eye = masks
    ones = jnp.ones((CH, CH), F32)
    inc = ateq > 0.0
    each = lambda f, *ls: [f(*t) for t in zip(*ls)]
    beta = each(lambda bl: jnp.sum(bl * sel_b, axis=-1, keepdims=True), bls)
    la = each(lambda bl: jnp.sum(bl * sel_l, axis=-1, keepdims=True), bls)
    gam = each(lambda a: _mask_nn(ateq, jnp.broadcast_to(a, (CH, HD))), la)
    gi = each(lambda a: _mask_nn(ateq, jnp.broadcast_to(a, (CH, CH))), la)
    gj = each(lambda g: _mask_nn(ones, eye * g), gi)
    kk = each(lambda k: _nt(k, k), ks)
    qk = each(_nt, qs, ks)
    dec = each(lambda a, b: jnp.where(inc, jnp.exp(jnp.where(inc, a - b, 0.0)), 0.0), gi, gj)
    lmat = each(lambda b, d, m: before * (b * d * m), beta, dec, kk)
    if xs is None:
        x = each(lambda m: eye - m, lmat)
        p2 = each(lambda m: _mdot(m, m), lmat)
        for it in range(5):
            x = each(lambda a, b: a + _mdot(a, b), x, p2)
            if it < 4:
                p2 = each(lambda m: _mdot(m, m), p2)
    else:
        x = each(_saved_inverse, lmat, xs)
    eg = each(jnp.exp, gam)
    u = each(lambda a, b, v: _mdot(a, b * v), x, beta, vs)
    w = each(lambda a, b, e, k: _mdot(a, (b * e) * k), x, beta, eg, ks)
    tot = each(lambda a: jnp.sum(a, axis=0, keepdims=True), la)
    kd = each(lambda k, t, g: k * jnp.exp(t - g), ks, tot, gam)
    gl = each(lambda t: jnp.broadcast_to(jnp.exp(t), (1, HD)), tot)
    qd = each(lambda q, e: q * e, qs, eg)
    p = each(lambda d, m: d * m, dec, qk)
    return (u, w, kd, qd, p, gl, x) if xs is None else (u, w, kd, qd, p, gl)


def _dir_head_sel(d, h):
    lane = lax.broadcasted_iota(jnp.int32, (1, HD), 1)
    return (lane == d * GH + h).astype(F32), (lane == 2 * GH + d * GH + h).astype(F32)


def _intra_specs(T, G):
    nc = T // CH
    assert nc % G == 0
    qkv = pl.BlockSpec((None, G * CH, HD), lambda d, h, c: (h, c, 0))
    bl = pl.BlockSpec((G * CH, HD), lambda d, h, c: (c, 0))
    big = pl.BlockSpec((None, None, G * CH, HD), lambda d, h, c: (d, h, c, 0))
    pm = pl.BlockSpec((None, None, G * CH, CH), lambda d, h, c: (d, h, c, 0))
    gl = pl.BlockSpec((None, None, G, 1, HD), lambda d, h, c: (d, h, c, 0, 0))
    shapes = (_sds((2, GH, T, HD)),) + (_sds((2, GH, T, HD), BF16),) * 3 + (
        _sds((2, GH, T, CH), BF16), _sds((2, GH, nc, 1, HD)), _sds((2, GH, T, CH)))
    return nc, qkv, bl, big, pm, gl, shapes


def _chunks_per_step(T, most):
    nc = T // CH
    return max(g for g in range(1, most + 1) if nc % g == 0)


def _intra_fwd(q, k, v, bl, exch):
    T = q.shape[1]
    G = _chunks_per_step(T, INTRA_FWD_CHUNKS)
    nc, qkv_s, bl_s, big, pm, gl_s, shapes = _intra_specs(T, G)

    def body(q_ref, k_ref, v_ref, bl_ref, u_ref, w_ref, kd_ref, qd_ref, p_ref, gl_ref, x_ref):
        d, h = pl.program_id(0), pl.program_id(1)
        sb, sl = _dir_head_sel(d, h)
        rows = [slice(g * CH, (g + 1) * CH) for g in range(G)]
        outs = _intra_fn(_chunk_masks(d), sb, sl, *[[r[s, :] for s in rows] for r in (q_ref, k_ref, v_ref, bl_ref)])
        for g in range(G):
            for r, o in zip((u_ref, w_ref, kd_ref, qd_ref, p_ref, x_ref), outs[:5] + outs[6:]):
                r[rows[g], :] = o[g].astype(r.dtype)
            gl_ref[g] = outs[5][g]

    return _call_carrying(body, exch, name="gdn_intra_fwd", out_shape=shapes, grid=(2, GH, nc // G),
                          in_specs=[qkv_s, qkv_s, qkv_s, bl_s], out_specs=(big, big, big, big, pm, gl_s, pm))(q, k, v, bl)


def _intra_bwd(q, k, v, bl, xinv, cts, exch):
    T = q.shape[1]
    G = _chunks_per_step(T, INTRA_BWD_CHUNKS)
    nc, qkv_s, bl_s, big, pm, gl_s, _ = _intra_specs(T, G)

    def body(q_ref, k_ref, v_ref, bl_ref, x_ref, du, dw, dkd, dqd, dp, dgl, dq_ref, dk_ref, dv_ref, dbl_ref):
        d, h = pl.program_id(0), pl.program_id(1)
        sb, sl = _dir_head_sel(d, h)
        rows = [slice(g * CH, (g + 1) * CH) for g in range(G)]
        fn = functools.partial(_intra_fn, _chunk_masks(d), sb, sl, xs=[x_ref[s, :] for s in rows])
        _, vjp = jax.vjp(fn, *[[r[s, :] for s in rows] for r in (q_ref, k_ref, v_ref, bl_ref)])
        cts = tuple([r[s, :] for s in rows] for r in (du, dw, dkd, dqd, dp)) + ([dgl[g] for g in range(G)],)
        grads = vjp(cts)
        for g in range(G):
            for r, o in zip((dq_ref, dk_ref, dv_ref, dbl_ref), grads):
                r[rows[g], :] = o[g]

    return _call_carrying(body, exch, name="gdn_intra_bwd", out_shape=(_sds((2, GH, T, HD)),) * 4,
                          grid=(2, GH, nc // G), in_specs=[qkv_s, qkv_s, qkv_s, bl_s, pm, big, big, big, big, pm, gl_s],
                          out_specs=(big,) * 4)(q, k, v, bl, xinv, *cts)


def _scan_fn(s, u, w, kd, qd, p, gl):
    each = lambda f, *ls: [f(*t) for t in zip(*ls)]
    ws = each(_nn, w, s)
    delta = each(lambda a, b: a - b, u, ws)
    kdd = each(_tn, kd, delta)
    s_new = each(lambda g, a, b: g * a + b, gl, s, kdd)
    qs = each(_nn, qd, s)
    pd = each(_nn, p, delta)
    return each(lambda a, b: a + b, qs, pd), s_new


SCAN_BLOCK = 4


def _scan_visit(t, d, nb, ncb):
    rev = jnp.where(t < ncb, ncb - 1 - t, nb - 1 - (t - ncb))
    return jnp.where(d == 0, t, rev)


def _scan_specs(T, L, back):
    tb = SCAN_BLOCK * CH
    assert T % tb == 0 and L % tb == 0
    nb, ncb = T // tb, L // tb

    def at(d, t):
        return _scan_visit(nb - 1 - t if back else t, d, nb, ncb)

    big = pl.BlockSpec((None, GH, tb, HD), lambda d, t: (d, 0, at(d, t), 0))
    pm = pl.BlockSpec((None, GH, tb, CH), lambda d, t: (d, 0, at(d, t), 0))
    gl = pl.BlockSpec((None, GH, SCAN_BLOCK, 1, HD), lambda d, t: (d, 0, at(d, t), 0, 0))
    st = pl.BlockSpec((None, GH, SCAN_BLOCK, HD, HD), lambda d, t: (d, 0, at(d, t), 0, 0))
    do = pl.BlockSpec((GH, tb, HD), lambda d, t: (0, at(d, t), 0))
    return nb, big, pm, gl, st, do


def _scan_fwd(u, w, kd, qd, p, gl, L):
    T = u.shape[2]
    nb, big, pm, gl_s, st, _ = _scan_specs(T, L, False)
    heads = range(GH)

    def body(u_ref, w_ref, kd_ref, qd_ref, p_ref, gl_ref, o_ref, st_ref, s_scr):
        d = pl.program_id(0)

        @pl.when(pl.program_id(1) == 0)
        def _():
            s_scr[...] = jnp.zeros_like(s_scr)

        s = [s_scr[h] for h in heads]
        for i in range(SCAN_BLOCK):
            c = jnp.where(d == 0, i, SCAN_BLOCK - 1 - i)
            rows = pl.ds(pl.multiple_of(c * CH, CH), CH)
            for h in heads:
                st_ref[h, c] = s[h]
            o, s = _scan_fn(s, *[[r[h, rows, :].astype(F32) for h in heads] for r in (u_ref, w_ref, kd_ref, qd_ref, p_ref)],
                            [gl_ref[h, c] for h in heads])
            for h in heads:
                o_ref[h, rows, :] = o[h]
        for h in heads:
            s_scr[h] = s[h]

    return _call(body, name="gdn_scan_fwd", out_shape=(_sds((2, GH, T, HD)), _sds((2, GH, T // CH, HD, HD))),
                 grid=(2, nb), in_specs=[big, big, big, big, pm, gl_s], out_specs=(big, st),
                 scratch=[pltpu.VMEM((GH, HD, HD), F32)], sem=("parallel", "arbitrary"))(u, w, kd, qd, p, gl)


def _scan_bwd(u, w, kd, qd, p, gl, states, do, L):
    T = u.shape[2]
    nb, big, pm, gl_s, st, do_s = _scan_specs(T, L, True)
    heads = range(GH)

    def body(u_ref, w_ref, kd_ref, qd_ref, p_ref, gl_ref, st_ref, do_ref,
             du_ref, dw_ref, dkd_ref, dqd_ref, dp_ref, dgl_ref, ds_scr):
        d = pl.program_id(0)

        @pl.when(pl.program_id(1) == 0)
        def _():
            ds_scr[...] = jnp.zeros_like(ds_scr)

        ds = [ds_scr[h] for h in heads]
        for i in range(SCAN_BLOCK):
            c = jnp.where(d == 0, SCAN_BLOCK - 1 - i, i)
            rows = pl.ds(pl.multiple_of(c * CH, CH), CH)
            _, vjp = jax.vjp(_scan_fn, [st_ref[h, c] for h in heads],
                             *[[r[h, rows, :].astype(F32) for h in heads] for r in (u_ref, w_ref, kd_ref, qd_ref, p_ref)],
                             [gl_ref[h, c] for h in heads])
            ds, gu, gw, gkd, gqd, gp, ggl = vjp(([do_ref[h, rows, :] for h in heads], ds))
            for h in heads:
                du_ref[h, rows, :] = gu[h]
                dw_ref[h, rows, :] = gw[h]
                dkd_ref[h, rows, :] = gkd[h]
                dqd_ref[h, rows, :] = gqd[h]
                dp_ref[h, rows, :] = gp[h]
                dgl_ref[h, c] = ggl[h]
        for h in heads:
            ds_scr[h] = ds[h]

    return _call(body, name="gdn_scan_bwd",
                 out_shape=(_sds((2, GH, T, HD)),) * 4 + (_sds((2, GH, T, CH)), _sds((2, GH, T // CH, 1, HD))),
                 grid=(2, nb), in_specs=[big, big, big, big, pm, gl_s, st, do_s],
                 out_specs=(big, big, big, big, pm, gl_s), scratch=[pltpu.VMEM((GH, HD, HD), F32)],
                 sem=("parallel", "arbitrary"))(u, w, kd, qd, p, gl, states, do)


def _gout_fn(o0, o1, z, gw):
    return _rms(o0 + o1) * gw * _silu(z)


def _gout_fwd(o, proj, gw, L, *, br=256):
    T = o.shape[2]
    N = T - L
    lb = L // br
    ob = pl.BlockSpec((None, None, br, HD), lambda i, h: (0, h, i + lb, 0))
    ob1 = pl.BlockSpec((None, None, br, HD), lambda i, h: (1, h, i + lb, 0))

    def body(o0_ref, o1_ref, z_ref, gw_ref, y_ref):
        y_ref[...] = _gout_fn(o0_ref[...], o1_ref[...], z_ref[...], gw_ref[...]).astype(BF16)

    return _call(body, name="gout_fwd", out_shape=_sds((N, GH * HD), BF16), grid=(N // br, GH),
                 in_specs=[ob, ob1, pl.BlockSpec((br, HD), lambda i, h: (i + lb, C_Z // HD + h)),
                           pl.BlockSpec((1, HD), lambda i, h: (0, 0))],
                 out_specs=pl.BlockSpec((br, HD), lambda i, h: (i, h)), sem=("parallel", "parallel"))(o, o, proj, gw)


def _gout_bwd(o, proj, gw, dy, L, *, br=256):
    T = o.shape[2]
    lb = L // br
    ob = pl.BlockSpec((None, None, br, HD), lambda i, h: (0, h, i, 0))
    ob1 = pl.BlockSpec((None, None, br, HD), lambda i, h: (1, h, i, 0))

    def body(o0_ref, o1_ref, z_ref, gw_ref, dy_ref, do_ref, dz_ref, dgw_ref):
        i, h = pl.program_id(0), pl.program_id(1)
        _, vjp = jax.vjp(_gout_fn, o0_ref[...], o1_ref[...], z_ref[...], gw_ref[...])
        g0, _, gz, ggw = vjp(dy_ref[...])
        lat = i >= lb
        do_ref[...] = jnp.where(lat, g0, 0.0)
        dz_ref[...] = jnp.where(lat, gz, 0.0).astype(BF16)

        @pl.when(jnp.logical_and(i == 0, h == 0))
        def _():
            dgw_ref[...] = jnp.zeros_like(dgw_ref)

        dgw_ref[...] += jnp.where(lat, ggw, 0.0)

    return _call(body, name="gout_bwd", out_shape=(_sds((GH, T, HD)), _sds((T, GH * HD), BF16), _sds((1, HD))),
                 grid=(T // br, GH),
                 in_specs=[ob, ob1, pl.BlockSpec((br, HD), lambda i, h: (i, C_Z // HD + h)),
                           pl.BlockSpec((1, HD), lambda i, h: (0, 0)),
                           pl.BlockSpec((br, HD), lambda i, h: (jnp.maximum(i - lb, 0), h))],
                 out_specs=(pl.BlockSpec((None, br, HD), lambda i, h: (h, i, 0)),
                            pl.BlockSpec((br, HD), lambda i, h: (i, h)),
                            pl.BlockSpec((1, HD), lambda i, h: (0, 0))),
                 sem=("arbitrary", "arbitrary"))(o, o, proj, gw, dy)


def _merge_fn(pa, pd, ga, gd):
    return jax.nn.sigmoid(ga) * pa + jax.nn.sigmoid(gd) * pd


def _merge_fwd(pa, pd, proj, L, *, br=256):
    N = pa.shape[0]
    lb = L // br
    row = pl.BlockSpec((br, D), lambda i: (i, 0))

    def body(pa_ref, pd_ref, ga_ref, gd_ref, y_ref):
        y_ref[...] = _merge_fn(pa_ref[...], pd_ref[...], ga_ref[...], gd_ref[...]).astype(BF16)

    return _call(body, name="merge_fwd", out_shape=_sds((N, D), BF16), grid=(N // br,),
                 in_specs=[row, row, pl.BlockSpec((br, D), lambda i: (i + lb, C_GATE // D)),
                           pl.BlockSpec((br, D), lambda i: (i + lb, C_GATE // D + 1))],
                 out_specs=row, sem=("parallel",))(pa, pd, proj, proj)


def _merge_bwd(pa, pd, proj, dy, L, *, br=256):
    N = pa.shape[0]
    T = N + L
    lb = L // br
    lrow = pl.BlockSpec((br, D), lambda i: (jnp.maximum(i - lb, 0), 0))

    def body(pa_ref, pd_ref, ga_ref, gd_ref, dy_ref, dpa_ref, dpd_ref, dg_ref):
        lat = pl.program_id(0) >= lb
        _, vjp = jax.vjp(_merge_fn, pa_ref[...], pd_ref[...], ga_ref[...], gd_ref[...])
        gpa, gpd, gga, ggd = vjp(dy_ref[...])
        dpa_ref[...] = gpa.astype(BF16)
        dpd_ref[...] = gpd.astype(BF16)
        dg_ref[:, :D] = jnp.where(lat, gga, 0.0).astype(BF16)
        dg_ref[:, D:] = jnp.where(lat, ggd, 0.0).astype(BF16)

    return _call(body, name="merge_bwd", out_shape=(_sds((N, D), BF16), _sds((N, D), BF16), _sds((T, 2 * D), BF16)),
                 grid=(T // br,),
                 in_specs=[lrow, lrow, pl.BlockSpec((br, D), lambda i: (i, C_GATE // D)),
                           pl.BlockSpec((br, D), lambda i: (i, C_GATE // D + 1)), lrow],
                 out_specs=(lrow, lrow, pl.BlockSpec((br, 2 * D), lambda i: (i, 0))),
                 sem=("arbitrary",))(pa, pd, proj, proj, dy)


def _resid_fwd(x, m, mod, i_g, *, name, br=256):
    R = x.shape[0]
    row = pl.BlockSpec((br, D), lambda i: (i, 0))

    def body(x_ref, m_ref, mod_ref, o_ref):
        o_ref[...] = x_ref[...] + mod_ref[i_g:i_g + 1, :] * m_ref[...]

    return _call(body, name=name, out_shape=_sds((R, D)), grid=(R // br,),
                 in_specs=[row, row, pl.BlockSpec((6, D), lambda i: (0, 0))], out_specs=row,
                 sem=("parallel",))(x, m, mod)


def _resid_bwd(dx, m, mod, i_g, *, name, br=256):
    R = dx.shape[0]
    row = pl.BlockSpec((br, D), lambda i: (i, 0))
    vec = pl.BlockSpec((1, D), lambda i: (0, 0))

    def body(dx_ref, m_ref, mod_ref, dm_ref, dg_ref):
        dxv = dx_ref[...]
        dm_ref[...] = (dxv * mod_ref[i_g:i_g + 1, :]).astype(BF16)

        @pl.when(pl.program_id(0) == 0)
        def _():
            dg_ref[...] = jnp.zeros_like(dg_ref)

        dg_ref[...] += jnp.sum(dxv * m_ref[...], axis=0, keepdims=True)

    return _call(body, name=name, out_shape=(_sds((R, D), BF16), _sds((1, D))), grid=(R // br,),
                 in_specs=[row, row, pl.BlockSpec((6, D), lambda i: (0, 0))], out_specs=(row, vec),
                 sem=("arbitrary",))(dx, m, mod)


def _ffn_fn(shifts, ug, uv, wg, wv, bg, bv):
    down, up = shifts

    def conv(x, w, b):
        return down(x) * w[0:1, :] + x * w[1:2, :] + up(x) * w[2:3, :] + b

    return _silu(conv(ug, wg, bg)) * conv(uv, wv, bv)


def _ffn_fwd(up, cw, cb, *, bw=256):
    N = up.shape[0]
    shifts = _make_shift(((0, N),))
    nb = DFF // bw

    def body(ug, uv, wg, wv, bg, bv, a_ref):
        a_ref[...] = _ffn_fn(shifts, ug[...], uv[...], wg[...], wv[...], bg[...], bv[...]).astype(BF16)

    def col(rows, off):
        return pl.BlockSpec((rows, bw), lambda j: (0, j + off))

    return _call(body, name="ffn_fwd", out_shape=_sds((N, DFF), BF16), grid=(nb,),
                 in_specs=[col(N, 0), col(N, nb), col(3, 0), col(3, nb), col(1, 0), col(1, nb)],
                 out_specs=col(N, 0), sem=("parallel",), vmem=VMEM_BIG)(up, up, cw, cw, cb, cb)


def _ffn_bwd(up, cw, cb, da, *, bw=256):
    N = up.shape[0]
    shifts = _make_shift(((0, N),))
    nb = DFF // bw

    def body(ug, uv, wg, wv, bg, bv, da_ref, dug, duv, dwg, dwv, dbg, dbv):
        _, vjp = jax.vjp(functools.partial(_ffn_fn, shifts), ug[...], uv[...], wg[...], wv[...], bg[...], bv[...])
        g = vjp(da_ref[...])
        dug[...] = g[0].astype(BF16)
        duv[...] = g[1].astype(BF16)
        dwg[...], dwv[...], dbg[...], dbv[...] = g[2], g[3], g[4], g[5]

    def col(rows, off):
        return pl.BlockSpec((rows, bw), lambda j: (0, j + off))

    half = (_sds((N, DFF), BF16), _sds((N, DFF), BF16), _sds((3, DFF)), _sds((3, DFF)), _sds((1, DFF)), _sds((1, DFF)))
    dug, duv, dwg, dwv, dbg, dbv = _call(
        body, name="ffn_bwd", out_shape=half, grid=(nb,),
        in_specs=[col(N, 0), col(N, nb), col(3, 0), col(3, nb), col(1, 0), col(1, nb), col(N, 0)],
        out_specs=(col(N, 0), col(N, 0), col(3, 0), col(3, 0), col(1, 0), col(1, 0)),
        sem=("parallel",), vmem=VMEM_BIG)(up, up, cw, cw, cb, cb, da)
    return (jnp.concatenate([dug, duv], axis=1), jnp.concatenate([dwg, dwv], axis=1),
            jnp.concatenate([dbg, dbv], axis=1))


def _head_fn(x1, dn, g2, fw, tgt):
    y = _rms(x1 + g2 * dn) * fw
    err = y - tgt
    return 0.5 * jnp.sum(jnp.mean(err * err, axis=-1))


def _head(x1, dn, mod, fw, tgt, *, br=256):
    N = x1.shape[0]
    row = pl.BlockSpec((br, D), lambda i: (i, 0))
    vec = pl.BlockSpec((1, D), lambda i: (0, 0))
    one = pl.BlockSpec((1, HD), lambda i: (0, 0))

    def body(x1_ref, dn_ref, mod_ref, fw_ref, tgt_ref, loss_ref, dx_ref, ddn_ref, dg_ref, dfw_ref):
        loss, (gx, gdn, gg, gfw) = jax.value_and_grad(_head_fn, argnums=(0, 1, 2, 3))(
            x1_ref[...], dn_ref[...], mod_ref[5:6, :], fw_ref[...], tgt_ref[...])
        dx_ref[...] = gx
        ddn_ref[...] = gdn.astype(BF16)

        @pl.when(pl.program_id(0) == 0)
        def _():
            loss_ref[...] = jnp.zeros_like(loss_ref)
            dg_ref[...] = jnp.zeros_like(dg_ref)
            dfw_ref[...] = jnp.zeros_like(dfw_ref)

        loss_ref[...] += jnp.broadcast_to(loss, (1, HD))
        dg_ref[...] += gg
        dfw_ref[...] += gfw

    return _call(body, name="head", out_shape=(_sds((1, HD)), _sds((N, D)), _sds((N, D), BF16), _sds((1, D)), _sds((1, D))),
                 grid=(N // br,), in_specs=[row, row, pl.BlockSpec((6, D), lambda i: (0, 0)), vec, row],
                 out_specs=(one, row, row, vec, vec), sem=("arbitrary",))(x1, dn, mod, fw, tgt)


def _adamw(w, g, m, v, *, name):
    shape = w.shape
    cols = shape[-1]
    rows = max(1, math.prod(shape[:-1]))
    w2, g2, m2, v2 = (t.reshape(rows, cols) for t in (w, g, m, v))
    br = 256 if rows % 256 == 0 else (128 if rows % 128 == 0 else (8 if rows % 8 == 0 and rows > 64 else rows))
    if rows % 352 == 0:
        br = 352
    c1 = 1.0 - B1 ** STEP
    c2 = 1.0 - B2 ** STEP

    def body(w_ref, g_ref, m_ref, v_ref, d_ref, nm_ref, nv_ref):
        gv = g_ref[...]
        nm = B1 * m_ref[...] + (1.0 - B1) * gv
        nv = B2 * v_ref[...] + (1.0 - B2) * (gv * gv)
        d_ref[...] = -LR * ((nm / c1) / (jnp.sqrt(nv / c2) + AEPS) + WD * w_ref[...])
        nm_ref[...] = nm
        nv_ref[...] = nv

    blk = pl.BlockSpec((br, cols), lambda i: (i, 0))
    outs = _call(body, name=name, out_shape=(_sds((rows, cols)),) * 3, grid=(rows // br,),
                 in_specs=[blk] * 4, out_specs=(blk,) * 3, sem=("parallel",))(w2, g2, m2, v2)
    return tuple(t.reshape(shape) for t in outs)


def _rope_tables(N, L):
    t = jnp.arange(N)
    pos = jnp.stack([(t // GRID_W).astype(F32), (t % GRID_W).astype(F32)], axis=1)
    inv = ROPE_THETA ** (-jnp.arange(0, HD // 2, 2, dtype=F32) / (HD // 2))
    ang = pos[:, :, None] * inv[None, None, :]
    cos = jnp.broadcast_to(jnp.cos(ang)[:, :, None, :], (N, 2, 2, HD // 4)).reshape(N, HD)
    sin = jnp.broadcast_to(jnp.sin(ang)[:, :, None, :], (N, 2, 2, HD // 4))
    sin = (sin * jnp.array([-1.0, 1.0], F32)[None, None, :, None]).reshape(N, HD)
    cos = jnp.concatenate([jnp.ones((L, HD), F32), cos], axis=0)
    sin = jnp.concatenate([jnp.zeros((L, HD), F32), sin], axis=0)
    return cos, sin


def _pad_lanes(v, off=0):
    return jnp.zeros((1, HD), F32).at[0, off:off + v.shape[0]].set(v)


def _local_step(x, ctx, tgt, mod_lat, mod_ctx, w_in, shards, small):
    N, L = x.shape[0], ctx.shape[0]
    T = N + L
    bounds = ((0, L), (L, T))
    qw, kw, gw = small["q_norm_w"], small["k_norm_w"], small["gdn_norm_w"]
    conv_w, ffn_w, ffn_b, fnw = small["conv_qkv_w"], small["ffn_conv_w"], small["ffn_conv_b"], small["final_norm_w"]
    alog = _pad_lanes(small["a_log"].reshape(-1), 2 * GH)
    dtb = _pad_lanes(small["dt_bias"].reshape(-1), 2 * GH)
    cos, sin = _rope_tables(N, L)
    bt = 256 if T % 768 else 768
    bnl = 256 if N % 1024 else 1024

    hc = _normmod_fwd(ctx, mod_ctx, 0, 1, name="normmod_ctx")
    hx = _normmod_fwd(x, mod_lat, 0, 1, name="normmod_x")
    h1 = jnp.concatenate([hc, hx], axis=0)
    proj = _mm(h1, w_in, name="mm_in", M=T, N=C_END, K=D, bm=bt, bn=1024)
    aq, ak, av = _aprep_fwd(proj, cos, sin, qw, kw)
    attn, (up_g,) = _attn_fwd(aq, ak, av, L, _Exchange([shards["w_up"]], False))
    gq = _gprep_fwd(proj, conv_w, 0, bounds)
    gk = _gprep_fwd(proj, conv_w, 1, bounds)
    gv = _gprep_fwd(proj, conv_w, 2, bounds)
    bl = _bl_fwd(proj, alog, dtb)
    intra, (down_g, pa_g, pd_g, out_g) = _intra_fwd(
        gq, gk, gv, bl, _Exchange([shards[n] for n in ("w_down", "w_pa", "w_pd", "w_out")], False))
    w_up, w_down = _by_columns(up_g), down_g.reshape(DFF, D)
    w_pa, w_pd, w_out = pa_g.reshape(D, D), pd_g.reshape(D, D), out_g.reshape(D, D)
    xinv, intra = intra[6], intra[:6]
    o, states = _scan_fwd(*intra, L)
    gdn = _gout_fwd(o, proj, gw, L)
    pa = _mm(attn, w_pa, name="mm_pa", M=N, N=D, K=D, bm=bnl)
    pd = _mm(gdn, w_pd, name="mm_pd", M=N, N=D, K=D, bm=bnl)
    y = _merge_fwd(pa, pd, proj, L)
    m = _mm(y, w_out, name="mm_out", M=N, N=D, K=D, bm=bnl)
    x1 = _resid_fwd(x, m, mod_lat, 2, name="resid1")
    h2 = _normmod_fwd(x1, mod_lat, 3, 4, name="normmod_x1")
    up = _mm(h2, w_up, name="mm_up", M=N, N=2 * DFF, K=D, bm=bnl, bn=2 * DFF // 4)
    a = _ffn_fwd(up, ffn_w, ffn_b)
    dn = _mm(a, w_down, name="mm_down", M=N, N=D, K=DFF, bm=bnl)
    loss, dx2, ddn, dg2, dfnw = _head(x1, dn, mod_lat, fnw, tgt)

    da = _mm(ddn, w_down, name="mm_down_dx", M=N, N=DFF, K=D, tb=True, bm=bnl, bn=DFF // 2)
    g_down = _mm(a, ddn, name="mm_down_dw", M=DFF, N=D, K=N, ta=True, bm=DFF // 2)
    dup, d_ffn_w, d_ffn_b = _ffn_bwd(up, ffn_w, ffn_b, da)
    dh2 = _mm(dup, w_up, name="mm_up_dx", M=N, N=D, K=2 * DFF, tb=True, bm=bnl, bk=2 * DFF // 4)
    g_up = _mm(h2, dup, name="mm_up_dw", M=D, N=2 * DFF, K=N, ta=True, bn=2 * DFF // 4)
    dx1, dsh2, dsc2 = _normmod_bwd(x1, mod_lat, 3, 4, dh2, 0, dx2, name="normmod_x1_bwd")
    dm, dg1 = _resid_bwd(dx1, m, mod_lat, 2, name="resid1_bwd")
    dy = _mm(dm, w_out, name="mm_out_dx", M=N, N=D, K=D, tb=True, bm=bnl)
    g_out = _mm(y, dm, name="mm_out_dw", M=D, N=D, K=N, ta=True)
    dpa, dpd, dgate = _merge_bwd(pa, pd, proj, dy, L)
    dattn = _mm(dpa, w_pa, name="mm_pa_dx", M=N, N=D, K=D, tb=True, bm=bnl)
    g_pa = _mm(attn, dpa, name="mm_pa_dw", M=D, N=D, K=N, ta=True)
    dgdn = _mm(dpd, w_pd, name="mm_pd_dx", M=N, N=D, K=D, tb=True, bm=bnl)
    g_pd = _mm(gdn, dpd, name="mm_pd_dw", M=D, N=D, K=N, ta=True)
    do, dz, dgw = _gout_bwd(o, proj, gw, dgdn, L)
    cts = _scan_bwd(*intra, states, do, L)
    parts = [g_pa.reshape(NDEV, D // NDEV, D), g_pd.reshape(NDEV, D // NDEV, D), g_out.reshape(NDEV, D // NDEV, D),
             _to_columns(g_up), g_down.reshape(NDEV, DFF // NDEV, D)]
    (dgq, dgk, dgv, dbl), recv = _intra_bwd(gq, gk, gv, bl, xinv, cts,
                                            _Exchange([p.astype(BF16) for p in parts], True))
    dxq, dwq = _gprep_bwd(proj, conv_w, 0, bounds, dgq)
    dxk, dwk = _gprep_bwd(proj, conv_w, 1, bounds, dgk)
    dxv, dwv = _gprep_bwd(proj, conv_w, 2, bounds, dgv)
    dxbl, dalog, ddtb = _bl_bwd(proj, alog, dtb, dbl)
    daq_h, dak_h, dav_h = _attn_bwd(aq, ak, av, dattn, L)
    daq, dkv, dqw, dkw = _aprep_bwd(proj, cos, sin, qw, kw, daq_h, dak_h, dav_h, L)
    dproj = jnp.concatenate([dkv, dxq, dxk, dxv, dxbl, jnp.zeros((T, C_AQ - C_BL - HD), BF16), daq, dz, dgate], axis=1)
    dh1 = _mm(dproj, w_in, name="mm_in_dx", M=T, N=D, K=C_END, tb=True, bm=bt, bk=1024)
    g_in = _mm(h1, dproj, name="mm_in_dw", M=D, N=C_END, K=T, ta=True, bn=1024)
    grad_x, dsh1, dsc1 = _normmod_bwd(x, mod_lat, 0, 1, dh1, L, dx1, name="normmod_x_bwd")
    _, dcsh1, dcsc1 = _normmod_bwd(ctx, mod_ctx, 0, 1, dh1, 0, None, name="normmod_ctx_bwd")

    z1 = jnp.zeros((1, D), F32)
    dmod_lat = jnp.concatenate([dsh1, dsc1, dg1, dsh2, dsc2, dg2], axis=0)
    dmod_ctx = jnp.concatenate([dcsh1, dcsc1, z1, z1, z1, z1], axis=0)
    gsmall = {
        "q_norm_w": dqw, "k_norm_w": dkw, "gdn_norm_w": dgw,
        "conv_qkv_w": jnp.concatenate([dwq, dwk, dwv], axis=1),
        "a_log": dalog[0, 2 * GH:4 * GH], "dt_bias": ddtb[0, 2 * GH:4 * GH],
        "ffn_conv_w": d_ffn_w, "ffn_conv_b": d_ffn_b, "final_norm_w": dfnw,
    }
    return loss[0, 0], grad_x, g_in, dict(zip(("w_pa", "w_pd", "w_out", "w_up", "w_down"), recv)), dmod_lat, dmod_ctx, gsmall


HBM = pl.BlockSpec(memory_space=pltpu.HBM)


def _position():
    x, y, c = lax.axis_index("x"), lax.axis_index("y"), lax.axis_index("c")
    return x, y, c, 4 * x + 2 * y + c


def _peer(x, y, c, k):
    px = 1 - x if k & 4 else x
    py = 1 - y if k & 2 else y
    pc = 1 - c if k & 1 else c
    return (px, py, pc), 4 * px + 2 * py + pc


def _exchange(arrs, *, name, scatter):
    exch = _Exchange(arrs, scatter)
    n = exch.n

    def body(*refs):
        ins, outs, sems = refs[:n], refs[n:2 * n], refs[2 * n:]
        exch.start(ins, outs, sems)
        exch.finish(ins, outs, sems)

    outs = pl.pallas_call(body, name=name, out_shape=exch.out_shape, in_specs=[HBM] * n, out_specs=(HBM,) * n,
                          scratch_shapes=exch.scratch,
                          compiler_params=pltpu.CompilerParams(has_side_effects=True))(*arrs)
    return list(outs)


class _Exchange:
    def __init__(self, arrs, scatter):
        self.arrs, self.scatter, self.n = list(arrs), scatter, len(arrs)
        self.out_shape = tuple(_sds(a.shape if scatter else (NDEV,) + a.shape, a.dtype) for a in arrs)
        self.scratch = [pltpu.SemaphoreType.DMA((self.n, NDEV - 1)), pltpu.SemaphoreType.DMA((self.n, NDEV - 1)),
                        pltpu.SemaphoreType.DMA((self.n,))]

    def _copies(self, ins, outs, sems):
        send, recv, loc = sems
        x, y, c, me = _position()
        local = [pltpu.make_async_copy(ins[a].at[me] if self.scatter else ins[a], outs[a].at[me], loc.at[a])
                 for a in range(self.n)]
        remote = []
        for k in range(1, NDEV):
            peer, pid = _peer(x, y, c, k)
            for a in range(self.n):
                src = ins[a].at[pid] if self.scatter else ins[a]
                remote.append(pltpu.make_async_remote_copy(
                    src_ref=src, dst_ref=outs[a].at[me], send_sem=send.at[a, k - 1], recv_sem=recv.at[a, k - 1],
                    device_id=peer, device_id_type=MESH))
        return local, remote

    def start(self, ins, outs, sems):
        local, remote = self._copies(ins, outs, sems)
        for cp in local + remote:
            cp.start()

    def finish(self, ins, outs, sems):
        local, remote = self._copies(ins, outs, sems)
        for cp in remote:
            cp.wait()
        for cp in local:
            cp.wait()


def _cast_bf16(w, *, name):
    rows, cols = w.shape
    br = 128 if rows % 128 == 0 else rows

    def body(w_ref, o_ref):
        o_ref[...] = w_ref[...].astype(BF16)

    blk = pl.BlockSpec((br, cols), lambda i: (i, 0))
    return _call(body, name=name, out_shape=_sds((rows, cols), BF16), grid=(rows // br,), in_specs=[blk],
                 out_specs=blk, sem=("parallel",))(w)


def _sum_slots(a, *, name):
    _, R, C = a.shape

    def body(a_ref, o_ref):
        s = a_ref[0]
        for d in range(1, NDEV):
            s = s + a_ref[d]
        o_ref[...] = s

    return _call(body, name=name, out_shape=_sds((R, C)))(a)


MODROWS = 16


def _mod_fwd(c9, w, b):
    cols = w.shape[1]

    def body(c_ref, w_ref, b_ref, o_ref):
        o_ref[...] = _nn(_silu(c_ref[...]), w_ref[...]) + b_ref[...]

    return _call(body, name="mod_fwd", out_shape=_sds((MODROWS, cols)))(c9, w, b)


def _mod_bwd(c9, dmy, dall, w):
    cols = w.shape[1]

    def body(c_ref, dmy_ref, dall_ref, w_ref, gw_ref, gb_ref, cp_ref):
        sc = _silu(c_ref[...])
        rows = lax.broadcasted_iota(jnp.int32, (MODROWS, 1), 0)
        d = dmy_ref[...]
        d_ctx = jnp.where(rows == NDEV, d, 0.0)
        sc_ctx = jnp.where(rows == NDEV, sc, 0.0)
        outer = lax.dot_general(sc_ctx, d_ctx, (((0,), (0,)), ((), ())), precision=HI, preferred_element_type=F32)
        gw_ref[...] = _tn(jnp.where(rows < NDEV, sc, 0.0), jnp.where(rows < NDEV, d, 0.0)) + outer
        gb_ref[...] = jnp.sum(dall_ref[...], axis=0, keepdims=True)
        cp_ref[...] = jnp.sum(_nt(d_ctx, w_ref[...]), axis=0, keepdims=True)

    return _call(body, name="mod_bwd", out_shape=(_sds((D, cols)), _sds((1, 6 * D)), _sds((1, D))),
                 vmem=VMEM_BIG)(c9, dmy, dall, w)


def _cctx_finish(parts, c_ctx):
    def body(p_ref, c_ref, o_ref):
        s = p_ref[0]
        for d in range(1, NDEV):
            s = s + p_ref[d]
        _, vjp = jax.vjp(_silu, c_ref[...])
        o_ref[...] = vjp(s)[0]

    return _call(body, name="cctx_finish", out_shape=_sds((1, D)))(parts, c_ctx)


def _adamw_recv(w, recv, m, v, *, name):
    rows, cols = w.shape
    br = 128 if rows % 128 == 0 else rows
    c1 = 1.0 - B1 ** STEP
    c2 = 1.0 - B2 ** STEP

    def body(w_ref, r_ref, m_ref, v_ref, g_ref, d_ref, nm_ref, nv_ref):
        gv = r_ref[0].astype(F32)
        for d in range(1, NDEV):
            gv = gv + r_ref[d].astype(F32)
        nm = B1 * m_ref[...] + (1.0 - B1) * gv
        nv = B2 * v_ref[...] + (1.0 - B2) * (gv * gv)
        g_ref[...] = gv
        d_ref[...] = -LR * ((nm / c1) / (jnp.sqrt(nv / c2) + AEPS) + WD * w_ref[...])
        nm_ref[...] = nm
        nv_ref[...] = nv

    blk = pl.BlockSpec((br, cols), lambda i: (i, 0))
    return _call(body, name=name, out_shape=(_sds((rows, cols)),) * 4, grid=(rows // br,),
                 in_specs=[blk, pl.BlockSpec((NDEV, br, cols), lambda i: (0, i, 0)), blk, blk], out_specs=(blk,) * 4,
                 sem=("parallel",))(w, recv, m, v)


P_LAT, P_CTX, P_FNW, P_FFNB, P_CONV, P_FFNW, P_MISC, P_ROWS = 0, 8, 16, 24, 32, 48, 72, 80


def _rows_of(v, nrows):
    flat = v.reshape(-1)
    return jnp.pad(flat, (0, nrows * D - flat.shape[0])).reshape(nrows, D)


def _by_columns(g):
    n, r, c = g.shape
    return jnp.transpose(g, (1, 0, 2)).reshape(r, n * c)


def _to_columns(a):
    r, nc = a.shape
    return jnp.transpose(a.reshape(r, NDEV, nc // NDEV), (1, 0, 2))


def kernel(x, c, ctx, c_ctx, w_mod, b_mod, w_in, q_norm_w, k_norm_w, conv_qkv_w, a_log, dt_bias, gdn_norm_w, w_pa, w_pd, w_out, w_up, ffn_conv_w, ffn_conv_b, w_down, final_norm_w, loss_target, m_c_ctx, m_w_mod, m_b_mod, m_w_in, m_q_norm_w, m_k_norm_w, m_conv_qkv_w, m_a_log, m_dt_bias, m_gdn_norm_w, m_w_pa, m_w_pd, m_w_out, m_w_up, m_ffn_conv_w, m_ffn_conv_b, m_w_down, m_final_norm_w, v_c_ctx, v_w_mod, v_b_mod, v_w_in, v_q_norm_w, v_k_norm_w, v_conv_qkv_w, v_a_log, v_dt_bias, v_gdn_norm_w, v_w_pa, v_w_pd, v_w_out, v_w_up, v_ffn_conv_w, v_ffn_conv_b, v_w_down, v_final_norm_w):
    _, _, _, me = _position()
    mcols = w_mod.shape[2]

    big = {"w_in": w_in[0], "w_pa": w_pa[0], "w_pd": w_pd[0], "w_out": w_out[0], "w_up": w_up[0], "w_down": w_down[0]}
    names = list(big)
    shards = {n: _cast_bf16(big[n], name="cast_" + n) for n in names}
    w_in_g, = _exchange([shards["w_in"]], name="gather_w_in", scatter=False)
    c_all, conv_g, ffnw_g = _exchange([c, conv_qkv_w[0], ffn_conv_w[0]], name="gather_small", scatter=False)
    w_in_full = _by_columns(w_in_g)
    w_in_pad = jnp.concatenate([w_in_full[:, :W_AQ], jnp.zeros((D, C_AQ - W_AQ), BF16), w_in_full[:, W_AQ:]], axis=1)

    c9 = jnp.concatenate([c_all.reshape(NDEV, D), jnp.pad(c_ctx[None], ((0, MODROWS - NDEV - 1), (0, 0)))], axis=0)
    b_loc = lax.dynamic_slice(b_mod, (0, me * mcols), (1, mcols))
    mod_all, = _exchange([_mod_fwd(c9, w_mod[0], b_loc)], name="gather_mod", scatter=False)
    mod_lat = lax.dynamic_index_in_dim(mod_all, me, axis=1, keepdims=False).reshape(6, D)
    mod_ctx = mod_all[:, NDEV, :].reshape(6, D)

    small = {"q_norm_w": q_norm_w, "k_norm_w": k_norm_w, "gdn_norm_w": gdn_norm_w, "a_log": a_log, "dt_bias": dt_bias,
             "conv_qkv_w": _by_columns(conv_g), "ffn_conv_w": _by_columns(ffnw_g), "ffn_conv_b": ffn_conv_b,
             "final_norm_w": final_norm_w[None]}
    loss_me, grad_x, g_in, recv, dmod_lat, dmod_ctx, gs = _local_step(x[0], ctx[0], loss_target[0], mod_lat, mod_ctx,
                                                                      w_in_pad, shards, small)

    g_in = jnp.concatenate([g_in[:, :W_AQ], g_in[:, C_AQ:]], axis=1)
    recv["w_in"], = _exchange([_to_columns(g_in).astype(BF16)], name="scatter_g_in", scatter=True)
    moments = {"w_in": (m_w_in, v_w_in), "w_pa": (m_w_pa, v_w_pa), "w_pd": (m_w_pd, v_w_pd),
               "w_out": (m_w_out, v_w_out), "w_up": (m_w_up, v_w_up), "w_down": (m_w_down, v_w_down)}
    res = {}
    for n in names:
        outs = _adamw_recv(big[n], recv[n], moments[n][0][0], moments[n][1][0], name="adamw_" + n)
        res[n] = tuple(t[None] for t in outs)

    misc = jnp.concatenate([gs["q_norm_w"][0], gs["k_norm_w"][0], gs["gdn_norm_w"][0], gs["a_log"], gs["dt_bias"],
                            loss_me[None]])
    pack = jnp.concatenate([_rows_of(dmod_lat, P_CTX - P_LAT), _rows_of(dmod_ctx, P_FNW - P_CTX),
                            _rows_of(gs["final_norm_w"], P_FFNB - P_FNW), _rows_of(gs["ffn_conv_b"], P_CONV - P_FFNB),
                            _rows_of(gs["conv_qkv_w"], P_FFNW - P_CONV), _rows_of(gs["ffn_conv_w"], P_MISC - P_FFNW),
                            _rows_of(misc, P_ROWS - P_MISC)], axis=0)
    pack_all, = _exchange([pack], name="gather_pack", scatter=False)
    tot = _sum_slots(pack_all, name="sum_pack")
    dall = jnp.concatenate([pack_all[:, P_LAT:P_LAT + 6, :].reshape(NDEV, 6 * D),
                            jnp.pad(tot[P_CTX:P_CTX + 6].reshape(1, 6 * D), ((0, MODROWS - NDEV - 1), (0, 0)))], axis=0)
    dmy = lax.dynamic_slice(dall, (0, me * mcols), (MODROWS, mcols))
    g_w_mod, g_b_mod, cpart = _mod_bwd(c9, dmy, dall, w_mod[0])
    cparts, = _exchange([cpart], name="gather_cctx", scatter=False)
    g_c_ctx = _cctx_finish(cparts, c_ctx[None])[0]

    nconv, nffn = 3 * GH * HD, 2 * DFF
    conv_tot = tot[P_CONV:P_FFNW].reshape(-1)[:3 * nconv].reshape(3, nconv)
    ffnw_tot = tot[P_FFNW:P_MISC].reshape(-1)[:3 * nffn].reshape(3, nffn)
    mrow = tot[P_MISC]
    grads = {
        "c_ctx": g_c_ctx, "w_mod": g_w_mod[None], "b_mod": g_b_mod,
        "q_norm_w": mrow[None, 0:HD], "k_norm_w": mrow[None, HD:2 * HD], "gdn_norm_w": mrow[None, 2 * HD:3 * HD],
        "conv_qkv_w": lax.dynamic_slice(conv_tot, (0, me * (nconv // NDEV)), (3, nconv // NDEV))[None],
        "a_log": mrow[3 * HD:3 * HD + 2 * GH].reshape(1, 2, GH),
        "dt_bias": mrow[3 * HD + 2 * GH:3 * HD + 4 * GH].reshape(1, 2, GH),
        "ffn_conv_w": lax.dynamic_slice(ffnw_tot, (0, me * (nffn // NDEV)), (3, nffn // NDEV))[None],
        "ffn_conv_b": tot[P_FFNB:P_CONV].reshape(-1)[:nffn][None],
        "final_norm_w": tot[P_FNW],
    }
    loss = mrow[3 * HD + 4 * GH]
    given = {"c_ctx": (c_ctx, m_c_ctx, v_c_ctx), "w_mod": (w_mod, m_w_mod, v_w_mod), "b_mod": (b_mod, m_b_mod, v_b_mod),
             "q_norm_w": (q_norm_w, m_q_norm_w, v_q_norm_w), "k_norm_w": (k_norm_w, m_k_norm_w, v_k_norm_w),
             "conv_qkv_w": (conv_qkv_w, m_conv_qkv_w, v_conv_qkv_w), "a_log": (a_log, m_a_log, v_a_log),
             "dt_bias": (dt_bias, m_dt_bias, v_dt_bias), "gdn_norm_w": (gdn_norm_w, m_gdn_norm_w, v_gdn_norm_w),
             "ffn_conv_w": (ffn_conv_w, m_ffn_conv_w, v_ffn_conv_w), "ffn_conv_b": (ffn_conv_b, m_ffn_conv_b, v_ffn_conv_b),
             "final_norm_w": (final_norm_w, m_final_norm_w, v_final_norm_w)}
    for n, (w, m, v) in given.items():
        res[n] = (grads[n],) + _adamw(w, grads[n], m, v, name="adamw_" + n)

    order = ["c_ctx", "w_mod", "b_mod", "w_in", "q_norm_w", "k_norm_w", "conv_qkv_w", "a_log", "dt_bias", "gdn_norm_w",
             "w_pa", "w_pd", "w_out", "w_up", "ffn_conv_w", "ffn_conv_b", "w_down", "final_norm_w"]
    return (loss, grad_x[None], *[res[n][0] for n in order], *[res[n][1] for n in order],
            *[res[n][2] for n in order], *[res[n][3] for n in order])
```

```python
import functools
import math

import jax
import jax.numpy as jnp
from jax import lax
from jax.experimental import pallas as pl
from jax.experimental.pallas import tpu as pltpu

F32 = jnp.float32
BF16 = jnp.bfloat16
HI = lax.Precision.HIGHEST
MESH = pl.DeviceIdType.MESH

NDEV = 8
D = 1024
HD = 128
AH, AKV, GRP = 8, 2, 4
GH = 8
CH = 64
DFF = 2816
GRID_W = 64
EPS = 1e-6
ROPE_THETA = 10000.0
C_KV, C_QKV, C_BL, C_AQ, C_Z, C_GATE, C_END = 0, 512, 3584, 4096, 5120, 6144, 8192
W_BL, W_AQ, W_END = 3584, 3616, 7712
LR, B1, B2, AEPS, WD, STEP = 0.001, 0.9, 0.999, 1e-08, 0.01, 10
VMEM_BIG = 56 * 1024 * 1024
INTRA_FWD_CHUNKS = 18
INTRA_BWD_CHUNKS = 12


def _call(body, *, name, out_shape, grid=None, in_specs=None, out_specs=None, scratch=(), sem=None,
          vmem=None, aliases=None):
    params = {}
    if sem is not None:
        params["dimension_semantics"] = sem
    if vmem is not None:
        params["vmem_limit_bytes"] = vmem
    kw = {}
    if grid is not None:
        kw["grid"] = grid
    if in_specs is not None:
        kw["in_specs"] = in_specs
    if out_specs is not None:
        kw["out_specs"] = out_specs
    if aliases:
        kw["input_output_aliases"] = aliases
    return pl.pallas_call(body, name=name, out_shape=out_shape, scratch_shapes=list(scratch),
                          compiler_params=pltpu.CompilerParams(**params), **kw)


def _call_carrying(body, exch, *, name, out_shape, grid, in_specs, out_specs, scratch=(), vmem=None):
    n, nin, nout, nscr = exch.n, len(in_specs), len(out_shape), len(scratch)

    def wrapped(*refs):
        ins, cins = refs[:nin], refs[nin:nin + n]
        outs, couts = refs[nin + n:nin + n + nout], refs[nin + n + nout:nin + 2 * n + nout]
        scr, sems = refs[nin + 2 * n + nout:nin + 2 * n + nout + nscr], refs[nin + 2 * n + nout + nscr:]
        ids = [pl.program_id(i) for i in range(len(grid))]
        first = functools.reduce(jnp.logical_and, [i == 0 for i in ids])
        last = functools.reduce(jnp.logical_and, [i == g - 1 for i, g in zip(ids, grid)])

        @pl.when(first)
        def _():
            exch.start(cins, couts, sems)

        body(*ins, *outs, *scr)

        @pl.when(last)
        def _():
            exch.finish(cins, couts, sems)

    params = {"dimension_semantics": ("arbitrary",) * len(grid)}
    if vmem is not None:
        params["vmem_limit_bytes"] = vmem
    fn = pl.pallas_call(wrapped, name=name, out_shape=tuple(out_shape) + exch.out_shape, grid=grid,
                        in_specs=list(in_specs) + [HBM] * n, out_specs=tuple(out_specs) + (HBM,) * n,
                        scratch_shapes=list(scratch) + exch.scratch, compiler_params=pltpu.CompilerParams(**params))

    def run(*args):
        res = fn(*args, *exch.arrs)
        return res[:nout], list(res[nout:])

    return run


def _sds(shape, dtype=F32):
    return jax.ShapeDtypeStruct(tuple(shape), dtype)


def _dot(a, b, ca, cb):
    return lax.dot_general(a.astype(BF16), b.astype(BF16), (((ca,), (cb,)), ((), ())),
                           preferred_element_type=F32)


@jax.custom_vjp
def _nn(a, b):
    return _dot(a, b, 1, 0)


@jax.custom_vjp
def _nt(a, b):
    return _dot(a, b, 1, 1)


@jax.custom_vjp
def _tn(a, b):
    return _dot(a, b, 0, 0)


_nn.defvjp(lambda a, b: (_nn(a, b), (a, b)), lambda r, g: (_nt(g, r[1]), _tn(r[0], g)))
_nt.defvjp(lambda a, b: (_nt(a, b), (a, b)), lambda r, g: (_nn(g, r[1]), _tn(g, r[0])))
_tn.defvjp(lambda a, b: (_tn(a, b), (a, b)), lambda r, g: (_nt(r[1], g), _nn(r[0], g)))


def _hdot(a, b):
    return jnp.dot(a, b, precision=HI, preferred_element_type=F32)


def _mdot(a, b):
    return jnp.dot(a, b, precision=lax.Precision.HIGH, preferred_element_type=F32)


def _maskdot(mask, a, cm):
    hi = a.astype(BF16)
    r = a - hi.astype(F32)
    mid = r.astype(BF16)
    lo = (r - mid.astype(F32)).astype(BF16)
    mb = mask.astype(BF16)
    dims = (((cm,), (0,)), ((), ()))
    return (lax.dot_general(mb, hi, dims, preferred_element_type=F32)
            + lax.dot_general(mb, mid, dims, preferred_element_type=F32)
            + lax.dot_general(mb, lo, dims, preferred_element_type=F32))


@jax.custom_vjp
def _mask_nn(mask, a):
    return _maskdot(mask, a, 1)


_mask_nn.defvjp(lambda mask, a: (_maskdot(mask, a, 1), mask),
                lambda mask, g: (jnp.zeros_like(mask), _maskdot(mask, g, 0)))


@jax.custom_vjp
def _saved_inverse(lmat, x):
    return x


def _saved_inverse_bwd(x, g):
    t = lax.dot_general(x, g, (((0,), (0,)), ((), ())), precision=lax.Precision.HIGH, preferred_element_type=F32)
    dl = lax.dot_general(t, x, (((1,), (1,)), ((), ())), precision=lax.Precision.HIGH, preferred_element_type=F32)
    return -dl, jnp.zeros_like(x)


_saved_inverse.defvjp(lambda lmat, x: (x, x), _saved_inverse_bwd)


def _row_ids(shape):
    return lax.broadcasted_iota(jnp.int32, shape, 0)


def _shift_rows(x, down, bounds):
    n = x.shape[0]
    rows = _row_ids(x.shape)
    y = pltpu.roll(x, 1 if down else n - 1, 0)
    edge = functools.reduce(jnp.logical_or, [rows == (s if down else e - 1) for s, e in bounds])
    return jnp.where(edge, 0.0, y)


def _make_shift(bounds):
    @jax.custom_vjp
    def down(x):
        return _shift_rows(x, True, bounds)

    @jax.custom_vjp
    def up(x):
        return _shift_rows(x, False, bounds)

    down.defvjp(lambda x: (down(x), None), lambda _, g: (up(g),))
    up.defvjp(lambda x: (up(x), None), lambda _, g: (down(g),))
    return down, up


@jax.custom_vjp
def _swap32(x):
    lane = lax.broadcasted_iota(jnp.int32, x.shape, x.ndim - 1)
    return jnp.where((lane % 64) < 32, pltpu.roll(x, HD - 32, x.ndim - 1), pltpu.roll(x, 32, x.ndim - 1))


_swap32.defvjp(lambda x: (_swap32(x), None), lambda _, g: (_swap32(g),))


def _rms(x):
    return x * lax.rsqrt(jnp.mean(x * x, axis=-1, keepdims=True) + EPS)


def _silu(x):
    return x * jax.nn.sigmoid(x)


def _mm(a, b, *, name, M, N, K, ta=False, tb=False, out_dtype=F32, bm=None, bn=None, bk=None,
        a_off=(0, 0), b_off=(0, 0), after=()):
    bm, bn, bk = bm or M, bn or N, bk or K
    assert M % bm == 0 and N % bn == 0 and K % bk == 0, (name, M, N, K, bm, bn, bk)
    nk = K // bk
    ca, cb = (0 if ta else 1), (1 if tb else 0)
    na = len(after)

    def body(a_ref, b_ref, *rest):
        o_ref, acc = rest[na], rest[na + 1:]
        r = _dot(a_ref[...], b_ref[...], ca, cb)
        if nk == 1:
            o_ref[...] = r.astype(out_dtype)
        else:
            acc_ref, = acc
            k = pl.program_id(2)

            @pl.when(k == 0)
            def _():
                acc_ref[...] = r

            @pl.when(k > 0)
            def _():
                acc_ref[...] += r

            @pl.when(k == nk - 1)
            def _():
                o_ref[...] = acc_ref[...].astype(out_dtype)

    def blk(off, bshape):
        assert off[0] % bshape[0] == 0 and off[1] % bshape[1] == 0, (name, off, bshape)
        return off[0] // bshape[0], off[1] // bshape[1]

    if ta:
        ao = blk(a_off, (bk, bm))
        a_spec = pl.BlockSpec((bk, bm), lambda i, j, k: (k + ao[0], i + ao[1]))
    else:
        ao = blk(a_off, (bm, bk))
        a_spec = pl.BlockSpec((bm, bk), lambda i, j, k: (i + ao[0], k + ao[1]))
    if tb:
        bo = blk(b_off, (bn, bk))
        b_spec = pl.BlockSpec((bn, bk), lambda i, j, k: (j + bo[0], k + bo[1]))
    else:
        bo = blk(b_off, (bk, bn))
        b_spec = pl.BlockSpec((bk, bn), lambda i, j, k: (k + bo[0], j + bo[1]))
    return _call(body, name=name, out_shape=_sds((M, N), out_dtype), grid=(M // bm, N // bn, nk),
                 in_specs=[a_spec, b_spec] + [pl.BlockSpec(memory_space=pl.ANY)] * na,
                 out_specs=pl.BlockSpec((bm, bn), lambda i, j, k: (i, j)),
                 scratch=[pltpu.VMEM((bm, bn), F32)] if nk > 1 else [],
                 sem=("parallel", "parallel", "arbitrary"), vmem=VMEM_BIG)(a, b, *after)


def _normmod_fn(x, sh, sc):
    return _rms(x) * (1.0 + sc) + sh


def _normmod_fwd(x, mod, i_sh, i_sc, *, name, br=256):
    R = x.shape[0]

    def body(x_ref, mod_ref, o_ref):
        o_ref[...] = _normmod_fn(x_ref[...], mod_ref[i_sh:i_sh + 1, :], mod_ref[i_sc:i_sc + 1, :]).astype(BF16)

    return _call(body, name=name, out_shape=_sds((R, D), BF16), grid=(R // br,),
                 in_specs=[pl.BlockSpec((br, D), lambda i: (i, 0)), pl.BlockSpec((6, D), lambda i: (0, 0))],
                 out_specs=pl.BlockSpec((br, D), lambda i: (i, 0)), sem=("parallel",))(x, mod)


def _normmod_bwd(x, mod, i_sh, i_sc, dh, dh_off, res, *, name, br=256):
    R = x.shape[0]
    ob = dh_off // br
    has_res = res is not None

    def body(x_ref, mod_ref, dh_ref, *rest):
        if has_res:
            res_ref, dx_ref, dsh_ref, dsc_ref = rest
        else:
            dx_ref, dsh_ref, dsc_ref = rest
        sh, sc = mod_ref[i_sh:i_sh + 1, :], mod_ref[i_sc:i_sc + 1, :]
        _, vjp = jax.vjp(_normmod_fn, x_ref[...], sh, sc)
        dx, dsh, dsc = vjp(dh_ref[...])
        dx_ref[...] = dx + res_ref[...] if has_res else dx

        @pl.when(pl.program_id(0) == 0)
        def _():
            dsh_ref[...] = jnp.zeros_like(dsh_ref)
            dsc_ref[...] = jnp.zeros_like(dsc_ref)

        dsh_ref[...] += dsh
        dsc_ref[...] += dsc

    row = pl.BlockSpec((br, D), lambda i: (i, 0))
    vec = pl.BlockSpec((1, D), lambda i: (0, 0))
    ins = [row, pl.BlockSpec((6, D), lambda i: (0, 0)), pl.BlockSpec((br, D), lambda i: (i + ob, 0))]
    args = [x, mod, dh]
    if has_res:
        ins.append(row)
        args.append(res)
    return _call(body, name=name, out_shape=(_sds((R, D)), _sds((1, D)), _sds((1, D))), grid=(R // br,),
                 in_specs=ins, out_specs=(row, vec, vec), sem=("arbitrary",))(*args)


def _rope(x, cos, sin):
    return x * cos + _swap32(x) * sin


def _aprep_fn(qs, ks, cos, sin, qw, kw):
    return ([_rope(_rms(q) * qw, cos, sin) for q in qs], [_rope(_rms(k) * kw, cos, sin) for k in ks])


def _aprep_fwd(proj, cos, sin, qw, kw, *, br=256):
    T = proj.shape[0]

    def body(aq_ref, kv_ref, cos_ref, sin_ref, qw_ref, kw_ref, q_ref, k_ref, v_ref):
        qs = [aq_ref[:, h * HD:(h + 1) * HD] for h in range(AH)]
        ks = [kv_ref[:, h * HD:(h + 1) * HD] for h in range(AKV)]
        qo, ko = _aprep_fn(qs, ks, cos_ref[...], sin_ref[...], qw_ref[...], kw_ref[...])
        for h in range(AH):
            q_ref[h] = qo[h].astype(BF16)
        for h in range(AKV):
            k_ref[h] = ko[h].astype(BF16)
            v_ref[h] = kv_ref[:, (AKV + h) * HD:(AKV + h + 1) * HD].astype(BF16)

    tab = pl.BlockSpec((br, HD), lambda i: (i, 0))
    vec = pl.BlockSpec((1, HD), lambda i: (0, 0))
    return _call(body, name="aprep_fwd",
                 out_shape=(_sds((AH, T, HD), BF16), _sds((AKV, T, HD), BF16), _sds((AKV, T, HD), BF16)),
                 grid=(T // br,),
                 in_specs=[pl.BlockSpec((br, AH * HD), lambda i: (i, C_AQ // (AH * HD))),
                           pl.BlockSpec((br, 2 * AKV * HD), lambda i: (i, 0)), tab, tab, vec, vec],
                 out_specs=(pl.BlockSpec((AH, br, HD), lambda i: (0, i, 0)),
                            pl.BlockSpec((AKV, br, HD), lambda i: (0, i, 0)),
                            pl.BlockSpec((AKV, br, HD), lambda i: (0, i, 0))),
                 sem=("parallel",))(proj, proj, cos, sin, qw, kw)


def _aprep_bwd(proj, cos, sin, qw, kw, dq, dk, dv, L, *, br=256):
    T = proj.shape[0]
    lb = L // br

    def body(aq_ref, kv_ref, cos_ref, sin_ref, qw_ref, kw_ref, dq_ref, dk_ref, dv_ref,
             daq_ref, dkv_ref, dqw_ref, dkw_ref):
        i = pl.program_id(0)
        qs = [aq_ref[:, h * HD:(h + 1) * HD] for h in range(AH)]
        ks = [kv_ref[:, h * HD:(h + 1) * HD] for h in range(AKV)]
        _, vjp = jax.vjp(_aprep_fn, qs, ks, cos_ref[...], sin_ref[...], qw_ref[...], kw_ref[...])
        is_lat = i >= lb
        dqs = [jnp.where(is_lat, dq_ref[h], 0.0) for h in range(AH)]
        dks = [dk_ref[h] for h in range(AKV)]
        gq, gk, _, _, gqw, gkw = vjp((dqs, dks))
        for h in range(AH):
            daq_ref[:, h * HD:(h + 1) * HD] = gq[h].astype(BF16)
        for h in range(AKV):
            dkv_ref[:, h * HD:(h + 1) * HD] = gk[h].astype(BF16)
            dkv_ref[:, (AKV + h) * HD:(AKV + h + 1) * HD] = dv_ref[h].astype(BF16)

        @pl.when(i == 0)
        def _():
            dqw_ref[...] = jnp.zeros_like(dqw_ref)
            dkw_ref[...] = jnp.zeros_like(dkw_ref)

        dqw_ref[...] += gqw
        dkw_ref[...] += gkw

    tab = pl.BlockSpec((br, HD), lambda i: (i, 0))
    vec = pl.BlockSpec((1, HD), lambda i: (0, 0))
    kvb = pl.BlockSpec((AKV, br, HD), lambda i: (0, i, 0))
    return _call(body, name="aprep_bwd",
                 out_shape=(_sds((T, AH * HD), BF16), _sds((T, 2 * AKV * HD), BF16), _sds((1, HD)), _sds((1, HD))),
                 grid=(T // br,),
                 in_specs=[pl.BlockSpec((br, AH * HD), lambda i: (i, C_AQ // (AH * HD))),
                           pl.BlockSpec((br, 2 * AKV * HD), lambda i: (i, 0)), tab, tab, vec, vec,
                           pl.BlockSpec((AH, br, HD), lambda i: (0, jnp.maximum(i - lb, 0), 0)), kvb, kvb],
                 out_specs=(pl.BlockSpec((br, AH * HD), lambda i: (i, 0)),
                            pl.BlockSpec((br, 2 * AKV * HD), lambda i: (i, 0)), vec, vec),
                 sem=("arbitrary",))(proj, proj, cos, sin, qw, kw, dq, dk, dv)


def _attn_fn(q, k, v):
    s = _nt(q, k) * (HD ** -0.5)
    m = lax.stop_gradient(jnp.max(s, axis=-1, keepdims=True))
    e = jnp.exp(s - m)
    p = e / jnp.sum(e, axis=-1, keepdims=True)
    return _nn(p, v)


def _attn_fwd(q, k, v, L, exch, *, bq=128):
    T = q.shape[1]
    N = T - L
    lb = L // bq

    def body(q_ref, k_ref, v_ref, o_ref):
        qv = q_ref[...].reshape(GRP * bq, HD).astype(F32)
        o = _attn_fn(qv, k_ref[...].astype(F32), v_ref[...].astype(F32))
        for g in range(GRP):
            o_ref[:, g * HD:(g + 1) * HD] = o[g * bq:(g + 1) * bq].astype(BF16)

    kvb = pl.BlockSpec((None, T, HD), lambda g, i: (g, 0, 0))
    (attn,), moved = _call_carrying(
        body, exch, name="attn_fwd", out_shape=(_sds((N, AH * HD), BF16),), grid=(AKV, N // bq),
        in_specs=[pl.BlockSpec((GRP, bq, HD), lambda g, i: (g, i + lb, 0)), kvb, kvb],
        out_specs=(pl.BlockSpec((bq, GRP * HD), lambda g, i: (i, g)),), vmem=VMEM_BIG)(q, k, v)
    return attn, moved


def _attn_bwd(q, k, v, do, L, *, bq=128):
    T = q.shape[1]
    N = T - L
    lb = L // bq

    def body(q_ref, k_ref, v_ref, do_ref, dq_ref, dk_ref, dv_ref):
        qv = q_ref[...].reshape(GRP * bq, HD).astype(F32)
        _, vjp = jax.vjp(_attn_fn, qv, k_ref[...].astype(F32), v_ref[...].astype(F32))
        dov = jnp.concatenate([do_ref[:, g * HD:(g + 1) * HD] for g in range(GRP)], axis=0)
        dq, dk, dv = vjp(dov)
        dq_ref[...] = dq.reshape(GRP, bq, HD)

        @pl.when(pl.program_id(1) == 0)
        def _():
            dk_ref[...] = jnp.zeros_like(dk_ref)
            dv_ref[...] = jnp.zeros_like(dv_ref)

        dk_ref[...] += dk
        dv_ref[...] += dv

    kvb = pl.BlockSpec((None, T, HD), lambda g, i: (g, 0, 0))
    return _call(body, name="attn_bwd",
                 out_shape=(_sds((AH, N, HD)), _sds((AKV, T, HD)), _sds((AKV, T, HD))), grid=(AKV, N // bq),
                 in_specs=[pl.BlockSpec((GRP, bq, HD), lambda g, i: (g, i + lb, 0)), kvb, kvb,
                           pl.BlockSpec((bq, GRP * HD), lambda g, i: (i, g))],
                 out_specs=(pl.BlockSpec((GRP, bq, HD), lambda g, i: (g, i, 0)), kvb, kvb),
                 sem=("parallel", "arbitrary"), vmem=VMEM_BIG)(q, k, v, do)


def _gprep_fn(kind, shifts, x, w):
    down, up = shifts
    y = down(x) * w[0:1, :] + x * w[1:2, :] + up(x) * w[2:3, :]
    a = _silu(y)
    if kind == 2:
        return a
    a = a * lax.rsqrt(jnp.sum(a * a, axis=-1, keepdims=True) + EPS)
    return a * (HD ** -0.5) if kind == 0 else a


def _gprep_fwd(proj, conv_w, kind, bounds):
    T = proj.shape[0]
    shifts = _make_shift(bounds)
    cb = C_QKV // HD + kind * GH

    def body(x_ref, w_ref, o_ref):
        o_ref[...] = _gprep_fn(kind, shifts, x_ref[...], w_ref[...])

    return _call(body, name=f"gprep_fwd{kind}", out_shape=_sds((GH, T, HD)), grid=(GH,),
                 in_specs=[pl.BlockSpec((T, HD), lambda h: (0, cb + h)),
                           pl.BlockSpec((3, HD), lambda h: (0, kind * GH + h))],
                 out_specs=pl.BlockSpec((None, T, HD), lambda h: (h, 0, 0)), sem=("parallel",))(proj, conv_w)


def _gprep_bwd(proj, conv_w, kind, bounds, dy):
    T = proj.shape[0]
    shifts = _make_shift(bounds)
    cb = C_QKV // HD + kind * GH

    def body(x_ref, w_ref, dy_ref, dx_ref, dw_ref):
        _, vjp = jax.vjp(functools.partial(_gprep_fn, kind, shifts), x_ref[...], w_ref[...])
        dx, dw = vjp(dy_ref[0] + dy_ref[1])
        dx_ref[...] = dx.astype(BF16)
        dw_ref[...] = dw

    return _call(body, name=f"gprep_bwd{kind}", out_shape=(_sds((T, GH * HD), BF16), _sds((3, GH * HD))), grid=(GH,),
                 in_specs=[pl.BlockSpec((T, HD), lambda h: (0, cb + h)),
                           pl.BlockSpec((3, HD), lambda h: (0, kind * GH + h)),
                           pl.BlockSpec((2, None, T, HD), lambda h: (0, h, 0, 0))],
                 out_specs=(pl.BlockSpec((T, HD), lambda h: (0, h)), pl.BlockSpec((3, HD), lambda h: (0, h))),
                 sem=("parallel",))(proj, conv_w, dy)


def _bl_fn(x, alog, dtb):
    lane = lax.broadcasted_iota(jnp.int32, x.shape, 1)
    beta = jax.nn.sigmoid(x)
    z = x + dtb
    sp = jnp.maximum(z, 0.0) + jnp.log1p(jnp.exp(-jnp.abs(z)))
    la = -jnp.exp(alog) * sp
    return jnp.where(lane < 2 * GH, beta, jnp.where(lane < 4 * GH, la, 0.0))


def _bl_fwd(proj, alog, dtb, *, br=256):
    T = proj.shape[0]

    def body(x_ref, a_ref, d_ref, o_ref):
        o_ref[...] = _bl_fn(x_ref[...], a_ref[...], d_ref[...])

    vec = pl.BlockSpec((1, HD), lambda i: (0, 0))
    return _call(body, name="bl_fwd", out_shape=_sds((T, HD)), grid=(T // br,),
                 in_specs=[pl.BlockSpec((br, HD), lambda i: (i, C_BL // HD)), vec, vec],
                 out_specs=pl.BlockSpec((br, HD), lambda i: (i, 0)), sem=("parallel",))(proj, alog, dtb)


def _bl_bwd(proj, alog, dtb, dbl, *, br=256):
    T = proj.shape[0]

    def body(x_ref, a_ref, d_ref, g_ref, dx_ref, da_ref, dd_ref):
        g = g_ref[0, 0]
        for d in range(2):
            for h in range(GH):
                if d or h:
                    g = g + g_ref[d, h]
        _, vjp = jax.vjp(_bl_fn, x_ref[...], a_ref[...], d_ref[...])
        dx, da, dd = vjp(g)
        dx_ref[...] = dx.astype(BF16)

        @pl.when(pl.program_id(0) == 0)
        def _():
            da_ref[...] = jnp.zeros_like(da_ref)
            dd_ref[...] = jnp.zeros_like(dd_ref)

        da_ref[...] += da
        dd_ref[...] += dd

    vec = pl.BlockSpec((1, HD), lambda i: (0, 0))
    return _call(body, name="bl_bwd", out_shape=(_sds((T, HD), BF16), _sds((1, HD)), _sds((1, HD))), grid=(T // br,),
                 in_specs=[pl.BlockSpec((br, HD), lambda i: (i, C_BL // HD)), vec, vec,
                           pl.BlockSpec((2, GH, br, HD), lambda i: (0, 0, i, 0))],
                 out_specs=(pl.BlockSpec((br, HD), lambda i: (i, 0)), vec, vec), sem=("arbitrary",))(proj, alog, dtb, dbl)


def _chunk_masks(d):
    ii = lax.broadcasted_iota(jnp.int32, (CH, CH), 0)
    jj = lax.broadcasted_iota(jnp.int32, (CH, CH), 1)
    eye = (ii == jj).astype(F32)
    before = jnp.where(d == 0, (jj < ii).astype(F32), (jj > ii).astype(F32))
    return before, before + eye, eye


def _intra_fn(masks, sel_b, sel_l, qs, ks, vs, bls, xs=None):
    before, ateq, eye = masks
    ones = jnp.ones((CH, CH), F32)
    inc = ateq > 0.0
    each = lambda f, *ls: [f(*t) for t in zip(*ls)]
    beta = each(lambda bl: jnp.sum(bl * sel_b, axis=-1, keepdims=True), bls)
    la = each(lambda bl: jnp.sum(bl * sel_l, axis=-1, keepdims=True), bls)
    gam = each(lambda a: _mask_nn(ateq, jnp.broadcast_to(a, (CH, HD))), la)
    gi = each(lambda a: _mask_nn(ateq, jnp.broadcast_to(a, (CH, CH))), la)
    gj = each(lambda g: _mask_nn(ones, eye * g), gi)
    kk = each(lambda k: _nt(k, k), ks)
    qk = each(_nt, qs, ks)
    dec = each(lambda a, b: jnp.where(inc, jnp.exp(jnp.where(inc, a - b, 0.0)), 0.0), gi, gj)
    lmat = each(lambda b, d, m: before * (b * d * m), beta, dec, kk)
    if xs is None:
        x = each(lambda m: eye - m, lmat)
        p2 = each(lambda m: _mdot(m, m), lmat)
        for it in range(5):
            x = each(lambda a, b: a + _mdot(a, b), x, p2)
            if it < 4:
                p2 = each(lambda m: _mdot(m, m), p2)
    else:
        x = each(_saved_inverse, lmat, xs)
    eg = each(jnp.exp, gam)
    u = each(lambda a, b, v: _mdot(a, b * v), x, beta, vs)
    w = each(lambda a, b, e, k: _mdot(a, (b * e) * k), x, beta, eg, ks)
    tot = each(lambda a: jnp.sum(a, axis=0, keepdims=True), la)
    kd = each(lambda k, t, g: k * jnp.exp(t - g), ks, tot, gam)
    gl = each(lambda t: jnp.broadcast_to(jnp.exp(t), (1, HD)), tot)
    qd = each(lambda q, e: q * e, qs, eg)
    p = each(lambda d, m: d * m, dec, qk)
    return (u, w, kd, qd, p, gl, x) if xs is None else (u, w, kd, qd, p, gl)


def _dir_head_sel(d, h):
    lane = lax.broadcasted_iota(jnp.int32, (1, HD), 1)
    return (lane == d * GH + h).astype(F32), (lane == 2 * GH + d * GH + h).astype(F32)


def _intra_specs(T, G):
    nc = T // CH
    assert nc % G == 0
    qkv = pl.BlockSpec((None, G * CH, HD), lambda d, h, c: (h, c, 0))
    bl = pl.BlockSpec((G * CH, HD), lambda d, h, c: (c, 0))
    big = pl.BlockSpec((None, None, G * CH, HD), lambda d, h, c: (d, h, c, 0))
    pm = pl.BlockSpec((None, None, G * CH, CH), lambda d, h, c: (d, h, c, 0))
    gl = pl.BlockSpec((None, None, G, 1, HD), lambda d, h, c: (d, h, c, 0, 0))
    shapes = (_sds((2, GH, T, HD)),) + (_sds((2, GH, T, HD), BF16),) * 3 + (
        _sds((2, GH, T, CH), BF16), _sds((2, GH, nc, 1, HD)), _sds((2, GH, T, CH)))
    return nc, qkv, bl, big, pm, gl, shapes


def _chunks_per_step(T, most):
    nc = T // CH
    return max(g for g in range(1, most + 1) if nc % g == 0)


def _intra_fwd(q, k, v, bl, exch):
    T = q.shape[1]
    G = _chunks_per_step(T, INTRA_FWD_CHUNKS)
    nc, qkv_s, bl_s, big, pm, gl_s, shapes = _intra_specs(T, G)

    def body(q_ref, k_ref, v_ref, bl_ref, u_ref, w_ref, kd_ref, qd_ref, p_ref, gl_ref, x_ref):
        d, h = pl.program_id(0), pl.program_id(1)
        sb, sl = _dir_head_sel(d, h)
        rows = [slice(g * CH, (g + 1) * CH) for g in range(G)]
        outs = _intra_fn(_chunk_masks(d), sb, sl, *[[r[s, :] for s in rows] for r in (q_ref, k_ref, v_ref, bl_ref)])
        for g in range(G):
            for r, o in zip((u_ref, w_ref, kd_ref, qd_ref, p_ref, x_ref), outs[:5] + outs[6:]):
                r[rows[g], :] = o[g].astype(r.dtype)
            gl_ref[g] = outs[5][g]

    return _call_carrying(body, exch, name="gdn_intra_fwd", out_shape=shapes, grid=(2, GH, nc // G),
                          in_specs=[qkv_s, qkv_s, qkv_s, bl_s], out_specs=(big, big, big, big, pm, gl_s, pm))(q, k, v, bl)


def _intra_bwd(q, k, v, bl, xinv, cts, exch):
    T = q.shape[1]
    G = _chunks_per_step(T, INTRA_BWD_CHUNKS)
    nc, qkv_s, bl_s, big, pm, gl_s, _ = _intra_specs(T, G)

    def body(q_ref, k_ref, v_ref, bl_ref, x_ref, du, dw, dkd, dqd, dp, dgl, dq_ref, dk_ref, dv_ref, dbl_ref):
        d, h = pl.program_id(0), pl.program_id(1)
        sb, sl = _dir_head_sel(d, h)
        rows = [slice(g * CH, (g + 1) * CH) for g in range(G)]
        fn = functools.partial(_intra_fn, _chunk_masks(d), sb, sl, xs=[x_ref[s, :] for s in rows])
        _, vjp = jax.vjp(fn, *[[r[s, :] for s in rows] for r in (q_ref, k_ref, v_ref, bl_ref)])
        cts = tuple([r[s, :] for s in rows] for r in (du, dw, dkd, dqd, dp)) + ([dgl[g] for g in range(G)],)
        grads = vjp(cts)
        for g in range(G):
            for r, o in zip((dq_ref, dk_ref, dv_ref, dbl_ref), grads):
                r[rows[g], :] = o[g]

    return _call_carrying(body, exch, name="gdn_intra_bwd", out_shape=(_sds((2, GH, T, HD)),) * 4,
                          grid=(2, GH, nc // G), in_specs=[qkv_s, qkv_s, qkv_s, bl_s, pm, big, big, big, big, pm, gl_s],
                          out_specs=(big,) * 4)(q, k, v, bl, xinv, *cts)


def _scan_fn(s, u, w, kd, qd, p, gl):
    each = lambda f, *ls: [f(*t) for t in zip(*ls)]
    ws = each(_nn, w, s)
    delta = each(lambda a, b: a - b, u, ws)
    kdd = each(_tn, kd, delta)
    s_new = each(lambda g, a, b: g * a + b, gl, s, kdd)
    qs = each(_nn, qd, s)
    pd = each(_nn, p, delta)
    return each(lambda a, b: a + b, qs, pd), s_new


SCAN_BLOCK = 4


def _scan_visit(t, d, nb, ncb):
    rev = jnp.where(t < ncb, ncb - 1 - t, nb - 1 - (t - ncb))
    return jnp.where(d == 0, t, rev)


def _scan_specs(T, L, back):
    tb = SCAN_BLOCK * CH
    assert T % tb == 0 and L % tb == 0
    nb, ncb = T // tb, L // tb

    def at(d, t):
        return _scan_visit(nb - 1 - t if back else t, d, nb, ncb)

    big = pl.BlockSpec((None, GH, tb, HD), lambda d, t: (d, 0, at(d, t), 0))
    pm = pl.BlockSpec((None, GH, tb, CH), lambda d, t: (d, 0, at(d, t), 0))
    gl = pl.BlockSpec((None, GH, SCAN_BLOCK, 1, HD), lambda d, t: (d, 0, at(d, t), 0, 0))
    st = pl.BlockSpec((None, GH, SCAN_BLOCK, HD, HD), lambda d, t: (d, 0, at(d, t), 0, 0))
    do = pl.BlockSpec((GH, tb, HD), lambda d, t: (0, at(d, t), 0))
    return nb, big, pm, gl, st, do


def _scan_fwd(u, w, kd, qd, p, gl, L):
    T = u.shape[2]
    nb, big, pm, gl_s, st, _ = _scan_specs(T, L, False)
    heads = range(GH)

    def body(u_ref, w_ref, kd_ref, qd_ref, p_ref, gl_ref, o_ref, st_ref, s_scr):
        d = pl.program_id(0)

        @pl.when(pl.program_id(1) == 0)
        def _():
            s_scr[...] = jnp.zeros_like(s_scr)

        s = [s_scr[h] for h in heads]
        for i in range(SCAN_BLOCK):
            c = jnp.where(d == 0, i, SCAN_BLOCK - 1 - i)
            rows = pl.ds(pl.multiple_of(c * CH, CH), CH)
            for h in heads:
                st_ref[h, c] = s[h]
            o, s = _scan_fn(s, *[[r[h, rows, :].astype(F32) for h in heads] for r in (u_ref, w_ref, kd_ref, qd_ref, p_ref)],
                            [gl_ref[h, c] for h in heads])
            for h in heads:
                o_ref[h, rows, :] = o[h]
        for h in heads:
            s_scr[h] = s[h]

    return _call(body, name="gdn_scan_fwd", out_shape=(_sds((2, GH, T, HD)), _sds((2, GH, T // CH, HD, HD))),
                 grid=(2, nb), in_specs=[big, big, big, big, pm, gl_s], out_specs=(big, st),
                 scratch=[pltpu.VMEM((GH, HD, HD), F32)], sem=("parallel", "arbitrary"))(u, w, kd, qd, p, gl)


def _scan_bwd(u, w, kd, qd, p, gl, states, do, L):
    T = u.shape[2]
    nb, big, pm, gl_s, st, do_s = _scan_specs(T, L, True)
    heads = range(GH)

    def body(u_ref, w_ref, kd_ref, qd_ref, p_ref, gl_ref, st_ref, do_ref,
             du_ref, dw_ref, dkd_ref, dqd_ref, dp_ref, dgl_ref, ds_scr):
        d = pl.program_id(0)

        @pl.when(pl.program_id(1) == 0)
        def _():
            ds_scr[...] = jnp.zeros_like(ds_scr)

        ds = [ds_scr[h] for h in heads]
        for i in range(SCAN_BLOCK):
            c = jnp.where(d == 0, SCAN_BLOCK - 1 - i, i)
            rows = pl.ds(pl.multiple_of(c * CH, CH), CH)
            _, vjp = jax.vjp(_scan_fn, [st_ref[h, c] for h in heads],
                             *[[r[h, rows, :].astype(F32) for h in heads] for r in (u_ref, w_ref, kd_ref, qd_ref, p_ref)],
                             [gl_ref[h, c] for h in heads])
            ds, gu, gw, gkd, gqd, gp, ggl = vjp(([do_ref[h, rows, :] for h in heads], ds))
            for h in heads:
                du_ref[h, rows, :] = gu[h]
                dw_ref[h, rows, :] = gw[h]
                dkd_ref[h, rows, :] = gkd[h]
                dqd_ref[h, rows, :] = gqd[h]
                dp_ref[h, rows, :] = gp[h]
                dgl_ref[h, c] = ggl[h]
        for h in heads:
            ds_scr[h] = ds[h]

    return _call(body, name="gdn_scan_bwd",
                 out_shape=(_sds((2, GH, T, HD)),) * 4 + (_sds((2, GH, T, CH)), _sds((2, GH, T // CH, 1, HD))),
                 grid=(2, nb), in_specs=[big, big, big, big, pm, gl_s, st, do_s],
                 out_specs=(big, big, big, big, pm, gl_s), scratch=[pltpu.VMEM((GH, HD, HD), F32)],
                 sem=("parallel", "arbitrary"))(u, w, kd, qd, p, gl, states, do)


def _gout_fn(o0, o1, z, gw):
    return _rms(o0 + o1) * gw * _silu(z)


def _gout_fwd(o, proj, gw, L, *, br=256):
    T = o.shape[2]
    N = T - L
    lb = L // br
    ob = pl.BlockSpec((None, None, br, HD), lambda i, h: (0, h, i + lb, 0))
    ob1 = pl.BlockSpec((None, None, br, HD), lambda i, h: (1, h, i + lb, 0))

    def body(o0_ref, o1_ref, z_ref, gw_ref, y_ref):
        y_ref[...] = _gout_fn(o0_ref[...], o1_ref[...], z_ref[...], gw_ref[...]).astype(BF16)

    return _call(body, name="gout_fwd", out_shape=_sds((N, GH * HD), BF16), grid=(N // br, GH),
                 in_specs=[ob, ob1, pl.BlockSpec((br, HD), lambda i, h: (i + lb, C_Z // HD + h)),
                           pl.BlockSpec((1, HD), lambda i, h: (0, 0))],
                 out_specs=pl.BlockSpec((br, HD), lambda i, h: (i, h)), sem=("parallel", "parallel"))(o, o, proj, gw)


def _gout_bwd(o, proj, gw, dy, L, *, br=256):
    T = o.shape[2]
    lb = L // br
    ob = pl.BlockSpec((None, None, br, HD), lambda i, h: (0, h, i, 0))
    ob1 = pl.BlockSpec((None, None, br, HD), lambda i, h: (1, h, i, 0))

    def body(o0_ref, o1_ref, z_ref, gw_ref, dy_ref, do_ref, dz_ref, dgw_ref):
        i, h = pl.program_id(0), pl.program_id(1)
        _, vjp = jax.vjp(_gout_fn, o0_ref[...], o1_ref[...], z_ref[...], gw_ref[...])
        g0, _, gz, ggw = vjp(dy_ref[...])
        lat = i >= lb
        do_ref[...] = jnp.where(lat, g0, 0.0)
        dz_ref[...] = jnp.where(lat, gz, 0.0).astype(BF16)

        @pl.when(jnp.logical_and(i == 0, h == 0))
        def _():
            dgw_ref[...] = jnp.zeros_like(dgw_ref)

        dgw_ref[...] += jnp.where(lat, ggw, 0.0)

    return _call(body, name="gout_bwd", out_shape=(_sds((GH, T, HD)), _sds((T, GH * HD), BF16), _sds((1, HD))),
                 grid=(T // br, GH),
                 in_specs=[ob, ob1, pl.BlockSpec((br, HD), lambda i, h: (i, C_Z // HD + h)),
                           pl.BlockSpec((1, HD), lambda i, h: (0, 0)),
                           pl.BlockSpec((br, HD), lambda i, h: (jnp.maximum(i - lb, 0), h))],
                 out_specs=(pl.BlockSpec((None, br, HD), lambda i, h: (h, i, 0)),
                            pl.BlockSpec((br, HD), lambda i, h: (i, h)),
                            pl.BlockSpec((1, HD), lambda i, h: (0, 0))),
                 sem=("arbitrary", "arbitrary"))(o, o, proj, gw, dy)


def _merge_fn(pa, pd, ga, gd):
    return jax.nn.sigmoid(ga) * pa + jax.nn.sigmoid(gd) * pd


def _merge_fwd(pa, pd, proj, L, *, br=256):
    N = pa.shape[0]
    lb = L // br
    row = pl.BlockSpec((br, D), lambda i: (i, 0))

    def body(pa_ref, pd_ref, ga_ref, gd_ref, y_ref):
        y_ref[...] = _merge_fn(pa_ref[...], pd_ref[...], ga_ref[...], gd_ref[...]).astype(BF16)

    return _call(body, name="merge_fwd", out_shape=_sds((N, D), BF16), grid=(N // br,),
                 in_specs=[row, row, pl.BlockSpec((br, D), lambda i: (i + lb, C_GATE // D)),
                           pl.BlockSpec((br, D), lambda i: (i + lb, C_GATE // D + 1))],
                 out_specs=row, sem=("parallel",))(pa, pd, proj, proj)


def _merge_bwd(pa, pd, proj, dy, L, *, br=256):
    N = pa.shape[0]
    T = N + L
    lb = L // br
    lrow = pl.BlockSpec((br, D), lambda i: (jnp.maximum(i - lb, 0), 0))

    def body(pa_ref, pd_ref, ga_ref, gd_ref, dy_ref, dpa_ref, dpd_ref, dg_ref):
        lat = pl.program_id(0) >= lb
        _, vjp = jax.vjp(_merge_fn, pa_ref[...], pd_ref[...], ga_ref[...], gd_ref[...])
        gpa, gpd, gga, ggd = vjp(dy_ref[...])
        dpa_ref[...] = gpa.astype(BF16)
        dpd_ref[...] = gpd.astype(BF16)
        dg_ref[:, :D] = jnp.where(lat, gga, 0.0).astype(BF16)
        dg_ref[:, D:] = jnp.where(lat, ggd, 0.0).astype(BF16)

    return _call(body, name="merge_bwd", out_shape=(_sds((N, D), BF16), _sds((N, D), BF16), _sds((T, 2 * D), BF16)),
                 grid=(T // br,),
                 in_specs=[lrow, lrow, pl.BlockSpec((br, D), lambda i: (i, C_GATE // D)),
                           pl.BlockSpec((br, D), lambda i: (i, C_GATE // D + 1)), lrow],
                 out_specs=(lrow, lrow, pl.BlockSpec((br, 2 * D), lambda i: (i, 0))),
                 sem=("arbitrary",))(pa, pd, proj, proj, dy)


def _resid_fwd(x, m, mod, i_g, *, name, br=256):
    R = x.shape[0]
    row = pl.BlockSpec((br, D), lambda i: (i, 0))

    def body(x_ref, m_ref, mod_ref, o_ref):
        o_ref[...] = x_ref[...] + mod_ref[i_g:i_g + 1, :] * m_ref[...]

    return _call(body, name=name, out_shape=_sds((R, D)), grid=(R // br,),
                 in_specs=[row, row, pl.BlockSpec((6, D), lambda i: (0, 0))], out_specs=row,
                 sem=("parallel",))(x, m, mod)


def _resid_bwd(dx, m, mod, i_g, *, name, br=256):
    R = dx.shape[0]
    row = pl.BlockSpec((br, D), lambda i: (i, 0))
    vec = pl.BlockSpec((1, D), lambda i: (0, 0))

    def body(dx_ref, m_ref, mod_ref, dm_ref, dg_ref):
        dxv = dx_ref[...]
        dm_ref[...] = (dxv * mod_ref[i_g:i_g + 1, :]).astype(BF16)

        @pl.when(pl.program_id(0) == 0)
        def _():
            dg_ref[...] = jnp.zeros_like(dg_ref)

        dg_ref[...] += jnp.sum(dxv * m_ref[...], axis=0, keepdims=True)

    return _call(body, name=name, out_shape=(_sds((R, D), BF16), _sds((1, D))), grid=(R // br,),
                 in_specs=[row, row, pl.BlockSpec((6, D), lambda i: (0, 0))], out_specs=(row, vec),
                 sem=("arbitrary",))(dx, m, mod)


def _ffn_fn(shifts, ug, uv, wg, wv, bg, bv):
    down, up = shifts

    def conv(x, w, b):
        return down(x) * w[0:1, :] + x * w[1:2, :] + up(x) * w[2:3, :] + b

    return _silu(conv(ug, wg, bg)) * conv(uv, wv, bv)


def _ffn_fwd(up, cw, cb, *, bw=256):
    N = up.shape[0]
    shifts = _make_shift(((0, N),))
    nb = DFF // bw

    def body(ug, uv, wg, wv, bg, bv, a_ref):
        a_ref[...] = _ffn_fn(shifts, ug[...], uv[...], wg[...], wv[...], bg[...], bv[...]).astype(BF16)

    def col(rows, off):
        return pl.BlockSpec((rows, bw), lambda j: (0, j + off))

    return _call(body, name="ffn_fwd", out_shape=_sds((N, DFF), BF16), grid=(nb,),
                 in_specs=[col(N, 0), col(N, nb), col(3, 0), col(3, nb), col(1, 0), col(1, nb)],
                 out_specs=col(N, 0), sem=("parallel",), vmem=VMEM_BIG)(up, up, cw, cw, cb, cb)


def _ffn_bwd(up, cw, cb, da, *, bw=256):
    N = up.shape[0]
    shifts = _make_shift(((0, N),))
    nb = DFF // bw

    def body(ug, uv, wg, wv, bg, bv, da_ref, dug, duv, dwg, dwv, dbg, dbv):
        _, vjp = jax.vjp(functools.partial(_ffn_fn, shifts), ug[...], uv[...], wg[...], wv[...], bg[...], bv[...])
        g = vjp(da_ref[...])
        dug[...] = g[0].astype(BF16)
        duv[...] = g[1].astype(BF16)
        dwg[...], dwv[...], dbg[...], dbv[...] = g[2], g[3], g[4], g[5]

    def col(rows, off):
        return pl.BlockSpec((rows, bw), lambda j: (0, j + off))

    half = (_sds((N, DFF), BF16), _sds((N, DFF), BF16), _sds((3, DFF)), _sds((3, DFF)), _sds((1, DFF)), _sds((1, DFF)))
    dug, duv, dwg, dwv, dbg, dbv = _call(
        body, name="ffn_bwd", out_shape=half, grid=(nb,),
        in_specs=[col(N, 0), col(N, nb), col(3, 0), col(3, nb), col(1, 0), col(1, nb), col(N, 0)],
        out_specs=(col(N, 0), col(N, 0), col(3, 0), col(3, 0), col(1, 0), col(1, 0)),
        sem=("parallel",), vmem=VMEM_BIG)(up, up, cw, cw, cb, cb, da)
    return (jnp.concatenate([dug, duv], axis=1), jnp.concatenate([dwg, dwv], axis=1),
            jnp.concatenate([dbg, dbv], axis=1))


def _head_fn(x1, dn, g2, fw, tgt):
    y = _rms(x1 + g2 * dn) * fw
    err = y - tgt
    return 0.5 * jnp.sum(jnp.mean(err * err, axis=-1))


def _head(x1, dn, mod, fw, tgt, *, br=256):
    N = x1.shape[0]
    row = pl.BlockSpec((br, D), lambda i: (i, 0))
    vec = pl.BlockSpec((1, D), lambda i: (0, 0))
    one = pl.BlockSpec((1, HD), lambda i: (0, 0))

    def body(x1_ref, dn_ref, mod_ref, fw_ref, tgt_ref, loss_ref, dx_ref, ddn_ref, dg_ref, dfw_ref):
        loss, (gx, gdn, gg, gfw) = jax.value_and_grad(_head_fn, argnums=(0, 1, 2, 3))(
            x1_ref[...], dn_ref[...], mod_ref[5:6, :], fw_ref[...], tgt_ref[...])
        dx_ref[...] = gx
        ddn_ref[...] = gdn.astype(BF16)

        @pl.when(pl.program_id(0) == 0)
        def _():
            loss_ref[...] = jnp.zeros_like(loss_ref)
            dg_ref[...] = jnp.zeros_like(dg_ref)
            dfw_ref[...] = jnp.zeros_like(dfw_ref)

        loss_ref[...] += jnp.broadcast_to(loss, (1, HD))
        dg_ref[...] += gg
        dfw_ref[...] += gfw

    return _call(body, name="head", out_shape=(_sds((1, HD)), _sds((N, D)), _sds((N, D), BF16), _sds((1, D)), _sds((1, D))),
                 grid=(N // br,), in_specs=[row, row, pl.BlockSpec((6, D), lambda i: (0, 0)), vec, row],
                 out_specs=(one, row, row, vec, vec), sem=("arbitrary",))(x1, dn, mod, fw, tgt)


def _adamw(w, g, m, v, *, name):
    shape = w.shape
    cols = shape[-1]
    rows = max(1, math.prod(shape[:-1]))
    w2, g2, m2, v2 = (t.reshape(rows, cols) for t in (w, g, m, v))
    br = 256 if rows % 256 == 0 else (128 if rows % 128 == 0 else (8 if rows % 8 == 0 and rows > 64 else rows))
    if rows % 352 == 0:
        br = 352
    c1 = 1.0 - B1 ** STEP
    c2 = 1.0 - B2 ** STEP

    def body(w_ref, g_ref, m_ref, v_ref, d_ref, nm_ref, nv_ref):
        gv = g_ref[...]
        nm = B1 * m_ref[...] + (1.0 - B1) * gv
        nv = B2 * v_ref[...] + (1.0 - B2) * (gv * gv)
        d_ref[...] = -LR * ((nm / c1) / (jnp.sqrt(nv / c2) + AEPS) + WD * w_ref[...])
        nm_ref[...] = nm
        nv_ref[...] = nv

    blk = pl.BlockSpec((br, cols), lambda i: (i, 0))
    outs = _call(body, name=name, out_shape=(_sds((rows, cols)),) * 3, grid=(rows // br,),
                 in_specs=[blk] * 4, out_specs=(blk,) * 3, sem=("parallel",))(w2, g2, m2, v2)
    return tuple(t.reshape(shape) for t in outs)


def _rope_tables(N, L):
    t = jnp.arange(N)
    pos = jnp.stack([(t // GRID_W).astype(F32), (t % GRID_W).astype(F32)], axis=1)
    inv = ROPE_THETA ** (-jnp.arange(0, HD // 2, 2, dtype=F32) / (HD // 2))
    ang = pos[:, :, None] * inv[None, None, :]
    cos = jnp.broadcast_to(jnp.cos(ang)[:, :, None, :], (N, 2, 2, HD // 4)).reshape(N, HD)
    sin = jnp.broadcast_to(jnp.sin(ang)[:, :, None, :], (N, 2, 2, HD // 4))
    sin = (sin * jnp.array([-1.0, 1.0], F32)[None, None, :, None]).reshape(N, HD)
    cos = jnp.concatenate([jnp.ones((L, HD), F32), cos], axis=0)
    sin = jnp.concatenate([jnp.zeros((L, HD), F32), sin], axis=0)
    return cos, sin


def _pad_lanes(v, off=0):
    return jnp.zeros((1, HD), F32).at[0, off:off + v.shape[0]].set(v)


def _local_step(x, ctx, tgt, mod_lat, mod_ctx, w_in, shards, small):
    N, L = x.shape[0], ctx.shape[0]
    T = N + L
    bounds = ((0, L), (L, T))
    qw, kw, gw = small["q_norm_w"], small["k_norm_w"], small["gdn_norm_w"]
    conv_w, ffn_w, ffn_b, fnw = small["conv_qkv_w"], small["ffn_conv_w"], small["ffn_conv_b"], small["final_norm_w"]
    alog = _pad_lanes(small["a_log"].reshape(-1), 2 * GH)
    dtb = _pad_lanes(small["dt_bias"].reshape(-1), 2 * GH)
    cos, sin = _rope_tables(N, L)
    bt = 256 if T % 768 else 768
    bnl = 256 if N % 1024 else 1024

    hc = _normmod_fwd(ctx, mod_ctx, 0, 1, name="normmod_ctx")
    hx = _normmod_fwd(x, mod_lat, 0, 1, name="normmod_x")
    h1 = jnp.concatenate([hc, hx], axis=0)
    proj = _mm(h1, w_in, name="mm_in", M=T, N=C_END, K=D, bm=bt, bn=1024)
    aq, ak, av = _aprep_fwd(proj, cos, sin, qw, kw)
    attn, (up_g,) = _attn_fwd(aq, ak, av, L, _Exchange([shards["w_up"]], False))
    gq = _gprep_fwd(proj, conv_w, 0, bounds)
    gk = _gprep_fwd(proj, conv_w, 1, bounds)
    gv = _gprep_fwd(proj, conv_w, 2, bounds)
    bl = _bl_fwd(proj, alog, dtb)
    intra, (down_g, pa_g, pd_g, out_g) = _intra_fwd(
        gq, gk, gv, bl, _Exchange([shards[n] for n in ("w_down", "w_pa", "w_pd", "w_out")], False))
    w_up, w_down = _by_columns(up_g), down_g.reshape(DFF, D)
    w_pa, w_pd, w_out = pa_g.reshape(D, D), pd_g.reshape(D, D), out_g.reshape(D, D)
    xinv, intra = intra[6], intra[:6]
    o, states = _scan_fwd(*intra, L)
    gdn = _gout_fwd(o, proj, gw, L)
    pa = _mm(attn, w_pa, name="mm_pa", M=N, N=D, K=D, bm=bnl)
    pd = _mm(gdn, w_pd, name="mm_pd", M=N, N=D, K=D, bm=bnl)
    y = _merge_fwd(pa, pd, proj, L)
    m = _mm(y, w_out, name="mm_out", M=N, N=D, K=D, bm=bnl)
    x1 = _resid_fwd(x, m, mod_lat, 2, name="resid1")
    h2 = _normmod_fwd(x1, mod_lat, 3, 4, name="normmod_x1")
    up = _mm(h2, w_up, name="mm_up", M=N, N=2 * DFF, K=D, bm=bnl, bn=2 * DFF // 4)
    a = _ffn_fwd(up, ffn_w, ffn_b)
    dn = _mm(a, w_down, name="mm_down", M=N, N=D, K=DFF, bm=bnl)
    loss, dx2, ddn, dg2, dfnw = _head(x1, dn, mod_lat, fnw, tgt)

    da = _mm(ddn, w_down, name="mm_down_dx", M=N, N=DFF, K=D, tb=True, bm=bnl, bn=DFF // 2)
    g_down = _mm(a, ddn, name="mm_down_dw", M=DFF, N=D, K=N, ta=True, bm=DFF // 2)
    dup, d_ffn_w, d_ffn_b = _ffn_bwd(up, ffn_w, ffn_b, da)
    dh2 = _mm(dup, w_up, name="mm_up_dx", M=N, N=D, K=2 * DFF, tb=True, bm=bnl, bk=2 * DFF // 4)
    g_up = _mm(h2, dup, name="mm_up_dw", M=D, N=2 * DFF, K=N, ta=True, bn=2 * DFF // 4)
    dx1, dsh2, dsc2 = _normmod_bwd(x1, mod_lat, 3, 4, dh2, 0, dx2, name="normmod_x1_bwd")
    dm, dg1 = _resid_bwd(dx1, m, mod_lat, 2, name="resid1_bwd")
    dy = _mm(dm, w_out, name="mm_out_dx", M=N, N=D, K=D, tb=True, bm=bnl)
    g_out = _mm(y, dm, name="mm_out_dw", M=D, N=D, K=N, ta=True)
    dpa, dpd, dgate = _merge_bwd(pa, pd, proj, dy, L)
    dattn = _mm(dpa, w_pa, name="mm_pa_dx", M=N, N=D, K=D, tb=True, bm=bnl)
    g_pa = _mm(attn, dpa, name="mm_pa_dw", M=D, N=D, K=N, ta=True)
    dgdn = _mm(dpd, w_pd, name="mm_pd_dx", M=N, N=D, K=D, tb=True, bm=bnl)
    g_pd = _mm(gdn, dpd, name="mm_pd_dw", M=D, N=D, K=N, ta=True)
    do, dz, dgw = _gout_bwd(o, proj, gw, dgdn, L)
    cts = _scan_bwd(*intra, states, do, L)
    parts = [g_pa.reshape(NDEV, D // NDEV, D), g_pd.reshape(NDEV, D // NDEV, D), g_out.reshape(NDEV, D // NDEV, D),
             _to_columns(g_up), g_down.reshape(NDEV, DFF // NDEV, D)]
    (dgq, dgk, dgv, dbl), recv = _intra_bwd(gq, gk, gv, bl, xinv, cts,
                                            _Exchange([p.astype(BF16) for p in parts], True))
    dxq, dwq = _gprep_bwd(proj, conv_w, 0, bounds, dgq)
    dxk, dwk = _gprep_bwd(proj, conv_w, 1, bounds, dgk)
    dxv, dwv = _gprep_bwd(proj, conv_w, 2, bounds, dgv)
    dxbl, dalog, ddtb = _bl_bwd(proj, alog, dtb, dbl)
    daq_h, dak_h, dav_h = _attn_bwd(aq, ak, av, dattn, L)
    daq, dkv, dqw, dkw = _aprep_bwd(proj, cos, sin, qw, kw, daq_h, dak_h, dav_h, L)
    dproj = jnp.concatenate([dkv, dxq, dxk, dxv, dxbl, jnp.zeros((T, C_AQ - C_BL - HD), BF16), daq, dz, dgate], axis=1)
    g_in = _mm(h1, dproj, name="mm_in_dw", M=D, N=C_END, K=T, ta=True, bn=1024)
    g_in = _to_columns(jnp.concatenate([g_in[:, :W_AQ], g_in[:, C_AQ:]], axis=1)).astype(BF16)
    own_in = lax.dynamic_index_in_dim(g_in, _position()[3], axis=0, keepdims=False)
    *pending, token = _scatter_start(g_in, name="scatter_g_in_start")
    dh1 = _mm(dproj, w_in, name="mm_in_dx", M=T, N=D, K=C_END, tb=True, bm=bt, bk=1024, after=(token,))
    grad_x, dsh1, dsc1 = _normmod_bwd(x, mod_lat, 0, 1, dh1, L, dx1, name="normmod_x_bwd")
    _, dcsh1, dcsc1 = _normmod_bwd(ctx, mod_ctx, 0, 1, dh1, 0, None, name="normmod_ctx_bwd")

    z1 = jnp.zeros((1, D), F32)
    dmod_lat = jnp.concatenate([dsh1, dsc1, dg1, dsh2, dsc2, dg2], axis=0)
    dmod_ctx = jnp.concatenate([dcsh1, dcsc1, z1, z1, z1, z1], axis=0)
    gsmall = {
        "q_norm_w": dqw, "k_norm_w": dkw, "gdn_norm_w": dgw,
        "conv_qkv_w": jnp.concatenate([dwq, dwk, dwv], axis=1),
        "a_log": dalog[0, 2 * GH:4 * GH], "dt_bias": ddtb[0, 2 * GH:4 * GH],
        "ffn_conv_w": d_ffn_w, "ffn_conv_b": d_ffn_b, "final_norm_w": dfnw,
    }
    return (loss[0, 0], grad_x, (pending, own_in), dict(zip(("w_pa", "w_pd", "w_out", "w_up", "w_down"), recv)),
            dmod_lat, dmod_ctx, gsmall)


HBM = pl.BlockSpec(memory_space=pltpu.HBM)


def _position():
    x, y, c = lax.axis_index("x"), lax.axis_index("y"), lax.axis_index("c")
    return x, y, c, 4 * x + 2 * y + c


def _peer(x, y, c, k):
    px = 1 - x if k & 4 else x
    py = 1 - y if k & 2 else y
    pc = 1 - c if k & 1 else c
    return (px, py, pc), 4 * px + 2 * py + pc


def _exchange(arrs, *, name, scatter):
    exch = _Exchange(arrs, scatter)
    n = exch.n

    def body(*refs):
        ins, outs, sems = refs[:n], refs[n:2 * n], refs[2 * n:]
        exch.start(ins, outs, sems)
        exch.finish(ins, outs, sems)

    outs = pl.pallas_call(body, name=name, out_shape=exch.out_shape, in_specs=[HBM] * n, out_specs=(HBM,) * n,
                          scratch_shapes=exch.scratch,
                          compiler_params=pltpu.CompilerParams(has_side_effects=True))(*arrs)
    return list(outs)


class _Exchange:
    def __init__(self, arrs, scatter):
        self.arrs, self.scatter, self.n = list(arrs), scatter, len(arrs)
        self.out_shape = tuple(_sds(a.shape if scatter else (NDEV,) + a.shape, a.dtype) for a in arrs)
        self.scratch = [pltpu.SemaphoreType.DMA((self.n, NDEV - 1)), pltpu.SemaphoreType.DMA((self.n, NDEV - 1)),
                        pltpu.SemaphoreType.DMA((self.n,))]

    def _copies(self, ins, outs, sems):
        send, recv, loc = sems
        x, y, c, me = _position()
        local = [pltpu.make_async_copy(ins[a].at[me] if self.scatter else ins[a], outs[a].at[me], loc.at[a])
                 for a in range(self.n)]
        remote = []
        for k in range(1, NDEV):
            peer, pid = _peer(x, y, c, k)
            for a in range(self.n):
                src = ins[a].at[pid] if self.scatter else ins[a]
                remote.append(pltpu.make_async_remote_copy(
                    src_ref=src, dst_ref=outs[a].at[me], send_sem=send.at[a, k - 1], recv_sem=recv.at[a, k - 1],
                    device_id=peer, device_id_type=MESH))
        return local, remote

    def start(self, ins, outs, sems):
        local, remote = self._copies(ins, outs, sems)
        for cp in local + remote:
            cp.start()

    def finish(self, ins, outs, sems):
        local, remote = self._copies(ins, outs, sems)
        for cp in remote:
            cp.wait()
        for cp in local:
            cp.wait()


SEM = pl.BlockSpec(memory_space=pltpu.SEMAPHORE)


def _scatter_copies(src_ref, land_ref, send_sems, recv_sems):
    x, y, c, me = _position()
    copies = []
    for k in range(1, NDEV):
        peer, pid = _peer(x, y, c, k)
        copies.append(pltpu.make_async_remote_copy(
            src_ref=src_ref.at[pid], dst_ref=land_ref.at[me], send_sem=send_sems.at[k - 1],
            recv_sem=recv_sems.at[k - 1], device_id=peer, device_id_type=MESH))
    return copies


def _scatter_start(parts, *, name):
    def body(src_ref, land_ref, send_sems, recv_sems, src_thru, land_thru, token):
        for cp in _scatter_copies(src_ref, land_ref, send_sems, recv_sems):
            cp.start()
        token[...] = jnp.zeros_like(token)

    return pl.pallas_call(
        body, name=name,
        out_shape=(pltpu.SemaphoreType.DMA((NDEV - 1,)), pltpu.SemaphoreType.DMA((NDEV - 1,)),
                   pltpu.HBM(parts.shape, parts.dtype), pltpu.HBM(parts.shape, parts.dtype), _sds((8, HD))),
        in_specs=(HBM, HBM), out_specs=(SEM, SEM, HBM, HBM, pl.BlockSpec(memory_space=pltpu.VMEM)),
        input_output_aliases={0: 2, 1: 3},
        compiler_params=pltpu.CompilerParams(has_side_effects=pltpu.SideEffectType.DATAFLOW_SIDE_EFFECTING),
    )(pltpu.with_memory_space_constraint(parts, pltpu.HBM),
      pltpu.with_memory_space_constraint(lax.empty(parts.shape, parts.dtype), pltpu.HBM))


def _scatter_wait(send_sems, recv_sems, src_thru, land_thru, after, *, name):
    na = len(after)

    def body(src_ref, land_ref, send_sems, recv_sems, *rest):
        for cp in _scatter_copies(src_ref, land_ref, send_sems, recv_sems):
            cp.wait_send()
            cp.wait_recv()

    return pl.pallas_call(
        body, name=name,
        out_shape=(pltpu.HBM(src_thru.shape, src_thru.dtype), pltpu.HBM(land_thru.shape, land_thru.dtype)),
        in_specs=(HBM, HBM, SEM, SEM) + (pl.BlockSpec(memory_space=pl.ANY),) * na, out_specs=(HBM, HBM),
        input_output_aliases={0: 0, 1: 1},
        compiler_params=pltpu.CompilerParams(has_side_effects=pltpu.SideEffectType.DATAFLOW_SIDE_EFFECTING),
    )(src_thru, land_thru, send_sems, recv_sems, *after)[1]


def _cast_bf16(w, *, name):
    rows, cols = w.shape
    br = 128 if rows % 128 == 0 else rows

    def body(w_ref, o_ref):
        o_ref[...] = w_ref[...].astype(BF16)

    blk = pl.BlockSpec((br, cols), lambda i: (i, 0))
    return _call(body, name=name, out_shape=_sds((rows, cols), BF16), grid=(rows // br,), in_specs=[blk],
                 out_specs=blk, sem=("parallel",))(w)


def _sum_slots(a, *, name):
    _, R, C = a.shape

    def body(a_ref, o_ref):
        s = a_ref[0]
        for d in range(1, NDEV):
            s = s + a_ref[d]
        o_ref[...] = s

    return _call(body, name=name, out_shape=_sds((R, C)))(a)


MODROWS = 16


def _mod_fwd(c9, w, b):
    cols = w.shape[1]

    def body(c_ref, w_ref, b_ref, o_ref):
        o_ref[...] = _nn(_silu(c_ref[...]), w_ref[...]) + b_ref[...]

    return _call(body, name="mod_fwd", out_shape=_sds((MODROWS, cols)))(c9, w, b)


def _mod_bwd(c9, dmy, dall, w):
    cols = w.shape[1]

    def body(c_ref, dmy_ref, dall_ref, w_ref, gw_ref, gb_ref, cp_ref):
        sc = _silu(c_ref[...])
        rows = lax.broadcasted_iota(jnp.int32, (MODROWS, 1), 0)
        d = dmy_ref[...]
        d_ctx = jnp.where(rows == NDEV, d, 0.0)
        sc_ctx = jnp.where(rows == NDEV, sc, 0.0)
        outer = lax.dot_general(sc_ctx, d_ctx, (((0,), (0,)), ((), ())), precision=HI, preferred_element_type=F32)
        gw_ref[...] = _tn(jnp.where(rows < NDEV, sc, 0.0), jnp.where(rows < NDEV, d, 0.0)) + outer
        gb_ref[...] = jnp.sum(dall_ref[...], axis=0, keepdims=True)
        cp_ref[...] = jnp.sum(_nt(d_ctx, w_ref[...]), axis=0, keepdims=True)

    return _call(body, name="mod_bwd", out_shape=(_sds((D, cols)), _sds((1, 6 * D)), _sds((1, D))),
                 vmem=VMEM_BIG)(c9, dmy, dall, w)


def _cctx_finish(parts, c_ctx):
    def body(p_ref, c_ref, o_ref):
        s = p_ref[0]
        for d in range(1, NDEV):
            s = s + p_ref[d]
        _, vjp = jax.vjp(_silu, c_ref[...])
        o_ref[...] = vjp(s)[0]

    return _call(body, name="cctx_finish", out_shape=_sds((1, D)))(parts, c_ctx)


def _adamw_recv(w, recv, m, v, *, name, own=None):
    rows, cols = w.shape
    br = 128 if rows % 128 == 0 else rows
    c1 = 1.0 - B1 ** STEP
    c2 = 1.0 - B2 ** STEP
    has_own = own is not None

    def body(w_ref, r_ref, m_ref, v_ref, *rest):
        g_ref, d_ref, nm_ref, nv_ref = rest[-4:]
        me = _position()[3]

        def slot(d):
            return jnp.where(me == d, rest[0][...], r_ref[d]) if has_own else r_ref[d]

        gv = slot(0).astype(F32)
        for d in range(1, NDEV):
            gv = gv + slot(d).astype(F32)
        nm = B1 * m_ref[...] + (1.0 - B1) * gv
        nv = B2 * v_ref[...] + (1.0 - B2) * (gv * gv)
        g_ref[...] = gv
        d_ref[...] = -LR * ((nm / c1) / (jnp.sqrt(nv / c2) + AEPS) + WD * w_ref[...])
        nm_ref[...] = nm
        nv_ref[...] = nv

    blk = pl.BlockSpec((br, cols), lambda i: (i, 0))
    return _call(body, name=name, out_shape=(_sds((rows, cols)),) * 4, grid=(rows // br,),
                 in_specs=[blk, pl.BlockSpec((NDEV, br, cols), lambda i: (0, i, 0)), blk, blk] + [blk] * has_own,
                 out_specs=(blk,) * 4, sem=("parallel",))(w, recv, m, v, *([own] if has_own else []))


P_LAT, P_CTX, P_FNW, P_FFNB, P_CONV, P_FFNW, P_MISC, P_ROWS = 0, 8, 16, 24, 32, 48, 72, 80


def _rows_of(v, nrows):
    flat = v.reshape(-1)
    return jnp.pad(flat, (0, nrows * D - flat.shape[0])).reshape(nrows, D)


def _by_columns(g):
    n, r, c = g.shape
    return jnp.transpose(g, (1, 0, 2)).reshape(r, n * c)


def _to_columns(a):
    r, nc = a.shape
    return jnp.transpose(a.reshape(r, NDEV, nc // NDEV), (1, 0, 2))


def kernel(x, c, ctx, c_ctx, w_mod, b_mod, w_in, q_norm_w, k_norm_w, conv_qkv_w, a_log, dt_bias, gdn_norm_w, w_pa, w_pd, w_out, w_up, ffn_conv_w, ffn_conv_b, w_down, final_norm_w, loss_target, m_c_ctx, m_w_mod, m_b_mod, m_w_in, m_q_norm_w, m_k_norm_w, m_conv_qkv_w, m_a_log, m_dt_bias, m_gdn_norm_w, m_w_pa, m_w_pd, m_w_out, m_w_up, m_ffn_conv_w, m_ffn_conv_b, m_w_down, m_final_norm_w, v_c_ctx, v_w_mod, v_b_mod, v_w_in, v_q_norm_w, v_k_norm_w, v_conv_qkv_w, v_a_log, v_dt_bias, v_gdn_norm_w, v_w_pa, v_w_pd, v_w_out, v_w_up, v_ffn_conv_w, v_ffn_conv_b, v_w_down, v_final_norm_w):
    _, _, _, me = _position()
    mcols = w_mod.shape[2]

    big = {"w_in": w_in[0], "w_pa": w_pa[0], "w_pd": w_pd[0], "w_out": w_out[0], "w_up": w_up[0], "w_down": w_down[0]}
    names = list(big)
    shards = {n: _cast_bf16(big[n], name="cast_" + n) for n in names}
    w_in_g, = _exchange([shards["w_in"]], name="gather_w_in", scatter=False)
    c_all, conv_g, ffnw_g = _exchange([c, conv_qkv_w[0], ffn_conv_w[0]], name="gather_small", scatter=False)
    w_in_full = _by_columns(w_in_g)
    w_in_pad = jnp.concatenate([w_in_full[:, :W_AQ], jnp.zeros((D, C_AQ - W_AQ), BF16), w_in_full[:, W_AQ:]], axis=1)

    c9 = jnp.concatenate([c_all.reshape(NDEV, D), jnp.pad(c_ctx[None], ((0, MODROWS - NDEV - 1), (0, 0)))], axis=0)
    b_loc = lax.dynamic_slice(b_mod, (0, me * mcols), (1, mcols))
    mod_all, = _exchange([_mod_fwd(c9, w_mod[0], b_loc)], name="gather_mod", scatter=False)
    mod_lat = lax.dynamic_index_in_dim(mod_all, me, axis=1, keepdims=False).reshape(6, D)
    mod_ctx = mod_all[:, NDEV, :].reshape(6, D)

    small = {"q_norm_w": q_norm_w, "k_norm_w": k_norm_w, "gdn_norm_w": gdn_norm_w, "a_log": a_log, "dt_bias": dt_bias,
             "conv_qkv_w": _by_columns(conv_g), "ffn_conv_w": _by_columns(ffnw_g), "ffn_conv_b": ffn_conv_b,
             "final_norm_w": final_norm_w[None]}
    loss_me, grad_x, (pending_in, own_in), recv, dmod_lat, dmod_ctx, gs = _local_step(
        x[0], ctx[0], loss_target[0], mod_lat, mod_ctx, w_in_pad, shards, small)

    moments = {"w_in": (m_w_in, v_w_in), "w_pa": (m_w_pa, v_w_pa), "w_pd": (m_w_pd, v_w_pd),
               "w_out": (m_w_out, v_w_out), "w_up": (m_w_up, v_w_up), "w_down": (m_w_down, v_w_down)}
    res = {}
    for n in recv:
        outs = _adamw_recv(big[n], recv[n], moments[n][0][0], moments[n][1][0], name="adamw_" + n)
        res[n] = tuple(t[None] for t in outs)

    misc = jnp.concatenate([gs["q_norm_w"][0], gs["k_norm_w"][0], gs["gdn_norm_w"][0], gs["a_log"], gs["dt_bias"],
                            loss_me[None]])
    pack = jnp.concatenate([_rows_of(dmod_lat, P_CTX - P_LAT), _rows_of(dmod_ctx, P_FNW - P_CTX),
                            _rows_of(gs["final_norm_w"], P_FFNB - P_FNW), _rows_of(gs["ffn_conv_b"], P_CONV - P_FFNB),
                            _rows_of(gs["conv_qkv_w"], P_FFNW - P_CONV), _rows_of(gs["ffn_conv_w"], P_MISC - P_FFNW),
                            _rows_of(misc, P_ROWS - P_MISC)], axis=0)
    pack_all, = _exchange([pack], name="gather_pack", scatter=False)
    tot = _sum_slots(pack_all, name="sum_pack")
    dall = jnp.concatenate([pack_all[:, P_LAT:P_LAT + 6, :].reshape(NDEV, 6 * D),
                            jnp.pad(tot[P_CTX:P_CTX + 6].reshape(1, 6 * D), ((0, MODROWS - NDEV - 1), (0, 0)))], axis=0)
    dmy = lax.dynamic_slice(dall, (0, me * mcols), (MODROWS, mcols))
    g_w_mod, g_b_mod, cpart = _mod_bwd(c9, dmy, dall, w_mod[0])
    cparts, = _exchange([cpart], name="gather_cctx", scatter=False)
    g_c_ctx = _cctx_finish(cparts, c_ctx[None])[0]

    nconv, nffn = 3 * GH * HD, 2 * DFF
    conv_tot = tot[P_CONV:P_FFNW].reshape(-1)[:3 * nconv].reshape(3, nconv)
    ffnw_tot = tot[P_FFNW:P_MISC].reshape(-1)[:3 * nffn].reshape(3, nffn)
    mrow = tot[P_MISC]
    grads = {
        "c_ctx": g_c_ctx, "w_mod": g_w_mod[None], "b_mod": g_b_mod,
        "q_norm_w": mrow[None, 0:HD], "k_norm_w": mrow[None, HD:2 * HD], "gdn_norm_w": mrow[None, 2 * HD:3 * HD],
        "conv_qkv_w": lax.dynamic_slice(conv_tot, (0, me * (nconv // NDEV)), (3, nconv // NDEV))[None],
        "a_log": mrow[3 * HD:3 * HD + 2 * GH].reshape(1, 2, GH),
        "dt_bias": mrow[3 * HD + 2 * GH:3 * HD + 4 * GH].reshape(1, 2, GH),
        "ffn_conv_w": lax.dynamic_slice(ffnw_tot, (0, me * (nffn // NDEV)), (3, nffn // NDEV))[None],
        "ffn_conv_b": tot[P_FFNB:P_CONV].reshape(-1)[:nffn][None],
        "final_norm_w": tot[P_FNW],
    }
    loss = mrow[3 * HD + 4 * GH]
    given = {"c_ctx": (c_ctx, m_c_ctx, v_c_ctx), "w_mod": (w_mod, m_w_mod, v_w_mod), "b_mod": (b_mod, m_b_mod, v_b_mod),
             "q_norm_w": (q_norm_w, m_q_norm_w, v_q_norm_w), "k_norm_w": (k_norm_w, m_k_norm_w, v_k_norm_w),
             "conv_qkv_w": (conv_qkv_w, m_conv_qkv_w, v_conv_qkv_w), "a_log": (a_log, m_a_log, v_a_log),
             "dt_bias": (dt_bias, m_dt_bias, v_dt_bias), "gdn_norm_w": (gdn_norm_w, m_gdn_norm_w, v_gdn_norm_w),
             "ffn_conv_w": (ffn_conv_w, m_ffn_conv_w, v_ffn_conv_w), "ffn_conv_b": (ffn_conv_b, m_ffn_conv_b, v_ffn_conv_b),
             "final_norm_w": (final_norm_w, m_final_norm_w, v_final_norm_w)}
    for n, (w, m, v) in given.items():
        res[n] = (grads[n],) + _adamw(w, grads[n], m, v, name="adamw_" + n)

    land = _scatter_wait(*pending_in, [res[n][1] for n in res], name="scatter_g_in_wait")
    outs = _adamw_recv(big["w_in"], land, m_w_in[0], v_w_in[0], name="adamw_w_in", own=own_in)
    res["w_in"] = tuple(t[None] for t in outs)

    order = ["c_ctx", "w_mod", "b_mod", "w_in", "q_norm_w", "k_norm_w", "conv_qkv_w", "a_log", "dt_bias", "gdn_norm_w",
             "w_pa", "w_pd", "w_out", "w_up", "ffn_conv_w", "ffn_conv_b", "w_down", "final_norm_w"]
    return (loss, grad_x[None], *[res[n][0] for n in order], *[res[n][1] for n in order],
            *[res[n][2] for n in order], *[res[n][3] for n in order])
```

```python
import functools
import math

import jax
import jax.numpy as jnp
from jax import lax
from jax.experimental import pallas as pl
from jax.experimental.pallas import tpu as pltpu

F32 = jnp.float32
BF16 = jnp.bfloat16
HI = lax.Precision.HIGHEST
MESH = pl.DeviceIdType.MESH

NDEV = 8
D = 1024
HD = 128
AH, AKV, GRP = 8, 2, 4
GH = 8
CH = 64
DFF = 2816
GRID_W = 64
EPS = 1e-6
ROPE_THETA = 10000.0
C_KV, C_QKV, C_BL, C_AQ, C_Z, C_GATE, C_END = 0, 512, 3584, 4096, 5120, 6144, 8192
W_BL, W_AQ, W_END = 3584, 3616, 7712
LR, B1, B2, AEPS, WD, STEP = 0.001, 0.9, 0.999, 1e-08, 0.01, 10
VMEM_BIG = 56 * 1024 * 1024
INTRA_FWD_CHUNKS = 18
INTRA_BWD_CHUNKS = 12


def _call(body, *, name, out_shape, grid=None, in_specs=None, out_specs=None, scratch=(), sem=None,
          vmem=None, aliases=None):
    params = {}
    if sem is not None:
        params["dimension_semantics"] = sem
    if vmem is not None:
        params["vmem_limit_bytes"] = vmem
    kw = {}
    if grid is not None:
        kw["grid"] = grid
    if in_specs is not None:
        kw["in_specs"] = in_specs
    if out_specs is not None:
        kw["out_specs"] = out_specs
    if aliases:
        kw["input_output_aliases"] = aliases
    return pl.pallas_call(body, name=name, out_shape=out_shape, scratch_shapes=list(scratch),
                          compiler_params=pltpu.CompilerParams(**params), **kw)


def _call_carrying(body, exch, *, name, out_shape, grid, in_specs, out_specs, scratch=(), vmem=None):
    n, nin, nout, nscr = exch.n, len(in_specs), len(out_shape), len(scratch)

    def wrapped(*refs):
        ins, cins = refs[:nin], refs[nin:nin + n]
        outs, couts = refs[nin + n:nin + n + nout], refs[nin + n + nout:nin + 2 * n + nout]
        scr, sems = refs[nin + 2 * n + nout:nin + 2 * n + nout + nscr], refs[nin + 2 * n + nout + nscr:]
        ids = [pl.program_id(i) for i in range(len(grid))]
        first = functools.reduce(jnp.logical_and, [i == 0 for i in ids])
        last = functools.reduce(jnp.logical_and, [i == g - 1 for i, g in zip(ids, grid)])

        @pl.when(first)
        def _():
            exch.start(cins, couts, sems)

        body(*ins, *outs, *scr)

        @pl.when(last)
        def _():
            exch.finish(cins, couts, sems)

    params = {"dimension_semantics": ("arbitrary",) * len(grid)}
    if vmem is not None:
        params["vmem_limit_bytes"] = vmem
    fn = pl.pallas_call(wrapped, name=name, out_shape=tuple(out_shape) + exch.out_shape, grid=grid,
                        in_specs=list(in_specs) + [HBM] * n, out_specs=tuple(out_specs) + (HBM,) * n,
                        scratch_shapes=list(scratch) + exch.scratch, compiler_params=pltpu.CompilerParams(**params))

    def run(*args):
        res = fn(*args, *exch.arrs)
        return res[:nout], list(res[nout:])

    return run


def _sds(shape, dtype=F32):
    return jax.ShapeDtypeStruct(tuple(shape), dtype)


def _dot(a, b, ca, cb):
    return lax.dot_general(a.astype(BF16), b.astype(BF16), (((ca,), (cb,)), ((), ())),
                           preferred_element_type=F32)


@jax.custom_vjp
def _nn(a, b):
    return _dot(a, b, 1, 0)


@jax.custom_vjp
def _nt(a, b):
    return _dot(a, b, 1, 1)


@jax.custom_vjp
def _tn(a, b):
    return _dot(a, b, 0, 0)


_nn.defvjp(lambda a, b: (_nn(a, b), (a, b)), lambda r, g: (_nt(g, r[1]), _tn(r[0], g)))
_nt.defvjp(lambda a, b: (_nt(a, b), (a, b)), lambda r, g: (_nn(g, r[1]), _tn(g, r[0])))
_tn.defvjp(lambda a, b: (_tn(a, b), (a, b)), lambda r, g: (_nt(r[1], g), _nn(r[0], g)))


def _hdot(a, b):
    return jnp.dot(a, b, precision=HI, preferred_element_type=F32)


def _mdot(a, b):
    return jnp.dot(a, b, precision=lax.Precision.HIGH, preferred_element_type=F32)


def _maskdot(mask, a, cm):
    hi = a.astype(BF16)
    r = a - hi.astype(F32)
    mid = r.astype(BF16)
    lo = (r - mid.astype(F32)).astype(BF16)
    mb = mask.astype(BF16)
    dims = (((cm,), (0,)), ((), ()))
    return (lax.dot_general(mb, hi, dims, preferred_element_type=F32)
            + lax.dot_general(mb, mid, dims, preferred_element_type=F32)
            + lax.dot_general(mb, lo, dims, preferred_element_type=F32))


@jax.custom_vjp
def _mask_nn(mask, a):
    return _maskdot(mask, a, 1)


_mask_nn.defvjp(lambda mask, a: (_maskdot(mask, a, 1), mask),
                lambda mask, g: (jnp.zeros_like(mask), _maskdot(mask, g, 0)))


@jax.custom_vjp
def _saved_inverse(lmat, x):
    return x


def _saved_inverse_bwd(x, g):
    t = lax.dot_general(x, g, (((0,), (0,)), ((), ())), precision=lax.Precision.HIGH, preferred_element_type=F32)
    dl = lax.dot_general(t, x, (((1,), (1,)), ((), ())), precision=lax.Precision.HIGH, preferred_element_type=F32)
    return -dl, jnp.zeros_like(x)


_saved_inverse.defvjp(lambda lmat, x: (x, x), _saved_inverse_bwd)


def _row_ids(shape):
    return lax.broadcasted_iota(jnp.int32, shape, 0)


def _shift_rows(x, down, bounds):
    n = x.shape[0]
    rows = _row_ids(x.shape)
    y = pltpu.roll(x, 1 if down else n - 1, 0)
    edge = functools.reduce(jnp.logical_or, [rows == (s if down else e - 1) for s, e in bounds])
    return jnp.where(edge, 0.0, y)


def _make_shift(bounds):
    @jax.custom_vjp
    def down(x):
        return _shift_rows(x, True, bounds)

    @jax.custom_vjp
    def up(x):
        return _shift_rows(x, False, bounds)

    down.defvjp(lambda x: (down(x), None), lambda _, g: (up(g),))
    up.defvjp(lambda x: (up(x), None), lambda _, g: (down(g),))
    return down, up


@jax.custom_vjp
def _swap32(x):
    lane = lax.broadcasted_iota(jnp.int32, x.shape, x.ndim - 1)
    return jnp.where((lane % 64) < 32, pltpu.roll(x, HD - 32, x.ndim - 1), pltpu.roll(x, 32, x.ndim - 1))


_swap32.defvjp(lambda x: (_swap32(x), None), lambda _, g: (_swap32(g),))


def _rms(x):
    return x * lax.rsqrt(jnp.mean(x * x, axis=-1, keepdims=True) + EPS)


def _silu(x):
    return x * jax.nn.sigmoid(x)


def _mm(a, b, *, name, M, N, K, ta=False, tb=False, out_dtype=F32, bm=None, bn=None, bk=None,
        a_off=(0, 0), b_off=(0, 0), after=()):
    bm, bn, bk = bm or M, bn or N, bk or K
    assert M % bm == 0 and N % bn == 0 and K % bk == 0, (name, M, N, K, bm, bn, bk)
    nk = K // bk
    ca, cb = (0 if ta else 1), (1 if tb else 0)
    na = len(after)

    def body(a_ref, b_ref, *rest):
        o_ref, acc = rest[na], rest[na + 1:]
        r = _dot(a_ref[...], b_ref[...], ca, cb)
        if nk == 1:
            o_ref[...] = r.astype(out_dtype)
        else:
            acc_ref, = acc
            k = pl.program_id(2)

            @pl.when(k == 0)
            def _():
                acc_ref[...] = r

            @pl.when(k > 0)
            def _():
                acc_ref[...] += r

            @pl.when(k == nk - 1)
            def _():
                o_ref[...] = acc_ref[...].astype(out_dtype)

    def blk(off, bshape):
        assert off[0] % bshape[0] == 0 and off[1] % bshape[1] == 0, (name, off, bshape)
        return off[0] // bshape[0], off[1] // bshape[1]

    if ta:
        ao = blk(a_off, (bk, bm))
        a_spec = pl.BlockSpec((bk, bm), lambda i, j, k: (k + ao[0], i + ao[1]))
    else:
        ao = blk(a_off, (bm, bk))
        a_spec = pl.BlockSpec((bm, bk), lambda i, j, k: (i + ao[0], k + ao[1]))
    if tb:
        bo = blk(b_off, (bn, bk))
        b_spec = pl.BlockSpec((bn, bk), lambda i, j, k: (j + bo[0], k + bo[1]))
    else:
        bo = blk(b_off, (bk, bn))
        b_spec = pl.BlockSpec((bk, bn), lambda i, j, k: (k + bo[0], j + bo[1]))
    return _call(body, name=name, out_shape=_sds((M, N), out_dtype), grid=(M // bm, N // bn, nk),
                 in_specs=[a_spec, b_spec] + [pl.BlockSpec(memory_space=pl.ANY)] * na,
                 out_specs=pl.BlockSpec((bm, bn), lambda i, j, k: (i, j)),
                 scratch=[pltpu.VMEM((bm, bn), F32)] if nk > 1 else [],
                 sem=("parallel", "parallel", "arbitrary"), vmem=VMEM_BIG)(a, b, *after)


def _normmod_fn(x, sh, sc):
    return _rms(x) * (1.0 + sc) + sh


def _normmod_fwd(x, mod, i_sh, i_sc, *, name, br=256):
    R = x.shape[0]

    def body(x_ref, mod_ref, o_ref):
        o_ref[...] = _normmod_fn(x_ref[...], mod_ref[i_sh:i_sh + 1, :], mod_ref[i_sc:i_sc + 1, :]).astype(BF16)

    return _call(body, name=name, out_shape=_sds((R, D), BF16), grid=(R // br,),
                 in_specs=[pl.BlockSpec((br, D), lambda i: (i, 0)), pl.BlockSpec((6, D), lambda i: (0, 0))],
                 out_specs=pl.BlockSpec((br, D), lambda i: (i, 0)), sem=("parallel",))(x, mod)


def _normmod_bwd(x, mod, i_sh, i_sc, dh, dh_off, res, *, name, br=256):
    R = x.shape[0]
    ob = dh_off // br
    has_res = res is not None

    def body(x_ref, mod_ref, dh_ref, *rest):
        if has_res:
            res_ref, dx_ref, dsh_ref, dsc_ref = rest
        else:
            dx_ref, dsh_ref, dsc_ref = rest
        sh, sc = mod_ref[i_sh:i_sh + 1, :], mod_ref[i_sc:i_sc + 1, :]
        _, vjp = jax.vjp(_normmod_fn, x_ref[...], sh, sc)
        dx, dsh, dsc = vjp(dh_ref[...])
        dx_ref[...] = dx + res_ref[...] if has_res else dx

        @pl.when(pl.program_id(0) == 0)
        def _():
            dsh_ref[...] = jnp.zeros_like(dsh_ref)
            dsc_ref[...] = jnp.zeros_like(dsc_ref)

        dsh_ref[...] += dsh
        dsc_ref[...] += dsc

    row = pl.BlockSpec((br, D), lambda i: (i, 0))
    vec = pl.BlockSpec((1, D), lambda i: (0, 0))
    ins = [row, pl.BlockSpec((6, D), lambda i: (0, 0)), pl.BlockSpec((br, D), lambda i: (i + ob, 0))]
    args = [x, mod, dh]
    if has_res:
        ins.append(row)
        args.append(res)
    return _call(body, name=name, out_shape=(_sds((R, D)), _sds((1, D)), _sds((1, D))), grid=(R // br,),
                 in_specs=ins, out_specs=(row, vec, vec), sem=("arbitrary",))(*args)


def _rope(x, cos, sin):
    return x * cos + _swap32(x) * sin


def _aprep_fn(qs, ks, cos, sin, qw, kw):
    return ([_rope(_rms(q) * qw, cos, sin) for q in qs], [_rope(_rms(k) * kw, cos, sin) for k in ks])


def _aprep_fwd(proj, cos, sin, qw, kw, *, br=256):
    T = proj.shape[0]

    def body(aq_ref, kv_ref, cos_ref, sin_ref, qw_ref, kw_ref, q_ref, k_ref, v_ref):
        qs = [aq_ref[:, h * HD:(h + 1) * HD] for h in range(AH)]
        ks = [kv_ref[:, h * HD:(h + 1) * HD] for h in range(AKV)]
        qo, ko = _aprep_fn(qs, ks, cos_ref[...], sin_ref[...], qw_ref[...], kw_ref[...])
        for h in range(AH):
            q_ref[h] = qo[h].astype(BF16)
        for h in range(AKV):
            k_ref[h] = ko[h].astype(BF16)
            v_ref[h] = kv_ref[:, (AKV + h) * HD:(AKV + h + 1) * HD].astype(BF16)

    tab = pl.BlockSpec((br, HD), lambda i: (i, 0))
    vec = pl.BlockSpec((1, HD), lambda i: (0, 0))
    return _call(body, name="aprep_fwd",
                 out_shape=(_sds((AH, T, HD), BF16), _sds((AKV, T, HD), BF16), _sds((AKV, T, HD), BF16)),
                 grid=(T // br,),
                 in_specs=[pl.BlockSpec((br, AH * HD), lambda i: (i, C_AQ // (AH * HD))),
                           pl.BlockSpec((br, 2 * AKV * HD), lambda i: (i, 0)), tab, tab, vec, vec],
                 out_specs=(pl.BlockSpec((AH, br, HD), lambda i: (0, i, 0)),
                            pl.BlockSpec((AKV, br, HD), lambda i: (0, i, 0)),
                            pl.BlockSpec((AKV, br, HD), lambda i: (0, i, 0))),
                 sem=("parallel",))(proj, proj, cos, sin, qw, kw)


def _aprep_bwd(proj, cos, sin, qw, kw, dq, dk, dv, L, *, br=256):
    T = proj.shape[0]
    lb = L // br

    def body(aq_ref, kv_ref, cos_ref, sin_ref, qw_ref, kw_ref, dq_ref, dk_ref, dv_ref,
             daq_ref, dkv_ref, dqw_ref, dkw_ref):
        i = pl.program_id(0)
        qs = [aq_ref[:, h * HD:(h + 1) * HD] for h in range(AH)]
        ks = [kv_ref[:, h * HD:(h + 1) * HD] for h in range(AKV)]
        _, vjp = jax.vjp(_aprep_fn, qs, ks, cos_ref[...], sin_ref[...], qw_ref[...], kw_ref[...])
        is_lat = i >= lb
        dqs = [jnp.where(is_lat, dq_ref[h], 0.0) for h in range(AH)]
        dks = [dk_ref[h] for h in range(AKV)]
        gq, gk, _, _, gqw, gkw = vjp((dqs, dks))
        for h in range(AH):
            daq_ref[:, h * HD:(h + 1) * HD] = gq[h].astype(BF16)
        for h in range(AKV):
            dkv_ref[:, h * HD:(h + 1) * HD] = gk[h].astype(BF16)
            dkv_ref[:, (AKV + h) * HD:(AKV + h + 1) * HD] = dv_ref[h].astype(BF16)

        @pl.when(i == 0)
        def _():
            dqw_ref[...] = jnp.zeros_like(dqw_ref)
            dkw_ref[...] = jnp.zeros_like(dkw_ref)

        dqw_ref[...] += gqw
        dkw_ref[...] += gkw

    tab = pl.BlockSpec((br, HD), lambda i: (i, 0))
    vec = pl.BlockSpec((1, HD), lambda i: (0, 0))
    kvb = pl.BlockSpec((AKV, br, HD), lambda i: (0, i, 0))
    return _call(body, name="aprep_bwd",
                 out_shape=(_sds((T, AH * HD), BF16), _sds((T, 2 * AKV * HD), BF16), _sds((1, HD)), _sds((1, HD))),
                 grid=(T // br,),
                 in_specs=[pl.BlockSpec((br, AH * HD), lambda i: (i, C_AQ // (AH * HD))),
                           pl.BlockSpec((br, 2 * AKV * HD), lambda i: (i, 0)), tab, tab, vec, vec,
                           pl.BlockSpec((AH, br, HD), lambda i: (0, jnp.maximum(i - lb, 0), 0)), kvb, kvb],
                 out_specs=(pl.BlockSpec((br, AH * HD), lambda i: (i, 0)),
                            pl.BlockSpec((br, 2 * AKV * HD), lambda i: (i, 0)), vec, vec),
                 sem=("arbitrary",))(proj, proj, cos, sin, qw, kw, dq, dk, dv)


def _attn_fn(q, k, v):
    s = _nt(q, k) * (HD ** -0.5)
    m = lax.stop_gradient(jnp.max(s, axis=-1, keepdims=True))
    e = jnp.exp(s - m)
    p = e / jnp.sum(e, axis=-1, keepdims=True)
    return _nn(p, v)


def _attn_fwd(q, k, v, L, exch, *, bq=128):
    T = q.shape[1]
    N = T - L
    lb = L // bq

    def body(q_ref, k_ref, v_ref, o_ref):
        qv = q_ref[...].reshape(GRP * bq, HD).astype(F32)
        o = _attn_fn(qv, k_ref[...].astype(F32), v_ref[...].astype(F32))
        for g in range(GRP):
            o_ref[:, g * HD:(g + 1) * HD] = o[g * bq:(g + 1) * bq].astype(BF16)

    kvb = pl.BlockSpec((None, T, HD), lambda g, i: (g, 0, 0))
    (attn,), moved = _call_carrying(
        body, exch, name="attn_fwd", out_shape=(_sds((N, AH * HD), BF16),), grid=(AKV, N // bq),
        in_specs=[pl.BlockSpec((GRP, bq, HD), lambda g, i: (g, i + lb, 0)), kvb, kvb],
        out_specs=(pl.BlockSpec((bq, GRP * HD), lambda g, i: (i, g)),), vmem=VMEM_BIG)(q, k, v)
    return attn, moved


def _attn_bwd(q, k, v, do, L, *, bq=128):
    T = q.shape[1]
    N = T - L
    lb = L // bq

    def body(q_ref, k_ref, v_ref, do_ref, dq_ref, dk_ref, dv_ref):
        qv = q_ref[...].reshape(GRP * bq, HD).astype(F32)
        _, vjp = jax.vjp(_attn_fn, qv, k_ref[...].astype(F32), v_ref[...].astype(F32))
        dov = jnp.concatenate([do_ref[:, g * HD:(g + 1) * HD] for g in range(GRP)], axis=0)
        dq, dk, dv = vjp(dov)
        dq_ref[...] = dq.reshape(GRP, bq, HD)

        @pl.when(pl.program_id(1) == 0)
        def _():
            dk_ref[...] = jnp.zeros_like(dk_ref)
            dv_ref[...] = jnp.zeros_like(dv_ref)

        dk_ref[...] += dk
        dv_ref[...] += dv

    kvb = pl.BlockSpec((None, T, HD), lambda g, i: (g, 0, 0))
    return _call(body, name="attn_bwd",
                 out_shape=(_sds((AH, N, HD)), _sds((AKV, T, HD)), _sds((AKV, T, HD))), grid=(AKV, N // bq),
                 in_specs=[pl.BlockSpec((GRP, bq, HD), lambda g, i: (g, i + lb, 0)), kvb, kvb,
                           pl.BlockSpec((bq, GRP * HD), lambda g, i: (i, g))],
                 out_specs=(pl.BlockSpec((GRP, bq, HD), lambda g, i: (g, i, 0)), kvb, kvb),
                 sem=("parallel", "arbitrary"), vmem=VMEM_BIG)(q, k, v, do)


def _gprep_fn(kind, shifts, x, w):
    down, up = shifts
    y = down(x) * w[0:1, :] + x * w[1:2, :] + up(x) * w[2:3, :]
    a = _silu(y)
    if kind == 2:
        return a
    a = a * lax.rsqrt(jnp.sum(a * a, axis=-1, keepdims=True) + EPS)
    return a * (HD ** -0.5) if kind == 0 else a


def _gprep_fwd(proj, conv_w, kind, bounds):
    T = proj.shape[0]
    shifts = _make_shift(bounds)
    cb = C_QKV // HD + kind * GH

    def body(x_ref, w_ref, o_ref):
        o_ref[...] = _gprep_fn(kind, shifts, x_ref[...], w_ref[...])

    return _call(body, name=f"gprep_fwd{kind}", out_shape=_sds((GH, T, HD)), grid=(GH,),
                 in_specs=[pl.BlockSpec((T, HD), lambda h: (0, cb + h)),
                           pl.BlockSpec((3, HD), lambda h: (0, kind * GH + h))],
                 out_specs=pl.BlockSpec((None, T, HD), lambda h: (h, 0, 0)), sem=("parallel",))(proj, conv_w)


def _gprep_bwd(proj, conv_w, kind, bounds, dy):
    T = proj.shape[0]
    shifts = _make_shift(bounds)
    cb = C_QKV // HD + kind * GH

    def body(x_ref, w_ref, dy_ref, dx_ref, dw_ref):
        _, vjp = jax.vjp(functools.partial(_gprep_fn, kind, shifts), x_ref[...], w_ref[...])
        dx, dw = vjp(dy_ref[0] + dy_ref[1])
        dx_ref[...] = dx.astype(BF16)
        dw_ref[...] = dw

    return _call(body, name=f"gprep_bwd{kind}", out_shape=(_sds((T, GH * HD), BF16), _sds((3, GH * HD))), grid=(GH,),
                 in_specs=[pl.BlockSpec((T, HD), lambda h: (0, cb + h)),
                           pl.BlockSpec((3, HD), lambda h: (0, kind * GH + h)),
                           pl.BlockSpec((2, None, T, HD), lambda h: (0, h, 0, 0))],
                 out_specs=(pl.BlockSpec((T, HD), lambda h: (0, h)), pl.BlockSpec((3, HD), lambda h: (0, h))),
                 sem=("parallel",))(proj, conv_w, dy)


def _bl_fn(x, alog, dtb):
    lane = lax.broadcasted_iota(jnp.int32, x.shape, 1)
    beta = jax.nn.sigmoid(x)
    z = x + dtb
    sp = jnp.maximum(z, 0.0) + jnp.log1p(jnp.exp(-jnp.abs(z)))
    la = -jnp.exp(alog) * sp
    return jnp.where(lane < 2 * GH, beta, jnp.where(lane < 4 * GH, la, 0.0))


def _bl_fwd(proj, alog, dtb, *, br=256):
    T = proj.shape[0]

    def body(x_ref, a_ref, d_ref, o_ref):
        o_ref[...] = _bl_fn(x_ref[...], a_ref[...], d_ref[...])

    vec = pl.BlockSpec((1, HD), lambda i: (0, 0))
    return _call(body, name="bl_fwd", out_shape=_sds((T, HD)), grid=(T // br,),
                 in_specs=[pl.BlockSpec((br, HD), lambda i: (i, C_BL // HD)), vec, vec],
                 out_specs=pl.BlockSpec((br, HD), lambda i: (i, 0)), sem=("parallel",))(proj, alog, dtb)


def _bl_bwd(proj, alog, dtb, dbl, *, br=256):
    T = proj.shape[0]

    def body(x_ref, a_ref, d_ref, g_ref, dx_ref, da_ref, dd_ref):
        g = g_ref[0, 0]
        for d in range(2):
            for h in range(GH):
                if d or h:
                    g = g + g_ref[d, h]
        _, vjp = jax.vjp(_bl_fn, x_ref[...], a_ref[...], d_ref[...])
        dx, da, dd = vjp(g)
        dx_ref[...] = dx.astype(BF16)

        @pl.when(pl.program_id(0) == 0)
        def _():
            da_ref[...] = jnp.zeros_like(da_ref)
            dd_ref[...] = jnp.zeros_like(dd_ref)

        da_ref[...] += da
        dd_ref[...] += dd

    vec = pl.BlockSpec((1, HD), lambda i: (0, 0))
    return _call(body, name="bl_bwd", out_shape=(_sds((T, HD), BF16), _sds((1, HD)), _sds((1, HD))), grid=(T // br,),
                 in_specs=[pl.BlockSpec((br, HD), lambda i: (i, C_BL // HD)), vec, vec,
                           pl.BlockSpec((2, GH, br, HD), lambda i: (0, 0, i, 0))],
                 out_specs=(pl.BlockSpec((br, HD), lambda i: (i, 0)), vec, vec), sem=("arbitrary",))(proj, alog, dtb, dbl)


def _chunk_masks(d):
    ii = lax.broadcasted_iota(jnp.int32, (CH, CH), 0)
    jj = lax.broadcasted_iota(jnp.int32, (CH, CH), 1)
    eye = (ii == jj).astype(F32)
    before = jnp.where(d == 0, (jj < ii).astype(F32), (jj > ii).astype(F32))
    return before, before + eye, eye


def _intra_fn(masks, sel_b, sel_l, qs, ks, vs, bls, xs=None):
    before, ateq, eye = masks
    ones = jnp.ones((CH, CH), F32)
    inc = ateq > 0.0
    each = lambda f, *ls: [f(*t) for t in zip(*ls)]
    beta = each(lambda bl: jnp.sum(bl * sel_b, axis=-1, keepdims=True), bls)
    la = each(lambda bl: jnp.sum(bl * sel_l, axis=-1, keepdims=True), bls)
    gam = each(lambda a: _mask_nn(ateq, jnp.broadcast_to(a, (CH, HD))), la)
    gi = each(lambda a: _mask_nn(ateq, jnp.broadcast_to(a, (CH, CH))), la)
    gj = each(lambda g: _mask_nn(ones, eye * g), gi)
    kk = each(lambda k: _nt(k, k), ks)
    qk = each(_nt, qs, ks)
    dec = each(lambda a, b: jnp.where(inc, jnp.exp(jnp.where(inc, a - b, 0.0)), 0.0), gi, gj)
    lmat = each(lambda b, d, m: before * (b * d * m), beta, dec, kk)
    if xs is None:
        x = each(lambda m: eye - m, lmat)
        p2 = each(lambda m: _mdot(m, m), lmat)
        for it in range(5):
            x = each(lambda a, b: a + _mdot(a, b), x, p2)
            if it < 4:
                p2 = each(lambda m: _mdot(m, m), p2)
    else:
        x = each(_saved_inverse, lmat, xs)
    eg = each(jnp.exp, gam)
    u = each(lambda a, b, v: _mdot(a, b * v), x, beta, vs)
    w = each(lambda a, b, e, k: _mdot(a, (b * e) * k), x, beta, eg, ks)
    tot = each(lambda a: jnp.sum(a, axis=0, keepdims=True), la)
    kd = each(lambda k, t, g: k * jnp.exp(t - g), ks, tot, gam)
    gl = each(lambda t: jnp.broadcast_to(jnp.exp(t), (1, HD)), tot)
    qd = each(lambda q, e: q * e, qs, eg)
    p = each(lambda d, m: d * m, dec, qk)
    return (u, w, kd, qd, p, gl, x) if xs is None else (u, w, kd, qd, p, gl)


def _dir_head_sel(d, h):
    lane = lax.broadcasted_iota(jnp.int32, (1, HD), 1)
    return (lane == d * GH + h).astype(F32), (lane == 2 * GH + d * GH + h).astype(F32)


def _intra_specs(T, G):
    nc = T // CH
    assert nc % G == 0
    qkv = pl.BlockSpec((None, G * CH, HD), lambda d, h, c: (h, c, 0))
    bl = pl.BlockSpec((G * CH, HD), lambda d, h, c: (c, 0))
    big = pl.BlockSpec((None, None, G * CH, HD), lambda d, h, c: (d, h, c, 0))
    pm = pl.BlockSpec((None, None, G * CH, CH), lambda d, h, c: (d, h, c, 0))
    gl = pl.BlockSpec((None, None, G, 1, HD), lambda d, h, c: (d, h, c, 0, 0))
    shapes = (_sds((2, GH, T, HD)),) + (_sds((2, GH, T, HD), BF16),) * 3 + (
        _sds((2, GH, T, CH), BF16), _sds((2, GH, nc, 1, HD)), _sds((2, GH, T, CH)))
    return nc, qkv, bl, big, pm, gl, shapes


def _chunks_per_step(T, most):
    nc = T // CH
    return max(g for g in range(1, most + 1) if nc % g == 0)


def _intra_fwd(q, k, v, bl, exch):
    T = q.shape[1]
    G = _chunks_per_step(T, INTRA_FWD_CHUNKS)
    nc, qkv_s, bl_s, big, pm, gl_s, shapes = _intra_specs(T, G)

    def body(q_ref, k_ref, v_ref, bl_ref, u_ref, w_ref, kd_ref, qd_ref, p_ref, gl_ref, x_ref):
        d, h = pl.program_id(0), pl.program_id(1)
        sb, sl = _dir_head_sel(d, h)
        rows = [slice(g * CH, (g + 1) * CH) for g in range(G)]
        outs = _intra_fn(_chunk_masks(d), sb, sl, *[[r[s, :] for s in rows] for r in (q_ref, k_ref, v_ref, bl_ref)])
        for g in range(G):
            for r, o in zip((u_ref, w_ref, kd_ref, qd_ref, p_ref, x_ref), outs[:5] + outs[6:]):
                r[rows[g], :] = o[g].astype(r.dtype)
            gl_ref[g] = outs[5][g]

    return _call_carrying(body, exch, name="gdn_intra_fwd", out_shape=shapes, grid=(2, GH, nc // G),
                          in_specs=[qkv_s, qkv_s, qkv_s, bl_s], out_specs=(big, big, big, big, pm, gl_s, pm))(q, k, v, bl)


def _intra_bwd(q, k, v, bl, xinv, cts, exch):
    T = q.shape[1]
    G = _chunks_per_step(T, INTRA_BWD_CHUNKS)
    nc, qkv_s, bl_s, big, pm, gl_s, _ = _intra_specs(T, G)

    def body(q_ref, k_ref, v_ref, bl_ref, x_ref, du, dw, dkd, dqd, dp, dgl, dq_ref, dk_ref, dv_ref, dbl_ref):
        d, h = pl.program_id(0), pl.program_id(1)
        sb, sl = _dir_head_sel(d, h)
        rows = [slice(g * CH, (g + 1) * CH) for g in range(G)]
        fn = functools.partial(_intra_fn, _chunk_masks(d), sb, sl, xs=[x_ref[s, :] for s in rows])
        _, vjp = jax.vjp(fn, *[[r[s, :] for s in rows] for r in (q_ref, k_ref, v_ref, bl_ref)])
        cts = tuple([r[s, :] for s in rows] for r in (du, dw, dkd, dqd, dp)) + ([dgl[g] for g in range(G)],)
        grads = vjp(cts)
        for g in range(G):
            for r, o in zip((dq_ref, dk_ref, dv_ref, dbl_ref), grads):
                r[rows[g], :] = o[g]

    return _call_carrying(body, exch, name="gdn_intra_bwd", out_shape=(_sds((2, GH, T, HD)),) * 4,
                          grid=(2, GH, nc // G), in_specs=[qkv_s, qkv_s, qkv_s, bl_s, pm, big, big, big, big, pm, gl_s],
                          out_specs=(big,) * 4)(q, k, v, bl, xinv, *cts)


def _scan_fn(s, u, w, kd, qd, p, gl):
    each = lambda f, *ls: [f(*t) for t in zip(*ls)]
    ws = each(_nn, w, s)
    delta = each(lambda a, b: a - b, u, ws)
    kdd = each(_tn, kd, delta)
    s_new = each(lambda g, a, b: g * a + b, gl, s, kdd)
    qs = each(_nn, qd, s)
    pd = each(_nn, p, delta)
    return each(lambda a, b: a + b, qs, pd), s_new


SCAN_BLOCK = 4


def _scan_visit(t, d, nb, ncb):
    rev = jnp.where(t < ncb, ncb - 1 - t, nb - 1 - (t - ncb))
    return jnp.where(d == 0, t, rev)


def _scan_specs(T, L, back):
    tb = SCAN_BLOCK * CH
    assert T % tb == 0 and L % tb == 0
    nb, ncb = T // tb, L // tb

    def at(d, t):
        return _scan_visit(nb - 1 - t if back else t, d, nb, ncb)

    big = pl.BlockSpec((None, GH, tb, HD), lambda d, t: (d, 0, at(d, t), 0))
    pm = pl.BlockSpec((None, GH, tb, CH), lambda d, t: (d, 0, at(d, t), 0))
    gl = pl.BlockSpec((None, GH, SCAN_BLOCK, 1, HD), lambda d, t: (d, 0, at(d, t), 0, 0))
    st = pl.BlockSpec((None, GH, SCAN_BLOCK, HD, HD), lambda d, t: (d, 0, at(d, t), 0, 0))
    do = pl.BlockSpec((GH, tb, HD), lambda d, t: (0, at(d, t), 0))
    return nb, big, pm, gl, st, do


def _scan_fwd(u, w, kd, qd, p, gl, L):
    T = u.shape[2]
    nb, big, pm, gl_s, st, _ = _scan_specs(T, L, False)
    heads = range(GH)

    def body(u_ref, w_ref, kd_ref, qd_ref, p_ref, gl_ref, o_ref, st_ref, s_scr):
        d = pl.program_id(0)

        @pl.when(pl.program_id(1) == 0)
        def _():
            s_scr[...] = jnp.zeros_like(s_scr)

        s = [s_scr[h] for h in heads]
        for i in range(SCAN_BLOCK):
            c = jnp.where(d == 0, i, SCAN_BLOCK - 1 - i)
            rows = pl.ds(pl.multiple_of(c * CH, CH), CH)
            for h in heads:
                st_ref[h, c] = s[h]
            o, s = _scan_fn(s, *[[r[h, rows, :].astype(F32) for h in heads] for r in (u_ref, w_ref, kd_ref, qd_ref, p_ref)],
                            [gl_ref[h, c] for h in heads])
            for h in heads:
                o_ref[h, rows, :] = o[h]
        for h in heads:
            s_scr[h] = s[h]

    return _call(body, name="gdn_scan_fwd", out_shape=(_sds((2, GH, T, HD)), _sds((2, GH, T // CH, HD, HD))),
                 grid=(2, nb), in_specs=[big, big, big, big, pm, gl_s], out_specs=(big, st),
                 scratch=[pltpu.VMEM((GH, HD, HD), F32)], sem=("parallel", "arbitrary"))(u, w, kd, qd, p, gl)


def _scan_bwd(u, w, kd, qd, p, gl, states, do, L):
    T = u.shape[2]
    nb, big, pm, gl_s, st, do_s = _scan_specs(T, L, True)
    heads = range(GH)

    def body(u_ref, w_ref, kd_ref, qd_ref, p_ref, gl_ref, st_ref, do_ref,
             du_ref, dw_ref, dkd_ref, dqd_ref, dp_ref, dgl_ref, ds_scr):
        d = pl.program_id(0)

        @pl.when(pl.program_id(1) == 0)
        def _():
            ds_scr[...] = jnp.zeros_like(ds_scr)

        ds = [ds_scr[h] for h in heads]
        for i in range(SCAN_BLOCK):
            c = jnp.where(d == 0, SCAN_BLOCK - 1 - i, i)
            rows = pl.ds(pl.multiple_of(c * CH, CH), CH)
            _, vjp = jax.vjp(_scan_fn, [st_ref[h, c] for h in heads],
                             *[[r[h, rows, :].astype(F32) for h in heads] for r in (u_ref, w_ref, kd_ref, qd_ref, p_ref)],
                             [gl_ref[h, c] for h in heads])
            ds, gu, gw, gkd, gqd, gp, ggl = vjp(([do_ref[h, rows, :] for h in heads], ds))
            for h in heads:
                du_ref[h, rows, :] = gu[h]
                dw_ref[h, rows, :] = gw[h]
                dkd_ref[h, rows, :] = gkd[h]
                dqd_ref[h, rows, :] = gqd[h]
                dp_ref[h, rows, :] = gp[h]
                dgl_ref[h, c] = ggl[h]
        for h in heads:
            ds_scr[h] = ds[h]

    return _call(body, name="gdn_scan_bwd",
                 out_shape=(_sds((2, GH, T, HD)),) * 4 + (_sds((2, GH, T, CH)), _sds((2, GH, T // CH, 1, HD))),
                 grid=(2, nb), in_specs=[big, big, big, big, pm, gl_s, st, do_s],
                 out_specs=(big, big, big, big, pm, gl_s), scratch=[pltpu.VMEM((GH, HD, HD), F32)],
                 sem=("parallel", "arbitrary"))(u, w, kd, qd, p, gl, states, do)


def _gout_fn(o0, o1, z, gw):
    return _rms(o0 + o1) * gw * _silu(z)


def _gout_fwd(o, proj, gw, L, *, br=256):
    T = o.shape[2]
    N = T - L
    lb = L // br
    ob = pl.BlockSpec((None, None, br, HD), lambda i, h: (0, h, i + lb, 0))
    ob1 = pl.BlockSpec((None, None, br, HD), lambda i, h: (1, h, i + lb, 0))

    def body(o0_ref, o1_ref, z_ref, gw_ref, y_ref):
        y_ref[...] = _gout_fn(o0_ref[...], o1_ref[...], z_ref[...], gw_ref[...]).astype(BF16)

    return _call(body, name="gout_fwd", out_shape=_sds((N, GH * HD), BF16), grid=(N // br, GH),
                 in_specs=[ob, ob1, pl.BlockSpec((br, HD), lambda i, h: (i + lb, C_Z // HD + h)),
                           pl.BlockSpec((1, HD), lambda i, h: (0, 0))],
                 out_specs=pl.BlockSpec((br, HD), lambda i, h: (i, h)), sem=("parallel", "parallel"))(o, o, proj, gw)


def _gout_bwd(o, proj, gw, dy, L, *, br=256):
    T = o.shape[2]
    lb = L // br
    ob = pl.BlockSpec((None, None, br, HD), lambda i, h: (0, h, i, 0))
    ob1 = pl.BlockSpec((None, None, br, HD), lambda i, h: (1, h, i, 0))

    def body(o0_ref, o1_ref, z_ref, gw_ref, dy_ref, do_ref, dz_ref, dgw_ref):
        i, h = pl.program_id(0), pl.program_id(1)
        _, vjp = jax.vjp(_gout_fn, o0_ref[...], o1_ref[...], z_ref[...], gw_ref[...])
        g0, _, gz, ggw = vjp(dy_ref[...])
        lat = i >= lb
        do_ref[...] = jnp.where(lat, g0, 0.0)
        dz_ref[...] = jnp.where(lat, gz, 0.0).astype(BF16)

        @pl.when(jnp.logical_and(i == 0, h == 0))
        def _():
            dgw_ref[...] = jnp.zeros_like(dgw_ref)

        dgw_ref[...] += jnp.where(lat, ggw, 0.0)

    return _call(body, name="gout_bwd", out_shape=(_sds((GH, T, HD)), _sds((T, GH * HD), BF16), _sds((1, HD))),
                 grid=(T // br, GH),
                 in_specs=[ob, ob1, pl.BlockSpec((br, HD), lambda i, h: (i, C_Z // HD + h)),
                           pl.BlockSpec((1, HD), lambda i, h: (0, 0)),
                           pl.BlockSpec((br, HD), lambda i, h: (jnp.maximum(i - lb, 0), h))],
                 out_specs=(pl.BlockSpec((None, br, HD), lambda i, h: (h, i, 0)),
                            pl.BlockSpec((br, HD), lambda i, h: (i, h)),
                            pl.BlockSpec((1, HD), lambda i, h: (0, 0))),
                 sem=("arbitrary", "arbitrary"))(o, o, proj, gw, dy)


def _merge_fn(pa, pd, ga, gd):
    return jax.nn.sigmoid(ga) * pa + jax.nn.sigmoid(gd) * pd


def _merge_fwd(pa, pd, proj, L, *, br=256):
    N = pa.shape[0]
    lb = L // br
    row = pl.BlockSpec((br, D), lambda i: (i, 0))

    def body(pa_ref, pd_ref, ga_ref, gd_ref, y_ref):
        y_ref[...] = _merge_fn(pa_ref[...], pd_ref[...], ga_ref[...], gd_ref[...]).astype(BF16)

    return _call(body, name="merge_fwd", out_shape=_sds((N, D), BF16), grid=(N // br,),
                 in_specs=[row, row, pl.BlockSpec((br, D), lambda i: (i + lb, C_GATE // D)),
                           pl.BlockSpec((br, D), lambda i: (i + lb, C_GATE // D + 1))],
                 out_specs=row, sem=("parallel",))(pa, pd, proj, proj)


def _merge_bwd(pa, pd, proj, dy, L, *, br=256):
    N = pa.shape[0]
    T = N + L
    lb = L // br
    lrow = pl.BlockSpec((br, D), lambda i: (jnp.maximum(i - lb, 0), 0))

    def body(pa_ref, pd_ref, ga_ref, gd_ref, dy_ref, dpa_ref, dpd_ref, dg_ref):
        lat = pl.program_id(0) >= lb
        _, vjp = jax.vjp(_merge_fn, pa_ref[...], pd_ref[...], ga_ref[...], gd_ref[...])
        gpa, gpd, gga, ggd = vjp(dy_ref[...])
        dpa_ref[...] = gpa.astype(BF16)
        dpd_ref[...] = gpd.astype(BF16)
        dg_ref[:, :D] = jnp.where(lat, gga, 0.0).astype(BF16)
        dg_ref[:, D:] = jnp.where(lat, ggd, 0.0).astype(BF16)

    return _call(body, name="merge_bwd", out_shape=(_sds((N, D), BF16), _sds((N, D), BF16), _sds((T, 2 * D), BF16)),
                 grid=(T // br,),
                 in_specs=[lrow, lrow, pl.BlockSpec((br, D), lambda i: (i, C_GATE // D)),
                           pl.BlockSpec((br, D), lambda i: (i, C_GATE // D + 1)), lrow],
                 out_specs=(lrow, lrow, pl.BlockSpec((br, 2 * D), lambda i: (i, 0))),
                 sem=("arbitrary",))(pa, pd, proj, proj, dy)


def _resid_fwd(x, m, mod, i_g, *, name, br=256):
    R = x.shape[0]
    row = pl.BlockSpec((br, D), lambda i: (i, 0))

    def body(x_ref, m_ref, mod_ref, o_ref):
        o_ref[...] = x_ref[...] + mod_ref[i_g:i_g + 1, :] * m_ref[...]

    return _call(body, name=name, out_shape=_sds((R, D)), grid=(R // br,),
                 in_specs=[row, row, pl.BlockSpec((6, D), lambda i: (0, 0))], out_specs=row,
                 sem=("parallel",))(x, m, mod)


def _resid_bwd(dx, m, mod, i_g, *, name, br=256):
    R = dx.shape[0]
    row = pl.BlockSpec((br, D), lambda i: (i, 0))
    vec = pl.BlockSpec((1, D), lambda i: (0, 0))

    def body(dx_ref, m_ref, mod_ref, dm_ref, dg_ref):
        dxv = dx_ref[...]
        dm_ref[...] = (dxv * mod_ref[i_g:i_g + 1, :]).astype(BF16)

        @pl.when(pl.program_id(0) == 0)
        def _():
            dg_ref[...] = jnp.zeros_like(dg_ref)

        dg_ref[...] += jnp.sum(dxv * m_ref[...], axis=0, keepdims=True)

    return _call(body, name=name, out_shape=(_sds((R, D), BF16), _sds((1, D))), grid=(R // br,),
                 in_specs=[row, row, pl.BlockSpec((6, D), lambda i: (0, 0))], out_specs=(row, vec),
                 sem=("arbitrary",))(dx, m, mod)


def _ffn_fn(shifts, ug, uv, wg, wv, bg, bv):
    down, up = shifts

    def conv(x, w, b):
        return down(x) * w[0:1, :] + x * w[1:2, :] + up(x) * w[2:3, :] + b

    return _silu(conv(ug, wg, bg)) * conv(uv, wv, bv)


def _ffn_fwd(up, cw, cb, *, bw=256):
    N = up.shape[0]
    shifts = _make_shift(((0, N),))
    nb = DFF // bw

    def body(ug, uv, wg, wv, bg, bv, a_ref):
        a_ref[...] = _ffn_fn(shifts, ug[...], uv[...], wg[...], wv[...], bg[...], bv[...]).astype(BF16)

    def col(rows, off):
        return pl.BlockSpec((rows, bw), lambda j: (0, j + off))

    return _call(body, name="ffn_fwd", out_shape=_sds((N, DFF), BF16), grid=(nb,),
                 in_specs=[col(N, 0), col(N, nb), col(3, 0), col(3, nb), col(1, 0), col(1, nb)],
                 out_specs=col(N, 0), sem=("parallel",), vmem=VMEM_BIG)(up, up, cw, cw, cb, cb)


def _ffn_bwd(up, cw, cb, da, *, bw=256):
    N = up.shape[0]
    shifts = _make_shift(((0, N),))
    nb = DFF // bw

    def body(ug, uv, wg, wv, bg, bv, da_ref, dug, duv, dwg, dwv, dbg, dbv):
        _, vjp = jax.vjp(functools.partial(_ffn_fn, shifts), ug[...], uv[...], wg[...], wv[...], bg[...], bv[...])
        g = vjp(da_ref[...])
        dug[...] = g[0].astype(BF16)
        duv[...] = g[1].astype(BF16)
        dwg[...], dwv[...], dbg[...], dbv[...] = g[2], g[3], g[4], g[5]

    def col(rows, off):
        return pl.BlockSpec((rows, bw), lambda j: (0, j + off))

    half = (_sds((N, DFF), BF16), _sds((N, DFF), BF16), _sds((3, DFF)), _sds((3, DFF)), _sds((1, DFF)), _sds((1, DFF)))
    dug, duv, dwg, dwv, dbg, dbv = _call(
        body, name="ffn_bwd", out_shape=half, grid=(nb,),
        in_specs=[col(N, 0), col(N, nb), col(3, 0), col(3, nb), col(1, 0), col(1, nb), col(N, 0)],
        out_specs=(col(N, 0), col(N, 0), col(3, 0), col(3, 0), col(1, 0), col(1, 0)),
        sem=("parallel",), vmem=VMEM_BIG)(up, up, cw, cw, cb, cb, da)
    return (jnp.concatenate([dug, duv], axis=1), jnp.concatenate([dwg, dwv], axis=1),
            jnp.concatenate([dbg, dbv], axis=1))


def _head_fn(x1, dn, g2, fw, tgt):
    y = _rms(x1 + g2 * dn) * fw
    err = y - tgt
    return 0.5 * jnp.sum(jnp.mean(err * err, axis=-1))


def _head(x1, dn, mod, fw, tgt, *, br=256):
    N = x1.shape[0]
    row = pl.BlockSpec((br, D), lambda i: (i, 0))
    vec = pl.BlockSpec((1, D), lambda i: (0, 0))
    one = pl.BlockSpec((1, HD), lambda i: (0, 0))

    def body(x1_ref, dn_ref, mod_ref, fw_ref, tgt_ref, loss_ref, dx_ref, ddn_ref, dg_ref, dfw_ref):
        loss, (gx, gdn, gg, gfw) = jax.value_and_grad(_head_fn, argnums=(0, 1, 2, 3))(
            x1_ref[...], dn_ref[...], mod_ref[5:6, :], fw_ref[...], tgt_ref[...])
        dx_ref[...] = gx
        ddn_ref[...] = gdn.astype(BF16)

        @pl.when(pl.program_id(0) == 0)
        def _():
            loss_ref[...] = jnp.zeros_like(loss_ref)
            dg_ref[...] = jnp.zeros_like(dg_ref)
            dfw_ref[...] = jnp.zeros_like(dfw_ref)

        loss_ref[...] += jnp.broadcast_to(loss, (1, HD))
        dg_ref[...] += gg
        dfw_ref[...] += gfw

    return _call(body, name="head", out_shape=(_sds((1, HD)), _sds((N, D)), _sds((N, D), BF16), _sds((1, D)), _sds((1, D))),
                 grid=(N // br,), in_specs=[row, row, pl.BlockSpec((6, D), lambda i: (0, 0)), vec, row],
                 out_specs=(one, row, row, vec, vec), sem=("arbitrary",))(x1, dn, mod, fw, tgt)


def _adamw(w, g, m, v, *, name):
    shape = w.shape
    cols = shape[-1]
    rows = max(1, math.prod(shape[:-1]))
    w2, g2, m2, v2 = (t.reshape(rows, cols) for t in (w, g, m, v))
    br = 256 if rows % 256 == 0 else (128 if rows % 128 == 0 else (8 if rows % 8 == 0 and rows > 64 else rows))
    if rows % 352 == 0:
        br = 352
    c1 = 1.0 - B1 ** STEP
    c2 = 1.0 - B2 ** STEP

    def body(w_ref, g_ref, m_ref, v_ref, d_ref, nm_ref, nv_ref):
        gv = g_ref[...]
        nm = B1 * m_ref[...] + (1.0 - B1) * gv
        nv = B2 * v_ref[...] + (1.0 - B2) * (gv * gv)
        d_ref[...] = -LR * ((nm / c1) / (jnp.sqrt(nv / c2) + AEPS) + WD * w_ref[...])
        nm_ref[...] = nm
        nv_ref[...] = nv

    blk = pl.BlockSpec((br, cols), lambda i: (i, 0))
    outs = _call(body, name=name, out_shape=(_sds((rows, cols)),) * 3, grid=(rows // br,),
                 in_specs=[blk] * 4, out_specs=(blk,) * 3, sem=("parallel",))(w2, g2, m2, v2)
    return tuple(t.reshape(shape) for t in outs)


def _rope_tables(N, L):
    t = jnp.arange(N)
    pos = jnp.stack([(t // GRID_W).astype(F32), (t % GRID_W).astype(F32)], axis=1)
    inv = ROPE_THETA ** (-jnp.arange(0, HD // 2, 2, dtype=F32) / (HD // 2))
    ang = pos[:, :, None] * inv[None, None, :]
    cos = jnp.broadcast_to(jnp.cos(ang)[:, :, None, :], (N, 2, 2, HD // 4)).reshape(N, HD)
    sin = jnp.broadcast_to(jnp.sin(ang)[:, :, None, :], (N, 2, 2, HD // 4))
    sin = (sin * jnp.array([-1.0, 1.0], F32)[None, None, :, None]).reshape(N, HD)
    cos = jnp.concatenate([jnp.ones((L, HD), F32), cos], axis=0)
    sin = jnp.concatenate([jnp.zeros((L, HD), F32), sin], axis=0)
    return cos, sin


def _pad_lanes(v, off=0):
    return jnp.zeros((1, HD), F32).at[0, off:off + v.shape[0]].set(v)


def _local_step(x, ctx, tgt, mod_lat, mod_ctx, w_in, shards, small):
    N, L = x.shape[0], ctx.shape[0]
    T = N + L
    bounds = ((0, L), (L, T))
    qw, kw, gw = small["q_norm_w"], small["k_norm_w"], small["gdn_norm_w"]
    conv_w, ffn_w, ffn_b, fnw = small["conv_qkv_w"], small["ffn_conv_w"], small["ffn_conv_b"], small["final_norm_w"]
    alog = _pad_lanes(small["a_log"].reshape(-1), 2 * GH)
    dtb = _pad_lanes(small["dt_bias"].reshape(-1), 2 * GH)
    cos, sin = _rope_tables(N, L)
    bt = 256 if T % 768 else 768
    bnl = 256 if N % 1024 else 1024

    hc = _normmod_fwd(ctx, mod_ctx, 0, 1, name="normmod_ctx")
    hx = _normmod_fwd(x, mod_lat, 0, 1, name="normmod_x")
    h1 = jnp.concatenate([hc, hx], axis=0)
    proj = _mm(h1, w_in, name="mm_in", M=T, N=C_END, K=D, bm=bt, bn=1024)
    aq, ak, av = _aprep_fwd(proj, cos, sin, qw, kw)
    attn, (up_g,) = _attn_fwd(aq, ak, av, L, _Exchange([shards["w_up"]], False))
    gq = _gprep_fwd(proj, conv_w, 0, bounds)
    gk = _gprep_fwd(proj, conv_w, 1, bounds)
    gv = _gprep_fwd(proj, conv_w, 2, bounds)
    bl = _bl_fwd(proj, alog, dtb)
    intra, (down_g, pa_g, pd_g, out_g) = _intra_fwd(
        gq, gk, gv, bl, _Exchange([shards[n] for n in ("w_down", "w_pa", "w_pd", "w_out")], False))
    w_up, w_down = _by_columns(up_g), down_g.reshape(DFF, D)
    w_pa, w_pd, w_out = pa_g.reshape(D, D), pd_g.reshape(D, D), out_g.reshape(D, D)
    xinv, intra = intra[6], intra[:6]
    o, states = _scan_fwd(*intra, L)
    gdn = _gout_fwd(o, proj, gw, L)
    pa = _mm(attn, w_pa, name="mm_pa", M=N, N=D, K=D, bm=bnl)
    pd = _mm(gdn, w_pd, name="mm_pd", M=N, N=D, K=D, bm=bnl)
    y = _merge_fwd(pa, pd, proj, L)
    m = _mm(y, w_out, name="mm_out", M=N, N=D, K=D, bm=bnl)
    x1 = _resid_fwd(x, m, mod_lat, 2, name="resid1")
    h2 = _normmod_fwd(x1, mod_lat, 3, 4, name="normmod_x1")
    up = _mm(h2, w_up, name="mm_up", M=N, N=2 * DFF, K=D, bm=bnl, bn=2 * DFF // 4)
    a = _ffn_fwd(up, ffn_w, ffn_b)
    dn = _mm(a, w_down, name="mm_down", M=N, N=D, K=DFF, bm=bnl)
    loss, dx2, ddn, dg2, dfnw = _head(x1, dn, mod_lat, fnw, tgt)

    da = _mm(ddn, w_down, name="mm_down_dx", M=N, N=DFF, K=D, tb=True, bm=bnl, bn=DFF // 2)
    g_down = _mm(a, ddn, name="mm_down_dw", M=DFF, N=D, K=N, ta=True, bm=DFF // 2)
    dup, d_ffn_w, d_ffn_b = _ffn_bwd(up, ffn_w, ffn_b, da)
    dh2 = _mm(dup, w_up, name="mm_up_dx", M=N, N=D, K=2 * DFF, tb=True, bm=bnl, bk=2 * DFF // 4)
    g_up = _mm(h2, dup, name="mm_up_dw", M=D, N=2 * DFF, K=N, ta=True, bn=2 * DFF // 4)
    dx1, dsh2, dsc2 = _normmod_bwd(x1, mod_lat, 3, 4, dh2, 0, dx2, name="normmod_x1_bwd")
    dm, dg1 = _resid_bwd(dx1, m, mod_lat, 2, name="resid1_bwd")
    dy = _mm(dm, w_out, name="mm_out_dx", M=N, N=D, K=D, tb=True, bm=bnl)
    g_out = _mm(y, dm, name="mm_out_dw", M=D, N=D, K=N, ta=True)
    dpa, dpd, dgate = _merge_bwd(pa, pd, proj, dy, L)
    dattn = _mm(dpa, w_pa, name="mm_pa_dx", M=N, N=D, K=D, tb=True, bm=bnl)
    g_pa = _mm(attn, dpa, name="mm_pa_dw", M=D, N=D, K=N, ta=True)
    dgdn = _mm(dpd, w_pd, name="mm_pd_dx", M=N, N=D, K=D, tb=True, bm=bnl)
    g_pd = _mm(gdn, dpd, name="mm_pd_dw", M=D, N=D, K=N, ta=True)
    do, dz, dgw = _gout_bwd(o, proj, gw, dgdn, L)
    cts = _scan_bwd(*intra, states, do, L)
    parts = [g_pa.reshape(NDEV, D // NDEV, D), g_pd.reshape(NDEV, D // NDEV, D), g_out.reshape(NDEV, D // NDEV, D),
             _to_columns(g_up), g_down.reshape(NDEV, DFF // NDEV, D)]
    (dgq, dgk, dgv, dbl), recv = _intra_bwd(gq, gk, gv, bl, xinv, cts,
                                            _Exchange([p.astype(BF16) for p in parts], True))
    dxq, dwq = _gprep_bwd(proj, conv_w, 0, bounds, dgq)
    dxk, dwk = _gprep_bwd(proj, conv_w, 1, bounds, dgk)
    dxv, dwv = _gprep_bwd(proj, conv_w, 2, bounds, dgv)
    dxbl, dalog, ddtb = _bl_bwd(proj, alog, dtb, dbl)
    daq_h, dak_h, dav_h = _attn_bwd(aq, ak, av, dattn, L)
    daq, dkv, dqw, dkw = _aprep_bwd(proj, cos, sin, qw, kw, daq_h, dak_h, dav_h, L)
    dproj = jnp.concatenate([dkv, dxq, dxk, dxv, dxbl, jnp.zeros((T, C_AQ - C_BL - HD), BF16), daq, dz, dgate], axis=1)
    g_in = _mm(h1, dproj, name="mm_in_dw", M=D, N=C_END, K=T, ta=True, bn=1024)
    g_in = _to_columns(jnp.concatenate([g_in[:, :W_AQ], g_in[:, C_AQ:]], axis=1)).astype(BF16)
    own_in = lax.dynamic_index_in_dim(g_in, _position()[3], axis=0, keepdims=False)
    *pending, token = _scatter_start(g_in, name="scatter_g_in_start")
    dh1 = _mm(dproj, w_in, name="mm_in_dx", M=T, N=D, K=C_END, tb=True, bm=bt, bk=1024, after=(token,))
    grad_x, dsh1, dsc1 = _normmod_bwd(x, mod_lat, 0, 1, dh1, L, dx1, name="normmod_x_bwd")
    _, dcsh1, dcsc1 = _normmod_bwd(ctx, mod_ctx, 0, 1, dh1, 0, None, name="normmod_ctx_bwd")

    z1 = jnp.zeros((1, D), F32)
    dmod_lat = jnp.concatenate([dsh1, dsc1, dg1, dsh2, dsc2, dg2], axis=0)
    dmod_ctx = jnp.concatenate([dcsh1, dcsc1, z1, z1, z1, z1], axis=0)
    gsmall = {
        "q_norm_w": dqw, "k_norm_w": dkw, "gdn_norm_w": dgw,
        "conv_qkv_w": jnp.concatenate([dwq, dwk, dwv], axis=1),
        "a_log": dalog[0, 2 * GH:4 * GH], "dt_bias": ddtb[0, 2 * GH:4 * GH],
        "ffn_conv_w": d_ffn_w, "ffn_conv_b": d_ffn_b, "final_norm_w": dfnw,
    }
    return (loss[0, 0], grad_x, (pending, own_in), dict(zip(("w_pa", "w_pd", "w_out", "w_up", "w_down"), recv)),
            dmod_lat, dmod_ctx, gsmall)


HBM = pl.BlockSpec(memory_space=pltpu.HBM)


def _position():
    x, y, c = lax.axis_index("x"), lax.axis_index("y"), lax.axis_index("c")
    return x, y, c, 4 * x + 2 * y + c


def _peer(x, y, c, k):
    px = 1 - x if k & 4 else x
    py = 1 - y if k & 2 else y
    pc = 1 - c if k & 1 else c
    return (px, py, pc), 4 * px + 2 * py + pc


def _exchange(arrs, *, name, scatter):
    exch = _Exchange(arrs, scatter)
    n = exch.n

    def body(*refs):
        ins, outs, sems = refs[:n], refs[n:2 * n], refs[2 * n:]
        exch.start(ins, outs, sems)
        exch.finish(ins, outs, sems)

    outs = pl.pallas_call(body, name=name, out_shape=exch.out_shape, in_specs=[HBM] * n, out_specs=(HBM,) * n,
                          scratch_shapes=exch.scratch,
                          compiler_params=pltpu.CompilerParams(has_side_effects=True))(*arrs)
    return list(outs)


class _Exchange:
    def __init__(self, arrs, scatter):
        self.arrs, self.scatter, self.n = list(arrs), scatter, len(arrs)
        self.out_shape = tuple(_sds(a.shape if scatter else (NDEV,) + a.shape, a.dtype) for a in arrs)
        self.scratch = [pltpu.SemaphoreType.DMA((self.n, NDEV - 1)), pltpu.SemaphoreType.DMA((self.n, NDEV - 1)),
                        pltpu.SemaphoreType.DMA((self.n,))]

    def _copies(self, ins, outs, sems):
        send, recv, loc = sems
        x, y, c, me = _position()
        local = [pltpu.make_async_copy(ins[a].at[me] if self.scatter else ins[a], outs[a].at[me], loc.at[a])
                 for a in range(self.n)]
        remote = []
        for k in range(1, NDEV):
            peer, pid = _peer(x, y, c, k)
            for a in range(self.n):
                src = ins[a].at[pid] if self.scatter else ins[a]
                remote.append(pltpu.make_async_remote_copy(
                    src_ref=src, dst_ref=outs[a].at[me], send_sem=send.at[a, k - 1], recv_sem=recv.at[a, k - 1],
                    device_id=peer, device_id_type=MESH))
        return local, remote

    def start(self, ins, outs, sems):
        local, remote = self._copies(ins, outs, sems)
        for cp in local + remote:
            cp.start()

    def finish(self, ins, outs, sems):
        local, remote = self._copies(ins, outs, sems)
        for cp in remote:
            cp.wait()
        for cp in local:
            cp.wait()


def _gather_two_level(block, *, name):
    def body(x_ref, out_ref, send_sems, recv_sems, local_sem):
        x, y, c, _ = _position()
        me, sibling = (x, y, c), (x, y, 1 - c)
        chips = [(1 - x, y), (x, 1 - y), (1 - x, 1 - y)]

        def slot(px, py, pc):
            return out_ref.at[4 * px + 2 * py + pc]

        def copy(k, owner, to, src=None):
            return pltpu.make_async_remote_copy(
                src_ref=slot(*owner) if src is None else src, dst_ref=slot(*owner), send_sem=send_sems.at[k],
                recv_sem=recv_sems.at[k], device_id=to, device_id_type=MESH)

        mine = pltpu.make_async_copy(x_ref, slot(*me), local_sem)
        mine.start()
        first = [copy(0, me, sibling, src=x_ref)]
        first += [copy(1 + j, me, (*chip, c), src=x_ref) for j, chip in enumerate(chips)]
        for cp in first:
            cp.start()
        passed = [copy(4 + j, (*chip, c), sibling) for j, chip in enumerate(chips)]
        for j, chip in enumerate(chips):
            copy(1 + j, (*chip, c), me).wait_recv()
            passed[j].start()
        copy(0, sibling, me).wait_recv()
        for j, chip in enumerate(chips):
            copy(4 + j, (*chip, 1 - c), me).wait_recv()
        for cp in first + passed:
            cp.wait_send()
        mine.wait()

    return pl.pallas_call(
        body, name=name, out_shape=_sds((NDEV,) + block.shape, block.dtype), in_specs=[HBM], out_specs=HBM,
        scratch_shapes=[pltpu.SemaphoreType.DMA((NDEV - 1,)), pltpu.SemaphoreType.DMA((NDEV - 1,)),
                        pltpu.SemaphoreType.DMA],
        compiler_params=pltpu.CompilerParams(has_side_effects=True))(block)


SEM = pl.BlockSpec(memory_space=pltpu.SEMAPHORE)


def _scatter_copies(src_ref, land_ref, send_sems, recv_sems):
    x, y, c, me = _position()
    copies = []
    for k in range(1, NDEV):
        peer, pid = _peer(x, y, c, k)
        copies.append(pltpu.make_async_remote_copy(
            src_ref=src_ref.at[pid], dst_ref=land_ref.at[me], send_sem=send_sems.at[k - 1],
            recv_sem=recv_sems.at[k - 1], device_id=peer, device_id_type=MESH))
    return copies


def _scatter_start(parts, *, name):
    def body(src_ref, land_ref, send_sems, recv_sems, src_thru, land_thru, token):
        for cp in _scatter_copies(src_ref, land_ref, send_sems, recv_sems):
            cp.start()
        token[...] = jnp.zeros_like(token)

    return pl.pallas_call(
        body, name=name,
        out_shape=(pltpu.SemaphoreType.DMA((NDEV - 1,)), pltpu.SemaphoreType.DMA((NDEV - 1,)),
                   pltpu.HBM(parts.shape, parts.dtype), pltpu.HBM(parts.shape, parts.dtype), _sds((8, HD))),
        in_specs=(HBM, HBM), out_specs=(SEM, SEM, HBM, HBM, pl.BlockSpec(memory_space=pltpu.VMEM)),
        input_output_aliases={0: 2, 1: 3},
        compiler_params=pltpu.CompilerParams(has_side_effects=pltpu.SideEffectType.DATAFLOW_SIDE_EFFECTING),
    )(pltpu.with_memory_space_constraint(parts, pltpu.HBM),
      pltpu.with_memory_space_constraint(lax.empty(parts.shape, parts.dtype), pltpu.HBM))


def _scatter_wait(send_sems, recv_sems, src_thru, land_thru, after, *, name):
    na = len(after)

    def body(src_ref, land_ref, send_sems, recv_sems, *rest):
        for cp in _scatter_copies(src_ref, land_ref, send_sems, recv_sems):
            cp.wait_send()
            cp.wait_recv()

    return pl.pallas_call(
        body, name=name,
        out_shape=(pltpu.HBM(src_thru.shape, src_thru.dtype), pltpu.HBM(land_thru.shape, land_thru.dtype)),
        in_specs=(HBM, HBM, SEM, SEM) + (pl.BlockSpec(memory_space=pl.ANY),) * na, out_specs=(HBM, HBM),
        input_output_aliases={0: 0, 1: 1},
        compiler_params=pltpu.CompilerParams(has_side_effects=pltpu.SideEffectType.DATAFLOW_SIDE_EFFECTING),
    )(src_thru, land_thru, send_sems, recv_sems, *after)[1]


def _cast_bf16(w, *, name):
    rows, cols = w.shape
    br = 128 if rows % 128 == 0 else rows

    def body(w_ref, o_ref):
        o_ref[...] = w_ref[...].astype(BF16)

    blk = pl.BlockSpec((br, cols), lambda i: (i, 0))
    return _call(body, name=name, out_shape=_sds((rows, cols), BF16), grid=(rows // br,), in_specs=[blk],
                 out_specs=blk, sem=("parallel",))(w)


def _sum_slots(a, *, name):
    _, R, C = a.shape

    def body(a_ref, o_ref):
        s = a_ref[0]
        for d in range(1, NDEV):
            s = s + a_ref[d]
        o_ref[...] = s

    return _call(body, name=name, out_shape=_sds((R, C)))(a)


MODROWS = 16


def _mod_fwd(c9, w, b):
    cols = w.shape[1]

    def body(c_ref, w_ref, b_ref, o_ref):
        o_ref[...] = _nn(_silu(c_ref[...]), w_ref[...]) + b_ref[...]

    return _call(body, name="mod_fwd", out_shape=_sds((MODROWS, cols)))(c9, w, b)


def _mod_bwd(c9, dmy, dall, w):
    cols = w.shape[1]

    def body(c_ref, dmy_ref, dall_ref, w_ref, gw_ref, gb_ref, cp_ref):
        sc = _silu(c_ref[...])
        rows = lax.broadcasted_iota(jnp.int32, (MODROWS, 1), 0)
        d = dmy_ref[...]
        d_ctx = jnp.where(rows == NDEV, d, 0.0)
        sc_ctx = jnp.where(rows == NDEV, sc, 0.0)
        outer = lax.dot_general(sc_ctx, d_ctx, (((0,), (0,)), ((), ())), precision=HI, preferred_element_type=F32)
        gw_ref[...] = _tn(jnp.where(rows < NDEV, sc, 0.0), jnp.where(rows < NDEV, d, 0.0)) + outer
        gb_ref[...] = jnp.sum(dall_ref[...], axis=0, keepdims=True)
        cp_ref[...] = jnp.sum(_nt(d_ctx, w_ref[...]), axis=0, keepdims=True)

    return _call(body, name="mod_bwd", out_shape=(_sds((D, cols)), _sds((1, 6 * D)), _sds((1, D))),
                 vmem=VMEM_BIG)(c9, dmy, dall, w)


def _cctx_finish(parts, c_ctx):
    def body(p_ref, c_ref, o_ref):
        s = p_ref[0]
        for d in range(1, NDEV):
            s = s + p_ref[d]
        _, vjp = jax.vjp(_silu, c_ref[...])
        o_ref[...] = vjp(s)[0]

    return _call(body, name="cctx_finish", out_shape=_sds((1, D)))(parts, c_ctx)


def _adamw_recv(w, recv, m, v, *, name, own=None):
    rows, cols = w.shape
    br = 128 if rows % 128 == 0 else rows
    c1 = 1.0 - B1 ** STEP
    c2 = 1.0 - B2 ** STEP
    has_own = own is not None

    def body(w_ref, r_ref, m_ref, v_ref, *rest):
        g_ref, d_ref, nm_ref, nv_ref = rest[-4:]
        me = _position()[3]

        def slot(d):
            return jnp.where(me == d, rest[0][...], r_ref[d]) if has_own else r_ref[d]

        gv = slot(0).astype(F32)
        for d in range(1, NDEV):
            gv = gv + slot(d).astype(F32)
        nm = B1 * m_ref[...] + (1.0 - B1) * gv
        nv = B2 * v_ref[...] + (1.0 - B2) * (gv * gv)
        g_ref[...] = gv
        d_ref[...] = -LR * ((nm / c1) / (jnp.sqrt(nv / c2) + AEPS) + WD * w_ref[...])
        nm_ref[...] = nm
        nv_ref[...] = nv

    blk = pl.BlockSpec((br, cols), lambda i: (i, 0))
    return _call(body, name=name, out_shape=(_sds((rows, cols)),) * 4, grid=(rows // br,),
                 in_specs=[blk, pl.BlockSpec((NDEV, br, cols), lambda i: (0, i, 0)), blk, blk] + [blk] * has_own,
                 out_specs=(blk,) * 4, sem=("parallel",))(w, recv, m, v, *([own] if has_own else []))


P_LAT, P_CTX, P_FNW, P_FFNB, P_CONV, P_FFNW, P_MISC, P_ROWS = 0, 8, 16, 24, 32, 48, 72, 80


def _rows_of(v, nrows):
    flat = v.reshape(-1)
    return jnp.pad(flat, (0, nrows * D - flat.shape[0])).reshape(nrows, D)


def _by_columns(g):
    n, r, c = g.shape
    return jnp.transpose(g, (1, 0, 2)).reshape(r, n * c)


def _to_columns(a):
    r, nc = a.shape
    return jnp.transpose(a.reshape(r, NDEV, nc // NDEV), (1, 0, 2))


def kernel(x, c, ctx, c_ctx, w_mod, b_mod, w_in, q_norm_w, k_norm_w, conv_qkv_w, a_log, dt_bias, gdn_norm_w, w_pa, w_pd, w_out, w_up, ffn_conv_w, ffn_conv_b, w_down, final_norm_w, loss_target, m_c_ctx, m_w_mod, m_b_mod, m_w_in, m_q_norm_w, m_k_norm_w, m_conv_qkv_w, m_a_log, m_dt_bias, m_gdn_norm_w, m_w_pa, m_w_pd, m_w_out, m_w_up, m_ffn_conv_w, m_ffn_conv_b, m_w_down, m_final_norm_w, v_c_ctx, v_w_mod, v_b_mod, v_w_in, v_q_norm_w, v_k_norm_w, v_conv_qkv_w, v_a_log, v_dt_bias, v_gdn_norm_w, v_w_pa, v_w_pd, v_w_out, v_w_up, v_ffn_conv_w, v_ffn_conv_b, v_w_down, v_final_norm_w):
    _, _, _, me = _position()
    mcols = w_mod.shape[2]

    big = {"w_in": w_in[0], "w_pa": w_pa[0], "w_pd": w_pd[0], "w_out": w_out[0], "w_up": w_up[0], "w_down": w_down[0]}
    names = list(big)
    shards = {n: _cast_bf16(big[n], name="cast_" + n) for n in names}
    w_in_g = _gather_two_level(shards["w_in"], name="gather_w_in")
    c_all, conv_g, ffnw_g = _exchange([c, conv_qkv_w[0], ffn_conv_w[0]], name="gather_small", scatter=False)
    w_in_full = _by_columns(w_in_g)
    w_in_pad = jnp.concatenate([w_in_full[:, :W_AQ], jnp.zeros((D, C_AQ - W_AQ), BF16), w_in_full[:, W_AQ:]], axis=1)

    c9 = jnp.concatenate([c_all.reshape(NDEV, D), jnp.pad(c_ctx[None], ((0, MODROWS - NDEV - 1), (0, 0)))], axis=0)
    b_loc = lax.dynamic_slice(b_mod, (0, me * mcols), (1, mcols))
    mod_all, = _exchange([_mod_fwd(c9, w_mod[0], b_loc)], name="gather_mod", scatter=False)
    mod_lat = lax.dynamic_index_in_dim(mod_all, me, axis=1, keepdims=False).reshape(6, D)
    mod_ctx = mod_all[:, NDEV, :].reshape(6, D)

    small = {"q_norm_w": q_norm_w, "k_norm_w": k_norm_w, "gdn_norm_w": gdn_norm_w, "a_log": a_log, "dt_bias": dt_bias,
             "conv_qkv_w": _by_columns(conv_g), "ffn_conv_w": _by_columns(ffnw_g), "ffn_conv_b": ffn_conv_b,
             "final_norm_w": final_norm_w[None]}
    loss_me, grad_x, (pending_in, own_in), recv, dmod_lat, dmod_ctx, gs = _local_step(
        x[0], ctx[0], loss_target[0], mod_lat, mod_ctx, w_in_pad, shards, small)

    moments = {"w_in": (m_w_in, v_w_in), "w_pa": (m_w_pa, v_w_pa), "w_pd": (m_w_pd, v_w_pd),
               "w_out": (m_w_out, v_w_out), "w_up": (m_w_up, v_w_up), "w_down": (m_w_down, v_w_down)}
    res = {}
    for n in recv:
        outs = _adamw_recv(big[n], recv[n], moments[n][0][0], moments[n][1][0], name="adamw_" + n)
        res[n] = tuple(t[None] for t in outs)

    misc = jnp.concatenate([gs["q_norm_w"][0], gs["k_norm_w"][0], gs["gdn_norm_w"][0], gs["a_log"], gs["dt_bias"],
                            loss_me[None]])
    pack = jnp.concatenate([_rows_of(dmod_lat, P_CTX - P_LAT), _rows_of(dmod_ctx, P_FNW - P_CTX),
                            _rows_of(gs["final_norm_w"], P_FFNB - P_FNW), _rows_of(gs["ffn_conv_b"], P_CONV - P_FFNB),
                            _rows_of(gs["conv_qkv_w"], P_FFNW - P_CONV), _rows_of(gs["ffn_conv_w"], P_MISC - P_FFNW),
                            _rows_of(misc, P_ROWS - P_MISC)], axis=0)
    pack_all, = _exchange([pack], name="gather_pack", scatter=False)
    tot = _sum_slots(pack_all, name="sum_pack")
    dall = jnp.concatenate([pack_all[:, P_LAT:P_LAT + 6, :].reshape(NDEV, 6 * D),
                            jnp.pad(tot[P_CTX:P_CTX + 6].reshape(1, 6 * D), ((0, MODROWS - NDEV - 1), (0, 0)))], axis=0)
    dmy = lax.dynamic_slice(dall, (0, me * mcols), (MODROWS, mcols))
    g_w_mod, g_b_mod, cpart = _mod_bwd(c9, dmy, dall, w_mod[0])
    cparts, = _exchange([cpart], name="gather_cctx", scatter=False)
    g_c_ctx = _cctx_finish(cparts, c_ctx[None])[0]

    nconv, nffn = 3 * GH * HD, 2 * DFF
    conv_tot = tot[P_CONV:P_FFNW].reshape(-1)[:3 * nconv].reshape(3, nconv)
    ffnw_tot = tot[P_FFNW:P_MISC].reshape(-1)[:3 * nffn].reshape(3, nffn)
    mrow = tot[P_MISC]
    grads = {
        "c_ctx": g_c_ctx, "w_mod": g_w_mod[None], "b_mod": g_b_mod,
        "q_norm_w": mrow[None, 0:HD], "k_norm_w": mrow[None, HD:2 * HD], "gdn_norm_w": mrow[None, 2 * HD:3 * HD],
        "conv_qkv_w": lax.dynamic_slice(conv_tot, (0, me * (nconv // NDEV)), (3, nconv // NDEV))[None],
        "a_log": mrow[3 * HD:3 * HD + 2 * GH].reshape(1, 2, GH),
        "dt_bias": mrow[3 * HD + 2 * GH:3 * HD + 4 * GH].reshape(1, 2, GH),
        "ffn_conv_w": lax.dynamic_slice(ffnw_tot, (0, me * (nffn // NDEV)), (3, nffn // NDEV))[None],
        "ffn_conv_b": tot[P_FFNB:P_CONV].reshape(-1)[:nffn][None],
        "final_norm_w": tot[P_FNW],
    }
    loss = mrow[3 * HD + 4 * GH]
    given = {"c_ctx": (c_ctx, m_c_ctx, v_c_ctx), "w_mod": (w_mod, m_w_mod, v_w_mod), "b_mod": (b_mod, m_b_mod, v_b_mod),
             "q_norm_w": (q_norm_w, m_q_norm_w, v_q_norm_w), "k_norm_w": (k_norm_w, m_k_norm_w, v_k_norm_w),
             "conv_qkv_w": (conv_qkv_w, m_conv_qkv_w, v_conv_qkv_w), "a_log": (a_log, m_a_log, v_a_log),
             "dt_bias": (dt_bias, m_dt_bias, v_dt_bias), "gdn_norm_w": (gdn_norm_w, m_gdn_norm_w, v_gdn_norm_w),
             "ffn_conv_w": (ffn_conv_w, m_ffn_conv_w, v_ffn_conv_w), "ffn_conv_b": (ffn_conv_b, m_ffn_conv_b, v_ffn_conv_b),
             "final_norm_w": (final_norm_w, m_final_norm_w, v_final_norm_w)}
    for n, (w, m, v) in given.items():
        res[n] = (grads[n],) + _adamw(w, grads[n], m, v, name="adamw_" + n)

    land = _scatter_wait(*pending_in, [res[n][1] for n in res], name="scatter_g_in_wait")
    outs = _adamw_recv(big["w_in"], land, m_w_in[0], v_w_in[0], name="adamw_w_in", own=own_in)
    res["w_in"] = tuple(t[None] for t in outs)

    order = ["c_ctx", "w_mod", "b_mod", "w_in", "q_norm_w", "k_norm_w", "conv_qkv_w", "a_log", "dt_bias", "gdn_norm_w",
             "w_pa", "w_pd", "w_out", "w_up", "ffn_conv_w", "ffn_conv_b", "w_down", "final_norm_w"]
    return (loss, grad_x[None], *[res[n][0] for n in order], *[res[n][1] for n in order],
            *[res[n][2] for n in order], *[res[n][3] for n in order])
```

```python
import functools
import math

import jax
import jax.numpy as jnp
from jax import lax
from jax.experimental import pallas as pl
from jax.experimental.pallas import tpu as pltpu

F32 = jnp.float32
BF16 = jnp.bfloat16
HI = lax.Precision.HIGHEST
MESH = pl.DeviceIdType.MESH

NDEV = 8
D = 1024
HD = 128
AH, AKV, GRP = 8, 2, 4
GH = 8
CH = 64
DFF = 2816
GRID_W = 64
EPS = 1e-6
ROPE_THETA = 10000.0
C_KV, C_QKV, C_BL, C_AQ, C_Z, C_GATE, C_END = 0, 512, 3584, 4096, 5120, 6144, 8192
W_BL, W_AQ, W_END = 3584, 3616, 7712
LR, B1, B2, AEPS, WD, STEP = 0.001, 0.9, 0.999, 1e-08, 0.01, 10
VMEM_BIG = 56 * 1024 * 1024
INTRA_FWD_CHUNKS = 18
INTRA_BWD_CHUNKS = 12


def _call(body, *, name, out_shape, grid=None, in_specs=None, out_specs=None, scratch=(), sem=None,
          vmem=None, aliases=None):
    params = {}
    if sem is not None:
        params["dimension_semantics"] = sem
    if vmem is not None:
        params["vmem_limit_bytes"] = vmem
    kw = {}
    if grid is not None:
        kw["grid"] = grid
    if in_specs is not None:
        kw["in_specs"] = in_specs
    if out_specs is not None:
        kw["out_specs"] = out_specs
    if aliases:
        kw["input_output_aliases"] = aliases
    return pl.pallas_call(body, name=name, out_shape=out_shape, scratch_shapes=list(scratch),
                          compiler_params=pltpu.CompilerParams(**params), **kw)


def _call_carrying(body, exch, *, name, out_shape, grid, in_specs, out_specs, scratch=(), vmem=None):
    n, nin, nout, nscr = exch.n, len(in_specs), len(out_shape), len(scratch)

    def wrapped(*refs):
        ins, cins = refs[:nin], refs[nin:nin + n]
        outs, couts = refs[nin + n:nin + n + nout], refs[nin + n + nout:nin + 2 * n + nout]
        scr, sems = refs[nin + 2 * n + nout:nin + 2 * n + nout + nscr], refs[nin + 2 * n + nout + nscr:]
        ids = [pl.program_id(i) for i in range(len(grid))]
        first = functools.reduce(jnp.logical_and, [i == 0 for i in ids])
        last = functools.reduce(jnp.logical_and, [i == g - 1 for i, g in zip(ids, grid)])

        @pl.when(first)
        def _():
            exch.start(cins, couts, sems)

        body(*ins, *outs, *scr)

        @pl.when(last)
        def _():
            exch.finish(cins, couts, sems)

    params = {"dimension_semantics": ("arbitrary",) * len(grid)}
    if vmem is not None:
        params["vmem_limit_bytes"] = vmem
    fn = pl.pallas_call(wrapped, name=name, out_shape=tuple(out_shape) + exch.out_shape, grid=grid,
                        in_specs=list(in_specs) + [HBM] * n, out_specs=tuple(out_specs) + (HBM,) * n,
                        scratch_shapes=list(scratch) + exch.scratch, compiler_params=pltpu.CompilerParams(**params))

    def run(*args):
        res = fn(*args, *exch.arrs)
        return res[:nout], list(res[nout:])

    return run


def _sds(shape, dtype=F32):
    return jax.ShapeDtypeStruct(tuple(shape), dtype)


def _dot(a, b, ca, cb):
    return lax.dot_general(a.astype(BF16), b.astype(BF16), (((ca,), (cb,)), ((), ())),
                           preferred_element_type=F32)


@jax.custom_vjp
def _nn(a, b):
    return _dot(a, b, 1, 0)


@jax.custom_vjp
def _nt(a, b):
    return _dot(a, b, 1, 1)


@jax.custom_vjp
def _tn(a, b):
    return _dot(a, b, 0, 0)


_nn.defvjp(lambda a, b: (_nn(a, b), (a, b)), lambda r, g: (_nt(g, r[1]), _tn(r[0], g)))
_nt.defvjp(lambda a, b: (_nt(a, b), (a, b)), lambda r, g: (_nn(g, r[1]), _tn(g, r[0])))
_tn.defvjp(lambda a, b: (_tn(a, b), (a, b)), lambda r, g: (_nt(r[1], g), _nn(r[0], g)))


def _hdot(a, b):
    return jnp.dot(a, b, precision=HI, preferred_element_type=F32)


def _mdot(a, b):
    return jnp.dot(a, b, precision=lax.Precision.HIGH, preferred_element_type=F32)


def _maskdot(mask, a, cm):
    hi = a.astype(BF16)
    r = a - hi.astype(F32)
    mid = r.astype(BF16)
    lo = (r - mid.astype(F32)).astype(BF16)
    mb = mask.astype(BF16)
    dims = (((cm,), (0,)), ((), ()))
    return (lax.dot_general(mb, hi, dims, preferred_element_type=F32)
            + lax.dot_general(mb, mid, dims, preferred_element_type=F32)
            + lax.dot_general(mb, lo, dims, preferred_element_type=F32))


@jax.custom_vjp
def _mask_nn(mask, a):
    return _maskdot(mask, a, 1)


_mask_nn.defvjp(lambda mask, a: (_maskdot(mask, a, 1), mask),
                lambda mask, g: (jnp.zeros_like(mask), _maskdot(mask, g, 0)))


@jax.custom_vjp
def _saved_inverse(lmat, x):
    return x


def _saved_inverse_bwd(x, g):
    t = lax.dot_general(x, g, (((0,), (0,)), ((), ())), precision=lax.Precision.HIGH, preferred_element_type=F32)
    dl = lax.dot_general(t, x, (((1,), (1,)), ((), ())), precision=lax.Precision.HIGH, preferred_element_type=F32)
    return -dl, jnp.zeros_like(x)


_saved_inverse.defvjp(lambda lmat, x: (x, x), _saved_inverse_bwd)


def _row_ids(shape):
    return lax.broadcasted_iota(jnp.int32, shape, 0)


def _shift_rows(x, down, bounds):
    n = x.shape[0]
    rows = _row_ids(x.shape)
    y = pltpu.roll(x, 1 if down else n - 1, 0)
    edge = functools.reduce(jnp.logical_or, [rows == (s if down else e - 1) for s, e in bounds])
    return jnp.where(edge, 0.0, y)


def _make_shift(bounds):
    @jax.custom_vjp
    def down(x):
        return _shift_rows(x, True, bounds)

    @jax.custom_vjp
    def up(x):
        return _shift_rows(x, False, bounds)

    down.defvjp(lambda x: (down(x), None), lambda _, g: (up(g),))
    up.defvjp(lambda x: (up(x), None), lambda _, g: (down(g),))
    return down, up


@jax.custom_vjp
def _swap32(x):
    lane = lax.broadcasted_iota(jnp.int32, x.shape, x.ndim - 1)
    return jnp.where((lane % 64) < 32, pltpu.roll(x, HD - 32, x.ndim - 1), pltpu.roll(x, 32, x.ndim - 1))


_swap32.defvjp(lambda x: (_swap32(x), None), lambda _, g: (_swap32(g),))


def _rms(x):
    return x * lax.rsqrt(jnp.mean(x * x, axis=-1, keepdims=True) + EPS)


def _silu(x):
    return x * jax.nn.sigmoid(x)


def _mm(a, b, *, name, M, N, K, ta=False, tb=False, out_dtype=F32, bm=None, bn=None, bk=None,
        a_off=(0, 0), b_off=(0, 0), after=()):
    bm, bn, bk = bm or M, bn or N, bk or K
    assert M % bm == 0 and N % bn == 0 and K % bk == 0, (name, M, N, K, bm, bn, bk)
    nk = K // bk
    ca, cb = (0 if ta else 1), (1 if tb else 0)
    na = len(after)

    def body(a_ref, b_ref, *rest):
        o_ref, acc = rest[na], rest[na + 1:]
        r = _dot(a_ref[...], b_ref[...], ca, cb)
        if nk == 1:
            o_ref[...] = r.astype(out_dtype)
        else:
            acc_ref, = acc
            k = pl.program_id(2)

            @pl.when(k == 0)
            def _():
                acc_ref[...] = r

            @pl.when(k > 0)
            def _():
                acc_ref[...] += r

            @pl.when(k == nk - 1)
            def _():
                o_ref[...] = acc_ref[...].astype(out_dtype)

    def blk(off, bshape):
        assert off[0] % bshape[0] == 0 and off[1] % bshape[1] == 0, (name, off, bshape)
        return off[0] // bshape[0], off[1] // bshape[1]

    if ta:
        ao = blk(a_off, (bk, bm))
        a_spec = pl.BlockSpec((bk, bm), lambda i, j, k: (k + ao[0], i + ao[1]))
    else:
        ao = blk(a_off, (bm, bk))
        a_spec = pl.BlockSpec((bm, bk), lambda i, j, k: (i + ao[0], k + ao[1]))
    if tb:
        bo = blk(b_off, (bn, bk))
        b_spec = pl.BlockSpec((bn, bk), lambda i, j, k: (j + bo[0], k + bo[1]))
    else:
        bo = blk(b_off, (bk, bn))
        b_spec = pl.BlockSpec((bk, bn), lambda i, j, k: (k + bo[0], j + bo[1]))
    return _call(body, name=name, out_shape=_sds((M, N), out_dtype), grid=(M // bm, N // bn, nk),
                 in_specs=[a_spec, b_spec] + [pl.BlockSpec(memory_space=pl.ANY)] * na,
                 out_specs=pl.BlockSpec((bm, bn), lambda i, j, k: (i, j)),
                 scratch=[pltpu.VMEM((bm, bn), F32)] if nk > 1 else [],
                 sem=("parallel", "parallel", "arbitrary"), vmem=VMEM_BIG)(a, b, *after)


def _normmod_fn(x, sh, sc):
    return _rms(x) * (1.0 + sc) + sh


def _normmod_fwd(x, mod, i_sh, i_sc, *, name, br=256):
    R = x.shape[0]

    def body(x_ref, mod_ref, o_ref):
        o_ref[...] = _normmod_fn(x_ref[...], mod_ref[i_sh:i_sh + 1, :], mod_ref[i_sc:i_sc + 1, :]).astype(BF16)

    return _call(body, name=name, out_shape=_sds((R, D), BF16), grid=(R // br,),
                 in_specs=[pl.BlockSpec((br, D), lambda i: (i, 0)), pl.BlockSpec((6, D), lambda i: (0, 0))],
                 out_specs=pl.BlockSpec((br, D), lambda i: (i, 0)), sem=("parallel",))(x, mod)


def _normmod_bwd(x, mod, i_sh, i_sc, dh, dh_off, res, *, name, br=256):
    R = x.shape[0]
    ob = dh_off // br
    has_res = res is not None

    def body(x_ref, mod_ref, dh_ref, *rest):
        if has_res:
            res_ref, dx_ref, dsh_ref, dsc_ref = rest
        else:
            dx_ref, dsh_ref, dsc_ref = rest
        sh, sc = mod_ref[i_sh:i_sh + 1, :], mod_ref[i_sc:i_sc + 1, :]
        _, vjp = jax.vjp(_normmod_fn, x_ref[...], sh, sc)
        dx, dsh, dsc = vjp(dh_ref[...])
        dx_ref[...] = dx + res_ref[...] if has_res else dx

        @pl.when(pl.program_id(0) == 0)
        def _():
            dsh_ref[...] = jnp.zeros_like(dsh_ref)
            dsc_ref[...] = jnp.zeros_like(dsc_ref)

        dsh_ref[...] += dsh
        dsc_ref[...] += dsc

    row = pl.BlockSpec((br, D), lambda i: (i, 0))
    vec = pl.BlockSpec((1, D), lambda i: (0, 0))
    ins = [row, pl.BlockSpec((6, D), lambda i: (0, 0)), pl.BlockSpec((br, D), lambda i: (i + ob, 0))]
    args = [x, mod, dh]
    if has_res:
        ins.append(row)
        args.append(res)
    return _call(body, name=name, out_shape=(_sds((R, D)), _sds((1, D)), _sds((1, D))), grid=(R // br,),
                 in_specs=ins, out_specs=(row, vec, vec), sem=("arbitrary",))(*args)


def _rope(x, cos, sin):
    return x * cos + _swap32(x) * sin


def _aprep_fn(qs, ks, cos, sin, qw, kw):
    return ([_rope(_rms(q) * qw, cos, sin) for q in qs], [_rope(_rms(k) * kw, cos, sin) for k in ks])


def _aprep_fwd(proj, cos, sin, qw, kw, *, br=256):
    T = proj.shape[0]

    def body(aq_ref, kv_ref, cos_ref, sin_ref, qw_ref, kw_ref, q_ref, k_ref, v_ref):
        qs = [aq_ref[:, h * HD:(h + 1) * HD] for h in range(AH)]
        ks = [kv_ref[:, h * HD:(h + 1) * HD] for h in range(AKV)]
        qo, ko = _aprep_fn(qs, ks, cos_ref[...], sin_ref[...], qw_ref[...], kw_ref[...])
        for h in range(AH):
            q_ref[h] = qo[h].astype(BF16)
        for h in range(AKV):
            k_ref[h] = ko[h].astype(BF16)
            v_ref[h] = kv_ref[:, (AKV + h) * HD:(AKV + h + 1) * HD].astype(BF16)

    tab = pl.BlockSpec((br, HD), lambda i: (i, 0))
    vec = pl.BlockSpec((1, HD), lambda i: (0, 0))
    return _call(body, name="aprep_fwd",
                 out_shape=(_sds((AH, T, HD), BF16), _sds((AKV, T, HD), BF16), _sds((AKV, T, HD), BF16)),
                 grid=(T // br,),
                 in_specs=[pl.BlockSpec((br, AH * HD), lambda i: (i, C_AQ // (AH * HD))),
                           pl.BlockSpec((br, 2 * AKV * HD), lambda i: (i, 0)), tab, tab, vec, vec],
                 out_specs=(pl.BlockSpec((AH, br, HD), lambda i: (0, i, 0)),
                            pl.BlockSpec((AKV, br, HD), lambda i: (0, i, 0)),
                            pl.BlockSpec((AKV, br, HD), lambda i: (0, i, 0))),
                 sem=("parallel",))(proj, proj, cos, sin, qw, kw)


def _aprep_bwd(proj, cos, sin, qw, kw, dq, dk, dv, L, *, br=256):
    T = proj.shape[0]
    lb = L // br

    def body(aq_ref, kv_ref, cos_ref, sin_ref, qw_ref, kw_ref, dq_ref, dk_ref, dv_ref,
             daq_ref, dkv_ref, dqw_ref, dkw_ref):
        i = pl.program_id(0)
        qs = [aq_ref[:, h * HD:(h + 1) * HD] for h in range(AH)]
        ks = [kv_ref[:, h * HD:(h + 1) * HD] for h in range(AKV)]
        _, vjp = jax.vjp(_aprep_fn, qs, ks, cos_ref[...], sin_ref[...], qw_ref[...], kw_ref[...])
        is_lat = i >= lb
        dqs = [jnp.where(is_lat, dq_ref[h], 0.0) for h in range(AH)]
        dks = [dk_ref[h] for h in range(AKV)]
        gq, gk, _, _, gqw, gkw = vjp((dqs, dks))
        for h in range(AH):
            daq_ref[:, h * HD:(h + 1) * HD] = gq[h].astype(BF16)
        for h in range(AKV):
            dkv_ref[:, h * HD:(h + 1) * HD] = gk[h].astype(BF16)
            dkv_ref[:, (AKV + h) * HD:(AKV + h + 1) * HD] = dv_ref[h].astype(BF16)

        @pl.when(i == 0)
        def _():
            dqw_ref[...] = jnp.zeros_like(dqw_ref)
            dkw_ref[...] = jnp.zeros_like(dkw_ref)

        dqw_ref[...] += gqw
        dkw_ref[...] += gkw

    tab = pl.BlockSpec((br, HD), lambda i: (i, 0))
    vec = pl.BlockSpec((1, HD), lambda i: (0, 0))
    kvb = pl.BlockSpec((AKV, br, HD), lambda i: (0, i, 0))
    return _call(body, name="aprep_bwd",
                 out_shape=(_sds((T, AH * HD), BF16), _sds((T, 2 * AKV * HD), BF16), _sds((1, HD)), _sds((1, HD))),
                 grid=(T // br,),
                 in_specs=[pl.BlockSpec((br, AH * HD), lambda i: (i, C_AQ // (AH * HD))),
                           pl.BlockSpec((br, 2 * AKV * HD), lambda i: (i, 0)), tab, tab, vec, vec,
                           pl.BlockSpec((AH, br, HD), lambda i: (0, jnp.maximum(i - lb, 0), 0)), kvb, kvb],
                 out_specs=(pl.BlockSpec((br, AH * HD), lambda i: (i, 0)),
                            pl.BlockSpec((br, 2 * AKV * HD), lambda i: (i, 0)), vec, vec),
                 sem=("arbitrary",))(proj, proj, cos, sin, qw, kw, dq, dk, dv)


def _attn_fn(q, k, v):
    s = _nt(q, k) * (HD ** -0.5)
    m = lax.stop_gradient(jnp.max(s, axis=-1, keepdims=True))
    e = jnp.exp(s - m)
    p = e / jnp.sum(e, axis=-1, keepdims=True)
    return _nn(p, v)


def _attn_fwd(q, k, v, L, exch, *, bq=128):
    T = q.shape[1]
    N = T - L
    lb = L // bq

    def body(q_ref, k_ref, v_ref, o_ref):
        qv = q_ref[...].reshape(GRP * bq, HD).astype(F32)
        o = _attn_fn(qv, k_ref[...].astype(F32), v_ref[...].astype(F32))
        for g in range(GRP):
            o_ref[:, g * HD:(g + 1) * HD] = o[g * bq:(g + 1) * bq].astype(BF16)

    kvb = pl.BlockSpec((None, T, HD), lambda g, i: (g, 0, 0))
    (attn,), moved = _call_carrying(
        body, exch, name="attn_fwd", out_shape=(_sds((N, AH * HD), BF16),), grid=(AKV, N // bq),
        in_specs=[pl.BlockSpec((GRP, bq, HD), lambda g, i: (g, i + lb, 0)), kvb, kvb],
        out_specs=(pl.BlockSpec((bq, GRP * HD), lambda g, i: (i, g)),), vmem=VMEM_BIG)(q, k, v)
    return attn, moved


def _attn_bwd(q, k, v, do, L, *, bq=128):
    T = q.shape[1]
    N = T - L
    lb = L // bq

    def body(q_ref, k_ref, v_ref, do_ref, dq_ref, dk_ref, dv_ref):
        qv = q_ref[...].reshape(GRP * bq, HD).astype(F32)
        _, vjp = jax.vjp(_attn_fn, qv, k_ref[...].astype(F32), v_ref[...].astype(F32))
        dov = jnp.concatenate([do_ref[:, g * HD:(g + 1) * HD] for g in range(GRP)], axis=0)
        dq, dk, dv = vjp(dov)
        dq_ref[...] = dq.reshape(GRP, bq, HD)

        @pl.when(pl.program_id(1) == 0)
        def _():
            dk_ref[...] = jnp.zeros_like(dk_ref)
            dv_ref[...] = jnp.zeros_like(dv_ref)

        dk_ref[...] += dk
        dv_ref[...] += dv

    kvb = pl.BlockSpec((None, T, HD), lambda g, i: (g, 0, 0))
    return _call(body, name="attn_bwd",
                 out_shape=(_sds((AH, N, HD)), _sds((AKV, T, HD)), _sds((AKV, T, HD))), grid=(AKV, N // bq),
                 in_specs=[pl.BlockSpec((GRP, bq, HD), lambda g, i: (g, i + lb, 0)), kvb, kvb,
                           pl.BlockSpec((bq, GRP * HD), lambda g, i: (i, g))],
                 out_specs=(pl.BlockSpec((GRP, bq, HD), lambda g, i: (g, i, 0)), kvb, kvb),
                 sem=("parallel", "arbitrary"), vmem=VMEM_BIG)(q, k, v, do)


def _gprep_fn(kind, shifts, x, w):
    down, up = shifts
    y = down(x) * w[0:1, :] + x * w[1:2, :] + up(x) * w[2:3, :]
    a = _silu(y)
    if kind == 2:
        return a
    a = a * lax.rsqrt(jnp.sum(a * a, axis=-1, keepdims=True) + EPS)
    return a * (HD ** -0.5) if kind == 0 else a


def _gprep_fwd(proj, conv_w, kind, bounds):
    T = proj.shape[0]
    shifts = _make_shift(bounds)
    cb = C_QKV // HD + kind * GH

    def body(x_ref, w_ref, o_ref):
        o_ref[...] = _gprep_fn(kind, shifts, x_ref[...], w_ref[...])

    return _call(body, name=f"gprep_fwd{kind}", out_shape=_sds((GH, T, HD)), grid=(GH,),
                 in_specs=[pl.BlockSpec((T, HD), lambda h: (0, cb + h)),
                           pl.BlockSpec((3, HD), lambda h: (0, kind * GH + h))],
                 out_specs=pl.BlockSpec((None, T, HD), lambda h: (h, 0, 0)), sem=("parallel",))(proj, conv_w)


def _gprep_bwd(proj, conv_w, kind, bounds, dy):
    T = proj.shape[0]
    shifts = _make_shift(bounds)
    cb = C_QKV // HD + kind * GH

    def body(x_ref, w_ref, dy_ref, dx_ref, dw_ref):
        _, vjp = jax.vjp(functools.partial(_gprep_fn, kind, shifts), x_ref[...], w_ref[...])
        dx, dw = vjp(dy_ref[0] + dy_ref[1])
        dx_ref[...] = dx.astype(BF16)
        dw_ref[...] = dw

    return _call(body, name=f"gprep_bwd{kind}", out_shape=(_sds((T, GH * HD), BF16), _sds((3, GH * HD))), grid=(GH,),
                 in_specs=[pl.BlockSpec((T, HD), lambda h: (0, cb + h)),
                           pl.BlockSpec((3, HD), lambda h: (0, kind * GH + h)),
                           pl.BlockSpec((2, None, T, HD), lambda h: (0, h, 0, 0))],
                 out_specs=(pl.BlockSpec((T, HD), lambda h: (0, h)), pl.BlockSpec((3, HD), lambda h: (0, h))),
                 sem=("parallel",))(proj, conv_w, dy)


def _bl_fn(x, alog, dtb):
    lane = lax.broadcasted_iota(jnp.int32, x.shape, 1)
    beta = jax.nn.sigmoid(x)
    z = x + dtb
    sp = jnp.maximum(z, 0.0) + jnp.log1p(jnp.exp(-jnp.abs(z)))
    la = -jnp.exp(alog) * sp
    return jnp.where(lane < 2 * GH, beta, jnp.where(lane < 4 * GH, la, 0.0))


def _bl_fwd(proj, alog, dtb, *, br=256):
    T = proj.shape[0]

    def body(x_ref, a_ref, d_ref, o_ref):
        o_ref[...] = _bl_fn(x_ref[...], a_ref[...], d_ref[...])

    vec = pl.BlockSpec((1, HD), lambda i: (0, 0))
    return _call(body, name="bl_fwd", out_shape=_sds((T, HD)), grid=(T // br,),
                 in_specs=[pl.BlockSpec((br, HD), lambda i: (i, C_BL // HD)), vec, vec],
                 out_specs=pl.BlockSpec((br, HD), lambda i: (i, 0)), sem=("parallel",))(proj, alog, dtb)


def _bl_bwd(proj, alog, dtb, dbl, *, br=256):
    T = proj.shape[0]

    def body(x_ref, a_ref, d_ref, g_ref, dx_ref, da_ref, dd_ref):
        g = g_ref[0, 0]
        for d in range(2):
            for h in range(GH):
                if d or h:
                    g = g + g_ref[d, h]
        _, vjp = jax.vjp(_bl_fn, x_ref[...], a_ref[...], d_ref[...])
        dx, da, dd = vjp(g)
        dx_ref[...] = dx.astype(BF16)

        @pl.when(pl.program_id(0) == 0)
        def _():
            da_ref[...] = jnp.zeros_like(da_ref)
            dd_ref[...] = jnp.zeros_like(dd_ref)

        da_ref[...] += da
        dd_ref[...] += dd

    vec = pl.BlockSpec((1, HD), lambda i: (0, 0))
    return _call(body, name="bl_bwd", out_shape=(_sds((T, HD), BF16), _sds((1, HD)), _sds((1, HD))), grid=(T // br,),
                 in_specs=[pl.BlockSpec((br, HD), lambda i: (i, C_BL // HD)), vec, vec,
                           pl.BlockSpec((2, GH, br, HD), lambda i: (0, 0, i, 0))],
                 out_specs=(pl.BlockSpec((br, HD), lambda i: (i, 0)), vec, vec), sem=("arbitrary",))(proj, alog, dtb, dbl)


def _chunk_masks(d):
    ii = lax.broadcasted_iota(jnp.int32, (CH, CH), 0)
    jj = lax.broadcasted_iota(jnp.int32, (CH, CH), 1)
    eye = (ii == jj).astype(F32)
    before = jnp.where(d == 0, (jj < ii).astype(F32), (jj > ii).astype(F32))
    return before, before + eye, eye


def _intra_fn(masks, sel_b, sel_l, qs, ks, vs, bls, xs=None):
    before, ateq, eye = masks
    ones = jnp.ones((CH, CH), F32)
    inc = ateq > 0.0
    each = lambda f, *ls: [f(*t) for t in zip(*ls)]
    beta = each(lambda bl: jnp.sum(bl * sel_b, axis=-1, keepdims=True), bls)
    la = each(lambda bl: jnp.sum(bl * sel_l, axis=-1, keepdims=True), bls)
    gam = each(lambda a: _mask_nn(ateq, jnp.broadcast_to(a, (CH, HD))), la)
    gi = each(lambda a: _mask_nn(ateq, jnp.broadcast_to(a, (CH, CH))), la)
    gj = each(lambda g: _mask_nn(ones, eye * g), gi)
    kk = each(lambda k: _nt(k, k), ks)
    qk = each(_nt, qs, ks)
    dec = each(lambda a, b: jnp.where(inc, jnp.exp(jnp.where(inc, a - b, 0.0)), 0.0), gi, gj)
    lmat = each(lambda b, d, m: before * (b * d * m), beta, dec, kk)
    if xs is None:
        x = each(lambda m: eye - m, lmat)
        p2 = each(lambda m: _mdot(m, m), lmat)
        for it in range(5):
            x = each(lambda a, b: a + _mdot(a, b), x, p2)
            if it < 4:
                p2 = each(lambda m: _mdot(m, m), p2)
    else:
        x = each(_saved_inverse, lmat, xs)
    eg = each(jnp.exp, gam)
    u = each(lambda a, b, v: _mdot(a, b * v), x, beta, vs)
    w = each(lambda a, b, e, k: _mdot(a, (b * e) * k), x, beta, eg, ks)
    tot = each(lambda a: jnp.sum(a, axis=0, keepdims=True), la)
    kd = each(lambda k, t, g: k * jnp.exp(t - g), ks, tot, gam)
    gl = each(lambda t: jnp.broadcast_to(jnp.exp(t), (1, HD)), tot)
    qd = each(lambda q, e: q * e, qs, eg)
    p = each(lambda d, m: d * m, dec, qk)
    return (u, w, kd, qd, p, gl, x) if xs is None else (u, w, kd, qd, p, gl)


def _dir_head_sel(d, h):
    lane = lax.broadcasted_iota(jnp.int32, (1, HD), 1)
    return (lane == d * GH + h).astype(F32), (lane == 2 * GH + d * GH + h).astype(F32)


def _intra_specs(T, G):
    nc = T // CH
    assert nc % G == 0
    qkv = pl.BlockSpec((None, G * CH, HD), lambda d, h, c: (h, c, 0))
    bl = pl.BlockSpec((G * CH, HD), lambda d, h, c: (c, 0))
    big = pl.BlockSpec((None, None, G * CH, HD), lambda d, h, c: (d, h, c, 0))
    pm = pl.BlockSpec((None, None, G * CH, CH), lambda d, h, c: (d, h, c, 0))
    gl = pl.BlockSpec((None, None, G, 1, HD), lambda d, h, c: (d, h, c, 0, 0))
    shapes = (_sds((2, GH, T, HD)),) + (_sds((2, GH, T, HD), BF16),) * 3 + (
        _sds((2, GH, T, CH), BF16), _sds((2, GH, nc, 1, HD)), _sds((2, GH, T, CH)))
    return nc, qkv, bl, big, pm, gl, shapes


def _chunks_per_step(T, most):
    nc = T // CH
    return max(g for g in range(1, most + 1) if nc % g == 0)


def _intra_fwd(q, k, v, bl, exch):
    T = q.shape[1]
    G = _chunks_per_step(T, INTRA_FWD_CHUNKS)
    nc, qkv_s, bl_s, big, pm, gl_s, shapes = _intra_specs(T, G)

    def body(q_ref, k_ref, v_ref, bl_ref, u_ref, w_ref, kd_ref, qd_ref, p_ref, gl_ref, x_ref):
        d, h = pl.program_id(0), pl.program_id(1)
        sb, sl = _dir_head_sel(d, h)
        rows = [slice(g * CH, (g + 1) * CH) for g in range(G)]
        outs = _intra_fn(_chunk_masks(d), sb, sl, *[[r[s, :] for s in rows] for r in (q_ref, k_ref, v_ref, bl_ref)])
        for g in range(G):
            for r, o in zip((u_ref, w_ref, kd_ref, qd_ref, p_ref, x_ref), outs[:5] + outs[6:]):
                r[rows[g], :] = o[g].astype(r.dtype)
            gl_ref[g] = outs[5][g]

    return _call_carrying(body, exch, name="gdn_intra_fwd", out_shape=shapes, grid=(2, GH, nc // G),
                          in_specs=[qkv_s, qkv_s, qkv_s, bl_s], out_specs=(big, big, big, big, pm, gl_s, pm))(q, k, v, bl)


def _intra_bwd(q, k, v, bl, xinv, cts, exch):
    T = q.shape[1]
    G = _chunks_per_step(T, INTRA_BWD_CHUNKS)
    nc, qkv_s, bl_s, big, pm, gl_s, _ = _intra_specs(T, G)

    def body(q_ref, k_ref, v_ref, bl_ref, x_ref, du, dw, dkd, dqd, dp, dgl, dq_ref, dk_ref, dv_ref, dbl_ref):
        d, h = pl.program_id(0), pl.program_id(1)
        sb, sl = _dir_head_sel(d, h)
        rows = [slice(g * CH, (g + 1) * CH) for g in range(G)]
        fn = functools.partial(_intra_fn, _chunk_masks(d), sb, sl, xs=[x_ref[s, :] for s in rows])
        _, vjp = jax.vjp(fn, *[[r[s, :] for s in rows] for r in (q_ref, k_ref, v_ref, bl_ref)])
        cts = tuple([r[s, :] for s in rows] for r in (du, dw, dkd, dqd, dp)) + ([dgl[g] for g in range(G)],)
        grads = vjp(cts)
        for g in range(G):
            for r, o in zip((dq_ref, dk_ref, dv_ref, dbl_ref), grads):
                r[rows[g], :] = o[g]

    return _call_carrying(body, exch, name="gdn_intra_bwd", out_shape=(_sds((2, GH, T, HD)),) * 4,
                          grid=(2, GH, nc // G), in_specs=[qkv_s, qkv_s, qkv_s, bl_s, pm, big, big, big, big, pm, gl_s],
                          out_specs=(big,) * 4)(q, k, v, bl, xinv, *cts)


def _scan_fn(s, u, w, kd, qd, p, gl):
    each = lambda f, *ls: [f(*t) for t in zip(*ls)]
    ws = each(_nn, w, s)
    delta = each(lambda a, b: a - b, u, ws)
    kdd = each(_tn, kd, delta)
    s_new = each(lambda g, a, b: g * a + b, gl, s, kdd)
    qs = each(_nn, qd, s)
    pd = each(_nn, p, delta)
    return each(lambda a, b: a + b, qs, pd), s_new


SCAN_BLOCK = 4


def _scan_visit(t, d, nb, ncb):
    rev = jnp.where(t < ncb, ncb - 1 - t, nb - 1 - (t - ncb))
    return jnp.where(d == 0, t, rev)


def _scan_specs(T, L, back):
    tb = SCAN_BLOCK * CH
    assert T % tb == 0 and L % tb == 0
    nb, ncb = T // tb, L // tb

    def at(d, t):
        return _scan_visit(nb - 1 - t if back else t, d, nb, ncb)

    big = pl.BlockSpec((None, GH, tb, HD), lambda d, t: (d, 0, at(d, t), 0))
    pm = pl.BlockSpec((None, GH, tb, CH), lambda d, t: (d, 0, at(d, t), 0))
    gl = pl.BlockSpec((None, GH, SCAN_BLOCK, 1, HD), lambda d, t: (d, 0, at(d, t), 0, 0))
    st = pl.BlockSpec((None, GH, SCAN_BLOCK, HD, HD), lambda d, t: (d, 0, at(d, t), 0, 0))
    do = pl.BlockSpec((GH, tb, HD), lambda d, t: (0, at(d, t), 0))
    return nb, big, pm, gl, st, do


def _scan_fwd(u, w, kd, qd, p, gl, L):
    T = u.shape[2]
    nb, big, pm, gl_s, st, _ = _scan_specs(T, L, False)
    heads = range(GH)

    def body(u_ref, w_ref, kd_ref, qd_ref, p_ref, gl_ref, o_ref, st_ref, s_scr):
        d = pl.program_id(0)

        @pl.when(pl.program_id(1) == 0)
        def _():
            s_scr[...] = jnp.zeros_like(s_scr)

        s = [s_scr[h] for h in heads]
        for i in range(SCAN_BLOCK):
            c = jnp.where(d == 0, i, SCAN_BLOCK - 1 - i)
            rows = pl.ds(pl.multiple_of(c * CH, CH), CH)
            for h in heads:
                st_ref[h, c] = s[h]
            o, s = _scan_fn(s, *[[r[h, rows, :].astype(F32) for h in heads] for r in (u_ref, w_ref, kd_ref, qd_ref, p_ref)],
                            [gl_ref[h, c] for h in heads])
            for h in heads:
                o_ref[h, rows, :] = o[h]
        for h in heads:
            s_scr[h] = s[h]

    return _call(body, name="gdn_scan_fwd", out_shape=(_sds((2, GH, T, HD)), _sds((2, GH, T // CH, HD, HD))),
                 grid=(2, nb), in_specs=[big, big, big, big, pm, gl_s], out_specs=(big, st),
                 scratch=[pltpu.VMEM((GH, HD, HD), F32)], sem=("parallel", "arbitrary"))(u, w, kd, qd, p, gl)


def _scan_bwd(u, w, kd, qd, p, gl, states, do, L):
    T = u.shape[2]
    nb, big, pm, gl_s, st, do_s = _scan_specs(T, L, True)
    heads = range(GH)

    def body(u_ref, w_ref, kd_ref, qd_ref, p_ref, gl_ref, st_ref, do_ref,
             du_ref, dw_ref, dkd_ref, dqd_ref, dp_ref, dgl_ref, ds_scr):
        d = pl.program_id(0)

        @pl.when(pl.program_id(1) == 0)
        def _():
            ds_scr[...] = jnp.zeros_like(ds_scr)

        ds = [ds_scr[h] for h in heads]
        for i in range(SCAN_BLOCK):
            c = jnp.where(d == 0, SCAN_BLOCK - 1 - i, i)
            rows = pl.ds(pl.multiple_of(c * CH, CH), CH)
            _, vjp = jax.vjp(_scan_fn, [st_ref[h, c] for h in heads],
                             *[[r[h, rows, :].astype(F32) for h in heads] for r in (u_ref, w_ref, kd_ref, qd_ref, p_ref)],
                             [gl_ref[h, c] for h in heads])
            ds, gu, gw, gkd, gqd, gp, ggl = vjp(([do_ref[h, rows, :] for h in heads], ds))
            for h in heads:
                du_ref[h, rows, :] = gu[h]
                dw_ref[h, rows, :] = gw[h]
                dkd_ref[h, rows, :] = gkd[h]
                dqd_ref[h, rows, :] = gqd[h]
                dp_ref[h, rows, :] = gp[h]
                dgl_ref[h, c] = ggl[h]
        for h in heads:
            ds_scr[h] = ds[h]

    return _call(body, name="gdn_scan_bwd",
                 out_shape=(_sds((2, GH, T, HD)),) * 4 + (_sds((2, GH, T, CH)), _sds((2, GH, T // CH, 1, HD))),
                 grid=(2, nb), in_specs=[big, big, big, big, pm, gl_s, st, do_s],
                 out_specs=(big, big, big, big, pm, gl_s), scratch=[pltpu.VMEM((GH, HD, HD), F32)],
                 sem=("parallel", "arbitrary"))(u, w, kd, qd, p, gl, states, do)


def _gout_fn(o0, o1, z, gw):
    return _rms(o0 + o1) * gw * _silu(z)


def _gout_fwd(o, proj, gw, L, *, br=256):
    T = o.shape[2]
    N = T - L
    lb = L // br
    ob = pl.BlockSpec((None, None, br, HD), lambda i, h: (0, h, i + lb, 0))
    ob1 = pl.BlockSpec((None, None, br, HD), lambda i, h: (1, h, i + lb, 0))

    def body(o0_ref, o1_ref, z_ref, gw_ref, y_ref):
        y_ref[...] = _gout_fn(o0_ref[...], o1_ref[...], z_ref[...], gw_ref[...]).astype(BF16)

    return _call(body, name="gout_fwd", out_shape=_sds((N, GH * HD), BF16), grid=(N // br, GH),
                 in_specs=[ob, ob1, pl.BlockSpec((br, HD), lambda i, h: (i + lb, C_Z // HD + h)),
                           pl.BlockSpec((1, HD), lambda i, h: (0, 0))],
                 out_specs=pl.BlockSpec((br, HD), lambda i, h: (i, h)), sem=("parallel", "parallel"))(o, o, proj, gw)


def _gout_bwd(o, proj, gw, dy, L, *, br=256):
    T = o.shape[2]
    lb = L // br
    ob = pl.BlockSpec((None, None, br, HD), lambda i, h: (0, h, i, 0))
    ob1 = pl.BlockSpec((None, None, br, HD), lambda i, h: (1, h, i, 0))

    def body(o0_ref, o1_ref, z_ref, gw_ref, dy_ref, do_ref, dz_ref, dgw_ref):
        i, h = pl.program_id(0), pl.program_id(1)
        _, vjp = jax.vjp(_gout_fn, o0_ref[...], o1_ref[...], z_ref[...], gw_ref[...])
        g0, _, gz, ggw = vjp(dy_ref[...])
        lat = i >= lb
        do_ref[...] = jnp.where(lat, g0, 0.0)
        dz_ref[...] = jnp.where(lat, gz, 0.0).astype(BF16)

        @pl.when(jnp.logical_and(i == 0, h == 0))
        def _():
            dgw_ref[...] = jnp.zeros_like(dgw_ref)

        dgw_ref[...] += jnp.where(lat, ggw, 0.0)

    return _call(body, name="gout_bwd", out_shape=(_sds((GH, T, HD)), _sds((T, GH * HD), BF16), _sds((1, HD))),
                 grid=(T // br, GH),
                 in_specs=[ob, ob1, pl.BlockSpec((br, HD), lambda i, h: (i, C_Z // HD + h)),
                           pl.BlockSpec((1, HD), lambda i, h: (0, 0)),
                           pl.BlockSpec((br, HD), lambda i, h: (jnp.maximum(i - lb, 0), h))],
                 out_specs=(pl.BlockSpec((None, br, HD), lambda i, h: (h, i, 0)),
                            pl.BlockSpec((br, HD), lambda i, h: (i, h)),
                            pl.BlockSpec((1, HD), lambda i, h: (0, 0))),
                 sem=("arbitrary", "arbitrary"))(o, o, proj, gw, dy)


def _merge_fn(pa, pd, ga, gd):
    return jax.nn.sigmoid(ga) * pa + jax.nn.sigmoid(gd) * pd


def _merge_fwd(pa, pd, proj, L, *, br=256):
    N = pa.shape[0]
    lb = L // br
    row = pl.BlockSpec((br, D), lambda i: (i, 0))

    def body(pa_ref, pd_ref, ga_ref, gd_ref, y_ref):
        y_ref[...] = _merge_fn(pa_ref[...], pd_ref[...], ga_ref[...], gd_ref[...]).astype(BF16)

    return _call(body, name="merge_fwd", out_shape=_sds((N, D), BF16), grid=(N // br,),
                 in_specs=[row, row, pl.BlockSpec((br, D), lambda i: (i + lb, C_GATE // D)),
                           pl.BlockSpec((br, D), lambda i: (i + lb, C_GATE // D + 1))],
                 out_specs=row, sem=("parallel",))(pa, pd, proj, proj)


def _merge_bwd(pa, pd, proj, dy, L, *, br=256):
    N = pa.shape[0]
    T = N + L
    lb = L // br
    lrow = pl.BlockSpec((br, D), lambda i: (jnp.maximum(i - lb, 0), 0))

    def body(pa_ref, pd_ref, ga_ref, gd_ref, dy_ref, dpa_ref, dpd_ref, dg_ref):
        lat = pl.program_id(0) >= lb
        _, vjp = jax.vjp(_merge_fn, pa_ref[...], pd_ref[...], ga_ref[...], gd_ref[...])
        gpa, gpd, gga, ggd = vjp(dy_ref[...])
        dpa_ref[...] = gpa.astype(BF16)
        dpd_ref[...] = gpd.astype(BF16)
        dg_ref[:, :D] = jnp.where(lat, gga, 0.0).astype(BF16)
        dg_ref[:, D:] = jnp.where(lat, ggd, 0.0).astype(BF16)

    return _call(body, name="merge_bwd", out_shape=(_sds((N, D), BF16), _sds((N, D), BF16), _sds((T, 2 * D), BF16)),
                 grid=(T // br,),
                 in_specs=[lrow, lrow, pl.BlockSpec((br, D), lambda i: (i, C_GATE // D)),
                           pl.BlockSpec((br, D), lambda i: (i, C_GATE // D + 1)), lrow],
                 out_specs=(lrow, lrow, pl.BlockSpec((br, 2 * D), lambda i: (i, 0))),
                 sem=("arbitrary",))(pa, pd, proj, proj, dy)


def _resid_fwd(x, m, mod, i_g, *, name, br=256):
    R = x.shape[0]
    row = pl.BlockSpec((br, D), lambda i: (i, 0))

    def body(x_ref, m_ref, mod_ref, o_ref):
        o_ref[...] = x_ref[...] + mod_ref[i_g:i_g + 1, :] * m_ref[...]

    return _call(body, name=name, out_shape=_sds((R, D)), grid=(R // br,),
                 in_specs=[row, row, pl.BlockSpec((6, D), lambda i: (0, 0))], out_specs=row,
                 sem=("parallel",))(x, m, mod)


def _resid_bwd(dx, m, mod, i_g, *, name, br=256):
    R = dx.shape[0]
    row = pl.BlockSpec((br, D), lambda i: (i, 0))
    vec = pl.BlockSpec((1, D), lambda i: (0, 0))

    def body(dx_ref, m_ref, mod_ref, dm_ref, dg_ref):
        dxv = dx_ref[...]
        dm_ref[...] = (dxv * mod_ref[i_g:i_g + 1, :]).astype(BF16)

        @pl.when(pl.program_id(0) == 0)
        def _():
            dg_ref[...] = jnp.zeros_like(dg_ref)

        dg_ref[...] += jnp.sum(dxv * m_ref[...], axis=0, keepdims=True)

    return _call(body, name=name, out_shape=(_sds((R, D), BF16), _sds((1, D))), grid=(R // br,),
                 in_specs=[row, row, pl.BlockSpec((6, D), lambda i: (0, 0))], out_specs=(row, vec),
                 sem=("arbitrary",))(dx, m, mod)


def _ffn_fn(shifts, ug, uv, wg, wv, bg, bv):
    down, up = shifts

    def conv(x, w, b):
        return down(x) * w[0:1, :] + x * w[1:2, :] + up(x) * w[2:3, :] + b

    return _silu(conv(ug, wg, bg)) * conv(uv, wv, bv)


def _ffn_fwd(up, cw, cb, *, bw=256):
    N = up.shape[0]
    shifts = _make_shift(((0, N),))
    nb = DFF // bw

    def body(ug, uv, wg, wv, bg, bv, a_ref):
        a_ref[...] = _ffn_fn(shifts, ug[...], uv[...], wg[...], wv[...], bg[...], bv[...]).astype(BF16)

    def col(rows, off):
        return pl.BlockSpec((rows, bw), lambda j: (0, j + off))

    return _call(body, name="ffn_fwd", out_shape=_sds((N, DFF), BF16), grid=(nb,),
                 in_specs=[col(N, 0), col(N, nb), col(3, 0), col(3, nb), col(1, 0), col(1, nb)],
                 out_specs=col(N, 0), sem=("parallel",), vmem=VMEM_BIG)(up, up, cw, cw, cb, cb)


def _ffn_bwd(up, cw, cb, da, *, bw=256):
    N = up.shape[0]
    shifts = _make_shift(((0, N),))
    nb = DFF // bw

    def body(ug, uv, wg, wv, bg, bv, da_ref, dug, duv, dwg, dwv, dbg, dbv):
        _, vjp = jax.vjp(functools.partial(_ffn_fn, shifts), ug[...], uv[...], wg[...], wv[...], bg[...], bv[...])
        g = vjp(da_ref[...])
        dug[...] = g[0].astype(BF16)
        duv[...] = g[1].astype(BF16)
        dwg[...], dwv[...], dbg[...], dbv[...] = g[2], g[3], g[4], g[5]

    def col(rows, off):
        return pl.BlockSpec((rows, bw), lambda j: (0, j + off))

    half = (_sds((N, DFF), BF16), _sds((N, DFF), BF16), _sds((3, DFF)), _sds((3, DFF)), _sds((1, DFF)), _sds((1, DFF)))
    dug, duv, dwg, dwv, dbg, dbv = _call(
        body, name="ffn_bwd", out_shape=half, grid=(nb,),
        in_specs=[col(N, 0), col(N, nb), col(3, 0), col(3, nb), col(1, 0), col(1, nb), col(N, 0)],
        out_specs=(col(N, 0), col(N, 0), col(3, 0), col(3, 0), col(1, 0), col(1, 0)),
        sem=("parallel",), vmem=VMEM_BIG)(up, up, cw, cw, cb, cb, da)
    return (jnp.concatenate([dug, duv], axis=1), jnp.concatenate([dwg, dwv], axis=1),
            jnp.concatenate([dbg, dbv], axis=1))


def _head_fn(x1, dn, g2, fw, tgt):
    y = _rms(x1 + g2 * dn) * fw
    err = y - tgt
    return 0.5 * jnp.sum(jnp.mean(err * err, axis=-1))


def _head(x1, dn, mod, fw, tgt, *, br=256):
    N = x1.shape[0]
    row = pl.BlockSpec((br, D), lambda i: (i, 0))
    vec = pl.BlockSpec((1, D), lambda i: (0, 0))
    one = pl.BlockSpec((1, HD), lambda i: (0, 0))

    def body(x1_ref, dn_ref, mod_ref, fw_ref, tgt_ref, loss_ref, dx_ref, ddn_ref, dg_ref, dfw_ref):
        loss, (gx, gdn, gg, gfw) = jax.value_and_grad(_head_fn, argnums=(0, 1, 2, 3))(
            x1_ref[...], dn_ref[...], mod_ref[5:6, :], fw_ref[...], tgt_ref[...])
        dx_ref[...] = gx
        ddn_ref[...] = gdn.astype(BF16)

        @pl.when(pl.program_id(0) == 0)
        def _():
            loss_ref[...] = jnp.zeros_like(loss_ref)
            dg_ref[...] = jnp.zeros_like(dg_ref)
            dfw_ref[...] = jnp.zeros_like(dfw_ref)

        loss_ref[...] += jnp.broadcast_to(loss, (1, HD))
        dg_ref[...] += gg
        dfw_ref[...] += gfw

    return _call(body, name="head", out_shape=(_sds((1, HD)), _sds((N, D)), _sds((N, D), BF16), _sds((1, D)), _sds((1, D))),
                 grid=(N // br,), in_specs=[row, row, pl.BlockSpec((6, D), lambda i: (0, 0)), vec, row],
                 out_specs=(one, row, row, vec, vec), sem=("arbitrary",))(x1, dn, mod, fw, tgt)


def _adamw(w, g, m, v, *, name):
    shape = w.shape
    cols = shape[-1]
    rows = max(1, math.prod(shape[:-1]))
    w2, g2, m2, v2 = (t.reshape(rows, cols) for t in (w, g, m, v))
    br = 256 if rows % 256 == 0 else (128 if rows % 128 == 0 else (8 if rows % 8 == 0 and rows > 64 else rows))
    if rows % 352 == 0:
        br = 352
    c1 = 1.0 - B1 ** STEP
    c2 = 1.0 - B2 ** STEP

    def body(w_ref, g_ref, m_ref, v_ref, d_ref, nm_ref, nv_ref):
        gv = g_ref[...]
        nm = B1 * m_ref[...] + (1.0 - B1) * gv
        nv = B2 * v_ref[...] + (1.0 - B2) * (gv * gv)
        d_ref[...] = -LR * ((nm / c1) / (jnp.sqrt(nv / c2) + AEPS) + WD * w_ref[...])
        nm_ref[...] = nm
        nv_ref[...] = nv

    blk = pl.BlockSpec((br, cols), lambda i: (i, 0))
    outs = _call(body, name=name, out_shape=(_sds((rows, cols)),) * 3, grid=(rows // br,),
                 in_specs=[blk] * 4, out_specs=(blk,) * 3, sem=("parallel",))(w2, g2, m2, v2)
    return tuple(t.reshape(shape) for t in outs)


def _rope_tables(N, L):
    t = jnp.arange(N)
    pos = jnp.stack([(t // GRID_W).astype(F32), (t % GRID_W).astype(F32)], axis=1)
    inv = ROPE_THETA ** (-jnp.arange(0, HD // 2, 2, dtype=F32) / (HD // 2))
    ang = pos[:, :, None] * inv[None, None, :]
    cos = jnp.broadcast_to(jnp.cos(ang)[:, :, None, :], (N, 2, 2, HD // 4)).reshape(N, HD)
    sin = jnp.broadcast_to(jnp.sin(ang)[:, :, None, :], (N, 2, 2, HD // 4))
    sin = (sin * jnp.array([-1.0, 1.0], F32)[None, None, :, None]).reshape(N, HD)
    cos = jnp.concatenate([jnp.ones((L, HD), F32), cos], axis=0)
    sin = jnp.concatenate([jnp.zeros((L, HD), F32), sin], axis=0)
    return cos, sin


def _pad_lanes(v, off=0):
    return jnp.zeros((1, HD), F32).at[0, off:off + v.shape[0]].set(v)


def _local_step(x, ctx, tgt, mod_lat, mod_ctx, w_in, shards, small):
    N, L = x.shape[0], ctx.shape[0]
    T = N + L
    bounds = ((0, L), (L, T))
    qw, kw, gw = small["q_norm_w"], small["k_norm_w"], small["gdn_norm_w"]
    conv_w, ffn_w, ffn_b, fnw = small["conv_qkv_w"], small["ffn_conv_w"], small["ffn_conv_b"], small["final_norm_w"]
    alog = _pad_lanes(small["a_log"].reshape(-1), 2 * GH)
    dtb = _pad_lanes(small["dt_bias"].reshape(-1), 2 * GH)
    cos, sin = _rope_tables(N, L)
    bt = 256 if T % 768 else 768
    bnl = 256 if N % 1024 else 1024

    hc = _normmod_fwd(ctx, mod_ctx, 0, 1, name="normmod_ctx")
    hx = _normmod_fwd(x, mod_lat, 0, 1, name="normmod_x")
    h1 = jnp.concatenate([hc, hx], axis=0)
    proj = _mm(h1, w_in, name="mm_in", M=T, N=C_END, K=D, tb=True, bm=bt, bn=1024)
    aq, ak, av = _aprep_fwd(proj, cos, sin, qw, kw)
    attn, (up_g,) = _attn_fwd(aq, ak, av, L, _Exchange([shards["w_up"]], False))
    gq = _gprep_fwd(proj, conv_w, 0, bounds)
    gk = _gprep_fwd(proj, conv_w, 1, bounds)
    gv = _gprep_fwd(proj, conv_w, 2, bounds)
    bl = _bl_fwd(proj, alog, dtb)
    intra, (down_g, pa_g, pd_g, out_g) = _intra_fwd(
        gq, gk, gv, bl, _Exchange([shards[n] for n in ("w_down", "w_pa", "w_pd", "w_out")], False))
    w_up, w_down = up_g.reshape(2 * DFF, D), down_g.reshape(DFF, D)
    w_pa, w_pd, w_out = pa_g.reshape(D, D), pd_g.reshape(D, D), out_g.reshape(D, D)
    xinv, intra = intra[6], intra[:6]
    o, states = _scan_fwd(*intra, L)
    gdn = _gout_fwd(o, proj, gw, L)
    pa = _mm(attn, w_pa, name="mm_pa", M=N, N=D, K=D, bm=bnl)
    pd = _mm(gdn, w_pd, name="mm_pd", M=N, N=D, K=D, bm=bnl)
    y = _merge_fwd(pa, pd, proj, L)
    m = _mm(y, w_out, name="mm_out", M=N, N=D, K=D, bm=bnl)
    x1 = _resid_fwd(x, m, mod_lat, 2, name="resid1")
    h2 = _normmod_fwd(x1, mod_lat, 3, 4, name="normmod_x1")
    up = _mm(h2, w_up, name="mm_up", M=N, N=2 * DFF, K=D, tb=True, bm=bnl, bn=2 * DFF // 4)
    a = _ffn_fwd(up, ffn_w, ffn_b)
    dn = _mm(a, w_down, name="mm_down", M=N, N=D, K=DFF, bm=bnl)
    loss, dx2, ddn, dg2, dfnw = _head(x1, dn, mod_lat, fnw, tgt)

    da = _mm(ddn, w_down, name="mm_down_dx", M=N, N=DFF, K=D, tb=True, bm=bnl, bn=DFF // 2)
    g_down = _mm(a, ddn, name="mm_down_dw", M=DFF, N=D, K=N, ta=True, bm=DFF // 2, out_dtype=BF16)
    dup, d_ffn_w, d_ffn_b = _ffn_bwd(up, ffn_w, ffn_b, da)
    dh2 = _mm(dup, w_up, name="mm_up_dx", M=N, N=D, K=2 * DFF, bm=bnl, bk=2 * DFF // 4)
    g_up = _mm(dup, h2, name="mm_up_dw", M=2 * DFF, N=D, K=N, ta=True, bm=2 * DFF // 4, out_dtype=BF16)
    dx1, dsh2, dsc2 = _normmod_bwd(x1, mod_lat, 3, 4, dh2, 0, dx2, name="normmod_x1_bwd")
    dm, dg1 = _resid_bwd(dx1, m, mod_lat, 2, name="resid1_bwd")
    dy = _mm(dm, w_out, name="mm_out_dx", M=N, N=D, K=D, tb=True, bm=bnl)
    g_out = _mm(y, dm, name="mm_out_dw", M=D, N=D, K=N, ta=True, out_dtype=BF16)
    dpa, dpd, dgate = _merge_bwd(pa, pd, proj, dy, L)
    dattn = _mm(dpa, w_pa, name="mm_pa_dx", M=N, N=D, K=D, tb=True, bm=bnl)
    g_pa = _mm(attn, dpa, name="mm_pa_dw", M=D, N=D, K=N, ta=True, out_dtype=BF16)
    dgdn = _mm(dpd, w_pd, name="mm_pd_dx", M=N, N=D, K=D, tb=True, bm=bnl)
    g_pd = _mm(gdn, dpd, name="mm_pd_dw", M=D, N=D, K=N, ta=True, out_dtype=BF16)
    do, dz, dgw = _gout_bwd(o, proj, gw, dgdn, L)
    cts = _scan_bwd(*intra, states, do, L)
    parts = [g_pa.reshape(NDEV, D // NDEV, D), g_pd.reshape(NDEV, D // NDEV, D), g_out.reshape(NDEV, D // NDEV, D),
             g_up.reshape(NDEV, 2 * DFF // NDEV, D), g_down.reshape(NDEV, DFF // NDEV, D)]
    (dgq, dgk, dgv, dbl), recv = _intra_bwd(gq, gk, gv, bl, xinv, cts, _Exchange(parts, True))
    dxq, dwq = _gprep_bwd(proj, conv_w, 0, bounds, dgq)
    dxk, dwk = _gprep_bwd(proj, conv_w, 1, bounds, dgk)
    dxv, dwv = _gprep_bwd(proj, conv_w, 2, bounds, dgv)
    dxbl, dalog, ddtb = _bl_bwd(proj, alog, dtb, dbl)
    daq_h, dak_h, dav_h = _attn_bwd(aq, ak, av, dattn, L)
    daq, dkv, dqw, dkw = _aprep_bwd(proj, cos, sin, qw, kw, daq_h, dak_h, dav_h, L)
    dproj = jnp.concatenate([dkv, dxq, dxk, dxv, dxbl, jnp.zeros((T, C_AQ - C_BL - HD), BF16), daq, dz, dgate], axis=1)
    g_in = _mm(dproj, h1, name="mm_in_dw", M=C_END, N=D, K=T, ta=True, bm=1024, out_dtype=BF16)
    g_in = jnp.concatenate([g_in[:W_AQ], g_in[C_AQ:]], axis=0).reshape(NDEV, W_END // NDEV, D)
    own_in = lax.dynamic_index_in_dim(g_in, _position()[3], axis=0, keepdims=False)
    *pending, token = _scatter_start(g_in, name="scatter_g_in_start")
    dh1 = _mm(dproj, w_in, name="mm_in_dx", M=T, N=D, K=C_END, bm=bt, bk=1024, after=(token,))
    grad_x, dsh1, dsc1 = _normmod_bwd(x, mod_lat, 0, 1, dh1, L, dx1, name="normmod_x_bwd")
    _, dcsh1, dcsc1 = _normmod_bwd(ctx, mod_ctx, 0, 1, dh1, 0, None, name="normmod_ctx_bwd")

    z1 = jnp.zeros((1, D), F32)
    dmod_lat = jnp.concatenate([dsh1, dsc1, dg1, dsh2, dsc2, dg2], axis=0)
    dmod_ctx = jnp.concatenate([dcsh1, dcsc1, z1, z1, z1, z1], axis=0)
    gsmall = {
        "q_norm_w": dqw, "k_norm_w": dkw, "gdn_norm_w": dgw,
        "conv_qkv_w": jnp.concatenate([dwq, dwk, dwv], axis=1),
        "a_log": dalog[0, 2 * GH:4 * GH], "dt_bias": ddtb[0, 2 * GH:4 * GH],
        "ffn_conv_w": d_ffn_w, "ffn_conv_b": d_ffn_b, "final_norm_w": dfnw,
    }
    return (loss[0, 0], grad_x, (pending, own_in), dict(zip(("w_pa", "w_pd", "w_out", "w_up", "w_down"), recv)),
            dmod_lat, dmod_ctx, gsmall)


HBM = pl.BlockSpec(memory_space=pltpu.HBM)


def _position():
    x, y, c = lax.axis_index("x"), lax.axis_index("y"), lax.axis_index("c")
    return x, y, c, 4 * x + 2 * y + c


def _peer(x, y, c, k):
    px = 1 - x if k & 4 else x
    py = 1 - y if k & 2 else y
    pc = 1 - c if k & 1 else c
    return (px, py, pc), 4 * px + 2 * py + pc


def _exchange(arrs, *, name, scatter):
    exch = _Exchange(arrs, scatter)
    n = exch.n

    def body(*refs):
        ins, outs, sems = refs[:n], refs[n:2 * n], refs[2 * n:]
        exch.start(ins, outs, sems)
        exch.finish(ins, outs, sems)

    outs = pl.pallas_call(body, name=name, out_shape=exch.out_shape, in_specs=[HBM] * n, out_specs=(HBM,) * n,
                          scratch_shapes=exch.scratch,
                          compiler_params=pltpu.CompilerParams(has_side_effects=True))(*arrs)
    return list(outs)


class _Exchange:
    def __init__(self, arrs, scatter):
        self.arrs, self.scatter, self.n = list(arrs), scatter, len(arrs)
        self.out_shape = tuple(_sds(a.shape if scatter else (NDEV,) + a.shape, a.dtype) for a in arrs)
        self.scratch = [pltpu.SemaphoreType.DMA((self.n, NDEV - 1)), pltpu.SemaphoreType.DMA((self.n, NDEV - 1)),
                        pltpu.SemaphoreType.DMA((self.n,))]

    def _copies(self, ins, outs, sems):
        send, recv, loc = sems
        x, y, c, me = _position()
        local = [pltpu.make_async_copy(ins[a].at[me] if self.scatter else ins[a], outs[a].at[me], loc.at[a])
                 for a in range(self.n)]
        remote = []
        for k in range(1, NDEV):
            peer, pid = _peer(x, y, c, k)
            for a in range(self.n):
                src = ins[a].at[pid] if self.scatter else ins[a]
                remote.append(pltpu.make_async_remote_copy(
                    src_ref=src, dst_ref=outs[a].at[me], send_sem=send.at[a, k - 1], recv_sem=recv.at[a, k - 1],
                    device_id=peer, device_id_type=MESH))
        return local, remote

    def start(self, ins, outs, sems):
        local, remote = self._copies(ins, outs, sems)
        for cp in local + remote:
            cp.start()

    def finish(self, ins, outs, sems):
        local, remote = self._copies(ins, outs, sems)
        for cp in remote:
            cp.wait()
        for cp in local:
            cp.wait()


def _gather_two_level(block, *, name):
    def body(x_ref, out_ref, send_sems, recv_sems, local_sem):
        x, y, c, _ = _position()
        me, sibling = (x, y, c), (x, y, 1 - c)
        chips = [(1 - x, y), (x, 1 - y), (1 - x, 1 - y)]

        def slot(px, py, pc):
            return out_ref.at[4 * px + 2 * py + pc]

        def copy(k, owner, to, src=None):
            return pltpu.make_async_remote_copy(
                src_ref=slot(*owner) if src is None else src, dst_ref=slot(*owner), send_sem=send_sems.at[k],
                recv_sem=recv_sems.at[k], device_id=to, device_id_type=MESH)

        mine = pltpu.make_async_copy(x_ref, slot(*me), local_sem)
        mine.start()
        first = [copy(0, me, sibling, src=x_ref)]
        first += [copy(1 + j, me, (*chip, c), src=x_ref) for j, chip in enumerate(chips)]
        for cp in first:
            cp.start()
        passed = [copy(4 + j, (*chip, c), sibling) for j, chip in enumerate(chips)]
        for j, chip in enumerate(chips):
            copy(1 + j, (*chip, c), me).wait_recv()
            passed[j].start()
        copy(0, sibling, me).wait_recv()
        for j, chip in enumerate(chips):
            copy(4 + j, (*chip, 1 - c), me).wait_recv()
        for cp in first + passed:
            cp.wait_send()
        mine.wait()

    return pl.pallas_call(
        body, name=name, out_shape=_sds((NDEV,) + block.shape, block.dtype), in_specs=[HBM], out_specs=HBM,
        scratch_shapes=[pltpu.SemaphoreType.DMA((NDEV - 1,)), pltpu.SemaphoreType.DMA((NDEV - 1,)),
                        pltpu.SemaphoreType.DMA],
        compiler_params=pltpu.CompilerParams(has_side_effects=True))(block)


SEM = pl.BlockSpec(memory_space=pltpu.SEMAPHORE)


def _scatter_copies(src_ref, land_ref, send_sems, recv_sems):
    x, y, c, me = _position()
    copies = []
    for k in range(1, NDEV):
        peer, pid = _peer(x, y, c, k)
        copies.append(pltpu.make_async_remote_copy(
            src_ref=src_ref.at[pid], dst_ref=land_ref.at[me], send_sem=send_sems.at[k - 1],
            recv_sem=recv_sems.at[k - 1], device_id=peer, device_id_type=MESH))
    return copies


def _scatter_start(parts, *, name):
    def body(src_ref, land_ref, send_sems, recv_sems, src_thru, land_thru, token):
        for cp in _scatter_copies(src_ref, land_ref, send_sems, recv_sems):
            cp.start()
        token[...] = jnp.zeros_like(token)

    return pl.pallas_call(
        body, name=name,
        out_shape=(pltpu.SemaphoreType.DMA((NDEV - 1,)), pltpu.SemaphoreType.DMA((NDEV - 1,)),
                   pltpu.HBM(parts.shape, parts.dtype), pltpu.HBM(parts.shape, parts.dtype), _sds((8, HD))),
        in_specs=(HBM, HBM), out_specs=(SEM, SEM, HBM, HBM, pl.BlockSpec(memory_space=pltpu.VMEM)),
        input_output_aliases={0: 2, 1: 3},
        compiler_params=pltpu.CompilerParams(has_side_effects=pltpu.SideEffectType.DATAFLOW_SIDE_EFFECTING),
    )(pltpu.with_memory_space_constraint(parts, pltpu.HBM),
      pltpu.with_memory_space_constraint(lax.empty(parts.shape, parts.dtype), pltpu.HBM))


def _scatter_wait(send_sems, recv_sems, src_thru, land_thru, after, *, name):
    na = len(after)

    def body(src_ref, land_ref, send_sems, recv_sems, *rest):
        for cp in _scatter_copies(src_ref, land_ref, send_sems, recv_sems):
            cp.wait_send()
            cp.wait_recv()

    return pl.pallas_call(
        body, name=name,
        out_shape=(pltpu.HBM(src_thru.shape, src_thru.dtype), pltpu.HBM(land_thru.shape, land_thru.dtype)),
        in_specs=(HBM, HBM, SEM, SEM) + (pl.BlockSpec(memory_space=pl.ANY),) * na, out_specs=(HBM, HBM),
        input_output_aliases={0: 0, 1: 1},
        compiler_params=pltpu.CompilerParams(has_side_effects=pltpu.SideEffectType.DATAFLOW_SIDE_EFFECTING),
    )(src_thru, land_thru, send_sems, recv_sems, *after)[1]


def _cast_bf16(w, *, name):
    rows, cols = w.shape
    br = 128 if rows % 128 == 0 else rows

    def body(w_ref, o_ref):
        o_ref[...] = w_ref[...].astype(BF16)

    blk = pl.BlockSpec((br, cols), lambda i: (i, 0))
    return _call(body, name=name, out_shape=_sds((rows, cols), BF16), grid=(rows // br,), in_specs=[blk],
                 out_specs=blk, sem=("parallel",))(w)


def _sum_slots(a, *, name):
    _, R, C = a.shape

    def body(a_ref, o_ref):
        s = a_ref[0]
        for d in range(1, NDEV):
            s = s + a_ref[d]
        o_ref[...] = s

    return _call(body, name=name, out_shape=_sds((R, C)))(a)


MODROWS = 16


def _mod_fwd(c9, w, b):
    cols = w.shape[1]

    def body(c_ref, w_ref, b_ref, o_ref):
        o_ref[...] = _nn(_silu(c_ref[...]), w_ref[...]) + b_ref[...]

    return _call(body, name="mod_fwd", out_shape=_sds((MODROWS, cols)))(c9, w, b)


def _mod_bwd(c9, dmy, dall, w):
    cols = w.shape[1]

    def body(c_ref, dmy_ref, dall_ref, w_ref, gw_ref, gb_ref, cp_ref):
        sc = _silu(c_ref[...])
        rows = lax.broadcasted_iota(jnp.int32, (MODROWS, 1), 0)
        d = dmy_ref[...]
        d_ctx = jnp.where(rows == NDEV, d, 0.0)
        sc_ctx = jnp.where(rows == NDEV, sc, 0.0)
        outer = lax.dot_general(sc_ctx, d_ctx, (((0,), (0,)), ((), ())), precision=HI, preferred_element_type=F32)
        gw_ref[...] = _tn(jnp.where(rows < NDEV, sc, 0.0), jnp.where(rows < NDEV, d, 0.0)) + outer
        gb_ref[...] = jnp.sum(dall_ref[...], axis=0, keepdims=True)
        cp_ref[...] = jnp.sum(_nt(d_ctx, w_ref[...]), axis=0, keepdims=True)

    return _call(body, name="mod_bwd", out_shape=(_sds((D, cols)), _sds((1, 6 * D)), _sds((1, D))),
                 vmem=VMEM_BIG)(c9, dmy, dall, w)


def _cctx_finish(parts, c_ctx):
    def body(p_ref, c_ref, o_ref):
        s = p_ref[0]
        for d in range(1, NDEV):
            s = s + p_ref[d]
        _, vjp = jax.vjp(_silu, c_ref[...])
        o_ref[...] = vjp(s)[0]

    return _call(body, name="cctx_finish", out_shape=_sds((1, D)))(parts, c_ctx)


def _adamw_recv(w, recv, m, v, *, name, own=None):
    rows, cols = w.shape
    bc = 256
    c1 = 1.0 - B1 ** STEP
    c2 = 1.0 - B2 ** STEP
    has_own = own is not None

    def body(w_ref, r_ref, m_ref, v_ref, *rest):
        g_ref, d_ref, nm_ref, nv_ref = rest[-4:]
        me = _position()[3]

        def slot(d):
            return jnp.where(me == d, rest[0][...], r_ref[d]) if has_own else r_ref[d]

        gv = slot(0).astype(F32)
        for d in range(1, NDEV):
            gv = gv + slot(d).astype(F32)
        nm = B1 * m_ref[...] + (1.0 - B1) * gv
        nv = B2 * v_ref[...] + (1.0 - B2) * (gv * gv)
        g_ref[...] = gv
        d_ref[...] = -LR * ((nm / c1) / (jnp.sqrt(nv / c2) + AEPS) + WD * w_ref[...])
        nm_ref[...] = nm
        nv_ref[...] = nv

    blk = pl.BlockSpec((rows, bc), lambda j: (0, j))
    return _call(body, name=name, out_shape=(_sds((rows, cols)),) * 4, grid=(cols // bc,),
                 in_specs=[blk, pl.BlockSpec((NDEV, rows, bc), lambda j: (0, 0, j)), blk, blk] + [blk] * has_own,
                 out_specs=(blk,) * 4, sem=("parallel",), vmem=VMEM_BIG)(w, recv, m, v, *([own] if has_own else []))


P_LAT, P_CTX, P_FNW, P_FFNB, P_CONV, P_FFNW, P_MISC, P_ROWS = 0, 8, 16, 24, 32, 48, 72, 80


def _rows_of(v, nrows):
    flat = v.reshape(-1)
    return jnp.pad(flat, (0, nrows * D - flat.shape[0])).reshape(nrows, D)


def _by_columns(g):
    n, r, c = g.shape
    return jnp.transpose(g, (1, 0, 2)).reshape(r, n * c)


def kernel(x, c, ctx, c_ctx, w_mod, b_mod, w_in, q_norm_w, k_norm_w, conv_qkv_w, a_log, dt_bias, gdn_norm_w, w_pa, w_pd, w_out, w_up, ffn_conv_w, ffn_conv_b, w_down, final_norm_w, loss_target, m_c_ctx, m_w_mod, m_b_mod, m_w_in, m_q_norm_w, m_k_norm_w, m_conv_qkv_w, m_a_log, m_dt_bias, m_gdn_norm_w, m_w_pa, m_w_pd, m_w_out, m_w_up, m_ffn_conv_w, m_ffn_conv_b, m_w_down, m_final_norm_w, v_c_ctx, v_w_mod, v_b_mod, v_w_in, v_q_norm_w, v_k_norm_w, v_conv_qkv_w, v_a_log, v_dt_bias, v_gdn_norm_w, v_w_pa, v_w_pd, v_w_out, v_w_up, v_ffn_conv_w, v_ffn_conv_b, v_w_down, v_final_norm_w):
    _, _, _, me = _position()
    mcols = w_mod.shape[2]

    transposed = ("w_in", "w_up")
    big = {"w_in": w_in[0].T, "w_pa": w_pa[0], "w_pd": w_pd[0], "w_out": w_out[0], "w_up": w_up[0].T, "w_down": w_down[0]}
    names = list(big)
    shards = {n: _cast_bf16(big[n], name="cast_" + n) for n in names}
    w_in_g = _gather_two_level(shards["w_in"], name="gather_w_in")
    c_all, conv_g, ffnw_g = _exchange([c, conv_qkv_w[0], ffn_conv_w[0]], name="gather_small", scatter=False)
    w_in_full = w_in_g.reshape(W_END, D)
    w_in_pad = jnp.concatenate([w_in_full[:W_AQ], jnp.zeros((C_AQ - W_AQ, D), BF16), w_in_full[W_AQ:]], axis=0)

    c9 = jnp.concatenate([c_all.reshape(NDEV, D), jnp.pad(c_ctx[None], ((0, MODROWS - NDEV - 1), (0, 0)))], axis=0)
    b_loc = lax.dynamic_slice(b_mod, (0, me * mcols), (1, mcols))
    mod_all, = _exchange([_mod_fwd(c9, w_mod[0], b_loc)], name="gather_mod", scatter=False)
    mod_lat = lax.dynamic_index_in_dim(mod_all, me, axis=1, keepdims=False).reshape(6, D)
    mod_ctx = mod_all[:, NDEV, :].reshape(6, D)

    small = {"q_norm_w": q_norm_w, "k_norm_w": k_norm_w, "gdn_norm_w": gdn_norm_w, "a_log": a_log, "dt_bias": dt_bias,
             "conv_qkv_w": _by_columns(conv_g), "ffn_conv_w": _by_columns(ffnw_g), "ffn_conv_b": ffn_conv_b,
             "final_norm_w": final_norm_w[None]}
    loss_me, grad_x, (pending_in, own_in), recv, dmod_lat, dmod_ctx, gs = _local_step(
        x[0], ctx[0], loss_target[0], mod_lat, mod_ctx, w_in_pad, shards, small)

    moments = {"w_in": (m_w_in, v_w_in), "w_pa": (m_w_pa, v_w_pa), "w_pd": (m_w_pd, v_w_pd),
               "w_out": (m_w_out, v_w_out), "w_up": (m_w_up, v_w_up), "w_down": (m_w_down, v_w_down)}
    res = {}
    def finish(n, outs):
        return tuple((t.T if n in transposed else t)[None] for t in outs)

    def moment(t, n):
        return t[0].T if n in transposed else t[0]

    for n in recv:
        res[n] = finish(n, _adamw_recv(big[n], recv[n], moment(moments[n][0], n), moment(moments[n][1], n),
                                       name="adamw_" + n))

    misc = jnp.concatenate([gs["q_norm_w"][0], gs["k_norm_w"][0], gs["gdn_norm_w"][0], gs["a_log"], gs["dt_bias"],
                            loss_me[None]])
    pack = jnp.concatenate([_rows_of(dmod_lat, P_CTX - P_LAT), _rows_of(dmod_ctx, P_FNW - P_CTX),
                            _rows_of(gs["final_norm_w"], P_FFNB - P_FNW), _rows_of(gs["ffn_conv_b"], P_CONV - P_FFNB),
                            _rows_of(gs["conv_qkv_w"], P_FFNW - P_CONV), _rows_of(gs["ffn_conv_w"], P_MISC - P_FFNW),
                            _rows_of(misc, P_ROWS - P_MISC)], axis=0)
    pack_all, = _exchange([pack], name="gather_pack", scatter=False)
    tot = _sum_slots(pack_all, name="sum_pack")
    dall = jnp.concatenate([pack_all[:, P_LAT:P_LAT + 6, :].reshape(NDEV, 6 * D),
                            jnp.pad(tot[P_CTX:P_CTX + 6].reshape(1, 6 * D), ((0, MODROWS - NDEV - 1), (0, 0)))], axis=0)
    dmy = lax.dynamic_slice(dall, (0, me * mcols), (MODROWS, mcols))
    g_w_mod, g_b_mod, cpart = _mod_bwd(c9, dmy, dall, w_mod[0])
    cparts, = _exchange([cpart], name="gather_cctx", scatter=False)
    g_c_ctx = _cctx_finish(cparts, c_ctx[None])[0]

    nconv, nffn = 3 * GH * HD, 2 * DFF
    conv_tot = tot[P_CONV:P_FFNW].reshape(-1)[:3 * nconv].reshape(3, nconv)
    ffnw_tot = tot[P_FFNW:P_MISC].reshape(-1)[:3 * nffn].reshape(3, nffn)
    mrow = tot[P_MISC]
    grads = {
        "c_ctx": g_c_ctx, "w_mod": g_w_mod[None], "b_mod": g_b_mod,
        "q_norm_w": mrow[None, 0:HD], "k_norm_w": mrow[None, HD:2 * HD], "gdn_norm_w": mrow[None, 2 * HD:3 * HD],
        "conv_qkv_w": lax.dynamic_slice(conv_tot, (0, me * (nconv // NDEV)), (3, nconv // NDEV))[None],
        "a_log": mrow[3 * HD:3 * HD + 2 * GH].reshape(1, 2, GH),
        "dt_bias": mrow[3 * HD + 2 * GH:3 * HD + 4 * GH].reshape(1, 2, GH),
        "ffn_conv_w": lax.dynamic_slice(ffnw_tot, (0, me * (nffn // NDEV)), (3, nffn // NDEV))[None],
        "ffn_conv_b": tot[P_FFNB:P_CONV].reshape(-1)[:nffn][None],
        "final_norm_w": tot[P_FNW],
    }
    loss = mrow[3 * HD + 4 * GH]
    given = {"c_ctx": (c_ctx, m_c_ctx, v_c_ctx), "w_mod": (w_mod, m_w_mod, v_w_mod), "b_mod": (b_mod, m_b_mod, v_b_mod),
             "q_norm_w": (q_norm_w, m_q_norm_w, v_q_norm_w), "k_norm_w": (k_norm_w, m_k_norm_w, v_k_norm_w),
             "conv_qkv_w": (conv_qkv_w, m_conv_qkv_w, v_conv_qkv_w), "a_log": (a_log, m_a_log, v_a_log),
             "dt_bias": (dt_bias, m_dt_bias, v_dt_bias), "gdn_norm_w": (gdn_norm_w, m_gdn_norm_w, v_gdn_norm_w),
             "ffn_conv_w": (ffn_conv_w, m_ffn_conv_w, v_ffn_conv_w), "ffn_conv_b": (ffn_conv_b, m_ffn_conv_b, v_ffn_conv_b),
             "final_norm_w": (final_norm_w, m_final_norm_w, v_final_norm_w)}
    for n, (w, m, v) in given.items():
        res[n] = (grads[n],) + _adamw(w, grads[n], m, v, name="adamw_" + n)

    land = _scatter_wait(*pending_in, [res[n][1] for n in res], name="scatter_g_in_wait")
    res["w_in"] = finish("w_in", _adamw_recv(big["w_in"], land, moment(m_w_in, "w_in"), moment(v_w_in, "w_in"),
                                             name="adamw_w_in", own=own_in))

    order = ["c_ctx", "w_mod", "b_mod", "w_in", "q_norm_w", "k_norm_w", "conv_qkv_w", "a_log", "dt_bias", "gdn_norm_w",
             "w_pa", "w_pd", "w_out", "w_up", "ffn_conv_w", "ffn_conv_b", "w_down", "final_norm_w"]
    return (loss, grad_x[None], *[res[n][0] for n in order], *[res[n][1] for n in order],
            *[res[n][2] for n in order], *[res[n][3] for n in order])
```

```python
import functools
import math

import jax
import jax.numpy as jnp
from jax import lax
from jax.experimental import pallas as pl
from jax.experimental.pallas import tpu as pltpu

F32 = jnp.float32
BF16 = jnp.bfloat16
HI = lax.Precision.HIGHEST
MESH = pl.DeviceIdType.MESH

NDEV = 8
D = 1024
HD = 128
AH, AKV, GRP = 8, 2, 4
GH = 8
CH = 64
DFF = 2816
GRID_W = 64
EPS = 1e-6
ROPE_THETA = 10000.0
C_KV, C_QKV, C_BL, C_AQ, C_Z, C_GATE, C_END = 0, 512, 3584, 4096, 5120, 6144, 8192
W_BL, W_AQ, W_END = 3584, 3616, 7712
LR, B1, B2, AEPS, WD, STEP = 0.001, 0.9, 0.999, 1e-08, 0.01, 10
VMEM_BIG = 56 * 1024 * 1024
INTRA_FWD_CHUNKS = 18
INTRA_BWD_CHUNKS = 12


def _call(body, *, name, out_shape, grid=None, in_specs=None, out_specs=None, scratch=(), sem=None,
          vmem=None, aliases=None):
    params = {}
    if sem is not None:
        params["dimension_semantics"] = sem
    if vmem is not None:
        params["vmem_limit_bytes"] = vmem
    kw = {}
    if grid is not None:
        kw["grid"] = grid
    if in_specs is not None:
        kw["in_specs"] = in_specs
    if out_specs is not None:
        kw["out_specs"] = out_specs
    if aliases:
        kw["input_output_aliases"] = aliases
    return pl.pallas_call(body, name=name, out_shape=out_shape, scratch_shapes=list(scratch),
                          compiler_params=pltpu.CompilerParams(**params), **kw)


def _call_carrying(body, exch, *, name, out_shape, grid, in_specs, out_specs, scratch=(), vmem=None):
    n, nin, nout, nscr = exch.n, len(in_specs), len(out_shape), len(scratch)

    def wrapped(*refs):
        ins, cins = refs[:nin], refs[nin:nin + n]
        outs, couts = refs[nin + n:nin + n + nout], refs[nin + n + nout:nin + 2 * n + nout]
        scr, sems = refs[nin + 2 * n + nout:nin + 2 * n + nout + nscr], refs[nin + 2 * n + nout + nscr:]
        ids = [pl.program_id(i) for i in range(len(grid))]
        first = functools.reduce(jnp.logical_and, [i == 0 for i in ids])
        last = functools.reduce(jnp.logical_and, [i == g - 1 for i, g in zip(ids, grid)])

        @pl.when(first)
        def _():
            exch.start(cins, couts, sems)

        body(*ins, *outs, *scr)

        @pl.when(last)
        def _():
            exch.finish(cins, couts, sems)

    params = {"dimension_semantics": ("arbitrary",) * len(grid)}
    if vmem is not None:
        params["vmem_limit_bytes"] = vmem
    fn = pl.pallas_call(wrapped, name=name, out_shape=tuple(out_shape) + exch.out_shape, grid=grid,
                        in_specs=list(in_specs) + [HBM] * n, out_specs=tuple(out_specs) + (HBM,) * n,
                        scratch_shapes=list(scratch) + exch.scratch, compiler_params=pltpu.CompilerParams(**params))

    def run(*args):
        res = fn(*args, *exch.arrs)
        return res[:nout], list(res[nout:])

    return run


def _sds(shape, dtype=F32):
    return jax.ShapeDtypeStruct(tuple(shape), dtype)


def _dot(a, b, ca, cb):
    return lax.dot_general(a.astype(BF16), b.astype(BF16), (((ca,), (cb,)), ((), ())),
                           preferred_element_type=F32)


@jax.custom_vjp
def _nn(a, b):
    return _dot(a, b, 1, 0)


@jax.custom_vjp
def _nt(a, b):
    return _dot(a, b, 1, 1)


@jax.custom_vjp
def _tn(a, b):
    return _dot(a, b, 0, 0)


_nn.defvjp(lambda a, b: (_nn(a, b), (a, b)), lambda r, g: (_nt(g, r[1]), _tn(r[0], g)))
_nt.defvjp(lambda a, b: (_nt(a, b), (a, b)), lambda r, g: (_nn(g, r[1]), _tn(g, r[0])))
_tn.defvjp(lambda a, b: (_tn(a, b), (a, b)), lambda r, g: (_nt(r[1], g), _nn(r[0], g)))


def _hdot(a, b):
    return jnp.dot(a, b, precision=HI, preferred_element_type=F32)


def _mdot(a, b):
    return jnp.dot(a, b, precision=lax.Precision.HIGH, preferred_element_type=F32)


def _maskdot(mask, a, cm):
    hi = a.astype(BF16)
    r = a - hi.astype(F32)
    mid = r.astype(BF16)
    lo = (r - mid.astype(F32)).astype(BF16)
    mb = mask.astype(BF16)
    dims = (((cm,), (0,)), ((), ()))
    return (lax.dot_general(mb, hi, dims, preferred_element_type=F32)
            + lax.dot_general(mb, mid, dims, preferred_element_type=F32)
            + lax.dot_general(mb, lo, dims, preferred_element_type=F32))


@jax.custom_vjp
def _mask_nn(mask, a):
    return _maskdot(mask, a, 1)


_mask_nn.defvjp(lambda mask, a: (_maskdot(mask, a, 1), mask),
                lambda mask, g: (jnp.zeros_like(mask), _maskdot(mask, g, 0)))


@jax.custom_vjp
def _saved_inverse(lmat, x):
    return x


def _saved_inverse_bwd(x, g):
    t = lax.dot_general(x, g, (((0,), (0,)), ((), ())), precision=lax.Precision.HIGH, preferred_element_type=F32)
    dl = lax.dot_general(t, x, (((1,), (1,)), ((), ())), precision=lax.Precision.HIGH, preferred_element_type=F32)
    return -dl, jnp.zeros_like(x)


_saved_inverse.defvjp(lambda lmat, x: (x, x), _saved_inverse_bwd)


def _row_ids(shape):
    return lax.broadcasted_iota(jnp.int32, shape, 0)


def _shift_rows(x, down, bounds):
    n = x.shape[0]
    rows = _row_ids(x.shape)
    y = pltpu.roll(x, 1 if down else n - 1, 0)
    edge = functools.reduce(jnp.logical_or, [rows == (s if down else e - 1) for s, e in bounds])
    return jnp.where(edge, 0.0, y)


def _make_shift(bounds):
    @jax.custom_vjp
    def down(x):
        return _shift_rows(x, True, bounds)

    @jax.custom_vjp
    def up(x):
        return _shift_rows(x, False, bounds)

    down.defvjp(lambda x: (down(x), None), lambda _, g: (up(g),))
    up.defvjp(lambda x: (up(x), None), lambda _, g: (down(g),))
    return down, up


@jax.custom_vjp
def _swap32(x):
    lane = lax.broadcasted_iota(jnp.int32, x.shape, x.ndim - 1)
    return jnp.where((lane % 64) < 32, pltpu.roll(x, HD - 32, x.ndim - 1), pltpu.roll(x, 32, x.ndim - 1))


_swap32.defvjp(lambda x: (_swap32(x), None), lambda _, g: (_swap32(g),))


def _rms(x):
    return x * lax.rsqrt(jnp.mean(x * x, axis=-1, keepdims=True) + EPS)


def _silu(x):
    return x * jax.nn.sigmoid(x)


def _mm(a, b, *, name, M, N, K, ta=False, tb=False, out_dtype=F32, bm=None, bn=None, bk=None,
        a_off=(0, 0), b_off=(0, 0), after=()):
    bm, bn, bk = bm or M, bn or N, bk or K
    assert M % bm == 0 and N % bn == 0 and K % bk == 0, (name, M, N, K, bm, bn, bk)
    nk = K // bk
    ca, cb = (0 if ta else 1), (1 if tb else 0)
    na = len(after)

    def body(a_ref, b_ref, *rest):
        o_ref, acc = rest[na], rest[na + 1:]
        r = _dot(a_ref[...], b_ref[...], ca, cb)
        if nk == 1:
            o_ref[...] = r.astype(out_dtype)
        else:
            acc_ref, = acc
            k = pl.program_id(2)

            @pl.when(k == 0)
            def _():
                acc_ref[...] = r

            @pl.when(k > 0)
            def _():
                acc_ref[...] += r

            @pl.when(k == nk - 1)
            def _():
                o_ref[...] = acc_ref[...].astype(out_dtype)

    def blk(off, bshape):
        assert off[0] % bshape[0] == 0 and off[1] % bshape[1] == 0, (name, off, bshape)
        return off[0] // bshape[0], off[1] // bshape[1]

    if ta:
        ao = blk(a_off, (bk, bm))
        a_spec = pl.BlockSpec((bk, bm), lambda i, j, k: (k + ao[0], i + ao[1]))
    else:
        ao = blk(a_off, (bm, bk))
        a_spec = pl.BlockSpec((bm, bk), lambda i, j, k: (i + ao[0], k + ao[1]))
    if tb:
        bo = blk(b_off, (bn, bk))
        b_spec = pl.BlockSpec((bn, bk), lambda i, j, k: (j + bo[0], k + bo[1]))
    else:
        bo = blk(b_off, (bk, bn))
        b_spec = pl.BlockSpec((bk, bn), lambda i, j, k: (k + bo[0], j + bo[1]))
    return _call(body, name=name, out_shape=_sds((M, N), out_dtype), grid=(M // bm, N // bn, nk),
                 in_specs=[a_spec, b_spec] + [pl.BlockSpec(memory_space=pl.ANY)] * na,
                 out_specs=pl.BlockSpec((bm, bn), lambda i, j, k: (i, j)),
                 scratch=[pltpu.VMEM((bm, bn), F32)] if nk > 1 else [],
                 sem=("parallel", "parallel", "arbitrary"), vmem=VMEM_BIG)(a, b, *after)


def _normmod_fn(x, sh, sc):
    return _rms(x) * (1.0 + sc) + sh


def _normmod_fwd(x, mod, i_sh, i_sc, *, name, br=256):
    R = x.shape[0]

    def body(x_ref, mod_ref, o_ref):
        o_ref[...] = _normmod_fn(x_ref[...], mod_ref[i_sh:i_sh + 1, :], mod_ref[i_sc:i_sc + 1, :]).astype(BF16)

    return _call(body, name=name, out_shape=_sds((R, D), BF16), grid=(R // br,),
                 in_specs=[pl.BlockSpec((br, D), lambda i: (i, 0)), pl.BlockSpec((6, D), lambda i: (0, 0))],
                 out_specs=pl.BlockSpec((br, D), lambda i: (i, 0)), sem=("parallel",))(x, mod)


def _normmod_bwd(x, mod, i_sh, i_sc, dh, dh_off, res, *, name, br=256):
    R = x.shape[0]
    ob = dh_off // br
    has_res = res is not None

    def body(x_ref, mod_ref, dh_ref, *rest):
        if has_res:
            res_ref, dx_ref, dsh_ref, dsc_ref = rest
        else:
            dx_ref, dsh_ref, dsc_ref = rest
        sh, sc = mod_ref[i_sh:i_sh + 1, :], mod_ref[i_sc:i_sc + 1, :]
        _, vjp = jax.vjp(_normmod_fn, x_ref[...], sh, sc)
        dx, dsh, dsc = vjp(dh_ref[...])
        dx_ref[...] = dx + res_ref[...] if has_res else dx

        @pl.when(pl.program_id(0) == 0)
        def _():
            dsh_ref[...] = jnp.zeros_like(dsh_ref)
            dsc_ref[...] = jnp.zeros_like(dsc_ref)

        dsh_ref[...] += dsh
        dsc_ref[...] += dsc

    row = pl.BlockSpec((br, D), lambda i: (i, 0))
    vec = pl.BlockSpec((1, D), lambda i: (0, 0))
    ins = [row, pl.BlockSpec((6, D), lambda i: (0, 0)), pl.BlockSpec((br, D), lambda i: (i + ob, 0))]
    args = [x, mod, dh]
    if has_res:
        ins.append(row)
        args.append(res)
    return _call(body, name=name, out_shape=(_sds((R, D)), _sds((1, D)), _sds((1, D))), grid=(R // br,),
                 in_specs=ins, out_specs=(row, vec, vec), sem=("arbitrary",))(*args)


def _rope(x, cos, sin):
    return x * cos + _swap32(x) * sin


def _aprep_fn(qs, ks, cos, sin, qw, kw):
    return ([_rope(_rms(q) * qw, cos, sin) for q in qs], [_rope(_rms(k) * kw, cos, sin) for k in ks])


def _aprep_fwd(proj, cos, sin, qw, kw, *, br=256):
    T = proj.shape[0]

    def body(aq_ref, kv_ref, cos_ref, sin_ref, qw_ref, kw_ref, q_ref, k_ref, v_ref):
        qs = [aq_ref[:, h * HD:(h + 1) * HD] for h in range(AH)]
        ks = [kv_ref[:, h * HD:(h + 1) * HD] for h in range(AKV)]
        qo, ko = _aprep_fn(qs, ks, cos_ref[...], sin_ref[...], qw_ref[...], kw_ref[...])
        for h in range(AH):
            q_ref[h] = qo[h].astype(BF16)
        for h in range(AKV):
            k_ref[h] = ko[h].astype(BF16)
            v_ref[h] = kv_ref[:, (AKV + h) * HD:(AKV + h + 1) * HD].astype(BF16)

    tab = pl.BlockSpec((br, HD), lambda i: (i, 0))
    vec = pl.BlockSpec((1, HD), lambda i: (0, 0))
    return _call(body, name="aprep_fwd",
                 out_shape=(_sds((AH, T, HD), BF16), _sds((AKV, T, HD), BF16), _sds((AKV, T, HD), BF16)),
                 grid=(T // br,),
                 in_specs=[pl.BlockSpec((br, AH * HD), lambda i: (i, C_AQ // (AH * HD))),
                           pl.BlockSpec((br, 2 * AKV * HD), lambda i: (i, 0)), tab, tab, vec, vec],
                 out_specs=(pl.BlockSpec((AH, br, HD), lambda i: (0, i, 0)),
                            pl.BlockSpec((AKV, br, HD), lambda i: (0, i, 0)),
                            pl.BlockSpec((AKV, br, HD), lambda i: (0, i, 0))),
                 sem=("parallel",))(proj, proj, cos, sin, qw, kw)


def _aprep_bwd(proj, cos, sin, qw, kw, dq, dk, dv, L, *, br=256):
    T = proj.shape[0]
    lb = L // br

    def body(aq_ref, kv_ref, cos_ref, sin_ref, qw_ref, kw_ref, dq_ref, dk_ref, dv_ref,
             daq_ref, dkv_ref, dqw_ref, dkw_ref):
        i = pl.program_id(0)
        qs = [aq_ref[:, h * HD:(h + 1) * HD] for h in range(AH)]
        ks = [kv_ref[:, h * HD:(h + 1) * HD] for h in range(AKV)]
        _, vjp = jax.vjp(_aprep_fn, qs, ks, cos_ref[...], sin_ref[...], qw_ref[...], kw_ref[...])
        is_lat = i >= lb
        dqs = [jnp.where(is_lat, dq_ref[h], 0.0) for h in range(AH)]
        dks = [dk_ref[h] for h in range(AKV)]
        gq, gk, _, _, gqw, gkw = vjp((dqs, dks))
        for h in range(AH):
            daq_ref[:, h * HD:(h + 1) * HD] = gq[h].astype(BF16)
        for h in range(AKV):
            dkv_ref[:, h * HD:(h + 1) * HD] = gk[h].astype(BF16)
            dkv_ref[:, (AKV + h) * HD:(AKV + h + 1) * HD] = dv_ref[h].astype(BF16)

        @pl.when(i == 0)
        def _():
            dqw_ref[...] = jnp.zeros_like(dqw_ref)
            dkw_ref[...] = jnp.zeros_like(dkw_ref)

        dqw_ref[...] += gqw
        dkw_ref[...] += gkw

    tab = pl.BlockSpec((br, HD), lambda i: (i, 0))
    vec = pl.BlockSpec((1, HD), lambda i: (0, 0))
    kvb = pl.BlockSpec((AKV, br, HD), lambda i: (0, i, 0))
    return _call(body, name="aprep_bwd",
                 out_shape=(_sds((T, AH * HD), BF16), _sds((T, 2 * AKV * HD), BF16), _sds((1, HD)), _sds((1, HD))),
                 grid=(T // br,),
                 in_specs=[pl.BlockSpec((br, AH * HD), lambda i: (i, C_AQ // (AH * HD))),
                           pl.BlockSpec((br, 2 * AKV * HD), lambda i: (i, 0)), tab, tab, vec, vec,
                           pl.BlockSpec((AH, br, HD), lambda i: (0, jnp.maximum(i - lb, 0), 0)), kvb, kvb],
                 out_specs=(pl.BlockSpec((br, AH * HD), lambda i: (i, 0)),
                            pl.BlockSpec((br, 2 * AKV * HD), lambda i: (i, 0)), vec, vec),
                 sem=("arbitrary",))(proj, proj, cos, sin, qw, kw, dq, dk, dv)


def _attn_fn(q, k, v):
    s = _nt(q, k) * (HD ** -0.5)
    m = lax.stop_gradient(jnp.max(s, axis=-1, keepdims=True))
    e = jnp.exp(s - m)
    p = e / jnp.sum(e, axis=-1, keepdims=True)
    return _nn(p, v)


def _attn_fwd(q, k, v, L, exch, *, bq=128):
    T = q.shape[1]
    N = T - L
    lb = L // bq

    def body(q_ref, k_ref, v_ref, o_ref):
        qv = q_ref[...].reshape(GRP * bq, HD).astype(F32)
        o = _attn_fn(qv, k_ref[...].astype(F32), v_ref[...].astype(F32))
        for g in range(GRP):
            o_ref[:, g * HD:(g + 1) * HD] = o[g * bq:(g + 1) * bq].astype(BF16)

    kvb = pl.BlockSpec((None, T, HD), lambda g, i: (g, 0, 0))
    (attn,), moved = _call_carrying(
        body, exch, name="attn_fwd", out_shape=(_sds((N, AH * HD), BF16),), grid=(AKV, N // bq),
        in_specs=[pl.BlockSpec((GRP, bq, HD), lambda g, i: (g, i + lb, 0)), kvb, kvb],
        out_specs=(pl.BlockSpec((bq, GRP * HD), lambda g, i: (i, g)),), vmem=VMEM_BIG)(q, k, v)
    return attn, moved


def _attn_bwd(q, k, v, do, L, *, bq=128):
    T = q.shape[1]
    N = T - L
    lb = L // bq

    def body(q_ref, k_ref, v_ref, do_ref, dq_ref, dk_ref, dv_ref):
        qv = q_ref[...].reshape(GRP * bq, HD).astype(F32)
        _, vjp = jax.vjp(_attn_fn, qv, k_ref[...].astype(F32), v_ref[...].astype(F32))
        dov = jnp.concatenate([do_ref[:, g * HD:(g + 1) * HD] for g in range(GRP)], axis=0)
        dq, dk, dv = vjp(dov)
        dq_ref[...] = dq.reshape(GRP, bq, HD)

        @pl.when(pl.program_id(1) == 0)
        def _():
            dk_ref[...] = jnp.zeros_like(dk_ref)
            dv_ref[...] = jnp.zeros_like(dv_ref)

        dk_ref[...] += dk
        dv_ref[...] += dv

    kvb = pl.BlockSpec((None, T, HD), lambda g, i: (g, 0, 0))
    return _call(body, name="attn_bwd",
                 out_shape=(_sds((AH, N, HD)), _sds((AKV, T, HD)), _sds((AKV, T, HD))), grid=(AKV, N // bq),
                 in_specs=[pl.BlockSpec((GRP, bq, HD), lambda g, i: (g, i + lb, 0)), kvb, kvb,
                           pl.BlockSpec((bq, GRP * HD), lambda g, i: (i, g))],
                 out_specs=(pl.BlockSpec((GRP, bq, HD), lambda g, i: (g, i, 0)), kvb, kvb),
                 sem=("parallel", "arbitrary"), vmem=VMEM_BIG)(q, k, v, do)


def _gprep_fn(kind, shifts, x, w):
    down, up = shifts
    y = down(x) * w[0:1, :] + x * w[1:2, :] + up(x) * w[2:3, :]
    a = _silu(y)
    if kind == 2:
        return a
    a = a * lax.rsqrt(jnp.sum(a * a, axis=-1, keepdims=True) + EPS)
    return a * (HD ** -0.5) if kind == 0 else a


def _gprep_fwd(proj, conv_w, kind, bounds):
    T = proj.shape[0]
    shifts = _make_shift(bounds)
    cb = C_QKV // HD + kind * GH

    def body(x_ref, w_ref, o_ref):
        o_ref[...] = _gprep_fn(kind, shifts, x_ref[...], w_ref[...])

    return _call(body, name=f"gprep_fwd{kind}", out_shape=_sds((GH, T, HD)), grid=(GH,),
                 in_specs=[pl.BlockSpec((T, HD), lambda h: (0, cb + h)),
                           pl.BlockSpec((3, HD), lambda h: (0, kind * GH + h))],
                 out_specs=pl.BlockSpec((None, T, HD), lambda h: (h, 0, 0)), sem=("parallel",))(proj, conv_w)


def _gprep_bwd(proj, conv_w, kind, bounds, dy):
    T = proj.shape[0]
    shifts = _make_shift(bounds)
    cb = C_QKV // HD + kind * GH

    def body(x_ref, w_ref, dy_ref, dx_ref, dw_ref):
        _, vjp = jax.vjp(functools.partial(_gprep_fn, kind, shifts), x_ref[...], w_ref[...])
        dx, dw = vjp(dy_ref[0] + dy_ref[1])
        dx_ref[...] = dx.astype(BF16)
        dw_ref[...] = dw

    return _call(body, name=f"gprep_bwd{kind}", out_shape=(_sds((T, GH * HD), BF16), _sds((3, GH * HD))), grid=(GH,),
                 in_specs=[pl.BlockSpec((T, HD), lambda h: (0, cb + h)),
                           pl.BlockSpec((3, HD), lambda h: (0, kind * GH + h)),
                           pl.BlockSpec((2, None, T, HD), lambda h: (0, h, 0, 0))],
                 out_specs=(pl.BlockSpec((T, HD), lambda h: (0, h)), pl.BlockSpec((3, HD), lambda h: (0, h))),
                 sem=("parallel",))(proj, conv_w, dy)


def _bl_fn(x, alog, dtb):
    lane = lax.broadcasted_iota(jnp.int32, x.shape, 1)
    beta = jax.nn.sigmoid(x)
    z = x + dtb
    sp = jnp.maximum(z, 0.0) + jnp.log1p(jnp.exp(-jnp.abs(z)))
    la = -jnp.exp(alog) * sp
    return jnp.where(lane < 2 * GH, beta, jnp.where(lane < 4 * GH, la, 0.0))


def _bl_fwd(proj, alog, dtb, *, br=256):
    T = proj.shape[0]

    def body(x_ref, a_ref, d_ref, o_ref):
        o_ref[...] = _bl_fn(x_ref[...], a_ref[...], d_ref[...])

    vec = pl.BlockSpec((1, HD), lambda i: (0, 0))
    return _call(body, name="bl_fwd", out_shape=_sds((T, HD)), grid=(T // br,),
                 in_specs=[pl.BlockSpec((br, HD), lambda i: (i, C_BL // HD)), vec, vec],
                 out_specs=pl.BlockSpec((br, HD), lambda i: (i, 0)), sem=("parallel",))(proj, alog, dtb)


def _bl_bwd(proj, alog, dtb, dbl, *, br=256):
    T = proj.shape[0]

    def body(x_ref, a_ref, d_ref, g_ref, dx_ref, da_ref, dd_ref):
        g = g_ref[0, 0]
        for d in range(2):
            for h in range(GH):
                if d or h:
                    g = g + g_ref[d, h]
        _, vjp = jax.vjp(_bl_fn, x_ref[...], a_ref[...], d_ref[...])
        dx, da, dd = vjp(g)
        dx_ref[...] = dx.astype(BF16)

        @pl.when(pl.program_id(0) == 0)
        def _():
            da_ref[...] = jnp.zeros_like(da_ref)
            dd_ref[...] = jnp.zeros_like(dd_ref)

        da_ref[...] += da
        dd_ref[...] += dd

    vec = pl.BlockSpec((1, HD), lambda i: (0, 0))
    return _call(body, name="bl_bwd", out_shape=(_sds((T, HD), BF16), _sds((1, HD)), _sds((1, HD))), grid=(T // br,),
                 in_specs=[pl.BlockSpec((br, HD), lambda i: (i, C_BL // HD)), vec, vec,
                           pl.BlockSpec((2, GH, br, HD), lambda i: (0, 0, i, 0))],
                 out_specs=(pl.BlockSpec((br, HD), lambda i: (i, 0)), vec, vec), sem=("arbitrary",))(proj, alog, dtb, dbl)


def _chunk_masks(d):
    ii = lax.broadcasted_iota(jnp.int32, (CH, CH), 0)
    jj = lax.broadcasted_iota(jnp.int32, (CH, CH), 1)
    eye = (ii == jj).astype(F32)
    before = jnp.where(d == 0, (jj < ii).astype(F32), (jj > ii).astype(F32))
    return before, before + eye, eye


def _intra_fn(masks, sel_b, sel_l, qs, ks, vs, bls, xs=None):
    before, ateq, eye = masks
    ones = jnp.ones((CH, CH), F32)
    inc = ateq > 0.0
    each = lambda f, *ls: [f(*t) for t in zip(*ls)]
    beta = each(lambda bl: jnp.sum(bl * sel_b, axis=-1, keepdims=True), bls)
    la = each(lambda bl: jnp.sum(bl * sel_l, axis=-1, keepdims=True), bls)
    gam = each(lambda a: _mask_nn(ateq, jnp.broadcast_to(a, (CH, HD))), la)
    gi = each(lambda a: _mask_nn(ateq, jnp.broadcast_to(a, (CH, CH))), la)
    gj = each(lambda g: _mask_nn(ones, eye * g), gi)
    kk = each(lambda k: _nt(k, k), ks)
    qk = each(_nt, qs, ks)
    dec = each(lambda a, b: jnp.where(inc, jnp.exp(jnp.where(inc, a - b, 0.0)), 0.0), gi, gj)
    lmat = each(lambda b, d, m: before * (b * d * m), beta, dec, kk)
    if xs is None:
        x = each(lambda m: eye - m, lmat)
        p2 = each(lambda m: _mdot(m, m), lmat)
        for it in range(5):
            x = each(lambda a, b: a + _mdot(a, b), x, p2)
            if it < 4:
                p2 = each(lambda m: _mdot(m, m), p2)
    else:
        x = each(_saved_inverse, lmat, xs)
    eg = each(jnp.exp, gam)
    u = each(lambda a, b, v: _mdot(a, b * v), x, beta, vs)
    w = each(lambda a, b, e, k: _mdot(a, (b * e) * k), x, beta, eg, ks)
    tot = each(lambda a: jnp.sum(a, axis=0, keepdims=True), la)
    kd = each(lambda k, t, g: k * jnp.exp(t - g), ks, tot, gam)
    gl = each(lambda t: jnp.broadcast_to(jnp.exp(t), (1, HD)), tot)
    qd = each(lambda q, e: q * e, qs, eg)
    p = each(lambda d, m: d * m, dec, qk)
    return (u, w, kd, qd, p, gl, x) if xs is None else (u, w, kd, qd, p, gl)


def _dir_head_sel(d, h):
    lane = lax.broadcasted_iota(jnp.int32, (1, HD), 1)
    return (lane == d * GH + h).astype(F32), (lane == 2 * GH + d * GH + h).astype(F32)


def _intra_specs(T, G):
    nc = T // CH
    assert nc % G == 0
    qkv = pl.BlockSpec((None, G * CH, HD), lambda d, h, c: (h, c, 0))
    bl = pl.BlockSpec((G * CH, HD), lambda d, h, c: (c, 0))
    big = pl.BlockSpec((None, None, G * CH, HD), lambda d, h, c: (d, h, c, 0))
    pm = pl.BlockSpec((None, None, G * CH, CH), lambda d, h, c: (d, h, c, 0))
    gl = pl.BlockSpec((None, None, G, 1, HD), lambda d, h, c: (d, h, c, 0, 0))
    shapes = (_sds((2, GH, T, HD)),) + (_sds((2, GH, T, HD), BF16),) * 3 + (
        _sds((2, GH, T, CH), BF16), _sds((2, GH, nc, 1, HD)), _sds((2, GH, T, CH)))
    return nc, qkv, bl, big, pm, gl, shapes


def _chunks_per_step(T, most):
    nc = T // CH
    return max(g for g in range(1, most + 1) if nc % g == 0)


def _intra_fwd(q, k, v, bl, exch):
    T = q.shape[1]
    G = _chunks_per_step(T, INTRA_FWD_CHUNKS)
    nc, qkv_s, bl_s, big, pm, gl_s, shapes = _intra_specs(T, G)

    def body(q_ref, k_ref, v_ref, bl_ref, u_ref, w_ref, kd_ref, qd_ref, p_ref, gl_ref, x_ref):
        d, h = pl.program_id(0), pl.program_id(1)
        sb, sl = _dir_head_sel(d, h)
        rows = [slice(g * CH, (g + 1) * CH) for g in range(G)]
        outs = _intra_fn(_chunk_masks(d), sb, sl, *[[r[s, :] for s in rows] for r in (q_ref, k_ref, v_ref, bl_ref)])
        for g in range(G):
            for r, o in zip((u_ref, w_ref, kd_ref, qd_ref, p_ref, x_ref), outs[:5] + outs[6:]):
                r[rows[g], :] = o[g].astype(r.dtype)
            gl_ref[g] = outs[5][g]

    return _call_carrying(body, exch, name="gdn_intra_fwd", out_shape=shapes, grid=(2, GH, nc // G),
                          in_specs=[qkv_s, qkv_s, qkv_s, bl_s], out_specs=(big, big, big, big, pm, gl_s, pm))(q, k, v, bl)


def _intra_bwd(q, k, v, bl, xinv, cts, exch):
    T = q.shape[1]
    G = _chunks_per_step(T, INTRA_BWD_CHUNKS)
    nc, qkv_s, bl_s, big, pm, gl_s, _ = _intra_specs(T, G)

    def body(q_ref, k_ref, v_ref, bl_ref, x_ref, du, dw, dkd, dqd, dp, dgl, dq_ref, dk_ref, dv_ref, dbl_ref):
        d, h = pl.program_id(0), pl.program_id(1)
        sb, sl = _dir_head_sel(d, h)
        rows = [slice(g * CH, (g + 1) * CH) for g in range(G)]
        fn = functools.partial(_intra_fn, _chunk_masks(d), sb, sl, xs=[x_ref[s, :] for s in rows])
        _, vjp = jax.vjp(fn, *[[r[s, :] for s in rows] for r in (q_ref, k_ref, v_ref, bl_ref)])
        cts = tuple([r[s, :] for s in rows] for r in (du, dw, dkd, dqd, dp)) + ([dgl[g] for g in range(G)],)
        grads = vjp(cts)
        for g in range(G):
            for r, o in zip((dq_ref, dk_ref, dv_ref, dbl_ref), grads):
                r[rows[g], :] = o[g]

    return _call_carrying(body, exch, name="gdn_intra_bwd", out_shape=(_sds((2, GH, T, HD)),) * 4,
                          grid=(2, GH, nc // G), in_specs=[qkv_s, qkv_s, qkv_s, bl_s, pm, big, big, big, big, pm, gl_s],
                          out_specs=(big,) * 4)(q, k, v, bl, xinv, *cts)


def _scan_fn(s, u, w, kd, qd, p, gl):
    each = lambda f, *ls: [f(*t) for t in zip(*ls)]
    ws = each(_nn, w, s)
    delta = each(lambda a, b: a - b, u, ws)
    kdd = each(_tn, kd, delta)
    s_new = each(lambda g, a, b: g * a + b, gl, s, kdd)
    qs = each(_nn, qd, s)
    pd = each(_nn, p, delta)
    return each(lambda a, b: a + b, qs, pd), s_new


SCAN_BLOCK = 4


def _scan_visit(t, d, nb, ncb):
    rev = jnp.where(t < ncb, ncb - 1 - t, nb - 1 - (t - ncb))
    return jnp.where(d == 0, t, rev)


def _scan_specs(T, L, back):
    tb = SCAN_BLOCK * CH
    assert T % tb == 0 and L % tb == 0
    nb, ncb = T // tb, L // tb

    def at(d, t):
        return _scan_visit(nb - 1 - t if back else t, d, nb, ncb)

    big = pl.BlockSpec((None, GH, tb, HD), lambda d, t: (d, 0, at(d, t), 0))
    pm = pl.BlockSpec((None, GH, tb, CH), lambda d, t: (d, 0, at(d, t), 0))
    gl = pl.BlockSpec((None, GH, SCAN_BLOCK, 1, HD), lambda d, t: (d, 0, at(d, t), 0, 0))
    st = pl.BlockSpec((None, GH, SCAN_BLOCK, HD, HD), lambda d, t: (d, 0, at(d, t), 0, 0))
    do = pl.BlockSpec((GH, tb, HD), lambda d, t: (0, at(d, t), 0))
    return nb, big, pm, gl, st, do


def _scan_fwd(u, w, kd, qd, p, gl, L):
    T = u.shape[2]
    nb, big, pm, gl_s, st, _ = _scan_specs(T, L, False)
    heads = range(GH)

    def body(u_ref, w_ref, kd_ref, qd_ref, p_ref, gl_ref, o_ref, st_ref, s_scr):
        d = pl.program_id(0)

        @pl.when(pl.program_id(1) == 0)
        def _():
            s_scr[...] = jnp.zeros_like(s_scr)

        s = [s_scr[h] for h in heads]
        for i in range(SCAN_BLOCK):
            c = jnp.where(d == 0, i, SCAN_BLOCK - 1 - i)
            rows = pl.ds(pl.multiple_of(c * CH, CH), CH)
            for h in heads:
                st_ref[h, c] = s[h]
            o, s = _scan_fn(s, *[[r[h, rows, :].astype(F32) for h in heads] for r in (u_ref, w_ref, kd_ref, qd_ref, p_ref)],
                            [gl_ref[h, c] for h in heads])
            for h in heads:
                o_ref[h, rows, :] = o[h]
        for h in heads:
            s_scr[h] = s[h]

    return _call(body, name="gdn_scan_fwd", out_shape=(_sds((2, GH, T, HD)), _sds((2, GH, T // CH, HD, HD))),
                 grid=(2, nb), in_specs=[big, big, big, big, pm, gl_s], out_specs=(big, st),
                 scratch=[pltpu.VMEM((GH, HD, HD), F32)], sem=("parallel", "arbitrary"))(u, w, kd, qd, p, gl)


def _scan_bwd(u, w, kd, qd, p, gl, states, do, L, exch):
    T = u.shape[2]
    nb, big, pm, gl_s, st, do_s = _scan_specs(T, L, True)
    heads = range(GH)

    def body(u_ref, w_ref, kd_ref, qd_ref, p_ref, gl_ref, st_ref, do_ref,
             du_ref, dw_ref, dkd_ref, dqd_ref, dp_ref, dgl_ref, ds_scr):
        d = pl.program_id(0)

        @pl.when(pl.program_id(1) == 0)
        def _():
            ds_scr[...] = jnp.zeros_like(ds_scr)

        ds = [ds_scr[h] for h in heads]
        for i in range(SCAN_BLOCK):
            c = jnp.where(d == 0, SCAN_BLOCK - 1 - i, i)
            rows = pl.ds(pl.multiple_of(c * CH, CH), CH)
            _, vjp = jax.vjp(_scan_fn, [st_ref[h, c] for h in heads],
                             *[[r[h, rows, :].astype(F32) for h in heads] for r in (u_ref, w_ref, kd_ref, qd_ref, p_ref)],
                             [gl_ref[h, c] for h in heads])
            ds, gu, gw, gkd, gqd, gp, ggl = vjp(([do_ref[h, rows, :] for h in heads], ds))
            for h in heads:
                du_ref[h, rows, :] = gu[h]
                dw_ref[h, rows, :] = gw[h]
                dkd_ref[h, rows, :] = gkd[h]
                dqd_ref[h, rows, :] = gqd[h]
                dp_ref[h, rows, :] = gp[h]
                dgl_ref[h, c] = ggl[h]
        for h in heads:
            ds_scr[h] = ds[h]

    return _call_carrying(
        body, exch, name="gdn_scan_bwd",
        out_shape=(_sds((2, GH, T, HD)),) * 4 + (_sds((2, GH, T, CH)), _sds((2, GH, T // CH, 1, HD))),
        grid=(2, nb), in_specs=[big, big, big, big, pm, gl_s, st, do_s], out_specs=(big, big, big, big, pm, gl_s),
        scratch=[pltpu.VMEM((GH, HD, HD), F32)])(u, w, kd, qd, p, gl, states, do)


def _gout_fn(o0, o1, z, gw):
    return _rms(o0 + o1) * gw * _silu(z)


def _gout_fwd(o, proj, gw, L):
    T = o.shape[2]
    N = T - L
    ob = pl.BlockSpec((2, None, T, HD), lambda h: (0, h, 0, 0))

    def body(o_ref, z_ref, gw_ref, y_ref):
        y_ref[...] = _gout_fn(o_ref[0, L:, :], o_ref[1, L:, :], z_ref[L:, :], gw_ref[...]).astype(BF16)

    return _call(body, name="gout_fwd", out_shape=_sds((N, GH * HD), BF16), grid=(GH,),
                 in_specs=[ob, pl.BlockSpec((T, HD), lambda h: (0, C_Z // HD + h)), pl.BlockSpec((1, HD), lambda h: (0, 0))],
                 out_specs=pl.BlockSpec((N, HD), lambda h: (0, h)), sem=("parallel",))(o, proj, gw)


def _gout_bwd(o, proj, gw, dy, L):
    T = o.shape[2]
    N = T - L
    ob = pl.BlockSpec((2, None, T, HD), lambda h: (0, h, 0, 0))

    def body(o_ref, z_ref, gw_ref, dy_ref, do_ref, dz_ref, dgw_ref):
        _, vjp = jax.vjp(_gout_fn, o_ref[0, L:, :], o_ref[1, L:, :], z_ref[L:, :], gw_ref[...])
        g0, _, gz, ggw = vjp(dy_ref[...])
        do_ref[:L, :] = jnp.zeros((L, HD), F32)
        do_ref[L:, :] = g0
        dz_ref[:L, :] = jnp.zeros((L, HD), BF16)
        dz_ref[L:, :] = gz.astype(BF16)

        @pl.when(pl.program_id(0) == 0)
        def _():
            dgw_ref[...] = jnp.zeros_like(dgw_ref)

        dgw_ref[...] += ggw

    return _call(body, name="gout_bwd", out_shape=(_sds((GH, T, HD)), _sds((T, GH * HD), BF16), _sds((1, HD))),
                 grid=(GH,),
                 in_specs=[ob, pl.BlockSpec((T, HD), lambda h: (0, C_Z // HD + h)), pl.BlockSpec((1, HD), lambda h: (0, 0)),
                           pl.BlockSpec((N, HD), lambda h: (0, h))],
                 out_specs=(pl.BlockSpec((None, T, HD), lambda h: (h, 0, 0)), pl.BlockSpec((T, HD), lambda h: (0, h)),
                            pl.BlockSpec((1, HD), lambda h: (0, 0))),
                 sem=("arbitrary",))(o, proj, gw, dy)


def _merge_fn(pa, pd, ga, gd):
    return jax.nn.sigmoid(ga) * pa + jax.nn.sigmoid(gd) * pd


def _merge_fwd(pa, pd, proj, L, *, br=256):
    N = pa.shape[0]
    lb = L // br
    row = pl.BlockSpec((br, D), lambda i: (i, 0))

    def body(pa_ref, pd_ref, ga_ref, gd_ref, y_ref):
        y_ref[...] = _merge_fn(pa_ref[...], pd_ref[...], ga_ref[...], gd_ref[...]).astype(BF16)

    return _call(body, name="merge_fwd", out_shape=_sds((N, D), BF16), grid=(N // br,),
                 in_specs=[row, row, pl.BlockSpec((br, D), lambda i: (i + lb, C_GATE // D)),
                           pl.BlockSpec((br, D), lambda i: (i + lb, C_GATE // D + 1))],
                 out_specs=row, sem=("parallel",))(pa, pd, proj, proj)


def _merge_bwd(pa, pd, proj, dy, L, *, br=256):
    N = pa.shape[0]
    T = N + L
    lb = L // br
    lrow = pl.BlockSpec((br, D), lambda i: (jnp.maximum(i - lb, 0), 0))

    def body(pa_ref, pd_ref, ga_ref, gd_ref, dy_ref, dpa_ref, dpd_ref, dg_ref):
        lat = pl.program_id(0) >= lb
        _, vjp = jax.vjp(_merge_fn, pa_ref[...], pd_ref[...], ga_ref[...], gd_ref[...])
        gpa, gpd, gga, ggd = vjp(dy_ref[...])
        dpa_ref[...] = gpa.astype(BF16)
        dpd_ref[...] = gpd.astype(BF16)
        dg_ref[:, :D] = jnp.where(lat, gga, 0.0).astype(BF16)
        dg_ref[:, D:] = jnp.where(lat, ggd, 0.0).astype(BF16)

    return _call(body, name="merge_bwd", out_shape=(_sds((N, D), BF16), _sds((N, D), BF16), _sds((T, 2 * D), BF16)),
                 grid=(T // br,),
                 in_specs=[lrow, lrow, pl.BlockSpec((br, D), lambda i: (i, C_GATE // D)),
                           pl.BlockSpec((br, D), lambda i: (i, C_GATE // D + 1)), lrow],
                 out_specs=(lrow, lrow, pl.BlockSpec((br, 2 * D), lambda i: (i, 0))),
                 sem=("arbitrary",))(pa, pd, proj, proj, dy)


def _resid_fwd(x, m, mod, i_g, *, name, br=256):
    R = x.shape[0]
    row = pl.BlockSpec((br, D), lambda i: (i, 0))

    def body(x_ref, m_ref, mod_ref, o_ref):
        o_ref[...] = x_ref[...] + mod_ref[i_g:i_g + 1, :] * m_ref[...]

    return _call(body, name=name, out_shape=_sds((R, D)), grid=(R // br,),
                 in_specs=[row, row, pl.BlockSpec((6, D), lambda i: (0, 0))], out_specs=row,
                 sem=("parallel",))(x, m, mod)


def _resid_bwd(dx, m, mod, i_g, *, name, br=256):
    R = dx.shape[0]
    row = pl.BlockSpec((br, D), lambda i: (i, 0))
    vec = pl.BlockSpec((1, D), lambda i: (0, 0))

    def body(dx_ref, m_ref, mod_ref, dm_ref, dg_ref):
        dxv = dx_ref[...]
        dm_ref[...] = (dxv * mod_ref[i_g:i_g + 1, :]).astype(BF16)

        @pl.when(pl.program_id(0) == 0)
        def _():
            dg_ref[...] = jnp.zeros_like(dg_ref)

        dg_ref[...] += jnp.sum(dxv * m_ref[...], axis=0, keepdims=True)

    return _call(body, name=name, out_shape=(_sds((R, D), BF16), _sds((1, D))), grid=(R // br,),
                 in_specs=[row, row, pl.BlockSpec((6, D), lambda i: (0, 0))], out_specs=(row, vec),
                 sem=("arbitrary",))(dx, m, mod)


def _ffn_fn(shifts, ug, uv, wg, wv, bg, bv):
    down, up = shifts

    def conv(x, w, b):
        return down(x) * w[0:1, :] + x * w[1:2, :] + up(x) * w[2:3, :] + b

    return _silu(conv(ug, wg, bg)) * conv(uv, wv, bv)


def _ffn_fwd(up, cw, cb, *, bw=256):
    N = up.shape[0]
    shifts = _make_shift(((0, N),))
    nb = DFF // bw

    def body(ug, uv, wg, wv, bg, bv, a_ref):
        a_ref[...] = _ffn_fn(shifts, ug[...], uv[...], wg[...], wv[...], bg[...], bv[...]).astype(BF16)

    def col(rows, off):
        return pl.BlockSpec((rows, bw), lambda j: (0, j + off))

    return _call(body, name="ffn_fwd", out_shape=_sds((N, DFF), BF16), grid=(nb,),
                 in_specs=[col(N, 0), col(N, nb), col(3, 0), col(3, nb), col(1, 0), col(1, nb)],
                 out_specs=col(N, 0), sem=("parallel",), vmem=VMEM_BIG)(up, up, cw, cw, cb, cb)


def _ffn_bwd(up, cw, cb, da, *, bw=256):
    N = up.shape[0]
    shifts = _make_shift(((0, N),))
    nb = DFF // bw

    def body(ug, uv, wg, wv, bg, bv, da_ref, dug, duv, dwg, dwv, dbg, dbv):
        _, vjp = jax.vjp(functools.partial(_ffn_fn, shifts), ug[...], uv[...], wg[...], wv[...], bg[...], bv[...])
        g = vjp(da_ref[...])
        dug[...] = g[0].astype(BF16)
        duv[...] = g[1].astype(BF16)
        dwg[...], dwv[...], dbg[...], dbv[...] = g[2], g[3], g[4], g[5]

    def col(rows, off):
        return pl.BlockSpec((rows, bw), lambda j: (0, j + off))

    half = (_sds((N, DFF), BF16), _sds((N, DFF), BF16), _sds((3, DFF)), _sds((3, DFF)), _sds((1, DFF)), _sds((1, DFF)))
    dug, duv, dwg, dwv, dbg, dbv = _call(
        body, name="ffn_bwd", out_shape=half, grid=(nb,),
        in_specs=[col(N, 0), col(N, nb), col(3, 0), col(3, nb), col(1, 0), col(1, nb), col(N, 0)],
        out_specs=(col(N, 0), col(N, 0), col(3, 0), col(3, 0), col(1, 0), col(1, 0)),
        sem=("parallel",), vmem=VMEM_BIG)(up, up, cw, cw, cb, cb, da)
    return (jnp.concatenate([dug, duv], axis=1), jnp.concatenate([dwg, dwv], axis=1),
            jnp.concatenate([dbg, dbv], axis=1))


def _head_fn(x1, dn, g2, fw, tgt):
    y = _rms(x1 + g2 * dn) * fw
    err = y - tgt
    return 0.5 * jnp.sum(jnp.mean(err * err, axis=-1))


def _head(x1, dn, mod, fw, tgt, *, br=256):
    N = x1.shape[0]
    row = pl.BlockSpec((br, D), lambda i: (i, 0))
    vec = pl.BlockSpec((1, D), lambda i: (0, 0))
    one = pl.BlockSpec((1, HD), lambda i: (0, 0))

    def body(x1_ref, dn_ref, mod_ref, fw_ref, tgt_ref, loss_ref, dx_ref, ddn_ref, dg_ref, dfw_ref):
        loss, (gx, gdn, gg, gfw) = jax.value_and_grad(_head_fn, argnums=(0, 1, 2, 3))(
            x1_ref[...], dn_ref[...], mod_ref[5:6, :], fw_ref[...], tgt_ref[...])
        dx_ref[...] = gx
        ddn_ref[...] = gdn.astype(BF16)

        @pl.when(pl.program_id(0) == 0)
        def _():
            loss_ref[...] = jnp.zeros_like(loss_ref)
            dg_ref[...] = jnp.zeros_like(dg_ref)
            dfw_ref[...] = jnp.zeros_like(dfw_ref)

        loss_ref[...] += jnp.broadcast_to(loss, (1, HD))
        dg_ref[...] += gg
        dfw_ref[...] += gfw

    return _call(body, name="head", out_shape=(_sds((1, HD)), _sds((N, D)), _sds((N, D), BF16), _sds((1, D)), _sds((1, D))),
                 grid=(N // br,), in_specs=[row, row, pl.BlockSpec((6, D), lambda i: (0, 0)), vec, row],
                 out_specs=(one, row, row, vec, vec), sem=("arbitrary",))(x1, dn, mod, fw, tgt)


def _adamw(w, g, m, v, *, name):
    shape = w.shape
    cols = shape[-1]
    rows = max(1, math.prod(shape[:-1]))
    w2, g2, m2, v2 = (t.reshape(rows, cols) for t in (w, g, m, v))
    br = 256 if rows % 256 == 0 else (128 if rows % 128 == 0 else (8 if rows % 8 == 0 and rows > 64 else rows))
    if rows % 352 == 0:
        br = 352
    c1 = 1.0 - B1 ** STEP
    c2 = 1.0 - B2 ** STEP

    def body(w_ref, g_ref, m_ref, v_ref, d_ref, nm_ref, nv_ref):
        gv = g_ref[...]
        nm = B1 * m_ref[...] + (1.0 - B1) * gv
        nv = B2 * v_ref[...] + (1.0 - B2) * (gv * gv)
        d_ref[...] = -LR * ((nm / c1) / (jnp.sqrt(nv / c2) + AEPS) + WD * w_ref[...])
        nm_ref[...] = nm
        nv_ref[...] = nv

    blk = pl.BlockSpec((br, cols), lambda i: (i, 0))
    outs = _call(body, name=name, out_shape=(_sds((rows, cols)),) * 3, grid=(rows // br,),
                 in_specs=[blk] * 4, out_specs=(blk,) * 3, sem=("parallel",))(w2, g2, m2, v2)
    return tuple(t.reshape(shape) for t in outs)


def _rope_tables(N, L):
    t = jnp.arange(N)
    pos = jnp.stack([(t // GRID_W).astype(F32), (t % GRID_W).astype(F32)], axis=1)
    inv = ROPE_THETA ** (-jnp.arange(0, HD // 2, 2, dtype=F32) / (HD // 2))
    ang = pos[:, :, None] * inv[None, None, :]
    cos = jnp.broadcast_to(jnp.cos(ang)[:, :, None, :], (N, 2, 2, HD // 4)).reshape(N, HD)
    sin = jnp.broadcast_to(jnp.sin(ang)[:, :, None, :], (N, 2, 2, HD // 4))
    sin = (sin * jnp.array([-1.0, 1.0], F32)[None, None, :, None]).reshape(N, HD)
    cos = jnp.concatenate([jnp.ones((L, HD), F32), cos], axis=0)
    sin = jnp.concatenate([jnp.zeros((L, HD), F32), sin], axis=0)
    return cos, sin


def _pad_lanes(v, off=0):
    return jnp.zeros((1, HD), F32).at[0, off:off + v.shape[0]].set(v)


def _local_step(x, ctx, tgt, mod_lat, mod_ctx, w_in, shards, small):
    N, L = x.shape[0], ctx.shape[0]
    T = N + L
    bounds = ((0, L), (L, T))
    qw, kw, gw = small["q_norm_w"], small["k_norm_w"], small["gdn_norm_w"]
    conv_w, ffn_w, ffn_b, fnw = small["conv_qkv_w"], small["ffn_conv_w"], small["ffn_conv_b"], small["final_norm_w"]
    alog = _pad_lanes(small["a_log"].reshape(-1), 2 * GH)
    dtb = _pad_lanes(small["dt_bias"].reshape(-1), 2 * GH)
    cos, sin = _rope_tables(N, L)
    bt = 256 if T % 768 else 768
    bnl = 256 if N % 1024 else 1024

    hc = _normmod_fwd(ctx, mod_ctx, 0, 1, name="normmod_ctx")
    hx = _normmod_fwd(x, mod_lat, 0, 1, name="normmod_x")
    h1 = jnp.concatenate([hc, hx], axis=0)
    proj = _mm(h1, w_in, name="mm_in", M=T, N=C_END, K=D, tb=True, bm=bt, bn=1024)
    aq, ak, av = _aprep_fwd(proj, cos, sin, qw, kw)
    attn, (up_g,) = _attn_fwd(aq, ak, av, L, _Exchange([shards["w_up"]], False))
    gq = _gprep_fwd(proj, conv_w, 0, bounds)
    gk = _gprep_fwd(proj, conv_w, 1, bounds)
    gv = _gprep_fwd(proj, conv_w, 2, bounds)
    bl = _bl_fwd(proj, alog, dtb)
    intra, (down_g, pa_g, pd_g, out_g) = _intra_fwd(
        gq, gk, gv, bl, _Exchange([shards[n] for n in ("w_down", "w_pa", "w_pd", "w_out")], False))
    w_up, w_down = up_g.reshape(2 * DFF, D), down_g.reshape(DFF, D)
    w_pa, w_pd, w_out = pa_g.reshape(D, D), pd_g.reshape(D, D), out_g.reshape(D, D)
    xinv, intra = intra[6], intra[:6]
    o, states = _scan_fwd(*intra, L)
    gdn = _gout_fwd(o, proj, gw, L)
    pa = _mm(attn, w_pa, name="mm_pa", M=N, N=D, K=D, bm=bnl)
    pd = _mm(gdn, w_pd, name="mm_pd", M=N, N=D, K=D, bm=bnl)
    y = _merge_fwd(pa, pd, proj, L)
    m = _mm(y, w_out, name="mm_out", M=N, N=D, K=D, bm=bnl)
    x1 = _resid_fwd(x, m, mod_lat, 2, name="resid1")
    h2 = _normmod_fwd(x1, mod_lat, 3, 4, name="normmod_x1")
    up = _mm(h2, w_up, name="mm_up", M=N, N=2 * DFF, K=D, tb=True, bm=bnl, bn=2 * DFF // 4)
    a = _ffn_fwd(up, ffn_w, ffn_b)
    dn = _mm(a, w_down, name="mm_down", M=N, N=D, K=DFF, bm=bnl)
    loss, dx2, ddn, dg2, dfnw = _head(x1, dn, mod_lat, fnw, tgt)

    da = _mm(ddn, w_down, name="mm_down_dx", M=N, N=DFF, K=D, tb=True, bm=bnl, bn=DFF // 2)
    g_down = _mm(a, ddn, name="mm_down_dw", M=DFF, N=D, K=N, ta=True, bm=DFF // 2, out_dtype=BF16)
    dup, d_ffn_w, d_ffn_b = _ffn_bwd(up, ffn_w, ffn_b, da)
    dh2 = _mm(dup, w_up, name="mm_up_dx", M=N, N=D, K=2 * DFF, bm=bnl, bk=2 * DFF // 4)
    g_up = _mm(dup, h2, name="mm_up_dw", M=2 * DFF, N=D, K=N, ta=True, bm=2 * DFF // 4, out_dtype=BF16)
    dx1, dsh2, dsc2 = _normmod_bwd(x1, mod_lat, 3, 4, dh2, 0, dx2, name="normmod_x1_bwd")
    dm, dg1 = _resid_bwd(dx1, m, mod_lat, 2, name="resid1_bwd")
    dy = _mm(dm, w_out, name="mm_out_dx", M=N, N=D, K=D, tb=True, bm=bnl)
    g_out = _mm(y, dm, name="mm_out_dw", M=D, N=D, K=N, ta=True, out_dtype=BF16)
    dpa, dpd, dgate = _merge_bwd(pa, pd, proj, dy, L)
    dattn = _mm(dpa, w_pa, name="mm_pa_dx", M=N, N=D, K=D, tb=True, bm=bnl)
    g_pa = _mm(attn, dpa, name="mm_pa_dw", M=D, N=D, K=N, ta=True, out_dtype=BF16)
    dgdn = _mm(dpd, w_pd, name="mm_pd_dx", M=N, N=D, K=D, tb=True, bm=bnl)
    g_pd = _mm(gdn, dpd, name="mm_pd_dw", M=D, N=D, K=N, ta=True, out_dtype=BF16)
    do, dz, dgw = _gout_bwd(o, proj, gw, dgdn, L)
    cts, recv_a = _scan_bwd(*intra, states, do, L, _Exchange(
        [g_out.reshape(NDEV, D // NDEV, D), g_down.reshape(NDEV, DFF // NDEV, D)], True))
    (dgq, dgk, dgv, dbl), recv_b = _intra_bwd(gq, gk, gv, bl, xinv, cts, _Exchange(
        [g_pa.reshape(NDEV, D // NDEV, D), g_pd.reshape(NDEV, D // NDEV, D), g_up.reshape(NDEV, 2 * DFF // NDEV, D)], True))
    recv = dict(zip(("w_out", "w_down", "w_pa", "w_pd", "w_up"), recv_a + recv_b))
    dxq, dwq = _gprep_bwd(proj, conv_w, 0, bounds, dgq)
    dxk, dwk = _gprep_bwd(proj, conv_w, 1, bounds, dgk)
    dxv, dwv = _gprep_bwd(proj, conv_w, 2, bounds, dgv)
    dxbl, dalog, ddtb = _bl_bwd(proj, alog, dtb, dbl)
    daq_h, dak_h, dav_h = _attn_bwd(aq, ak, av, dattn, L)
    daq, dkv, dqw, dkw = _aprep_bwd(proj, cos, sin, qw, kw, daq_h, dak_h, dav_h, L)
    dproj = jnp.concatenate([dkv, dxq, dxk, dxv, dxbl, jnp.zeros((T, C_AQ - C_BL - HD), BF16), daq, dz, dgate], axis=1)
    g_in = _mm(dproj, h1, name="mm_in_dw", M=C_END, N=D, K=T, ta=True, bm=1024, out_dtype=BF16)
    g_in = jnp.concatenate([g_in[:W_AQ], g_in[C_AQ:]], axis=0).reshape(NDEV, W_END // NDEV, D)
    own_in = lax.dynamic_index_in_dim(g_in, _position()[3], axis=0, keepdims=False)
    *pending, token = _scatter_start(g_in, None, (0, D // 2), (), name="scatter_g_in_a_start")
    dh1 = _mm(dproj, w_in, name="mm_in_dx", M=T, N=D, K=C_END, bm=bt, bk=1024, after=(token,))
    grad_x, dsh1, dsc1 = _normmod_bwd(x, mod_lat, 0, 1, dh1, L, dx1, name="normmod_x_bwd")
    _, dcsh1, dcsc1 = _normmod_bwd(ctx, mod_ctx, 0, 1, dh1, 0, None, name="normmod_ctx_bwd")

    z1 = jnp.zeros((1, D), F32)
    dmod_lat = jnp.concatenate([dsh1, dsc1, dg1, dsh2, dsc2, dg2], axis=0)
    dmod_ctx = jnp.concatenate([dcsh1, dcsc1, z1, z1, z1, z1], axis=0)
    gsmall = {
        "q_norm_w": dqw, "k_norm_w": dkw, "gdn_norm_w": dgw,
        "conv_qkv_w": jnp.concatenate([dwq, dwk, dwv], axis=1),
        "a_log": dalog[0, 2 * GH:4 * GH], "dt_bias": ddtb[0, 2 * GH:4 * GH],
        "ffn_conv_w": d_ffn_w, "ffn_conv_b": d_ffn_b, "final_norm_w": dfnw,
    }
    return loss[0, 0], grad_x, (pending, own_in), recv, dmod_lat, dmod_ctx, gsmall


HBM = pl.BlockSpec(memory_space=pltpu.HBM)


def _position():
    x, y, c = lax.axis_index("x"), lax.axis_index("y"), lax.axis_index("c")
    return x, y, c, 4 * x + 2 * y + c


def _peer(x, y, c, k):
    px = 1 - x if k & 4 else x
    py = 1 - y if k & 2 else y
    pc = 1 - c if k & 1 else c
    return (px, py, pc), 4 * px + 2 * py + pc


def _exchange(arrs, *, name, scatter):
    exch = _Exchange(arrs, scatter)
    n = exch.n

    def body(*refs):
        ins, outs, sems = refs[:n], refs[n:2 * n], refs[2 * n:]
        exch.start(ins, outs, sems)
        exch.finish(ins, outs, sems)

    outs = pl.pallas_call(body, name=name, out_shape=exch.out_shape, in_specs=[HBM] * n, out_specs=(HBM,) * n,
                          scratch_shapes=exch.scratch,
                          compiler_params=pltpu.CompilerParams(has_side_effects=True))(*arrs)
    return list(outs)


class _Exchange:
    def __init__(self, arrs, scatter):
        self.arrs, self.scatter, self.n = list(arrs), scatter, len(arrs)
        self.out_shape = tuple(_sds(a.shape if scatter else (NDEV,) + a.shape, a.dtype) for a in arrs)
        self.scratch = [pltpu.SemaphoreType.DMA((self.n, NDEV - 1)), pltpu.SemaphoreType.DMA((self.n, NDEV - 1)),
                        pltpu.SemaphoreType.DMA((self.n,))]

    def _copies(self, ins, outs, sems):
        send, recv, loc = sems
        x, y, c, me = _position()
        local = [pltpu.make_async_copy(ins[a].at[me] if self.scatter else ins[a], outs[a].at[me], loc.at[a])
                 for a in range(self.n)]
        remote = []
        for k in range(1, NDEV):
            peer, pid = _peer(x, y, c, k)
            for a in range(self.n):
                src = ins[a].at[pid] if self.scatter else ins[a]
                remote.append(pltpu.make_async_remote_copy(
                    src_ref=src, dst_ref=outs[a].at[me], send_sem=send.at[a, k - 1], recv_sem=recv.at[a, k - 1],
                    device_id=peer, device_id_type=MESH))
        return local, remote

    def start(self, ins, outs, sems):
        local, remote = self._copies(ins, outs, sems)
        for cp in local + remote:
            cp.start()

    def finish(self, ins, outs, sems):
        local, remote = self._copies(ins, outs, sems)
        for cp in remote:
            cp.wait()
        for cp in local:
            cp.wait()


def _gather_two_level(block, *, name):
    def body(x_ref, out_ref, send_sems, recv_sems, local_sem):
        x, y, c, _ = _position()
        me, sibling = (x, y, c), (x, y, 1 - c)
        chips = [(1 - x, y), (x, 1 - y), (1 - x, 1 - y)]

        def slot(px, py, pc):
            return out_ref.at[4 * px + 2 * py + pc]

        def copy(k, owner, to, src=None):
            return pltpu.make_async_remote_copy(
                src_ref=slot(*owner) if src is None else src, dst_ref=slot(*owner), send_sem=send_sems.at[k],
                recv_sem=recv_sems.at[k], device_id=to, device_id_type=MESH)

        mine = pltpu.make_async_copy(x_ref, slot(*me), local_sem)
        mine.start()
        first = [copy(0, me, sibling, src=x_ref)]
        first += [copy(1 + j, me, (*chip, c), src=x_ref) for j, chip in enumerate(chips)]
        for cp in first:
            cp.start()
        passed = [copy(4 + j, (*chip, c), sibling) for j, chip in enumerate(chips)]
        for j, chip in enumerate(chips):
            copy(1 + j, (*chip, c), me).wait_recv()
            passed[j].start()
        copy(0, sibling, me).wait_recv()
        for j, chip in enumerate(chips):
            copy(4 + j, (*chip, 1 - c), me).wait_recv()
        for cp in first + passed:
            cp.wait_send()
        mine.wait()

    return pl.pallas_call(
        body, name=name, out_shape=_sds((NDEV,) + block.shape, block.dtype), in_specs=[HBM], out_specs=HBM,
        scratch_shapes=[pltpu.SemaphoreType.DMA((NDEV - 1,)), pltpu.SemaphoreType.DMA((NDEV - 1,)),
                        pltpu.SemaphoreType.DMA],
        compiler_params=pltpu.CompilerParams(has_side_effects=True))(block)


SEM = pl.BlockSpec(memory_space=pltpu.SEMAPHORE)


def _scatter_copies(src_ref, land_ref, send_sems, recv_sems, cols):
    x, y, c, me = _position()
    span = (slice(None), pl.ds(*cols))
    copies = []
    for k in range(1, NDEV):
        peer, pid = _peer(x, y, c, k)
        copies.append(pltpu.make_async_remote_copy(
            src_ref=src_ref.at[pid].at[span], dst_ref=land_ref.at[me].at[span], send_sem=send_sems.at[k - 1],
            recv_sem=recv_sems.at[k - 1], device_id=peer, device_id_type=MESH))
    return copies


SPLIT_EFFECT = pltpu.SideEffectType.DATAFLOW_SIDE_EFFECTING


def _scatter_start(parts, land, cols, after, *, name):
    na = len(after)
    if land is None:
        land = lax.empty(parts.shape, parts.dtype)

    def body(src_ref, land_ref, *rest):
        send_sems, recv_sems, _, _, token = rest[na:]
        for cp in _scatter_copies(src_ref, land_ref, send_sems, recv_sems, cols):
            cp.start()
        token[...] = jnp.zeros_like(token)

    return pl.pallas_call(
        body, name=name,
        out_shape=(pltpu.SemaphoreType.DMA((NDEV - 1,)), pltpu.SemaphoreType.DMA((NDEV - 1,)),
                   pltpu.HBM(parts.shape, parts.dtype), pltpu.HBM(parts.shape, parts.dtype), _sds((8, HD))),
        in_specs=(HBM, HBM) + (pl.BlockSpec(memory_space=pl.ANY),) * na,
        out_specs=(SEM, SEM, HBM, HBM, pl.BlockSpec(memory_space=pltpu.VMEM)),
        input_output_aliases={0: 2, 1: 3}, compiler_params=pltpu.CompilerParams(has_side_effects=SPLIT_EFFECT),
    )(pltpu.with_memory_space_constraint(parts, pltpu.HBM), pltpu.with_memory_space_constraint(land, pltpu.HBM), *after)


def _scatter_wait(send_sems, recv_sems, src_thru, land_thru, cols, after, *, name):
    na = len(after)

    def body(src_ref, land_ref, send_sems, recv_sems, *rest):
        for cp in _scatter_copies(src_ref, land_ref, send_sems, recv_sems, cols):
            cp.wait_send()
            cp.wait_recv()

    return pl.pallas_call(
        body, name=name,
        out_shape=(pltpu.HBM(src_thru.shape, src_thru.dtype), pltpu.HBM(land_thru.shape, land_thru.dtype)),
        in_specs=(HBM, HBM, SEM, SEM) + (pl.BlockSpec(memory_space=pl.ANY),) * na, out_specs=(HBM, HBM),
        input_output_aliases={0: 0, 1: 1}, compiler_params=pltpu.CompilerParams(has_side_effects=SPLIT_EFFECT),
    )(src_thru, land_thru, send_sems, recv_sems, *after)


def _cast_bf16(w, *, name):
    rows, cols = w.shape
    br = 128 if rows % 128 == 0 else rows

    def body(w_ref, o_ref):
        o_ref[...] = w_ref[...].astype(BF16)

    blk = pl.BlockSpec((br, cols), lambda i: (i, 0))
    return _call(body, name=name, out_shape=_sds((rows, cols), BF16), grid=(rows // br,), in_specs=[blk],
                 out_specs=blk, sem=("parallel",))(w)


def _sum_slots(a, *, name):
    _, R, C = a.shape

    def body(a_ref, o_ref):
        s = a_ref[0]
        for d in range(1, NDEV):
            s = s + a_ref[d]
        o_ref[...] = s

    return _call(body, name=name, out_shape=_sds((R, C)))(a)


MODROWS = 16


def _mod_fwd(c9, w, b):
    cols = w.shape[1]

    def body(c_ref, w_ref, b_ref, o_ref):
        o_ref[...] = _nn(_silu(c_ref[...]), w_ref[...]) + b_ref[...]

    return _call(body, name="mod_fwd", out_shape=_sds((MODROWS, cols)))(c9, w, b)


def _mod_bwd(c9, dmy, dall, w, after):
    cols = w.shape[1]
    VM = pl.BlockSpec(memory_space=pltpu.VMEM)

    def body(c_ref, dmy_ref, dall_ref, w_ref, *rest):
        gw_ref, gb_ref, cp_ref = rest[-3:]
        sc = _silu(c_ref[...])
        rows = lax.broadcasted_iota(jnp.int32, (MODROWS, 1), 0)
        d = dmy_ref[...]
        d_ctx = jnp.where(rows == NDEV, d, 0.0)
        sc_ctx = jnp.where(rows == NDEV, sc, 0.0)
        outer = lax.dot_general(sc_ctx, d_ctx, (((0,), (0,)), ((), ())), precision=HI, preferred_element_type=F32)
        gw_ref[...] = _tn(jnp.where(rows < NDEV, sc, 0.0), jnp.where(rows < NDEV, d, 0.0)) + outer
        gb_ref[...] = jnp.sum(dall_ref[...], axis=0, keepdims=True)
        cp_ref[...] = jnp.sum(_nt(d_ctx, w_ref[...]), axis=0, keepdims=True)

    return _call(body, name="mod_bwd", out_shape=(_sds((D, cols)), _sds((1, 6 * D)), _sds((1, D))),
                 in_specs=[VM] * 4 + [pl.BlockSpec(memory_space=pl.ANY)] * len(after), vmem=VMEM_BIG)(c9, dmy, dall, w, *after)


def _cctx_finish(parts, c_ctx):
    def body(p_ref, c_ref, o_ref):
        s = p_ref[0]
        for d in range(1, NDEV):
            s = s + p_ref[d]
        _, vjp = jax.vjp(_silu, c_ref[...])
        o_ref[...] = vjp(s)[0]

    return _call(body, name="cctx_finish", out_shape=_sds((1, D)))(parts, c_ctx)


def _adamw_recv(w, recv, m, v, *, name, own=None):
    rows, cols = w.shape
    bc = 256
    c1 = 1.0 - B1 ** STEP
    c2 = 1.0 - B2 ** STEP
    has_own = own is not None

    def body(w_ref, r_ref, m_ref, v_ref, *rest):
        g_ref, d_ref, nm_ref, nv_ref = rest[-4:]
        me = _position()[3]

        def slot(d):
            return jnp.where(me == d, rest[0][...], r_ref[d]) if has_own else r_ref[d]

        gv = slot(0).astype(F32)
        for d in range(1, NDEV):
            gv = gv + slot(d).astype(F32)
        nm = B1 * m_ref[...] + (1.0 - B1) * gv
        nv = B2 * v_ref[...] + (1.0 - B2) * (gv * gv)
        g_ref[...] = gv
        d_ref[...] = -LR * ((nm / c1) / (jnp.sqrt(nv / c2) + AEPS) + WD * w_ref[...])
        nm_ref[...] = nm
        nv_ref[...] = nv

    blk = pl.BlockSpec((rows, bc), lambda j: (0, j))
    return _call(body, name=name, out_shape=(_sds((rows, cols)),) * 4, grid=(cols // bc,),
                 in_specs=[blk, pl.BlockSpec((NDEV, rows, bc), lambda j: (0, 0, j)), blk, blk] + [blk] * has_own,
                 out_specs=(blk,) * 4, sem=("parallel",), vmem=VMEM_BIG)(w, recv, m, v, *([own] if has_own else []))


P_LAT, P_CTX, P_FNW, P_FFNB, P_CONV, P_FFNW, P_MISC, P_ROWS = 0, 8, 16, 24, 32, 48, 72, 80


def _rows_of(v, nrows):
    flat = v.reshape(-1)
    return jnp.pad(flat, (0, nrows * D - flat.shape[0])).reshape(nrows, D)


def _by_columns(g):
    n, r, c = g.shape
    return jnp.transpose(g, (1, 0, 2)).reshape(r, n * c)


def kernel(x, c, ctx, c_ctx, w_mod, b_mod, w_in, q_norm_w, k_norm_w, conv_qkv_w, a_log, dt_bias, gdn_norm_w, w_pa, w_pd, w_out, w_up, ffn_conv_w, ffn_conv_b, w_down, final_norm_w, loss_target, m_c_ctx, m_w_mod, m_b_mod, m_w_in, m_q_norm_w, m_k_norm_w, m_conv_qkv_w, m_a_log, m_dt_bias, m_gdn_norm_w, m_w_pa, m_w_pd, m_w_out, m_w_up, m_ffn_conv_w, m_ffn_conv_b, m_w_down, m_final_norm_w, v_c_ctx, v_w_mod, v_b_mod, v_w_in, v_q_norm_w, v_k_norm_w, v_conv_qkv_w, v_a_log, v_dt_bias, v_gdn_norm_w, v_w_pa, v_w_pd, v_w_out, v_w_up, v_ffn_conv_w, v_ffn_conv_b, v_w_down, v_final_norm_w):
    _, _, _, me = _position()
    mcols = w_mod.shape[2]

    transposed = ("w_in", "w_up")
    big = {"w_in": w_in[0].T, "w_pa": w_pa[0], "w_pd": w_pd[0], "w_out": w_out[0], "w_up": w_up[0].T, "w_down": w_down[0]}
    names = list(big)
    shards = {n: _cast_bf16(big[n], name="cast_" + n) for n in names}
    w_in_g = _gather_two_level(shards["w_in"], name="gather_w_in")
    c_all, conv_g, ffnw_g = _exchange([c, conv_qkv_w[0], ffn_conv_w[0]], name="gather_small", scatter=False)
    w_in_full = w_in_g.reshape(W_END, D)
    w_in_pad = jnp.concatenate([w_in_full[:W_AQ], jnp.zeros((C_AQ - W_AQ, D), BF16), w_in_full[W_AQ:]], axis=0)

    c9 = jnp.concatenate([c_all.reshape(NDEV, D), jnp.pad(c_ctx[None], ((0, MODROWS - NDEV - 1), (0, 0)))], axis=0)
    b_loc = lax.dynamic_slice(b_mod, (0, me * mcols), (1, mcols))
    mod_all, = _exchange([_mod_fwd(c9, w_mod[0], b_loc)], name="gather_mod", scatter=False)
    mod_lat = lax.dynamic_index_in_dim(mod_all, me, axis=1, keepdims=False).reshape(6, D)
    mod_ctx = mod_all[:, NDEV, :].reshape(6, D)

    small = {"q_norm_w": q_norm_w, "k_norm_w": k_norm_w, "gdn_norm_w": gdn_norm_w, "a_log": a_log, "dt_bias": dt_bias,
             "conv_qkv_w": _by_columns(conv_g), "ffn_conv_w": _by_columns(ffnw_g), "ffn_conv_b": ffn_conv_b,
             "final_norm_w": final_norm_w[None]}
    loss_me, grad_x, (pending_in, own_in), recv, dmod_lat, dmod_ctx, gs = _local_step(
        x[0], ctx[0], loss_target[0], mod_lat, mod_ctx, w_in_pad, shards, small)

    moments = {"w_in": (m_w_in, v_w_in), "w_pa": (m_w_pa, v_w_pa), "w_pd": (m_w_pd, v_w_pd),
               "w_out": (m_w_out, v_w_out), "w_up": (m_w_up, v_w_up), "w_down": (m_w_down, v_w_down)}
    res = {}
    def finish(n, outs):
        return tuple((t.T if n in transposed else t)[None] for t in outs)

    def moment(t, n):
        return t[0].T if n in transposed else t[0]

    for n in recv:
        res[n] = finish(n, _adamw_recv(big[n], recv[n], moment(moments[n][0], n), moment(moments[n][1], n),
                                       name="adamw_" + n))

    misc = jnp.concatenate([gs["q_norm_w"][0], gs["k_norm_w"][0], gs["gdn_norm_w"][0], gs["a_log"], gs["dt_bias"],
                            loss_me[None]])
    pack = jnp.concatenate([_rows_of(dmod_lat, P_CTX - P_LAT), _rows_of(dmod_ctx, P_FNW - P_CTX),
                            _rows_of(gs["final_norm_w"], P_FFNB - P_FNW), _rows_of(gs["ffn_conv_b"], P_CONV - P_FFNB),
                            _rows_of(gs["conv_qkv_w"], P_FFNW - P_CONV), _rows_of(gs["ffn_conv_w"], P_MISC - P_FFNW),
                            _rows_of(misc, P_ROWS - P_MISC)], axis=0)
    pack_all, = _exchange([pack], name="gather_pack", scatter=False)
    tot = _sum_slots(pack_all, name="sum_pack")
    dall = jnp.concatenate([pack_all[:, P_LAT:P_LAT + 6, :].reshape(NDEV, 6 * D),
                            jnp.pad(tot[P_CTX:P_CTX + 6].reshape(1, 6 * D), ((0, MODROWS - NDEV - 1), (0, 0)))], axis=0)
    dmy = lax.dynamic_slice(dall, (0, me * mcols), (MODROWS, mcols))
    sems_a, land = pending_in[:2], pending_in[3]
    *sems_b, g_in_thru, land, token_b = _scatter_start(pending_in[2], land, (D // 2, D // 2), (tot,),
                                                       name="scatter_g_in_b_start")
    g_w_mod, g_b_mod, cpart = _mod_bwd(c9, dmy, dall, w_mod[0], (token_b,))
    cparts, = _exchange([cpart], name="gather_cctx", scatter=False)
    g_c_ctx = _cctx_finish(cparts, c_ctx[None])[0]

    nconv, nffn = 3 * GH * HD, 2 * DFF
    conv_tot = tot[P_CONV:P_FFNW].reshape(-1)[:3 * nconv].reshape(3, nconv)
    ffnw_tot = tot[P_FFNW:P_MISC].reshape(-1)[:3 * nffn].reshape(3, nffn)
    mrow = tot[P_MISC]
    grads = {
        "c_ctx": g_c_ctx, "w_mod": g_w_mod[None], "b_mod": g_b_mod,
        "q_norm_w": mrow[None, 0:HD], "k_norm_w": mrow[None, HD:2 * HD], "gdn_norm_w": mrow[None, 2 * HD:3 * HD],
        "conv_qkv_w": lax.dynamic_slice(conv_tot, (0, me * (nconv // NDEV)), (3, nconv // NDEV))[None],
        "a_log": mrow[3 * HD:3 * HD + 2 * GH].reshape(1, 2, GH),
        "dt_bias": mrow[3 * HD + 2 * GH:3 * HD + 4 * GH].reshape(1, 2, GH),
        "ffn_conv_w": lax.dynamic_slice(ffnw_tot, (0, me * (nffn // NDEV)), (3, nffn // NDEV))[None],
        "ffn_conv_b": tot[P_FFNB:P_CONV].reshape(-1)[:nffn][None],
        "final_norm_w": tot[P_FNW],
    }
    loss = mrow[3 * HD + 4 * GH]
    given = {"c_ctx": (c_ctx, m_c_ctx, v_c_ctx), "w_mod": (w_mod, m_w_mod, v_w_mod), "b_mod": (b_mod, m_b_mod, v_b_mod),
             "q_norm_w": (q_norm_w, m_q_norm_w, v_q_norm_w), "k_norm_w": (k_norm_w, m_k_norm_w, v_k_norm_w),
             "conv_qkv_w": (conv_qkv_w, m_conv_qkv_w, v_conv_qkv_w), "a_log": (a_log, m_a_log, v_a_log),
             "dt_bias": (dt_bias, m_dt_bias, v_dt_bias), "gdn_norm_w": (gdn_norm_w, m_gdn_norm_w, v_gdn_norm_w),
             "ffn_conv_w": (ffn_conv_w, m_ffn_conv_w, v_ffn_conv_w), "ffn_conv_b": (ffn_conv_b, m_ffn_conv_b, v_ffn_conv_b),
             "final_norm_w": (final_norm_w, m_final_norm_w, v_final_norm_w)}
    for n, (w, m, v) in given.items():
        res[n] = (grads[n],) + _adamw(w, grads[n], m, v, name="adamw_" + n)

    g_in_thru, land = _scatter_wait(*sems_a, g_in_thru, land, (0, D // 2), [res[n][1] for n in res],
                                    name="scatter_g_in_a_wait")
    _, land = _scatter_wait(*sems_b, g_in_thru, land, (D // 2, D // 2), (), name="scatter_g_in_b_wait")
    res["w_in"] = finish("w_in", _adamw_recv(big["w_in"], land, moment(m_w_in, "w_in"), moment(v_w_in, "w_in"),
                                             name="adamw_w_in", own=own_in))

    order = ["c_ctx", "w_mod", "b_mod", "w_in", "q_norm_w", "k_norm_w", "conv_qkv_w", "a_log", "dt_bias", "gdn_norm_w",
             "w_pa", "w_pd", "w_out", "w_up", "ffn_conv_w", "ffn_conv_b", "w_down", "final_norm_w"]
    return (loss, grad_x[None], *[res[n][0] for n in order], *[res[n][1] for n in order],
            *[res[n][2] for n in order], *[res[n][3] for n in order])
```

```python
import functools
import math

import jax
import jax.numpy as jnp
from jax import lax
from jax.experimental import pallas as pl
from jax.experimental.pallas import tpu as pltpu

F32 = jnp.float32
BF16 = jnp.bfloat16
HI = lax.Precision.HIGHEST
MESH = pl.DeviceIdType.MESH

NDEV = 8
D = 1024
HD = 128
AH, AKV, GRP = 8, 2, 4
GH = 8
CH = 64
DFF = 2816
GRID_W = 64
EPS = 1e-6
ROPE_THETA = 10000.0
C_KV, C_QKV, C_BL, C_AQ, C_Z, C_GATE, C_END = 0, 512, 3584, 4096, 5120, 6144, 8192
W_BL, W_AQ, W_END = 3584, 3616, 7712
LR, B1, B2, AEPS, WD, STEP = 0.001, 0.9, 0.999, 1e-08, 0.01, 10
VMEM_BIG = 56 * 1024 * 1024
INTRA_FWD_CHUNKS = 18
INTRA_BWD_CHUNKS = 18


def _call(body, *, name, out_shape, grid=None, in_specs=None, out_specs=None, scratch=(), sem=None,
          vmem=None, aliases=None):
    params = {}
    if sem is not None:
        params["dimension_semantics"] = sem
    if vmem is not None:
        params["vmem_limit_bytes"] = vmem
    kw = {}
    if grid is not None:
        kw["grid"] = grid
    if in_specs is not None:
        kw["in_specs"] = in_specs
    if out_specs is not None:
        kw["out_specs"] = out_specs
    if aliases:
        kw["input_output_aliases"] = aliases
    return pl.pallas_call(body, name=name, out_shape=out_shape, scratch_shapes=list(scratch),
                          compiler_params=pltpu.CompilerParams(**params), **kw)


def _call_carrying(body, exch, *, name, out_shape, grid, in_specs, out_specs, scratch=(), vmem=None):
    n, nin, nout, nscr = exch.n, len(in_specs), len(out_shape), len(scratch)

    def wrapped(*refs):
        ins, cins = refs[:nin], refs[nin:nin + n]
        outs, couts = refs[nin + n:nin + n + nout], refs[nin + n + nout:nin + 2 * n + nout]
        scr, sems = refs[nin + 2 * n + nout:nin + 2 * n + nout + nscr], refs[nin + 2 * n + nout + nscr:]
        ids = [pl.program_id(i) for i in range(len(grid))]
        first = functools.reduce(jnp.logical_and, [i == 0 for i in ids])
        last = functools.reduce(jnp.logical_and, [i == g - 1 for i, g in zip(ids, grid)])

        @pl.when(first)
        def _():
            exch.start(cins, couts, sems)

        body(*ins, *outs, *scr)

        @pl.when(last)
        def _():
            exch.finish(cins, couts, sems)

    params = {"dimension_semantics": ("arbitrary",) * len(grid)}
    if vmem is not None:
        params["vmem_limit_bytes"] = vmem
    fn = pl.pallas_call(wrapped, name=name, out_shape=tuple(out_shape) + exch.out_shape, grid=grid,
                        in_specs=list(in_specs) + [HBM] * n, out_specs=tuple(out_specs) + (HBM,) * n,
                        scratch_shapes=list(scratch) + exch.scratch, compiler_params=pltpu.CompilerParams(**params))

    def run(*args):
        res = fn(*args, *exch.arrs)
        return res[:nout], list(res[nout:])

    return run


def _sds(shape, dtype=F32):
    return jax.ShapeDtypeStruct(tuple(shape), dtype)


def _dot(a, b, ca, cb):
    return lax.dot_general(a.astype(BF16), b.astype(BF16), (((ca,), (cb,)), ((), ())),
                           preferred_element_type=F32)


@jax.custom_vjp
def _nn(a, b):
    return _dot(a, b, 1, 0)


@jax.custom_vjp
def _nt(a, b):
    return _dot(a, b, 1, 1)


@jax.custom_vjp
def _tn(a, b):
    return _dot(a, b, 0, 0)


_nn.defvjp(lambda a, b: (_nn(a, b), (a, b)), lambda r, g: (_nt(g, r[1]), _tn(r[0], g)))
_nt.defvjp(lambda a, b: (_nt(a, b), (a, b)), lambda r, g: (_nn(g, r[1]), _tn(g, r[0])))
_tn.defvjp(lambda a, b: (_tn(a, b), (a, b)), lambda r, g: (_nt(r[1], g), _nn(r[0], g)))


def _hdot(a, b):
    return jnp.dot(a, b, precision=HI, preferred_element_type=F32)


def _mdot(a, b):
    return jnp.dot(a, b, precision=lax.Precision.HIGH, preferred_element_type=F32)


def _maskdot(mask, a, cm):
    hi = a.astype(BF16)
    r = a - hi.astype(F32)
    mid = r.astype(BF16)
    lo = (r - mid.astype(F32)).astype(BF16)
    mb = mask.astype(BF16)
    dims = (((cm,), (0,)), ((), ()))
    return (lax.dot_general(mb, hi, dims, preferred_element_type=F32)
            + lax.dot_general(mb, mid, dims, preferred_element_type=F32)
            + lax.dot_general(mb, lo, dims, preferred_element_type=F32))


@jax.custom_vjp
def _mask_nn(mask, a):
    return _maskdot(mask, a, 1)


_mask_nn.defvjp(lambda mask, a: (_maskdot(mask, a, 1), mask),
                lambda mask, g: (jnp.zeros_like(mask), _maskdot(mask, g, 0)))


@jax.custom_vjp
def _saved_inverse(lmat, x):
    return x


def _saved_inverse_bwd(x, g):
    t = lax.dot_general(x, g, (((0,), (0,)), ((), ())), precision=lax.Precision.HIGH, preferred_element_type=F32)
    dl = lax.dot_general(t, x, (((1,), (1,)), ((), ())), precision=lax.Precision.HIGH, preferred_element_type=F32)
    return -dl, jnp.zeros_like(x)


_saved_inverse.defvjp(lambda lmat, x: (x, x), _saved_inverse_bwd)


def _row_ids(shape):
    return lax.broadcasted_iota(jnp.int32, shape, 0)


def _shift_rows(x, down, bounds):
    n = x.shape[0]
    rows = _row_ids(x.shape)
    y = pltpu.roll(x, 1 if down else n - 1, 0)
    edge = functools.reduce(jnp.logical_or, [rows == (s if down else e - 1) for s, e in bounds])
    return jnp.where(edge, 0.0, y)


def _make_shift(bounds):
    @jax.custom_vjp
    def down(x):
        return _shift_rows(x, True, bounds)

    @jax.custom_vjp
    def up(x):
        return _shift_rows(x, False, bounds)

    down.defvjp(lambda x: (down(x), None), lambda _, g: (up(g),))
    up.defvjp(lambda x: (up(x), None), lambda _, g: (down(g),))
    return down, up


@jax.custom_vjp
def _swap32(x):
    lane = lax.broadcasted_iota(jnp.int32, x.shape, x.ndim - 1)
    return jnp.where((lane % 64) < 32, pltpu.roll(x, HD - 32, x.ndim - 1), pltpu.roll(x, 32, x.ndim - 1))


_swap32.defvjp(lambda x: (_swap32(x), None), lambda _, g: (_swap32(g),))


def _rms(x):
    return x * lax.rsqrt(jnp.mean(x * x, axis=-1, keepdims=True) + EPS)


def _silu(x):
    return x * jax.nn.sigmoid(x)


def _mm(a, b, *, name, M, N, K, ta=False, tb=False, out_dtype=F32, bm=None, bn=None, bk=None,
        a_off=(0, 0), b_off=(0, 0), after=()):
    bm, bn, bk = bm or M, bn or N, bk or K
    assert M % bm == 0 and N % bn == 0 and K % bk == 0, (name, M, N, K, bm, bn, bk)
    nk = K // bk
    ca, cb = (0 if ta else 1), (1 if tb else 0)
    na = len(after)

    def body(a_ref, b_ref, *rest):
        o_ref, acc = rest[na], rest[na + 1:]
        r = _dot(a_ref[...], b_ref[...], ca, cb)
        if nk == 1:
            o_ref[...] = r.astype(out_dtype)
        else:
            acc_ref, = acc
            k = pl.program_id(2)

            @pl.when(k == 0)
            def _():
                acc_ref[...] = r

            @pl.when(k > 0)
            def _():
                acc_ref[...] += r

            @pl.when(k == nk - 1)
            def _():
                o_ref[...] = acc_ref[...].astype(out_dtype)

    def blk(off, bshape):
        assert off[0] % bshape[0] == 0 and off[1] % bshape[1] == 0, (name, off, bshape)
        return off[0] // bshape[0], off[1] // bshape[1]

    if ta:
        ao = blk(a_off, (bk, bm))
        a_spec = pl.BlockSpec((bk, bm), lambda i, j, k: (k + ao[0], i + ao[1]))
    else:
        ao = blk(a_off, (bm, bk))
        a_spec = pl.BlockSpec((bm, bk), lambda i, j, k: (i + ao[0], k + ao[1]))
    if tb:
        bo = blk(b_off, (bn, bk))
        b_spec = pl.BlockSpec((bn, bk), lambda i, j, k: (j + bo[0], k + bo[1]))
    else:
        bo = blk(b_off, (bk, bn))
        b_spec = pl.BlockSpec((bk, bn), lambda i, j, k: (k + bo[0], j + bo[1]))
    return _call(body, name=name, out_shape=_sds((M, N), out_dtype), grid=(M // bm, N // bn, nk),
                 in_specs=[a_spec, b_spec] + [pl.BlockSpec(memory_space=pl.ANY)] * na,
                 out_specs=pl.BlockSpec((bm, bn), lambda i, j, k: (i, j)),
                 scratch=[pltpu.VMEM((bm, bn), F32)] if nk > 1 else [],
                 sem=("parallel", "parallel", "arbitrary"), vmem=VMEM_BIG)(a, b, *after)


def _normmod_fn(x, sh, sc):
    return _rms(x) * (1.0 + sc) + sh


def _normmod_fwd(x, mod, i_sh, i_sc, *, name, br=256):
    R = x.shape[0]

    def body(x_ref, mod_ref, o_ref):
        o_ref[...] = _normmod_fn(x_ref[...], mod_ref[i_sh:i_sh + 1, :], mod_ref[i_sc:i_sc + 1, :]).astype(BF16)

    return _call(body, name=name, out_shape=_sds((R, D), BF16), grid=(R // br,),
                 in_specs=[pl.BlockSpec((br, D), lambda i: (i, 0)), pl.BlockSpec((6, D), lambda i: (0, 0))],
                 out_specs=pl.BlockSpec((br, D), lambda i: (i, 0)), sem=("parallel",))(x, mod)


def _normmod_bwd(x, mod, i_sh, i_sc, dh, dh_off, res, *, name, br=256):
    R = x.shape[0]
    ob = dh_off // br
    has_res = res is not None

    def body(x_ref, mod_ref, dh_ref, *rest):
        if has_res:
            res_ref, dx_ref, dsh_ref, dsc_ref = rest
        else:
            dx_ref, dsh_ref, dsc_ref = rest
        sh, sc = mod_ref[i_sh:i_sh + 1, :], mod_ref[i_sc:i_sc + 1, :]
        _, vjp = jax.vjp(_normmod_fn, x_ref[...], sh, sc)
        dx, dsh, dsc = vjp(dh_ref[...])
        dx_ref[...] = dx + res_ref[...] if has_res else dx

        @pl.when(pl.program_id(0) == 0)
        def _():
            dsh_ref[...] = jnp.zeros_like(dsh_ref)
            dsc_ref[...] = jnp.zeros_like(dsc_ref)

        dsh_ref[...] += dsh
        dsc_ref[...] += dsc

    row = pl.BlockSpec((br, D), lambda i: (i, 0))
    vec = pl.BlockSpec((1, D), lambda i: (0, 0))
    ins = [row, pl.BlockSpec((6, D), lambda i: (0, 0)), pl.BlockSpec((br, D), lambda i: (i + ob, 0))]
    args = [x, mod, dh]
    if has_res:
        ins.append(row)
        args.append(res)
    return _call(body, name=name, out_shape=(_sds((R, D)), _sds((1, D)), _sds((1, D))), grid=(R // br,),
                 in_specs=ins, out_specs=(row, vec, vec), sem=("arbitrary",))(*args)


def _rope(x, cos, sin):
    return x * cos + _swap32(x) * sin


def _aprep_fn(qs, ks, cos, sin, qw, kw):
    return ([_rope(_rms(q) * qw, cos, sin) for q in qs], [_rope(_rms(k) * kw, cos, sin) for k in ks])


def _aprep_fwd(proj, cos, sin, qw, kw, *, br=256):
    T = proj.shape[0]

    def body(aq_ref, kv_ref, cos_ref, sin_ref, qw_ref, kw_ref, q_ref, k_ref, v_ref):
        qs = [aq_ref[:, h * HD:(h + 1) * HD] for h in range(AH)]
        ks = [kv_ref[:, h * HD:(h + 1) * HD] for h in range(AKV)]
        qo, ko = _aprep_fn(qs, ks, cos_ref[...], sin_ref[...], qw_ref[...], kw_ref[...])
        for h in range(AH):
            q_ref[h] = qo[h].astype(BF16)
        for h in range(AKV):
            k_ref[h] = ko[h].astype(BF16)
            v_ref[h] = kv_ref[:, (AKV + h) * HD:(AKV + h + 1) * HD].astype(BF16)

    tab = pl.BlockSpec((br, HD), lambda i: (i, 0))
    vec = pl.BlockSpec((1, HD), lambda i: (0, 0))
    return _call(body, name="aprep_fwd",
                 out_shape=(_sds((AH, T, HD), BF16), _sds((AKV, T, HD), BF16), _sds((AKV, T, HD), BF16)),
                 grid=(T // br,),
                 in_specs=[pl.BlockSpec((br, AH * HD), lambda i: (i, C_AQ // (AH * HD))),
                           pl.BlockSpec((br, 2 * AKV * HD), lambda i: (i, 0)), tab, tab, vec, vec],
                 out_specs=(pl.BlockSpec((AH, br, HD), lambda i: (0, i, 0)),
                            pl.BlockSpec((AKV, br, HD), lambda i: (0, i, 0)),
                            pl.BlockSpec((AKV, br, HD), lambda i: (0, i, 0))),
                 sem=("parallel",))(proj, proj, cos, sin, qw, kw)


def _aprep_bwd(proj, cos, sin, qw, kw, dq, dk, dv, L, *, br=256):
    T = proj.shape[0]
    lb = L // br

    def body(aq_ref, kv_ref, cos_ref, sin_ref, qw_ref, kw_ref, dq_ref, dk_ref, dv_ref,
             daq_ref, dkv_ref, dqw_ref, dkw_ref):
        i = pl.program_id(0)
        qs = [aq_ref[:, h * HD:(h + 1) * HD] for h in range(AH)]
        ks = [kv_ref[:, h * HD:(h + 1) * HD] for h in range(AKV)]
        _, vjp = jax.vjp(_aprep_fn, qs, ks, cos_ref[...], sin_ref[...], qw_ref[...], kw_ref[...])
        is_lat = i >= lb
        dqs = [jnp.where(is_lat, dq_ref[h], 0.0) for h in range(AH)]
        dks = [dk_ref[h] for h in range(AKV)]
        gq, gk, _, _, gqw, gkw = vjp((dqs, dks))
        for h in range(AH):
            daq_ref[:, h * HD:(h + 1) * HD] = gq[h].astype(BF16)
        for h in range(AKV):
            dkv_ref[:, h * HD:(h + 1) * HD] = gk[h].astype(BF16)
            dkv_ref[:, (AKV + h) * HD:(AKV + h + 1) * HD] = dv_ref[h].astype(BF16)

        @pl.when(i == 0)
        def _():
            dqw_ref[...] = jnp.zeros_like(dqw_ref)
            dkw_ref[...] = jnp.zeros_like(dkw_ref)

        dqw_ref[...] += gqw
        dkw_ref[...] += gkw

    tab = pl.BlockSpec((br, HD), lambda i: (i, 0))
    vec = pl.BlockSpec((1, HD), lambda i: (0, 0))
    kvb = pl.BlockSpec((AKV, br, HD), lambda i: (0, i, 0))
    return _call(body, name="aprep_bwd",
                 out_shape=(_sds((T, AH * HD), BF16), _sds((T, 2 * AKV * HD), BF16), _sds((1, HD)), _sds((1, HD))),
                 grid=(T // br,),
                 in_specs=[pl.BlockSpec((br, AH * HD), lambda i: (i, C_AQ // (AH * HD))),
                           pl.BlockSpec((br, 2 * AKV * HD), lambda i: (i, 0)), tab, tab, vec, vec,
                           pl.BlockSpec((AH, br, HD), lambda i: (0, jnp.maximum(i - lb, 0), 0)), kvb, kvb],
                 out_specs=(pl.BlockSpec((br, AH * HD), lambda i: (i, 0)),
                            pl.BlockSpec((br, 2 * AKV * HD), lambda i: (i, 0)), vec, vec),
                 sem=("arbitrary",))(proj, proj, cos, sin, qw, kw, dq, dk, dv)


def _attn_fn(q, k, v):
    s = _nt(q, k) * (HD ** -0.5)
    m = lax.stop_gradient(jnp.max(s, axis=-1, keepdims=True))
    e = jnp.exp(s - m)
    p = e / jnp.sum(e, axis=-1, keepdims=True)
    return _nn(p, v)


def _attn_fwd(q, k, v, L, exch, *, bq=128):
    T = q.shape[1]
    N = T - L
    lb = L // bq

    def body(q_ref, k_ref, v_ref, o_ref):
        qv = q_ref[...].reshape(GRP * bq, HD).astype(F32)
        o = _attn_fn(qv, k_ref[...].astype(F32), v_ref[...].astype(F32))
        for g in range(GRP):
            o_ref[:, g * HD:(g + 1) * HD] = o[g * bq:(g + 1) * bq].astype(BF16)

    kvb = pl.BlockSpec((None, T, HD), lambda g, i: (g, 0, 0))
    (attn,), moved = _call_carrying(
        body, exch, name="attn_fwd", out_shape=(_sds((N, AH * HD), BF16),), grid=(AKV, N // bq),
        in_specs=[pl.BlockSpec((GRP, bq, HD), lambda g, i: (g, i + lb, 0)), kvb, kvb],
        out_specs=(pl.BlockSpec((bq, GRP * HD), lambda g, i: (i, g)),), vmem=VMEM_BIG)(q, k, v)
    return attn, moved


def _attn_bwd(q, k, v, do, L, *, bq=128):
    T = q.shape[1]
    N = T - L
    lb = L // bq

    def body(q_ref, k_ref, v_ref, do_ref, dq_ref, dk_ref, dv_ref):
        qv = q_ref[...].reshape(GRP * bq, HD).astype(F32)
        _, vjp = jax.vjp(_attn_fn, qv, k_ref[...].astype(F32), v_ref[...].astype(F32))
        dov = jnp.concatenate([do_ref[:, g * HD:(g + 1) * HD] for g in range(GRP)], axis=0)
        dq, dk, dv = vjp(dov)
        dq_ref[...] = dq.reshape(GRP, bq, HD)

        @pl.when(pl.program_id(1) == 0)
        def _():
            dk_ref[...] = jnp.zeros_like(dk_ref)
            dv_ref[...] = jnp.zeros_like(dv_ref)

        dk_ref[...] += dk
        dv_ref[...] += dv

    kvb = pl.BlockSpec((None, T, HD), lambda g, i: (g, 0, 0))
    return _call(body, name="attn_bwd",
                 out_shape=(_sds((AH, N, HD)), _sds((AKV, T, HD)), _sds((AKV, T, HD))), grid=(AKV, N // bq),
                 in_specs=[pl.BlockSpec((GRP, bq, HD), lambda g, i: (g, i + lb, 0)), kvb, kvb,
                           pl.BlockSpec((bq, GRP * HD), lambda g, i: (i, g))],
                 out_specs=(pl.BlockSpec((GRP, bq, HD), lambda g, i: (g, i, 0)), kvb, kvb),
                 sem=("parallel", "arbitrary"), vmem=VMEM_BIG)(q, k, v, do)


def _gprep_fn(kind, shifts, x, w):
    down, up = shifts
    y = down(x) * w[0:1, :] + x * w[1:2, :] + up(x) * w[2:3, :]
    a = _silu(y)
    if kind == 2:
        return a
    a = a * lax.rsqrt(jnp.sum(a * a, axis=-1, keepdims=True) + EPS)
    return a * (HD ** -0.5) if kind == 0 else a


def _gprep_fwd(proj, conv_w, kind, bounds):
    T = proj.shape[0]
    shifts = _make_shift(bounds)
    cb = C_QKV // HD + kind * GH

    def body(x_ref, w_ref, o_ref):
        o_ref[...] = _gprep_fn(kind, shifts, x_ref[...], w_ref[...])

    return _call(body, name=f"gprep_fwd{kind}", out_shape=_sds((GH, T, HD)), grid=(GH,),
                 in_specs=[pl.BlockSpec((T, HD), lambda h: (0, cb + h)),
                           pl.BlockSpec((3, HD), lambda h: (0, kind * GH + h))],
                 out_specs=pl.BlockSpec((None, T, HD), lambda h: (h, 0, 0)), sem=("parallel",))(proj, conv_w)


def _gprep_bwd(proj, conv_w, kind, bounds, dy):
    T = proj.shape[0]
    shifts = _make_shift(bounds)
    cb = C_QKV // HD + kind * GH

    def body(x_ref, w_ref, dy_ref, dx_ref, dw_ref):
        _, vjp = jax.vjp(functools.partial(_gprep_fn, kind, shifts), x_ref[...], w_ref[...])
        dx, dw = vjp(dy_ref[0] + dy_ref[1])
        dx_ref[...] = dx.astype(BF16)
        dw_ref[...] = dw

    return _call(body, name=f"gprep_bwd{kind}", out_shape=(_sds((T, GH * HD), BF16), _sds((3, GH * HD))), grid=(GH,),
                 in_specs=[pl.BlockSpec((T, HD), lambda h: (0, cb + h)),
                           pl.BlockSpec((3, HD), lambda h: (0, kind * GH + h)),
                           pl.BlockSpec((2, None, T, HD), lambda h: (0, h, 0, 0))],
                 out_specs=(pl.BlockSpec((T, HD), lambda h: (0, h)), pl.BlockSpec((3, HD), lambda h: (0, h))),
                 sem=("parallel",))(proj, conv_w, dy)


def _bl_fn(x, alog, dtb):
    lane = lax.broadcasted_iota(jnp.int32, x.shape, 1)
    beta = jax.nn.sigmoid(x)
    z = x + dtb
    sp = jnp.maximum(z, 0.0) + jnp.log1p(jnp.exp(-jnp.abs(z)))
    la = -jnp.exp(alog) * sp
    return jnp.where(lane < 2 * GH, beta, jnp.where(lane < 4 * GH, la, 0.0))


def _bl_fwd(proj, alog, dtb, *, br=256):
    T = proj.shape[0]

    def body(x_ref, a_ref, d_ref, o_ref):
        o_ref[...] = _bl_fn(x_ref[...], a_ref[...], d_ref[...])

    vec = pl.BlockSpec((1, HD), lambda i: (0, 0))
    return _call(body, name="bl_fwd", out_shape=_sds((T, HD)), grid=(T // br,),
                 in_specs=[pl.BlockSpec((br, HD), lambda i: (i, C_BL // HD)), vec, vec],
                 out_specs=pl.BlockSpec((br, HD), lambda i: (i, 0)), sem=("parallel",))(proj, alog, dtb)


def _bl_bwd(proj, alog, dtb, dbl, *, br=256):
    T = proj.shape[0]

    def body(x_ref, a_ref, d_ref, g_ref, dx_ref, da_ref, dd_ref):
        g = g_ref[0, 0]
        for d in range(2):
            for h in range(GH):
                if d or h:
                    g = g + g_ref[d, h]
        _, vjp = jax.vjp(_bl_fn, x_ref[...], a_ref[...], d_ref[...])
        dx, da, dd = vjp(g)
        dx_ref[...] = dx.astype(BF16)

        @pl.when(pl.program_id(0) == 0)
        def _():
            da_ref[...] = jnp.zeros_like(da_ref)
            dd_ref[...] = jnp.zeros_like(dd_ref)

        da_ref[...] += da
        dd_ref[...] += dd

    vec = pl.BlockSpec((1, HD), lambda i: (0, 0))
    return _call(body, name="bl_bwd", out_shape=(_sds((T, HD), BF16), _sds((1, HD)), _sds((1, HD))), grid=(T // br,),
                 in_specs=[pl.BlockSpec((br, HD), lambda i: (i, C_BL // HD)), vec, vec,
                           pl.BlockSpec((2, GH, br, HD), lambda i: (0, 0, i, 0))],
                 out_specs=(pl.BlockSpec((br, HD), lambda i: (i, 0)), vec, vec), sem=("arbitrary",))(proj, alog, dtb, dbl)


def _chunk_masks(d):
    ii = lax.broadcasted_iota(jnp.int32, (CH, CH), 0)
    jj = lax.broadcasted_iota(jnp.int32, (CH, CH), 1)
    eye = (ii == jj).astype(F32)
    before = jnp.where(d == 0, (jj < ii).astype(F32), (jj > ii).astype(F32))
    return before, before + eye, eye


def _intra_fn(masks, sel_b, sel_l, qs, ks, vs, bls, xs=None):
    before, ateq, eye = masks
    ones = jnp.ones((CH, CH), F32)
    inc = ateq > 0.0
    each = lambda f, *ls: [f(*t) for t in zip(*ls)]
    beta = each(lambda bl: jnp.sum(bl * sel_b, axis=-1, keepdims=True), bls)
    la = each(lambda bl: jnp.sum(bl * sel_l, axis=-1, keepdims=True), bls)
    gam = each(lambda a: _mask_nn(ateq, jnp.broadcast_to(a, (CH, HD))), la)
    gi = each(lambda a: _mask_nn(ateq, jnp.broadcast_to(a, (CH, CH))), la)
    gj = each(lambda g: _mask_nn(ones, eye * g), gi)
    kk = each(lambda k: _nt(k, k), ks)
    qk = each(_nt, qs, ks)
    dec = each(lambda a, b: jnp.where(inc, jnp.exp(jnp.where(inc, a - b, 0.0)), 0.0), gi, gj)
    lmat = each(lambda b, d, m: before * (b * d * m), beta, dec, kk)
    if xs is None:
        x = each(lambda m: eye - m, lmat)
        p2 = each(lambda m: _mdot(m, m), lmat)
        for it in range(5):
            x = each(lambda a, b: a + _mdot(a, b), x, p2)
            if it < 4:
                p2 = each(lambda m: _mdot(m, m), p2)
    else:
        x = each(_saved_inverse, lmat, xs)
    eg = each(jnp.exp, gam)
    u = each(lambda a, b, v: _mdot(a, b * v), x, beta, vs)
    w = each(lambda a, b, e, k: _mdot(a, (b * e) * k), x, beta, eg, ks)
    tot = each(lambda a: jnp.sum(a, axis=0, keepdims=True), la)
    kd = each(lambda k, t, g: k * jnp.exp(t - g), ks, tot, gam)
    gl = each(lambda t: jnp.broadcast_to(jnp.exp(t), (1, HD)), tot)
    qd = each(lambda q, e: q * e, qs, eg)
    p = each(lambda d, m: d * m, dec, qk)
    return (u, w, kd, qd, p, gl, x) if xs is None else (u, w, kd, qd, p, gl)


def _dir_head_sel(d, h):
    lane = lax.broadcasted_iota(jnp.int32, (1, HD), 1)
    return (lane == d * GH + h).astype(F32), (lane == 2 * GH + d * GH + h).astype(F32)


def _intra_specs(T, G):
    nc = T // CH
    assert nc % G == 0
    qkv = pl.BlockSpec((None, G * CH, HD), lambda d, h, c: (h, c, 0))
    bl = pl.BlockSpec((G * CH, HD), lambda d, h, c: (c, 0))
    big = pl.BlockSpec((None, None, G * CH, HD), lambda d, h, c: (d, h, c, 0))
    pm = pl.BlockSpec((None, None, G * CH, CH), lambda d, h, c: (d, h, c, 0))
    gl = pl.BlockSpec((None, None, G, 1, HD), lambda d, h, c: (d, h, c, 0, 0))
    shapes = (_sds((2, GH, T, HD)),) + (_sds((2, GH, T, HD), BF16),) * 3 + (
        _sds((2, GH, T, CH), BF16), _sds((2, GH, nc, 1, HD)), _sds((2, GH, T, CH)))
    return nc, qkv, bl, big, pm, gl, shapes


def _chunks_per_step(T, most):
    nc = T // CH
    return max(g for g in range(1, most + 1) if nc % g == 0)


def _intra_fwd(q, k, v, bl, exch):
    T = q.shape[1]
    G = _chunks_per_step(T, INTRA_FWD_CHUNKS)
    nc, qkv_s, bl_s, big, pm, gl_s, shapes = _intra_specs(T, G)

    def body(q_ref, k_ref, v_ref, bl_ref, u_ref, w_ref, kd_ref, qd_ref, p_ref, gl_ref, x_ref):
        d, h = pl.program_id(0), pl.program_id(1)
        sb, sl = _dir_head_sel(d, h)
        rows = [slice(g * CH, (g + 1) * CH) for g in range(G)]
        outs = _intra_fn(_chunk_masks(d), sb, sl, *[[r[s, :] for s in rows] for r in (q_ref, k_ref, v_ref, bl_ref)])
        for g in range(G):
            for r, o in zip((u_ref, w_ref, kd_ref, qd_ref, p_ref, x_ref), outs[:5] + outs[6:]):
                r[rows[g], :] = o[g].astype(r.dtype)
            gl_ref[g] = outs[5][g]

    return _call_carrying(body, exch, name="gdn_intra_fwd", out_shape=shapes, grid=(2, GH, nc // G),
                          in_specs=[qkv_s, qkv_s, qkv_s, bl_s], out_specs=(big, big, big, big, pm, gl_s, pm))(q, k, v, bl)


def _intra_bwd(q, k, v, bl, xinv, cts, exch):
    T = q.shape[1]
    G = _chunks_per_step(T, INTRA_BWD_CHUNKS)
    nc, qkv_s, bl_s, big, pm, gl_s, _ = _intra_specs(T, G)

    def body(q_ref, k_ref, v_ref, bl_ref, x_ref, du, dw, dkd, dqd, dp, dgl, dq_ref, dk_ref, dv_ref, dbl_ref):
        d, h = pl.program_id(0), pl.program_id(1)
        sb, sl = _dir_head_sel(d, h)
        rows = [slice(g * CH, (g + 1) * CH) for g in range(G)]
        fn = functools.partial(_intra_fn, _chunk_masks(d), sb, sl, xs=[x_ref[s, :] for s in rows])
        _, vjp = jax.vjp(fn, *[[r[s, :] for s in rows] for r in (q_ref, k_ref, v_ref, bl_ref)])
        cts = tuple([r[s, :] for s in rows] for r in (du, dw, dkd, dqd, dp)) + ([dgl[g] for g in range(G)],)
        grads = vjp(cts)
        for g in range(G):
            for r, o in zip((dq_ref, dk_ref, dv_ref, dbl_ref), grads):
                r[rows[g], :] = o[g]

    return _call_carrying(body, exch, name="gdn_intra_bwd", out_shape=(_sds((2, GH, T, HD)),) * 4,
                          grid=(2, GH, nc // G), in_specs=[qkv_s, qkv_s, qkv_s, bl_s, pm, big, big, big, big, pm, gl_s],
                          out_specs=(big,) * 4)(q, k, v, bl, xinv, *cts)


def _scan_fn(s, u, w, kd, qd, p, gl):
    each = lambda f, *ls: [f(*t) for t in zip(*ls)]
    ws = each(_nn, w, s)
    delta = each(lambda a, b: a - b, u, ws)
    kdd = each(_tn, kd, delta)
    s_new = each(lambda g, a, b: g * a + b, gl, s, kdd)
    qs = each(_nn, qd, s)
    pd = each(_nn, p, delta)
    return each(lambda a, b: a + b, qs, pd), s_new


SCAN_BLOCK = 4


def _scan_visit(t, d, nb, ncb):
    rev = jnp.where(t < ncb, ncb - 1 - t, nb - 1 - (t - ncb))
    return jnp.where(d == 0, t, rev)


def _scan_specs(T, L, back):
    tb = SCAN_BLOCK * CH
    assert T % tb == 0 and L % tb == 0
    nb, ncb = T // tb, L // tb

    def at(d, t):
        return _scan_visit(nb - 1 - t if back else t, d, nb, ncb)

    big = pl.BlockSpec((None, GH, tb, HD), lambda d, t: (d, 0, at(d, t), 0))
    pm = pl.BlockSpec((None, GH, tb, CH), lambda d, t: (d, 0, at(d, t), 0))
    gl = pl.BlockSpec((None, GH, SCAN_BLOCK, 1, HD), lambda d, t: (d, 0, at(d, t), 0, 0))
    st = pl.BlockSpec((None, GH, SCAN_BLOCK, HD, HD), lambda d, t: (d, 0, at(d, t), 0, 0))
    do = pl.BlockSpec((GH, tb, HD), lambda d, t: (0, at(d, t), 0))
    return nb, big, pm, gl, st, do


def _scan_fwd(u, w, kd, qd, p, gl, L):
    T = u.shape[2]
    nb, big, pm, gl_s, st, _ = _scan_specs(T, L, False)
    heads = range(GH)

    def body(u_ref, w_ref, kd_ref, qd_ref, p_ref, gl_ref, o_ref, st_ref, s_scr):
        d = pl.program_id(0)

        @pl.when(pl.program_id(1) == 0)
        def _():
            s_scr[...] = jnp.zeros_like(s_scr)

        s = [s_scr[h] for h in heads]
        for i in range(SCAN_BLOCK):
            c = jnp.where(d == 0, i, SCAN_BLOCK - 1 - i)
            rows = pl.ds(pl.multiple_of(c * CH, CH), CH)
            for h in heads:
                st_ref[h, c] = s[h]
            o, s = _scan_fn(s, *[[r[h, rows, :].astype(F32) for h in heads] for r in (u_ref, w_ref, kd_ref, qd_ref, p_ref)],
                            [gl_ref[h, c] for h in heads])
            for h in heads:
                o_ref[h, rows, :] = o[h]
        for h in heads:
            s_scr[h] = s[h]

    return _call(body, name="gdn_scan_fwd", out_shape=(_sds((2, GH, T, HD)), _sds((2, GH, T // CH, HD, HD))),
                 grid=(2, nb), in_specs=[big, big, big, big, pm, gl_s], out_specs=(big, st),
                 scratch=[pltpu.VMEM((GH, HD, HD), F32)], sem=("parallel", "arbitrary"))(u, w, kd, qd, p, gl)


def _scan_bwd(u, w, kd, qd, p, gl, states, do, L, exch):
    T = u.shape[2]
    nb, big, pm, gl_s, st, do_s = _scan_specs(T, L, True)
    heads = range(GH)

    def body(u_ref, w_ref, kd_ref, qd_ref, p_ref, gl_ref, st_ref, do_ref,
             du_ref, dw_ref, dkd_ref, dqd_ref, dp_ref, dgl_ref, ds_scr):
        d = pl.program_id(0)

        @pl.when(pl.program_id(1) == 0)
        def _():
            ds_scr[...] = jnp.zeros_like(ds_scr)

        ds = [ds_scr[h] for h in heads]
        for i in range(SCAN_BLOCK):
            c = jnp.where(d == 0, SCAN_BLOCK - 1 - i, i)
            rows = pl.ds(pl.multiple_of(c * CH, CH), CH)
            _, vjp = jax.vjp(_scan_fn, [st_ref[h, c] for h in heads],
                             *[[r[h, rows, :].astype(F32) for h in heads] for r in (u_ref, w_ref, kd_ref, qd_ref, p_ref)],
                             [gl_ref[h, c] for h in heads])
            ds, gu, gw, gkd, gqd, gp, ggl = vjp(([do_ref[h, rows, :] for h in heads], ds))
            for h in heads:
                du_ref[h, rows, :] = gu[h]
                dw_ref[h, rows, :] = gw[h]
                dkd_ref[h, rows, :] = gkd[h]
                dqd_ref[h, rows, :] = gqd[h]
                dp_ref[h, rows, :] = gp[h]
                dgl_ref[h, c] = ggl[h]
        for h in heads:
            ds_scr[h] = ds[h]

    return _call_carrying(
        body, exch, name="gdn_scan_bwd",
        out_shape=(_sds((2, GH, T, HD)),) * 4 + (_sds((2, GH, T, CH)), _sds((2, GH, T // CH, 1, HD))),
        grid=(2, nb), in_specs=[big, big, big, big, pm, gl_s, st, do_s], out_specs=(big, big, big, big, pm, gl_s),
        scratch=[pltpu.VMEM((GH, HD, HD), F32)])(u, w, kd, qd, p, gl, states, do)


def _gout_fn(o0, o1, z, gw):
    return _rms(o0 + o1) * gw * _silu(z)


def _gout_fwd(o, proj, gw, L):
    T = o.shape[2]
    N = T - L
    ob = pl.BlockSpec((2, None, T, HD), lambda h: (0, h, 0, 0))

    def body(o_ref, z_ref, gw_ref, y_ref):
        y_ref[...] = _gout_fn(o_ref[0, L:, :], o_ref[1, L:, :], z_ref[L:, :], gw_ref[...]).astype(BF16)

    return _call(body, name="gout_fwd", out_shape=_sds((N, GH * HD), BF16), grid=(GH,),
                 in_specs=[ob, pl.BlockSpec((T, HD), lambda h: (0, C_Z // HD + h)), pl.BlockSpec((1, HD), lambda h: (0, 0))],
                 out_specs=pl.BlockSpec((N, HD), lambda h: (0, h)), sem=("parallel",))(o, proj, gw)


def _gout_bwd(o, proj, gw, dy, L):
    T = o.shape[2]
    N = T - L
    ob = pl.BlockSpec((2, None, T, HD), lambda h: (0, h, 0, 0))

    def body(o_ref, z_ref, gw_ref, dy_ref, do_ref, dz_ref, dgw_ref):
        _, vjp = jax.vjp(_gout_fn, o_ref[0, L:, :], o_ref[1, L:, :], z_ref[L:, :], gw_ref[...])
        g0, _, gz, ggw = vjp(dy_ref[...])
        do_ref[:L, :] = jnp.zeros((L, HD), F32)
        do_ref[L:, :] = g0
        dz_ref[:L, :] = jnp.zeros((L, HD), BF16)
        dz_ref[L:, :] = gz.astype(BF16)

        @pl.when(pl.program_id(0) == 0)
        def _():
            dgw_ref[...] = jnp.zeros_like(dgw_ref)

        dgw_ref[...] += ggw

    return _call(body, name="gout_bwd", out_shape=(_sds((GH, T, HD)), _sds((T, GH * HD), BF16), _sds((1, HD))),
                 grid=(GH,),
                 in_specs=[ob, pl.BlockSpec((T, HD), lambda h: (0, C_Z // HD + h)), pl.BlockSpec((1, HD), lambda h: (0, 0)),
                           pl.BlockSpec((N, HD), lambda h: (0, h))],
                 out_specs=(pl.BlockSpec((None, T, HD), lambda h: (h, 0, 0)), pl.BlockSpec((T, HD), lambda h: (0, h)),
                            pl.BlockSpec((1, HD), lambda h: (0, 0))),
                 sem=("arbitrary",))(o, proj, gw, dy)


def _merge_fn(pa, pd, ga, gd):
    return jax.nn.sigmoid(ga) * pa + jax.nn.sigmoid(gd) * pd


def _merge_fwd(pa, pd, proj, L, *, br=256):
    N = pa.shape[0]
    lb = L // br
    row = pl.BlockSpec((br, D), lambda i: (i, 0))

    def body(pa_ref, pd_ref, ga_ref, gd_ref, y_ref):
        y_ref[...] = _merge_fn(pa_ref[...], pd_ref[...], ga_ref[...], gd_ref[...]).astype(BF16)

    return _call(body, name="merge_fwd", out_shape=_sds((N, D), BF16), grid=(N // br,),
                 in_specs=[row, row, pl.BlockSpec((br, D), lambda i: (i + lb, C_GATE // D)),
                           pl.BlockSpec((br, D), lambda i: (i + lb, C_GATE // D + 1))],
                 out_specs=row, sem=("parallel",))(pa, pd, proj, proj)


def _merge_bwd(pa, pd, proj, dy, L, *, br=256):
    N = pa.shape[0]
    T = N + L
    lb = L // br
    lrow = pl.BlockSpec((br, D), lambda i: (jnp.maximum(i - lb, 0), 0))

    def body(pa_ref, pd_ref, ga_ref, gd_ref, dy_ref, dpa_ref, dpd_ref, dg_ref):
        lat = pl.program_id(0) >= lb
        _, vjp = jax.vjp(_merge_fn, pa_ref[...], pd_ref[...], ga_ref[...], gd_ref[...])
        gpa, gpd, gga, ggd = vjp(dy_ref[...])
        dpa_ref[...] = gpa.astype(BF16)
        dpd_ref[...] = gpd.astype(BF16)
        dg_ref[:, :D] = jnp.where(lat, gga, 0.0).astype(BF16)
        dg_ref[:, D:] = jnp.where(lat, ggd, 0.0).astype(BF16)

    return _call(body, name="merge_bwd", out_shape=(_sds((N, D), BF16), _sds((N, D), BF16), _sds((T, 2 * D), BF16)),
                 grid=(T // br,),
                 in_specs=[lrow, lrow, pl.BlockSpec((br, D), lambda i: (i, C_GATE // D)),
                           pl.BlockSpec((br, D), lambda i: (i, C_GATE // D + 1)), lrow],
                 out_specs=(lrow, lrow, pl.BlockSpec((br, 2 * D), lambda i: (i, 0))),
                 sem=("arbitrary",))(pa, pd, proj, proj, dy)


def _resid_fwd(x, m, mod, i_g, *, name, br=256):
    R = x.shape[0]
    row = pl.BlockSpec((br, D), lambda i: (i, 0))

    def body(x_ref, m_ref, mod_ref, o_ref):
        o_ref[...] = x_ref[...] + mod_ref[i_g:i_g + 1, :] * m_ref[...]

    return _call(body, name=name, out_shape=_sds((R, D)), grid=(R // br,),
                 in_specs=[row, row, pl.BlockSpec((6, D), lambda i: (0, 0))], out_specs=row,
                 sem=("parallel",))(x, m, mod)


def _resid_bwd(dx, m, mod, i_g, *, name, br=256):
    R = dx.shape[0]
    row = pl.BlockSpec((br, D), lambda i: (i, 0))
    vec = pl.BlockSpec((1, D), lambda i: (0, 0))

    def body(dx_ref, m_ref, mod_ref, dm_ref, dg_ref):
        dxv = dx_ref[...]
        dm_ref[...] = (dxv * mod_ref[i_g:i_g + 1, :]).astype(BF16)

        @pl.when(pl.program_id(0) == 0)
        def _():
            dg_ref[...] = jnp.zeros_like(dg_ref)

        dg_ref[...] += jnp.sum(dxv * m_ref[...], axis=0, keepdims=True)

    return _call(body, name=name, out_shape=(_sds((R, D), BF16), _sds((1, D))), grid=(R // br,),
                 in_specs=[row, row, pl.BlockSpec((6, D), lambda i: (0, 0))], out_specs=(row, vec),
                 sem=("arbitrary",))(dx, m, mod)


def _ffn_fn(shifts, ug, uv, wg, wv, bg, bv):
    down, up = shifts

    def conv(x, w, b):
        return down(x) * w[0:1, :] + x * w[1:2, :] + up(x) * w[2:3, :] + b

    return _silu(conv(ug, wg, bg)) * conv(uv, wv, bv)


def _ffn_fwd(up, cw, cb, *, bw=256):
    N = up.shape[0]
    shifts = _make_shift(((0, N),))
    nb = DFF // bw

    def body(ug, uv, wg, wv, bg, bv, a_ref):
        a_ref[...] = _ffn_fn(shifts, ug[...], uv[...], wg[...], wv[...], bg[...], bv[...]).astype(BF16)

    def col(rows, off):
        return pl.BlockSpec((rows, bw), lambda j: (0, j + off))

    return _call(body, name="ffn_fwd", out_shape=_sds((N, DFF), BF16), grid=(nb,),
                 in_specs=[col(N, 0), col(N, nb), col(3, 0), col(3, nb), col(1, 0), col(1, nb)],
                 out_specs=col(N, 0), sem=("parallel",), vmem=VMEM_BIG)(up, up, cw, cw, cb, cb)


def _ffn_bwd(up, cw, cb, da, *, bw=256):
    N = up.shape[0]
    shifts = _make_shift(((0, N),))
    nb = DFF // bw

    def body(ug, uv, wg, wv, bg, bv, da_ref, dug, duv, dwg, dwv, dbg, dbv):
        _, vjp = jax.vjp(functools.partial(_ffn_fn, shifts), ug[...], uv[...], wg[...], wv[...], bg[...], bv[...])
        g = vjp(da_ref[...])
        dug[...] = g[0].astype(BF16)
        duv[...] = g[1].astype(BF16)
        dwg[...], dwv[...], dbg[...], dbv[...] = g[2], g[3], g[4], g[5]

    def col(rows, off):
        return pl.BlockSpec((rows, bw), lambda j: (0, j + off))

    half = (_sds((N, DFF), BF16), _sds((N, DFF), BF16), _sds((3, DFF)), _sds((3, DFF)), _sds((1, DFF)), _sds((1, DFF)))
    dug, duv, dwg, dwv, dbg, dbv = _call(
        body, name="ffn_bwd", out_shape=half, grid=(nb,),
        in_specs=[col(N, 0), col(N, nb), col(3, 0), col(3, nb), col(1, 0), col(1, nb), col(N, 0)],
        out_specs=(col(N, 0), col(N, 0), col(3, 0), col(3, 0), col(1, 0), col(1, 0)),
        sem=("parallel",), vmem=VMEM_BIG)(up, up, cw, cw, cb, cb, da)
    return (jnp.concatenate([dug, duv], axis=1), jnp.concatenate([dwg, dwv], axis=1),
            jnp.concatenate([dbg, dbv], axis=1))


def _head_fn(x1, dn, g2, fw, tgt):
    y = _rms(x1 + g2 * dn) * fw
    err = y - tgt
    return 0.5 * jnp.sum(jnp.mean(err * err, axis=-1))


def _head(x1, dn, mod, fw, tgt, *, br=256):
    N = x1.shape[0]
    row = pl.BlockSpec((br, D), lambda i: (i, 0))
    vec = pl.BlockSpec((1, D), lambda i: (0, 0))
    one = pl.BlockSpec((1, HD), lambda i: (0, 0))

    def body(x1_ref, dn_ref, mod_ref, fw_ref, tgt_ref, loss_ref, dx_ref, ddn_ref, dg_ref, dfw_ref):
        loss, (gx, gdn, gg, gfw) = jax.value_and_grad(_head_fn, argnums=(0, 1, 2, 3))(
            x1_ref[...], dn_ref[...], mod_ref[5:6, :], fw_ref[...], tgt_ref[...])
        dx_ref[...] = gx
        ddn_ref[...] = gdn.astype(BF16)

        @pl.when(pl.program_id(0) == 0)
        def _():
            loss_ref[...] = jnp.zeros_like(loss_ref)
            dg_ref[...] = jnp.zeros_like(dg_ref)
            dfw_ref[...] = jnp.zeros_like(dfw_ref)

        loss_ref[...] += jnp.broadcast_to(loss, (1, HD))
        dg_ref[...] += gg
        dfw_ref[...] += gfw

    return _call(body, name="head", out_shape=(_sds((1, HD)), _sds((N, D)), _sds((N, D), BF16), _sds((1, D)), _sds((1, D))),
                 grid=(N // br,), in_specs=[row, row, pl.BlockSpec((6, D), lambda i: (0, 0)), vec, row],
                 out_specs=(one, row, row, vec, vec), sem=("arbitrary",))(x1, dn, mod, fw, tgt)


def _adamw(w, g, m, v, *, name):
    shape = w.shape
    cols = shape[-1]
    rows = max(1, math.prod(shape[:-1]))
    w2, g2, m2, v2 = (t.reshape(rows, cols) for t in (w, g, m, v))
    br = 256 if rows % 256 == 0 else (128 if rows % 128 == 0 else (8 if rows % 8 == 0 and rows > 64 else rows))
    if rows % 352 == 0:
        br = 352
    c1 = 1.0 - B1 ** STEP
    c2 = 1.0 - B2 ** STEP

    def body(w_ref, g_ref, m_ref, v_ref, d_ref, nm_ref, nv_ref):
        gv = g_ref[...]
        nm = B1 * m_ref[...] + (1.0 - B1) * gv
        nv = B2 * v_ref[...] + (1.0 - B2) * (gv * gv)
        d_ref[...] = -LR * ((nm / c1) / (jnp.sqrt(nv / c2) + AEPS) + WD * w_ref[...])
        nm_ref[...] = nm
        nv_ref[...] = nv

    blk = pl.BlockSpec((br, cols), lambda i: (i, 0))
    outs = _call(body, name=name, out_shape=(_sds((rows, cols)),) * 3, grid=(rows // br,),
                 in_specs=[blk] * 4, out_specs=(blk,) * 3, sem=("parallel",))(w2, g2, m2, v2)
    return tuple(t.reshape(shape) for t in outs)


def _rope_tables(N, L):
    t = jnp.arange(N)
    pos = jnp.stack([(t // GRID_W).astype(F32), (t % GRID_W).astype(F32)], axis=1)
    inv = ROPE_THETA ** (-jnp.arange(0, HD // 2, 2, dtype=F32) / (HD // 2))
    ang = pos[:, :, None] * inv[None, None, :]
    cos = jnp.broadcast_to(jnp.cos(ang)[:, :, None, :], (N, 2, 2, HD // 4)).reshape(N, HD)
    sin = jnp.broadcast_to(jnp.sin(ang)[:, :, None, :], (N, 2, 2, HD // 4))
    sin = (sin * jnp.array([-1.0, 1.0], F32)[None, None, :, None]).reshape(N, HD)
    cos = jnp.concatenate([jnp.ones((L, HD), F32), cos], axis=0)
    sin = jnp.concatenate([jnp.zeros((L, HD), F32), sin], axis=0)
    return cos, sin


def _pad_lanes(v, off=0):
    return jnp.zeros((1, HD), F32).at[0, off:off + v.shape[0]].set(v)


def _local_step(x, ctx, tgt, mod_lat, mod_ctx, w_in, shards, small):
    N, L = x.shape[0], ctx.shape[0]
    T = N + L
    bounds = ((0, L), (L, T))
    qw, kw, gw = small["q_norm_w"], small["k_norm_w"], small["gdn_norm_w"]
    conv_w, ffn_w, ffn_b, fnw = small["conv_qkv_w"], small["ffn_conv_w"], small["ffn_conv_b"], small["final_norm_w"]
    alog = _pad_lanes(small["a_log"].reshape(-1), 2 * GH)
    dtb = _pad_lanes(small["dt_bias"].reshape(-1), 2 * GH)
    cos, sin = _rope_tables(N, L)
    bt = T
    bnl = 256 if N % 1024 else 1024

    hc = _normmod_fwd(ctx, mod_ctx, 0, 1, name="normmod_ctx")
    hx = _normmod_fwd(x, mod_lat, 0, 1, name="normmod_x")
    h1 = jnp.concatenate([hc, hx], axis=0)
    proj = _mm(h1, w_in, name="mm_in", M=T, N=C_END, K=D, tb=True, bm=bt, bn=1024)
    aq, ak, av = _aprep_fwd(proj, cos, sin, qw, kw)
    attn, (up_g,) = _attn_fwd(aq, ak, av, L, _Exchange([shards["w_up"]], False))
    gq = _gprep_fwd(proj, conv_w, 0, bounds)
    gk = _gprep_fwd(proj, conv_w, 1, bounds)
    gv = _gprep_fwd(proj, conv_w, 2, bounds)
    bl = _bl_fwd(proj, alog, dtb)
    intra, (down_g, pa_g, pd_g, out_g) = _intra_fwd(
        gq, gk, gv, bl, _Exchange([shards[n] for n in ("w_down", "w_pa", "w_pd", "w_out")], False))
    w_up, w_down = up_g.reshape(2 * DFF, D), down_g.reshape(DFF, D)
    w_pa, w_pd, w_out = pa_g.reshape(D, D), pd_g.reshape(D, D), out_g.reshape(D, D)
    xinv, intra = intra[6], intra[:6]
    o, states = _scan_fwd(*intra, L)
    gdn = _gout_fwd(o, proj, gw, L)
    pa = _mm(attn, w_pa, name="mm_pa", M=N, N=D, K=D, bm=bnl)
    pd = _mm(gdn, w_pd, name="mm_pd", M=N, N=D, K=D, bm=bnl)
    y = _merge_fwd(pa, pd, proj, L)
    m = _mm(y, w_out, name="mm_out", M=N, N=D, K=D, bm=bnl)
    x1 = _resid_fwd(x, m, mod_lat, 2, name="resid1")
    h2 = _normmod_fwd(x1, mod_lat, 3, 4, name="normmod_x1")
    up = _mm(h2, w_up, name="mm_up", M=N, N=2 * DFF, K=D, tb=True, bm=bnl, bn=2 * DFF // 4)
    a = _ffn_fwd(up, ffn_w, ffn_b)
    dn = _mm(a, w_down, name="mm_down", M=N, N=D, K=DFF, bm=bnl)
    loss, dx2, ddn, dg2, dfnw = _head(x1, dn, mod_lat, fnw, tgt)

    da = _mm(ddn, w_down, name="mm_down_dx", M=N, N=DFF, K=D, tb=True, bm=bnl, bn=DFF // 2)
    g_down = _mm(a, ddn, name="mm_down_dw", M=DFF, N=D, K=N, ta=True, bm=DFF // 2, out_dtype=BF16)
    dup, d_ffn_w, d_ffn_b = _ffn_bwd(up, ffn_w, ffn_b, da)
    dh2 = _mm(dup, w_up, name="mm_up_dx", M=N, N=D, K=2 * DFF, bm=bnl, bk=2 * DFF // 4)
    g_up = _mm(dup, h2, name="mm_up_dw", M=2 * DFF, N=D, K=N, ta=True, bm=2 * DFF // 4, out_dtype=BF16)
    dx1, dsh2, dsc2 = _normmod_bwd(x1, mod_lat, 3, 4, dh2, 0, dx2, name="normmod_x1_bwd")
    dm, dg1 = _resid_bwd(dx1, m, mod_lat, 2, name="resid1_bwd")
    dy = _mm(dm, w_out, name="mm_out_dx", M=N, N=D, K=D, tb=True, bm=bnl)
    g_out = _mm(y, dm, name="mm_out_dw", M=D, N=D, K=N, ta=True, out_dtype=BF16)
    dpa, dpd, dgate = _merge_bwd(pa, pd, proj, dy, L)
    dattn = _mm(dpa, w_pa, name="mm_pa_dx", M=N, N=D, K=D, tb=True, bm=bnl)
    g_pa = _mm(attn, dpa, name="mm_pa_dw", M=D, N=D, K=N, ta=True, out_dtype=BF16)
    dgdn = _mm(dpd, w_pd, name="mm_pd_dx", M=N, N=D, K=D, tb=True, bm=bnl)
    g_pd = _mm(gdn, dpd, name="mm_pd_dw", M=D, N=D, K=N, ta=True, out_dtype=BF16)
    do, dz, dgw = _gout_bwd(o, proj, gw, dgdn, L)
    cts, recv_a = _scan_bwd(*intra, states, do, L, _Exchange(
        [g_out.reshape(NDEV, D // NDEV, D), g_down.reshape(NDEV, DFF // NDEV, D)], True))
    (dgq, dgk, dgv, dbl), recv_b = _intra_bwd(gq, gk, gv, bl, xinv, cts, _Exchange(
        [g_pa.reshape(NDEV, D // NDEV, D), g_pd.reshape(NDEV, D // NDEV, D), g_up.reshape(NDEV, 2 * DFF // NDEV, D)], True))
    recv = dict(zip(("w_out", "w_down", "w_pa", "w_pd", "w_up"), recv_a + recv_b))
    dxq, dwq = _gprep_bwd(proj, conv_w, 0, bounds, dgq)
    dxk, dwk = _gprep_bwd(proj, conv_w, 1, bounds, dgk)
    dxv, dwv = _gprep_bwd(proj, conv_w, 2, bounds, dgv)
    dxbl, dalog, ddtb = _bl_bwd(proj, alog, dtb, dbl)
    daq_h, dak_h, dav_h = _attn_bwd(aq, ak, av, dattn, L)
    daq, dkv, dqw, dkw = _aprep_bwd(proj, cos, sin, qw, kw, daq_h, dak_h, dav_h, L)
    dproj = jnp.concatenate([dkv, dxq, dxk, dxv, dxbl, jnp.zeros((T, C_AQ - C_BL - HD), BF16), daq, dz, dgate], axis=1)
    g_in = _mm(dproj, h1, name="mm_in_dw", M=C_END, N=D, K=T, ta=True, bm=1024, out_dtype=BF16)
    g_in = jnp.concatenate([g_in[:W_AQ], g_in[C_AQ:]], axis=0).reshape(NDEV, W_END // NDEV, D)
    own_in = lax.dynamic_index_in_dim(g_in, _position()[3], axis=0, keepdims=False)
    *pending, token = _scatter_start(g_in, None, (0, D // 2), (), name="scatter_g_in_a_start")
    dh1 = _mm(dproj, w_in, name="mm_in_dx", M=T, N=D, K=C_END, bm=bt, bk=1024, after=(token,))
    grad_x, dsh1, dsc1 = _normmod_bwd(x, mod_lat, 0, 1, dh1, L, dx1, name="normmod_x_bwd")
    _, dcsh1, dcsc1 = _normmod_bwd(ctx, mod_ctx, 0, 1, dh1, 0, None, name="normmod_ctx_bwd")

    z1 = jnp.zeros((1, D), F32)
    dmod_lat = jnp.concatenate([dsh1, dsc1, dg1, dsh2, dsc2, dg2], axis=0)
    dmod_ctx = jnp.concatenate([dcsh1, dcsc1, z1, z1, z1, z1], axis=0)
    gsmall = {
        "q_norm_w": dqw, "k_norm_w": dkw, "gdn_norm_w": dgw,
        "conv_qkv_w": jnp.concatenate([dwq, dwk, dwv], axis=1),
        "a_log": dalog[0, 2 * GH:4 * GH], "dt_bias": ddtb[0, 2 * GH:4 * GH],
        "ffn_conv_w": d_ffn_w, "ffn_conv_b": d_ffn_b, "final_norm_w": dfnw,
    }
    return loss[0, 0], grad_x, (pending, own_in), recv, dmod_lat, dmod_ctx, gsmall


HBM = pl.BlockSpec(memory_space=pltpu.HBM)


def _position():
    x, y, c = lax.axis_index("x"), lax.axis_index("y"), lax.axis_index("c")
    return x, y, c, 4 * x + 2 * y + c


def _peer(x, y, c, k):
    px = 1 - x if k & 4 else x
    py = 1 - y if k & 2 else y
    pc = 1 - c if k & 1 else c
    return (px, py, pc), 4 * px + 2 * py + pc


def _exchange(arrs, *, name, scatter):
    exch = _Exchange(arrs, scatter)
    n = exch.n

    def body(*refs):
        ins, outs, sems = refs[:n], refs[n:2 * n], refs[2 * n:]
        exch.start(ins, outs, sems)
        exch.finish(ins, outs, sems)

    outs = pl.pallas_call(body, name=name, out_shape=exch.out_shape, in_specs=[HBM] * n, out_specs=(HBM,) * n,
                          scratch_shapes=exch.scratch,
                          compiler_params=pltpu.CompilerParams(has_side_effects=True))(*arrs)
    return list(outs)


class _Exchange:
    def __init__(self, arrs, scatter):
        self.arrs, self.scatter, self.n = list(arrs), scatter, len(arrs)
        self.out_shape = tuple(_sds(a.shape if scatter else (NDEV,) + a.shape, a.dtype) for a in arrs)
        self.scratch = [pltpu.SemaphoreType.DMA((self.n, NDEV - 1)), pltpu.SemaphoreType.DMA((self.n, NDEV - 1)),
                        pltpu.SemaphoreType.DMA((self.n,))]

    def _copies(self, ins, outs, sems):
        send, recv, loc = sems
        x, y, c, me = _position()
        local = [pltpu.make_async_copy(ins[a].at[me] if self.scatter else ins[a], outs[a].at[me], loc.at[a])
                 for a in range(self.n)]
        remote = []
        for k in range(1, NDEV):
            peer, pid = _peer(x, y, c, k)
            for a in range(self.n):
                src = ins[a].at[pid] if self.scatter else ins[a]
                remote.append(pltpu.make_async_remote_copy(
                    src_ref=src, dst_ref=outs[a].at[me], send_sem=send.at[a, k - 1], recv_sem=recv.at[a, k - 1],
                    device_id=peer, device_id_type=MESH))
        return local, remote

    def start(self, ins, outs, sems):
        local, remote = self._copies(ins, outs, sems)
        for cp in local + remote:
            cp.start()

    def finish(self, ins, outs, sems):
        local, remote = self._copies(ins, outs, sems)
        for cp in remote:
            cp.wait()
        for cp in local:
            cp.wait()


def _gather_two_level(block, *, name):
    def body(x_ref, out_ref, send_sems, recv_sems, local_sem):
        x, y, c, _ = _position()
        me, sibling = (x, y, c), (x, y, 1 - c)
        chips = [(1 - x, y), (x, 1 - y), (1 - x, 1 - y)]

        def slot(px, py, pc):
            return out_ref.at[4 * px + 2 * py + pc]

        def copy(k, owner, to, src=None):
            return pltpu.make_async_remote_copy(
                src_ref=slot(*owner) if src is None else src, dst_ref=slot(*owner), send_sem=send_sems.at[k],
                recv_sem=recv_sems.at[k], device_id=to, device_id_type=MESH)

        mine = pltpu.make_async_copy(x_ref, slot(*me), local_sem)
        mine.start()
        first = [copy(0, me, sibling, src=x_ref)]
        first += [copy(1 + j, me, (*chip, c), src=x_ref) for j, chip in enumerate(chips)]
        for cp in first:
            cp.start()
        passed = [copy(4 + j, (*chip, c), sibling) for j, chip in enumerate(chips)]
        for j, chip in enumerate(chips):
            copy(1 + j, (*chip, c), me).wait_recv()
            passed[j].start()
        copy(0, sibling, me).wait_recv()
        for j, chip in enumerate(chips):
            copy(4 + j, (*chip, 1 - c), me).wait_recv()
        for cp in first + passed:
            cp.wait_send()
        mine.wait()

    return pl.pallas_call(
        body, name=name, out_shape=_sds((NDEV,) + block.shape, block.dtype), in_specs=[HBM], out_specs=HBM,
        scratch_shapes=[pltpu.SemaphoreType.DMA((NDEV - 1,)), pltpu.SemaphoreType.DMA((NDEV - 1,)),
                        pltpu.SemaphoreType.DMA],
        compiler_params=pltpu.CompilerParams(has_side_effects=True))(block)


SEM = pl.BlockSpec(memory_space=pltpu.SEMAPHORE)


def _scatter_copies(src_ref, land_ref, send_sems, recv_sems, cols):
    x, y, c, me = _position()
    span = (slice(None), pl.ds(*cols))
    copies = []
    for k in range(1, NDEV):
        peer, pid = _peer(x, y, c, k)
        copies.append(pltpu.make_async_remote_copy(
            src_ref=src_ref.at[pid].at[span], dst_ref=land_ref.at[me].at[span], send_sem=send_sems.at[k - 1],
            recv_sem=recv_sems.at[k - 1], device_id=peer, device_id_type=MESH))
    return copies


SPLIT_EFFECT = pltpu.SideEffectType.DATAFLOW_SIDE_EFFECTING


def _scatter_start(parts, land, cols, after, *, name):
    na = len(after)
    if land is None:
        land = lax.empty(parts.shape, parts.dtype)

    def body(src_ref, land_ref, *rest):
        send_sems, recv_sems, _, _, token = rest[na:]
        for cp in _scatter_copies(src_ref, land_ref, send_sems, recv_sems, cols):
            cp.start()
        token[...] = jnp.zeros_like(token)

    return pl.pallas_call(
        body, name=name,
        out_shape=(pltpu.SemaphoreType.DMA((NDEV - 1,)), pltpu.SemaphoreType.DMA((NDEV - 1,)),
                   pltpu.HBM(parts.shape, parts.dtype), pltpu.HBM(parts.shape, parts.dtype), _sds((8, HD))),
        in_specs=(HBM, HBM) + (pl.BlockSpec(memory_space=pl.ANY),) * na,
        out_specs=(SEM, SEM, HBM, HBM, pl.BlockSpec(memory_space=pltpu.VMEM)),
        input_output_aliases={0: 2, 1: 3}, compiler_params=pltpu.CompilerParams(has_side_effects=SPLIT_EFFECT),
    )(pltpu.with_memory_space_constraint(parts, pltpu.HBM), pltpu.with_memory_space_constraint(land, pltpu.HBM), *after)


def _scatter_wait(send_sems, recv_sems, src_thru, land_thru, cols, after, *, name):
    na = len(after)

    def body(src_ref, land_ref, send_sems, recv_sems, *rest):
        for cp in _scatter_copies(src_ref, land_ref, send_sems, recv_sems, cols):
            cp.wait_send()
            cp.wait_recv()

    return pl.pallas_call(
        body, name=name,
        out_shape=(pltpu.HBM(src_thru.shape, src_thru.dtype), pltpu.HBM(land_thru.shape, land_thru.dtype)),
        in_specs=(HBM, HBM, SEM, SEM) + (pl.BlockSpec(memory_space=pl.ANY),) * na, out_specs=(HBM, HBM),
        input_output_aliases={0: 0, 1: 1}, compiler_params=pltpu.CompilerParams(has_side_effects=SPLIT_EFFECT),
    )(src_thru, land_thru, send_sems, recv_sems, *after)


def _cast_bf16(w, *, name):
    rows, cols = w.shape
    br = 128 if rows % 128 == 0 else rows

    def body(w_ref, o_ref):
        o_ref[...] = w_ref[...].astype(BF16)

    blk = pl.BlockSpec((br, cols), lambda i: (i, 0))
    return _call(body, name=name, out_shape=_sds((rows, cols), BF16), grid=(rows // br,), in_specs=[blk],
                 out_specs=blk, sem=("parallel",))(w)


def _sum_slots(a, *, name):
    _, R, C = a.shape

    def body(a_ref, o_ref):
        s = a_ref[0]
        for d in range(1, NDEV):
            s = s + a_ref[d]
        o_ref[...] = s

    return _call(body, name=name, out_shape=_sds((R, C)))(a)


MODROWS = 16


def _mod_fwd(c9, w, b):
    cols = w.shape[1]

    def body(c_ref, w_ref, b_ref, o_ref):
        o_ref[...] = _nn(_silu(c_ref[...]), w_ref[...]) + b_ref[...]

    return _call(body, name="mod_fwd", out_shape=_sds((MODROWS, cols)))(c9, w, b)


def _mod_bwd(c9, dmy, dall, w):
    cols = w.shape[1]

    def body(c_ref, dmy_ref, dall_ref, w_ref, gw_ref, gb_ref, cp_ref):
        sc = _silu(c_ref[...])
        rows = lax.broadcasted_iota(jnp.int32, (MODROWS, 1), 0)
        d = dmy_ref[...]
        d_ctx = jnp.where(rows == NDEV, d, 0.0)
        sc_ctx = jnp.where(rows == NDEV, sc, 0.0)
        outer = lax.dot_general(sc_ctx, d_ctx, (((0,), (0,)), ((), ())), precision=HI, preferred_element_type=F32)
        gw_ref[...] = _tn(jnp.where(rows < NDEV, sc, 0.0), jnp.where(rows < NDEV, d, 0.0)) + outer
        gb_ref[...] = jnp.sum(dall_ref[...], axis=0, keepdims=True)
        cp_ref[...] = jnp.sum(_nt(d_ctx, w_ref[...]), axis=0, keepdims=True)

    return _call(body, name="mod_bwd", out_shape=(_sds((D, cols)), _sds((1, 6 * D)), _sds((1, D))),
                 vmem=VMEM_BIG)(c9, dmy, dall, w)


def _cctx_finish(parts, c_ctx, after):
    VM = pl.BlockSpec(memory_space=pltpu.VMEM)

    def body(p_ref, c_ref, *rest):
        o_ref = rest[-1]
        s = p_ref[0]
        for d in range(1, NDEV):
            s = s + p_ref[d]
        _, vjp = jax.vjp(_silu, c_ref[...])
        o_ref[...] = vjp(s)[0]

    return _call(body, name="cctx_finish", out_shape=_sds((1, D)),
                 in_specs=[VM, VM] + [pl.BlockSpec(memory_space=pl.ANY)] * len(after))(parts, c_ctx, *after)


def _adamw_recv(w, recv, m, v, *, name, own=None):
    rows, cols = w.shape
    bc = 256
    c1 = 1.0 - B1 ** STEP
    c2 = 1.0 - B2 ** STEP
    has_own = own is not None

    def body(w_ref, r_ref, m_ref, v_ref, *rest):
        g_ref, d_ref, nm_ref, nv_ref = rest[-4:]
        me = _position()[3]

        def slot(d):
            return jnp.where(me == d, rest[0][...], r_ref[d]) if has_own else r_ref[d]

        gv = slot(0).astype(F32)
        for d in range(1, NDEV):
            gv = gv + slot(d).astype(F32)
        nm = B1 * m_ref[...] + (1.0 - B1) * gv
        nv = B2 * v_ref[...] + (1.0 - B2) * (gv * gv)
        g_ref[...] = gv
        d_ref[...] = -LR * ((nm / c1) / (jnp.sqrt(nv / c2) + AEPS) + WD * w_ref[...])
        nm_ref[...] = nm
        nv_ref[...] = nv

    blk = pl.BlockSpec((rows, bc), lambda j: (0, j))
    return _call(body, name=name, out_shape=(_sds((rows, cols)),) * 4, grid=(cols // bc,),
                 in_specs=[blk, pl.BlockSpec((NDEV, rows, bc), lambda j: (0, 0, j)), blk, blk] + [blk] * has_own,
                 out_specs=(blk,) * 4, sem=("parallel",), vmem=VMEM_BIG)(w, recv, m, v, *([own] if has_own else []))


P_LAT, P_CTX, P_FNW, P_FFNB, P_CONV, P_FFNW, P_MISC, P_ROWS = 0, 8, 16, 24, 32, 48, 72, 80


def _rows_of(v, nrows):
    flat = v.reshape(-1)
    return jnp.pad(flat, (0, nrows * D - flat.shape[0])).reshape(nrows, D)


def _by_columns(g):
    n, r, c = g.shape
    return jnp.transpose(g, (1, 0, 2)).reshape(r, n * c)


def kernel(x, c, ctx, c_ctx, w_mod, b_mod, w_in, q_norm_w, k_norm_w, conv_qkv_w, a_log, dt_bias, gdn_norm_w, w_pa, w_pd, w_out, w_up, ffn_conv_w, ffn_conv_b, w_down, final_norm_w, loss_target, m_c_ctx, m_w_mod, m_b_mod, m_w_in, m_q_norm_w, m_k_norm_w, m_conv_qkv_w, m_a_log, m_dt_bias, m_gdn_norm_w, m_w_pa, m_w_pd, m_w_out, m_w_up, m_ffn_conv_w, m_ffn_conv_b, m_w_down, m_final_norm_w, v_c_ctx, v_w_mod, v_b_mod, v_w_in, v_q_norm_w, v_k_norm_w, v_conv_qkv_w, v_a_log, v_dt_bias, v_gdn_norm_w, v_w_pa, v_w_pd, v_w_out, v_w_up, v_ffn_conv_w, v_ffn_conv_b, v_w_down, v_final_norm_w):
    _, _, _, me = _position()
    mcols = w_mod.shape[2]

    transposed = ("w_in", "w_up")
    big = {"w_in": w_in[0].T, "w_pa": w_pa[0], "w_pd": w_pd[0], "w_out": w_out[0], "w_up": w_up[0].T, "w_down": w_down[0]}
    names = list(big)
    shards = {n: _cast_bf16(big[n], name="cast_" + n) for n in names}
    w_in_g = _gather_two_level(shards["w_in"], name="gather_w_in")
    c_all, conv_g, ffnw_g = _exchange([c, conv_qkv_w[0], ffn_conv_w[0]], name="gather_small", scatter=False)
    w_in_full = w_in_g.reshape(W_END, D)
    w_in_pad = jnp.concatenate([w_in_full[:W_AQ], jnp.zeros((C_AQ - W_AQ, D), BF16), w_in_full[W_AQ:]], axis=0)

    c9 = jnp.concatenate([c_all.reshape(NDEV, D), jnp.pad(c_ctx[None], ((0, MODROWS - NDEV - 1), (0, 0)))], axis=0)
    b_loc = lax.dynamic_slice(b_mod, (0, me * mcols), (1, mcols))
    mod_all, = _exchange([_mod_fwd(c9, w_mod[0], b_loc)], name="gather_mod", scatter=False)
    mod_lat = lax.dynamic_index_in_dim(mod_all, me, axis=1, keepdims=False).reshape(6, D)
    mod_ctx = mod_all[:, NDEV, :].reshape(6, D)

    small = {"q_norm_w": q_norm_w, "k_norm_w": k_norm_w, "gdn_norm_w": gdn_norm_w, "a_log": a_log, "dt_bias": dt_bias,
             "conv_qkv_w": _by_columns(conv_g), "ffn_conv_w": _by_columns(ffnw_g), "ffn_conv_b": ffn_conv_b,
             "final_norm_w": final_norm_w[None]}
    loss_me, grad_x, (pending_in, own_in), recv, dmod_lat, dmod_ctx, gs = _local_step(
        x[0], ctx[0], loss_target[0], mod_lat, mod_ctx, w_in_pad, shards, small)

    moments = {"w_in": (m_w_in, v_w_in), "w_pa": (m_w_pa, v_w_pa), "w_pd": (m_w_pd, v_w_pd),
               "w_out": (m_w_out, v_w_out), "w_up": (m_w_up, v_w_up), "w_down": (m_w_down, v_w_down)}
    res = {}
    def finish(n, outs):
        return tuple((t.T if n in transposed else t)[None] for t in outs)

    def moment(t, n):
        return t[0].T if n in transposed else t[0]

    for n in recv:
        res[n] = finish(n, _adamw_recv(big[n], recv[n], moment(moments[n][0], n), moment(moments[n][1], n),
                                       name="adamw_" + n))

    misc = jnp.concatenate([gs["q_norm_w"][0], gs["k_norm_w"][0], gs["gdn_norm_w"][0], gs["a_log"], gs["dt_bias"],
                            loss_me[None]])
    pack = jnp.concatenate([_rows_of(dmod_lat, P_CTX - P_LAT), _rows_of(dmod_ctx, P_FNW - P_CTX),
                            _rows_of(gs["final_norm_w"], P_FFNB - P_FNW), _rows_of(gs["ffn_conv_b"], P_CONV - P_FFNB),
                            _rows_of(gs["conv_qkv_w"], P_FFNW - P_CONV), _rows_of(gs["ffn_conv_w"], P_MISC - P_FFNW),
                            _rows_of(misc, P_ROWS - P_MISC)], axis=0)
    pack_all, = _exchange([pack], name="gather_pack", scatter=False)
    tot = _sum_slots(pack_all, name="sum_pack")
    dall = jnp.concatenate([pack_all[:, P_LAT:P_LAT + 6, :].reshape(NDEV, 6 * D),
                            jnp.pad(tot[P_CTX:P_CTX + 6].reshape(1, 6 * D), ((0, MODROWS - NDEV - 1), (0, 0)))], axis=0)
    dmy = lax.dynamic_slice(dall, (0, me * mcols), (MODROWS, mcols))
    g_w_mod, g_b_mod, cpart = _mod_bwd(c9, dmy, dall, w_mod[0])
    cparts, = _exchange([cpart], name="gather_cctx", scatter=False)
    sems_a, land = pending_in[:2], pending_in[3]
    *sems_b, g_in_thru, land, token_b = _scatter_start(pending_in[2], land, (D // 2, D // 2), (cparts,),
                                                       name="scatter_g_in_b_start")
    g_c_ctx = _cctx_finish(cparts, c_ctx[None], (token_b,))[0]

    nconv, nffn = 3 * GH * HD, 2 * DFF
    conv_tot = tot[P_CONV:P_FFNW].reshape(-1)[:3 * nconv].reshape(3, nconv)
    ffnw_tot = tot[P_FFNW:P_MISC].reshape(-1)[:3 * nffn].reshape(3, nffn)
    mrow = tot[P_MISC]
    grads = {
        "c_ctx": g_c_ctx, "w_mod": g_w_mod[None], "b_mod": g_b_mod,
        "q_norm_w": mrow[None, 0:HD], "k_norm_w": mrow[None, HD:2 * HD], "gdn_norm_w": mrow[None, 2 * HD:3 * HD],
        "conv_qkv_w": lax.dynamic_slice(conv_tot, (0, me * (nconv // NDEV)), (3, nconv // NDEV))[None],
        "a_log": mrow[3 * HD:3 * HD + 2 * GH].reshape(1, 2, GH),
        "dt_bias": mrow[3 * HD + 2 * GH:3 * HD + 4 * GH].reshape(1, 2, GH),
        "ffn_conv_w": lax.dynamic_slice(ffnw_tot, (0, me * (nffn // NDEV)), (3, nffn // NDEV))[None],
        "ffn_conv_b": tot[P_FFNB:P_CONV].reshape(-1)[:nffn][None],
        "final_norm_w": tot[P_FNW],
    }
    loss = mrow[3 * HD + 4 * GH]
    given = {"c_ctx": (c_ctx, m_c_ctx, v_c_ctx), "w_mod": (w_mod, m_w_mod, v_w_mod), "b_mod": (b_mod, m_b_mod, v_b_mod),
             "q_norm_w": (q_norm_w, m_q_norm_w, v_q_norm_w), "k_norm_w": (k_norm_w, m_k_norm_w, v_k_norm_w),
             "conv_qkv_w": (conv_qkv_w, m_conv_qkv_w, v_conv_qkv_w), "a_log": (a_log, m_a_log, v_a_log),
             "dt_bias": (dt_bias, m_dt_bias, v_dt_bias), "gdn_norm_w": (gdn_norm_w, m_gdn_norm_w, v_gdn_norm_w),
             "ffn_conv_w": (ffn_conv_w, m_ffn_conv_w, v_ffn_conv_w), "ffn_conv_b": (ffn_conv_b, m_ffn_conv_b, v_ffn_conv_b),
             "final_norm_w": (final_norm_w, m_final_norm_w, v_final_norm_w)}
    for n, (w, m, v) in given.items():
        res[n] = (grads[n],) + _adamw(w, grads[n], m, v, name="adamw_" + n)

    g_in_thru, land = _scatter_wait(*sems_a, g_in_thru, land, (0, D // 2), [res[n][1] for n in res],
                                    name="scatter_g_in_a_wait")
    _, land = _scatter_wait(*sems_b, g_in_thru, land, (D // 2, D // 2), (), name="scatter_g_in_b_wait")
    res["w_in"] = finish("w_in", _adamw_recv(big["w_in"], land, moment(m_w_in, "w_in"), moment(v_w_in, "w_in"),
                                             name="adamw_w_in", own=own_in))

    order = ["c_ctx", "w_mod", "b_mod", "w_in", "q_norm_w", "k_norm_w", "conv_qkv_w", "a_log", "dt_bias", "gdn_norm_w",
             "w_pa", "w_pd", "w_out", "w_up", "ffn_conv_w", "ffn_conv_b", "w_down", "final_norm_w"]
    return (loss, grad_x[None], *[res[n][0] for n in order], *[res[n][1] for n in order],
            *[res[n][2] for n in order], *[res[n][3] for n in order])
```

```python
import functools
import math

import jax
import jax.numpy as jnp
from jax import lax
from jax.experimental import pallas as pl
from jax.experimental.pallas import tpu as pltpu

F32 = jnp.float32
BF16 = jnp.bfloat16
HI = lax.Precision.HIGHEST
MESH = pl.DeviceIdType.MESH

NDEV = 8
D = 1024
HD = 128
AH, AKV, GRP = 8, 2, 4
GH = 8
CH = 64
DFF = 2816
GRID_W = 64
EPS = 1e-6
ROPE_THETA = 10000.0
C_KV, C_AQ, C_QKV, C_BL, C_Z, C_GATE, C_END = 0, 512, 1536, 4608, 5120, 6144, 8192
W_QKV, W_AQ, W_Z, W_END = 512, 3616, 4640, 7712


def _pad_columns(w):
    zeros = jnp.zeros((C_Z - C_QKV - (W_AQ - W_QKV), D), w.dtype)
    return jnp.concatenate([w[:W_QKV], w[W_AQ:W_Z], w[W_QKV:W_AQ], zeros, w[W_Z:]], axis=0)


def _unpad_columns(g):
    return jnp.concatenate([g[:C_AQ], g[C_QKV:C_QKV + W_AQ - W_QKV], g[C_AQ:C_QKV], g[C_Z:]], axis=0)
LR, B1, B2, AEPS, WD, STEP = 0.001, 0.9, 0.999, 1e-08, 0.01, 10
VMEM_BIG = 56 * 1024 * 1024
INTRA_FWD_CHUNKS = 18
INTRA_BWD_CHUNKS = 18


def _call(body, *, name, out_shape, grid=None, in_specs=None, out_specs=None, scratch=(), sem=None,
          vmem=None, aliases=None):
    params = {}
    if sem is not None:
        params["dimension_semantics"] = sem
    if vmem is not None:
        params["vmem_limit_bytes"] = vmem
    kw = {}
    if grid is not None:
        kw["grid"] = grid
    if in_specs is not None:
        kw["in_specs"] = in_specs
    if out_specs is not None:
        kw["out_specs"] = out_specs
    if aliases:
        kw["input_output_aliases"] = aliases
    return pl.pallas_call(body, name=name, out_shape=out_shape, scratch_shapes=list(scratch),
                          compiler_params=pltpu.CompilerParams(**params), **kw)


def _call_carrying(body, exch, *, name, out_shape, grid, in_specs, out_specs, scratch=(), vmem=None):
    n, nin, nout, nscr = exch.n, len(in_specs), len(out_shape), len(scratch)

    def wrapped(*refs):
        ins, cins = refs[:nin], refs[nin:nin + n]
        outs, couts = refs[nin + n:nin + n + nout], refs[nin + n + nout:nin + 2 * n + nout]
        scr, sems = refs[nin + 2 * n + nout:nin + 2 * n + nout + nscr], refs[nin + 2 * n + nout + nscr:]
        ids = [pl.program_id(i) for i in range(len(grid))]
        first = functools.reduce(jnp.logical_and, [i == 0 for i in ids])
        last = functools.reduce(jnp.logical_and, [i == g - 1 for i, g in zip(ids, grid)])

        @pl.when(first)
        def _():
            exch.start(cins, couts, sems)

        body(*ins, *outs, *scr)

        @pl.when(last)
        def _():
            exch.finish(cins, couts, sems)

    params = {"dimension_semantics": ("arbitrary",) * len(grid)}
    if vmem is not None:
        params["vmem_limit_bytes"] = vmem
    fn = pl.pallas_call(wrapped, name=name, out_shape=tuple(out_shape) + exch.out_shape, grid=grid,
                        in_specs=list(in_specs) + [HBM] * n, out_specs=tuple(out_specs) + (HBM,) * n,
                        scratch_shapes=list(scratch) + exch.scratch, compiler_params=pltpu.CompilerParams(**params))

    def run(*args):
        res = fn(*args, *exch.arrs)
        return res[:nout], list(res[nout:])

    return run


def _sds(shape, dtype=F32):
    return jax.ShapeDtypeStruct(tuple(shape), dtype)


def _dot(a, b, ca, cb):
    return lax.dot_general(a.astype(BF16), b.astype(BF16), (((ca,), (cb,)), ((), ())),
                           preferred_element_type=F32)


@jax.custom_vjp
def _nn(a, b):
    return _dot(a, b, 1, 0)


@jax.custom_vjp
def _nt(a, b):
    return _dot(a, b, 1, 1)


@jax.custom_vjp
def _tn(a, b):
    return _dot(a, b, 0, 0)


_nn.defvjp(lambda a, b: (_nn(a, b), (a, b)), lambda r, g: (_nt(g, r[1]), _tn(r[0], g)))
_nt.defvjp(lambda a, b: (_nt(a, b), (a, b)), lambda r, g: (_nn(g, r[1]), _tn(g, r[0])))
_tn.defvjp(lambda a, b: (_tn(a, b), (a, b)), lambda r, g: (_nt(r[1], g), _nn(r[0], g)))


def _hdot(a, b):
    return jnp.dot(a, b, precision=HI, preferred_element_type=F32)


def _mdot(a, b):
    return jnp.dot(a, b, precision=lax.Precision.HIGH, preferred_element_type=F32)


def _maskdot(mask, a, cm):
    hi = a.astype(BF16)
    r = a - hi.astype(F32)
    mid = r.astype(BF16)
    lo = (r - mid.astype(F32)).astype(BF16)
    mb = mask.astype(BF16)
    dims = (((cm,), (0,)), ((), ()))
    return (lax.dot_general(mb, hi, dims, preferred_element_type=F32)
            + lax.dot_general(mb, mid, dims, preferred_element_type=F32)
            + lax.dot_general(mb, lo, dims, preferred_element_type=F32))


@jax.custom_vjp
def _mask_nn(mask, a):
    return _maskdot(mask, a, 1)


_mask_nn.defvjp(lambda mask, a: (_maskdot(mask, a, 1), mask),
                lambda mask, g: (jnp.zeros_like(mask), _maskdot(mask, g, 0)))


@jax.custom_vjp
def _saved_inverse(lmat, x):
    return x


def _saved_inverse_bwd(x, g):
    t = lax.dot_general(x, g, (((0,), (0,)), ((), ())), precision=lax.Precision.HIGH, preferred_element_type=F32)
    dl = lax.dot_general(t, x, (((1,), (1,)), ((), ())), precision=lax.Precision.HIGH, preferred_element_type=F32)
    return -dl, jnp.zeros_like(x)


_saved_inverse.defvjp(lambda lmat, x: (x, x), _saved_inverse_bwd)


def _row_ids(shape):
    return lax.broadcasted_iota(jnp.int32, shape, 0)


def _shift_rows(x, down, bounds):
    n = x.shape[0]
    rows = _row_ids(x.shape)
    y = pltpu.roll(x, 1 if down else n - 1, 0)
    edge = functools.reduce(jnp.logical_or, [rows == (s if down else e - 1) for s, e in bounds])
    return jnp.where(edge, 0.0, y)


def _make_shift(bounds):
    @jax.custom_vjp
    def down(x):
        return _shift_rows(x, True, bounds)

    @jax.custom_vjp
    def up(x):
        return _shift_rows(x, False, bounds)

    down.defvjp(lambda x: (down(x), None), lambda _, g: (up(g),))
    up.defvjp(lambda x: (up(x), None), lambda _, g: (down(g),))
    return down, up


@jax.custom_vjp
def _swap32(x):
    lane = lax.broadcasted_iota(jnp.int32, x.shape, x.ndim - 1)
    return jnp.where((lane % 64) < 32, pltpu.roll(x, HD - 32, x.ndim - 1), pltpu.roll(x, 32, x.ndim - 1))


_swap32.defvjp(lambda x: (_swap32(x), None), lambda _, g: (_swap32(g),))


def _rms(x):
    return x * lax.rsqrt(jnp.mean(x * x, axis=-1, keepdims=True) + EPS)


def _silu(x):
    return x * jax.nn.sigmoid(x)


def _mm(a, b, *, name, M, N, K, ta=False, tb=False, out_dtype=F32, bm=None, bn=None, bk=None,
        a_off=(0, 0), b_off=(0, 0), after=()):
    bm, bn, bk = bm or M, bn or N, bk or K
    assert M % bm == 0 and N % bn == 0 and K % bk == 0, (name, M, N, K, bm, bn, bk)
    nk = K // bk
    ca, cb = (0 if ta else 1), (1 if tb else 0)
    na = len(after)

    def body(a_ref, b_ref, *rest):
        o_ref, acc = rest[na], rest[na + 1:]
        r = _dot(a_ref[...], b_ref[...], ca, cb)
        if nk == 1:
            o_ref[...] = r.astype(out_dtype)
        else:
            acc_ref, = acc
            k = pl.program_id(2)

            @pl.when(k == 0)
            def _():
                acc_ref[...] = r

            @pl.when(k > 0)
            def _():
                acc_ref[...] += r

            @pl.when(k == nk - 1)
            def _():
                o_ref[...] = acc_ref[...].astype(out_dtype)

    def blk(off, bshape):
        assert off[0] % bshape[0] == 0 and off[1] % bshape[1] == 0, (name, off, bshape)
        return off[0] // bshape[0], off[1] // bshape[1]

    if ta:
        ao = blk(a_off, (bk, bm))
        a_spec = pl.BlockSpec((bk, bm), lambda i, j, k: (k + ao[0], i + ao[1]))
    else:
        ao = blk(a_off, (bm, bk))
        a_spec = pl.BlockSpec((bm, bk), lambda i, j, k: (i + ao[0], k + ao[1]))
    if tb:
        bo = blk(b_off, (bn, bk))
        b_spec = pl.BlockSpec((bn, bk), lambda i, j, k: (j + bo[0], k + bo[1]))
    else:
        bo = blk(b_off, (bk, bn))
        b_spec = pl.BlockSpec((bk, bn), lambda i, j, k: (k + bo[0], j + bo[1]))
    return _call(body, name=name, out_shape=_sds((M, N), out_dtype), grid=(M // bm, N // bn, nk),
                 in_specs=[a_spec, b_spec] + [pl.BlockSpec(memory_space=pl.ANY)] * na,
                 out_specs=pl.BlockSpec((bm, bn), lambda i, j, k: (i, j)),
                 scratch=[pltpu.VMEM((bm, bn), F32)] if nk > 1 else [],
                 sem=("parallel", "parallel", "arbitrary"), vmem=VMEM_BIG)(a, b, *after)


def _normmod_fn(x, sh, sc):
    return _rms(x) * (1.0 + sc) + sh


def _normmod_fwd(x, mod, i_sh, i_sc, *, name, br=256):
    R = x.shape[0]

    def body(x_ref, mod_ref, o_ref):
        o_ref[...] = _normmod_fn(x_ref[...], mod_ref[i_sh:i_sh + 1, :], mod_ref[i_sc:i_sc + 1, :]).astype(BF16)

    return _call(body, name=name, out_shape=_sds((R, D), BF16), grid=(R // br,),
                 in_specs=[pl.BlockSpec((br, D), lambda i: (i, 0)), pl.BlockSpec((6, D), lambda i: (0, 0))],
                 out_specs=pl.BlockSpec((br, D), lambda i: (i, 0)), sem=("parallel",))(x, mod)


def _normmod_bwd(x, mod, i_sh, i_sc, dh, dh_off, res, *, name, br=256):
    R = x.shape[0]
    ob = dh_off // br
    has_res = res is not None

    def body(x_ref, mod_ref, dh_ref, *rest):
        if has_res:
            res_ref, dx_ref, dsh_ref, dsc_ref = rest
        else:
            dx_ref, dsh_ref, dsc_ref = rest
        sh, sc = mod_ref[i_sh:i_sh + 1, :], mod_ref[i_sc:i_sc + 1, :]
        _, vjp = jax.vjp(_normmod_fn, x_ref[...], sh, sc)
        dx, dsh, dsc = vjp(dh_ref[...])
        dx_ref[...] = dx + res_ref[...] if has_res else dx

        @pl.when(pl.program_id(0) == 0)
        def _():
            dsh_ref[...] = jnp.zeros_like(dsh_ref)
            dsc_ref[...] = jnp.zeros_like(dsc_ref)

        dsh_ref[...] += dsh
        dsc_ref[...] += dsc

    row = pl.BlockSpec((br, D), lambda i: (i, 0))
    vec = pl.BlockSpec((1, D), lambda i: (0, 0))
    ins = [row, pl.BlockSpec((6, D), lambda i: (0, 0)), pl.BlockSpec((br, D), lambda i: (i + ob, 0))]
    args = [x, mod, dh]
    if has_res:
        ins.append(row)
        args.append(res)
    return _call(body, name=name, out_shape=(_sds((R, D)), _sds((1, D)), _sds((1, D))), grid=(R // br,),
                 in_specs=ins, out_specs=(row, vec, vec), sem=("arbitrary",))(*args)


def _rope(x, cos, sin):
    return x * cos + _swap32(x) * sin


def _aprep_fn(qs, ks, cos, sin, qw, kw):
    return ([_rope(_rms(q) * qw, cos, sin) for q in qs], [_rope(_rms(k) * kw, cos, sin) for k in ks])


def _aprep_fwd(proj, cos, sin, qw, kw, *, br=256):
    T = proj.shape[0]

    def body(x_ref, cos_ref, sin_ref, qw_ref, kw_ref, q_ref, k_ref, v_ref):
        qs = [x_ref[:, C_AQ + h * HD:C_AQ + (h + 1) * HD] for h in range(AH)]
        ks = [x_ref[:, h * HD:(h + 1) * HD] for h in range(AKV)]
        qo, ko = _aprep_fn(qs, ks, cos_ref[...], sin_ref[...], qw_ref[...], kw_ref[...])
        for h in range(AH):
            q_ref[h] = qo[h].astype(BF16)
        for h in range(AKV):
            k_ref[h] = ko[h].astype(BF16)
            v_ref[h] = x_ref[:, (AKV + h) * HD:(AKV + h + 1) * HD].astype(BF16)

    tab = pl.BlockSpec((br, HD), lambda i: (i, 0))
    vec = pl.BlockSpec((1, HD), lambda i: (0, 0))
    return _call(body, name="aprep_fwd",
                 out_shape=(_sds((AH, T, HD), BF16), _sds((AKV, T, HD), BF16), _sds((AKV, T, HD), BF16)),
                 grid=(T // br,),
                 in_specs=[pl.BlockSpec((br, C_QKV), lambda i: (i, 0)), tab, tab, vec, vec],
                 out_specs=(pl.BlockSpec((AH, br, HD), lambda i: (0, i, 0)),
                            pl.BlockSpec((AKV, br, HD), lambda i: (0, i, 0)),
                            pl.BlockSpec((AKV, br, HD), lambda i: (0, i, 0))),
                 sem=("parallel",))(proj, cos, sin, qw, kw)


def _aprep_bwd(proj, cos, sin, qw, kw, dq, dk, dv, dproj, L, *, br=256):
    T = proj.shape[0]
    lb = L // br

    def body(x_ref, cos_ref, sin_ref, qw_ref, kw_ref, dq_ref, dk_ref, dv_ref, _, dx_ref, dqw_ref, dkw_ref):
        i = pl.program_id(0)
        qs = [x_ref[:, C_AQ + h * HD:C_AQ + (h + 1) * HD] for h in range(AH)]
        ks = [x_ref[:, h * HD:(h + 1) * HD] for h in range(AKV)]
        _, vjp = jax.vjp(_aprep_fn, qs, ks, cos_ref[...], sin_ref[...], qw_ref[...], kw_ref[...])
        is_lat = i >= lb
        dqs = [jnp.where(is_lat, dq_ref[h], 0.0) for h in range(AH)]
        dks = [dk_ref[h] for h in range(AKV)]
        gq, gk, _, _, gqw, gkw = vjp((dqs, dks))
        for h in range(AH):
            dx_ref[:, C_AQ + h * HD:C_AQ + (h + 1) * HD] = gq[h].astype(BF16)
        for h in range(AKV):
            dx_ref[:, h * HD:(h + 1) * HD] = gk[h].astype(BF16)
            dx_ref[:, (AKV + h) * HD:(AKV + h + 1) * HD] = dv_ref[h].astype(BF16)

        @pl.when(i == 0)
        def _():
            dqw_ref[...] = jnp.zeros_like(dqw_ref)
            dkw_ref[...] = jnp.zeros_like(dkw_ref)

        dqw_ref[...] += gqw
        dkw_ref[...] += gkw

    tab = pl.BlockSpec((br, HD), lambda i: (i, 0))
    vec = pl.BlockSpec((1, HD), lambda i: (0, 0))
    kvb = pl.BlockSpec((AKV, br, HD), lambda i: (0, i, 0))
    blk = pl.BlockSpec((br, C_QKV), lambda i: (i, 0))
    return _call(body, name="aprep_bwd", out_shape=(_sds(dproj.shape, BF16), _sds((1, HD)), _sds((1, HD))),
                 grid=(T // br,),
                 in_specs=[blk, tab, tab, vec, vec,
                           pl.BlockSpec((AH, br, HD), lambda i: (0, jnp.maximum(i - lb, 0), 0)), kvb, kvb, ANYSPEC],
                 out_specs=(blk, vec, vec), aliases={8: 0},
                 sem=("arbitrary",))(proj, cos, sin, qw, kw, dq, dk, dv, dproj)


def _attn_fn(q, k, v):
    s = _dot(q, k, 1, 1) * (HD ** -0.5)
    m = jnp.max(s, axis=-1, keepdims=True)
    e = jnp.exp(s - m)
    l = jnp.sum(e, axis=-1, keepdims=True)
    return _dot(e / l, v, 1, 0), m + jnp.log(l)


def _attn_grad(q, k, v, o, lse, do):
    scale = HD ** -0.5
    p = jnp.exp(_dot(q, k, 1, 1) * scale - lse)
    dp = _dot(do, v, 1, 1)
    ds = p * (dp - jnp.sum(do * o, axis=-1, keepdims=True)) * scale
    return _dot(ds, k, 1, 0), _dot(ds, q, 0, 0), _dot(p, do, 0, 0)


def _attn_fwd(q, k, v, L, exch, *, bq=128):
    T = q.shape[1]
    N = T - L
    lb = L // bq

    def body(q_ref, k_ref, v_ref, o_ref, o32_ref, lse_ref):
        o, lse = _attn_fn(q_ref[...].reshape(GRP * bq, HD), k_ref[...], v_ref[...])
        for g in range(GRP):
            o_ref[:, g * HD:(g + 1) * HD] = o[g * bq:(g + 1) * bq].astype(BF16)
            o32_ref[:, g * HD:(g + 1) * HD] = o[g * bq:(g + 1) * bq]
        lse_ref[...] = jnp.broadcast_to(lse, (GRP * bq, HD)).reshape(GRP, bq, HD)

    kvb = pl.BlockSpec((None, T, HD), lambda g, i: (g, 0, 0))
    ob = pl.BlockSpec((bq, GRP * HD), lambda g, i: (i, g))
    return _call_carrying(
        body, exch, name="attn_fwd",
        out_shape=(_sds((N, AH * HD), BF16), _sds((N, AH * HD)), _sds((AH, N, HD))), grid=(AKV, N // bq),
        in_specs=[pl.BlockSpec((GRP, bq, HD), lambda g, i: (g, i + lb, 0)), kvb, kvb],
        out_specs=(ob, ob, pl.BlockSpec((GRP, bq, HD), lambda g, i: (g, i, 0))), vmem=VMEM_BIG)(q, k, v)


def _attn_bwd(q, k, v, o32, lse, do, L, *, bq=128):
    T = q.shape[1]
    N = T - L
    lb = L // bq

    def body(q_ref, k_ref, v_ref, o_ref, lse_ref, do_ref, dq_ref, dk_ref, dv_ref):
        rows = lambda r: jnp.concatenate([r[:, g * HD:(g + 1) * HD] for g in range(GRP)], axis=0)
        lse = jnp.max(lse_ref[...].reshape(GRP * bq, HD), axis=-1, keepdims=True)
        dq, dk, dv = _attn_grad(q_ref[...].reshape(GRP * bq, HD), k_ref[...], v_ref[...], rows(o_ref), lse, rows(do_ref))
        dq_ref[...] = dq.reshape(GRP, bq, HD)

        @pl.when(pl.program_id(1) == 0)
        def _():
            dk_ref[...] = jnp.zeros_like(dk_ref)
            dv_ref[...] = jnp.zeros_like(dv_ref)

        dk_ref[...] += dk
        dv_ref[...] += dv

    kvb = pl.BlockSpec((None, T, HD), lambda g, i: (g, 0, 0))
    qb = pl.BlockSpec((GRP, bq, HD), lambda g, i: (g, i + lb, 0))
    hb = pl.BlockSpec((GRP, bq, HD), lambda g, i: (g, i, 0))
    ob = pl.BlockSpec((bq, GRP * HD), lambda g, i: (i, g))
    return _call(body, name="attn_bwd",
                 out_shape=(_sds((AH, N, HD)), _sds((AKV, T, HD)), _sds((AKV, T, HD))), grid=(AKV, N // bq),
                 in_specs=[qb, kvb, kvb, ob, hb, ob], out_specs=(hb, kvb, kvb),
                 sem=("parallel", "arbitrary"), vmem=VMEM_BIG)(q, k, v, o32, lse, do)


def _gprep_fn(kind, shifts, x, w):
    down, up = shifts
    y = down(x) * w[0:1, :] + x * w[1:2, :] + up(x) * w[2:3, :]
    a = _silu(y)
    if kind == 2:
        return a
    a = a * lax.rsqrt(jnp.sum(a * a, axis=-1, keepdims=True) + EPS)
    return a * (HD ** -0.5) if kind == 0 else a


def _gprep_fwd(proj, conv_w, kind, bounds):
    T = proj.shape[0]
    shifts = _make_shift(bounds)
    cb = C_QKV // HD + kind * GH

    def body(x_ref, w_ref, o_ref):
        o_ref[...] = _gprep_fn(kind, shifts, x_ref[...], w_ref[...])

    return _call(body, name=f"gprep_fwd{kind}", out_shape=_sds((GH, T, HD)), grid=(GH,),
                 in_specs=[pl.BlockSpec((T, HD), lambda h: (0, cb + h)),
                           pl.BlockSpec((3, HD), lambda h: (0, kind * GH + h))],
                 out_specs=pl.BlockSpec((None, T, HD), lambda h: (h, 0, 0)), sem=("parallel",))(proj, conv_w)


def _gprep_bwd(proj, conv_w, kind, bounds, dy, dproj):
    T = proj.shape[0]
    shifts = _make_shift(bounds)
    cb = C_QKV // HD + kind * GH

    def body(x_ref, w_ref, dy_ref, _, dx_ref, dw_ref):
        _, vjp = jax.vjp(functools.partial(_gprep_fn, kind, shifts), x_ref[...], w_ref[...])
        dx, dw = vjp(dy_ref[0] + dy_ref[1])
        dx_ref[...] = dx.astype(BF16)
        dw_ref[...] = dw

    return _call(body, name=f"gprep_bwd{kind}", out_shape=(_sds(dproj.shape, BF16), _sds((3, GH * HD))), grid=(GH,),
                 in_specs=[pl.BlockSpec((T, HD), lambda h: (0, cb + h)),
                           pl.BlockSpec((3, HD), lambda h: (0, kind * GH + h)),
                           pl.BlockSpec((2, None, T, HD), lambda h: (0, h, 0, 0)), ANYSPEC],
                 out_specs=(pl.BlockSpec((T, HD), lambda h: (0, cb + h)), pl.BlockSpec((3, HD), lambda h: (0, h))),
                 aliases={3: 0}, sem=("parallel",))(proj, conv_w, dy, dproj)


def _bl_fn(x, alog, dtb):
    lane = lax.broadcasted_iota(jnp.int32, x.shape, 1)
    beta = jax.nn.sigmoid(x)
    z = x + dtb
    sp = jnp.maximum(z, 0.0) + jnp.log1p(jnp.exp(-jnp.abs(z)))
    la = -jnp.exp(alog) * sp
    return jnp.where(lane < 2 * GH, beta, jnp.where(lane < 4 * GH, la, 0.0))


def _bl_fwd(proj, alog, dtb, *, br=256):
    T = proj.shape[0]

    def body(x_ref, a_ref, d_ref, o_ref):
        o_ref[...] = _bl_fn(x_ref[...], a_ref[...], d_ref[...])

    vec = pl.BlockSpec((1, HD), lambda i: (0, 0))
    return _call(body, name="bl_fwd", out_shape=_sds((T, HD)), grid=(T // br,),
                 in_specs=[pl.BlockSpec((br, HD), lambda i: (i, C_BL // HD)), vec, vec],
                 out_specs=pl.BlockSpec((br, HD), lambda i: (i, 0)), sem=("parallel",))(proj, alog, dtb)


def _bl_bwd(proj, alog, dtb, dbl, dproj, *, br=256):
    T = proj.shape[0]
    wide = C_Z - C_BL

    def body(x_ref, a_ref, d_ref, g_ref, _, dx_ref, da_ref, dd_ref):
        g = g_ref[0, 0]
        for d in range(2):
            for h in range(GH):
                if d or h:
                    g = g + g_ref[d, h]
        _, vjp = jax.vjp(_bl_fn, x_ref[...], a_ref[...], d_ref[...])
        dx, da, dd = vjp(g)
        dx_ref[:, :HD] = dx.astype(BF16)
        dx_ref[:, HD:] = jnp.zeros((br, wide - HD), BF16)

        @pl.when(pl.program_id(0) == 0)
        def _():
            da_ref[...] = jnp.zeros_like(da_ref)
            dd_ref[...] = jnp.zeros_like(dd_ref)

        da_ref[...] += da
        dd_ref[...] += dd

    vec = pl.BlockSpec((1, HD), lambda i: (0, 0))
    return _call(body, name="bl_bwd", out_shape=(_sds(dproj.shape, BF16), _sds((1, HD)), _sds((1, HD))), grid=(T // br,),
                 in_specs=[pl.BlockSpec((br, HD), lambda i: (i, C_BL // HD)), vec, vec,
                           pl.BlockSpec((2, GH, br, HD), lambda i: (0, 0, i, 0)), ANYSPEC],
                 out_specs=(pl.BlockSpec((br, wide), lambda i: (i, C_BL // wide)), vec, vec), aliases={4: 0},
                 sem=("arbitrary",))(proj, alog, dtb, dbl, dproj)


def _chunk_masks(d):
    ii = lax.broadcasted_iota(jnp.int32, (CH, CH), 0)
    jj = lax.broadcasted_iota(jnp.int32, (CH, CH), 1)
    eye = (ii == jj).astype(F32)
    before = jnp.where(d == 0, (jj < ii).astype(F32), (jj > ii).astype(F32))
    return before, before + eye, eye


def _intra_fn(masks, sel_b, sel_l, qs, ks, vs, bls, xs=None):
    before, ateq, eye = masks
    ones = jnp.ones((CH, CH), F32)
    inc = ateq > 0.0
    each = lambda f, *ls: [f(*t) for t in zip(*ls)]
    beta = each(lambda bl: jnp.sum(bl * sel_b, axis=-1, keepdims=True), bls)
    la = each(lambda bl: jnp.sum(bl * sel_l, axis=-1, keepdims=True), bls)
    gam = each(lambda a: _mask_nn(ateq, jnp.broadcast_to(a, (CH, HD))), la)
    gi = each(lambda a: _mask_nn(ateq, jnp.broadcast_to(a, (CH, CH))), la)
    gj = each(lambda g: _mask_nn(ones, eye * g), gi)
    kk = each(lambda k: _nt(k, k), ks)
    qk = each(_nt, qs, ks)
    dec = each(lambda a, b: jnp.where(inc, jnp.exp(jnp.where(inc, a - b, 0.0)), 0.0), gi, gj)
    lmat = each(lambda b, d, m: before * (b * d * m), beta, dec, kk)
    if xs is None:
        x = each(lambda m: eye - m, lmat)
        p2 = each(lambda m: _mdot(m, m), lmat)
        for it in range(5):
            x = each(lambda a, b: a + _mdot(a, b), x, p2)
            if it < 4:
                p2 = each(lambda m: _mdot(m, m), p2)
    else:
        x = each(_saved_inverse, lmat, xs)
    eg = each(jnp.exp, gam)
    u = each(lambda a, b, v: _mdot(a, b * v), x, beta, vs)
    w = each(lambda a, b, e, k: _mdot(a, (b * e) * k), x, beta, eg, ks)
    tot = each(lambda a: jnp.sum(a, axis=0, keepdims=True), la)
    kd = each(lambda k, t, g: k * jnp.exp(t - g), ks, tot, gam)
    gl = each(lambda t: jnp.broadcast_to(jnp.exp(t), (1, HD)), tot)
    qd = each(lambda q, e: q * e, qs, eg)
    p = each(lambda d, m: d * m, dec, qk)
    return (u, w, kd, qd, p, gl, x) if xs is None else (u, w, kd, qd, p, gl)


def _dir_head_sel(d, h):
    lane = lax.broadcasted_iota(jnp.int32, (1, HD), 1)
    return (lane == d * GH + h).astype(F32), (lane == 2 * GH + d * GH + h).astype(F32)


def _intra_specs(T, G):
    nc = T // CH
    assert nc % G == 0
    qkv = pl.BlockSpec((None, G * CH, HD), lambda d, h, c: (h, c, 0))
    bl = pl.BlockSpec((G * CH, HD), lambda d, h, c: (c, 0))
    big = pl.BlockSpec((None, None, G * CH, HD), lambda d, h, c: (d, h, c, 0))
    pm = pl.BlockSpec((None, None, G * CH, CH), lambda d, h, c: (d, h, c, 0))
    gl = pl.BlockSpec((None, None, G, 1, HD), lambda d, h, c: (d, h, c, 0, 0))
    shapes = (_sds((2, GH, T, HD)),) + (_sds((2, GH, T, HD), BF16),) * 3 + (
        _sds((2, GH, T, CH), BF16), _sds((2, GH, nc, 1, HD)), _sds((2, GH, T, CH)))
    return nc, qkv, bl, big, pm, gl, shapes


def _chunks_per_step(T, most):
    nc = T // CH
    return max(g for g in range(1, most + 1) if nc % g == 0)


def _intra_fwd(q, k, v, bl, exch):
    T = q.shape[1]
    G = _chunks_per_step(T, INTRA_FWD_CHUNKS)
    nc, qkv_s, bl_s, big, pm, gl_s, shapes = _intra_specs(T, G)

    def body(q_ref, k_ref, v_ref, bl_ref, u_ref, w_ref, kd_ref, qd_ref, p_ref, gl_ref, x_ref):
        d, h = pl.program_id(0), pl.program_id(1)
        sb, sl = _dir_head_sel(d, h)
        rows = [slice(g * CH, (g + 1) * CH) for g in range(G)]
        outs = _intra_fn(_chunk_masks(d), sb, sl, *[[r[s, :] for s in rows] for r in (q_ref, k_ref, v_ref, bl_ref)])
        for g in range(G):
            for r, o in zip((u_ref, w_ref, kd_ref, qd_ref, p_ref, x_ref), outs[:5] + outs[6:]):
                r[rows[g], :] = o[g].astype(r.dtype)
            gl_ref[g] = outs[5][g]

    return _call_carrying(body, exch, name="gdn_intra_fwd", out_shape=shapes, grid=(2, GH, nc // G),
                          in_specs=[qkv_s, qkv_s, qkv_s, bl_s], out_specs=(big, big, big, big, pm, gl_s, pm))(q, k, v, bl)


def _intra_bwd(q, k, v, bl, xinv, cts, exch):
    T = q.shape[1]
    G = _chunks_per_step(T, INTRA_BWD_CHUNKS)
    nc, qkv_s, bl_s, big, pm, gl_s, _ = _intra_specs(T, G)

    def body(q_ref, k_ref, v_ref, bl_ref, x_ref, du, dw, dkd, dqd, dp, dgl, dq_ref, dk_ref, dv_ref, dbl_ref):
        d, h = pl.program_id(0), pl.program_id(1)
        sb, sl = _dir_head_sel(d, h)
        rows = [slice(g * CH, (g + 1) * CH) for g in range(G)]
        fn = functools.partial(_intra_fn, _chunk_masks(d), sb, sl, xs=[x_ref[s, :] for s in rows])
        _, vjp = jax.vjp(fn, *[[r[s, :] for s in rows] for r in (q_ref, k_ref, v_ref, bl_ref)])
        cts = tuple([r[s, :] for s in rows] for r in (du, dw, dkd, dqd, dp)) + ([dgl[g] for g in range(G)],)
        grads = vjp(cts)
        for g in range(G):
            for r, o in zip((dq_ref, dk_ref, dv_ref, dbl_ref), grads):
                r[rows[g], :] = o[g]

    return _call_carrying(body, exch, name="gdn_intra_bwd", out_shape=(_sds((2, GH, T, HD)),) * 4,
                          grid=(2, GH, nc // G), in_specs=[qkv_s, qkv_s, qkv_s, bl_s, pm, big, big, big, big, pm, gl_s],
                          out_specs=(big,) * 4)(q, k, v, bl, xinv, *cts)


def _scan_fn(s, u, w, kd, qd, p, gl):
    each = lambda f, *ls: [f(*t) for t in zip(*ls)]
    ws = each(_nn, w, s)
    delta = each(lambda a, b: a - b, u, ws)
    kdd = each(_tn, kd, delta)
    s_new = each(lambda g, a, b: g * a + b, gl, s, kdd)
    qs = each(_nn, qd, s)
    pd = each(_nn, p, delta)
    return each(lambda a, b: a + b, qs, pd), s_new


SCAN_BLOCK = 4


def _scan_visit(t, d, nb, ncb):
    rev = jnp.where(t < ncb, ncb - 1 - t, nb - 1 - (t - ncb))
    return jnp.where(d == 0, t, rev)


def _scan_specs(T, L, back):
    tb = SCAN_BLOCK * CH
    assert T % tb == 0 and L % tb == 0
    nb, ncb = T // tb, L // tb

    def at(d, t):
        return _scan_visit(nb - 1 - t if back else t, d, nb, ncb)

    big = pl.BlockSpec((None, GH, tb, HD), lambda d, t: (d, 0, at(d, t), 0))
    pm = pl.BlockSpec((None, GH, tb, CH), lambda d, t: (d, 0, at(d, t), 0))
    gl = pl.BlockSpec((None, GH, SCAN_BLOCK, 1, HD), lambda d, t: (d, 0, at(d, t), 0, 0))
    st = pl.BlockSpec((None, GH, SCAN_BLOCK, HD, HD), lambda d, t: (d, 0, at(d, t), 0, 0))
    do = pl.BlockSpec((GH, tb, HD), lambda d, t: (0, at(d, t), 0))
    return nb, big, pm, gl, st, do


def _scan_fwd(u, w, kd, qd, p, gl, L):
    T = u.shape[2]
    nb, big, pm, gl_s, st, _ = _scan_specs(T, L, False)
    heads = range(GH)

    def body(u_ref, w_ref, kd_ref, qd_ref, p_ref, gl_ref, o_ref, st_ref, s_scr):
        d = pl.program_id(0)

        @pl.when(pl.program_id(1) == 0)
        def _():
            s_scr[...] = jnp.zeros_like(s_scr)

        s = [s_scr[h] for h in heads]
        for i in range(SCAN_BLOCK):
            c = jnp.where(d == 0, i, SCAN_BLOCK - 1 - i)
            rows = pl.ds(pl.multiple_of(c * CH, CH), CH)
            for h in heads:
                st_ref[h, c] = s[h]
            o, s = _scan_fn(s, *[[r[h, rows, :].astype(F32) for h in heads] for r in (u_ref, w_ref, kd_ref, qd_ref, p_ref)],
                            [gl_ref[h, c] for h in heads])
            for h in heads:
                o_ref[h, rows, :] = o[h]
        for h in heads:
            s_scr[h] = s[h]

    return _call(body, name="gdn_scan_fwd", out_shape=(_sds((2, GH, T, HD)), _sds((2, GH, T // CH, HD, HD))),
                 grid=(2, nb), in_specs=[big, big, big, big, pm, gl_s], out_specs=(big, st),
                 scratch=[pltpu.VMEM((GH, HD, HD), F32)], sem=("parallel", "arbitrary"))(u, w, kd, qd, p, gl)


def _scan_bwd(u, w, kd, qd, p, gl, states, do, L, exch):
    T = u.shape[2]
    nb, big, pm, gl_s, st, do_s = _scan_specs(T, L, True)
    heads = range(GH)

    def body(u_ref, w_ref, kd_ref, qd_ref, p_ref, gl_ref, st_ref, do_ref,
             du_ref, dw_ref, dkd_ref, dqd_ref, dp_ref, dgl_ref, ds_scr):
        d = pl.program_id(0)

        @pl.when(pl.program_id(1) == 0)
        def _():
            ds_scr[...] = jnp.zeros_like(ds_scr)

        ds = [ds_scr[h] for h in heads]
        for i in range(SCAN_BLOCK):
            c = jnp.where(d == 0, SCAN_BLOCK - 1 - i, i)
            rows = pl.ds(pl.multiple_of(c * CH, CH), CH)
            _, vjp = jax.vjp(_scan_fn, [st_ref[h, c] for h in heads],
                             *[[r[h, rows, :].astype(F32) for h in heads] for r in (u_ref, w_ref, kd_ref, qd_ref, p_ref)],
                             [gl_ref[h, c] for h in heads])
            ds, gu, gw, gkd, gqd, gp, ggl = vjp(([do_ref[h, rows, :] for h in heads], ds))
            for h in heads:
                du_ref[h, rows, :] = gu[h]
                dw_ref[h, rows, :] = gw[h]
                dkd_ref[h, rows, :] = gkd[h]
                dqd_ref[h, rows, :] = gqd[h]
                dp_ref[h, rows, :] = gp[h]
                dgl_ref[h, c] = ggl[h]
        for h in heads:
            ds_scr[h] = ds[h]

    return _call_carrying(
        body, exch, name="gdn_scan_bwd",
        out_shape=(_sds((2, GH, T, HD)),) * 4 + (_sds((2, GH, T, CH)), _sds((2, GH, T // CH, 1, HD))),
        grid=(2, nb), in_specs=[big, big, big, big, pm, gl_s, st, do_s], out_specs=(big, big, big, big, pm, gl_s),
        scratch=[pltpu.VMEM((GH, HD, HD), F32)])(u, w, kd, qd, p, gl, states, do)


def _gout_fn(o0, o1, z, gw):
    return _rms(o0 + o1) * gw * _silu(z)


def _gout_fwd(o, proj, gw, L):
    T = o.shape[2]
    N = T - L
    ob = pl.BlockSpec((2, None, T, HD), lambda h: (0, h, 0, 0))

    def body(o_ref, z_ref, gw_ref, y_ref):
        y_ref[...] = _gout_fn(o_ref[0, L:, :], o_ref[1, L:, :], z_ref[L:, :], gw_ref[...]).astype(BF16)

    return _call(body, name="gout_fwd", out_shape=_sds((N, GH * HD), BF16), grid=(GH,),
                 in_specs=[ob, pl.BlockSpec((T, HD), lambda h: (0, C_Z // HD + h)), pl.BlockSpec((1, HD), lambda h: (0, 0))],
                 out_specs=pl.BlockSpec((N, HD), lambda h: (0, h)), sem=("parallel",))(o, proj, gw)


def _gout_bwd(o, proj, gw, dy, dproj, L):
    T = o.shape[2]
    N = T - L
    ob = pl.BlockSpec((2, None, T, HD), lambda h: (0, h, 0, 0))

    def body(o_ref, z_ref, gw_ref, dy_ref, _, do_ref, dz_ref, dgw_ref):
        _, vjp = jax.vjp(_gout_fn, o_ref[0, L:, :], o_ref[1, L:, :], z_ref[L:, :], gw_ref[...])
        g0, _, gz, ggw = vjp(dy_ref[...])
        do_ref[:L, :] = jnp.zeros((L, HD), F32)
        do_ref[L:, :] = g0
        dz_ref[:L, :] = jnp.zeros((L, HD), BF16)
        dz_ref[L:, :] = gz.astype(BF16)

        @pl.when(pl.program_id(0) == 0)
        def _():
            dgw_ref[...] = jnp.zeros_like(dgw_ref)

        dgw_ref[...] += ggw

    zb = pl.BlockSpec((T, HD), lambda h: (0, C_Z // HD + h))
    return _call(body, name="gout_bwd", out_shape=(_sds((GH, T, HD)), _sds(dproj.shape, BF16), _sds((1, HD))),
                 grid=(GH,),
                 in_specs=[ob, zb, pl.BlockSpec((1, HD), lambda h: (0, 0)), pl.BlockSpec((N, HD), lambda h: (0, h)), ANYSPEC],
                 out_specs=(pl.BlockSpec((None, T, HD), lambda h: (h, 0, 0)), zb, pl.BlockSpec((1, HD), lambda h: (0, 0))),
                 aliases={4: 1}, sem=("arbitrary",))(o, proj, gw, dy, dproj)


def _merge_fn(pa, pd, ga, gd):
    return jax.nn.sigmoid(ga) * pa + jax.nn.sigmoid(gd) * pd


def _merge_fwd(pa, pd, proj, L, *, br=256):
    N = pa.shape[0]
    lb = L // br
    row = pl.BlockSpec((br, D), lambda i: (i, 0))

    def body(pa_ref, pd_ref, ga_ref, gd_ref, y_ref):
        y_ref[...] = _merge_fn(pa_ref[...], pd_ref[...], ga_ref[...], gd_ref[...]).astype(BF16)

    return _call(body, name="merge_fwd", out_shape=_sds((N, D), BF16), grid=(N // br,),
                 in_specs=[row, row, pl.BlockSpec((br, D), lambda i: (i + lb, C_GATE // D)),
                           pl.BlockSpec((br, D), lambda i: (i + lb, C_GATE // D + 1))],
                 out_specs=row, sem=("parallel",))(pa, pd, proj, proj)


def _merge_bwd(pa, pd, proj, dy, L, *, br=256):
    N = pa.shape[0]
    T = N + L
    lb = L // br
    lrow = pl.BlockSpec((br, D), lambda i: (jnp.maximum(i - lb, 0), 0))

    def body(pa_ref, pd_ref, ga_ref, gd_ref, dy_ref, dpa_ref, dpd_ref, dg_ref):
        lat = pl.program_id(0) >= lb
        _, vjp = jax.vjp(_merge_fn, pa_ref[...], pd_ref[...], ga_ref[...], gd_ref[...])
        gpa, gpd, gga, ggd = vjp(dy_ref[...])
        dpa_ref[...] = gpa.astype(BF16)
        dpd_ref[...] = gpd.astype(BF16)
        dg_ref[:, :D] = jnp.where(lat, gga, 0.0).astype(BF16)
        dg_ref[:, D:] = jnp.where(lat, ggd, 0.0).astype(BF16)

    return _call(body, name="merge_bwd", out_shape=(_sds((N, D), BF16), _sds((N, D), BF16), _sds((T, C_END), BF16)),
                 grid=(T // br,),
                 in_specs=[lrow, lrow, pl.BlockSpec((br, D), lambda i: (i, C_GATE // D)),
                           pl.BlockSpec((br, D), lambda i: (i, C_GATE // D + 1)), lrow],
                 out_specs=(lrow, lrow, pl.BlockSpec((br, 2 * D), lambda i: (i, C_GATE // (2 * D)))),
                 sem=("arbitrary",))(pa, pd, proj, proj, dy)


def _resid_fwd(x, m, mod, i_g, *, name, br=256):
    R = x.shape[0]
    row = pl.BlockSpec((br, D), lambda i: (i, 0))

    def body(x_ref, m_ref, mod_ref, o_ref):
        o_ref[...] = x_ref[...] + mod_ref[i_g:i_g + 1, :] * m_ref[...]

    return _call(body, name=name, out_shape=_sds((R, D)), grid=(R // br,),
                 in_specs=[row, row, pl.BlockSpec((6, D), lambda i: (0, 0))], out_specs=row,
                 sem=("parallel",))(x, m, mod)


def _resid_bwd(dx, m, mod, i_g, *, name, br=256):
    R = dx.shape[0]
    row = pl.BlockSpec((br, D), lambda i: (i, 0))
    vec = pl.BlockSpec((1, D), lambda i: (0, 0))

    def body(dx_ref, m_ref, mod_ref, dm_ref, dg_ref):
        dxv = dx_ref[...]
        dm_ref[...] = (dxv * mod_ref[i_g:i_g + 1, :]).astype(BF16)

        @pl.when(pl.program_id(0) == 0)
        def _():
            dg_ref[...] = jnp.zeros_like(dg_ref)

        dg_ref[...] += jnp.sum(dxv * m_ref[...], axis=0, keepdims=True)

    return _call(body, name=name, out_shape=(_sds((R, D), BF16), _sds((1, D))), grid=(R // br,),
                 in_specs=[row, row, pl.BlockSpec((6, D), lambda i: (0, 0))], out_specs=(row, vec),
                 sem=("arbitrary",))(dx, m, mod)


def _ffn_fn(shifts, ug, uv, wg, wv, bg, bv):
    down, up = shifts

    def conv(x, w, b):
        return down(x) * w[0:1, :] + x * w[1:2, :] + up(x) * w[2:3, :] + b

    return _silu(conv(ug, wg, bg)) * conv(uv, wv, bv)


def _ffn_fwd(up, cw, cb, *, bw=256):
    N = up.shape[0]
    shifts = _make_shift(((0, N),))
    nb = DFF // bw

    def body(ug, uv, wg, wv, bg, bv, a_ref):
        a_ref[...] = _ffn_fn(shifts, ug[...], uv[...], wg[...], wv[...], bg[...], bv[...]).astype(BF16)

    def col(rows, off):
        return pl.BlockSpec((rows, bw), lambda j: (0, j + off))

    return _call(body, name="ffn_fwd", out_shape=_sds((N, DFF), BF16), grid=(nb,),
                 in_specs=[col(N, 0), col(N, nb), col(3, 0), col(3, nb), col(1, 0), col(1, nb)],
                 out_specs=col(N, 0), sem=("parallel",), vmem=VMEM_BIG)(up, up, cw, cw, cb, cb)


def _ffn_bwd(up, cw, cb, da, *, bw=256):
    N = up.shape[0]
    shifts = _make_shift(((0, N),))
    nb = DFF // bw

    def body(ug, uv, wg, wv, bg, bv, da_ref, dug, duv, dwg, dwv, dbg, dbv):
        _, vjp = jax.vjp(functools.partial(_ffn_fn, shifts), ug[...], uv[...], wg[...], wv[...], bg[...], bv[...])
        g = vjp(da_ref[...])
        dug[...] = g[0].astype(BF16)
        duv[...] = g[1].astype(BF16)
        dwg[...], dwv[...], dbg[...], dbv[...] = g[2], g[3], g[4], g[5]

    def col(rows, off):
        return pl.BlockSpec((rows, bw), lambda j: (0, j + off))

    half = (_sds((N, DFF), BF16), _sds((N, DFF), BF16), _sds((3, DFF)), _sds((3, DFF)), _sds((1, DFF)), _sds((1, DFF)))
    dug, duv, dwg, dwv, dbg, dbv = _call(
        body, name="ffn_bwd", out_shape=half, grid=(nb,),
        in_specs=[col(N, 0), col(N, nb), col(3, 0), col(3, nb), col(1, 0), col(1, nb), col(N, 0)],
        out_specs=(col(N, 0), col(N, 0), col(3, 0), col(3, 0), col(1, 0), col(1, 0)),
        sem=("parallel",), vmem=VMEM_BIG)(up, up, cw, cw, cb, cb, da)
    return (jnp.concatenate([dug, duv], axis=1), jnp.concatenate([dwg, dwv], axis=1),
            jnp.concatenate([dbg, dbv], axis=1))


def _head_fn(x1, dn, g2, fw, tgt):
    y = _rms(x1 + g2 * dn) * fw
    err = y - tgt
    return 0.5 * jnp.sum(jnp.mean(err * err, axis=-1))


def _head(x1, dn, mod, fw, tgt, *, br=256):
    N = x1.shape[0]
    row = pl.BlockSpec((br, D), lambda i: (i, 0))
    vec = pl.BlockSpec((1, D), lambda i: (0, 0))
    one = pl.BlockSpec((1, HD), lambda i: (0, 0))

    def body(x1_ref, dn_ref, mod_ref, fw_ref, tgt_ref, loss_ref, dx_ref, ddn_ref, dg_ref, dfw_ref):
        loss, (gx, gdn, gg, gfw) = jax.value_and_grad(_head_fn, argnums=(0, 1, 2, 3))(
            x1_ref[...], dn_ref[...], mod_ref[5:6, :], fw_ref[...], tgt_ref[...])
        dx_ref[...] = gx
        ddn_ref[...] = gdn.astype(BF16)

        @pl.when(pl.program_id(0) == 0)
        def _():
            loss_ref[...] = jnp.zeros_like(loss_ref)
            dg_ref[...] = jnp.zeros_like(dg_ref)
            dfw_ref[...] = jnp.zeros_like(dfw_ref)

        loss_ref[...] += jnp.broadcast_to(loss, (1, HD))
        dg_ref[...] += gg
        dfw_ref[...] += gfw

    return _call(body, name="head", out_shape=(_sds((1, HD)), _sds((N, D)), _sds((N, D), BF16), _sds((1, D)), _sds((1, D))),
                 grid=(N // br,), in_specs=[row, row, pl.BlockSpec((6, D), lambda i: (0, 0)), vec, row],
                 out_specs=(one, row, row, vec, vec), sem=("arbitrary",))(x1, dn, mod, fw, tgt)


def _adamw(w, g, m, v, *, name):
    shape = w.shape
    cols = shape[-1]
    rows = max(1, math.prod(shape[:-1]))
    w2, g2, m2, v2 = (t.reshape(rows, cols) for t in (w, g, m, v))
    br = 256 if rows % 256 == 0 else (128 if rows % 128 == 0 else (8 if rows % 8 == 0 and rows > 64 else rows))
    if rows % 352 == 0:
        br = 352
    c1 = 1.0 - B1 ** STEP
    c2 = 1.0 - B2 ** STEP

    def body(w_ref, g_ref, m_ref, v_ref, d_ref, nm_ref, nv_ref):
        gv = g_ref[...]
        nm = B1 * m_ref[...] + (1.0 - B1) * gv
        nv = B2 * v_ref[...] + (1.0 - B2) * (gv * gv)
        d_ref[...] = -LR * ((nm / c1) / (jnp.sqrt(nv / c2) + AEPS) + WD * w_ref[...])
        nm_ref[...] = nm
        nv_ref[...] = nv

    blk = pl.BlockSpec((br, cols), lambda i: (i, 0))
    outs = _call(body, name=name, out_shape=(_sds((rows, cols)),) * 3, grid=(rows // br,),
                 in_specs=[blk] * 4, out_specs=(blk,) * 3, sem=("parallel",))(w2, g2, m2, v2)
    return tuple(t.reshape(shape) for t in outs)


def _rope_tables(N, L):
    t = jnp.arange(N)
    pos = jnp.stack([(t // GRID_W).astype(F32), (t % GRID_W).astype(F32)], axis=1)
    inv = ROPE_THETA ** (-jnp.arange(0, HD // 2, 2, dtype=F32) / (HD // 2))
    ang = pos[:, :, None] * inv[None, None, :]
    cos = jnp.broadcast_to(jnp.cos(ang)[:, :, None, :], (N, 2, 2, HD // 4)).reshape(N, HD)
    sin = jnp.broadcast_to(jnp.sin(ang)[:, :, None, :], (N, 2, 2, HD // 4))
    sin = (sin * jnp.array([-1.0, 1.0], F32)[None, None, :, None]).reshape(N, HD)
    cos = jnp.concatenate([jnp.ones((L, HD), F32), cos], axis=0)
    sin = jnp.concatenate([jnp.zeros((L, HD), F32), sin], axis=0)
    return cos, sin


def _pad_lanes(v, off=0):
    return jnp.zeros((1, HD), F32).at[0, off:off + v.shape[0]].set(v)


def _local_step(x, ctx, tgt, mod_lat, mod_ctx, w_in, shards, small):
    N, L = x.shape[0], ctx.shape[0]
    T = N + L
    bounds = ((0, L), (L, T))
    qw, kw, gw = small["q_norm_w"], small["k_norm_w"], small["gdn_norm_w"]
    conv_w, ffn_w, ffn_b, fnw = small["conv_qkv_w"], small["ffn_conv_w"], small["ffn_conv_b"], small["final_norm_w"]
    alog = _pad_lanes(small["a_log"].reshape(-1), 2 * GH)
    dtb = _pad_lanes(small["dt_bias"].reshape(-1), 2 * GH)
    cos, sin = _rope_tables(N, L)
    bt = T
    bnl = 256 if N % 1024 else 1024

    hc = _normmod_fwd(ctx, mod_ctx, 0, 1, name="normmod_ctx")
    hx = _normmod_fwd(x, mod_lat, 0, 1, name="normmod_x")
    h1 = jnp.concatenate([hc, hx], axis=0)
    proj = _mm(h1, w_in, name="mm_in", M=T, N=C_END, K=D, tb=True, bm=bt, bn=1024)
    aq, ak, av = _aprep_fwd(proj, cos, sin, qw, kw)
    (attn, attn32, lse), (up_g,) = _attn_fwd(aq, ak, av, L, _Exchange([shards["w_up"]], False))
    gq = _gprep_fwd(proj, conv_w, 0, bounds)
    gk = _gprep_fwd(proj, conv_w, 1, bounds)
    gv = _gprep_fwd(proj, conv_w, 2, bounds)
    bl = _bl_fwd(proj, alog, dtb)
    intra, (down_g, pa_g, pd_g, out_g) = _intra_fwd(
        gq, gk, gv, bl, _Exchange([shards[n] for n in ("w_down", "w_pa", "w_pd", "w_out")], False))
    w_up, w_down = up_g.reshape(2 * DFF, D), down_g.reshape(DFF, D)
    w_pa, w_pd, w_out = pa_g.reshape(D, D), pd_g.reshape(D, D), out_g.reshape(D, D)
    xinv, intra = intra[6], intra[:6]
    o, states = _scan_fwd(*intra, L)
    gdn = _gout_fwd(o, proj, gw, L)
    pa = _mm(attn, w_pa, name="mm_pa", M=N, N=D, K=D, bm=bnl)
    pd = _mm(gdn, w_pd, name="mm_pd", M=N, N=D, K=D, bm=bnl)
    y = _merge_fwd(pa, pd, proj, L)
    m = _mm(y, w_out, name="mm_out", M=N, N=D, K=D, bm=bnl)
    x1 = _resid_fwd(x, m, mod_lat, 2, name="resid1")
    h2 = _normmod_fwd(x1, mod_lat, 3, 4, name="normmod_x1")
    up = _mm(h2, w_up, name="mm_up", M=N, N=2 * DFF, K=D, tb=True, bm=bnl, bn=2 * DFF // 4)
    a = _ffn_fwd(up, ffn_w, ffn_b)
    dn = _mm(a, w_down, name="mm_down", M=N, N=D, K=DFF, bm=bnl)
    loss, dx2, ddn, dg2, dfnw = _head(x1, dn, mod_lat, fnw, tgt)

    da = _mm(ddn, w_down, name="mm_down_dx", M=N, N=DFF, K=D, tb=True, bm=bnl, bn=DFF // 2)
    g_down = _mm(a, ddn, name="mm_down_dw", M=DFF, N=D, K=N, ta=True, bm=DFF // 2, out_dtype=BF16)
    dup, d_ffn_w, d_ffn_b = _ffn_bwd(up, ffn_w, ffn_b, da)
    dh2 = _mm(dup, w_up, name="mm_up_dx", M=N, N=D, K=2 * DFF, bm=bnl, bk=2 * DFF // 4)
    g_up = _mm(dup, h2, name="mm_up_dw", M=2 * DFF, N=D, K=N, ta=True, bm=2 * DFF // 4, out_dtype=BF16)
    dx1, dsh2, dsc2 = _normmod_bwd(x1, mod_lat, 3, 4, dh2, 0, dx2, name="normmod_x1_bwd")
    dm, dg1 = _resid_bwd(dx1, m, mod_lat, 2, name="resid1_bwd")
    dy = _mm(dm, w_out, name="mm_out_dx", M=N, N=D, K=D, tb=True, bm=bnl)
    g_out = _mm(y, dm, name="mm_out_dw", M=D, N=D, K=N, ta=True, out_dtype=BF16)
    dpa, dpd, dproj = _merge_bwd(pa, pd, proj, dy, L)
    dattn = _mm(dpa, w_pa, name="mm_pa_dx", M=N, N=D, K=D, tb=True, bm=bnl)
    g_pa = _mm(attn, dpa, name="mm_pa_dw", M=D, N=D, K=N, ta=True, out_dtype=BF16)
    dgdn = _mm(dpd, w_pd, name="mm_pd_dx", M=N, N=D, K=D, tb=True, bm=bnl)
    g_pd = _mm(gdn, dpd, name="mm_pd_dw", M=D, N=D, K=N, ta=True, out_dtype=BF16)
    do, dproj, dgw = _gout_bwd(o, proj, gw, dgdn, dproj, L)
    cts, recv_a = _scan_bwd(*intra, states, do, L, _Exchange(
        [g_out.reshape(NDEV, D // NDEV, D), g_down.reshape(NDEV, DFF // NDEV, D)], True))
    (dgq, dgk, dgv, dbl), recv_b = _intra_bwd(gq, gk, gv, bl, xinv, cts, _Exchange(
        [g_pa.reshape(NDEV, D // NDEV, D), g_pd.reshape(NDEV, D // NDEV, D), g_up.reshape(NDEV, 2 * DFF // NDEV, D)], True))
    recv = dict(zip(("w_out", "w_down", "w_pa", "w_pd", "w_up"), recv_a + recv_b))
    dproj, dwq = _gprep_bwd(proj, conv_w, 0, bounds, dgq, dproj)
    dproj, dwk = _gprep_bwd(proj, conv_w, 1, bounds, dgk, dproj)
    dproj, dwv = _gprep_bwd(proj, conv_w, 2, bounds, dgv, dproj)
    dproj, dalog, ddtb = _bl_bwd(proj, alog, dtb, dbl, dproj)
    daq_h, dak_h, dav_h = _attn_bwd(aq, ak, av, attn32, lse, dattn, L)
    dproj, dqw, dkw = _aprep_bwd(proj, cos, sin, qw, kw, daq_h, dak_h, dav_h, dproj, L)
    g_in = _mm(dproj, h1, name="mm_in_dw", M=C_END, N=D, K=T, ta=True, bm=1024, out_dtype=BF16)
    g_in = _unpad_columns(g_in).reshape(NDEV, W_END // NDEV, D)
    own_in = lax.dynamic_index_in_dim(g_in, _position()[3], axis=0, keepdims=False)
    *pending, token = _scatter_start(g_in, None, (0, D // 2), (), name="scatter_g_in_a_start")
    dh1 = _mm(dproj, w_in, name="mm_in_dx", M=T, N=D, K=C_END, bm=bt, bk=1024, after=(token,))
    grad_x, dsh1, dsc1 = _normmod_bwd(x, mod_lat, 0, 1, dh1, L, dx1, name="normmod_x_bwd")
    _, dcsh1, dcsc1 = _normmod_bwd(ctx, mod_ctx, 0, 1, dh1, 0, None, name="normmod_ctx_bwd")

    z1 = jnp.zeros((1, D), F32)
    dmod_lat = jnp.concatenate([dsh1, dsc1, dg1, dsh2, dsc2, dg2], axis=0)
    dmod_ctx = jnp.concatenate([dcsh1, dcsc1, z1, z1, z1, z1], axis=0)
    gsmall = {
        "q_norm_w": dqw, "k_norm_w": dkw, "gdn_norm_w": dgw,
        "conv_qkv_w": jnp.concatenate([dwq, dwk, dwv], axis=1),
        "a_log": dalog[0, 2 * GH:4 * GH], "dt_bias": ddtb[0, 2 * GH:4 * GH],
        "ffn_conv_w": d_ffn_w, "ffn_conv_b": d_ffn_b, "final_norm_w": dfnw,
    }
    return loss[0, 0], grad_x, (pending, own_in), recv, dmod_lat, dmod_ctx, gsmall


HBM = pl.BlockSpec(memory_space=pltpu.HBM)
ANYSPEC = pl.BlockSpec(memory_space=pl.ANY)


def _position():
    x, y, c = lax.axis_index("x"), lax.axis_index("y"), lax.axis_index("c")
    return x, y, c, 4 * x + 2 * y + c


def _peer(x, y, c, k):
    px = 1 - x if k & 4 else x
    py = 1 - y if k & 2 else y
    pc = 1 - c if k & 1 else c
    return (px, py, pc), 4 * px + 2 * py + pc


def _exchange(arrs, *, name, scatter):
    exch = _Exchange(arrs, scatter)
    n = exch.n

    def body(*refs):
        ins, outs, sems = refs[:n], refs[n:2 * n], refs[2 * n:]
        exch.start(ins, outs, sems)
        exch.finish(ins, outs, sems)

    outs = pl.pallas_call(body, name=name, out_shape=exch.out_shape, in_specs=[HBM] * n, out_specs=(HBM,) * n,
                          scratch_shapes=exch.scratch,
                          compiler_params=pltpu.CompilerParams(has_side_effects=True))(*arrs)
    return list(outs)


class _Exchange:
    def __init__(self, arrs, scatter):
        self.arrs, self.scatter, self.n = list(arrs), scatter, len(arrs)
        self.out_shape = tuple(_sds(a.shape if scatter else (NDEV,) + a.shape, a.dtype) for a in arrs)
        self.scratch = [pltpu.SemaphoreType.DMA((self.n, NDEV - 1)), pltpu.SemaphoreType.DMA((self.n, NDEV - 1)),
                        pltpu.SemaphoreType.DMA((self.n,))]

    def _copies(self, ins, outs, sems):
        send, recv, loc = sems
        x, y, c, me = _position()
        local = [pltpu.make_async_copy(ins[a].at[me] if self.scatter else ins[a], outs[a].at[me], loc.at[a])
                 for a in range(self.n)]
        remote = []
        for k in range(1, NDEV):
            peer, pid = _peer(x, y, c, k)
            for a in range(self.n):
                src = ins[a].at[pid] if self.scatter else ins[a]
                remote.append(pltpu.make_async_remote_copy(
                    src_ref=src, dst_ref=outs[a].at[me], send_sem=send.at[a, k - 1], recv_sem=recv.at[a, k - 1],
                    device_id=peer, device_id_type=MESH))
        return local, remote

    def start(self, ins, outs, sems):
        local, remote = self._copies(ins, outs, sems)
        for cp in local + remote:
            cp.start()

    def finish(self, ins, outs, sems):
        local, remote = self._copies(ins, outs, sems)
        for cp in remote:
            cp.wait()
        for cp in local:
            cp.wait()


def _gather_two_level(block, *, name):
    def body(x_ref, out_ref, send_sems, recv_sems, local_sem):
        x, y, c, _ = _position()
        me, sibling = (x, y, c), (x, y, 1 - c)
        chips = [(1 - x, y), (x, 1 - y), (1 - x, 1 - y)]

        def slot(px, py, pc):
            return out_ref.at[4 * px + 2 * py + pc]

        def copy(k, owner, to, src=None):
            return pltpu.make_async_remote_copy(
                src_ref=slot(*owner) if src is None else src, dst_ref=slot(*owner), send_sem=send_sems.at[k],
                recv_sem=recv_sems.at[k], device_id=to, device_id_type=MESH)

        mine = pltpu.make_async_copy(x_ref, slot(*me), local_sem)
        mine.start()
        first = [copy(0, me, sibling, src=x_ref)]
        first += [copy(1 + j, me, (*chip, c), src=x_ref) for j, chip in enumerate(chips)]
        for cp in first:
            cp.start()
        passed = [copy(4 + j, (*chip, c), sibling) for j, chip in enumerate(chips)]
        for j, chip in enumerate(chips):
            copy(1 + j, (*chip, c), me).wait_recv()
            passed[j].start()
        copy(0, sibling, me).wait_recv()
        for j, chip in enumerate(chips):
            copy(4 + j, (*chip, 1 - c), me).wait_recv()
        for cp in first + passed:
            cp.wait_send()
        mine.wait()

    return pl.pallas_call(
        body, name=name, out_shape=_sds((NDEV,) + block.shape, block.dtype), in_specs=[HBM], out_specs=HBM,
        scratch_shapes=[pltpu.SemaphoreType.DMA((NDEV - 1,)), pltpu.SemaphoreType.DMA((NDEV - 1,)),
                        pltpu.SemaphoreType.DMA],
        compiler_params=pltpu.CompilerParams(has_side_effects=True))(block)


SEM = pl.BlockSpec(memory_space=pltpu.SEMAPHORE)


def _scatter_copies(src_ref, land_ref, send_sems, recv_sems, cols):
    x, y, c, me = _position()
    span = (slice(None), pl.ds(*cols))
    copies = []
    for k in range(1, NDEV):
        peer, pid = _peer(x, y, c, k)
        copies.append(pltpu.make_async_remote_copy(
            src_ref=src_ref.at[pid].at[span], dst_ref=land_ref.at[me].at[span], send_sem=send_sems.at[k - 1],
            recv_sem=recv_sems.at[k - 1], device_id=peer, device_id_type=MESH))
    return copies


SPLIT_EFFECT = pltpu.SideEffectType.DATAFLOW_SIDE_EFFECTING


def _scatter_start(parts, land, cols, after, *, name):
    na = len(after)
    if land is None:
        land = lax.empty(parts.shape, parts.dtype)

    def body(src_ref, land_ref, *rest):
        send_sems, recv_sems, _, _, token = rest[na:]
        for cp in _scatter_copies(src_ref, land_ref, send_sems, recv_sems, cols):
            cp.start()
        token[...] = jnp.zeros_like(token)

    return pl.pallas_call(
        body, name=name,
        out_shape=(pltpu.SemaphoreType.DMA((NDEV - 1,)), pltpu.SemaphoreType.DMA((NDEV - 1,)),
                   pltpu.HBM(parts.shape, parts.dtype), pltpu.HBM(parts.shape, parts.dtype), _sds((8, HD))),
        in_specs=(HBM, HBM) + (pl.BlockSpec(memory_space=pl.ANY),) * na,
        out_specs=(SEM, SEM, HBM, HBM, pl.BlockSpec(memory_space=pltpu.VMEM)),
        input_output_aliases={0: 2, 1: 3}, compiler_params=pltpu.CompilerParams(has_side_effects=SPLIT_EFFECT),
    )(pltpu.with_memory_space_constraint(parts, pltpu.HBM), pltpu.with_memory_space_constraint(land, pltpu.HBM), *after)


def _scatter_wait(send_sems, recv_sems, src_thru, land_thru, cols, after, *, name):
    na = len(after)

    def body(src_ref, land_ref, send_sems, recv_sems, *rest):
        for cp in _scatter_copies(src_ref, land_ref, send_sems, recv_sems, cols):
            cp.wait_send()
            cp.wait_recv()

    return pl.pallas_call(
        body, name=name,
        out_shape=(pltpu.HBM(src_thru.shape, src_thru.dtype), pltpu.HBM(land_thru.shape, land_thru.dtype)),
        in_specs=(HBM, HBM, SEM, SEM) + (pl.BlockSpec(memory_space=pl.ANY),) * na, out_specs=(HBM, HBM),
        input_output_aliases={0: 0, 1: 1}, compiler_params=pltpu.CompilerParams(has_side_effects=SPLIT_EFFECT),
    )(src_thru, land_thru, send_sems, recv_sems, *after)


def _cast_bf16(w, *, name):
    rows, cols = w.shape
    br = 128 if rows % 128 == 0 else rows

    def body(w_ref, o_ref):
        o_ref[...] = w_ref[...].astype(BF16)

    blk = pl.BlockSpec((br, cols), lambda i: (i, 0))
    return _call(body, name=name, out_shape=_sds((rows, cols), BF16), grid=(rows // br,), in_specs=[blk],
                 out_specs=blk, sem=("parallel",))(w)


def _sum_slots(a, *, name):
    _, R, C = a.shape

    def body(a_ref, o_ref):
        s = a_ref[0]
        for d in range(1, NDEV):
            s = s + a_ref[d]
        o_ref[...] = s

    return _call(body, name=name, out_shape=_sds((R, C)))(a)


MODROWS = 16


def _mod_fwd(c9, w, b):
    cols = w.shape[1]

    def body(c_ref, w_ref, b_ref, o_ref):
        o_ref[...] = _nn(_silu(c_ref[...]), w_ref[...]) + b_ref[...]

    return _call(body, name="mod_fwd", out_shape=_sds((MODROWS, cols)))(c9, w, b)


def _mod_bwd(c9, dmy, dall, w):
    cols = w.shape[1]

    def body(c_ref, dmy_ref, dall_ref, w_ref, gw_ref, gb_ref, cp_ref):
        sc = _silu(c_ref[...])
        rows = lax.broadcasted_iota(jnp.int32, (MODROWS, 1), 0)
        d = dmy_ref[...]
        d_ctx = jnp.where(rows == NDEV, d, 0.0)
        sc_ctx = jnp.where(rows == NDEV, sc, 0.0)
        outer = lax.dot_general(sc_ctx, d_ctx, (((0,), (0,)), ((), ())), precision=HI, preferred_element_type=F32)
        gw_ref[...] = _tn(jnp.where(rows < NDEV, sc, 0.0), jnp.where(rows < NDEV, d, 0.0)) + outer
        gb_ref[...] = jnp.sum(dall_ref[...], axis=0, keepdims=True)
        cp_ref[...] = jnp.sum(_nt(d_ctx, w_ref[...]), axis=0, keepdims=True)

    return _call(body, name="mod_bwd", out_shape=(_sds((D, cols)), _sds((1, 6 * D)), _sds((1, D))),
                 vmem=VMEM_BIG)(c9, dmy, dall, w)


def _cctx_finish(parts, c_ctx, after):
    VM = pl.BlockSpec(memory_space=pltpu.VMEM)

    def body(p_ref, c_ref, *rest):
        o_ref = rest[-1]
        s = p_ref[0]
        for d in range(1, NDEV):
            s = s + p_ref[d]
        _, vjp = jax.vjp(_silu, c_ref[...])
        o_ref[...] = vjp(s)[0]

    return _call(body, name="cctx_finish", out_shape=_sds((1, D)),
                 in_specs=[VM, VM] + [pl.BlockSpec(memory_space=pl.ANY)] * len(after))(parts, c_ctx, *after)


def _adamw_recv(w, recv, m, v, *, name, own=None):
    rows, cols = w.shape
    bc = 256
    c1 = 1.0 - B1 ** STEP
    c2 = 1.0 - B2 ** STEP
    has_own = own is not None

    def body(w_ref, r_ref, m_ref, v_ref, *rest):
        g_ref, d_ref, nm_ref, nv_ref = rest[-4:]
        me = _position()[3]

        def slot(d):
            return jnp.where(me == d, rest[0][...], r_ref[d]) if has_own else r_ref[d]

        gv = slot(0).astype(F32)
        for d in range(1, NDEV):
            gv = gv + slot(d).astype(F32)
        nm = B1 * m_ref[...] + (1.0 - B1) * gv
        nv = B2 * v_ref[...] + (1.0 - B2) * (gv * gv)
        g_ref[...] = gv
        d_ref[...] = -LR * ((nm / c1) / (jnp.sqrt(nv / c2) + AEPS) + WD * w_ref[...])
        nm_ref[...] = nm
        nv_ref[...] = nv

    blk = pl.BlockSpec((rows, bc), lambda j: (0, j))
    return _call(body, name=name, out_shape=(_sds((rows, cols)),) * 4, grid=(cols // bc,),
                 in_specs=[blk, pl.BlockSpec((NDEV, rows, bc), lambda j: (0, 0, j)), blk, blk] + [blk] * has_own,
                 out_specs=(blk,) * 4, sem=("parallel",), vmem=VMEM_BIG)(w, recv, m, v, *([own] if has_own else []))


P_LAT, P_CTX, P_FNW, P_FFNB, P_CONV, P_FFNW, P_MISC, P_ROWS = 0, 8, 16, 24, 32, 48, 72, 80


def _rows_of(v, nrows):
    flat = v.reshape(-1)
    return jnp.pad(flat, (0, nrows * D - flat.shape[0])).reshape(nrows, D)


def _by_columns(g):
    n, r, c = g.shape
    return jnp.transpose(g, (1, 0, 2)).reshape(r, n * c)


def kernel(x, c, ctx, c_ctx, w_mod, b_mod, w_in, q_norm_w, k_norm_w, conv_qkv_w, a_log, dt_bias, gdn_norm_w, w_pa, w_pd, w_out, w_up, ffn_conv_w, ffn_conv_b, w_down, final_norm_w, loss_target, m_c_ctx, m_w_mod, m_b_mod, m_w_in, m_q_norm_w, m_k_norm_w, m_conv_qkv_w, m_a_log, m_dt_bias, m_gdn_norm_w, m_w_pa, m_w_pd, m_w_out, m_w_up, m_ffn_conv_w, m_ffn_conv_b, m_w_down, m_final_norm_w, v_c_ctx, v_w_mod, v_b_mod, v_w_in, v_q_norm_w, v_k_norm_w, v_conv_qkv_w, v_a_log, v_dt_bias, v_gdn_norm_w, v_w_pa, v_w_pd, v_w_out, v_w_up, v_ffn_conv_w, v_ffn_conv_b, v_w_down, v_final_norm_w):
    _, _, _, me = _position()
    mcols = w_mod.shape[2]

    transposed = ("w_in", "w_up")
    big = {"w_in": w_in[0].T, "w_pa": w_pa[0], "w_pd": w_pd[0], "w_out": w_out[0], "w_up": w_up[0].T, "w_down": w_down[0]}
    names = list(big)
    shards = {n: _cast_bf16(big[n], name="cast_" + n) for n in names}
    w_in_g = _gather_two_level(shards["w_in"], name="gather_w_in")
    c_all, conv_g, ffnw_g = _exchange([c, conv_qkv_w[0], ffn_conv_w[0]], name="gather_small", scatter=False)
    w_in_full = w_in_g.reshape(W_END, D)
    w_in_pad = _pad_columns(w_in_full)

    c9 = jnp.concatenate([c_all.reshape(NDEV, D), jnp.pad(c_ctx[None], ((0, MODROWS - NDEV - 1), (0, 0)))], axis=0)
    b_loc = lax.dynamic_slice(b_mod, (0, me * mcols), (1, mcols))
    mod_all, = _exchange([_mod_fwd(c9, w_mod[0], b_loc)], name="gather_mod", scatter=False)
    mod_lat = lax.dynamic_index_in_dim(mod_all, me, axis=1, keepdims=False).reshape(6, D)
    mod_ctx = mod_all[:, NDEV, :].reshape(6, D)

    small = {"q_norm_w": q_norm_w, "k_norm_w": k_norm_w, "gdn_norm_w": gdn_norm_w, "a_log": a_log, "dt_bias": dt_bias,
             "conv_qkv_w": _by_columns(conv_g), "ffn_conv_w": _by_columns(ffnw_g), "ffn_conv_b": ffn_conv_b,
             "final_norm_w": final_norm_w[None]}
    loss_me, grad_x, (pending_in, own_in), recv, dmod_lat, dmod_ctx, gs = _local_step(
        x[0], ctx[0], loss_target[0], mod_lat, mod_ctx, w_in_pad, shards, small)

    moments = {"w_in": (m_w_in, v_w_in), "w_pa": (m_w_pa, v_w_pa), "w_pd": (m_w_pd, v_w_pd),
               "w_out": (m_w_out, v_w_out), "w_up": (m_w_up, v_w_up), "w_down": (m_w_down, v_w_down)}
    res = {}
    def finish(n, outs):
        return tuple((t.T if n in transposed else t)[None] for t in outs)

    def moment(t, n):
        return t[0].T if n in transposed else t[0]

    for n in recv:
        res[n] = finish(n, _adamw_recv(big[n], recv[n], moment(moments[n][0], n), moment(moments[n][1], n),
                                       name="adamw_" + n))

    misc = jnp.concatenate([gs["q_norm_w"][0], gs["k_norm_w"][0], gs["gdn_norm_w"][0], gs["a_log"], gs["dt_bias"],
                            loss_me[None]])
    pack = jnp.concatenate([_rows_of(dmod_lat, P_CTX - P_LAT), _rows_of(dmod_ctx, P_FNW - P_CTX),
                            _rows_of(gs["final_norm_w"], P_FFNB - P_FNW), _rows_of(gs["ffn_conv_b"], P_CONV - P_FFNB),
                            _rows_of(gs["conv_qkv_w"], P_FFNW - P_CONV), _rows_of(gs["ffn_conv_w"], P_MISC - P_FFNW),
                            _rows_of(misc, P_ROWS - P_MISC)], axis=0)
    pack_all, = _exchange([pack], name="gather_pack", scatter=False)
    tot = _sum_slots(pack_all, name="sum_pack")
    dall = jnp.concatenate([pack_all[:, P_LAT:P_LAT + 6, :].reshape(NDEV, 6 * D),
                            jnp.pad(tot[P_CTX:P_CTX + 6].reshape(1, 6 * D), ((0, MODROWS - NDEV - 1), (0, 0)))], axis=0)
    dmy = lax.dynamic_slice(dall, (0, me * mcols), (MODROWS, mcols))
    g_w_mod, g_b_mod, cpart = _mod_bwd(c9, dmy, dall, w_mod[0])
    cparts, = _exchange([cpart], name="gather_cctx", scatter=False)
    sems_a, land = pending_in[:2], pending_in[3]
    *sems_b, g_in_thru, land, token_b = _scatter_start(pending_in[2], land, (D // 2, D // 2), (cparts,),
                                                       name="scatter_g_in_b_start")
    g_c_ctx = _cctx_finish(cparts, c_ctx[None], (token_b,))[0]

    nconv, nffn = 3 * GH * HD, 2 * DFF
    conv_tot = tot[P_CONV:P_FFNW].reshape(-1)[:3 * nconv].reshape(3, nconv)
    ffnw_tot = tot[P_FFNW:P_MISC].reshape(-1)[:3 * nffn].reshape(3, nffn)
    mrow = tot[P_MISC]
    grads = {
        "c_ctx": g_c_ctx, "w_mod": g_w_mod[None], "b_mod": g_b_mod,
        "q_norm_w": mrow[None, 0:HD], "k_norm_w": mrow[None, HD:2 * HD], "gdn_norm_w": mrow[None, 2 * HD:3 * HD],
        "conv_qkv_w": lax.dynamic_slice(conv_tot, (0, me * (nconv // NDEV)), (3, nconv // NDEV))[None],
        "a_log": mrow[3 * HD:3 * HD + 2 * GH].reshape(1, 2, GH),
        "dt_bias": mrow[3 * HD + 2 * GH:3 * HD + 4 * GH].reshape(1, 2, GH),
        "ffn_conv_w": lax.dynamic_slice(ffnw_tot, (0, me * (nffn // NDEV)), (3, nffn // NDEV))[None],
        "ffn_conv_b": tot[P_FFNB:P_CONV].reshape(-1)[:nffn][None],
        "final_norm_w": tot[P_FNW],
    }
    loss = mrow[3 * HD + 4 * GH]
    given = {"c_ctx": (c_ctx, m_c_ctx, v_c_ctx), "w_mod": (w_mod, m_w_mod, v_w_mod), "b_mod": (b_mod, m_b_mod, v_b_mod),
             "q_norm_w": (q_norm_w, m_q_norm_w, v_q_norm_w), "k_norm_w": (k_norm_w, m_k_norm_w, v_k_norm_w),
             "conv_qkv_w": (conv_qkv_w, m_conv_qkv_w, v_conv_qkv_w), "a_log": (a_log, m_a_log, v_a_log),
             "dt_bias": (dt_bias, m_dt_bias, v_dt_bias), "gdn_norm_w": (gdn_norm_w, m_gdn_norm_w, v_gdn_norm_w),
             "ffn_conv_w": (ffn_conv_w, m_ffn_conv_w, v_ffn_conv_w), "ffn_conv_b": (ffn_conv_b, m_ffn_conv_b, v_ffn_conv_b),
             "final_norm_w": (final_norm_w, m_final_norm_w, v_final_norm_w)}
    for n, (w, m, v) in given.items():
        res[n] = (grads[n],) + _adamw(w, grads[n], m, v, name="adamw_" + n)

    g_in_thru, land = _scatter_wait(*sems_a, g_in_thru, land, (0, D // 2), [res[n][1] for n in res],
                                    name="scatter_g_in_a_wait")
    _, land = _scatter_wait(*sems_b, g_in_thru, land, (D // 2, D // 2), (), name="scatter_g_in_b_wait")
    res["w_in"] = finish("w_in", _adamw_recv(big["w_in"], land, moment(m_w_in, "w_in"), moment(v_w_in, "w_in"),
                                             name="adamw_w_in", own=own_in))

    order = ["c_ctx", "w_mod", "b_mod", "w_in", "q_norm_w", "k_norm_w", "conv_qkv_w", "a_log", "dt_bias", "gdn_norm_w",
             "w_pa", "w_pd", "w_out", "w_up", "ffn_conv_w", "ffn_conv_b", "w_down", "final_norm_w"]
    return (loss, grad_x[None], *[res[n][0] for n in order], *[res[n][1] for n in order],
            *[res[n][2] for n in order], *[res[n][3] for n in order])
```

```python
import functools
import math

import jax
import jax.numpy as jnp
from jax import lax
from jax.experimental import pallas as pl
from jax.experimental.pallas import tpu as pltpu

F32 = jnp.float32
BF16 = jnp.bfloat16
HI = lax.Precision.HIGHEST
MESH = pl.DeviceIdType.MESH

NDEV = 8
D = 1024
HD = 128
AH, AKV, GRP = 8, 2, 4
GH = 8
CH = 64
DFF = 2816
GRID_W = 64
EPS = 1e-6
ROPE_THETA = 10000.0
C_KV, C_AQ, C_QKV, C_BL, C_Z, C_GATE, C_END = 0, 512, 1536, 4608, 5120, 6144, 8192
W_QKV, W_AQ, W_Z, W_END = 512, 3616, 4640, 7712


def _pad_columns(w):
    zeros = jnp.zeros((C_Z - C_QKV - (W_AQ - W_QKV), D), w.dtype)
    return jnp.concatenate([w[:W_QKV], w[W_AQ:W_Z], w[W_QKV:W_AQ], zeros, w[W_Z:]], axis=0)


def _unpad_columns(g):
    return jnp.concatenate([g[:C_AQ], g[C_QKV:C_QKV + W_AQ - W_QKV], g[C_AQ:C_QKV], g[C_Z:]], axis=0)
LR, B1, B2, AEPS, WD, STEP = 0.001, 0.9, 0.999, 1e-08, 0.01, 10
VMEM_BIG = 56 * 1024 * 1024
INTRA_FWD_CHUNKS = 18
INTRA_BWD_CHUNKS = 18


def _call(body, *, name, out_shape, grid=None, in_specs=None, out_specs=None, scratch=(), sem=None,
          vmem=None, aliases=None):
    params = {}
    if sem is not None:
        params["dimension_semantics"] = sem
    if vmem is not None:
        params["vmem_limit_bytes"] = vmem
    kw = {}
    if grid is not None:
        kw["grid"] = grid
    if in_specs is not None:
        kw["in_specs"] = in_specs
    if out_specs is not None:
        kw["out_specs"] = out_specs
    if aliases:
        kw["input_output_aliases"] = aliases
    return pl.pallas_call(body, name=name, out_shape=out_shape, scratch_shapes=list(scratch),
                          compiler_params=pltpu.CompilerParams(**params), **kw)


def _call_carrying(body, exch, *, name, out_shape, grid, in_specs, out_specs, scratch=(), vmem=None):
    n, nin, nout, nscr = exch.n, len(in_specs), len(out_shape), len(scratch)

    def wrapped(*refs):
        ins, cins = refs[:nin], refs[nin:nin + n]
        outs, couts = refs[nin + n:nin + n + nout], refs[nin + n + nout:nin + 2 * n + nout]
        scr, sems = refs[nin + 2 * n + nout:nin + 2 * n + nout + nscr], refs[nin + 2 * n + nout + nscr:]
        ids = [pl.program_id(i) for i in range(len(grid))]
        first = functools.reduce(jnp.logical_and, [i == 0 for i in ids])
        last = functools.reduce(jnp.logical_and, [i == g - 1 for i, g in zip(ids, grid)])

        @pl.when(first)
        def _():
            exch.start(cins, couts, sems)

        body(*ins, *outs, *scr)

        @pl.when(last)
        def _():
            exch.finish(cins, couts, sems)

    params = {"dimension_semantics": ("arbitrary",) * len(grid)}
    if vmem is not None:
        params["vmem_limit_bytes"] = vmem
    fn = pl.pallas_call(wrapped, name=name, out_shape=tuple(out_shape) + exch.out_shape, grid=grid,
                        in_specs=list(in_specs) + [HBM] * n, out_specs=tuple(out_specs) + (HBM,) * n,
                        scratch_shapes=list(scratch) + exch.scratch, compiler_params=pltpu.CompilerParams(**params))

    def run(*args):
        res = fn(*args, *exch.arrs)
        return res[:nout], list(res[nout:])

    return run


def _sds(shape, dtype=F32):
    return jax.ShapeDtypeStruct(tuple(shape), dtype)


def _dot(a, b, ca, cb):
    return lax.dot_general(a.astype(BF16), b.astype(BF16), (((ca,), (cb,)), ((), ())),
                           preferred_element_type=F32)


@jax.custom_vjp
def _nn(a, b):
    return _dot(a, b, 1, 0)


@jax.custom_vjp
def _nt(a, b):
    return _dot(a, b, 1, 1)


@jax.custom_vjp
def _tn(a, b):
    return _dot(a, b, 0, 0)


_nn.defvjp(lambda a, b: (_nn(a, b), (a, b)), lambda r, g: (_nt(g, r[1]), _tn(r[0], g)))
_nt.defvjp(lambda a, b: (_nt(a, b), (a, b)), lambda r, g: (_nn(g, r[1]), _tn(g, r[0])))
_tn.defvjp(lambda a, b: (_tn(a, b), (a, b)), lambda r, g: (_nt(r[1], g), _nn(r[0], g)))


def _hdot(a, b):
    return jnp.dot(a, b, precision=HI, preferred_element_type=F32)


def _mdot(a, b):
    return jnp.dot(a, b, precision=lax.Precision.HIGH, preferred_element_type=F32)


def _maskdot(mask, a, cm):
    hi = a.astype(BF16)
    r = a - hi.astype(F32)
    mid = r.astype(BF16)
    lo = (r - mid.astype(F32)).astype(BF16)
    mb = mask.astype(BF16)
    dims = (((cm,), (0,)), ((), ()))
    return (lax.dot_general(mb, hi, dims, preferred_element_type=F32)
            + lax.dot_general(mb, mid, dims, preferred_element_type=F32)
            + lax.dot_general(mb, lo, dims, preferred_element_type=F32))


@jax.custom_vjp
def _mask_nn(mask, a):
    return _maskdot(mask, a, 1)


_mask_nn.defvjp(lambda mask, a: (_maskdot(mask, a, 1), mask),
                lambda mask, g: (jnp.zeros_like(mask), _maskdot(mask, g, 0)))


def _row_ids(shape):
    return lax.broadcasted_iota(jnp.int32, shape, 0)


def _shift_rows(x, down, bounds):
    n = x.shape[0]
    rows = _row_ids(x.shape)
    y = pltpu.roll(x, 1 if down else n - 1, 0)
    edge = functools.reduce(jnp.logical_or, [rows == (s if down else e - 1) for s, e in bounds])
    return jnp.where(edge, 0.0, y)


def _make_shift(bounds):
    @jax.custom_vjp
    def down(x):
        return _shift_rows(x, True, bounds)

    @jax.custom_vjp
    def up(x):
        return _shift_rows(x, False, bounds)

    down.defvjp(lambda x: (down(x), None), lambda _, g: (up(g),))
    up.defvjp(lambda x: (up(x), None), lambda _, g: (down(g),))
    return down, up


@jax.custom_vjp
def _swap32(x):
    lane = lax.broadcasted_iota(jnp.int32, x.shape, x.ndim - 1)
    return jnp.where((lane % 64) < 32, pltpu.roll(x, HD - 32, x.ndim - 1), pltpu.roll(x, 32, x.ndim - 1))


_swap32.defvjp(lambda x: (_swap32(x), None), lambda _, g: (_swap32(g),))


def _rms(x):
    return x * lax.rsqrt(jnp.mean(x * x, axis=-1, keepdims=True) + EPS)


def _silu(x):
    return x * jax.nn.sigmoid(x)


def _mm(a, b, *, name, M, N, K, ta=False, tb=False, out_dtype=F32, bm=None, bn=None, bk=None,
        a_off=(0, 0), b_off=(0, 0), after=()):
    bm, bn, bk = bm or M, bn or N, bk or K
    assert M % bm == 0 and N % bn == 0 and K % bk == 0, (name, M, N, K, bm, bn, bk)
    nk = K // bk
    ca, cb = (0 if ta else 1), (1 if tb else 0)
    na = len(after)

    def body(a_ref, b_ref, *rest):
        o_ref, acc = rest[na], rest[na + 1:]
        r = _dot(a_ref[...], b_ref[...], ca, cb)
        if nk == 1:
            o_ref[...] = r.astype(out_dtype)
        else:
            acc_ref, = acc
            k = pl.program_id(2)

            @pl.when(k == 0)
            def _():
                acc_ref[...] = r

            @pl.when(k > 0)
            def _():
                acc_ref[...] += r

            @pl.when(k == nk - 1)
            def _():
                o_ref[...] = acc_ref[...].astype(out_dtype)

    def blk(off, bshape):
        assert off[0] % bshape[0] == 0 and off[1] % bshape[1] == 0, (name, off, bshape)
        return off[0] // bshape[0], off[1] // bshape[1]

    if ta:
        ao = blk(a_off, (bk, bm))
        a_spec = pl.BlockSpec((bk, bm), lambda i, j, k: (k + ao[0], i + ao[1]))
    else:
        ao = blk(a_off, (bm, bk))
        a_spec = pl.BlockSpec((bm, bk), lambda i, j, k: (i + ao[0], k + ao[1]))
    if tb:
        bo = blk(b_off, (bn, bk))
        b_spec = pl.BlockSpec((bn, bk), lambda i, j, k: (j + bo[0], k + bo[1]))
    else:
        bo = blk(b_off, (bk, bn))
        b_spec = pl.BlockSpec((bk, bn), lambda i, j, k: (k + bo[0], j + bo[1]))
    return _call(body, name=name, out_shape=_sds((M, N), out_dtype), grid=(M // bm, N // bn, nk),
                 in_specs=[a_spec, b_spec] + [pl.BlockSpec(memory_space=pl.ANY)] * na,
                 out_specs=pl.BlockSpec((bm, bn), lambda i, j, k: (i, j)),
                 scratch=[pltpu.VMEM((bm, bn), F32)] if nk > 1 else [],
                 sem=("parallel", "parallel", "arbitrary"), vmem=VMEM_BIG)(a, b, *after)


def _normmod_fn(x, sh, sc):
    return _rms(x) * (1.0 + sc) + sh


def _normmod_fwd(x, mod, i_sh, i_sc, *, name, br=256):
    R = x.shape[0]

    def body(x_ref, mod_ref, o_ref):
        o_ref[...] = _normmod_fn(x_ref[...], mod_ref[i_sh:i_sh + 1, :], mod_ref[i_sc:i_sc + 1, :]).astype(BF16)

    return _call(body, name=name, out_shape=_sds((R, D), BF16), grid=(R // br,),
                 in_specs=[pl.BlockSpec((br, D), lambda i: (i, 0)), pl.BlockSpec((6, D), lambda i: (0, 0))],
                 out_specs=pl.BlockSpec((br, D), lambda i: (i, 0)), sem=("parallel",))(x, mod)


def _normmod_bwd(x, mod, i_sh, i_sc, dh, dh_off, res, *, name, br=256):
    R = x.shape[0]
    ob = dh_off // br
    has_res = res is not None

    def body(x_ref, mod_ref, dh_ref, *rest):
        if has_res:
            res_ref, dx_ref, dsh_ref, dsc_ref = rest
        else:
            dx_ref, dsh_ref, dsc_ref = rest
        sh, sc = mod_ref[i_sh:i_sh + 1, :], mod_ref[i_sc:i_sc + 1, :]
        _, vjp = jax.vjp(_normmod_fn, x_ref[...], sh, sc)
        dx, dsh, dsc = vjp(dh_ref[...])
        dx_ref[...] = dx + res_ref[...] if has_res else dx

        @pl.when(pl.program_id(0) == 0)
        def _():
            dsh_ref[...] = jnp.zeros_like(dsh_ref)
            dsc_ref[...] = jnp.zeros_like(dsc_ref)

        dsh_ref[...] += dsh
        dsc_ref[...] += dsc

    row = pl.BlockSpec((br, D), lambda i: (i, 0))
    vec = pl.BlockSpec((1, D), lambda i: (0, 0))
    ins = [row, pl.BlockSpec((6, D), lambda i: (0, 0)), pl.BlockSpec((br, D), lambda i: (i + ob, 0))]
    args = [x, mod, dh]
    if has_res:
        ins.append(row)
        args.append(res)
    return _call(body, name=name, out_shape=(_sds((R, D)), _sds((1, D)), _sds((1, D))), grid=(R // br,),
                 in_specs=ins, out_specs=(row, vec, vec), sem=("arbitrary",))(*args)


def _rope(x, cos, sin):
    return x * cos + _swap32(x) * sin


def _aprep_fn(qs, ks, cos, sin, qw, kw):
    return ([_rope(_rms(q) * qw, cos, sin) for q in qs], [_rope(_rms(k) * kw, cos, sin) for k in ks])


def _aprep_fwd(proj, cos, sin, qw, kw, *, br=256):
    T = proj.shape[0]

    def body(x_ref, cos_ref, sin_ref, qw_ref, kw_ref, q_ref, k_ref, v_ref):
        qs = [x_ref[:, C_AQ + h * HD:C_AQ + (h + 1) * HD] for h in range(AH)]
        ks = [x_ref[:, h * HD:(h + 1) * HD] for h in range(AKV)]
        qo, ko = _aprep_fn(qs, ks, cos_ref[...], sin_ref[...], qw_ref[...], kw_ref[...])
        for h in range(AH):
            q_ref[h] = qo[h].astype(BF16)
        for h in range(AKV):
            k_ref[h] = ko[h].astype(BF16)
            v_ref[h] = x_ref[:, (AKV + h) * HD:(AKV + h + 1) * HD].astype(BF16)

    tab = pl.BlockSpec((br, HD), lambda i: (i, 0))
    vec = pl.BlockSpec((1, HD), lambda i: (0, 0))
    return _call(body, name="aprep_fwd",
                 out_shape=(_sds((AH, T, HD), BF16), _sds((AKV, T, HD), BF16), _sds((AKV, T, HD), BF16)),
                 grid=(T // br,),
                 in_specs=[pl.BlockSpec((br, C_QKV), lambda i: (i, 0)), tab, tab, vec, vec],
                 out_specs=(pl.BlockSpec((AH, br, HD), lambda i: (0, i, 0)),
                            pl.BlockSpec((AKV, br, HD), lambda i: (0, i, 0)),
                            pl.BlockSpec((AKV, br, HD), lambda i: (0, i, 0))),
                 sem=("parallel",))(proj, cos, sin, qw, kw)


def _aprep_bwd(proj, cos, sin, qw, kw, dq, dk, dv, dproj, L, *, br=256):
    T = proj.shape[0]
    lb = L // br

    def body(x_ref, cos_ref, sin_ref, qw_ref, kw_ref, dq_ref, dk_ref, dv_ref, _, dx_ref, dqw_ref, dkw_ref):
        i = pl.program_id(0)
        qs = [x_ref[:, C_AQ + h * HD:C_AQ + (h + 1) * HD] for h in range(AH)]
        ks = [x_ref[:, h * HD:(h + 1) * HD] for h in range(AKV)]
        _, vjp = jax.vjp(_aprep_fn, qs, ks, cos_ref[...], sin_ref[...], qw_ref[...], kw_ref[...])
        is_lat = i >= lb
        dqs = [jnp.where(is_lat, dq_ref[h], 0.0) for h in range(AH)]
        dks = [dk_ref[h] for h in range(AKV)]
        gq, gk, _, _, gqw, gkw = vjp((dqs, dks))
        for h in range(AH):
            dx_ref[:, C_AQ + h * HD:C_AQ + (h + 1) * HD] = gq[h].astype(BF16)
        for h in range(AKV):
            dx_ref[:, h * HD:(h + 1) * HD] = gk[h].astype(BF16)
            dx_ref[:, (AKV + h) * HD:(AKV + h + 1) * HD] = dv_ref[h].astype(BF16)

        @pl.when(i == 0)
        def _():
            dqw_ref[...] = jnp.zeros_like(dqw_ref)
            dkw_ref[...] = jnp.zeros_like(dkw_ref)

        dqw_ref[...] += gqw
        dkw_ref[...] += gkw

    tab = pl.BlockSpec((br, HD), lambda i: (i, 0))
    vec = pl.BlockSpec((1, HD), lambda i: (0, 0))
    kvb = pl.BlockSpec((AKV, br, HD), lambda i: (0, i, 0))
    blk = pl.BlockSpec((br, C_QKV), lambda i: (i, 0))
    return _call(body, name="aprep_bwd", out_shape=(_sds(dproj.shape, BF16), _sds((1, HD)), _sds((1, HD))),
                 grid=(T // br,),
                 in_specs=[blk, tab, tab, vec, vec,
                           pl.BlockSpec((AH, br, HD), lambda i: (0, jnp.maximum(i - lb, 0), 0)), kvb, kvb, ANYSPEC],
                 out_specs=(blk, vec, vec), aliases={8: 0},
                 sem=("arbitrary",))(proj, cos, sin, qw, kw, dq, dk, dv, dproj)


def _attn_fn(q, k, v):
    s = _dot(q, k, 1, 1) * (HD ** -0.5)
    m = jnp.max(s, axis=-1, keepdims=True)
    e = jnp.exp(s - m)
    l = jnp.sum(e, axis=-1, keepdims=True)
    return _dot(e / l, v, 1, 0), m + jnp.log(l)


def _attn_grad(q, k, v, o, lse, do):
    scale = HD ** -0.5
    p = jnp.exp(_dot(q, k, 1, 1) * scale - lse)
    dp = _dot(do, v, 1, 1)
    ds = p * (dp - jnp.sum(do * o, axis=-1, keepdims=True)) * scale
    return _dot(ds, k, 1, 0), _dot(ds, q, 0, 0), _dot(p, do, 0, 0)


def _attn_fwd(q, k, v, L, exch, *, bq=128):
    T = q.shape[1]
    N = T - L
    lb = L // bq

    def body(q_ref, k_ref, v_ref, o_ref, o32_ref, lse_ref):
        o, lse = _attn_fn(q_ref[...].reshape(GRP * bq, HD), k_ref[...], v_ref[...])
        for g in range(GRP):
            o_ref[:, g * HD:(g + 1) * HD] = o[g * bq:(g + 1) * bq].astype(BF16)
            o32_ref[:, g * HD:(g + 1) * HD] = o[g * bq:(g + 1) * bq]
        lse_ref[...] = jnp.broadcast_to(lse, (GRP * bq, HD)).reshape(GRP, bq, HD)

    kvb = pl.BlockSpec((None, T, HD), lambda g, i: (g, 0, 0))
    ob = pl.BlockSpec((bq, GRP * HD), lambda g, i: (i, g))
    return _call_carrying(
        body, exch, name="attn_fwd",
        out_shape=(_sds((N, AH * HD), BF16), _sds((N, AH * HD)), _sds((AH, N, HD))), grid=(AKV, N // bq),
        in_specs=[pl.BlockSpec((GRP, bq, HD), lambda g, i: (g, i + lb, 0)), kvb, kvb],
        out_specs=(ob, ob, pl.BlockSpec((GRP, bq, HD), lambda g, i: (g, i, 0))), vmem=VMEM_BIG)(q, k, v)


def _attn_bwd(q, k, v, o32, lse, do, L, *, bq=128):
    T = q.shape[1]
    N = T - L
    lb = L // bq

    def body(q_ref, k_ref, v_ref, o_ref, lse_ref, do_ref, dq_ref, dk_ref, dv_ref):
        rows = lambda r: jnp.concatenate([r[:, g * HD:(g + 1) * HD] for g in range(GRP)], axis=0)
        lse = jnp.max(lse_ref[...].reshape(GRP * bq, HD), axis=-1, keepdims=True)
        dq, dk, dv = _attn_grad(q_ref[...].reshape(GRP * bq, HD), k_ref[...], v_ref[...], rows(o_ref), lse, rows(do_ref))
        dq_ref[...] = dq.reshape(GRP, bq, HD)

        @pl.when(pl.program_id(1) == 0)
        def _():
            dk_ref[...] = jnp.zeros_like(dk_ref)
            dv_ref[...] = jnp.zeros_like(dv_ref)

        dk_ref[...] += dk
        dv_ref[...] += dv

    kvb = pl.BlockSpec((None, T, HD), lambda g, i: (g, 0, 0))
    qb = pl.BlockSpec((GRP, bq, HD), lambda g, i: (g, i + lb, 0))
    hb = pl.BlockSpec((GRP, bq, HD), lambda g, i: (g, i, 0))
    ob = pl.BlockSpec((bq, GRP * HD), lambda g, i: (i, g))
    return _call(body, name="attn_bwd",
                 out_shape=(_sds((AH, N, HD)), _sds((AKV, T, HD)), _sds((AKV, T, HD))), grid=(AKV, N // bq),
                 in_specs=[qb, kvb, kvb, ob, hb, ob], out_specs=(hb, kvb, kvb),
                 sem=("parallel", "arbitrary"), vmem=VMEM_BIG)(q, k, v, o32, lse, do)


def _gprep_fn(kind, shifts, x, w):
    down, up = shifts
    y = down(x) * w[0:1, :] + x * w[1:2, :] + up(x) * w[2:3, :]
    a = _silu(y)
    if kind == 2:
        return a
    a = a * lax.rsqrt(jnp.sum(a * a, axis=-1, keepdims=True) + EPS)
    return a * (HD ** -0.5) if kind == 0 else a


def _gprep_fwd(proj, conv_w, kind, bounds):
    T = proj.shape[0]
    shifts = _make_shift(bounds)
    cb = C_QKV // HD + kind * GH

    def body(x_ref, w_ref, o_ref):
        o_ref[...] = _gprep_fn(kind, shifts, x_ref[...], w_ref[...])

    return _call(body, name=f"gprep_fwd{kind}", out_shape=_sds((GH, T, HD)), grid=(GH,),
                 in_specs=[pl.BlockSpec((T, HD), lambda h: (0, cb + h)),
                           pl.BlockSpec((3, HD), lambda h: (0, kind * GH + h))],
                 out_specs=pl.BlockSpec((None, T, HD), lambda h: (h, 0, 0)), sem=("parallel",))(proj, conv_w)


def _gprep_bwd(proj, conv_w, kind, bounds, dy, dproj):
    T = proj.shape[0]
    shifts = _make_shift(bounds)
    cb = C_QKV // HD + kind * GH

    def body(x_ref, w_ref, dy_ref, _, dx_ref, dw_ref):
        _, vjp = jax.vjp(functools.partial(_gprep_fn, kind, shifts), x_ref[...], w_ref[...])
        dx, dw = vjp(dy_ref[0] + dy_ref[1])
        dx_ref[...] = dx.astype(BF16)
        dw_ref[...] = dw

    return _call(body, name=f"gprep_bwd{kind}", out_shape=(_sds(dproj.shape, BF16), _sds((3, GH * HD))), grid=(GH,),
                 in_specs=[pl.BlockSpec((T, HD), lambda h: (0, cb + h)),
                           pl.BlockSpec((3, HD), lambda h: (0, kind * GH + h)),
                           pl.BlockSpec((2, None, T, HD), lambda h: (0, h, 0, 0)), ANYSPEC],
                 out_specs=(pl.BlockSpec((T, HD), lambda h: (0, cb + h)), pl.BlockSpec((3, HD), lambda h: (0, h))),
                 aliases={3: 0}, sem=("parallel",))(proj, conv_w, dy, dproj)


def _bl_fn(x, alog, dtb):
    lane = lax.broadcasted_iota(jnp.int32, x.shape, 1)
    beta = jax.nn.sigmoid(x)
    z = x + dtb
    sp = jnp.maximum(z, 0.0) + jnp.log1p(jnp.exp(-jnp.abs(z)))
    la = -jnp.exp(alog) * sp
    return jnp.where(lane < 2 * GH, beta, jnp.where(lane < 4 * GH, la, 0.0))


def _bl_fwd(proj, alog, dtb, *, br=256):
    T = proj.shape[0]

    def body(x_ref, a_ref, d_ref, o_ref):
        o_ref[...] = _bl_fn(x_ref[...], a_ref[...], d_ref[...])

    vec = pl.BlockSpec((1, HD), lambda i: (0, 0))
    return _call(body, name="bl_fwd", out_shape=_sds((T, HD)), grid=(T // br,),
                 in_specs=[pl.BlockSpec((br, HD), lambda i: (i, C_BL // HD)), vec, vec],
                 out_specs=pl.BlockSpec((br, HD), lambda i: (i, 0)), sem=("parallel",))(proj, alog, dtb)


def _bl_bwd(proj, alog, dtb, dbl, dproj, *, br=256):
    T = proj.shape[0]
    wide = C_Z - C_BL

    def body(x_ref, a_ref, d_ref, g_ref, _, dx_ref, da_ref, dd_ref):
        g = g_ref[0, 0]
        for d in range(2):
            for h in range(GH):
                if d or h:
                    g = g + g_ref[d, h]
        _, vjp = jax.vjp(_bl_fn, x_ref[...], a_ref[...], d_ref[...])
        dx, da, dd = vjp(g)
        dx_ref[:, :HD] = dx.astype(BF16)
        dx_ref[:, HD:] = jnp.zeros((br, wide - HD), BF16)

        @pl.when(pl.program_id(0) == 0)
        def _():
            da_ref[...] = jnp.zeros_like(da_ref)
            dd_ref[...] = jnp.zeros_like(dd_ref)

        da_ref[...] += da
        dd_ref[...] += dd

    vec = pl.BlockSpec((1, HD), lambda i: (0, 0))
    return _call(body, name="bl_bwd", out_shape=(_sds(dproj.shape, BF16), _sds((1, HD)), _sds((1, HD))), grid=(T // br,),
                 in_specs=[pl.BlockSpec((br, HD), lambda i: (i, C_BL // HD)), vec, vec,
                           pl.BlockSpec((2, GH, br, HD), lambda i: (0, 0, i, 0)), ANYSPEC],
                 out_specs=(pl.BlockSpec((br, wide), lambda i: (i, C_BL // wide)), vec, vec), aliases={4: 0},
                 sem=("arbitrary",))(proj, alog, dtb, dbl, dproj)


def _chunk_masks(d):
    ii = lax.broadcasted_iota(jnp.int32, (CH, CH), 0)
    jj = lax.broadcasted_iota(jnp.int32, (CH, CH), 1)
    eye = (ii == jj).astype(F32)
    before = jnp.where(d == 0, (jj < ii).astype(F32), (jj > ii).astype(F32))
    return before, before + eye, eye


def _rows2(fwd, bwd):
    f = jax.custom_vjp(fwd)
    f.defvjp(lambda *a: (fwd(*a), None), lambda _, g: bwd(g))
    return f


_stack = _rows2(lambda a, b: jnp.concatenate([a, b], axis=0), lambda g: (g[:CH], g[CH:]))
_unstack = _rows2(lambda x: (x[:CH], x[CH:]), lambda g: (jnp.concatenate(g, axis=0),))
_fold = _rows2(lambda x: x[:CH] + x[CH:], lambda g: (jnp.concatenate([g, g], axis=0),))
_dup = _rows2(lambda a: jnp.concatenate([a, a], axis=0), lambda g: (g[:CH] + g[CH:],))
_lanes4 = _rows2(lambda x: tuple(x[:, i * HD:(i + 1) * HD] for i in range(4)),
                 lambda g: (jnp.concatenate(g, axis=1),))


def _pair_masks(d):
    ii = lax.broadcasted_iota(jnp.int32, (CH, 2 * CH), 0)
    ll = lax.broadcasted_iota(jnp.int32, (CH, 2 * CH), 1)
    jj = jnp.where(ll < CH, ll, ll - CH)
    eye2 = (ii == jj).astype(F32)
    before2 = jnp.where(d == 0, (jj < ii).astype(F32), (jj > ii).astype(F32))
    left = (ll < CH).astype(F32)
    r = lax.broadcasted_iota(jnp.int32, (2 * CH, 2 * CH), 0)
    c = lax.broadcasted_iota(jnp.int32, (2 * CH, 2 * CH), 1)
    bd = ((r < CH) == (c < CH)).astype(F32)
    ri, ci = jnp.where(r < CH, r, r - CH), jnp.where(c < CH, c, c - CH)
    bd_ateq = bd * jnp.where(d == 0, (ci <= ri).astype(F32), (ci >= ri).astype(F32))
    before1, ateq1, _ = _chunk_masks(d)
    return before2, before2 + eye2, eye2, left, ateq1, bd, bd_ateq


@jax.custom_vjp
def _saved_inverse2(lmat2, x2, bd):
    return x2


def _saved_inverse2_bwd(res, g):
    x2, bd = res
    hi = lax.Precision.HIGH
    t = lax.dot_general(x2, g, (((0,), (0,)), ((), ())), precision=hi, preferred_element_type=F32) * bd
    xb = jnp.concatenate([x2, x2], axis=0) * bd
    dl = lax.dot_general(t, xb, (((1,), (1,)), ((), ())), precision=hi, preferred_element_type=F32)
    return -(dl[:CH] + dl[CH:]), jnp.zeros_like(x2), jnp.zeros_like(bd)


_saved_inverse2.defvjp(lambda l, x, bd: (x, (x, bd)), _saved_inverse2_bwd)


def _intra_fn(masks, sel_b, sel_l, qs, ks, vs, bls, xs=None):
    before2, ateq2, eye2, left, ateq1, bd, bd_ateq = masks
    right = 1.0 - left
    ones1 = jnp.ones((CH, CH), F32)
    top = (lax.broadcasted_iota(jnp.int32, (2 * CH, 1), 0) < CH).astype(F32)
    inc2 = ateq2 > 0.0
    each = lambda f, *ls: [f(*t) for t in zip(*ls)]
    first, second = (lambda l: l[0::2]), (lambda l: l[1::2])
    beta = each(lambda bl: jnp.sum(bl * sel_b, axis=-1, keepdims=True), bls)
    la = each(lambda bl: jnp.sum(bl * sel_l, axis=-1, keepdims=True), bls)
    beta2 = each(lambda a, b: a * left + b * right, first(beta), second(beta))
    la2 = each(lambda a, b: a * left + b * right, first(la), second(la))
    la_rows = each(lambda a, b: _stack(jnp.broadcast_to(a, (CH, HD)), jnp.broadcast_to(b, (CH, HD))), first(la), second(la))
    beta_rows = each(_stack, first(beta), second(beta))
    k_rows, q_rows, v_rows = each(_stack, first(ks), second(ks)), each(_stack, first(qs), second(qs)), each(_stack, first(vs), second(vs))
    gam = each(lambda a: _mask_nn(bd_ateq, a), la_rows)
    gi = each(lambda a: _mask_nn(ateq1, a), la2)
    gj = each(lambda g: _mask_nn(ones1, eye2 * g), gi)
    kk = each(lambda k: _fold(_nt(k, k) * bd), k_rows)
    qk = each(lambda q, k: _fold(_nt(q, k) * bd), q_rows, k_rows)
    dec = each(lambda a, b: jnp.where(inc2, jnp.exp(jnp.where(inc2, a - b, 0.0)), 0.0), gi, gj)
    lmat = each(lambda b, d, m: before2 * (b * d * m), beta2, dec, kk)
    bdiag = lambda m: _dup(m) * bd
    if xs is None:
        x = each(lambda m: eye2 - m, lmat)
        p2 = each(lambda m: _mdot(m, bdiag(m)), lmat)
        for it in range(5):
            x = each(lambda a, b: a + _mdot(a, bdiag(b)), x, p2)
            if it < 4:
                p2 = each(lambda m: _mdot(m, bdiag(m)), p2)
    else:
        x = each(lambda m, s: _saved_inverse2(m, s, bd), lmat, xs)
    eg = each(jnp.exp, gam)
    rv = each(lambda b, v: b * v, beta_rows, v_rows)
    rk = each(lambda b, e, k: (b * e) * k, beta_rows, eg, k_rows)
    bot = 1.0 - top
    rhs = each(lambda a, b: jnp.concatenate([a * top, b * top, a * bot, b * bot], axis=1), rv, rk)
    uw = each(lambda a, r: _lanes4(_mdot(a, r)), x, rhs)
    tot = each(lambda a: jnp.sum(a, axis=0, keepdims=True), la)
    tot_rows = each(lambda a, b: _stack(jnp.broadcast_to(a, (CH, 1)), jnp.broadcast_to(b, (CH, 1))), first(tot), second(tot))
    kd = each(lambda k, t, g: _unstack(k * jnp.exp(t - g)), k_rows, tot_rows, gam)
    qd = each(lambda q, e: _unstack(q * e), q_rows, eg)
    gl = each(lambda t: jnp.broadcast_to(jnp.exp(t), (1, HD)), tot)
    p = each(lambda d, m: d * m, dec, qk)
    chunks = lambda pairs: [c for pr in pairs for c in pr]
    u = chunks([(t[0], t[2]) for t in uw])
    w = chunks([(t[1], t[3]) for t in uw])
    outs = (u, w, chunks(kd), chunks(qd), p, gl)
    return outs + (x,) if xs is None else outs


def _dir_head_sel(d, h):
    lane = lax.broadcasted_iota(jnp.int32, (1, HD), 1)
    return (lane == d * GH + h).astype(F32), (lane == 2 * GH + d * GH + h).astype(F32)


def _intra_specs(T, G):
    nc = T // CH
    assert nc % G == 0
    qkv = pl.BlockSpec((None, G * CH, HD), lambda d, h, c: (h, c, 0))
    bl = pl.BlockSpec((G * CH, HD), lambda d, h, c: (c, 0))
    big = pl.BlockSpec((None, None, G * CH, HD), lambda d, h, c: (d, h, c, 0))
    pm = pl.BlockSpec((None, None, G // 2 * CH, 2 * CH), lambda d, h, c: (d, h, c, 0))
    gl = pl.BlockSpec((None, None, G, 1, HD), lambda d, h, c: (d, h, c, 0, 0))
    shapes = (_sds((2, GH, T, HD)),) + (_sds((2, GH, T, HD), BF16),) * 3 + (
        _sds((2, GH, T // 2, 2 * CH), BF16), _sds((2, GH, nc, 1, HD)), _sds((2, GH, T // 2, 2 * CH)))
    return nc, qkv, bl, big, pm, gl, shapes


def _chunks_per_step(T, most):
    nc = T // CH
    return max(g for g in range(2, most + 1, 2) if nc % g == 0)


def _intra_fwd(q, k, v, bl, exch):
    T = q.shape[1]
    G = _chunks_per_step(T, INTRA_FWD_CHUNKS)
    nc, qkv_s, bl_s, big, pm, gl_s, shapes = _intra_specs(T, G)

    def body(q_ref, k_ref, v_ref, bl_ref, u_ref, w_ref, kd_ref, qd_ref, p_ref, gl_ref, x_ref):
        d, h = pl.program_id(0), pl.program_id(1)
        sb, sl = _dir_head_sel(d, h)
        rows = [slice(g * CH, (g + 1) * CH) for g in range(G)]
        u, w, kd, qd, p, gl, x = _intra_fn(_pair_masks(d), sb, sl,
                                           *[[r[s, :] for s in rows] for r in (q_ref, k_ref, v_ref, bl_ref)])
        for g in range(G):
            for r, o in zip((u_ref, w_ref, kd_ref, qd_ref), (u, w, kd, qd)):
                r[rows[g], :] = o[g].astype(r.dtype)
            gl_ref[g] = gl[g]
        for g in range(G // 2):
            p_ref[rows[g], :] = p[g].astype(p_ref.dtype)
            x_ref[rows[g], :] = x[g]

    return _call_carrying(body, exch, name="gdn_intra_fwd", out_shape=shapes, grid=(2, GH, nc // G),
                          in_specs=[qkv_s, qkv_s, qkv_s, bl_s], out_specs=(big, big, big, big, pm, gl_s, pm))(q, k, v, bl)


def _intra_bwd(q, k, v, bl, xinv, cts, exch):
    T = q.shape[1]
    G = _chunks_per_step(T, INTRA_BWD_CHUNKS)
    nc, qkv_s, bl_s, big, pm, gl_s, _ = _intra_specs(T, G)

    def body(q_ref, k_ref, v_ref, bl_ref, x_ref, du, dw, dkd, dqd, dp, dgl, dq_ref, dk_ref, dv_ref, dbl_ref):
        d, h = pl.program_id(0), pl.program_id(1)
        sb, sl = _dir_head_sel(d, h)
        rows = [slice(g * CH, (g + 1) * CH) for g in range(G)]
        fn = functools.partial(_intra_fn, _pair_masks(d), sb, sl, xs=[x_ref[s, :] for s in rows[:G // 2]])
        _, vjp = jax.vjp(fn, *[[r[s, :] for s in rows] for r in (q_ref, k_ref, v_ref, bl_ref)])
        cts = tuple([r[s, :] for s in rows] for r in (du, dw, dkd, dqd)) + (
            [dp[s, :] for s in rows[:G // 2]], [dgl[g] for g in range(G)])
        grads = vjp(cts)
        for g in range(G):
            for r, o in zip((dq_ref, dk_ref, dv_ref, dbl_ref), grads):
                r[rows[g], :] = o[g]

    return _call_carrying(body, exch, name="gdn_intra_bwd", out_shape=(_sds((2, GH, T, HD)),) * 4,
                          grid=(2, GH, nc // G), in_specs=[qkv_s, qkv_s, qkv_s, bl_s, pm, big, big, big, big, pm, gl_s],
                          out_specs=(big,) * 4)(q, k, v, bl, xinv, *cts)


def _scan_fn(half, s, u, w, kd, qd, p, gl):
    each = lambda f, *ls: [f(*t) for t in zip(*ls)]
    ws = each(_nn, w, s)
    delta = each(lambda a, b: a - b, u, ws)
    kdd = each(_tn, kd, delta)
    s_new = each(lambda g, a, b: g * a + b, gl, s, kdd)
    qs = each(_nn, qd, s)
    pd = each(lambda m, dl: _nn(m, _stack(dl * (1.0 - half), dl * half)), p, delta)
    return each(lambda a, b: a + b, qs, pd), s_new


SCAN_BLOCK = 4


def _scan_visit(t, d, nb, ncb):
    rev = jnp.where(t < ncb, ncb - 1 - t, nb - 1 - (t - ncb))
    return jnp.where(d == 0, t, rev)


def _scan_specs(T, L, back):
    tb = SCAN_BLOCK * CH
    assert T % tb == 0 and L % tb == 0
    nb, ncb = T // tb, L // tb

    def at(d, t):
        return _scan_visit(nb - 1 - t if back else t, d, nb, ncb)

    big = pl.BlockSpec((None, GH, tb, HD), lambda d, t: (d, 0, at(d, t), 0))
    pm = pl.BlockSpec((None, GH, tb // 2, 2 * CH), lambda d, t: (d, 0, at(d, t), 0))
    gl = pl.BlockSpec((None, GH, SCAN_BLOCK, 1, HD), lambda d, t: (d, 0, at(d, t), 0, 0))
    st = pl.BlockSpec((None, GH, SCAN_BLOCK, HD, HD), lambda d, t: (d, 0, at(d, t), 0, 0))
    do = pl.BlockSpec((GH, tb, HD), lambda d, t: (0, at(d, t), 0))
    return nb, big, pm, gl, st, do


def _scan_fwd(u, w, kd, qd, p, gl, L):
    T = u.shape[2]
    nb, big, pm, gl_s, st, _ = _scan_specs(T, L, False)
    heads = range(GH)

    def body(u_ref, w_ref, kd_ref, qd_ref, p_ref, gl_ref, o_ref, st_ref, s_scr):
        d = pl.program_id(0)

        @pl.when(pl.program_id(1) == 0)
        def _():
            s_scr[...] = jnp.zeros_like(s_scr)

        s = [s_scr[h] for h in heads]
        for i in range(SCAN_BLOCK):
            c = jnp.where(d == 0, i, SCAN_BLOCK - 1 - i)
            rows = pl.ds(pl.multiple_of(c * CH, CH), CH)
            pair = pl.ds(pl.multiple_of((c // 2) * CH, CH), CH)
            for h in heads:
                st_ref[h, c] = s[h]
            o, s = _scan_fn((c % 2).astype(F32), s,
                            *[[r[h, rows, :].astype(F32) for h in heads] for r in (u_ref, w_ref, kd_ref, qd_ref)],
                            [p_ref[h, pair, :].astype(F32) for h in heads], [gl_ref[h, c] for h in heads])
            for h in heads:
                o_ref[h, rows, :] = o[h]
        for h in heads:
            s_scr[h] = s[h]

    return _call(body, name="gdn_scan_fwd", out_shape=(_sds((2, GH, T, HD)), _sds((2, GH, T // CH, HD, HD))),
                 grid=(2, nb), in_specs=[big, big, big, big, pm, gl_s], out_specs=(big, st),
                 scratch=[pltpu.VMEM((GH, HD, HD), F32)], sem=("parallel", "arbitrary"))(u, w, kd, qd, p, gl)


def _scan_bwd(u, w, kd, qd, p, gl, states, do, L, exch):
    T = u.shape[2]
    nb, big, pm, gl_s, st, do_s = _scan_specs(T, L, True)
    heads = range(GH)

    def body(u_ref, w_ref, kd_ref, qd_ref, p_ref, gl_ref, st_ref, do_ref,
             du_ref, dw_ref, dkd_ref, dqd_ref, dp_ref, dgl_ref, ds_scr):
        d = pl.program_id(0)

        @pl.when(pl.program_id(1) == 0)
        def _():
            ds_scr[...] = jnp.zeros_like(ds_scr)

        ds = [ds_scr[h] for h in heads]
        for i in range(SCAN_BLOCK):
            c = jnp.where(d == 0, SCAN_BLOCK - 1 - i, i)
            rows = pl.ds(pl.multiple_of(c * CH, CH), CH)
            pair = pl.ds(pl.multiple_of((c // 2) * CH, CH), CH)
            _, vjp = jax.vjp(functools.partial(_scan_fn, (c % 2).astype(F32)), [st_ref[h, c] for h in heads],
                             *[[r[h, rows, :].astype(F32) for h in heads] for r in (u_ref, w_ref, kd_ref, qd_ref)],
                             [p_ref[h, pair, :].astype(F32) for h in heads], [gl_ref[h, c] for h in heads])
            ds, gu, gw, gkd, gqd, gp, ggl = vjp(([do_ref[h, rows, :] for h in heads], ds))
            for h in heads:
                du_ref[h, rows, :] = gu[h]
                dw_ref[h, rows, :] = gw[h]
                dkd_ref[h, rows, :] = gkd[h]
                dqd_ref[h, rows, :] = gqd[h]
                dgl_ref[h, c] = ggl[h]
                if i % 2 == 0:
                    dp_ref[h, pair, :] = gp[h]
                else:
                    dp_ref[h, pair, :] += gp[h]
        for h in heads:
            ds_scr[h] = ds[h]

    return _call_carrying(
        body, exch, name="gdn_scan_bwd",
        out_shape=(_sds((2, GH, T, HD)),) * 4 + (_sds((2, GH, T // 2, 2 * CH)), _sds((2, GH, T // CH, 1, HD))),
        grid=(2, nb), in_specs=[big, big, big, big, pm, gl_s, st, do_s], out_specs=(big, big, big, big, pm, gl_s),
        scratch=[pltpu.VMEM((GH, HD, HD), F32)])(u, w, kd, qd, p, gl, states, do)


def _gout_fn(o0, o1, z, gw):
    return _rms(o0 + o1) * gw * _silu(z)


def _gout_fwd(o, proj, gw, L):
    T = o.shape[2]
    N = T - L
    ob = pl.BlockSpec((2, None, T, HD), lambda h: (0, h, 0, 0))

    def body(o_ref, z_ref, gw_ref, y_ref):
        y_ref[...] = _gout_fn(o_ref[0, L:, :], o_ref[1, L:, :], z_ref[L:, :], gw_ref[...]).astype(BF16)

    return _call(body, name="gout_fwd", out_shape=_sds((N, GH * HD), BF16), grid=(GH,),
                 in_specs=[ob, pl.BlockSpec((T, HD), lambda h: (0, C_Z // HD + h)), pl.BlockSpec((1, HD), lambda h: (0, 0))],
                 out_specs=pl.BlockSpec((N, HD), lambda h: (0, h)), sem=("parallel",))(o, proj, gw)


def _gout_bwd(o, proj, gw, dy, dproj, L):
    T = o.shape[2]
    N = T - L
    ob = pl.BlockSpec((2, None, T, HD), lambda h: (0, h, 0, 0))

    def body(o_ref, z_ref, gw_ref, dy_ref, _, do_ref, dz_ref, dgw_ref):
        _, vjp = jax.vjp(_gout_fn, o_ref[0, L:, :], o_ref[1, L:, :], z_ref[L:, :], gw_ref[...])
        g0, _, gz, ggw = vjp(dy_ref[...])
        do_ref[:L, :] = jnp.zeros((L, HD), F32)
        do_ref[L:, :] = g0
        dz_ref[:L, :] = jnp.zeros((L, HD), BF16)
        dz_ref[L:, :] = gz.astype(BF16)

        @pl.when(pl.program_id(0) == 0)
        def _():
            dgw_ref[...] = jnp.zeros_like(dgw_ref)

        dgw_ref[...] += ggw

    zb = pl.BlockSpec((T, HD), lambda h: (0, C_Z // HD + h))
    return _call(body, name="gout_bwd", out_shape=(_sds((GH, T, HD)), _sds(dproj.shape, BF16), _sds((1, HD))),
                 grid=(GH,),
                 in_specs=[ob, zb, pl.BlockSpec((1, HD), lambda h: (0, 0)), pl.BlockSpec((N, HD), lambda h: (0, h)), ANYSPEC],
                 out_specs=(pl.BlockSpec((None, T, HD), lambda h: (h, 0, 0)), zb, pl.BlockSpec((1, HD), lambda h: (0, 0))),
                 aliases={4: 1}, sem=("arbitrary",))(o, proj, gw, dy, dproj)


def _merge_fn(pa, pd, ga, gd):
    return jax.nn.sigmoid(ga) * pa + jax.nn.sigmoid(gd) * pd


def _merge_fwd(pa, pd, proj, L, *, br=256):
    N = pa.shape[0]
    lb = L // br
    row = pl.BlockSpec((br, D), lambda i: (i, 0))

    def body(pa_ref, pd_ref, ga_ref, gd_ref, y_ref):
        y_ref[...] = _merge_fn(pa_ref[...], pd_ref[...], ga_ref[...], gd_ref[...]).astype(BF16)

    return _call(body, name="merge_fwd", out_shape=_sds((N, D), BF16), grid=(N // br,),
                 in_specs=[row, row, pl.BlockSpec((br, D), lambda i: (i + lb, C_GATE // D)),
                           pl.BlockSpec((br, D), lambda i: (i + lb, C_GATE // D + 1))],
                 out_specs=row, sem=("parallel",))(pa, pd, proj, proj)


def _merge_bwd(pa, pd, proj, dy, L, *, br=256):
    N = pa.shape[0]
    T = N + L
    lb = L // br
    lrow = pl.BlockSpec((br, D), lambda i: (jnp.maximum(i - lb, 0), 0))

    def body(pa_ref, pd_ref, ga_ref, gd_ref, dy_ref, dpa_ref, dpd_ref, dg_ref):
        lat = pl.program_id(0) >= lb
        _, vjp = jax.vjp(_merge_fn, pa_ref[...], pd_ref[...], ga_ref[...], gd_ref[...])
        gpa, gpd, gga, ggd = vjp(dy_ref[...])
        dpa_ref[...] = gpa.astype(BF16)
        dpd_ref[...] = gpd.astype(BF16)
        dg_ref[:, :D] = jnp.where(lat, gga, 0.0).astype(BF16)
        dg_ref[:, D:] = jnp.where(lat, ggd, 0.0).astype(BF16)

    return _call(body, name="merge_bwd", out_shape=(_sds((N, D), BF16), _sds((N, D), BF16), _sds((T, C_END), BF16)),
                 grid=(T // br,),
                 in_specs=[lrow, lrow, pl.BlockSpec((br, D), lambda i: (i, C_GATE // D)),
                           pl.BlockSpec((br, D), lambda i: (i, C_GATE // D + 1)), lrow],
                 out_specs=(lrow, lrow, pl.BlockSpec((br, 2 * D), lambda i: (i, C_GATE // (2 * D)))),
                 sem=("arbitrary",))(pa, pd, proj, proj, dy)


def _resid_fwd(x, m, mod, i_g, *, name, br=256):
    R = x.shape[0]
    row = pl.BlockSpec((br, D), lambda i: (i, 0))

    def body(x_ref, m_ref, mod_ref, o_ref):
        o_ref[...] = x_ref[...] + mod_ref[i_g:i_g + 1, :] * m_ref[...]

    return _call(body, name=name, out_shape=_sds((R, D)), grid=(R // br,),
                 in_specs=[row, row, pl.BlockSpec((6, D), lambda i: (0, 0))], out_specs=row,
                 sem=("parallel",))(x, m, mod)


def _resid_bwd(dx, m, mod, i_g, *, name, br=256):
    R = dx.shape[0]
    row = pl.BlockSpec((br, D), lambda i: (i, 0))
    vec = pl.BlockSpec((1, D), lambda i: (0, 0))

    def body(dx_ref, m_ref, mod_ref, dm_ref, dg_ref):
        dxv = dx_ref[...]
        dm_ref[...] = (dxv * mod_ref[i_g:i_g + 1, :]).astype(BF16)

        @pl.when(pl.program_id(0) == 0)
        def _():
            dg_ref[...] = jnp.zeros_like(dg_ref)

        dg_ref[...] += jnp.sum(dxv * m_ref[...], axis=0, keepdims=True)

    return _call(body, name=name, out_shape=(_sds((R, D), BF16), _sds((1, D))), grid=(R // br,),
                 in_specs=[row, row, pl.BlockSpec((6, D), lambda i: (0, 0))], out_specs=(row, vec),
                 sem=("arbitrary",))(dx, m, mod)


def _ffn_fn(shifts, ug, uv, wg, wv, bg, bv):
    down, up = shifts

    def conv(x, w, b):
        return down(x) * w[0:1, :] + x * w[1:2, :] + up(x) * w[2:3, :] + b

    return _silu(conv(ug, wg, bg)) * conv(uv, wv, bv)


def _ffn_fwd(up, cw, cb, *, bw=256):
    N = up.shape[0]
    shifts = _make_shift(((0, N),))
    nb = DFF // bw

    def body(ug, uv, wg, wv, bg, bv, a_ref):
        a_ref[...] = _ffn_fn(shifts, ug[...], uv[...], wg[...], wv[...], bg[...], bv[...]).astype(BF16)

    def col(rows, off):
        return pl.BlockSpec((rows, bw), lambda j: (0, j + off))

    return _call(body, name="ffn_fwd", out_shape=_sds((N, DFF), BF16), grid=(nb,),
                 in_specs=[col(N, 0), col(N, nb), col(3, 0), col(3, nb), col(1, 0), col(1, nb)],
                 out_specs=col(N, 0), sem=("parallel",), vmem=VMEM_BIG)(up, up, cw, cw, cb, cb)


def _ffn_bwd(up, cw, cb, da, *, bw=256):
    N = up.shape[0]
    shifts = _make_shift(((0, N),))
    nb = DFF // bw

    def body(ug, uv, wg, wv, bg, bv, da_ref, dug, duv, dwg, dwv, dbg, dbv):
        _, vjp = jax.vjp(functools.partial(_ffn_fn, shifts), ug[...], uv[...], wg[...], wv[...], bg[...], bv[...])
        g = vjp(da_ref[...])
        dug[...] = g[0].astype(BF16)
        duv[...] = g[1].astype(BF16)
        dwg[...], dwv[...], dbg[...], dbv[...] = g[2], g[3], g[4], g[5]

    def col(rows, off):
        return pl.BlockSpec((rows, bw), lambda j: (0, j + off))

    half = (_sds((N, DFF), BF16), _sds((N, DFF), BF16), _sds((3, DFF)), _sds((3, DFF)), _sds((1, DFF)), _sds((1, DFF)))
    dug, duv, dwg, dwv, dbg, dbv = _call(
        body, name="ffn_bwd", out_shape=half, grid=(nb,),
        in_specs=[col(N, 0), col(N, nb), col(3, 0), col(3, nb), col(1, 0), col(1, nb), col(N, 0)],
        out_specs=(col(N, 0), col(N, 0), col(3, 0), col(3, 0), col(1, 0), col(1, 0)),
        sem=("parallel",), vmem=VMEM_BIG)(up, up, cw, cw, cb, cb, da)
    return (jnp.concatenate([dug, duv], axis=1), jnp.concatenate([dwg, dwv], axis=1),
            jnp.concatenate([dbg, dbv], axis=1))


def _head_fn(x1, dn, g2, fw, tgt):
    y = _rms(x1 + g2 * dn) * fw
    err = y - tgt
    return 0.5 * jnp.sum(jnp.mean(err * err, axis=-1))


def _head(x1, dn, mod, fw, tgt, *, br=256):
    N = x1.shape[0]
    row = pl.BlockSpec((br, D), lambda i: (i, 0))
    vec = pl.BlockSpec((1, D), lambda i: (0, 0))
    one = pl.BlockSpec((1, HD), lambda i: (0, 0))

    def body(x1_ref, dn_ref, mod_ref, fw_ref, tgt_ref, loss_ref, dx_ref, ddn_ref, dg_ref, dfw_ref):
        loss, (gx, gdn, gg, gfw) = jax.value_and_grad(_head_fn, argnums=(0, 1, 2, 3))(
            x1_ref[...], dn_ref[...], mod_ref[5:6, :], fw_ref[...], tgt_ref[...])
        dx_ref[...] = gx
        ddn_ref[...] = gdn.astype(BF16)

        @pl.when(pl.program_id(0) == 0)
        def _():
            loss_ref[...] = jnp.zeros_like(loss_ref)
            dg_ref[...] = jnp.zeros_like(dg_ref)
            dfw_ref[...] = jnp.zeros_like(dfw_ref)

        loss_ref[...] += jnp.broadcast_to(loss, (1, HD))
        dg_ref[...] += gg
        dfw_ref[...] += gfw

    return _call(body, name="head", out_shape=(_sds((1, HD)), _sds((N, D)), _sds((N, D), BF16), _sds((1, D)), _sds((1, D))),
                 grid=(N // br,), in_specs=[row, row, pl.BlockSpec((6, D), lambda i: (0, 0)), vec, row],
                 out_specs=(one, row, row, vec, vec), sem=("arbitrary",))(x1, dn, mod, fw, tgt)


def _adamw(w, g, m, v, *, name):
    shape = w.shape
    cols = shape[-1]
    rows = max(1, math.prod(shape[:-1]))
    w2, g2, m2, v2 = (t.reshape(rows, cols) for t in (w, g, m, v))
    br = 256 if rows % 256 == 0 else (128 if rows % 128 == 0 else (8 if rows % 8 == 0 and rows > 64 else rows))
    if rows % 352 == 0:
        br = 352
    c1 = 1.0 - B1 ** STEP
    c2 = 1.0 - B2 ** STEP

    def body(w_ref, g_ref, m_ref, v_ref, d_ref, nm_ref, nv_ref):
        gv = g_ref[...]
        nm = B1 * m_ref[...] + (1.0 - B1) * gv
        nv = B2 * v_ref[...] + (1.0 - B2) * (gv * gv)
        d_ref[...] = -LR * ((nm / c1) / (jnp.sqrt(nv / c2) + AEPS) + WD * w_ref[...])
        nm_ref[...] = nm
        nv_ref[...] = nv

    blk = pl.BlockSpec((br, cols), lambda i: (i, 0))
    outs = _call(body, name=name, out_shape=(_sds((rows, cols)),) * 3, grid=(rows // br,),
                 in_specs=[blk] * 4, out_specs=(blk,) * 3, sem=("parallel",))(w2, g2, m2, v2)
    return tuple(t.reshape(shape) for t in outs)


def _rope_tables(N, L):
    t = jnp.arange(N)
    pos = jnp.stack([(t // GRID_W).astype(F32), (t % GRID_W).astype(F32)], axis=1)
    inv = ROPE_THETA ** (-jnp.arange(0, HD // 2, 2, dtype=F32) / (HD // 2))
    ang = pos[:, :, None] * inv[None, None, :]
    cos = jnp.broadcast_to(jnp.cos(ang)[:, :, None, :], (N, 2, 2, HD // 4)).reshape(N, HD)
    sin = jnp.broadcast_to(jnp.sin(ang)[:, :, None, :], (N, 2, 2, HD // 4))
    sin = (sin * jnp.array([-1.0, 1.0], F32)[None, None, :, None]).reshape(N, HD)
    cos = jnp.concatenate([jnp.ones((L, HD), F32), cos], axis=0)
    sin = jnp.concatenate([jnp.zeros((L, HD), F32), sin], axis=0)
    return cos, sin


def _pad_lanes(v, off=0):
    return jnp.zeros((1, HD), F32).at[0, off:off + v.shape[0]].set(v)


def _local_step(x, ctx, tgt, mod_lat, mod_ctx, w_in, shards, small):
    N, L = x.shape[0], ctx.shape[0]
    T = N + L
    bounds = ((0, L), (L, T))
    qw, kw, gw = small["q_norm_w"], small["k_norm_w"], small["gdn_norm_w"]
    conv_w, ffn_w, ffn_b, fnw = small["conv_qkv_w"], small["ffn_conv_w"], small["ffn_conv_b"], small["final_norm_w"]
    alog = _pad_lanes(small["a_log"].reshape(-1), 2 * GH)
    dtb = _pad_lanes(small["dt_bias"].reshape(-1), 2 * GH)
    cos, sin = _rope_tables(N, L)
    bt = T
    bnl = 256 if N % 1024 else 1024

    hc = _normmod_fwd(ctx, mod_ctx, 0, 1, name="normmod_ctx")
    hx = _normmod_fwd(x, mod_lat, 0, 1, name="normmod_x")
    h1 = jnp.concatenate([hc, hx], axis=0)
    proj = _mm(h1, w_in, name="mm_in", M=T, N=C_END, K=D, tb=True, bm=bt, bn=1024)
    aq, ak, av = _aprep_fwd(proj, cos, sin, qw, kw)
    (attn, attn32, lse), (up_g,) = _attn_fwd(aq, ak, av, L, _Exchange([shards["w_up"]], False))
    gq = _gprep_fwd(proj, conv_w, 0, bounds)
    gk = _gprep_fwd(proj, conv_w, 1, bounds)
    gv = _gprep_fwd(proj, conv_w, 2, bounds)
    bl = _bl_fwd(proj, alog, dtb)
    intra, (down_g, pa_g, pd_g, out_g) = _intra_fwd(
        gq, gk, gv, bl, _Exchange([shards[n] for n in ("w_down", "w_pa", "w_pd", "w_out")], False))
    w_up, w_down = up_g.reshape(2 * DFF, D), down_g.reshape(DFF, D)
    w_pa, w_pd, w_out = pa_g.reshape(D, D), pd_g.reshape(D, D), out_g.reshape(D, D)
    xinv, intra = intra[6], intra[:6]
    o, states = _scan_fwd(*intra, L)
    gdn = _gout_fwd(o, proj, gw, L)
    pa = _mm(attn, w_pa, name="mm_pa", M=N, N=D, K=D, bm=bnl)
    pd = _mm(gdn, w_pd, name="mm_pd", M=N, N=D, K=D, bm=bnl)
    y = _merge_fwd(pa, pd, proj, L)
    m = _mm(y, w_out, name="mm_out", M=N, N=D, K=D, bm=bnl)
    x1 = _resid_fwd(x, m, mod_lat, 2, name="resid1")
    h2 = _normmod_fwd(x1, mod_lat, 3, 4, name="normmod_x1")
    up = _mm(h2, w_up, name="mm_up", M=N, N=2 * DFF, K=D, tb=True, bm=bnl, bn=2 * DFF // 4)
    a = _ffn_fwd(up, ffn_w, ffn_b)
    dn = _mm(a, w_down, name="mm_down", M=N, N=D, K=DFF, bm=bnl)
    loss, dx2, ddn, dg2, dfnw = _head(x1, dn, mod_lat, fnw, tgt)

    da = _mm(ddn, w_down, name="mm_down_dx", M=N, N=DFF, K=D, tb=True, bm=bnl, bn=DFF // 2)
    g_down = _mm(a, ddn, name="mm_down_dw", M=DFF, N=D, K=N, ta=True, bm=DFF // 2, out_dtype=BF16)
    dup, d_ffn_w, d_ffn_b = _ffn_bwd(up, ffn_w, ffn_b, da)
    dh2 = _mm(dup, w_up, name="mm_up_dx", M=N, N=D, K=2 * DFF, bm=bnl, bk=2 * DFF // 4)
    g_up = _mm(dup, h2, name="mm_up_dw", M=2 * DFF, N=D, K=N, ta=True, bm=2 * DFF // 4, out_dtype=BF16)
    dx1, dsh2, dsc2 = _normmod_bwd(x1, mod_lat, 3, 4, dh2, 0, dx2, name="normmod_x1_bwd")
    dm, dg1 = _resid_bwd(dx1, m, mod_lat, 2, name="resid1_bwd")
    dy = _mm(dm, w_out, name="mm_out_dx", M=N, N=D, K=D, tb=True, bm=bnl)
    g_out = _mm(y, dm, name="mm_out_dw", M=D, N=D, K=N, ta=True, out_dtype=BF16)
    dpa, dpd, dproj = _merge_bwd(pa, pd, proj, dy, L)
    dattn = _mm(dpa, w_pa, name="mm_pa_dx", M=N, N=D, K=D, tb=True, bm=bnl)
    g_pa = _mm(attn, dpa, name="mm_pa_dw", M=D, N=D, K=N, ta=True, out_dtype=BF16)
    dgdn = _mm(dpd, w_pd, name="mm_pd_dx", M=N, N=D, K=D, tb=True, bm=bnl)
    g_pd = _mm(gdn, dpd, name="mm_pd_dw", M=D, N=D, K=N, ta=True, out_dtype=BF16)
    do, dproj, dgw = _gout_bwd(o, proj, gw, dgdn, dproj, L)
    cts, recv_a = _scan_bwd(*intra, states, do, L, _Exchange(
        [g_out.reshape(NDEV, D // NDEV, D), g_down.reshape(NDEV, DFF // NDEV, D)], True))
    (dgq, dgk, dgv, dbl), recv_b = _intra_bwd(gq, gk, gv, bl, xinv, cts, _Exchange(
        [g_pa.reshape(NDEV, D // NDEV, D), g_pd.reshape(NDEV, D // NDEV, D), g_up.reshape(NDEV, 2 * DFF // NDEV, D)], True))
    recv = dict(zip(("w_out", "w_down", "w_pa", "w_pd", "w_up"), recv_a + recv_b))
    dproj, dwq = _gprep_bwd(proj, conv_w, 0, bounds, dgq, dproj)
    dproj, dwk = _gprep_bwd(proj, conv_w, 1, bounds, dgk, dproj)
    dproj, dwv = _gprep_bwd(proj, conv_w, 2, bounds, dgv, dproj)
    dproj, dalog, ddtb = _bl_bwd(proj, alog, dtb, dbl, dproj)
    daq_h, dak_h, dav_h = _attn_bwd(aq, ak, av, attn32, lse, dattn, L)
    dproj, dqw, dkw = _aprep_bwd(proj, cos, sin, qw, kw, daq_h, dak_h, dav_h, dproj, L)
    g_in = _mm(dproj, h1, name="mm_in_dw", M=C_END, N=D, K=T, ta=True, bm=1024, out_dtype=BF16)
    g_in = _unpad_columns(g_in).reshape(NDEV, W_END // NDEV, D)
    own_in = lax.dynamic_index_in_dim(g_in, _position()[3], axis=0, keepdims=False)
    *pending, token = _scatter_start(g_in, None, (0, D // 2), (), name="scatter_g_in_a_start")
    dh1 = _mm(dproj, w_in, name="mm_in_dx", M=T, N=D, K=C_END, bm=bt, bk=1024, after=(token,))
    grad_x, dsh1, dsc1 = _normmod_bwd(x, mod_lat, 0, 1, dh1, L, dx1, name="normmod_x_bwd")
    _, dcsh1, dcsc1 = _normmod_bwd(ctx, mod_ctx, 0, 1, dh1, 0, None, name="normmod_ctx_bwd")

    z1 = jnp.zeros((1, D), F32)
    dmod_lat = jnp.concatenate([dsh1, dsc1, dg1, dsh2, dsc2, dg2], axis=0)
    dmod_ctx = jnp.concatenate([dcsh1, dcsc1, z1, z1, z1, z1], axis=0)
    gsmall = {
        "q_norm_w": dqw, "k_norm_w": dkw, "gdn_norm_w": dgw,
        "conv_qkv_w": jnp.concatenate([dwq, dwk, dwv], axis=1),
        "a_log": dalog[0, 2 * GH:4 * GH], "dt_bias": ddtb[0, 2 * GH:4 * GH],
        "ffn_conv_w": d_ffn_w, "ffn_conv_b": d_ffn_b, "final_norm_w": dfnw,
    }
    return loss[0, 0], grad_x, (pending, own_in), recv, dmod_lat, dmod_ctx, gsmall


HBM = pl.BlockSpec(memory_space=pltpu.HBM)
ANYSPEC = pl.BlockSpec(memory_space=pl.ANY)


def _position():
    x, y, c = lax.axis_index("x"), lax.axis_index("y"), lax.axis_index("c")
    return x, y, c, 4 * x + 2 * y + c


def _peer(x, y, c, k):
    px = 1 - x if k & 4 else x
    py = 1 - y if k & 2 else y
    pc = 1 - c if k & 1 else c
    return (px, py, pc), 4 * px + 2 * py + pc


def _exchange(arrs, *, name, scatter):
    exch = _Exchange(arrs, scatter)
    n = exch.n

    def body(*refs):
        ins, outs, sems = refs[:n], refs[n:2 * n], refs[2 * n:]
        exch.start(ins, outs, sems)
        exch.finish(ins, outs, sems)

    outs = pl.pallas_call(body, name=name, out_shape=exch.out_shape, in_specs=[HBM] * n, out_specs=(HBM,) * n,
                          scratch_shapes=exch.scratch,
                          compiler_params=pltpu.CompilerParams(has_side_effects=True))(*arrs)
    return list(outs)


class _Exchange:
    def __init__(self, arrs, scatter):
        self.arrs, self.scatter, self.n = list(arrs), scatter, len(arrs)
        self.out_shape = tuple(_sds(a.shape if scatter else (NDEV,) + a.shape, a.dtype) for a in arrs)
        self.scratch = [pltpu.SemaphoreType.DMA((self.n, NDEV - 1)), pltpu.SemaphoreType.DMA((self.n, NDEV - 1)),
                        pltpu.SemaphoreType.DMA((self.n,))]

    def _copies(self, ins, outs, sems):
        send, recv, loc = sems
        x, y, c, me = _position()
        local = [pltpu.make_async_copy(ins[a].at[me] if self.scatter else ins[a], outs[a].at[me], loc.at[a])
                 for a in range(self.n)]
        remote = []
        for k in range(1, NDEV):
            peer, pid = _peer(x, y, c, k)
            for a in range(self.n):
                src = ins[a].at[pid] if self.scatter else ins[a]
                remote.append(pltpu.make_async_remote_copy(
                    src_ref=src, dst_ref=outs[a].at[me], send_sem=send.at[a, k - 1], recv_sem=recv.at[a, k - 1],
                    device_id=peer, device_id_type=MESH))
        return local, remote

    def start(self, ins, outs, sems):
        local, remote = self._copies(ins, outs, sems)
        for cp in local + remote:
            cp.start()

    def finish(self, ins, outs, sems):
        local, remote = self._copies(ins, outs, sems)
        for cp in remote:
            cp.wait()
        for cp in local:
            cp.wait()


def _gather_two_level(block, *, name):
    def body(x_ref, out_ref, send_sems, recv_sems, local_sem):
        x, y, c, _ = _position()
        me, sibling = (x, y, c), (x, y, 1 - c)
        chips = [(1 - x, y), (x, 1 - y), (1 - x, 1 - y)]

        def slot(px, py, pc):
            return out_ref.at[4 * px + 2 * py + pc]

        def copy(k, owner, to, src=None):
            return pltpu.make_async_remote_copy(
                src_ref=slot(*owner) if src is None else src, dst_ref=slot(*owner), send_sem=send_sems.at[k],
                recv_sem=recv_sems.at[k], device_id=to, device_id_type=MESH)

        mine = pltpu.make_async_copy(x_ref, slot(*me), local_sem)
        mine.start()
        first = [copy(0, me, sibling, src=x_ref)]
        first += [copy(1 + j, me, (*chip, c), src=x_ref) for j, chip in enumerate(chips)]
        for cp in first:
            cp.start()
        passed = [copy(4 + j, (*chip, c), sibling) for j, chip in enumerate(chips)]
        for j, chip in enumerate(chips):
            copy(1 + j, (*chip, c), me).wait_recv()
            passed[j].start()
        copy(0, sibling, me).wait_recv()
        for j, chip in enumerate(chips):
            copy(4 + j, (*chip, 1 - c), me).wait_recv()
        for cp in first + passed:
            cp.wait_send()
        mine.wait()

    return pl.pallas_call(
        body, name=name, out_shape=_sds((NDEV,) + block.shape, block.dtype), in_specs=[HBM], out_specs=HBM,
        scratch_shapes=[pltpu.SemaphoreType.DMA((NDEV - 1,)), pltpu.SemaphoreType.DMA((NDEV - 1,)),
                        pltpu.SemaphoreType.DMA],
        compiler_params=pltpu.CompilerParams(has_side_effects=True))(block)


SEM = pl.BlockSpec(memory_space=pltpu.SEMAPHORE)


def _scatter_copies(src_ref, land_ref, send_sems, recv_sems, cols):
    x, y, c, me = _position()
    span = (slice(None), pl.ds(*cols))
    copies = []
    for k in range(1, NDEV):
        peer, pid = _peer(x, y, c, k)
        copies.append(pltpu.make_async_remote_copy(
            src_ref=src_ref.at[pid].at[span], dst_ref=land_ref.at[me].at[span], send_sem=send_sems.at[k - 1],
            recv_sem=recv_sems.at[k - 1], device_id=peer, device_id_type=MESH))
    return copies


SPLIT_EFFECT = pltpu.SideEffectType.DATAFLOW_SIDE_EFFECTING


def _scatter_start(parts, land, cols, after, *, name):
    na = len(after)
    if land is None:
        land = lax.empty(parts.shape, parts.dtype)

    def body(src_ref, land_ref, *rest):
        send_sems, recv_sems, _, _, token = rest[na:]
        for cp in _scatter_copies(src_ref, land_ref, send_sems, recv_sems, cols):
            cp.start()
        token[...] = jnp.zeros_like(token)

    return pl.pallas_call(
        body, name=name,
        out_shape=(pltpu.SemaphoreType.DMA((NDEV - 1,)), pltpu.SemaphoreType.DMA((NDEV - 1,)),
                   pltpu.HBM(parts.shape, parts.dtype), pltpu.HBM(parts.shape, parts.dtype), _sds((8, HD))),
        in_specs=(HBM, HBM) + (pl.BlockSpec(memory_space=pl.ANY),) * na,
        out_specs=(SEM, SEM, HBM, HBM, pl.BlockSpec(memory_space=pltpu.VMEM)),
        input_output_aliases={0: 2, 1: 3}, compiler_params=pltpu.CompilerParams(has_side_effects=SPLIT_EFFECT),
    )(pltpu.with_memory_space_constraint(parts, pltpu.HBM), pltpu.with_memory_space_constraint(land, pltpu.HBM), *after)


def _scatter_wait(send_sems, recv_sems, src_thru, land_thru, cols, after, *, name):
    na = len(after)

    def body(src_ref, land_ref, send_sems, recv_sems, *rest):
        for cp in _scatter_copies(src_ref, land_ref, send_sems, recv_sems, cols):
            cp.wait_send()
            cp.wait_recv()

    return pl.pallas_call(
        body, name=name,
        out_shape=(pltpu.HBM(src_thru.shape, src_thru.dtype), pltpu.HBM(land_thru.shape, land_thru.dtype)),
        in_specs=(HBM, HBM, SEM, SEM) + (pl.BlockSpec(memory_space=pl.ANY),) * na, out_specs=(HBM, HBM),
        input_output_aliases={0: 0, 1: 1}, compiler_params=pltpu.CompilerParams(has_side_effects=SPLIT_EFFECT),
    )(src_thru, land_thru, send_sems, recv_sems, *after)


def _cast_bf16(w, *, name):
    rows, cols = w.shape
    br = 128 if rows % 128 == 0 else rows

    def body(w_ref, o_ref):
        o_ref[...] = w_ref[...].astype(BF16)

    blk = pl.BlockSpec((br, cols), lambda i: (i, 0))
    return _call(body, name=name, out_shape=_sds((rows, cols), BF16), grid=(rows // br,), in_specs=[blk],
                 out_specs=blk, sem=("parallel",))(w)


def _sum_slots(a, *, name):
    _, R, C = a.shape

    def body(a_ref, o_ref):
        s = a_ref[0]
        for d in range(1, NDEV):
            s = s + a_ref[d]
        o_ref[...] = s

    return _call(body, name=name, out_shape=_sds((R, C)))(a)


MODROWS = 16


def _mod_fwd(c9, w, b):
    cols = w.shape[1]

    def body(c_ref, w_ref, b_ref, o_ref):
        o_ref[...] = _nn(_silu(c_ref[...]), w_ref[...]) + b_ref[...]

    return _call(body, name="mod_fwd", out_shape=_sds((MODROWS, cols)))(c9, w, b)


def _mod_bwd(c9, dmy, dall, w):
    cols = w.shape[1]

    def body(c_ref, dmy_ref, dall_ref, w_ref, gw_ref, gb_ref, cp_ref):
        sc = _silu(c_ref[...])
        rows = lax.broadcasted_iota(jnp.int32, (MODROWS, 1), 0)
        d = dmy_ref[...]
        d_ctx = jnp.where(rows == NDEV, d, 0.0)
        sc_ctx = jnp.where(rows == NDEV, sc, 0.0)
        outer = lax.dot_general(sc_ctx, d_ctx, (((0,), (0,)), ((), ())), precision=HI, preferred_element_type=F32)
        gw_ref[...] = _tn(jnp.where(rows < NDEV, sc, 0.0), jnp.where(rows < NDEV, d, 0.0)) + outer
        gb_ref[...] = jnp.sum(dall_ref[...], axis=0, keepdims=True)
        cp_ref[...] = jnp.sum(_nt(d_ctx, w_ref[...]), axis=0, keepdims=True)

    return _call(body, name="mod_bwd", out_shape=(_sds((D, cols)), _sds((1, 6 * D)), _sds((1, D))),
                 vmem=VMEM_BIG)(c9, dmy, dall, w)


def _cctx_finish(parts, c_ctx, after):
    VM = pl.BlockSpec(memory_space=pltpu.VMEM)

    def body(p_ref, c_ref, *rest):
        o_ref = rest[-1]
        s = p_ref[0]
        for d in range(1, NDEV):
            s = s + p_ref[d]
        _, vjp = jax.vjp(_silu, c_ref[...])
        o_ref[...] = vjp(s)[0]

    return _call(body, name="cctx_finish", out_shape=_sds((1, D)),
                 in_specs=[VM, VM] + [pl.BlockSpec(memory_space=pl.ANY)] * len(after))(parts, c_ctx, *after)


def _adamw_recv(w, recv, m, v, *, name, own=None):
    rows, cols = w.shape
    bc = 256
    c1 = 1.0 - B1 ** STEP
    c2 = 1.0 - B2 ** STEP
    has_own = own is not None

    def body(w_ref, r_ref, m_ref, v_ref, *rest):
        g_ref, d_ref, nm_ref, nv_ref = rest[-4:]
        me = _position()[3]

        def slot(d):
            return jnp.where(me == d, rest[0][...], r_ref[d]) if has_own else r_ref[d]

        gv = slot(0).astype(F32)
        for d in range(1, NDEV):
            gv = gv + slot(d).astype(F32)
        nm = B1 * m_ref[...] + (1.0 - B1) * gv
        nv = B2 * v_ref[...] + (1.0 - B2) * (gv * gv)
        g_ref[...] = gv
        d_ref[...] = -LR * ((nm / c1) / (jnp.sqrt(nv / c2) + AEPS) + WD * w_ref[...])
        nm_ref[...] = nm
        nv_ref[...] = nv

    blk = pl.BlockSpec((rows, bc), lambda j: (0, j))
    return _call(body, name=name, out_shape=(_sds((rows, cols)),) * 4, grid=(cols // bc,),
                 in_specs=[blk, pl.BlockSpec((NDEV, rows, bc), lambda j: (0, 0, j)), blk, blk] + [blk] * has_own,
                 out_specs=(blk,) * 4, sem=("parallel",), vmem=VMEM_BIG)(w, recv, m, v, *([own] if has_own else []))


P_LAT, P_CTX, P_FNW, P_FFNB, P_CONV, P_FFNW, P_MISC, P_ROWS = 0, 8, 16, 24, 32, 48, 72, 80


def _rows_of(v, nrows):
    flat = v.reshape(-1)
    return jnp.pad(flat, (0, nrows * D - flat.shape[0])).reshape(nrows, D)


def _by_columns(g):
    n, r, c = g.shape
    return jnp.transpose(g, (1, 0, 2)).reshape(r, n * c)


def kernel(x, c, ctx, c_ctx, w_mod, b_mod, w_in, q_norm_w, k_norm_w, conv_qkv_w, a_log, dt_bias, gdn_norm_w, w_pa, w_pd, w_out, w_up, ffn_conv_w, ffn_conv_b, w_down, final_norm_w, loss_target, m_c_ctx, m_w_mod, m_b_mod, m_w_in, m_q_norm_w, m_k_norm_w, m_conv_qkv_w, m_a_log, m_dt_bias, m_gdn_norm_w, m_w_pa, m_w_pd, m_w_out, m_w_up, m_ffn_conv_w, m_ffn_conv_b, m_w_down, m_final_norm_w, v_c_ctx, v_w_mod, v_b_mod, v_w_in, v_q_norm_w, v_k_norm_w, v_conv_qkv_w, v_a_log, v_dt_bias, v_gdn_norm_w, v_w_pa, v_w_pd, v_w_out, v_w_up, v_ffn_conv_w, v_ffn_conv_b, v_w_down, v_final_norm_w):
    _, _, _, me = _position()
    mcols = w_mod.shape[2]

    transposed = ("w_in", "w_up")
    big = {"w_in": w_in[0].T, "w_pa": w_pa[0], "w_pd": w_pd[0], "w_out": w_out[0], "w_up": w_up[0].T, "w_down": w_down[0]}
    names = list(big)
    shards = {n: _cast_bf16(big[n], name="cast_" + n) for n in names}
    w_in_g = _gather_two_level(shards["w_in"], name="gather_w_in")
    c_all, conv_g, ffnw_g = _exchange([c, conv_qkv_w[0], ffn_conv_w[0]], name="gather_small", scatter=False)
    w_in_full = w_in_g.reshape(W_END, D)
    w_in_pad = _pad_columns(w_in_full)

    c9 = jnp.concatenate([c_all.reshape(NDEV, D), jnp.pad(c_ctx[None], ((0, MODROWS - NDEV - 1), (0, 0)))], axis=0)
    b_loc = lax.dynamic_slice(b_mod, (0, me * mcols), (1, mcols))
    mod_all, = _exchange([_mod_fwd(c9, w_mod[0], b_loc)], name="gather_mod", scatter=False)
    mod_lat = lax.dynamic_index_in_dim(mod_all, me, axis=1, keepdims=False).reshape(6, D)
    mod_ctx = mod_all[:, NDEV, :].reshape(6, D)

    small = {"q_norm_w": q_norm_w, "k_norm_w": k_norm_w, "gdn_norm_w": gdn_norm_w, "a_log": a_log, "dt_bias": dt_bias,
             "conv_qkv_w": _by_columns(conv_g), "ffn_conv_w": _by_columns(ffnw_g), "ffn_conv_b": ffn_conv_b,
             "final_norm_w": final_norm_w[None]}
    loss_me, grad_x, (pending_in, own_in), recv, dmod_lat, dmod_ctx, gs = _local_step(
        x[0], ctx[0], loss_target[0], mod_lat, mod_ctx, w_in_pad, shards, small)

    moments = {"w_in": (m_w_in, v_w_in), "w_pa": (m_w_pa, v_w_pa), "w_pd": (m_w_pd, v_w_pd),
               "w_out": (m_w_out, v_w_out), "w_up": (m_w_up, v_w_up), "w_down": (m_w_down, v_w_down)}
    res = {}
    def finish(n, outs):
        return tuple((t.T if n in transposed else t)[None] for t in outs)

    def moment(t, n):
        return t[0].T if n in transposed else t[0]

    for n in recv:
        res[n] = finish(n, _adamw_recv(big[n], recv[n], moment(moments[n][0], n), moment(moments[n][1], n),
                                       name="adamw_" + n))

    misc = jnp.concatenate([gs["q_norm_w"][0], gs["k_norm_w"][0], gs["gdn_norm_w"][0], gs["a_log"], gs["dt_bias"],
                            loss_me[None]])
    pack = jnp.concatenate([_rows_of(dmod_lat, P_CTX - P_LAT), _rows_of(dmod_ctx, P_FNW - P_CTX),
                            _rows_of(gs["final_norm_w"], P_FFNB - P_FNW), _rows_of(gs["ffn_conv_b"], P_CONV - P_FFNB),
                            _rows_of(gs["conv_qkv_w"], P_FFNW - P_CONV), _rows_of(gs["ffn_conv_w"], P_MISC - P_FFNW),
                            _rows_of(misc, P_ROWS - P_MISC)], axis=0)
    pack_all, = _exchange([pack], name="gather_pack", scatter=False)
    tot = _sum_slots(pack_all, name="sum_pack")
    dall = jnp.concatenate([pack_all[:, P_LAT:P_LAT + 6, :].reshape(NDEV, 6 * D),
                            jnp.pad(tot[P_CTX:P_CTX + 6].reshape(1, 6 * D), ((0, MODROWS - NDEV - 1), (0, 0)))], axis=0)
    dmy = lax.dynamic_slice(dall, (0, me * mcols), (MODROWS, mcols))
    g_w_mod, g_b_mod, cpart = _mod_bwd(c9, dmy, dall, w_mod[0])
    cparts, = _exchange([cpart], name="gather_cctx", scatter=False)
    sems_a, land = pending_in[:2], pending_in[3]
    *sems_b, g_in_thru, land, token_b = _scatter_start(pending_in[2], land, (D // 2, D // 2), (cparts,),
                                                       name="scatter_g_in_b_start")
    g_c_ctx = _cctx_finish(cparts, c_ctx[None], (token_b,))[0]

    nconv, nffn = 3 * GH * HD, 2 * DFF
    conv_tot = tot[P_CONV:P_FFNW].reshape(-1)[:3 * nconv].reshape(3, nconv)
    ffnw_tot = tot[P_FFNW:P_MISC].reshape(-1)[:3 * nffn].reshape(3, nffn)
    mrow = tot[P_MISC]
    grads = {
        "c_ctx": g_c_ctx, "w_mod": g_w_mod[None], "b_mod": g_b_mod,
        "q_norm_w": mrow[None, 0:HD], "k_norm_w": mrow[None, HD:2 * HD], "gdn_norm_w": mrow[None, 2 * HD:3 * HD],
        "conv_qkv_w": lax.dynamic_slice(conv_tot, (0, me * (nconv // NDEV)), (3, nconv // NDEV))[None],
        "a_log": mrow[3 * HD:3 * HD + 2 * GH].reshape(1, 2, GH),
        "dt_bias": mrow[3 * HD + 2 * GH:3 * HD + 4 * GH].reshape(1, 2, GH),
        "ffn_conv_w": lax.dynamic_slice(ffnw_tot, (0, me * (nffn // NDEV)), (3, nffn // NDEV))[None],
        "ffn_conv_b": tot[P_FFNB:P_CONV].reshape(-1)[:nffn][None],
        "final_norm_w": tot[P_FNW],
    }
    loss = mrow[3 * HD + 4 * GH]
    given = {"c_ctx": (c_ctx, m_c_ctx, v_c_ctx), "w_mod": (w_mod, m_w_mod, v_w_mod), "b_mod": (b_mod, m_b_mod, v_b_mod),
             "q_norm_w": (q_norm_w, m_q_norm_w, v_q_norm_w), "k_norm_w": (k_norm_w, m_k_norm_w, v_k_norm_w),
             "conv_qkv_w": (conv_qkv_w, m_conv_qkv_w, v_conv_qkv_w), "a_log": (a_log, m_a_log, v_a_log),
             "dt_bias": (dt_bias, m_dt_bias, v_dt_bias), "gdn_norm_w": (gdn_norm_w, m_gdn_norm_w, v_gdn_norm_w),
             "ffn_conv_w": (ffn_conv_w, m_ffn_conv_w, v_ffn_conv_w), "ffn_conv_b": (ffn_conv_b, m_ffn_conv_b, v_ffn_conv_b),
             "final_norm_w": (final_norm_w, m_final_norm_w, v_final_norm_w)}
    for n, (w, m, v) in given.items():
        res[n] = (grads[n],) + _adamw(w, grads[n], m, v, name="adamw_" + n)

    g_in_thru, land = _scatter_wait(*sems_a, g_in_thru, land, (0, D // 2), [res[n][1] for n in res],
                                    name="scatter_g_in_a_wait")
    _, land = _scatter_wait(*sems_b, g_in_thru, land, (D // 2, D // 2), (), name="scatter_g_in_b_wait")
    res["w_in"] = finish("w_in", _adamw_recv(big["w_in"], land, moment(m_w_in, "w_in"), moment(v_w_in, "w_in"),
                                             name="adamw_w_in", own=own_in))

    order = ["c_ctx", "w_mod", "b_mod", "w_in", "q_norm_w", "k_norm_w", "conv_qkv_w", "a_log", "dt_bias", "gdn_norm_w",
             "w_pa", "w_pd", "w_out", "w_up", "ffn_conv_w", "ffn_conv_b", "w_down", "final_norm_w"]
    return (loss, grad_x[None], *[res[n][0] for n in order], *[res[n][1] for n in order],
            *[res[n][2] for n in order], *[res[n][3] for n in order])
```

```python
import functools
import math

import jax
import jax.numpy as jnp
from jax import lax
from jax.experimental import pallas as pl
from jax.experimental.pallas import tpu as pltpu

F32 = jnp.float32
BF16 = jnp.bfloat16
HI = lax.Precision.HIGHEST
MESH = pl.DeviceIdType.MESH

NDEV = 8
D = 1024
HD = 128
AH, AKV, GRP = 8, 2, 4
GH = 8
CH = 64
DFF = 2816
GRID_W = 64
EPS = 1e-6
ROPE_THETA = 10000.0
C_KV, C_AQ, C_QKV, C_BL, C_Z, C_GATE, C_END = 0, 512, 1536, 4608, 5120, 6144, 8192
W_QKV, W_AQ, W_Z, W_END = 512, 3616, 4640, 7712


def _pad_columns(w):
    zeros = jnp.zeros((C_Z - C_QKV - (W_AQ - W_QKV), D), w.dtype)
    return jnp.concatenate([w[:W_QKV], w[W_AQ:W_Z], w[W_QKV:W_AQ], zeros, w[W_Z:]], axis=0)


def _unpad_columns(g):
    return jnp.concatenate([g[:C_AQ], g[C_QKV:C_QKV + W_AQ - W_QKV], g[C_AQ:C_QKV], g[C_Z:]], axis=0)
LR, B1, B2, AEPS, WD, STEP = 0.001, 0.9, 0.999, 1e-08, 0.01, 10
VMEM_BIG = 56 * 1024 * 1024
INTRA_FWD_CHUNKS = 36
INTRA_BWD_CHUNKS = 36


def _call(body, *, name, out_shape, grid=None, in_specs=None, out_specs=None, scratch=(), sem=None,
          vmem=None, aliases=None):
    params = {}
    if sem is not None:
        params["dimension_semantics"] = sem
    if vmem is not None:
        params["vmem_limit_bytes"] = vmem
    kw = {}
    if grid is not None:
        kw["grid"] = grid
    if in_specs is not None:
        kw["in_specs"] = in_specs
    if out_specs is not None:
        kw["out_specs"] = out_specs
    if aliases:
        kw["input_output_aliases"] = aliases
    return pl.pallas_call(body, name=name, out_shape=out_shape, scratch_shapes=list(scratch),
                          compiler_params=pltpu.CompilerParams(**params), **kw)


def _call_carrying(body, exch, *, name, out_shape, grid, in_specs, out_specs, scratch=(), vmem=None):
    n, nin, nout, nscr = exch.n, len(in_specs), len(out_shape), len(scratch)
    steps = math.prod(grid)
    mid = (2 * steps) // 3

    def wrapped(*refs):
        ins, cins = refs[:nin], refs[nin:nin + n]
        outs, couts = refs[nin + n:nin + n + nout], refs[nin + n + nout:nin + 2 * n + nout]
        scr, sems = refs[nin + 2 * n + nout:nin + 2 * n + nout + nscr], refs[nin + 2 * n + nout + nscr:]
        ids = [pl.program_id(i) for i in range(len(grid))]
        first = functools.reduce(jnp.logical_and, [i == 0 for i in ids])
        last = functools.reduce(jnp.logical_and, [i == g - 1 for i, g in zip(ids, grid)])

        @pl.when(first)
        def _():
            exch.start(cins, couts, sems)

        if hasattr(exch, "middle"):
            linear = functools.reduce(lambda acc, ig: acc * ig[1] + ig[0], zip(ids, grid), 0)

            @pl.when(linear == mid)
            def _():
                exch.middle(cins, couts, sems)

        body(*ins, *outs, *scr)

        @pl.when(last)
        def _():
            exch.finish(cins, couts, sems)

    params = {"dimension_semantics": ("arbitrary",) * len(grid)}
    if vmem is not None:
        params["vmem_limit_bytes"] = vmem
    fn = pl.pallas_call(wrapped, name=name, out_shape=tuple(out_shape) + exch.out_shape, grid=grid,
                        in_specs=list(in_specs) + [HBM] * n, out_specs=tuple(out_specs) + (HBM,) * n,
                        scratch_shapes=list(scratch) + exch.scratch, compiler_params=pltpu.CompilerParams(**params))

    def run(*args):
        res = fn(*args, *exch.arrs)
        return res[:nout], list(res[nout:])

    return run


def _sds(shape, dtype=F32):
    return jax.ShapeDtypeStruct(tuple(shape), dtype)


def _dot(a, b, ca, cb):
    return lax.dot_general(a.astype(BF16), b.astype(BF16), (((ca,), (cb,)), ((), ())),
                           preferred_element_type=F32)


@jax.custom_vjp
def _nn(a, b):
    return _dot(a, b, 1, 0)


@jax.custom_vjp
def _nt(a, b):
    return _dot(a, b, 1, 1)


@jax.custom_vjp
def _tn(a, b):
    return _dot(a, b, 0, 0)


_nn.defvjp(lambda a, b: (_nn(a, b), (a, b)), lambda r, g: (_nt(g, r[1]), _tn(r[0], g)))
_nt.defvjp(lambda a, b: (_nt(a, b), (a, b)), lambda r, g: (_nn(g, r[1]), _tn(g, r[0])))
_tn.defvjp(lambda a, b: (_tn(a, b), (a, b)), lambda r, g: (_nt(r[1], g), _nn(r[0], g)))


def _hdot(a, b):
    return jnp.dot(a, b, precision=HI, preferred_element_type=F32)


def _mdot(a, b):
    return jnp.dot(a, b, precision=lax.Precision.HIGH, preferred_element_type=F32)


def _maskdot(mask, a, cm):
    hi = a.astype(BF16)
    r = a - hi.astype(F32)
    mid = r.astype(BF16)
    lo = (r - mid.astype(F32)).astype(BF16)
    mb = mask.astype(BF16)
    dims = (((cm,), (0,)), ((), ()))
    return (lax.dot_general(mb, hi, dims, preferred_element_type=F32)
            + lax.dot_general(mb, mid, dims, preferred_element_type=F32)
            + lax.dot_general(mb, lo, dims, preferred_element_type=F32))


@jax.custom_vjp
def _mask_nn(mask, a):
    return _maskdot(mask, a, 1)


_mask_nn.defvjp(lambda mask, a: (_maskdot(mask, a, 1), mask),
                lambda mask, g: (jnp.zeros_like(mask), _maskdot(mask, g, 0)))


@jax.custom_vjp
def _saved_inverse(lmat, x):
    return x


def _saved_inverse_bwd(x, g):
    t = lax.dot_general(x, g, (((0,), (0,)), ((), ())), precision=lax.Precision.HIGH, preferred_element_type=F32)
    dl = lax.dot_general(t, x, (((1,), (1,)), ((), ())), precision=lax.Precision.HIGH, preferred_element_type=F32)
    return -dl, jnp.zeros_like(x)


_saved_inverse.defvjp(lambda lmat, x: (x, x), _saved_inverse_bwd)


def _row_ids(shape):
    return lax.broadcasted_iota(jnp.int32, shape, 0)


def _shift_rows(x, down, bounds):
    n = x.shape[0]
    rows = _row_ids(x.shape)
    y = pltpu.roll(x, 1 if down else n - 1, 0)
    edge = functools.reduce(jnp.logical_or, [rows == (s if down else e - 1) for s, e in bounds])
    return jnp.where(edge, 0.0, y)


def _make_shift(bounds):
    @jax.custom_vjp
    def down(x):
        return _shift_rows(x, True, bounds)

    @jax.custom_vjp
    def up(x):
        return _shift_rows(x, False, bounds)

    down.defvjp(lambda x: (down(x), None), lambda _, g: (up(g),))
    up.defvjp(lambda x: (up(x), None), lambda _, g: (down(g),))
    return down, up


@jax.custom_vjp
def _swap32(x):
    lane = lax.broadcasted_iota(jnp.int32, x.shape, x.ndim - 1)
    return jnp.where((lane % 64) < 32, pltpu.roll(x, HD - 32, x.ndim - 1), pltpu.roll(x, 32, x.ndim - 1))


_swap32.defvjp(lambda x: (_swap32(x), None), lambda _, g: (_swap32(g),))


def _rms(x):
    return x * lax.rsqrt(jnp.mean(x * x, axis=-1, keepdims=True) + EPS)


def _silu(x):
    return x * jax.nn.sigmoid(x)


def _mm(a, b, *, name, M, N, K, ta=False, tb=False, out_dtype=F32, bm=None, bn=None, bk=None,
        a_off=(0, 0), b_off=(0, 0), after=()):
    bm, bn, bk = bm or M, bn or N, bk or K
    assert M % bm == 0 and N % bn == 0 and K % bk == 0, (name, M, N, K, bm, bn, bk)
    nk = K // bk
    ca, cb = (0 if ta else 1), (1 if tb else 0)
    na = len(after)

    def body(a_ref, b_ref, *rest):
        o_ref, acc = rest[na], rest[na + 1:]
        r = _dot(a_ref[...], b_ref[...], ca, cb)
        if nk == 1:
            o_ref[...] = r.astype(out_dtype)
        else:
            acc_ref, = acc
            k = pl.program_id(2)

            @pl.when(k == 0)
            def _():
                acc_ref[...] = r

            @pl.when(k > 0)
            def _():
                acc_ref[...] += r

            @pl.when(k == nk - 1)
            def _():
                o_ref[...] = acc_ref[...].astype(out_dtype)

    def blk(off, bshape):
        assert off[0] % bshape[0] == 0 and off[1] % bshape[1] == 0, (name, off, bshape)
        return off[0] // bshape[0], off[1] // bshape[1]

    if ta:
        ao = blk(a_off, (bk, bm))
        a_spec = pl.BlockSpec((bk, bm), lambda i, j, k: (k + ao[0], i + ao[1]))
    else:
        ao = blk(a_off, (bm, bk))
        a_spec = pl.BlockSpec((bm, bk), lambda i, j, k: (i + ao[0], k + ao[1]))
    if tb:
        bo = blk(b_off, (bn, bk))
        b_spec = pl.BlockSpec((bn, bk), lambda i, j, k: (j + bo[0], k + bo[1]))
    else:
        bo = blk(b_off, (bk, bn))
        b_spec = pl.BlockSpec((bk, bn), lambda i, j, k: (k + bo[0], j + bo[1]))
    return _call(body, name=name, out_shape=_sds((M, N), out_dtype), grid=(M // bm, N // bn, nk),
                 in_specs=[a_spec, b_spec] + [pl.BlockSpec(memory_space=pl.ANY)] * na,
                 out_specs=pl.BlockSpec((bm, bn), lambda i, j, k: (i, j)),
                 scratch=[pltpu.VMEM((bm, bn), F32)] if nk > 1 else [],
                 sem=("parallel", "parallel", "arbitrary"), vmem=VMEM_BIG)(a, b, *after)


def _normmod_fn(x, sh, sc):
    return _rms(x) * (1.0 + sc) + sh


def _normmod_fwd(x, mod, i_sh, i_sc, *, name, br=256):
    R = x.shape[0]

    def body(x_ref, mod_ref, o_ref):
        o_ref[...] = _normmod_fn(x_ref[...], mod_ref[i_sh:i_sh + 1, :], mod_ref[i_sc:i_sc + 1, :]).astype(BF16)

    return _call(body, name=name, out_shape=_sds((R, D), BF16), grid=(R // br,),
                 in_specs=[pl.BlockSpec((br, D), lambda i: (i, 0)), pl.BlockSpec((6, D), lambda i: (0, 0))],
                 out_specs=pl.BlockSpec((br, D), lambda i: (i, 0)), sem=("parallel",))(x, mod)


def _normmod_bwd(x, mod, i_sh, i_sc, dh, dh_off, res, *, name, br=256):
    R = x.shape[0]
    ob = dh_off // br
    has_res = res is not None

    def body(x_ref, mod_ref, dh_ref, *rest):
        if has_res:
            res_ref, dx_ref, dsh_ref, dsc_ref = rest
        else:
            dx_ref, dsh_ref, dsc_ref = rest
        sh, sc = mod_ref[i_sh:i_sh + 1, :], mod_ref[i_sc:i_sc + 1, :]
        _, vjp = jax.vjp(_normmod_fn, x_ref[...], sh, sc)
        dx, dsh, dsc = vjp(dh_ref[...])
        dx_ref[...] = dx + res_ref[...] if has_res else dx

        @pl.when(pl.program_id(0) == 0)
        def _():
            dsh_ref[...] = jnp.zeros_like(dsh_ref)
            dsc_ref[...] = jnp.zeros_like(dsc_ref)

        dsh_ref[...] += dsh
        dsc_ref[...] += dsc

    row = pl.BlockSpec((br, D), lambda i: (i, 0))
    vec = pl.BlockSpec((1, D), lambda i: (0, 0))
    ins = [row, pl.BlockSpec((6, D), lambda i: (0, 0)), pl.BlockSpec((br, D), lambda i: (i + ob, 0))]
    args = [x, mod, dh]
    if has_res:
        ins.append(row)
        args.append(res)
    return _call(body, name=name, out_shape=(_sds((R, D)), _sds((1, D)), _sds((1, D))), grid=(R // br,),
                 in_specs=ins, out_specs=(row, vec, vec), sem=("arbitrary",))(*args)


def _rope(x, cos, sin):
    return x * cos + _swap32(x) * sin


def _aprep_fn(qs, ks, cos, sin, qw, kw):
    return ([_rope(_rms(q) * qw, cos, sin) for q in qs], [_rope(_rms(k) * kw, cos, sin) for k in ks])


def _aprep_fwd(proj, cos, sin, qw, kw, *, br=256):
    T = proj.shape[0]

    def body(x_ref, cos_ref, sin_ref, qw_ref, kw_ref, q_ref, k_ref, v_ref):
        qs = [x_ref[:, C_AQ + h * HD:C_AQ + (h + 1) * HD] for h in range(AH)]
        ks = [x_ref[:, h * HD:(h + 1) * HD] for h in range(AKV)]
        qo, ko = _aprep_fn(qs, ks, cos_ref[...], sin_ref[...], qw_ref[...], kw_ref[...])
        for h in range(AH):
            q_ref[h] = qo[h].astype(BF16)
        for h in range(AKV):
            k_ref[h] = ko[h].astype(BF16)
            v_ref[h] = x_ref[:, (AKV + h) * HD:(AKV + h + 1) * HD].astype(BF16)

    tab = pl.BlockSpec((br, HD), lambda i: (i, 0))
    vec = pl.BlockSpec((1, HD), lambda i: (0, 0))
    return _call(body, name="aprep_fwd",
                 out_shape=(_sds((AH, T, HD), BF16), _sds((AKV, T, HD), BF16), _sds((AKV, T, HD), BF16)),
                 grid=(T // br,),
                 in_specs=[pl.BlockSpec((br, C_QKV), lambda i: (i, 0)), tab, tab, vec, vec],
                 out_specs=(pl.BlockSpec((AH, br, HD), lambda i: (0, i, 0)),
                            pl.BlockSpec((AKV, br, HD), lambda i: (0, i, 0)),
                            pl.BlockSpec((AKV, br, HD), lambda i: (0, i, 0))),
                 sem=("parallel",))(proj, cos, sin, qw, kw)


def _aprep_bwd(proj, cos, sin, qw, kw, dq, dk, dv, dproj, L, *, br=256):
    T = proj.shape[0]
    lb = L // br

    def body(x_ref, cos_ref, sin_ref, qw_ref, kw_ref, dq_ref, dk_ref, dv_ref, _, dx_ref, dqw_ref, dkw_ref):
        i = pl.program_id(0)
        qs = [x_ref[:, C_AQ + h * HD:C_AQ + (h + 1) * HD] for h in range(AH)]
        ks = [x_ref[:, h * HD:(h + 1) * HD] for h in range(AKV)]
        _, vjp = jax.vjp(_aprep_fn, qs, ks, cos_ref[...], sin_ref[...], qw_ref[...], kw_ref[...])
        is_lat = i >= lb
        dqs = [jnp.where(is_lat, dq_ref[h], 0.0) for h in range(AH)]
        dks = [dk_ref[h] for h in range(AKV)]
        gq, gk, _, _, gqw, gkw = vjp((dqs, dks))
        for h in range(AH):
            dx_ref[:, C_AQ + h * HD:C_AQ + (h + 1) * HD] = gq[h].astype(BF16)
        for h in range(AKV):
            dx_ref[:, h * HD:(h + 1) * HD] = gk[h].astype(BF16)
            dx_ref[:, (AKV + h) * HD:(AKV + h + 1) * HD] = dv_ref[h].astype(BF16)

        @pl.when(i == 0)
        def _():
            dqw_ref[...] = jnp.zeros_like(dqw_ref)
            dkw_ref[...] = jnp.zeros_like(dkw_ref)

        dqw_ref[...] += gqw
        dkw_ref[...] += gkw

    tab = pl.BlockSpec((br, HD), lambda i: (i, 0))
    vec = pl.BlockSpec((1, HD), lambda i: (0, 0))
    kvb = pl.BlockSpec((AKV, br, HD), lambda i: (0, i, 0))
    blk = pl.BlockSpec((br, C_QKV), lambda i: (i, 0))
    return _call(body, name="aprep_bwd", out_shape=(_sds(dproj.shape, BF16), _sds((1, HD)), _sds((1, HD))),
                 grid=(T // br,),
                 in_specs=[blk, tab, tab, vec, vec,
                           pl.BlockSpec((AH, br, HD), lambda i: (0, jnp.maximum(i - lb, 0), 0)), kvb, kvb, ANYSPEC],
                 out_specs=(blk, vec, vec), aliases={8: 0},
                 sem=("arbitrary",))(proj, cos, sin, qw, kw, dq, dk, dv, dproj)


def _attn_fn(q, k, v):
    s = _dot(q, k, 1, 1) * (HD ** -0.5)
    m = jnp.max(s, axis=-1, keepdims=True)
    e = jnp.exp(s - m)
    l = jnp.sum(e, axis=-1, keepdims=True)
    return _dot(e / l, v, 1, 0), m + jnp.log(l)


def _attn_grad(q, k, v, o, lse, do):
    scale = HD ** -0.5
    p = jnp.exp(_dot(q, k, 1, 1) * scale - lse)
    dp = _dot(do, v, 1, 1)
    ds = p * (dp - jnp.sum(do * o, axis=-1, keepdims=True)) * scale
    return _dot(ds, k, 1, 0), _dot(ds, q, 0, 0), _dot(p, do, 0, 0)


def _attn_fwd(q, k, v, L, exch, *, bq=128):
    T = q.shape[1]
    N = T - L
    lb = L // bq

    def body(q_ref, k_ref, v_ref, o_ref, o32_ref, lse_ref):
        o, lse = _attn_fn(q_ref[...].reshape(GRP * bq, HD), k_ref[...], v_ref[...])
        for g in range(GRP):
            o_ref[:, g * HD:(g + 1) * HD] = o[g * bq:(g + 1) * bq].astype(BF16)
            o32_ref[:, g * HD:(g + 1) * HD] = o[g * bq:(g + 1) * bq]
        lse_ref[...] = jnp.broadcast_to(lse, (GRP * bq, HD)).reshape(GRP, bq, HD)

    kvb = pl.BlockSpec((None, T, HD), lambda g, i: (g, 0, 0))
    ob = pl.BlockSpec((bq, GRP * HD), lambda g, i: (i, g))
    return _call_carrying(
        body, exch, name="attn_fwd",
        out_shape=(_sds((N, AH * HD), BF16), _sds((N, AH * HD)), _sds((AH, N, HD))), grid=(AKV, N // bq),
        in_specs=[pl.BlockSpec((GRP, bq, HD), lambda g, i: (g, i + lb, 0)), kvb, kvb],
        out_specs=(ob, ob, pl.BlockSpec((GRP, bq, HD), lambda g, i: (g, i, 0))), vmem=VMEM_BIG)(q, k, v)


def _attn_bwd(q, k, v, o32, lse, do, L, *, bq=128):
    T = q.shape[1]
    N = T - L
    lb = L // bq

    def body(q_ref, k_ref, v_ref, o_ref, lse_ref, do_ref, dq_ref, dk_ref, dv_ref):
        rows = lambda r: jnp.concatenate([r[:, g * HD:(g + 1) * HD] for g in range(GRP)], axis=0)
        lse = jnp.max(lse_ref[...].reshape(GRP * bq, HD), axis=-1, keepdims=True)
        dq, dk, dv = _attn_grad(q_ref[...].reshape(GRP * bq, HD), k_ref[...], v_ref[...], rows(o_ref), lse, rows(do_ref))
        dq_ref[...] = dq.reshape(GRP, bq, HD)

        @pl.when(pl.program_id(1) == 0)
        def _():
            dk_ref[...] = jnp.zeros_like(dk_ref)
            dv_ref[...] = jnp.zeros_like(dv_ref)

        dk_ref[...] += dk
        dv_ref[...] += dv

    kvb = pl.BlockSpec((None, T, HD), lambda g, i: (g, 0, 0))
    qb = pl.BlockSpec((GRP, bq, HD), lambda g, i: (g, i + lb, 0))
    hb = pl.BlockSpec((GRP, bq, HD), lambda g, i: (g, i, 0))
    ob = pl.BlockSpec((bq, GRP * HD), lambda g, i: (i, g))
    return _call(body, name="attn_bwd",
                 out_shape=(_sds((AH, N, HD)), _sds((AKV, T, HD)), _sds((AKV, T, HD))), grid=(AKV, N // bq),
                 in_specs=[qb, kvb, kvb, ob, hb, ob], out_specs=(hb, kvb, kvb),
                 sem=("parallel", "arbitrary"), vmem=VMEM_BIG)(q, k, v, o32, lse, do)


def _gprep_fn(kind, shifts, x, w):
    down, up = shifts
    y = down(x) * w[0:1, :] + x * w[1:2, :] + up(x) * w[2:3, :]
    a = _silu(y)
    if kind == 2:
        return a
    a = a * lax.rsqrt(jnp.sum(a * a, axis=-1, keepdims=True) + EPS)
    return a * (HD ** -0.5) if kind == 0 else a


def _gprep_fwd(proj, conv_w, kind, bounds):
    T = proj.shape[0]
    shifts = _make_shift(bounds)
    cb = C_QKV // HD + kind * GH

    def body(x_ref, w_ref, o_ref):
        o_ref[...] = _gprep_fn(kind, shifts, x_ref[...], w_ref[...])

    return _call(body, name=f"gprep_fwd{kind}", out_shape=_sds((GH, T, HD)), grid=(GH,),
                 in_specs=[pl.BlockSpec((T, HD), lambda h: (0, cb + h)),
                           pl.BlockSpec((3, HD), lambda h: (0, kind * GH + h))],
                 out_specs=pl.BlockSpec((None, T, HD), lambda h: (h, 0, 0)), sem=("parallel",))(proj, conv_w)


def _gprep_bwd(proj, conv_w, kind, bounds, dy, dproj):
    T = proj.shape[0]
    shifts = _make_shift(bounds)
    cb = C_QKV // HD + kind * GH

    def body(x_ref, w_ref, dy_ref, _, dx_ref, dw_ref):
        _, vjp = jax.vjp(functools.partial(_gprep_fn, kind, shifts), x_ref[...], w_ref[...])
        dx, dw = vjp(dy_ref[0] + dy_ref[1])
        dx_ref[...] = dx.astype(BF16)
        dw_ref[...] = dw

    return _call(body, name=f"gprep_bwd{kind}", out_shape=(_sds(dproj.shape, BF16), _sds((3, GH * HD))), grid=(GH,),
                 in_specs=[pl.BlockSpec((T, HD), lambda h: (0, cb + h)),
                           pl.BlockSpec((3, HD), lambda h: (0, kind * GH + h)),
                           pl.BlockSpec((2, None, T, HD), lambda h: (0, h, 0, 0)), ANYSPEC],
                 out_specs=(pl.BlockSpec((T, HD), lambda h: (0, cb + h)), pl.BlockSpec((3, HD), lambda h: (0, h))),
                 aliases={3: 0}, sem=("parallel",))(proj, conv_w, dy, dproj)


def _bl_fn(x, alog, dtb):
    lane = lax.broadcasted_iota(jnp.int32, x.shape, 1)
    beta = jax.nn.sigmoid(x)
    z = x + dtb
    sp = jnp.maximum(z, 0.0) + jnp.log1p(jnp.exp(-jnp.abs(z)))
    la = -jnp.exp(alog) * sp
    return jnp.where(lane < 2 * GH, beta, jnp.where(lane < 4 * GH, la, 0.0))


def _bl_fwd(proj, alog, dtb, *, br=256):
    T = proj.shape[0]

    def body(x_ref, a_ref, d_ref, o_ref):
        o_ref[...] = _bl_fn(x_ref[...], a_ref[...], d_ref[...])

    vec = pl.BlockSpec((1, HD), lambda i: (0, 0))
    return _call(body, name="bl_fwd", out_shape=_sds((T, HD)), grid=(T // br,),
                 in_specs=[pl.BlockSpec((br, HD), lambda i: (i, C_BL // HD)), vec, vec],
                 out_specs=pl.BlockSpec((br, HD), lambda i: (i, 0)), sem=("parallel",))(proj, alog, dtb)


def _bl_bwd(proj, alog, dtb, dbl, dproj, *, br=256):
    T = proj.shape[0]
    wide = C_Z - C_BL

    def body(x_ref, a_ref, d_ref, g_ref, _, dx_ref, da_ref, dd_ref):
        g = g_ref[0, 0]
        for d in range(2):
            for h in range(GH):
                if d or h:
                    g = g + g_ref[d, h]
        _, vjp = jax.vjp(_bl_fn, x_ref[...], a_ref[...], d_ref[...])
        dx, da, dd = vjp(g)
        dx_ref[:, :HD] = dx.astype(BF16)
        dx_ref[:, HD:] = jnp.zeros((br, wide - HD), BF16)

        @pl.when(pl.program_id(0) == 0)
        def _():
            da_ref[...] = jnp.zeros_like(da_ref)
            dd_ref[...] = jnp.zeros_like(dd_ref)

        da_ref[...] += da
        dd_ref[...] += dd

    vec = pl.BlockSpec((1, HD), lambda i: (0, 0))
    return _call(body, name="bl_bwd", out_shape=(_sds(dproj.shape, BF16), _sds((1, HD)), _sds((1, HD))), grid=(T // br,),
                 in_specs=[pl.BlockSpec((br, HD), lambda i: (i, C_BL // HD)), vec, vec,
                           pl.BlockSpec((2, GH, br, HD), lambda i: (0, 0, i, 0)), ANYSPEC],
                 out_specs=(pl.BlockSpec((br, wide), lambda i: (i, C_BL // wide)), vec, vec), aliases={4: 0},
                 sem=("arbitrary",))(proj, alog, dtb, dbl, dproj)


def _chunk_masks(d):
    ii = lax.broadcasted_iota(jnp.int32, (CH, CH), 0)
    jj = lax.broadcasted_iota(jnp.int32, (CH, CH), 1)
    eye = (ii == jj).astype(F32)
    before = jnp.where(d == 0, (jj < ii).astype(F32), (jj > ii).astype(F32))
    return before, before + eye, eye


def _intra_fn(masks, sel_b, sel_l, qs, ks, vs, bls, xs=None):
    before, ateq, eye = masks
    ones = jnp.ones((CH, CH), F32)
    inc = ateq > 0.0
    each = lambda f, *ls: [f(*t) for t in zip(*ls)]
    beta = each(lambda bl: jnp.sum(bl * sel_b, axis=-1, keepdims=True), bls)
    la = each(lambda bl: jnp.sum(bl * sel_l, axis=-1, keepdims=True), bls)
    gam = each(lambda a: _mask_nn(ateq, jnp.broadcast_to(a, (CH, HD))), la)
    gi = each(lambda a: _mask_nn(ateq, jnp.broadcast_to(a, (CH, CH))), la)
    gj = each(lambda g: _mask_nn(ones, eye * g), gi)
    kk = each(lambda k: _nt(k, k), ks)
    qk = each(_nt, qs, ks)
    dec = each(lambda a, b: jnp.where(inc, jnp.exp(jnp.where(inc, a - b, 0.0)), 0.0), gi, gj)
    lmat = each(lambda b, d, m: before * (b * d * m), beta, dec, kk)
    if xs is None:
        x = each(lambda m: eye - m, lmat)
        p2 = each(lambda m: _mdot(m, m), lmat)
        for it in range(4):
            y = each(lambda a, b: _mdot(jnp.concatenate([a, b], axis=0), b), x, p2)
            x = each(lambda a, t: a + t[:CH], x, y)
            p2 = each(lambda t: t[CH:], y)
        x = each(lambda a, b: a + _mdot(a, b), x, p2)
    else:
        x = each(_saved_inverse, lmat, xs)
    eg = each(jnp.exp, gam)
    u = each(lambda a, b, v: _mdot(a, b * v), x, beta, vs)
    w = each(lambda a, b, e, k: _mdot(a, (b * e) * k), x, beta, eg, ks)
    tot = each(lambda a: jnp.sum(a, axis=0, keepdims=True), la)
    kd = each(lambda k, t, g: k * jnp.exp(t - g), ks, tot, gam)
    gl = each(lambda t: jnp.broadcast_to(jnp.exp(t), (1, HD)), tot)
    qd = each(lambda q, e: q * e, qs, eg)
    p = each(lambda d, m: d * m, dec, qk)
    return (u, w, kd, qd, p, gl, x) if xs is None else (u, w, kd, qd, p, gl)


def _dir_head_sel(d, h):
    lane = lax.broadcasted_iota(jnp.int32, (1, HD), 1)
    return (lane == d * GH + h).astype(F32), (lane == 2 * GH + d * GH + h).astype(F32)


def _intra_specs(T, G):
    nc = T // CH
    assert nc % G == 0
    qkv = pl.BlockSpec((None, G * CH, HD), lambda d, h, c: (h, c, 0))
    bl = pl.BlockSpec((G * CH, HD), lambda d, h, c: (c, 0))
    big = pl.BlockSpec((None, None, G * CH, HD), lambda d, h, c: (d, h, c, 0))
    pm = pl.BlockSpec((None, None, G * CH, CH), lambda d, h, c: (d, h, c, 0))
    gl = pl.BlockSpec((None, None, G, 1, HD), lambda d, h, c: (d, h, c, 0, 0))
    shapes = (_sds((2, GH, T, HD)),) + (_sds((2, GH, T, HD), BF16),) * 3 + (
        _sds((2, GH, T, CH), BF16), _sds((2, GH, nc, 1, HD)), _sds((2, GH, T, CH)))
    return nc, qkv, bl, big, pm, gl, shapes


def _chunks_per_step(T, most):
    nc = T // CH
    return max(g for g in range(1, most + 1) if nc % g == 0)


def _intra_fwd(q, k, v, bl, exch):
    T = q.shape[1]
    G = _chunks_per_step(T, INTRA_FWD_CHUNKS)
    nc, qkv_s, bl_s, big, pm, gl_s, shapes = _intra_specs(T, G)

    def body(q_ref, k_ref, v_ref, bl_ref, u_ref, w_ref, kd_ref, qd_ref, p_ref, gl_ref, x_ref):
        d, h = pl.program_id(0), pl.program_id(1)
        sb, sl = _dir_head_sel(d, h)
        rows = [slice(g * CH, (g + 1) * CH) for g in range(G)]
        outs = _intra_fn(_chunk_masks(d), sb, sl, *[[r[s, :] for s in rows] for r in (q_ref, k_ref, v_ref, bl_ref)])
        for g in range(G):
            for r, o in zip((u_ref, w_ref, kd_ref, qd_ref, p_ref, x_ref), outs[:5] + outs[6:]):
                r[rows[g], :] = o[g].astype(r.dtype)
            gl_ref[g] = outs[5][g]

    return _call_carrying(body, exch, name="gdn_intra_fwd", out_shape=shapes, grid=(2, GH, nc // G),
                          in_specs=[qkv_s, qkv_s, qkv_s, bl_s], out_specs=(big, big, big, big, pm, gl_s, pm))(q, k, v, bl)


def _intra_bwd(q, k, v, bl, xinv, cts, exch):
    T = q.shape[1]
    G = _chunks_per_step(T, INTRA_BWD_CHUNKS)
    nc, qkv_s, bl_s, big, pm, gl_s, _ = _intra_specs(T, G)

    def body(q_ref, k_ref, v_ref, bl_ref, x_ref, du, dw, dkd, dqd, dp, dgl, dq_ref, dk_ref, dv_ref, dbl_ref):
        d, h = pl.program_id(0), pl.program_id(1)
        sb, sl = _dir_head_sel(d, h)
        rows = [slice(g * CH, (g + 1) * CH) for g in range(G)]
        fn = functools.partial(_intra_fn, _chunk_masks(d), sb, sl, xs=[x_ref[s, :] for s in rows])
        _, vjp = jax.vjp(fn, *[[r[s, :] for s in rows] for r in (q_ref, k_ref, v_ref, bl_ref)])
        cts = tuple([r[s, :] for s in rows] for r in (du, dw, dkd, dqd, dp)) + ([dgl[g] for g in range(G)],)
        grads = vjp(cts)
        for g in range(G):
            for r, o in zip((dq_ref, dk_ref, dv_ref, dbl_ref), grads):
                r[rows[g], :] = o[g]

    return _call_carrying(body, exch, name="gdn_intra_bwd", out_shape=(_sds((2, GH, T, HD)),) * 4,
                          grid=(2, GH, nc // G), in_specs=[qkv_s, qkv_s, qkv_s, bl_s, pm, big, big, big, big, pm, gl_s],
                          out_specs=(big,) * 4)(q, k, v, bl, xinv, *cts)


def _scan_fn(s, u, w, kd, qd, p, gl):
    each = lambda f, *ls: [f(*t) for t in zip(*ls)]
    ws = each(_nn, w, s)
    delta = each(lambda a, b: a - b, u, ws)
    kdd = each(_tn, kd, delta)
    s_new = each(lambda g, a, b: g * a + b, gl, s, kdd)
    qs = each(_nn, qd, s)
    pd = each(_nn, p, delta)
    return each(lambda a, b: a + b, qs, pd), s_new


SCAN_BLOCK = 4


def _scan_visit(t, d, nb, ncb):
    rev = jnp.where(t < ncb, ncb - 1 - t, nb - 1 - (t - ncb))
    return jnp.where(d == 0, t, rev)


def _scan_specs(T, L, back):
    tb = SCAN_BLOCK * CH
    assert T % tb == 0 and L % tb == 0
    nb, ncb = T // tb, L // tb

    def at(d, t):
        return _scan_visit(nb - 1 - t if back else t, d, nb, ncb)

    big = pl.BlockSpec((None, GH, tb, HD), lambda d, t: (d, 0, at(d, t), 0))
    pm = pl.BlockSpec((None, GH, tb, CH), lambda d, t: (d, 0, at(d, t), 0))
    gl = pl.BlockSpec((None, GH, SCAN_BLOCK, 1, HD), lambda d, t: (d, 0, at(d, t), 0, 0))
    st = pl.BlockSpec((None, GH, SCAN_BLOCK, HD, HD), lambda d, t: (d, 0, at(d, t), 0, 0))
    do = pl.BlockSpec((GH, tb, HD), lambda d, t: (0, at(d, t), 0))
    return nb, big, pm, gl, st, do


def _scan_fwd(u, w, kd, qd, p, gl, L):
    T = u.shape[2]
    nb, big, pm, gl_s, st, _ = _scan_specs(T, L, False)
    heads = range(GH)

    def body(u_ref, w_ref, kd_ref, qd_ref, p_ref, gl_ref, o_ref, st_ref, s_scr):
        d = pl.program_id(0)

        @pl.when(pl.program_id(1) == 0)
        def _():
            s_scr[...] = jnp.zeros_like(s_scr)

        s = [s_scr[h] for h in heads]
        for i in range(SCAN_BLOCK):
            c = jnp.where(d == 0, i, SCAN_BLOCK - 1 - i)
            rows = pl.ds(pl.multiple_of(c * CH, CH), CH)
            for h in heads:
                st_ref[h, c] = s[h]
            o, s = _scan_fn(s, *[[r[h, rows, :].astype(F32) for h in heads] for r in (u_ref, w_ref, kd_ref, qd_ref, p_ref)],
                            [gl_ref[h, c] for h in heads])
            for h in heads:
                o_ref[h, rows, :] = o[h]
        for h in heads:
            s_scr[h] = s[h]

    return _call(body, name="gdn_scan_fwd", out_shape=(_sds((2, GH, T, HD)), _sds((2, GH, T // CH, HD, HD))),
                 grid=(2, nb), in_specs=[big, big, big, big, pm, gl_s], out_specs=(big, st),
                 scratch=[pltpu.VMEM((GH, HD, HD), F32)], sem=("parallel", "arbitrary"))(u, w, kd, qd, p, gl)


def _scan_bwd(u, w, kd, qd, p, gl, states, do, L, exch):
    T = u.shape[2]
    nb, big, pm, gl_s, st, do_s = _scan_specs(T, L, True)
    heads = range(GH)

    def body(u_ref, w_ref, kd_ref, qd_ref, p_ref, gl_ref, st_ref, do_ref,
             du_ref, dw_ref, dkd_ref, dqd_ref, dp_ref, dgl_ref, ds_scr):
        d = pl.program_id(0)

        @pl.when(pl.program_id(1) == 0)
        def _():
            ds_scr[...] = jnp.zeros_like(ds_scr)

        ds = [ds_scr[h] for h in heads]
        for i in range(SCAN_BLOCK):
            c = jnp.where(d == 0, SCAN_BLOCK - 1 - i, i)
            rows = pl.ds(pl.multiple_of(c * CH, CH), CH)
            _, vjp = jax.vjp(_scan_fn, [st_ref[h, c] for h in heads],
                             *[[r[h, rows, :].astype(F32) for h in heads] for r in (u_ref, w_ref, kd_ref, qd_ref, p_ref)],
                             [gl_ref[h, c] for h in heads])
            ds, gu, gw, gkd, gqd, gp, ggl = vjp(([do_ref[h, rows, :] for h in heads], ds))
            for h in heads:
                du_ref[h, rows, :] = gu[h]
                dw_ref[h, rows, :] = gw[h]
                dkd_ref[h, rows, :] = gkd[h]
                dqd_ref[h, rows, :] = gqd[h]
                dp_ref[h, rows, :] = gp[h]
                dgl_ref[h, c] = ggl[h]
        for h in heads:
            ds_scr[h] = ds[h]

    return _call_carrying(
        body, exch, name="gdn_scan_bwd",
        out_shape=(_sds((2, GH, T, HD)),) * 4 + (_sds((2, GH, T, CH)), _sds((2, GH, T // CH, 1, HD))),
        grid=(2, nb), in_specs=[big, big, big, big, pm, gl_s, st, do_s], out_specs=(big, big, big, big, pm, gl_s),
        scratch=[pltpu.VMEM((GH, HD, HD), F32)])(u, w, kd, qd, p, gl, states, do)


def _gout_fn(o0, o1, z, gw):
    return _rms(o0 + o1) * gw * _silu(z)


def _gout_fwd(o, proj, gw, L):
    T = o.shape[2]
    N = T - L
    ob = pl.BlockSpec((2, None, T, HD), lambda h: (0, h, 0, 0))

    def body(o_ref, z_ref, gw_ref, y_ref):
        y_ref[...] = _gout_fn(o_ref[0, L:, :], o_ref[1, L:, :], z_ref[L:, :], gw_ref[...]).astype(BF16)

    return _call(body, name="gout_fwd", out_shape=_sds((N, GH * HD), BF16), grid=(GH,),
                 in_specs=[ob, pl.BlockSpec((T, HD), lambda h: (0, C_Z // HD + h)), pl.BlockSpec((1, HD), lambda h: (0, 0))],
                 out_specs=pl.BlockSpec((N, HD), lambda h: (0, h)), sem=("parallel",))(o, proj, gw)


def _gout_bwd(o, proj, gw, dy, dproj, L):
    T = o.shape[2]
    N = T - L
    ob = pl.BlockSpec((2, None, T, HD), lambda h: (0, h, 0, 0))

    def body(o_ref, z_ref, gw_ref, dy_ref, _, do_ref, dz_ref, dgw_ref):
        _, vjp = jax.vjp(_gout_fn, o_ref[0, L:, :], o_ref[1, L:, :], z_ref[L:, :], gw_ref[...])
        g0, _, gz, ggw = vjp(dy_ref[...])
        do_ref[:L, :] = jnp.zeros((L, HD), F32)
        do_ref[L:, :] = g0
        dz_ref[:L, :] = jnp.zeros((L, HD), BF16)
        dz_ref[L:, :] = gz.astype(BF16)

        @pl.when(pl.program_id(0) == 0)
        def _():
            dgw_ref[...] = jnp.zeros_like(dgw_ref)

        dgw_ref[...] += ggw

    zb = pl.BlockSpec((T, HD), lambda h: (0, C_Z // HD + h))
    return _call(body, name="gout_bwd", out_shape=(_sds((GH, T, HD)), _sds(dproj.shape, BF16), _sds((1, HD))),
                 grid=(GH,),
                 in_specs=[ob, zb, pl.BlockSpec((1, HD), lambda h: (0, 0)), pl.BlockSpec((N, HD), lambda h: (0, h)), ANYSPEC],
                 out_specs=(pl.BlockSpec((None, T, HD), lambda h: (h, 0, 0)), zb, pl.BlockSpec((1, HD), lambda h: (0, 0))),
                 aliases={4: 1}, sem=("arbitrary",))(o, proj, gw, dy, dproj)


def _merge_fn(pa, pd, ga, gd):
    return jax.nn.sigmoid(ga) * pa + jax.nn.sigmoid(gd) * pd


def _merge_fwd(pa, pd, proj, L, *, br=256):
    N = pa.shape[0]
    lb = L // br
    row = pl.BlockSpec((br, D), lambda i: (i, 0))

    def body(pa_ref, pd_ref, ga_ref, gd_ref, y_ref):
        y_ref[...] = _merge_fn(pa_ref[...], pd_ref[...], ga_ref[...], gd_ref[...]).astype(BF16)

    return _call(body, name="merge_fwd", out_shape=_sds((N, D), BF16), grid=(N // br,),
                 in_specs=[row, row, pl.BlockSpec((br, D), lambda i: (i + lb, C_GATE // D)),
                           pl.BlockSpec((br, D), lambda i: (i + lb, C_GATE // D + 1))],
                 out_specs=row, sem=("parallel",))(pa, pd, proj, proj)


def _merge_bwd(pa, pd, proj, dy, L, *, br=256):
    N = pa.shape[0]
    T = N + L
    lb = L // br
    lrow = pl.BlockSpec((br, D), lambda i: (jnp.maximum(i - lb, 0), 0))

    def body(pa_ref, pd_ref, ga_ref, gd_ref, dy_ref, dpa_ref, dpd_ref, dg_ref):
        lat = pl.program_id(0) >= lb
        _, vjp = jax.vjp(_merge_fn, pa_ref[...], pd_ref[...], ga_ref[...], gd_ref[...])
        gpa, gpd, gga, ggd = vjp(dy_ref[...])
        dpa_ref[...] = gpa.astype(BF16)
        dpd_ref[...] = gpd.astype(BF16)
        dg_ref[:, :D] = jnp.where(lat, gga, 0.0).astype(BF16)
        dg_ref[:, D:] = jnp.where(lat, ggd, 0.0).astype(BF16)

    return _call(body, name="merge_bwd", out_shape=(_sds((N, D), BF16), _sds((N, D), BF16), _sds((T, C_END), BF16)),
                 grid=(T // br,),
                 in_specs=[lrow, lrow, pl.BlockSpec((br, D), lambda i: (i, C_GATE // D)),
                           pl.BlockSpec((br, D), lambda i: (i, C_GATE // D + 1)), lrow],
                 out_specs=(lrow, lrow, pl.BlockSpec((br, 2 * D), lambda i: (i, C_GATE // (2 * D)))),
                 sem=("arbitrary",))(pa, pd, proj, proj, dy)


def _resid_fwd(x, m, mod, i_g, *, name, br=256):
    R = x.shape[0]
    row = pl.BlockSpec((br, D), lambda i: (i, 0))

    def body(x_ref, m_ref, mod_ref, o_ref):
        o_ref[...] = x_ref[...] + mod_ref[i_g:i_g + 1, :] * m_ref[...]

    return _call(body, name=name, out_shape=_sds((R, D)), grid=(R // br,),
                 in_specs=[row, row, pl.BlockSpec((6, D), lambda i: (0, 0))], out_specs=row,
                 sem=("parallel",))(x, m, mod)


def _resid_bwd(dx, m, mod, i_g, *, name, br=256):
    R = dx.shape[0]
    row = pl.BlockSpec((br, D), lambda i: (i, 0))
    vec = pl.BlockSpec((1, D), lambda i: (0, 0))

    def body(dx_ref, m_ref, mod_ref, dm_ref, dg_ref):
        dxv = dx_ref[...]
        dm_ref[...] = (dxv * mod_ref[i_g:i_g + 1, :]).astype(BF16)

        @pl.when(pl.program_id(0) == 0)
        def _():
            dg_ref[...] = jnp.zeros_like(dg_ref)

        dg_ref[...] += jnp.sum(dxv * m_ref[...], axis=0, keepdims=True)

    return _call(body, name=name, out_shape=(_sds((R, D), BF16), _sds((1, D))), grid=(R // br,),
                 in_specs=[row, row, pl.BlockSpec((6, D), lambda i: (0, 0))], out_specs=(row, vec),
                 sem=("arbitrary",))(dx, m, mod)


def _ffn_fn(shifts, ug, uv, wg, wv, bg, bv):
    down, up = shifts

    def conv(x, w, b):
        return down(x) * w[0:1, :] + x * w[1:2, :] + up(x) * w[2:3, :] + b

    return _silu(conv(ug, wg, bg)) * conv(uv, wv, bv)


def _ffn_fwd(up, cw, cb, *, bw=256):
    N = up.shape[0]
    shifts = _make_shift(((0, N),))
    nb = DFF // bw

    def body(ug, uv, wg, wv, bg, bv, a_ref):
        a_ref[...] = _ffn_fn(shifts, ug[...], uv[...], wg[...], wv[...], bg[...], bv[...]).astype(BF16)

    def col(rows, off):
        return pl.BlockSpec((rows, bw), lambda j: (0, j + off))

    return _call(body, name="ffn_fwd", out_shape=_sds((N, DFF), BF16), grid=(nb,),
                 in_specs=[col(N, 0), col(N, nb), col(3, 0), col(3, nb), col(1, 0), col(1, nb)],
                 out_specs=col(N, 0), sem=("parallel",), vmem=VMEM_BIG)(up, up, cw, cw, cb, cb)


def _ffn_bwd(up, cw, cb, da, *, bw=256):
    N = up.shape[0]
    shifts = _make_shift(((0, N),))
    nb = DFF // bw

    def body(ug, uv, wg, wv, bg, bv, da_ref, dug, duv, dwg, dwv, dbg, dbv):
        _, vjp = jax.vjp(functools.partial(_ffn_fn, shifts), ug[...], uv[...], wg[...], wv[...], bg[...], bv[...])
        g = vjp(da_ref[...])
        dug[...] = g[0].astype(BF16)
        duv[...] = g[1].astype(BF16)
        dwg[...], dwv[...], dbg[...], dbv[...] = g[2], g[3], g[4], g[5]

    def col(rows, off):
        return pl.BlockSpec((rows, bw), lambda j: (0, j + off))

    half = (_sds((N, DFF), BF16), _sds((N, DFF), BF16), _sds((3, DFF)), _sds((3, DFF)), _sds((1, DFF)), _sds((1, DFF)))
    dug, duv, dwg, dwv, dbg, dbv = _call(
        body, name="ffn_bwd", out_shape=half, grid=(nb,),
        in_specs=[col(N, 0), col(N, nb), col(3, 0), col(3, nb), col(1, 0), col(1, nb), col(N, 0)],
        out_specs=(col(N, 0), col(N, 0), col(3, 0), col(3, 0), col(1, 0), col(1, 0)),
        sem=("parallel",), vmem=VMEM_BIG)(up, up, cw, cw, cb, cb, da)
    return (jnp.concatenate([dug, duv], axis=1), jnp.concatenate([dwg, dwv], axis=1),
            jnp.concatenate([dbg, dbv], axis=1))


def _head_fn(x1, dn, g2, fw, tgt):
    y = _rms(x1 + g2 * dn) * fw
    err = y - tgt
    return 0.5 * jnp.sum(jnp.mean(err * err, axis=-1))


def _head(x1, dn, mod, fw, tgt, *, br=256):
    N = x1.shape[0]
    row = pl.BlockSpec((br, D), lambda i: (i, 0))
    vec = pl.BlockSpec((1, D), lambda i: (0, 0))
    one = pl.BlockSpec((1, HD), lambda i: (0, 0))

    def body(x1_ref, dn_ref, mod_ref, fw_ref, tgt_ref, loss_ref, dx_ref, ddn_ref, dg_ref, dfw_ref):
        loss, (gx, gdn, gg, gfw) = jax.value_and_grad(_head_fn, argnums=(0, 1, 2, 3))(
            x1_ref[...], dn_ref[...], mod_ref[5:6, :], fw_ref[...], tgt_ref[...])
        dx_ref[...] = gx
        ddn_ref[...] = gdn.astype(BF16)

        @pl.when(pl.program_id(0) == 0)
        def _():
            loss_ref[...] = jnp.zeros_like(loss_ref)
            dg_ref[...] = jnp.zeros_like(dg_ref)
            dfw_ref[...] = jnp.zeros_like(dfw_ref)

        loss_ref[...] += jnp.broadcast_to(loss, (1, HD))
        dg_ref[...] += gg
        dfw_ref[...] += gfw

    return _call(body, name="head", out_shape=(_sds((1, HD)), _sds((N, D)), _sds((N, D), BF16), _sds((1, D)), _sds((1, D))),
                 grid=(N // br,), in_specs=[row, row, pl.BlockSpec((6, D), lambda i: (0, 0)), vec, row],
                 out_specs=(one, row, row, vec, vec), sem=("arbitrary",))(x1, dn, mod, fw, tgt)


def _adamw(w, g, m, v, *, name):
    shape = w.shape
    cols = shape[-1]
    rows = max(1, math.prod(shape[:-1]))
    w2, g2, m2, v2 = (t.reshape(rows, cols) for t in (w, g, m, v))
    br = 256 if rows % 256 == 0 else rows
    c1 = 1.0 - B1 ** STEP
    c2 = 1.0 - B2 ** STEP

    def body(w_ref, g_ref, m_ref, v_ref, d_ref, nm_ref, nv_ref):
        gv = g_ref[...]
        nm = B1 * m_ref[...] + (1.0 - B1) * gv
        nv = B2 * v_ref[...] + (1.0 - B2) * (gv * gv)
        d_ref[...] = -LR * ((nm / c1) / (jnp.sqrt(nv / c2) + AEPS) + WD * w_ref[...])
        nm_ref[...] = nm
        nv_ref[...] = nv

    blk = pl.BlockSpec((br, cols), lambda i: (i, 0))
    outs = _call(body, name=name, out_shape=(_sds((rows, cols)),) * 3, grid=(rows // br,),
                 in_specs=[blk] * 4, out_specs=(blk,) * 3, sem=("parallel",))(w2, g2, m2, v2)
    return tuple(t.reshape(shape) for t in outs)


def _adamw_many(items, *, name):
    k = len(items)
    shapes = [w.shape for w, _, _, _ in items]
    flat = [t.reshape(max(1, math.prod(t.shape[:-1])), t.shape[-1]) for it in items for t in it]
    c1 = 1.0 - B1 ** STEP
    c2 = 1.0 - B2 ** STEP

    def body(*refs):
        ins, outs = refs[:4 * k], refs[4 * k:]
        for i in range(k):
            w_ref, g_ref, m_ref, v_ref = ins[4 * i:4 * i + 4]
            gv = g_ref[...]
            nm = B1 * m_ref[...] + (1.0 - B1) * gv
            nv = B2 * v_ref[...] + (1.0 - B2) * (gv * gv)
            outs[3 * i][...] = -LR * ((nm / c1) / (jnp.sqrt(nv / c2) + AEPS) + WD * w_ref[...])
            outs[3 * i + 1][...] = nm
            outs[3 * i + 2][...] = nv

    res = _call(body, name=name, out_shape=tuple(_sds(flat[4 * i].shape) for i in range(k) for _ in range(3)))(*flat)
    return [tuple(res[3 * i + j].reshape(shapes[i]) for j in range(3)) for i in range(k)]


def _rope_tables(N, L):
    t = jnp.arange(N)
    pos = jnp.stack([(t // GRID_W).astype(F32), (t % GRID_W).astype(F32)], axis=1)
    inv = ROPE_THETA ** (-jnp.arange(0, HD // 2, 2, dtype=F32) / (HD // 2))
    ang = pos[:, :, None] * inv[None, None, :]
    cos = jnp.broadcast_to(jnp.cos(ang)[:, :, None, :], (N, 2, 2, HD // 4)).reshape(N, HD)
    sin = jnp.broadcast_to(jnp.sin(ang)[:, :, None, :], (N, 2, 2, HD // 4))
    sin = (sin * jnp.array([-1.0, 1.0], F32)[None, None, :, None]).reshape(N, HD)
    cos = jnp.concatenate([jnp.ones((L, HD), F32), cos], axis=0)
    sin = jnp.concatenate([jnp.zeros((L, HD), F32), sin], axis=0)
    return cos, sin


def _pad_lanes(v, off=0):
    return jnp.zeros((1, HD), F32).at[0, off:off + v.shape[0]].set(v)


def _local_step(x, ctx, tgt, mod_lat, mod_ctx, w_in, shards, small):
    N, L = x.shape[0], ctx.shape[0]
    T = N + L
    bounds = ((0, L), (L, T))
    qw, kw, gw = small["q_norm_w"], small["k_norm_w"], small["gdn_norm_w"]
    conv_w, ffn_w, ffn_b, fnw = small["conv_qkv_w"], small["ffn_conv_w"], small["ffn_conv_b"], small["final_norm_w"]
    alog = _pad_lanes(small["a_log"].reshape(-1), 2 * GH)
    dtb = _pad_lanes(small["dt_bias"].reshape(-1), 2 * GH)
    cos, sin = _rope_tables(N, L)
    bt = T
    bnl = 256 if N % 1024 else 1024

    hc = _normmod_fwd(ctx, mod_ctx, 0, 1, name="normmod_ctx")
    hx = _normmod_fwd(x, mod_lat, 0, 1, name="normmod_x")
    h1 = jnp.concatenate([hc, hx], axis=0)
    proj = _mm(h1, w_in, name="mm_in", M=T, N=C_END, K=D, tb=True, bm=bt, bn=1024)
    aq, ak, av = _aprep_fwd(proj, cos, sin, qw, kw)
    (attn, attn32, lse), (up_g,) = _attn_fwd(aq, ak, av, L, _GatherTwoLevel([shards["w_up"]]))
    gq = _gprep_fwd(proj, conv_w, 0, bounds)
    gk = _gprep_fwd(proj, conv_w, 1, bounds)
    gv = _gprep_fwd(proj, conv_w, 2, bounds)
    bl = _bl_fwd(proj, alog, dtb)
    intra, (down_g, pa_g, pd_g, out_g) = _intra_fwd(
        gq, gk, gv, bl, _GatherTwoLevel([shards[n] for n in ("w_down", "w_pa", "w_pd", "w_out")]))
    w_up, w_down = up_g.reshape(2 * DFF, D), down_g.reshape(DFF, D)
    w_pa, w_pd, w_out = pa_g.reshape(D, D), pd_g.reshape(D, D), out_g.reshape(D, D)
    xinv, intra = intra[6], intra[:6]
    o, states = _scan_fwd(*intra, L)
    gdn = _gout_fwd(o, proj, gw, L)
    pa = _mm(attn, w_pa, name="mm_pa", M=N, N=D, K=D, bm=bnl)
    pd = _mm(gdn, w_pd, name="mm_pd", M=N, N=D, K=D, bm=bnl)
    y = _merge_fwd(pa, pd, proj, L)
    m = _mm(y, w_out, name="mm_out", M=N, N=D, K=D, bm=bnl)
    x1 = _resid_fwd(x, m, mod_lat, 2, name="resid1")
    h2 = _normmod_fwd(x1, mod_lat, 3, 4, name="normmod_x1")
    up = _mm(h2, w_up, name="mm_up", M=N, N=2 * DFF, K=D, tb=True, bm=bnl, bn=2 * DFF // 4)
    a = _ffn_fwd(up, ffn_w, ffn_b)
    dn = _mm(a, w_down, name="mm_down", M=N, N=D, K=DFF, bm=bnl)
    loss, dx2, ddn, dg2, dfnw = _head(x1, dn, mod_lat, fnw, tgt)

    da = _mm(ddn, w_down, name="mm_down_dx", M=N, N=DFF, K=D, tb=True, bm=bnl, bn=DFF // 2)
    g_down = _mm(a, ddn, name="mm_down_dw", M=DFF, N=D, K=N, ta=True, bm=DFF // 2, out_dtype=BF16)
    dup, d_ffn_w, d_ffn_b = _ffn_bwd(up, ffn_w, ffn_b, da)
    dh2 = _mm(dup, w_up, name="mm_up_dx", M=N, N=D, K=2 * DFF, bm=bnl, bk=2 * DFF // 4)
    g_up = _mm(dup, h2, name="mm_up_dw", M=2 * DFF, N=D, K=N, ta=True, bm=2 * DFF // 4, out_dtype=BF16)
    dx1, dsh2, dsc2 = _normmod_bwd(x1, mod_lat, 3, 4, dh2, 0, dx2, name="normmod_x1_bwd")
    dm, dg1 = _resid_bwd(dx1, m, mod_lat, 2, name="resid1_bwd")
    dy = _mm(dm, w_out, name="mm_out_dx", M=N, N=D, K=D, tb=True, bm=bnl)
    g_out = _mm(y, dm, name="mm_out_dw", M=D, N=D, K=N, ta=True, out_dtype=BF16)
    dpa, dpd, dproj = _merge_bwd(pa, pd, proj, dy, L)
    dattn = _mm(dpa, w_pa, name="mm_pa_dx", M=N, N=D, K=D, tb=True, bm=bnl)
    g_pa = _mm(attn, dpa, name="mm_pa_dw", M=D, N=D, K=N, ta=True, out_dtype=BF16)
    dgdn = _mm(dpd, w_pd, name="mm_pd_dx", M=N, N=D, K=D, tb=True, bm=bnl)
    g_pd = _mm(gdn, dpd, name="mm_pd_dw", M=D, N=D, K=N, ta=True, out_dtype=BF16)
    do, dproj, dgw = _gout_bwd(o, proj, gw, dgdn, dproj, L)
    cts, recv_a = _scan_bwd(*intra, states, do, L, _Exchange(
        [g_out.reshape(NDEV, D // NDEV, D), g_down.reshape(NDEV, DFF // NDEV, D)], True))
    (dgq, dgk, dgv, dbl), recv_b = _intra_bwd(gq, gk, gv, bl, xinv, cts, _Exchange(
        [g_pa.reshape(NDEV, D // NDEV, D), g_pd.reshape(NDEV, D // NDEV, D), g_up.reshape(NDEV, 2 * DFF // NDEV, D)], True))
    recv = dict(zip(("w_out", "w_down", "w_pa", "w_pd", "w_up"), recv_a + recv_b))
    dproj, dwq = _gprep_bwd(proj, conv_w, 0, bounds, dgq, dproj)
    dproj, dwk = _gprep_bwd(proj, conv_w, 1, bounds, dgk, dproj)
    dproj, dwv = _gprep_bwd(proj, conv_w, 2, bounds, dgv, dproj)
    dproj, dalog, ddtb = _bl_bwd(proj, alog, dtb, dbl, dproj)
    daq_h, dak_h, dav_h = _attn_bwd(aq, ak, av, attn32, lse, dattn, L)
    dproj, dqw, dkw = _aprep_bwd(proj, cos, sin, qw, kw, daq_h, dak_h, dav_h, dproj, L)
    g_in = _mm(dproj, h1, name="mm_in_dw", M=C_END, N=D, K=T, ta=True, bm=1024, out_dtype=BF16)
    g_in = _unpad_columns(g_in).reshape(NDEV, W_END // NDEV, D)
    own_in = lax.dynamic_index_in_dim(g_in, _position()[3], axis=0, keepdims=False)
    *pending, token = _scatter_start(g_in, None, (0, D // 2), (), name="scatter_g_in_a_start")
    dh1 = _mm(dproj, w_in, name="mm_in_dx", M=T, N=D, K=C_END, bm=bt, bk=1024, after=(token,))
    grad_x, dsh1, dsc1 = _normmod_bwd(x, mod_lat, 0, 1, dh1, L, dx1, name="normmod_x_bwd")
    _, dcsh1, dcsc1 = _normmod_bwd(ctx, mod_ctx, 0, 1, dh1, 0, None, name="normmod_ctx_bwd")

    z1 = jnp.zeros((1, D), F32)
    dmod_lat = jnp.concatenate([dsh1, dsc1, dg1, dsh2, dsc2, dg2], axis=0)
    dmod_ctx = jnp.concatenate([dcsh1, dcsc1, z1, z1, z1, z1], axis=0)
    gsmall = {
        "q_norm_w": dqw, "k_norm_w": dkw, "gdn_norm_w": dgw,
        "conv_qkv_w": jnp.concatenate([dwq, dwk, dwv], axis=1),
        "a_log": dalog[0, 2 * GH:4 * GH], "dt_bias": ddtb[0, 2 * GH:4 * GH],
        "ffn_conv_w": d_ffn_w, "ffn_conv_b": d_ffn_b, "final_norm_w": dfnw,
    }
    return loss[0, 0], grad_x, (pending, own_in), recv, dmod_lat, dmod_ctx, gsmall


HBM = pl.BlockSpec(memory_space=pltpu.HBM)
ANYSPEC = pl.BlockSpec(memory_space=pl.ANY)


def _position():
    x, y, c = lax.axis_index("x"), lax.axis_index("y"), lax.axis_index("c")
    return x, y, c, 4 * x + 2 * y + c


def _peer(x, y, c, k):
    px = 1 - x if k & 4 else x
    py = 1 - y if k & 2 else y
    pc = 1 - c if k & 1 else c
    return (px, py, pc), 4 * px + 2 * py + pc


def _exchange(arrs, *, name, scatter):
    exch = _Exchange(arrs, scatter)
    n = exch.n

    def body(*refs):
        ins, outs, sems = refs[:n], refs[n:2 * n], refs[2 * n:]
        exch.start(ins, outs, sems)
        exch.finish(ins, outs, sems)

    outs = pl.pallas_call(body, name=name, out_shape=exch.out_shape, in_specs=[HBM] * n, out_specs=(HBM,) * n,
                          scratch_shapes=exch.scratch,
                          compiler_params=pltpu.CompilerParams(has_side_effects=True))(*arrs)
    return list(outs)


class _Exchange:
    def __init__(self, arrs, scatter):
        self.arrs, self.scatter, self.n = list(arrs), scatter, len(arrs)
        self.out_shape = tuple(_sds(a.shape if scatter else (NDEV,) + a.shape, a.dtype) for a in arrs)
        self.scratch = [pltpu.SemaphoreType.DMA((self.n, NDEV - 1)), pltpu.SemaphoreType.DMA((self.n, NDEV - 1)),
                        pltpu.SemaphoreType.DMA((self.n,))]

    def _copies(self, ins, outs, sems):
        send, recv, loc = sems
        x, y, c, me = _position()
        local = [pltpu.make_async_copy(ins[a].at[me] if self.scatter else ins[a], outs[a].at[me], loc.at[a])
                 for a in range(self.n)]
        remote = []
        for k in range(1, NDEV):
            peer, pid = _peer(x, y, c, k)
            for a in range(self.n):
                src = ins[a].at[pid] if self.scatter else ins[a]
                remote.append(pltpu.make_async_remote_copy(
                    src_ref=src, dst_ref=outs[a].at[me], send_sem=send.at[a, k - 1], recv_sem=recv.at[a, k - 1],
                    device_id=peer, device_id_type=MESH))
        return local, remote

    def start(self, ins, outs, sems):
        local, remote = self._copies(ins, outs, sems)
        for cp in local + remote:
            cp.start()

    def finish(self, ins, outs, sems):
        local, remote = self._copies(ins, outs, sems)
        for cp in remote:
            cp.wait()
        for cp in local:
            cp.wait()


class _GatherTwoLevel:
    scatter = False

    def __init__(self, arrs):
        self.arrs, self.n = list(arrs), len(arrs)
        self.out_shape = tuple(_sds((NDEV,) + a.shape, a.dtype) for a in arrs)
        self.scratch = [pltpu.SemaphoreType.DMA((self.n, NDEV - 1)), pltpu.SemaphoreType.DMA((self.n, NDEV - 1)),
                        pltpu.SemaphoreType.DMA((self.n,))]

    def _parts(self, ins, outs, sems):
        send, recv, loc = sems
        x, y, c, _ = _position()
        me, sibling = (x, y, c), (x, y, 1 - c)
        chips = [(1 - x, y), (x, 1 - y), (1 - x, 1 - y)]
        parts = []
        for a in range(self.n):
            slot = lambda px, py, pc, a=a: outs[a].at[4 * px + 2 * py + pc]

            def copy(k, owner, to, src=None, a=a, slot=slot):
                return pltpu.make_async_remote_copy(
                    src_ref=slot(*owner) if src is None else src, dst_ref=slot(*owner), send_sem=send.at[a, k],
                    recv_sem=recv.at[a, k], device_id=to, device_id_type=MESH)

            parts.append(dict(
                mine=pltpu.make_async_copy(ins[a], slot(*me), loc.at[a]),
                first=[copy(0, me, sibling, src=ins[a])] + [copy(1 + j, me, (*ch, c), src=ins[a]) for j, ch in enumerate(chips)],
                arrive=[copy(1 + j, (*ch, c), me) for j, ch in enumerate(chips)],
                passed=[copy(4 + j, (*ch, c), sibling) for j, ch in enumerate(chips)],
                rest=[copy(0, sibling, me)] + [copy(4 + j, (*ch, 1 - c), me) for j, ch in enumerate(chips)]))
        return parts

    def start(self, ins, outs, sems):
        for p in self._parts(ins, outs, sems):
            p["mine"].start()
            for cp in p["first"]:
                cp.start()

    def middle(self, ins, outs, sems):
        for p in self._parts(ins, outs, sems):
            for got, fwd in zip(p["arrive"], p["passed"]):
                got.wait_recv()
                fwd.start()

    def finish(self, ins, outs, sems):
        for p in self._parts(ins, outs, sems):
            for cp in p["rest"]:
                cp.wait_recv()
            for cp in p["first"] + p["passed"]:
                cp.wait_send()
            p["mine"].wait()


def _gather_two_level(block, *, name):
    def body(x_ref, out_ref, send_sems, recv_sems, local_sem):
        x, y, c, _ = _position()
        me, sibling = (x, y, c), (x, y, 1 - c)
        chips = [(1 - x, y), (x, 1 - y), (1 - x, 1 - y)]

        def slot(px, py, pc):
            return out_ref.at[4 * px + 2 * py + pc]

        def copy(k, owner, to, src=None):
            return pltpu.make_async_remote_copy(
                src_ref=slot(*owner) if src is None else src, dst_ref=slot(*owner), send_sem=send_sems.at[k],
                recv_sem=recv_sems.at[k], device_id=to, device_id_type=MESH)

        mine = pltpu.make_async_copy(x_ref, slot(*me), local_sem)
        mine.start()
        first = [copy(0, me, sibling, src=x_ref)]
        first += [copy(1 + j, me, (*chip, c), src=x_ref) for j, chip in enumerate(chips)]
        for cp in first:
            cp.start()
        passed = [copy(4 + j, (*chip, c), sibling) for j, chip in enumerate(chips)]
        for j, chip in enumerate(chips):
            copy(1 + j, (*chip, c), me).wait_recv()
            passed[j].start()
        copy(0, sibling, me).wait_recv()
        for j, chip in enumerate(chips):
            copy(4 + j, (*chip, 1 - c), me).wait_recv()
        for cp in first + passed:
            cp.wait_send()
        mine.wait()

    return pl.pallas_call(
        body, name=name, out_shape=_sds((NDEV,) + block.shape, block.dtype), in_specs=[HBM], out_specs=HBM,
        scratch_shapes=[pltpu.SemaphoreType.DMA((NDEV - 1,)), pltpu.SemaphoreType.DMA((NDEV - 1,)),
                        pltpu.SemaphoreType.DMA],
        compiler_params=pltpu.CompilerParams(has_side_effects=True))(block)


SEM = pl.BlockSpec(memory_space=pltpu.SEMAPHORE)


def _scatter_copies(src_ref, land_ref, send_sems, recv_sems, cols):
    x, y, c, me = _position()
    span = (slice(None), pl.ds(*cols))
    copies = []
    for k in range(1, NDEV):
        peer, pid = _peer(x, y, c, k)
        copies.append(pltpu.make_async_remote_copy(
            src_ref=src_ref.at[pid].at[span], dst_ref=land_ref.at[me].at[span], send_sem=send_sems.at[k - 1],
            recv_sem=recv_sems.at[k - 1], device_id=peer, device_id_type=MESH))
    return copies


SPLIT_EFFECT = pltpu.SideEffectType.DATAFLOW_SIDE_EFFECTING


def _scatter_start(parts, land, cols, after, *, name):
    na = len(after)
    if land is None:
        land = lax.empty(parts.shape, parts.dtype)

    def body(src_ref, land_ref, *rest):
        send_sems, recv_sems, _, _, token = rest[na:]
        for cp in _scatter_copies(src_ref, land_ref, send_sems, recv_sems, cols):
            cp.start()
        token[...] = jnp.zeros_like(token)

    return pl.pallas_call(
        body, name=name,
        out_shape=(pltpu.SemaphoreType.DMA((NDEV - 1,)), pltpu.SemaphoreType.DMA((NDEV - 1,)),
                   pltpu.HBM(parts.shape, parts.dtype), pltpu.HBM(parts.shape, parts.dtype), _sds((8, HD))),
        in_specs=(HBM, HBM) + (pl.BlockSpec(memory_space=pl.ANY),) * na,
        out_specs=(SEM, SEM, HBM, HBM, pl.BlockSpec(memory_space=pltpu.VMEM)),
        input_output_aliases={0: 2, 1: 3}, compiler_params=pltpu.CompilerParams(has_side_effects=SPLIT_EFFECT),
    )(pltpu.with_memory_space_constraint(parts, pltpu.HBM), pltpu.with_memory_space_constraint(land, pltpu.HBM), *after)


def _scatter_wait(send_sems, recv_sems, src_thru, land_thru, cols, after, *, name):
    na = len(after)

    def body(src_ref, land_ref, send_sems, recv_sems, *rest):
        for cp in _scatter_copies(src_ref, land_ref, send_sems, recv_sems, cols):
            cp.wait_send()
            cp.wait_recv()

    return pl.pallas_call(
        body, name=name,
        out_shape=(pltpu.HBM(src_thru.shape, src_thru.dtype), pltpu.HBM(land_thru.shape, land_thru.dtype)),
        in_specs=(HBM, HBM, SEM, SEM) + (pl.BlockSpec(memory_space=pl.ANY),) * na, out_specs=(HBM, HBM),
        input_output_aliases={0: 0, 1: 1}, compiler_params=pltpu.CompilerParams(has_side_effects=SPLIT_EFFECT),
    )(src_thru, land_thru, send_sems, recv_sems, *after)


def _cast_bf16(w, *, name):
    rows, cols = w.shape
    br = 128 if rows % 128 == 0 else rows

    def body(w_ref, o_ref):
        o_ref[...] = w_ref[...].astype(BF16)

    blk = pl.BlockSpec((br, cols), lambda i: (i, 0))
    return _call(body, name=name, out_shape=_sds((rows, cols), BF16), grid=(rows // br,), in_specs=[blk],
                 out_specs=blk, sem=("parallel",))(w)


def _sum_slots(a, *, name):
    _, R, C = a.shape

    def body(a_ref, o_ref):
        s = a_ref[0]
        for d in range(1, NDEV):
            s = s + a_ref[d]
        o_ref[...] = s

    return _call(body, name=name, out_shape=_sds((R, C)))(a)


MODROWS = 16


def _mod_fwd(c9, w, b):
    cols = w.shape[1]

    def body(c_ref, w_ref, b_ref, o_ref):
        o_ref[...] = _nn(_silu(c_ref[...]), w_ref[...]) + b_ref[...]

    return _call(body, name="mod_fwd", out_shape=_sds((MODROWS, cols)))(c9, w, b)


def _mod_bwd(c9, dmy, dall, w):
    cols = w.shape[1]

    def body(c_ref, dmy_ref, dall_ref, w_ref, gw_ref, gb_ref, cp_ref):
        sc = _silu(c_ref[...])
        rows = lax.broadcasted_iota(jnp.int32, (MODROWS, 1), 0)
        d = dmy_ref[...]
        d_ctx = jnp.where(rows == NDEV, d, 0.0)
        sc_ctx = jnp.where(rows == NDEV, sc, 0.0)
        outer = lax.dot_general(sc_ctx, d_ctx, (((0,), (0,)), ((), ())), precision=HI, preferred_element_type=F32)
        gw_ref[...] = _tn(jnp.where(rows < NDEV, sc, 0.0), jnp.where(rows < NDEV, d, 0.0)) + outer
        gb_ref[...] = jnp.sum(dall_ref[...], axis=0, keepdims=True)
        cp_ref[...] = jnp.sum(_nt(d_ctx, w_ref[...]), axis=0, keepdims=True)

    return _call(body, name="mod_bwd", out_shape=(_sds((D, cols)), _sds((1, 6 * D)), _sds((1, D))),
                 vmem=VMEM_BIG)(c9, dmy, dall, w)


def _cctx_finish(parts, c_ctx, after):
    VM = pl.BlockSpec(memory_space=pltpu.VMEM)

    def body(p_ref, c_ref, *rest):
        o_ref = rest[-1]
        s = p_ref[0]
        for d in range(1, NDEV):
            s = s + p_ref[d]
        _, vjp = jax.vjp(_silu, c_ref[...])
        o_ref[...] = vjp(s)[0]

    return _call(body, name="cctx_finish", out_shape=_sds((1, D)),
                 in_specs=[VM, VM] + [pl.BlockSpec(memory_space=pl.ANY)] * len(after))(parts, c_ctx, *after)


def _adamw_recv(w, recv, m, v, *, name, own=None):
    rows, cols = w.shape
    bc = 256
    c1 = 1.0 - B1 ** STEP
    c2 = 1.0 - B2 ** STEP
    has_own = own is not None

    def body(w_ref, r_ref, m_ref, v_ref, *rest):
        g_ref, d_ref, nm_ref, nv_ref = rest[-4:]
        me = _position()[3]

        def slot(d):
            return jnp.where(me == d, rest[0][...], r_ref[d]) if has_own else r_ref[d]

        gv = slot(0).astype(F32)
        for d in range(1, NDEV):
            gv = gv + slot(d).astype(F32)
        nm = B1 * m_ref[...] + (1.0 - B1) * gv
        nv = B2 * v_ref[...] + (1.0 - B2) * (gv * gv)
        g_ref[...] = gv
        d_ref[...] = -LR * ((nm / c1) / (jnp.sqrt(nv / c2) + AEPS) + WD * w_ref[...])
        nm_ref[...] = nm
        nv_ref[...] = nv

    blk = pl.BlockSpec((rows, bc), lambda j: (0, j))
    return _call(body, name=name, out_shape=(_sds((rows, cols)),) * 4, grid=(cols // bc,),
                 in_specs=[blk, pl.BlockSpec((NDEV, rows, bc), lambda j: (0, 0, j)), blk, blk] + [blk] * has_own,
                 out_specs=(blk,) * 4, sem=("parallel",), vmem=VMEM_BIG)(w, recv, m, v, *([own] if has_own else []))


P_LAT, P_CTX, P_FNW, P_FFNB, P_CONV, P_FFNW, P_MISC, P_ROWS = 0, 8, 16, 24, 32, 48, 72, 80


def _rows_of(v, nrows):
    flat = v.reshape(-1)
    return jnp.pad(flat, (0, nrows * D - flat.shape[0])).reshape(nrows, D)


def _by_columns(g):
    n, r, c = g.shape
    return jnp.transpose(g, (1, 0, 2)).reshape(r, n * c)


def kernel(x, c, ctx, c_ctx, w_mod, b_mod, w_in, q_norm_w, k_norm_w, conv_qkv_w, a_log, dt_bias, gdn_norm_w, w_pa, w_pd, w_out, w_up, ffn_conv_w, ffn_conv_b, w_down, final_norm_w, loss_target, m_c_ctx, m_w_mod, m_b_mod, m_w_in, m_q_norm_w, m_k_norm_w, m_conv_qkv_w, m_a_log, m_dt_bias, m_gdn_norm_w, m_w_pa, m_w_pd, m_w_out, m_w_up, m_ffn_conv_w, m_ffn_conv_b, m_w_down, m_final_norm_w, v_c_ctx, v_w_mod, v_b_mod, v_w_in, v_q_norm_w, v_k_norm_w, v_conv_qkv_w, v_a_log, v_dt_bias, v_gdn_norm_w, v_w_pa, v_w_pd, v_w_out, v_w_up, v_ffn_conv_w, v_ffn_conv_b, v_w_down, v_final_norm_w):
    _, _, _, me = _position()
    mcols = w_mod.shape[2]

    transposed = ("w_in", "w_up")
    big = {"w_in": w_in[0].T, "w_pa": w_pa[0], "w_pd": w_pd[0], "w_out": w_out[0], "w_up": w_up[0].T, "w_down": w_down[0]}
    names = list(big)
    shards = {n: _cast_bf16(big[n], name="cast_" + n) for n in names}
    w_in_g = _gather_two_level(shards["w_in"], name="gather_w_in")
    c_all, conv_g, ffnw_g = _exchange([c, conv_qkv_w[0], ffn_conv_w[0]], name="gather_small", scatter=False)
    w_in_full = w_in_g.reshape(W_END, D)
    w_in_pad = _pad_columns(w_in_full)

    c9 = jnp.concatenate([c_all.reshape(NDEV, D), jnp.pad(c_ctx[None], ((0, MODROWS - NDEV - 1), (0, 0)))], axis=0)
    b_loc = lax.dynamic_slice(b_mod, (0, me * mcols), (1, mcols))
    mod_all, = _exchange([_mod_fwd(c9, w_mod[0], b_loc)], name="gather_mod", scatter=False)
    mod_lat = lax.dynamic_index_in_dim(mod_all, me, axis=1, keepdims=False).reshape(6, D)
    mod_ctx = mod_all[:, NDEV, :].reshape(6, D)

    small = {"q_norm_w": q_norm_w, "k_norm_w": k_norm_w, "gdn_norm_w": gdn_norm_w, "a_log": a_log, "dt_bias": dt_bias,
             "conv_qkv_w": _by_columns(conv_g), "ffn_conv_w": _by_columns(ffnw_g), "ffn_conv_b": ffn_conv_b,
             "final_norm_w": final_norm_w[None]}
    loss_me, grad_x, (pending_in, own_in), recv, dmod_lat, dmod_ctx, gs = _local_step(
        x[0], ctx[0], loss_target[0], mod_lat, mod_ctx, w_in_pad, shards, small)

    moments = {"w_in": (m_w_in, v_w_in), "w_pa": (m_w_pa, v_w_pa), "w_pd": (m_w_pd, v_w_pd),
               "w_out": (m_w_out, v_w_out), "w_up": (m_w_up, v_w_up), "w_down": (m_w_down, v_w_down)}
    res = {}
    def finish(n, outs):
        return tuple((t.T if n in transposed else t)[None] for t in outs)

    def moment(t, n):
        return t[0].T if n in transposed else t[0]

    for n in recv:
        res[n] = finish(n, _adamw_recv(big[n], recv[n], moment(moments[n][0], n), moment(moments[n][1], n),
                                       name="adamw_" + n))

    misc = jnp.concatenate([gs["q_norm_w"][0], gs["k_norm_w"][0], gs["gdn_norm_w"][0], gs["a_log"], gs["dt_bias"],
                            loss_me[None]])
    pack = jnp.concatenate([_rows_of(dmod_lat, P_CTX - P_LAT), _rows_of(dmod_ctx, P_FNW - P_CTX),
                            _rows_of(gs["final_norm_w"], P_FFNB - P_FNW), _rows_of(gs["ffn_conv_b"], P_CONV - P_FFNB),
                            _rows_of(gs["conv_qkv_w"], P_FFNW - P_CONV), _rows_of(gs["ffn_conv_w"], P_MISC - P_FFNW),
                            _rows_of(misc, P_ROWS - P_MISC)], axis=0)
    pack_all, = _exchange([pack], name="gather_pack", scatter=False)
    tot = _sum_slots(pack_all, name="sum_pack")
    dall = jnp.concatenate([pack_all[:, P_LAT:P_LAT + 6, :].reshape(NDEV, 6 * D),
                            jnp.pad(tot[P_CTX:P_CTX + 6].reshape(1, 6 * D), ((0, MODROWS - NDEV - 1), (0, 0)))], axis=0)
    dmy = lax.dynamic_slice(dall, (0, me * mcols), (MODROWS, mcols))
    g_w_mod, g_b_mod, cpart = _mod_bwd(c9, dmy, dall, w_mod[0])
    cparts, = _exchange([cpart], name="gather_cctx", scatter=False)
    sems_a, land = pending_in[:2], pending_in[3]
    *sems_b, g_in_thru, land, token_b = _scatter_start(pending_in[2], land, (D // 2, D // 2), (cparts,),
                                                       name="scatter_g_in_b_start")
    g_c_ctx = _cctx_finish(cparts, c_ctx[None], (token_b,))[0]

    nconv, nffn = 3 * GH * HD, 2 * DFF
    conv_tot = tot[P_CONV:P_FFNW].reshape(-1)[:3 * nconv].reshape(3, nconv)
    ffnw_tot = tot[P_FFNW:P_MISC].reshape(-1)[:3 * nffn].reshape(3, nffn)
    mrow = tot[P_MISC]
    grads = {
        "c_ctx": g_c_ctx, "w_mod": g_w_mod[None], "b_mod": g_b_mod,
        "q_norm_w": mrow[None, 0:HD], "k_norm_w": mrow[None, HD:2 * HD], "gdn_norm_w": mrow[None, 2 * HD:3 * HD],
        "conv_qkv_w": lax.dynamic_slice(conv_tot, (0, me * (nconv // NDEV)), (3, nconv // NDEV))[None],
        "a_log": mrow[3 * HD:3 * HD + 2 * GH].reshape(1, 2, GH),
        "dt_bias": mrow[3 * HD + 2 * GH:3 * HD + 4 * GH].reshape(1, 2, GH),
        "ffn_conv_w": lax.dynamic_slice(ffnw_tot, (0, me * (nffn // NDEV)), (3, nffn // NDEV))[None],
        "ffn_conv_b": tot[P_FFNB:P_CONV].reshape(-1)[:nffn][None],
        "final_norm_w": tot[P_FNW],
    }
    loss = mrow[3 * HD + 4 * GH]
    given = {"c_ctx": (c_ctx, m_c_ctx, v_c_ctx), "w_mod": (w_mod, m_w_mod, v_w_mod), "b_mod": (b_mod, m_b_mod, v_b_mod),
             "q_norm_w": (q_norm_w, m_q_norm_w, v_q_norm_w), "k_norm_w": (k_norm_w, m_k_norm_w, v_k_norm_w),
             "conv_qkv_w": (conv_qkv_w, m_conv_qkv_w, v_conv_qkv_w), "a_log": (a_log, m_a_log, v_a_log),
             "dt_bias": (dt_bias, m_dt_bias, v_dt_bias), "gdn_norm_w": (gdn_norm_w, m_gdn_norm_w, v_gdn_norm_w),
             "ffn_conv_w": (ffn_conv_w, m_ffn_conv_w, v_ffn_conv_w), "ffn_conv_b": (ffn_conv_b, m_ffn_conv_b, v_ffn_conv_b),
             "final_norm_w": (final_norm_w, m_final_norm_w, v_final_norm_w)}
    res["w_mod"] = (grads["w_mod"],) + _adamw(w_mod, grads["w_mod"], m_w_mod, v_w_mod, name="adamw_w_mod")
    small_names = [n for n in given if n != "w_mod"]
    updates = _adamw_many([(given[n][0], grads[n], given[n][1], given[n][2]) for n in small_names], name="adamw_small")
    for n, upd in zip(small_names, updates):
        res[n] = (grads[n],) + upd

    g_in_thru, land = _scatter_wait(*sems_a, g_in_thru, land, (0, D // 2), [res[n][1] for n in res],
                                    name="scatter_g_in_a_wait")
    _, land = _scatter_wait(*sems_b, g_in_thru, land, (D // 2, D // 2), (), name="scatter_g_in_b_wait")
    res["w_in"] = finish("w_in", _adamw_recv(big["w_in"], land, moment(m_w_in, "w_in"), moment(v_w_in, "w_in"),
                                             name="adamw_w_in", own=own_in))

    order = ["c_ctx", "w_mod", "b_mod", "w_in", "q_norm_w", "k_norm_w", "conv_qkv_w", "a_log", "dt_bias", "gdn_norm_w",
             "w_pa", "w_pd", "w_out", "w_up", "ffn_conv_w", "ffn_conv_b", "w_down", "final_norm_w"]
    return (loss, grad_x[None], *[res[n][0] for n in order], *[res[n][1] for n in order],
            *[res[n][2] for n in order], *[res[n][3] for n in order])
```

```python
import functools
import math

import jax
import jax.numpy as jnp
from jax import lax
from jax.experimental import pallas as pl
from jax.experimental.pallas import tpu as pltpu

F32 = jnp.float32
BF16 = jnp.bfloat16
HI = lax.Precision.HIGHEST
MESH = pl.DeviceIdType.MESH

NDEV = 8
D = 1024
HD = 128
AH, AKV, GRP = 8, 2, 4
GH = 8
CH = 64
DFF = 2816
GRID_W = 64
EPS = 1e-6
ROPE_THETA = 10000.0
C_KV, C_AQ, C_QKV, C_BL, C_Z, C_GATE, C_END = 0, 512, 1536, 4608, 5120, 6144, 8192
W_QKV, W_AQ, W_Z, W_END = 512, 3616, 4640, 7712


def _pad_columns(w):
    zeros = jnp.zeros((C_Z - C_QKV - (W_AQ - W_QKV), D), w.dtype)
    return jnp.concatenate([w[:W_QKV], w[W_AQ:W_Z], w[W_QKV:W_AQ], zeros, w[W_Z:]], axis=0)


def _unpad_columns(g):
    return jnp.concatenate([g[:C_AQ], g[C_QKV:C_QKV + W_AQ - W_QKV], g[C_AQ:C_QKV], g[C_Z:]], axis=0)
LR, B1, B2, AEPS, WD, STEP = 0.001, 0.9, 0.999, 1e-08, 0.01, 10
VMEM_BIG = 56 * 1024 * 1024
INTRA_FWD_CHUNKS = 36
INTRA_BWD_CHUNKS = 36


def _call(body, *, name, out_shape, grid=None, in_specs=None, out_specs=None, scratch=(), sem=None,
          vmem=None, aliases=None):
    params = {}
    if sem is not None:
        params["dimension_semantics"] = sem
    if vmem is not None:
        params["vmem_limit_bytes"] = vmem
    kw = {}
    if grid is not None:
        kw["grid"] = grid
    if in_specs is not None:
        kw["in_specs"] = in_specs
    if out_specs is not None:
        kw["out_specs"] = out_specs
    if aliases:
        kw["input_output_aliases"] = aliases
    return pl.pallas_call(body, name=name, out_shape=out_shape, scratch_shapes=list(scratch),
                          compiler_params=pltpu.CompilerParams(**params), **kw)


def _call_carrying(body, exch, *, name, out_shape, grid, in_specs, out_specs, scratch=(), vmem=None):
    n, nin, nout, nscr = exch.n, len(in_specs), len(out_shape), len(scratch)
    steps = math.prod(grid)
    mid = (2 * steps) // 3

    def wrapped(*refs):
        ins, cins = refs[:nin], refs[nin:nin + n]
        outs, couts = refs[nin + n:nin + n + nout], refs[nin + n + nout:nin + 2 * n + nout]
        scr, sems = refs[nin + 2 * n + nout:nin + 2 * n + nout + nscr], refs[nin + 2 * n + nout + nscr:]
        ids = [pl.program_id(i) for i in range(len(grid))]
        first = functools.reduce(jnp.logical_and, [i == 0 for i in ids])
        last = functools.reduce(jnp.logical_and, [i == g - 1 for i, g in zip(ids, grid)])

        @pl.when(first)
        def _():
            exch.start(cins, couts, sems)

        if hasattr(exch, "middle"):
            linear = functools.reduce(lambda acc, ig: acc * ig[1] + ig[0], zip(ids, grid), 0)

            @pl.when(linear == mid)
            def _():
                exch.middle(cins, couts, sems)

        body(*ins, *outs, *scr)

        @pl.when(last)
        def _():
            exch.finish(cins, couts, sems)

    params = {"dimension_semantics": ("arbitrary",) * len(grid)}
    if vmem is not None:
        params["vmem_limit_bytes"] = vmem
    fn = pl.pallas_call(wrapped, name=name, out_shape=tuple(out_shape) + exch.out_shape, grid=grid,
                        in_specs=list(in_specs) + [HBM] * n, out_specs=tuple(out_specs) + (HBM,) * n,
                        scratch_shapes=list(scratch) + exch.scratch, compiler_params=pltpu.CompilerParams(**params))

    def run(*args):
        res = fn(*args, *exch.arrs)
        return res[:nout], list(res[nout:])

    return run


def _sds(shape, dtype=F32):
    return jax.ShapeDtypeStruct(tuple(shape), dtype)


def _dot(a, b, ca, cb):
    return lax.dot_general(a.astype(BF16), b.astype(BF16), (((ca,), (cb,)), ((), ())),
                           preferred_element_type=F32)


@jax.custom_vjp
def _nn(a, b):
    return _dot(a, b, 1, 0)


@jax.custom_vjp
def _nt(a, b):
    return _dot(a, b, 1, 1)


@jax.custom_vjp
def _tn(a, b):
    return _dot(a, b, 0, 0)


_nn.defvjp(lambda a, b: (_nn(a, b), (a, b)), lambda r, g: (_nt(g, r[1]), _tn(r[0], g)))
_nt.defvjp(lambda a, b: (_nt(a, b), (a, b)), lambda r, g: (_nn(g, r[1]), _tn(g, r[0])))
_tn.defvjp(lambda a, b: (_tn(a, b), (a, b)), lambda r, g: (_nt(r[1], g), _nn(r[0], g)))


def _hdot(a, b):
    return jnp.dot(a, b, precision=HI, preferred_element_type=F32)


def _mdot(a, b):
    return jnp.dot(a, b, precision=lax.Precision.HIGH, preferred_element_type=F32)


def _maskdot(mask, a, cm):
    hi = a.astype(BF16)
    r = a - hi.astype(F32)
    mid = r.astype(BF16)
    lo = (r - mid.astype(F32)).astype(BF16)
    mb = mask.astype(BF16)
    dims = (((cm,), (0,)), ((), ()))
    return (lax.dot_general(mb, hi, dims, preferred_element_type=F32)
            + lax.dot_general(mb, mid, dims, preferred_element_type=F32)
            + lax.dot_general(mb, lo, dims, preferred_element_type=F32))


@jax.custom_vjp
def _mask_nn(mask, a):
    return _maskdot(mask, a, 1)


_mask_nn.defvjp(lambda mask, a: (_maskdot(mask, a, 1), mask),
                lambda mask, g: (jnp.zeros_like(mask), _maskdot(mask, g, 0)))


@jax.custom_vjp
def _saved_inverse(lmat, x):
    return x


def _saved_inverse_bwd(x, g):
    t = lax.dot_general(x, g, (((0,), (0,)), ((), ())), precision=lax.Precision.HIGH, preferred_element_type=F32)
    dl = lax.dot_general(t, x, (((1,), (1,)), ((), ())), precision=lax.Precision.HIGH, preferred_element_type=F32)
    return -dl, jnp.zeros_like(x)


_saved_inverse.defvjp(lambda lmat, x: (x, x), _saved_inverse_bwd)


def _row_ids(shape):
    return lax.broadcasted_iota(jnp.int32, shape, 0)


def _shift_rows(x, down, bounds):
    n = x.shape[0]
    rows = _row_ids(x.shape)
    y = pltpu.roll(x, 1 if down else n - 1, 0)
    edge = functools.reduce(jnp.logical_or, [rows == (s if down else e - 1) for s, e in bounds])
    return jnp.where(edge, 0.0, y)


def _make_shift(bounds):
    @jax.custom_vjp
    def down(x):
        return _shift_rows(x, True, bounds)

    @jax.custom_vjp
    def up(x):
        return _shift_rows(x, False, bounds)

    down.defvjp(lambda x: (down(x), None), lambda _, g: (up(g),))
    up.defvjp(lambda x: (up(x), None), lambda _, g: (down(g),))
    return down, up


@jax.custom_vjp
def _swap32(x):
    lane = lax.broadcasted_iota(jnp.int32, x.shape, x.ndim - 1)
    return jnp.where((lane % 64) < 32, pltpu.roll(x, HD - 32, x.ndim - 1), pltpu.roll(x, 32, x.ndim - 1))


_swap32.defvjp(lambda x: (_swap32(x), None), lambda _, g: (_swap32(g),))


def _rms(x):
    return x * lax.rsqrt(jnp.mean(x * x, axis=-1, keepdims=True) + EPS)


def _silu(x):
    return x * jax.nn.sigmoid(x)


def _mm(a, b, *, name, M, N, K, ta=False, tb=False, out_dtype=F32, bm=None, bn=None, bk=None,
        a_off=(0, 0), b_off=(0, 0), after=()):
    bm, bn, bk = bm or M, bn or N, bk or K
    assert M % bm == 0 and N % bn == 0 and K % bk == 0, (name, M, N, K, bm, bn, bk)
    nk = K // bk
    ca, cb = (0 if ta else 1), (1 if tb else 0)
    na = len(after)

    def body(a_ref, b_ref, *rest):
        o_ref, acc = rest[na], rest[na + 1:]
        r = _dot(a_ref[...], b_ref[...], ca, cb)
        if nk == 1:
            o_ref[...] = r.astype(out_dtype)
        else:
            acc_ref, = acc
            k = pl.program_id(2)

            @pl.when(k == 0)
            def _():
                acc_ref[...] = r

            @pl.when(k > 0)
            def _():
                acc_ref[...] += r

            @pl.when(k == nk - 1)
            def _():
                o_ref[...] = acc_ref[...].astype(out_dtype)

    def blk(off, bshape):
        assert off[0] % bshape[0] == 0 and off[1] % bshape[1] == 0, (name, off, bshape)
        return off[0] // bshape[0], off[1] // bshape[1]

    if ta:
        ao = blk(a_off, (bk, bm))
        a_spec = pl.BlockSpec((bk, bm), lambda i, j, k: (k + ao[0], i + ao[1]))
    else:
        ao = blk(a_off, (bm, bk))
        a_spec = pl.BlockSpec((bm, bk), lambda i, j, k: (i + ao[0], k + ao[1]))
    if tb:
        bo = blk(b_off, (bn, bk))
        b_spec = pl.BlockSpec((bn, bk), lambda i, j, k: (j + bo[0], k + bo[1]))
    else:
        bo = blk(b_off, (bk, bn))
        b_spec = pl.BlockSpec((bk, bn), lambda i, j, k: (k + bo[0], j + bo[1]))
    return _call(body, name=name, out_shape=_sds((M, N), out_dtype), grid=(M // bm, N // bn, nk),
                 in_specs=[a_spec, b_spec] + [pl.BlockSpec(memory_space=pl.ANY)] * na,
                 out_specs=pl.BlockSpec((bm, bn), lambda i, j, k: (i, j)),
                 scratch=[pltpu.VMEM((bm, bn), F32)] if nk > 1 else [],
                 sem=("parallel", "parallel", "arbitrary"), vmem=VMEM_BIG)(a, b, *after)


def _normmod_fn(x, sh, sc):
    return _rms(x) * (1.0 + sc) + sh


def _normmod_fwd(x, mod, i_sh, i_sc, *, name, br=256):
    R = x.shape[0]

    def body(x_ref, mod_ref, o_ref):
        o_ref[...] = _normmod_fn(x_ref[...], mod_ref[i_sh:i_sh + 1, :], mod_ref[i_sc:i_sc + 1, :]).astype(BF16)

    return _call(body, name=name, out_shape=_sds((R, D), BF16), grid=(R // br,),
                 in_specs=[pl.BlockSpec((br, D), lambda i: (i, 0)), pl.BlockSpec((6, D), lambda i: (0, 0))],
                 out_specs=pl.BlockSpec((br, D), lambda i: (i, 0)), sem=("parallel",))(x, mod)


def _normmod_bwd(x, mod, i_sh, i_sc, dh, dh_off, res, *, name, br=256):
    R = x.shape[0]
    ob = dh_off // br
    has_res = res is not None

    def body(x_ref, mod_ref, dh_ref, *rest):
        if has_res:
            res_ref, dx_ref, dsh_ref, dsc_ref = rest
        else:
            dx_ref, dsh_ref, dsc_ref = rest
        sh, sc = mod_ref[i_sh:i_sh + 1, :], mod_ref[i_sc:i_sc + 1, :]
        _, vjp = jax.vjp(_normmod_fn, x_ref[...], sh, sc)
        dx, dsh, dsc = vjp(dh_ref[...])
        dx_ref[...] = dx + res_ref[...] if has_res else dx

        @pl.when(pl.program_id(0) == 0)
        def _():
            dsh_ref[...] = jnp.zeros_like(dsh_ref)
            dsc_ref[...] = jnp.zeros_like(dsc_ref)

        dsh_ref[...] += dsh
        dsc_ref[...] += dsc

    row = pl.BlockSpec((br, D), lambda i: (i, 0))
    vec = pl.BlockSpec((1, D), lambda i: (0, 0))
    ins = [row, pl.BlockSpec((6, D), lambda i: (0, 0)), pl.BlockSpec((br, D), lambda i: (i + ob, 0))]
    args = [x, mod, dh]
    if has_res:
        ins.append(row)
        args.append(res)
    return _call(body, name=name, out_shape=(_sds((R, D)), _sds((1, D)), _sds((1, D))), grid=(R // br,),
                 in_specs=ins, out_specs=(row, vec, vec), sem=("arbitrary",))(*args)


def _rope(x, cos, sin):
    return x * cos + _swap32(x) * sin


def _aprep_fn(qs, ks, cos, sin, qw, kw):
    return ([_rope(_rms(q) * qw, cos, sin) for q in qs], [_rope(_rms(k) * kw, cos, sin) for k in ks])


def _aprep_fwd(proj, cos, sin, qw, kw, *, br=256):
    T = proj.shape[0]

    def body(x_ref, cos_ref, sin_ref, qw_ref, kw_ref, q_ref, k_ref, v_ref):
        qs = [x_ref[:, C_AQ + h * HD:C_AQ + (h + 1) * HD] for h in range(AH)]
        ks = [x_ref[:, h * HD:(h + 1) * HD] for h in range(AKV)]
        qo, ko = _aprep_fn(qs, ks, cos_ref[...], sin_ref[...], qw_ref[...], kw_ref[...])
        for h in range(AH):
            q_ref[h] = qo[h].astype(BF16)
        for h in range(AKV):
            k_ref[h] = ko[h].astype(BF16)
            v_ref[h] = x_ref[:, (AKV + h) * HD:(AKV + h + 1) * HD].astype(BF16)

    tab = pl.BlockSpec((br, HD), lambda i: (i, 0))
    vec = pl.BlockSpec((1, HD), lambda i: (0, 0))
    return _call(body, name="aprep_fwd",
                 out_shape=(_sds((AH, T, HD), BF16), _sds((AKV, T, HD), BF16), _sds((AKV, T, HD), BF16)),
                 grid=(T // br,),
                 in_specs=[pl.BlockSpec((br, C_QKV), lambda i: (i, 0)), tab, tab, vec, vec],
                 out_specs=(pl.BlockSpec((AH, br, HD), lambda i: (0, i, 0)),
                            pl.BlockSpec((AKV, br, HD), lambda i: (0, i, 0)),
                            pl.BlockSpec((AKV, br, HD), lambda i: (0, i, 0))),
                 sem=("parallel",))(proj, cos, sin, qw, kw)


def _aprep_bwd(proj, cos, sin, qw, kw, dq, dk, dv, dproj, L, *, br=256):
    T = proj.shape[0]
    lb = L // br

    def body(x_ref, cos_ref, sin_ref, qw_ref, kw_ref, dq_ref, dk_ref, dv_ref, _, dx_ref, dqw_ref, dkw_ref):
        i = pl.program_id(0)
        qs = [x_ref[:, C_AQ + h * HD:C_AQ + (h + 1) * HD] for h in range(AH)]
        ks = [x_ref[:, h * HD:(h + 1) * HD] for h in range(AKV)]
        _, vjp = jax.vjp(_aprep_fn, qs, ks, cos_ref[...], sin_ref[...], qw_ref[...], kw_ref[...])
        is_lat = i >= lb
        dqs = [jnp.where(is_lat, dq_ref[h], 0.0) for h in range(AH)]
        dks = [dk_ref[h] for h in range(AKV)]
        gq, gk, _, _, gqw, gkw = vjp((dqs, dks))
        for h in range(AH):
            dx_ref[:, C_AQ + h * HD:C_AQ + (h + 1) * HD] = gq[h].astype(BF16)
        for h in range(AKV):
            dx_ref[:, h * HD:(h + 1) * HD] = gk[h].astype(BF16)
            dx_ref[:, (AKV + h) * HD:(AKV + h + 1) * HD] = dv_ref[h].astype(BF16)

        @pl.when(i == 0)
        def _():
            dqw_ref[...] = jnp.zeros_like(dqw_ref)
            dkw_ref[...] = jnp.zeros_like(dkw_ref)

        dqw_ref[...] += gqw
        dkw_ref[...] += gkw

    tab = pl.BlockSpec((br, HD), lambda i: (i, 0))
    vec = pl.BlockSpec((1, HD), lambda i: (0, 0))
    kvb = pl.BlockSpec((AKV, br, HD), lambda i: (0, i, 0))
    blk = pl.BlockSpec((br, C_QKV), lambda i: (i, 0))
    return _call(body, name="aprep_bwd", out_shape=(_sds(dproj.shape, BF16), _sds((1, HD)), _sds((1, HD))),
                 grid=(T // br,),
                 in_specs=[blk, tab, tab, vec, vec,
                           pl.BlockSpec((AH, br, HD), lambda i: (0, jnp.maximum(i - lb, 0), 0)), kvb, kvb, ANYSPEC],
                 out_specs=(blk, vec, vec), aliases={8: 0},
                 sem=("arbitrary",))(proj, cos, sin, qw, kw, dq, dk, dv, dproj)


def _attn_fn(q, k, v):
    s = _dot(q, k, 1, 1) * (HD ** -0.5)
    m = jnp.max(s, axis=-1, keepdims=True)
    e = jnp.exp(s - m)
    l = jnp.sum(e, axis=-1, keepdims=True)
    return _dot(e / l, v, 1, 0), m + jnp.log(l)


def _attn_grad(q, k, v, o, lse, do):
    scale = HD ** -0.5
    p = jnp.exp(_dot(q, k, 1, 1) * scale - lse)
    dp = _dot(do, v, 1, 1)
    ds = p * (dp - jnp.sum(do * o, axis=-1, keepdims=True)) * scale
    return _dot(ds, k, 1, 0), _dot(ds, q, 0, 0), _dot(p, do, 0, 0)


def _attn_fwd(q, k, v, L, exch, *, bq=128):
    T = q.shape[1]
    N = T - L
    lb = L // bq

    def body(q_ref, k_ref, v_ref, o_ref, o32_ref, lse_ref):
        o, lse = _attn_fn(q_ref[...].reshape(GRP * bq, HD), k_ref[...], v_ref[...])
        for g in range(GRP):
            o_ref[:, g * HD:(g + 1) * HD] = o[g * bq:(g + 1) * bq].astype(BF16)
            o32_ref[:, g * HD:(g + 1) * HD] = o[g * bq:(g + 1) * bq]
        lse_ref[...] = jnp.broadcast_to(lse, (GRP * bq, HD)).reshape(GRP, bq, HD)

    kvb = pl.BlockSpec((None, T, HD), lambda g, i: (g, 0, 0))
    ob = pl.BlockSpec((bq, GRP * HD), lambda g, i: (i, g))
    return _call_carrying(
        body, exch, name="attn_fwd",
        out_shape=(_sds((N, AH * HD), BF16), _sds((N, AH * HD)), _sds((AH, N, HD))), grid=(AKV, N // bq),
        in_specs=[pl.BlockSpec((GRP, bq, HD), lambda g, i: (g, i + lb, 0)), kvb, kvb],
        out_specs=(ob, ob, pl.BlockSpec((GRP, bq, HD), lambda g, i: (g, i, 0))), vmem=VMEM_BIG)(q, k, v)


def _attn_bwd(q, k, v, o32, lse, do, L, exch, *, bq=128):
    T = q.shape[1]
    N = T - L
    lb = L // bq

    def body(q_ref, k_ref, v_ref, o_ref, lse_ref, do_ref, dq_ref, dk_ref, dv_ref):
        rows = lambda r: jnp.concatenate([r[:, g * HD:(g + 1) * HD] for g in range(GRP)], axis=0)
        lse = jnp.max(lse_ref[...].reshape(GRP * bq, HD), axis=-1, keepdims=True)
        dq, dk, dv = _attn_grad(q_ref[...].reshape(GRP * bq, HD), k_ref[...], v_ref[...], rows(o_ref), lse, rows(do_ref))
        dq_ref[...] = dq.reshape(GRP, bq, HD)

        @pl.when(pl.program_id(1) == 0)
        def _():
            dk_ref[...] = jnp.zeros_like(dk_ref)
            dv_ref[...] = jnp.zeros_like(dv_ref)

        dk_ref[...] += dk
        dv_ref[...] += dv

    kvb = pl.BlockSpec((None, T, HD), lambda g, i: (g, 0, 0))
    qb = pl.BlockSpec((GRP, bq, HD), lambda g, i: (g, i + lb, 0))
    hb = pl.BlockSpec((GRP, bq, HD), lambda g, i: (g, i, 0))
    ob = pl.BlockSpec((bq, GRP * HD), lambda g, i: (i, g))
    return _call_carrying(body, exch, name="attn_bwd",
                          out_shape=(_sds((AH, N, HD)), _sds((AKV, T, HD)), _sds((AKV, T, HD))), grid=(AKV, N // bq),
                          in_specs=[qb, kvb, kvb, ob, hb, ob], out_specs=(hb, kvb, kvb),
                          vmem=VMEM_BIG)(q, k, v, o32, lse, do)


def _gprep_fn(kind, shifts, x, w):
    down, up = shifts
    y = down(x) * w[0:1, :] + x * w[1:2, :] + up(x) * w[2:3, :]
    a = _silu(y)
    if kind == 2:
        return a
    a = a * lax.rsqrt(jnp.sum(a * a, axis=-1, keepdims=True) + EPS)
    return a * (HD ** -0.5) if kind == 0 else a


def _gprep_fwd(proj, conv_w, kind, bounds):
    T = proj.shape[0]
    shifts = _make_shift(bounds)
    cb = C_QKV // HD + kind * GH

    def body(x_ref, w_ref, o_ref):
        o_ref[...] = _gprep_fn(kind, shifts, x_ref[...], w_ref[...])

    return _call(body, name=f"gprep_fwd{kind}", out_shape=_sds((GH, T, HD)), grid=(GH,),
                 in_specs=[pl.BlockSpec((T, HD), lambda h: (0, cb + h)),
                           pl.BlockSpec((3, HD), lambda h: (0, kind * GH + h))],
                 out_specs=pl.BlockSpec((None, T, HD), lambda h: (h, 0, 0)), sem=("parallel",))(proj, conv_w)


def _gprep_bwd(proj, conv_w, kind, bounds, dy, dproj):
    T = proj.shape[0]
    shifts = _make_shift(bounds)
    cb = C_QKV // HD + kind * GH

    def body(x_ref, w_ref, dy_ref, _, dx_ref, dw_ref):
        _, vjp = jax.vjp(functools.partial(_gprep_fn, kind, shifts), x_ref[...], w_ref[...])
        dx, dw = vjp(dy_ref[0] + dy_ref[1])
        dx_ref[...] = dx.astype(BF16)
        dw_ref[...] = dw

    return _call(body, name=f"gprep_bwd{kind}", out_shape=(_sds(dproj.shape, BF16), _sds((3, GH * HD))), grid=(GH,),
                 in_specs=[pl.BlockSpec((T, HD), lambda h: (0, cb + h)),
                           pl.BlockSpec((3, HD), lambda h: (0, kind * GH + h)),
                           pl.BlockSpec((2, None, T, HD), lambda h: (0, h, 0, 0)), ANYSPEC],
                 out_specs=(pl.BlockSpec((T, HD), lambda h: (0, cb + h)), pl.BlockSpec((3, HD), lambda h: (0, h))),
                 aliases={3: 0}, sem=("parallel",))(proj, conv_w, dy, dproj)


def _bl_fn(x, alog, dtb):
    lane = lax.broadcasted_iota(jnp.int32, x.shape, 1)
    beta = jax.nn.sigmoid(x)
    z = x + dtb
    sp = jnp.maximum(z, 0.0) + jnp.log1p(jnp.exp(-jnp.abs(z)))
    la = -jnp.exp(alog) * sp
    return jnp.where(lane < 2 * GH, beta, jnp.where(lane < 4 * GH, la, 0.0))


def _bl_fwd(proj, alog, dtb, *, br=256):
    T = proj.shape[0]

    def body(x_ref, a_ref, d_ref, o_ref):
        o_ref[...] = _bl_fn(x_ref[...], a_ref[...], d_ref[...])

    vec = pl.BlockSpec((1, HD), lambda i: (0, 0))
    return _call(body, name="bl_fwd", out_shape=_sds((T, HD)), grid=(T // br,),
                 in_specs=[pl.BlockSpec((br, HD), lambda i: (i, C_BL // HD)), vec, vec],
                 out_specs=pl.BlockSpec((br, HD), lambda i: (i, 0)), sem=("parallel",))(proj, alog, dtb)


def _bl_bwd(proj, alog, dtb, dbl, dproj, *, br=256):
    T = proj.shape[0]
    wide = C_Z - C_BL

    def body(x_ref, a_ref, d_ref, g_ref, _, dx_ref, da_ref, dd_ref):
        g = g_ref[0, 0]
        for d in range(2):
            for h in range(GH):
                if d or h:
                    g = g + g_ref[d, h]
        _, vjp = jax.vjp(_bl_fn, x_ref[...], a_ref[...], d_ref[...])
        dx, da, dd = vjp(g)
        dx_ref[:, :HD] = dx.astype(BF16)
        dx_ref[:, HD:] = jnp.zeros((br, wide - HD), BF16)

        @pl.when(pl.program_id(0) == 0)
        def _():
            da_ref[...] = jnp.zeros_like(da_ref)
            dd_ref[...] = jnp.zeros_like(dd_ref)

        da_ref[...] += da
        dd_ref[...] += dd

    vec = pl.BlockSpec((1, HD), lambda i: (0, 0))
    return _call(body, name="bl_bwd", out_shape=(_sds(dproj.shape, BF16), _sds((1, HD)), _sds((1, HD))), grid=(T // br,),
                 in_specs=[pl.BlockSpec((br, HD), lambda i: (i, C_BL // HD)), vec, vec,
                           pl.BlockSpec((2, GH, br, HD), lambda i: (0, 0, i, 0)), ANYSPEC],
                 out_specs=(pl.BlockSpec((br, wide), lambda i: (i, C_BL // wide)), vec, vec), aliases={4: 0},
                 sem=("arbitrary",))(proj, alog, dtb, dbl, dproj)


def _chunk_masks(d):
    ii = lax.broadcasted_iota(jnp.int32, (CH, CH), 0)
    jj = lax.broadcasted_iota(jnp.int32, (CH, CH), 1)
    eye = (ii == jj).astype(F32)
    before = jnp.where(d == 0, (jj < ii).astype(F32), (jj > ii).astype(F32))
    return before, before + eye, eye


def _intra_fn(masks, sel_b, sel_l, qs, ks, vs, bls, xs=None):
    before, ateq, eye = masks
    ones = jnp.ones((CH, CH), F32)
    inc = ateq > 0.0
    each = lambda f, *ls: [f(*t) for t in zip(*ls)]
    beta = each(lambda bl: jnp.sum(bl * sel_b, axis=-1, keepdims=True), bls)
    la = each(lambda bl: jnp.sum(bl * sel_l, axis=-1, keepdims=True), bls)
    gam = each(lambda a: _mask_nn(ateq, jnp.broadcast_to(a, (CH, HD))), la)
    gi = each(lambda a: _mask_nn(ateq, jnp.broadcast_to(a, (CH, CH))), la)
    gj = each(lambda g: _mask_nn(ones, eye * g), gi)
    kk = each(lambda k: _nt(k, k), ks)
    qk = each(_nt, qs, ks)
    dec = each(lambda a, b: jnp.where(inc, jnp.exp(jnp.where(inc, a - b, 0.0)), 0.0), gi, gj)
    lmat = each(lambda b, d, m: before * (b * d * m), beta, dec, kk)
    if xs is None:
        x = each(lambda m: eye - m, lmat)
        p2 = each(lambda m: _mdot(m, m), lmat)
        for it in range(4):
            y = each(lambda a, b: _mdot(jnp.concatenate([a, b], axis=0), b), x, p2)
            x = each(lambda a, t: a + t[:CH], x, y)
            p2 = each(lambda t: t[CH:], y)
        x = each(lambda a, b: a + _mdot(a, b), x, p2)
    else:
        x = each(_saved_inverse, lmat, xs)
    eg = each(jnp.exp, gam)
    u = each(lambda a, b, v: _mdot(a, b * v), x, beta, vs)
    w = each(lambda a, b, e, k: _mdot(a, (b * e) * k), x, beta, eg, ks)
    tot = each(lambda a: jnp.sum(a, axis=0, keepdims=True), la)
    kd = each(lambda k, t, g: k * jnp.exp(t - g), ks, tot, gam)
    gl = each(lambda t: jnp.broadcast_to(jnp.exp(t), (1, HD)), tot)
    qd = each(lambda q, e: q * e, qs, eg)
    p = each(lambda d, m: d * m, dec, qk)
    return (u, w, kd, qd, p, gl, x) if xs is None else (u, w, kd, qd, p, gl)


def _dir_head_sel(d, h):
    lane = lax.broadcasted_iota(jnp.int32, (1, HD), 1)
    return (lane == d * GH + h).astype(F32), (lane == 2 * GH + d * GH + h).astype(F32)


def _intra_specs(T, G):
    nc = T // CH
    assert nc % G == 0
    qkv = pl.BlockSpec((None, G * CH, HD), lambda d, h, c: (h, c, 0))
    bl = pl.BlockSpec((G * CH, HD), lambda d, h, c: (c, 0))
    big = pl.BlockSpec((None, None, G * CH, HD), lambda d, h, c: (d, h, c, 0))
    pm = pl.BlockSpec((None, None, G * CH, CH), lambda d, h, c: (d, h, c, 0))
    gl = pl.BlockSpec((None, None, G, 1, HD), lambda d, h, c: (d, h, c, 0, 0))
    shapes = (_sds((2, GH, T, HD)),) + (_sds((2, GH, T, HD), BF16),) * 3 + (
        _sds((2, GH, T, CH), BF16), _sds((2, GH, nc, 1, HD)), _sds((2, GH, T, CH)))
    return nc, qkv, bl, big, pm, gl, shapes


def _chunks_per_step(T, most):
    nc = T // CH
    return max(g for g in range(1, most + 1) if nc % g == 0)


def _intra_fwd(q, k, v, bl, exch):
    T = q.shape[1]
    G = _chunks_per_step(T, INTRA_FWD_CHUNKS)
    nc, qkv_s, bl_s, big, pm, gl_s, shapes = _intra_specs(T, G)

    def body(q_ref, k_ref, v_ref, bl_ref, u_ref, w_ref, kd_ref, qd_ref, p_ref, gl_ref, x_ref):
        d, h = pl.program_id(0), pl.program_id(1)
        sb, sl = _dir_head_sel(d, h)
        rows = [slice(g * CH, (g + 1) * CH) for g in range(G)]
        outs = _intra_fn(_chunk_masks(d), sb, sl, *[[r[s, :] for s in rows] for r in (q_ref, k_ref, v_ref, bl_ref)])
        for g in range(G):
            for r, o in zip((u_ref, w_ref, kd_ref, qd_ref, p_ref, x_ref), outs[:5] + outs[6:]):
                r[rows[g], :] = o[g].astype(r.dtype)
            gl_ref[g] = outs[5][g]

    return _call_carrying(body, exch, name="gdn_intra_fwd", out_shape=shapes, grid=(2, GH, nc // G),
                          in_specs=[qkv_s, qkv_s, qkv_s, bl_s], out_specs=(big, big, big, big, pm, gl_s, pm))(q, k, v, bl)


def _intra_bwd(q, k, v, bl, xinv, cts, exch):
    T = q.shape[1]
    G = _chunks_per_step(T, INTRA_BWD_CHUNKS)
    nc, qkv_s, bl_s, big, pm, gl_s, _ = _intra_specs(T, G)

    def body(q_ref, k_ref, v_ref, bl_ref, x_ref, du, dw, dkd, dqd, dp, dgl, dq_ref, dk_ref, dv_ref, dbl_ref):
        d, h = pl.program_id(0), pl.program_id(1)
        sb, sl = _dir_head_sel(d, h)
        rows = [slice(g * CH, (g + 1) * CH) for g in range(G)]
        fn = functools.partial(_intra_fn, _chunk_masks(d), sb, sl, xs=[x_ref[s, :] for s in rows])
        _, vjp = jax.vjp(fn, *[[r[s, :] for s in rows] for r in (q_ref, k_ref, v_ref, bl_ref)])
        cts = tuple([r[s, :] for s in rows] for r in (du, dw, dkd, dqd, dp)) + ([dgl[g] for g in range(G)],)
        grads = vjp(cts)
        for g in range(G):
            for r, o in zip((dq_ref, dk_ref, dv_ref, dbl_ref), grads):
                r[rows[g], :] = o[g]

    return _call_carrying(body, exch, name="gdn_intra_bwd", out_shape=(_sds((2, GH, T, HD)),) * 4,
                          grid=(2, GH, nc // G), in_specs=[qkv_s, qkv_s, qkv_s, bl_s, pm, big, big, big, big, pm, gl_s],
                          out_specs=(big,) * 4)(q, k, v, bl, xinv, *cts)


def _scan_fn(s, u, w, kd, qd, p, gl):
    each = lambda f, *ls: [f(*t) for t in zip(*ls)]
    ws = each(_nn, w, s)
    delta = each(lambda a, b: a - b, u, ws)
    kdd = each(_tn, kd, delta)
    s_new = each(lambda g, a, b: g * a + b, gl, s, kdd)
    qs = each(_nn, qd, s)
    pd = each(_nn, p, delta)
    return each(lambda a, b: a + b, qs, pd), s_new


SCAN_BLOCK = 4


def _scan_visit(t, d, nb, ncb):
    rev = jnp.where(t < ncb, ncb - 1 - t, nb - 1 - (t - ncb))
    return jnp.where(d == 0, t, rev)


def _scan_specs(T, L, back):
    tb = SCAN_BLOCK * CH
    assert T % tb == 0 and L % tb == 0
    nb, ncb = T // tb, L // tb

    def at(d, t):
        return _scan_visit(nb - 1 - t if back else t, d, nb, ncb)

    big = pl.BlockSpec((None, GH, tb, HD), lambda d, t: (d, 0, at(d, t), 0))
    pm = pl.BlockSpec((None, GH, tb, CH), lambda d, t: (d, 0, at(d, t), 0))
    gl = pl.BlockSpec((None, GH, SCAN_BLOCK, 1, HD), lambda d, t: (d, 0, at(d, t), 0, 0))
    st = pl.BlockSpec((None, GH, SCAN_BLOCK, HD, HD), lambda d, t: (d, 0, at(d, t), 0, 0))
    do = pl.BlockSpec((GH, tb, HD), lambda d, t: (0, at(d, t), 0))
    return nb, big, pm, gl, st, do


def _scan_fwd(u, w, kd, qd, p, gl, L):
    T = u.shape[2]
    nb, big, pm, gl_s, st, _ = _scan_specs(T, L, False)
    heads = range(GH)

    def body(u_ref, w_ref, kd_ref, qd_ref, p_ref, gl_ref, o_ref, st_ref, s_scr):
        d = pl.program_id(0)

        @pl.when(pl.program_id(1) == 0)
        def _():
            s_scr[...] = jnp.zeros_like(s_scr)

        s = [s_scr[h] for h in heads]
        for i in range(SCAN_BLOCK):
            c = jnp.where(d == 0, i, SCAN_BLOCK - 1 - i)
            rows = pl.ds(pl.multiple_of(c * CH, CH), CH)
            for h in heads:
                st_ref[h, c] = s[h]
            o, s = _scan_fn(s, *[[r[h, rows, :].astype(F32) for h in heads] for r in (u_ref, w_ref, kd_ref, qd_ref, p_ref)],
                            [gl_ref[h, c] for h in heads])
            for h in heads:
                o_ref[h, rows, :] = o[h]
        for h in heads:
            s_scr[h] = s[h]

    return _call(body, name="gdn_scan_fwd", out_shape=(_sds((2, GH, T, HD)), _sds((2, GH, T // CH, HD, HD))),
                 grid=(2, nb), in_specs=[big, big, big, big, pm, gl_s], out_specs=(big, st),
                 scratch=[pltpu.VMEM((GH, HD, HD), F32)], sem=("parallel", "arbitrary"))(u, w, kd, qd, p, gl)


def _scan_bwd(u, w, kd, qd, p, gl, states, do, L, exch):
    T = u.shape[2]
    nb, big, pm, gl_s, st, do_s = _scan_specs(T, L, True)
    heads = range(GH)

    def body(u_ref, w_ref, kd_ref, qd_ref, p_ref, gl_ref, st_ref, do_ref,
             du_ref, dw_ref, dkd_ref, dqd_ref, dp_ref, dgl_ref, ds_scr):
        d = pl.program_id(0)

        @pl.when(pl.program_id(1) == 0)
        def _():
            ds_scr[...] = jnp.zeros_like(ds_scr)

        ds = [ds_scr[h] for h in heads]
        for i in range(SCAN_BLOCK):
            c = jnp.where(d == 0, SCAN_BLOCK - 1 - i, i)
            rows = pl.ds(pl.multiple_of(c * CH, CH), CH)
            _, vjp = jax.vjp(_scan_fn, [st_ref[h, c] for h in heads],
                             *[[r[h, rows, :].astype(F32) for h in heads] for r in (u_ref, w_ref, kd_ref, qd_ref, p_ref)],
                             [gl_ref[h, c] for h in heads])
            ds, gu, gw, gkd, gqd, gp, ggl = vjp(([do_ref[h, rows, :] for h in heads], ds))
            for h in heads:
                du_ref[h, rows, :] = gu[h]
                dw_ref[h, rows, :] = gw[h]
                dkd_ref[h, rows, :] = gkd[h]
                dqd_ref[h, rows, :] = gqd[h]
                dp_ref[h, rows, :] = gp[h]
                dgl_ref[h, c] = ggl[h]
        for h in heads:
            ds_scr[h] = ds[h]

    return _call_carrying(
        body, exch, name="gdn_scan_bwd",
        out_shape=(_sds((2, GH, T, HD)),) * 4 + (_sds((2, GH, T, CH)), _sds((2, GH, T // CH, 1, HD))),
        grid=(2, nb), in_specs=[big, big, big, big, pm, gl_s, st, do_s], out_specs=(big, big, big, big, pm, gl_s),
        scratch=[pltpu.VMEM((GH, HD, HD), F32)])(u, w, kd, qd, p, gl, states, do)


def _gout_fn(o0, o1, z, gw):
    return _rms(o0 + o1) * gw * _silu(z)


def _gout_fwd(o, proj, gw, L):
    T = o.shape[2]
    N = T - L
    ob = pl.BlockSpec((2, None, T, HD), lambda h: (0, h, 0, 0))

    def body(o_ref, z_ref, gw_ref, y_ref):
        y_ref[...] = _gout_fn(o_ref[0, L:, :], o_ref[1, L:, :], z_ref[L:, :], gw_ref[...]).astype(BF16)

    return _call(body, name="gout_fwd", out_shape=_sds((N, GH * HD), BF16), grid=(GH,),
                 in_specs=[ob, pl.BlockSpec((T, HD), lambda h: (0, C_Z // HD + h)), pl.BlockSpec((1, HD), lambda h: (0, 0))],
                 out_specs=pl.BlockSpec((N, HD), lambda h: (0, h)), sem=("parallel",))(o, proj, gw)


def _gout_bwd(o, proj, gw, dy, dproj, L):
    T = o.shape[2]
    N = T - L
    ob = pl.BlockSpec((2, None, T, HD), lambda h: (0, h, 0, 0))

    def body(o_ref, z_ref, gw_ref, dy_ref, _, do_ref, dz_ref, dgw_ref):
        _, vjp = jax.vjp(_gout_fn, o_ref[0, L:, :], o_ref[1, L:, :], z_ref[L:, :], gw_ref[...])
        g0, _, gz, ggw = vjp(dy_ref[...])
        do_ref[:L, :] = jnp.zeros((L, HD), F32)
        do_ref[L:, :] = g0
        dz_ref[:L, :] = jnp.zeros((L, HD), BF16)
        dz_ref[L:, :] = gz.astype(BF16)

        @pl.when(pl.program_id(0) == 0)
        def _():
            dgw_ref[...] = jnp.zeros_like(dgw_ref)

        dgw_ref[...] += ggw

    zb = pl.BlockSpec((T, HD), lambda h: (0, C_Z // HD + h))
    return _call(body, name="gout_bwd", out_shape=(_sds((GH, T, HD)), _sds(dproj.shape, BF16), _sds((1, HD))),
                 grid=(GH,),
                 in_specs=[ob, zb, pl.BlockSpec((1, HD), lambda h: (0, 0)), pl.BlockSpec((N, HD), lambda h: (0, h)), ANYSPEC],
                 out_specs=(pl.BlockSpec((None, T, HD), lambda h: (h, 0, 0)), zb, pl.BlockSpec((1, HD), lambda h: (0, 0))),
                 aliases={4: 1}, sem=("arbitrary",))(o, proj, gw, dy, dproj)


def _merge_fn(pa, pd, ga, gd):
    return jax.nn.sigmoid(ga) * pa + jax.nn.sigmoid(gd) * pd


def _merge_fwd(pa, pd, proj, L, *, br=256):
    N = pa.shape[0]
    lb = L // br
    row = pl.BlockSpec((br, D), lambda i: (i, 0))

    def body(pa_ref, pd_ref, ga_ref, gd_ref, y_ref):
        y_ref[...] = _merge_fn(pa_ref[...], pd_ref[...], ga_ref[...], gd_ref[...]).astype(BF16)

    return _call(body, name="merge_fwd", out_shape=_sds((N, D), BF16), grid=(N // br,),
                 in_specs=[row, row, pl.BlockSpec((br, D), lambda i: (i + lb, C_GATE // D)),
                           pl.BlockSpec((br, D), lambda i: (i + lb, C_GATE // D + 1))],
                 out_specs=row, sem=("parallel",))(pa, pd, proj, proj)


def _merge_bwd(pa, pd, proj, dy, L, *, br=256):
    N = pa.shape[0]
    T = N + L
    lb = L // br
    lrow = pl.BlockSpec((br, D), lambda i: (jnp.maximum(i - lb, 0), 0))

    def body(pa_ref, pd_ref, ga_ref, gd_ref, dy_ref, dpa_ref, dpd_ref, dg_ref):
        lat = pl.program_id(0) >= lb
        _, vjp = jax.vjp(_merge_fn, pa_ref[...], pd_ref[...], ga_ref[...], gd_ref[...])
        gpa, gpd, gga, ggd = vjp(dy_ref[...])
        dpa_ref[...] = gpa.astype(BF16)
        dpd_ref[...] = gpd.astype(BF16)
        dg_ref[:, :D] = jnp.where(lat, gga, 0.0).astype(BF16)
        dg_ref[:, D:] = jnp.where(lat, ggd, 0.0).astype(BF16)

    return _call(body, name="merge_bwd", out_shape=(_sds((N, D), BF16), _sds((N, D), BF16), _sds((T, C_END), BF16)),
                 grid=(T // br,),
                 in_specs=[lrow, lrow, pl.BlockSpec((br, D), lambda i: (i, C_GATE // D)),
                           pl.BlockSpec((br, D), lambda i: (i, C_GATE // D + 1)), lrow],
                 out_specs=(lrow, lrow, pl.BlockSpec((br, 2 * D), lambda i: (i, C_GATE // (2 * D)))),
                 sem=("arbitrary",))(pa, pd, proj, proj, dy)


def _resid_fwd(x, m, mod, i_g, *, name, br=256):
    R = x.shape[0]
    row = pl.BlockSpec((br, D), lambda i: (i, 0))

    def body(x_ref, m_ref, mod_ref, o_ref):
        o_ref[...] = x_ref[...] + mod_ref[i_g:i_g + 1, :] * m_ref[...]

    return _call(body, name=name, out_shape=_sds((R, D)), grid=(R // br,),
                 in_specs=[row, row, pl.BlockSpec((6, D), lambda i: (0, 0))], out_specs=row,
                 sem=("parallel",))(x, m, mod)


def _resid_bwd(dx, m, mod, i_g, *, name, br=256):
    R = dx.shape[0]
    row = pl.BlockSpec((br, D), lambda i: (i, 0))
    vec = pl.BlockSpec((1, D), lambda i: (0, 0))

    def body(dx_ref, m_ref, mod_ref, dm_ref, dg_ref):
        dxv = dx_ref[...]
        dm_ref[...] = (dxv * mod_ref[i_g:i_g + 1, :]).astype(BF16)

        @pl.when(pl.program_id(0) == 0)
        def _():
            dg_ref[...] = jnp.zeros_like(dg_ref)

        dg_ref[...] += jnp.sum(dxv * m_ref[...], axis=0, keepdims=True)

    return _call(body, name=name, out_shape=(_sds((R, D), BF16), _sds((1, D))), grid=(R // br,),
                 in_specs=[row, row, pl.BlockSpec((6, D), lambda i: (0, 0))], out_specs=(row, vec),
                 sem=("arbitrary",))(dx, m, mod)


def _ffn_fn(shifts, ug, uv, wg, wv, bg, bv):
    down, up = shifts

    def conv(x, w, b):
        return down(x) * w[0:1, :] + x * w[1:2, :] + up(x) * w[2:3, :] + b

    return _silu(conv(ug, wg, bg)) * conv(uv, wv, bv)


def _ffn_fwd(up, cw, cb, *, bw=256):
    N = up.shape[0]
    shifts = _make_shift(((0, N),))
    nb = DFF // bw

    def body(ug, uv, wg, wv, bg, bv, a_ref):
        a_ref[...] = _ffn_fn(shifts, ug[...], uv[...], wg[...], wv[...], bg[...], bv[...]).astype(BF16)

    def col(rows, off):
        return pl.BlockSpec((rows, bw), lambda j: (0, j + off))

    return _call(body, name="ffn_fwd", out_shape=_sds((N, DFF), BF16), grid=(nb,),
                 in_specs=[col(N, 0), col(N, nb), col(3, 0), col(3, nb), col(1, 0), col(1, nb)],
                 out_specs=col(N, 0), sem=("parallel",), vmem=VMEM_BIG)(up, up, cw, cw, cb, cb)


def _ffn_bwd(up, cw, cb, da, *, bw=256):
    N = up.shape[0]
    shifts = _make_shift(((0, N),))
    nb = DFF // bw

    def body(ug, uv, wg, wv, bg, bv, da_ref, dug, duv, dwg, dwv, dbg, dbv):
        _, vjp = jax.vjp(functools.partial(_ffn_fn, shifts), ug[...], uv[...], wg[...], wv[...], bg[...], bv[...])
        g = vjp(da_ref[...])
        dug[...] = g[0].astype(BF16)
        duv[...] = g[1].astype(BF16)
        dwg[...], dwv[...], dbg[...], dbv[...] = g[2], g[3], g[4], g[5]

    def col(rows, off):
        return pl.BlockSpec((rows, bw), lambda j: (0, j + off))

    half = (_sds((N, DFF), BF16), _sds((N, DFF), BF16), _sds((3, DFF)), _sds((3, DFF)), _sds((1, DFF)), _sds((1, DFF)))
    dug, duv, dwg, dwv, dbg, dbv = _call(
        body, name="ffn_bwd", out_shape=half, grid=(nb,),
        in_specs=[col(N, 0), col(N, nb), col(3, 0), col(3, nb), col(1, 0), col(1, nb), col(N, 0)],
        out_specs=(col(N, 0), col(N, 0), col(3, 0), col(3, 0), col(1, 0), col(1, 0)),
        sem=("parallel",), vmem=VMEM_BIG)(up, up, cw, cw, cb, cb, da)
    return (jnp.concatenate([dug, duv], axis=1), jnp.concatenate([dwg, dwv], axis=1),
            jnp.concatenate([dbg, dbv], axis=1))


def _head_fn(x1, dn, g2, fw, tgt):
    y = _rms(x1 + g2 * dn) * fw
    err = y - tgt
    return 0.5 * jnp.sum(jnp.mean(err * err, axis=-1))


def _head(x1, dn, mod, fw, tgt, *, br=256):
    N = x1.shape[0]
    row = pl.BlockSpec((br, D), lambda i: (i, 0))
    vec = pl.BlockSpec((1, D), lambda i: (0, 0))
    one = pl.BlockSpec((1, HD), lambda i: (0, 0))

    def body(x1_ref, dn_ref, mod_ref, fw_ref, tgt_ref, loss_ref, dx_ref, ddn_ref, dg_ref, dfw_ref):
        loss, (gx, gdn, gg, gfw) = jax.value_and_grad(_head_fn, argnums=(0, 1, 2, 3))(
            x1_ref[...], dn_ref[...], mod_ref[5:6, :], fw_ref[...], tgt_ref[...])
        dx_ref[...] = gx
        ddn_ref[...] = gdn.astype(BF16)

        @pl.when(pl.program_id(0) == 0)
        def _():
            loss_ref[...] = jnp.zeros_like(loss_ref)
            dg_ref[...] = jnp.zeros_like(dg_ref)
            dfw_ref[...] = jnp.zeros_like(dfw_ref)

        loss_ref[...] += jnp.broadcast_to(loss, (1, HD))
        dg_ref[...] += gg
        dfw_ref[...] += gfw

    return _call(body, name="head", out_shape=(_sds((1, HD)), _sds((N, D)), _sds((N, D), BF16), _sds((1, D)), _sds((1, D))),
                 grid=(N // br,), in_specs=[row, row, pl.BlockSpec((6, D), lambda i: (0, 0)), vec, row],
                 out_specs=(one, row, row, vec, vec), sem=("arbitrary",))(x1, dn, mod, fw, tgt)


def _adamw(w, g, m, v, *, name):
    shape = w.shape
    cols = shape[-1]
    rows = max(1, math.prod(shape[:-1]))
    w2, g2, m2, v2 = (t.reshape(rows, cols) for t in (w, g, m, v))
    br = 256 if rows % 256 == 0 else rows
    c1 = 1.0 - B1 ** STEP
    c2 = 1.0 - B2 ** STEP

    def body(w_ref, g_ref, m_ref, v_ref, d_ref, nm_ref, nv_ref):
        gv = g_ref[...]
        nm = B1 * m_ref[...] + (1.0 - B1) * gv
        nv = B2 * v_ref[...] + (1.0 - B2) * (gv * gv)
        d_ref[...] = -LR * ((nm / c1) / (jnp.sqrt(nv / c2) + AEPS) + WD * w_ref[...])
        nm_ref[...] = nm
        nv_ref[...] = nv

    blk = pl.BlockSpec((br, cols), lambda i: (i, 0))
    outs = _call(body, name=name, out_shape=(_sds((rows, cols)),) * 3, grid=(rows // br,),
                 in_specs=[blk] * 4, out_specs=(blk,) * 3, sem=("parallel",))(w2, g2, m2, v2)
    return tuple(t.reshape(shape) for t in outs)


def _adamw_many(items, *, name):
    k = len(items)
    shapes = [w.shape for w, _, _, _ in items]
    flat = [t.reshape(max(1, math.prod(t.shape[:-1])), t.shape[-1]) for it in items for t in it]
    c1 = 1.0 - B1 ** STEP
    c2 = 1.0 - B2 ** STEP

    def body(*refs):
        ins, outs = refs[:4 * k], refs[4 * k:]
        for i in range(k):
            w_ref, g_ref, m_ref, v_ref = ins[4 * i:4 * i + 4]
            gv = g_ref[...]
            nm = B1 * m_ref[...] + (1.0 - B1) * gv
            nv = B2 * v_ref[...] + (1.0 - B2) * (gv * gv)
            outs[3 * i][...] = -LR * ((nm / c1) / (jnp.sqrt(nv / c2) + AEPS) + WD * w_ref[...])
            outs[3 * i + 1][...] = nm
            outs[3 * i + 2][...] = nv

    res = _call(body, name=name, out_shape=tuple(_sds(flat[4 * i].shape) for i in range(k) for _ in range(3)))(*flat)
    return [tuple(res[3 * i + j].reshape(shapes[i]) for j in range(3)) for i in range(k)]


def _rope_tables(N, L):
    t = jnp.arange(N)
    pos = jnp.stack([(t // GRID_W).astype(F32), (t % GRID_W).astype(F32)], axis=1)
    inv = ROPE_THETA ** (-jnp.arange(0, HD // 2, 2, dtype=F32) / (HD // 2))
    ang = pos[:, :, None] * inv[None, None, :]
    cos = jnp.broadcast_to(jnp.cos(ang)[:, :, None, :], (N, 2, 2, HD // 4)).reshape(N, HD)
    sin = jnp.broadcast_to(jnp.sin(ang)[:, :, None, :], (N, 2, 2, HD // 4))
    sin = (sin * jnp.array([-1.0, 1.0], F32)[None, None, :, None]).reshape(N, HD)
    cos = jnp.concatenate([jnp.ones((L, HD), F32), cos], axis=0)
    sin = jnp.concatenate([jnp.zeros((L, HD), F32), sin], axis=0)
    return cos, sin


def _pad_lanes(v, off=0):
    return jnp.zeros((1, HD), F32).at[0, off:off + v.shape[0]].set(v)


def _local_step(x, ctx, tgt, mod_lat, mod_ctx, w_in, shards, small):
    N, L = x.shape[0], ctx.shape[0]
    T = N + L
    bounds = ((0, L), (L, T))
    qw, kw, gw = small["q_norm_w"], small["k_norm_w"], small["gdn_norm_w"]
    conv_w, ffn_w, ffn_b, fnw = small["conv_qkv_w"], small["ffn_conv_w"], small["ffn_conv_b"], small["final_norm_w"]
    alog = _pad_lanes(small["a_log"].reshape(-1), 2 * GH)
    dtb = _pad_lanes(small["dt_bias"].reshape(-1), 2 * GH)
    cos, sin = _rope_tables(N, L)
    bt = T
    bnl = 256 if N % 1024 else 1024

    hc = _normmod_fwd(ctx, mod_ctx, 0, 1, name="normmod_ctx")
    hx = _normmod_fwd(x, mod_lat, 0, 1, name="normmod_x")
    h1 = jnp.concatenate([hc, hx], axis=0)
    proj = _mm(h1, w_in, name="mm_in", M=T, N=C_END, K=D, tb=True, bm=bt, bn=1024)
    aq, ak, av = _aprep_fwd(proj, cos, sin, qw, kw)
    (attn, attn32, lse), (up_g,) = _attn_fwd(aq, ak, av, L, _GatherTwoLevel([shards["w_up"]]))
    gq = _gprep_fwd(proj, conv_w, 0, bounds)
    gk = _gprep_fwd(proj, conv_w, 1, bounds)
    gv = _gprep_fwd(proj, conv_w, 2, bounds)
    bl = _bl_fwd(proj, alog, dtb)
    intra, (down_g, pa_g, pd_g, out_g) = _intra_fwd(
        gq, gk, gv, bl, _GatherTwoLevel([shards[n] for n in ("w_down", "w_pa", "w_pd", "w_out")]))
    w_up, w_down = up_g.reshape(2 * DFF, D), down_g.reshape(DFF, D)
    w_pa, w_pd, w_out = pa_g.reshape(D, D), pd_g.reshape(D, D), out_g.reshape(D, D)
    xinv, intra = intra[6], intra[:6]
    o, states = _scan_fwd(*intra, L)
    gdn = _gout_fwd(o, proj, gw, L)
    pa = _mm(attn, w_pa, name="mm_pa", M=N, N=D, K=D, bm=bnl)
    pd = _mm(gdn, w_pd, name="mm_pd", M=N, N=D, K=D, bm=bnl)
    y = _merge_fwd(pa, pd, proj, L)
    m = _mm(y, w_out, name="mm_out", M=N, N=D, K=D, bm=bnl)
    x1 = _resid_fwd(x, m, mod_lat, 2, name="resid1")
    h2 = _normmod_fwd(x1, mod_lat, 3, 4, name="normmod_x1")
    up = _mm(h2, w_up, name="mm_up", M=N, N=2 * DFF, K=D, tb=True, bm=bnl, bn=2 * DFF // 4)
    a = _ffn_fwd(up, ffn_w, ffn_b)
    dn = _mm(a, w_down, name="mm_down", M=N, N=D, K=DFF, bm=bnl)
    loss, dx2, ddn, dg2, dfnw = _head(x1, dn, mod_lat, fnw, tgt)

    da = _mm(ddn, w_down, name="mm_down_dx", M=N, N=DFF, K=D, tb=True, bm=bnl, bn=DFF // 2)
    g_down = _mm(a, ddn, name="mm_down_dw", M=DFF, N=D, K=N, ta=True, bm=DFF // 2, out_dtype=BF16)
    dup, d_ffn_w, d_ffn_b = _ffn_bwd(up, ffn_w, ffn_b, da)
    dh2 = _mm(dup, w_up, name="mm_up_dx", M=N, N=D, K=2 * DFF, bm=bnl, bk=2 * DFF // 4)
    g_up = _mm(dup, h2, name="mm_up_dw", M=2 * DFF, N=D, K=N, ta=True, bm=2 * DFF // 4, out_dtype=BF16)
    dx1, dsh2, dsc2 = _normmod_bwd(x1, mod_lat, 3, 4, dh2, 0, dx2, name="normmod_x1_bwd")
    dm, dg1 = _resid_bwd(dx1, m, mod_lat, 2, name="resid1_bwd")
    dy = _mm(dm, w_out, name="mm_out_dx", M=N, N=D, K=D, tb=True, bm=bnl)
    g_out = _mm(y, dm, name="mm_out_dw", M=D, N=D, K=N, ta=True, out_dtype=BF16)
    dpa, dpd, dproj = _merge_bwd(pa, pd, proj, dy, L)
    dattn = _mm(dpa, w_pa, name="mm_pa_dx", M=N, N=D, K=D, tb=True, bm=bnl)
    g_pa = _mm(attn, dpa, name="mm_pa_dw", M=D, N=D, K=N, ta=True, out_dtype=BF16)
    dgdn = _mm(dpd, w_pd, name="mm_pd_dx", M=N, N=D, K=D, tb=True, bm=bnl)
    g_pd = _mm(gdn, dpd, name="mm_pd_dw", M=D, N=D, K=N, ta=True, out_dtype=BF16)
    do, dproj, dgw = _gout_bwd(o, proj, gw, dgdn, dproj, L)
    cts, recv_a = _scan_bwd(*intra, states, do, L, _Exchange([g_out.reshape(NDEV, D // NDEV, D)], True))
    (dgq, dgk, dgv, dbl), recv_b = _intra_bwd(gq, gk, gv, bl, xinv, cts, _Exchange(
        [g_up.reshape(NDEV, 2 * DFF // NDEV, D)], True))
    dproj, dwq = _gprep_bwd(proj, conv_w, 0, bounds, dgq, dproj)
    dproj, dwk = _gprep_bwd(proj, conv_w, 1, bounds, dgk, dproj)
    dproj, dwv = _gprep_bwd(proj, conv_w, 2, bounds, dgv, dproj)
    dproj, dalog, ddtb = _bl_bwd(proj, alog, dtb, dbl, dproj)
    (daq_h, dak_h, dav_h), recv_c = _attn_bwd(aq, ak, av, attn32, lse, dattn, L, _Exchange(
        [g_pa.reshape(NDEV, D // NDEV, D), g_pd.reshape(NDEV, D // NDEV, D), g_down.reshape(NDEV, DFF // NDEV, D)], True))
    recv = dict(zip(("w_out", "w_up", "w_pa", "w_pd", "w_down"), recv_a + recv_b + recv_c))
    dproj, dqw, dkw = _aprep_bwd(proj, cos, sin, qw, kw, daq_h, dak_h, dav_h, dproj, L)
    g_in = _mm(dproj, h1, name="mm_in_dw", M=C_END, N=D, K=T, ta=True, bm=1024, out_dtype=BF16)
    g_in = _unpad_columns(g_in).reshape(NDEV, W_END // NDEV, D)
    own_in = lax.dynamic_index_in_dim(g_in, _position()[3], axis=0, keepdims=False)
    *pending, token = _scatter_start(g_in, None, (0, D // 2), (), name="scatter_g_in_a_start")
    dh1 = _mm(dproj, w_in, name="mm_in_dx", M=T, N=D, K=C_END, bm=bt, bk=1024, after=(token,))
    grad_x, dsh1, dsc1 = _normmod_bwd(x, mod_lat, 0, 1, dh1, L, dx1, name="normmod_x_bwd")
    _, dcsh1, dcsc1 = _normmod_bwd(ctx, mod_ctx, 0, 1, dh1, 0, None, name="normmod_ctx_bwd")

    z1 = jnp.zeros((1, D), F32)
    dmod_lat = jnp.concatenate([dsh1, dsc1, dg1, dsh2, dsc2, dg2], axis=0)
    dmod_ctx = jnp.concatenate([dcsh1, dcsc1, z1, z1, z1, z1], axis=0)
    gsmall = {
        "q_norm_w": dqw, "k_norm_w": dkw, "gdn_norm_w": dgw,
        "conv_qkv_w": jnp.concatenate([dwq, dwk, dwv], axis=1),
        "a_log": dalog[0, 2 * GH:4 * GH], "dt_bias": ddtb[0, 2 * GH:4 * GH],
        "ffn_conv_w": d_ffn_w, "ffn_conv_b": d_ffn_b, "final_norm_w": dfnw,
    }
    return loss[0, 0], grad_x, (pending, own_in), recv, dmod_lat, dmod_ctx, gsmall


HBM = pl.BlockSpec(memory_space=pltpu.HBM)
ANYSPEC = pl.BlockSpec(memory_space=pl.ANY)


def _position():
    x, y, c = lax.axis_index("x"), lax.axis_index("y"), lax.axis_index("c")
    return x, y, c, 4 * x + 2 * y + c


def _peer(x, y, c, k):
    px = 1 - x if k & 4 else x
    py = 1 - y if k & 2 else y
    pc = 1 - c if k & 1 else c
    return (px, py, pc), 4 * px + 2 * py + pc


def _exchange(arrs, *, name, scatter):
    exch = _Exchange(arrs, scatter)
    n = exch.n

    def body(*refs):
        ins, outs, sems = refs[:n], refs[n:2 * n], refs[2 * n:]
        exch.start(ins, outs, sems)
        exch.finish(ins, outs, sems)

    outs = pl.pallas_call(body, name=name, out_shape=exch.out_shape, in_specs=[HBM] * n, out_specs=(HBM,) * n,
                          scratch_shapes=exch.scratch,
                          compiler_params=pltpu.CompilerParams(has_side_effects=True))(*arrs)
    return list(outs)


class _Exchange:
    def __init__(self, arrs, scatter):
        self.arrs, self.scatter, self.n = list(arrs), scatter, len(arrs)
        self.out_shape = tuple(_sds(a.shape if scatter else (NDEV,) + a.shape, a.dtype) for a in arrs)
        self.scratch = [pltpu.SemaphoreType.DMA((self.n, NDEV - 1)), pltpu.SemaphoreType.DMA((self.n, NDEV - 1)),
                        pltpu.SemaphoreType.DMA((self.n,))]

    def _copies(self, ins, outs, sems):
        send, recv, loc = sems
        x, y, c, me = _position()
        local = [pltpu.make_async_copy(ins[a].at[me] if self.scatter else ins[a], outs[a].at[me], loc.at[a])
                 for a in range(self.n)]
        remote = []
        for k in range(1, NDEV):
            peer, pid = _peer(x, y, c, k)
            for a in range(self.n):
                src = ins[a].at[pid] if self.scatter else ins[a]
                remote.append(pltpu.make_async_remote_copy(
                    src_ref=src, dst_ref=outs[a].at[me], send_sem=send.at[a, k - 1], recv_sem=recv.at[a, k - 1],
                    device_id=peer, device_id_type=MESH))
        return local, remote

    def start(self, ins, outs, sems):
        local, remote = self._copies(ins, outs, sems)
        for cp in local + remote:
            cp.start()

    def finish(self, ins, outs, sems):
        local, remote = self._copies(ins, outs, sems)
        for cp in remote:
            cp.wait()
        for cp in local:
            cp.wait()


class _GatherTwoLevel:
    scatter = False

    def __init__(self, arrs):
        self.arrs, self.n = list(arrs), len(arrs)
        self.out_shape = tuple(_sds((NDEV,) + a.shape, a.dtype) for a in arrs)
        self.scratch = [pltpu.SemaphoreType.DMA((self.n, NDEV - 1)), pltpu.SemaphoreType.DMA((self.n, NDEV - 1)),
                        pltpu.SemaphoreType.DMA((self.n,))]

    def _parts(self, ins, outs, sems):
        send, recv, loc = sems
        x, y, c, _ = _position()
        me, sibling = (x, y, c), (x, y, 1 - c)
        chips = [(1 - x, y), (x, 1 - y), (1 - x, 1 - y)]
        parts = []
        for a in range(self.n):
            slot = lambda px, py, pc, a=a: outs[a].at[4 * px + 2 * py + pc]

            def copy(k, owner, to, src=None, a=a, slot=slot):
                return pltpu.make_async_remote_copy(
                    src_ref=slot(*owner) if src is None else src, dst_ref=slot(*owner), send_sem=send.at[a, k],
                    recv_sem=recv.at[a, k], device_id=to, device_id_type=MESH)

            parts.append(dict(
                mine=pltpu.make_async_copy(ins[a], slot(*me), loc.at[a]),
                first=[copy(0, me, sibling, src=ins[a])] + [copy(1 + j, me, (*ch, c), src=ins[a]) for j, ch in enumerate(chips)],
                arrive=[copy(1 + j, (*ch, c), me) for j, ch in enumerate(chips)],
                passed=[copy(4 + j, (*ch, c), sibling) for j, ch in enumerate(chips)],
                rest=[copy(0, sibling, me)] + [copy(4 + j, (*ch, 1 - c), me) for j, ch in enumerate(chips)]))
        return parts

    def start(self, ins, outs, sems):
        for p in self._parts(ins, outs, sems):
            p["mine"].start()
            for cp in p["first"]:
                cp.start()

    def middle(self, ins, outs, sems):
        for p in self._parts(ins, outs, sems):
            for got, fwd in zip(p["arrive"], p["passed"]):
                got.wait_recv()
                fwd.start()

    def finish(self, ins, outs, sems):
        for p in self._parts(ins, outs, sems):
            for cp in p["rest"]:
                cp.wait_recv()
            for cp in p["first"] + p["passed"]:
                cp.wait_send()
            p["mine"].wait()


def _gather_two_level(block, *, name):
    def body(x_ref, out_ref, send_sems, recv_sems, local_sem):
        x, y, c, _ = _position()
        me, sibling = (x, y, c), (x, y, 1 - c)
        chips = [(1 - x, y), (x, 1 - y), (1 - x, 1 - y)]

        def slot(px, py, pc):
            return out_ref.at[4 * px + 2 * py + pc]

        def copy(k, owner, to, src=None):
            return pltpu.make_async_remote_copy(
                src_ref=slot(*owner) if src is None else src, dst_ref=slot(*owner), send_sem=send_sems.at[k],
                recv_sem=recv_sems.at[k], device_id=to, device_id_type=MESH)

        mine = pltpu.make_async_copy(x_ref, slot(*me), local_sem)
        mine.start()
        first = [copy(0, me, sibling, src=x_ref)]
        first += [copy(1 + j, me, (*chip, c), src=x_ref) for j, chip in enumerate(chips)]
        for cp in first:
            cp.start()
        passed = [copy(4 + j, (*chip, c), sibling) for j, chip in enumerate(chips)]
        for j, chip in enumerate(chips):
            copy(1 + j, (*chip, c), me).wait_recv()
            passed[j].start()
        copy(0, sibling, me).wait_recv()
        for j, chip in enumerate(chips):
            copy(4 + j, (*chip, 1 - c), me).wait_recv()
        for cp in first + passed:
            cp.wait_send()
        mine.wait()

    return pl.pallas_call(
        body, name=name, out_shape=_sds((NDEV,) + block.shape, block.dtype), in_specs=[HBM], out_specs=HBM,
        scratch_shapes=[pltpu.SemaphoreType.DMA((NDEV - 1,)), pltpu.SemaphoreType.DMA((NDEV - 1,)),
                        pltpu.SemaphoreType.DMA],
        compiler_params=pltpu.CompilerParams(has_side_effects=True))(block)


SEM = pl.BlockSpec(memory_space=pltpu.SEMAPHORE)


def _scatter_copies(src_ref, land_ref, send_sems, recv_sems, cols):
    x, y, c, me = _position()
    span = (slice(None), pl.ds(*cols))
    copies = []
    for k in range(1, NDEV):
        peer, pid = _peer(x, y, c, k)
        copies.append(pltpu.make_async_remote_copy(
            src_ref=src_ref.at[pid].at[span], dst_ref=land_ref.at[me].at[span], send_sem=send_sems.at[k - 1],
            recv_sem=recv_sems.at[k - 1], device_id=peer, device_id_type=MESH))
    return copies


SPLIT_EFFECT = pltpu.SideEffectType.DATAFLOW_SIDE_EFFECTING


def _scatter_start(parts, land, cols, after, *, name):
    na = len(after)
    if land is None:
        land = lax.empty(parts.shape, parts.dtype)

    def body(src_ref, land_ref, *rest):
        send_sems, recv_sems, _, _, token = rest[na:]
        for cp in _scatter_copies(src_ref, land_ref, send_sems, recv_sems, cols):
            cp.start()
        token[...] = jnp.zeros_like(token)

    return pl.pallas_call(
        body, name=name,
        out_shape=(pltpu.SemaphoreType.DMA((NDEV - 1,)), pltpu.SemaphoreType.DMA((NDEV - 1,)),
                   pltpu.HBM(parts.shape, parts.dtype), pltpu.HBM(parts.shape, parts.dtype), _sds((8, HD))),
        in_specs=(HBM, HBM) + (pl.BlockSpec(memory_space=pl.ANY),) * na,
        out_specs=(SEM, SEM, HBM, HBM, pl.BlockSpec(memory_space=pltpu.VMEM)),
        input_output_aliases={0: 2, 1: 3}, compiler_params=pltpu.CompilerParams(has_side_effects=SPLIT_EFFECT),
    )(pltpu.with_memory_space_constraint(parts, pltpu.HBM), pltpu.with_memory_space_constraint(land, pltpu.HBM), *after)


def _scatter_wait(send_sems, recv_sems, src_thru, land_thru, cols, after, *, name):
    na = len(after)

    def body(src_ref, land_ref, send_sems, recv_sems, *rest):
        for cp in _scatter_copies(src_ref, land_ref, send_sems, recv_sems, cols):
            cp.wait_send()
            cp.wait_recv()

    return pl.pallas_call(
        body, name=name,
        out_shape=(pltpu.HBM(src_thru.shape, src_thru.dtype), pltpu.HBM(land_thru.shape, land_thru.dtype)),
        in_specs=(HBM, HBM, SEM, SEM) + (pl.BlockSpec(memory_space=pl.ANY),) * na, out_specs=(HBM, HBM),
        input_output_aliases={0: 0, 1: 1}, compiler_params=pltpu.CompilerParams(has_side_effects=SPLIT_EFFECT),
    )(src_thru, land_thru, send_sems, recv_sems, *after)


def _cast_bf16(w, *, name):
    rows, cols = w.shape
    br = 128 if rows % 128 == 0 else rows

    def body(w_ref, o_ref):
        o_ref[...] = w_ref[...].astype(BF16)

    blk = pl.BlockSpec((br, cols), lambda i: (i, 0))
    return _call(body, name=name, out_shape=_sds((rows, cols), BF16), grid=(rows // br,), in_specs=[blk],
                 out_specs=blk, sem=("parallel",))(w)


def _sum_slots(a, *, name):
    _, R, C = a.shape

    def body(a_ref, o_ref):
        s = a_ref[0]
        for d in range(1, NDEV):
            s = s + a_ref[d]
        o_ref[...] = s

    return _call(body, name=name, out_shape=_sds((R, C)))(a)


MODROWS = 16


def _mod_fwd(c9, w, b):
    cols = w.shape[1]

    def body(c_ref, w_ref, b_ref, o_ref):
        o_ref[...] = _nn(_silu(c_ref[...]), w_ref[...]) + b_ref[...]

    return _call(body, name="mod_fwd", out_shape=_sds((MODROWS, cols)))(c9, w, b)


def _mod_bwd(c9, dmy, dall, w):
    cols = w.shape[1]

    def body(c_ref, dmy_ref, dall_ref, w_ref, gw_ref, gb_ref, cp_ref):
        sc = _silu(c_ref[...])
        rows = lax.broadcasted_iota(jnp.int32, (MODROWS, 1), 0)
        d = dmy_ref[...]
        d_ctx = jnp.where(rows == NDEV, d, 0.0)
        sc_ctx = jnp.where(rows == NDEV, sc, 0.0)
        outer = lax.dot_general(sc_ctx, d_ctx, (((0,), (0,)), ((), ())), precision=HI, preferred_element_type=F32)
        gw_ref[...] = _tn(jnp.where(rows < NDEV, sc, 0.0), jnp.where(rows < NDEV, d, 0.0)) + outer
        gb_ref[...] = jnp.sum(dall_ref[...], axis=0, keepdims=True)
        cp_ref[...] = jnp.sum(_nt(d_ctx, w_ref[...]), axis=0, keepdims=True)

    return _call(body, name="mod_bwd", out_shape=(_sds((D, cols)), _sds((1, 6 * D)), _sds((1, D))),
                 vmem=VMEM_BIG)(c9, dmy, dall, w)


def _cctx_finish(parts, c_ctx, after):
    VM = pl.BlockSpec(memory_space=pltpu.VMEM)

    def body(p_ref, c_ref, *rest):
        o_ref = rest[-1]
        s = p_ref[0]
        for d in range(1, NDEV):
            s = s + p_ref[d]
        _, vjp = jax.vjp(_silu, c_ref[...])
        o_ref[...] = vjp(s)[0]

    return _call(body, name="cctx_finish", out_shape=_sds((1, D)),
                 in_specs=[VM, VM] + [pl.BlockSpec(memory_space=pl.ANY)] * len(after))(parts, c_ctx, *after)


def _adamw_recv(w, recv, m, v, *, name, own=None):
    rows, cols = w.shape
    bc = 256
    c1 = 1.0 - B1 ** STEP
    c2 = 1.0 - B2 ** STEP
    has_own = own is not None

    def body(w_ref, r_ref, m_ref, v_ref, *rest):
        g_ref, d_ref, nm_ref, nv_ref = rest[-4:]
        me = _position()[3]

        def slot(d):
            return jnp.where(me == d, rest[0][...], r_ref[d]) if has_own else r_ref[d]

        gv = slot(0).astype(F32)
        for d in range(1, NDEV):
            gv = gv + slot(d).astype(F32)
        nm = B1 * m_ref[...] + (1.0 - B1) * gv
        nv = B2 * v_ref[...] + (1.0 - B2) * (gv * gv)
        g_ref[...] = gv
        d_ref[...] = -LR * ((nm / c1) / (jnp.sqrt(nv / c2) + AEPS) + WD * w_ref[...])
        nm_ref[...] = nm
        nv_ref[...] = nv

    blk = pl.BlockSpec((rows, bc), lambda j: (0, j))
    return _call(body, name=name, out_shape=(_sds((rows, cols)),) * 4, grid=(cols // bc,),
                 in_specs=[blk, pl.BlockSpec((NDEV, rows, bc), lambda j: (0, 0, j)), blk, blk] + [blk] * has_own,
                 out_specs=(blk,) * 4, sem=("parallel",), vmem=VMEM_BIG)(w, recv, m, v, *([own] if has_own else []))


P_LAT, P_CTX, P_FNW, P_FFNB, P_CONV, P_FFNW, P_MISC, P_ROWS = 0, 8, 16, 24, 32, 48, 72, 80


def _rows_of(v, nrows):
    flat = v.reshape(-1)
    return jnp.pad(flat, (0, nrows * D - flat.shape[0])).reshape(nrows, D)


def _by_columns(g):
    n, r, c = g.shape
    return jnp.transpose(g, (1, 0, 2)).reshape(r, n * c)


def kernel(x, c, ctx, c_ctx, w_mod, b_mod, w_in, q_norm_w, k_norm_w, conv_qkv_w, a_log, dt_bias, gdn_norm_w, w_pa, w_pd, w_out, w_up, ffn_conv_w, ffn_conv_b, w_down, final_norm_w, loss_target, m_c_ctx, m_w_mod, m_b_mod, m_w_in, m_q_norm_w, m_k_norm_w, m_conv_qkv_w, m_a_log, m_dt_bias, m_gdn_norm_w, m_w_pa, m_w_pd, m_w_out, m_w_up, m_ffn_conv_w, m_ffn_conv_b, m_w_down, m_final_norm_w, v_c_ctx, v_w_mod, v_b_mod, v_w_in, v_q_norm_w, v_k_norm_w, v_conv_qkv_w, v_a_log, v_dt_bias, v_gdn_norm_w, v_w_pa, v_w_pd, v_w_out, v_w_up, v_ffn_conv_w, v_ffn_conv_b, v_w_down, v_final_norm_w):
    _, _, _, me = _position()
    mcols = w_mod.shape[2]

    transposed = ("w_in", "w_up")
    big = {"w_in": w_in[0].T, "w_pa": w_pa[0], "w_pd": w_pd[0], "w_out": w_out[0], "w_up": w_up[0].T, "w_down": w_down[0]}
    names = list(big)
    shards = {n: _cast_bf16(big[n], name="cast_" + n) for n in names}
    w_in_g = _gather_two_level(shards["w_in"], name="gather_w_in")
    c_all, conv_g, ffnw_g = _exchange([c, conv_qkv_w[0], ffn_conv_w[0]], name="gather_small", scatter=False)
    w_in_full = w_in_g.reshape(W_END, D)
    w_in_pad = _pad_columns(w_in_full)

    c9 = jnp.concatenate([c_all.reshape(NDEV, D), jnp.pad(c_ctx[None], ((0, MODROWS - NDEV - 1), (0, 0)))], axis=0)
    b_loc = lax.dynamic_slice(b_mod, (0, me * mcols), (1, mcols))
    mod_all, = _exchange([_mod_fwd(c9, w_mod[0], b_loc)], name="gather_mod", scatter=False)
    mod_lat = lax.dynamic_index_in_dim(mod_all, me, axis=1, keepdims=False).reshape(6, D)
    mod_ctx = mod_all[:, NDEV, :].reshape(6, D)

    small = {"q_norm_w": q_norm_w, "k_norm_w": k_norm_w, "gdn_norm_w": gdn_norm_w, "a_log": a_log, "dt_bias": dt_bias,
             "conv_qkv_w": _by_columns(conv_g), "ffn_conv_w": _by_columns(ffnw_g), "ffn_conv_b": ffn_conv_b,
             "final_norm_w": final_norm_w[None]}
    loss_me, grad_x, (pending_in, own_in), recv, dmod_lat, dmod_ctx, gs = _local_step(
        x[0], ctx[0], loss_target[0], mod_lat, mod_ctx, w_in_pad, shards, small)

    moments = {"w_in": (m_w_in, v_w_in), "w_pa": (m_w_pa, v_w_pa), "w_pd": (m_w_pd, v_w_pd),
               "w_out": (m_w_out, v_w_out), "w_up": (m_w_up, v_w_up), "w_down": (m_w_down, v_w_down)}
    res = {}
    def finish(n, outs):
        return tuple((t.T if n in transposed else t)[None] for t in outs)

    def moment(t, n):
        return t[0].T if n in transposed else t[0]

    for n in recv:
        res[n] = finish(n, _adamw_recv(big[n], recv[n], moment(moments[n][0], n), moment(moments[n][1], n),
                                       name="adamw_" + n))

    misc = jnp.concatenate([gs["q_norm_w"][0], gs["k_norm_w"][0], gs["gdn_norm_w"][0], gs["a_log"], gs["dt_bias"],
                            loss_me[None]])
    pack = jnp.concatenate([_rows_of(dmod_lat, P_CTX - P_LAT), _rows_of(dmod_ctx, P_FNW - P_CTX),
                            _rows_of(gs["final_norm_w"], P_FFNB - P_FNW), _rows_of(gs["ffn_conv_b"], P_CONV - P_FFNB),
                            _rows_of(gs["conv_qkv_w"], P_FFNW - P_CONV), _rows_of(gs["ffn_conv_w"], P_MISC - P_FFNW),
                            _rows_of(misc, P_ROWS - P_MISC)], axis=0)
    pack_all, = _exchange([pack], name="gather_pack", scatter=False)
    tot = _sum_slots(pack_all, name="sum_pack")
    dall = jnp.concatenate([pack_all[:, P_LAT:P_LAT + 6, :].reshape(NDEV, 6 * D),
                            jnp.pad(tot[P_CTX:P_CTX + 6].reshape(1, 6 * D), ((0, MODROWS - NDEV - 1), (0, 0)))], axis=0)
    dmy = lax.dynamic_slice(dall, (0, me * mcols), (MODROWS, mcols))
    g_w_mod, g_b_mod, cpart = _mod_bwd(c9, dmy, dall, w_mod[0])
    cparts, = _exchange([cpart], name="gather_cctx", scatter=False)
    sems_a, land = pending_in[:2], pending_in[3]
    *sems_b, g_in_thru, land, token_b = _scatter_start(pending_in[2], land, (D // 2, D // 2), (cparts,),
                                                       name="scatter_g_in_b_start")
    g_c_ctx = _cctx_finish(cparts, c_ctx[None], (token_b,))[0]

    nconv, nffn = 3 * GH * HD, 2 * DFF
    conv_tot = tot[P_CONV:P_FFNW].reshape(-1)[:3 * nconv].reshape(3, nconv)
    ffnw_tot = tot[P_FFNW:P_MISC].reshape(-1)[:3 * nffn].reshape(3, nffn)
    mrow = tot[P_MISC]
    grads = {
        "c_ctx": g_c_ctx, "w_mod": g_w_mod[None], "b_mod": g_b_mod,
        "q_norm_w": mrow[None, 0:HD], "k_norm_w": mrow[None, HD:2 * HD], "gdn_norm_w": mrow[None, 2 * HD:3 * HD],
        "conv_qkv_w": lax.dynamic_slice(conv_tot, (0, me * (nconv // NDEV)), (3, nconv // NDEV))[None],
        "a_log": mrow[3 * HD:3 * HD + 2 * GH].reshape(1, 2, GH),
        "dt_bias": mrow[3 * HD + 2 * GH:3 * HD + 4 * GH].reshape(1, 2, GH),
        "ffn_conv_w": lax.dynamic_slice(ffnw_tot, (0, me * (nffn // NDEV)), (3, nffn // NDEV))[None],
        "ffn_conv_b": tot[P_FFNB:P_CONV].reshape(-1)[:nffn][None],
        "final_norm_w": tot[P_FNW],
    }
    loss = mrow[3 * HD + 4 * GH]
    given = {"c_ctx": (c_ctx, m_c_ctx, v_c_ctx), "w_mod": (w_mod, m_w_mod, v_w_mod), "b_mod": (b_mod, m_b_mod, v_b_mod),
             "q_norm_w": (q_norm_w, m_q_norm_w, v_q_norm_w), "k_norm_w": (k_norm_w, m_k_norm_w, v_k_norm_w),
             "conv_qkv_w": (conv_qkv_w, m_conv_qkv_w, v_conv_qkv_w), "a_log": (a_log, m_a_log, v_a_log),
             "dt_bias": (dt_bias, m_dt_bias, v_dt_bias), "gdn_norm_w": (gdn_norm_w, m_gdn_norm_w, v_gdn_norm_w),
             "ffn_conv_w": (ffn_conv_w, m_ffn_conv_w, v_ffn_conv_w), "ffn_conv_b": (ffn_conv_b, m_ffn_conv_b, v_ffn_conv_b),
             "final_norm_w": (final_norm_w, m_final_norm_w, v_final_norm_w)}
    res["w_mod"] = (grads["w_mod"],) + _adamw(w_mod, grads["w_mod"], m_w_mod, v_w_mod, name="adamw_w_mod")
    small_names = [n for n in given if n != "w_mod"]
    updates = _adamw_many([(given[n][0], grads[n], given[n][1], given[n][2]) for n in small_names], name="adamw_small")
    for n, upd in zip(small_names, updates):
        res[n] = (grads[n],) + upd

    g_in_thru, land = _scatter_wait(*sems_a, g_in_thru, land, (0, D // 2), [res[n][1] for n in res],
                                    name="scatter_g_in_a_wait")
    _, land = _scatter_wait(*sems_b, g_in_thru, land, (D // 2, D // 2), (), name="scatter_g_in_b_wait")
    res["w_in"] = finish("w_in", _adamw_recv(big["w_in"], land, moment(m_w_in, "w_in"), moment(v_w_in, "w_in"),
                                             name="adamw_w_in", own=own_in))

    order = ["c_ctx", "w_mod", "b_mod", "w_in", "q_norm_w", "k_norm_w", "conv_qkv_w", "a_log", "dt_bias", "gdn_norm_w",
             "w_pa", "w_pd", "w_out", "w_up", "ffn_conv_w", "ffn_conv_b", "w_down", "final_norm_w"]
    return (loss, grad_x[None], *[res[n][0] for n in order], *[res[n][1] for n in order],
            *[res[n][2] for n in order], *[res[n][3] for n in order])
```

```python
import functools
import math

import jax
import jax.numpy as jnp
from jax import lax
from jax.experimental import pallas as pl
from jax.experimental.pallas import tpu as pltpu

F32 = jnp.float32
BF16 = jnp.bfloat16
HI = lax.Precision.HIGHEST
MESH = pl.DeviceIdType.MESH

NDEV = 8
D = 1024
HD = 128
AH, AKV, GRP = 8, 2, 4
GH = 8
CH = 64
DFF = 2816
GRID_W = 64
EPS = 1e-6
ROPE_THETA = 10000.0
LOG2E = math.log2(math.e)
C_KV, C_AQ, C_QKV, C_BL, C_Z, C_GATE, C_END = 0, 512, 1536, 4608, 5120, 6144, 8192
W_QKV, W_AQ, W_Z, W_END = 512, 3616, 4640, 7712


def _pad_columns(w):
    zeros = jnp.zeros((C_Z - C_QKV - (W_AQ - W_QKV), D), w.dtype)
    return jnp.concatenate([w[:W_QKV], w[W_AQ:W_Z], w[W_QKV:W_AQ], zeros, w[W_Z:]], axis=0)


def _unpad_columns(g):
    return jnp.concatenate([g[:C_AQ], g[C_QKV:C_QKV + W_AQ - W_QKV], g[C_AQ:C_QKV], g[C_Z:]], axis=0)
LR, B1, B2, AEPS, WD, STEP = 0.001, 0.9, 0.999, 1e-08, 0.01, 10
VMEM_BIG = 56 * 1024 * 1024
INTRA_FWD_CHUNKS = 36
INTRA_BWD_CHUNKS = 36


def _call(body, *, name, out_shape, grid=None, in_specs=None, out_specs=None, scratch=(), sem=None,
          vmem=None, aliases=None):
    params = {}
    if sem is not None:
        params["dimension_semantics"] = sem
    if vmem is not None:
        params["vmem_limit_bytes"] = vmem
    kw = {}
    if grid is not None:
        kw["grid"] = grid
    if in_specs is not None:
        kw["in_specs"] = in_specs
    if out_specs is not None:
        kw["out_specs"] = out_specs
    if aliases:
        kw["input_output_aliases"] = aliases
    return pl.pallas_call(body, name=name, out_shape=out_shape, scratch_shapes=list(scratch),
                          compiler_params=pltpu.CompilerParams(**params), **kw)


def _call_carrying(body, exch, *, name, out_shape, grid, in_specs, out_specs, scratch=(), vmem=None):
    n, nin, nout, nscr = exch.n, len(in_specs), len(out_shape), len(scratch)
    steps = math.prod(grid)
    mid = (2 * steps) // 3

    def wrapped(*refs):
        ins, cins = refs[:nin], refs[nin:nin + n]
        outs, couts = refs[nin + n:nin + n + nout], refs[nin + n + nout:nin + 2 * n + nout]
        scr, sems = refs[nin + 2 * n + nout:nin + 2 * n + nout + nscr], refs[nin + 2 * n + nout + nscr:]
        ids = [pl.program_id(i) for i in range(len(grid))]
        first = functools.reduce(jnp.logical_and, [i == 0 for i in ids])
        last = functools.reduce(jnp.logical_and, [i == g - 1 for i, g in zip(ids, grid)])

        @pl.when(first)
        def _():
            exch.start(cins, couts, sems)

        if hasattr(exch, "middle"):
            linear = functools.reduce(lambda acc, ig: acc * ig[1] + ig[0], zip(ids, grid), 0)

            @pl.when(linear == mid)
            def _():
                exch.middle(cins, couts, sems)

        body(*ins, *outs, *scr)

        @pl.when(last)
        def _():
            exch.finish(cins, couts, sems)

    params = {"dimension_semantics": ("arbitrary",) * len(grid)}
    if vmem is not None:
        params["vmem_limit_bytes"] = vmem
    fn = pl.pallas_call(wrapped, name=name, out_shape=tuple(out_shape) + exch.out_shape, grid=grid,
                        in_specs=list(in_specs) + [HBM] * n, out_specs=tuple(out_specs) + (HBM,) * n,
                        scratch_shapes=list(scratch) + exch.scratch, compiler_params=pltpu.CompilerParams(**params))

    def run(*args):
        res = fn(*args, *exch.arrs)
        return res[:nout], list(res[nout:])

    return run


def _sds(shape, dtype=F32):
    return jax.ShapeDtypeStruct(tuple(shape), dtype)


def _dot(a, b, ca, cb):
    return lax.dot_general(a.astype(BF16), b.astype(BF16), (((ca,), (cb,)), ((), ())),
                           preferred_element_type=F32)


@jax.custom_vjp
def _nn(a, b):
    return _dot(a, b, 1, 0)


@jax.custom_vjp
def _nt(a, b):
    return _dot(a, b, 1, 1)


@jax.custom_vjp
def _tn(a, b):
    return _dot(a, b, 0, 0)


_nn.defvjp(lambda a, b: (_nn(a, b), (a, b)), lambda r, g: (_nt(g, r[1]), _tn(r[0], g)))
_nt.defvjp(lambda a, b: (_nt(a, b), (a, b)), lambda r, g: (_nn(g, r[1]), _tn(g, r[0])))
_tn.defvjp(lambda a, b: (_tn(a, b), (a, b)), lambda r, g: (_nt(r[1], g), _nn(r[0], g)))


def _hdot(a, b):
    return jnp.dot(a, b, precision=HI, preferred_element_type=F32)


def _mdot(a, b):
    return jnp.dot(a, b, precision=lax.Precision.HIGH, preferred_element_type=F32)


def _maskdot(mask, a, cm):
    hi = a.astype(BF16)
    r = a - hi.astype(F32)
    mid = r.astype(BF16)
    lo = (r - mid.astype(F32)).astype(BF16)
    mb = mask.astype(BF16)
    dims = (((cm,), (0,)), ((), ()))
    return (lax.dot_general(mb, hi, dims, preferred_element_type=F32)
            + lax.dot_general(mb, mid, dims, preferred_element_type=F32)
            + lax.dot_general(mb, lo, dims, preferred_element_type=F32))


@jax.custom_vjp
def _mask_nn(mask, a):
    return _maskdot(mask, a, 1)


_mask_nn.defvjp(lambda mask, a: (_maskdot(mask, a, 1), mask),
                lambda mask, g: (jnp.zeros_like(mask), _maskdot(mask, g, 0)))


@jax.custom_vjp
def _saved_inverse(lmat, x):
    return x


def _saved_inverse_bwd(x, g):
    t = lax.dot_general(x, g, (((0,), (0,)), ((), ())), precision=lax.Precision.HIGH, preferred_element_type=F32)
    dl = lax.dot_general(t, x, (((1,), (1,)), ((), ())), precision=lax.Precision.HIGH, preferred_element_type=F32)
    return -dl, jnp.zeros_like(x)


_saved_inverse.defvjp(lambda lmat, x: (x, x), _saved_inverse_bwd)


def _row_ids(shape):
    return lax.broadcasted_iota(jnp.int32, shape, 0)


def _shift_rows(x, down, bounds):
    n = x.shape[0]
    rows = _row_ids(x.shape)
    y = pltpu.roll(x, 1 if down else n - 1, 0)
    edge = functools.reduce(jnp.logical_or, [rows == (s if down else e - 1) for s, e in bounds])
    return jnp.where(edge, 0.0, y)


def _make_shift(bounds):
    @jax.custom_vjp
    def down(x):
        return _shift_rows(x, True, bounds)

    @jax.custom_vjp
    def up(x):
        return _shift_rows(x, False, bounds)

    down.defvjp(lambda x: (down(x), None), lambda _, g: (up(g),))
    up.defvjp(lambda x: (up(x), None), lambda _, g: (down(g),))
    return down, up


@jax.custom_vjp
def _swap32(x):
    lane = lax.broadcasted_iota(jnp.int32, x.shape, x.ndim - 1)
    return jnp.where((lane % 64) < 32, pltpu.roll(x, HD - 32, x.ndim - 1), pltpu.roll(x, 32, x.ndim - 1))


_swap32.defvjp(lambda x: (_swap32(x), None), lambda _, g: (_swap32(g),))


def _rms(x):
    return x * lax.rsqrt(jnp.mean(x * x, axis=-1, keepdims=True) + EPS)


def _silu(x):
    return x * jax.nn.sigmoid(x)


def _mm(a, b, *, name, M, N, K, ta=False, tb=False, out_dtype=F32, bm=None, bn=None, bk=None,
        a_off=(0, 0), b_off=(0, 0), after=()):
    bm, bn, bk = bm or M, bn or N, bk or K
    assert M % bm == 0 and N % bn == 0 and K % bk == 0, (name, M, N, K, bm, bn, bk)
    nk = K // bk
    ca, cb = (0 if ta else 1), (1 if tb else 0)
    na = len(after)

    def body(a_ref, b_ref, *rest):
        o_ref, acc = rest[na], rest[na + 1:]
        r = _dot(a_ref[...], b_ref[...], ca, cb)
        if nk == 1:
            o_ref[...] = r.astype(out_dtype)
        else:
            acc_ref, = acc
            k = pl.program_id(2)

            @pl.when(k == 0)
            def _():
                acc_ref[...] = r

            @pl.when(k > 0)
            def _():
                acc_ref[...] += r

            @pl.when(k == nk - 1)
            def _():
                o_ref[...] = acc_ref[...].astype(out_dtype)

    def blk(off, bshape):
        assert off[0] % bshape[0] == 0 and off[1] % bshape[1] == 0, (name, off, bshape)
        return off[0] // bshape[0], off[1] // bshape[1]

    if ta:
        ao = blk(a_off, (bk, bm))
        a_spec = pl.BlockSpec((bk, bm), lambda i, j, k: (k + ao[0], i + ao[1]))
    else:
        ao = blk(a_off, (bm, bk))
        a_spec = pl.BlockSpec((bm, bk), lambda i, j, k: (i + ao[0], k + ao[1]))
    if tb:
        bo = blk(b_off, (bn, bk))
        b_spec = pl.BlockSpec((bn, bk), lambda i, j, k: (j + bo[0], k + bo[1]))
    else:
        bo = blk(b_off, (bk, bn))
        b_spec = pl.BlockSpec((bk, bn), lambda i, j, k: (k + bo[0], j + bo[1]))
    return _call(body, name=name, out_shape=_sds((M, N), out_dtype), grid=(M // bm, N // bn, nk),
                 in_specs=[a_spec, b_spec] + [pl.BlockSpec(memory_space=pl.ANY)] * na,
                 out_specs=pl.BlockSpec((bm, bn), lambda i, j, k: (i, j)),
                 scratch=[pltpu.VMEM((bm, bn), F32)] if nk > 1 else [],
                 sem=("parallel", "parallel", "arbitrary"), vmem=VMEM_BIG)(a, b, *after)


def _normmod_fn(x, sh, sc):
    return _rms(x) * (1.0 + sc) + sh


def _normmod_fwd(x, mod, i_sh, i_sc, *, name, br=256):
    R = x.shape[0]

    def body(x_ref, mod_ref, o_ref):
        o_ref[...] = _normmod_fn(x_ref[...], mod_ref[i_sh:i_sh + 1, :], mod_ref[i_sc:i_sc + 1, :]).astype(BF16)

    return _call(body, name=name, out_shape=_sds((R, D), BF16), grid=(R // br,),
                 in_specs=[pl.BlockSpec((br, D), lambda i: (i, 0)), pl.BlockSpec((6, D), lambda i: (0, 0))],
                 out_specs=pl.BlockSpec((br, D), lambda i: (i, 0)), sem=("parallel",))(x, mod)


def _normmod_bwd(x, mod, i_sh, i_sc, dh, dh_off, res, *, name, br=256):
    R = x.shape[0]
    ob = dh_off // br
    has_res = res is not None

    def body(x_ref, mod_ref, dh_ref, *rest):
        if has_res:
            res_ref, dx_ref, dsh_ref, dsc_ref = rest
        else:
            dx_ref, dsh_ref, dsc_ref = rest
        sh, sc = mod_ref[i_sh:i_sh + 1, :], mod_ref[i_sc:i_sc + 1, :]
        _, vjp = jax.vjp(_normmod_fn, x_ref[...], sh, sc)
        dx, dsh, dsc = vjp(dh_ref[...])
        dx_ref[...] = dx + res_ref[...] if has_res else dx

        @pl.when(pl.program_id(0) == 0)
        def _():
            dsh_ref[...] = jnp.zeros_like(dsh_ref)
            dsc_ref[...] = jnp.zeros_like(dsc_ref)

        dsh_ref[...] += dsh
        dsc_ref[...] += dsc

    row = pl.BlockSpec((br, D), lambda i: (i, 0))
    vec = pl.BlockSpec((1, D), lambda i: (0, 0))
    ins = [row, pl.BlockSpec((6, D), lambda i: (0, 0)), pl.BlockSpec((br, D), lambda i: (i + ob, 0))]
    args = [x, mod, dh]
    if has_res:
        ins.append(row)
        args.append(res)
    return _call(body, name=name, out_shape=(_sds((R, D)), _sds((1, D)), _sds((1, D))), grid=(R // br,),
                 in_specs=ins, out_specs=(row, vec, vec), sem=("arbitrary",))(*args)


def _rope(x, cos, sin):
    return x * cos + _swap32(x) * sin


def _aprep_fn(qs, ks, cos, sin, qw, kw):
    return ([_rope(_rms(q) * qw, cos, sin) for q in qs], [_rope(_rms(k) * kw, cos, sin) for k in ks])


def _aprep_fwd(proj, cos, sin, qw, kw, *, br=256):
    T = proj.shape[0]

    def body(x_ref, cos_ref, sin_ref, qw_ref, kw_ref, q_ref, k_ref, v_ref):
        qs = [x_ref[:, C_AQ + h * HD:C_AQ + (h + 1) * HD] for h in range(AH)]
        ks = [x_ref[:, h * HD:(h + 1) * HD] for h in range(AKV)]
        qo, ko = _aprep_fn(qs, ks, cos_ref[...], sin_ref[...], qw_ref[...], kw_ref[...])
        for h in range(AH):
            q_ref[h] = qo[h].astype(BF16)
        for h in range(AKV):
            k_ref[h] = ko[h].astype(BF16)
            v_ref[h] = x_ref[:, (AKV + h) * HD:(AKV + h + 1) * HD].astype(BF16)

    tab = pl.BlockSpec((br, HD), lambda i: (i, 0))
    vec = pl.BlockSpec((1, HD), lambda i: (0, 0))
    return _call(body, name="aprep_fwd",
                 out_shape=(_sds((AH, T, HD), BF16), _sds((AKV, T, HD), BF16), _sds((AKV, T, HD), BF16)),
                 grid=(T // br,),
                 in_specs=[pl.BlockSpec((br, C_QKV), lambda i: (i, 0)), tab, tab, vec, vec],
                 out_specs=(pl.BlockSpec((AH, br, HD), lambda i: (0, i, 0)),
                            pl.BlockSpec((AKV, br, HD), lambda i: (0, i, 0)),
                            pl.BlockSpec((AKV, br, HD), lambda i: (0, i, 0))),
                 sem=("parallel",))(proj, cos, sin, qw, kw)


def _aprep_bwd(proj, cos, sin, qw, kw, dq, dk, dv, dproj, L, *, br=256):
    T = proj.shape[0]
    lb = L // br

    def body(x_ref, cos_ref, sin_ref, qw_ref, kw_ref, dq_ref, dk_ref, dv_ref, _, dx_ref, dqw_ref, dkw_ref):
        i = pl.program_id(0)
        qs = [x_ref[:, C_AQ + h * HD:C_AQ + (h + 1) * HD] for h in range(AH)]
        ks = [x_ref[:, h * HD:(h + 1) * HD] for h in range(AKV)]
        _, vjp = jax.vjp(_aprep_fn, qs, ks, cos_ref[...], sin_ref[...], qw_ref[...], kw_ref[...])
        is_lat = i >= lb
        dqs = [jnp.where(is_lat, dq_ref[h], 0.0) for h in range(AH)]
        dks = [dk_ref[h] for h in range(AKV)]
        gq, gk, _, _, gqw, gkw = vjp((dqs, dks))
        for h in range(AH):
            dx_ref[:, C_AQ + h * HD:C_AQ + (h + 1) * HD] = gq[h].astype(BF16)
        for h in range(AKV):
            dx_ref[:, h * HD:(h + 1) * HD] = gk[h].astype(BF16)
            dx_ref[:, (AKV + h) * HD:(AKV + h + 1) * HD] = dv_ref[h].astype(BF16)

        @pl.when(i == 0)
        def _():
            dqw_ref[...] = jnp.zeros_like(dqw_ref)
            dkw_ref[...] = jnp.zeros_like(dkw_ref)

        dqw_ref[...] += gqw
        dkw_ref[...] += gkw

    tab = pl.BlockSpec((br, HD), lambda i: (i, 0))
    vec = pl.BlockSpec((1, HD), lambda i: (0, 0))
    kvb = pl.BlockSpec((AKV, br, HD), lambda i: (0, i, 0))
    blk = pl.BlockSpec((br, C_QKV), lambda i: (i, 0))
    return _call(body, name="aprep_bwd", out_shape=(_sds(dproj.shape, BF16), _sds((1, HD)), _sds((1, HD))),
                 grid=(T // br,),
                 in_specs=[blk, tab, tab, vec, vec,
                           pl.BlockSpec((AH, br, HD), lambda i: (0, jnp.maximum(i - lb, 0), 0)), kvb, kvb, ANYSPEC],
                 out_specs=(blk, vec, vec), aliases={8: 0},
                 sem=("arbitrary",))(proj, cos, sin, qw, kw, dq, dk, dv, dproj)


def _attn_grad(q, k, v, o, lse2, do):
    scale = HD ** -0.5
    p = jnp.exp2(_dot(q, k, 1, 1) * (scale * LOG2E) - lse2)
    dp = _dot(do, v, 1, 1)
    ds = p * (dp - jnp.sum(do * o, axis=-1, keepdims=True)) * scale
    return _dot(ds, k, 1, 0), _dot(ds, q, 0, 0), _dot(p, do, 0, 0)


ATTN_KEYS = 256


def _attn_fwd(q, k, v, L, exch, *, bq=128):
    T = q.shape[1]
    N = T - L
    lb = L // bq
    assert T % ATTN_KEYS == 0
    scale = HD ** -0.5
    heads = range(GRP)

    def body(q_ref, k_ref, v_ref, o_ref, o32_ref, lse_ref):
        qs = [q_ref[g] for g in heads]
        m = [jnp.full((bq, 1), -jnp.inf, F32) for _ in heads]
        l = [jnp.zeros((bq, 1), F32) for _ in heads]
        acc = [jnp.zeros((bq, HD), F32) for _ in heads]
        for c in range(T // ATTN_KEYS):
            kc, vc = k_ref[c * ATTN_KEYS:(c + 1) * ATTN_KEYS, :], v_ref[c * ATTN_KEYS:(c + 1) * ATTN_KEYS, :]
            s = [_dot(qs[g], kc, 1, 1) * (scale * LOG2E) for g in heads]
            m_new = [jnp.maximum(m[g], jnp.max(s[g], axis=-1, keepdims=True)) for g in heads]
            alpha = [jnp.exp2(m[g] - m_new[g]) for g in heads]
            p = [jnp.exp2(s[g] - m_new[g]) for g in heads]
            l = [l[g] * alpha[g] + jnp.sum(p[g], axis=-1, keepdims=True) for g in heads]
            acc = [acc[g] * alpha[g] + _dot(p[g], vc, 1, 0) for g in heads]
            m = m_new
        for g in heads:
            o = acc[g] / l[g]
            o_ref[:, g * HD:(g + 1) * HD] = o.astype(BF16)
            o32_ref[:, g * HD:(g + 1) * HD] = o
            lse_ref[g] = jnp.broadcast_to(m[g] + jnp.log2(l[g]), (bq, HD))

    kvb = pl.BlockSpec((None, T, HD), lambda g, i: (g, 0, 0))
    ob = pl.BlockSpec((bq, GRP * HD), lambda g, i: (i, g))
    return _call_carrying(
        body, exch, name="attn_fwd",
        out_shape=(_sds((N, AH * HD), BF16), _sds((N, AH * HD)), _sds((AH, N, HD))), grid=(AKV, N // bq),
        in_specs=[pl.BlockSpec((GRP, bq, HD), lambda g, i: (g, i + lb, 0)), kvb, kvb],
        out_specs=(ob, ob, pl.BlockSpec((GRP, bq, HD), lambda g, i: (g, i, 0))), vmem=VMEM_BIG)(q, k, v)


def _attn_bwd(q, k, v, o32, lse, do, L, *, bq=128):
    T = q.shape[1]
    N = T - L
    lb = L // bq

    def body(q_ref, k_ref, v_ref, o_ref, lse_ref, do_ref, dq_ref, dk_ref, dv_ref):
        rows = lambda r: jnp.concatenate([r[:, g * HD:(g + 1) * HD] for g in range(GRP)], axis=0)
        lse = jnp.max(lse_ref[...].reshape(GRP * bq, HD), axis=-1, keepdims=True)
        dq, dk, dv = _attn_grad(q_ref[...].reshape(GRP * bq, HD), k_ref[...], v_ref[...], rows(o_ref), lse, rows(do_ref))
        dq_ref[...] = dq.reshape(GRP, bq, HD)

        @pl.when(pl.program_id(1) == 0)
        def _():
            dk_ref[...] = jnp.zeros_like(dk_ref)
            dv_ref[...] = jnp.zeros_like(dv_ref)

        dk_ref[...] += dk
        dv_ref[...] += dv

    kvb = pl.BlockSpec((None, T, HD), lambda g, i: (g, 0, 0))
    qb = pl.BlockSpec((GRP, bq, HD), lambda g, i: (g, i + lb, 0))
    hb = pl.BlockSpec((GRP, bq, HD), lambda g, i: (g, i, 0))
    ob = pl.BlockSpec((bq, GRP * HD), lambda g, i: (i, g))
    return _call(body, name="attn_bwd",
                 out_shape=(_sds((AH, N, HD)), _sds((AKV, T, HD)), _sds((AKV, T, HD))), grid=(AKV, N // bq),
                 in_specs=[qb, kvb, kvb, ob, hb, ob], out_specs=(hb, kvb, kvb),
                 sem=("parallel", "arbitrary"), vmem=VMEM_BIG)(q, k, v, o32, lse, do)


def _gprep_fn(kind, shifts, x, w):
    down, up = shifts
    y = down(x) * w[0:1, :] + x * w[1:2, :] + up(x) * w[2:3, :]
    a = _silu(y)
    if kind == 2:
        return a
    a = a * lax.rsqrt(jnp.sum(a * a, axis=-1, keepdims=True) + EPS)
    return a * (HD ** -0.5) if kind == 0 else a


def _gprep_fwd(proj, conv_w, kind, bounds):
    T = proj.shape[0]
    shifts = _make_shift(bounds)
    cb = C_QKV // HD + kind * GH

    def body(x_ref, w_ref, o_ref):
        o_ref[...] = _gprep_fn(kind, shifts, x_ref[...], w_ref[...])

    return _call(body, name=f"gprep_fwd{kind}", out_shape=_sds((GH, T, HD)), grid=(GH,),
                 in_specs=[pl.BlockSpec((T, HD), lambda h: (0, cb + h)),
                           pl.BlockSpec((3, HD), lambda h: (0, kind * GH + h))],
                 out_specs=pl.BlockSpec((None, T, HD), lambda h: (h, 0, 0)), sem=("parallel",))(proj, conv_w)


def _gprep_bwd(proj, conv_w, kind, bounds, dy, dproj):
    T = proj.shape[0]
    shifts = _make_shift(bounds)
    cb = C_QKV // HD + kind * GH

    def body(x_ref, w_ref, dy_ref, _, dx_ref, dw_ref):
        _, vjp = jax.vjp(functools.partial(_gprep_fn, kind, shifts), x_ref[...], w_ref[...])
        dx, dw = vjp(dy_ref[0] + dy_ref[1])
        dx_ref[...] = dx.astype(BF16)
        dw_ref[...] = dw

    return _call(body, name=f"gprep_bwd{kind}", out_shape=(_sds(dproj.shape, BF16), _sds((3, GH * HD))), grid=(GH,),
                 in_specs=[pl.BlockSpec((T, HD), lambda h: (0, cb + h)),
                           pl.BlockSpec((3, HD), lambda h: (0, kind * GH + h)),
                           pl.BlockSpec((2, None, T, HD), lambda h: (0, h, 0, 0)), ANYSPEC],
                 out_specs=(pl.BlockSpec((T, HD), lambda h: (0, cb + h)), pl.BlockSpec((3, HD), lambda h: (0, h))),
                 aliases={3: 0}, sem=("parallel",))(proj, conv_w, dy, dproj)


def _bl_fn(x, alog, dtb):
    lane = lax.broadcasted_iota(jnp.int32, x.shape, 1)
    beta = jax.nn.sigmoid(x)
    z = x + dtb
    sp = jnp.maximum(z, 0.0) + jnp.log1p(jnp.exp(-jnp.abs(z)))
    la = -jnp.exp(alog) * sp
    return jnp.where(lane < 2 * GH, beta, jnp.where(lane < 4 * GH, la, 0.0))


def _bl_fwd(proj, alog, dtb, *, br=256):
    T = proj.shape[0]

    def body(x_ref, a_ref, d_ref, o_ref):
        o_ref[...] = _bl_fn(x_ref[...], a_ref[...], d_ref[...])

    vec = pl.BlockSpec((1, HD), lambda i: (0, 0))
    return _call(body, name="bl_fwd", out_shape=_sds((T, HD)), grid=(T // br,),
                 in_specs=[pl.BlockSpec((br, HD), lambda i: (i, C_BL // HD)), vec, vec],
                 out_specs=pl.BlockSpec((br, HD), lambda i: (i, 0)), sem=("parallel",))(proj, alog, dtb)


def _bl_bwd(proj, alog, dtb, dbl, dproj, *, br=256):
    T = proj.shape[0]
    wide = C_Z - C_BL

    def body(x_ref, a_ref, d_ref, g_ref, _, dx_ref, da_ref, dd_ref):
        g = g_ref[0, 0]
        for d in range(2):
            for h in range(GH):
                if d or h:
                    g = g + g_ref[d, h]
        _, vjp = jax.vjp(_bl_fn, x_ref[...], a_ref[...], d_ref[...])
        dx, da, dd = vjp(g)
        dx_ref[:, :HD] = dx.astype(BF16)
        dx_ref[:, HD:] = jnp.zeros((br, wide - HD), BF16)

        @pl.when(pl.program_id(0) == 0)
        def _():
            da_ref[...] = jnp.zeros_like(da_ref)
            dd_ref[...] = jnp.zeros_like(dd_ref)

        da_ref[...] += da
        dd_ref[...] += dd

    vec = pl.BlockSpec((1, HD), lambda i: (0, 0))
    return _call(body, name="bl_bwd", out_shape=(_sds(dproj.shape, BF16), _sds((1, HD)), _sds((1, HD))), grid=(T // br,),
                 in_specs=[pl.BlockSpec((br, HD), lambda i: (i, C_BL // HD)), vec, vec,
                           pl.BlockSpec((2, GH, br, HD), lambda i: (0, 0, i, 0)), ANYSPEC],
                 out_specs=(pl.BlockSpec((br, wide), lambda i: (i, C_BL // wide)), vec, vec), aliases={4: 0},
                 sem=("arbitrary",))(proj, alog, dtb, dbl, dproj)


def _chunk_masks(d):
    ii = lax.broadcasted_iota(jnp.int32, (CH, CH), 0)
    jj = lax.broadcasted_iota(jnp.int32, (CH, CH), 1)
    eye = (ii == jj).astype(F32)
    before = jnp.where(d == 0, (jj < ii).astype(F32), (jj > ii).astype(F32))
    return before, before + eye, eye


def _intra_fn(masks, sel_b, sel_l, qs, ks, vs, bls, xs=None):
    before, ateq, eye = masks
    ones = jnp.ones((CH, CH), F32)
    inc = ateq > 0.0
    each = lambda f, *ls: [f(*t) for t in zip(*ls)]
    beta = each(lambda bl: jnp.sum(bl * sel_b, axis=-1, keepdims=True), bls)
    la = each(lambda bl: jnp.sum(bl * sel_l, axis=-1, keepdims=True), bls)
    gam = each(lambda a: _mask_nn(ateq, jnp.broadcast_to(a, (CH, HD))), la)
    gi = each(lambda a: _mask_nn(ateq, jnp.broadcast_to(a, (CH, CH))), la)
    gj = each(lambda g: _mask_nn(ones, eye * g), gi)
    kk = each(lambda k: _nt(k, k), ks)
    qk = each(_nt, qs, ks)
    dec = each(lambda a, b: jnp.where(inc, jnp.exp(jnp.where(inc, a - b, 0.0)), 0.0), gi, gj)
    lmat = each(lambda b, d, m: before * (b * d * m), beta, dec, kk)
    if xs is None:
        x = each(lambda m: eye - m, lmat)
        p2 = each(lambda m: _mdot(m, m), lmat)
        for it in range(4):
            y = each(lambda a, b: _mdot(jnp.concatenate([a, b], axis=0), b), x, p2)
            x = each(lambda a, t: a + t[:CH], x, y)
            p2 = each(lambda t: t[CH:], y)
        x = each(lambda a, b: a + _mdot(a, b), x, p2)
    else:
        x = each(_saved_inverse, lmat, xs)
    eg = each(jnp.exp, gam)
    u = each(lambda a, b, v: _mdot(a, b * v), x, beta, vs)
    w = each(lambda a, b, e, k: _mdot(a, (b * e) * k), x, beta, eg, ks)
    tot = each(lambda a: jnp.sum(a, axis=0, keepdims=True), la)
    kd = each(lambda k, t, g: k * jnp.exp(t - g), ks, tot, gam)
    gl = each(lambda t: jnp.broadcast_to(jnp.exp(t), (1, HD)), tot)
    qd = each(lambda q, e: q * e, qs, eg)
    p = each(lambda d, m: d * m, dec, qk)
    return (u, w, kd, qd, p, gl, x) if xs is None else (u, w, kd, qd, p, gl)


def _dir_head_sel(d, h):
    lane = lax.broadcasted_iota(jnp.int32, (1, HD), 1)
    return (lane == d * GH + h).astype(F32), (lane == 2 * GH + d * GH + h).astype(F32)


def _intra_specs(T, G):
    nc = T // CH
    assert nc % G == 0
    qkv = pl.BlockSpec((None, G * CH, HD), lambda d, h, c: (h, c, 0))
    bl = pl.BlockSpec((G * CH, HD), lambda d, h, c: (c, 0))
    big = pl.BlockSpec((None, None, G * CH, HD), lambda d, h, c: (d, h, c, 0))
    pm = pl.BlockSpec((None, None, G * CH, CH), lambda d, h, c: (d, h, c, 0))
    gl = pl.BlockSpec((None, None, G, 1, HD), lambda d, h, c: (d, h, c, 0, 0))
    shapes = (_sds((2, GH, T, HD)),) + (_sds((2, GH, T, HD), BF16),) * 3 + (
        _sds((2, GH, T, CH), BF16), _sds((2, GH, nc, 1, HD)), _sds((2, GH, T, CH)))
    return nc, qkv, bl, big, pm, gl, shapes


def _chunks_per_step(T, most):
    nc = T // CH
    return max(g for g in range(1, most + 1) if nc % g == 0)


def _intra_fwd(q, k, v, bl, exch):
    T = q.shape[1]
    G = _chunks_per_step(T, INTRA_FWD_CHUNKS)
    nc, qkv_s, bl_s, big, pm, gl_s, shapes = _intra_specs(T, G)

    def body(q_ref, k_ref, v_ref, bl_ref, u_ref, w_ref, kd_ref, qd_ref, p_ref, gl_ref, x_ref):
        d, h = pl.program_id(0), pl.program_id(1)
        sb, sl = _dir_head_sel(d, h)
        rows = [slice(g * CH, (g + 1) * CH) for g in range(G)]
        outs = _intra_fn(_chunk_masks(d), sb, sl, *[[r[s, :] for s in rows] for r in (q_ref, k_ref, v_ref, bl_ref)])
        for g in range(G):
            for r, o in zip((u_ref, w_ref, kd_ref, qd_ref, p_ref, x_ref), outs[:5] + outs[6:]):
                r[rows[g], :] = o[g].astype(r.dtype)
            gl_ref[g] = outs[5][g]

    return _call_carrying(body, exch, name="gdn_intra_fwd", out_shape=shapes, grid=(2, GH, nc // G),
                          in_specs=[qkv_s, qkv_s, qkv_s, bl_s], out_specs=(big, big, big, big, pm, gl_s, pm))(q, k, v, bl)


def _intra_bwd(q, k, v, bl, xinv, cts, exch):
    T = q.shape[1]
    G = _chunks_per_step(T, INTRA_BWD_CHUNKS)
    nc, qkv_s, bl_s, big, pm, gl_s, _ = _intra_specs(T, G)

    def body(q_ref, k_ref, v_ref, bl_ref, x_ref, du, dw, dkd, dqd, dp, dgl, dq_ref, dk_ref, dv_ref, dbl_ref):
        d, h = pl.program_id(0), pl.program_id(1)
        sb, sl = _dir_head_sel(d, h)
        rows = [slice(g * CH, (g + 1) * CH) for g in range(G)]
        fn = functools.partial(_intra_fn, _chunk_masks(d), sb, sl, xs=[x_ref[s, :] for s in rows])
        _, vjp = jax.vjp(fn, *[[r[s, :] for s in rows] for r in (q_ref, k_ref, v_ref, bl_ref)])
        cts = tuple([r[s, :] for s in rows] for r in (du, dw, dkd, dqd, dp)) + ([dgl[g] for g in range(G)],)
        grads = vjp(cts)
        for g in range(G):
            for r, o in zip((dq_ref, dk_ref, dv_ref, dbl_ref), grads):
                r[rows[g], :] = o[g]

    return _call_carrying(body, exch, name="gdn_intra_bwd", out_shape=(_sds((2, GH, T, HD)),) * 4,
                          grid=(2, GH, nc // G), in_specs=[qkv_s, qkv_s, qkv_s, bl_s, pm, big, big, big, big, pm, gl_s],
                          out_specs=(big,) * 4)(q, k, v, bl, xinv, *cts)


def _scan_fn(s, u, w, kd, qd, p, gl):
    each = lambda f, *ls: [f(*t) for t in zip(*ls)]
    ws = each(_nn, w, s)
    delta = each(lambda a, b: a - b, u, ws)
    kdd = each(_tn, kd, delta)
    s_new = each(lambda g, a, b: g * a + b, gl, s, kdd)
    qs = each(_nn, qd, s)
    pd = each(_nn, p, delta)
    return each(lambda a, b: a + b, qs, pd), s_new


SCAN_BLOCK = 4


def _scan_visit(t, d, nb, ncb):
    rev = jnp.where(t < ncb, ncb - 1 - t, nb - 1 - (t - ncb))
    return jnp.where(d == 0, t, rev)


def _scan_specs(T, L, back):
    tb = SCAN_BLOCK * CH
    assert T % tb == 0 and L % tb == 0
    nb, ncb = T // tb, L // tb

    def at(d, t):
        return _scan_visit(nb - 1 - t if back else t, d, nb, ncb)

    big = pl.BlockSpec((None, GH, tb, HD), lambda d, t: (d, 0, at(d, t), 0))
    pm = pl.BlockSpec((None, GH, tb, CH), lambda d, t: (d, 0, at(d, t), 0))
    gl = pl.BlockSpec((None, GH, SCAN_BLOCK, 1, HD), lambda d, t: (d, 0, at(d, t), 0, 0))
    st = pl.BlockSpec((None, GH, SCAN_BLOCK, HD, HD), lambda d, t: (d, 0, at(d, t), 0, 0))
    do = pl.BlockSpec((GH, tb, HD), lambda d, t: (0, at(d, t), 0))
    return nb, big, pm, gl, st, do


def _scan_fwd(u, w, kd, qd, p, gl, L):
    T = u.shape[2]
    nb, big, pm, gl_s, st, _ = _scan_specs(T, L, False)
    heads = range(GH)

    def body(u_ref, w_ref, kd_ref, qd_ref, p_ref, gl_ref, o_ref, st_ref, s_scr):
        d = pl.program_id(0)

        @pl.when(pl.program_id(1) == 0)
        def _():
            s_scr[...] = jnp.zeros_like(s_scr)

        s = [s_scr[h] for h in heads]
        for i in range(SCAN_BLOCK):
            c = jnp.where(d == 0, i, SCAN_BLOCK - 1 - i)
            rows = pl.ds(pl.multiple_of(c * CH, CH), CH)
            for h in heads:
                st_ref[h, c] = s[h]
            o, s = _scan_fn(s, *[[r[h, rows, :].astype(F32) for h in heads] for r in (u_ref, w_ref, kd_ref, qd_ref, p_ref)],
                            [gl_ref[h, c] for h in heads])
            for h in heads:
                o_ref[h, rows, :] = o[h]
        for h in heads:
            s_scr[h] = s[h]

    return _call(body, name="gdn_scan_fwd", out_shape=(_sds((2, GH, T, HD)), _sds((2, GH, T // CH, HD, HD))),
                 grid=(2, nb), in_specs=[big, big, big, big, pm, gl_s], out_specs=(big, st),
                 scratch=[pltpu.VMEM((GH, HD, HD), F32)], sem=("parallel", "arbitrary"))(u, w, kd, qd, p, gl)


def _scan_bwd(u, w, kd, qd, p, gl, states, do, L, exch):
    T = u.shape[2]
    nb, big, pm, gl_s, st, do_s = _scan_specs(T, L, True)
    heads = range(GH)

    def body(u_ref, w_ref, kd_ref, qd_ref, p_ref, gl_ref, st_ref, do_ref,
             du_ref, dw_ref, dkd_ref, dqd_ref, dp_ref, dgl_ref, ds_scr):
        d = pl.program_id(0)

        @pl.when(pl.program_id(1) == 0)
        def _():
            ds_scr[...] = jnp.zeros_like(ds_scr)

        ds = [ds_scr[h] for h in heads]
        for i in range(SCAN_BLOCK):
            c = jnp.where(d == 0, SCAN_BLOCK - 1 - i, i)
            rows = pl.ds(pl.multiple_of(c * CH, CH), CH)
            _, vjp = jax.vjp(_scan_fn, [st_ref[h, c] for h in heads],
                             *[[r[h, rows, :].astype(F32) for h in heads] for r in (u_ref, w_ref, kd_ref, qd_ref, p_ref)],
                             [gl_ref[h, c] for h in heads])
            ds, gu, gw, gkd, gqd, gp, ggl = vjp(([do_ref[h, rows, :] for h in heads], ds))
            for h in heads:
                du_ref[h, rows, :] = gu[h]
                dw_ref[h, rows, :] = gw[h]
                dkd_ref[h, rows, :] = gkd[h]
                dqd_ref[h, rows, :] = gqd[h]
                dp_ref[h, rows, :] = gp[h]
                dgl_ref[h, c] = ggl[h]
        for h in heads:
            ds_scr[h] = ds[h]

    return _call_carrying(
        body, exch, name="gdn_scan_bwd",
        out_shape=(_sds((2, GH, T, HD)),) * 4 + (_sds((2, GH, T, CH)), _sds((2, GH, T // CH, 1, HD))),
        grid=(2, nb), in_specs=[big, big, big, big, pm, gl_s, st, do_s], out_specs=(big, big, big, big, pm, gl_s),
        scratch=[pltpu.VMEM((GH, HD, HD), F32)])(u, w, kd, qd, p, gl, states, do)


def _gout_fn(o0, o1, z, gw):
    return _rms(o0 + o1) * gw * _silu(z)


def _gout_fwd(o, proj, gw, L):
    T = o.shape[2]
    N = T - L
    ob = pl.BlockSpec((2, None, T, HD), lambda h: (0, h, 0, 0))

    def body(o_ref, z_ref, gw_ref, y_ref):
        y_ref[...] = _gout_fn(o_ref[0, L:, :], o_ref[1, L:, :], z_ref[L:, :], gw_ref[...]).astype(BF16)

    return _call(body, name="gout_fwd", out_shape=_sds((N, GH * HD), BF16), grid=(GH,),
                 in_specs=[ob, pl.BlockSpec((T, HD), lambda h: (0, C_Z // HD + h)), pl.BlockSpec((1, HD), lambda h: (0, 0))],
                 out_specs=pl.BlockSpec((N, HD), lambda h: (0, h)), sem=("parallel",))(o, proj, gw)


def _gout_bwd(o, proj, gw, dy, dproj, L):
    T = o.shape[2]
    N = T - L
    ob = pl.BlockSpec((2, None, T, HD), lambda h: (0, h, 0, 0))

    def body(o_ref, z_ref, gw_ref, dy_ref, _, do_ref, dz_ref, dgw_ref):
        _, vjp = jax.vjp(_gout_fn, o_ref[0, L:, :], o_ref[1, L:, :], z_ref[L:, :], gw_ref[...])
        g0, _, gz, ggw = vjp(dy_ref[...])
        do_ref[:L, :] = jnp.zeros((L, HD), F32)
        do_ref[L:, :] = g0
        dz_ref[:L, :] = jnp.zeros((L, HD), BF16)
        dz_ref[L:, :] = gz.astype(BF16)

        @pl.when(pl.program_id(0) == 0)
        def _():
            dgw_ref[...] = jnp.zeros_like(dgw_ref)

        dgw_ref[...] += ggw

    zb = pl.BlockSpec((T, HD), lambda h: (0, C_Z // HD + h))
    return _call(body, name="gout_bwd", out_shape=(_sds((GH, T, HD)), _sds(dproj.shape, BF16), _sds((1, HD))),
                 grid=(GH,),
                 in_specs=[ob, zb, pl.BlockSpec((1, HD), lambda h: (0, 0)), pl.BlockSpec((N, HD), lambda h: (0, h)), ANYSPEC],
                 out_specs=(pl.BlockSpec((None, T, HD), lambda h: (h, 0, 0)), zb, pl.BlockSpec((1, HD), lambda h: (0, 0))),
                 aliases={4: 1}, sem=("arbitrary",))(o, proj, gw, dy, dproj)


def _merge_fn(pa, pd, ga, gd):
    return jax.nn.sigmoid(ga) * pa + jax.nn.sigmoid(gd) * pd


def _merge_fwd(pa, pd, proj, L, *, br=256):
    N = pa.shape[0]
    lb = L // br
    row = pl.BlockSpec((br, D), lambda i: (i, 0))

    def body(pa_ref, pd_ref, ga_ref, gd_ref, y_ref):
        y_ref[...] = _merge_fn(pa_ref[...], pd_ref[...], ga_ref[...], gd_ref[...]).astype(BF16)

    return _call(body, name="merge_fwd", out_shape=_sds((N, D), BF16), grid=(N // br,),
                 in_specs=[row, row, pl.BlockSpec((br, D), lambda i: (i + lb, C_GATE // D)),
                           pl.BlockSpec((br, D), lambda i: (i + lb, C_GATE // D + 1))],
                 out_specs=row, sem=("parallel",))(pa, pd, proj, proj)


def _merge_bwd(pa, pd, proj, dy, L, *, br=256):
    N = pa.shape[0]
    T = N + L
    lb = L // br
    lrow = pl.BlockSpec((br, D), lambda i: (jnp.maximum(i - lb, 0), 0))

    def body(pa_ref, pd_ref, ga_ref, gd_ref, dy_ref, dpa_ref, dpd_ref, dg_ref):
        lat = pl.program_id(0) >= lb
        _, vjp = jax.vjp(_merge_fn, pa_ref[...], pd_ref[...], ga_ref[...], gd_ref[...])
        gpa, gpd, gga, ggd = vjp(dy_ref[...])
        dpa_ref[...] = gpa.astype(BF16)
        dpd_ref[...] = gpd.astype(BF16)
        dg_ref[:, :D] = jnp.where(lat, gga, 0.0).astype(BF16)
        dg_ref[:, D:] = jnp.where(lat, ggd, 0.0).astype(BF16)

    return _call(body, name="merge_bwd", out_shape=(_sds((N, D), BF16), _sds((N, D), BF16), _sds((T, C_END), BF16)),
                 grid=(T // br,),
                 in_specs=[lrow, lrow, pl.BlockSpec((br, D), lambda i: (i, C_GATE // D)),
                           pl.BlockSpec((br, D), lambda i: (i, C_GATE // D + 1)), lrow],
                 out_specs=(lrow, lrow, pl.BlockSpec((br, 2 * D), lambda i: (i, C_GATE // (2 * D)))),
                 sem=("arbitrary",))(pa, pd, proj, proj, dy)


def _resid_fwd(x, m, mod, i_g, *, name, br=256):
    R = x.shape[0]
    row = pl.BlockSpec((br, D), lambda i: (i, 0))

    def body(x_ref, m_ref, mod_ref, o_ref):
        o_ref[...] = x_ref[...] + mod_ref[i_g:i_g + 1, :] * m_ref[...]

    return _call(body, name=name, out_shape=_sds((R, D)), grid=(R // br,),
                 in_specs=[row, row, pl.BlockSpec((6, D), lambda i: (0, 0))], out_specs=row,
                 sem=("parallel",))(x, m, mod)


def _resid_bwd(dx, m, mod, i_g, *, name, br=256):
    R = dx.shape[0]
    row = pl.BlockSpec((br, D), lambda i: (i, 0))
    vec = pl.BlockSpec((1, D), lambda i: (0, 0))

    def body(dx_ref, m_ref, mod_ref, dm_ref, dg_ref):
        dxv = dx_ref[...]
        dm_ref[...] = (dxv * mod_ref[i_g:i_g + 1, :]).astype(BF16)

        @pl.when(pl.program_id(0) == 0)
        def _():
            dg_ref[...] = jnp.zeros_like(dg_ref)

        dg_ref[...] += jnp.sum(dxv * m_ref[...], axis=0, keepdims=True)

    return _call(body, name=name, out_shape=(_sds((R, D), BF16), _sds((1, D))), grid=(R // br,),
                 in_specs=[row, row, pl.BlockSpec((6, D), lambda i: (0, 0))], out_specs=(row, vec),
                 sem=("arbitrary",))(dx, m, mod)


def _ffn_fn(shifts, ug, uv, wg, wv, bg, bv):
    down, up = shifts

    def conv(x, w, b):
        return down(x) * w[0:1, :] + x * w[1:2, :] + up(x) * w[2:3, :] + b

    return _silu(conv(ug, wg, bg)) * conv(uv, wv, bv)


def _ffn_fwd(up, cw, cb, *, bw=256):
    N = up.shape[0]
    shifts = _make_shift(((0, N),))
    nb = DFF // bw

    def body(ug, uv, wg, wv, bg, bv, a_ref):
        a_ref[...] = _ffn_fn(shifts, ug[...], uv[...], wg[...], wv[...], bg[...], bv[...]).astype(BF16)

    def col(rows, off):
        return pl.BlockSpec((rows, bw), lambda j: (0, j + off))

    return _call(body, name="ffn_fwd", out_shape=_sds((N, DFF), BF16), grid=(nb,),
                 in_specs=[col(N, 0), col(N, nb), col(3, 0), col(3, nb), col(1, 0), col(1, nb)],
                 out_specs=col(N, 0), sem=("parallel",), vmem=VMEM_BIG)(up, up, cw, cw, cb, cb)


def _ffn_bwd(up, cw, cb, da, *, bw=256):
    N = up.shape[0]
    shifts = _make_shift(((0, N),))
    nb = DFF // bw

    def body(ug, uv, wg, wv, bg, bv, da_ref, dug, duv, dwg, dwv, dbg, dbv):
        _, vjp = jax.vjp(functools.partial(_ffn_fn, shifts), ug[...], uv[...], wg[...], wv[...], bg[...], bv[...])
        g = vjp(da_ref[...])
        dug[...] = g[0].astype(BF16)
        duv[...] = g[1].astype(BF16)
        dwg[...], dwv[...], dbg[...], dbv[...] = g[2], g[3], g[4], g[5]

    def col(rows, off):
        return pl.BlockSpec((rows, bw), lambda j: (0, j + off))

    half = (_sds((N, DFF), BF16), _sds((N, DFF), BF16), _sds((3, DFF)), _sds((3, DFF)), _sds((1, DFF)), _sds((1, DFF)))
    dug, duv, dwg, dwv, dbg, dbv = _call(
        body, name="ffn_bwd", out_shape=half, grid=(nb,),
        in_specs=[col(N, 0), col(N, nb), col(3, 0), col(3, nb), col(1, 0), col(1, nb), col(N, 0)],
        out_specs=(col(N, 0), col(N, 0), col(3, 0), col(3, 0), col(1, 0), col(1, 0)),
        sem=("parallel",), vmem=VMEM_BIG)(up, up, cw, cw, cb, cb, da)
    return (jnp.concatenate([dug, duv], axis=1), jnp.concatenate([dwg, dwv], axis=1),
            jnp.concatenate([dbg, dbv], axis=1))


def _head_fn(x1, dn, g2, fw, tgt):
    y = _rms(x1 + g2 * dn) * fw
    err = y - tgt
    return 0.5 * jnp.sum(jnp.mean(err * err, axis=-1))


def _head(x1, dn, mod, fw, tgt, *, br=256):
    N = x1.shape[0]
    row = pl.BlockSpec((br, D), lambda i: (i, 0))
    vec = pl.BlockSpec((1, D), lambda i: (0, 0))
    one = pl.BlockSpec((1, HD), lambda i: (0, 0))

    def body(x1_ref, dn_ref, mod_ref, fw_ref, tgt_ref, loss_ref, dx_ref, ddn_ref, dg_ref, dfw_ref):
        loss, (gx, gdn, gg, gfw) = jax.value_and_grad(_head_fn, argnums=(0, 1, 2, 3))(
            x1_ref[...], dn_ref[...], mod_ref[5:6, :], fw_ref[...], tgt_ref[...])
        dx_ref[...] = gx
        ddn_ref[...] = gdn.astype(BF16)

        @pl.when(pl.program_id(0) == 0)
        def _():
            loss_ref[...] = jnp.zeros_like(loss_ref)
            dg_ref[...] = jnp.zeros_like(dg_ref)
            dfw_ref[...] = jnp.zeros_like(dfw_ref)

        loss_ref[...] += jnp.broadcast_to(loss, (1, HD))
        dg_ref[...] += gg
        dfw_ref[...] += gfw

    return _call(body, name="head", out_shape=(_sds((1, HD)), _sds((N, D)), _sds((N, D), BF16), _sds((1, D)), _sds((1, D))),
                 grid=(N // br,), in_specs=[row, row, pl.BlockSpec((6, D), lambda i: (0, 0)), vec, row],
                 out_specs=(one, row, row, vec, vec), sem=("arbitrary",))(x1, dn, mod, fw, tgt)


def _adamw(w, g, m, v, *, name):
    shape = w.shape
    cols = shape[-1]
    rows = max(1, math.prod(shape[:-1]))
    w2, g2, m2, v2 = (t.reshape(rows, cols) for t in (w, g, m, v))
    br = 256 if rows % 256 == 0 else rows
    c1 = 1.0 - B1 ** STEP
    c2 = 1.0 - B2 ** STEP

    def body(w_ref, g_ref, m_ref, v_ref, d_ref, nm_ref, nv_ref):
        gv = g_ref[...]
        nm = B1 * m_ref[...] + (1.0 - B1) * gv
        nv = B2 * v_ref[...] + (1.0 - B2) * (gv * gv)
        d_ref[...] = -LR * ((nm / c1) / (jnp.sqrt(nv / c2) + AEPS) + WD * w_ref[...])
        nm_ref[...] = nm
        nv_ref[...] = nv

    blk = pl.BlockSpec((br, cols), lambda i: (i, 0))
    outs = _call(body, name=name, out_shape=(_sds((rows, cols)),) * 3, grid=(rows // br,),
                 in_specs=[blk] * 4, out_specs=(blk,) * 3, sem=("parallel",))(w2, g2, m2, v2)
    return tuple(t.reshape(shape) for t in outs)


def _adamw_many(items, *, name):
    k = len(items)
    shapes = [w.shape for w, _, _, _ in items]
    flat = [t.reshape(max(1, math.prod(t.shape[:-1])), t.shape[-1]) for it in items for t in it]
    c1 = 1.0 - B1 ** STEP
    c2 = 1.0 - B2 ** STEP

    def body(*refs):
        ins, outs = refs[:4 * k], refs[4 * k:]
        for i in range(k):
            w_ref, g_ref, m_ref, v_ref = ins[4 * i:4 * i + 4]
            gv = g_ref[...]
            nm = B1 * m_ref[...] + (1.0 - B1) * gv
            nv = B2 * v_ref[...] + (1.0 - B2) * (gv * gv)
            outs[3 * i][...] = -LR * ((nm / c1) / (jnp.sqrt(nv / c2) + AEPS) + WD * w_ref[...])
            outs[3 * i + 1][...] = nm
            outs[3 * i + 2][...] = nv

    res = _call(body, name=name, out_shape=tuple(_sds(flat[4 * i].shape) for i in range(k) for _ in range(3)))(*flat)
    return [tuple(res[3 * i + j].reshape(shapes[i]) for j in range(3)) for i in range(k)]


def _rope_tables(N, L):
    t = jnp.arange(N)
    pos = jnp.stack([(t // GRID_W).astype(F32), (t % GRID_W).astype(F32)], axis=1)
    inv = ROPE_THETA ** (-jnp.arange(0, HD // 2, 2, dtype=F32) / (HD // 2))
    ang = pos[:, :, None] * inv[None, None, :]
    cos = jnp.broadcast_to(jnp.cos(ang)[:, :, None, :], (N, 2, 2, HD // 4)).reshape(N, HD)
    sin = jnp.broadcast_to(jnp.sin(ang)[:, :, None, :], (N, 2, 2, HD // 4))
    sin = (sin * jnp.array([-1.0, 1.0], F32)[None, None, :, None]).reshape(N, HD)
    cos = jnp.concatenate([jnp.ones((L, HD), F32), cos], axis=0)
    sin = jnp.concatenate([jnp.zeros((L, HD), F32), sin], axis=0)
    return cos, sin


def _pad_lanes(v, off=0):
    return jnp.zeros((1, HD), F32).at[0, off:off + v.shape[0]].set(v)


def _local_step(x, ctx, tgt, mod_lat, mod_ctx, w_in, shards, small):
    N, L = x.shape[0], ctx.shape[0]
    T = N + L
    bounds = ((0, L), (L, T))
    qw, kw, gw = small["q_norm_w"], small["k_norm_w"], small["gdn_norm_w"]
    conv_w, ffn_w, ffn_b, fnw = small["conv_qkv_w"], small["ffn_conv_w"], small["ffn_conv_b"], small["final_norm_w"]
    alog = _pad_lanes(small["a_log"].reshape(-1), 2 * GH)
    dtb = _pad_lanes(small["dt_bias"].reshape(-1), 2 * GH)
    cos, sin = _rope_tables(N, L)
    bt = T
    bnl = 256 if N % 1024 else 1024

    hc = _normmod_fwd(ctx, mod_ctx, 0, 1, name="normmod_ctx")
    hx = _normmod_fwd(x, mod_lat, 0, 1, name="normmod_x")
    h1 = jnp.concatenate([hc, hx], axis=0)
    proj = _mm(h1, w_in, name="mm_in", M=T, N=C_END, K=D, tb=True, bm=bt, bn=1024)
    aq, ak, av = _aprep_fwd(proj, cos, sin, qw, kw)
    (attn, attn32, lse), (up_g,) = _attn_fwd(aq, ak, av, L, _GatherTwoLevel([shards["w_up"]]))
    gq = _gprep_fwd(proj, conv_w, 0, bounds)
    gk = _gprep_fwd(proj, conv_w, 1, bounds)
    gv = _gprep_fwd(proj, conv_w, 2, bounds)
    bl = _bl_fwd(proj, alog, dtb)
    intra, (down_g, pa_g, pd_g, out_g) = _intra_fwd(
        gq, gk, gv, bl, _GatherTwoLevel([shards[n] for n in ("w_down", "w_pa", "w_pd", "w_out")]))
    w_up, w_down = up_g.reshape(2 * DFF, D), down_g.reshape(DFF, D)
    w_pa, w_pd, w_out = pa_g.reshape(D, D), pd_g.reshape(D, D), out_g.reshape(D, D)
    xinv, intra = intra[6], intra[:6]
    o, states = _scan_fwd(*intra, L)
    gdn = _gout_fwd(o, proj, gw, L)
    pa = _mm(attn, w_pa, name="mm_pa", M=N, N=D, K=D, bm=bnl)
    pd = _mm(gdn, w_pd, name="mm_pd", M=N, N=D, K=D, bm=bnl)
    y = _merge_fwd(pa, pd, proj, L)
    m = _mm(y, w_out, name="mm_out", M=N, N=D, K=D, bm=bnl)
    x1 = _resid_fwd(x, m, mod_lat, 2, name="resid1")
    h2 = _normmod_fwd(x1, mod_lat, 3, 4, name="normmod_x1")
    up = _mm(h2, w_up, name="mm_up", M=N, N=2 * DFF, K=D, tb=True, bm=bnl, bn=2 * DFF // 4)
    a = _ffn_fwd(up, ffn_w, ffn_b)
    dn = _mm(a, w_down, name="mm_down", M=N, N=D, K=DFF, bm=bnl)
    loss, dx2, ddn, dg2, dfnw = _head(x1, dn, mod_lat, fnw, tgt)

    da = _mm(ddn, w_down, name="mm_down_dx", M=N, N=DFF, K=D, tb=True, bm=bnl, bn=DFF // 2)
    g_down = _mm(a, ddn, name="mm_down_dw", M=DFF, N=D, K=N, ta=True, bm=DFF // 2, out_dtype=BF16)
    dup, d_ffn_w, d_ffn_b = _ffn_bwd(up, ffn_w, ffn_b, da)
    dh2 = _mm(dup, w_up, name="mm_up_dx", M=N, N=D, K=2 * DFF, bm=bnl, bk=2 * DFF // 4)
    g_up = _mm(dup, h2, name="mm_up_dw", M=2 * DFF, N=D, K=N, ta=True, bm=2 * DFF // 4, out_dtype=BF16)
    dx1, dsh2, dsc2 = _normmod_bwd(x1, mod_lat, 3, 4, dh2, 0, dx2, name="normmod_x1_bwd")
    dm, dg1 = _resid_bwd(dx1, m, mod_lat, 2, name="resid1_bwd")
    dy = _mm(dm, w_out, name="mm_out_dx", M=N, N=D, K=D, tb=True, bm=bnl)
    g_out = _mm(y, dm, name="mm_out_dw", M=D, N=D, K=N, ta=True, out_dtype=BF16)
    dpa, dpd, dproj = _merge_bwd(pa, pd, proj, dy, L)
    dattn = _mm(dpa, w_pa, name="mm_pa_dx", M=N, N=D, K=D, tb=True, bm=bnl)
    g_pa = _mm(attn, dpa, name="mm_pa_dw", M=D, N=D, K=N, ta=True, out_dtype=BF16)
    dgdn = _mm(dpd, w_pd, name="mm_pd_dx", M=N, N=D, K=D, tb=True, bm=bnl)
    g_pd = _mm(gdn, dpd, name="mm_pd_dw", M=D, N=D, K=N, ta=True, out_dtype=BF16)
    do, dproj, dgw = _gout_bwd(o, proj, gw, dgdn, dproj, L)
    cts, recv_a = _scan_bwd(*intra, states, do, L, _Exchange(
        [g_out.reshape(NDEV, D // NDEV, D), g_down.reshape(NDEV, DFF // NDEV, D)], True))
    (dgq, dgk, dgv, dbl), recv_b = _intra_bwd(gq, gk, gv, bl, xinv, cts, _Exchange(
        [g_pa.reshape(NDEV, D // NDEV, D), g_pd.reshape(NDEV, D // NDEV, D), g_up.reshape(NDEV, 2 * DFF // NDEV, D)], True))
    recv = dict(zip(("w_out", "w_down", "w_pa", "w_pd", "w_up"), recv_a + recv_b))
    dproj, dwq = _gprep_bwd(proj, conv_w, 0, bounds, dgq, dproj)
    dproj, dwk = _gprep_bwd(proj, conv_w, 1, bounds, dgk, dproj)
    dproj, dwv = _gprep_bwd(proj, conv_w, 2, bounds, dgv, dproj)
    dproj, dalog, ddtb = _bl_bwd(proj, alog, dtb, dbl, dproj)
    daq_h, dak_h, dav_h = _attn_bwd(aq, ak, av, attn32, lse, dattn, L)
    dproj, dqw, dkw = _aprep_bwd(proj, cos, sin, qw, kw, daq_h, dak_h, dav_h, dproj, L)
    g_in = _mm(dproj, h1, name="mm_in_dw", M=C_END, N=D, K=T, ta=True, bm=1024, out_dtype=BF16)
    g_in = _unpad_columns(g_in).reshape(NDEV, W_END // NDEV, D)
    own_in = lax.dynamic_index_in_dim(g_in, _position()[3], axis=0, keepdims=False)
    *pending, token = _scatter_start(g_in, None, (0, D // 2), (), name="scatter_g_in_a_start")
    dh1 = _mm(dproj, w_in, name="mm_in_dx", M=T, N=D, K=C_END, bm=bt, bk=1024, after=(token,))
    grad_x, dsh1, dsc1 = _normmod_bwd(x, mod_lat, 0, 1, dh1, L, dx1, name="normmod_x_bwd")
    _, dcsh1, dcsc1 = _normmod_bwd(ctx, mod_ctx, 0, 1, dh1, 0, None, name="normmod_ctx_bwd")

    z1 = jnp.zeros((1, D), F32)
    dmod_lat = jnp.concatenate([dsh1, dsc1, dg1, dsh2, dsc2, dg2], axis=0)
    dmod_ctx = jnp.concatenate([dcsh1, dcsc1, z1, z1, z1, z1], axis=0)
    gsmall = {
        "q_norm_w": dqw, "k_norm_w": dkw, "gdn_norm_w": dgw,
        "conv_qkv_w": jnp.concatenate([dwq, dwk, dwv], axis=1),
        "a_log": dalog[0, 2 * GH:4 * GH], "dt_bias": ddtb[0, 2 * GH:4 * GH],
        "ffn_conv_w": d_ffn_w, "ffn_conv_b": d_ffn_b, "final_norm_w": dfnw,
    }
    return loss[0, 0], grad_x, (pending, own_in), recv, dmod_lat, dmod_ctx, gsmall


HBM = pl.BlockSpec(memory_space=pltpu.HBM)
ANYSPEC = pl.BlockSpec(memory_space=pl.ANY)


def _position():
    x, y, c = lax.axis_index("x"), lax.axis_index("y"), lax.axis_index("c")
    return x, y, c, 4 * x + 2 * y + c


def _peer(x, y, c, k):
    px = 1 - x if k & 4 else x
    py = 1 - y if k & 2 else y
    pc = 1 - c if k & 1 else c
    return (px, py, pc), 4 * px + 2 * py + pc


def _exchange(arrs, *, name, scatter):
    exch = _Exchange(arrs, scatter)
    n = exch.n

    def body(*refs):
        ins, outs, sems = refs[:n], refs[n:2 * n], refs[2 * n:]
        exch.start(ins, outs, sems)
        exch.finish(ins, outs, sems)

    outs = pl.pallas_call(body, name=name, out_shape=exch.out_shape, in_specs=[HBM] * n, out_specs=(HBM,) * n,
                          scratch_shapes=exch.scratch,
                          compiler_params=pltpu.CompilerParams(has_side_effects=True))(*arrs)
    return list(outs)


class _Exchange:
    def __init__(self, arrs, scatter):
        self.arrs, self.scatter, self.n = list(arrs), scatter, len(arrs)
        self.out_shape = tuple(_sds(a.shape if scatter else (NDEV,) + a.shape, a.dtype) for a in arrs)
        self.scratch = [pltpu.SemaphoreType.DMA((self.n, NDEV - 1)), pltpu.SemaphoreType.DMA((self.n, NDEV - 1)),
                        pltpu.SemaphoreType.DMA((self.n,))]

    def _copies(self, ins, outs, sems):
        send, recv, loc = sems
        x, y, c, me = _position()
        local = [pltpu.make_async_copy(ins[a].at[me] if self.scatter else ins[a], outs[a].at[me], loc.at[a])
                 for a in range(self.n)]
        remote = []
        for k in range(1, NDEV):
            peer, pid = _peer(x, y, c, k)
            for a in range(self.n):
                src = ins[a].at[pid] if self.scatter else ins[a]
                remote.append(pltpu.make_async_remote_copy(
                    src_ref=src, dst_ref=outs[a].at[me], send_sem=send.at[a, k - 1], recv_sem=recv.at[a, k - 1],
                    device_id=peer, device_id_type=MESH))
        return local, remote

    def start(self, ins, outs, sems):
        local, remote = self._copies(ins, outs, sems)
        for cp in local + remote:
            cp.start()

    def finish(self, ins, outs, sems):
        local, remote = self._copies(ins, outs, sems)
        for cp in remote:
            cp.wait()
        for cp in local:
            cp.wait()


class _GatherTwoLevel:
    scatter = False

    def __init__(self, arrs):
        self.arrs, self.n = list(arrs), len(arrs)
        self.out_shape = tuple(_sds((NDEV,) + a.shape, a.dtype) for a in arrs)
        self.scratch = [pltpu.SemaphoreType.DMA((self.n, NDEV - 1)), pltpu.SemaphoreType.DMA((self.n, NDEV - 1)),
                        pltpu.SemaphoreType.DMA((self.n,))]

    def _parts(self, ins, outs, sems):
        send, recv, loc = sems
        x, y, c, _ = _position()
        me, sibling = (x, y, c), (x, y, 1 - c)
        chips = [(1 - x, y), (x, 1 - y), (1 - x, 1 - y)]
        parts = []
        for a in range(self.n):
            slot = lambda px, py, pc, a=a: outs[a].at[4 * px + 2 * py + pc]

            def copy(k, owner, to, src=None, a=a, slot=slot):
                return pltpu.make_async_remote_copy(
                    src_ref=slot(*owner) if src is None else src, dst_ref=slot(*owner), send_sem=send.at[a, k],
                    recv_sem=recv.at[a, k], device_id=to, device_id_type=MESH)

            parts.append(dict(
                mine=pltpu.make_async_copy(ins[a], slot(*me), loc.at[a]),
                first=[copy(0, me, sibling, src=ins[a])] + [copy(1 + j, me, (*ch, c), src=ins[a]) for j, ch in enumerate(chips)],
                arrive=[copy(1 + j, (*ch, c), me) for j, ch in enumerate(chips)],
                passed=[copy(4 + j, (*ch, c), sibling) for j, ch in enumerate(chips)],
                rest=[copy(0, sibling, me)] + [copy(4 + j, (*ch, 1 - c), me) for j, ch in enumerate(chips)]))
        return parts

    def start(self, ins, outs, sems):
        for p in self._parts(ins, outs, sems):
            p["mine"].start()
            for cp in p["first"]:
                cp.start()

    def middle(self, ins, outs, sems):
        for p in self._parts(ins, outs, sems):
            for got, fwd in zip(p["arrive"], p["passed"]):
                got.wait_recv()
                fwd.start()

    def finish(self, ins, outs, sems):
        for p in self._parts(ins, outs, sems):
            for cp in p["rest"]:
                cp.wait_recv()
            for cp in p["first"] + p["passed"]:
                cp.wait_send()
            p["mine"].wait()


def _gather_two_level(block, *, name):
    def body(x_ref, out_ref, send_sems, recv_sems, local_sem):
        x, y, c, _ = _position()
        me, sibling = (x, y, c), (x, y, 1 - c)
        chips = [(1 - x, y), (x, 1 - y), (1 - x, 1 - y)]

        def slot(px, py, pc):
            return out_ref.at[4 * px + 2 * py + pc]

        def copy(k, owner, to, src=None):
            return pltpu.make_async_remote_copy(
                src_ref=slot(*owner) if src is None else src, dst_ref=slot(*owner), send_sem=send_sems.at[k],
                recv_sem=recv_sems.at[k], device_id=to, device_id_type=MESH)

        mine = pltpu.make_async_copy(x_ref, slot(*me), local_sem)
        mine.start()
        first = [copy(0, me, sibling, src=x_ref)]
        first += [copy(1 + j, me, (*chip, c), src=x_ref) for j, chip in enumerate(chips)]
        for cp in first:
            cp.start()
        passed = [copy(4 + j, (*chip, c), sibling) for j, chip in enumerate(chips)]
        for j, chip in enumerate(chips):
            copy(1 + j, (*chip, c), me).wait_recv()
            passed[j].start()
        copy(0, sibling, me).wait_recv()
        for j, chip in enumerate(chips):
            copy(4 + j, (*chip, 1 - c), me).wait_recv()
        for cp in first + passed:
            cp.wait_send()
        mine.wait()

    return pl.pallas_call(
        body, name=name, out_shape=_sds((NDEV,) + block.shape, block.dtype), in_specs=[HBM], out_specs=HBM,
        scratch_shapes=[pltpu.SemaphoreType.DMA((NDEV - 1,)), pltpu.SemaphoreType.DMA((NDEV - 1,)),
                        pltpu.SemaphoreType.DMA],
        compiler_params=pltpu.CompilerParams(has_side_effects=True))(block)


SEM = pl.BlockSpec(memory_space=pltpu.SEMAPHORE)


def _scatter_copies(src_ref, land_ref, send_sems, recv_sems, cols):
    x, y, c, me = _position()
    span = (slice(None), pl.ds(*cols))
    copies = []
    for k in range(1, NDEV):
        peer, pid = _peer(x, y, c, k)
        copies.append(pltpu.make_async_remote_copy(
            src_ref=src_ref.at[pid].at[span], dst_ref=land_ref.at[me].at[span], send_sem=send_sems.at[k - 1],
            recv_sem=recv_sems.at[k - 1], device_id=peer, device_id_type=MESH))
    return copies


SPLIT_EFFECT = pltpu.SideEffectType.DATAFLOW_SIDE_EFFECTING


def _scatter_start(parts, land, cols, after, *, name):
    na = len(after)
    if land is None:
        land = lax.empty(parts.shape, parts.dtype)

    def body(src_ref, land_ref, *rest):
        send_sems, recv_sems, _, _, token = rest[na:]
        for cp in _scatter_copies(src_ref, land_ref, send_sems, recv_sems, cols):
            cp.start()
        token[...] = jnp.zeros_like(token)

    return pl.pallas_call(
        body, name=name,
        out_shape=(pltpu.SemaphoreType.DMA((NDEV - 1,)), pltpu.SemaphoreType.DMA((NDEV - 1,)),
                   pltpu.HBM(parts.shape, parts.dtype), pltpu.HBM(parts.shape, parts.dtype), _sds((8, HD))),
        in_specs=(HBM, HBM) + (pl.BlockSpec(memory_space=pl.ANY),) * na,
        out_specs=(SEM, SEM, HBM, HBM, pl.BlockSpec(memory_space=pltpu.VMEM)),
        input_output_aliases={0: 2, 1: 3}, compiler_params=pltpu.CompilerParams(has_side_effects=SPLIT_EFFECT),
    )(pltpu.with_memory_space_constraint(parts, pltpu.HBM), pltpu.with_memory_space_constraint(land, pltpu.HBM), *after)


def _scatter_wait(send_sems, recv_sems, src_thru, land_thru, cols, after, *, name):
    na = len(after)

    def body(src_ref, land_ref, send_sems, recv_sems, *rest):
        for cp in _scatter_copies(src_ref, land_ref, send_sems, recv_sems, cols):
            cp.wait_send()
            cp.wait_recv()

    return pl.pallas_call(
        body, name=name,
        out_shape=(pltpu.HBM(src_thru.shape, src_thru.dtype), pltpu.HBM(land_thru.shape, land_thru.dtype)),
        in_specs=(HBM, HBM, SEM, SEM) + (pl.BlockSpec(memory_space=pl.ANY),) * na, out_specs=(HBM, HBM),
        input_output_aliases={0: 0, 1: 1}, compiler_params=pltpu.CompilerParams(has_side_effects=SPLIT_EFFECT),
    )(src_thru, land_thru, send_sems, recv_sems, *after)


def _cast_bf16(w, *, name):
    rows, cols = w.shape
    br = 128 if rows % 128 == 0 else rows

    def body(w_ref, o_ref):
        o_ref[...] = w_ref[...].astype(BF16)

    blk = pl.BlockSpec((br, cols), lambda i: (i, 0))
    return _call(body, name=name, out_shape=_sds((rows, cols), BF16), grid=(rows // br,), in_specs=[blk],
                 out_specs=blk, sem=("parallel",))(w)


def _sum_slots(a, *, name):
    _, R, C = a.shape

    def body(a_ref, o_ref):
        s = a_ref[0]
        for d in range(1, NDEV):
            s = s + a_ref[d]
        o_ref[...] = s

    return _call(body, name=name, out_shape=_sds((R, C)))(a)


MODROWS = 16


def _mod_fwd(c9, w, b):
    cols = w.shape[1]

    def body(c_ref, w_ref, b_ref, o_ref):
        o_ref[...] = _nn(_silu(c_ref[...]), w_ref[...]) + b_ref[...]

    return _call(body, name="mod_fwd", out_shape=_sds((MODROWS, cols)))(c9, w, b)


def _mod_bwd(c9, dmy, dall, w):
    cols = w.shape[1]

    def body(c_ref, dmy_ref, dall_ref, w_ref, gw_ref, gb_ref, cp_ref):
        sc = _silu(c_ref[...])
        rows = lax.broadcasted_iota(jnp.int32, (MODROWS, 1), 0)
        d = dmy_ref[...]
        d_ctx = jnp.where(rows == NDEV, d, 0.0)
        sc_ctx = jnp.where(rows == NDEV, sc, 0.0)
        outer = lax.dot_general(sc_ctx, d_ctx, (((0,), (0,)), ((), ())), precision=HI, preferred_element_type=F32)
        gw_ref[...] = _tn(jnp.where(rows < NDEV, sc, 0.0), jnp.where(rows < NDEV, d, 0.0)) + outer
        gb_ref[...] = jnp.sum(dall_ref[...], axis=0, keepdims=True)
        cp_ref[...] = jnp.sum(_nt(d_ctx, w_ref[...]), axis=0, keepdims=True)

    return _call(body, name="mod_bwd", out_shape=(_sds((D, cols)), _sds((1, 6 * D)), _sds((1, D))),
                 vmem=VMEM_BIG)(c9, dmy, dall, w)


def _cctx_finish(parts, c_ctx, after):
    VM = pl.BlockSpec(memory_space=pltpu.VMEM)

    def body(p_ref, c_ref, *rest):
        o_ref = rest[-1]
        s = p_ref[0]
        for d in range(1, NDEV):
            s = s + p_ref[d]
        _, vjp = jax.vjp(_silu, c_ref[...])
        o_ref[...] = vjp(s)[0]

    return _call(body, name="cctx_finish", out_shape=_sds((1, D)),
                 in_specs=[VM, VM] + [pl.BlockSpec(memory_space=pl.ANY)] * len(after))(parts, c_ctx, *after)


def _adamw_recv(w, recv, m, v, *, name, own=None):
    rows, cols = w.shape
    bc = 256
    c1 = 1.0 - B1 ** STEP
    c2 = 1.0 - B2 ** STEP
    has_own = own is not None

    def body(w_ref, r_ref, m_ref, v_ref, *rest):
        g_ref, d_ref, nm_ref, nv_ref = rest[-4:]
        me = _position()[3]

        def slot(d):
            return jnp.where(me == d, rest[0][...], r_ref[d]) if has_own else r_ref[d]

        gv = slot(0).astype(F32)
        for d in range(1, NDEV):
            gv = gv + slot(d).astype(F32)
        nm = B1 * m_ref[...] + (1.0 - B1) * gv
        nv = B2 * v_ref[...] + (1.0 - B2) * (gv * gv)
        g_ref[...] = gv
        d_ref[...] = -LR * ((nm / c1) / (jnp.sqrt(nv / c2) + AEPS) + WD * w_ref[...])
        nm_ref[...] = nm
        nv_ref[...] = nv

    blk = pl.BlockSpec((rows, bc), lambda j: (0, j))
    return _call(body, name=name, out_shape=(_sds((rows, cols)),) * 4, grid=(cols // bc,),
                 in_specs=[blk, pl.BlockSpec((NDEV, rows, bc), lambda j: (0, 0, j)), blk, blk] + [blk] * has_own,
                 out_specs=(blk,) * 4, sem=("parallel",), vmem=VMEM_BIG)(w, recv, m, v, *([own] if has_own else []))


P_LAT, P_CTX, P_FNW, P_FFNB, P_CONV, P_FFNW, P_MISC, P_ROWS = 0, 8, 16, 24, 32, 48, 72, 80


def _rows_of(v, nrows):
    flat = v.reshape(-1)
    return jnp.pad(flat, (0, nrows * D - flat.shape[0])).reshape(nrows, D)


def _by_columns(g):
    n, r, c = g.shape
    return jnp.transpose(g, (1, 0, 2)).reshape(r, n * c)


def kernel(x, c, ctx, c_ctx, w_mod, b_mod, w_in, q_norm_w, k_norm_w, conv_qkv_w, a_log, dt_bias, gdn_norm_w, w_pa, w_pd, w_out, w_up, ffn_conv_w, ffn_conv_b, w_down, final_norm_w, loss_target, m_c_ctx, m_w_mod, m_b_mod, m_w_in, m_q_norm_w, m_k_norm_w, m_conv_qkv_w, m_a_log, m_dt_bias, m_gdn_norm_w, m_w_pa, m_w_pd, m_w_out, m_w_up, m_ffn_conv_w, m_ffn_conv_b, m_w_down, m_final_norm_w, v_c_ctx, v_w_mod, v_b_mod, v_w_in, v_q_norm_w, v_k_norm_w, v_conv_qkv_w, v_a_log, v_dt_bias, v_gdn_norm_w, v_w_pa, v_w_pd, v_w_out, v_w_up, v_ffn_conv_w, v_ffn_conv_b, v_w_down, v_final_norm_w):
    _, _, _, me = _position()
    mcols = w_mod.shape[2]

    transposed = ("w_in", "w_up")
    big = {"w_in": w_in[0].T, "w_pa": w_pa[0], "w_pd": w_pd[0], "w_out": w_out[0], "w_up": w_up[0].T, "w_down": w_down[0]}
    names = list(big)
    shards = {n: _cast_bf16(big[n], name="cast_" + n) for n in names}
    w_in_g = _gather_two_level(shards["w_in"], name="gather_w_in")
    c_all, conv_g, ffnw_g = _exchange([c, conv_qkv_w[0], ffn_conv_w[0]], name="gather_small", scatter=False)
    w_in_full = w_in_g.reshape(W_END, D)
    w_in_pad = _pad_columns(w_in_full)

    c9 = jnp.concatenate([c_all.reshape(NDEV, D), jnp.pad(c_ctx[None], ((0, MODROWS - NDEV - 1), (0, 0)))], axis=0)
    b_loc = lax.dynamic_slice(b_mod, (0, me * mcols), (1, mcols))
    mod_all, = _exchange([_mod_fwd(c9, w_mod[0], b_loc)], name="gather_mod", scatter=False)
    mod_lat = lax.dynamic_index_in_dim(mod_all, me, axis=1, keepdims=False).reshape(6, D)
    mod_ctx = mod_all[:, NDEV, :].reshape(6, D)

    small = {"q_norm_w": q_norm_w, "k_norm_w": k_norm_w, "gdn_norm_w": gdn_norm_w, "a_log": a_log, "dt_bias": dt_bias,
             "conv_qkv_w": _by_columns(conv_g), "ffn_conv_w": _by_columns(ffnw_g), "ffn_conv_b": ffn_conv_b,
             "final_norm_w": final_norm_w[None]}
    loss_me, grad_x, (pending_in, own_in), recv, dmod_lat, dmod_ctx, gs = _local_step(
        x[0], ctx[0], loss_target[0], mod_lat, mod_ctx, w_in_pad, shards, small)

    moments = {"w_in": (m_w_in, v_w_in), "w_pa": (m_w_pa, v_w_pa), "w_pd": (m_w_pd, v_w_pd),
               "w_out": (m_w_out, v_w_out), "w_up": (m_w_up, v_w_up), "w_down": (m_w_down, v_w_down)}
    res = {}
    def finish(n, outs):
        return tuple((t.T if n in transposed else t)[None] for t in outs)

    def moment(t, n):
        return t[0].T if n in transposed else t[0]

    for n in recv:
        res[n] = finish(n, _adamw_recv(big[n], recv[n], moment(moments[n][0], n), moment(moments[n][1], n),
                                       name="adamw_" + n))

    misc = jnp.concatenate([gs["q_norm_w"][0], gs["k_norm_w"][0], gs["gdn_norm_w"][0], gs["a_log"], gs["dt_bias"],
                            loss_me[None]])
    pack = jnp.concatenate([_rows_of(dmod_lat, P_CTX - P_LAT), _rows_of(dmod_ctx, P_FNW - P_CTX),
                            _rows_of(gs["final_norm_w"], P_FFNB - P_FNW), _rows_of(gs["ffn_conv_b"], P_CONV - P_FFNB),
                            _rows_of(gs["conv_qkv_w"], P_FFNW - P_CONV), _rows_of(gs["ffn_conv_w"], P_MISC - P_FFNW),
                            _rows_of(misc, P_ROWS - P_MISC)], axis=0)
    pack_all, = _exchange([pack], name="gather_pack", scatter=False)
    tot = _sum_slots(pack_all, name="sum_pack")
    dall = jnp.concatenate([pack_all[:, P_LAT:P_LAT + 6, :].reshape(NDEV, 6 * D),
                            jnp.pad(tot[P_CTX:P_CTX + 6].reshape(1, 6 * D), ((0, MODROWS - NDEV - 1), (0, 0)))], axis=0)
    dmy = lax.dynamic_slice(dall, (0, me * mcols), (MODROWS, mcols))
    g_w_mod, g_b_mod, cpart = _mod_bwd(c9, dmy, dall, w_mod[0])
    cparts, = _exchange([cpart], name="gather_cctx", scatter=False)
    sems_a, land = pending_in[:2], pending_in[3]
    *sems_b, g_in_thru, land, token_b = _scatter_start(pending_in[2], land, (D // 2, D // 2), (cparts,),
                                                       name="scatter_g_in_b_start")
    g_c_ctx = _cctx_finish(cparts, c_ctx[None], (token_b,))[0]

    nconv, nffn = 3 * GH * HD, 2 * DFF
    conv_tot = tot[P_CONV:P_FFNW].reshape(-1)[:3 * nconv].reshape(3, nconv)
    ffnw_tot = tot[P_FFNW:P_MISC].reshape(-1)[:3 * nffn].reshape(3, nffn)
    mrow = tot[P_MISC]
    grads = {
        "c_ctx": g_c_ctx, "w_mod": g_w_mod[None], "b_mod": g_b_mod,
        "q_norm_w": mrow[None, 0:HD], "k_norm_w": mrow[None, HD:2 * HD], "gdn_norm_w": mrow[None, 2 * HD:3 * HD],
        "conv_qkv_w": lax.dynamic_slice(conv_tot, (0, me * (nconv // NDEV)), (3, nconv // NDEV))[None],
        "a_log": mrow[3 * HD:3 * HD + 2 * GH].reshape(1, 2, GH),
        "dt_bias": mrow[3 * HD + 2 * GH:3 * HD + 4 * GH].reshape(1, 2, GH),
        "ffn_conv_w": lax.dynamic_slice(ffnw_tot, (0, me * (nffn // NDEV)), (3, nffn // NDEV))[None],
        "ffn_conv_b": tot[P_FFNB:P_CONV].reshape(-1)[:nffn][None],
        "final_norm_w": tot[P_FNW],
    }
    loss = mrow[3 * HD + 4 * GH]
    given = {"c_ctx": (c_ctx, m_c_ctx, v_c_ctx), "w_mod": (w_mod, m_w_mod, v_w_mod), "b_mod": (b_mod, m_b_mod, v_b_mod),
             "q_norm_w": (q_norm_w, m_q_norm_w, v_q_norm_w), "k_norm_w": (k_norm_w, m_k_norm_w, v_k_norm_w),
             "conv_qkv_w": (conv_qkv_w, m_conv_qkv_w, v_conv_qkv_w), "a_log": (a_log, m_a_log, v_a_log),
             "dt_bias": (dt_bias, m_dt_bias, v_dt_bias), "gdn_norm_w": (gdn_norm_w, m_gdn_norm_w, v_gdn_norm_w),
             "ffn_conv_w": (ffn_conv_w, m_ffn_conv_w, v_ffn_conv_w), "ffn_conv_b": (ffn_conv_b, m_ffn_conv_b, v_ffn_conv_b),
             "final_norm_w": (final_norm_w, m_final_norm_w, v_final_norm_w)}
    res["w_mod"] = (grads["w_mod"],) + _adamw(w_mod, grads["w_mod"], m_w_mod, v_w_mod, name="adamw_w_mod")
    small_names = [n for n in given if n != "w_mod"]
    updates = _adamw_many([(given[n][0], grads[n], given[n][1], given[n][2]) for n in small_names], name="adamw_small")
    for n, upd in zip(small_names, updates):
        res[n] = (grads[n],) + upd

    g_in_thru, land = _scatter_wait(*sems_a, g_in_thru, land, (0, D // 2), [res[n][1] for n in res],
                                    name="scatter_g_in_a_wait")
    _, land = _scatter_wait(*sems_b, g_in_thru, land, (D // 2, D // 2), (), name="scatter_g_in_b_wait")
    res["w_in"] = finish("w_in", _adamw_recv(big["w_in"], land, moment(m_w_in, "w_in"), moment(v_w_in, "w_in"),
                                             name="adamw_w_in", own=own_in))

    order = ["c_ctx", "w_mod", "b_mod", "w_in", "q_norm_w", "k_norm_w", "conv_qkv_w", "a_log", "dt_bias", "gdn_norm_w",
             "w_pa", "w_pd", "w_out", "w_up", "ffn_conv_w", "ffn_conv_b", "w_down", "final_norm_w"]
    return (loss, grad_x[None], *[res[n][0] for n in order], *[res[n][1] for n in order],
            *[res[n][2] for n in order], *[res[n][3] for n in order])
```

```python
import functools
import math

import jax
import jax.numpy as jnp
from jax import lax
from jax.experimental import pallas as pl
from jax.experimental.pallas import tpu as pltpu

F32 = jnp.float32
BF16 = jnp.bfloat16
HI = lax.Precision.HIGHEST
MESH = pl.DeviceIdType.MESH

NDEV = 8
D = 1024
HD = 128
AH, AKV, GRP = 8, 2, 4
GH = 8
CH = 64
DFF = 2816
GRID_W = 64
EPS = 1e-6
ROPE_THETA = 10000.0
LOG2E = math.log2(math.e)
C_KV, C_AQ, C_QKV, C_BL, C_Z, C_GATE, C_END = 0, 512, 1536, 4608, 5120, 6144, 8192
W_QKV, W_AQ, W_Z, W_END = 512, 3616, 4640, 7712


def _pad_columns(w):
    zeros = jnp.zeros((C_Z - C_QKV - (W_AQ - W_QKV), D), w.dtype)
    return jnp.concatenate([w[:W_QKV], w[W_AQ:W_Z], w[W_QKV:W_AQ], zeros, w[W_Z:]], axis=0)


def _unpad_columns(g):
    return jnp.concatenate([g[:C_AQ], g[C_QKV:C_QKV + W_AQ - W_QKV], g[C_AQ:C_QKV], g[C_Z:]], axis=0)
LR, B1, B2, AEPS, WD, STEP = 0.001, 0.9, 0.999, 1e-08, 0.01, 10
VMEM_BIG = 56 * 1024 * 1024
INTRA_FWD_CHUNKS = 36
INTRA_BWD_CHUNKS = 36


def _call(body, *, name, out_shape, grid=None, in_specs=None, out_specs=None, scratch=(), sem=None,
          vmem=None, aliases=None):
    params = {}
    if sem is not None:
        params["dimension_semantics"] = sem
    if vmem is not None:
        params["vmem_limit_bytes"] = vmem
    kw = {}
    if grid is not None:
        kw["grid"] = grid
    if in_specs is not None:
        kw["in_specs"] = in_specs
    if out_specs is not None:
        kw["out_specs"] = out_specs
    if aliases:
        kw["input_output_aliases"] = aliases
    return pl.pallas_call(body, name=name, out_shape=out_shape, scratch_shapes=list(scratch),
                          compiler_params=pltpu.CompilerParams(**params), **kw)


def _call_carrying(body, exch, *, name, out_shape, grid, in_specs, out_specs, scratch=(), vmem=None):
    n, nin, nout, nscr = exch.n, len(in_specs), len(out_shape), len(scratch)
    steps = math.prod(grid)
    mid = (2 * steps) // 3

    def wrapped(*refs):
        ins, cins = refs[:nin], refs[nin:nin + n]
        outs, couts = refs[nin + n:nin + n + nout], refs[nin + n + nout:nin + 2 * n + nout]
        scr, sems = refs[nin + 2 * n + nout:nin + 2 * n + nout + nscr], refs[nin + 2 * n + nout + nscr:]
        ids = [pl.program_id(i) for i in range(len(grid))]
        first = functools.reduce(jnp.logical_and, [i == 0 for i in ids])
        last = functools.reduce(jnp.logical_and, [i == g - 1 for i, g in zip(ids, grid)])

        @pl.when(first)
        def _():
            exch.start(cins, couts, sems)

        if hasattr(exch, "middle"):
            linear = functools.reduce(lambda acc, ig: acc * ig[1] + ig[0], zip(ids, grid), 0)

            @pl.when(linear == mid)
            def _():
                exch.middle(cins, couts, sems)

        body(*ins, *outs, *scr)

        @pl.when(last)
        def _():
            exch.finish(cins, couts, sems)

    params = {"dimension_semantics": ("arbitrary",) * len(grid)}
    if vmem is not None:
        params["vmem_limit_bytes"] = vmem
    fn = pl.pallas_call(wrapped, name=name, out_shape=tuple(out_shape) + exch.out_shape, grid=grid,
                        in_specs=list(in_specs) + [HBM] * n, out_specs=tuple(out_specs) + (HBM,) * n,
                        scratch_shapes=list(scratch) + exch.scratch, compiler_params=pltpu.CompilerParams(**params))

    def run(*args):
        res = fn(*args, *exch.arrs)
        return res[:nout], list(res[nout:])

    return run


def _sds(shape, dtype=F32):
    return jax.ShapeDtypeStruct(tuple(shape), dtype)


def _dot(a, b, ca, cb):
    return lax.dot_general(a.astype(BF16), b.astype(BF16), (((ca,), (cb,)), ((), ())),
                           preferred_element_type=F32)


@jax.custom_vjp
def _nn(a, b):
    return _dot(a, b, 1, 0)


@jax.custom_vjp
def _nt(a, b):
    return _dot(a, b, 1, 1)


@jax.custom_vjp
def _tn(a, b):
    return _dot(a, b, 0, 0)


_nn.defvjp(lambda a, b: (_nn(a, b), (a, b)), lambda r, g: (_nt(g, r[1]), _tn(r[0], g)))
_nt.defvjp(lambda a, b: (_nt(a, b), (a, b)), lambda r, g: (_nn(g, r[1]), _tn(g, r[0])))
_tn.defvjp(lambda a, b: (_tn(a, b), (a, b)), lambda r, g: (_nt(r[1], g), _nn(r[0], g)))


def _hdot(a, b):
    return jnp.dot(a, b, precision=HI, preferred_element_type=F32)


def _mdot(a, b):
    return jnp.dot(a, b, precision=lax.Precision.HIGH, preferred_element_type=F32)


def _maskdot(mask, a, cm):
    hi = a.astype(BF16)
    r = a - hi.astype(F32)
    mid = r.astype(BF16)
    lo = (r - mid.astype(F32)).astype(BF16)
    mb = mask.astype(BF16)
    dims = (((cm,), (0,)), ((), ()))
    return (lax.dot_general(mb, hi, dims, preferred_element_type=F32)
            + lax.dot_general(mb, mid, dims, preferred_element_type=F32)
            + lax.dot_general(mb, lo, dims, preferred_element_type=F32))


@jax.custom_vjp
def _mask_nn(mask, a):
    return _maskdot(mask, a, 1)


_mask_nn.defvjp(lambda mask, a: (_maskdot(mask, a, 1), mask),
                lambda mask, g: (jnp.zeros_like(mask), _maskdot(mask, g, 0)))


@jax.custom_vjp
def _saved_inverse(lmat, x):
    return x


def _saved_inverse_bwd(x, g):
    t = lax.dot_general(x, g, (((0,), (0,)), ((), ())), precision=lax.Precision.HIGH, preferred_element_type=F32)
    dl = lax.dot_general(t, x, (((1,), (1,)), ((), ())), precision=lax.Precision.HIGH, preferred_element_type=F32)
    return -dl, jnp.zeros_like(x)


_saved_inverse.defvjp(lambda lmat, x: (x, x), _saved_inverse_bwd)


def _row_ids(shape):
    return lax.broadcasted_iota(jnp.int32, shape, 0)


def _shift_rows(x, down, bounds):
    n = x.shape[0]
    rows = _row_ids(x.shape)
    y = pltpu.roll(x, 1 if down else n - 1, 0)
    edge = functools.reduce(jnp.logical_or, [rows == (s if down else e - 1) for s, e in bounds])
    return jnp.where(edge, 0.0, y)


def _make_shift(bounds):
    @jax.custom_vjp
    def down(x):
        return _shift_rows(x, True, bounds)

    @jax.custom_vjp
    def up(x):
        return _shift_rows(x, False, bounds)

    down.defvjp(lambda x: (down(x), None), lambda _, g: (up(g),))
    up.defvjp(lambda x: (up(x), None), lambda _, g: (down(g),))
    return down, up


@jax.custom_vjp
def _swap32(x):
    lane = lax.broadcasted_iota(jnp.int32, x.shape, x.ndim - 1)
    return jnp.where((lane % 64) < 32, pltpu.roll(x, HD - 32, x.ndim - 1), pltpu.roll(x, 32, x.ndim - 1))


_swap32.defvjp(lambda x: (_swap32(x), None), lambda _, g: (_swap32(g),))


def _rms(x):
    return x * lax.rsqrt(jnp.mean(x * x, axis=-1, keepdims=True) + EPS)


def _silu(x):
    return x * jax.nn.sigmoid(x)


def _mm(a, b, *, name, M, N, K, ta=False, tb=False, out_dtype=F32, bm=None, bn=None, bk=None,
        a_off=(0, 0), b_off=(0, 0), after=()):
    bm, bn, bk = bm or M, bn or N, bk or K
    assert M % bm == 0 and N % bn == 0 and K % bk == 0, (name, M, N, K, bm, bn, bk)
    nk = K // bk
    ca, cb = (0 if ta else 1), (1 if tb else 0)
    na = len(after)

    def body(a_ref, b_ref, *rest):
        o_ref, acc = rest[na], rest[na + 1:]
        r = _dot(a_ref[...], b_ref[...], ca, cb)
        if nk == 1:
            o_ref[...] = r.astype(out_dtype)
        else:
            acc_ref, = acc
            k = pl.program_id(2)

            @pl.when(k == 0)
            def _():
                acc_ref[...] = r

            @pl.when(k > 0)
            def _():
                acc_ref[...] += r

            @pl.when(k == nk - 1)
            def _():
                o_ref[...] = acc_ref[...].astype(out_dtype)

    def blk(off, bshape):
        assert off[0] % bshape[0] == 0 and off[1] % bshape[1] == 0, (name, off, bshape)
        return off[0] // bshape[0], off[1] // bshape[1]

    if ta:
        ao = blk(a_off, (bk, bm))
        a_spec = pl.BlockSpec((bk, bm), lambda i, j, k: (k + ao[0], i + ao[1]))
    else:
        ao = blk(a_off, (bm, bk))
        a_spec = pl.BlockSpec((bm, bk), lambda i, j, k: (i + ao[0], k + ao[1]))
    if tb:
        bo = blk(b_off, (bn, bk))
        b_spec = pl.BlockSpec((bn, bk), lambda i, j, k: (j + bo[0], k + bo[1]))
    else:
        bo = blk(b_off, (bk, bn))
        b_spec = pl.BlockSpec((bk, bn), lambda i, j, k: (k + bo[0], j + bo[1]))
    return _call(body, name=name, out_shape=_sds((M, N), out_dtype), grid=(M // bm, N // bn, nk),
                 in_specs=[a_spec, b_spec] + [pl.BlockSpec(memory_space=pl.ANY)] * na,
                 out_specs=pl.BlockSpec((bm, bn), lambda i, j, k: (i, j)),
                 scratch=[pltpu.VMEM((bm, bn), F32)] if nk > 1 else [],
                 sem=("parallel", "parallel", "arbitrary"), vmem=VMEM_BIG)(a, b, *after)


def _normmod_fn(x, sh, sc):
    return _rms(x) * (1.0 + sc) + sh


def _normmod_fwd(x, mod, i_sh, i_sc, *, name, br=256):
    R = x.shape[0]

    def body(x_ref, mod_ref, o_ref):
        o_ref[...] = _normmod_fn(x_ref[...], mod_ref[i_sh:i_sh + 1, :], mod_ref[i_sc:i_sc + 1, :]).astype(BF16)

    return _call(body, name=name, out_shape=_sds((R, D), BF16), grid=(R // br,),
                 in_specs=[pl.BlockSpec((br, D), lambda i: (i, 0)), pl.BlockSpec((6, D), lambda i: (0, 0))],
                 out_specs=pl.BlockSpec((br, D), lambda i: (i, 0)), sem=("parallel",))(x, mod)


def _normmod_bwd(x, mod, i_sh, i_sc, dh, dh_off, res, *, name, br=256):
    R = x.shape[0]
    ob = dh_off // br
    has_res = res is not None

    def body(x_ref, mod_ref, dh_ref, *rest):
        if has_res:
            res_ref, dx_ref, dsh_ref, dsc_ref = rest
        else:
            dx_ref, dsh_ref, dsc_ref = rest
        sh, sc = mod_ref[i_sh:i_sh + 1, :], mod_ref[i_sc:i_sc + 1, :]
        _, vjp = jax.vjp(_normmod_fn, x_ref[...], sh, sc)
        dx, dsh, dsc = vjp(dh_ref[...])
        dx_ref[...] = dx + res_ref[...] if has_res else dx

        @pl.when(pl.program_id(0) == 0)
        def _():
            dsh_ref[...] = jnp.zeros_like(dsh_ref)
            dsc_ref[...] = jnp.zeros_like(dsc_ref)

        dsh_ref[...] += dsh
        dsc_ref[...] += dsc

    row = pl.BlockSpec((br, D), lambda i: (i, 0))
    vec = pl.BlockSpec((1, D), lambda i: (0, 0))
    ins = [row, pl.BlockSpec((6, D), lambda i: (0, 0)), pl.BlockSpec((br, D), lambda i: (i + ob, 0))]
    args = [x, mod, dh]
    if has_res:
        ins.append(row)
        args.append(res)
    return _call(body, name=name, out_shape=(_sds((R, D)), _sds((1, D)), _sds((1, D))), grid=(R // br,),
                 in_specs=ins, out_specs=(row, vec, vec), sem=("arbitrary",))(*args)


def _rope(x, cos, sin):
    return x * cos + _swap32(x) * sin


def _aprep_fn(qs, ks, cos, sin, qw, kw):
    return ([_rope(_rms(q) * qw, cos, sin) for q in qs], [_rope(_rms(k) * kw, cos, sin) for k in ks])


def _aprep_fwd(proj, cos, sin, qw, kw, *, br=256):
    T = proj.shape[0]

    def body(x_ref, cos_ref, sin_ref, qw_ref, kw_ref, q_ref, k_ref, v_ref):
        qs = [x_ref[:, C_AQ + h * HD:C_AQ + (h + 1) * HD] for h in range(AH)]
        ks = [x_ref[:, h * HD:(h + 1) * HD] for h in range(AKV)]
        qo, ko = _aprep_fn(qs, ks, cos_ref[...], sin_ref[...], qw_ref[...], kw_ref[...])
        for h in range(AH):
            q_ref[h] = qo[h].astype(BF16)
        for h in range(AKV):
            k_ref[h] = ko[h].astype(BF16)
            v_ref[h] = x_ref[:, (AKV + h) * HD:(AKV + h + 1) * HD].astype(BF16)

    tab = pl.BlockSpec((br, HD), lambda i: (i, 0))
    vec = pl.BlockSpec((1, HD), lambda i: (0, 0))
    return _call(body, name="aprep_fwd",
                 out_shape=(_sds((AH, T, HD), BF16), _sds((AKV, T, HD), BF16), _sds((AKV, T, HD), BF16)),
                 grid=(T // br,),
                 in_specs=[pl.BlockSpec((br, C_QKV), lambda i: (i, 0)), tab, tab, vec, vec],
                 out_specs=(pl.BlockSpec((AH, br, HD), lambda i: (0, i, 0)),
                            pl.BlockSpec((AKV, br, HD), lambda i: (0, i, 0)),
                            pl.BlockSpec((AKV, br, HD), lambda i: (0, i, 0))),
                 sem=("parallel",))(proj, cos, sin, qw, kw)


def _aprep_bwd(proj, cos, sin, qw, kw, dq, dk, dv, dproj, L, *, br=256):
    T = proj.shape[0]
    lb = L // br

    def body(x_ref, cos_ref, sin_ref, qw_ref, kw_ref, dq_ref, dk_ref, dv_ref, _, dx_ref, dqw_ref, dkw_ref):
        i = pl.program_id(0)
        qs = [x_ref[:, C_AQ + h * HD:C_AQ + (h + 1) * HD] for h in range(AH)]
        ks = [x_ref[:, h * HD:(h + 1) * HD] for h in range(AKV)]
        _, vjp = jax.vjp(_aprep_fn, qs, ks, cos_ref[...], sin_ref[...], qw_ref[...], kw_ref[...])
        is_lat = i >= lb
        dqs = [jnp.where(is_lat, dq_ref[h], 0.0) for h in range(AH)]
        dks = [dk_ref[h] for h in range(AKV)]
        gq, gk, _, _, gqw, gkw = vjp((dqs, dks))
        for h in range(AH):
            dx_ref[:, C_AQ + h * HD:C_AQ + (h + 1) * HD] = gq[h].astype(BF16)
        for h in range(AKV):
            dx_ref[:, h * HD:(h + 1) * HD] = gk[h].astype(BF16)
            dx_ref[:, (AKV + h) * HD:(AKV + h + 1) * HD] = dv_ref[h].astype(BF16)

        @pl.when(i == 0)
        def _():
            dqw_ref[...] = jnp.zeros_like(dqw_ref)
            dkw_ref[...] = jnp.zeros_like(dkw_ref)

        dqw_ref[...] += gqw
        dkw_ref[...] += gkw

    tab = pl.BlockSpec((br, HD), lambda i: (i, 0))
    vec = pl.BlockSpec((1, HD), lambda i: (0, 0))
    kvb = pl.BlockSpec((AKV, br, HD), lambda i: (0, i, 0))
    blk = pl.BlockSpec((br, C_QKV), lambda i: (i, 0))
    return _call(body, name="aprep_bwd", out_shape=(_sds(dproj.shape, BF16), _sds((1, HD)), _sds((1, HD))),
                 grid=(T // br,),
                 in_specs=[blk, tab, tab, vec, vec,
                           pl.BlockSpec((AH, br, HD), lambda i: (0, jnp.maximum(i - lb, 0), 0)), kvb, kvb, ANYSPEC],
                 out_specs=(blk, vec, vec), aliases={8: 0},
                 sem=("arbitrary",))(proj, cos, sin, qw, kw, dq, dk, dv, dproj)


def _attn_grad(q, k, v, o, lse2, do):
    scale = HD ** -0.5
    p = jnp.exp2(_dot(q, k, 1, 1) * (scale * LOG2E) - lse2)
    dp = _dot(do, v, 1, 1)
    ds = p * (dp - jnp.sum(do * o, axis=-1, keepdims=True)) * scale
    return _dot(ds, k, 1, 0), _dot(ds, q, 0, 0), _dot(p, do, 0, 0)


ATTN_KEYS = 256


def _attn_fwd(q, k, v, L, exch, *, bq=128):
    T = q.shape[1]
    N = T - L
    lb = L // bq
    assert T % ATTN_KEYS == 0
    scale = HD ** -0.5
    heads = range(GRP)

    def body(q_ref, k_ref, v_ref, o_ref, o32_ref, lse_ref):
        qs = [q_ref[g] for g in heads]
        m = [jnp.full((bq, 1), -jnp.inf, F32) for _ in heads]
        l = [jnp.zeros((bq, 1), F32) for _ in heads]
        acc = [jnp.zeros((bq, HD), F32) for _ in heads]
        for c in range(T // ATTN_KEYS):
            kc, vc = k_ref[c * ATTN_KEYS:(c + 1) * ATTN_KEYS, :], v_ref[c * ATTN_KEYS:(c + 1) * ATTN_KEYS, :]
            s = [_dot(qs[g], kc, 1, 1) * (scale * LOG2E) for g in heads]
            m_new = [jnp.maximum(m[g], jnp.max(s[g], axis=-1, keepdims=True)) for g in heads]
            alpha = [jnp.exp2(m[g] - m_new[g]) for g in heads]
            p = [jnp.exp2(s[g] - m_new[g]) for g in heads]
            l = [l[g] * alpha[g] + jnp.sum(p[g], axis=-1, keepdims=True) for g in heads]
            acc = [acc[g] * alpha[g] + _dot(p[g], vc, 1, 0) for g in heads]
            m = m_new
        for g in heads:
            o = acc[g] / l[g]
            o_ref[:, g * HD:(g + 1) * HD] = o.astype(BF16)
            o32_ref[:, g * HD:(g + 1) * HD] = o
            lse_ref[g] = jnp.broadcast_to(m[g] + jnp.log2(l[g]), (bq, HD))

    kvb = pl.BlockSpec((None, T, HD), lambda g, i: (g, 0, 0))
    ob = pl.BlockSpec((bq, GRP * HD), lambda g, i: (i, g))
    return _call_carrying(
        body, exch, name="attn_fwd",
        out_shape=(_sds((N, AH * HD), BF16), _sds((N, AH * HD)), _sds((AH, N, HD))), grid=(AKV, N // bq),
        in_specs=[pl.BlockSpec((GRP, bq, HD), lambda g, i: (g, i + lb, 0)), kvb, kvb],
        out_specs=(ob, ob, pl.BlockSpec((GRP, bq, HD), lambda g, i: (g, i, 0))), vmem=VMEM_BIG)(q, k, v)


def _attn_bwd(q, k, v, o32, lse, do, L, *, bq=128):
    T = q.shape[1]
    N = T - L
    lb = L // bq

    def body(q_ref, k_ref, v_ref, o_ref, lse_ref, do_ref, dq_ref, dk_ref, dv_ref):
        rows = lambda r: jnp.concatenate([r[:, g * HD:(g + 1) * HD] for g in range(GRP)], axis=0)
        lse = jnp.max(lse_ref[...].reshape(GRP * bq, HD), axis=-1, keepdims=True)
        dq, dk, dv = _attn_grad(q_ref[...].reshape(GRP * bq, HD), k_ref[...], v_ref[...], rows(o_ref), lse, rows(do_ref))
        dq_ref[...] = dq.reshape(GRP, bq, HD)

        @pl.when(pl.program_id(1) == 0)
        def _():
            dk_ref[...] = jnp.zeros_like(dk_ref)
            dv_ref[...] = jnp.zeros_like(dv_ref)

        dk_ref[...] += dk
        dv_ref[...] += dv

    kvb = pl.BlockSpec((None, T, HD), lambda g, i: (g, 0, 0))
    qb = pl.BlockSpec((GRP, bq, HD), lambda g, i: (g, i + lb, 0))
    hb = pl.BlockSpec((GRP, bq, HD), lambda g, i: (g, i, 0))
    ob = pl.BlockSpec((bq, GRP * HD), lambda g, i: (i, g))
    return _call(body, name="attn_bwd",
                 out_shape=(_sds((AH, N, HD)), _sds((AKV, T, HD)), _sds((AKV, T, HD))), grid=(AKV, N // bq),
                 in_specs=[qb, kvb, kvb, ob, hb, ob], out_specs=(hb, kvb, kvb),
                 sem=("parallel", "arbitrary"), vmem=VMEM_BIG)(q, k, v, o32, lse, do)


def _gprep_fn(kind, shifts, x, w):
    down, up = shifts
    y = down(x) * w[0:1, :] + x * w[1:2, :] + up(x) * w[2:3, :]
    a = _silu(y)
    if kind == 2:
        return a
    a = a * lax.rsqrt(jnp.sum(a * a, axis=-1, keepdims=True) + EPS)
    return a * (HD ** -0.5) if kind == 0 else a


def _gprep_fwd(proj, conv_w, kind, bounds):
    T = proj.shape[0]
    shifts = _make_shift(bounds)
    cb = C_QKV // HD + kind * GH

    def body(x_ref, w_ref, o_ref):
        o_ref[...] = _gprep_fn(kind, shifts, x_ref[...], w_ref[...])

    return _call(body, name=f"gprep_fwd{kind}", out_shape=_sds((GH, T, HD)), grid=(GH,),
                 in_specs=[pl.BlockSpec((T, HD), lambda h: (0, cb + h)),
                           pl.BlockSpec((3, HD), lambda h: (0, kind * GH + h))],
                 out_specs=pl.BlockSpec((None, T, HD), lambda h: (h, 0, 0)), sem=("parallel",))(proj, conv_w)


def _gprep_bwd(proj, conv_w, kind, bounds, dy, dproj):
    T = proj.shape[0]
    shifts = _make_shift(bounds)
    cb = C_QKV // HD + kind * GH

    def body(x_ref, w_ref, dy_ref, _, dx_ref, dw_ref):
        _, vjp = jax.vjp(functools.partial(_gprep_fn, kind, shifts), x_ref[...], w_ref[...])
        dx, dw = vjp(dy_ref[0] + dy_ref[1])
        dx_ref[...] = dx.astype(BF16)
        dw_ref[...] = dw

    return _call(body, name=f"gprep_bwd{kind}", out_shape=(_sds(dproj.shape, BF16), _sds((3, GH * HD))), grid=(GH,),
                 in_specs=[pl.BlockSpec((T, HD), lambda h: (0, cb + h)),
                           pl.BlockSpec((3, HD), lambda h: (0, kind * GH + h)),
                           pl.BlockSpec((2, None, T, HD), lambda h: (0, h, 0, 0)), ANYSPEC],
                 out_specs=(pl.BlockSpec((T, HD), lambda h: (0, cb + h)), pl.BlockSpec((3, HD), lambda h: (0, h))),
                 aliases={3: 0}, sem=("parallel",))(proj, conv_w, dy, dproj)


def _bl_fn(x, alog, dtb):
    lane = lax.broadcasted_iota(jnp.int32, x.shape, 1)
    beta = jax.nn.sigmoid(x)
    z = x + dtb
    sp = jnp.maximum(z, 0.0) + jnp.log1p(jnp.exp(-jnp.abs(z)))
    la = -jnp.exp(alog) * sp
    return jnp.where(lane < 2 * GH, beta, jnp.where(lane < 4 * GH, la, 0.0))


def _bl_fwd(proj, alog, dtb, *, br=256):
    T = proj.shape[0]

    def body(x_ref, a_ref, d_ref, o_ref):
        o_ref[...] = _bl_fn(x_ref[...], a_ref[...], d_ref[...])

    vec = pl.BlockSpec((1, HD), lambda i: (0, 0))
    return _call(body, name="bl_fwd", out_shape=_sds((T, HD)), grid=(T // br,),
                 in_specs=[pl.BlockSpec((br, HD), lambda i: (i, C_BL // HD)), vec, vec],
                 out_specs=pl.BlockSpec((br, HD), lambda i: (i, 0)), sem=("parallel",))(proj, alog, dtb)


def _bl_bwd(proj, alog, dtb, dbl, dproj, *, br=256):
    T = proj.shape[0]
    wide = C_Z - C_BL

    def body(x_ref, a_ref, d_ref, g_ref, _, dx_ref, da_ref, dd_ref):
        g = g_ref[0, 0]
        for d in range(2):
            for h in range(GH):
                if d or h:
                    g = g + g_ref[d, h]
        _, vjp = jax.vjp(_bl_fn, x_ref[...], a_ref[...], d_ref[...])
        dx, da, dd = vjp(g)
        dx_ref[:, :HD] = dx.astype(BF16)
        dx_ref[:, HD:] = jnp.zeros((br, wide - HD), BF16)

        @pl.when(pl.program_id(0) == 0)
        def _():
            da_ref[...] = jnp.zeros_like(da_ref)
            dd_ref[...] = jnp.zeros_like(dd_ref)

        da_ref[...] += da
        dd_ref[...] += dd

    vec = pl.BlockSpec((1, HD), lambda i: (0, 0))
    return _call(body, name="bl_bwd", out_shape=(_sds(dproj.shape, BF16), _sds((1, HD)), _sds((1, HD))), grid=(T // br,),
                 in_specs=[pl.BlockSpec((br, HD), lambda i: (i, C_BL // HD)), vec, vec,
                           pl.BlockSpec((2, GH, br, HD), lambda i: (0, 0, i, 0)), ANYSPEC],
                 out_specs=(pl.BlockSpec((br, wide), lambda i: (i, C_BL // wide)), vec, vec), aliases={4: 0},
                 sem=("arbitrary",))(proj, alog, dtb, dbl, dproj)


def _chunk_masks(d):
    ii = lax.broadcasted_iota(jnp.int32, (CH, CH), 0)
    jj = lax.broadcasted_iota(jnp.int32, (CH, CH), 1)
    eye = (ii == jj).astype(F32)
    before = jnp.where(d == 0, (jj < ii).astype(F32), (jj > ii).astype(F32))
    return before, before + eye, eye


def _intra_fn(masks, sel_b, sel_l, qs, ks, vs, bls, xs=None):
    before, ateq, eye = masks
    ones = jnp.ones((CH, CH), F32)
    inc = ateq > 0.0
    each = lambda f, *ls: [f(*t) for t in zip(*ls)]
    beta = each(lambda bl: jnp.sum(bl * sel_b, axis=-1, keepdims=True), bls)
    la = each(lambda bl: jnp.sum(bl * sel_l, axis=-1, keepdims=True), bls)
    gam = each(lambda a: _mask_nn(ateq, jnp.broadcast_to(a, (CH, HD))), la)
    gi = each(lambda a: _mask_nn(ateq, jnp.broadcast_to(a, (CH, CH))), la)
    gj = each(lambda g: _mask_nn(ones, eye * g), gi)
    kk = each(lambda k: _nt(k, k), ks)
    qk = each(_nt, qs, ks)
    dec = each(lambda a, b: jnp.where(inc, jnp.exp(jnp.where(inc, a - b, 0.0)), 0.0), gi, gj)
    lmat = each(lambda b, d, m: before * (b * d * m), beta, dec, kk)
    if xs is None:
        x = each(lambda m: eye - m, lmat)
        p2 = each(lambda m: _mdot(m, m), lmat)
        for it in range(4):
            y = each(lambda a, b: _mdot(jnp.concatenate([a, b], axis=0), b), x, p2)
            x = each(lambda a, t: a + t[:CH], x, y)
            p2 = each(lambda t: t[CH:], y)
        x = each(lambda a, b: a + _mdot(a, b), x, p2)
    else:
        x = each(_saved_inverse, lmat, xs)
    eg = each(jnp.exp, gam)
    u = each(lambda a, b, v: _mdot(a, b * v), x, beta, vs)
    w = each(lambda a, b, e, k: _mdot(a, (b * e) * k), x, beta, eg, ks)
    tot = each(lambda a: jnp.sum(a, axis=0, keepdims=True), la)
    kd = each(lambda k, t, g: k * jnp.exp(t - g), ks, tot, gam)
    gl = each(lambda t: jnp.broadcast_to(jnp.exp(t), (1, HD)), tot)
    qd = each(lambda q, e: q * e, qs, eg)
    p = each(lambda d, m: d * m, dec, qk)
    return (u, w, kd, qd, p, gl, x) if xs is None else (u, w, kd, qd, p, gl)


def _dir_head_sel(d, h):
    lane = lax.broadcasted_iota(jnp.int32, (1, HD), 1)
    return (lane == d * GH + h).astype(F32), (lane == 2 * GH + d * GH + h).astype(F32)


def _intra_specs(T, G):
    nc = T // CH
    assert nc % G == 0
    qkv = pl.BlockSpec((None, G * CH, HD), lambda d, h, c: (h, c, 0))
    bl = pl.BlockSpec((G * CH, HD), lambda d, h, c: (c, 0))
    big = pl.BlockSpec((None, None, G * CH, HD), lambda d, h, c: (d, h, c, 0))
    pm = pl.BlockSpec((None, None, G * CH, CH), lambda d, h, c: (d, h, c, 0))
    gl = pl.BlockSpec((None, None, G, 1, HD), lambda d, h, c: (d, h, c, 0, 0))
    shapes = (_sds((2, GH, T, HD)),) + (_sds((2, GH, T, HD), BF16),) * 3 + (
        _sds((2, GH, T, CH), BF16), _sds((2, GH, nc, 1, HD)), _sds((2, GH, T, CH)))
    return nc, qkv, bl, big, pm, gl, shapes


def _chunks_per_step(T, most):
    nc = T // CH
    return max(g for g in range(1, most + 1) if nc % g == 0)


def _intra_fwd(q, k, v, bl, exch):
    T = q.shape[1]
    G = _chunks_per_step(T, INTRA_FWD_CHUNKS)
    nc, qkv_s, bl_s, big, pm, gl_s, shapes = _intra_specs(T, G)

    def body(q_ref, k_ref, v_ref, bl_ref, u_ref, w_ref, kd_ref, qd_ref, p_ref, gl_ref, x_ref):
        d, h = pl.program_id(0), pl.program_id(1)
        sb, sl = _dir_head_sel(d, h)
        rows = [slice(g * CH, (g + 1) * CH) for g in range(G)]
        outs = _intra_fn(_chunk_masks(d), sb, sl, *[[r[s, :] for s in rows] for r in (q_ref, k_ref, v_ref, bl_ref)])
        for g in range(G):
            for r, o in zip((u_ref, w_ref, kd_ref, qd_ref, p_ref, x_ref), outs[:5] + outs[6:]):
                r[rows[g], :] = o[g].astype(r.dtype)
            gl_ref[g] = outs[5][g]

    return _call_carrying(body, exch, name="gdn_intra_fwd", out_shape=shapes, grid=(2, GH, nc // G),
                          in_specs=[qkv_s, qkv_s, qkv_s, bl_s], out_specs=(big, big, big, big, pm, gl_s, pm))(q, k, v, bl)


def _intra_bwd(q, k, v, bl, xinv, cts, exch):
    T = q.shape[1]
    G = _chunks_per_step(T, INTRA_BWD_CHUNKS)
    nc, qkv_s, bl_s, big, pm, gl_s, _ = _intra_specs(T, G)

    def body(q_ref, k_ref, v_ref, bl_ref, x_ref, du, dw, dkd, dqd, dp, dgl, dq_ref, dk_ref, dv_ref, dbl_ref):
        d, h = pl.program_id(0), pl.program_id(1)
        sb, sl = _dir_head_sel(d, h)
        rows = [slice(g * CH, (g + 1) * CH) for g in range(G)]
        fn = functools.partial(_intra_fn, _chunk_masks(d), sb, sl, xs=[x_ref[s, :] for s in rows])
        _, vjp = jax.vjp(fn, *[[r[s, :] for s in rows] for r in (q_ref, k_ref, v_ref, bl_ref)])
        cts = tuple([r[s, :] for s in rows] for r in (du, dw, dkd, dqd, dp)) + ([dgl[g] for g in range(G)],)
        grads = vjp(cts)
        for g in range(G):
            for r, o in zip((dq_ref, dk_ref, dv_ref, dbl_ref), grads):
                r[rows[g], :] = o[g]

    return _call_carrying(body, exch, name="gdn_intra_bwd", out_shape=(_sds((2, GH, T, HD)),) * 4,
                          grid=(2, GH, nc // G), in_specs=[qkv_s, qkv_s, qkv_s, bl_s, pm, big, big, big, big, pm, gl_s],
                          out_specs=(big,) * 4)(q, k, v, bl, xinv, *cts)


def _scan_fn(s, u, w, kd, qd, p, gl):
    each = lambda f, *ls: [f(*t) for t in zip(*ls)]
    ws = each(_nn, w, s)
    delta = each(lambda a, b: a - b, u, ws)
    kdd = each(_tn, kd, delta)
    s_new = each(lambda g, a, b: g * a + b, gl, s, kdd)
    qs = each(_nn, qd, s)
    pd = each(_nn, p, delta)
    return each(lambda a, b: a + b, qs, pd), s_new


SCAN_BLOCK = 4


def _scan_visit(t, d, nb, ncb):
    rev = jnp.where(t < ncb, ncb - 1 - t, nb - 1 - (t - ncb))
    return jnp.where(d == 0, t, rev)


def _scan_specs(T, L, back):
    tb = SCAN_BLOCK * CH
    assert T % tb == 0 and L % tb == 0
    nb, ncb = T // tb, L // tb

    def at(d, t):
        return _scan_visit(nb - 1 - t if back else t, d, nb, ncb)

    big = pl.BlockSpec((None, GH, tb, HD), lambda d, t: (d, 0, at(d, t), 0))
    pm = pl.BlockSpec((None, GH, tb, CH), lambda d, t: (d, 0, at(d, t), 0))
    gl = pl.BlockSpec((None, GH, SCAN_BLOCK, 1, HD), lambda d, t: (d, 0, at(d, t), 0, 0))
    st = pl.BlockSpec((None, GH, SCAN_BLOCK, HD, HD), lambda d, t: (d, 0, at(d, t), 0, 0))
    do = pl.BlockSpec((GH, tb, HD), lambda d, t: (0, at(d, t), 0))
    return nb, big, pm, gl, st, do


def _scan_fwd(u, w, kd, qd, p, gl, L):
    T = u.shape[2]
    nb, big, pm, gl_s, st, _ = _scan_specs(T, L, False)
    heads = range(GH)

    def body(u_ref, w_ref, kd_ref, qd_ref, p_ref, gl_ref, o_ref, st_ref, s_scr):
        d = pl.program_id(0)

        @pl.when(pl.program_id(1) == 0)
        def _():
            s_scr[...] = jnp.zeros_like(s_scr)

        s = [s_scr[h] for h in heads]
        for i in range(SCAN_BLOCK):
            c = jnp.where(d == 0, i, SCAN_BLOCK - 1 - i)
            rows = pl.ds(pl.multiple_of(c * CH, CH), CH)
            for h in heads:
                st_ref[h, c] = s[h]
            o, s = _scan_fn(s, *[[r[h, rows, :].astype(F32) for h in heads] for r in (u_ref, w_ref, kd_ref, qd_ref, p_ref)],
                            [gl_ref[h, c] for h in heads])
            for h in heads:
                o_ref[h, rows, :] = o[h]
        for h in heads:
            s_scr[h] = s[h]

    return _call(body, name="gdn_scan_fwd", out_shape=(_sds((2, GH, T, HD)), _sds((2, GH, T // CH, HD, HD))),
                 grid=(2, nb), in_specs=[big, big, big, big, pm, gl_s], out_specs=(big, st),
                 scratch=[pltpu.VMEM((GH, HD, HD), F32)], sem=("parallel", "arbitrary"))(u, w, kd, qd, p, gl)


def _scan_bwd(u, w, kd, qd, p, gl, states, do, L, exch):
    T = u.shape[2]
    nb, big, pm, gl_s, st, do_s = _scan_specs(T, L, True)
    heads = range(GH)

    def body(u_ref, w_ref, kd_ref, qd_ref, p_ref, gl_ref, st_ref, do_ref,
             du_ref, dw_ref, dkd_ref, dqd_ref, dp_ref, dgl_ref, ds_scr):
        d = pl.program_id(0)

        @pl.when(pl.program_id(1) == 0)
        def _():
            ds_scr[...] = jnp.zeros_like(ds_scr)

        ds = [ds_scr[h] for h in heads]
        for i in range(SCAN_BLOCK):
            c = jnp.where(d == 0, SCAN_BLOCK - 1 - i, i)
            rows = pl.ds(pl.multiple_of(c * CH, CH), CH)
            _, vjp = jax.vjp(_scan_fn, [st_ref[h, c] for h in heads],
                             *[[r[h, rows, :].astype(F32) for h in heads] for r in (u_ref, w_ref, kd_ref, qd_ref, p_ref)],
                             [gl_ref[h, c] for h in heads])
            ds, gu, gw, gkd, gqd, gp, ggl = vjp(([do_ref[h, rows, :] for h in heads], ds))
            for h in heads:
                du_ref[h, rows, :] = gu[h]
                dw_ref[h, rows, :] = gw[h]
                dkd_ref[h, rows, :] = gkd[h]
                dqd_ref[h, rows, :] = gqd[h]
                dp_ref[h, rows, :] = gp[h]
                dgl_ref[h, c] = ggl[h]
        for h in heads:
            ds_scr[h] = ds[h]

    return _call_carrying(
        body, exch, name="gdn_scan_bwd",
        out_shape=(_sds((2, GH, T, HD)),) * 4 + (_sds((2, GH, T, CH)), _sds((2, GH, T // CH, 1, HD))),
        grid=(2, nb), in_specs=[big, big, big, big, pm, gl_s, st, do_s], out_specs=(big, big, big, big, pm, gl_s),
        scratch=[pltpu.VMEM((GH, HD, HD), F32)])(u, w, kd, qd, p, gl, states, do)


def _gout_fn(o0, o1, z, gw):
    return _rms(o0 + o1) * gw * _silu(z)


def _gout_fwd(o, proj, gw, L):
    T = o.shape[2]
    N = T - L
    ob = pl.BlockSpec((2, None, T, HD), lambda h: (0, h, 0, 0))

    def body(o_ref, z_ref, gw_ref, y_ref):
        y_ref[...] = _gout_fn(o_ref[0, L:, :], o_ref[1, L:, :], z_ref[L:, :], gw_ref[...]).astype(BF16)

    return _call(body, name="gout_fwd", out_shape=_sds((N, GH * HD), BF16), grid=(GH,),
                 in_specs=[ob, pl.BlockSpec((T, HD), lambda h: (0, C_Z // HD + h)), pl.BlockSpec((1, HD), lambda h: (0, 0))],
                 out_specs=pl.BlockSpec((N, HD), lambda h: (0, h)), sem=("parallel",))(o, proj, gw)


def _gout_bwd(o, proj, gw, dy, dproj, L):
    T = o.shape[2]
    N = T - L
    ob = pl.BlockSpec((2, None, T, HD), lambda h: (0, h, 0, 0))

    def body(o_ref, z_ref, gw_ref, dy_ref, _, do_ref, dz_ref, dgw_ref):
        _, vjp = jax.vjp(_gout_fn, o_ref[0, L:, :], o_ref[1, L:, :], z_ref[L:, :], gw_ref[...])
        g0, _, gz, ggw = vjp(dy_ref[...])
        do_ref[:L, :] = jnp.zeros((L, HD), F32)
        do_ref[L:, :] = g0
        dz_ref[:L, :] = jnp.zeros((L, HD), BF16)
        dz_ref[L:, :] = gz.astype(BF16)

        @pl.when(pl.program_id(0) == 0)
        def _():
            dgw_ref[...] = jnp.zeros_like(dgw_ref)

        dgw_ref[...] += ggw

    zb = pl.BlockSpec((T, HD), lambda h: (0, C_Z // HD + h))
    return _call(body, name="gout_bwd", out_shape=(_sds((GH, T, HD)), _sds(dproj.shape, BF16), _sds((1, HD))),
                 grid=(GH,),
                 in_specs=[ob, zb, pl.BlockSpec((1, HD), lambda h: (0, 0)), pl.BlockSpec((N, HD), lambda h: (0, h)), ANYSPEC],
                 out_specs=(pl.BlockSpec((None, T, HD), lambda h: (h, 0, 0)), zb, pl.BlockSpec((1, HD), lambda h: (0, 0))),
                 aliases={4: 1}, sem=("arbitrary",))(o, proj, gw, dy, dproj)


def _merge_fn(pa, pd, ga, gd):
    return jax.nn.sigmoid(ga) * pa + jax.nn.sigmoid(gd) * pd


def _merge_fwd(pa, pd, proj, L, *, br=256):
    N = pa.shape[0]
    lb = L // br
    row = pl.BlockSpec((br, D), lambda i: (i, 0))

    def body(pa_ref, pd_ref, ga_ref, gd_ref, y_ref):
        y_ref[...] = _merge_fn(pa_ref[...], pd_ref[...], ga_ref[...], gd_ref[...]).astype(BF16)

    return _call(body, name="merge_fwd", out_shape=_sds((N, D), BF16), grid=(N // br,),
                 in_specs=[row, row, pl.BlockSpec((br, D), lambda i: (i + lb, C_GATE // D)),
                           pl.BlockSpec((br, D), lambda i: (i + lb, C_GATE // D + 1))],
                 out_specs=row, sem=("parallel",))(pa, pd, proj, proj)


def _merge_bwd(pa, pd, proj, dy, L, *, br=256):
    N = pa.shape[0]
    T = N + L
    lb = L // br
    lrow = pl.BlockSpec((br, D), lambda i: (jnp.maximum(i - lb, 0), 0))

    def body(pa_ref, pd_ref, ga_ref, gd_ref, dy_ref, dpa_ref, dpd_ref, dg_ref):
        lat = pl.program_id(0) >= lb
        _, vjp = jax.vjp(_merge_fn, pa_ref[...], pd_ref[...], ga_ref[...], gd_ref[...])
        gpa, gpd, gga, ggd = vjp(dy_ref[...])
        dpa_ref[...] = gpa.astype(BF16)
        dpd_ref[...] = gpd.astype(BF16)
        dg_ref[:, :D] = jnp.where(lat, gga, 0.0).astype(BF16)
        dg_ref[:, D:] = jnp.where(lat, ggd, 0.0).astype(BF16)

    return _call(body, name="merge_bwd", out_shape=(_sds((N, D), BF16), _sds((N, D), BF16), _sds((T, C_END), BF16)),
                 grid=(T // br,),
                 in_specs=[lrow, lrow, pl.BlockSpec((br, D), lambda i: (i, C_GATE // D)),
                           pl.BlockSpec((br, D), lambda i: (i, C_GATE // D + 1)), lrow],
                 out_specs=(lrow, lrow, pl.BlockSpec((br, 2 * D), lambda i: (i, C_GATE // (2 * D)))),
                 sem=("arbitrary",))(pa, pd, proj, proj, dy)


def _resid_fwd(x, m, mod, i_g, *, name, br=256):
    R = x.shape[0]
    row = pl.BlockSpec((br, D), lambda i: (i, 0))

    def body(x_ref, m_ref, mod_ref, o_ref):
        o_ref[...] = x_ref[...] + mod_ref[i_g:i_g + 1, :] * m_ref[...]

    return _call(body, name=name, out_shape=_sds((R, D)), grid=(R // br,),
                 in_specs=[row, row, pl.BlockSpec((6, D), lambda i: (0, 0))], out_specs=row,
                 sem=("parallel",))(x, m, mod)


def _resid_bwd(dx, m, mod, i_g, *, name, br=256):
    R = dx.shape[0]
    row = pl.BlockSpec((br, D), lambda i: (i, 0))
    vec = pl.BlockSpec((1, D), lambda i: (0, 0))

    def body(dx_ref, m_ref, mod_ref, dm_ref, dg_ref):
        dxv = dx_ref[...]
        dm_ref[...] = (dxv * mod_ref[i_g:i_g + 1, :]).astype(BF16)

        @pl.when(pl.program_id(0) == 0)
        def _():
            dg_ref[...] = jnp.zeros_like(dg_ref)

        dg_ref[...] += jnp.sum(dxv * m_ref[...], axis=0, keepdims=True)

    return _call(body, name=name, out_shape=(_sds((R, D), BF16), _sds((1, D))), grid=(R // br,),
                 in_specs=[row, row, pl.BlockSpec((6, D), lambda i: (0, 0))], out_specs=(row, vec),
                 sem=("arbitrary",))(dx, m, mod)


def _ffn_fn(shifts, ug, uv, wg, wv, bg, bv):
    down, up = shifts

    def conv(x, w, b):
        return down(x) * w[0:1, :] + x * w[1:2, :] + up(x) * w[2:3, :] + b

    return _silu(conv(ug, wg, bg)) * conv(uv, wv, bv)


def _ffn_fwd(up, cw, cb, *, bw=256):
    N = up.shape[0]
    shifts = _make_shift(((0, N),))
    nb = DFF // bw

    def body(ug, uv, wg, wv, bg, bv, a_ref):
        a_ref[...] = _ffn_fn(shifts, ug[...], uv[...], wg[...], wv[...], bg[...], bv[...]).astype(BF16)

    def col(rows, off):
        return pl.BlockSpec((rows, bw), lambda j: (0, j + off))

    return _call(body, name="ffn_fwd", out_shape=_sds((N, DFF), BF16), grid=(nb,),
                 in_specs=[col(N, 0), col(N, nb), col(3, 0), col(3, nb), col(1, 0), col(1, nb)],
                 out_specs=col(N, 0), sem=("parallel",), vmem=VMEM_BIG)(up, up, cw, cw, cb, cb)


def _ffn_bwd(up, cw, cb, da, *, bw=256):
    N = up.shape[0]
    shifts = _make_shift(((0, N),))
    nb = DFF // bw

    def body(ug, uv, wg, wv, bg, bv, da_ref, dug, duv, dwg, dwv, dbg, dbv):
        _, vjp = jax.vjp(functools.partial(_ffn_fn, shifts), ug[...], uv[...], wg[...], wv[...], bg[...], bv[...])
        g = vjp(da_ref[...])
        dug[...] = g[0].astype(BF16)
        duv[...] = g[1].astype(BF16)
        dwg[...], dwv[...], dbg[...], dbv[...] = g[2], g[3], g[4], g[5]

    def col(rows, off):
        return pl.BlockSpec((rows, bw), lambda j: (0, j + off))

    half = (_sds((N, DFF), BF16), _sds((N, DFF), BF16), _sds((3, DFF)), _sds((3, DFF)), _sds((1, DFF)), _sds((1, DFF)))
    dug, duv, dwg, dwv, dbg, dbv = _call(
        body, name="ffn_bwd", out_shape=half, grid=(nb,),
        in_specs=[col(N, 0), col(N, nb), col(3, 0), col(3, nb), col(1, 0), col(1, nb), col(N, 0)],
        out_specs=(col(N, 0), col(N, 0), col(3, 0), col(3, 0), col(1, 0), col(1, 0)),
        sem=("parallel",), vmem=VMEM_BIG)(up, up, cw, cw, cb, cb, da)
    return (jnp.concatenate([dug, duv], axis=1), jnp.concatenate([dwg, dwv], axis=1),
            jnp.concatenate([dbg, dbv], axis=1))


def _head_fn(x1, dn, g2, fw, tgt):
    y = _rms(x1 + g2 * dn) * fw
    err = y - tgt
    return 0.5 * jnp.sum(jnp.mean(err * err, axis=-1))


def _head(x1, dn, mod, fw, tgt, *, br=256):
    N = x1.shape[0]
    row = pl.BlockSpec((br, D), lambda i: (i, 0))
    vec = pl.BlockSpec((1, D), lambda i: (0, 0))
    one = pl.BlockSpec((1, HD), lambda i: (0, 0))

    def body(x1_ref, dn_ref, mod_ref, fw_ref, tgt_ref, loss_ref, dx_ref, ddn_ref, dg_ref, dfw_ref):
        loss, (gx, gdn, gg, gfw) = jax.value_and_grad(_head_fn, argnums=(0, 1, 2, 3))(
            x1_ref[...], dn_ref[...], mod_ref[5:6, :], fw_ref[...], tgt_ref[...])
        dx_ref[...] = gx
        ddn_ref[...] = gdn.astype(BF16)

        @pl.when(pl.program_id(0) == 0)
        def _():
            loss_ref[...] = jnp.zeros_like(loss_ref)
            dg_ref[...] = jnp.zeros_like(dg_ref)
            dfw_ref[...] = jnp.zeros_like(dfw_ref)

        loss_ref[...] += jnp.broadcast_to(loss, (1, HD))
        dg_ref[...] += gg
        dfw_ref[...] += gfw

    return _call(body, name="head", out_shape=(_sds((1, HD)), _sds((N, D)), _sds((N, D), BF16), _sds((1, D)), _sds((1, D))),
                 grid=(N // br,), in_specs=[row, row, pl.BlockSpec((6, D), lambda i: (0, 0)), vec, row],
                 out_specs=(one, row, row, vec, vec), sem=("arbitrary",))(x1, dn, mod, fw, tgt)


def _adamw(w, g, m, v, *, name):
    shape = w.shape
    cols = shape[-1]
    rows = max(1, math.prod(shape[:-1]))
    w2, g2, m2, v2 = (t.reshape(rows, cols) for t in (w, g, m, v))
    br = 256 if rows % 256 == 0 else rows
    c1 = 1.0 - B1 ** STEP
    c2 = 1.0 - B2 ** STEP

    def body(w_ref, g_ref, m_ref, v_ref, d_ref, nm_ref, nv_ref):
        gv = g_ref[...]
        nm = B1 * m_ref[...] + (1.0 - B1) * gv
        nv = B2 * v_ref[...] + (1.0 - B2) * (gv * gv)
        d_ref[...] = -LR * ((nm / c1) / (jnp.sqrt(nv / c2) + AEPS) + WD * w_ref[...])
        nm_ref[...] = nm
        nv_ref[...] = nv

    blk = pl.BlockSpec((br, cols), lambda i: (i, 0))
    outs = _call(body, name=name, out_shape=(_sds((rows, cols)),) * 3, grid=(rows // br,),
                 in_specs=[blk] * 4, out_specs=(blk,) * 3, sem=("parallel",))(w2, g2, m2, v2)
    return tuple(t.reshape(shape) for t in outs)


def _adamw_many(items, *, name):
    k = len(items)
    shapes = [w.shape for w, _, _, _ in items]
    flat = [t.reshape(max(1, math.prod(t.shape[:-1])), t.shape[-1]) for it in items for t in it]
    c1 = 1.0 - B1 ** STEP
    c2 = 1.0 - B2 ** STEP

    def body(*refs):
        ins, outs = refs[:4 * k], refs[4 * k:]
        for i in range(k):
            w_ref, g_ref, m_ref, v_ref = ins[4 * i:4 * i + 4]
            gv = g_ref[...]
            nm = B1 * m_ref[...] + (1.0 - B1) * gv
            nv = B2 * v_ref[...] + (1.0 - B2) * (gv * gv)
            outs[3 * i][...] = -LR * ((nm / c1) / (jnp.sqrt(nv / c2) + AEPS) + WD * w_ref[...])
            outs[3 * i + 1][...] = nm
            outs[3 * i + 2][...] = nv

    res = _call(body, name=name, out_shape=tuple(_sds(flat[4 * i].shape) for i in range(k) for _ in range(3)))(*flat)
    return [tuple(res[3 * i + j].reshape(shapes[i]) for j in range(3)) for i in range(k)]


def _rope_tables(N, L):
    t = jnp.arange(N)
    pos = jnp.stack([(t // GRID_W).astype(F32), (t % GRID_W).astype(F32)], axis=1)
    inv = ROPE_THETA ** (-jnp.arange(0, HD // 2, 2, dtype=F32) / (HD // 2))
    ang = pos[:, :, None] * inv[None, None, :]
    cos = jnp.broadcast_to(jnp.cos(ang)[:, :, None, :], (N, 2, 2, HD // 4)).reshape(N, HD)
    sin = jnp.broadcast_to(jnp.sin(ang)[:, :, None, :], (N, 2, 2, HD // 4))
    sin = (sin * jnp.array([-1.0, 1.0], F32)[None, None, :, None]).reshape(N, HD)
    cos = jnp.concatenate([jnp.ones((L, HD), F32), cos], axis=0)
    sin = jnp.concatenate([jnp.zeros((L, HD), F32), sin], axis=0)
    return cos, sin


def _pad_lanes(v, off=0):
    return jnp.zeros((1, HD), F32).at[0, off:off + v.shape[0]].set(v)


def _local_step(x, ctx, tgt, mod_lat, mod_ctx, w_in, shards, small):
    N, L = x.shape[0], ctx.shape[0]
    T = N + L
    bounds = ((0, L), (L, T))
    qw, kw, gw = small["q_norm_w"], small["k_norm_w"], small["gdn_norm_w"]
    conv_w, ffn_w, ffn_b, fnw = small["conv_qkv_w"], small["ffn_conv_w"], small["ffn_conv_b"], small["final_norm_w"]
    alog = _pad_lanes(small["a_log"].reshape(-1), 2 * GH)
    dtb = _pad_lanes(small["dt_bias"].reshape(-1), 2 * GH)
    cos, sin = _rope_tables(N, L)
    bt = T
    bnl = 256 if N % 1024 else 1024

    hc = _normmod_fwd(ctx, mod_ctx, 0, 1, name="normmod_ctx")
    hx = _normmod_fwd(x, mod_lat, 0, 1, name="normmod_x")
    h1 = jnp.concatenate([hc, hx], axis=0)
    proj = _mm(h1, w_in, name="mm_in", M=T, N=C_END, K=D, tb=True, bm=bt, bn=1024)
    aq, ak, av = _aprep_fwd(proj, cos, sin, qw, kw)
    (attn, attn32, lse), (pa_g, pd_g, out_g) = _attn_fwd(
        aq, ak, av, L, _GatherTwoLevel([shards[n] for n in ("w_pa", "w_pd", "w_out")]))
    gq = _gprep_fwd(proj, conv_w, 0, bounds)
    gk = _gprep_fwd(proj, conv_w, 1, bounds)
    gv = _gprep_fwd(proj, conv_w, 2, bounds)
    bl = _bl_fwd(proj, alog, dtb)
    intra, (up_g, down_g) = _intra_fwd(gq, gk, gv, bl, _GatherTwoLevel([shards["w_up"], shards["w_down"]]))
    w_up, w_down = up_g.reshape(2 * DFF, D), down_g.reshape(DFF, D)
    w_pa, w_pd, w_out = pa_g.reshape(D, D), pd_g.reshape(D, D), out_g.reshape(D, D)
    xinv, intra = intra[6], intra[:6]
    o, states = _scan_fwd(*intra, L)
    gdn = _gout_fwd(o, proj, gw, L)
    pa = _mm(attn, w_pa, name="mm_pa", M=N, N=D, K=D, bm=bnl)
    pd = _mm(gdn, w_pd, name="mm_pd", M=N, N=D, K=D, bm=bnl)
    y = _merge_fwd(pa, pd, proj, L)
    m = _mm(y, w_out, name="mm_out", M=N, N=D, K=D, bm=bnl)
    x1 = _resid_fwd(x, m, mod_lat, 2, name="resid1")
    h2 = _normmod_fwd(x1, mod_lat, 3, 4, name="normmod_x1")
    up = _mm(h2, w_up, name="mm_up", M=N, N=2 * DFF, K=D, tb=True, bm=bnl, bn=2 * DFF // 4)
    a = _ffn_fwd(up, ffn_w, ffn_b)
    dn = _mm(a, w_down, name="mm_down", M=N, N=D, K=DFF, bm=bnl)
    loss, dx2, ddn, dg2, dfnw = _head(x1, dn, mod_lat, fnw, tgt)

    da = _mm(ddn, w_down, name="mm_down_dx", M=N, N=DFF, K=D, tb=True, bm=bnl, bn=DFF // 2)
    g_down = _mm(a, ddn, name="mm_down_dw", M=DFF, N=D, K=N, ta=True, bm=DFF // 2, out_dtype=BF16)
    dup, d_ffn_w, d_ffn_b = _ffn_bwd(up, ffn_w, ffn_b, da)
    dh2 = _mm(dup, w_up, name="mm_up_dx", M=N, N=D, K=2 * DFF, bm=bnl, bk=2 * DFF // 4)
    g_up = _mm(dup, h2, name="mm_up_dw", M=2 * DFF, N=D, K=N, ta=True, bm=2 * DFF // 4, out_dtype=BF16)
    dx1, dsh2, dsc2 = _normmod_bwd(x1, mod_lat, 3, 4, dh2, 0, dx2, name="normmod_x1_bwd")
    dm, dg1 = _resid_bwd(dx1, m, mod_lat, 2, name="resid1_bwd")
    dy = _mm(dm, w_out, name="mm_out_dx", M=N, N=D, K=D, tb=True, bm=bnl)
    g_out = _mm(y, dm, name="mm_out_dw", M=D, N=D, K=N, ta=True, out_dtype=BF16)
    dpa, dpd, dproj = _merge_bwd(pa, pd, proj, dy, L)
    dattn = _mm(dpa, w_pa, name="mm_pa_dx", M=N, N=D, K=D, tb=True, bm=bnl)
    g_pa = _mm(attn, dpa, name="mm_pa_dw", M=D, N=D, K=N, ta=True, out_dtype=BF16)
    dgdn = _mm(dpd, w_pd, name="mm_pd_dx", M=N, N=D, K=D, tb=True, bm=bnl)
    g_pd = _mm(gdn, dpd, name="mm_pd_dw", M=D, N=D, K=N, ta=True, out_dtype=BF16)
    do, dproj, dgw = _gout_bwd(o, proj, gw, dgdn, dproj, L)
    cts, recv_a = _scan_bwd(*intra, states, do, L, _Exchange(
        [g_out.reshape(NDEV, D // NDEV, D), g_down.reshape(NDEV, DFF // NDEV, D)], True))
    (dgq, dgk, dgv, dbl), recv_b = _intra_bwd(gq, gk, gv, bl, xinv, cts, _Exchange(
        [g_pa.reshape(NDEV, D // NDEV, D), g_pd.reshape(NDEV, D // NDEV, D), g_up.reshape(NDEV, 2 * DFF // NDEV, D)], True))
    recv = dict(zip(("w_out", "w_down", "w_pa", "w_pd", "w_up"), recv_a + recv_b))
    dproj, dwq = _gprep_bwd(proj, conv_w, 0, bounds, dgq, dproj)
    dproj, dwk = _gprep_bwd(proj, conv_w, 1, bounds, dgk, dproj)
    dproj, dwv = _gprep_bwd(proj, conv_w, 2, bounds, dgv, dproj)
    dproj, dalog, ddtb = _bl_bwd(proj, alog, dtb, dbl, dproj)
    daq_h, dak_h, dav_h = _attn_bwd(aq, ak, av, attn32, lse, dattn, L)
    dproj, dqw, dkw = _aprep_bwd(proj, cos, sin, qw, kw, daq_h, dak_h, dav_h, dproj, L)
    g_in = _mm(dproj, h1, name="mm_in_dw", M=C_END, N=D, K=T, ta=True, bm=1024, out_dtype=BF16)
    g_in = _unpad_columns(g_in).reshape(NDEV, W_END // NDEV, D)
    own_in = lax.dynamic_index_in_dim(g_in, _position()[3], axis=0, keepdims=False)
    *pending, token = _scatter_start(g_in, None, (0, D // 2), (), name="scatter_g_in_a_start")
    dh1 = _mm(dproj, w_in, name="mm_in_dx", M=T, N=D, K=C_END, bm=bt, bk=1024, after=(token,))
    grad_x, dsh1, dsc1 = _normmod_bwd(x, mod_lat, 0, 1, dh1, L, dx1, name="normmod_x_bwd")
    _, dcsh1, dcsc1 = _normmod_bwd(ctx, mod_ctx, 0, 1, dh1, 0, None, name="normmod_ctx_bwd")

    z1 = jnp.zeros((1, D), F32)
    dmod_lat = jnp.concatenate([dsh1, dsc1, dg1, dsh2, dsc2, dg2], axis=0)
    dmod_ctx = jnp.concatenate([dcsh1, dcsc1, z1, z1, z1, z1], axis=0)
    gsmall = {
        "q_norm_w": dqw, "k_norm_w": dkw, "gdn_norm_w": dgw,
        "conv_qkv_w": jnp.concatenate([dwq, dwk, dwv], axis=1),
        "a_log": dalog[0, 2 * GH:4 * GH], "dt_bias": ddtb[0, 2 * GH:4 * GH],
        "ffn_conv_w": d_ffn_w, "ffn_conv_b": d_ffn_b, "final_norm_w": dfnw,
    }
    return loss[0, 0], grad_x, (pending, own_in), recv, dmod_lat, dmod_ctx, gsmall


HBM = pl.BlockSpec(memory_space=pltpu.HBM)
ANYSPEC = pl.BlockSpec(memory_space=pl.ANY)


def _position():
    x, y, c = lax.axis_index("x"), lax.axis_index("y"), lax.axis_index("c")
    return x, y, c, 4 * x + 2 * y + c


def _peer(x, y, c, k):
    px = 1 - x if k & 4 else x
    py = 1 - y if k & 2 else y
    pc = 1 - c if k & 1 else c
    return (px, py, pc), 4 * px + 2 * py + pc


def _exchange(arrs, *, name, scatter):
    exch = _Exchange(arrs, scatter)
    n = exch.n

    def body(*refs):
        ins, outs, sems = refs[:n], refs[n:2 * n], refs[2 * n:]
        exch.start(ins, outs, sems)
        exch.finish(ins, outs, sems)

    outs = pl.pallas_call(body, name=name, out_shape=exch.out_shape, in_specs=[HBM] * n, out_specs=(HBM,) * n,
                          scratch_shapes=exch.scratch,
                          compiler_params=pltpu.CompilerParams(has_side_effects=True))(*arrs)
    return list(outs)


class _Exchange:
    def __init__(self, arrs, scatter):
        self.arrs, self.scatter, self.n = list(arrs), scatter, len(arrs)
        self.out_shape = tuple(_sds(a.shape if scatter else (NDEV,) + a.shape, a.dtype) for a in arrs)
        self.scratch = [pltpu.SemaphoreType.DMA((self.n, NDEV - 1)), pltpu.SemaphoreType.DMA((self.n, NDEV - 1)),
                        pltpu.SemaphoreType.DMA((self.n,))]

    def _copies(self, ins, outs, sems):
        send, recv, loc = sems
        x, y, c, me = _position()
        local = [pltpu.make_async_copy(ins[a].at[me] if self.scatter else ins[a], outs[a].at[me], loc.at[a])
                 for a in range(self.n)]
        remote = []
        for k in range(1, NDEV):
            peer, pid = _peer(x, y, c, k)
            for a in range(self.n):
                src = ins[a].at[pid] if self.scatter else ins[a]
                remote.append(pltpu.make_async_remote_copy(
                    src_ref=src, dst_ref=outs[a].at[me], send_sem=send.at[a, k - 1], recv_sem=recv.at[a, k - 1],
                    device_id=peer, device_id_type=MESH))
        return local, remote

    def start(self, ins, outs, sems):
        local, remote = self._copies(ins, outs, sems)
        for cp in local + remote:
            cp.start()

    def finish(self, ins, outs, sems):
        local, remote = self._copies(ins, outs, sems)
        for cp in remote:
            cp.wait()
        for cp in local:
            cp.wait()


class _GatherTwoLevel:
    scatter = False

    def __init__(self, arrs):
        self.arrs, self.n = list(arrs), len(arrs)
        self.out_shape = tuple(_sds((NDEV,) + a.shape, a.dtype) for a in arrs)
        self.scratch = [pltpu.SemaphoreType.DMA((self.n, NDEV - 1)), pltpu.SemaphoreType.DMA((self.n, NDEV - 1)),
                        pltpu.SemaphoreType.DMA((self.n,))]

    def _parts(self, ins, outs, sems):
        send, recv, loc = sems
        x, y, c, _ = _position()
        me, sibling = (x, y, c), (x, y, 1 - c)
        chips = [(1 - x, y), (x, 1 - y), (1 - x, 1 - y)]
        parts = []
        for a in range(self.n):
            slot = lambda px, py, pc, a=a: outs[a].at[4 * px + 2 * py + pc]

            def copy(k, owner, to, src=None, a=a, slot=slot):
                return pltpu.make_async_remote_copy(
                    src_ref=slot(*owner) if src is None else src, dst_ref=slot(*owner), send_sem=send.at[a, k],
                    recv_sem=recv.at[a, k], device_id=to, device_id_type=MESH)

            parts.append(dict(
                mine=pltpu.make_async_copy(ins[a], slot(*me), loc.at[a]),
                first=[copy(0, me, sibling, src=ins[a])] + [copy(1 + j, me, (*ch, c), src=ins[a]) for j, ch in enumerate(chips)],
                arrive=[copy(1 + j, (*ch, c), me) for j, ch in enumerate(chips)],
                passed=[copy(4 + j, (*ch, c), sibling) for j, ch in enumerate(chips)],
                rest=[copy(0, sibling, me)] + [copy(4 + j, (*ch, 1 - c), me) for j, ch in enumerate(chips)]))
        return parts

    def start(self, ins, outs, sems):
        for p in self._parts(ins, outs, sems):
            p["mine"].start()
            for cp in p["first"]:
                cp.start()

    def middle(self, ins, outs, sems):
        for p in self._parts(ins, outs, sems):
            for got, fwd in zip(p["arrive"], p["passed"]):
                got.wait_recv()
                fwd.start()

    def finish(self, ins, outs, sems):
        for p in self._parts(ins, outs, sems):
            for cp in p["rest"]:
                cp.wait_recv()
            for cp in p["first"] + p["passed"]:
                cp.wait_send()
            p["mine"].wait()


def _gather_two_level(block, *, name):
    def body(x_ref, out_ref, send_sems, recv_sems, local_sem):
        x, y, c, _ = _position()
        me, sibling = (x, y, c), (x, y, 1 - c)
        chips = [(1 - x, y), (x, 1 - y), (1 - x, 1 - y)]

        def slot(px, py, pc):
            return out_ref.at[4 * px + 2 * py + pc]

        def copy(k, owner, to, src=None):
            return pltpu.make_async_remote_copy(
                src_ref=slot(*owner) if src is None else src, dst_ref=slot(*owner), send_sem=send_sems.at[k],
                recv_sem=recv_sems.at[k], device_id=to, device_id_type=MESH)

        mine = pltpu.make_async_copy(x_ref, slot(*me), local_sem)
        mine.start()
        first = [copy(0, me, sibling, src=x_ref)]
        first += [copy(1 + j, me, (*chip, c), src=x_ref) for j, chip in enumerate(chips)]
        for cp in first:
            cp.start()
        passed = [copy(4 + j, (*chip, c), sibling) for j, chip in enumerate(chips)]
        for j, chip in enumerate(chips):
            copy(1 + j, (*chip, c), me).wait_recv()
            passed[j].start()
        copy(0, sibling, me).wait_recv()
        for j, chip in enumerate(chips):
            copy(4 + j, (*chip, 1 - c), me).wait_recv()
        for cp in first + passed:
            cp.wait_send()
        mine.wait()

    return pl.pallas_call(
        body, name=name, out_shape=_sds((NDEV,) + block.shape, block.dtype), in_specs=[HBM], out_specs=HBM,
        scratch_shapes=[pltpu.SemaphoreType.DMA((NDEV - 1,)), pltpu.SemaphoreType.DMA((NDEV - 1,)),
                        pltpu.SemaphoreType.DMA],
        compiler_params=pltpu.CompilerParams(has_side_effects=True))(block)


SEM = pl.BlockSpec(memory_space=pltpu.SEMAPHORE)


def _scatter_copies(src_ref, land_ref, send_sems, recv_sems, cols):
    x, y, c, me = _position()
    span = (slice(None), pl.ds(*cols))
    copies = []
    for k in range(1, NDEV):
        peer, pid = _peer(x, y, c, k)
        copies.append(pltpu.make_async_remote_copy(
            src_ref=src_ref.at[pid].at[span], dst_ref=land_ref.at[me].at[span], send_sem=send_sems.at[k - 1],
            recv_sem=recv_sems.at[k - 1], device_id=peer, device_id_type=MESH))
    return copies


SPLIT_EFFECT = pltpu.SideEffectType.DATAFLOW_SIDE_EFFECTING


def _scatter_start(parts, land, cols, after, *, name):
    na = len(after)
    if land is None:
        land = lax.empty(parts.shape, parts.dtype)

    def body(src_ref, land_ref, *rest):
        send_sems, recv_sems, _, _, token = rest[na:]
        for cp in _scatter_copies(src_ref, land_ref, send_sems, recv_sems, cols):
            cp.start()
        token[...] = jnp.zeros_like(token)

    return pl.pallas_call(
        body, name=name,
        out_shape=(pltpu.SemaphoreType.DMA((NDEV - 1,)), pltpu.SemaphoreType.DMA((NDEV - 1,)),
                   pltpu.HBM(parts.shape, parts.dtype), pltpu.HBM(parts.shape, parts.dtype), _sds((8, HD))),
        in_specs=(HBM, HBM) + (pl.BlockSpec(memory_space=pl.ANY),) * na,
        out_specs=(SEM, SEM, HBM, HBM, pl.BlockSpec(memory_space=pltpu.VMEM)),
        input_output_aliases={0: 2, 1: 3}, compiler_params=pltpu.CompilerParams(has_side_effects=SPLIT_EFFECT),
    )(pltpu.with_memory_space_constraint(parts, pltpu.HBM), pltpu.with_memory_space_constraint(land, pltpu.HBM), *after)


def _scatter_wait(send_sems, recv_sems, src_thru, land_thru, cols, after, *, name):
    na = len(after)

    def body(src_ref, land_ref, send_sems, recv_sems, *rest):
        for cp in _scatter_copies(src_ref, land_ref, send_sems, recv_sems, cols):
            cp.wait_send()
            cp.wait_recv()

    return pl.pallas_call(
        body, name=name,
        out_shape=(pltpu.HBM(src_thru.shape, src_thru.dtype), pltpu.HBM(land_thru.shape, land_thru.dtype)),
        in_specs=(HBM, HBM, SEM, SEM) + (pl.BlockSpec(memory_space=pl.ANY),) * na, out_specs=(HBM, HBM),
        input_output_aliases={0: 0, 1: 1}, compiler_params=pltpu.CompilerParams(has_side_effects=SPLIT_EFFECT),
    )(src_thru, land_thru, send_sems, recv_sems, *after)


def _cast_bf16(w, *, name):
    rows, cols = w.shape
    br = 128 if rows % 128 == 0 else rows

    def body(w_ref, o_ref):
        o_ref[...] = w_ref[...].astype(BF16)

    blk = pl.BlockSpec((br, cols), lambda i: (i, 0))
    return _call(body, name=name, out_shape=_sds((rows, cols), BF16), grid=(rows // br,), in_specs=[blk],
                 out_specs=blk, sem=("parallel",))(w)


def _sum_slots(a, *, name):
    _, R, C = a.shape

    def body(a_ref, o_ref):
        s = a_ref[0]
        for d in range(1, NDEV):
            s = s + a_ref[d]
        o_ref[...] = s

    return _call(body, name=name, out_shape=_sds((R, C)))(a)


MODROWS = 16


def _mod_fwd(c9, w, b):
    cols = w.shape[1]

    def body(c_ref, w_ref, b_ref, o_ref):
        o_ref[...] = _nn(_silu(c_ref[...]), w_ref[...]) + b_ref[...]

    return _call(body, name="mod_fwd", out_shape=_sds((MODROWS, cols)))(c9, w, b)


def _mod_bwd(c9, dmy, dall, w):
    cols = w.shape[1]

    def body(c_ref, dmy_ref, dall_ref, w_ref, gw_ref, gb_ref, cp_ref):
        sc = _silu(c_ref[...])
        rows = lax.broadcasted_iota(jnp.int32, (MODROWS, 1), 0)
        d = dmy_ref[...]
        d_ctx = jnp.where(rows == NDEV, d, 0.0)
        sc_ctx = jnp.where(rows == NDEV, sc, 0.0)
        outer = lax.dot_general(sc_ctx, d_ctx, (((0,), (0,)), ((), ())), precision=HI, preferred_element_type=F32)
        gw_ref[...] = _tn(jnp.where(rows < NDEV, sc, 0.0), jnp.where(rows < NDEV, d, 0.0)) + outer
        gb_ref[...] = jnp.sum(dall_ref[...], axis=0, keepdims=True)
        cp_ref[...] = jnp.sum(_nt(d_ctx, w_ref[...]), axis=0, keepdims=True)

    return _call(body, name="mod_bwd", out_shape=(_sds((D, cols)), _sds((1, 6 * D)), _sds((1, D))),
                 vmem=VMEM_BIG)(c9, dmy, dall, w)


def _cctx_finish(parts, c_ctx, after):
    VM = pl.BlockSpec(memory_space=pltpu.VMEM)

    def body(p_ref, c_ref, *rest):
        o_ref = rest[-1]
        s = p_ref[0]
        for d in range(1, NDEV):
            s = s + p_ref[d]
        _, vjp = jax.vjp(_silu, c_ref[...])
        o_ref[...] = vjp(s)[0]

    return _call(body, name="cctx_finish", out_shape=_sds((1, D)),
                 in_specs=[VM, VM] + [pl.BlockSpec(memory_space=pl.ANY)] * len(after))(parts, c_ctx, *after)


def _adamw_recv(w, recv, m, v, *, name, own=None):
    rows, cols = w.shape
    bc = 256
    c1 = 1.0 - B1 ** STEP
    c2 = 1.0 - B2 ** STEP
    has_own = own is not None

    def body(w_ref, r_ref, m_ref, v_ref, *rest):
        g_ref, d_ref, nm_ref, nv_ref = rest[-4:]
        me = _position()[3]

        def slot(d):
            return jnp.where(me == d, rest[0][...], r_ref[d]) if has_own else r_ref[d]

        gv = slot(0).astype(F32)
        for d in range(1, NDEV):
            gv = gv + slot(d).astype(F32)
        nm = B1 * m_ref[...] + (1.0 - B1) * gv
        nv = B2 * v_ref[...] + (1.0 - B2) * (gv * gv)
        g_ref[...] = gv
        d_ref[...] = -LR * ((nm / c1) / (jnp.sqrt(nv / c2) + AEPS) + WD * w_ref[...])
        nm_ref[...] = nm
        nv_ref[...] = nv

    blk = pl.BlockSpec((rows, bc), lambda j: (0, j))
    return _call(body, name=name, out_shape=(_sds((rows, cols)),) * 4, grid=(cols // bc,),
                 in_specs=[blk, pl.BlockSpec((NDEV, rows, bc), lambda j: (0, 0, j)), blk, blk] + [blk] * has_own,
                 out_specs=(blk,) * 4, sem=("parallel",), vmem=VMEM_BIG)(w, recv, m, v, *([own] if has_own else []))


P_LAT, P_CTX, P_FNW, P_FFNB, P_CONV, P_FFNW, P_MISC, P_ROWS = 0, 8, 16, 24, 32, 48, 72, 80


def _rows_of(v, nrows):
    flat = v.reshape(-1)
    return jnp.pad(flat, (0, nrows * D - flat.shape[0])).reshape(nrows, D)


def _by_columns(g):
    n, r, c = g.shape
    return jnp.transpose(g, (1, 0, 2)).reshape(r, n * c)


def kernel(x, c, ctx, c_ctx, w_mod, b_mod, w_in, q_norm_w, k_norm_w, conv_qkv_w, a_log, dt_bias, gdn_norm_w, w_pa, w_pd, w_out, w_up, ffn_conv_w, ffn_conv_b, w_down, final_norm_w, loss_target, m_c_ctx, m_w_mod, m_b_mod, m_w_in, m_q_norm_w, m_k_norm_w, m_conv_qkv_w, m_a_log, m_dt_bias, m_gdn_norm_w, m_w_pa, m_w_pd, m_w_out, m_w_up, m_ffn_conv_w, m_ffn_conv_b, m_w_down, m_final_norm_w, v_c_ctx, v_w_mod, v_b_mod, v_w_in, v_q_norm_w, v_k_norm_w, v_conv_qkv_w, v_a_log, v_dt_bias, v_gdn_norm_w, v_w_pa, v_w_pd, v_w_out, v_w_up, v_ffn_conv_w, v_ffn_conv_b, v_w_down, v_final_norm_w):
    _, _, _, me = _position()
    mcols = w_mod.shape[2]

    transposed = ("w_in", "w_up")
    big = {"w_in": w_in[0].T, "w_pa": w_pa[0], "w_pd": w_pd[0], "w_out": w_out[0], "w_up": w_up[0].T, "w_down": w_down[0]}
    names = list(big)
    shards = {n: _cast_bf16(big[n], name="cast_" + n) for n in names}
    w_in_g = _gather_two_level(shards["w_in"], name="gather_w_in")
    c_all, conv_g, ffnw_g = _exchange([c, conv_qkv_w[0], ffn_conv_w[0]], name="gather_small", scatter=False)
    w_in_full = w_in_g.reshape(W_END, D)
    w_in_pad = _pad_columns(w_in_full)

    c9 = jnp.concatenate([c_all.reshape(NDEV, D), jnp.pad(c_ctx[None], ((0, MODROWS - NDEV - 1), (0, 0)))], axis=0)
    b_loc = lax.dynamic_slice(b_mod, (0, me * mcols), (1, mcols))
    mod_all, = _exchange([_mod_fwd(c9, w_mod[0], b_loc)], name="gather_mod", scatter=False)
    mod_lat = lax.dynamic_index_in_dim(mod_all, me, axis=1, keepdims=False).reshape(6, D)
    mod_ctx = mod_all[:, NDEV, :].reshape(6, D)

    small = {"q_norm_w": q_norm_w, "k_norm_w": k_norm_w, "gdn_norm_w": gdn_norm_w, "a_log": a_log, "dt_bias": dt_bias,
             "conv_qkv_w": _by_columns(conv_g), "ffn_conv_w": _by_columns(ffnw_g), "ffn_conv_b": ffn_conv_b,
             "final_norm_w": final_norm_w[None]}
    loss_me, grad_x, (pending_in, own_in), recv, dmod_lat, dmod_ctx, gs = _local_step(
        x[0], ctx[0], loss_target[0], mod_lat, mod_ctx, w_in_pad, shards, small)

    moments = {"w_in": (m_w_in, v_w_in), "w_pa": (m_w_pa, v_w_pa), "w_pd": (m_w_pd, v_w_pd),
               "w_out": (m_w_out, v_w_out), "w_up": (m_w_up, v_w_up), "w_down": (m_w_down, v_w_down)}
    res = {}
    def finish(n, outs):
        return tuple((t.T if n in transposed else t)[None] for t in outs)

    def moment(t, n):
        return t[0].T if n in transposed else t[0]

    for n in recv:
        res[n] = finish(n, _adamw_recv(big[n], recv[n], moment(moments[n][0], n), moment(moments[n][1], n),
                                       name="adamw_" + n))

    misc = jnp.concatenate([gs["q_norm_w"][0], gs["k_norm_w"][0], gs["gdn_norm_w"][0], gs["a_log"], gs["dt_bias"],
                            loss_me[None]])
    pack = jnp.concatenate([_rows_of(dmod_lat, P_CTX - P_LAT), _rows_of(dmod_ctx, P_FNW - P_CTX),
                            _rows_of(gs["final_norm_w"], P_FFNB - P_FNW), _rows_of(gs["ffn_conv_b"], P_CONV - P_FFNB),
                            _rows_of(gs["conv_qkv_w"], P_FFNW - P_CONV), _rows_of(gs["ffn_conv_w"], P_MISC - P_FFNW),
                            _rows_of(misc, P_ROWS - P_MISC)], axis=0)
    pack_all, = _exchange([pack], name="gather_pack", scatter=False)
    tot = _sum_slots(pack_all, name="sum_pack")
    dall = jnp.concatenate([pack_all[:, P_LAT:P_LAT + 6, :].reshape(NDEV, 6 * D),
                            jnp.pad(tot[P_CTX:P_CTX + 6].reshape(1, 6 * D), ((0, MODROWS - NDEV - 1), (0, 0)))], axis=0)
    dmy = lax.dynamic_slice(dall, (0, me * mcols), (MODROWS, mcols))
    g_w_mod, g_b_mod, cpart = _mod_bwd(c9, dmy, dall, w_mod[0])
    cparts, = _exchange([cpart], name="gather_cctx", scatter=False)
    sems_a, land = pending_in[:2], pending_in[3]
    *sems_b, g_in_thru, land, token_b = _scatter_start(pending_in[2], land, (D // 2, D // 2), (cparts,),
                                                       name="scatter_g_in_b_start")
    g_c_ctx = _cctx_finish(cparts, c_ctx[None], (token_b,))[0]

    nconv, nffn = 3 * GH * HD, 2 * DFF
    conv_tot = tot[P_CONV:P_FFNW].reshape(-1)[:3 * nconv].reshape(3, nconv)
    ffnw_tot = tot[P_FFNW:P_MISC].reshape(-1)[:3 * nffn].reshape(3, nffn)
    mrow = tot[P_MISC]
    grads = {
        "c_ctx": g_c_ctx, "w_mod": g_w_mod[None], "b_mod": g_b_mod,
        "q_norm_w": mrow[None, 0:HD], "k_norm_w": mrow[None, HD:2 * HD], "gdn_norm_w": mrow[None, 2 * HD:3 * HD],
        "conv_qkv_w": lax.dynamic_slice(conv_tot, (0, me * (nconv // NDEV)), (3, nconv // NDEV))[None],
        "a_log": mrow[3 * HD:3 * HD + 2 * GH].reshape(1, 2, GH),
        "dt_bias": mrow[3 * HD + 2 * GH:3 * HD + 4 * GH].reshape(1, 2, GH),
        "ffn_conv_w": lax.dynamic_slice(ffnw_tot, (0, me * (nffn // NDEV)), (3, nffn // NDEV))[None],
        "ffn_conv_b": tot[P_FFNB:P_CONV].reshape(-1)[:nffn][None],
        "final_norm_w": tot[P_FNW],
    }
    loss = mrow[3 * HD + 4 * GH]
    given = {"c_ctx": (c_ctx, m_c_ctx, v_c_ctx), "w_mod": (w_mod, m_w_mod, v_w_mod), "b_mod": (b_mod, m_b_mod, v_b_mod),
             "q_norm_w": (q_norm_w, m_q_norm_w, v_q_norm_w), "k_norm_w": (k_norm_w, m_k_norm_w, v_k_norm_w),
             "conv_qkv_w": (conv_qkv_w, m_conv_qkv_w, v_conv_qkv_w), "a_log": (a_log, m_a_log, v_a_log),
             "dt_bias": (dt_bias, m_dt_bias, v_dt_bias), "gdn_norm_w": (gdn_norm_w, m_gdn_norm_w, v_gdn_norm_w),
             "ffn_conv_w": (ffn_conv_w, m_ffn_conv_w, v_ffn_conv_w), "ffn_conv_b": (ffn_conv_b, m_ffn_conv_b, v_ffn_conv_b),
             "final_norm_w": (final_norm_w, m_final_norm_w, v_final_norm_w)}
    res["w_mod"] = (grads["w_mod"],) + _adamw(w_mod, grads["w_mod"], m_w_mod, v_w_mod, name="adamw_w_mod")
    small_names = [n for n in given if n != "w_mod"]
    updates = _adamw_many([(given[n][0], grads[n], given[n][1], given[n][2]) for n in small_names], name="adamw_small")
    for n, upd in zip(small_names, updates):
        res[n] = (grads[n],) + upd

    g_in_thru, land = _scatter_wait(*sems_a, g_in_thru, land, (0, D // 2), [res[n][1] for n in res],
                                    name="scatter_g_in_a_wait")
    _, land = _scatter_wait(*sems_b, g_in_thru, land, (D // 2, D // 2), (), name="scatter_g_in_b_wait")
    res["w_in"] = finish("w_in", _adamw_recv(big["w_in"], land, moment(m_w_in, "w_in"), moment(v_w_in, "w_in"),
                                             name="adamw_w_in", own=own_in))

    order = ["c_ctx", "w_mod", "b_mod", "w_in", "q_norm_w", "k_norm_w", "conv_qkv_w", "a_log", "dt_bias", "gdn_norm_w",
             "w_pa", "w_pd", "w_out", "w_up", "ffn_conv_w", "ffn_conv_b", "w_down", "final_norm_w"]
    return (loss, grad_x[None], *[res[n][0] for n in order], *[res[n][1] for n in order],
            *[res[n][2] for n in order], *[res[n][3] for n in order])
```

```python
import functools
import math

import jax
import jax.numpy as jnp
from jax import lax
from jax.experimental import pallas as pl
from jax.experimental.pallas import tpu as pltpu

F32 = jnp.float32
BF16 = jnp.bfloat16
HI = lax.Precision.HIGHEST
MESH = pl.DeviceIdType.MESH

NDEV = 8
D = 1024
HD = 128
AH, AKV, GRP = 8, 2, 4
GH = 8
CH = 64
DFF = 2816
GRID_W = 64
EPS = 1e-6
ROPE_THETA = 10000.0
LOG2E = math.log2(math.e)
C_KV, C_AQ, C_QKV, C_BL, C_Z, C_GATE, C_END = 0, 512, 1536, 4608, 5120, 6144, 8192
W_QKV, W_AQ, W_Z, W_END = 512, 3616, 4640, 7712


def _pad_columns(w):
    zeros = jnp.zeros((C_Z - C_QKV - (W_AQ - W_QKV), D), w.dtype)
    return jnp.concatenate([w[:W_QKV], w[W_AQ:W_Z], w[W_QKV:W_AQ], zeros, w[W_Z:]], axis=0)


def _unpad_columns(g):
    return jnp.concatenate([g[:C_AQ], g[C_QKV:C_QKV + W_AQ - W_QKV], g[C_AQ:C_QKV], g[C_Z:]], axis=0)
LR, B1, B2, AEPS, WD, STEP = 0.001, 0.9, 0.999, 1e-08, 0.01, 10
VMEM_BIG = 56 * 1024 * 1024
INTRA_FWD_CHUNKS = 36
INTRA_BWD_CHUNKS = 36


def _call(body, *, name, out_shape, grid=None, in_specs=None, out_specs=None, scratch=(), sem=None,
          vmem=None, aliases=None):
    params = {}
    if sem is not None:
        params["dimension_semantics"] = sem
    if vmem is not None:
        params["vmem_limit_bytes"] = vmem
    kw = {}
    if grid is not None:
        kw["grid"] = grid
    if in_specs is not None:
        kw["in_specs"] = in_specs
    if out_specs is not None:
        kw["out_specs"] = out_specs
    if aliases:
        kw["input_output_aliases"] = aliases
    return pl.pallas_call(body, name=name, out_shape=out_shape, scratch_shapes=list(scratch),
                          compiler_params=pltpu.CompilerParams(**params), **kw)


def _call_carrying(body, exch, *, name, out_shape, grid, in_specs, out_specs, scratch=(), vmem=None):
    n, nin, nout, nscr = exch.n, len(in_specs), len(out_shape), len(scratch)
    steps = math.prod(grid)
    mid = (2 * steps) // 3

    def wrapped(*refs):
        ins, cins = refs[:nin], refs[nin:nin + n]
        outs, couts = refs[nin + n:nin + n + nout], refs[nin + n + nout:nin + 2 * n + nout]
        scr, sems = refs[nin + 2 * n + nout:nin + 2 * n + nout + nscr], refs[nin + 2 * n + nout + nscr:]
        ids = [pl.program_id(i) for i in range(len(grid))]
        first = functools.reduce(jnp.logical_and, [i == 0 for i in ids])
        last = functools.reduce(jnp.logical_and, [i == g - 1 for i, g in zip(ids, grid)])

        @pl.when(first)
        def _():
            exch.start(cins, couts, sems)

        if hasattr(exch, "middle"):
            linear = functools.reduce(lambda acc, ig: acc * ig[1] + ig[0], zip(ids, grid), 0)

            @pl.when(linear == mid)
            def _():
                exch.middle(cins, couts, sems)

        body(*ins, *outs, *scr)

        @pl.when(last)
        def _():
            exch.finish(cins, couts, sems)

    params = {"dimension_semantics": ("arbitrary",) * len(grid)}
    if vmem is not None:
        params["vmem_limit_bytes"] = vmem
    fn = pl.pallas_call(wrapped, name=name, out_shape=tuple(out_shape) + exch.out_shape, grid=grid,
                        in_specs=list(in_specs) + [HBM] * n, out_specs=tuple(out_specs) + (HBM,) * n,
                        scratch_shapes=list(scratch) + exch.scratch, compiler_params=pltpu.CompilerParams(**params))

    def run(*args):
        res = fn(*args, *exch.arrs)
        return res[:nout], list(res[nout:])

    return run


def _sds(shape, dtype=F32):
    return jax.ShapeDtypeStruct(tuple(shape), dtype)


def _dot(a, b, ca, cb):
    return lax.dot_general(a.astype(BF16), b.astype(BF16), (((ca,), (cb,)), ((), ())),
                           preferred_element_type=F32)


@jax.custom_vjp
def _nn(a, b):
    return _dot(a, b, 1, 0)


@jax.custom_vjp
def _nt(a, b):
    return _dot(a, b, 1, 1)


@jax.custom_vjp
def _tn(a, b):
    return _dot(a, b, 0, 0)


_nn.defvjp(lambda a, b: (_nn(a, b), (a, b)), lambda r, g: (_nt(g, r[1]), _tn(r[0], g)))
_nt.defvjp(lambda a, b: (_nt(a, b), (a, b)), lambda r, g: (_nn(g, r[1]), _tn(g, r[0])))
_tn.defvjp(lambda a, b: (_tn(a, b), (a, b)), lambda r, g: (_nt(r[1], g), _nn(r[0], g)))


def _mdot(a, b):
    return jnp.dot(a, b, precision=lax.Precision.HIGH, preferred_element_type=F32)


def _maskdot(mask, a, cm):
    hi = a.astype(BF16)
    r = a - hi.astype(F32)
    mid = r.astype(BF16)
    lo = (r - mid.astype(F32)).astype(BF16)
    mb = mask.astype(BF16)
    dims = (((cm,), (0,)), ((), ()))
    return (lax.dot_general(mb, hi, dims, preferred_element_type=F32)
            + lax.dot_general(mb, mid, dims, preferred_element_type=F32)
            + lax.dot_general(mb, lo, dims, preferred_element_type=F32))


@jax.custom_vjp
def _mask_nn(mask, a):
    return _maskdot(mask, a, 1)


_mask_nn.defvjp(lambda mask, a: (_maskdot(mask, a, 1), mask),
                lambda mask, g: (jnp.zeros_like(mask), _maskdot(mask, g, 0)))


@jax.custom_vjp
def _saved_inverse(lmat, x):
    return x


def _saved_inverse_bwd(x, g):
    t = lax.dot_general(x, g, (((0,), (0,)), ((), ())), precision=lax.Precision.HIGH, preferred_element_type=F32)
    dl = lax.dot_general(t, x, (((1,), (1,)), ((), ())), precision=lax.Precision.HIGH, preferred_element_type=F32)
    return -dl, jnp.zeros_like(x)


_saved_inverse.defvjp(lambda lmat, x: (x, x), _saved_inverse_bwd)


def _row_ids(shape):
    return lax.broadcasted_iota(jnp.int32, shape, 0)


def _shift_rows(x, down, bounds):
    n = x.shape[0]
    rows = _row_ids(x.shape)
    y = pltpu.roll(x, 1 if down else n - 1, 0)
    edge = functools.reduce(jnp.logical_or, [rows == (s if down else e - 1) for s, e in bounds])
    return jnp.where(edge, 0.0, y)


def _make_shift(bounds):
    @jax.custom_vjp
    def down(x):
        return _shift_rows(x, True, bounds)

    @jax.custom_vjp
    def up(x):
        return _shift_rows(x, False, bounds)

    down.defvjp(lambda x: (down(x), None), lambda _, g: (up(g),))
    up.defvjp(lambda x: (up(x), None), lambda _, g: (down(g),))
    return down, up


@jax.custom_vjp
def _swap32(x):
    lane = lax.broadcasted_iota(jnp.int32, x.shape, x.ndim - 1)
    return jnp.where((lane % 64) < 32, pltpu.roll(x, HD - 32, x.ndim - 1), pltpu.roll(x, 32, x.ndim - 1))


_swap32.defvjp(lambda x: (_swap32(x), None), lambda _, g: (_swap32(g),))


def _rms(x):
    return x * lax.rsqrt(jnp.mean(x * x, axis=-1, keepdims=True) + EPS)


def _silu(x):
    return x * jax.nn.sigmoid(x)


def _mm(a, b, *, name, M, N, K, ta=False, tb=False, out_dtype=F32, bm=None, bn=None, bk=None, after=()):
    bm, bn, bk = bm or M, bn or N, bk or K
    assert M % bm == 0 and N % bn == 0 and K % bk == 0, (name, M, N, K, bm, bn, bk)
    nk = K // bk
    ca, cb = (0 if ta else 1), (1 if tb else 0)
    na = len(after)

    def body(a_ref, b_ref, *rest):
        o_ref, acc = rest[na], rest[na + 1:]
        r = _dot(a_ref[...], b_ref[...], ca, cb)
        if nk == 1:
            o_ref[...] = r.astype(out_dtype)
        else:
            acc_ref, = acc
            k = pl.program_id(2)

            @pl.when(k == 0)
            def _():
                acc_ref[...] = r

            @pl.when(k > 0)
            def _():
                acc_ref[...] += r

            @pl.when(k == nk - 1)
            def _():
                o_ref[...] = acc_ref[...].astype(out_dtype)

    a_spec = pl.BlockSpec((bk, bm), lambda i, j, k: (k, i)) if ta else pl.BlockSpec((bm, bk), lambda i, j, k: (i, k))
    b_spec = pl.BlockSpec((bn, bk), lambda i, j, k: (j, k)) if tb else pl.BlockSpec((bk, bn), lambda i, j, k: (k, j))
    return _call(body, name=name, out_shape=_sds((M, N), out_dtype), grid=(M // bm, N // bn, nk),
                 in_specs=[a_spec, b_spec] + [pl.BlockSpec(memory_space=pl.ANY)] * na,
                 out_specs=pl.BlockSpec((bm, bn), lambda i, j, k: (i, j)),
                 scratch=[pltpu.VMEM((bm, bn), F32)] if nk > 1 else [],
                 sem=("parallel", "parallel", "arbitrary"), vmem=VMEM_BIG)(a, b, *after)


def _normmod_fn(x, sh, sc):
    return _rms(x) * (1.0 + sc) + sh


def _normmod_fwd(x, mod, i_sh, i_sc, *, name, br=256):
    R = x.shape[0]

    def body(x_ref, mod_ref, o_ref):
        o_ref[...] = _normmod_fn(x_ref[...], mod_ref[i_sh:i_sh + 1, :], mod_ref[i_sc:i_sc + 1, :]).astype(BF16)

    return _call(body, name=name, out_shape=_sds((R, D), BF16), grid=(R // br,),
                 in_specs=[pl.BlockSpec((br, D), lambda i: (i, 0)), pl.BlockSpec((6, D), lambda i: (0, 0))],
                 out_specs=pl.BlockSpec((br, D), lambda i: (i, 0)), sem=("parallel",))(x, mod)


def _normmod_bwd(x, mod, i_sh, i_sc, dh, dh_off, res, *, name, br=256):
    R = x.shape[0]
    ob = dh_off // br
    has_res = res is not None

    def body(x_ref, mod_ref, dh_ref, *rest):
        if has_res:
            res_ref, dx_ref, dsh_ref, dsc_ref = rest
        else:
            dx_ref, dsh_ref, dsc_ref = rest
        sh, sc = mod_ref[i_sh:i_sh + 1, :], mod_ref[i_sc:i_sc + 1, :]
        _, vjp = jax.vjp(_normmod_fn, x_ref[...], sh, sc)
        dx, dsh, dsc = vjp(dh_ref[...])
        dx_ref[...] = dx + res_ref[...] if has_res else dx

        @pl.when(pl.program_id(0) == 0)
        def _():
            dsh_ref[...] = jnp.zeros_like(dsh_ref)
            dsc_ref[...] = jnp.zeros_like(dsc_ref)

        dsh_ref[...] += dsh
        dsc_ref[...] += dsc

    row = pl.BlockSpec((br, D), lambda i: (i, 0))
    vec = pl.BlockSpec((1, D), lambda i: (0, 0))
    ins = [row, pl.BlockSpec((6, D), lambda i: (0, 0)), pl.BlockSpec((br, D), lambda i: (i + ob, 0))]
    args = [x, mod, dh]
    if has_res:
        ins.append(row)
        args.append(res)
    return _call(body, name=name, out_shape=(_sds((R, D)), _sds((1, D)), _sds((1, D))), grid=(R // br,),
                 in_specs=ins, out_specs=(row, vec, vec), sem=("arbitrary",))(*args)


def _rope(x, cos, sin):
    return x * cos + _swap32(x) * sin


def _aprep_fn(qs, ks, cos, sin, qw, kw):
    return ([_rope(_rms(q) * qw, cos, sin) for q in qs], [_rope(_rms(k) * kw, cos, sin) for k in ks])


def _aprep_fwd(proj, cos, sin, qw, kw, *, br=256):
    T = proj.shape[0]

    def body(x_ref, cos_ref, sin_ref, qw_ref, kw_ref, q_ref, k_ref, v_ref):
        qs = [x_ref[:, C_AQ + h * HD:C_AQ + (h + 1) * HD] for h in range(AH)]
        ks = [x_ref[:, h * HD:(h + 1) * HD] for h in range(AKV)]
        qo, ko = _aprep_fn(qs, ks, cos_ref[...], sin_ref[...], qw_ref[...], kw_ref[...])
        for h in range(AH):
            q_ref[h] = qo[h].astype(BF16)
        for h in range(AKV):
            k_ref[h] = ko[h].astype(BF16)
            v_ref[h] = x_ref[:, (AKV + h) * HD:(AKV + h + 1) * HD].astype(BF16)

    tab = pl.BlockSpec((br, HD), lambda i: (i, 0))
    vec = pl.BlockSpec((1, HD), lambda i: (0, 0))
    return _call(body, name="aprep_fwd",
                 out_shape=(_sds((AH, T, HD), BF16), _sds((AKV, T, HD), BF16), _sds((AKV, T, HD), BF16)),
                 grid=(T // br,),
                 in_specs=[pl.BlockSpec((br, C_QKV), lambda i: (i, 0)), tab, tab, vec, vec],
                 out_specs=(pl.BlockSpec((AH, br, HD), lambda i: (0, i, 0)),
                            pl.BlockSpec((AKV, br, HD), lambda i: (0, i, 0)),
                            pl.BlockSpec((AKV, br, HD), lambda i: (0, i, 0))),
                 sem=("parallel",))(proj, cos, sin, qw, kw)


def _aprep_bwd(proj, cos, sin, qw, kw, dq, dk, dv, dproj, L, *, br=256):
    T = proj.shape[0]
    lb = L // br

    def body(x_ref, cos_ref, sin_ref, qw_ref, kw_ref, dq_ref, dk_ref, dv_ref, _, dx_ref, dqw_ref, dkw_ref):
        i = pl.program_id(0)
        qs = [x_ref[:, C_AQ + h * HD:C_AQ + (h + 1) * HD] for h in range(AH)]
        ks = [x_ref[:, h * HD:(h + 1) * HD] for h in range(AKV)]
        _, vjp = jax.vjp(_aprep_fn, qs, ks, cos_ref[...], sin_ref[...], qw_ref[...], kw_ref[...])
        is_lat = i >= lb
        dqs = [jnp.where(is_lat, dq_ref[h], 0.0) for h in range(AH)]
        dks = [dk_ref[h] for h in range(AKV)]
        gq, gk, _, _, gqw, gkw = vjp((dqs, dks))
        for h in range(AH):
            dx_ref[:, C_AQ + h * HD:C_AQ + (h + 1) * HD] = gq[h].astype(BF16)
        for h in range(AKV):
            dx_ref[:, h * HD:(h + 1) * HD] = gk[h].astype(BF16)
            dx_ref[:, (AKV + h) * HD:(AKV + h + 1) * HD] = dv_ref[h].astype(BF16)

        @pl.when(i == 0)
        def _():
            dqw_ref[...] = jnp.zeros_like(dqw_ref)
            dkw_ref[...] = jnp.zeros_like(dkw_ref)

        dqw_ref[...] += gqw
        dkw_ref[...] += gkw

    tab = pl.BlockSpec((br, HD), lambda i: (i, 0))
    vec = pl.BlockSpec((1, HD), lambda i: (0, 0))
    kvb = pl.BlockSpec((AKV, br, HD), lambda i: (0, i, 0))
    blk = pl.BlockSpec((br, C_QKV), lambda i: (i, 0))
    return _call(body, name="aprep_bwd", out_shape=(_sds(dproj.shape, BF16), _sds((1, HD)), _sds((1, HD))),
                 grid=(T // br,),
                 in_specs=[blk, tab, tab, vec, vec,
                           pl.BlockSpec((AH, br, HD), lambda i: (0, jnp.maximum(i - lb, 0), 0)), kvb, kvb, ANYSPEC],
                 out_specs=(blk, vec, vec), aliases={8: 0},
                 sem=("arbitrary",))(proj, cos, sin, qw, kw, dq, dk, dv, dproj)


def _attn_grad(q, k, v, o, lse2, do):
    scale = HD ** -0.5
    p = jnp.exp2(_dot(q, k, 1, 1) * (scale * LOG2E) - lse2)
    dp = _dot(do, v, 1, 1)
    ds = p * (dp - jnp.sum(do * o, axis=-1, keepdims=True)) * scale
    return _dot(ds, k, 1, 0), _dot(ds, q, 0, 0), _dot(p, do, 0, 0)


ATTN_KEYS = 256


def _attn_fwd(q, k, v, L, exch, *, bq=128):
    T = q.shape[1]
    N = T - L
    lb = L // bq
    assert T % ATTN_KEYS == 0
    scale = HD ** -0.5
    heads = range(GRP)

    def body(q_ref, k_ref, v_ref, o_ref, o32_ref, lse_ref):
        qs = [q_ref[g] for g in heads]
        m = [jnp.full((bq, 1), -jnp.inf, F32) for _ in heads]
        l = [jnp.zeros((bq, 1), F32) for _ in heads]
        acc = [jnp.zeros((bq, HD), F32) for _ in heads]
        for c in range(T // ATTN_KEYS):
            kc, vc = k_ref[c * ATTN_KEYS:(c + 1) * ATTN_KEYS, :], v_ref[c * ATTN_KEYS:(c + 1) * ATTN_KEYS, :]
            s = [_dot(qs[g], kc, 1, 1) * (scale * LOG2E) for g in heads]
            m_new = [jnp.maximum(m[g], jnp.max(s[g], axis=-1, keepdims=True)) for g in heads]
            alpha = [jnp.exp2(m[g] - m_new[g]) for g in heads]
            p = [jnp.exp2(s[g] - m_new[g]) for g in heads]
            l = [l[g] * alpha[g] + jnp.sum(p[g], axis=-1, keepdims=True) for g in heads]
            acc = [acc[g] * alpha[g] + _dot(p[g], vc, 1, 0) for g in heads]
            m = m_new
        for g in heads:
            o = acc[g] / l[g]
            o_ref[:, g * HD:(g + 1) * HD] = o.astype(BF16)
            o32_ref[:, g * HD:(g + 1) * HD] = o
            lse_ref[g] = jnp.broadcast_to(m[g] + jnp.log2(l[g]), (bq, HD))

    kvb = pl.BlockSpec((None, T, HD), lambda g, i: (g, 0, 0))
    ob = pl.BlockSpec((bq, GRP * HD), lambda g, i: (i, g))
    return _call_carrying(
        body, exch, name="attn_fwd",
        out_shape=(_sds((N, AH * HD), BF16), _sds((N, AH * HD)), _sds((AH, N, HD))), grid=(AKV, N // bq),
        in_specs=[pl.BlockSpec((GRP, bq, HD), lambda g, i: (g, i + lb, 0)), kvb, kvb],
        out_specs=(ob, ob, pl.BlockSpec((GRP, bq, HD), lambda g, i: (g, i, 0))), vmem=VMEM_BIG)(q, k, v)


def _attn_bwd(q, k, v, o32, lse, do, L, *, bq=128):
    T = q.shape[1]
    N = T - L
    lb = L // bq

    def body(q_ref, k_ref, v_ref, o_ref, lse_ref, do_ref, dq_ref, dk_ref, dv_ref):
        rows = lambda r: jnp.concatenate([r[:, g * HD:(g + 1) * HD] for g in range(GRP)], axis=0)
        lse = jnp.max(lse_ref[...].reshape(GRP * bq, HD), axis=-1, keepdims=True)
        dq, dk, dv = _attn_grad(q_ref[...].reshape(GRP * bq, HD), k_ref[...], v_ref[...], rows(o_ref), lse, rows(do_ref))
        dq_ref[...] = dq.reshape(GRP, bq, HD)

        @pl.when(pl.program_id(1) == 0)
        def _():
            dk_ref[...] = jnp.zeros_like(dk_ref)
            dv_ref[...] = jnp.zeros_like(dv_ref)

        dk_ref[...] += dk
        dv_ref[...] += dv

    kvb = pl.BlockSpec((None, T, HD), lambda g, i: (g, 0, 0))
    qb = pl.BlockSpec((GRP, bq, HD), lambda g, i: (g, i + lb, 0))
    hb = pl.BlockSpec((GRP, bq, HD), lambda g, i: (g, i, 0))
    ob = pl.BlockSpec((bq, GRP * HD), lambda g, i: (i, g))
    return _call(body, name="attn_bwd",
                 out_shape=(_sds((AH, N, HD)), _sds((AKV, T, HD)), _sds((AKV, T, HD))), grid=(AKV, N // bq),
                 in_specs=[qb, kvb, kvb, ob, hb, ob], out_specs=(hb, kvb, kvb),
                 sem=("parallel", "arbitrary"), vmem=VMEM_BIG)(q, k, v, o32, lse, do)


def _gprep_fn(kind, shifts, x, w):
    down, up = shifts
    y = down(x) * w[0:1, :] + x * w[1:2, :] + up(x) * w[2:3, :]
    a = _silu(y)
    if kind == 2:
        return a
    a = a * lax.rsqrt(jnp.sum(a * a, axis=-1, keepdims=True) + EPS)
    return a * (HD ** -0.5) if kind == 0 else a


def _gprep_fwd(proj, conv_w, kind, bounds):
    T = proj.shape[0]
    shifts = _make_shift(bounds)
    cb = C_QKV // HD + kind * GH

    def body(x_ref, w_ref, o_ref):
        o_ref[...] = _gprep_fn(kind, shifts, x_ref[...], w_ref[...])

    return _call(body, name=f"gprep_fwd{kind}", out_shape=_sds((GH, T, HD)), grid=(GH,),
                 in_specs=[pl.BlockSpec((T, HD), lambda h: (0, cb + h)),
                           pl.BlockSpec((3, HD), lambda h: (0, kind * GH + h))],
                 out_specs=pl.BlockSpec((None, T, HD), lambda h: (h, 0, 0)), sem=("parallel",))(proj, conv_w)


def _gprep_bwd(proj, conv_w, kind, bounds, dy, dproj):
    T = proj.shape[0]
    shifts = _make_shift(bounds)
    cb = C_QKV // HD + kind * GH

    def body(x_ref, w_ref, dy_ref, _, dx_ref, dw_ref):
        _, vjp = jax.vjp(functools.partial(_gprep_fn, kind, shifts), x_ref[...], w_ref[...])
        dx, dw = vjp(dy_ref[0] + dy_ref[1])
        dx_ref[...] = dx.astype(BF16)
        dw_ref[...] = dw

    return _call(body, name=f"gprep_bwd{kind}", out_shape=(_sds(dproj.shape, BF16), _sds((3, GH * HD))), grid=(GH,),
                 in_specs=[pl.BlockSpec((T, HD), lambda h: (0, cb + h)),
                           pl.BlockSpec((3, HD), lambda h: (0, kind * GH + h)),
                           pl.BlockSpec((2, None, T, HD), lambda h: (0, h, 0, 0)), ANYSPEC],
                 out_specs=(pl.BlockSpec((T, HD), lambda h: (0, cb + h)), pl.BlockSpec((3, HD), lambda h: (0, h))),
                 aliases={3: 0}, sem=("parallel",))(proj, conv_w, dy, dproj)


def _bl_fn(x, alog, dtb):
    lane = lax.broadcasted_iota(jnp.int32, x.shape, 1)
    beta = jax.nn.sigmoid(x)
    z = x + dtb
    sp = jnp.maximum(z, 0.0) + jnp.log1p(jnp.exp(-jnp.abs(z)))
    la = -jnp.exp(alog) * sp
    return jnp.where(lane < 2 * GH, beta, jnp.where(lane < 4 * GH, la, 0.0))


def _bl_fwd(proj, alog, dtb, *, br=256):
    T = proj.shape[0]

    def body(x_ref, a_ref, d_ref, o_ref):
        o_ref[...] = _bl_fn(x_ref[...], a_ref[...], d_ref[...])

    vec = pl.BlockSpec((1, HD), lambda i: (0, 0))
    return _call(body, name="bl_fwd", out_shape=_sds((T, HD)), grid=(T // br,),
                 in_specs=[pl.BlockSpec((br, HD), lambda i: (i, C_BL // HD)), vec, vec],
                 out_specs=pl.BlockSpec((br, HD), lambda i: (i, 0)), sem=("parallel",))(proj, alog, dtb)


def _bl_bwd(proj, alog, dtb, dbl, dproj, *, br=256):
    T = proj.shape[0]
    wide = C_Z - C_BL

    def body(x_ref, a_ref, d_ref, g_ref, _, dx_ref, da_ref, dd_ref):
        g = g_ref[0, 0]
        for d in range(2):
            for h in range(GH):
                if d or h:
                    g = g + g_ref[d, h]
        _, vjp = jax.vjp(_bl_fn, x_ref[...], a_ref[...], d_ref[...])
        dx, da, dd = vjp(g)
        dx_ref[:, :HD] = dx.astype(BF16)
        dx_ref[:, HD:] = jnp.zeros((br, wide - HD), BF16)

        @pl.when(pl.program_id(0) == 0)
        def _():
            da_ref[...] = jnp.zeros_like(da_ref)
            dd_ref[...] = jnp.zeros_like(dd_ref)

        da_ref[...] += da
        dd_ref[...] += dd

    vec = pl.BlockSpec((1, HD), lambda i: (0, 0))
    return _call(body, name="bl_bwd", out_shape=(_sds(dproj.shape, BF16), _sds((1, HD)), _sds((1, HD))), grid=(T // br,),
                 in_specs=[pl.BlockSpec((br, HD), lambda i: (i, C_BL // HD)), vec, vec,
                           pl.BlockSpec((2, GH, br, HD), lambda i: (0, 0, i, 0)), ANYSPEC],
                 out_specs=(pl.BlockSpec((br, wide), lambda i: (i, C_BL // wide)), vec, vec), aliases={4: 0},
                 sem=("arbitrary",))(proj, alog, dtb, dbl, dproj)


def _chunk_masks(d):
    ii = lax.broadcasted_iota(jnp.int32, (CH, CH), 0)
    jj = lax.broadcasted_iota(jnp.int32, (CH, CH), 1)
    eye = (ii == jj).astype(F32)
    before = jnp.where(d == 0, (jj < ii).astype(F32), (jj > ii).astype(F32))
    return before, before + eye, eye


def _intra_fn(masks, sel_b, sel_l, qs, ks, vs, bls, xs=None):
    before, ateq, eye = masks
    ones = jnp.ones((CH, CH), F32)
    inc = ateq > 0.0
    each = lambda f, *ls: [f(*t) for t in zip(*ls)]
    beta = each(lambda bl: jnp.sum(bl * sel_b, axis=-1, keepdims=True), bls)
    la = each(lambda bl: jnp.sum(bl * sel_l, axis=-1, keepdims=True), bls)
    gam = each(lambda a: _mask_nn(ateq, jnp.broadcast_to(a, (CH, HD))), la)
    gi = each(lambda a: _mask_nn(ateq, jnp.broadcast_to(a, (CH, CH))), la)
    gj = each(lambda g: _mask_nn(ones, eye * g), gi)
    kk = each(lambda k: _nt(k, k), ks)
    qk = each(_nt, qs, ks)
    dec = each(lambda a, b: jnp.where(inc, jnp.exp(jnp.where(inc, a - b, 0.0)), 0.0), gi, gj)
    lmat = each(lambda b, d, m: before * (b * d * m), beta, dec, kk)
    if xs is None:
        x = each(lambda m: eye - m, lmat)
        p2 = each(lambda m: _mdot(m, m), lmat)
        for it in range(4):
            y = each(lambda a, b: _mdot(jnp.concatenate([a, b], axis=0), b), x, p2)
            x = each(lambda a, t: a + t[:CH], x, y)
            p2 = each(lambda t: t[CH:], y)
        x = each(lambda a, b: a + _mdot(a, b), x, p2)
    else:
        x = each(_saved_inverse, lmat, xs)
    eg = each(jnp.exp, gam)
    u = each(lambda a, b, v: _mdot(a, b * v), x, beta, vs)
    w = each(lambda a, b, e, k: _mdot(a, (b * e) * k), x, beta, eg, ks)
    tot = each(lambda a: jnp.sum(a, axis=0, keepdims=True), la)
    kd = each(lambda k, t, g: k * jnp.exp(t - g), ks, tot, gam)
    gl = each(lambda t: jnp.broadcast_to(jnp.exp(t), (1, HD)), tot)
    qd = each(lambda q, e: q * e, qs, eg)
    p = each(lambda d, m: d * m, dec, qk)
    return (u, w, kd, qd, p, gl, x) if xs is None else (u, w, kd, qd, p, gl)


def _dir_head_sel(d, h):
    lane = lax.broadcasted_iota(jnp.int32, (1, HD), 1)
    return (lane == d * GH + h).astype(F32), (lane == 2 * GH + d * GH + h).astype(F32)


def _intra_specs(T, G):
    nc = T // CH
    assert nc % G == 0
    qkv = pl.BlockSpec((None, G * CH, HD), lambda d, h, c: (h, c, 0))
    bl = pl.BlockSpec((G * CH, HD), lambda d, h, c: (c, 0))
    big = pl.BlockSpec((None, None, G * CH, HD), lambda d, h, c: (d, h, c, 0))
    pm = pl.BlockSpec((None, None, G * CH, CH), lambda d, h, c: (d, h, c, 0))
    gl = pl.BlockSpec((None, None, G, 1, HD), lambda d, h, c: (d, h, c, 0, 0))
    shapes = (_sds((2, GH, T, HD)),) + (_sds((2, GH, T, HD), BF16),) * 3 + (
        _sds((2, GH, T, CH), BF16), _sds((2, GH, nc, 1, HD)), _sds((2, GH, T, CH)))
    return nc, qkv, bl, big, pm, gl, shapes


def _chunks_per_step(T, most):
    nc = T // CH
    return max(g for g in range(1, most + 1) if nc % g == 0)


def _chunk_at(g, d, nc, ncc):
    pos = _visit_pos(g, d, nc, ncc)
    return pos, pl.ds(pl.multiple_of(pos * CH, CH), CH)


def _intra_fwd(q, k, v, bl, L, exch):
    T = q.shape[1]
    G = _chunks_per_step(T, INTRA_FWD_CHUNKS)
    nc, qkv_s, bl_s, big, pm, gl_s, shapes = _intra_specs(T, G)
    assert G == nc
    ncc = L // CH

    def body(q_ref, k_ref, v_ref, bl_ref, u_ref, w_ref, kd_ref, qd_ref, p_ref, gl_ref, x_ref):
        d, h = pl.program_id(0), pl.program_id(1)
        sb, sl = _dir_head_sel(d, h)
        rows = [slice(g * CH, (g + 1) * CH) for g in range(G)]
        outs = _intra_fn(_chunk_masks(d), sb, sl, *[[r[s, :] for s in rows] for r in (q_ref, k_ref, v_ref, bl_ref)])
        for g in range(G):
            pos, at = _chunk_at(g, d, nc, ncc)
            for r, o in zip((u_ref, w_ref, kd_ref, qd_ref, p_ref, x_ref), outs[:5] + outs[6:]):
                r[at, :] = o[g].astype(r.dtype)
            gl_ref[pos] = outs[5][g]

    return _call_carrying(body, exch, name="gdn_intra_fwd", out_shape=shapes, grid=(2, GH, nc // G),
                          in_specs=[qkv_s, qkv_s, qkv_s, bl_s], out_specs=(big, big, big, big, pm, gl_s, pm))(q, k, v, bl)


def _intra_bwd(q, k, v, bl, xinv, cts, L, exch):
    T = q.shape[1]
    G = _chunks_per_step(T, INTRA_BWD_CHUNKS)
    nc, qkv_s, bl_s, big, pm, gl_s, _ = _intra_specs(T, G)
    assert G == nc
    ncc = L // CH

    def body(q_ref, k_ref, v_ref, bl_ref, x_ref, du, dw, dkd, dqd, dp, dgl, dq_ref, dk_ref, dv_ref, dbl_ref):
        d, h = pl.program_id(0), pl.program_id(1)
        sb, sl = _dir_head_sel(d, h)
        rows = [slice(g * CH, (g + 1) * CH) for g in range(G)]
        places = [_chunk_at(g, d, nc, ncc) for g in range(G)]
        fn = functools.partial(_intra_fn, _chunk_masks(d), sb, sl, xs=[x_ref[at, :] for _, at in places])
        _, vjp = jax.vjp(fn, *[[r[s, :] for s in rows] for r in (q_ref, k_ref, v_ref, bl_ref)])
        cts = tuple([r[at, :] for _, at in places] for r in (du, dw, dkd, dqd, dp)) + ([dgl[pos] for pos, _ in places],)
        grads = vjp(cts)
        for g in range(G):
            for r, o in zip((dq_ref, dk_ref, dv_ref, dbl_ref), grads):
                r[rows[g], :] = o[g]

    return _call_carrying(body, exch, name="gdn_intra_bwd", out_shape=(_sds((2, GH, T, HD)),) * 4,
                          grid=(2, GH, nc // G), in_specs=[qkv_s, qkv_s, qkv_s, bl_s, pm, big, big, big, big, pm, gl_s],
                          out_specs=(big,) * 4)(q, k, v, bl, xinv, *cts)


def _scan_fn(s, u, w, kd, qd, p, gl):
    each = lambda f, *ls: [f(*t) for t in zip(*ls)]
    ws = each(_nn, w, s)
    delta = each(lambda a, b: a - b, u, ws)
    kdd = each(_tn, kd, delta)
    s_new = each(lambda g, a, b: g * a + b, gl, s, kdd)
    qs = each(_nn, qd, s)
    pd = each(_nn, p, delta)
    return each(lambda a, b: a + b, qs, pd), s_new


SCAN_BLOCK = 4


def _visit_pos(c, d, nc, ncc):
    back = ncc - 1 - c if c < ncc else ncc + (nc - 1 - c)
    return jnp.where(d == 0, c, back)


def _scan_specs(T, L, back):
    tb = SCAN_BLOCK * CH
    assert T % tb == 0 and L % tb == 0
    nb, ncb = T // tb, L // tb
    at = (lambda t: nb - 1 - t) if back else (lambda t: t)
    big = pl.BlockSpec((2, GH, tb, HD), lambda t: (0, 0, at(t), 0))
    pm = pl.BlockSpec((2, GH, tb, CH), lambda t: (0, 0, at(t), 0))
    gl = pl.BlockSpec((2, GH, SCAN_BLOCK, 1, HD), lambda t: (0, 0, at(t), 0, 0))
    st = pl.BlockSpec((2, GH, SCAN_BLOCK, HD, HD), lambda t: (0, 0, at(t), 0, 0))

    def natural(b):
        return jnp.where(b < ncb, ncb - 1 - b, nb - 1 - (b - ncb))

    do_specs = (pl.BlockSpec((GH, tb, HD), lambda t: (0, at(t), 0)),
                pl.BlockSpec((GH, tb, HD), lambda t: (0, natural(at(t)), 0)))
    return nb, big, pm, gl, st, do_specs


SCAN_STREAMS = [(d, h) for d in (0, 1) for h in range(GH)]


def _scan_fwd(u, w, kd, qd, p, gl, L):
    T = u.shape[2]
    nb, big, pm, gl_s, st, _ = _scan_specs(T, L, False)

    def body(u_ref, w_ref, kd_ref, qd_ref, p_ref, gl_ref, o_ref, st_ref, s_scr):
        @pl.when(pl.program_id(0) == 0)
        def _():
            s_scr[...] = jnp.zeros_like(s_scr)

        s = [s_scr[d, h] for d, h in SCAN_STREAMS]
        for i in range(SCAN_BLOCK):
            rows = slice(i * CH, (i + 1) * CH)
            for (d, h), sv in zip(SCAN_STREAMS, s):
                st_ref[d, h, i] = sv
            o, s = _scan_fn(s, *[[r[d, h, rows, :].astype(F32) for d, h in SCAN_STREAMS]
                                 for r in (u_ref, w_ref, kd_ref, qd_ref, p_ref)],
                            [gl_ref[d, h, i] for d, h in SCAN_STREAMS])
            for (d, h), ov in zip(SCAN_STREAMS, o):
                o_ref[d, h, rows, :] = ov
        for (d, h), sv in zip(SCAN_STREAMS, s):
            s_scr[d, h] = sv

    return _call(body, name="gdn_scan_fwd", out_shape=(_sds((2, GH, T, HD)), _sds((2, GH, T // CH, HD, HD))),
                 grid=(nb,), in_specs=[big, big, big, big, pm, gl_s], out_specs=(big, st),
                 scratch=[pltpu.VMEM((2, GH, HD, HD), F32)], sem=("arbitrary",), vmem=VMEM_BIG)(u, w, kd, qd, p, gl)


def _scan_bwd(u, w, kd, qd, p, gl, states, do, L, exch):
    T = u.shape[2]
    nb, big, pm, gl_s, st, do_specs = _scan_specs(T, L, True)

    def body(u_ref, w_ref, kd_ref, qd_ref, p_ref, gl_ref, st_ref, do0_ref, do1_ref,
             du_ref, dw_ref, dkd_ref, dqd_ref, dp_ref, dgl_ref, ds_scr):
        @pl.when(pl.program_id(0) == 0)
        def _():
            ds_scr[...] = jnp.zeros_like(ds_scr)

        ds = [ds_scr[d, h] for d, h in SCAN_STREAMS]
        for i in reversed(range(SCAN_BLOCK)):
            rows = slice(i * CH, (i + 1) * CH)
            mirror = slice((SCAN_BLOCK - 1 - i) * CH, (SCAN_BLOCK - i) * CH)
            _, vjp = jax.vjp(_scan_fn, [st_ref[d, h, i] for d, h in SCAN_STREAMS],
                             *[[r[d, h, rows, :].astype(F32) for d, h in SCAN_STREAMS]
                               for r in (u_ref, w_ref, kd_ref, qd_ref, p_ref)],
                             [gl_ref[d, h, i] for d, h in SCAN_STREAMS])
            dos = [do0_ref[h, rows, :] if d == 0 else do1_ref[h, mirror, :] for d, h in SCAN_STREAMS]
            ds, gu, gw, gkd, gqd, gp, ggl = vjp((dos, ds))
            for n, (d, h) in enumerate(SCAN_STREAMS):
                du_ref[d, h, rows, :] = gu[n]
                dw_ref[d, h, rows, :] = gw[n]
                dkd_ref[d, h, rows, :] = gkd[n]
                dqd_ref[d, h, rows, :] = gqd[n]
                dp_ref[d, h, rows, :] = gp[n]
                dgl_ref[d, h, i] = ggl[n]
        for (d, h), dv in zip(SCAN_STREAMS, ds):
            ds_scr[d, h] = dv

    return _call_carrying(
        body, exch, name="gdn_scan_bwd",
        out_shape=(_sds((2, GH, T, HD)),) * 4 + (_sds((2, GH, T, CH)), _sds((2, GH, T // CH, 1, HD))),
        grid=(nb,), in_specs=[big, big, big, big, pm, gl_s, st, *do_specs], out_specs=(big, big, big, big, pm, gl_s),
        scratch=[pltpu.VMEM((2, GH, HD, HD), F32)], vmem=VMEM_BIG)(u, w, kd, qd, p, gl, states, do, do)


def _gout_fn(o0, o1, z, gw):
    return _rms(o0 + o1) * gw * _silu(z)


def _backward_latent(o_ref, L):
    nl = (o_ref.shape[1] - L) // CH
    return jnp.concatenate([o_ref[1, L + (nl - 1 - j) * CH:L + (nl - j) * CH, :] for j in range(nl)], axis=0)


def _gout_fwd(o, proj, gw, L):
    T = o.shape[2]
    N = T - L
    ob = pl.BlockSpec((2, None, T, HD), lambda h: (0, h, 0, 0))

    def body(o_ref, z_ref, gw_ref, y_ref):
        y_ref[...] = _gout_fn(o_ref[0, L:, :], _backward_latent(o_ref, L), z_ref[L:, :], gw_ref[...]).astype(BF16)

    return _call(body, name="gout_fwd", out_shape=_sds((N, GH * HD), BF16), grid=(GH,),
                 in_specs=[ob, pl.BlockSpec((T, HD), lambda h: (0, C_Z // HD + h)), pl.BlockSpec((1, HD), lambda h: (0, 0))],
                 out_specs=pl.BlockSpec((N, HD), lambda h: (0, h)), sem=("parallel",))(o, proj, gw)


def _gout_bwd(o, proj, gw, dy, dproj, L):
    T = o.shape[2]
    N = T - L
    ob = pl.BlockSpec((2, None, T, HD), lambda h: (0, h, 0, 0))

    def body(o_ref, z_ref, gw_ref, dy_ref, _, do_ref, dz_ref, dgw_ref):
        _, vjp = jax.vjp(_gout_fn, o_ref[0, L:, :], _backward_latent(o_ref, L), z_ref[L:, :], gw_ref[...])
        g0, _, gz, ggw = vjp(dy_ref[...])
        do_ref[:L, :] = jnp.zeros((L, HD), F32)
        do_ref[L:, :] = g0
        dz_ref[:L, :] = jnp.zeros((L, HD), BF16)
        dz_ref[L:, :] = gz.astype(BF16)

        @pl.when(pl.program_id(0) == 0)
        def _():
            dgw_ref[...] = jnp.zeros_like(dgw_ref)

        dgw_ref[...] += ggw

    zb = pl.BlockSpec((T, HD), lambda h: (0, C_Z // HD + h))
    return _call(body, name="gout_bwd", out_shape=(_sds((GH, T, HD)), _sds(dproj.shape, BF16), _sds((1, HD))),
                 grid=(GH,),
                 in_specs=[ob, zb, pl.BlockSpec((1, HD), lambda h: (0, 0)), pl.BlockSpec((N, HD), lambda h: (0, h)), ANYSPEC],
                 out_specs=(pl.BlockSpec((None, T, HD), lambda h: (h, 0, 0)), zb, pl.BlockSpec((1, HD), lambda h: (0, 0))),
                 aliases={4: 1}, sem=("arbitrary",))(o, proj, gw, dy, dproj)


def _merge_fn(pa, pd, ga, gd):
    return jax.nn.sigmoid(ga) * pa + jax.nn.sigmoid(gd) * pd


def _merge_fwd(pa, pd, proj, L, *, br=256):
    N = pa.shape[0]
    lb = L // br
    row = pl.BlockSpec((br, D), lambda i: (i, 0))

    def body(pa_ref, pd_ref, ga_ref, gd_ref, y_ref):
        y_ref[...] = _merge_fn(pa_ref[...], pd_ref[...], ga_ref[...], gd_ref[...]).astype(BF16)

    return _call(body, name="merge_fwd", out_shape=_sds((N, D), BF16), grid=(N // br,),
                 in_specs=[row, row, pl.BlockSpec((br, D), lambda i: (i + lb, C_GATE // D)),
                           pl.BlockSpec((br, D), lambda i: (i + lb, C_GATE // D + 1))],
                 out_specs=row, sem=("parallel",))(pa, pd, proj, proj)


def _merge_bwd(pa, pd, proj, dy, L, *, br=256):
    N = pa.shape[0]
    T = N + L
    lb = L // br
    lrow = pl.BlockSpec((br, D), lambda i: (jnp.maximum(i - lb, 0), 0))

    def body(pa_ref, pd_ref, ga_ref, gd_ref, dy_ref, dpa_ref, dpd_ref, dg_ref):
        lat = pl.program_id(0) >= lb
        _, vjp = jax.vjp(_merge_fn, pa_ref[...], pd_ref[...], ga_ref[...], gd_ref[...])
        gpa, gpd, gga, ggd = vjp(dy_ref[...])
        dpa_ref[...] = gpa.astype(BF16)
        dpd_ref[...] = gpd.astype(BF16)
        dg_ref[:, :D] = jnp.where(lat, gga, 0.0).astype(BF16)
        dg_ref[:, D:] = jnp.where(lat, ggd, 0.0).astype(BF16)

    return _call(body, name="merge_bwd", out_shape=(_sds((N, D), BF16), _sds((N, D), BF16), _sds((T, C_END), BF16)),
                 grid=(T // br,),
                 in_specs=[lrow, lrow, pl.BlockSpec((br, D), lambda i: (i, C_GATE // D)),
                           pl.BlockSpec((br, D), lambda i: (i, C_GATE // D + 1)), lrow],
                 out_specs=(lrow, lrow, pl.BlockSpec((br, 2 * D), lambda i: (i, C_GATE // (2 * D)))),
                 sem=("arbitrary",))(pa, pd, proj, proj, dy)


def _resid_fwd(x, m, mod, i_g, *, name, br=256):
    R = x.shape[0]
    row = pl.BlockSpec((br, D), lambda i: (i, 0))

    def body(x_ref, m_ref, mod_ref, o_ref):
        o_ref[...] = x_ref[...] + mod_ref[i_g:i_g + 1, :] * m_ref[...]

    return _call(body, name=name, out_shape=_sds((R, D)), grid=(R // br,),
                 in_specs=[row, row, pl.BlockSpec((6, D), lambda i: (0, 0))], out_specs=row,
                 sem=("parallel",))(x, m, mod)


def _resid_bwd(dx, m, mod, i_g, *, name, br=256):
    R = dx.shape[0]
    row = pl.BlockSpec((br, D), lambda i: (i, 0))
    vec = pl.BlockSpec((1, D), lambda i: (0, 0))

    def body(dx_ref, m_ref, mod_ref, dm_ref, dg_ref):
        dxv = dx_ref[...]
        dm_ref[...] = (dxv * mod_ref[i_g:i_g + 1, :]).astype(BF16)

        @pl.when(pl.program_id(0) == 0)
        def _():
            dg_ref[...] = jnp.zeros_like(dg_ref)

        dg_ref[...] += jnp.sum(dxv * m_ref[...], axis=0, keepdims=True)

    return _call(body, name=name, out_shape=(_sds((R, D), BF16), _sds((1, D))), grid=(R // br,),
                 in_specs=[row, row, pl.BlockSpec((6, D), lambda i: (0, 0))], out_specs=(row, vec),
                 sem=("arbitrary",))(dx, m, mod)


def _ffn_fn(shifts, ug, uv, wg, wv, bg, bv):
    down, up = shifts

    def conv(x, w, b):
        return down(x) * w[0:1, :] + x * w[1:2, :] + up(x) * w[2:3, :] + b

    return _silu(conv(ug, wg, bg)) * conv(uv, wv, bv)


def _ffn_fwd(up, cw, cb, *, bw=256):
    N = up.shape[0]
    shifts = _make_shift(((0, N),))
    nb = DFF // bw

    def body(ug, uv, wg, wv, bg, bv, a_ref):
        a_ref[...] = _ffn_fn(shifts, ug[...], uv[...], wg[...], wv[...], bg[...], bv[...]).astype(BF16)

    def col(rows, off):
        return pl.BlockSpec((rows, bw), lambda j: (0, j + off))

    return _call(body, name="ffn_fwd", out_shape=_sds((N, DFF), BF16), grid=(nb,),
                 in_specs=[col(N, 0), col(N, nb), col(3, 0), col(3, nb), col(1, 0), col(1, nb)],
                 out_specs=col(N, 0), sem=("parallel",), vmem=VMEM_BIG)(up, up, cw, cw, cb, cb)


def _ffn_bwd(up, cw, cb, da, *, bw=256):
    N = up.shape[0]
    shifts = _make_shift(((0, N),))
    nb = DFF // bw

    def body(ug, uv, wg, wv, bg, bv, da_ref, dug, duv, dwg, dwv, dbg, dbv):
        _, vjp = jax.vjp(functools.partial(_ffn_fn, shifts), ug[...], uv[...], wg[...], wv[...], bg[...], bv[...])
        g = vjp(da_ref[...])
        dug[...] = g[0].astype(BF16)
        duv[...] = g[1].astype(BF16)
        dwg[...], dwv[...], dbg[...], dbv[...] = g[2], g[3], g[4], g[5]

    def col(rows, off):
        return pl.BlockSpec((rows, bw), lambda j: (0, j + off))

    half = (_sds((N, DFF), BF16), _sds((N, DFF), BF16), _sds((3, DFF)), _sds((3, DFF)), _sds((1, DFF)), _sds((1, DFF)))
    dug, duv, dwg, dwv, dbg, dbv = _call(
        body, name="ffn_bwd", out_shape=half, grid=(nb,),
        in_specs=[col(N, 0), col(N, nb), col(3, 0), col(3, nb), col(1, 0), col(1, nb), col(N, 0)],
        out_specs=(col(N, 0), col(N, 0), col(3, 0), col(3, 0), col(1, 0), col(1, 0)),
        sem=("parallel",), vmem=VMEM_BIG)(up, up, cw, cw, cb, cb, da)
    return (jnp.concatenate([dug, duv], axis=1), jnp.concatenate([dwg, dwv], axis=1),
            jnp.concatenate([dbg, dbv], axis=1))


def _head_fn(x1, dn, g2, fw, tgt):
    y = _rms(x1 + g2 * dn) * fw
    err = y - tgt
    return 0.5 * jnp.sum(jnp.mean(err * err, axis=-1))


def _head(x1, dn, mod, fw, tgt, *, br=256):
    N = x1.shape[0]
    row = pl.BlockSpec((br, D), lambda i: (i, 0))
    vec = pl.BlockSpec((1, D), lambda i: (0, 0))
    one = pl.BlockSpec((1, HD), lambda i: (0, 0))

    def body(x1_ref, dn_ref, mod_ref, fw_ref, tgt_ref, loss_ref, dx_ref, ddn_ref, dg_ref, dfw_ref):
        loss, (gx, gdn, gg, gfw) = jax.value_and_grad(_head_fn, argnums=(0, 1, 2, 3))(
            x1_ref[...], dn_ref[...], mod_ref[5:6, :], fw_ref[...], tgt_ref[...])
        dx_ref[...] = gx
        ddn_ref[...] = gdn.astype(BF16)

        @pl.when(pl.program_id(0) == 0)
        def _():
            loss_ref[...] = jnp.zeros_like(loss_ref)
            dg_ref[...] = jnp.zeros_like(dg_ref)
            dfw_ref[...] = jnp.zeros_like(dfw_ref)

        loss_ref[...] += jnp.broadcast_to(loss, (1, HD))
        dg_ref[...] += gg
        dfw_ref[...] += gfw

    return _call(body, name="head", out_shape=(_sds((1, HD)), _sds((N, D)), _sds((N, D), BF16), _sds((1, D)), _sds((1, D))),
                 grid=(N // br,), in_specs=[row, row, pl.BlockSpec((6, D), lambda i: (0, 0)), vec, row],
                 out_specs=(one, row, row, vec, vec), sem=("arbitrary",))(x1, dn, mod, fw, tgt)


def _adamw(w, g, m, v, *, name):
    shape = w.shape
    cols = shape[-1]
    rows = max(1, math.prod(shape[:-1]))
    w2, g2, m2, v2 = (t.reshape(rows, cols) for t in (w, g, m, v))
    br = 256 if rows % 256 == 0 else rows
    c1 = 1.0 - B1 ** STEP
    c2 = 1.0 - B2 ** STEP

    def body(w_ref, g_ref, m_ref, v_ref, d_ref, nm_ref, nv_ref):
        gv = g_ref[...]
        nm = B1 * m_ref[...] + (1.0 - B1) * gv
        nv = B2 * v_ref[...] + (1.0 - B2) * (gv * gv)
        d_ref[...] = -LR * ((nm / c1) / (jnp.sqrt(nv / c2) + AEPS) + WD * w_ref[...])
        nm_ref[...] = nm
        nv_ref[...] = nv

    blk = pl.BlockSpec((br, cols), lambda i: (i, 0))
    outs = _call(body, name=name, out_shape=(_sds((rows, cols)),) * 3, grid=(rows // br,),
                 in_specs=[blk] * 4, out_specs=(blk,) * 3, sem=("parallel",))(w2, g2, m2, v2)
    return tuple(t.reshape(shape) for t in outs)


def _adamw_many(items, *, name):
    k = len(items)
    shapes = [w.shape for w, _, _, _ in items]
    flat = [t.reshape(max(1, math.prod(t.shape[:-1])), t.shape[-1]) for it in items for t in it]
    c1 = 1.0 - B1 ** STEP
    c2 = 1.0 - B2 ** STEP

    def body(*refs):
        ins, outs = refs[:4 * k], refs[4 * k:]
        for i in range(k):
            w_ref, g_ref, m_ref, v_ref = ins[4 * i:4 * i + 4]
            gv = g_ref[...]
            nm = B1 * m_ref[...] + (1.0 - B1) * gv
            nv = B2 * v_ref[...] + (1.0 - B2) * (gv * gv)
            outs[3 * i][...] = -LR * ((nm / c1) / (jnp.sqrt(nv / c2) + AEPS) + WD * w_ref[...])
            outs[3 * i + 1][...] = nm
            outs[3 * i + 2][...] = nv

    res = _call(body, name=name, out_shape=tuple(_sds(flat[4 * i].shape) for i in range(k) for _ in range(3)))(*flat)
    return [tuple(res[3 * i + j].reshape(shapes[i]) for j in range(3)) for i in range(k)]


def _rope_tables(N, L):
    t = jnp.arange(N)
    pos = jnp.stack([(t // GRID_W).astype(F32), (t % GRID_W).astype(F32)], axis=1)
    inv = ROPE_THETA ** (-jnp.arange(0, HD // 2, 2, dtype=F32) / (HD // 2))
    ang = pos[:, :, None] * inv[None, None, :]
    cos = jnp.broadcast_to(jnp.cos(ang)[:, :, None, :], (N, 2, 2, HD // 4)).reshape(N, HD)
    sin = jnp.broadcast_to(jnp.sin(ang)[:, :, None, :], (N, 2, 2, HD // 4))
    sin = (sin * jnp.array([-1.0, 1.0], F32)[None, None, :, None]).reshape(N, HD)
    cos = jnp.concatenate([jnp.ones((L, HD), F32), cos], axis=0)
    sin = jnp.concatenate([jnp.zeros((L, HD), F32), sin], axis=0)
    return cos, sin


def _pad_lanes(v, off=0):
    return jnp.zeros((1, HD), F32).at[0, off:off + v.shape[0]].set(v)


def _local_step(x, ctx, tgt, mod_lat, mod_ctx, w_in, shards, small):
    N, L = x.shape[0], ctx.shape[0]
    T = N + L
    bounds = ((0, L), (L, T))
    qw, kw, gw = small["q_norm_w"], small["k_norm_w"], small["gdn_norm_w"]
    conv_w, ffn_w, ffn_b, fnw = small["conv_qkv_w"], small["ffn_conv_w"], small["ffn_conv_b"], small["final_norm_w"]
    alog = _pad_lanes(small["a_log"].reshape(-1), 2 * GH)
    dtb = _pad_lanes(small["dt_bias"].reshape(-1), 2 * GH)
    cos, sin = _rope_tables(N, L)
    bt = T
    bnl = 256 if N % 1024 else 1024

    hc = _normmod_fwd(ctx, mod_ctx, 0, 1, name="normmod_ctx")
    hx = _normmod_fwd(x, mod_lat, 0, 1, name="normmod_x")
    h1 = jnp.concatenate([hc, hx], axis=0)
    proj = _mm(h1, w_in, name="mm_in", M=T, N=C_END, K=D, tb=True, bm=bt, bn=1024)
    aq, ak, av = _aprep_fwd(proj, cos, sin, qw, kw)
    (attn, attn32, lse), (up_g,) = _attn_fwd(aq, ak, av, L, _GatherTwoLevel([shards["w_up"]]))
    gq = _gprep_fwd(proj, conv_w, 0, bounds)
    gk = _gprep_fwd(proj, conv_w, 1, bounds)
    gv = _gprep_fwd(proj, conv_w, 2, bounds)
    bl = _bl_fwd(proj, alog, dtb)
    intra, (down_g, pa_g, pd_g, out_g) = _intra_fwd(
        gq, gk, gv, bl, L, _GatherTwoLevel([shards[n] for n in ("w_down", "w_pa", "w_pd", "w_out")]))
    w_up, w_down = up_g.reshape(2 * DFF, D), down_g.reshape(DFF, D)
    w_pa, w_pd, w_out = pa_g.reshape(D, D), pd_g.reshape(D, D), out_g.reshape(D, D)
    xinv, intra = intra[6], intra[:6]
    o, states = _scan_fwd(*intra, L)
    gdn = _gout_fwd(o, proj, gw, L)
    pa = _mm(attn, w_pa, name="mm_pa", M=N, N=D, K=D, bm=bnl)
    pd = _mm(gdn, w_pd, name="mm_pd", M=N, N=D, K=D, bm=bnl)
    y = _merge_fwd(pa, pd, proj, L)
    m = _mm(y, w_out, name="mm_out", M=N, N=D, K=D, bm=bnl)
    x1 = _resid_fwd(x, m, mod_lat, 2, name="resid1")
    h2 = _normmod_fwd(x1, mod_lat, 3, 4, name="normmod_x1")
    up = _mm(h2, w_up, name="mm_up", M=N, N=2 * DFF, K=D, tb=True, bm=bnl, bn=2 * DFF // 4)
    a = _ffn_fwd(up, ffn_w, ffn_b)
    dn = _mm(a, w_down, name="mm_down", M=N, N=D, K=DFF, bm=bnl)
    loss, dx2, ddn, dg2, dfnw = _head(x1, dn, mod_lat, fnw, tgt)

    da = _mm(ddn, w_down, name="mm_down_dx", M=N, N=DFF, K=D, tb=True, bm=bnl, bn=DFF // 2)
    g_down = _mm(a, ddn, name="mm_down_dw", M=DFF, N=D, K=N, ta=True, bm=DFF // 2, out_dtype=BF16)
    dup, d_ffn_w, d_ffn_b = _ffn_bwd(up, ffn_w, ffn_b, da)
    dh2 = _mm(dup, w_up, name="mm_up_dx", M=N, N=D, K=2 * DFF, bm=bnl, bk=2 * DFF // 4)
    g_up = _mm(dup, h2, name="mm_up_dw", M=2 * DFF, N=D, K=N, ta=True, bm=2 * DFF // 4, out_dtype=BF16)
    dx1, dsh2, dsc2 = _normmod_bwd(x1, mod_lat, 3, 4, dh2, 0, dx2, name="normmod_x1_bwd")
    dm, dg1 = _resid_bwd(dx1, m, mod_lat, 2, name="resid1_bwd")
    dy = _mm(dm, w_out, name="mm_out_dx", M=N, N=D, K=D, tb=True, bm=bnl)
    g_out = _mm(y, dm, name="mm_out_dw", M=D, N=D, K=N, ta=True, out_dtype=BF16)
    dpa, dpd, dproj = _merge_bwd(pa, pd, proj, dy, L)
    dattn = _mm(dpa, w_pa, name="mm_pa_dx", M=N, N=D, K=D, tb=True, bm=bnl)
    g_pa = _mm(attn, dpa, name="mm_pa_dw", M=D, N=D, K=N, ta=True, out_dtype=BF16)
    dgdn = _mm(dpd, w_pd, name="mm_pd_dx", M=N, N=D, K=D, tb=True, bm=bnl)
    g_pd = _mm(gdn, dpd, name="mm_pd_dw", M=D, N=D, K=N, ta=True, out_dtype=BF16)
    do, dproj, dgw = _gout_bwd(o, proj, gw, dgdn, dproj, L)
    cts, recv_a = _scan_bwd(*intra, states, do, L, _Exchange(
        [g_out.reshape(NDEV, D // NDEV, D), g_down.reshape(NDEV, DFF // NDEV, D)], True))
    (dgq, dgk, dgv, dbl), recv_b = _intra_bwd(gq, gk, gv, bl, xinv, cts, L, _Exchange(
        [g_pa.reshape(NDEV, D // NDEV, D), g_pd.reshape(NDEV, D // NDEV, D), g_up.reshape(NDEV, 2 * DFF // NDEV, D)], True))
    recv = dict(zip(("w_out", "w_down", "w_pa", "w_pd", "w_up"), recv_a + recv_b))
    dproj, dwq = _gprep_bwd(proj, conv_w, 0, bounds, dgq, dproj)
    dproj, dwk = _gprep_bwd(proj, conv_w, 1, bounds, dgk, dproj)
    dproj, dwv = _gprep_bwd(proj, conv_w, 2, bounds, dgv, dproj)
    dproj, dalog, ddtb = _bl_bwd(proj, alog, dtb, dbl, dproj)
    daq_h, dak_h, dav_h = _attn_bwd(aq, ak, av, attn32, lse, dattn, L)
    dproj, dqw, dkw = _aprep_bwd(proj, cos, sin, qw, kw, daq_h, dak_h, dav_h, dproj, L)
    g_in = _mm(dproj, h1, name="mm_in_dw", M=C_END, N=D, K=T, ta=True, bm=1024, out_dtype=BF16)
    g_in = _unpad_columns(g_in).reshape(NDEV, W_END // NDEV, D)
    own_in = lax.dynamic_index_in_dim(g_in, _position()[3], axis=0, keepdims=False)
    *pending, token = _scatter_start(g_in, None, (0, D // 2), (), name="scatter_g_in_a_start")
    dh1 = _mm(dproj, w_in, name="mm_in_dx", M=T, N=D, K=C_END, bm=bt, bk=1024, after=(token,))
    grad_x, dsh1, dsc1 = _normmod_bwd(x, mod_lat, 0, 1, dh1, L, dx1, name="normmod_x_bwd")
    _, dcsh1, dcsc1 = _normmod_bwd(ctx, mod_ctx, 0, 1, dh1, 0, None, name="normmod_ctx_bwd")

    z1 = jnp.zeros((1, D), F32)
    dmod_lat = jnp.concatenate([dsh1, dsc1, dg1, dsh2, dsc2, dg2], axis=0)
    dmod_ctx = jnp.concatenate([dcsh1, dcsc1, z1, z1, z1, z1], axis=0)
    gsmall = {
        "q_norm_w": dqw, "k_norm_w": dkw, "gdn_norm_w": dgw,
        "conv_qkv_w": jnp.concatenate([dwq, dwk, dwv], axis=1),
        "a_log": dalog[0, 2 * GH:4 * GH], "dt_bias": ddtb[0, 2 * GH:4 * GH],
        "ffn_conv_w": d_ffn_w, "ffn_conv_b": d_ffn_b, "final_norm_w": dfnw,
    }
    return loss[0, 0], grad_x, (pending, own_in), recv, dmod_lat, dmod_ctx, gsmall


HBM = pl.BlockSpec(memory_space=pltpu.HBM)
ANYSPEC = pl.BlockSpec(memory_space=pl.ANY)


def _position():
    x, y, c = lax.axis_index("x"), lax.axis_index("y"), lax.axis_index("c")
    return x, y, c, 4 * x + 2 * y + c


def _peer(x, y, c, k):
    px = 1 - x if k & 4 else x
    py = 1 - y if k & 2 else y
    pc = 1 - c if k & 1 else c
    return (px, py, pc), 4 * px + 2 * py + pc


def _exchange(arrs, *, name, scatter):
    exch = _Exchange(arrs, scatter)
    n = exch.n

    def body(*refs):
        ins, outs, sems = refs[:n], refs[n:2 * n], refs[2 * n:]
        exch.start(ins, outs, sems)
        exch.finish(ins, outs, sems)

    outs = pl.pallas_call(body, name=name, out_shape=exch.out_shape, in_specs=[HBM] * n, out_specs=(HBM,) * n,
                          scratch_shapes=exch.scratch,
                          compiler_params=pltpu.CompilerParams(has_side_effects=True))(*arrs)
    return list(outs)


class _Exchange:
    def __init__(self, arrs, scatter):
        self.arrs, self.scatter, self.n = list(arrs), scatter, len(arrs)
        self.out_shape = tuple(_sds(a.shape if scatter else (NDEV,) + a.shape, a.dtype) for a in arrs)
        self.scratch = [pltpu.SemaphoreType.DMA((self.n, NDEV - 1)), pltpu.SemaphoreType.DMA((self.n, NDEV - 1)),
                        pltpu.SemaphoreType.DMA((self.n,))]

    def _copies(self, ins, outs, sems):
        send, recv, loc = sems
        x, y, c, me = _position()
        local = [pltpu.make_async_copy(ins[a].at[me] if self.scatter else ins[a], outs[a].at[me], loc.at[a])
                 for a in range(self.n)]
        remote = []
        for k in range(1, NDEV):
            peer, pid = _peer(x, y, c, k)
            for a in range(self.n):
                src = ins[a].at[pid] if self.scatter else ins[a]
                remote.append(pltpu.make_async_remote_copy(
                    src_ref=src, dst_ref=outs[a].at[me], send_sem=send.at[a, k - 1], recv_sem=recv.at[a, k - 1],
                    device_id=peer, device_id_type=MESH))
        return local, remote

    def start(self, ins, outs, sems):
        local, remote = self._copies(ins, outs, sems)
        for cp in local + remote:
            cp.start()

    def finish(self, ins, outs, sems):
        local, remote = self._copies(ins, outs, sems)
        for cp in remote:
            cp.wait()
        for cp in local:
            cp.wait()


class _GatherTwoLevel:
    scatter = False

    def __init__(self, arrs):
        self.arrs, self.n = list(arrs), len(arrs)
        self.out_shape = tuple(_sds((NDEV,) + a.shape, a.dtype) for a in arrs)
        self.scratch = [pltpu.SemaphoreType.DMA((self.n, NDEV - 1)), pltpu.SemaphoreType.DMA((self.n, NDEV - 1)),
                        pltpu.SemaphoreType.DMA((self.n,))]

    def _parts(self, ins, outs, sems):
        send, recv, loc = sems
        x, y, c, _ = _position()
        me, sibling = (x, y, c), (x, y, 1 - c)
        chips = [(1 - x, y), (x, 1 - y), (1 - x, 1 - y)]
        parts = []
        for a in range(self.n):
            slot = lambda px, py, pc, a=a: outs[a].at[4 * px + 2 * py + pc]

            def copy(k, owner, to, src=None, a=a, slot=slot):
                return pltpu.make_async_remote_copy(
                    src_ref=slot(*owner) if src is None else src, dst_ref=slot(*owner), send_sem=send.at[a, k],
                    recv_sem=recv.at[a, k], device_id=to, device_id_type=MESH)

            parts.append(dict(
                mine=pltpu.make_async_copy(ins[a], slot(*me), loc.at[a]),
                first=[copy(0, me, sibling, src=ins[a])] + [copy(1 + j, me, (*ch, c), src=ins[a]) for j, ch in enumerate(chips)],
                arrive=[copy(1 + j, (*ch, c), me) for j, ch in enumerate(chips)],
                passed=[copy(4 + j, (*ch, c), sibling) for j, ch in enumerate(chips)],
                rest=[copy(0, sibling, me)] + [copy(4 + j, (*ch, 1 - c), me) for j, ch in enumerate(chips)]))
        return parts

    def start(self, ins, outs, sems):
        for p in self._parts(ins, outs, sems):
            p["mine"].start()
            for cp in p["first"]:
                cp.start()

    def middle(self, ins, outs, sems):
        for p in self._parts(ins, outs, sems):
            for got, fwd in zip(p["arrive"], p["passed"]):
                got.wait_recv()
                fwd.start()

    def finish(self, ins, outs, sems):
        for p in self._parts(ins, outs, sems):
            for cp in p["rest"]:
                cp.wait_recv()
            for cp in p["first"] + p["passed"]:
                cp.wait_send()
            p["mine"].wait()


def _gather_two_level(block, *, name):
    def body(x_ref, out_ref, send_sems, recv_sems, local_sem):
        x, y, c, _ = _position()
        me, sibling = (x, y, c), (x, y, 1 - c)
        chips = [(1 - x, y), (x, 1 - y), (1 - x, 1 - y)]

        def slot(px, py, pc):
            return out_ref.at[4 * px + 2 * py + pc]

        def copy(k, owner, to, src=None):
            return pltpu.make_async_remote_copy(
                src_ref=slot(*owner) if src is None else src, dst_ref=slot(*owner), send_sem=send_sems.at[k],
                recv_sem=recv_sems.at[k], device_id=to, device_id_type=MESH)

        mine = pltpu.make_async_copy(x_ref, slot(*me), local_sem)
        mine.start()
        first = [copy(0, me, sibling, src=x_ref)]
        first += [copy(1 + j, me, (*chip, c), src=x_ref) for j, chip in enumerate(chips)]
        for cp in first:
            cp.start()
        passed = [copy(4 + j, (*chip, c), sibling) for j, chip in enumerate(chips)]
        for j, chip in enumerate(chips):
            copy(1 + j, (*chip, c), me).wait_recv()
            passed[j].start()
        copy(0, sibling, me).wait_recv()
        for j, chip in enumerate(chips):
            copy(4 + j, (*chip, 1 - c), me).wait_recv()
        for cp in first + passed:
            cp.wait_send()
        mine.wait()

    return pl.pallas_call(
        body, name=name, out_shape=_sds((NDEV,) + block.shape, block.dtype), in_specs=[HBM], out_specs=HBM,
        scratch_shapes=[pltpu.SemaphoreType.DMA((NDEV - 1,)), pltpu.SemaphoreType.DMA((NDEV - 1,)),
                        pltpu.SemaphoreType.DMA],
        compiler_params=pltpu.CompilerParams(has_side_effects=True))(block)


SEM = pl.BlockSpec(memory_space=pltpu.SEMAPHORE)


def _scatter_copies(src_ref, land_ref, send_sems, recv_sems, cols):
    x, y, c, me = _position()
    span = (slice(None), pl.ds(*cols))
    copies = []
    for k in range(1, NDEV):
        peer, pid = _peer(x, y, c, k)
        copies.append(pltpu.make_async_remote_copy(
            src_ref=src_ref.at[pid].at[span], dst_ref=land_ref.at[me].at[span], send_sem=send_sems.at[k - 1],
            recv_sem=recv_sems.at[k - 1], device_id=peer, device_id_type=MESH))
    return copies


SPLIT_EFFECT = pltpu.SideEffectType.DATAFLOW_SIDE_EFFECTING


def _scatter_start(parts, land, cols, after, *, name):
    na = len(after)
    if land is None:
        land = lax.empty(parts.shape, parts.dtype)

    def body(src_ref, land_ref, *rest):
        send_sems, recv_sems, _, _, token = rest[na:]
        for cp in _scatter_copies(src_ref, land_ref, send_sems, recv_sems, cols):
            cp.start()
        token[...] = jnp.zeros_like(token)

    return pl.pallas_call(
        body, name=name,
        out_shape=(pltpu.SemaphoreType.DMA((NDEV - 1,)), pltpu.SemaphoreType.DMA((NDEV - 1,)),
                   pltpu.HBM(parts.shape, parts.dtype), pltpu.HBM(parts.shape, parts.dtype), _sds((8, HD))),
        in_specs=(HBM, HBM) + (pl.BlockSpec(memory_space=pl.ANY),) * na,
        out_specs=(SEM, SEM, HBM, HBM, pl.BlockSpec(memory_space=pltpu.VMEM)),
        input_output_aliases={0: 2, 1: 3}, compiler_params=pltpu.CompilerParams(has_side_effects=SPLIT_EFFECT),
    )(pltpu.with_memory_space_constraint(parts, pltpu.HBM), pltpu.with_memory_space_constraint(land, pltpu.HBM), *after)


def _scatter_wait(send_sems, recv_sems, src_thru, land_thru, cols, after, *, name):
    na = len(after)

    def body(src_ref, land_ref, send_sems, recv_sems, *rest):
        for cp in _scatter_copies(src_ref, land_ref, send_sems, recv_sems, cols):
            cp.wait_send()
            cp.wait_recv()

    return pl.pallas_call(
        body, name=name,
        out_shape=(pltpu.HBM(src_thru.shape, src_thru.dtype), pltpu.HBM(land_thru.shape, land_thru.dtype)),
        in_specs=(HBM, HBM, SEM, SEM) + (pl.BlockSpec(memory_space=pl.ANY),) * na, out_specs=(HBM, HBM),
        input_output_aliases={0: 0, 1: 1}, compiler_params=pltpu.CompilerParams(has_side_effects=SPLIT_EFFECT),
    )(src_thru, land_thru, send_sems, recv_sems, *after)


def _cast_bf16(w, *, name):
    rows, cols = w.shape
    br = 128 if rows % 128 == 0 else rows

    def body(w_ref, o_ref):
        o_ref[...] = w_ref[...].astype(BF16)

    blk = pl.BlockSpec((br, cols), lambda i: (i, 0))
    return _call(body, name=name, out_shape=_sds((rows, cols), BF16), grid=(rows // br,), in_specs=[blk],
                 out_specs=blk, sem=("parallel",))(w)


def _sum_slots(a, *, name):
    _, R, C = a.shape

    def body(a_ref, o_ref):
        s = a_ref[0]
        for d in range(1, NDEV):
            s = s + a_ref[d]
        o_ref[...] = s

    return _call(body, name=name, out_shape=_sds((R, C)))(a)


MODROWS = 16


def _mod_fwd(c9, w, b):
    cols = w.shape[1]

    def body(c_ref, w_ref, b_ref, o_ref):
        o_ref[...] = _nn(_silu(c_ref[...]), w_ref[...]) + b_ref[...]

    return _call(body, name="mod_fwd", out_shape=_sds((MODROWS, cols)))(c9, w, b)


def _mod_bwd(c9, dmy, dall, w):
    cols = w.shape[1]

    def body(c_ref, dmy_ref, dall_ref, w_ref, gw_ref, gb_ref, cp_ref):
        sc = _silu(c_ref[...])
        rows = lax.broadcasted_iota(jnp.int32, (MODROWS, 1), 0)
        d = dmy_ref[...]
        d_ctx = jnp.where(rows == NDEV, d, 0.0)
        sc_ctx = jnp.where(rows == NDEV, sc, 0.0)
        outer = lax.dot_general(sc_ctx, d_ctx, (((0,), (0,)), ((), ())), precision=HI, preferred_element_type=F32)
        gw_ref[...] = _tn(jnp.where(rows < NDEV, sc, 0.0), jnp.where(rows < NDEV, d, 0.0)) + outer
        gb_ref[...] = jnp.sum(dall_ref[...], axis=0, keepdims=True)
        cp_ref[...] = jnp.sum(_nt(d_ctx, w_ref[...]), axis=0, keepdims=True)

    return _call(body, name="mod_bwd", out_shape=(_sds((D, cols)), _sds((1, 6 * D)), _sds((1, D))),
                 vmem=VMEM_BIG)(c9, dmy, dall, w)


def _cctx_finish(parts, c_ctx, after):
    VM = pl.BlockSpec(memory_space=pltpu.VMEM)

    def body(p_ref, c_ref, *rest):
        o_ref = rest[-1]
        s = p_ref[0]
        for d in range(1, NDEV):
            s = s + p_ref[d]
        _, vjp = jax.vjp(_silu, c_ref[...])
        o_ref[...] = vjp(s)[0]

    return _call(body, name="cctx_finish", out_shape=_sds((1, D)),
                 in_specs=[VM, VM] + [pl.BlockSpec(memory_space=pl.ANY)] * len(after))(parts, c_ctx, *after)


def _adamw_recv(w, recv, m, v, *, name, own=None):
    rows, cols = w.shape
    bc = 256
    c1 = 1.0 - B1 ** STEP
    c2 = 1.0 - B2 ** STEP
    has_own = own is not None

    def body(w_ref, r_ref, m_ref, v_ref, *rest):
        g_ref, d_ref, nm_ref, nv_ref = rest[-4:]
        me = _position()[3]

        def slot(d):
            return jnp.where(me == d, rest[0][...], r_ref[d]) if has_own else r_ref[d]

        gv = slot(0).astype(F32)
        for d in range(1, NDEV):
            gv = gv + slot(d).astype(F32)
        nm = B1 * m_ref[...] + (1.0 - B1) * gv
        nv = B2 * v_ref[...] + (1.0 - B2) * (gv * gv)
        g_ref[...] = gv
        d_ref[...] = -LR * ((nm / c1) / (jnp.sqrt(nv / c2) + AEPS) + WD * w_ref[...])
        nm_ref[...] = nm
        nv_ref[...] = nv

    blk = pl.BlockSpec((rows, bc), lambda j: (0, j))
    return _call(body, name=name, out_shape=(_sds((rows, cols)),) * 4, grid=(cols // bc,),
                 in_specs=[blk, pl.BlockSpec((NDEV, rows, bc), lambda j: (0, 0, j)), blk, blk] + [blk] * has_own,
                 out_specs=(blk,) * 4, sem=("parallel",), vmem=VMEM_BIG)(w, recv, m, v, *([own] if has_own else []))


P_LAT, P_CTX, P_FNW, P_FFNB, P_CONV, P_FFNW, P_MISC, P_ROWS = 0, 8, 16, 24, 32, 48, 72, 80


def _rows_of(v, nrows):
    flat = v.reshape(-1)
    return jnp.pad(flat, (0, nrows * D - flat.shape[0])).reshape(nrows, D)


def _by_columns(g):
    n, r, c = g.shape
    return jnp.transpose(g, (1, 0, 2)).reshape(r, n * c)


def kernel(x, c, ctx, c_ctx, w_mod, b_mod, w_in, q_norm_w, k_norm_w, conv_qkv_w, a_log, dt_bias, gdn_norm_w, w_pa, w_pd, w_out, w_up, ffn_conv_w, ffn_conv_b, w_down, final_norm_w, loss_target, m_c_ctx, m_w_mod, m_b_mod, m_w_in, m_q_norm_w, m_k_norm_w, m_conv_qkv_w, m_a_log, m_dt_bias, m_gdn_norm_w, m_w_pa, m_w_pd, m_w_out, m_w_up, m_ffn_conv_w, m_ffn_conv_b, m_w_down, m_final_norm_w, v_c_ctx, v_w_mod, v_b_mod, v_w_in, v_q_norm_w, v_k_norm_w, v_conv_qkv_w, v_a_log, v_dt_bias, v_gdn_norm_w, v_w_pa, v_w_pd, v_w_out, v_w_up, v_ffn_conv_w, v_ffn_conv_b, v_w_down, v_final_norm_w):
    _, _, _, me = _position()
    mcols = w_mod.shape[2]

    transposed = ("w_in", "w_up")
    big = {"w_in": w_in[0].T, "w_pa": w_pa[0], "w_pd": w_pd[0], "w_out": w_out[0], "w_up": w_up[0].T, "w_down": w_down[0]}
    names = list(big)
    shards = {n: _cast_bf16(big[n], name="cast_" + n) for n in names}
    w_in_g = _gather_two_level(shards["w_in"], name="gather_w_in")
    c_all, conv_g, ffnw_g = _exchange([c, conv_qkv_w[0], ffn_conv_w[0]], name="gather_small", scatter=False)
    w_in_full = w_in_g.reshape(W_END, D)
    w_in_pad = _pad_columns(w_in_full)

    c9 = jnp.concatenate([c_all.reshape(NDEV, D), jnp.pad(c_ctx[None], ((0, MODROWS - NDEV - 1), (0, 0)))], axis=0)
    b_loc = lax.dynamic_slice(b_mod, (0, me * mcols), (1, mcols))
    mod_all, = _exchange([_mod_fwd(c9, w_mod[0], b_loc)], name="gather_mod", scatter=False)
    mod_lat = lax.dynamic_index_in_dim(mod_all, me, axis=1, keepdims=False).reshape(6, D)
    mod_ctx = mod_all[:, NDEV, :].reshape(6, D)

    small = {"q_norm_w": q_norm_w, "k_norm_w": k_norm_w, "gdn_norm_w": gdn_norm_w, "a_log": a_log, "dt_bias": dt_bias,
             "conv_qkv_w": _by_columns(conv_g), "ffn_conv_w": _by_columns(ffnw_g), "ffn_conv_b": ffn_conv_b,
             "final_norm_w": final_norm_w[None]}
    loss_me, grad_x, (pending_in, own_in), recv, dmod_lat, dmod_ctx, gs = _local_step(
        x[0], ctx[0], loss_target[0], mod_lat, mod_ctx, w_in_pad, shards, small)

    moments = {"w_in": (m_w_in, v_w_in), "w_pa": (m_w_pa, v_w_pa), "w_pd": (m_w_pd, v_w_pd),
               "w_out": (m_w_out, v_w_out), "w_up": (m_w_up, v_w_up), "w_down": (m_w_down, v_w_down)}
    res = {}
    def finish(n, outs):
        return tuple((t.T if n in transposed else t)[None] for t in outs)

    def moment(t, n):
        return t[0].T if n in transposed else t[0]

    for n in recv:
        res[n] = finish(n, _adamw_recv(big[n], recv[n], moment(moments[n][0], n), moment(moments[n][1], n),
                                       name="adamw_" + n))

    misc = jnp.concatenate([gs["q_norm_w"][0], gs["k_norm_w"][0], gs["gdn_norm_w"][0], gs["a_log"], gs["dt_bias"],
                            loss_me[None]])
    pack = jnp.concatenate([_rows_of(dmod_lat, P_CTX - P_LAT), _rows_of(dmod_ctx, P_FNW - P_CTX),
                            _rows_of(gs["final_norm_w"], P_FFNB - P_FNW), _rows_of(gs["ffn_conv_b"], P_CONV - P_FFNB),
                            _rows_of(gs["conv_qkv_w"], P_FFNW - P_CONV), _rows_of(gs["ffn_conv_w"], P_MISC - P_FFNW),
                            _rows_of(misc, P_ROWS - P_MISC)], axis=0)
    pack_all, = _exchange([pack], name="gather_pack", scatter=False)
    tot = _sum_slots(pack_all, name="sum_pack")
    dall = jnp.concatenate([pack_all[:, P_LAT:P_LAT + 6, :].reshape(NDEV, 6 * D),
                            jnp.pad(tot[P_CTX:P_CTX + 6].reshape(1, 6 * D), ((0, MODROWS - NDEV - 1), (0, 0)))], axis=0)
    dmy = lax.dynamic_slice(dall, (0, me * mcols), (MODROWS, mcols))
    g_w_mod, g_b_mod, cpart = _mod_bwd(c9, dmy, dall, w_mod[0])
    cparts, = _exchange([cpart], name="gather_cctx", scatter=False)
    sems_a, land = pending_in[:2], pending_in[3]
    *sems_b, g_in_thru, land, token_b = _scatter_start(pending_in[2], land, (D // 2, D // 2), (cparts,),
                                                       name="scatter_g_in_b_start")
    g_c_ctx = _cctx_finish(cparts, c_ctx[None], (token_b,))[0]

    nconv, nffn = 3 * GH * HD, 2 * DFF
    conv_tot = tot[P_CONV:P_FFNW].reshape(-1)[:3 * nconv].reshape(3, nconv)
    ffnw_tot = tot[P_FFNW:P_MISC].reshape(-1)[:3 * nffn].reshape(3, nffn)
    mrow = tot[P_MISC]
    grads = {
        "c_ctx": g_c_ctx, "w_mod": g_w_mod[None], "b_mod": g_b_mod,
        "q_norm_w": mrow[None, 0:HD], "k_norm_w": mrow[None, HD:2 * HD], "gdn_norm_w": mrow[None, 2 * HD:3 * HD],
        "conv_qkv_w": lax.dynamic_slice(conv_tot, (0, me * (nconv // NDEV)), (3, nconv // NDEV))[None],
        "a_log": mrow[3 * HD:3 * HD + 2 * GH].reshape(1, 2, GH),
        "dt_bias": mrow[3 * HD + 2 * GH:3 * HD + 4 * GH].reshape(1, 2, GH),
        "ffn_conv_w": lax.dynamic_slice(ffnw_tot, (0, me * (nffn // NDEV)), (3, nffn // NDEV))[None],
        "ffn_conv_b": tot[P_FFNB:P_CONV].reshape(-1)[:nffn][None],
        "final_norm_w": tot[P_FNW],
    }
    loss = mrow[3 * HD + 4 * GH]
    given = {"c_ctx": (c_ctx, m_c_ctx, v_c_ctx), "w_mod": (w_mod, m_w_mod, v_w_mod), "b_mod": (b_mod, m_b_mod, v_b_mod),
             "q_norm_w": (q_norm_w, m_q_norm_w, v_q_norm_w), "k_norm_w": (k_norm_w, m_k_norm_w, v_k_norm_w),
             "conv_qkv_w": (conv_qkv_w, m_conv_qkv_w, v_conv_qkv_w), "a_log": (a_log, m_a_log, v_a_log),
             "dt_bias": (dt_bias, m_dt_bias, v_dt_bias), "gdn_norm_w": (gdn_norm_w, m_gdn_norm_w, v_gdn_norm_w),
             "ffn_conv_w": (ffn_conv_w, m_ffn_conv_w, v_ffn_conv_w), "ffn_conv_b": (ffn_conv_b, m_ffn_conv_b, v_ffn_conv_b),
             "final_norm_w": (final_norm_w, m_final_norm_w, v_final_norm_w)}
    res["w_mod"] = (grads["w_mod"],) + _adamw(w_mod, grads["w_mod"], m_w_mod, v_w_mod, name="adamw_w_mod")
    small_names = [n for n in given if n != "w_mod"]
    updates = _adamw_many([(given[n][0], grads[n], given[n][1], given[n][2]) for n in small_names], name="adamw_small")
    for n, upd in zip(small_names, updates):
        res[n] = (grads[n],) + upd

    g_in_thru, land = _scatter_wait(*sems_a, g_in_thru, land, (0, D // 2), [res[n][1] for n in res],
                                    name="scatter_g_in_a_wait")
    _, land = _scatter_wait(*sems_b, g_in_thru, land, (D // 2, D // 2), (), name="scatter_g_in_b_wait")
    res["w_in"] = finish("w_in", _adamw_recv(big["w_in"], land, moment(m_w_in, "w_in"), moment(v_w_in, "w_in"),
                                             name="adamw_w_in", own=own_in))

    order = ["c_ctx", "w_mod", "b_mod", "w_in", "q_norm_w", "k_norm_w", "conv_qkv_w", "a_log", "dt_bias", "gdn_norm_w",
             "w_pa", "w_pd", "w_out", "w_up", "ffn_conv_w", "ffn_conv_b", "w_down", "final_norm_w"]
    return (loss, grad_x[None], *[res[n][0] for n in order], *[res[n][1] for n in order],
            *[res[n][2] for n in order], *[res[n][3] for n in order])
```

```python
import functools
import math

import jax
import jax.numpy as jnp
from jax import lax
from jax.experimental import pallas as pl
from jax.experimental.pallas import tpu as pltpu

F32 = jnp.float32
BF16 = jnp.bfloat16
HI = lax.Precision.HIGHEST
MESH = pl.DeviceIdType.MESH

NDEV = 8
D = 1024
HD = 128
AH, AKV, GRP = 8, 2, 4
GH = 8
CH = 64
DFF = 2816
GRID_W = 64
EPS = 1e-6
ROPE_THETA = 10000.0
LOG2E = math.log2(math.e)
C_KV, C_AQ, C_QKV, C_BL, C_Z, C_GATE, C_END = 0, 512, 1536, 4608, 5120, 6144, 8192
W_QKV, W_AQ, W_Z, W_END = 512, 3616, 4640, 7712


def _pad_columns(w):
    zeros = jnp.zeros((C_Z - C_QKV - (W_AQ - W_QKV), D), w.dtype)
    return jnp.concatenate([w[:W_QKV], w[W_AQ:W_Z], w[W_QKV:W_AQ], zeros, w[W_Z:]], axis=0)


def _unpad_columns(g):
    return jnp.concatenate([g[:C_AQ], g[C_QKV:C_QKV + W_AQ - W_QKV], g[C_AQ:C_QKV], g[C_Z:]], axis=0)
LR, B1, B2, AEPS, WD, STEP = 0.001, 0.9, 0.999, 1e-08, 0.01, 10
VMEM_BIG = 56 * 1024 * 1024
INTRA_FWD_CHUNKS = 36
INTRA_BWD_CHUNKS = 36


def _call(body, *, name, out_shape, grid=None, in_specs=None, out_specs=None, scratch=(), sem=None,
          vmem=None, aliases=None):
    params = {}
    if sem is not None:
        params["dimension_semantics"] = sem
    if vmem is not None:
        params["vmem_limit_bytes"] = vmem
    kw = {}
    if grid is not None:
        kw["grid"] = grid
    if in_specs is not None:
        kw["in_specs"] = in_specs
    if out_specs is not None:
        kw["out_specs"] = out_specs
    if aliases:
        kw["input_output_aliases"] = aliases
    return pl.pallas_call(body, name=name, out_shape=out_shape, scratch_shapes=list(scratch),
                          compiler_params=pltpu.CompilerParams(**params), **kw)


def _call_carrying(body, exch, *, name, out_shape, grid, in_specs, out_specs, scratch=(), vmem=None):
    n, nin, nout, nscr = exch.n, len(in_specs), len(out_shape), len(scratch)
    steps = math.prod(grid)
    mid = (2 * steps) // 3

    def wrapped(*refs):
        ins, cins = refs[:nin], refs[nin:nin + n]
        outs, couts = refs[nin + n:nin + n + nout], refs[nin + n + nout:nin + 2 * n + nout]
        scr, sems = refs[nin + 2 * n + nout:nin + 2 * n + nout + nscr], refs[nin + 2 * n + nout + nscr:]
        ids = [pl.program_id(i) for i in range(len(grid))]
        first = functools.reduce(jnp.logical_and, [i == 0 for i in ids])
        last = functools.reduce(jnp.logical_and, [i == g - 1 for i, g in zip(ids, grid)])

        @pl.when(first)
        def _():
            exch.start(cins, couts, sems)

        if hasattr(exch, "middle"):
            linear = functools.reduce(lambda acc, ig: acc * ig[1] + ig[0], zip(ids, grid), 0)

            @pl.when(linear == mid)
            def _():
                exch.middle(cins, couts, sems)

        body(*ins, *outs, *scr)

        @pl.when(last)
        def _():
            exch.finish(cins, couts, sems)

    params = {"dimension_semantics": ("arbitrary",) * len(grid)}
    if vmem is not None:
        params["vmem_limit_bytes"] = vmem
    fn = pl.pallas_call(wrapped, name=name, out_shape=tuple(out_shape) + exch.out_shape, grid=grid,
                        in_specs=list(in_specs) + [HBM] * n, out_specs=tuple(out_specs) + (HBM,) * n,
                        scratch_shapes=list(scratch) + exch.scratch, compiler_params=pltpu.CompilerParams(**params))

    def run(*args):
        res = fn(*args, *exch.arrs)
        return res[:nout], list(res[nout:])

    return run


def _sds(shape, dtype=F32):
    return jax.ShapeDtypeStruct(tuple(shape), dtype)


def _dot(a, b, ca, cb):
    return lax.dot_general(a.astype(BF16), b.astype(BF16), (((ca,), (cb,)), ((), ())),
                           preferred_element_type=F32)


@jax.custom_vjp
def _nn(a, b):
    return _dot(a, b, 1, 0)


@jax.custom_vjp
def _nt(a, b):
    return _dot(a, b, 1, 1)


@jax.custom_vjp
def _tn(a, b):
    return _dot(a, b, 0, 0)


_nn.defvjp(lambda a, b: (_nn(a, b), (a, b)), lambda r, g: (_nt(g, r[1]), _tn(r[0], g)))
_nt.defvjp(lambda a, b: (_nt(a, b), (a, b)), lambda r, g: (_nn(g, r[1]), _tn(g, r[0])))
_tn.defvjp(lambda a, b: (_tn(a, b), (a, b)), lambda r, g: (_nt(r[1], g), _nn(r[0], g)))


def _mdot(a, b):
    return jnp.dot(a, b, precision=lax.Precision.HIGH, preferred_element_type=F32)


def _maskdot(mask, a, cm):
    hi = a.astype(BF16)
    r = a - hi.astype(F32)
    mid = r.astype(BF16)
    lo = (r - mid.astype(F32)).astype(BF16)
    mb = mask.astype(BF16)
    dims = (((cm,), (0,)), ((), ()))
    return (lax.dot_general(mb, hi, dims, preferred_element_type=F32)
            + lax.dot_general(mb, mid, dims, preferred_element_type=F32)
            + lax.dot_general(mb, lo, dims, preferred_element_type=F32))


@jax.custom_vjp
def _mask_nn(mask, a):
    return _maskdot(mask, a, 1)


_mask_nn.defvjp(lambda mask, a: (_maskdot(mask, a, 1), mask),
                lambda mask, g: (jnp.zeros_like(mask), _maskdot(mask, g, 0)))


@jax.custom_vjp
def _saved_inverse(lmat, x):
    return x


def _saved_inverse_bwd(x, g):
    t = lax.dot_general(x, g, (((0,), (0,)), ((), ())), precision=lax.Precision.HIGH, preferred_element_type=F32)
    dl = lax.dot_general(t, x, (((1,), (1,)), ((), ())), precision=lax.Precision.HIGH, preferred_element_type=F32)
    return -dl, jnp.zeros_like(x)


_saved_inverse.defvjp(lambda lmat, x: (x, x), _saved_inverse_bwd)


def _row_ids(shape):
    return lax.broadcasted_iota(jnp.int32, shape, 0)


def _shift_rows(x, down, bounds):
    n = x.shape[0]
    rows = _row_ids(x.shape)
    y = pltpu.roll(x, 1 if down else n - 1, 0)
    edge = functools.reduce(jnp.logical_or, [rows == (s if down else e - 1) for s, e in bounds])
    return jnp.where(edge, 0.0, y)


def _make_shift(bounds):
    @jax.custom_vjp
    def down(x):
        return _shift_rows(x, True, bounds)

    @jax.custom_vjp
    def up(x):
        return _shift_rows(x, False, bounds)

    down.defvjp(lambda x: (down(x), None), lambda _, g: (up(g),))
    up.defvjp(lambda x: (up(x), None), lambda _, g: (down(g),))
    return down, up


@jax.custom_vjp
def _swap32(x):
    lane = lax.broadcasted_iota(jnp.int32, x.shape, x.ndim - 1)
    return jnp.where((lane % 64) < 32, pltpu.roll(x, HD - 32, x.ndim - 1), pltpu.roll(x, 32, x.ndim - 1))


_swap32.defvjp(lambda x: (_swap32(x), None), lambda _, g: (_swap32(g),))


def _rms(x):
    return x * lax.rsqrt(jnp.mean(x * x, axis=-1, keepdims=True) + EPS)


def _silu(x):
    return x * jax.nn.sigmoid(x)


def _mm(a, b, *, name, M, N, K, ta=False, tb=False, out_dtype=F32, bm=None, bn=None, bk=None, after=()):
    bm, bn, bk = bm or M, bn or N, bk or K
    assert M % bm == 0 and N % bn == 0 and K % bk == 0, (name, M, N, K, bm, bn, bk)
    nk = K // bk
    ca, cb = (0 if ta else 1), (1 if tb else 0)
    na = len(after)

    def body(a_ref, b_ref, *rest):
        o_ref, acc = rest[na], rest[na + 1:]
        r = _dot(a_ref[...], b_ref[...], ca, cb)
        if nk == 1:
            o_ref[...] = r.astype(out_dtype)
        else:
            acc_ref, = acc
            k = pl.program_id(2)

            @pl.when(k == 0)
            def _():
                acc_ref[...] = r

            @pl.when(k > 0)
            def _():
                acc_ref[...] += r

            @pl.when(k == nk - 1)
            def _():
                o_ref[...] = acc_ref[...].astype(out_dtype)

    a_spec = pl.BlockSpec((bk, bm), lambda i, j, k: (k, i)) if ta else pl.BlockSpec((bm, bk), lambda i, j, k: (i, k))
    b_spec = pl.BlockSpec((bn, bk), lambda i, j, k: (j, k)) if tb else pl.BlockSpec((bk, bn), lambda i, j, k: (k, j))
    return _call(body, name=name, out_shape=_sds((M, N), out_dtype), grid=(M // bm, N // bn, nk),
                 in_specs=[a_spec, b_spec] + [pl.BlockSpec(memory_space=pl.ANY)] * na,
                 out_specs=pl.BlockSpec((bm, bn), lambda i, j, k: (i, j)),
                 scratch=[pltpu.VMEM((bm, bn), F32)] if nk > 1 else [],
                 sem=("parallel", "parallel", "arbitrary"), vmem=VMEM_BIG)(a, b, *after)


def _normmod_fn(x, sh, sc):
    return _rms(x) * (1.0 + sc) + sh


def _normmod_fwd(x, mod, i_sh, i_sc, *, name, br=256):
    R = x.shape[0]

    def body(x_ref, mod_ref, o_ref):
        o_ref[...] = _normmod_fn(x_ref[...], mod_ref[i_sh:i_sh + 1, :], mod_ref[i_sc:i_sc + 1, :]).astype(BF16)

    return _call(body, name=name, out_shape=_sds((R, D), BF16), grid=(R // br,),
                 in_specs=[pl.BlockSpec((br, D), lambda i: (i, 0)), pl.BlockSpec((6, D), lambda i: (0, 0))],
                 out_specs=pl.BlockSpec((br, D), lambda i: (i, 0)), sem=("parallel",))(x, mod)


def _normmod_bwd(x, mod, i_sh, i_sc, dh, dh_off, res, *, name, br=256):
    R = x.shape[0]
    ob = dh_off // br
    has_res = res is not None

    def body(x_ref, mod_ref, dh_ref, *rest):
        if has_res:
            res_ref, dx_ref, dsh_ref, dsc_ref = rest
        else:
            dx_ref, dsh_ref, dsc_ref = rest
        sh, sc = mod_ref[i_sh:i_sh + 1, :], mod_ref[i_sc:i_sc + 1, :]
        _, vjp = jax.vjp(_normmod_fn, x_ref[...], sh, sc)
        dx, dsh, dsc = vjp(dh_ref[...])
        dx_ref[...] = dx + res_ref[...] if has_res else dx

        @pl.when(pl.program_id(0) == 0)
        def _():
            dsh_ref[...] = jnp.zeros_like(dsh_ref)
            dsc_ref[...] = jnp.zeros_like(dsc_ref)

        dsh_ref[...] += dsh
        dsc_ref[...] += dsc

    row = pl.BlockSpec((br, D), lambda i: (i, 0))
    vec = pl.BlockSpec((1, D), lambda i: (0, 0))
    ins = [row, pl.BlockSpec((6, D), lambda i: (0, 0)), pl.BlockSpec((br, D), lambda i: (i + ob, 0))]
    args = [x, mod, dh]
    if has_res:
        ins.append(row)
        args.append(res)
    return _call(body, name=name, out_shape=(_sds((R, D)), _sds((1, D)), _sds((1, D))), grid=(R // br,),
                 in_specs=ins, out_specs=(row, vec, vec), sem=("arbitrary",))(*args)


def _rope(x, cos, sin):
    return x * cos + _swap32(x) * sin


def _aprep_fn(qs, ks, cos, sin, qw, kw):
    return ([_rope(_rms(q) * qw, cos, sin) for q in qs], [_rope(_rms(k) * kw, cos, sin) for k in ks])


def _aprep_fwd(proj, cos, sin, qw, kw, *, br=256):
    T = proj.shape[0]

    def body(x_ref, cos_ref, sin_ref, qw_ref, kw_ref, q_ref, k_ref, v_ref):
        qs = [x_ref[:, C_AQ + h * HD:C_AQ + (h + 1) * HD] for h in range(AH)]
        ks = [x_ref[:, h * HD:(h + 1) * HD] for h in range(AKV)]
        qo, ko = _aprep_fn(qs, ks, cos_ref[...], sin_ref[...], qw_ref[...], kw_ref[...])
        for h in range(AH):
            q_ref[h] = qo[h].astype(BF16)
        for h in range(AKV):
            k_ref[h] = ko[h].astype(BF16)
            v_ref[h] = x_ref[:, (AKV + h) * HD:(AKV + h + 1) * HD].astype(BF16)

    tab = pl.BlockSpec((br, HD), lambda i: (i, 0))
    vec = pl.BlockSpec((1, HD), lambda i: (0, 0))
    return _call(body, name="aprep_fwd",
                 out_shape=(_sds((AH, T, HD), BF16), _sds((AKV, T, HD), BF16), _sds((AKV, T, HD), BF16)),
                 grid=(T // br,),
                 in_specs=[pl.BlockSpec((br, C_QKV), lambda i: (i, 0)), tab, tab, vec, vec],
                 out_specs=(pl.BlockSpec((AH, br, HD), lambda i: (0, i, 0)),
                            pl.BlockSpec((AKV, br, HD), lambda i: (0, i, 0)),
                            pl.BlockSpec((AKV, br, HD), lambda i: (0, i, 0))),
                 sem=("parallel",))(proj, cos, sin, qw, kw)


def _aprep_bwd(proj, cos, sin, qw, kw, dq, dk, dv, dproj, L, *, br=256):
    T = proj.shape[0]
    lb = L // br

    def body(x_ref, cos_ref, sin_ref, qw_ref, kw_ref, dq_ref, dk_ref, dv_ref, _, dx_ref, dqw_ref, dkw_ref):
        i = pl.program_id(0)
        qs = [x_ref[:, C_AQ + h * HD:C_AQ + (h + 1) * HD] for h in range(AH)]
        ks = [x_ref[:, h * HD:(h + 1) * HD] for h in range(AKV)]
        _, vjp = jax.vjp(_aprep_fn, qs, ks, cos_ref[...], sin_ref[...], qw_ref[...], kw_ref[...])
        is_lat = i >= lb
        dqs = [jnp.where(is_lat, dq_ref[h], 0.0) for h in range(AH)]
        dks = [dk_ref[h] for h in range(AKV)]
        gq, gk, _, _, gqw, gkw = vjp((dqs, dks))
        for h in range(AH):
            dx_ref[:, C_AQ + h * HD:C_AQ + (h + 1) * HD] = gq[h].astype(BF16)
        for h in range(AKV):
            dx_ref[:, h * HD:(h + 1) * HD] = gk[h].astype(BF16)
            dx_ref[:, (AKV + h) * HD:(AKV + h + 1) * HD] = dv_ref[h].astype(BF16)

        @pl.when(i == 0)
        def _():
            dqw_ref[...] = jnp.zeros_like(dqw_ref)
            dkw_ref[...] = jnp.zeros_like(dkw_ref)

        dqw_ref[...] += gqw
        dkw_ref[...] += gkw

    tab = pl.BlockSpec((br, HD), lambda i: (i, 0))
    vec = pl.BlockSpec((1, HD), lambda i: (0, 0))
    kvb = pl.BlockSpec((AKV, br, HD), lambda i: (0, i, 0))
    blk = pl.BlockSpec((br, C_QKV), lambda i: (i, 0))
    return _call(body, name="aprep_bwd", out_shape=(_sds(dproj.shape, BF16), _sds((1, HD)), _sds((1, HD))),
                 grid=(T // br,),
                 in_specs=[blk, tab, tab, vec, vec,
                           pl.BlockSpec((AH, br, HD), lambda i: (0, jnp.maximum(i - lb, 0), 0)), kvb, kvb, ANYSPEC],
                 out_specs=(blk, vec, vec), aliases={8: 0},
                 sem=("arbitrary",))(proj, cos, sin, qw, kw, dq, dk, dv, dproj)


def _attn_grad(q, k, v, o, lse2, do):
    scale = HD ** -0.5
    p = jnp.exp2(_dot(q, k, 1, 1) * (scale * LOG2E) - lse2)
    dp = _dot(do, v, 1, 1)
    ds = p * (dp - jnp.sum(do * o, axis=-1, keepdims=True)) * scale
    return _dot(ds, k, 1, 0), _dot(ds, q, 0, 0), _dot(p, do, 0, 0)


ATTN_KEYS = 256


def _attn_fwd(q, k, v, L, exch, *, bq=128):
    T = q.shape[1]
    N = T - L
    lb = L // bq
    assert T % ATTN_KEYS == 0
    scale = HD ** -0.5
    heads = range(GRP)

    def body(q_ref, k_ref, v_ref, o_ref, o32_ref, lse_ref):
        qs = [q_ref[g] for g in heads]
        m = [jnp.full((bq, 1), -jnp.inf, F32) for _ in heads]
        l = [jnp.zeros((bq, 1), F32) for _ in heads]
        acc = [jnp.zeros((bq, HD), F32) for _ in heads]
        for c in range(T // ATTN_KEYS):
            kc, vc = k_ref[c * ATTN_KEYS:(c + 1) * ATTN_KEYS, :], v_ref[c * ATTN_KEYS:(c + 1) * ATTN_KEYS, :]
            s = [_dot(qs[g], kc, 1, 1) * (scale * LOG2E) for g in heads]
            m_new = [jnp.maximum(m[g], jnp.max(s[g], axis=-1, keepdims=True)) for g in heads]
            alpha = [jnp.exp2(m[g] - m_new[g]) for g in heads]
            p = [jnp.exp2(s[g] - m_new[g]) for g in heads]
            l = [l[g] * alpha[g] + jnp.sum(p[g], axis=-1, keepdims=True) for g in heads]
            acc = [acc[g] * alpha[g] + _dot(p[g], vc, 1, 0) for g in heads]
            m = m_new
        for g in heads:
            o = acc[g] / l[g]
            o_ref[:, g * HD:(g + 1) * HD] = o.astype(BF16)
            o32_ref[:, g * HD:(g + 1) * HD] = o
            lse_ref[g] = jnp.broadcast_to(m[g] + jnp.log2(l[g]), (bq, HD))

    kvb = pl.BlockSpec((None, T, HD), lambda g, i: (g, 0, 0))
    ob = pl.BlockSpec((bq, GRP * HD), lambda g, i: (i, g))
    return _call_carrying(
        body, exch, name="attn_fwd",
        out_shape=(_sds((N, AH * HD), BF16), _sds((N, AH * HD)), _sds((AH, N, HD))), grid=(AKV, N // bq),
        in_specs=[pl.BlockSpec((GRP, bq, HD), lambda g, i: (g, i + lb, 0)), kvb, kvb],
        out_specs=(ob, ob, pl.BlockSpec((GRP, bq, HD), lambda g, i: (g, i, 0))), vmem=VMEM_BIG)(q, k, v)


def _attn_bwd(q, k, v, o32, lse, do, L, exch, *, bq=128):
    T = q.shape[1]
    N = T - L
    lb = L // bq

    def body(q_ref, k_ref, v_ref, o_ref, lse_ref, do_ref, dq_ref, dk_ref, dv_ref):
        rows = lambda r: jnp.concatenate([r[:, g * HD:(g + 1) * HD] for g in range(GRP)], axis=0)
        lse = jnp.max(lse_ref[...].reshape(GRP * bq, HD), axis=-1, keepdims=True)
        dq, dk, dv = _attn_grad(q_ref[...].reshape(GRP * bq, HD), k_ref[...], v_ref[...], rows(o_ref), lse, rows(do_ref))
        dq_ref[...] = dq.reshape(GRP, bq, HD)

        @pl.when(pl.program_id(1) == 0)
        def _():
            dk_ref[...] = jnp.zeros_like(dk_ref)
            dv_ref[...] = jnp.zeros_like(dv_ref)

        dk_ref[...] += dk
        dv_ref[...] += dv

    kvb = pl.BlockSpec((None, T, HD), lambda g, i: (g, 0, 0))
    qb = pl.BlockSpec((GRP, bq, HD), lambda g, i: (g, i + lb, 0))
    hb = pl.BlockSpec((GRP, bq, HD), lambda g, i: (g, i, 0))
    ob = pl.BlockSpec((bq, GRP * HD), lambda g, i: (i, g))
    return _call_carrying(body, exch, name="attn_bwd",
                          out_shape=(_sds((AH, N, HD)), _sds((AKV, T, HD)), _sds((AKV, T, HD))), grid=(AKV, N // bq),
                          in_specs=[qb, kvb, kvb, ob, hb, ob], out_specs=(hb, kvb, kvb),
                          vmem=VMEM_BIG)(q, k, v, o32, lse, do)


def _gprep_fn(kind, shifts, x, w):
    down, up = shifts
    y = down(x) * w[0:1, :] + x * w[1:2, :] + up(x) * w[2:3, :]
    a = _silu(y)
    if kind == 2:
        return a
    a = a * lax.rsqrt(jnp.sum(a * a, axis=-1, keepdims=True) + EPS)
    return a * (HD ** -0.5) if kind == 0 else a


def _gprep_fwd(proj, conv_w, kind, bounds):
    T = proj.shape[0]
    shifts = _make_shift(bounds)
    cb = C_QKV // HD + kind * GH

    def body(x_ref, w_ref, o_ref):
        o_ref[...] = _gprep_fn(kind, shifts, x_ref[...], w_ref[...])

    return _call(body, name=f"gprep_fwd{kind}", out_shape=_sds((GH, T, HD)), grid=(GH,),
                 in_specs=[pl.BlockSpec((T, HD), lambda h: (0, cb + h)),
                           pl.BlockSpec((3, HD), lambda h: (0, kind * GH + h))],
                 out_specs=pl.BlockSpec((None, T, HD), lambda h: (h, 0, 0)), sem=("parallel",))(proj, conv_w)


def _gprep_bwd(proj, conv_w, kind, bounds, dy, dproj):
    T = proj.shape[0]
    shifts = _make_shift(bounds)
    cb = C_QKV // HD + kind * GH

    def body(x_ref, w_ref, dy_ref, _, dx_ref, dw_ref):
        _, vjp = jax.vjp(functools.partial(_gprep_fn, kind, shifts), x_ref[...], w_ref[...])
        dx, dw = vjp(dy_ref[0] + dy_ref[1])
        dx_ref[...] = dx.astype(BF16)
        dw_ref[...] = dw

    return _call(body, name=f"gprep_bwd{kind}", out_shape=(_sds(dproj.shape, BF16), _sds((3, GH * HD))), grid=(GH,),
                 in_specs=[pl.BlockSpec((T, HD), lambda h: (0, cb + h)),
                           pl.BlockSpec((3, HD), lambda h: (0, kind * GH + h)),
                           pl.BlockSpec((2, None, T, HD), lambda h: (0, h, 0, 0)), ANYSPEC],
                 out_specs=(pl.BlockSpec((T, HD), lambda h: (0, cb + h)), pl.BlockSpec((3, HD), lambda h: (0, h))),
                 aliases={3: 0}, sem=("parallel",))(proj, conv_w, dy, dproj)


def _bl_fn(x, alog, dtb):
    lane = lax.broadcasted_iota(jnp.int32, x.shape, 1)
    beta = jax.nn.sigmoid(x)
    z = x + dtb
    sp = jnp.maximum(z, 0.0) + jnp.log1p(jnp.exp(-jnp.abs(z)))
    la = -jnp.exp(alog) * sp
    return jnp.where(lane < 2 * GH, beta, jnp.where(lane < 4 * GH, la, 0.0))


def _bl_fwd(proj, alog, dtb, *, br=256):
    T = proj.shape[0]

    def body(x_ref, a_ref, d_ref, o_ref):
        o_ref[...] = _bl_fn(x_ref[...], a_ref[...], d_ref[...])

    vec = pl.BlockSpec((1, HD), lambda i: (0, 0))
    return _call(body, name="bl_fwd", out_shape=_sds((T, HD)), grid=(T // br,),
                 in_specs=[pl.BlockSpec((br, HD), lambda i: (i, C_BL // HD)), vec, vec],
                 out_specs=pl.BlockSpec((br, HD), lambda i: (i, 0)), sem=("parallel",))(proj, alog, dtb)


def _bl_bwd(proj, alog, dtb, dbl, dproj, *, br=256):
    T = proj.shape[0]
    wide = C_Z - C_BL

    def body(x_ref, a_ref, d_ref, g_ref, _, dx_ref, da_ref, dd_ref):
        g = g_ref[0, 0]
        for d in range(2):
            for h in range(GH):
                if d or h:
                    g = g + g_ref[d, h]
        _, vjp = jax.vjp(_bl_fn, x_ref[...], a_ref[...], d_ref[...])
        dx, da, dd = vjp(g)
        dx_ref[:, :HD] = dx.astype(BF16)
        dx_ref[:, HD:] = jnp.zeros((br, wide - HD), BF16)

        @pl.when(pl.program_id(0) == 0)
        def _():
            da_ref[...] = jnp.zeros_like(da_ref)
            dd_ref[...] = jnp.zeros_like(dd_ref)

        da_ref[...] += da
        dd_ref[...] += dd

    vec = pl.BlockSpec((1, HD), lambda i: (0, 0))
    return _call(body, name="bl_bwd", out_shape=(_sds(dproj.shape, BF16), _sds((1, HD)), _sds((1, HD))), grid=(T // br,),
                 in_specs=[pl.BlockSpec((br, HD), lambda i: (i, C_BL // HD)), vec, vec,
                           pl.BlockSpec((2, GH, br, HD), lambda i: (0, 0, i, 0)), ANYSPEC],
                 out_specs=(pl.BlockSpec((br, wide), lambda i: (i, C_BL // wide)), vec, vec), aliases={4: 0},
                 sem=("arbitrary",))(proj, alog, dtb, dbl, dproj)


def _chunk_masks(d):
    ii = lax.broadcasted_iota(jnp.int32, (CH, CH), 0)
    jj = lax.broadcasted_iota(jnp.int32, (CH, CH), 1)
    eye = (ii == jj).astype(F32)
    before = jnp.where(d == 0, (jj < ii).astype(F32), (jj > ii).astype(F32))
    return before, before + eye, eye


def _intra_fn(masks, sel_b, sel_l, qs, ks, vs, bls, xs=None):
    before, ateq, eye = masks
    ones = jnp.ones((CH, CH), F32)
    inc = ateq > 0.0
    each = lambda f, *ls: [f(*t) for t in zip(*ls)]
    beta = each(lambda bl: jnp.sum(bl * sel_b, axis=-1, keepdims=True), bls)
    la = each(lambda bl: jnp.sum(bl * sel_l, axis=-1, keepdims=True), bls)
    gam = each(lambda a: _mask_nn(ateq, jnp.broadcast_to(a, (CH, HD))), la)
    gi = each(lambda a: _mask_nn(ateq, jnp.broadcast_to(a, (CH, CH))), la)
    gj = each(lambda g: _mask_nn(ones, eye * g), gi)
    kk = each(lambda k: _nt(k, k), ks)
    qk = each(_nt, qs, ks)
    dec = each(lambda a, b: jnp.where(inc, jnp.exp(jnp.where(inc, a - b, 0.0)), 0.0), gi, gj)
    lmat = each(lambda b, d, m: before * (b * d * m), beta, dec, kk)
    if xs is None:
        x = each(lambda m: eye - m, lmat)
        p2 = each(lambda m: _mdot(m, m), lmat)
        for it in range(4):
            y = each(lambda a, b: _mdot(jnp.concatenate([a, b], axis=0), b), x, p2)
            x = each(lambda a, t: a + t[:CH], x, y)
            p2 = each(lambda t: t[CH:], y)
        x = each(lambda a, b: a + _mdot(a, b), x, p2)
    else:
        x = each(_saved_inverse, lmat, xs)
    eg = each(jnp.exp, gam)
    u = each(lambda a, b, v: _mdot(a, b * v), x, beta, vs)
    w = each(lambda a, b, e, k: _mdot(a, (b * e) * k), x, beta, eg, ks)
    tot = each(lambda a: jnp.sum(a, axis=0, keepdims=True), la)
    kd = each(lambda k, t, g: k * jnp.exp(t - g), ks, tot, gam)
    gl = each(lambda t: jnp.broadcast_to(jnp.exp(t), (1, HD)), tot)
    qd = each(lambda q, e: q * e, qs, eg)
    p = each(lambda d, m: d * m, dec, qk)
    return (u, w, kd, qd, p, gl, x) if xs is None else (u, w, kd, qd, p, gl)


def _dir_head_sel(d, h):
    lane = lax.broadcasted_iota(jnp.int32, (1, HD), 1)
    return (lane == d * GH + h).astype(F32), (lane == 2 * GH + d * GH + h).astype(F32)


def _intra_specs(T, G):
    nc = T // CH
    assert nc % G == 0
    qkv = pl.BlockSpec((None, G * CH, HD), lambda d, h, c: (h, c, 0))
    bl = pl.BlockSpec((G * CH, HD), lambda d, h, c: (c, 0))
    big = pl.BlockSpec((None, None, G * CH, HD), lambda d, h, c: (d, h, c, 0))
    pm = pl.BlockSpec((None, None, G * CH, CH), lambda d, h, c: (d, h, c, 0))
    gl = pl.BlockSpec((None, None, G, 1, HD), lambda d, h, c: (d, h, c, 0, 0))
    shapes = (_sds((2, GH, T, HD)),) + (_sds((2, GH, T, HD), BF16),) * 3 + (
        _sds((2, GH, T, CH), BF16), _sds((2, GH, nc, 1, HD)), _sds((2, GH, T, CH)))
    return nc, qkv, bl, big, pm, gl, shapes


def _chunks_per_step(T, most):
    nc = T // CH
    return max(g for g in range(1, most + 1) if nc % g == 0)


def _chunk_at(g, d, nc, ncc):
    pos = _visit_pos(g, d, nc, ncc)
    return pos, pl.ds(pl.multiple_of(pos * CH, CH), CH)


def _intra_fwd(q, k, v, bl, L, exch):
    T = q.shape[1]
    G = _chunks_per_step(T, INTRA_FWD_CHUNKS)
    nc, qkv_s, bl_s, big, pm, gl_s, shapes = _intra_specs(T, G)
    assert G == nc
    ncc = L // CH

    def body(q_ref, k_ref, v_ref, bl_ref, u_ref, w_ref, kd_ref, qd_ref, p_ref, gl_ref, x_ref):
        d, h = pl.program_id(0), pl.program_id(1)
        sb, sl = _dir_head_sel(d, h)
        rows = [slice(g * CH, (g + 1) * CH) for g in range(G)]
        outs = _intra_fn(_chunk_masks(d), sb, sl, *[[r[s, :] for s in rows] for r in (q_ref, k_ref, v_ref, bl_ref)])
        for g in range(G):
            pos, at = _chunk_at(g, d, nc, ncc)
            for r, o in zip((u_ref, w_ref, kd_ref, qd_ref, p_ref, x_ref), outs[:5] + outs[6:]):
                r[at, :] = o[g].astype(r.dtype)
            gl_ref[pos] = outs[5][g]

    return _call_carrying(body, exch, name="gdn_intra_fwd", out_shape=shapes, grid=(2, GH, nc // G),
                          in_specs=[qkv_s, qkv_s, qkv_s, bl_s], out_specs=(big, big, big, big, pm, gl_s, pm))(q, k, v, bl)


def _intra_bwd(q, k, v, bl, xinv, cts, L, exch):
    T = q.shape[1]
    G = _chunks_per_step(T, INTRA_BWD_CHUNKS)
    nc, qkv_s, bl_s, big, pm, gl_s, _ = _intra_specs(T, G)
    assert G == nc
    ncc = L // CH

    def body(q_ref, k_ref, v_ref, bl_ref, x_ref, du, dw, dkd, dqd, dp, dgl, dq_ref, dk_ref, dv_ref, dbl_ref):
        d, h = pl.program_id(0), pl.program_id(1)
        sb, sl = _dir_head_sel(d, h)
        rows = [slice(g * CH, (g + 1) * CH) for g in range(G)]
        places = [_chunk_at(g, d, nc, ncc) for g in range(G)]
        fn = functools.partial(_intra_fn, _chunk_masks(d), sb, sl, xs=[x_ref[at, :] for _, at in places])
        _, vjp = jax.vjp(fn, *[[r[s, :] for s in rows] for r in (q_ref, k_ref, v_ref, bl_ref)])
        cts = tuple([r[at, :] for _, at in places] for r in (du, dw, dkd, dqd, dp)) + ([dgl[pos] for pos, _ in places],)
        grads = vjp(cts)
        for g in range(G):
            for r, o in zip((dq_ref, dk_ref, dv_ref, dbl_ref), grads):
                r[rows[g], :] = o[g]

    return _call_carrying(body, exch, name="gdn_intra_bwd", out_shape=(_sds((2, GH, T, HD)),) * 4,
                          grid=(2, GH, nc // G), in_specs=[qkv_s, qkv_s, qkv_s, bl_s, pm, big, big, big, big, pm, gl_s],
                          out_specs=(big,) * 4)(q, k, v, bl, xinv, *cts)


def _scan_fn(s, u, w, kd, qd, p, gl):
    each = lambda f, *ls: [f(*t) for t in zip(*ls)]
    ws = each(_nn, w, s)
    delta = each(lambda a, b: a - b, u, ws)
    kdd = each(_tn, kd, delta)
    s_new = each(lambda g, a, b: g * a + b, gl, s, kdd)
    qs = each(_nn, qd, s)
    pd = each(_nn, p, delta)
    return each(lambda a, b: a + b, qs, pd), s_new


SCAN_BLOCK = 4


def _visit_pos(c, d, nc, ncc):
    back = ncc - 1 - c if c < ncc else ncc + (nc - 1 - c)
    return jnp.where(d == 0, c, back)


def _scan_specs(T, L, back):
    tb = SCAN_BLOCK * CH
    assert T % tb == 0 and L % tb == 0
    nb, ncb = T // tb, L // tb
    at = (lambda t: nb - 1 - t) if back else (lambda t: t)
    big = pl.BlockSpec((2, GH, tb, HD), lambda t: (0, 0, at(t), 0))
    pm = pl.BlockSpec((2, GH, tb, CH), lambda t: (0, 0, at(t), 0))
    gl = pl.BlockSpec((2, GH, SCAN_BLOCK, 1, HD), lambda t: (0, 0, at(t), 0, 0))
    st = pl.BlockSpec((2, GH, SCAN_BLOCK, HD, HD), lambda t: (0, 0, at(t), 0, 0))

    def natural(b):
        return jnp.where(b < ncb, ncb - 1 - b, nb - 1 - (b - ncb))

    do_specs = (pl.BlockSpec((GH, tb, HD), lambda t: (0, at(t), 0)),
                pl.BlockSpec((GH, tb, HD), lambda t: (0, natural(at(t)), 0)))
    return nb, big, pm, gl, st, do_specs


SCAN_STREAMS = [(d, h) for d in (0, 1) for h in range(GH)]


def _scan_fwd(u, w, kd, qd, p, gl, L):
    T = u.shape[2]
    nb, big, pm, gl_s, st, _ = _scan_specs(T, L, False)

    def body(u_ref, w_ref, kd_ref, qd_ref, p_ref, gl_ref, o_ref, st_ref, s_scr):
        @pl.when(pl.program_id(0) == 0)
        def _():
            s_scr[...] = jnp.zeros_like(s_scr)

        s = [s_scr[d, h] for d, h in SCAN_STREAMS]
        for i in range(SCAN_BLOCK):
            rows = slice(i * CH, (i + 1) * CH)
            for (d, h), sv in zip(SCAN_STREAMS, s):
                st_ref[d, h, i] = sv
            o, s = _scan_fn(s, *[[r[d, h, rows, :].astype(F32) for d, h in SCAN_STREAMS]
                                 for r in (u_ref, w_ref, kd_ref, qd_ref, p_ref)],
                            [gl_ref[d, h, i] for d, h in SCAN_STREAMS])
            for (d, h), ov in zip(SCAN_STREAMS, o):
                o_ref[d, h, rows, :] = ov
        for (d, h), sv in zip(SCAN_STREAMS, s):
            s_scr[d, h] = sv

    return _call(body, name="gdn_scan_fwd", out_shape=(_sds((2, GH, T, HD)), _sds((2, GH, T // CH, HD, HD))),
                 grid=(nb,), in_specs=[big, big, big, big, pm, gl_s], out_specs=(big, st),
                 scratch=[pltpu.VMEM((2, GH, HD, HD), F32)], sem=("arbitrary",), vmem=VMEM_BIG)(u, w, kd, qd, p, gl)


def _scan_bwd(u, w, kd, qd, p, gl, states, do, L, exch):
    T = u.shape[2]
    nb, big, pm, gl_s, st, do_specs = _scan_specs(T, L, True)

    def body(u_ref, w_ref, kd_ref, qd_ref, p_ref, gl_ref, st_ref, do0_ref, do1_ref,
             du_ref, dw_ref, dkd_ref, dqd_ref, dp_ref, dgl_ref, ds_scr):
        @pl.when(pl.program_id(0) == 0)
        def _():
            ds_scr[...] = jnp.zeros_like(ds_scr)

        ds = [ds_scr[d, h] for d, h in SCAN_STREAMS]
        for i in reversed(range(SCAN_BLOCK)):
            rows = slice(i * CH, (i + 1) * CH)
            mirror = slice((SCAN_BLOCK - 1 - i) * CH, (SCAN_BLOCK - i) * CH)
            _, vjp = jax.vjp(_scan_fn, [st_ref[d, h, i] for d, h in SCAN_STREAMS],
                             *[[r[d, h, rows, :].astype(F32) for d, h in SCAN_STREAMS]
                               for r in (u_ref, w_ref, kd_ref, qd_ref, p_ref)],
                             [gl_ref[d, h, i] for d, h in SCAN_STREAMS])
            dos = [do0_ref[h, rows, :] if d == 0 else do1_ref[h, mirror, :] for d, h in SCAN_STREAMS]
            ds, gu, gw, gkd, gqd, gp, ggl = vjp((dos, ds))
            for n, (d, h) in enumerate(SCAN_STREAMS):
                du_ref[d, h, rows, :] = gu[n]
                dw_ref[d, h, rows, :] = gw[n]
                dkd_ref[d, h, rows, :] = gkd[n]
                dqd_ref[d, h, rows, :] = gqd[n]
                dp_ref[d, h, rows, :] = gp[n]
                dgl_ref[d, h, i] = ggl[n]
        for (d, h), dv in zip(SCAN_STREAMS, ds):
            ds_scr[d, h] = dv

    return _call_carrying(
        body, exch, name="gdn_scan_bwd",
        out_shape=(_sds((2, GH, T, HD)),) * 4 + (_sds((2, GH, T, CH)), _sds((2, GH, T // CH, 1, HD))),
        grid=(nb,), in_specs=[big, big, big, big, pm, gl_s, st, *do_specs], out_specs=(big, big, big, big, pm, gl_s),
        scratch=[pltpu.VMEM((2, GH, HD, HD), F32)], vmem=VMEM_BIG)(u, w, kd, qd, p, gl, states, do, do)


def _gout_fn(o0, o1, z, gw):
    return _rms(o0 + o1) * gw * _silu(z)


def _backward_latent(o_ref, L):
    nl = (o_ref.shape[1] - L) // CH
    return jnp.concatenate([o_ref[1, L + (nl - 1 - j) * CH:L + (nl - j) * CH, :] for j in range(nl)], axis=0)


def _gout_fwd(o, proj, gw, L):
    T = o.shape[2]
    N = T - L
    ob = pl.BlockSpec((2, None, T, HD), lambda h: (0, h, 0, 0))

    def body(o_ref, z_ref, gw_ref, y_ref):
        y_ref[...] = _gout_fn(o_ref[0, L:, :], _backward_latent(o_ref, L), z_ref[L:, :], gw_ref[...]).astype(BF16)

    return _call(body, name="gout_fwd", out_shape=_sds((N, GH * HD), BF16), grid=(GH,),
                 in_specs=[ob, pl.BlockSpec((T, HD), lambda h: (0, C_Z // HD + h)), pl.BlockSpec((1, HD), lambda h: (0, 0))],
                 out_specs=pl.BlockSpec((N, HD), lambda h: (0, h)), sem=("parallel",))(o, proj, gw)


def _gout_bwd(o, proj, gw, dy, dproj, L):
    T = o.shape[2]
    N = T - L
    ob = pl.BlockSpec((2, None, T, HD), lambda h: (0, h, 0, 0))

    def body(o_ref, z_ref, gw_ref, dy_ref, _, do_ref, dz_ref, dgw_ref):
        _, vjp = jax.vjp(_gout_fn, o_ref[0, L:, :], _backward_latent(o_ref, L), z_ref[L:, :], gw_ref[...])
        g0, _, gz, ggw = vjp(dy_ref[...])
        do_ref[:L, :] = jnp.zeros((L, HD), F32)
        do_ref[L:, :] = g0
        dz_ref[:L, :] = jnp.zeros((L, HD), BF16)
        dz_ref[L:, :] = gz.astype(BF16)

        @pl.when(pl.program_id(0) == 0)
        def _():
            dgw_ref[...] = jnp.zeros_like(dgw_ref)

        dgw_ref[...] += ggw

    zb = pl.BlockSpec((T, HD), lambda h: (0, C_Z // HD + h))
    return _call(body, name="gout_bwd", out_shape=(_sds((GH, T, HD)), _sds(dproj.shape, BF16), _sds((1, HD))),
                 grid=(GH,),
                 in_specs=[ob, zb, pl.BlockSpec((1, HD), lambda h: (0, 0)), pl.BlockSpec((N, HD), lambda h: (0, h)), ANYSPEC],
                 out_specs=(pl.BlockSpec((None, T, HD), lambda h: (h, 0, 0)), zb, pl.BlockSpec((1, HD), lambda h: (0, 0))),
                 aliases={4: 1}, sem=("arbitrary",))(o, proj, gw, dy, dproj)


def _merge_fn(pa, pd, ga, gd):
    return jax.nn.sigmoid(ga) * pa + jax.nn.sigmoid(gd) * pd


def _merge_fwd(pa, pd, proj, L, *, br=256):
    N = pa.shape[0]
    lb = L // br
    row = pl.BlockSpec((br, D), lambda i: (i, 0))

    def body(pa_ref, pd_ref, ga_ref, gd_ref, y_ref):
        y_ref[...] = _merge_fn(pa_ref[...], pd_ref[...], ga_ref[...], gd_ref[...]).astype(BF16)

    return _call(body, name="merge_fwd", out_shape=_sds((N, D), BF16), grid=(N // br,),
                 in_specs=[row, row, pl.BlockSpec((br, D), lambda i: (i + lb, C_GATE // D)),
                           pl.BlockSpec((br, D), lambda i: (i + lb, C_GATE // D + 1))],
                 out_specs=row, sem=("parallel",))(pa, pd, proj, proj)


def _merge_bwd(pa, pd, proj, dy, L, *, br=256):
    N = pa.shape[0]
    T = N + L
    lb = L // br
    lrow = pl.BlockSpec((br, D), lambda i: (jnp.maximum(i - lb, 0), 0))

    def body(pa_ref, pd_ref, ga_ref, gd_ref, dy_ref, dpa_ref, dpd_ref, dg_ref):
        lat = pl.program_id(0) >= lb
        _, vjp = jax.vjp(_merge_fn, pa_ref[...], pd_ref[...], ga_ref[...], gd_ref[...])
        gpa, gpd, gga, ggd = vjp(dy_ref[...])
        dpa_ref[...] = gpa.astype(BF16)
        dpd_ref[...] = gpd.astype(BF16)
        dg_ref[:, :D] = jnp.where(lat, gga, 0.0).astype(BF16)
        dg_ref[:, D:] = jnp.where(lat, ggd, 0.0).astype(BF16)

    return _call(body, name="merge_bwd", out_shape=(_sds((N, D), BF16), _sds((N, D), BF16), _sds((T, C_END), BF16)),
                 grid=(T // br,),
                 in_specs=[lrow, lrow, pl.BlockSpec((br, D), lambda i: (i, C_GATE // D)),
                           pl.BlockSpec((br, D), lambda i: (i, C_GATE // D + 1)), lrow],
                 out_specs=(lrow, lrow, pl.BlockSpec((br, 2 * D), lambda i: (i, C_GATE // (2 * D)))),
                 sem=("arbitrary",))(pa, pd, proj, proj, dy)


def _resid_fwd(x, m, mod, i_g, *, name, br=256):
    R = x.shape[0]
    row = pl.BlockSpec((br, D), lambda i: (i, 0))

    def body(x_ref, m_ref, mod_ref, o_ref):
        o_ref[...] = x_ref[...] + mod_ref[i_g:i_g + 1, :] * m_ref[...]

    return _call(body, name=name, out_shape=_sds((R, D)), grid=(R // br,),
                 in_specs=[row, row, pl.BlockSpec((6, D), lambda i: (0, 0))], out_specs=row,
                 sem=("parallel",))(x, m, mod)


def _resid_bwd(dx, m, mod, i_g, *, name, br=256):
    R = dx.shape[0]
    row = pl.BlockSpec((br, D), lambda i: (i, 0))
    vec = pl.BlockSpec((1, D), lambda i: (0, 0))

    def body(dx_ref, m_ref, mod_ref, dm_ref, dg_ref):
        dxv = dx_ref[...]
        dm_ref[...] = (dxv * mod_ref[i_g:i_g + 1, :]).astype(BF16)

        @pl.when(pl.program_id(0) == 0)
        def _():
            dg_ref[...] = jnp.zeros_like(dg_ref)

        dg_ref[...] += jnp.sum(dxv * m_ref[...], axis=0, keepdims=True)

    return _call(body, name=name, out_shape=(_sds((R, D), BF16), _sds((1, D))), grid=(R // br,),
                 in_specs=[row, row, pl.BlockSpec((6, D), lambda i: (0, 0))], out_specs=(row, vec),
                 sem=("arbitrary",))(dx, m, mod)


def _ffn_fn(shifts, ug, uv, wg, wv, bg, bv):
    down, up = shifts

    def conv(x, w, b):
        return down(x) * w[0:1, :] + x * w[1:2, :] + up(x) * w[2:3, :] + b

    return _silu(conv(ug, wg, bg)) * conv(uv, wv, bv)


def _ffn_fwd(up, cw, cb, *, bw=256):
    N = up.shape[0]
    shifts = _make_shift(((0, N),))
    nb = DFF // bw

    def body(ug, uv, wg, wv, bg, bv, a_ref):
        a_ref[...] = _ffn_fn(shifts, ug[...], uv[...], wg[...], wv[...], bg[...], bv[...]).astype(BF16)

    def col(rows, off):
        return pl.BlockSpec((rows, bw), lambda j: (0, j + off))

    return _call(body, name="ffn_fwd", out_shape=_sds((N, DFF), BF16), grid=(nb,),
                 in_specs=[col(N, 0), col(N, nb), col(3, 0), col(3, nb), col(1, 0), col(1, nb)],
                 out_specs=col(N, 0), sem=("parallel",), vmem=VMEM_BIG)(up, up, cw, cw, cb, cb)


def _ffn_bwd(up, cw, cb, da, *, bw=256):
    N = up.shape[0]
    shifts = _make_shift(((0, N),))
    nb = DFF // bw

    def body(ug, uv, wg, wv, bg, bv, da_ref, dug, duv, dwg, dwv, dbg, dbv):
        _, vjp = jax.vjp(functools.partial(_ffn_fn, shifts), ug[...], uv[...], wg[...], wv[...], bg[...], bv[...])
        g = vjp(da_ref[...])
        dug[...] = g[0].astype(BF16)
        duv[...] = g[1].astype(BF16)
        dwg[...], dwv[...], dbg[...], dbv[...] = g[2], g[3], g[4], g[5]

    def col(rows, off):
        return pl.BlockSpec((rows, bw), lambda j: (0, j + off))

    half = (_sds((N, DFF), BF16), _sds((N, DFF), BF16), _sds((3, DFF)), _sds((3, DFF)), _sds((1, DFF)), _sds((1, DFF)))
    dug, duv, dwg, dwv, dbg, dbv = _call(
        body, name="ffn_bwd", out_shape=half, grid=(nb,),
        in_specs=[col(N, 0), col(N, nb), col(3, 0), col(3, nb), col(1, 0), col(1, nb), col(N, 0)],
        out_specs=(col(N, 0), col(N, 0), col(3, 0), col(3, 0), col(1, 0), col(1, 0)),
        sem=("parallel",), vmem=VMEM_BIG)(up, up, cw, cw, cb, cb, da)
    return (jnp.concatenate([dug, duv], axis=1), jnp.concatenate([dwg, dwv], axis=1),
            jnp.concatenate([dbg, dbv], axis=1))


def _head_fn(x1, dn, g2, fw, tgt):
    y = _rms(x1 + g2 * dn) * fw
    err = y - tgt
    return 0.5 * jnp.sum(jnp.mean(err * err, axis=-1))


def _head(x1, dn, mod, fw, tgt, *, br=256):
    N = x1.shape[0]
    row = pl.BlockSpec((br, D), lambda i: (i, 0))
    vec = pl.BlockSpec((1, D), lambda i: (0, 0))
    one = pl.BlockSpec((1, HD), lambda i: (0, 0))

    def body(x1_ref, dn_ref, mod_ref, fw_ref, tgt_ref, loss_ref, dx_ref, ddn_ref, dg_ref, dfw_ref):
        loss, (gx, gdn, gg, gfw) = jax.value_and_grad(_head_fn, argnums=(0, 1, 2, 3))(
            x1_ref[...], dn_ref[...], mod_ref[5:6, :], fw_ref[...], tgt_ref[...])
        dx_ref[...] = gx
        ddn_ref[...] = gdn.astype(BF16)

        @pl.when(pl.program_id(0) == 0)
        def _():
            loss_ref[...] = jnp.zeros_like(loss_ref)
            dg_ref[...] = jnp.zeros_like(dg_ref)
            dfw_ref[...] = jnp.zeros_like(dfw_ref)

        loss_ref[...] += jnp.broadcast_to(loss, (1, HD))
        dg_ref[...] += gg
        dfw_ref[...] += gfw

    return _call(body, name="head", out_shape=(_sds((1, HD)), _sds((N, D)), _sds((N, D), BF16), _sds((1, D)), _sds((1, D))),
                 grid=(N // br,), in_specs=[row, row, pl.BlockSpec((6, D), lambda i: (0, 0)), vec, row],
                 out_specs=(one, row, row, vec, vec), sem=("arbitrary",))(x1, dn, mod, fw, tgt)


def _adamw(w, g, m, v, *, name):
    shape = w.shape
    cols = shape[-1]
    rows = max(1, math.prod(shape[:-1]))
    w2, g2, m2, v2 = (t.reshape(rows, cols) for t in (w, g, m, v))
    br = 256 if rows % 256 == 0 else rows
    c1 = 1.0 - B1 ** STEP
    c2 = 1.0 - B2 ** STEP

    def body(w_ref, g_ref, m_ref, v_ref, d_ref, nm_ref, nv_ref):
        gv = g_ref[...]
        nm = B1 * m_ref[...] + (1.0 - B1) * gv
        nv = B2 * v_ref[...] + (1.0 - B2) * (gv * gv)
        d_ref[...] = -LR * ((nm / c1) / (jnp.sqrt(nv / c2) + AEPS) + WD * w_ref[...])
        nm_ref[...] = nm
        nv_ref[...] = nv

    blk = pl.BlockSpec((br, cols), lambda i: (i, 0))
    outs = _call(body, name=name, out_shape=(_sds((rows, cols)),) * 3, grid=(rows // br,),
                 in_specs=[blk] * 4, out_specs=(blk,) * 3, sem=("parallel",))(w2, g2, m2, v2)
    return tuple(t.reshape(shape) for t in outs)


def _adamw_many(items, *, name):
    k = len(items)
    shapes = [w.shape for w, _, _, _ in items]
    flat = [t.reshape(max(1, math.prod(t.shape[:-1])), t.shape[-1]) for it in items for t in it]
    c1 = 1.0 - B1 ** STEP
    c2 = 1.0 - B2 ** STEP

    def body(*refs):
        ins, outs = refs[:4 * k], refs[4 * k:]
        for i in range(k):
            w_ref, g_ref, m_ref, v_ref = ins[4 * i:4 * i + 4]
            gv = g_ref[...]
            nm = B1 * m_ref[...] + (1.0 - B1) * gv
            nv = B2 * v_ref[...] + (1.0 - B2) * (gv * gv)
            outs[3 * i][...] = -LR * ((nm / c1) / (jnp.sqrt(nv / c2) + AEPS) + WD * w_ref[...])
            outs[3 * i + 1][...] = nm
            outs[3 * i + 2][...] = nv

    res = _call(body, name=name, out_shape=tuple(_sds(flat[4 * i].shape) for i in range(k) for _ in range(3)))(*flat)
    return [tuple(res[3 * i + j].reshape(shapes[i]) for j in range(3)) for i in range(k)]


def _rope_tables(N, L):
    t = jnp.arange(N)
    pos = jnp.stack([(t // GRID_W).astype(F32), (t % GRID_W).astype(F32)], axis=1)
    inv = ROPE_THETA ** (-jnp.arange(0, HD // 2, 2, dtype=F32) / (HD // 2))
    ang = pos[:, :, None] * inv[None, None, :]
    cos = jnp.broadcast_to(jnp.cos(ang)[:, :, None, :], (N, 2, 2, HD // 4)).reshape(N, HD)
    sin = jnp.broadcast_to(jnp.sin(ang)[:, :, None, :], (N, 2, 2, HD // 4))
    sin = (sin * jnp.array([-1.0, 1.0], F32)[None, None, :, None]).reshape(N, HD)
    cos = jnp.concatenate([jnp.ones((L, HD), F32), cos], axis=0)
    sin = jnp.concatenate([jnp.zeros((L, HD), F32), sin], axis=0)
    return cos, sin


def _pad_lanes(v, off=0):
    return jnp.zeros((1, HD), F32).at[0, off:off + v.shape[0]].set(v)


def _local_step(x, ctx, tgt, mod_lat, mod_ctx, w_in, shards, small):
    N, L = x.shape[0], ctx.shape[0]
    T = N + L
    bounds = ((0, L), (L, T))
    qw, kw, gw = small["q_norm_w"], small["k_norm_w"], small["gdn_norm_w"]
    conv_w, ffn_w, ffn_b, fnw = small["conv_qkv_w"], small["ffn_conv_w"], small["ffn_conv_b"], small["final_norm_w"]
    alog = _pad_lanes(small["a_log"].reshape(-1), 2 * GH)
    dtb = _pad_lanes(small["dt_bias"].reshape(-1), 2 * GH)
    cos, sin = _rope_tables(N, L)
    bt = T
    bnl = 256 if N % 1024 else 1024

    hc = _normmod_fwd(ctx, mod_ctx, 0, 1, name="normmod_ctx")
    hx = _normmod_fwd(x, mod_lat, 0, 1, name="normmod_x")
    h1 = jnp.concatenate([hc, hx], axis=0)
    proj = _mm(h1, w_in, name="mm_in", M=T, N=C_END, K=D, tb=True, bm=bt, bn=1024)
    aq, ak, av = _aprep_fwd(proj, cos, sin, qw, kw)
    (attn, attn32, lse), (up_g,) = _attn_fwd(aq, ak, av, L, _GatherTwoLevel([shards["w_up"]]))
    gq = _gprep_fwd(proj, conv_w, 0, bounds)
    gk = _gprep_fwd(proj, conv_w, 1, bounds)
    gv = _gprep_fwd(proj, conv_w, 2, bounds)
    bl = _bl_fwd(proj, alog, dtb)
    intra, (down_g, pa_g, pd_g, out_g) = _intra_fwd(
        gq, gk, gv, bl, L, _GatherTwoLevel([shards[n] for n in ("w_down", "w_pa", "w_pd", "w_out")]))
    w_up, w_down = up_g.reshape(2 * DFF, D), down_g.reshape(DFF, D)
    w_pa, w_pd, w_out = pa_g.reshape(D, D), pd_g.reshape(D, D), out_g.reshape(D, D)
    xinv, intra = intra[6], intra[:6]
    o, states = _scan_fwd(*intra, L)
    gdn = _gout_fwd(o, proj, gw, L)
    pa = _mm(attn, w_pa, name="mm_pa", M=N, N=D, K=D, bm=bnl)
    pd = _mm(gdn, w_pd, name="mm_pd", M=N, N=D, K=D, bm=bnl)
    y = _merge_fwd(pa, pd, proj, L)
    m = _mm(y, w_out, name="mm_out", M=N, N=D, K=D, bm=bnl)
    x1 = _resid_fwd(x, m, mod_lat, 2, name="resid1")
    h2 = _normmod_fwd(x1, mod_lat, 3, 4, name="normmod_x1")
    up = _mm(h2, w_up, name="mm_up", M=N, N=2 * DFF, K=D, tb=True, bm=bnl, bn=2 * DFF // 4)
    a = _ffn_fwd(up, ffn_w, ffn_b)
    dn = _mm(a, w_down, name="mm_down", M=N, N=D, K=DFF, bm=bnl)
    loss, dx2, ddn, dg2, dfnw = _head(x1, dn, mod_lat, fnw, tgt)

    da = _mm(ddn, w_down, name="mm_down_dx", M=N, N=DFF, K=D, tb=True, bm=bnl, bn=DFF // 2)
    g_down = _mm(a, ddn, name="mm_down_dw", M=DFF, N=D, K=N, ta=True, bm=DFF // 2, out_dtype=BF16)
    dup, d_ffn_w, d_ffn_b = _ffn_bwd(up, ffn_w, ffn_b, da)
    dh2 = _mm(dup, w_up, name="mm_up_dx", M=N, N=D, K=2 * DFF, bm=bnl, bk=2 * DFF // 4)
    g_up = _mm(dup, h2, name="mm_up_dw", M=2 * DFF, N=D, K=N, ta=True, bm=2 * DFF // 4, out_dtype=BF16)
    dx1, dsh2, dsc2 = _normmod_bwd(x1, mod_lat, 3, 4, dh2, 0, dx2, name="normmod_x1_bwd")
    dm, dg1 = _resid_bwd(dx1, m, mod_lat, 2, name="resid1_bwd")
    dy = _mm(dm, w_out, name="mm_out_dx", M=N, N=D, K=D, tb=True, bm=bnl)
    g_out = _mm(y, dm, name="mm_out_dw", M=D, N=D, K=N, ta=True, out_dtype=BF16)
    dpa, dpd, dproj = _merge_bwd(pa, pd, proj, dy, L)
    dattn = _mm(dpa, w_pa, name="mm_pa_dx", M=N, N=D, K=D, tb=True, bm=bnl)
    g_pa = _mm(attn, dpa, name="mm_pa_dw", M=D, N=D, K=N, ta=True, out_dtype=BF16)
    dgdn = _mm(dpd, w_pd, name="mm_pd_dx", M=N, N=D, K=D, tb=True, bm=bnl)
    g_pd = _mm(gdn, dpd, name="mm_pd_dw", M=D, N=D, K=N, ta=True, out_dtype=BF16)
    do, dproj, dgw = _gout_bwd(o, proj, gw, dgdn, dproj, L)
    cts, recv_a = _scan_bwd(*intra, states, do, L, _Exchange([g_out.reshape(NDEV, D // NDEV, D)], True))
    (dgq, dgk, dgv, dbl), recv_b = _intra_bwd(gq, gk, gv, bl, xinv, cts, L, _Exchange(
        [g_pa.reshape(NDEV, D // NDEV, D), g_pd.reshape(NDEV, D // NDEV, D), g_up.reshape(NDEV, 2 * DFF // NDEV, D)], True))
    dproj, dwq = _gprep_bwd(proj, conv_w, 0, bounds, dgq, dproj)
    dproj, dwk = _gprep_bwd(proj, conv_w, 1, bounds, dgk, dproj)
    dproj, dwv = _gprep_bwd(proj, conv_w, 2, bounds, dgv, dproj)
    dproj, dalog, ddtb = _bl_bwd(proj, alog, dtb, dbl, dproj)
    (daq_h, dak_h, dav_h), recv_c = _attn_bwd(aq, ak, av, attn32, lse, dattn, L, _Exchange(
        [g_down.reshape(NDEV, DFF // NDEV, D)], True))
    recv = dict(zip(("w_out", "w_pa", "w_pd", "w_up", "w_down"), recv_a + recv_b + recv_c))
    dproj, dqw, dkw = _aprep_bwd(proj, cos, sin, qw, kw, daq_h, dak_h, dav_h, dproj, L)
    g_in = _mm(dproj, h1, name="mm_in_dw", M=C_END, N=D, K=T, ta=True, bm=1024, out_dtype=BF16)
    g_in = _unpad_columns(g_in).reshape(NDEV, W_END // NDEV, D)
    own_in = lax.dynamic_index_in_dim(g_in, _position()[3], axis=0, keepdims=False)
    *pending, token = _scatter_start(g_in, None, (0, D // 2), (), name="scatter_g_in_a_start")
    dh1 = _mm(dproj, w_in, name="mm_in_dx", M=T, N=D, K=C_END, bm=bt, bk=1024, after=(token,))
    grad_x, dsh1, dsc1 = _normmod_bwd(x, mod_lat, 0, 1, dh1, L, dx1, name="normmod_x_bwd")
    _, dcsh1, dcsc1 = _normmod_bwd(ctx, mod_ctx, 0, 1, dh1, 0, None, name="normmod_ctx_bwd")

    z1 = jnp.zeros((1, D), F32)
    dmod_lat = jnp.concatenate([dsh1, dsc1, dg1, dsh2, dsc2, dg2], axis=0)
    dmod_ctx = jnp.concatenate([dcsh1, dcsc1, z1, z1, z1, z1], axis=0)
    gsmall = {
        "q_norm_w": dqw, "k_norm_w": dkw, "gdn_norm_w": dgw,
        "conv_qkv_w": jnp.concatenate([dwq, dwk, dwv], axis=1),
        "a_log": dalog[0, 2 * GH:4 * GH], "dt_bias": ddtb[0, 2 * GH:4 * GH],
        "ffn_conv_w": d_ffn_w, "ffn_conv_b": d_ffn_b, "final_norm_w": dfnw,
    }
    return loss[0, 0], grad_x, (pending, own_in), recv, dmod_lat, dmod_ctx, gsmall


HBM = pl.BlockSpec(memory_space=pltpu.HBM)
ANYSPEC = pl.BlockSpec(memory_space=pl.ANY)


def _position():
    x, y, c = lax.axis_index("x"), lax.axis_index("y"), lax.axis_index("c")
    return x, y, c, 4 * x + 2 * y + c


def _peer(x, y, c, k):
    px = 1 - x if k & 4 else x
    py = 1 - y if k & 2 else y
    pc = 1 - c if k & 1 else c
    return (px, py, pc), 4 * px + 2 * py + pc


def _exchange(arrs, *, name, scatter):
    exch = _Exchange(arrs, scatter)
    n = exch.n

    def body(*refs):
        ins, outs, sems = refs[:n], refs[n:2 * n], refs[2 * n:]
        exch.start(ins, outs, sems)
        exch.finish(ins, outs, sems)

    outs = pl.pallas_call(body, name=name, out_shape=exch.out_shape, in_specs=[HBM] * n, out_specs=(HBM,) * n,
                          scratch_shapes=exch.scratch,
                          compiler_params=pltpu.CompilerParams(has_side_effects=True))(*arrs)
    return list(outs)


class _Exchange:
    def __init__(self, arrs, scatter):
        self.arrs, self.scatter, self.n = list(arrs), scatter, len(arrs)
        self.out_shape = tuple(_sds(a.shape if scatter else (NDEV,) + a.shape, a.dtype) for a in arrs)
        self.scratch = [pltpu.SemaphoreType.DMA((self.n, NDEV - 1)), pltpu.SemaphoreType.DMA((self.n, NDEV - 1)),
                        pltpu.SemaphoreType.DMA((self.n,))]

    def _copies(self, ins, outs, sems):
        send, recv, loc = sems
        x, y, c, me = _position()
        local = [pltpu.make_async_copy(ins[a].at[me] if self.scatter else ins[a], outs[a].at[me], loc.at[a])
                 for a in range(self.n)]
        remote = []
        for k in range(1, NDEV):
            peer, pid = _peer(x, y, c, k)
            for a in range(self.n):
                src = ins[a].at[pid] if self.scatter else ins[a]
                remote.append(pltpu.make_async_remote_copy(
                    src_ref=src, dst_ref=outs[a].at[me], send_sem=send.at[a, k - 1], recv_sem=recv.at[a, k - 1],
                    device_id=peer, device_id_type=MESH))
        return local, remote

    def start(self, ins, outs, sems):
        local, remote = self._copies(ins, outs, sems)
        for cp in local + remote:
            cp.start()

    def finish(self, ins, outs, sems):
        local, remote = self._copies(ins, outs, sems)
        for cp in remote:
            cp.wait()
        for cp in local:
            cp.wait()


class _GatherTwoLevel:
    scatter = False

    def __init__(self, arrs):
        self.arrs, self.n = list(arrs), len(arrs)
        self.out_shape = tuple(_sds((NDEV,) + a.shape, a.dtype) for a in arrs)
        self.scratch = [pltpu.SemaphoreType.DMA((self.n, NDEV - 1)), pltpu.SemaphoreType.DMA((self.n, NDEV - 1)),
                        pltpu.SemaphoreType.DMA((self.n,))]

    def _parts(self, ins, outs, sems):
        send, recv, loc = sems
        x, y, c, _ = _position()
        me, sibling = (x, y, c), (x, y, 1 - c)
        chips = [(1 - x, y), (x, 1 - y), (1 - x, 1 - y)]
        parts = []
        for a in range(self.n):
            slot = lambda px, py, pc, a=a: outs[a].at[4 * px + 2 * py + pc]

            def copy(k, owner, to, src=None, a=a, slot=slot):
                return pltpu.make_async_remote_copy(
                    src_ref=slot(*owner) if src is None else src, dst_ref=slot(*owner), send_sem=send.at[a, k],
                    recv_sem=recv.at[a, k], device_id=to, device_id_type=MESH)

            parts.append(dict(
                mine=pltpu.make_async_copy(ins[a], slot(*me), loc.at[a]),
                first=[copy(0, me, sibling, src=ins[a])] + [copy(1 + j, me, (*ch, c), src=ins[a]) for j, ch in enumerate(chips)],
                arrive=[copy(1 + j, (*ch, c), me) for j, ch in enumerate(chips)],
                passed=[copy(4 + j, (*ch, c), sibling) for j, ch in enumerate(chips)],
                rest=[copy(0, sibling, me)] + [copy(4 + j, (*ch, 1 - c), me) for j, ch in enumerate(chips)]))
        return parts

    def start(self, ins, outs, sems):
        for p in self._parts(ins, outs, sems):
            p["mine"].start()
            for cp in p["first"]:
                cp.start()

    def middle(self, ins, outs, sems):
        for p in self._parts(ins, outs, sems):
            for got, fwd in zip(p["arrive"], p["passed"]):
                got.wait_recv()
                fwd.start()

    def finish(self, ins, outs, sems):
        for p in self._parts(ins, outs, sems):
            for cp in p["rest"]:
                cp.wait_recv()
            for cp in p["first"] + p["passed"]:
                cp.wait_send()
            p["mine"].wait()


def _gather_two_level(block, *, name):
    def body(x_ref, out_ref, send_sems, recv_sems, local_sem):
        x, y, c, _ = _position()
        me, sibling = (x, y, c), (x, y, 1 - c)
        chips = [(1 - x, y), (x, 1 - y), (1 - x, 1 - y)]

        def slot(px, py, pc):
            return out_ref.at[4 * px + 2 * py + pc]

        def copy(k, owner, to, src=None):
            return pltpu.make_async_remote_copy(
                src_ref=slot(*owner) if src is None else src, dst_ref=slot(*owner), send_sem=send_sems.at[k],
                recv_sem=recv_sems.at[k], device_id=to, device_id_type=MESH)

        mine = pltpu.make_async_copy(x_ref, slot(*me), local_sem)
        mine.start()
        first = [copy(0, me, sibling, src=x_ref)]
        first += [copy(1 + j, me, (*chip, c), src=x_ref) for j, chip in enumerate(chips)]
        for cp in first:
            cp.start()
        passed = [copy(4 + j, (*chip, c), sibling) for j, chip in enumerate(chips)]
        for j, chip in enumerate(chips):
            copy(1 + j, (*chip, c), me).wait_recv()
            passed[j].start()
        copy(0, sibling, me).wait_recv()
        for j, chip in enumerate(chips):
            copy(4 + j, (*chip, 1 - c), me).wait_recv()
        for cp in first + passed:
            cp.wait_send()
        mine.wait()

    return pl.pallas_call(
        body, name=name, out_shape=_sds((NDEV,) + block.shape, block.dtype), in_specs=[HBM], out_specs=HBM,
        scratch_shapes=[pltpu.SemaphoreType.DMA((NDEV - 1,)), pltpu.SemaphoreType.DMA((NDEV - 1,)),
                        pltpu.SemaphoreType.DMA],
        compiler_params=pltpu.CompilerParams(has_side_effects=True))(block)


SEM = pl.BlockSpec(memory_space=pltpu.SEMAPHORE)


def _scatter_copies(src_ref, land_ref, send_sems, recv_sems, cols):
    x, y, c, me = _position()
    span = (slice(None), pl.ds(*cols))
    copies = []
    for k in range(1, NDEV):
        peer, pid = _peer(x, y, c, k)
        copies.append(pltpu.make_async_remote_copy(
            src_ref=src_ref.at[pid].at[span], dst_ref=land_ref.at[me].at[span], send_sem=send_sems.at[k - 1],
            recv_sem=recv_sems.at[k - 1], device_id=peer, device_id_type=MESH))
    return copies


SPLIT_EFFECT = pltpu.SideEffectType.DATAFLOW_SIDE_EFFECTING


def _scatter_start(parts, land, cols, after, *, name):
    na = len(after)
    if land is None:
        land = lax.empty(parts.shape, parts.dtype)

    def body(src_ref, land_ref, *rest):
        send_sems, recv_sems, _, _, token = rest[na:]
        for cp in _scatter_copies(src_ref, land_ref, send_sems, recv_sems, cols):
            cp.start()
        token[...] = jnp.zeros_like(token)

    return pl.pallas_call(
        body, name=name,
        out_shape=(pltpu.SemaphoreType.DMA((NDEV - 1,)), pltpu.SemaphoreType.DMA((NDEV - 1,)),
                   pltpu.HBM(parts.shape, parts.dtype), pltpu.HBM(parts.shape, parts.dtype), _sds((8, HD))),
        in_specs=(HBM, HBM) + (pl.BlockSpec(memory_space=pl.ANY),) * na,
        out_specs=(SEM, SEM, HBM, HBM, pl.BlockSpec(memory_space=pltpu.VMEM)),
        input_output_aliases={0: 2, 1: 3}, compiler_params=pltpu.CompilerParams(has_side_effects=SPLIT_EFFECT),
    )(pltpu.with_memory_space_constraint(parts, pltpu.HBM), pltpu.with_memory_space_constraint(land, pltpu.HBM), *after)


def _scatter_wait(send_sems, recv_sems, src_thru, land_thru, cols, after, *, name):
    na = len(after)

    def body(src_ref, land_ref, send_sems, recv_sems, *rest):
        for cp in _scatter_copies(src_ref, land_ref, send_sems, recv_sems, cols):
            cp.wait_send()
            cp.wait_recv()

    return pl.pallas_call(
        body, name=name,
        out_shape=(pltpu.HBM(src_thru.shape, src_thru.dtype), pltpu.HBM(land_thru.shape, land_thru.dtype)),
        in_specs=(HBM, HBM, SEM, SEM) + (pl.BlockSpec(memory_space=pl.ANY),) * na, out_specs=(HBM, HBM),
        input_output_aliases={0: 0, 1: 1}, compiler_params=pltpu.CompilerParams(has_side_effects=SPLIT_EFFECT),
    )(src_thru, land_thru, send_sems, recv_sems, *after)


def _cast_bf16(w, *, name):
    rows, cols = w.shape
    br = 128 if rows % 128 == 0 else rows

    def body(w_ref, o_ref):
        o_ref[...] = w_ref[...].astype(BF16)

    blk = pl.BlockSpec((br, cols), lambda i: (i, 0))
    return _call(body, name=name, out_shape=_sds((rows, cols), BF16), grid=(rows // br,), in_specs=[blk],
                 out_specs=blk, sem=("parallel",))(w)


def _sum_slots(a, *, name):
    _, R, C = a.shape

    def body(a_ref, o_ref):
        s = a_ref[0]
        for d in range(1, NDEV):
            s = s + a_ref[d]
        o_ref[...] = s

    return _call(body, name=name, out_shape=_sds((R, C)))(a)


MODROWS = 16


def _mod_fwd(c9, w, b):
    cols = w.shape[1]

    def body(c_ref, w_ref, b_ref, o_ref):
        o_ref[...] = _nn(_silu(c_ref[...]), w_ref[...]) + b_ref[...]

    return _call(body, name="mod_fwd", out_shape=_sds((MODROWS, cols)))(c9, w, b)


def _mod_bwd(c9, dmy, dall, w):
    cols = w.shape[1]

    def body(c_ref, dmy_ref, dall_ref, w_ref, gw_ref, gb_ref, cp_ref):
        sc = _silu(c_ref[...])
        rows = lax.broadcasted_iota(jnp.int32, (MODROWS, 1), 0)
        d = dmy_ref[...]
        d_ctx = jnp.where(rows == NDEV, d, 0.0)
        sc_ctx = jnp.where(rows == NDEV, sc, 0.0)
        outer = lax.dot_general(sc_ctx, d_ctx, (((0,), (0,)), ((), ())), precision=HI, preferred_element_type=F32)
        gw_ref[...] = _tn(jnp.where(rows < NDEV, sc, 0.0), jnp.where(rows < NDEV, d, 0.0)) + outer
        gb_ref[...] = jnp.sum(dall_ref[...], axis=0, keepdims=True)
        cp_ref[...] = jnp.sum(_nt(d_ctx, w_ref[...]), axis=0, keepdims=True)

    return _call(body, name="mod_bwd", out_shape=(_sds((D, cols)), _sds((1, 6 * D)), _sds((1, D))),
                 vmem=VMEM_BIG)(c9, dmy, dall, w)


def _cctx_finish(parts, c_ctx, after):
    VM = pl.BlockSpec(memory_space=pltpu.VMEM)

    def body(p_ref, c_ref, *rest):
        o_ref = rest[-1]
        s = p_ref[0]
        for d in range(1, NDEV):
            s = s + p_ref[d]
        _, vjp = jax.vjp(_silu, c_ref[...])
        o_ref[...] = vjp(s)[0]

    return _call(body, name="cctx_finish", out_shape=_sds((1, D)),
                 in_specs=[VM, VM] + [pl.BlockSpec(memory_space=pl.ANY)] * len(after))(parts, c_ctx, *after)


def _adamw_recv(w, recv, m, v, *, name, own=None):
    rows, cols = w.shape
    bc = 256
    c1 = 1.0 - B1 ** STEP
    c2 = 1.0 - B2 ** STEP
    has_own = own is not None

    def body(w_ref, r_ref, m_ref, v_ref, *rest):
        g_ref, d_ref, nm_ref, nv_ref = rest[-4:]
        me = _position()[3]

        def slot(d):
            return jnp.where(me == d, rest[0][...], r_ref[d]) if has_own else r_ref[d]

        gv = slot(0).astype(F32)
        for d in range(1, NDEV):
            gv = gv + slot(d).astype(F32)
        nm = B1 * m_ref[...] + (1.0 - B1) * gv
        nv = B2 * v_ref[...] + (1.0 - B2) * (gv * gv)
        g_ref[...] = gv
        d_ref[...] = -LR * ((nm / c1) / (jnp.sqrt(nv / c2) + AEPS) + WD * w_ref[...])
        nm_ref[...] = nm
        nv_ref[...] = nv

    blk = pl.BlockSpec((rows, bc), lambda j: (0, j))
    return _call(body, name=name, out_shape=(_sds((rows, cols)),) * 4, grid=(cols // bc,),
                 in_specs=[blk, pl.BlockSpec((NDEV, rows, bc), lambda j: (0, 0, j)), blk, blk] + [blk] * has_own,
                 out_specs=(blk,) * 4, sem=("parallel",), vmem=VMEM_BIG)(w, recv, m, v, *([own] if has_own else []))


P_LAT, P_CTX, P_FNW, P_FFNB, P_CONV, P_FFNW, P_MISC, P_ROWS = 0, 8, 16, 24, 32, 48, 72, 80


def _rows_of(v, nrows):
    flat = v.reshape(-1)
    return jnp.pad(flat, (0, nrows * D - flat.shape[0])).reshape(nrows, D)


def _by_columns(g):
    n, r, c = g.shape
    return jnp.transpose(g, (1, 0, 2)).reshape(r, n * c)


def kernel(x, c, ctx, c_ctx, w_mod, b_mod, w_in, q_norm_w, k_norm_w, conv_qkv_w, a_log, dt_bias, gdn_norm_w, w_pa, w_pd, w_out, w_up, ffn_conv_w, ffn_conv_b, w_down, final_norm_w, loss_target, m_c_ctx, m_w_mod, m_b_mod, m_w_in, m_q_norm_w, m_k_norm_w, m_conv_qkv_w, m_a_log, m_dt_bias, m_gdn_norm_w, m_w_pa, m_w_pd, m_w_out, m_w_up, m_ffn_conv_w, m_ffn_conv_b, m_w_down, m_final_norm_w, v_c_ctx, v_w_mod, v_b_mod, v_w_in, v_q_norm_w, v_k_norm_w, v_conv_qkv_w, v_a_log, v_dt_bias, v_gdn_norm_w, v_w_pa, v_w_pd, v_w_out, v_w_up, v_ffn_conv_w, v_ffn_conv_b, v_w_down, v_final_norm_w):
    _, _, _, me = _position()
    mcols = w_mod.shape[2]

    transposed = ("w_in", "w_up")
    big = {"w_in": w_in[0].T, "w_pa": w_pa[0], "w_pd": w_pd[0], "w_out": w_out[0], "w_up": w_up[0].T, "w_down": w_down[0]}
    names = list(big)
    shards = {n: _cast_bf16(big[n], name="cast_" + n) for n in names}
    w_in_g = _gather_two_level(shards["w_in"], name="gather_w_in")
    c_all, conv_g, ffnw_g = _exchange([c, conv_qkv_w[0], ffn_conv_w[0]], name="gather_small", scatter=False)
    w_in_full = w_in_g.reshape(W_END, D)
    w_in_pad = _pad_columns(w_in_full)

    c9 = jnp.concatenate([c_all.reshape(NDEV, D), jnp.pad(c_ctx[None], ((0, MODROWS - NDEV - 1), (0, 0)))], axis=0)
    b_loc = lax.dynamic_slice(b_mod, (0, me * mcols), (1, mcols))
    mod_all, = _exchange([_mod_fwd(c9, w_mod[0], b_loc)], name="gather_mod", scatter=False)
    mod_lat = lax.dynamic_index_in_dim(mod_all, me, axis=1, keepdims=False).reshape(6, D)
    mod_ctx = mod_all[:, NDEV, :].reshape(6, D)

    small = {"q_norm_w": q_norm_w, "k_norm_w": k_norm_w, "gdn_norm_w": gdn_norm_w, "a_log": a_log, "dt_bias": dt_bias,
             "conv_qkv_w": _by_columns(conv_g), "ffn_conv_w": _by_columns(ffnw_g), "ffn_conv_b": ffn_conv_b,
             "final_norm_w": final_norm_w[None]}
    loss_me, grad_x, (pending_in, own_in), recv, dmod_lat, dmod_ctx, gs = _local_step(
        x[0], ctx[0], loss_target[0], mod_lat, mod_ctx, w_in_pad, shards, small)

    moments = {"w_in": (m_w_in, v_w_in), "w_pa": (m_w_pa, v_w_pa), "w_pd": (m_w_pd, v_w_pd),
               "w_out": (m_w_out, v_w_out), "w_up": (m_w_up, v_w_up), "w_down": (m_w_down, v_w_down)}
    res = {}
    def finish(n, outs):
        return tuple((t.T if n in transposed else t)[None] for t in outs)

    def moment(t, n):
        return t[0].T if n in transposed else t[0]

    for n in recv:
        res[n] = finish(n, _adamw_recv(big[n], recv[n], moment(moments[n][0], n), moment(moments[n][1], n),
                                       name="adamw_" + n))

    misc = jnp.concatenate([gs["q_norm_w"][0], gs["k_norm_w"][0], gs["gdn_norm_w"][0], gs["a_log"], gs["dt_bias"],
                            loss_me[None]])
    pack = jnp.concatenate([_rows_of(dmod_lat, P_CTX - P_LAT), _rows_of(dmod_ctx, P_FNW - P_CTX),
                            _rows_of(gs["final_norm_w"], P_FFNB - P_FNW), _rows_of(gs["ffn_conv_b"], P_CONV - P_FFNB),
                            _rows_of(gs["conv_qkv_w"], P_FFNW - P_CONV), _rows_of(gs["ffn_conv_w"], P_MISC - P_FFNW),
                            _rows_of(misc, P_ROWS - P_MISC)], axis=0)
    pack_all, = _exchange([pack], name="gather_pack", scatter=False)
    tot = _sum_slots(pack_all, name="sum_pack")
    dall = jnp.concatenate([pack_all[:, P_LAT:P_LAT + 6, :].reshape(NDEV, 6 * D),
                            jnp.pad(tot[P_CTX:P_CTX + 6].reshape(1, 6 * D), ((0, MODROWS - NDEV - 1), (0, 0)))], axis=0)
    dmy = lax.dynamic_slice(dall, (0, me * mcols), (MODROWS, mcols))
    g_w_mod, g_b_mod, cpart = _mod_bwd(c9, dmy, dall, w_mod[0])
    cparts, = _exchange([cpart], name="gather_cctx", scatter=False)
    sems_a, land = pending_in[:2], pending_in[3]
    *sems_b, g_in_thru, land, token_b = _scatter_start(pending_in[2], land, (D // 2, D // 2), (cparts,),
                                                       name="scatter_g_in_b_start")
    g_c_ctx = _cctx_finish(cparts, c_ctx[None], (token_b,))[0]

    nconv, nffn = 3 * GH * HD, 2 * DFF
    conv_tot = tot[P_CONV:P_FFNW].reshape(-1)[:3 * nconv].reshape(3, nconv)
    ffnw_tot = tot[P_FFNW:P_MISC].reshape(-1)[:3 * nffn].reshape(3, nffn)
    mrow = tot[P_MISC]
    grads = {
        "c_ctx": g_c_ctx, "w_mod": g_w_mod[None], "b_mod": g_b_mod,
        "q_norm_w": mrow[None, 0:HD], "k_norm_w": mrow[None, HD:2 * HD], "gdn_norm_w": mrow[None, 2 * HD:3 * HD],
        "conv_qkv_w": lax.dynamic_slice(conv_tot, (0, me * (nconv // NDEV)), (3, nconv // NDEV))[None],
        "a_log": mrow[3 * HD:3 * HD + 2 * GH].reshape(1, 2, GH),
        "dt_bias": mrow[3 * HD + 2 * GH:3 * HD + 4 * GH].reshape(1, 2, GH),
        "ffn_conv_w": lax.dynamic_slice(ffnw_tot, (0, me * (nffn // NDEV)), (3, nffn // NDEV))[None],
        "ffn_conv_b": tot[P_FFNB:P_CONV].reshape(-1)[:nffn][None],
        "final_norm_w": tot[P_FNW],
    }
    loss = mrow[3 * HD + 4 * GH]
    given = {"c_ctx": (c_ctx, m_c_ctx, v_c_ctx), "w_mod": (w_mod, m_w_mod, v_w_mod), "b_mod": (b_mod, m_b_mod, v_b_mod),
             "q_norm_w": (q_norm_w, m_q_norm_w, v_q_norm_w), "k_norm_w": (k_norm_w, m_k_norm_w, v_k_norm_w),
             "conv_qkv_w": (conv_qkv_w, m_conv_qkv_w, v_conv_qkv_w), "a_log": (a_log, m_a_log, v_a_log),
             "dt_bias": (dt_bias, m_dt_bias, v_dt_bias), "gdn_norm_w": (gdn_norm_w, m_gdn_norm_w, v_gdn_norm_w),
             "ffn_conv_w": (ffn_conv_w, m_ffn_conv_w, v_ffn_conv_w), "ffn_conv_b": (ffn_conv_b, m_ffn_conv_b, v_ffn_conv_b),
             "final_norm_w": (final_norm_w, m_final_norm_w, v_final_norm_w)}
    res["w_mod"] = (grads["w_mod"],) + _adamw(w_mod, grads["w_mod"], m_w_mod, v_w_mod, name="adamw_w_mod")
    small_names = [n for n in given if n != "w_mod"]
    updates = _adamw_many([(given[n][0], grads[n], given[n][1], given[n][2]) for n in small_names], name="adamw_small")
    for n, upd in zip(small_names, updates):
        res[n] = (grads[n],) + upd

    g_in_thru, land = _scatter_wait(*sems_a, g_in_thru, land, (0, D // 2), [res[n][1] for n in res],
                                    name="scatter_g_in_a_wait")
    _, land = _scatter_wait(*sems_b, g_in_thru, land, (D // 2, D // 2), (), name="scatter_g_in_b_wait")
    res["w_in"] = finish("w_in", _adamw_recv(big["w_in"], land, moment(m_w_in, "w_in"), moment(v_w_in, "w_in"),
                                             name="adamw_w_in", own=own_in))

    order = ["c_ctx", "w_mod", "b_mod", "w_in", "q_norm_w", "k_norm_w", "conv_qkv_w", "a_log", "dt_bias", "gdn_norm_w",
             "w_pa", "w_pd", "w_out", "w_up", "ffn_conv_w", "ffn_conv_b", "w_down", "final_norm_w"]
    return (loss, grad_x[None], *[res[n][0] for n in order], *[res[n][1] for n in order],
            *[res[n][2] for n in order], *[res[n][3] for n in order])
```

```python
import functools
import math

import jax
import jax.numpy as jnp
from jax import lax
from jax.experimental import pallas as pl
from jax.experimental.pallas import tpu as pltpu

F32 = jnp.float32
BF16 = jnp.bfloat16
HI = lax.Precision.HIGHEST
MESH = pl.DeviceIdType.MESH

NDEV = 8
D = 1024
HD = 128
AH, AKV, GRP = 8, 2, 4
GH = 8
CH = 64
DFF = 2816
GRID_W = 64
EPS = 1e-6
ROPE_THETA = 10000.0
LOG2E = math.log2(math.e)
C_KV, C_AQ, C_QKV, C_BL, C_Z, C_GATE, C_END = 0, 512, 1536, 4608, 5120, 6144, 8192
W_QKV, W_AQ, W_Z, W_END = 512, 3616, 4640, 7712


def _pad_columns(w):
    zeros = jnp.zeros((C_Z - C_QKV - (W_AQ - W_QKV), D), w.dtype)
    return jnp.concatenate([w[:W_QKV], w[W_AQ:W_Z], w[W_QKV:W_AQ], zeros, w[W_Z:]], axis=0)


def _unpad_columns(g):
    return jnp.concatenate([g[:C_AQ], g[C_QKV:C_QKV + W_AQ - W_QKV], g[C_AQ:C_QKV], g[C_Z:]], axis=0)
LR, B1, B2, AEPS, WD, STEP = 0.001, 0.9, 0.999, 1e-08, 0.01, 10
VMEM_BIG = 56 * 1024 * 1024
INTRA_FWD_CHUNKS = 36
INTRA_BWD_CHUNKS = 36


def _call(body, *, name, out_shape, grid=None, in_specs=None, out_specs=None, scratch=(), sem=None,
          vmem=None, aliases=None):
    params = {}
    if sem is not None:
        params["dimension_semantics"] = sem
    if vmem is not None:
        params["vmem_limit_bytes"] = vmem
    kw = {}
    if grid is not None:
        kw["grid"] = grid
    if in_specs is not None:
        kw["in_specs"] = in_specs
    if out_specs is not None:
        kw["out_specs"] = out_specs
    if aliases:
        kw["input_output_aliases"] = aliases
    return pl.pallas_call(body, name=name, out_shape=out_shape, scratch_shapes=list(scratch),
                          compiler_params=pltpu.CompilerParams(**params), **kw)


def _call_carrying(body, exch, *, name, out_shape, grid, in_specs, out_specs, scratch=(), vmem=None):
    n, nin, nout, nscr = exch.n, len(in_specs), len(out_shape), len(scratch)
    steps = math.prod(grid)
    mid = (2 * steps) // 3

    def wrapped(*refs):
        ins, cins = refs[:nin], refs[nin:nin + n]
        outs, couts = refs[nin + n:nin + n + nout], refs[nin + n + nout:nin + 2 * n + nout]
        scr, sems = refs[nin + 2 * n + nout:nin + 2 * n + nout + nscr], refs[nin + 2 * n + nout + nscr:]
        ids = [pl.program_id(i) for i in range(len(grid))]
        first = functools.reduce(jnp.logical_and, [i == 0 for i in ids])
        last = functools.reduce(jnp.logical_and, [i == g - 1 for i, g in zip(ids, grid)])

        @pl.when(first)
        def _():
            exch.start(cins, couts, sems)

        if hasattr(exch, "middle"):
            linear = functools.reduce(lambda acc, ig: acc * ig[1] + ig[0], zip(ids, grid), 0)

            @pl.when(linear == mid)
            def _():
                exch.middle(cins, couts, sems)

        body(*ins, *outs, *scr)

        @pl.when(last)
        def _():
            exch.finish(cins, couts, sems)

    params = {"dimension_semantics": ("arbitrary",) * len(grid)}
    if vmem is not None:
        params["vmem_limit_bytes"] = vmem
    fn = pl.pallas_call(wrapped, name=name, out_shape=tuple(out_shape) + exch.out_shape, grid=grid,
                        in_specs=list(in_specs) + [HBM] * n, out_specs=tuple(out_specs) + (HBM,) * n,
                        scratch_shapes=list(scratch) + exch.scratch, compiler_params=pltpu.CompilerParams(**params))

    def run(*args):
        res = fn(*args, *exch.arrs)
        return res[:nout], list(res[nout:])

    return run


def _sds(shape, dtype=F32):
    return jax.ShapeDtypeStruct(tuple(shape), dtype)


def _dot(a, b, ca, cb):
    return lax.dot_general(a.astype(BF16), b.astype(BF16), (((ca,), (cb,)), ((), ())),
                           preferred_element_type=F32)


@jax.custom_vjp
def _nn(a, b):
    return _dot(a, b, 1, 0)


@jax.custom_vjp
def _nt(a, b):
    return _dot(a, b, 1, 1)


@jax.custom_vjp
def _tn(a, b):
    return _dot(a, b, 0, 0)


_nn.defvjp(lambda a, b: (_nn(a, b), (a, b)), lambda r, g: (_nt(g, r[1]), _tn(r[0], g)))
_nt.defvjp(lambda a, b: (_nt(a, b), (a, b)), lambda r, g: (_nn(g, r[1]), _tn(g, r[0])))
_tn.defvjp(lambda a, b: (_tn(a, b), (a, b)), lambda r, g: (_nt(r[1], g), _nn(r[0], g)))


def _mdot(a, b):
    return jnp.dot(a, b, precision=lax.Precision.HIGH, preferred_element_type=F32)


def _maskdot(mask, a, cm):
    hi = a.astype(BF16)
    r = a - hi.astype(F32)
    mid = r.astype(BF16)
    lo = (r - mid.astype(F32)).astype(BF16)
    mb = mask.astype(BF16)
    dims = (((cm,), (0,)), ((), ()))
    return (lax.dot_general(mb, hi, dims, preferred_element_type=F32)
            + lax.dot_general(mb, mid, dims, preferred_element_type=F32)
            + lax.dot_general(mb, lo, dims, preferred_element_type=F32))


@jax.custom_vjp
def _mask_nn(mask, a):
    return _maskdot(mask, a, 1)


_mask_nn.defvjp(lambda mask, a: (_maskdot(mask, a, 1), mask),
                lambda mask, g: (jnp.zeros_like(mask), _maskdot(mask, g, 0)))


@jax.custom_vjp
def _saved_inverse(lmat, x):
    return x


def _saved_inverse_bwd(x, g):
    t = lax.dot_general(x, g, (((0,), (0,)), ((), ())), precision=lax.Precision.HIGH, preferred_element_type=F32)
    dl = lax.dot_general(t, x, (((1,), (1,)), ((), ())), precision=lax.Precision.HIGH, preferred_element_type=F32)
    return -dl, jnp.zeros_like(x)


_saved_inverse.defvjp(lambda lmat, x: (x, x), _saved_inverse_bwd)


def _row_ids(shape):
    return lax.broadcasted_iota(jnp.int32, shape, 0)


def _shift_rows(x, down, bounds):
    n = x.shape[0]
    rows = _row_ids(x.shape)
    y = pltpu.roll(x, 1 if down else n - 1, 0)
    edge = functools.reduce(jnp.logical_or, [rows == (s if down else e - 1) for s, e in bounds])
    return jnp.where(edge, 0.0, y)


def _make_shift(bounds):
    @jax.custom_vjp
    def down(x):
        return _shift_rows(x, True, bounds)

    @jax.custom_vjp
    def up(x):
        return _shift_rows(x, False, bounds)

    down.defvjp(lambda x: (down(x), None), lambda _, g: (up(g),))
    up.defvjp(lambda x: (up(x), None), lambda _, g: (down(g),))
    return down, up


@jax.custom_vjp
def _swap32(x):
    lane = lax.broadcasted_iota(jnp.int32, x.shape, x.ndim - 1)
    return jnp.where((lane % 64) < 32, pltpu.roll(x, HD - 32, x.ndim - 1), pltpu.roll(x, 32, x.ndim - 1))


_swap32.defvjp(lambda x: (_swap32(x), None), lambda _, g: (_swap32(g),))


def _rms(x):
    return x * lax.rsqrt(jnp.mean(x * x, axis=-1, keepdims=True) + EPS)


def _silu(x):
    return x * jax.nn.sigmoid(x)


def _mm(a, b, *, name, M, N, K, ta=False, tb=False, out_dtype=F32, bm=None, bn=None, bk=None, after=()):
    bm, bn, bk = bm or M, bn or N, bk or K
    assert M % bm == 0 and N % bn == 0 and K % bk == 0, (name, M, N, K, bm, bn, bk)
    nk = K // bk
    ca, cb = (0 if ta else 1), (1 if tb else 0)
    na = len(after)

    def body(a_ref, b_ref, *rest):
        o_ref, acc = rest[na], rest[na + 1:]
        r = _dot(a_ref[...], b_ref[...], ca, cb)
        if nk == 1:
            o_ref[...] = r.astype(out_dtype)
        else:
            acc_ref, = acc
            k = pl.program_id(2)

            @pl.when(k == 0)
            def _():
                acc_ref[...] = r

            @pl.when(k > 0)
            def _():
                acc_ref[...] += r

            @pl.when(k == nk - 1)
            def _():
                o_ref[...] = acc_ref[...].astype(out_dtype)

    a_spec = pl.BlockSpec((bk, bm), lambda i, j, k: (k, i)) if ta else pl.BlockSpec((bm, bk), lambda i, j, k: (i, k))
    b_spec = pl.BlockSpec((bn, bk), lambda i, j, k: (j, k)) if tb else pl.BlockSpec((bk, bn), lambda i, j, k: (k, j))
    return _call(body, name=name, out_shape=_sds((M, N), out_dtype), grid=(M // bm, N // bn, nk),
                 in_specs=[a_spec, b_spec] + [pl.BlockSpec(memory_space=pl.ANY)] * na,
                 out_specs=pl.BlockSpec((bm, bn), lambda i, j, k: (i, j)),
                 scratch=[pltpu.VMEM((bm, bn), F32)] if nk > 1 else [],
                 sem=("parallel", "parallel", "arbitrary"), vmem=VMEM_BIG)(a, b, *after)


def _normmod_fn(x, sh, sc):
    return _rms(x) * (1.0 + sc) + sh


def _normmod_fwd(x, mod, i_sh, i_sc, *, name, br=256):
    R = x.shape[0]

    def body(x_ref, mod_ref, o_ref):
        o_ref[...] = _normmod_fn(x_ref[...], mod_ref[i_sh:i_sh + 1, :], mod_ref[i_sc:i_sc + 1, :]).astype(BF16)

    return _call(body, name=name, out_shape=_sds((R, D), BF16), grid=(R // br,),
                 in_specs=[pl.BlockSpec((br, D), lambda i: (i, 0)), pl.BlockSpec((6, D), lambda i: (0, 0))],
                 out_specs=pl.BlockSpec((br, D), lambda i: (i, 0)), sem=("parallel",))(x, mod)


def _normmod_bwd(x, mod, i_sh, i_sc, dh, dh_off, res, *, name, br=256):
    R = x.shape[0]
    ob = dh_off // br
    has_res = res is not None

    def body(x_ref, mod_ref, dh_ref, *rest):
        if has_res:
            res_ref, dx_ref, dsh_ref, dsc_ref = rest
        else:
            dx_ref, dsh_ref, dsc_ref = rest
        sh, sc = mod_ref[i_sh:i_sh + 1, :], mod_ref[i_sc:i_sc + 1, :]
        _, vjp = jax.vjp(_normmod_fn, x_ref[...], sh, sc)
        dx, dsh, dsc = vjp(dh_ref[...])
        dx_ref[...] = dx + res_ref[...] if has_res else dx

        @pl.when(pl.program_id(0) == 0)
        def _():
            dsh_ref[...] = jnp.zeros_like(dsh_ref)
            dsc_ref[...] = jnp.zeros_like(dsc_ref)

        dsh_ref[...] += dsh
        dsc_ref[...] += dsc

    row = pl.BlockSpec((br, D), lambda i: (i, 0))
    vec = pl.BlockSpec((1, D), lambda i: (0, 0))
    ins = [row, pl.BlockSpec((6, D), lambda i: (0, 0)), pl.BlockSpec((br, D), lambda i: (i + ob, 0))]
    args = [x, mod, dh]
    if has_res:
        ins.append(row)
        args.append(res)
    return _call(body, name=name, out_shape=(_sds((R, D)), _sds((1, D)), _sds((1, D))), grid=(R // br,),
                 in_specs=ins, out_specs=(row, vec, vec), sem=("arbitrary",))(*args)


def _rope(x, cos, sin):
    return x * cos + _swap32(x) * sin


def _aprep_fn(qs, ks, cos, sin, qw, kw):
    return ([_rope(_rms(q) * qw, cos, sin) for q in qs], [_rope(_rms(k) * kw, cos, sin) for k in ks])


def _aprep_fwd(proj, cos, sin, qw, kw, *, br=256):
    T = proj.shape[0]

    def body(x_ref, cos_ref, sin_ref, qw_ref, kw_ref, q_ref, k_ref, v_ref):
        qs = [x_ref[:, C_AQ + h * HD:C_AQ + (h + 1) * HD] for h in range(AH)]
        ks = [x_ref[:, h * HD:(h + 1) * HD] for h in range(AKV)]
        qo, ko = _aprep_fn(qs, ks, cos_ref[...], sin_ref[...], qw_ref[...], kw_ref[...])
        for h in range(AH):
            q_ref[h] = qo[h].astype(BF16)
        for h in range(AKV):
            k_ref[h] = ko[h].astype(BF16)
            v_ref[h] = x_ref[:, (AKV + h) * HD:(AKV + h + 1) * HD].astype(BF16)

    tab = pl.BlockSpec((br, HD), lambda i: (i, 0))
    vec = pl.BlockSpec((1, HD), lambda i: (0, 0))
    return _call(body, name="aprep_fwd",
                 out_shape=(_sds((AH, T, HD), BF16), _sds((AKV, T, HD), BF16), _sds((AKV, T, HD), BF16)),
                 grid=(T // br,),
                 in_specs=[pl.BlockSpec((br, C_QKV), lambda i: (i, 0)), tab, tab, vec, vec],
                 out_specs=(pl.BlockSpec((AH, br, HD), lambda i: (0, i, 0)),
                            pl.BlockSpec((AKV, br, HD), lambda i: (0, i, 0)),
                            pl.BlockSpec((AKV, br, HD), lambda i: (0, i, 0))),
                 sem=("parallel",))(proj, cos, sin, qw, kw)


def _aprep_bwd(proj, cos, sin, qw, kw, dq, dk, dv, dproj, L, *, br=256):
    T = proj.shape[0]
    lb = L // br

    def body(x_ref, cos_ref, sin_ref, qw_ref, kw_ref, dq_ref, dk_ref, dv_ref, _, dx_ref, dqw_ref, dkw_ref):
        i = pl.program_id(0)
        qs = [x_ref[:, C_AQ + h * HD:C_AQ + (h + 1) * HD] for h in range(AH)]
        ks = [x_ref[:, h * HD:(h + 1) * HD] for h in range(AKV)]
        _, vjp = jax.vjp(_aprep_fn, qs, ks, cos_ref[...], sin_ref[...], qw_ref[...], kw_ref[...])
        is_lat = i >= lb
        dqs = [jnp.where(is_lat, dq_ref[h], 0.0) for h in range(AH)]
        dks = [dk_ref[h] for h in range(AKV)]
        gq, gk, _, _, gqw, gkw = vjp((dqs, dks))
        for h in range(AH):
            dx_ref[:, C_AQ + h * HD:C_AQ + (h + 1) * HD] = gq[h].astype(BF16)
        for h in range(AKV):
            dx_ref[:, h * HD:(h + 1) * HD] = gk[h].astype(BF16)
            dx_ref[:, (AKV + h) * HD:(AKV + h + 1) * HD] = dv_ref[h].astype(BF16)

        @pl.when(i == 0)
        def _():
            dqw_ref[...] = jnp.zeros_like(dqw_ref)
            dkw_ref[...] = jnp.zeros_like(dkw_ref)

        dqw_ref[...] += gqw
        dkw_ref[...] += gkw

    tab = pl.BlockSpec((br, HD), lambda i: (i, 0))
    vec = pl.BlockSpec((1, HD), lambda i: (0, 0))
    kvb = pl.BlockSpec((AKV, br, HD), lambda i: (0, i, 0))
    blk = pl.BlockSpec((br, C_QKV), lambda i: (i, 0))
    return _call(body, name="aprep_bwd", out_shape=(_sds(dproj.shape, BF16), _sds((1, HD)), _sds((1, HD))),
                 grid=(T // br,),
                 in_specs=[blk, tab, tab, vec, vec,
                           pl.BlockSpec((AH, br, HD), lambda i: (0, jnp.maximum(i - lb, 0), 0)), kvb, kvb, ANYSPEC],
                 out_specs=(blk, vec, vec), aliases={8: 0},
                 sem=("arbitrary",))(proj, cos, sin, qw, kw, dq, dk, dv, dproj)


def _attn_grad(q, k, v, o, lse2, do):
    scale = HD ** -0.5
    p = jnp.exp2(_dot(q, k, 1, 1) * (scale * LOG2E) - lse2)
    dp = _dot(do, v, 1, 1)
    ds = p * (dp - jnp.sum(do * o, axis=-1, keepdims=True)) * scale
    return _dot(ds, k, 1, 0), _dot(ds, q, 0, 0), _dot(p, do, 0, 0)


ATTN_KEYS = 256


def _attn_fwd(q, k, v, L, exch, *, bq=128):
    T = q.shape[1]
    N = T - L
    lb = L // bq
    assert T % ATTN_KEYS == 0
    scale = HD ** -0.5
    heads = range(GRP)

    def body(q_ref, k_ref, v_ref, o_ref, o32_ref, lse_ref):
        qs = [q_ref[g] for g in heads]
        m = [jnp.full((bq, 1), -jnp.inf, F32) for _ in heads]
        l = [jnp.zeros((bq, 1), F32) for _ in heads]
        acc = [jnp.zeros((bq, HD), F32) for _ in heads]
        for c in range(T // ATTN_KEYS):
            kc, vc = k_ref[c * ATTN_KEYS:(c + 1) * ATTN_KEYS, :], v_ref[c * ATTN_KEYS:(c + 1) * ATTN_KEYS, :]
            s = [_dot(qs[g], kc, 1, 1) * (scale * LOG2E) for g in heads]
            m_new = [jnp.maximum(m[g], jnp.max(s[g], axis=-1, keepdims=True)) for g in heads]
            alpha = [jnp.exp2(m[g] - m_new[g]) for g in heads]
            p = [jnp.exp2(s[g] - m_new[g]) for g in heads]
            l = [l[g] * alpha[g] + jnp.sum(p[g], axis=-1, keepdims=True) for g in heads]
            acc = [acc[g] * alpha[g] + _dot(p[g], vc, 1, 0) for g in heads]
            m = m_new
        for g in heads:
            o = acc[g] / l[g]
            o_ref[:, g * HD:(g + 1) * HD] = o.astype(BF16)
            o32_ref[:, g * HD:(g + 1) * HD] = o
            lse_ref[g] = jnp.broadcast_to(m[g] + jnp.log2(l[g]), (bq, HD))

    kvb = pl.BlockSpec((None, T, HD), lambda g, i: (g, 0, 0))
    ob = pl.BlockSpec((bq, GRP * HD), lambda g, i: (i, g))
    return _call_carrying(
        body, exch, name="attn_fwd",
        out_shape=(_sds((N, AH * HD), BF16), _sds((N, AH * HD)), _sds((AH, N, HD))), grid=(AKV, N // bq),
        in_specs=[pl.BlockSpec((GRP, bq, HD), lambda g, i: (g, i + lb, 0)), kvb, kvb],
        out_specs=(ob, ob, pl.BlockSpec((GRP, bq, HD), lambda g, i: (g, i, 0))), vmem=VMEM_BIG)(q, k, v)


def _attn_bwd(q, k, v, o32, lse, do, L, exch, *, bq=128):
    T = q.shape[1]
    N = T - L
    lb = L // bq

    def body(q_ref, k_ref, v_ref, o_ref, lse_ref, do_ref, dq_ref, dk_ref, dv_ref):
        rows = lambda r: jnp.concatenate([r[:, g * HD:(g + 1) * HD] for g in range(GRP)], axis=0)
        lse = jnp.max(lse_ref[...].reshape(GRP * bq, HD), axis=-1, keepdims=True)
        dq, dk, dv = _attn_grad(q_ref[...].reshape(GRP * bq, HD), k_ref[...], v_ref[...], rows(o_ref), lse, rows(do_ref))
        dq_ref[...] = dq.reshape(GRP, bq, HD)

        @pl.when(pl.program_id(1) == 0)
        def _():
            dk_ref[...] = jnp.zeros_like(dk_ref)
            dv_ref[...] = jnp.zeros_like(dv_ref)

        dk_ref[...] += dk
        dv_ref[...] += dv

    kvb = pl.BlockSpec((None, T, HD), lambda g, i: (g, 0, 0))
    qb = pl.BlockSpec((GRP, bq, HD), lambda g, i: (g, i + lb, 0))
    hb = pl.BlockSpec((GRP, bq, HD), lambda g, i: (g, i, 0))
    ob = pl.BlockSpec((bq, GRP * HD), lambda g, i: (i, g))
    return _call_carrying(body, exch, name="attn_bwd",
                          out_shape=(_sds((AH, N, HD)), _sds((AKV, T, HD)), _sds((AKV, T, HD))), grid=(AKV, N // bq),
                          in_specs=[qb, kvb, kvb, ob, hb, ob], out_specs=(hb, kvb, kvb),
                          vmem=VMEM_BIG)(q, k, v, o32, lse, do)


def _gprep_fn(kind, shifts, x, w):
    down, up = shifts
    y = down(x) * w[0:1, :] + x * w[1:2, :] + up(x) * w[2:3, :]
    a = _silu(y)
    if kind == 2:
        return a
    a = a * lax.rsqrt(jnp.sum(a * a, axis=-1, keepdims=True) + EPS)
    return a * (HD ** -0.5) if kind == 0 else a


def _gprep_fwd(proj, conv_w, kind, bounds):
    T = proj.shape[0]
    shifts = _make_shift(bounds)
    cb = C_QKV // HD + kind * GH

    def body(x_ref, w_ref, o_ref):
        o_ref[...] = _gprep_fn(kind, shifts, x_ref[...], w_ref[...])

    return _call(body, name=f"gprep_fwd{kind}", out_shape=_sds((GH, T, HD)), grid=(GH,),
                 in_specs=[pl.BlockSpec((T, HD), lambda h: (0, cb + h)),
                           pl.BlockSpec((3, HD), lambda h: (0, kind * GH + h))],
                 out_specs=pl.BlockSpec((None, T, HD), lambda h: (h, 0, 0)), sem=("parallel",))(proj, conv_w)


def _gprep_bwd(proj, conv_w, kind, bounds, dy, dproj):
    T = proj.shape[0]
    shifts = _make_shift(bounds)
    cb = C_QKV // HD + kind * GH

    def body(x_ref, w_ref, dy_ref, _, dx_ref, dw_ref):
        _, vjp = jax.vjp(functools.partial(_gprep_fn, kind, shifts), x_ref[...], w_ref[...])
        dx, dw = vjp(dy_ref[0] + dy_ref[1])
        dx_ref[...] = dx.astype(BF16)
        dw_ref[...] = dw

    return _call(body, name=f"gprep_bwd{kind}", out_shape=(_sds(dproj.shape, BF16), _sds((3, GH * HD))), grid=(GH,),
                 in_specs=[pl.BlockSpec((T, HD), lambda h: (0, cb + h)),
                           pl.BlockSpec((3, HD), lambda h: (0, kind * GH + h)),
                           pl.BlockSpec((2, None, T, HD), lambda h: (0, h, 0, 0)), ANYSPEC],
                 out_specs=(pl.BlockSpec((T, HD), lambda h: (0, cb + h)), pl.BlockSpec((3, HD), lambda h: (0, h))),
                 aliases={3: 0}, sem=("parallel",))(proj, conv_w, dy, dproj)


def _bl_fn(x, alog, dtb):
    lane = lax.broadcasted_iota(jnp.int32, x.shape, 1)
    beta = jax.nn.sigmoid(x)
    z = x + dtb
    sp = jnp.maximum(z, 0.0) + jnp.log1p(jnp.exp(-jnp.abs(z)))
    la = -jnp.exp(alog) * sp
    return jnp.where(lane < 2 * GH, beta, jnp.where(lane < 4 * GH, la, 0.0))


def _bl_fwd(proj, alog, dtb, *, br=256):
    T = proj.shape[0]

    def body(x_ref, a_ref, d_ref, o_ref):
        o_ref[...] = _bl_fn(x_ref[...], a_ref[...], d_ref[...])

    vec = pl.BlockSpec((1, HD), lambda i: (0, 0))
    return _call(body, name="bl_fwd", out_shape=_sds((T, HD)), grid=(T // br,),
                 in_specs=[pl.BlockSpec((br, HD), lambda i: (i, C_BL // HD)), vec, vec],
                 out_specs=pl.BlockSpec((br, HD), lambda i: (i, 0)), sem=("parallel",))(proj, alog, dtb)


def _bl_bwd(proj, alog, dtb, dbl, dproj, *, br=256):
    T = proj.shape[0]
    wide = C_Z - C_BL

    def body(x_ref, a_ref, d_ref, g_ref, _, dx_ref, da_ref, dd_ref):
        g = g_ref[0, 0]
        for d in range(2):
            for h in range(GH):
                if d or h:
                    g = g + g_ref[d, h]
        _, vjp = jax.vjp(_bl_fn, x_ref[...], a_ref[...], d_ref[...])
        dx, da, dd = vjp(g)
        dx_ref[:, :HD] = dx.astype(BF16)
        dx_ref[:, HD:] = jnp.zeros((br, wide - HD), BF16)

        @pl.when(pl.program_id(0) == 0)
        def _():
            da_ref[...] = jnp.zeros_like(da_ref)
            dd_ref[...] = jnp.zeros_like(dd_ref)

        da_ref[...] += da
        dd_ref[...] += dd

    vec = pl.BlockSpec((1, HD), lambda i: (0, 0))
    return _call(body, name="bl_bwd", out_shape=(_sds(dproj.shape, BF16), _sds((1, HD)), _sds((1, HD))), grid=(T // br,),
                 in_specs=[pl.BlockSpec((br, HD), lambda i: (i, C_BL // HD)), vec, vec,
                           pl.BlockSpec((2, GH, br, HD), lambda i: (0, 0, i, 0)), ANYSPEC],
                 out_specs=(pl.BlockSpec((br, wide), lambda i: (i, C_BL // wide)), vec, vec), aliases={4: 0},
                 sem=("arbitrary",))(proj, alog, dtb, dbl, dproj)


def _chunk_masks(d):
    ii = lax.broadcasted_iota(jnp.int32, (CH, CH), 0)
    jj = lax.broadcasted_iota(jnp.int32, (CH, CH), 1)
    eye = (ii == jj).astype(F32)
    before = jnp.where(d == 0, (jj < ii).astype(F32), (jj > ii).astype(F32))
    return before, before + eye, eye


def _same_block(b):
    ii = lax.broadcasted_iota(jnp.int32, (CH, CH), 0)
    jj = lax.broadcasted_iota(jnp.int32, (CH, CH), 1)
    shift = b.bit_length() - 1
    return (jnp.right_shift(ii, shift) == jnp.right_shift(jj, shift)).astype(F32)


def _intra_fn(masks, sel_b, sel_l, qs, ks, vs, bls, xs=None):
    before, ateq, eye = masks
    ones = jnp.ones((CH, CH), F32)
    inc = ateq > 0.0
    each = lambda f, *ls: [f(*t) for t in zip(*ls)]
    beta = each(lambda bl: jnp.sum(bl * sel_b, axis=-1, keepdims=True), bls)
    la = each(lambda bl: jnp.sum(bl * sel_l, axis=-1, keepdims=True), bls)
    gam = each(lambda a: _mask_nn(ateq, jnp.broadcast_to(a, (CH, HD))), la)
    gi = each(lambda a: _mask_nn(ateq, jnp.broadcast_to(a, (CH, CH))), la)
    gj = each(lambda g: _mask_nn(ones, eye * g), gi)
    kk = each(lambda k: _nt(k, k), ks)
    qk = each(_nt, qs, ks)
    dec = each(lambda a, b: jnp.where(inc, jnp.exp(jnp.where(inc, a - b, 0.0)), 0.0), gi, gj)
    lmat = each(lambda b, d, m: before * (b * d * m), beta, dec, kk)
    if xs is None:
        same = lambda b: _same_block(b)
        l8 = each(lambda m: m * same(8), lmat)
        x = each(lambda m: eye - m, l8)
        p2 = each(lambda m: _mdot(m, m), l8)
        y = each(lambda a, b: _mdot(jnp.concatenate([a, b], axis=0), b), x, p2)
        x = each(lambda a, t: a + t[:CH], x, y)
        x = each(lambda a, t: a + _mdot(a, t[CH:]), x, y)
        for b in (8, 16, 32):
            below = same(2 * b) - same(b)
            x = each(lambda a, m: a - _mdot(a, _mdot(m * below, a)), x, lmat)
    else:
        x = each(_saved_inverse, lmat, xs)
    eg = each(jnp.exp, gam)
    u = each(lambda a, b, v: _mdot(a, b * v), x, beta, vs)
    w = each(lambda a, b, e, k: _mdot(a, (b * e) * k), x, beta, eg, ks)
    tot = each(lambda a: jnp.sum(a, axis=0, keepdims=True), la)
    kd = each(lambda k, t, g: k * jnp.exp(t - g), ks, tot, gam)
    gl = each(lambda t: jnp.broadcast_to(jnp.exp(t), (1, HD)), tot)
    qd = each(lambda q, e: q * e, qs, eg)
    p = each(lambda d, m: d * m, dec, qk)
    return (u, w, kd, qd, p, gl, x) if xs is None else (u, w, kd, qd, p, gl)


def _dir_head_sel(d, h):
    lane = lax.broadcasted_iota(jnp.int32, (1, HD), 1)
    return (lane == d * GH + h).astype(F32), (lane == 2 * GH + d * GH + h).astype(F32)


def _intra_specs(T, G):
    nc = T // CH
    assert nc % G == 0
    qkv = pl.BlockSpec((None, G * CH, HD), lambda d, h, c: (h, c, 0))
    bl = pl.BlockSpec((G * CH, HD), lambda d, h, c: (c, 0))
    big = pl.BlockSpec((None, None, G * CH, HD), lambda d, h, c: (d, h, c, 0))
    pm = pl.BlockSpec((None, None, G * CH, CH), lambda d, h, c: (d, h, c, 0))
    gl = pl.BlockSpec((None, None, G, 1, HD), lambda d, h, c: (d, h, c, 0, 0))
    shapes = (_sds((2, GH, T, HD)),) + (_sds((2, GH, T, HD), BF16),) * 3 + (
        _sds((2, GH, T, CH), BF16), _sds((2, GH, nc, 1, HD)), _sds((2, GH, T, CH)))
    return nc, qkv, bl, big, pm, gl, shapes


def _chunks_per_step(T, most):
    nc = T // CH
    return max(g for g in range(1, most + 1) if nc % g == 0)


def _chunk_at(g, d, nc, ncc):
    pos = _visit_pos(g, d, nc, ncc)
    return pos, pl.ds(pl.multiple_of(pos * CH, CH), CH)


def _intra_fwd(q, k, v, bl, L, exch):
    T = q.shape[1]
    G = _chunks_per_step(T, INTRA_FWD_CHUNKS)
    nc, qkv_s, bl_s, big, pm, gl_s, shapes = _intra_specs(T, G)
    assert G == nc
    ncc = L // CH

    def body(q_ref, k_ref, v_ref, bl_ref, u_ref, w_ref, kd_ref, qd_ref, p_ref, gl_ref, x_ref):
        d, h = pl.program_id(0), pl.program_id(1)
        sb, sl = _dir_head_sel(d, h)
        rows = [slice(g * CH, (g + 1) * CH) for g in range(G)]
        outs = _intra_fn(_chunk_masks(d), sb, sl, *[[r[s, :] for s in rows] for r in (q_ref, k_ref, v_ref, bl_ref)])
        for g in range(G):
            pos, at = _chunk_at(g, d, nc, ncc)
            for r, o in zip((u_ref, w_ref, kd_ref, qd_ref, p_ref, x_ref), outs[:5] + outs[6:]):
                r[at, :] = o[g].astype(r.dtype)
            gl_ref[pos] = outs[5][g]

    return _call_carrying(body, exch, name="gdn_intra_fwd", out_shape=shapes, grid=(2, GH, nc // G),
                          in_specs=[qkv_s, qkv_s, qkv_s, bl_s], out_specs=(big, big, big, big, pm, gl_s, pm))(q, k, v, bl)


def _intra_bwd(q, k, v, bl, xinv, cts, L, exch):
    T = q.shape[1]
    G = _chunks_per_step(T, INTRA_BWD_CHUNKS)
    nc, qkv_s, bl_s, big, pm, gl_s, _ = _intra_specs(T, G)
    assert G == nc
    ncc = L // CH

    def body(q_ref, k_ref, v_ref, bl_ref, x_ref, du, dw, dkd, dqd, dp, dgl, dq_ref, dk_ref, dv_ref, dbl_ref):
        d, h = pl.program_id(0), pl.program_id(1)
        sb, sl = _dir_head_sel(d, h)
        rows = [slice(g * CH, (g + 1) * CH) for g in range(G)]
        places = [_chunk_at(g, d, nc, ncc) for g in range(G)]
        fn = functools.partial(_intra_fn, _chunk_masks(d), sb, sl, xs=[x_ref[at, :] for _, at in places])
        _, vjp = jax.vjp(fn, *[[r[s, :] for s in rows] for r in (q_ref, k_ref, v_ref, bl_ref)])
        cts = tuple([r[at, :] for _, at in places] for r in (du, dw, dkd, dqd, dp)) + ([dgl[pos] for pos, _ in places],)
        grads = vjp(cts)
        for g in range(G):
            for r, o in zip((dq_ref, dk_ref, dv_ref, dbl_ref), grads):
                r[rows[g], :] = o[g]

    return _call_carrying(body, exch, name="gdn_intra_bwd", out_shape=(_sds((2, GH, T, HD)),) * 4,
                          grid=(2, GH, nc // G), in_specs=[qkv_s, qkv_s, qkv_s, bl_s, pm, big, big, big, big, pm, gl_s],
                          out_specs=(big,) * 4)(q, k, v, bl, xinv, *cts)


def _scan_fn(s, u, w, kd, qd, p, gl):
    each = lambda f, *ls: [f(*t) for t in zip(*ls)]
    ws = each(_nn, w, s)
    delta = each(lambda a, b: a - b, u, ws)
    kdd = each(_tn, kd, delta)
    s_new = each(lambda g, a, b: g * a + b, gl, s, kdd)
    qs = each(_nn, qd, s)
    pd = each(_nn, p, delta)
    return each(lambda a, b: a + b, qs, pd), s_new


SCAN_BLOCK = 4


def _visit_pos(c, d, nc, ncc):
    back = ncc - 1 - c if c < ncc else ncc + (nc - 1 - c)
    return jnp.where(d == 0, c, back)


def _scan_specs(T, L, back):
    tb = SCAN_BLOCK * CH
    assert T % tb == 0 and L % tb == 0
    nb, ncb = T // tb, L // tb
    at = (lambda t: nb - 1 - t) if back else (lambda t: t)
    big = pl.BlockSpec((2, GH, tb, HD), lambda t: (0, 0, at(t), 0))
    pm = pl.BlockSpec((2, GH, tb, CH), lambda t: (0, 0, at(t), 0))
    gl = pl.BlockSpec((2, GH, SCAN_BLOCK, 1, HD), lambda t: (0, 0, at(t), 0, 0))
    st = pl.BlockSpec((2, GH, SCAN_BLOCK, HD, HD), lambda t: (0, 0, at(t), 0, 0))

    def natural(b):
        return jnp.where(b < ncb, ncb - 1 - b, nb - 1 - (b - ncb))

    do_specs = (pl.BlockSpec((GH, tb, HD), lambda t: (0, at(t), 0)),
                pl.BlockSpec((GH, tb, HD), lambda t: (0, natural(at(t)), 0)))
    return nb, big, pm, gl, st, do_specs


SCAN_STREAMS = [(d, h) for d in (0, 1) for h in range(GH)]


def _scan_fwd(u, w, kd, qd, p, gl, L):
    T = u.shape[2]
    nb, big, pm, gl_s, st, _ = _scan_specs(T, L, False)

    def body(u_ref, w_ref, kd_ref, qd_ref, p_ref, gl_ref, o_ref, st_ref, s_scr):
        @pl.when(pl.program_id(0) == 0)
        def _():
            s_scr[...] = jnp.zeros_like(s_scr)

        s = [s_scr[d, h] for d, h in SCAN_STREAMS]
        for i in range(SCAN_BLOCK):
            rows = slice(i * CH, (i + 1) * CH)
            for (d, h), sv in zip(SCAN_STREAMS, s):
                st_ref[d, h, i] = sv
            o, s = _scan_fn(s, *[[r[d, h, rows, :].astype(F32) for d, h in SCAN_STREAMS]
                                 for r in (u_ref, w_ref, kd_ref, qd_ref, p_ref)],
                            [gl_ref[d, h, i] for d, h in SCAN_STREAMS])
            for (d, h), ov in zip(SCAN_STREAMS, o):
                o_ref[d, h, rows, :] = ov
        for (d, h), sv in zip(SCAN_STREAMS, s):
            s_scr[d, h] = sv

    return _call(body, name="gdn_scan_fwd", out_shape=(_sds((2, GH, T, HD)), _sds((2, GH, T // CH, HD, HD))),
                 grid=(nb,), in_specs=[big, big, big, big, pm, gl_s], out_specs=(big, st),
                 scratch=[pltpu.VMEM((2, GH, HD, HD), F32)], sem=("arbitrary",), vmem=VMEM_BIG)(u, w, kd, qd, p, gl)


def _scan_bwd(u, w, kd, qd, p, gl, states, do, L, exch):
    T = u.shape[2]
    nb, big, pm, gl_s, st, do_specs = _scan_specs(T, L, True)

    def body(u_ref, w_ref, kd_ref, qd_ref, p_ref, gl_ref, st_ref, do0_ref, do1_ref,
             du_ref, dw_ref, dkd_ref, dqd_ref, dp_ref, dgl_ref, ds_scr):
        @pl.when(pl.program_id(0) == 0)
        def _():
            ds_scr[...] = jnp.zeros_like(ds_scr)

        ds = [ds_scr[d, h] for d, h in SCAN_STREAMS]
        for i in reversed(range(SCAN_BLOCK)):
            rows = slice(i * CH, (i + 1) * CH)
            mirror = slice((SCAN_BLOCK - 1 - i) * CH, (SCAN_BLOCK - i) * CH)
            _, vjp = jax.vjp(_scan_fn, [st_ref[d, h, i] for d, h in SCAN_STREAMS],
                             *[[r[d, h, rows, :].astype(F32) for d, h in SCAN_STREAMS]
                               for r in (u_ref, w_ref, kd_ref, qd_ref, p_ref)],
                             [gl_ref[d, h, i] for d, h in SCAN_STREAMS])
            dos = [do0_ref[h, rows, :] if d == 0 else do1_ref[h, mirror, :] for d, h in SCAN_STREAMS]
            ds, gu, gw, gkd, gqd, gp, ggl = vjp((dos, ds))
            for n, (d, h) in enumerate(SCAN_STREAMS):
                du_ref[d, h, rows, :] = gu[n]
                dw_ref[d, h, rows, :] = gw[n]
                dkd_ref[d, h, rows, :] = gkd[n]
                dqd_ref[d, h, rows, :] = gqd[n]
                dp_ref[d, h, rows, :] = gp[n]
                dgl_ref[d, h, i] = ggl[n]
        for (d, h), dv in zip(SCAN_STREAMS, ds):
            ds_scr[d, h] = dv

    return _call_carrying(
        body, exch, name="gdn_scan_bwd",
        out_shape=(_sds((2, GH, T, HD)),) * 4 + (_sds((2, GH, T, CH)), _sds((2, GH, T // CH, 1, HD))),
        grid=(nb,), in_specs=[big, big, big, big, pm, gl_s, st, *do_specs], out_specs=(big, big, big, big, pm, gl_s),
        scratch=[pltpu.VMEM((2, GH, HD, HD), F32)], vmem=VMEM_BIG)(u, w, kd, qd, p, gl, states, do, do)


def _gout_fn(o0, o1, z, gw):
    return _rms(o0 + o1) * gw * _silu(z)


def _backward_latent(o_ref, L):
    nl = (o_ref.shape[1] - L) // CH
    return jnp.concatenate([o_ref[1, L + (nl - 1 - j) * CH:L + (nl - j) * CH, :] for j in range(nl)], axis=0)


def _gout_fwd(o, proj, gw, L):
    T = o.shape[2]
    N = T - L
    ob = pl.BlockSpec((2, None, T, HD), lambda h: (0, h, 0, 0))

    def body(o_ref, z_ref, gw_ref, y_ref):
        y_ref[...] = _gout_fn(o_ref[0, L:, :], _backward_latent(o_ref, L), z_ref[L:, :], gw_ref[...]).astype(BF16)

    return _call(body, name="gout_fwd", out_shape=_sds((N, GH * HD), BF16), grid=(GH,),
                 in_specs=[ob, pl.BlockSpec((T, HD), lambda h: (0, C_Z // HD + h)), pl.BlockSpec((1, HD), lambda h: (0, 0))],
                 out_specs=pl.BlockSpec((N, HD), lambda h: (0, h)), sem=("parallel",))(o, proj, gw)


def _gout_bwd(o, proj, gw, dy, dproj, L):
    T = o.shape[2]
    N = T - L
    ob = pl.BlockSpec((2, None, T, HD), lambda h: (0, h, 0, 0))

    def body(o_ref, z_ref, gw_ref, dy_ref, _, do_ref, dz_ref, dgw_ref):
        _, vjp = jax.vjp(_gout_fn, o_ref[0, L:, :], _backward_latent(o_ref, L), z_ref[L:, :], gw_ref[...])
        g0, _, gz, ggw = vjp(dy_ref[...])
        do_ref[:L, :] = jnp.zeros((L, HD), F32)
        do_ref[L:, :] = g0
        dz_ref[:L, :] = jnp.zeros((L, HD), BF16)
        dz_ref[L:, :] = gz.astype(BF16)

        @pl.when(pl.program_id(0) == 0)
        def _():
            dgw_ref[...] = jnp.zeros_like(dgw_ref)

        dgw_ref[...] += ggw

    zb = pl.BlockSpec((T, HD), lambda h: (0, C_Z // HD + h))
    return _call(body, name="gout_bwd", out_shape=(_sds((GH, T, HD)), _sds(dproj.shape, BF16), _sds((1, HD))),
                 grid=(GH,),
                 in_specs=[ob, zb, pl.BlockSpec((1, HD), lambda h: (0, 0)), pl.BlockSpec((N, HD), lambda h: (0, h)), ANYSPEC],
                 out_specs=(pl.BlockSpec((None, T, HD), lambda h: (h, 0, 0)), zb, pl.BlockSpec((1, HD), lambda h: (0, 0))),
                 aliases={4: 1}, sem=("arbitrary",))(o, proj, gw, dy, dproj)


def _merge_fn(pa, pd, ga, gd):
    return jax.nn.sigmoid(ga) * pa + jax.nn.sigmoid(gd) * pd


def _merge_fwd(pa, pd, proj, L, *, br=256):
    N = pa.shape[0]
    lb = L // br
    row = pl.BlockSpec((br, D), lambda i: (i, 0))

    def body(pa_ref, pd_ref, ga_ref, gd_ref, y_ref):
        y_ref[...] = _merge_fn(pa_ref[...], pd_ref[...], ga_ref[...], gd_ref[...]).astype(BF16)

    return _call(body, name="merge_fwd", out_shape=_sds((N, D), BF16), grid=(N // br,),
                 in_specs=[row, row, pl.BlockSpec((br, D), lambda i: (i + lb, C_GATE // D)),
                           pl.BlockSpec((br, D), lambda i: (i + lb, C_GATE // D + 1))],
                 out_specs=row, sem=("parallel",))(pa, pd, proj, proj)


def _merge_bwd(pa, pd, proj, dy, L, *, br=256):
    N = pa.shape[0]
    T = N + L
    lb = L // br
    lrow = pl.BlockSpec((br, D), lambda i: (jnp.maximum(i - lb, 0), 0))

    def body(pa_ref, pd_ref, ga_ref, gd_ref, dy_ref, dpa_ref, dpd_ref, dg_ref):
        lat = pl.program_id(0) >= lb
        _, vjp = jax.vjp(_merge_fn, pa_ref[...], pd_ref[...], ga_ref[...], gd_ref[...])
        gpa, gpd, gga, ggd = vjp(dy_ref[...])
        dpa_ref[...] = gpa.astype(BF16)
        dpd_ref[...] = gpd.astype(BF16)
        dg_ref[:, :D] = jnp.where(lat, gga, 0.0).astype(BF16)
        dg_ref[:, D:] = jnp.where(lat, ggd, 0.0).astype(BF16)

    return _call(body, name="merge_bwd", out_shape=(_sds((N, D), BF16), _sds((N, D), BF16), _sds((T, C_END), BF16)),
                 grid=(T // br,),
                 in_specs=[lrow, lrow, pl.BlockSpec((br, D), lambda i: (i, C_GATE // D)),
                           pl.BlockSpec((br, D), lambda i: (i, C_GATE // D + 1)), lrow],
                 out_specs=(lrow, lrow, pl.BlockSpec((br, 2 * D), lambda i: (i, C_GATE // (2 * D)))),
                 sem=("arbitrary",))(pa, pd, proj, proj, dy)


def _resid_fwd(x, m, mod, i_g, *, name, br=256):
    R = x.shape[0]
    row = pl.BlockSpec((br, D), lambda i: (i, 0))

    def body(x_ref, m_ref, mod_ref, o_ref):
        o_ref[...] = x_ref[...] + mod_ref[i_g:i_g + 1, :] * m_ref[...]

    return _call(body, name=name, out_shape=_sds((R, D)), grid=(R // br,),
                 in_specs=[row, row, pl.BlockSpec((6, D), lambda i: (0, 0))], out_specs=row,
                 sem=("parallel",))(x, m, mod)


def _resid_bwd(dx, m, mod, i_g, *, name, br=256):
    R = dx.shape[0]
    row = pl.BlockSpec((br, D), lambda i: (i, 0))
    vec = pl.BlockSpec((1, D), lambda i: (0, 0))

    def body(dx_ref, m_ref, mod_ref, dm_ref, dg_ref):
        dxv = dx_ref[...]
        dm_ref[...] = (dxv * mod_ref[i_g:i_g + 1, :]).astype(BF16)

        @pl.when(pl.program_id(0) == 0)
        def _():
            dg_ref[...] = jnp.zeros_like(dg_ref)

        dg_ref[...] += jnp.sum(dxv * m_ref[...], axis=0, keepdims=True)

    return _call(body, name=name, out_shape=(_sds((R, D), BF16), _sds((1, D))), grid=(R // br,),
                 in_specs=[row, row, pl.BlockSpec((6, D), lambda i: (0, 0))], out_specs=(row, vec),
                 sem=("arbitrary",))(dx, m, mod)


def _ffn_fn(shifts, ug, uv, wg, wv, bg, bv):
    down, up = shifts

    def conv(x, w, b):
        return down(x) * w[0:1, :] + x * w[1:2, :] + up(x) * w[2:3, :] + b

    return _silu(conv(ug, wg, bg)) * conv(uv, wv, bv)


def _ffn_fwd(up, cw, cb, *, bw=256):
    N = up.shape[0]
    shifts = _make_shift(((0, N),))
    nb = DFF // bw

    def body(ug, uv, wg, wv, bg, bv, a_ref):
        a_ref[...] = _ffn_fn(shifts, ug[...], uv[...], wg[...], wv[...], bg[...], bv[...]).astype(BF16)

    def col(rows, off):
        return pl.BlockSpec((rows, bw), lambda j: (0, j + off))

    return _call(body, name="ffn_fwd", out_shape=_sds((N, DFF), BF16), grid=(nb,),
                 in_specs=[col(N, 0), col(N, nb), col(3, 0), col(3, nb), col(1, 0), col(1, nb)],
                 out_specs=col(N, 0), sem=("parallel",), vmem=VMEM_BIG)(up, up, cw, cw, cb, cb)


def _ffn_bwd(up, cw, cb, da, *, bw=256):
    N = up.shape[0]
    shifts = _make_shift(((0, N),))
    nb = DFF // bw

    def body(ug, uv, wg, wv, bg, bv, da_ref, dug, duv, dwg, dwv, dbg, dbv):
        _, vjp = jax.vjp(functools.partial(_ffn_fn, shifts), ug[...], uv[...], wg[...], wv[...], bg[...], bv[...])
        g = vjp(da_ref[...])
        dug[...] = g[0].astype(BF16)
        duv[...] = g[1].astype(BF16)
        dwg[...], dwv[...], dbg[...], dbv[...] = g[2], g[3], g[4], g[5]

    def col(rows, off):
        return pl.BlockSpec((rows, bw), lambda j: (0, j + off))

    half = (_sds((N, DFF), BF16), _sds((N, DFF), BF16), _sds((3, DFF)), _sds((3, DFF)), _sds((1, DFF)), _sds((1, DFF)))
    dug, duv, dwg, dwv, dbg, dbv = _call(
        body, name="ffn_bwd", out_shape=half, grid=(nb,),
        in_specs=[col(N, 0), col(N, nb), col(3, 0), col(3, nb), col(1, 0), col(1, nb), col(N, 0)],
        out_specs=(col(N, 0), col(N, 0), col(3, 0), col(3, 0), col(1, 0), col(1, 0)),
        sem=("parallel",), vmem=VMEM_BIG)(up, up, cw, cw, cb, cb, da)
    return (jnp.concatenate([dug, duv], axis=1), jnp.concatenate([dwg, dwv], axis=1),
            jnp.concatenate([dbg, dbv], axis=1))


def _head_fn(x1, dn, g2, fw, tgt):
    y = _rms(x1 + g2 * dn) * fw
    err = y - tgt
    return 0.5 * jnp.sum(jnp.mean(err * err, axis=-1))


def _head(x1, dn, mod, fw, tgt, *, br=256):
    N = x1.shape[0]
    row = pl.BlockSpec((br, D), lambda i: (i, 0))
    vec = pl.BlockSpec((1, D), lambda i: (0, 0))
    one = pl.BlockSpec((1, HD), lambda i: (0, 0))

    def body(x1_ref, dn_ref, mod_ref, fw_ref, tgt_ref, loss_ref, dx_ref, ddn_ref, dg_ref, dfw_ref):
        loss, (gx, gdn, gg, gfw) = jax.value_and_grad(_head_fn, argnums=(0, 1, 2, 3))(
            x1_ref[...], dn_ref[...], mod_ref[5:6, :], fw_ref[...], tgt_ref[...])
        dx_ref[...] = gx
        ddn_ref[...] = gdn.astype(BF16)

        @pl.when(pl.program_id(0) == 0)
        def _():
            loss_ref[...] = jnp.zeros_like(loss_ref)
            dg_ref[...] = jnp.zeros_like(dg_ref)
            dfw_ref[...] = jnp.zeros_like(dfw_ref)

        loss_ref[...] += jnp.broadcast_to(loss, (1, HD))
        dg_ref[...] += gg
        dfw_ref[...] += gfw

    return _call(body, name="head", out_shape=(_sds((1, HD)), _sds((N, D)), _sds((N, D), BF16), _sds((1, D)), _sds((1, D))),
                 grid=(N // br,), in_specs=[row, row, pl.BlockSpec((6, D), lambda i: (0, 0)), vec, row],
                 out_specs=(one, row, row, vec, vec), sem=("arbitrary",))(x1, dn, mod, fw, tgt)


def _adamw(w, g, m, v, *, name):
    shape = w.shape
    cols = shape[-1]
    rows = max(1, math.prod(shape[:-1]))
    w2, g2, m2, v2 = (t.reshape(rows, cols) for t in (w, g, m, v))
    br = 256 if rows % 256 == 0 else rows
    c1 = 1.0 - B1 ** STEP
    c2 = 1.0 - B2 ** STEP

    def body(w_ref, g_ref, m_ref, v_ref, d_ref, nm_ref, nv_ref):
        gv = g_ref[...]
        nm = B1 * m_ref[...] + (1.0 - B1) * gv
        nv = B2 * v_ref[...] + (1.0 - B2) * (gv * gv)
        d_ref[...] = -LR * ((nm / c1) / (jnp.sqrt(nv / c2) + AEPS) + WD * w_ref[...])
        nm_ref[...] = nm
        nv_ref[...] = nv

    blk = pl.BlockSpec((br, cols), lambda i: (i, 0))
    outs = _call(body, name=name, out_shape=(_sds((rows, cols)),) * 3, grid=(rows // br,),
                 in_specs=[blk] * 4, out_specs=(blk,) * 3, sem=("parallel",))(w2, g2, m2, v2)
    return tuple(t.reshape(shape) for t in outs)


def _adamw_many(items, *, name):
    k = len(items)
    shapes = [w.shape for w, _, _, _ in items]
    flat = [t.reshape(max(1, math.prod(t.shape[:-1])), t.shape[-1]) for it in items for t in it]
    c1 = 1.0 - B1 ** STEP
    c2 = 1.0 - B2 ** STEP

    def body(*refs):
        ins, outs = refs[:4 * k], refs[4 * k:]
        for i in range(k):
            w_ref, g_ref, m_ref, v_ref = ins[4 * i:4 * i + 4]
            gv = g_ref[...]
            nm = B1 * m_ref[...] + (1.0 - B1) * gv
            nv = B2 * v_ref[...] + (1.0 - B2) * (gv * gv)
            outs[3 * i][...] = -LR * ((nm / c1) / (jnp.sqrt(nv / c2) + AEPS) + WD * w_ref[...])
            outs[3 * i + 1][...] = nm
            outs[3 * i + 2][...] = nv

    res = _call(body, name=name, out_shape=tuple(_sds(flat[4 * i].shape) for i in range(k) for _ in range(3)))(*flat)
    return [tuple(res[3 * i + j].reshape(shapes[i]) for j in range(3)) for i in range(k)]


def _rope_tables(N, L):
    t = jnp.arange(N)
    pos = jnp.stack([(t // GRID_W).astype(F32), (t % GRID_W).astype(F32)], axis=1)
    inv = ROPE_THETA ** (-jnp.arange(0, HD // 2, 2, dtype=F32) / (HD // 2))
    ang = pos[:, :, None] * inv[None, None, :]
    cos = jnp.broadcast_to(jnp.cos(ang)[:, :, None, :], (N, 2, 2, HD // 4)).reshape(N, HD)
    sin = jnp.broadcast_to(jnp.sin(ang)[:, :, None, :], (N, 2, 2, HD // 4))
    sin = (sin * jnp.array([-1.0, 1.0], F32)[None, None, :, None]).reshape(N, HD)
    cos = jnp.concatenate([jnp.ones((L, HD), F32), cos], axis=0)
    sin = jnp.concatenate([jnp.zeros((L, HD), F32), sin], axis=0)
    return cos, sin


def _pad_lanes(v, off=0):
    return jnp.zeros((1, HD), F32).at[0, off:off + v.shape[0]].set(v)


def _local_step(x, ctx, tgt, mod_lat, mod_ctx, w_in, shards, small):
    N, L = x.shape[0], ctx.shape[0]
    T = N + L
    bounds = ((0, L), (L, T))
    qw, kw, gw = small["q_norm_w"], small["k_norm_w"], small["gdn_norm_w"]
    conv_w, ffn_w, ffn_b, fnw = small["conv_qkv_w"], small["ffn_conv_w"], small["ffn_conv_b"], small["final_norm_w"]
    alog = _pad_lanes(small["a_log"].reshape(-1), 2 * GH)
    dtb = _pad_lanes(small["dt_bias"].reshape(-1), 2 * GH)
    cos, sin = _rope_tables(N, L)
    bt = T
    bnl = 256 if N % 1024 else 1024

    hc = _normmod_fwd(ctx, mod_ctx, 0, 1, name="normmod_ctx")
    hx = _normmod_fwd(x, mod_lat, 0, 1, name="normmod_x")
    h1 = jnp.concatenate([hc, hx], axis=0)
    proj = _mm(h1, w_in, name="mm_in", M=T, N=C_END, K=D, tb=True, bm=bt, bn=1024)
    aq, ak, av = _aprep_fwd(proj, cos, sin, qw, kw)
    (attn, attn32, lse), (up_g,) = _attn_fwd(aq, ak, av, L, _GatherTwoLevel([shards["w_up"]]))
    gq = _gprep_fwd(proj, conv_w, 0, bounds)
    gk = _gprep_fwd(proj, conv_w, 1, bounds)
    gv = _gprep_fwd(proj, conv_w, 2, bounds)
    bl = _bl_fwd(proj, alog, dtb)
    intra, (down_g, pa_g, pd_g, out_g) = _intra_fwd(
        gq, gk, gv, bl, L, _GatherTwoLevel([shards[n] for n in ("w_down", "w_pa", "w_pd", "w_out")]))
    w_up, w_down = up_g.reshape(2 * DFF, D), down_g.reshape(DFF, D)
    w_pa, w_pd, w_out = pa_g.reshape(D, D), pd_g.reshape(D, D), out_g.reshape(D, D)
    xinv, intra = intra[6], intra[:6]
    o, states = _scan_fwd(*intra, L)
    gdn = _gout_fwd(o, proj, gw, L)
    pa = _mm(attn, w_pa, name="mm_pa", M=N, N=D, K=D, bm=bnl)
    pd = _mm(gdn, w_pd, name="mm_pd", M=N, N=D, K=D, bm=bnl)
    y = _merge_fwd(pa, pd, proj, L)
    m = _mm(y, w_out, name="mm_out", M=N, N=D, K=D, bm=bnl)
    x1 = _resid_fwd(x, m, mod_lat, 2, name="resid1")
    h2 = _normmod_fwd(x1, mod_lat, 3, 4, name="normmod_x1")
    up = _mm(h2, w_up, name="mm_up", M=N, N=2 * DFF, K=D, tb=True, bm=bnl, bn=2 * DFF // 4)
    a = _ffn_fwd(up, ffn_w, ffn_b)
    dn = _mm(a, w_down, name="mm_down", M=N, N=D, K=DFF, bm=bnl)
    loss, dx2, ddn, dg2, dfnw = _head(x1, dn, mod_lat, fnw, tgt)

    da = _mm(ddn, w_down, name="mm_down_dx", M=N, N=DFF, K=D, tb=True, bm=bnl, bn=DFF // 2)
    g_down = _mm(a, ddn, name="mm_down_dw", M=DFF, N=D, K=N, ta=True, bm=DFF // 2, out_dtype=BF16)
    dup, d_ffn_w, d_ffn_b = _ffn_bwd(up, ffn_w, ffn_b, da)
    dh2 = _mm(dup, w_up, name="mm_up_dx", M=N, N=D, K=2 * DFF, bm=bnl, bk=2 * DFF // 4)
    g_up = _mm(dup, h2, name="mm_up_dw", M=2 * DFF, N=D, K=N, ta=True, bm=2 * DFF // 4, out_dtype=BF16)
    dx1, dsh2, dsc2 = _normmod_bwd(x1, mod_lat, 3, 4, dh2, 0, dx2, name="normmod_x1_bwd")
    dm, dg1 = _resid_bwd(dx1, m, mod_lat, 2, name="resid1_bwd")
    dy = _mm(dm, w_out, name="mm_out_dx", M=N, N=D, K=D, tb=True, bm=bnl)
    g_out = _mm(y, dm, name="mm_out_dw", M=D, N=D, K=N, ta=True, out_dtype=BF16)
    dpa, dpd, dproj = _merge_bwd(pa, pd, proj, dy, L)
    dattn = _mm(dpa, w_pa, name="mm_pa_dx", M=N, N=D, K=D, tb=True, bm=bnl)
    g_pa = _mm(attn, dpa, name="mm_pa_dw", M=D, N=D, K=N, ta=True, out_dtype=BF16)
    dgdn = _mm(dpd, w_pd, name="mm_pd_dx", M=N, N=D, K=D, tb=True, bm=bnl)
    g_pd = _mm(gdn, dpd, name="mm_pd_dw", M=D, N=D, K=N, ta=True, out_dtype=BF16)
    do, dproj, dgw = _gout_bwd(o, proj, gw, dgdn, dproj, L)
    cts, recv_a = _scan_bwd(*intra, states, do, L, _Exchange([g_out.reshape(NDEV, D // NDEV, D)], True))
    (dgq, dgk, dgv, dbl), recv_b = _intra_bwd(gq, gk, gv, bl, xinv, cts, L, _Exchange(
        [g_pa.reshape(NDEV, D // NDEV, D), g_pd.reshape(NDEV, D // NDEV, D), g_up.reshape(NDEV, 2 * DFF // NDEV, D)], True))
    dproj, dwq = _gprep_bwd(proj, conv_w, 0, bounds, dgq, dproj)
    dproj, dwk = _gprep_bwd(proj, conv_w, 1, bounds, dgk, dproj)
    dproj, dwv = _gprep_bwd(proj, conv_w, 2, bounds, dgv, dproj)
    dproj, dalog, ddtb = _bl_bwd(proj, alog, dtb, dbl, dproj)
    (daq_h, dak_h, dav_h), recv_c = _attn_bwd(aq, ak, av, attn32, lse, dattn, L, _Exchange(
        [g_down.reshape(NDEV, DFF // NDEV, D)], True))
    recv = dict(zip(("w_out", "w_pa", "w_pd", "w_up", "w_down"), recv_a + recv_b + recv_c))
    dproj, dqw, dkw = _aprep_bwd(proj, cos, sin, qw, kw, daq_h, dak_h, dav_h, dproj, L)
    g_in = _mm(dproj, h1, name="mm_in_dw", M=C_END, N=D, K=T, ta=True, bm=1024, out_dtype=BF16)
    g_in = _unpad_columns(g_in).reshape(NDEV, W_END // NDEV, D)
    own_in = lax.dynamic_index_in_dim(g_in, _position()[3], axis=0, keepdims=False)
    *pending, token = _scatter_start(g_in, None, (0, D // 2), (), name="scatter_g_in_a_start")
    dh1 = _mm(dproj, w_in, name="mm_in_dx", M=T, N=D, K=C_END, bm=bt, bk=1024, after=(token,))
    grad_x, dsh1, dsc1 = _normmod_bwd(x, mod_lat, 0, 1, dh1, L, dx1, name="normmod_x_bwd")
    _, dcsh1, dcsc1 = _normmod_bwd(ctx, mod_ctx, 0, 1, dh1, 0, None, name="normmod_ctx_bwd")

    z1 = jnp.zeros((1, D), F32)
    dmod_lat = jnp.concatenate([dsh1, dsc1, dg1, dsh2, dsc2, dg2], axis=0)
    dmod_ctx = jnp.concatenate([dcsh1, dcsc1, z1, z1, z1, z1], axis=0)
    gsmall = {
        "q_norm_w": dqw, "k_norm_w": dkw, "gdn_norm_w": dgw,
        "conv_qkv_w": jnp.concatenate([dwq, dwk, dwv], axis=1),
        "a_log": dalog[0, 2 * GH:4 * GH], "dt_bias": ddtb[0, 2 * GH:4 * GH],
        "ffn_conv_w": d_ffn_w, "ffn_conv_b": d_ffn_b, "final_norm_w": dfnw,
    }
    return loss[0, 0], grad_x, (pending, own_in), recv, dmod_lat, dmod_ctx, gsmall


HBM = pl.BlockSpec(memory_space=pltpu.HBM)
ANYSPEC = pl.BlockSpec(memory_space=pl.ANY)


def _position():
    x, y, c = lax.axis_index("x"), lax.axis_index("y"), lax.axis_index("c")
    return x, y, c, 4 * x + 2 * y + c


def _peer(x, y, c, k):
    px = 1 - x if k & 4 else x
    py = 1 - y if k & 2 else y
    pc = 1 - c if k & 1 else c
    return (px, py, pc), 4 * px + 2 * py + pc


def _exchange(arrs, *, name, scatter):
    exch = _Exchange(arrs, scatter)
    n = exch.n

    def body(*refs):
        ins, outs, sems = refs[:n], refs[n:2 * n], refs[2 * n:]
        exch.start(ins, outs, sems)
        exch.finish(ins, outs, sems)

    outs = pl.pallas_call(body, name=name, out_shape=exch.out_shape, in_specs=[HBM] * n, out_specs=(HBM,) * n,
                          scratch_shapes=exch.scratch,
                          compiler_params=pltpu.CompilerParams(has_side_effects=True))(*arrs)
    return list(outs)


class _Exchange:
    def __init__(self, arrs, scatter):
        self.arrs, self.scatter, self.n = list(arrs), scatter, len(arrs)
        self.out_shape = tuple(_sds(a.shape if scatter else (NDEV,) + a.shape, a.dtype) for a in arrs)
        self.scratch = [pltpu.SemaphoreType.DMA((self.n, NDEV - 1)), pltpu.SemaphoreType.DMA((self.n, NDEV - 1)),
                        pltpu.SemaphoreType.DMA((self.n,))]

    def _copies(self, ins, outs, sems):
        send, recv, loc = sems
        x, y, c, me = _position()
        local = [pltpu.make_async_copy(ins[a].at[me] if self.scatter else ins[a], outs[a].at[me], loc.at[a])
                 for a in range(self.n)]
        remote = []
        for k in range(1, NDEV):
            peer, pid = _peer(x, y, c, k)
            for a in range(self.n):
                src = ins[a].at[pid] if self.scatter else ins[a]
                remote.append(pltpu.make_async_remote_copy(
                    src_ref=src, dst_ref=outs[a].at[me], send_sem=send.at[a, k - 1], recv_sem=recv.at[a, k - 1],
                    device_id=peer, device_id_type=MESH))
        return local, remote

    def start(self, ins, outs, sems):
        local, remote = self._copies(ins, outs, sems)
        for cp in local + remote:
            cp.start()

    def finish(self, ins, outs, sems):
        local, remote = self._copies(ins, outs, sems)
        for cp in remote:
            cp.wait()
        for cp in local:
            cp.wait()


class _GatherTwoLevel:
    scatter = False

    def __init__(self, arrs):
        self.arrs, self.n = list(arrs), len(arrs)
        self.out_shape = tuple(_sds((NDEV,) + a.shape, a.dtype) for a in arrs)
        self.scratch = [pltpu.SemaphoreType.DMA((self.n, NDEV - 1)), pltpu.SemaphoreType.DMA((self.n, NDEV - 1)),
                        pltpu.SemaphoreType.DMA((self.n,))]

    def _parts(self, ins, outs, sems):
        send, recv, loc = sems
        x, y, c, _ = _position()
        me, sibling = (x, y, c), (x, y, 1 - c)
        chips = [(1 - x, y), (x, 1 - y), (1 - x, 1 - y)]
        parts = []
        for a in range(self.n):
            slot = lambda px, py, pc, a=a: outs[a].at[4 * px + 2 * py + pc]

            def copy(k, owner, to, src=None, a=a, slot=slot):
                return pltpu.make_async_remote_copy(
                    src_ref=slot(*owner) if src is None else src, dst_ref=slot(*owner), send_sem=send.at[a, k],
                    recv_sem=recv.at[a, k], device_id=to, device_id_type=MESH)

            parts.append(dict(
                mine=pltpu.make_async_copy(ins[a], slot(*me), loc.at[a]),
                first=[copy(0, me, sibling, src=ins[a])] + [copy(1 + j, me, (*ch, c), src=ins[a]) for j, ch in enumerate(chips)],
                arrive=[copy(1 + j, (*ch, c), me) for j, ch in enumerate(chips)],
                passed=[copy(4 + j, (*ch, c), sibling) for j, ch in enumerate(chips)],
                rest=[copy(0, sibling, me)] + [copy(4 + j, (*ch, 1 - c), me) for j, ch in enumerate(chips)]))
        return parts

    def start(self, ins, outs, sems):
        for p in self._parts(ins, outs, sems):
            p["mine"].start()
            for cp in p["first"]:
                cp.start()

    def middle(self, ins, outs, sems):
        for p in self._parts(ins, outs, sems):
            for got, fwd in zip(p["arrive"], p["passed"]):
                got.wait_recv()
                fwd.start()

    def finish(self, ins, outs, sems):
        for p in self._parts(ins, outs, sems):
            for cp in p["rest"]:
                cp.wait_recv()
            for cp in p["first"] + p["passed"]:
                cp.wait_send()
            p["mine"].wait()


def _gather_two_level(block, *, name):
    def body(x_ref, out_ref, send_sems, recv_sems, local_sem):
        x, y, c, _ = _position()
        me, sibling = (x, y, c), (x, y, 1 - c)
        chips = [(1 - x, y), (x, 1 - y), (1 - x, 1 - y)]

        def slot(px, py, pc):
            return out_ref.at[4 * px + 2 * py + pc]

        def copy(k, owner, to, src=None):
            return pltpu.make_async_remote_copy(
                src_ref=slot(*owner) if src is None else src, dst_ref=slot(*owner), send_sem=send_sems.at[k],
                recv_sem=recv_sems.at[k], device_id=to, device_id_type=MESH)

        mine = pltpu.make_async_copy(x_ref, slot(*me), local_sem)
        mine.start()
        first = [copy(0, me, sibling, src=x_ref)]
        first += [copy(1 + j, me, (*chip, c), src=x_ref) for j, chip in enumerate(chips)]
        for cp in first:
            cp.start()
        passed = [copy(4 + j, (*chip, c), sibling) for j, chip in enumerate(chips)]
        for j, chip in enumerate(chips):
            copy(1 + j, (*chip, c), me).wait_recv()
            passed[j].start()
        copy(0, sibling, me).wait_recv()
        for j, chip in enumerate(chips):
            copy(4 + j, (*chip, 1 - c), me).wait_recv()
        for cp in first + passed:
            cp.wait_send()
        mine.wait()

    return pl.pallas_call(
        body, name=name, out_shape=_sds((NDEV,) + block.shape, block.dtype), in_specs=[HBM], out_specs=HBM,
        scratch_shapes=[pltpu.SemaphoreType.DMA((NDEV - 1,)), pltpu.SemaphoreType.DMA((NDEV - 1,)),
                        pltpu.SemaphoreType.DMA],
        compiler_params=pltpu.CompilerParams(has_side_effects=True))(block)


SEM = pl.BlockSpec(memory_space=pltpu.SEMAPHORE)


def _scatter_copies(src_ref, land_ref, send_sems, recv_sems, cols):
    x, y, c, me = _position()
    span = (slice(None), pl.ds(*cols))
    copies = []
    for k in range(1, NDEV):
        peer, pid = _peer(x, y, c, k)
        copies.append(pltpu.make_async_remote_copy(
            src_ref=src_ref.at[pid].at[span], dst_ref=land_ref.at[me].at[span], send_sem=send_sems.at[k - 1],
            recv_sem=recv_sems.at[k - 1], device_id=peer, device_id_type=MESH))
    return copies


SPLIT_EFFECT = pltpu.SideEffectType.DATAFLOW_SIDE_EFFECTING


def _scatter_start(parts, land, cols, after, *, name):
    na = len(after)
    if land is None:
        land = lax.empty(parts.shape, parts.dtype)

    def body(src_ref, land_ref, *rest):
        send_sems, recv_sems, _, _, token = rest[na:]
        for cp in _scatter_copies(src_ref, land_ref, send_sems, recv_sems, cols):
            cp.start()
        token[...] = jnp.zeros_like(token)

    return pl.pallas_call(
        body, name=name,
        out_shape=(pltpu.SemaphoreType.DMA((NDEV - 1,)), pltpu.SemaphoreType.DMA((NDEV - 1,)),
                   pltpu.HBM(parts.shape, parts.dtype), pltpu.HBM(parts.shape, parts.dtype), _sds((8, HD))),
        in_specs=(HBM, HBM) + (pl.BlockSpec(memory_space=pl.ANY),) * na,
        out_specs=(SEM, SEM, HBM, HBM, pl.BlockSpec(memory_space=pltpu.VMEM)),
        input_output_aliases={0: 2, 1: 3}, compiler_params=pltpu.CompilerParams(has_side_effects=SPLIT_EFFECT),
    )(pltpu.with_memory_space_constraint(parts, pltpu.HBM), pltpu.with_memory_space_constraint(land, pltpu.HBM), *after)


def _scatter_wait(send_sems, recv_sems, src_thru, land_thru, cols, after, *, name):
    na = len(after)

    def body(src_ref, land_ref, send_sems, recv_sems, *rest):
        for cp in _scatter_copies(src_ref, land_ref, send_sems, recv_sems, cols):
            cp.wait_send()
            cp.wait_recv()

    return pl.pallas_call(
        body, name=name,
        out_shape=(pltpu.HBM(src_thru.shape, src_thru.dtype), pltpu.HBM(land_thru.shape, land_thru.dtype)),
        in_specs=(HBM, HBM, SEM, SEM) + (pl.BlockSpec(memory_space=pl.ANY),) * na, out_specs=(HBM, HBM),
        input_output_aliases={0: 0, 1: 1}, compiler_params=pltpu.CompilerParams(has_side_effects=SPLIT_EFFECT),
    )(src_thru, land_thru, send_sems, recv_sems, *after)


def _cast_bf16(w, *, name):
    rows, cols = w.shape
    br = 128 if rows % 128 == 0 else rows

    def body(w_ref, o_ref):
        o_ref[...] = w_ref[...].astype(BF16)

    blk = pl.BlockSpec((br, cols), lambda i: (i, 0))
    return _call(body, name=name, out_shape=_sds((rows, cols), BF16), grid=(rows // br,), in_specs=[blk],
                 out_specs=blk, sem=("parallel",))(w)


def _sum_slots(a, *, name):
    _, R, C = a.shape

    def body(a_ref, o_ref):
        s = a_ref[0]
        for d in range(1, NDEV):
            s = s + a_ref[d]
        o_ref[...] = s

    return _call(body, name=name, out_shape=_sds((R, C)))(a)


MODROWS = 16


def _mod_fwd(c9, w, b):
    cols = w.shape[1]

    def body(c_ref, w_ref, b_ref, o_ref):
        o_ref[...] = _nn(_silu(c_ref[...]), w_ref[...]) + b_ref[...]

    return _call(body, name="mod_fwd", out_shape=_sds((MODROWS, cols)))(c9, w, b)


def _mod_bwd(c9, dmy, dall, w):
    cols = w.shape[1]

    def body(c_ref, dmy_ref, dall_ref, w_ref, gw_ref, gb_ref, cp_ref):
        sc = _silu(c_ref[...])
        rows = lax.broadcasted_iota(jnp.int32, (MODROWS, 1), 0)
        d = dmy_ref[...]
        d_ctx = jnp.where(rows == NDEV, d, 0.0)
        sc_ctx = jnp.where(rows == NDEV, sc, 0.0)
        outer = lax.dot_general(sc_ctx, d_ctx, (((0,), (0,)), ((), ())), precision=HI, preferred_element_type=F32)
        gw_ref[...] = _tn(jnp.where(rows < NDEV, sc, 0.0), jnp.where(rows < NDEV, d, 0.0)) + outer
        gb_ref[...] = jnp.sum(dall_ref[...], axis=0, keepdims=True)
        cp_ref[...] = jnp.sum(_nt(d_ctx, w_ref[...]), axis=0, keepdims=True)

    return _call(body, name="mod_bwd", out_shape=(_sds((D, cols)), _sds((1, 6 * D)), _sds((1, D))),
                 vmem=VMEM_BIG)(c9, dmy, dall, w)


def _cctx_finish(parts, c_ctx, after):
    VM = pl.BlockSpec(memory_space=pltpu.VMEM)

    def body(p_ref, c_ref, *rest):
        o_ref = rest[-1]
        s = p_ref[0]
        for d in range(1, NDEV):
            s = s + p_ref[d]
        _, vjp = jax.vjp(_silu, c_ref[...])
        o_ref[...] = vjp(s)[0]

    return _call(body, name="cctx_finish", out_shape=_sds((1, D)),
                 in_specs=[VM, VM] + [pl.BlockSpec(memory_space=pl.ANY)] * len(after))(parts, c_ctx, *after)


def _adamw_recv(w, recv, m, v, *, name, own=None):
    rows, cols = w.shape
    bc = 256
    c1 = 1.0 - B1 ** STEP
    c2 = 1.0 - B2 ** STEP
    has_own = own is not None

    def body(w_ref, r_ref, m_ref, v_ref, *rest):
        g_ref, d_ref, nm_ref, nv_ref = rest[-4:]
        me = _position()[3]

        def slot(d):
            return jnp.where(me == d, rest[0][...], r_ref[d]) if has_own else r_ref[d]

        gv = slot(0).astype(F32)
        for d in range(1, NDEV):
            gv = gv + slot(d).astype(F32)
        nm = B1 * m_ref[...] + (1.0 - B1) * gv
        nv = B2 * v_ref[...] + (1.0 - B2) * (gv * gv)
        g_ref[...] = gv
        d_ref[...] = -LR * ((nm / c1) / (jnp.sqrt(nv / c2) + AEPS) + WD * w_ref[...])
        nm_ref[...] = nm
        nv_ref[...] = nv

    blk = pl.BlockSpec((rows, bc), lambda j: (0, j))
    return _call(body, name=name, out_shape=(_sds((rows, cols)),) * 4, grid=(cols // bc,),
                 in_specs=[blk, pl.BlockSpec((NDEV, rows, bc), lambda j: (0, 0, j)), blk, blk] + [blk] * has_own,
                 out_specs=(blk,) * 4, sem=("parallel",), vmem=VMEM_BIG)(w, recv, m, v, *([own] if has_own else []))


P_LAT, P_CTX, P_FNW, P_FFNB, P_CONV, P_FFNW, P_MISC, P_ROWS = 0, 8, 16, 24, 32, 48, 72, 80


def _rows_of(v, nrows):
    flat = v.reshape(-1)
    return jnp.pad(flat, (0, nrows * D - flat.shape[0])).reshape(nrows, D)


def _by_columns(g):
    n, r, c = g.shape
    return jnp.transpose(g, (1, 0, 2)).reshape(r, n * c)


def kernel(x, c, ctx, c_ctx, w_mod, b_mod, w_in, q_norm_w, k_norm_w, conv_qkv_w, a_log, dt_bias, gdn_norm_w, w_pa, w_pd, w_out, w_up, ffn_conv_w, ffn_conv_b, w_down, final_norm_w, loss_target, m_c_ctx, m_w_mod, m_b_mod, m_w_in, m_q_norm_w, m_k_norm_w, m_conv_qkv_w, m_a_log, m_dt_bias, m_gdn_norm_w, m_w_pa, m_w_pd, m_w_out, m_w_up, m_ffn_conv_w, m_ffn_conv_b, m_w_down, m_final_norm_w, v_c_ctx, v_w_mod, v_b_mod, v_w_in, v_q_norm_w, v_k_norm_w, v_conv_qkv_w, v_a_log, v_dt_bias, v_gdn_norm_w, v_w_pa, v_w_pd, v_w_out, v_w_up, v_ffn_conv_w, v_ffn_conv_b, v_w_down, v_final_norm_w):
    _, _, _, me = _position()
    mcols = w_mod.shape[2]

    transposed = ("w_in", "w_up")
    big = {"w_in": w_in[0].T, "w_pa": w_pa[0], "w_pd": w_pd[0], "w_out": w_out[0], "w_up": w_up[0].T, "w_down": w_down[0]}
    names = list(big)
    shards = {n: _cast_bf16(big[n], name="cast_" + n) for n in names}
    w_in_g = _gather_two_level(shards["w_in"], name="gather_w_in")
    c_all, conv_g, ffnw_g = _exchange([c, conv_qkv_w[0], ffn_conv_w[0]], name="gather_small", scatter=False)
    w_in_full = w_in_g.reshape(W_END, D)
    w_in_pad = _pad_columns(w_in_full)

    c9 = jnp.concatenate([c_all.reshape(NDEV, D), jnp.pad(c_ctx[None], ((0, MODROWS - NDEV - 1), (0, 0)))], axis=0)
    b_loc = lax.dynamic_slice(b_mod, (0, me * mcols), (1, mcols))
    mod_all, = _exchange([_mod_fwd(c9, w_mod[0], b_loc)], name="gather_mod", scatter=False)
    mod_lat = lax.dynamic_index_in_dim(mod_all, me, axis=1, keepdims=False).reshape(6, D)
    mod_ctx = mod_all[:, NDEV, :].reshape(6, D)

    small = {"q_norm_w": q_norm_w, "k_norm_w": k_norm_w, "gdn_norm_w": gdn_norm_w, "a_log": a_log, "dt_bias": dt_bias,
             "conv_qkv_w": _by_columns(conv_g), "ffn_conv_w": _by_columns(ffnw_g), "ffn_conv_b": ffn_conv_b,
             "final_norm_w": final_norm_w[None]}
    loss_me, grad_x, (pending_in, own_in), recv, dmod_lat, dmod_ctx, gs = _local_step(
        x[0], ctx[0], loss_target[0], mod_lat, mod_ctx, w_in_pad, shards, small)

    moments = {"w_in": (m_w_in, v_w_in), "w_pa": (m_w_pa, v_w_pa), "w_pd": (m_w_pd, v_w_pd),
               "w_out": (m_w_out, v_w_out), "w_up": (m_w_up, v_w_up), "w_down": (m_w_down, v_w_down)}
    res = {}
    def finish(n, outs):
        return tuple((t.T if n in transposed else t)[None] for t in outs)

    def moment(t, n):
        return t[0].T if n in transposed else t[0]

    for n in recv:
        res[n] = finish(n, _adamw_recv(big[n], recv[n], moment(moments[n][0], n), moment(moments[n][1], n),
                                       name="adamw_" + n))

    misc = jnp.concatenate([gs["q_norm_w"][0], gs["k_norm_w"][0], gs["gdn_norm_w"][0], gs["a_log"], gs["dt_bias"],
                            loss_me[None]])
    pack = jnp.concatenate([_rows_of(dmod_lat, P_CTX - P_LAT), _rows_of(dmod_ctx, P_FNW - P_CTX),
                            _rows_of(gs["final_norm_w"], P_FFNB - P_FNW), _rows_of(gs["ffn_conv_b"], P_CONV - P_FFNB),
                            _rows_of(gs["conv_qkv_w"], P_FFNW - P_CONV), _rows_of(gs["ffn_conv_w"], P_MISC - P_FFNW),
                            _rows_of(misc, P_ROWS - P_MISC)], axis=0)
    pack_all, = _exchange([pack], name="gather_pack", scatter=False)
    tot = _sum_slots(pack_all, name="sum_pack")
    dall = jnp.concatenate([pack_all[:, P_LAT:P_LAT + 6, :].reshape(NDEV, 6 * D),
                            jnp.pad(tot[P_CTX:P_CTX + 6].reshape(1, 6 * D), ((0, MODROWS - NDEV - 1), (0, 0)))], axis=0)
    dmy = lax.dynamic_slice(dall, (0, me * mcols), (MODROWS, mcols))
    g_w_mod, g_b_mod, cpart = _mod_bwd(c9, dmy, dall, w_mod[0])
    cparts, = _exchange([cpart], name="gather_cctx", scatter=False)
    sems_a, land = pending_in[:2], pending_in[3]
    *sems_b, g_in_thru, land, token_b = _scatter_start(pending_in[2], land, (D // 2, D // 2), (cparts,),
                                                       name="scatter_g_in_b_start")
    g_c_ctx = _cctx_finish(cparts, c_ctx[None], (token_b,))[0]

    nconv, nffn = 3 * GH * HD, 2 * DFF
    conv_tot = tot[P_CONV:P_FFNW].reshape(-1)[:3 * nconv].reshape(3, nconv)
    ffnw_tot = tot[P_FFNW:P_MISC].reshape(-1)[:3 * nffn].reshape(3, nffn)
    mrow = tot[P_MISC]
    grads = {
        "c_ctx": g_c_ctx, "w_mod": g_w_mod[None], "b_mod": g_b_mod,
        "q_norm_w": mrow[None, 0:HD], "k_norm_w": mrow[None, HD:2 * HD], "gdn_norm_w": mrow[None, 2 * HD:3 * HD],
        "conv_qkv_w": lax.dynamic_slice(conv_tot, (0, me * (nconv // NDEV)), (3, nconv // NDEV))[None],
        "a_log": mrow[3 * HD:3 * HD + 2 * GH].reshape(1, 2, GH),
        "dt_bias": mrow[3 * HD + 2 * GH:3 * HD + 4 * GH].reshape(1, 2, GH),
        "ffn_conv_w": lax.dynamic_slice(ffnw_tot, (0, me * (nffn // NDEV)), (3, nffn // NDEV))[None],
        "ffn_conv_b": tot[P_FFNB:P_CONV].reshape(-1)[:nffn][None],
        "final_norm_w": tot[P_FNW],
    }
    loss = mrow[3 * HD + 4 * GH]
    given = {"c_ctx": (c_ctx, m_c_ctx, v_c_ctx), "w_mod": (w_mod, m_w_mod, v_w_mod), "b_mod": (b_mod, m_b_mod, v_b_mod),
             "q_norm_w": (q_norm_w, m_q_norm_w, v_q_norm_w), "k_norm_w": (k_norm_w, m_k_norm_w, v_k_norm_w),
             "conv_qkv_w": (conv_qkv_w, m_conv_qkv_w, v_conv_qkv_w), "a_log": (a_log, m_a_log, v_a_log),
             "dt_bias": (dt_bias, m_dt_bias, v_dt_bias), "gdn_norm_w": (gdn_norm_w, m_gdn_norm_w, v_gdn_norm_w),
             "ffn_conv_w": (ffn_conv_w, m_ffn_conv_w, v_ffn_conv_w), "ffn_conv_b": (ffn_conv_b, m_ffn_conv_b, v_ffn_conv_b),
             "final_norm_w": (final_norm_w, m_final_norm_w, v_final_norm_w)}
    res["w_mod"] = (grads["w_mod"],) + _adamw(w_mod, grads["w_mod"], m_w_mod, v_w_mod, name="adamw_w_mod")
    small_names = [n for n in given if n != "w_mod"]
    updates = _adamw_many([(given[n][0], grads[n], given[n][1], given[n][2]) for n in small_names], name="adamw_small")
    for n, upd in zip(small_names, updates):
        res[n] = (grads[n],) + upd

    g_in_thru, land = _scatter_wait(*sems_a, g_in_thru, land, (0, D // 2), [res[n][1] for n in res],
                                    name="scatter_g_in_a_wait")
    _, land = _scatter_wait(*sems_b, g_in_thru, land, (D // 2, D // 2), (), name="scatter_g_in_b_wait")
    res["w_in"] = finish("w_in", _adamw_recv(big["w_in"], land, moment(m_w_in, "w_in"), moment(v_w_in, "w_in"),
                                             name="adamw_w_in", own=own_in))

    order = ["c_ctx", "w_mod", "b_mod", "w_in", "q_norm_w", "k_norm_w", "conv_qkv_w", "a_log", "dt_bias", "gdn_norm_w",
             "w_pa", "w_pd", "w_out", "w_up", "ffn_conv_w", "ffn_conv_b", "w_down", "final_norm_w"]
    return (loss, grad_x[None], *[res[n][0] for n in order], *[res[n][1] for n in order],
            *[res[n][2] for n in order], *[res[n][3] for n in order])
```

```python
import functools
import math

import jax
import jax.numpy as jnp
from jax import lax
from jax.experimental import pallas as pl
from jax.experimental.pallas import tpu as pltpu

F32 = jnp.float32
BF16 = jnp.bfloat16
HI = lax.Precision.HIGHEST
MESH = pl.DeviceIdType.MESH

NDEV = 8
D = 1024
HD = 128
AH, AKV, GRP = 8, 2, 4
GH = 8
CH = 64
DFF = 2816
GRID_W = 64
EPS = 1e-6
ROPE_THETA = 10000.0
LOG2E = math.log2(math.e)
C_KV, C_AQ, C_QKV, C_BL, C_Z, C_GATE, C_END = 0, 512, 1536, 4608, 5120, 6144, 8192
W_QKV, W_AQ, W_Z, W_END = 512, 3616, 4640, 7712


def _pad_columns(w):
    zeros = jnp.zeros((C_Z - C_QKV - (W_AQ - W_QKV), D), w.dtype)
    return jnp.concatenate([w[:W_QKV], w[W_AQ:W_Z], w[W_QKV:W_AQ], zeros, w[W_Z:]], axis=0)


def _unpad_columns(g):
    return jnp.concatenate([g[:C_AQ], g[C_QKV:C_QKV + W_AQ - W_QKV], g[C_AQ:C_QKV], g[C_Z:]], axis=0)
LR, B1, B2, AEPS, WD, STEP = 0.001, 0.9, 0.999, 1e-08, 0.01, 10
VMEM_BIG = 56 * 1024 * 1024
INTRA_FWD_CHUNKS = 36
INTRA_BWD_CHUNKS = 36


def _call(body, *, name, out_shape, grid=None, in_specs=None, out_specs=None, scratch=(), sem=None,
          vmem=None, aliases=None):
    params = {}
    if sem is not None:
        params["dimension_semantics"] = sem
    if vmem is not None:
        params["vmem_limit_bytes"] = vmem
    kw = {}
    if grid is not None:
        kw["grid"] = grid
    if in_specs is not None:
        kw["in_specs"] = in_specs
    if out_specs is not None:
        kw["out_specs"] = out_specs
    if aliases:
        kw["input_output_aliases"] = aliases
    return pl.pallas_call(body, name=name, out_shape=out_shape, scratch_shapes=list(scratch),
                          compiler_params=pltpu.CompilerParams(**params), **kw)


def _call_carrying(body, exch, *, name, out_shape, grid, in_specs, out_specs, scratch=(), vmem=None):
    n, nin, nout, nscr = exch.n, len(in_specs), len(out_shape), len(scratch)
    steps = math.prod(grid)
    mid = (2 * steps) // 3

    def wrapped(*refs):
        ins, cins = refs[:nin], refs[nin:nin + n]
        outs, couts = refs[nin + n:nin + n + nout], refs[nin + n + nout:nin + 2 * n + nout]
        scr, sems = refs[nin + 2 * n + nout:nin + 2 * n + nout + nscr], refs[nin + 2 * n + nout + nscr:]
        ids = [pl.program_id(i) for i in range(len(grid))]
        first = functools.reduce(jnp.logical_and, [i == 0 for i in ids])
        last = functools.reduce(jnp.logical_and, [i == g - 1 for i, g in zip(ids, grid)])

        @pl.when(first)
        def _():
            exch.start(cins, couts, sems)

        if hasattr(exch, "middle"):
            linear = functools.reduce(lambda acc, ig: acc * ig[1] + ig[0], zip(ids, grid), 0)

            @pl.when(linear == mid)
            def _():
                exch.middle(cins, couts, sems)

        body(*ins, *outs, *scr)

        @pl.when(last)
        def _():
            exch.finish(cins, couts, sems)

    params = {"dimension_semantics": ("arbitrary",) * len(grid)}
    if vmem is not None:
        params["vmem_limit_bytes"] = vmem
    fn = pl.pallas_call(wrapped, name=name, out_shape=tuple(out_shape) + exch.out_shape, grid=grid,
                        in_specs=list(in_specs) + [HBM] * n, out_specs=tuple(out_specs) + (HBM,) * n,
                        scratch_shapes=list(scratch) + exch.scratch, compiler_params=pltpu.CompilerParams(**params))

    def run(*args):
        res = fn(*args, *exch.arrs)
        return res[:nout], list(res[nout:])

    return run


def _sds(shape, dtype=F32):
    return jax.ShapeDtypeStruct(tuple(shape), dtype)


def _dot(a, b, ca, cb):
    return lax.dot_general(a.astype(BF16), b.astype(BF16), (((ca,), (cb,)), ((), ())),
                           preferred_element_type=F32)


@jax.custom_vjp
def _nn(a, b):
    return _dot(a, b, 1, 0)


@jax.custom_vjp
def _nt(a, b):
    return _dot(a, b, 1, 1)


@jax.custom_vjp
def _tn(a, b):
    return _dot(a, b, 0, 0)


_nn.defvjp(lambda a, b: (_nn(a, b), (a, b)), lambda r, g: (_nt(g, r[1]), _tn(r[0], g)))
_nt.defvjp(lambda a, b: (_nt(a, b), (a, b)), lambda r, g: (_nn(g, r[1]), _tn(g, r[0])))
_tn.defvjp(lambda a, b: (_tn(a, b), (a, b)), lambda r, g: (_nt(r[1], g), _nn(r[0], g)))


def _mdot(a, b):
    return jnp.dot(a, b, precision=lax.Precision.HIGH, preferred_element_type=F32)


def _maskdot(mask, a, cm):
    hi = a.astype(BF16)
    r = a - hi.astype(F32)
    mid = r.astype(BF16)
    lo = (r - mid.astype(F32)).astype(BF16)
    mb = mask.astype(BF16)
    dims = (((cm,), (0,)), ((), ()))
    return (lax.dot_general(mb, hi, dims, preferred_element_type=F32)
            + lax.dot_general(mb, mid, dims, preferred_element_type=F32)
            + lax.dot_general(mb, lo, dims, preferred_element_type=F32))


@jax.custom_vjp
def _mask_nn(mask, a):
    return _maskdot(mask, a, 1)


_mask_nn.defvjp(lambda mask, a: (_maskdot(mask, a, 1), mask),
                lambda mask, g: (jnp.zeros_like(mask), _maskdot(mask, g, 0)))


@jax.custom_vjp
def _saved_inverse(lmat, x):
    return x


def _saved_inverse_bwd(x, g):
    t = lax.dot_general(x, g, (((0,), (0,)), ((), ())), precision=lax.Precision.HIGH, preferred_element_type=F32)
    dl = lax.dot_general(t, x, (((1,), (1,)), ((), ())), precision=lax.Precision.HIGH, preferred_element_type=F32)
    return -dl, jnp.zeros_like(x)


_saved_inverse.defvjp(lambda lmat, x: (x, x), _saved_inverse_bwd)


def _row_ids(shape):
    return lax.broadcasted_iota(jnp.int32, shape, 0)


def _shift_rows(x, down, bounds):
    n = x.shape[0]
    rows = _row_ids(x.shape)
    y = pltpu.roll(x, 1 if down else n - 1, 0)
    edge = functools.reduce(jnp.logical_or, [rows == (s if down else e - 1) for s, e in bounds])
    return jnp.where(edge, 0.0, y)


def _make_shift(bounds):
    @jax.custom_vjp
    def down(x):
        return _shift_rows(x, True, bounds)

    @jax.custom_vjp
    def up(x):
        return _shift_rows(x, False, bounds)

    down.defvjp(lambda x: (down(x), None), lambda _, g: (up(g),))
    up.defvjp(lambda x: (up(x), None), lambda _, g: (down(g),))
    return down, up


@jax.custom_vjp
def _swap32(x):
    lane = lax.broadcasted_iota(jnp.int32, x.shape, x.ndim - 1)
    return jnp.where((lane % 64) < 32, pltpu.roll(x, HD - 32, x.ndim - 1), pltpu.roll(x, 32, x.ndim - 1))


_swap32.defvjp(lambda x: (_swap32(x), None), lambda _, g: (_swap32(g),))


def _rms(x):
    return x * lax.rsqrt(jnp.mean(x * x, axis=-1, keepdims=True) + EPS)


def _silu(x):
    return x * jax.nn.sigmoid(x)


def _mm(a, b, *, name, M, N, K, ta=False, tb=False, out_dtype=F32, bm=None, bn=None, bk=None, after=()):
    bm, bn, bk = bm or M, bn or N, bk or K
    assert M % bm == 0 and N % bn == 0 and K % bk == 0, (name, M, N, K, bm, bn, bk)
    nk = K // bk
    ca, cb = (0 if ta else 1), (1 if tb else 0)
    na = len(after)

    def body(a_ref, b_ref, *rest):
        o_ref, acc = rest[na], rest[na + 1:]
        r = _dot(a_ref[...], b_ref[...], ca, cb)
        if nk == 1:
            o_ref[...] = r.astype(out_dtype)
        else:
            acc_ref, = acc
            k = pl.program_id(2)

            @pl.when(k == 0)
            def _():
                acc_ref[...] = r

            @pl.when(k > 0)
            def _():
                acc_ref[...] += r

            @pl.when(k == nk - 1)
            def _():
                o_ref[...] = acc_ref[...].astype(out_dtype)

    a_spec = pl.BlockSpec((bk, bm), lambda i, j, k: (k, i)) if ta else pl.BlockSpec((bm, bk), lambda i, j, k: (i, k))
    b_spec = pl.BlockSpec((bn, bk), lambda i, j, k: (j, k)) if tb else pl.BlockSpec((bk, bn), lambda i, j, k: (k, j))
    return _call(body, name=name, out_shape=_sds((M, N), out_dtype), grid=(M // bm, N // bn, nk),
                 in_specs=[a_spec, b_spec] + [pl.BlockSpec(memory_space=pl.ANY)] * na,
                 out_specs=pl.BlockSpec((bm, bn), lambda i, j, k: (i, j)),
                 scratch=[pltpu.VMEM((bm, bn), F32)] if nk > 1 else [],
                 sem=("parallel", "parallel", "arbitrary"), vmem=VMEM_BIG)(a, b, *after)


def _normmod_fn(x, sh, sc):
    return _rms(x) * (1.0 + sc) + sh


def _normmod_fwd(x, mod, i_sh, i_sc, *, name, br=256):
    R = x.shape[0]

    def body(x_ref, mod_ref, o_ref):
        o_ref[...] = _normmod_fn(x_ref[...], mod_ref[i_sh:i_sh + 1, :], mod_ref[i_sc:i_sc + 1, :]).astype(BF16)

    return _call(body, name=name, out_shape=_sds((R, D), BF16), grid=(R // br,),
                 in_specs=[pl.BlockSpec((br, D), lambda i: (i, 0)), pl.BlockSpec((6, D), lambda i: (0, 0))],
                 out_specs=pl.BlockSpec((br, D), lambda i: (i, 0)), sem=("parallel",))(x, mod)


def _normmod_bwd(x, mod, i_sh, i_sc, dh, dh_off, res, *, name, br=256):
    R = x.shape[0]
    ob = dh_off // br
    has_res = res is not None

    def body(x_ref, mod_ref, dh_ref, *rest):
        if has_res:
            res_ref, dx_ref, dsh_ref, dsc_ref = rest
        else:
            dx_ref, dsh_ref, dsc_ref = rest
        sh, sc = mod_ref[i_sh:i_sh + 1, :], mod_ref[i_sc:i_sc + 1, :]
        _, vjp = jax.vjp(_normmod_fn, x_ref[...], sh, sc)
        dx, dsh, dsc = vjp(dh_ref[...])
        dx_ref[...] = dx + res_ref[...] if has_res else dx

        @pl.when(pl.program_id(0) == 0)
        def _():
            dsh_ref[...] = jnp.zeros_like(dsh_ref)
            dsc_ref[...] = jnp.zeros_like(dsc_ref)

        dsh_ref[...] += dsh
        dsc_ref[...] += dsc

    row = pl.BlockSpec((br, D), lambda i: (i, 0))
    vec = pl.BlockSpec((1, D), lambda i: (0, 0))
    ins = [row, pl.BlockSpec((6, D), lambda i: (0, 0)), pl.BlockSpec((br, D), lambda i: (i + ob, 0))]
    args = [x, mod, dh]
    if has_res:
        ins.append(row)
        args.append(res)
    return _call(body, name=name, out_shape=(_sds((R, D)), _sds((1, D)), _sds((1, D))), grid=(R // br,),
                 in_specs=ins, out_specs=(row, vec, vec), sem=("arbitrary",))(*args)


def _rope(x, cos, sin):
    return x * cos + _swap32(x) * sin


def _aprep_fn(qs, ks, cos, sin, qw, kw):
    return ([_rope(_rms(q) * qw, cos, sin) for q in qs], [_rope(_rms(k) * kw, cos, sin) for k in ks])


def _aprep_fwd(proj, cos, sin, qw, kw, *, br=256):
    T = proj.shape[0]

    def body(x_ref, cos_ref, sin_ref, qw_ref, kw_ref, q_ref, k_ref, v_ref):
        qs = [x_ref[:, C_AQ + h * HD:C_AQ + (h + 1) * HD] for h in range(AH)]
        ks = [x_ref[:, h * HD:(h + 1) * HD] for h in range(AKV)]
        qo, ko = _aprep_fn(qs, ks, cos_ref[...], sin_ref[...], qw_ref[...], kw_ref[...])
        for h in range(AH):
            q_ref[h] = qo[h].astype(BF16)
        for h in range(AKV):
            k_ref[h] = ko[h].astype(BF16)
            v_ref[h] = x_ref[:, (AKV + h) * HD:(AKV + h + 1) * HD].astype(BF16)

    tab = pl.BlockSpec((br, HD), lambda i: (i, 0))
    vec = pl.BlockSpec((1, HD), lambda i: (0, 0))
    return _call(body, name="aprep_fwd",
                 out_shape=(_sds((AH, T, HD), BF16), _sds((AKV, T, HD), BF16), _sds((AKV, T, HD), BF16)),
                 grid=(T // br,),
                 in_specs=[pl.BlockSpec((br, C_QKV), lambda i: (i, 0)), tab, tab, vec, vec],
                 out_specs=(pl.BlockSpec((AH, br, HD), lambda i: (0, i, 0)),
                            pl.BlockSpec((AKV, br, HD), lambda i: (0, i, 0)),
                            pl.BlockSpec((AKV, br, HD), lambda i: (0, i, 0))),
                 sem=("parallel",))(proj, cos, sin, qw, kw)


def _aprep_bwd(proj, cos, sin, qw, kw, dq, dk, dv, dproj, L, *, br=256):
    T = proj.shape[0]
    lb = L // br

    def body(x_ref, cos_ref, sin_ref, qw_ref, kw_ref, dq_ref, dk_ref, dv_ref, _, dx_ref, dqw_ref, dkw_ref):
        i = pl.program_id(0)
        qs = [x_ref[:, C_AQ + h * HD:C_AQ + (h + 1) * HD] for h in range(AH)]
        ks = [x_ref[:, h * HD:(h + 1) * HD] for h in range(AKV)]
        _, vjp = jax.vjp(_aprep_fn, qs, ks, cos_ref[...], sin_ref[...], qw_ref[...], kw_ref[...])
        is_lat = i >= lb
        dqs = [jnp.where(is_lat, dq_ref[h], 0.0) for h in range(AH)]
        dks = [dk_ref[h] for h in range(AKV)]
        gq, gk, _, _, gqw, gkw = vjp((dqs, dks))
        for h in range(AH):
            dx_ref[:, C_AQ + h * HD:C_AQ + (h + 1) * HD] = gq[h].astype(BF16)
        for h in range(AKV):
            dx_ref[:, h * HD:(h + 1) * HD] = gk[h].astype(BF16)
            dx_ref[:, (AKV + h) * HD:(AKV + h + 1) * HD] = dv_ref[h].astype(BF16)

        @pl.when(i == 0)
        def _():
            dqw_ref[...] = jnp.zeros_like(dqw_ref)
            dkw_ref[...] = jnp.zeros_like(dkw_ref)

        dqw_ref[...] += gqw
        dkw_ref[...] += gkw

    tab = pl.BlockSpec((br, HD), lambda i: (i, 0))
    vec = pl.BlockSpec((1, HD), lambda i: (0, 0))
    kvb = pl.BlockSpec((AKV, br, HD), lambda i: (0, i, 0))
    blk = pl.BlockSpec((br, C_QKV), lambda i: (i, 0))
    return _call(body, name="aprep_bwd", out_shape=(_sds(dproj.shape, BF16), _sds((1, HD)), _sds((1, HD))),
                 grid=(T // br,),
                 in_specs=[blk, tab, tab, vec, vec,
                           pl.BlockSpec((AH, br, HD), lambda i: (0, jnp.maximum(i - lb, 0), 0)), kvb, kvb, ANYSPEC],
                 out_specs=(blk, vec, vec), aliases={8: 0},
                 sem=("arbitrary",))(proj, cos, sin, qw, kw, dq, dk, dv, dproj)


def _attn_grad(q, k, v, o, lse2, do):
    scale = HD ** -0.5
    p = jnp.exp2(_dot(q, k, 1, 1) * (scale * LOG2E) - lse2)
    dp = _dot(do, v, 1, 1)
    ds = p * (dp - jnp.sum(do * o, axis=-1, keepdims=True)) * scale
    return _dot(ds, k, 1, 0), _dot(ds, q, 0, 0), _dot(p, do, 0, 0)


ATTN_KEYS = 256


def _attn_fwd(q, k, v, L, exch, *, bq=128):
    T = q.shape[1]
    N = T - L
    lb = L // bq
    assert T % ATTN_KEYS == 0
    scale = HD ** -0.5
    heads = range(GRP)

    def body(q_ref, k_ref, v_ref, o_ref, o32_ref, lse_ref):
        qs = [q_ref[g] for g in heads]
        m = [jnp.full((bq, 1), -jnp.inf, F32) for _ in heads]
        l = [jnp.zeros((bq, 1), F32) for _ in heads]
        acc = [jnp.zeros((bq, HD), F32) for _ in heads]
        for c in range(T // ATTN_KEYS):
            kc, vc = k_ref[c * ATTN_KEYS:(c + 1) * ATTN_KEYS, :], v_ref[c * ATTN_KEYS:(c + 1) * ATTN_KEYS, :]
            s = [_dot(qs[g], kc, 1, 1) * (scale * LOG2E) for g in heads]
            m_new = [jnp.maximum(m[g], jnp.max(s[g], axis=-1, keepdims=True)) for g in heads]
            alpha = [jnp.exp2(m[g] - m_new[g]) for g in heads]
            p = [jnp.exp2(s[g] - m_new[g]) for g in heads]
            l = [l[g] * alpha[g] + jnp.sum(p[g], axis=-1, keepdims=True) for g in heads]
            acc = [acc[g] * alpha[g] + _dot(p[g], vc, 1, 0) for g in heads]
            m = m_new
        for g in heads:
            o = acc[g] / l[g]
            o_ref[:, g * HD:(g + 1) * HD] = o.astype(BF16)
            o32_ref[:, g * HD:(g + 1) * HD] = o
            lse_ref[g] = jnp.broadcast_to(m[g] + jnp.log2(l[g]), (bq, HD))

    kvb = pl.BlockSpec((None, T, HD), lambda g, i: (g, 0, 0))
    ob = pl.BlockSpec((bq, GRP * HD), lambda g, i: (i, g))
    return _call_carrying(
        body, exch, name="attn_fwd",
        out_shape=(_sds((N, AH * HD), BF16), _sds((N, AH * HD)), _sds((AH, N, HD))), grid=(AKV, N // bq),
        in_specs=[pl.BlockSpec((GRP, bq, HD), lambda g, i: (g, i + lb, 0)), kvb, kvb],
        out_specs=(ob, ob, pl.BlockSpec((GRP, bq, HD), lambda g, i: (g, i, 0))), vmem=VMEM_BIG)(q, k, v)


def _attn_bwd(q, k, v, o32, lse, do, L, exch, *, bq=128):
    T = q.shape[1]
    N = T - L
    lb = L // bq

    def body(q_ref, k_ref, v_ref, o_ref, lse_ref, do_ref, dq_ref, dk_ref, dv_ref):
        rows = lambda r: jnp.concatenate([r[:, g * HD:(g + 1) * HD] for g in range(GRP)], axis=0)
        lse = jnp.max(lse_ref[...].reshape(GRP * bq, HD), axis=-1, keepdims=True)
        dq, dk, dv = _attn_grad(q_ref[...].reshape(GRP * bq, HD), k_ref[...], v_ref[...], rows(o_ref), lse, rows(do_ref))
        dq_ref[...] = dq.reshape(GRP, bq, HD)

        @pl.when(pl.program_id(1) == 0)
        def _():
            dk_ref[...] = jnp.zeros_like(dk_ref)
            dv_ref[...] = jnp.zeros_like(dv_ref)

        dk_ref[...] += dk
        dv_ref[...] += dv

    kvb = pl.BlockSpec((None, T, HD), lambda g, i: (g, 0, 0))
    qb = pl.BlockSpec((GRP, bq, HD), lambda g, i: (g, i + lb, 0))
    hb = pl.BlockSpec((GRP, bq, HD), lambda g, i: (g, i, 0))
    ob = pl.BlockSpec((bq, GRP * HD), lambda g, i: (i, g))
    return _call_carrying(body, exch, name="attn_bwd",
                          out_shape=(_sds((AH, N, HD)), _sds((AKV, T, HD)), _sds((AKV, T, HD))), grid=(AKV, N // bq),
                          in_specs=[qb, kvb, kvb, ob, hb, ob], out_specs=(hb, kvb, kvb),
                          vmem=VMEM_BIG)(q, k, v, o32, lse, do)


def _gprep_fn(kind, shifts, x, w):
    down, up = shifts
    y = down(x) * w[0:1, :] + x * w[1:2, :] + up(x) * w[2:3, :]
    a = _silu(y)
    if kind == 2:
        return a
    a = a * lax.rsqrt(jnp.sum(a * a, axis=-1, keepdims=True) + EPS)
    return a * (HD ** -0.5) if kind == 0 else a


def _gprep_fwd(proj, conv_w, kind, bounds):
    T = proj.shape[0]
    shifts = _make_shift(bounds)
    cb = C_QKV // HD + kind * GH

    def body(x_ref, w_ref, o_ref):
        o_ref[...] = _gprep_fn(kind, shifts, x_ref[...], w_ref[...])

    return _call(body, name=f"gprep_fwd{kind}", out_shape=_sds((GH, T, HD)), grid=(GH,),
                 in_specs=[pl.BlockSpec((T, HD), lambda h: (0, cb + h)),
                           pl.BlockSpec((3, HD), lambda h: (0, kind * GH + h))],
                 out_specs=pl.BlockSpec((None, T, HD), lambda h: (h, 0, 0)), sem=("parallel",))(proj, conv_w)


def _gprep_bwd(proj, conv_w, kind, bounds, dy, dproj):
    T = proj.shape[0]
    shifts = _make_shift(bounds)
    cb = C_QKV // HD + kind * GH

    def body(x_ref, w_ref, dy_ref, _, dx_ref, dw_ref):
        _, vjp = jax.vjp(functools.partial(_gprep_fn, kind, shifts), x_ref[...], w_ref[...])
        dx, dw = vjp(dy_ref[0] + dy_ref[1])
        dx_ref[...] = dx.astype(BF16)
        dw_ref[...] = dw

    return _call(body, name=f"gprep_bwd{kind}", out_shape=(_sds(dproj.shape, BF16), _sds((3, GH * HD))), grid=(GH,),
                 in_specs=[pl.BlockSpec((T, HD), lambda h: (0, cb + h)),
                           pl.BlockSpec((3, HD), lambda h: (0, kind * GH + h)),
                           pl.BlockSpec((2, None, T, HD), lambda h: (0, h, 0, 0)), ANYSPEC],
                 out_specs=(pl.BlockSpec((T, HD), lambda h: (0, cb + h)), pl.BlockSpec((3, HD), lambda h: (0, h))),
                 aliases={3: 0}, sem=("parallel",))(proj, conv_w, dy, dproj)


def _bl_fn(x, alog, dtb):
    lane = lax.broadcasted_iota(jnp.int32, x.shape, 1)
    beta = jax.nn.sigmoid(x)
    z = x + dtb
    sp = jnp.maximum(z, 0.0) + jnp.log1p(jnp.exp(-jnp.abs(z)))
    la = -jnp.exp(alog) * sp
    return jnp.where(lane < 2 * GH, beta, jnp.where(lane < 4 * GH, la, 0.0))


def _bl_fwd(proj, alog, dtb, *, br=256):
    T = proj.shape[0]

    def body(x_ref, a_ref, d_ref, o_ref):
        o_ref[...] = _bl_fn(x_ref[...], a_ref[...], d_ref[...])

    vec = pl.BlockSpec((1, HD), lambda i: (0, 0))
    return _call(body, name="bl_fwd", out_shape=_sds((T, HD)), grid=(T // br,),
                 in_specs=[pl.BlockSpec((br, HD), lambda i: (i, C_BL // HD)), vec, vec],
                 out_specs=pl.BlockSpec((br, HD), lambda i: (i, 0)), sem=("parallel",))(proj, alog, dtb)


def _bl_bwd(proj, alog, dtb, dbl, dproj, *, br=256):
    T = proj.shape[0]
    wide = C_Z - C_BL

    def body(x_ref, a_ref, d_ref, g_ref, _, dx_ref, da_ref, dd_ref):
        g = g_ref[0, 0]
        for d in range(2):
            for h in range(GH):
                if d or h:
                    g = g + g_ref[d, h]
        _, vjp = jax.vjp(_bl_fn, x_ref[...], a_ref[...], d_ref[...])
        dx, da, dd = vjp(g)
        dx_ref[:, :HD] = dx.astype(BF16)
        dx_ref[:, HD:] = jnp.zeros((br, wide - HD), BF16)

        @pl.when(pl.program_id(0) == 0)
        def _():
            da_ref[...] = jnp.zeros_like(da_ref)
            dd_ref[...] = jnp.zeros_like(dd_ref)

        da_ref[...] += da
        dd_ref[...] += dd

    vec = pl.BlockSpec((1, HD), lambda i: (0, 0))
    return _call(body, name="bl_bwd", out_shape=(_sds(dproj.shape, BF16), _sds((1, HD)), _sds((1, HD))), grid=(T // br,),
                 in_specs=[pl.BlockSpec((br, HD), lambda i: (i, C_BL // HD)), vec, vec,
                           pl.BlockSpec((2, GH, br, HD), lambda i: (0, 0, i, 0)), ANYSPEC],
                 out_specs=(pl.BlockSpec((br, wide), lambda i: (i, C_BL // wide)), vec, vec), aliases={4: 0},
                 sem=("arbitrary",))(proj, alog, dtb, dbl, dproj)


def _chunk_masks(d):
    ii = lax.broadcasted_iota(jnp.int32, (CH, CH), 0)
    jj = lax.broadcasted_iota(jnp.int32, (CH, CH), 1)
    eye = (ii == jj).astype(F32)
    before = jnp.where(d == 0, (jj < ii).astype(F32), (jj > ii).astype(F32))
    return before, before + eye, eye


def _same_block(b):
    ii = lax.broadcasted_iota(jnp.int32, (CH, CH), 0)
    jj = lax.broadcasted_iota(jnp.int32, (CH, CH), 1)
    shift = b.bit_length() - 1
    return (jnp.right_shift(ii, shift) == jnp.right_shift(jj, shift)).astype(F32)


def _intra_fn(masks, sel_b, sel_l, qs, ks, vs, bls, xs=None):
    before, ateq, eye = masks
    ones = jnp.ones((CH, CH), F32)
    inc = ateq > 0.0
    each = lambda f, *ls: [f(*t) for t in zip(*ls)]
    beta = each(lambda bl: jnp.sum(bl * sel_b, axis=-1, keepdims=True), bls)
    la = each(lambda bl: jnp.sum(bl * sel_l, axis=-1, keepdims=True), bls)
    gam = each(lambda a: _mask_nn(ateq, jnp.broadcast_to(a, (CH, HD))), la)
    gi = each(lambda a: _mask_nn(ateq, jnp.broadcast_to(a, (CH, CH))), la)
    gj = each(lambda g: _mask_nn(ones, eye * g), gi)
    kk = each(lambda k: _nt(k, k), ks)
    qk = each(_nt, qs, ks)
    dec = each(lambda a, b: jnp.where(inc, jnp.exp(jnp.where(inc, a - b, 0.0)), 0.0), gi, gj)
    lmat = each(lambda b, d, m: before * (b * d * m), beta, dec, kk)
    if xs is None:
        same = lambda b: _same_block(b)
        l8 = each(lambda m: m * same(8), lmat)
        x = each(lambda m: eye - m, l8)
        p2 = each(lambda m: _mdot(m, m), l8)
        y = each(lambda a, b: _mdot(jnp.concatenate([a, b], axis=0), b), x, p2)
        x = each(lambda a, t: a + t[:CH], x, y)
        x = each(lambda a, t: a + _mdot(a, t[CH:]), x, y)
        for b in (8, 16, 32):
            below = same(2 * b) - same(b)
            x = each(lambda a, m: a - _mdot(a, _mdot(m * below, a)), x, lmat)
    else:
        x = each(_saved_inverse, lmat, xs)
    eg = each(jnp.exp, gam)
    u = each(lambda a, b, v: _mdot(a, b * v), x, beta, vs)
    w = each(lambda a, b, e, k: _mdot(a, (b * e) * k), x, beta, eg, ks)
    tot = each(lambda a: jnp.sum(a, axis=0, keepdims=True), la)
    kd = each(lambda k, t, g: k * jnp.exp(t - g), ks, tot, gam)
    gl = each(lambda t: jnp.broadcast_to(jnp.exp(t), (1, HD)), tot)
    qd = each(lambda q, e: q * e, qs, eg)
    p = each(lambda d, m: d * m, dec, qk)
    return (u, w, kd, qd, p, gl, x) if xs is None else (u, w, kd, qd, p, gl)


def _dir_head_sel(d, h):
    lane = lax.broadcasted_iota(jnp.int32, (1, HD), 1)
    return (lane == d * GH + h).astype(F32), (lane == 2 * GH + d * GH + h).astype(F32)


def _intra_specs(T, G):
    nc = T // CH
    assert nc % G == 0
    qkv = pl.BlockSpec((None, G * CH, HD), lambda d, h, c: (h, c, 0))
    bl = pl.BlockSpec((G * CH, HD), lambda d, h, c: (c, 0))
    big = pl.BlockSpec((None, None, G * CH, HD), lambda d, h, c: (d, h, c, 0))
    pm = pl.BlockSpec((None, None, G * CH, CH), lambda d, h, c: (d, h, c, 0))
    gl = pl.BlockSpec((None, None, G, 1, HD), lambda d, h, c: (d, h, c, 0, 0))
    shapes = (_sds((2, GH, T, HD)),) + (_sds((2, GH, T, HD), BF16),) * 3 + (
        _sds((2, GH, T, CH), BF16), _sds((2, GH, nc, 1, HD)), _sds((2, GH, T, CH)))
    return nc, qkv, bl, big, pm, gl, shapes


def _chunks_per_step(T, most):
    nc = T // CH
    return max(g for g in range(1, most + 1) if nc % g == 0)


def _chunk_at(g, d, nc, ncc):
    pos = _visit_pos(g, d, nc, ncc)
    return pos, pl.ds(pl.multiple_of(pos * CH, CH), CH)


def _intra_fwd(q, k, v, bl, L, exch):
    T = q.shape[1]
    G = _chunks_per_step(T, INTRA_FWD_CHUNKS)
    nc, qkv_s, bl_s, big, pm, gl_s, shapes = _intra_specs(T, G)
    assert G == nc
    ncc = L // CH

    def body(q_ref, k_ref, v_ref, bl_ref, u_ref, w_ref, kd_ref, qd_ref, p_ref, gl_ref, x_ref):
        d, h = pl.program_id(0), pl.program_id(1)
        sb, sl = _dir_head_sel(d, h)
        rows = [slice(g * CH, (g + 1) * CH) for g in range(G)]
        outs = _intra_fn(_chunk_masks(d), sb, sl, *[[r[s, :] for s in rows] for r in (q_ref, k_ref, v_ref, bl_ref)])
        for g in range(G):
            pos, at = _chunk_at(g, d, nc, ncc)
            for r, o in zip((u_ref, w_ref, kd_ref, qd_ref, p_ref, x_ref), outs[:5] + outs[6:]):
                r[at, :] = o[g].astype(r.dtype)
            gl_ref[pos] = outs[5][g]

    return _call_carrying(body, exch, name="gdn_intra_fwd", out_shape=shapes, grid=(2, GH, nc // G),
                          in_specs=[qkv_s, qkv_s, qkv_s, bl_s], out_specs=(big, big, big, big, pm, gl_s, pm))(q, k, v, bl)


def _intra_bwd(q, k, v, bl, xinv, cts, L, exch):
    T = q.shape[1]
    G = _chunks_per_step(T, INTRA_BWD_CHUNKS)
    nc, qkv_s, bl_s, big, pm, gl_s, _ = _intra_specs(T, G)
    assert G == nc
    ncc = L // CH

    def body(q_ref, k_ref, v_ref, bl_ref, x_ref, du, dw, dkd, dqd, dp, dgl, dq_ref, dk_ref, dv_ref, dbl_ref):
        d, h = pl.program_id(0), pl.program_id(1)
        sb, sl = _dir_head_sel(d, h)
        rows = [slice(g * CH, (g + 1) * CH) for g in range(G)]
        places = [_chunk_at(g, d, nc, ncc) for g in range(G)]
        fn = functools.partial(_intra_fn, _chunk_masks(d), sb, sl, xs=[x_ref[at, :] for _, at in places])
        _, vjp = jax.vjp(fn, *[[r[s, :] for s in rows] for r in (q_ref, k_ref, v_ref, bl_ref)])
        cts = tuple([r[at, :] for _, at in places] for r in (du, dw, dkd, dqd, dp)) + ([dgl[pos] for pos, _ in places],)
        grads = vjp(cts)
        for g in range(G):
            for r, o in zip((dq_ref, dk_ref, dv_ref, dbl_ref), grads):
                r[rows[g], :] = o[g]

    return _call_carrying(body, exch, name="gdn_intra_bwd", out_shape=(_sds((2, GH, T, HD)),) * 4,
                          grid=(2, GH, nc // G), in_specs=[qkv_s, qkv_s, qkv_s, bl_s, pm, big, big, big, big, pm, gl_s],
                          out_specs=(big,) * 4)(q, k, v, bl, xinv, *cts)


def _scan_fn(s, u, w, kd, qd, p, gl):
    each = lambda f, *ls: [f(*t) for t in zip(*ls)]
    ws = each(_nn, w, s)
    delta = each(lambda a, b: a - b, u, ws)
    kdd = each(_tn, kd, delta)
    s_new = each(lambda g, a, b: g * a + b, gl, s, kdd)
    qs = each(_nn, qd, s)
    pd = each(_nn, p, delta)
    return each(lambda a, b: a + b, qs, pd), s_new


SCAN_BLOCK = 4


def _visit_pos(c, d, nc, ncc):
    back = ncc - 1 - c if c < ncc else ncc + (nc - 1 - c)
    return jnp.where(d == 0, c, back)


def _scan_specs(T, L, back):
    tb = SCAN_BLOCK * CH
    assert T % tb == 0 and L % tb == 0
    nb, ncb = T // tb, L // tb
    at = (lambda t: nb - 1 - t) if back else (lambda t: t)
    big = pl.BlockSpec((2, GH, tb, HD), lambda t: (0, 0, at(t), 0))
    pm = pl.BlockSpec((2, GH, tb, CH), lambda t: (0, 0, at(t), 0))
    gl = pl.BlockSpec((2, GH, SCAN_BLOCK, 1, HD), lambda t: (0, 0, at(t), 0, 0))
    st = pl.BlockSpec((2, GH, SCAN_BLOCK, HD, HD), lambda t: (0, 0, at(t), 0, 0))

    def natural(b):
        return jnp.where(b < ncb, ncb - 1 - b, nb - 1 - (b - ncb))

    do_specs = (pl.BlockSpec((GH, tb, HD), lambda t: (0, at(t), 0)),
                pl.BlockSpec((GH, tb, HD), lambda t: (0, natural(at(t)), 0)))
    return nb, big, pm, gl, st, do_specs


SCAN_STREAMS = [(d, h) for d in (0, 1) for h in range(GH)]


def _scan_fwd(u, w, kd, qd, p, gl, L):
    T = u.shape[2]
    nb, big, pm, gl_s, st, _ = _scan_specs(T, L, False)

    def body(u_ref, w_ref, kd_ref, qd_ref, p_ref, gl_ref, o_ref, st_ref, s_scr):
        @pl.when(pl.program_id(0) == 0)
        def _():
            s_scr[...] = jnp.zeros_like(s_scr)

        s = [s_scr[d, h] for d, h in SCAN_STREAMS]
        for i in range(SCAN_BLOCK):
            rows = slice(i * CH, (i + 1) * CH)
            for (d, h), sv in zip(SCAN_STREAMS, s):
                st_ref[d, h, i] = sv
            o, s = _scan_fn(s, *[[r[d, h, rows, :].astype(F32) for d, h in SCAN_STREAMS]
                                 for r in (u_ref, w_ref, kd_ref, qd_ref, p_ref)],
                            [gl_ref[d, h, i] for d, h in SCAN_STREAMS])
            for (d, h), ov in zip(SCAN_STREAMS, o):
                o_ref[d, h, rows, :] = ov
        for (d, h), sv in zip(SCAN_STREAMS, s):
            s_scr[d, h] = sv

    return _call(body, name="gdn_scan_fwd", out_shape=(_sds((2, GH, T, HD)), _sds((2, GH, T // CH, HD, HD))),
                 grid=(nb,), in_specs=[big, big, big, big, pm, gl_s], out_specs=(big, st),
                 scratch=[pltpu.VMEM((2, GH, HD, HD), F32)], sem=("arbitrary",), vmem=VMEM_BIG)(u, w, kd, qd, p, gl)


def _scan_bwd(u, w, kd, qd, p, gl, states, do, L, exch):
    T = u.shape[2]
    nb, big, pm, gl_s, st, do_specs = _scan_specs(T, L, True)

    def body(u_ref, w_ref, kd_ref, qd_ref, p_ref, gl_ref, st_ref, do0_ref, do1_ref,
             du_ref, dw_ref, dkd_ref, dqd_ref, dp_ref, dgl_ref, ds_scr):
        @pl.when(pl.program_id(0) == 0)
        def _():
            ds_scr[...] = jnp.zeros_like(ds_scr)

        ds = [ds_scr[d, h] for d, h in SCAN_STREAMS]
        for i in reversed(range(SCAN_BLOCK)):
            rows = slice(i * CH, (i + 1) * CH)
            mirror = slice((SCAN_BLOCK - 1 - i) * CH, (SCAN_BLOCK - i) * CH)
            _, vjp = jax.vjp(_scan_fn, [st_ref[d, h, i] for d, h in SCAN_STREAMS],
                             *[[r[d, h, rows, :].astype(F32) for d, h in SCAN_STREAMS]
                               for r in (u_ref, w_ref, kd_ref, qd_ref, p_ref)],
                             [gl_ref[d, h, i] for d, h in SCAN_STREAMS])
            dos = [do0_ref[h, rows, :] if d == 0 else do1_ref[h, mirror, :] for d, h in SCAN_STREAMS]
            ds, gu, gw, gkd, gqd, gp, ggl = vjp((dos, ds))
            for n, (d, h) in enumerate(SCAN_STREAMS):
                du_ref[d, h, rows, :] = gu[n]
                dw_ref[d, h, rows, :] = gw[n]
                dkd_ref[d, h, rows, :] = gkd[n]
                dqd_ref[d, h, rows, :] = gqd[n]
                dp_ref[d, h, rows, :] = gp[n]
                dgl_ref[d, h, i] = ggl[n]
        for (d, h), dv in zip(SCAN_STREAMS, ds):
            ds_scr[d, h] = dv

    return _call_carrying(
        body, exch, name="gdn_scan_bwd",
        out_shape=(_sds((2, GH, T, HD)),) * 4 + (_sds((2, GH, T, CH)), _sds((2, GH, T // CH, 1, HD))),
        grid=(nb,), in_specs=[big, big, big, big, pm, gl_s, st, *do_specs], out_specs=(big, big, big, big, pm, gl_s),
        scratch=[pltpu.VMEM((2, GH, HD, HD), F32)], vmem=VMEM_BIG)(u, w, kd, qd, p, gl, states, do, do)


def _gout_fn(o0, o1, z, gw):
    return _rms(o0 + o1) * gw * _silu(z)


def _backward_latent(o_ref, L):
    nl = (o_ref.shape[1] - L) // CH
    return jnp.concatenate([o_ref[1, L + (nl - 1 - j) * CH:L + (nl - j) * CH, :] for j in range(nl)], axis=0)


def _gout_fwd(o, proj, gw, L):
    T = o.shape[2]
    N = T - L
    ob = pl.BlockSpec((2, None, T, HD), lambda h: (0, h, 0, 0))

    def body(o_ref, z_ref, gw_ref, y_ref):
        y_ref[...] = _gout_fn(o_ref[0, L:, :], _backward_latent(o_ref, L), z_ref[L:, :], gw_ref[...]).astype(BF16)

    return _call(body, name="gout_fwd", out_shape=_sds((N, GH * HD), BF16), grid=(GH,),
                 in_specs=[ob, pl.BlockSpec((T, HD), lambda h: (0, C_Z // HD + h)), pl.BlockSpec((1, HD), lambda h: (0, 0))],
                 out_specs=pl.BlockSpec((N, HD), lambda h: (0, h)), sem=("parallel",))(o, proj, gw)


def _gout_bwd(o, proj, gw, dy, dproj, L):
    T = o.shape[2]
    N = T - L
    ob = pl.BlockSpec((2, None, T, HD), lambda h: (0, h, 0, 0))

    def body(o_ref, z_ref, gw_ref, dy_ref, _, do_ref, dz_ref, dgw_ref):
        _, vjp = jax.vjp(_gout_fn, o_ref[0, L:, :], _backward_latent(o_ref, L), z_ref[L:, :], gw_ref[...])
        g0, _, gz, ggw = vjp(dy_ref[...])
        do_ref[:L, :] = jnp.zeros((L, HD), F32)
        do_ref[L:, :] = g0
        dz_ref[:L, :] = jnp.zeros((L, HD), BF16)
        dz_ref[L:, :] = gz.astype(BF16)

        @pl.when(pl.program_id(0) == 0)
        def _():
            dgw_ref[...] = jnp.zeros_like(dgw_ref)

        dgw_ref[...] += ggw

    zb = pl.BlockSpec((T, HD), lambda h: (0, C_Z // HD + h))
    return _call(body, name="gout_bwd", out_shape=(_sds((GH, T, HD)), _sds(dproj.shape, BF16), _sds((1, HD))),
                 grid=(GH,),
                 in_specs=[ob, zb, pl.BlockSpec((1, HD), lambda h: (0, 0)), pl.BlockSpec((N, HD), lambda h: (0, h)), ANYSPEC],
                 out_specs=(pl.BlockSpec((None, T, HD), lambda h: (h, 0, 0)), zb, pl.BlockSpec((1, HD), lambda h: (0, 0))),
                 aliases={4: 1}, sem=("arbitrary",))(o, proj, gw, dy, dproj)


def _merge_fn(pa, pd, ga, gd):
    return jax.nn.sigmoid(ga) * pa + jax.nn.sigmoid(gd) * pd


def _merge_fwd(pa, pd, proj, L, *, br=256):
    N = pa.shape[0]
    lb = L // br
    row = pl.BlockSpec((br, D), lambda i: (i, 0))

    def body(pa_ref, pd_ref, ga_ref, gd_ref, y_ref):
        y_ref[...] = _merge_fn(pa_ref[...], pd_ref[...], ga_ref[...], gd_ref[...]).astype(BF16)

    return _call(body, name="merge_fwd", out_shape=_sds((N, D), BF16), grid=(N // br,),
                 in_specs=[row, row, pl.BlockSpec((br, D), lambda i: (i + lb, C_GATE // D)),
                           pl.BlockSpec((br, D), lambda i: (i + lb, C_GATE // D + 1))],
                 out_specs=row, sem=("parallel",))(pa, pd, proj, proj)


def _merge_bwd(pa, pd, proj, dy, L, *, br=256):
    N = pa.shape[0]
    T = N + L
    lb = L // br
    lrow = pl.BlockSpec((br, D), lambda i: (jnp.maximum(i - lb, 0), 0))

    def body(pa_ref, pd_ref, ga_ref, gd_ref, dy_ref, dpa_ref, dpd_ref, dg_ref):
        lat = pl.program_id(0) >= lb
        _, vjp = jax.vjp(_merge_fn, pa_ref[...], pd_ref[...], ga_ref[...], gd_ref[...])
        gpa, gpd, gga, ggd = vjp(dy_ref[...])
        dpa_ref[...] = gpa.astype(BF16)
        dpd_ref[...] = gpd.astype(BF16)
        dg_ref[:, :D] = jnp.where(lat, gga, 0.0).astype(BF16)
        dg_ref[:, D:] = jnp.where(lat, ggd, 0.0).astype(BF16)

    return _call(body, name="merge_bwd", out_shape=(_sds((N, D), BF16), _sds((N, D), BF16), _sds((T, C_END), BF16)),
                 grid=(T // br,),
                 in_specs=[lrow, lrow, pl.BlockSpec((br, D), lambda i: (i, C_GATE // D)),
                           pl.BlockSpec((br, D), lambda i: (i, C_GATE // D + 1)), lrow],
                 out_specs=(lrow, lrow, pl.BlockSpec((br, 2 * D), lambda i: (i, C_GATE // (2 * D)))),
                 sem=("arbitrary",))(pa, pd, proj, proj, dy)


def _resid_fwd(x, m, mod, i_g, *, name, br=256):
    R = x.shape[0]
    row = pl.BlockSpec((br, D), lambda i: (i, 0))

    def body(x_ref, m_ref, mod_ref, o_ref):
        o_ref[...] = x_ref[...] + mod_ref[i_g:i_g + 1, :] * m_ref[...]

    return _call(body, name=name, out_shape=_sds((R, D)), grid=(R // br,),
                 in_specs=[row, row, pl.BlockSpec((6, D), lambda i: (0, 0))], out_specs=row,
                 sem=("parallel",))(x, m, mod)


def _resid_bwd(dx, m, mod, i_g, *, name, br=256):
    R = dx.shape[0]
    row = pl.BlockSpec((br, D), lambda i: (i, 0))
    vec = pl.BlockSpec((1, D), lambda i: (0, 0))

    def body(dx_ref, m_ref, mod_ref, dm_ref, dg_ref):
        dxv = dx_ref[...]
        dm_ref[...] = (dxv * mod_ref[i_g:i_g + 1, :]).astype(BF16)

        @pl.when(pl.program_id(0) == 0)
        def _():
            dg_ref[...] = jnp.zeros_like(dg_ref)

        dg_ref[...] += jnp.sum(dxv * m_ref[...], axis=0, keepdims=True)

    return _call(body, name=name, out_shape=(_sds((R, D), BF16), _sds((1, D))), grid=(R // br,),
                 in_specs=[row, row, pl.BlockSpec((6, D), lambda i: (0, 0))], out_specs=(row, vec),
                 sem=("arbitrary",))(dx, m, mod)


def _ffn_fn(shifts, ug, uv, wg, wv, bg, bv):
    down, up = shifts

    def conv(x, w, b):
        return down(x) * w[0:1, :] + x * w[1:2, :] + up(x) * w[2:3, :] + b

    return _silu(conv(ug, wg, bg)) * conv(uv, wv, bv)


def _ffn_fwd(up, cw, cb, *, bw=256):
    N = up.shape[0]
    shifts = _make_shift(((0, N),))
    nb = DFF // bw

    def body(ug, uv, wg, wv, bg, bv, a_ref):
        a_ref[...] = _ffn_fn(shifts, ug[...], uv[...], wg[...], wv[...], bg[...], bv[...]).astype(BF16)

    def col(rows, off):
        return pl.BlockSpec((rows, bw), lambda j: (0, j + off))

    return _call(body, name="ffn_fwd", out_shape=_sds((N, DFF), BF16), grid=(nb,),
                 in_specs=[col(N, 0), col(N, nb), col(3, 0), col(3, nb), col(1, 0), col(1, nb)],
                 out_specs=col(N, 0), sem=("parallel",), vmem=VMEM_BIG)(up, up, cw, cw, cb, cb)


def _ffn_bwd(up, cw, cb, da, *, bw=256):
    N = up.shape[0]
    shifts = _make_shift(((0, N),))
    nb = DFF // bw

    def body(ug, uv, wg, wv, bg, bv, da_ref, dug, duv, dwg, dwv, dbg, dbv):
        _, vjp = jax.vjp(functools.partial(_ffn_fn, shifts), ug[...], uv[...], wg[...], wv[...], bg[...], bv[...])
        g = vjp(da_ref[...])
        dug[...] = g[0].astype(BF16)
        duv[...] = g[1].astype(BF16)
        dwg[...], dwv[...], dbg[...], dbv[...] = g[2], g[3], g[4], g[5]

    def col(rows, off):
        return pl.BlockSpec((rows, bw), lambda j: (0, j + off))

    half = (_sds((N, DFF), BF16), _sds((N, DFF), BF16), _sds((3, DFF)), _sds((3, DFF)), _sds((1, DFF)), _sds((1, DFF)))
    dug, duv, dwg, dwv, dbg, dbv = _call(
        body, name="ffn_bwd", out_shape=half, grid=(nb,),
        in_specs=[col(N, 0), col(N, nb), col(3, 0), col(3, nb), col(1, 0), col(1, nb), col(N, 0)],
        out_specs=(col(N, 0), col(N, 0), col(3, 0), col(3, 0), col(1, 0), col(1, 0)),
        sem=("parallel",), vmem=VMEM_BIG)(up, up, cw, cw, cb, cb, da)
    return (jnp.concatenate([dug, duv], axis=1), jnp.concatenate([dwg, dwv], axis=1),
            jnp.concatenate([dbg, dbv], axis=1))


def _head_fn(x1, dn, g2, fw, tgt):
    y = _rms(x1 + g2 * dn) * fw
    err = y - tgt
    return 0.5 * jnp.sum(jnp.mean(err * err, axis=-1))


def _head(x1, dn, mod, fw, tgt, *, br=256):
    N = x1.shape[0]
    row = pl.BlockSpec((br, D), lambda i: (i, 0))
    vec = pl.BlockSpec((1, D), lambda i: (0, 0))
    one = pl.BlockSpec((1, HD), lambda i: (0, 0))

    def body(x1_ref, dn_ref, mod_ref, fw_ref, tgt_ref, loss_ref, dx_ref, ddn_ref, dg_ref, dfw_ref):
        loss, (gx, gdn, gg, gfw) = jax.value_and_grad(_head_fn, argnums=(0, 1, 2, 3))(
            x1_ref[...], dn_ref[...], mod_ref[5:6, :], fw_ref[...], tgt_ref[...])
        dx_ref[...] = gx
        ddn_ref[...] = gdn.astype(BF16)

        @pl.when(pl.program_id(0) == 0)
        def _():
            loss_ref[...] = jnp.zeros_like(loss_ref)
            dg_ref[...] = jnp.zeros_like(dg_ref)
            dfw_ref[...] = jnp.zeros_like(dfw_ref)

        loss_ref[...] += jnp.broadcast_to(loss, (1, HD))
        dg_ref[...] += gg
        dfw_ref[...] += gfw

    return _call(body, name="head", out_shape=(_sds((1, HD)), _sds((N, D)), _sds((N, D), BF16), _sds((1, D)), _sds((1, D))),
                 grid=(N // br,), in_specs=[row, row, pl.BlockSpec((6, D), lambda i: (0, 0)), vec, row],
                 out_specs=(one, row, row, vec, vec), sem=("arbitrary",))(x1, dn, mod, fw, tgt)


def _adamw(w, g, m, v, *, name):
    shape = w.shape
    cols = shape[-1]
    rows = max(1, math.prod(shape[:-1]))
    w2, g2, m2, v2 = (t.reshape(rows, cols) for t in (w, g, m, v))
    br = 256 if rows % 256 == 0 else rows
    c1 = 1.0 - B1 ** STEP
    c2 = 1.0 - B2 ** STEP

    def body(w_ref, g_ref, m_ref, v_ref, d_ref, nm_ref, nv_ref):
        gv = g_ref[...]
        nm = B1 * m_ref[...] + (1.0 - B1) * gv
        nv = B2 * v_ref[...] + (1.0 - B2) * (gv * gv)
        d_ref[...] = -LR * ((nm / c1) / (jnp.sqrt(nv / c2) + AEPS) + WD * w_ref[...])
        nm_ref[...] = nm
        nv_ref[...] = nv

    blk = pl.BlockSpec((br, cols), lambda i: (i, 0))
    outs = _call(body, name=name, out_shape=(_sds((rows, cols)),) * 3, grid=(rows // br,),
                 in_specs=[blk] * 4, out_specs=(blk,) * 3, sem=("parallel",))(w2, g2, m2, v2)
    return tuple(t.reshape(shape) for t in outs)


def _adamw_many(items, *, name):
    k = len(items)
    shapes = [w.shape for w, _, _, _ in items]
    flat = [t.reshape(max(1, math.prod(t.shape[:-1])), t.shape[-1]) for it in items for t in it]
    c1 = 1.0 - B1 ** STEP
    c2 = 1.0 - B2 ** STEP

    def body(*refs):
        ins, outs = refs[:4 * k], refs[4 * k:]
        for i in range(k):
            w_ref, g_ref, m_ref, v_ref = ins[4 * i:4 * i + 4]
            gv = g_ref[...]
            nm = B1 * m_ref[...] + (1.0 - B1) * gv
            nv = B2 * v_ref[...] + (1.0 - B2) * (gv * gv)
            outs[3 * i][...] = -LR * ((nm / c1) / (jnp.sqrt(nv / c2) + AEPS) + WD * w_ref[...])
            outs[3 * i + 1][...] = nm
            outs[3 * i + 2][...] = nv

    res = _call(body, name=name, out_shape=tuple(_sds(flat[4 * i].shape) for i in range(k) for _ in range(3)))(*flat)
    return [tuple(res[3 * i + j].reshape(shapes[i]) for j in range(3)) for i in range(k)]


def _rope_tables(N, L):
    t = jnp.arange(N)
    pos = jnp.stack([(t // GRID_W).astype(F32), (t % GRID_W).astype(F32)], axis=1)
    inv = ROPE_THETA ** (-jnp.arange(0, HD // 2, 2, dtype=F32) / (HD // 2))
    ang = pos[:, :, None] * inv[None, None, :]
    cos = jnp.broadcast_to(jnp.cos(ang)[:, :, None, :], (N, 2, 2, HD // 4)).reshape(N, HD)
    sin = jnp.broadcast_to(jnp.sin(ang)[:, :, None, :], (N, 2, 2, HD // 4))
    sin = (sin * jnp.array([-1.0, 1.0], F32)[None, None, :, None]).reshape(N, HD)
    cos = jnp.concatenate([jnp.ones((L, HD), F32), cos], axis=0)
    sin = jnp.concatenate([jnp.zeros((L, HD), F32), sin], axis=0)
    return cos, sin


def _pad_lanes(v, off=0):
    return jnp.zeros((1, HD), F32).at[0, off:off + v.shape[0]].set(v)


def _local_step(x, ctx, tgt, mod_lat, mod_ctx, w_in, shards, small):
    N, L = x.shape[0], ctx.shape[0]
    T = N + L
    bounds = ((0, L), (L, T))
    qw, kw, gw = small["q_norm_w"], small["k_norm_w"], small["gdn_norm_w"]
    conv_w, ffn_w, ffn_b, fnw = small["conv_qkv_w"], small["ffn_conv_w"], small["ffn_conv_b"], small["final_norm_w"]
    alog = _pad_lanes(small["a_log"].reshape(-1), 2 * GH)
    dtb = _pad_lanes(small["dt_bias"].reshape(-1), 2 * GH)
    cos, sin = _rope_tables(N, L)
    bt = T
    bnl = 256 if N % 1024 else 1024

    hc = _normmod_fwd(ctx, mod_ctx, 0, 1, name="normmod_ctx")
    hx = _normmod_fwd(x, mod_lat, 0, 1, name="normmod_x")
    h1 = jnp.concatenate([hc, hx], axis=0)
    proj = _mm(h1, w_in, name="mm_in", M=T, N=C_END, K=D, tb=True, bm=bt, bn=1024)
    aq, ak, av = _aprep_fwd(proj, cos, sin, qw, kw)
    (attn, attn32, lse), (up_g,) = _attn_fwd(aq, ak, av, L, _GatherTwoLevel([shards["w_up"]]))
    gq = _gprep_fwd(proj, conv_w, 0, bounds)
    gk = _gprep_fwd(proj, conv_w, 1, bounds)
    gv = _gprep_fwd(proj, conv_w, 2, bounds)
    bl = _bl_fwd(proj, alog, dtb)
    intra, (down_g, pa_g, pd_g, out_g) = _intra_fwd(
        gq, gk, gv, bl, L, _GatherTwoLevel([shards[n] for n in ("w_down", "w_pa", "w_pd", "w_out")]))
    w_up, w_down = up_g.reshape(2 * DFF, D), down_g.reshape(DFF, D)
    w_pa, w_pd, w_out = pa_g.reshape(D, D), pd_g.reshape(D, D), out_g.reshape(D, D)
    xinv, intra = intra[6], intra[:6]
    o, states = _scan_fwd(*intra, L)
    gdn = _gout_fwd(o, proj, gw, L)
    pa = _mm(attn, w_pa, name="mm_pa", M=N, N=D, K=D, bm=bnl)
    pd = _mm(gdn, w_pd, name="mm_pd", M=N, N=D, K=D, bm=bnl)
    y = _merge_fwd(pa, pd, proj, L)
    m = _mm(y, w_out, name="mm_out", M=N, N=D, K=D, bm=bnl)
    x1 = _resid_fwd(x, m, mod_lat, 2, name="resid1")
    h2 = _normmod_fwd(x1, mod_lat, 3, 4, name="normmod_x1")
    up = _mm(h2, w_up, name="mm_up", M=N, N=2 * DFF, K=D, tb=True, bm=bnl, bn=2 * DFF // 4)
    a = _ffn_fwd(up, ffn_w, ffn_b)
    dn = _mm(a, w_down, name="mm_down", M=N, N=D, K=DFF, bm=bnl)
    loss, dx2, ddn, dg2, dfnw = _head(x1, dn, mod_lat, fnw, tgt)

    da = _mm(ddn, w_down, name="mm_down_dx", M=N, N=DFF, K=D, tb=True, bm=bnl, bn=DFF // 2)
    g_down = _mm(a, ddn, name="mm_down_dw", M=DFF, N=D, K=N, ta=True, bm=DFF // 2, out_dtype=BF16)
    dup, d_ffn_w, d_ffn_b = _ffn_bwd(up, ffn_w, ffn_b, da)
    dh2 = _mm(dup, w_up, name="mm_up_dx", M=N, N=D, K=2 * DFF, bm=bnl, bk=2 * DFF // 4)
    g_up = _mm(dup, h2, name="mm_up_dw", M=2 * DFF, N=D, K=N, ta=True, bm=2 * DFF // 4, out_dtype=BF16)
    dx1, dsh2, dsc2 = _normmod_bwd(x1, mod_lat, 3, 4, dh2, 0, dx2, name="normmod_x1_bwd")
    dm, dg1 = _resid_bwd(dx1, m, mod_lat, 2, name="resid1_bwd")
    dy = _mm(dm, w_out, name="mm_out_dx", M=N, N=D, K=D, tb=True, bm=bnl)
    g_out = _mm(y, dm, name="mm_out_dw", M=D, N=D, K=N, ta=True, out_dtype=BF16)
    dpa, dpd, dproj = _merge_bwd(pa, pd, proj, dy, L)
    dattn = _mm(dpa, w_pa, name="mm_pa_dx", M=N, N=D, K=D, tb=True, bm=bnl)
    g_pa = _mm(attn, dpa, name="mm_pa_dw", M=D, N=D, K=N, ta=True, out_dtype=BF16)
    dgdn = _mm(dpd, w_pd, name="mm_pd_dx", M=N, N=D, K=D, tb=True, bm=bnl)
    g_pd = _mm(gdn, dpd, name="mm_pd_dw", M=D, N=D, K=N, ta=True, out_dtype=BF16)
    do, dproj, dgw = _gout_bwd(o, proj, gw, dgdn, dproj, L)
    cts, recv_a = _scan_bwd(*intra, states, do, L, _Exchange([g_out.reshape(NDEV, D // NDEV, D)], True))
    (dgq, dgk, dgv, dbl), recv_b = _intra_bwd(gq, gk, gv, bl, xinv, cts, L, _Exchange(
        [g_pa.reshape(NDEV, D // NDEV, D), g_pd.reshape(NDEV, D // NDEV, D), g_up.reshape(NDEV, 2 * DFF // NDEV, D)], True))
    dproj, dwq = _gprep_bwd(proj, conv_w, 0, bounds, dgq, dproj)
    dproj, dwk = _gprep_bwd(proj, conv_w, 1, bounds, dgk, dproj)
    dproj, dwv = _gprep_bwd(proj, conv_w, 2, bounds, dgv, dproj)
    dproj, dalog, ddtb = _bl_bwd(proj, alog, dtb, dbl, dproj)
    (daq_h, dak_h, dav_h), recv_c = _attn_bwd(aq, ak, av, attn32, lse, dattn, L, _Exchange(
        [g_down.reshape(NDEV, DFF // NDEV, D)], True))
    recv = dict(zip(("w_out", "w_pa", "w_pd", "w_up", "w_down"), recv_a + recv_b + recv_c))
    dproj, dqw, dkw = _aprep_bwd(proj, cos, sin, qw, kw, daq_h, dak_h, dav_h, dproj, L)
    g_in = _mm(dproj, h1, name="mm_in_dw", M=C_END, N=D, K=T, ta=True, bm=1024, out_dtype=BF16)
    g_in = _unpad_columns(g_in).reshape(NDEV, W_END // NDEV, D)
    own_in = lax.dynamic_index_in_dim(g_in, _position()[3], axis=0, keepdims=False)
    *pending, token = _scatter_start(g_in, None, (0, D // 2), (), name="scatter_g_in_a_start")
    dh1 = _mm(dproj, w_in, name="mm_in_dx", M=T, N=D, K=C_END, bm=bt, bk=1024, after=(token,))
    grad_x, dsh1, dsc1 = _normmod_bwd(x, mod_lat, 0, 1, dh1, L, dx1, name="normmod_x_bwd")
    _, dcsh1, dcsc1 = _normmod_bwd(ctx, mod_ctx, 0, 1, dh1, 0, None, name="normmod_ctx_bwd")

    z1 = jnp.zeros((1, D), F32)
    dmod_lat = jnp.concatenate([dsh1, dsc1, dg1, dsh2, dsc2, dg2], axis=0)
    dmod_ctx = jnp.concatenate([dcsh1, dcsc1, z1, z1, z1, z1], axis=0)
    gsmall = {
        "q_norm_w": dqw, "k_norm_w": dkw, "gdn_norm_w": dgw,
        "conv_qkv_w": jnp.concatenate([dwq, dwk, dwv], axis=1),
        "a_log": dalog[0, 2 * GH:4 * GH], "dt_bias": ddtb[0, 2 * GH:4 * GH],
        "ffn_conv_w": d_ffn_w, "ffn_conv_b": d_ffn_b, "final_norm_w": dfnw,
    }
    return loss[0, 0], grad_x, (pending, own_in), recv, dmod_lat, dmod_ctx, gsmall


HBM = pl.BlockSpec(memory_space=pltpu.HBM)
ANYSPEC = pl.BlockSpec(memory_space=pl.ANY)


def _position():
    x, y, c = lax.axis_index("x"), lax.axis_index("y"), lax.axis_index("c")
    return x, y, c, 4 * x + 2 * y + c


def _peer(x, y, c, k):
    px = 1 - x if k & 4 else x
    py = 1 - y if k & 2 else y
    pc = 1 - c if k & 1 else c
    return (px, py, pc), 4 * px + 2 * py + pc


def _exchange(arrs, *, name, scatter):
    exch = _Exchange(arrs, scatter)
    n = exch.n

    def body(*refs):
        ins, outs, sems = refs[:n], refs[n:2 * n], refs[2 * n:]
        exch.start(ins, outs, sems)
        exch.finish(ins, outs, sems)

    outs = pl.pallas_call(body, name=name, out_shape=exch.out_shape, in_specs=[HBM] * n, out_specs=(HBM,) * n,
                          scratch_shapes=exch.scratch,
                          compiler_params=pltpu.CompilerParams(has_side_effects=True))(*arrs)
    return list(outs)


class _Exchange:
    def __init__(self, arrs, scatter):
        self.arrs, self.scatter, self.n = list(arrs), scatter, len(arrs)
        self.out_shape = tuple(_sds(a.shape if scatter else (NDEV,) + a.shape, a.dtype) for a in arrs)
        self.scratch = [pltpu.SemaphoreType.DMA((self.n, NDEV - 1)), pltpu.SemaphoreType.DMA((self.n, NDEV - 1)),
                        pltpu.SemaphoreType.DMA((self.n,))]

    def _copies(self, ins, outs, sems):
        send, recv, loc = sems
        x, y, c, me = _position()
        local = [pltpu.make_async_copy(ins[a].at[me] if self.scatter else ins[a], outs[a].at[me], loc.at[a])
                 for a in range(self.n)]
        remote = []
        for k in range(1, NDEV):
            peer, pid = _peer(x, y, c, k)
            for a in range(self.n):
                src = ins[a].at[pid] if self.scatter else ins[a]
                remote.append(pltpu.make_async_remote_copy(
                    src_ref=src, dst_ref=outs[a].at[me], send_sem=send.at[a, k - 1], recv_sem=recv.at[a, k - 1],
                    device_id=peer, device_id_type=MESH))
        return local, remote

    def start(self, ins, outs, sems):
        local, remote = self._copies(ins, outs, sems)
        for cp in local + remote:
            cp.start()

    def finish(self, ins, outs, sems):
        local, remote = self._copies(ins, outs, sems)
        for cp in remote:
            cp.wait()
        for cp in local:
            cp.wait()


class _GatherTwoLevel:
    scatter = False

    def __init__(self, arrs):
        self.arrs, self.n = list(arrs), len(arrs)
        self.out_shape = tuple(_sds((NDEV,) + a.shape, a.dtype) for a in arrs)
        self.scratch = [pltpu.SemaphoreType.DMA((self.n, NDEV - 1)), pltpu.SemaphoreType.DMA((self.n, NDEV - 1)),
                        pltpu.SemaphoreType.DMA((self.n,))]

    def _parts(self, ins, outs, sems):
        send, recv, loc = sems
        x, y, c, _ = _position()
        me, sibling = (x, y, c), (x, y, 1 - c)
        chips = [(1 - x, y), (x, 1 - y), (1 - x, 1 - y)]
        parts = []
        for a in range(self.n):
            slot = lambda px, py, pc, a=a: outs[a].at[4 * px + 2 * py + pc]

            def copy(k, owner, to, src=None, a=a, slot=slot):
                return pltpu.make_async_remote_copy(
                    src_ref=slot(*owner) if src is None else src, dst_ref=slot(*owner), send_sem=send.at[a, k],
                    recv_sem=recv.at[a, k], device_id=to, device_id_type=MESH)

            parts.append(dict(
                mine=pltpu.make_async_copy(ins[a], slot(*me), loc.at[a]),
                first=[copy(0, me, sibling, src=ins[a])] + [copy(1 + j, me, (*ch, c), src=ins[a]) for j, ch in enumerate(chips)],
                arrive=[copy(1 + j, (*ch, c), me) for j, ch in enumerate(chips)],
                passed=[copy(4 + j, (*ch, c), sibling) for j, ch in enumerate(chips)],
                rest=[copy(0, sibling, me)] + [copy(4 + j, (*ch, 1 - c), me) for j, ch in enumerate(chips)]))
        return parts

    def start(self, ins, outs, sems):
        for p in self._parts(ins, outs, sems):
            p["mine"].start()
            for cp in p["first"]:
                cp.start()

    def middle(self, ins, outs, sems):
        for p in self._parts(ins, outs, sems):
            for got, fwd in zip(p["arrive"], p["passed"]):
                got.wait_recv()
                fwd.start()

    def finish(self, ins, outs, sems):
        for p in self._parts(ins, outs, sems):
            for cp in p["rest"]:
                cp.wait_recv()
            for cp in p["first"] + p["passed"]:
                cp.wait_send()
            p["mine"].wait()


def _gather_two_level(blocks, *, name):
    exch = _GatherTwoLevel(blocks)
    n = exch.n

    def body(*refs):
        ins, outs, sems = refs[:n], refs[n:2 * n], refs[2 * n:]
        exch.start(ins, outs, sems)
        exch.middle(ins, outs, sems)
        exch.finish(ins, outs, sems)

    outs = pl.pallas_call(body, name=name, out_shape=exch.out_shape, in_specs=[HBM] * n, out_specs=(HBM,) * n,
                          scratch_shapes=exch.scratch,
                          compiler_params=pltpu.CompilerParams(has_side_effects=True))(*blocks)
    return list(outs)


SEM = pl.BlockSpec(memory_space=pltpu.SEMAPHORE)


def _scatter_copies(src_ref, land_ref, send_sems, recv_sems, cols):
    x, y, c, me = _position()
    span = (slice(None), pl.ds(*cols))
    copies = []
    for k in range(1, NDEV):
        peer, pid = _peer(x, y, c, k)
        copies.append(pltpu.make_async_remote_copy(
            src_ref=src_ref.at[pid].at[span], dst_ref=land_ref.at[me].at[span], send_sem=send_sems.at[k - 1],
            recv_sem=recv_sems.at[k - 1], device_id=peer, device_id_type=MESH))
    return copies


SPLIT_EFFECT = pltpu.SideEffectType.DATAFLOW_SIDE_EFFECTING


def _scatter_start(parts, land, cols, after, *, name):
    na = len(after)
    if land is None:
        land = lax.empty(parts.shape, parts.dtype)

    def body(src_ref, land_ref, *rest):
        send_sems, recv_sems, _, _, token = rest[na:]
        for cp in _scatter_copies(src_ref, land_ref, send_sems, recv_sems, cols):
            cp.start()
        token[...] = jnp.zeros_like(token)

    return pl.pallas_call(
        body, name=name,
        out_shape=(pltpu.SemaphoreType.DMA((NDEV - 1,)), pltpu.SemaphoreType.DMA((NDEV - 1,)),
                   pltpu.HBM(parts.shape, parts.dtype), pltpu.HBM(parts.shape, parts.dtype), _sds((8, HD))),
        in_specs=(HBM, HBM) + (pl.BlockSpec(memory_space=pl.ANY),) * na,
        out_specs=(SEM, SEM, HBM, HBM, pl.BlockSpec(memory_space=pltpu.VMEM)),
        input_output_aliases={0: 2, 1: 3}, compiler_params=pltpu.CompilerParams(has_side_effects=SPLIT_EFFECT),
    )(pltpu.with_memory_space_constraint(parts, pltpu.HBM), pltpu.with_memory_space_constraint(land, pltpu.HBM), *after)


def _scatter_wait(send_sems, recv_sems, src_thru, land_thru, cols, after, *, name):
    na = len(after)

    def body(src_ref, land_ref, send_sems, recv_sems, *rest):
        for cp in _scatter_copies(src_ref, land_ref, send_sems, recv_sems, cols):
            cp.wait_send()
            cp.wait_recv()

    return pl.pallas_call(
        body, name=name,
        out_shape=(pltpu.HBM(src_thru.shape, src_thru.dtype), pltpu.HBM(land_thru.shape, land_thru.dtype)),
        in_specs=(HBM, HBM, SEM, SEM) + (pl.BlockSpec(memory_space=pl.ANY),) * na, out_specs=(HBM, HBM),
        input_output_aliases={0: 0, 1: 1}, compiler_params=pltpu.CompilerParams(has_side_effects=SPLIT_EFFECT),
    )(src_thru, land_thru, send_sems, recv_sems, *after)


def _cast_bf16(w, *, name):
    rows, cols = w.shape
    br = 128 if rows % 128 == 0 else rows

    def body(w_ref, o_ref):
        o_ref[...] = w_ref[...].astype(BF16)

    blk = pl.BlockSpec((br, cols), lambda i: (i, 0))
    return _call(body, name=name, out_shape=_sds((rows, cols), BF16), grid=(rows // br,), in_specs=[blk],
                 out_specs=blk, sem=("parallel",))(w)


def _sum_slots(a, *, name):
    _, R, C = a.shape

    def body(a_ref, o_ref):
        s = a_ref[0]
        for d in range(1, NDEV):
            s = s + a_ref[d]
        o_ref[...] = s

    return _call(body, name=name, out_shape=_sds((R, C)))(a)


MODROWS = 16


def _mod_fwd(c9, w, b):
    cols = w.shape[1]

    def body(c_ref, w_ref, b_ref, o_ref):
        o_ref[...] = _nn(_silu(c_ref[...]), w_ref[...]) + b_ref[...]

    return _call(body, name="mod_fwd", out_shape=_sds((MODROWS, cols)))(c9, w, b)


def _mod_bwd(c9, dmy, dall, w):
    cols = w.shape[1]

    def body(c_ref, dmy_ref, dall_ref, w_ref, gw_ref, gb_ref, cp_ref):
        sc = _silu(c_ref[...])
        rows = lax.broadcasted_iota(jnp.int32, (MODROWS, 1), 0)
        d = dmy_ref[...]
        d_ctx = jnp.where(rows == NDEV, d, 0.0)
        sc_ctx = jnp.where(rows == NDEV, sc, 0.0)
        outer = lax.dot_general(sc_ctx, d_ctx, (((0,), (0,)), ((), ())), precision=HI, preferred_element_type=F32)
        gw_ref[...] = _tn(jnp.where(rows < NDEV, sc, 0.0), jnp.where(rows < NDEV, d, 0.0)) + outer
        gb_ref[...] = jnp.sum(dall_ref[...], axis=0, keepdims=True)
        cp_ref[...] = jnp.sum(_nt(d_ctx, w_ref[...]), axis=0, keepdims=True)

    return _call(body, name="mod_bwd", out_shape=(_sds((D, cols)), _sds((1, 6 * D)), _sds((1, D))),
                 vmem=VMEM_BIG)(c9, dmy, dall, w)


def _cctx_finish(parts, c_ctx, after):
    VM = pl.BlockSpec(memory_space=pltpu.VMEM)

    def body(p_ref, c_ref, *rest):
        o_ref = rest[-1]
        s = p_ref[0]
        for d in range(1, NDEV):
            s = s + p_ref[d]
        _, vjp = jax.vjp(_silu, c_ref[...])
        o_ref[...] = vjp(s)[0]

    return _call(body, name="cctx_finish", out_shape=_sds((1, D)),
                 in_specs=[VM, VM] + [pl.BlockSpec(memory_space=pl.ANY)] * len(after))(parts, c_ctx, *after)


def _adamw_recv(w, recv, m, v, *, name, own=None):
    rows, cols = w.shape
    bc = 256
    c1 = 1.0 - B1 ** STEP
    c2 = 1.0 - B2 ** STEP
    has_own = own is not None

    def body(w_ref, r_ref, m_ref, v_ref, *rest):
        g_ref, d_ref, nm_ref, nv_ref = rest[-4:]
        me = _position()[3]

        def slot(d):
            return jnp.where(me == d, rest[0][...], r_ref[d]) if has_own else r_ref[d]

        gv = slot(0).astype(F32)
        for d in range(1, NDEV):
            gv = gv + slot(d).astype(F32)
        nm = B1 * m_ref[...] + (1.0 - B1) * gv
        nv = B2 * v_ref[...] + (1.0 - B2) * (gv * gv)
        g_ref[...] = gv
        d_ref[...] = -LR * ((nm / c1) / (jnp.sqrt(nv / c2) + AEPS) + WD * w_ref[...])
        nm_ref[...] = nm
        nv_ref[...] = nv

    blk = pl.BlockSpec((rows, bc), lambda j: (0, j))
    return _call(body, name=name, out_shape=(_sds((rows, cols)),) * 4, grid=(cols // bc,),
                 in_specs=[blk, pl.BlockSpec((NDEV, rows, bc), lambda j: (0, 0, j)), blk, blk] + [blk] * has_own,
                 out_specs=(blk,) * 4, sem=("parallel",), vmem=VMEM_BIG)(w, recv, m, v, *([own] if has_own else []))


P_LAT, P_CTX, P_FNW, P_FFNB, P_CONV, P_FFNW, P_MISC, P_ROWS = 0, 8, 16, 24, 32, 48, 72, 80


def _rows_of(v, nrows):
    flat = v.reshape(-1)
    return jnp.pad(flat, (0, nrows * D - flat.shape[0])).reshape(nrows, D)


def _by_columns(g):
    n, r, c = g.shape
    return jnp.transpose(g, (1, 0, 2)).reshape(r, n * c)


def kernel(x, c, ctx, c_ctx, w_mod, b_mod, w_in, q_norm_w, k_norm_w, conv_qkv_w, a_log, dt_bias, gdn_norm_w, w_pa, w_pd, w_out, w_up, ffn_conv_w, ffn_conv_b, w_down, final_norm_w, loss_target, m_c_ctx, m_w_mod, m_b_mod, m_w_in, m_q_norm_w, m_k_norm_w, m_conv_qkv_w, m_a_log, m_dt_bias, m_gdn_norm_w, m_w_pa, m_w_pd, m_w_out, m_w_up, m_ffn_conv_w, m_ffn_conv_b, m_w_down, m_final_norm_w, v_c_ctx, v_w_mod, v_b_mod, v_w_in, v_q_norm_w, v_k_norm_w, v_conv_qkv_w, v_a_log, v_dt_bias, v_gdn_norm_w, v_w_pa, v_w_pd, v_w_out, v_w_up, v_ffn_conv_w, v_ffn_conv_b, v_w_down, v_final_norm_w):
    _, _, _, me = _position()
    mcols = w_mod.shape[2]

    transposed = ("w_in", "w_up")
    big = {"w_in": w_in[0].T, "w_pa": w_pa[0], "w_pd": w_pd[0], "w_out": w_out[0], "w_up": w_up[0].T, "w_down": w_down[0]}
    names = list(big)
    shards = {n: _cast_bf16(big[n], name="cast_" + n) for n in names}
    w_in_g, c_all, conv_g, ffnw_g = _gather_two_level([shards["w_in"], c, conv_qkv_w[0], ffn_conv_w[0]],
                                                      name="gather_w_in")
    w_in_full = w_in_g.reshape(W_END, D)
    w_in_pad = _pad_columns(w_in_full)

    c9 = jnp.concatenate([c_all.reshape(NDEV, D), jnp.pad(c_ctx[None], ((0, MODROWS - NDEV - 1), (0, 0)))], axis=0)
    b_loc = lax.dynamic_slice(b_mod, (0, me * mcols), (1, mcols))
    mod_all, = _exchange([_mod_fwd(c9, w_mod[0], b_loc)], name="gather_mod", scatter=False)
    mod_lat = lax.dynamic_index_in_dim(mod_all, me, axis=1, keepdims=False).reshape(6, D)
    mod_ctx = mod_all[:, NDEV, :].reshape(6, D)

    small = {"q_norm_w": q_norm_w, "k_norm_w": k_norm_w, "gdn_norm_w": gdn_norm_w, "a_log": a_log, "dt_bias": dt_bias,
             "conv_qkv_w": _by_columns(conv_g), "ffn_conv_w": _by_columns(ffnw_g), "ffn_conv_b": ffn_conv_b,
             "final_norm_w": final_norm_w[None]}
    loss_me, grad_x, (pending_in, own_in), recv, dmod_lat, dmod_ctx, gs = _local_step(
        x[0], ctx[0], loss_target[0], mod_lat, mod_ctx, w_in_pad, shards, small)

    moments = {"w_in": (m_w_in, v_w_in), "w_pa": (m_w_pa, v_w_pa), "w_pd": (m_w_pd, v_w_pd),
               "w_out": (m_w_out, v_w_out), "w_up": (m_w_up, v_w_up), "w_down": (m_w_down, v_w_down)}
    res = {}
    def finish(n, outs):
        return tuple((t.T if n in transposed else t)[None] for t in outs)

    def moment(t, n):
        return t[0].T if n in transposed else t[0]

    for n in recv:
        res[n] = finish(n, _adamw_recv(big[n], recv[n], moment(moments[n][0], n), moment(moments[n][1], n),
                                       name="adamw_" + n))

    misc = jnp.concatenate([gs["q_norm_w"][0], gs["k_norm_w"][0], gs["gdn_norm_w"][0], gs["a_log"], gs["dt_bias"],
                            loss_me[None]])
    pack = jnp.concatenate([_rows_of(dmod_lat, P_CTX - P_LAT), _rows_of(dmod_ctx, P_FNW - P_CTX),
                            _rows_of(gs["final_norm_w"], P_FFNB - P_FNW), _rows_of(gs["ffn_conv_b"], P_CONV - P_FFNB),
                            _rows_of(gs["conv_qkv_w"], P_FFNW - P_CONV), _rows_of(gs["ffn_conv_w"], P_MISC - P_FFNW),
                            _rows_of(misc, P_ROWS - P_MISC)], axis=0)
    pack_all, = _exchange([pack], name="gather_pack", scatter=False)
    tot = _sum_slots(pack_all, name="sum_pack")
    dall = jnp.concatenate([pack_all[:, P_LAT:P_LAT + 6, :].reshape(NDEV, 6 * D),
                            jnp.pad(tot[P_CTX:P_CTX + 6].reshape(1, 6 * D), ((0, MODROWS - NDEV - 1), (0, 0)))], axis=0)
    dmy = lax.dynamic_slice(dall, (0, me * mcols), (MODROWS, mcols))
    g_w_mod, g_b_mod, cpart = _mod_bwd(c9, dmy, dall, w_mod[0])
    cparts, = _exchange([cpart], name="gather_cctx", scatter=False)
    sems_a, land = pending_in[:2], pending_in[3]
    *sems_b, g_in_thru, land, token_b = _scatter_start(pending_in[2], land, (D // 2, D // 2), (cparts,),
                                                       name="scatter_g_in_b_start")
    g_c_ctx = _cctx_finish(cparts, c_ctx[None], (token_b,))[0]

    nconv, nffn = 3 * GH * HD, 2 * DFF
    conv_tot = tot[P_CONV:P_FFNW].reshape(-1)[:3 * nconv].reshape(3, nconv)
    ffnw_tot = tot[P_FFNW:P_MISC].reshape(-1)[:3 * nffn].reshape(3, nffn)
    mrow = tot[P_MISC]
    grads = {
        "c_ctx": g_c_ctx, "w_mod": g_w_mod[None], "b_mod": g_b_mod,
        "q_norm_w": mrow[None, 0:HD], "k_norm_w": mrow[None, HD:2 * HD], "gdn_norm_w": mrow[None, 2 * HD:3 * HD],
        "conv_qkv_w": lax.dynamic_slice(conv_tot, (0, me * (nconv // NDEV)), (3, nconv // NDEV))[None],
        "a_log": mrow[3 * HD:3 * HD + 2 * GH].reshape(1, 2, GH),
        "dt_bias": mrow[3 * HD + 2 * GH:3 * HD + 4 * GH].reshape(1, 2, GH),
        "ffn_conv_w": lax.dynamic_slice(ffnw_tot, (0, me * (nffn // NDEV)), (3, nffn // NDEV))[None],
        "ffn_conv_b": tot[P_FFNB:P_CONV].reshape(-1)[:nffn][None],
        "final_norm_w": tot[P_FNW],
    }
    loss = mrow[3 * HD + 4 * GH]
    given = {"c_ctx": (c_ctx, m_c_ctx, v_c_ctx), "w_mod": (w_mod, m_w_mod, v_w_mod), "b_mod": (b_mod, m_b_mod, v_b_mod),
             "q_norm_w": (q_norm_w, m_q_norm_w, v_q_norm_w), "k_norm_w": (k_norm_w, m_k_norm_w, v_k_norm_w),
             "conv_qkv_w": (conv_qkv_w, m_conv_qkv_w, v_conv_qkv_w), "a_log": (a_log, m_a_log, v_a_log),
             "dt_bias": (dt_bias, m_dt_bias, v_dt_bias), "gdn_norm_w": (gdn_norm_w, m_gdn_norm_w, v_gdn_norm_w),
             "ffn_conv_w": (ffn_conv_w, m_ffn_conv_w, v_ffn_conv_w), "ffn_conv_b": (ffn_conv_b, m_ffn_conv_b, v_ffn_conv_b),
             "final_norm_w": (final_norm_w, m_final_norm_w, v_final_norm_w)}
    res["w_mod"] = (grads["w_mod"],) + _adamw(w_mod, grads["w_mod"], m_w_mod, v_w_mod, name="adamw_w_mod")
    small_names = [n for n in given if n != "w_mod"]
    updates = _adamw_many([(given[n][0], grads[n], given[n][1], given[n][2]) for n in small_names], name="adamw_small")
    for n, upd in zip(small_names, updates):
        res[n] = (grads[n],) + upd

    g_in_thru, land = _scatter_wait(*sems_a, g_in_thru, land, (0, D // 2), [res[n][1] for n in res],
                                    name="scatter_g_in_a_wait")
    _, land = _scatter_wait(*sems_b, g_in_thru, land, (D // 2, D // 2), (), name="scatter_g_in_b_wait")
    res["w_in"] = finish("w_in", _adamw_recv(big["w_in"], land, moment(m_w_in, "w_in"), moment(v_w_in, "w_in"),
                                             name="adamw_w_in", own=own_in))

    order = ["c_ctx", "w_mod", "b_mod", "w_in", "q_norm_w", "k_norm_w", "conv_qkv_w", "a_log", "dt_bias", "gdn_norm_w",
             "w_pa", "w_pd", "w_out", "w_up", "ffn_conv_w", "ffn_conv_b", "w_down", "final_norm_w"]
    return (loss, grad_x[None], *[res[n][0] for n in order], *[res[n][1] for n in order],
            *[res[n][2] for n in order], *[res[n][3] for n in order])
```

```python
import functools
import math

import jax
import jax.numpy as jnp
from jax import lax
from jax.experimental import pallas as pl
from jax.experimental.pallas import tpu as pltpu

F32 = jnp.float32
BF16 = jnp.bfloat16
HI = lax.Precision.HIGHEST
MESH = pl.DeviceIdType.MESH

NDEV = 8
D = 1024
HD = 128
AH, AKV, GRP = 8, 2, 4
GH = 8
CH = 64
DFF = 2816
GRID_W = 64
EPS = 1e-6
ROPE_THETA = 10000.0
LOG2E = math.log2(math.e)
C_KV, C_AQ, C_QKV, C_BL, C_Z, C_GATE, C_END = 0, 512, 1536, 4608, 5120, 6144, 8192
W_QKV, W_AQ, W_Z, W_END = 512, 3616, 4640, 7712


def _pad_columns(w):
    zeros = jnp.zeros((C_Z - C_QKV - (W_AQ - W_QKV), D), w.dtype)
    return jnp.concatenate([w[:W_QKV], w[W_AQ:W_Z], w[W_QKV:W_AQ], zeros, w[W_Z:]], axis=0)


def _unpad_columns(g):
    return jnp.concatenate([g[:C_AQ], g[C_QKV:C_QKV + W_AQ - W_QKV], g[C_AQ:C_QKV], g[C_Z:]], axis=0)
LR, B1, B2, AEPS, WD, STEP = 0.001, 0.9, 0.999, 1e-08, 0.01, 10
VMEM_BIG = 56 * 1024 * 1024
INTRA_FWD_CHUNKS = 36
INTRA_BWD_CHUNKS = 36


def _call(body, *, name, out_shape, grid=None, in_specs=None, out_specs=None, scratch=(), sem=None,
          vmem=None, aliases=None):
    params = {}
    if sem is not None:
        params["dimension_semantics"] = sem
    if vmem is not None:
        params["vmem_limit_bytes"] = vmem
    kw = {}
    if grid is not None:
        kw["grid"] = grid
    if in_specs is not None:
        kw["in_specs"] = in_specs
    if out_specs is not None:
        kw["out_specs"] = out_specs
    if aliases:
        kw["input_output_aliases"] = aliases
    return pl.pallas_call(body, name=name, out_shape=out_shape, scratch_shapes=list(scratch),
                          compiler_params=pltpu.CompilerParams(**params), **kw)


def _call_carrying(body, exch, *, name, out_shape, grid, in_specs, out_specs, scratch=(), vmem=None):
    n, nin, nout, nscr = exch.n, len(in_specs), len(out_shape), len(scratch)
    steps = math.prod(grid)
    mid = (2 * steps) // 3

    def wrapped(*refs):
        ins, cins = refs[:nin], refs[nin:nin + n]
        outs, couts = refs[nin + n:nin + n + nout], refs[nin + n + nout:nin + 2 * n + nout]
        scr, sems = refs[nin + 2 * n + nout:nin + 2 * n + nout + nscr], refs[nin + 2 * n + nout + nscr:]
        ids = [pl.program_id(i) for i in range(len(grid))]
        first = functools.reduce(jnp.logical_and, [i == 0 for i in ids])
        last = functools.reduce(jnp.logical_and, [i == g - 1 for i, g in zip(ids, grid)])

        @pl.when(first)
        def _():
            exch.start(cins, couts, sems)

        if hasattr(exch, "middle"):
            linear = functools.reduce(lambda acc, ig: acc * ig[1] + ig[0], zip(ids, grid), 0)

            @pl.when(linear == mid)
            def _():
                exch.middle(cins, couts, sems)

        body(*ins, *outs, *scr)

        @pl.when(last)
        def _():
            exch.finish(cins, couts, sems)

    params = {"dimension_semantics": ("arbitrary",) * len(grid)}
    if vmem is not None:
        params["vmem_limit_bytes"] = vmem
    fn = pl.pallas_call(wrapped, name=name, out_shape=tuple(out_shape) + exch.out_shape, grid=grid,
                        in_specs=list(in_specs) + [HBM] * n, out_specs=tuple(out_specs) + (HBM,) * n,
                        scratch_shapes=list(scratch) + exch.scratch, compiler_params=pltpu.CompilerParams(**params))

    def run(*args):
        res = fn(*args, *exch.arrs)
        return res[:nout], list(res[nout:])

    return run


def _sds(shape, dtype=F32):
    return jax.ShapeDtypeStruct(tuple(shape), dtype)


def _dot(a, b, ca, cb):
    return lax.dot_general(a.astype(BF16), b.astype(BF16), (((ca,), (cb,)), ((), ())),
                           preferred_element_type=F32)


@jax.custom_vjp
def _nn(a, b):
    return _dot(a, b, 1, 0)


@jax.custom_vjp
def _nt(a, b):
    return _dot(a, b, 1, 1)


@jax.custom_vjp
def _tn(a, b):
    return _dot(a, b, 0, 0)


_nn.defvjp(lambda a, b: (_nn(a, b), (a, b)), lambda r, g: (_nt(g, r[1]), _tn(r[0], g)))
_nt.defvjp(lambda a, b: (_nt(a, b), (a, b)), lambda r, g: (_nn(g, r[1]), _tn(g, r[0])))
_tn.defvjp(lambda a, b: (_tn(a, b), (a, b)), lambda r, g: (_nt(r[1], g), _nn(r[0], g)))


def _mdot(a, b):
    return jnp.dot(a, b, precision=lax.Precision.HIGH, preferred_element_type=F32)


def _maskdot(mask, a, cm):
    hi = a.astype(BF16)
    r = a - hi.astype(F32)
    mid = r.astype(BF16)
    lo = (r - mid.astype(F32)).astype(BF16)
    mb = mask.astype(BF16)
    dims = (((cm,), (0,)), ((), ()))
    return (lax.dot_general(mb, hi, dims, preferred_element_type=F32)
            + lax.dot_general(mb, mid, dims, preferred_element_type=F32)
            + lax.dot_general(mb, lo, dims, preferred_element_type=F32))


@jax.custom_vjp
def _mask_nn(mask, a):
    return _maskdot(mask, a, 1)


_mask_nn.defvjp(lambda mask, a: (_maskdot(mask, a, 1), mask),
                lambda mask, g: (jnp.zeros_like(mask), _maskdot(mask, g, 0)))


@jax.custom_vjp
def _saved_inverse(lmat, x):
    return x


def _saved_inverse_bwd(x, g):
    t = lax.dot_general(x, g, (((0,), (0,)), ((), ())), precision=lax.Precision.HIGH, preferred_element_type=F32)
    dl = lax.dot_general(t, x, (((1,), (1,)), ((), ())), precision=lax.Precision.HIGH, preferred_element_type=F32)
    return -dl, jnp.zeros_like(x)


_saved_inverse.defvjp(lambda lmat, x: (x, x), _saved_inverse_bwd)


def _row_ids(shape):
    return lax.broadcasted_iota(jnp.int32, shape, 0)


def _shift_rows(x, down, bounds):
    n = x.shape[0]
    rows = _row_ids(x.shape)
    y = pltpu.roll(x, 1 if down else n - 1, 0)
    edge = functools.reduce(jnp.logical_or, [rows == (s if down else e - 1) for s, e in bounds])
    return jnp.where(edge, 0.0, y)


def _make_shift(bounds):
    @jax.custom_vjp
    def down(x):
        return _shift_rows(x, True, bounds)

    @jax.custom_vjp
    def up(x):
        return _shift_rows(x, False, bounds)

    down.defvjp(lambda x: (down(x), None), lambda _, g: (up(g),))
    up.defvjp(lambda x: (up(x), None), lambda _, g: (down(g),))
    return down, up


@jax.custom_vjp
def _swap32(x):
    lane = lax.broadcasted_iota(jnp.int32, x.shape, x.ndim - 1)
    return jnp.where((lane % 64) < 32, pltpu.roll(x, HD - 32, x.ndim - 1), pltpu.roll(x, 32, x.ndim - 1))


_swap32.defvjp(lambda x: (_swap32(x), None), lambda _, g: (_swap32(g),))


def _rms(x):
    return x * lax.rsqrt(jnp.mean(x * x, axis=-1, keepdims=True) + EPS)


def _silu(x):
    return x * jax.nn.sigmoid(x)


def _mm(a, b, *, name, M, N, K, ta=False, tb=False, out_dtype=F32, bm=None, bn=None, bk=None, after=()):
    bm, bn, bk = bm or M, bn or N, bk or K
    assert M % bm == 0 and N % bn == 0 and K % bk == 0, (name, M, N, K, bm, bn, bk)
    nk = K // bk
    ca, cb = (0 if ta else 1), (1 if tb else 0)
    na = len(after)

    def body(a_ref, b_ref, *rest):
        o_ref, acc = rest[na], rest[na + 1:]
        r = _dot(a_ref[...], b_ref[...], ca, cb)
        if nk == 1:
            o_ref[...] = r.astype(out_dtype)
        else:
            acc_ref, = acc
            k = pl.program_id(2)

            @pl.when(k == 0)
            def _():
                acc_ref[...] = r

            @pl.when(k > 0)
            def _():
                acc_ref[...] += r

            @pl.when(k == nk - 1)
            def _():
                o_ref[...] = acc_ref[...].astype(out_dtype)

    a_spec = pl.BlockSpec((bk, bm), lambda i, j, k: (k, i)) if ta else pl.BlockSpec((bm, bk), lambda i, j, k: (i, k))
    b_spec = pl.BlockSpec((bn, bk), lambda i, j, k: (j, k)) if tb else pl.BlockSpec((bk, bn), lambda i, j, k: (k, j))
    return _call(body, name=name, out_shape=_sds((M, N), out_dtype), grid=(M // bm, N // bn, nk),
                 in_specs=[a_spec, b_spec] + [pl.BlockSpec(memory_space=pl.ANY)] * na,
                 out_specs=pl.BlockSpec((bm, bn), lambda i, j, k: (i, j)),
                 scratch=[pltpu.VMEM((bm, bn), F32)] if nk > 1 else [],
                 sem=("parallel", "parallel", "arbitrary"), vmem=VMEM_BIG)(a, b, *after)


def _normmod_fn(x, sh, sc):
    return _rms(x) * (1.0 + sc) + sh


def _normmod_fwd(x, mod, i_sh, i_sc, *, name, br=256):
    R = x.shape[0]

    def body(x_ref, mod_ref, o_ref):
        o_ref[...] = _normmod_fn(x_ref[...], mod_ref[i_sh:i_sh + 1, :], mod_ref[i_sc:i_sc + 1, :]).astype(BF16)

    return _call(body, name=name, out_shape=_sds((R, D), BF16), grid=(R // br,),
                 in_specs=[pl.BlockSpec((br, D), lambda i: (i, 0)), pl.BlockSpec((6, D), lambda i: (0, 0))],
                 out_specs=pl.BlockSpec((br, D), lambda i: (i, 0)), sem=("parallel",))(x, mod)


def _normmod_bwd(x, mod, i_sh, i_sc, dh, dh_off, res, *, name, br=256):
    R = x.shape[0]
    ob = dh_off // br
    has_res = res is not None

    def body(x_ref, mod_ref, dh_ref, *rest):
        if has_res:
            res_ref, dx_ref, dsh_ref, dsc_ref = rest
        else:
            dx_ref, dsh_ref, dsc_ref = rest
        sh, sc = mod_ref[i_sh:i_sh + 1, :], mod_ref[i_sc:i_sc + 1, :]
        _, vjp = jax.vjp(_normmod_fn, x_ref[...], sh, sc)
        dx, dsh, dsc = vjp(dh_ref[...])
        dx_ref[...] = dx + res_ref[...] if has_res else dx

        @pl.when(pl.program_id(0) == 0)
        def _():
            dsh_ref[...] = jnp.zeros_like(dsh_ref)
            dsc_ref[...] = jnp.zeros_like(dsc_ref)

        dsh_ref[...] += dsh
        dsc_ref[...] += dsc

    row = pl.BlockSpec((br, D), lambda i: (i, 0))
    vec = pl.BlockSpec((1, D), lambda i: (0, 0))
    ins = [row, pl.BlockSpec((6, D), lambda i: (0, 0)), pl.BlockSpec((br, D), lambda i: (i + ob, 0))]
    args = [x, mod, dh]
    if has_res:
        ins.append(row)
        args.append(res)
    return _call(body, name=name, out_shape=(_sds((R, D)), _sds((1, D)), _sds((1, D))), grid=(R // br,),
                 in_specs=ins, out_specs=(row, vec, vec), sem=("arbitrary",))(*args)


def _rope(x, cos, sin):
    return x * cos + _swap32(x) * sin


def _aprep_fn(qs, ks, cos, sin, qw, kw):
    return ([_rope(_rms(q) * qw, cos, sin) for q in qs], [_rope(_rms(k) * kw, cos, sin) for k in ks])


def _aprep_fwd(proj, cos, sin, qw, kw, *, br=256):
    T = proj.shape[0]

    def body(x_ref, cos_ref, sin_ref, qw_ref, kw_ref, q_ref, k_ref, v_ref):
        qs = [x_ref[:, C_AQ + h * HD:C_AQ + (h + 1) * HD] for h in range(AH)]
        ks = [x_ref[:, h * HD:(h + 1) * HD] for h in range(AKV)]
        qo, ko = _aprep_fn(qs, ks, cos_ref[...], sin_ref[...], qw_ref[...], kw_ref[...])
        for h in range(AH):
            q_ref[h] = qo[h].astype(BF16)
        for h in range(AKV):
            k_ref[h] = ko[h].astype(BF16)
            v_ref[h] = x_ref[:, (AKV + h) * HD:(AKV + h + 1) * HD].astype(BF16)

    tab = pl.BlockSpec((br, HD), lambda i: (i, 0))
    vec = pl.BlockSpec((1, HD), lambda i: (0, 0))
    return _call(body, name="aprep_fwd",
                 out_shape=(_sds((AH, T, HD), BF16), _sds((AKV, T, HD), BF16), _sds((AKV, T, HD), BF16)),
                 grid=(T // br,),
                 in_specs=[pl.BlockSpec((br, C_QKV), lambda i: (i, 0)), tab, tab, vec, vec],
                 out_specs=(pl.BlockSpec((AH, br, HD), lambda i: (0, i, 0)),
                            pl.BlockSpec((AKV, br, HD), lambda i: (0, i, 0)),
                            pl.BlockSpec((AKV, br, HD), lambda i: (0, i, 0))),
                 sem=("parallel",))(proj, cos, sin, qw, kw)


def _aprep_bwd(proj, cos, sin, qw, kw, dq, dk, dv, dproj, L, *, br=256):
    T = proj.shape[0]
    lb = L // br

    def body(x_ref, cos_ref, sin_ref, qw_ref, kw_ref, dq_ref, dk_ref, dv_ref, _, dx_ref, dqw_ref, dkw_ref):
        i = pl.program_id(0)
        qs = [x_ref[:, C_AQ + h * HD:C_AQ + (h + 1) * HD] for h in range(AH)]
        ks = [x_ref[:, h * HD:(h + 1) * HD] for h in range(AKV)]
        _, vjp = jax.vjp(_aprep_fn, qs, ks, cos_ref[...], sin_ref[...], qw_ref[...], kw_ref[...])
        is_lat = i >= lb
        dqs = [jnp.where(is_lat, dq_ref[h], 0.0) for h in range(AH)]
        dks = [dk_ref[h] for h in range(AKV)]
        gq, gk, _, _, gqw, gkw = vjp((dqs, dks))
        for h in range(AH):
            dx_ref[:, C_AQ + h * HD:C_AQ + (h + 1) * HD] = gq[h].astype(BF16)
        for h in range(AKV):
            dx_ref[:, h * HD:(h + 1) * HD] = gk[h].astype(BF16)
            dx_ref[:, (AKV + h) * HD:(AKV + h + 1) * HD] = dv_ref[h].astype(BF16)

        @pl.when(i == 0)
        def _():
            dqw_ref[...] = jnp.zeros_like(dqw_ref)
            dkw_ref[...] = jnp.zeros_like(dkw_ref)

        dqw_ref[...] += gqw
        dkw_ref[...] += gkw

    tab = pl.BlockSpec((br, HD), lambda i: (i, 0))
    vec = pl.BlockSpec((1, HD), lambda i: (0, 0))
    kvb = pl.BlockSpec((AKV, br, HD), lambda i: (0, i, 0))
    blk = pl.BlockSpec((br, C_QKV), lambda i: (i, 0))
    return _call(body, name="aprep_bwd", out_shape=(_sds(dproj.shape, BF16), _sds((1, HD)), _sds((1, HD))),
                 grid=(T // br,),
                 in_specs=[blk, tab, tab, vec, vec,
                           pl.BlockSpec((AH, br, HD), lambda i: (0, jnp.maximum(i - lb, 0), 0)), kvb, kvb, ANYSPEC],
                 out_specs=(blk, vec, vec), aliases={8: 0},
                 sem=("arbitrary",))(proj, cos, sin, qw, kw, dq, dk, dv, dproj)


def _attn_grad(q, k, v, o, lse2, do):
    scale = HD ** -0.5
    p = jnp.exp2(_dot(q, k, 1, 1) * (scale * LOG2E) - lse2)
    dp = _dot(do, v, 1, 1)
    ds = p * (dp - jnp.sum(do * o, axis=-1, keepdims=True)) * scale
    return _dot(ds, k, 1, 0), _dot(ds, q, 0, 0), _dot(p, do, 0, 0)


ATTN_KEYS = 256


def _attn_fwd(q, k, v, L, exch, *, bq=128):
    T = q.shape[1]
    N = T - L
    lb = L // bq
    assert T % ATTN_KEYS == 0
    scale = HD ** -0.5
    heads = range(GRP)

    def body(q_ref, k_ref, v_ref, o_ref, o32_ref, lse_ref):
        qs = [q_ref[g] for g in heads]
        m = [jnp.full((bq, 1), -jnp.inf, F32) for _ in heads]
        l = [jnp.zeros((bq, 1), F32) for _ in heads]
        acc = [jnp.zeros((bq, HD), F32) for _ in heads]
        for c in range(T // ATTN_KEYS):
            kc, vc = k_ref[c * ATTN_KEYS:(c + 1) * ATTN_KEYS, :], v_ref[c * ATTN_KEYS:(c + 1) * ATTN_KEYS, :]
            s = [_dot(qs[g], kc, 1, 1) * (scale * LOG2E) for g in heads]
            m_new = [jnp.maximum(m[g], jnp.max(s[g], axis=-1, keepdims=True)) for g in heads]
            alpha = [jnp.exp2(m[g] - m_new[g]) for g in heads]
            p = [jnp.exp2(s[g] - m_new[g]) for g in heads]
            l = [l[g] * alpha[g] + jnp.sum(p[g], axis=-1, keepdims=True) for g in heads]
            acc = [acc[g] * alpha[g] + _dot(p[g], vc, 1, 0) for g in heads]
            m = m_new
        for g in heads:
            o = acc[g] / l[g]
            o_ref[:, g * HD:(g + 1) * HD] = o.astype(BF16)
            o32_ref[:, g * HD:(g + 1) * HD] = o
            lse_ref[g] = jnp.broadcast_to(m[g] + jnp.log2(l[g]), (bq, HD))

    kvb = pl.BlockSpec((None, T, HD), lambda g, i: (g, 0, 0))
    ob = pl.BlockSpec((bq, GRP * HD), lambda g, i: (i, g))
    return _call_carrying(
        body, exch, name="attn_fwd",
        out_shape=(_sds((N, AH * HD), BF16), _sds((N, AH * HD)), _sds((AH, N, HD))), grid=(AKV, N // bq),
        in_specs=[pl.BlockSpec((GRP, bq, HD), lambda g, i: (g, i + lb, 0)), kvb, kvb],
        out_specs=(ob, ob, pl.BlockSpec((GRP, bq, HD), lambda g, i: (g, i, 0))), vmem=VMEM_BIG)(q, k, v)


def _attn_bwd(q, k, v, o32, lse, do, L, exch, *, bq=128):
    T = q.shape[1]
    N = T - L
    lb = L // bq

    def body(q_ref, k_ref, v_ref, o_ref, lse_ref, do_ref, dq_ref, dk_ref, dv_ref):
        rows = lambda r: jnp.concatenate([r[:, g * HD:(g + 1) * HD] for g in range(GRP)], axis=0)
        lse = jnp.max(lse_ref[...].reshape(GRP * bq, HD), axis=-1, keepdims=True)
        dq, dk, dv = _attn_grad(q_ref[...].reshape(GRP * bq, HD), k_ref[...], v_ref[...], rows(o_ref), lse, rows(do_ref))
        dq_ref[...] = dq.reshape(GRP, bq, HD)

        @pl.when(pl.program_id(1) == 0)
        def _():
            dk_ref[...] = jnp.zeros_like(dk_ref)
            dv_ref[...] = jnp.zeros_like(dv_ref)

        dk_ref[...] += dk
        dv_ref[...] += dv

    kvb = pl.BlockSpec((None, T, HD), lambda g, i: (g, 0, 0))
    qb = pl.BlockSpec((GRP, bq, HD), lambda g, i: (g, i + lb, 0))
    hb = pl.BlockSpec((GRP, bq, HD), lambda g, i: (g, i, 0))
    ob = pl.BlockSpec((bq, GRP * HD), lambda g, i: (i, g))
    return _call_carrying(body, exch, name="attn_bwd",
                          out_shape=(_sds((AH, N, HD)), _sds((AKV, T, HD)), _sds((AKV, T, HD))), grid=(AKV, N // bq),
                          in_specs=[qb, kvb, kvb, ob, hb, ob], out_specs=(hb, kvb, kvb),
                          vmem=VMEM_BIG)(q, k, v, o32, lse, do)


def _gprep_fn(kind, shifts, x, w):
    down, up = shifts
    y = down(x) * w[0:1, :] + x * w[1:2, :] + up(x) * w[2:3, :]
    a = _silu(y)
    if kind == 2:
        return a
    a = a * lax.rsqrt(jnp.sum(a * a, axis=-1, keepdims=True) + EPS)
    return a * (HD ** -0.5) if kind == 0 else a


def _gprep_fwd(proj, conv_w, kind, bounds):
    T = proj.shape[0]
    shifts = _make_shift(bounds)
    cb = C_QKV // HD + kind * GH

    def body(x_ref, w_ref, o_ref):
        o_ref[...] = _gprep_fn(kind, shifts, x_ref[...], w_ref[...])

    return _call(body, name=f"gprep_fwd{kind}", out_shape=_sds((GH, T, HD)), grid=(GH,),
                 in_specs=[pl.BlockSpec((T, HD), lambda h: (0, cb + h)),
                           pl.BlockSpec((3, HD), lambda h: (0, kind * GH + h))],
                 out_specs=pl.BlockSpec((None, T, HD), lambda h: (h, 0, 0)), sem=("parallel",))(proj, conv_w)


def _gprep_bwd(proj, conv_w, kind, bounds, dy, dproj):
    T = proj.shape[0]
    shifts = _make_shift(bounds)
    cb = C_QKV // HD + kind * GH

    def body(x_ref, w_ref, dy_ref, _, dx_ref, dw_ref):
        _, vjp = jax.vjp(functools.partial(_gprep_fn, kind, shifts), x_ref[...], w_ref[...])
        dx, dw = vjp(dy_ref[0] + dy_ref[1])
        dx_ref[...] = dx.astype(BF16)
        dw_ref[...] = dw

    return _call(body, name=f"gprep_bwd{kind}", out_shape=(_sds(dproj.shape, BF16), _sds((3, GH * HD))), grid=(GH,),
                 in_specs=[pl.BlockSpec((T, HD), lambda h: (0, cb + h)),
                           pl.BlockSpec((3, HD), lambda h: (0, kind * GH + h)),
                           pl.BlockSpec((2, None, T, HD), lambda h: (0, h, 0, 0)), ANYSPEC],
                 out_specs=(pl.BlockSpec((T, HD), lambda h: (0, cb + h)), pl.BlockSpec((3, HD), lambda h: (0, h))),
                 aliases={3: 0}, sem=("parallel",))(proj, conv_w, dy, dproj)


def _bl_fn(x, alog, dtb):
    lane = lax.broadcasted_iota(jnp.int32, x.shape, 1)
    beta = jax.nn.sigmoid(x)
    z = x + dtb
    sp = jnp.maximum(z, 0.0) + jnp.log1p(jnp.exp(-jnp.abs(z)))
    la = -jnp.exp(alog) * sp
    return jnp.where(lane < 2 * GH, beta, jnp.where(lane < 4 * GH, la, 0.0))


def _bl_fwd(proj, alog, dtb, *, br=256):
    T = proj.shape[0]

    def body(x_ref, a_ref, d_ref, o_ref):
        o_ref[...] = _bl_fn(x_ref[...], a_ref[...], d_ref[...])

    vec = pl.BlockSpec((1, HD), lambda i: (0, 0))
    return _call(body, name="bl_fwd", out_shape=_sds((T, HD)), grid=(T // br,),
                 in_specs=[pl.BlockSpec((br, HD), lambda i: (i, C_BL // HD)), vec, vec],
                 out_specs=pl.BlockSpec((br, HD), lambda i: (i, 0)), sem=("parallel",))(proj, alog, dtb)


def _bl_bwd(proj, alog, dtb, dbl, dproj, *, br=256):
    T = proj.shape[0]
    wide = C_Z - C_BL

    def body(x_ref, a_ref, d_ref, g_ref, _, dx_ref, da_ref, dd_ref):
        g = g_ref[0, 0]
        for d in range(2):
            for h in range(GH):
                if d or h:
                    g = g + g_ref[d, h]
        _, vjp = jax.vjp(_bl_fn, x_ref[...], a_ref[...], d_ref[...])
        dx, da, dd = vjp(g)
        dx_ref[:, :HD] = dx.astype(BF16)
        dx_ref[:, HD:] = jnp.zeros((br, wide - HD), BF16)

        @pl.when(pl.program_id(0) == 0)
        def _():
            da_ref[...] = jnp.zeros_like(da_ref)
            dd_ref[...] = jnp.zeros_like(dd_ref)

        da_ref[...] += da
        dd_ref[...] += dd

    vec = pl.BlockSpec((1, HD), lambda i: (0, 0))
    return _call(body, name="bl_bwd", out_shape=(_sds(dproj.shape, BF16), _sds((1, HD)), _sds((1, HD))), grid=(T // br,),
                 in_specs=[pl.BlockSpec((br, HD), lambda i: (i, C_BL // HD)), vec, vec,
                           pl.BlockSpec((2, GH, br, HD), lambda i: (0, 0, i, 0)), ANYSPEC],
                 out_specs=(pl.BlockSpec((br, wide), lambda i: (i, C_BL // wide)), vec, vec), aliases={4: 0},
                 sem=("arbitrary",))(proj, alog, dtb, dbl, dproj)


def _chunk_masks(d):
    ii = lax.broadcasted_iota(jnp.int32, (CH, CH), 0)
    jj = lax.broadcasted_iota(jnp.int32, (CH, CH), 1)
    eye = (ii == jj).astype(F32)
    before = jnp.where(d == 0, (jj < ii).astype(F32), (jj > ii).astype(F32))
    return before, before + eye, eye


def _same_block(b):
    ii = lax.broadcasted_iota(jnp.int32, (CH, CH), 0)
    jj = lax.broadcasted_iota(jnp.int32, (CH, CH), 1)
    shift = b.bit_length() - 1
    return (jnp.right_shift(ii, shift) == jnp.right_shift(jj, shift)).astype(F32)


def _intra_fn(masks, sel_b, sel_l, qs, ks, vs, bls, xs=None):
    before, ateq, eye = masks
    inc = ateq > 0.0
    each = lambda f, *ls: [f(*t) for t in zip(*ls)]
    beta = each(lambda bl: jnp.sum(bl * sel_b, axis=-1, keepdims=True), bls)
    la = each(lambda bl: jnp.sum(bl * sel_l, axis=-1, keepdims=True), bls)
    gam = each(lambda a: _mask_nn(ateq, jnp.broadcast_to(a, (CH, HD))), la)
    gi = each(lambda g: g[:, :CH], gam)
    gj = each(lambda g: jnp.transpose(g)[:CH, :], gam)
    kk = each(lambda k: _nt(k, k), ks)
    qk = each(_nt, qs, ks)
    dec = each(lambda a, b: jnp.where(inc, jnp.exp(jnp.where(inc, a - b, 0.0)), 0.0), gi, gj)
    lmat = each(lambda b, d, m: before * (b * d * m), beta, dec, kk)
    if xs is None:
        same = lambda b: _same_block(b)
        l8 = each(lambda m: m * same(8), lmat)
        x = each(lambda m: eye - m, l8)
        p2 = each(lambda m: _mdot(m, m), l8)
        y = each(lambda a, b: _mdot(jnp.concatenate([a, b], axis=0), b), x, p2)
        x = each(lambda a, t: a + t[:CH], x, y)
        x = each(lambda a, t: a + _mdot(a, t[CH:]), x, y)
        for b in (8, 16, 32):
            below = same(2 * b) - same(b)
            x = each(lambda a, m: a - _mdot(a, _mdot(m * below, a)), x, lmat)
    else:
        x = each(_saved_inverse, lmat, xs)
    eg = each(jnp.exp, gam)
    u = each(lambda a, b, v: _mdot(a, b * v), x, beta, vs)
    w = each(lambda a, b, e, k: _mdot(a, (b * e) * k), x, beta, eg, ks)
    tot = each(lambda a: jnp.sum(a, axis=0, keepdims=True), la)
    kd = each(lambda k, t, g: k * jnp.exp(t - g), ks, tot, gam)
    gl = each(lambda t: jnp.broadcast_to(jnp.exp(t), (1, HD)), tot)
    qd = each(lambda q, e: q * e, qs, eg)
    p = each(lambda d, m: d * m, dec, qk)
    return (u, w, kd, qd, p, gl, x) if xs is None else (u, w, kd, qd, p, gl)


def _dir_head_sel(d, h):
    lane = lax.broadcasted_iota(jnp.int32, (1, HD), 1)
    return (lane == d * GH + h).astype(F32), (lane == 2 * GH + d * GH + h).astype(F32)


def _intra_specs(T, G):
    nc = T // CH
    assert nc % G == 0
    qkv = pl.BlockSpec((None, G * CH, HD), lambda d, h, c: (h, c, 0))
    bl = pl.BlockSpec((G * CH, HD), lambda d, h, c: (c, 0))
    big = pl.BlockSpec((None, None, G * CH, HD), lambda d, h, c: (d, h, c, 0))
    pm = pl.BlockSpec((None, None, G * CH, CH), lambda d, h, c: (d, h, c, 0))
    gl = pl.BlockSpec((None, None, G, 1, HD), lambda d, h, c: (d, h, c, 0, 0))
    shapes = (_sds((2, GH, T, HD)),) + (_sds((2, GH, T, HD), BF16),) * 3 + (
        _sds((2, GH, T, CH), BF16), _sds((2, GH, nc, 1, HD)), _sds((2, GH, T, CH)))
    return nc, qkv, bl, big, pm, gl, shapes


def _chunks_per_step(T, most):
    nc = T // CH
    return max(g for g in range(1, most + 1) if nc % g == 0)


def _chunk_at(g, d, nc, ncc):
    pos = _visit_pos(g, d, nc, ncc)
    return pos, pl.ds(pl.multiple_of(pos * CH, CH), CH)


def _intra_fwd(q, k, v, bl, L, exch):
    T = q.shape[1]
    G = _chunks_per_step(T, INTRA_FWD_CHUNKS)
    nc, qkv_s, bl_s, big, pm, gl_s, shapes = _intra_specs(T, G)
    assert G == nc
    ncc = L // CH

    def body(q_ref, k_ref, v_ref, bl_ref, u_ref, w_ref, kd_ref, qd_ref, p_ref, gl_ref, x_ref):
        d, h = pl.program_id(0), pl.program_id(1)
        sb, sl = _dir_head_sel(d, h)
        rows = [slice(g * CH, (g + 1) * CH) for g in range(G)]
        outs = _intra_fn(_chunk_masks(d), sb, sl, *[[r[s, :] for s in rows] for r in (q_ref, k_ref, v_ref, bl_ref)])
        for g in range(G):
            pos, at = _chunk_at(g, d, nc, ncc)
            for r, o in zip((u_ref, w_ref, kd_ref, qd_ref, p_ref, x_ref), outs[:5] + outs[6:]):
                r[at, :] = o[g].astype(r.dtype)
            gl_ref[pos] = outs[5][g]

    return _call_carrying(body, exch, name="gdn_intra_fwd", out_shape=shapes, grid=(2, GH, nc // G),
                          in_specs=[qkv_s, qkv_s, qkv_s, bl_s], out_specs=(big, big, big, big, pm, gl_s, pm))(q, k, v, bl)


def _intra_bwd(q, k, v, bl, xinv, cts, L, exch):
    T = q.shape[1]
    G = _chunks_per_step(T, INTRA_BWD_CHUNKS)
    nc, qkv_s, bl_s, big, pm, gl_s, _ = _intra_specs(T, G)
    assert G == nc
    ncc = L // CH

    def body(q_ref, k_ref, v_ref, bl_ref, x_ref, du, dw, dkd, dqd, dp, dgl, dq_ref, dk_ref, dv_ref, dbl_ref):
        d, h = pl.program_id(0), pl.program_id(1)
        sb, sl = _dir_head_sel(d, h)
        rows = [slice(g * CH, (g + 1) * CH) for g in range(G)]
        places = [_chunk_at(g, d, nc, ncc) for g in range(G)]
        fn = functools.partial(_intra_fn, _chunk_masks(d), sb, sl, xs=[x_ref[at, :] for _, at in places])
        _, vjp = jax.vjp(fn, *[[r[s, :] for s in rows] for r in (q_ref, k_ref, v_ref, bl_ref)])
        cts = tuple([r[at, :] for _, at in places] for r in (du, dw, dkd, dqd, dp)) + ([dgl[pos] for pos, _ in places],)
        grads = vjp(cts)
        for g in range(G):
            for r, o in zip((dq_ref, dk_ref, dv_ref, dbl_ref), grads):
                r[rows[g], :] = o[g]

    return _call_carrying(body, exch, name="gdn_intra_bwd", out_shape=(_sds((2, GH, T, HD)),) * 4,
                          grid=(2, GH, nc // G), in_specs=[qkv_s, qkv_s, qkv_s, bl_s, pm, big, big, big, big, pm, gl_s],
                          out_specs=(big,) * 4)(q, k, v, bl, xinv, *cts)


def _scan_fn(s, u, w, kd, qd, p, gl):
    each = lambda f, *ls: [f(*t) for t in zip(*ls)]
    ws = each(_nn, w, s)
    delta = each(lambda a, b: a - b, u, ws)
    kdd = each(_tn, kd, delta)
    s_new = each(lambda g, a, b: g * a + b, gl, s, kdd)
    qs = each(_nn, qd, s)
    pd = each(_nn, p, delta)
    return each(lambda a, b: a + b, qs, pd), s_new


SCAN_BLOCK = 4


def _visit_pos(c, d, nc, ncc):
    back = ncc - 1 - c if c < ncc else ncc + (nc - 1 - c)
    return jnp.where(d == 0, c, back)


def _scan_specs(T, L, back):
    tb = SCAN_BLOCK * CH
    assert T % tb == 0 and L % tb == 0
    nb, ncb = T // tb, L // tb
    at = (lambda t: nb - 1 - t) if back else (lambda t: t)
    big = pl.BlockSpec((2, GH, tb, HD), lambda t: (0, 0, at(t), 0))
    pm = pl.BlockSpec((2, GH, tb, CH), lambda t: (0, 0, at(t), 0))
    gl = pl.BlockSpec((2, GH, SCAN_BLOCK, 1, HD), lambda t: (0, 0, at(t), 0, 0))
    st = pl.BlockSpec((2, GH, SCAN_BLOCK, HD, HD), lambda t: (0, 0, at(t), 0, 0))

    def natural(b):
        return jnp.where(b < ncb, ncb - 1 - b, nb - 1 - (b - ncb))

    do_specs = (pl.BlockSpec((GH, tb, HD), lambda t: (0, at(t), 0)),
                pl.BlockSpec((GH, tb, HD), lambda t: (0, natural(at(t)), 0)))
    return nb, big, pm, gl, st, do_specs


SCAN_STREAMS = [(d, h) for d in (0, 1) for h in range(GH)]


def _scan_fwd(u, w, kd, qd, p, gl, L):
    T = u.shape[2]
    nb, big, pm, gl_s, st, _ = _scan_specs(T, L, False)

    def body(u_ref, w_ref, kd_ref, qd_ref, p_ref, gl_ref, o_ref, st_ref, s_scr):
        @pl.when(pl.program_id(0) == 0)
        def _():
            s_scr[...] = jnp.zeros_like(s_scr)

        s = [s_scr[d, h] for d, h in SCAN_STREAMS]
        for i in range(SCAN_BLOCK):
            rows = slice(i * CH, (i + 1) * CH)
            for (d, h), sv in zip(SCAN_STREAMS, s):
                st_ref[d, h, i] = sv
            o, s = _scan_fn(s, *[[r[d, h, rows, :].astype(F32) for d, h in SCAN_STREAMS]
                                 for r in (u_ref, w_ref, kd_ref, qd_ref, p_ref)],
                            [gl_ref[d, h, i] for d, h in SCAN_STREAMS])
            for (d, h), ov in zip(SCAN_STREAMS, o):
                o_ref[d, h, rows, :] = ov
        for (d, h), sv in zip(SCAN_STREAMS, s):
            s_scr[d, h] = sv

    return _call(body, name="gdn_scan_fwd", out_shape=(_sds((2, GH, T, HD)), _sds((2, GH, T // CH, HD, HD))),
                 grid=(nb,), in_specs=[big, big, big, big, pm, gl_s], out_specs=(big, st),
                 scratch=[pltpu.VMEM((2, GH, HD, HD), F32)], sem=("arbitrary",), vmem=VMEM_BIG)(u, w, kd, qd, p, gl)


def _scan_bwd(u, w, kd, qd, p, gl, states, do, L, exch):
    T = u.shape[2]
    nb, big, pm, gl_s, st, do_specs = _scan_specs(T, L, True)

    def body(u_ref, w_ref, kd_ref, qd_ref, p_ref, gl_ref, st_ref, do0_ref, do1_ref,
             du_ref, dw_ref, dkd_ref, dqd_ref, dp_ref, dgl_ref, ds_scr):
        @pl.when(pl.program_id(0) == 0)
        def _():
            ds_scr[...] = jnp.zeros_like(ds_scr)

        ds = [ds_scr[d, h] for d, h in SCAN_STREAMS]
        for i in reversed(range(SCAN_BLOCK)):
            rows = slice(i * CH, (i + 1) * CH)
            mirror = slice((SCAN_BLOCK - 1 - i) * CH, (SCAN_BLOCK - i) * CH)
            _, vjp = jax.vjp(_scan_fn, [st_ref[d, h, i] for d, h in SCAN_STREAMS],
                             *[[r[d, h, rows, :].astype(F32) for d, h in SCAN_STREAMS]
                               for r in (u_ref, w_ref, kd_ref, qd_ref, p_ref)],
                             [gl_ref[d, h, i] for d, h in SCAN_STREAMS])
            dos = [do0_ref[h, rows, :] if d == 0 else do1_ref[h, mirror, :] for d, h in SCAN_STREAMS]
            ds, gu, gw, gkd, gqd, gp, ggl = vjp((dos, ds))
            for n, (d, h) in enumerate(SCAN_STREAMS):
                du_ref[d, h, rows, :] = gu[n]
                dw_ref[d, h, rows, :] = gw[n]
                dkd_ref[d, h, rows, :] = gkd[n]
                dqd_ref[d, h, rows, :] = gqd[n]
                dp_ref[d, h, rows, :] = gp[n]
                dgl_ref[d, h, i] = ggl[n]
        for (d, h), dv in zip(SCAN_STREAMS, ds):
            ds_scr[d, h] = dv

    return _call_carrying(
        body, exch, name="gdn_scan_bwd",
        out_shape=(_sds((2, GH, T, HD)),) * 4 + (_sds((2, GH, T, CH)), _sds((2, GH, T // CH, 1, HD))),
        grid=(nb,), in_specs=[big, big, big, big, pm, gl_s, st, *do_specs], out_specs=(big, big, big, big, pm, gl_s),
        scratch=[pltpu.VMEM((2, GH, HD, HD), F32)], vmem=VMEM_BIG)(u, w, kd, qd, p, gl, states, do, do)


def _gout_fn(o0, o1, z, gw):
    return _rms(o0 + o1) * gw * _silu(z)


def _backward_latent(o_ref, L):
    nl = (o_ref.shape[1] - L) // CH
    return jnp.concatenate([o_ref[1, L + (nl - 1 - j) * CH:L + (nl - j) * CH, :] for j in range(nl)], axis=0)


def _gout_fwd(o, proj, gw, L):
    T = o.shape[2]
    N = T - L
    ob = pl.BlockSpec((2, None, T, HD), lambda h: (0, h, 0, 0))

    def body(o_ref, z_ref, gw_ref, y_ref):
        y_ref[...] = _gout_fn(o_ref[0, L:, :], _backward_latent(o_ref, L), z_ref[L:, :], gw_ref[...]).astype(BF16)

    return _call(body, name="gout_fwd", out_shape=_sds((N, GH * HD), BF16), grid=(GH,),
                 in_specs=[ob, pl.BlockSpec((T, HD), lambda h: (0, C_Z // HD + h)), pl.BlockSpec((1, HD), lambda h: (0, 0))],
                 out_specs=pl.BlockSpec((N, HD), lambda h: (0, h)), sem=("parallel",))(o, proj, gw)


def _gout_bwd(o, proj, gw, dy, dproj, L):
    T = o.shape[2]
    N = T - L
    ob = pl.BlockSpec((2, None, T, HD), lambda h: (0, h, 0, 0))

    def body(o_ref, z_ref, gw_ref, dy_ref, _, do_ref, dz_ref, dgw_ref):
        _, vjp = jax.vjp(_gout_fn, o_ref[0, L:, :], _backward_latent(o_ref, L), z_ref[L:, :], gw_ref[...])
        g0, _, gz, ggw = vjp(dy_ref[...])
        do_ref[:L, :] = jnp.zeros((L, HD), F32)
        do_ref[L:, :] = g0
        dz_ref[:L, :] = jnp.zeros((L, HD), BF16)
        dz_ref[L:, :] = gz.astype(BF16)

        @pl.when(pl.program_id(0) == 0)
        def _():
            dgw_ref[...] = jnp.zeros_like(dgw_ref)

        dgw_ref[...] += ggw

    zb = pl.BlockSpec((T, HD), lambda h: (0, C_Z // HD + h))
    return _call(body, name="gout_bwd", out_shape=(_sds((GH, T, HD)), _sds(dproj.shape, BF16), _sds((1, HD))),
                 grid=(GH,),
                 in_specs=[ob, zb, pl.BlockSpec((1, HD), lambda h: (0, 0)), pl.BlockSpec((N, HD), lambda h: (0, h)), ANYSPEC],
                 out_specs=(pl.BlockSpec((None, T, HD), lambda h: (h, 0, 0)), zb, pl.BlockSpec((1, HD), lambda h: (0, 0))),
                 aliases={4: 1}, sem=("arbitrary",))(o, proj, gw, dy, dproj)


def _merge_fn(pa, pd, ga, gd):
    return jax.nn.sigmoid(ga) * pa + jax.nn.sigmoid(gd) * pd


def _merge_fwd(pa, pd, proj, L, *, br=256):
    N = pa.shape[0]
    lb = L // br
    row = pl.BlockSpec((br, D), lambda i: (i, 0))

    def body(pa_ref, pd_ref, ga_ref, gd_ref, y_ref):
        y_ref[...] = _merge_fn(pa_ref[...], pd_ref[...], ga_ref[...], gd_ref[...]).astype(BF16)

    return _call(body, name="merge_fwd", out_shape=_sds((N, D), BF16), grid=(N // br,),
                 in_specs=[row, row, pl.BlockSpec((br, D), lambda i: (i + lb, C_GATE // D)),
                           pl.BlockSpec((br, D), lambda i: (i + lb, C_GATE // D + 1))],
                 out_specs=row, sem=("parallel",))(pa, pd, proj, proj)


def _merge_bwd(pa, pd, proj, dy, L, *, br=256):
    N = pa.shape[0]
    T = N + L
    lb = L // br
    lrow = pl.BlockSpec((br, D), lambda i: (jnp.maximum(i - lb, 0), 0))

    def body(pa_ref, pd_ref, ga_ref, gd_ref, dy_ref, dpa_ref, dpd_ref, dg_ref):
        lat = pl.program_id(0) >= lb
        _, vjp = jax.vjp(_merge_fn, pa_ref[...], pd_ref[...], ga_ref[...], gd_ref[...])
        gpa, gpd, gga, ggd = vjp(dy_ref[...])
        dpa_ref[...] = gpa.astype(BF16)
        dpd_ref[...] = gpd.astype(BF16)
        dg_ref[:, :D] = jnp.where(lat, gga, 0.0).astype(BF16)
        dg_ref[:, D:] = jnp.where(lat, ggd, 0.0).astype(BF16)

    return _call(body, name="merge_bwd", out_shape=(_sds((N, D), BF16), _sds((N, D), BF16), _sds((T, C_END), BF16)),
                 grid=(T // br,),
                 in_specs=[lrow, lrow, pl.BlockSpec((br, D), lambda i: (i, C_GATE // D)),
                           pl.BlockSpec((br, D), lambda i: (i, C_GATE // D + 1)), lrow],
                 out_specs=(lrow, lrow, pl.BlockSpec((br, 2 * D), lambda i: (i, C_GATE // (2 * D)))),
                 sem=("arbitrary",))(pa, pd, proj, proj, dy)


def _resid_fwd(x, m, mod, i_g, *, name, br=256):
    R = x.shape[0]
    row = pl.BlockSpec((br, D), lambda i: (i, 0))

    def body(x_ref, m_ref, mod_ref, o_ref):
        o_ref[...] = x_ref[...] + mod_ref[i_g:i_g + 1, :] * m_ref[...]

    return _call(body, name=name, out_shape=_sds((R, D)), grid=(R // br,),
                 in_specs=[row, row, pl.BlockSpec((6, D), lambda i: (0, 0))], out_specs=row,
                 sem=("parallel",))(x, m, mod)


def _resid_bwd(dx, m, mod, i_g, *, name, br=256):
    R = dx.shape[0]
    row = pl.BlockSpec((br, D), lambda i: (i, 0))
    vec = pl.BlockSpec((1, D), lambda i: (0, 0))

    def body(dx_ref, m_ref, mod_ref, dm_ref, dg_ref):
        dxv = dx_ref[...]
        dm_ref[...] = (dxv * mod_ref[i_g:i_g + 1, :]).astype(BF16)

        @pl.when(pl.program_id(0) == 0)
        def _():
            dg_ref[...] = jnp.zeros_like(dg_ref)

        dg_ref[...] += jnp.sum(dxv * m_ref[...], axis=0, keepdims=True)

    return _call(body, name=name, out_shape=(_sds((R, D), BF16), _sds((1, D))), grid=(R // br,),
                 in_specs=[row, row, pl.BlockSpec((6, D), lambda i: (0, 0))], out_specs=(row, vec),
                 sem=("arbitrary",))(dx, m, mod)


def _ffn_fn(shifts, ug, uv, wg, wv, bg, bv):
    down, up = shifts

    def conv(x, w, b):
        return down(x) * w[0:1, :] + x * w[1:2, :] + up(x) * w[2:3, :] + b

    return _silu(conv(ug, wg, bg)) * conv(uv, wv, bv)


def _ffn_fwd(up, cw, cb, *, bw=256):
    N = up.shape[0]
    shifts = _make_shift(((0, N),))
    nb = DFF // bw

    def body(ug, uv, wg, wv, bg, bv, a_ref):
        a_ref[...] = _ffn_fn(shifts, ug[...], uv[...], wg[...], wv[...], bg[...], bv[...]).astype(BF16)

    def col(rows, off):
        return pl.BlockSpec((rows, bw), lambda j: (0, j + off))

    return _call(body, name="ffn_fwd", out_shape=_sds((N, DFF), BF16), grid=(nb,),
                 in_specs=[col(N, 0), col(N, nb), col(3, 0), col(3, nb), col(1, 0), col(1, nb)],
                 out_specs=col(N, 0), sem=("parallel",), vmem=VMEM_BIG)(up, up, cw, cw, cb, cb)


def _ffn_bwd(up, cw, cb, da, *, bw=256):
    N = up.shape[0]
    shifts = _make_shift(((0, N),))
    nb = DFF // bw

    def body(ug, uv, wg, wv, bg, bv, da_ref, dug, duv, dwg, dwv, dbg, dbv):
        _, vjp = jax.vjp(functools.partial(_ffn_fn, shifts), ug[...], uv[...], wg[...], wv[...], bg[...], bv[...])
        g = vjp(da_ref[...])
        dug[...] = g[0].astype(BF16)
        duv[...] = g[1].astype(BF16)
        dwg[...], dwv[...], dbg[...], dbv[...] = g[2], g[3], g[4], g[5]

    def col(rows, off):
        return pl.BlockSpec((rows, bw), lambda j: (0, j + off))

    half = (_sds((N, DFF), BF16), _sds((N, DFF), BF16), _sds((3, DFF)), _sds((3, DFF)), _sds((1, DFF)), _sds((1, DFF)))
    dug, duv, dwg, dwv, dbg, dbv = _call(
        body, name="ffn_bwd", out_shape=half, grid=(nb,),
        in_specs=[col(N, 0), col(N, nb), col(3, 0), col(3, nb), col(1, 0), col(1, nb), col(N, 0)],
        out_specs=(col(N, 0), col(N, 0), col(3, 0), col(3, 0), col(1, 0), col(1, 0)),
        sem=("parallel",), vmem=VMEM_BIG)(up, up, cw, cw, cb, cb, da)
    return (jnp.concatenate([dug, duv], axis=1), jnp.concatenate([dwg, dwv], axis=1),
            jnp.concatenate([dbg, dbv], axis=1))


def _head_fn(x1, dn, g2, fw, tgt):
    y = _rms(x1 + g2 * dn) * fw
    err = y - tgt
    return 0.5 * jnp.sum(jnp.mean(err * err, axis=-1))


def _head(x1, dn, mod, fw, tgt, *, br=256):
    N = x1.shape[0]
    row = pl.BlockSpec((br, D), lambda i: (i, 0))
    vec = pl.BlockSpec((1, D), lambda i: (0, 0))
    one = pl.BlockSpec((1, HD), lambda i: (0, 0))

    def body(x1_ref, dn_ref, mod_ref, fw_ref, tgt_ref, loss_ref, dx_ref, ddn_ref, dg_ref, dfw_ref):
        loss, (gx, gdn, gg, gfw) = jax.value_and_grad(_head_fn, argnums=(0, 1, 2, 3))(
            x1_ref[...], dn_ref[...], mod_ref[5:6, :], fw_ref[...], tgt_ref[...])
        dx_ref[...] = gx
        ddn_ref[...] = gdn.astype(BF16)

        @pl.when(pl.program_id(0) == 0)
        def _():
            loss_ref[...] = jnp.zeros_like(loss_ref)
            dg_ref[...] = jnp.zeros_like(dg_ref)
            dfw_ref[...] = jnp.zeros_like(dfw_ref)

        loss_ref[...] += jnp.broadcast_to(loss, (1, HD))
        dg_ref[...] += gg
        dfw_ref[...] += gfw

    return _call(body, name="head", out_shape=(_sds((1, HD)), _sds((N, D)), _sds((N, D), BF16), _sds((1, D)), _sds((1, D))),
                 grid=(N // br,), in_specs=[row, row, pl.BlockSpec((6, D), lambda i: (0, 0)), vec, row],
                 out_specs=(one, row, row, vec, vec), sem=("arbitrary",))(x1, dn, mod, fw, tgt)


def _adamw(w, g, m, v, *, name):
    shape = w.shape
    cols = shape[-1]
    rows = max(1, math.prod(shape[:-1]))
    w2, g2, m2, v2 = (t.reshape(rows, cols) for t in (w, g, m, v))
    br = 256 if rows % 256 == 0 else rows
    c1 = 1.0 - B1 ** STEP
    c2 = 1.0 - B2 ** STEP

    def body(w_ref, g_ref, m_ref, v_ref, d_ref, nm_ref, nv_ref):
        gv = g_ref[...]
        nm = B1 * m_ref[...] + (1.0 - B1) * gv
        nv = B2 * v_ref[...] + (1.0 - B2) * (gv * gv)
        d_ref[...] = -LR * ((nm / c1) / (jnp.sqrt(nv / c2) + AEPS) + WD * w_ref[...])
        nm_ref[...] = nm
        nv_ref[...] = nv

    blk = pl.BlockSpec((br, cols), lambda i: (i, 0))
    outs = _call(body, name=name, out_shape=(_sds((rows, cols)),) * 3, grid=(rows // br,),
                 in_specs=[blk] * 4, out_specs=(blk,) * 3, sem=("parallel",))(w2, g2, m2, v2)
    return tuple(t.reshape(shape) for t in outs)


def _adamw_many(items, *, name):
    k = len(items)
    shapes = [w.shape for w, _, _, _ in items]
    flat = [t.reshape(max(1, math.prod(t.shape[:-1])), t.shape[-1]) for it in items for t in it]
    c1 = 1.0 - B1 ** STEP
    c2 = 1.0 - B2 ** STEP

    def body(*refs):
        ins, outs = refs[:4 * k], refs[4 * k:]
        for i in range(k):
            w_ref, g_ref, m_ref, v_ref = ins[4 * i:4 * i + 4]
            gv = g_ref[...]
            nm = B1 * m_ref[...] + (1.0 - B1) * gv
            nv = B2 * v_ref[...] + (1.0 - B2) * (gv * gv)
            outs[3 * i][...] = -LR * ((nm / c1) / (jnp.sqrt(nv / c2) + AEPS) + WD * w_ref[...])
            outs[3 * i + 1][...] = nm
            outs[3 * i + 2][...] = nv

    res = _call(body, name=name, out_shape=tuple(_sds(flat[4 * i].shape) for i in range(k) for _ in range(3)))(*flat)
    return [tuple(res[3 * i + j].reshape(shapes[i]) for j in range(3)) for i in range(k)]


def _rope_tables(N, L):
    t = jnp.arange(N)
    pos = jnp.stack([(t // GRID_W).astype(F32), (t % GRID_W).astype(F32)], axis=1)
    inv = ROPE_THETA ** (-jnp.arange(0, HD // 2, 2, dtype=F32) / (HD // 2))
    ang = pos[:, :, None] * inv[None, None, :]
    cos = jnp.broadcast_to(jnp.cos(ang)[:, :, None, :], (N, 2, 2, HD // 4)).reshape(N, HD)
    sin = jnp.broadcast_to(jnp.sin(ang)[:, :, None, :], (N, 2, 2, HD // 4))
    sin = (sin * jnp.array([-1.0, 1.0], F32)[None, None, :, None]).reshape(N, HD)
    cos = jnp.concatenate([jnp.ones((L, HD), F32), cos], axis=0)
    sin = jnp.concatenate([jnp.zeros((L, HD), F32), sin], axis=0)
    return cos, sin


def _pad_lanes(v, off=0):
    return jnp.zeros((1, HD), F32).at[0, off:off + v.shape[0]].set(v)


def _local_step(x, ctx, tgt, mod_lat, mod_ctx, w_in, shards, small):
    N, L = x.shape[0], ctx.shape[0]
    T = N + L
    bounds = ((0, L), (L, T))
    qw, kw, gw = small["q_norm_w"], small["k_norm_w"], small["gdn_norm_w"]
    conv_w, ffn_w, ffn_b, fnw = small["conv_qkv_w"], small["ffn_conv_w"], small["ffn_conv_b"], small["final_norm_w"]
    alog = _pad_lanes(small["a_log"].reshape(-1), 2 * GH)
    dtb = _pad_lanes(small["dt_bias"].reshape(-1), 2 * GH)
    cos, sin = _rope_tables(N, L)
    bt = T
    bnl = 256 if N % 1024 else 1024

    hc = _normmod_fwd(ctx, mod_ctx, 0, 1, name="normmod_ctx")
    hx = _normmod_fwd(x, mod_lat, 0, 1, name="normmod_x")
    h1 = jnp.concatenate([hc, hx], axis=0)
    proj = _mm(h1, w_in, name="mm_in", M=T, N=C_END, K=D, tb=True, bm=bt, bn=1024)
    aq, ak, av = _aprep_fwd(proj, cos, sin, qw, kw)
    (attn, attn32, lse), (up_g,) = _attn_fwd(aq, ak, av, L, _GatherTwoLevel([shards["w_up"]]))
    gq = _gprep_fwd(proj, conv_w, 0, bounds)
    gk = _gprep_fwd(proj, conv_w, 1, bounds)
    gv = _gprep_fwd(proj, conv_w, 2, bounds)
    bl = _bl_fwd(proj, alog, dtb)
    intra, (down_g, pa_g, pd_g, out_g) = _intra_fwd(
        gq, gk, gv, bl, L, _GatherTwoLevel([shards[n] for n in ("w_down", "w_pa", "w_pd", "w_out")]))
    w_up, w_down = up_g.reshape(2 * DFF, D), down_g.reshape(DFF, D)
    w_pa, w_pd, w_out = pa_g.reshape(D, D), pd_g.reshape(D, D), out_g.reshape(D, D)
    xinv, intra = intra[6], intra[:6]
    o, states = _scan_fwd(*intra, L)
    gdn = _gout_fwd(o, proj, gw, L)
    pa = _mm(attn, w_pa, name="mm_pa", M=N, N=D, K=D, bm=bnl)
    pd = _mm(gdn, w_pd, name="mm_pd", M=N, N=D, K=D, bm=bnl)
    y = _merge_fwd(pa, pd, proj, L)
    m = _mm(y, w_out, name="mm_out", M=N, N=D, K=D, bm=bnl)
    x1 = _resid_fwd(x, m, mod_lat, 2, name="resid1")
    h2 = _normmod_fwd(x1, mod_lat, 3, 4, name="normmod_x1")
    up = _mm(h2, w_up, name="mm_up", M=N, N=2 * DFF, K=D, tb=True, bm=bnl, bn=2 * DFF // 4)
    a = _ffn_fwd(up, ffn_w, ffn_b)
    dn = _mm(a, w_down, name="mm_down", M=N, N=D, K=DFF, bm=bnl)
    loss, dx2, ddn, dg2, dfnw = _head(x1, dn, mod_lat, fnw, tgt)

    da = _mm(ddn, w_down, name="mm_down_dx", M=N, N=DFF, K=D, tb=True, bm=bnl, bn=DFF // 2)
    g_down = _mm(a, ddn, name="mm_down_dw", M=DFF, N=D, K=N, ta=True, bm=DFF // 2, out_dtype=BF16)
    dup, d_ffn_w, d_ffn_b = _ffn_bwd(up, ffn_w, ffn_b, da)
    dh2 = _mm(dup, w_up, name="mm_up_dx", M=N, N=D, K=2 * DFF, bm=bnl, bk=2 * DFF // 4)
    g_up = _mm(dup, h2, name="mm_up_dw", M=2 * DFF, N=D, K=N, ta=True, bm=2 * DFF // 4, out_dtype=BF16)
    dx1, dsh2, dsc2 = _normmod_bwd(x1, mod_lat, 3, 4, dh2, 0, dx2, name="normmod_x1_bwd")
    dm, dg1 = _resid_bwd(dx1, m, mod_lat, 2, name="resid1_bwd")
    dy = _mm(dm, w_out, name="mm_out_dx", M=N, N=D, K=D, tb=True, bm=bnl)
    g_out = _mm(y, dm, name="mm_out_dw", M=D, N=D, K=N, ta=True, out_dtype=BF16)
    dpa, dpd, dproj = _merge_bwd(pa, pd, proj, dy, L)
    dattn = _mm(dpa, w_pa, name="mm_pa_dx", M=N, N=D, K=D, tb=True, bm=bnl)
    g_pa = _mm(attn, dpa, name="mm_pa_dw", M=D, N=D, K=N, ta=True, out_dtype=BF16)
    dgdn = _mm(dpd, w_pd, name="mm_pd_dx", M=N, N=D, K=D, tb=True, bm=bnl)
    g_pd = _mm(gdn, dpd, name="mm_pd_dw", M=D, N=D, K=N, ta=True, out_dtype=BF16)
    do, dproj, dgw = _gout_bwd(o, proj, gw, dgdn, dproj, L)
    cts, recv_a = _scan_bwd(*intra, states, do, L, _Exchange([g_out.reshape(NDEV, D // NDEV, D)], True))
    (dgq, dgk, dgv, dbl), recv_b = _intra_bwd(gq, gk, gv, bl, xinv, cts, L, _Exchange(
        [g_pa.reshape(NDEV, D // NDEV, D), g_pd.reshape(NDEV, D // NDEV, D), g_up.reshape(NDEV, 2 * DFF // NDEV, D)], True))
    dproj, dwq = _gprep_bwd(proj, conv_w, 0, bounds, dgq, dproj)
    dproj, dwk = _gprep_bwd(proj, conv_w, 1, bounds, dgk, dproj)
    dproj, dwv = _gprep_bwd(proj, conv_w, 2, bounds, dgv, dproj)
    dproj, dalog, ddtb = _bl_bwd(proj, alog, dtb, dbl, dproj)
    (daq_h, dak_h, dav_h), recv_c = _attn_bwd(aq, ak, av, attn32, lse, dattn, L, _Exchange(
        [g_down.reshape(NDEV, DFF // NDEV, D)], True))
    recv = dict(zip(("w_out", "w_pa", "w_pd", "w_up", "w_down"), recv_a + recv_b + recv_c))
    dproj, dqw, dkw = _aprep_bwd(proj, cos, sin, qw, kw, daq_h, dak_h, dav_h, dproj, L)
    g_in = _mm(dproj, h1, name="mm_in_dw", M=C_END, N=D, K=T, ta=True, bm=1024, out_dtype=BF16)
    g_in = _unpad_columns(g_in).reshape(NDEV, W_END // NDEV, D)
    own_in = lax.dynamic_index_in_dim(g_in, _position()[3], axis=0, keepdims=False)
    *pending, token = _scatter_start(g_in, None, (0, D // 2), (), name="scatter_g_in_a_start")
    dh1 = _mm(dproj, w_in, name="mm_in_dx", M=T, N=D, K=C_END, bm=bt, bk=1024, after=(token,))
    grad_x, dsh1, dsc1 = _normmod_bwd(x, mod_lat, 0, 1, dh1, L, dx1, name="normmod_x_bwd")
    _, dcsh1, dcsc1 = _normmod_bwd(ctx, mod_ctx, 0, 1, dh1, 0, None, name="normmod_ctx_bwd")

    z1 = jnp.zeros((1, D), F32)
    dmod_lat = jnp.concatenate([dsh1, dsc1, dg1, dsh2, dsc2, dg2], axis=0)
    dmod_ctx = jnp.concatenate([dcsh1, dcsc1, z1, z1, z1, z1], axis=0)
    gsmall = {
        "q_norm_w": dqw, "k_norm_w": dkw, "gdn_norm_w": dgw,
        "conv_qkv_w": jnp.concatenate([dwq, dwk, dwv], axis=1),
        "a_log": dalog[0, 2 * GH:4 * GH], "dt_bias": ddtb[0, 2 * GH:4 * GH],
        "ffn_conv_w": d_ffn_w, "ffn_conv_b": d_ffn_b, "final_norm_w": dfnw,
    }
    return loss[0, 0], grad_x, (pending, own_in), recv, dmod_lat, dmod_ctx, gsmall


HBM = pl.BlockSpec(memory_space=pltpu.HBM)
ANYSPEC = pl.BlockSpec(memory_space=pl.ANY)


def _position():
    x, y, c = lax.axis_index("x"), lax.axis_index("y"), lax.axis_index("c")
    return x, y, c, 4 * x + 2 * y + c


def _peer(x, y, c, k):
    px = 1 - x if k & 4 else x
    py = 1 - y if k & 2 else y
    pc = 1 - c if k & 1 else c
    return (px, py, pc), 4 * px + 2 * py + pc


def _exchange(arrs, *, name, scatter):
    exch = _Exchange(arrs, scatter)
    n = exch.n

    def body(*refs):
        ins, outs, sems = refs[:n], refs[n:2 * n], refs[2 * n:]
        exch.start(ins, outs, sems)
        exch.finish(ins, outs, sems)

    outs = pl.pallas_call(body, name=name, out_shape=exch.out_shape, in_specs=[HBM] * n, out_specs=(HBM,) * n,
                          scratch_shapes=exch.scratch,
                          compiler_params=pltpu.CompilerParams(has_side_effects=True))(*arrs)
    return list(outs)


class _Exchange:
    def __init__(self, arrs, scatter):
        self.arrs, self.scatter, self.n = list(arrs), scatter, len(arrs)
        self.out_shape = tuple(_sds(a.shape if scatter else (NDEV,) + a.shape, a.dtype) for a in arrs)
        self.scratch = [pltpu.SemaphoreType.DMA((self.n, NDEV - 1)), pltpu.SemaphoreType.DMA((self.n, NDEV - 1)),
                        pltpu.SemaphoreType.DMA((self.n,))]

    def _copies(self, ins, outs, sems):
        send, recv, loc = sems
        x, y, c, me = _position()
        local = [pltpu.make_async_copy(ins[a].at[me] if self.scatter else ins[a], outs[a].at[me], loc.at[a])
                 for a in range(self.n)]
        remote = []
        for k in range(1, NDEV):
            peer, pid = _peer(x, y, c, k)
            for a in range(self.n):
                src = ins[a].at[pid] if self.scatter else ins[a]
                remote.append(pltpu.make_async_remote_copy(
                    src_ref=src, dst_ref=outs[a].at[me], send_sem=send.at[a, k - 1], recv_sem=recv.at[a, k - 1],
                    device_id=peer, device_id_type=MESH))
        return local, remote

    def start(self, ins, outs, sems):
        local, remote = self._copies(ins, outs, sems)
        for cp in local + remote:
            cp.start()

    def finish(self, ins, outs, sems):
        local, remote = self._copies(ins, outs, sems)
        for cp in remote:
            cp.wait()
        for cp in local:
            cp.wait()


class _GatherTwoLevel:
    scatter = False

    def __init__(self, arrs):
        self.arrs, self.n = list(arrs), len(arrs)
        self.out_shape = tuple(_sds((NDEV,) + a.shape, a.dtype) for a in arrs)
        self.scratch = [pltpu.SemaphoreType.DMA((self.n, NDEV - 1)), pltpu.SemaphoreType.DMA((self.n, NDEV - 1)),
                        pltpu.SemaphoreType.DMA((self.n,))]

    def _parts(self, ins, outs, sems):
        send, recv, loc = sems
        x, y, c, _ = _position()
        me, sibling = (x, y, c), (x, y, 1 - c)
        chips = [(1 - x, y), (x, 1 - y), (1 - x, 1 - y)]
        parts = []
        for a in range(self.n):
            slot = lambda px, py, pc, a=a: outs[a].at[4 * px + 2 * py + pc]

            def copy(k, owner, to, src=None, a=a, slot=slot):
                return pltpu.make_async_remote_copy(
                    src_ref=slot(*owner) if src is None else src, dst_ref=slot(*owner), send_sem=send.at[a, k],
                    recv_sem=recv.at[a, k], device_id=to, device_id_type=MESH)

            parts.append(dict(
                mine=pltpu.make_async_copy(ins[a], slot(*me), loc.at[a]),
                first=[copy(0, me, sibling, src=ins[a])] + [copy(1 + j, me, (*ch, c), src=ins[a]) for j, ch in enumerate(chips)],
                arrive=[copy(1 + j, (*ch, c), me) for j, ch in enumerate(chips)],
                passed=[copy(4 + j, (*ch, c), sibling) for j, ch in enumerate(chips)],
                rest=[copy(0, sibling, me)] + [copy(4 + j, (*ch, 1 - c), me) for j, ch in enumerate(chips)]))
        return parts

    def start(self, ins, outs, sems):
        for p in self._parts(ins, outs, sems):
            p["mine"].start()
            for cp in p["first"]:
                cp.start()

    def middle(self, ins, outs, sems):
        for p in self._parts(ins, outs, sems):
            for got, fwd in zip(p["arrive"], p["passed"]):
                got.wait_recv()
                fwd.start()

    def finish(self, ins, outs, sems):
        for p in self._parts(ins, outs, sems):
            for cp in p["rest"]:
                cp.wait_recv()
            for cp in p["first"] + p["passed"]:
                cp.wait_send()
            p["mine"].wait()


def _gather_two_level(blocks, *, name):
    exch = _GatherTwoLevel(blocks)
    n = exch.n

    def body(*refs):
        ins, outs, sems = refs[:n], refs[n:2 * n], refs[2 * n:]
        exch.start(ins, outs, sems)
        exch.middle(ins, outs, sems)
        exch.finish(ins, outs, sems)

    outs = pl.pallas_call(body, name=name, out_shape=exch.out_shape, in_specs=[HBM] * n, out_specs=(HBM,) * n,
                          scratch_shapes=exch.scratch,
                          compiler_params=pltpu.CompilerParams(has_side_effects=True))(*blocks)
    return list(outs)


SEM = pl.BlockSpec(memory_space=pltpu.SEMAPHORE)


def _scatter_copies(src_ref, land_ref, send_sems, recv_sems, cols):
    x, y, c, me = _position()
    span = (slice(None), pl.ds(*cols))
    copies = []
    for k in range(1, NDEV):
        peer, pid = _peer(x, y, c, k)
        copies.append(pltpu.make_async_remote_copy(
            src_ref=src_ref.at[pid].at[span], dst_ref=land_ref.at[me].at[span], send_sem=send_sems.at[k - 1],
            recv_sem=recv_sems.at[k - 1], device_id=peer, device_id_type=MESH))
    return copies


SPLIT_EFFECT = pltpu.SideEffectType.DATAFLOW_SIDE_EFFECTING


def _scatter_start(parts, land, cols, after, *, name):
    na = len(after)
    if land is None:
        land = lax.empty(parts.shape, parts.dtype)

    def body(src_ref, land_ref, *rest):
        send_sems, recv_sems, _, _, token = rest[na:]
        for cp in _scatter_copies(src_ref, land_ref, send_sems, recv_sems, cols):
            cp.start()
        token[...] = jnp.zeros_like(token)

    return pl.pallas_call(
        body, name=name,
        out_shape=(pltpu.SemaphoreType.DMA((NDEV - 1,)), pltpu.SemaphoreType.DMA((NDEV - 1,)),
                   pltpu.HBM(parts.shape, parts.dtype), pltpu.HBM(parts.shape, parts.dtype), _sds((8, HD))),
        in_specs=(HBM, HBM) + (pl.BlockSpec(memory_space=pl.ANY),) * na,
        out_specs=(SEM, SEM, HBM, HBM, pl.BlockSpec(memory_space=pltpu.VMEM)),
        input_output_aliases={0: 2, 1: 3}, compiler_params=pltpu.CompilerParams(has_side_effects=SPLIT_EFFECT),
    )(pltpu.with_memory_space_constraint(parts, pltpu.HBM), pltpu.with_memory_space_constraint(land, pltpu.HBM), *after)


def _scatter_wait(send_sems, recv_sems, src_thru, land_thru, cols, after, *, name):
    na = len(after)

    def body(src_ref, land_ref, send_sems, recv_sems, *rest):
        for cp in _scatter_copies(src_ref, land_ref, send_sems, recv_sems, cols):
            cp.wait_send()
            cp.wait_recv()

    return pl.pallas_call(
        body, name=name,
        out_shape=(pltpu.HBM(src_thru.shape, src_thru.dtype), pltpu.HBM(land_thru.shape, land_thru.dtype)),
        in_specs=(HBM, HBM, SEM, SEM) + (pl.BlockSpec(memory_space=pl.ANY),) * na, out_specs=(HBM, HBM),
        input_output_aliases={0: 0, 1: 1}, compiler_params=pltpu.CompilerParams(has_side_effects=SPLIT_EFFECT),
    )(src_thru, land_thru, send_sems, recv_sems, *after)


def _cast_bf16(w, *, name):
    rows, cols = w.shape
    br = 128 if rows % 128 == 0 else rows

    def body(w_ref, o_ref):
        o_ref[...] = w_ref[...].astype(BF16)

    blk = pl.BlockSpec((br, cols), lambda i: (i, 0))
    return _call(body, name=name, out_shape=_sds((rows, cols), BF16), grid=(rows // br,), in_specs=[blk],
                 out_specs=blk, sem=("parallel",))(w)


def _sum_slots(a, *, name):
    _, R, C = a.shape

    def body(a_ref, o_ref):
        s = a_ref[0]
        for d in range(1, NDEV):
            s = s + a_ref[d]
        o_ref[...] = s

    return _call(body, name=name, out_shape=_sds((R, C)))(a)


MODROWS = 16


def _mod_fwd(c9, w, b):
    cols = w.shape[1]

    def body(c_ref, w_ref, b_ref, o_ref):
        o_ref[...] = _nn(_silu(c_ref[...]), w_ref[...]) + b_ref[...]

    return _call(body, name="mod_fwd", out_shape=_sds((MODROWS, cols)))(c9, w, b)


def _mod_bwd(c9, dmy, dall, w):
    cols = w.shape[1]

    def body(c_ref, dmy_ref, dall_ref, w_ref, gw_ref, gb_ref, cp_ref):
        sc = _silu(c_ref[...])
        rows = lax.broadcasted_iota(jnp.int32, (MODROWS, 1), 0)
        d = dmy_ref[...]
        d_ctx = jnp.where(rows == NDEV, d, 0.0)
        sc_ctx = jnp.where(rows == NDEV, sc, 0.0)
        outer = lax.dot_general(sc_ctx, d_ctx, (((0,), (0,)), ((), ())), precision=HI, preferred_element_type=F32)
        gw_ref[...] = _tn(jnp.where(rows < NDEV, sc, 0.0), jnp.where(rows < NDEV, d, 0.0)) + outer
        gb_ref[...] = jnp.sum(dall_ref[...], axis=0, keepdims=True)
        cp_ref[...] = jnp.sum(_nt(d_ctx, w_ref[...]), axis=0, keepdims=True)

    return _call(body, name="mod_bwd", out_shape=(_sds((D, cols)), _sds((1, 6 * D)), _sds((1, D))),
                 vmem=VMEM_BIG)(c9, dmy, dall, w)


def _cctx_finish(parts, c_ctx, after):
    VM = pl.BlockSpec(memory_space=pltpu.VMEM)

    def body(p_ref, c_ref, *rest):
        o_ref = rest[-1]
        s = p_ref[0]
        for d in range(1, NDEV):
            s = s + p_ref[d]
        _, vjp = jax.vjp(_silu, c_ref[...])
        o_ref[...] = vjp(s)[0]

    return _call(body, name="cctx_finish", out_shape=_sds((1, D)),
                 in_specs=[VM, VM] + [pl.BlockSpec(memory_space=pl.ANY)] * len(after))(parts, c_ctx, *after)


def _adamw_recv(w, recv, m, v, *, name, own=None):
    rows, cols = w.shape
    bc = 256
    c1 = 1.0 - B1 ** STEP
    c2 = 1.0 - B2 ** STEP
    has_own = own is not None

    def body(w_ref, r_ref, m_ref, v_ref, *rest):
        g_ref, d_ref, nm_ref, nv_ref = rest[-4:]
        me = _position()[3]

        def slot(d):
            return jnp.where(me == d, rest[0][...], r_ref[d]) if has_own else r_ref[d]

        gv = slot(0).astype(F32)
        for d in range(1, NDEV):
            gv = gv + slot(d).astype(F32)
        nm = B1 * m_ref[...] + (1.0 - B1) * gv
        nv = B2 * v_ref[...] + (1.0 - B2) * (gv * gv)
        g_ref[...] = gv
        d_ref[...] = -LR * ((nm / c1) / (jnp.sqrt(nv / c2) + AEPS) + WD * w_ref[...])
        nm_ref[...] = nm
        nv_ref[...] = nv

    blk = pl.BlockSpec((rows, bc), lambda j: (0, j))
    return _call(body, name=name, out_shape=(_sds((rows, cols)),) * 4, grid=(cols // bc,),
                 in_specs=[blk, pl.BlockSpec((NDEV, rows, bc), lambda j: (0, 0, j)), blk, blk] + [blk] * has_own,
                 out_specs=(blk,) * 4, sem=("parallel",), vmem=VMEM_BIG)(w, recv, m, v, *([own] if has_own else []))


P_LAT, P_CTX, P_FNW, P_FFNB, P_CONV, P_FFNW, P_MISC, P_ROWS = 0, 8, 16, 24, 32, 48, 72, 80


def _rows_of(v, nrows):
    flat = v.reshape(-1)
    return jnp.pad(flat, (0, nrows * D - flat.shape[0])).reshape(nrows, D)


def _by_columns(g):
    n, r, c = g.shape
    return jnp.transpose(g, (1, 0, 2)).reshape(r, n * c)


def kernel(x, c, ctx, c_ctx, w_mod, b_mod, w_in, q_norm_w, k_norm_w, conv_qkv_w, a_log, dt_bias, gdn_norm_w, w_pa, w_pd, w_out, w_up, ffn_conv_w, ffn_conv_b, w_down, final_norm_w, loss_target, m_c_ctx, m_w_mod, m_b_mod, m_w_in, m_q_norm_w, m_k_norm_w, m_conv_qkv_w, m_a_log, m_dt_bias, m_gdn_norm_w, m_w_pa, m_w_pd, m_w_out, m_w_up, m_ffn_conv_w, m_ffn_conv_b, m_w_down, m_final_norm_w, v_c_ctx, v_w_mod, v_b_mod, v_w_in, v_q_norm_w, v_k_norm_w, v_conv_qkv_w, v_a_log, v_dt_bias, v_gdn_norm_w, v_w_pa, v_w_pd, v_w_out, v_w_up, v_ffn_conv_w, v_ffn_conv_b, v_w_down, v_final_norm_w):
    _, _, _, me = _position()
    mcols = w_mod.shape[2]

    transposed = ("w_in", "w_up")
    big = {"w_in": w_in[0].T, "w_pa": w_pa[0], "w_pd": w_pd[0], "w_out": w_out[0], "w_up": w_up[0].T, "w_down": w_down[0]}
    names = list(big)
    shards = {n: _cast_bf16(big[n], name="cast_" + n) for n in names}
    w_in_g, c_all, conv_g, ffnw_g = _gather_two_level([shards["w_in"], c, conv_qkv_w[0], ffn_conv_w[0]],
                                                      name="gather_w_in")
    w_in_full = w_in_g.reshape(W_END, D)
    w_in_pad = _pad_columns(w_in_full)

    c9 = jnp.concatenate([c_all.reshape(NDEV, D), jnp.pad(c_ctx[None], ((0, MODROWS - NDEV - 1), (0, 0)))], axis=0)
    b_loc = lax.dynamic_slice(b_mod, (0, me * mcols), (1, mcols))
    mod_all, = _exchange([_mod_fwd(c9, w_mod[0], b_loc)], name="gather_mod", scatter=False)
    mod_lat = lax.dynamic_index_in_dim(mod_all, me, axis=1, keepdims=False).reshape(6, D)
    mod_ctx = mod_all[:, NDEV, :].reshape(6, D)

    small = {"q_norm_w": q_norm_w, "k_norm_w": k_norm_w, "gdn_norm_w": gdn_norm_w, "a_log": a_log, "dt_bias": dt_bias,
             "conv_qkv_w": _by_columns(conv_g), "ffn_conv_w": _by_columns(ffnw_g), "ffn_conv_b": ffn_conv_b,
             "final_norm_w": final_norm_w[None]}
    loss_me, grad_x, (pending_in, own_in), recv, dmod_lat, dmod_ctx, gs = _local_step(
        x[0], ctx[0], loss_target[0], mod_lat, mod_ctx, w_in_pad, shards, small)

    moments = {"w_in": (m_w_in, v_w_in), "w_pa": (m_w_pa, v_w_pa), "w_pd": (m_w_pd, v_w_pd),
               "w_out": (m_w_out, v_w_out), "w_up": (m_w_up, v_w_up), "w_down": (m_w_down, v_w_down)}
    res = {}
    def finish(n, outs):
        return tuple((t.T if n in transposed else t)[None] for t in outs)

    def moment(t, n):
        return t[0].T if n in transposed else t[0]

    for n in recv:
        res[n] = finish(n, _adamw_recv(big[n], recv[n], moment(moments[n][0], n), moment(moments[n][1], n),
                                       name="adamw_" + n))

    misc = jnp.concatenate([gs["q_norm_w"][0], gs["k_norm_w"][0], gs["gdn_norm_w"][0], gs["a_log"], gs["dt_bias"],
                            loss_me[None]])
    pack = jnp.concatenate([_rows_of(dmod_lat, P_CTX - P_LAT), _rows_of(dmod_ctx, P_FNW - P_CTX),
                            _rows_of(gs["final_norm_w"], P_FFNB - P_FNW), _rows_of(gs["ffn_conv_b"], P_CONV - P_FFNB),
                            _rows_of(gs["conv_qkv_w"], P_FFNW - P_CONV), _rows_of(gs["ffn_conv_w"], P_MISC - P_FFNW),
                            _rows_of(misc, P_ROWS - P_MISC)], axis=0)
    pack_all, = _exchange([pack], name="gather_pack", scatter=False)
    tot = _sum_slots(pack_all, name="sum_pack")
    dall = jnp.concatenate([pack_all[:, P_LAT:P_LAT + 6, :].reshape(NDEV, 6 * D),
                            jnp.pad(tot[P_CTX:P_CTX + 6].reshape(1, 6 * D), ((0, MODROWS - NDEV - 1), (0, 0)))], axis=0)
    dmy = lax.dynamic_slice(dall, (0, me * mcols), (MODROWS, mcols))
    g_w_mod, g_b_mod, cpart = _mod_bwd(c9, dmy, dall, w_mod[0])
    cparts, = _exchange([cpart], name="gather_cctx", scatter=False)
    sems_a, land = pending_in[:2], pending_in[3]
    *sems_b, g_in_thru, land, token_b = _scatter_start(pending_in[2], land, (D // 2, D // 2), (cparts,),
                                                       name="scatter_g_in_b_start")
    g_c_ctx = _cctx_finish(cparts, c_ctx[None], (token_b,))[0]

    nconv, nffn = 3 * GH * HD, 2 * DFF
    conv_tot = tot[P_CONV:P_FFNW].reshape(-1)[:3 * nconv].reshape(3, nconv)
    ffnw_tot = tot[P_FFNW:P_MISC].reshape(-1)[:3 * nffn].reshape(3, nffn)
    mrow = tot[P_MISC]
    grads = {
        "c_ctx": g_c_ctx, "w_mod": g_w_mod[None], "b_mod": g_b_mod,
        "q_norm_w": mrow[None, 0:HD], "k_norm_w": mrow[None, HD:2 * HD], "gdn_norm_w": mrow[None, 2 * HD:3 * HD],
        "conv_qkv_w": lax.dynamic_slice(conv_tot, (0, me * (nconv // NDEV)), (3, nconv // NDEV))[None],
        "a_log": mrow[3 * HD:3 * HD + 2 * GH].reshape(1, 2, GH),
        "dt_bias": mrow[3 * HD + 2 * GH:3 * HD + 4 * GH].reshape(1, 2, GH),
        "ffn_conv_w": lax.dynamic_slice(ffnw_tot, (0, me * (nffn // NDEV)), (3, nffn // NDEV))[None],
        "ffn_conv_b": tot[P_FFNB:P_CONV].reshape(-1)[:nffn][None],
        "final_norm_w": tot[P_FNW],
    }
    loss = mrow[3 * HD + 4 * GH]
    given = {"c_ctx": (c_ctx, m_c_ctx, v_c_ctx), "w_mod": (w_mod, m_w_mod, v_w_mod), "b_mod": (b_mod, m_b_mod, v_b_mod),
             "q_norm_w": (q_norm_w, m_q_norm_w, v_q_norm_w), "k_norm_w": (k_norm_w, m_k_norm_w, v_k_norm_w),
             "conv_qkv_w": (conv_qkv_w, m_conv_qkv_w, v_conv_qkv_w), "a_log": (a_log, m_a_log, v_a_log),
             "dt_bias": (dt_bias, m_dt_bias, v_dt_bias), "gdn_norm_w": (gdn_norm_w, m_gdn_norm_w, v_gdn_norm_w),
             "ffn_conv_w": (ffn_conv_w, m_ffn_conv_w, v_ffn_conv_w), "ffn_conv_b": (ffn_conv_b, m_ffn_conv_b, v_ffn_conv_b),
             "final_norm_w": (final_norm_w, m_final_norm_w, v_final_norm_w)}
    res["w_mod"] = (grads["w_mod"],) + _adamw(w_mod, grads["w_mod"], m_w_mod, v_w_mod, name="adamw_w_mod")
    small_names = [n for n in given if n != "w_mod"]
    updates = _adamw_many([(given[n][0], grads[n], given[n][1], given[n][2]) for n in small_names], name="adamw_small")
    for n, upd in zip(small_names, updates):
        res[n] = (grads[n],) + upd

    g_in_thru, land = _scatter_wait(*sems_a, g_in_thru, land, (0, D // 2), [res[n][1] for n in res],
                                    name="scatter_g_in_a_wait")
    _, land = _scatter_wait(*sems_b, g_in_thru, land, (D // 2, D // 2), (), name="scatter_g_in_b_wait")
    res["w_in"] = finish("w_in", _adamw_recv(big["w_in"], land, moment(m_w_in, "w_in"), moment(v_w_in, "w_in"),
                                             name="adamw_w_in", own=own_in))

    order = ["c_ctx", "w_mod", "b_mod", "w_in", "q_norm_w", "k_norm_w", "conv_qkv_w", "a_log", "dt_bias", "gdn_norm_w",
             "w_pa", "w_pd", "w_out", "w_up", "ffn_conv_w", "ffn_conv_b", "w_down", "final_norm_w"]
    return (loss, grad_x[None], *[res[n][0] for n in order], *[res[n][1] for n in order],
            *[res[n][2] for n in order], *[res[n][3] for n in order])
```

```python
import functools
import math

import jax
import jax.numpy as jnp
from jax import lax
from jax.experimental import pallas as pl
from jax.experimental.pallas import tpu as pltpu

F32 = jnp.float32
BF16 = jnp.bfloat16
HI = lax.Precision.HIGHEST
MESH = pl.DeviceIdType.MESH

NDEV = 8
D = 1024
HD = 128
AH, AKV, GRP = 8, 2, 4
GH = 8
CH = 64
DFF = 2816
GRID_W = 64
EPS = 1e-6
ROPE_THETA = 10000.0
LOG2E = math.log2(math.e)
C_KV, C_AQ, C_QKV, C_BL, C_Z, C_GATE, C_END = 0, 512, 1536, 4608, 5120, 6144, 8192
W_QKV, W_AQ, W_Z, W_END = 512, 3616, 4640, 7712


def _pad_columns(w):
    zeros = jnp.zeros((C_Z - C_QKV - (W_AQ - W_QKV), D), w.dtype)
    return jnp.concatenate([w[:W_QKV], w[W_AQ:W_Z], w[W_QKV:W_AQ], zeros, w[W_Z:]], axis=0)


def _unpad_columns(g):
    return jnp.concatenate([g[:C_AQ], g[C_QKV:C_QKV + W_AQ - W_QKV], g[C_AQ:C_QKV], g[C_Z:]], axis=0)
LR, B1, B2, AEPS, WD, STEP = 0.001, 0.9, 0.999, 1e-08, 0.01, 10
VMEM_BIG = 56 * 1024 * 1024
INTRA_FWD_CHUNKS = 36
INTRA_BWD_CHUNKS = 36


def _call(body, *, name, out_shape, grid=None, in_specs=None, out_specs=None, scratch=(), sem=None,
          vmem=None, aliases=None):
    params = {}
    if sem is not None:
        params["dimension_semantics"] = sem
    if vmem is not None:
        params["vmem_limit_bytes"] = vmem
    kw = {}
    if grid is not None:
        kw["grid"] = grid
    if in_specs is not None:
        kw["in_specs"] = in_specs
    if out_specs is not None:
        kw["out_specs"] = out_specs
    if aliases:
        kw["input_output_aliases"] = aliases
    return pl.pallas_call(body, name=name, out_shape=out_shape, scratch_shapes=list(scratch),
                          compiler_params=pltpu.CompilerParams(**params), **kw)


def _call_carrying(body, exch, *, name, out_shape, grid, in_specs, out_specs, scratch=(), vmem=None):
    n, nin, nout, nscr = exch.n, len(in_specs), len(out_shape), len(scratch)
    steps = math.prod(grid)
    mid = (2 * steps) // 3

    def wrapped(*refs):
        ins, cins = refs[:nin], refs[nin:nin + n]
        outs, couts = refs[nin + n:nin + n + nout], refs[nin + n + nout:nin + 2 * n + nout]
        scr, sems = refs[nin + 2 * n + nout:nin + 2 * n + nout + nscr], refs[nin + 2 * n + nout + nscr:]
        ids = [pl.program_id(i) for i in range(len(grid))]
        first = functools.reduce(jnp.logical_and, [i == 0 for i in ids])
        last = functools.reduce(jnp.logical_and, [i == g - 1 for i, g in zip(ids, grid)])

        @pl.when(first)
        def _():
            exch.start(cins, couts, sems)

        if hasattr(exch, "middle"):
            linear = functools.reduce(lambda acc, ig: acc * ig[1] + ig[0], zip(ids, grid), 0)

            @pl.when(linear == mid)
            def _():
                exch.middle(cins, couts, sems)

        body(*ins, *outs, *scr)

        @pl.when(last)
        def _():
            exch.finish(cins, couts, sems)

    params = {"dimension_semantics": ("arbitrary",) * len(grid)}
    if vmem is not None:
        params["vmem_limit_bytes"] = vmem
    fn = pl.pallas_call(wrapped, name=name, out_shape=tuple(out_shape) + exch.out_shape, grid=grid,
                        in_specs=list(in_specs) + [HBM] * n, out_specs=tuple(out_specs) + (HBM,) * n,
                        scratch_shapes=list(scratch) + exch.scratch, compiler_params=pltpu.CompilerParams(**params))

    def run(*args):
        res = fn(*args, *exch.arrs)
        return res[:nout], list(res[nout:])

    return run


def _sds(shape, dtype=F32):
    return jax.ShapeDtypeStruct(tuple(shape), dtype)


def _dot(a, b, ca, cb):
    return lax.dot_general(a.astype(BF16), b.astype(BF16), (((ca,), (cb,)), ((), ())),
                           preferred_element_type=F32)


@jax.custom_vjp
def _nn(a, b):
    return _dot(a, b, 1, 0)


@jax.custom_vjp
def _nt(a, b):
    return _dot(a, b, 1, 1)


@jax.custom_vjp
def _tn(a, b):
    return _dot(a, b, 0, 0)


_nn.defvjp(lambda a, b: (_nn(a, b), (a, b)), lambda r, g: (_nt(g, r[1]), _tn(r[0], g)))
_nt.defvjp(lambda a, b: (_nt(a, b), (a, b)), lambda r, g: (_nn(g, r[1]), _tn(g, r[0])))
_tn.defvjp(lambda a, b: (_tn(a, b), (a, b)), lambda r, g: (_nt(r[1], g), _nn(r[0], g)))


def _mdot(a, b):
    return jnp.dot(a, b, precision=lax.Precision.HIGH, preferred_element_type=F32)


def _maskdot(mask, a, cm):
    hi = a.astype(BF16)
    r = a - hi.astype(F32)
    mid = r.astype(BF16)
    lo = (r - mid.astype(F32)).astype(BF16)
    mb = mask.astype(BF16)
    dims = (((cm,), (0,)), ((), ()))
    return (lax.dot_general(mb, hi, dims, preferred_element_type=F32)
            + lax.dot_general(mb, mid, dims, preferred_element_type=F32)
            + lax.dot_general(mb, lo, dims, preferred_element_type=F32))


@jax.custom_vjp
def _mask_nn(mask, a):
    return _maskdot(mask, a, 1)


_mask_nn.defvjp(lambda mask, a: (_maskdot(mask, a, 1), mask),
                lambda mask, g: (jnp.zeros_like(mask), _maskdot(mask, g, 0)))


@jax.custom_vjp
def _saved_inverse(lmat, x):
    return x


def _saved_inverse_bwd(x, g):
    t = lax.dot_general(x, g, (((0,), (0,)), ((), ())), precision=lax.Precision.HIGH, preferred_element_type=F32)
    dl = lax.dot_general(t, x, (((1,), (1,)), ((), ())), precision=lax.Precision.HIGH, preferred_element_type=F32)
    return -dl, jnp.zeros_like(x)


_saved_inverse.defvjp(lambda lmat, x: (x, x), _saved_inverse_bwd)


def _row_ids(shape):
    return lax.broadcasted_iota(jnp.int32, shape, 0)


def _shift_rows(x, down, bounds):
    n = x.shape[0]
    rows = _row_ids(x.shape)
    y = pltpu.roll(x, 1 if down else n - 1, 0)
    edge = functools.reduce(jnp.logical_or, [rows == (s if down else e - 1) for s, e in bounds])
    return jnp.where(edge, 0.0, y)


def _make_shift(bounds):
    @jax.custom_vjp
    def down(x):
        return _shift_rows(x, True, bounds)

    @jax.custom_vjp
    def up(x):
        return _shift_rows(x, False, bounds)

    down.defvjp(lambda x: (down(x), None), lambda _, g: (up(g),))
    up.defvjp(lambda x: (up(x), None), lambda _, g: (down(g),))
    return down, up


@jax.custom_vjp
def _swap32(x):
    lane = lax.broadcasted_iota(jnp.int32, x.shape, x.ndim - 1)
    return jnp.where((lane % 64) < 32, pltpu.roll(x, HD - 32, x.ndim - 1), pltpu.roll(x, 32, x.ndim - 1))


_swap32.defvjp(lambda x: (_swap32(x), None), lambda _, g: (_swap32(g),))


def _rms(x):
    return x * lax.rsqrt(jnp.mean(x * x, axis=-1, keepdims=True) + EPS)


def _silu(x):
    return x * jax.nn.sigmoid(x)


def _mm(a, b, *, name, M, N, K, ta=False, tb=False, out_dtype=F32, bm=None, bn=None, bk=None, after=()):
    bm, bn, bk = bm or M, bn or N, bk or K
    assert M % bm == 0 and N % bn == 0 and K % bk == 0, (name, M, N, K, bm, bn, bk)
    nk = K // bk
    ca, cb = (0 if ta else 1), (1 if tb else 0)
    na = len(after)

    def body(a_ref, b_ref, *rest):
        o_ref, acc = rest[na], rest[na + 1:]
        r = _dot(a_ref[...], b_ref[...], ca, cb)
        if nk == 1:
            o_ref[...] = r.astype(out_dtype)
        else:
            acc_ref, = acc
            k = pl.program_id(2)

            @pl.when(k == 0)
            def _():
                acc_ref[...] = r

            @pl.when(k > 0)
            def _():
                acc_ref[...] += r

            @pl.when(k == nk - 1)
            def _():
                o_ref[...] = acc_ref[...].astype(out_dtype)

    a_spec = pl.BlockSpec((bk, bm), lambda i, j, k: (k, i)) if ta else pl.BlockSpec((bm, bk), lambda i, j, k: (i, k))
    b_spec = pl.BlockSpec((bn, bk), lambda i, j, k: (j, k)) if tb else pl.BlockSpec((bk, bn), lambda i, j, k: (k, j))
    return _call(body, name=name, out_shape=_sds((M, N), out_dtype), grid=(M // bm, N // bn, nk),
                 in_specs=[a_spec, b_spec] + [pl.BlockSpec(memory_space=pl.ANY)] * na,
                 out_specs=pl.BlockSpec((bm, bn), lambda i, j, k: (i, j)),
                 scratch=[pltpu.VMEM((bm, bn), F32)] if nk > 1 else [],
                 sem=("parallel", "parallel", "arbitrary"), vmem=VMEM_BIG)(a, b, *after)


def _normmod_fn(x, sh, sc):
    return _rms(x) * (1.0 + sc) + sh


def _normmod_fwd(x, mod, i_sh, i_sc, *, name, br=256):
    R = x.shape[0]

    def body(x_ref, mod_ref, o_ref):
        o_ref[...] = _normmod_fn(x_ref[...], mod_ref[i_sh:i_sh + 1, :], mod_ref[i_sc:i_sc + 1, :]).astype(BF16)

    return _call(body, name=name, out_shape=_sds((R, D), BF16), grid=(R // br,),
                 in_specs=[pl.BlockSpec((br, D), lambda i: (i, 0)), pl.BlockSpec((6, D), lambda i: (0, 0))],
                 out_specs=pl.BlockSpec((br, D), lambda i: (i, 0)), sem=("parallel",))(x, mod)


def _normmod_bwd(x, mod, i_sh, i_sc, dh, dh_off, res, *, name, br=256):
    R = x.shape[0]
    ob = dh_off // br
    has_res = res is not None

    def body(x_ref, mod_ref, dh_ref, *rest):
        if has_res:
            res_ref, dx_ref, dsh_ref, dsc_ref = rest
        else:
            dx_ref, dsh_ref, dsc_ref = rest
        sh, sc = mod_ref[i_sh:i_sh + 1, :], mod_ref[i_sc:i_sc + 1, :]
        _, vjp = jax.vjp(_normmod_fn, x_ref[...], sh, sc)
        dx, dsh, dsc = vjp(dh_ref[...])
        dx_ref[...] = dx + res_ref[...] if has_res else dx

        @pl.when(pl.program_id(0) == 0)
        def _():
            dsh_ref[...] = jnp.zeros_like(dsh_ref)
            dsc_ref[...] = jnp.zeros_like(dsc_ref)

        dsh_ref[...] += dsh
        dsc_ref[...] += dsc

    row = pl.BlockSpec((br, D), lambda i: (i, 0))
    vec = pl.BlockSpec((1, D), lambda i: (0, 0))
    ins = [row, pl.BlockSpec((6, D), lambda i: (0, 0)), pl.BlockSpec((br, D), lambda i: (i + ob, 0))]
    args = [x, mod, dh]
    if has_res:
        ins.append(row)
        args.append(res)
    return _call(body, name=name, out_shape=(_sds((R, D)), _sds((1, D)), _sds((1, D))), grid=(R // br,),
                 in_specs=ins, out_specs=(row, vec, vec), sem=("arbitrary",))(*args)


def _rope(x, cos, sin):
    return x * cos + _swap32(x) * sin


def _aprep_fn(qs, ks, cos, sin, qw, kw):
    return ([_rope(_rms(q) * qw, cos, sin) for q in qs], [_rope(_rms(k) * kw, cos, sin) for k in ks])


def _aprep_fwd(proj, cos, sin, qw, kw, *, br=256):
    T = proj.shape[0]

    def body(x_ref, cos_ref, sin_ref, qw_ref, kw_ref, q_ref, k_ref, v_ref):
        qs = [x_ref[:, C_AQ + h * HD:C_AQ + (h + 1) * HD] for h in range(AH)]
        ks = [x_ref[:, h * HD:(h + 1) * HD] for h in range(AKV)]
        qo, ko = _aprep_fn(qs, ks, cos_ref[...], sin_ref[...], qw_ref[...], kw_ref[...])
        for h in range(AH):
            q_ref[h] = qo[h].astype(BF16)
        for h in range(AKV):
            k_ref[h] = ko[h].astype(BF16)
            v_ref[h] = x_ref[:, (AKV + h) * HD:(AKV + h + 1) * HD].astype(BF16)

    tab = pl.BlockSpec((br, HD), lambda i: (i, 0))
    vec = pl.BlockSpec((1, HD), lambda i: (0, 0))
    return _call(body, name="aprep_fwd",
                 out_shape=(_sds((AH, T, HD), BF16), _sds((AKV, T, HD), BF16), _sds((AKV, T, HD), BF16)),
                 grid=(T // br,),
                 in_specs=[pl.BlockSpec((br, C_QKV), lambda i: (i, 0)), tab, tab, vec, vec],
                 out_specs=(pl.BlockSpec((AH, br, HD), lambda i: (0, i, 0)),
                            pl.BlockSpec((AKV, br, HD), lambda i: (0, i, 0)),
                            pl.BlockSpec((AKV, br, HD), lambda i: (0, i, 0))),
                 sem=("parallel",))(proj, cos, sin, qw, kw)


def _aprep_bwd(proj, cos, sin, qw, kw, dq, dk, dv, dproj, L, *, br=256):
    T = proj.shape[0]
    lb = L // br

    def body(x_ref, cos_ref, sin_ref, qw_ref, kw_ref, dq_ref, dk_ref, dv_ref, _, dx_ref, dqw_ref, dkw_ref):
        i = pl.program_id(0)
        qs = [x_ref[:, C_AQ + h * HD:C_AQ + (h + 1) * HD] for h in range(AH)]
        ks = [x_ref[:, h * HD:(h + 1) * HD] for h in range(AKV)]
        _, vjp = jax.vjp(_aprep_fn, qs, ks, cos_ref[...], sin_ref[...], qw_ref[...], kw_ref[...])
        is_lat = i >= lb
        dqs = [jnp.where(is_lat, dq_ref[h], 0.0) for h in range(AH)]
        dks = [dk_ref[h] for h in range(AKV)]
        gq, gk, _, _, gqw, gkw = vjp((dqs, dks))
        for h in range(AH):
            dx_ref[:, C_AQ + h * HD:C_AQ + (h + 1) * HD] = gq[h].astype(BF16)
        for h in range(AKV):
            dx_ref[:, h * HD:(h + 1) * HD] = gk[h].astype(BF16)
            dx_ref[:, (AKV + h) * HD:(AKV + h + 1) * HD] = dv_ref[h].astype(BF16)

        @pl.when(i == 0)
        def _():
            dqw_ref[...] = jnp.zeros_like(dqw_ref)
            dkw_ref[...] = jnp.zeros_like(dkw_ref)

        dqw_ref[...] += gqw
        dkw_ref[...] += gkw

    tab = pl.BlockSpec((br, HD), lambda i: (i, 0))
    vec = pl.BlockSpec((1, HD), lambda i: (0, 0))
    kvb = pl.BlockSpec((AKV, br, HD), lambda i: (0, i, 0))
    blk = pl.BlockSpec((br, C_QKV), lambda i: (i, 0))
    return _call(body, name="aprep_bwd", out_shape=(_sds(dproj.shape, BF16), _sds((1, HD)), _sds((1, HD))),
                 grid=(T // br,),
                 in_specs=[blk, tab, tab, vec, vec,
                           pl.BlockSpec((AH, br, HD), lambda i: (0, jnp.maximum(i - lb, 0), 0)), kvb, kvb, ANYSPEC],
                 out_specs=(blk, vec, vec), aliases={8: 0},
                 sem=("arbitrary",))(proj, cos, sin, qw, kw, dq, dk, dv, dproj)


def _attn_grad(q, k, v, o, lse2, do):
    scale = HD ** -0.5
    p = jnp.exp2(_dot(q, k, 1, 1) * (scale * LOG2E) - lse2)
    dp = _dot(do, v, 1, 1)
    ds = p * (dp - jnp.sum(do * o, axis=-1, keepdims=True)) * scale
    return _dot(ds, k, 1, 0), _dot(ds, q, 0, 0), _dot(p, do, 0, 0)


ATTN_KEYS = 256


def _attn_fwd(q, k, v, L, exch, *, bq=128):
    T = q.shape[1]
    N = T - L
    lb = L // bq
    assert T % ATTN_KEYS == 0
    scale = HD ** -0.5
    heads = range(GRP)

    def body(q_ref, k_ref, v_ref, o_ref, o32_ref, lse_ref):
        qs = [q_ref[g] for g in heads]
        m = [jnp.full((bq, 1), -jnp.inf, F32) for _ in heads]
        l = [jnp.zeros((bq, 1), F32) for _ in heads]
        acc = [jnp.zeros((bq, HD), F32) for _ in heads]
        for c in range(T // ATTN_KEYS):
            kc, vc = k_ref[c * ATTN_KEYS:(c + 1) * ATTN_KEYS, :], v_ref[c * ATTN_KEYS:(c + 1) * ATTN_KEYS, :]
            s = [_dot(qs[g], kc, 1, 1) * (scale * LOG2E) for g in heads]
            m_new = [jnp.maximum(m[g], jnp.max(s[g], axis=-1, keepdims=True)) for g in heads]
            alpha = [jnp.exp2(m[g] - m_new[g]) for g in heads]
            p = [jnp.exp2(s[g] - m_new[g]) for g in heads]
            l = [l[g] * alpha[g] + jnp.sum(p[g], axis=-1, keepdims=True) for g in heads]
            acc = [acc[g] * alpha[g] + _dot(p[g], vc, 1, 0) for g in heads]
            m = m_new
        for g in heads:
            o = acc[g] / l[g]
            o_ref[:, g * HD:(g + 1) * HD] = o.astype(BF16)
            o32_ref[:, g * HD:(g + 1) * HD] = o
            lse_ref[g] = jnp.broadcast_to(m[g] + jnp.log2(l[g]), (bq, HD))

    kvb = pl.BlockSpec((None, T, HD), lambda g, i: (g, 0, 0))
    ob = pl.BlockSpec((bq, GRP * HD), lambda g, i: (i, g))
    return _call_carrying(
        body, exch, name="attn_fwd",
        out_shape=(_sds((N, AH * HD), BF16), _sds((N, AH * HD)), _sds((AH, N, HD))), grid=(AKV, N // bq),
        in_specs=[pl.BlockSpec((GRP, bq, HD), lambda g, i: (g, i + lb, 0)), kvb, kvb],
        out_specs=(ob, ob, pl.BlockSpec((GRP, bq, HD), lambda g, i: (g, i, 0))), vmem=VMEM_BIG)(q, k, v)


def _attn_bwd(q, k, v, o32, lse, do, L, exch, *, bq=128):
    T = q.shape[1]
    N = T - L
    lb = L // bq

    def body(q_ref, k_ref, v_ref, o_ref, lse_ref, do_ref, dq_ref, dk_ref, dv_ref):
        rows = lambda r: jnp.concatenate([r[:, g * HD:(g + 1) * HD] for g in range(GRP)], axis=0)
        lse = jnp.max(lse_ref[...].reshape(GRP * bq, HD), axis=-1, keepdims=True)
        dq, dk, dv = _attn_grad(q_ref[...].reshape(GRP * bq, HD), k_ref[...], v_ref[...], rows(o_ref), lse, rows(do_ref))
        dq_ref[...] = dq.reshape(GRP, bq, HD)

        @pl.when(pl.program_id(1) == 0)
        def _():
            dk_ref[...] = jnp.zeros_like(dk_ref)
            dv_ref[...] = jnp.zeros_like(dv_ref)

        dk_ref[...] += dk
        dv_ref[...] += dv

    kvb = pl.BlockSpec((None, T, HD), lambda g, i: (g, 0, 0))
    qb = pl.BlockSpec((GRP, bq, HD), lambda g, i: (g, i + lb, 0))
    hb = pl.BlockSpec((GRP, bq, HD), lambda g, i: (g, i, 0))
    ob = pl.BlockSpec((bq, GRP * HD), lambda g, i: (i, g))
    return _call_carrying(body, exch, name="attn_bwd",
                          out_shape=(_sds((AH, N, HD)), _sds((AKV, T, HD)), _sds((AKV, T, HD))), grid=(AKV, N // bq),
                          in_specs=[qb, kvb, kvb, ob, hb, ob], out_specs=(hb, kvb, kvb),
                          vmem=VMEM_BIG)(q, k, v, o32, lse, do)


def _gprep_fn(kind, shifts, x, w):
    down, up = shifts
    y = down(x) * w[0:1, :] + x * w[1:2, :] + up(x) * w[2:3, :]
    a = _silu(y)
    if kind == 2:
        return a
    a = a * lax.rsqrt(jnp.sum(a * a, axis=-1, keepdims=True) + EPS)
    return a * (HD ** -0.5) if kind == 0 else a


def _gprep_fwd(proj, conv_w, kind, bounds):
    T = proj.shape[0]
    shifts = _make_shift(bounds)
    cb = C_QKV // HD + kind * GH

    def body(x_ref, w_ref, o_ref):
        o_ref[...] = _gprep_fn(kind, shifts, x_ref[...], w_ref[...])

    return _call(body, name=f"gprep_fwd{kind}", out_shape=_sds((GH, T, HD)), grid=(GH,),
                 in_specs=[pl.BlockSpec((T, HD), lambda h: (0, cb + h)),
                           pl.BlockSpec((3, HD), lambda h: (0, kind * GH + h))],
                 out_specs=pl.BlockSpec((None, T, HD), lambda h: (h, 0, 0)), sem=("parallel",))(proj, conv_w)


def _gprep_bwd(proj, conv_w, kind, bounds, dy, dproj):
    T = proj.shape[0]
    shifts = _make_shift(bounds)
    cb = C_QKV // HD + kind * GH

    def body(x_ref, w_ref, dy_ref, _, dx_ref, dw_ref):
        _, vjp = jax.vjp(functools.partial(_gprep_fn, kind, shifts), x_ref[...], w_ref[...])
        dx, dw = vjp(dy_ref[0] + dy_ref[1])
        dx_ref[...] = dx.astype(BF16)
        dw_ref[...] = dw

    return _call(body, name=f"gprep_bwd{kind}", out_shape=(_sds(dproj.shape, BF16), _sds((3, GH * HD))), grid=(GH,),
                 in_specs=[pl.BlockSpec((T, HD), lambda h: (0, cb + h)),
                           pl.BlockSpec((3, HD), lambda h: (0, kind * GH + h)),
                           pl.BlockSpec((2, None, T, HD), lambda h: (0, h, 0, 0)), ANYSPEC],
                 out_specs=(pl.BlockSpec((T, HD), lambda h: (0, cb + h)), pl.BlockSpec((3, HD), lambda h: (0, h))),
                 aliases={3: 0}, sem=("parallel",))(proj, conv_w, dy, dproj)


def _bl_fn(x, alog, dtb):
    lane = lax.broadcasted_iota(jnp.int32, x.shape, 1)
    beta = jax.nn.sigmoid(x)
    z = x + dtb
    sp = jnp.maximum(z, 0.0) + jnp.log1p(jnp.exp(-jnp.abs(z)))
    la = -jnp.exp(alog) * sp
    return jnp.where(lane < 2 * GH, beta, jnp.where(lane < 4 * GH, la, 0.0))


def _bl_fwd(proj, alog, dtb, *, br=256):
    T = proj.shape[0]

    def body(x_ref, a_ref, d_ref, o_ref):
        o_ref[...] = _bl_fn(x_ref[...], a_ref[...], d_ref[...])

    vec = pl.BlockSpec((1, HD), lambda i: (0, 0))
    return _call(body, name="bl_fwd", out_shape=_sds((T, HD)), grid=(T // br,),
                 in_specs=[pl.BlockSpec((br, HD), lambda i: (i, C_BL // HD)), vec, vec],
                 out_specs=pl.BlockSpec((br, HD), lambda i: (i, 0)), sem=("parallel",))(proj, alog, dtb)


def _bl_bwd(proj, alog, dtb, dbl, dproj, *, br=256):
    T = proj.shape[0]
    wide = C_Z - C_BL

    def body(x_ref, a_ref, d_ref, g_ref, _, dx_ref, da_ref, dd_ref):
        g = g_ref[0, 0]
        for d in range(2):
            for h in range(GH):
                if d or h:
                    g = g + g_ref[d, h]
        _, vjp = jax.vjp(_bl_fn, x_ref[...], a_ref[...], d_ref[...])
        dx, da, dd = vjp(g)
        dx_ref[:, :HD] = dx.astype(BF16)
        dx_ref[:, HD:] = jnp.zeros((br, wide - HD), BF16)

        @pl.when(pl.program_id(0) == 0)
        def _():
            da_ref[...] = jnp.zeros_like(da_ref)
            dd_ref[...] = jnp.zeros_like(dd_ref)

        da_ref[...] += da
        dd_ref[...] += dd

    vec = pl.BlockSpec((1, HD), lambda i: (0, 0))
    return _call(body, name="bl_bwd", out_shape=(_sds(dproj.shape, BF16), _sds((1, HD)), _sds((1, HD))), grid=(T // br,),
                 in_specs=[pl.BlockSpec((br, HD), lambda i: (i, C_BL // HD)), vec, vec,
                           pl.BlockSpec((2, GH, br, HD), lambda i: (0, 0, i, 0)), ANYSPEC],
                 out_specs=(pl.BlockSpec((br, wide), lambda i: (i, C_BL // wide)), vec, vec), aliases={4: 0},
                 sem=("arbitrary",))(proj, alog, dtb, dbl, dproj)


def _chunk_masks(d):
    ii = lax.broadcasted_iota(jnp.int32, (CH, CH), 0)
    jj = lax.broadcasted_iota(jnp.int32, (CH, CH), 1)
    eye = (ii == jj).astype(F32)
    before = jnp.where(d == 0, (jj < ii).astype(F32), (jj > ii).astype(F32))
    return before, before + eye, eye


def _same_block(b):
    ii = lax.broadcasted_iota(jnp.int32, (CH, CH), 0)
    jj = lax.broadcasted_iota(jnp.int32, (CH, CH), 1)
    shift = b.bit_length() - 1
    return (jnp.right_shift(ii, shift) == jnp.right_shift(jj, shift)).astype(F32)


def _intra_fn(masks, sel_b, sel_l, qs, ks, vs, bls, xs=None):
    before, ateq, eye = masks
    inc = ateq > 0.0
    each = lambda f, *ls: [f(*t) for t in zip(*ls)]
    beta = each(lambda bl: jnp.sum(bl * sel_b, axis=-1, keepdims=True), bls)
    la = each(lambda bl: jnp.sum(bl * sel_l, axis=-1, keepdims=True), bls)
    gam = each(lambda a: _mask_nn(ateq, jnp.broadcast_to(a, (CH, HD))), la)
    gi = each(lambda g: g[:, :CH], gam)
    gj = each(lambda g: jnp.transpose(g)[:CH, :], gam)
    kq = each(lambda k, q: _nt(jnp.concatenate([k, q], axis=0), k), ks, qs)
    kk = each(lambda t: t[:CH], kq)
    qk = each(lambda t: t[CH:], kq)
    dec = each(lambda a, b: jnp.where(inc, jnp.exp(jnp.where(inc, a - b, 0.0)), 0.0), gi, gj)
    lmat = each(lambda b, d, m: before * (b * d * m), beta, dec, kk)
    if xs is None:
        same = lambda b: _same_block(b)
        l8 = each(lambda m: m * same(8), lmat)
        x = each(lambda m: eye - m, l8)
        p2 = each(lambda m: _mdot(m, m), l8)
        y = each(lambda a, b: _mdot(jnp.concatenate([a, b], axis=0), b), x, p2)
        x = each(lambda a, t: a + t[:CH], x, y)
        x = each(lambda a, t: a + _mdot(a, t[CH:]), x, y)
        for b in (8, 16, 32):
            below = same(2 * b) - same(b)
            x = each(lambda a, m: a - _mdot(a, _mdot(m * below, a)), x, lmat)
    else:
        x = each(_saved_inverse, lmat, xs)
    eg = each(jnp.exp, gam)
    uw = each(lambda a, b, v, e, k: _mdot(a, jnp.concatenate([b * v, (b * e) * k], axis=1)), x, beta, vs, eg, ks)
    u = each(lambda t: t[:, :HD], uw)
    w = each(lambda t: t[:, HD:], uw)
    tot = each(lambda a: jnp.sum(a, axis=0, keepdims=True), la)
    kd = each(lambda k, t, g: k * jnp.exp(t - g), ks, tot, gam)
    gl = each(lambda t: jnp.broadcast_to(jnp.exp(t), (1, HD)), tot)
    qd = each(lambda q, e: q * e, qs, eg)
    p = each(lambda d, m: d * m, dec, qk)
    return (u, w, kd, qd, p, gl, x) if xs is None else (u, w, kd, qd, p, gl)


def _dir_head_sel(d, h):
    lane = lax.broadcasted_iota(jnp.int32, (1, HD), 1)
    return (lane == d * GH + h).astype(F32), (lane == 2 * GH + d * GH + h).astype(F32)


def _intra_specs(T, G):
    nc = T // CH
    assert nc % G == 0
    qkv = pl.BlockSpec((None, G * CH, HD), lambda d, h, c: (h, c, 0))
    bl = pl.BlockSpec((G * CH, HD), lambda d, h, c: (c, 0))
    big = pl.BlockSpec((None, None, G * CH, HD), lambda d, h, c: (d, h, c, 0))
    pm = pl.BlockSpec((None, None, G * CH, CH), lambda d, h, c: (d, h, c, 0))
    gl = pl.BlockSpec((None, None, G, 1, HD), lambda d, h, c: (d, h, c, 0, 0))
    shapes = (_sds((2, GH, T, HD)),) + (_sds((2, GH, T, HD), BF16),) * 3 + (
        _sds((2, GH, T, CH), BF16), _sds((2, GH, nc, 1, HD)), _sds((2, GH, T, CH)))
    return nc, qkv, bl, big, pm, gl, shapes


def _chunks_per_step(T, most):
    nc = T // CH
    return max(g for g in range(1, most + 1) if nc % g == 0)


def _chunk_at(g, d, nc, ncc):
    pos = _visit_pos(g, d, nc, ncc)
    return pos, pl.ds(pl.multiple_of(pos * CH, CH), CH)


def _intra_fwd(q, k, v, bl, L, exch):
    T = q.shape[1]
    G = _chunks_per_step(T, INTRA_FWD_CHUNKS)
    nc, qkv_s, bl_s, big, pm, gl_s, shapes = _intra_specs(T, G)
    assert G == nc
    ncc = L // CH

    def body(q_ref, k_ref, v_ref, bl_ref, u_ref, w_ref, kd_ref, qd_ref, p_ref, gl_ref, x_ref):
        d, h = pl.program_id(0), pl.program_id(1)
        sb, sl = _dir_head_sel(d, h)
        rows = [slice(g * CH, (g + 1) * CH) for g in range(G)]
        outs = _intra_fn(_chunk_masks(d), sb, sl, *[[r[s, :] for s in rows] for r in (q_ref, k_ref, v_ref, bl_ref)])
        for g in range(G):
            pos, at = _chunk_at(g, d, nc, ncc)
            for r, o in zip((u_ref, w_ref, kd_ref, qd_ref, p_ref, x_ref), outs[:5] + outs[6:]):
                r[at, :] = o[g].astype(r.dtype)
            gl_ref[pos] = outs[5][g]

    return _call_carrying(body, exch, name="gdn_intra_fwd", out_shape=shapes, grid=(2, GH, nc // G),
                          in_specs=[qkv_s, qkv_s, qkv_s, bl_s], out_specs=(big, big, big, big, pm, gl_s, pm))(q, k, v, bl)


def _intra_bwd(q, k, v, bl, xinv, cts, L, exch):
    T = q.shape[1]
    G = _chunks_per_step(T, INTRA_BWD_CHUNKS)
    nc, qkv_s, bl_s, big, pm, gl_s, _ = _intra_specs(T, G)
    assert G == nc
    ncc = L // CH

    def body(q_ref, k_ref, v_ref, bl_ref, x_ref, du, dw, dkd, dqd, dp, dgl, dq_ref, dk_ref, dv_ref, dbl_ref):
        d, h = pl.program_id(0), pl.program_id(1)
        sb, sl = _dir_head_sel(d, h)
        rows = [slice(g * CH, (g + 1) * CH) for g in range(G)]
        places = [_chunk_at(g, d, nc, ncc) for g in range(G)]
        fn = functools.partial(_intra_fn, _chunk_masks(d), sb, sl, xs=[x_ref[at, :] for _, at in places])
        _, vjp = jax.vjp(fn, *[[r[s, :] for s in rows] for r in (q_ref, k_ref, v_ref, bl_ref)])
        cts = tuple([r[at, :] for _, at in places] for r in (du, dw, dkd, dqd, dp)) + ([dgl[pos] for pos, _ in places],)
        grads = vjp(cts)
        for g in range(G):
            for r, o in zip((dq_ref, dk_ref, dv_ref, dbl_ref), grads):
                r[rows[g], :] = o[g]

    return _call_carrying(body, exch, name="gdn_intra_bwd", out_shape=(_sds((2, GH, T, HD)),) * 4,
                          grid=(2, GH, nc // G), in_specs=[qkv_s, qkv_s, qkv_s, bl_s, pm, big, big, big, big, pm, gl_s],
                          out_specs=(big,) * 4)(q, k, v, bl, xinv, *cts)


def _scan_fn(s, u, w, kd, qd, p, gl):
    each = lambda f, *ls: [f(*t) for t in zip(*ls)]
    ws = each(_nn, w, s)
    delta = each(lambda a, b: a - b, u, ws)
    kdd = each(_tn, kd, delta)
    s_new = each(lambda g, a, b: g * a + b, gl, s, kdd)
    qs = each(_nn, qd, s)
    pd = each(_nn, p, delta)
    return each(lambda a, b: a + b, qs, pd), s_new


SCAN_BLOCK = 4


def _visit_pos(c, d, nc, ncc):
    back = ncc - 1 - c if c < ncc else ncc + (nc - 1 - c)
    return jnp.where(d == 0, c, back)


def _scan_specs(T, L, back):
    tb = SCAN_BLOCK * CH
    assert T % tb == 0 and L % tb == 0
    nb, ncb = T // tb, L // tb
    at = (lambda t: nb - 1 - t) if back else (lambda t: t)
    big = pl.BlockSpec((2, GH, tb, HD), lambda t: (0, 0, at(t), 0))
    pm = pl.BlockSpec((2, GH, tb, CH), lambda t: (0, 0, at(t), 0))
    gl = pl.BlockSpec((2, GH, SCAN_BLOCK, 1, HD), lambda t: (0, 0, at(t), 0, 0))
    st = pl.BlockSpec((2, GH, SCAN_BLOCK, HD, HD), lambda t: (0, 0, at(t), 0, 0))

    def natural(b):
        return jnp.where(b < ncb, ncb - 1 - b, nb - 1 - (b - ncb))

    do_specs = (pl.BlockSpec((GH, tb, HD), lambda t: (0, at(t), 0)),
                pl.BlockSpec((GH, tb, HD), lambda t: (0, natural(at(t)), 0)))
    return nb, big, pm, gl, st, do_specs


SCAN_STREAMS = [(d, h) for d in (0, 1) for h in range(GH)]


def _scan_fwd(u, w, kd, qd, p, gl, L):
    T = u.shape[2]
    nb, big, pm, gl_s, st, _ = _scan_specs(T, L, False)

    def body(u_ref, w_ref, kd_ref, qd_ref, p_ref, gl_ref, o_ref, st_ref, s_scr):
        @pl.when(pl.program_id(0) == 0)
        def _():
            s_scr[...] = jnp.zeros_like(s_scr)

        s = [s_scr[d, h] for d, h in SCAN_STREAMS]
        for i in range(SCAN_BLOCK):
            rows = slice(i * CH, (i + 1) * CH)
            for (d, h), sv in zip(SCAN_STREAMS, s):
                st_ref[d, h, i] = sv
            o, s = _scan_fn(s, *[[r[d, h, rows, :].astype(F32) for d, h in SCAN_STREAMS]
                                 for r in (u_ref, w_ref, kd_ref, qd_ref, p_ref)],
                            [gl_ref[d, h, i] for d, h in SCAN_STREAMS])
            for (d, h), ov in zip(SCAN_STREAMS, o):
                o_ref[d, h, rows, :] = ov
        for (d, h), sv in zip(SCAN_STREAMS, s):
            s_scr[d, h] = sv

    return _call(body, name="gdn_scan_fwd", out_shape=(_sds((2, GH, T, HD)), _sds((2, GH, T // CH, HD, HD))),
                 grid=(nb,), in_specs=[big, big, big, big, pm, gl_s], out_specs=(big, st),
                 scratch=[pltpu.VMEM((2, GH, HD, HD), F32)], sem=("arbitrary",), vmem=VMEM_BIG)(u, w, kd, qd, p, gl)


def _scan_bwd(u, w, kd, qd, p, gl, states, do, L, exch):
    T = u.shape[2]
    nb, big, pm, gl_s, st, do_specs = _scan_specs(T, L, True)

    def body(u_ref, w_ref, kd_ref, qd_ref, p_ref, gl_ref, st_ref, do0_ref, do1_ref,
             du_ref, dw_ref, dkd_ref, dqd_ref, dp_ref, dgl_ref, ds_scr):
        @pl.when(pl.program_id(0) == 0)
        def _():
            ds_scr[...] = jnp.zeros_like(ds_scr)

        ds = [ds_scr[d, h] for d, h in SCAN_STREAMS]
        for i in reversed(range(SCAN_BLOCK)):
            rows = slice(i * CH, (i + 1) * CH)
            mirror = slice((SCAN_BLOCK - 1 - i) * CH, (SCAN_BLOCK - i) * CH)
            _, vjp = jax.vjp(_scan_fn, [st_ref[d, h, i] for d, h in SCAN_STREAMS],
                             *[[r[d, h, rows, :].astype(F32) for d, h in SCAN_STREAMS]
                               for r in (u_ref, w_ref, kd_ref, qd_ref, p_ref)],
                             [gl_ref[d, h, i] for d, h in SCAN_STREAMS])
            dos = [do0_ref[h, rows, :] if d == 0 else do1_ref[h, mirror, :] for d, h in SCAN_STREAMS]
            ds, gu, gw, gkd, gqd, gp, ggl = vjp((dos, ds))
            for n, (d, h) in enumerate(SCAN_STREAMS):
                du_ref[d, h, rows, :] = gu[n]
                dw_ref[d, h, rows, :] = gw[n]
                dkd_ref[d, h, rows, :] = gkd[n]
                dqd_ref[d, h, rows, :] = gqd[n]
                dp_ref[d, h, rows, :] = gp[n]
                dgl_ref[d, h, i] = ggl[n]
        for (d, h), dv in zip(SCAN_STREAMS, ds):
            ds_scr[d, h] = dv

    return _call_carrying(
        body, exch, name="gdn_scan_bwd",
        out_shape=(_sds((2, GH, T, HD)),) * 4 + (_sds((2, GH, T, CH)), _sds((2, GH, T // CH, 1, HD))),
        grid=(nb,), in_specs=[big, big, big, big, pm, gl_s, st, *do_specs], out_specs=(big, big, big, big, pm, gl_s),
        scratch=[pltpu.VMEM((2, GH, HD, HD), F32)], vmem=VMEM_BIG)(u, w, kd, qd, p, gl, states, do, do)


def _gout_fn(o0, o1, z, gw):
    return _rms(o0 + o1) * gw * _silu(z)


def _backward_latent(o_ref, L):
    nl = (o_ref.shape[1] - L) // CH
    return jnp.concatenate([o_ref[1, L + (nl - 1 - j) * CH:L + (nl - j) * CH, :] for j in range(nl)], axis=0)


def _gout_fwd(o, proj, gw, L):
    T = o.shape[2]
    N = T - L
    ob = pl.BlockSpec((2, None, T, HD), lambda h: (0, h, 0, 0))

    def body(o_ref, z_ref, gw_ref, y_ref):
        y_ref[...] = _gout_fn(o_ref[0, L:, :], _backward_latent(o_ref, L), z_ref[L:, :], gw_ref[...]).astype(BF16)

    return _call(body, name="gout_fwd", out_shape=_sds((N, GH * HD), BF16), grid=(GH,),
                 in_specs=[ob, pl.BlockSpec((T, HD), lambda h: (0, C_Z // HD + h)), pl.BlockSpec((1, HD), lambda h: (0, 0))],
                 out_specs=pl.BlockSpec((N, HD), lambda h: (0, h)), sem=("parallel",))(o, proj, gw)


def _gout_bwd(o, proj, gw, dy, dproj, L):
    T = o.shape[2]
    N = T - L
    ob = pl.BlockSpec((2, None, T, HD), lambda h: (0, h, 0, 0))

    def body(o_ref, z_ref, gw_ref, dy_ref, _, do_ref, dz_ref, dgw_ref):
        _, vjp = jax.vjp(_gout_fn, o_ref[0, L:, :], _backward_latent(o_ref, L), z_ref[L:, :], gw_ref[...])
        g0, _, gz, ggw = vjp(dy_ref[...])
        do_ref[:L, :] = jnp.zeros((L, HD), F32)
        do_ref[L:, :] = g0
        dz_ref[:L, :] = jnp.zeros((L, HD), BF16)
        dz_ref[L:, :] = gz.astype(BF16)

        @pl.when(pl.program_id(0) == 0)
        def _():
            dgw_ref[...] = jnp.zeros_like(dgw_ref)

        dgw_ref[...] += ggw

    zb = pl.BlockSpec((T, HD), lambda h: (0, C_Z // HD + h))
    return _call(body, name="gout_bwd", out_shape=(_sds((GH, T, HD)), _sds(dproj.shape, BF16), _sds((1, HD))),
                 grid=(GH,),
                 in_specs=[ob, zb, pl.BlockSpec((1, HD), lambda h: (0, 0)), pl.BlockSpec((N, HD), lambda h: (0, h)), ANYSPEC],
                 out_specs=(pl.BlockSpec((None, T, HD), lambda h: (h, 0, 0)), zb, pl.BlockSpec((1, HD), lambda h: (0, 0))),
                 aliases={4: 1}, sem=("arbitrary",))(o, proj, gw, dy, dproj)


def _merge_fn(pa, pd, ga, gd):
    return jax.nn.sigmoid(ga) * pa + jax.nn.sigmoid(gd) * pd


def _merge_fwd(pa, pd, proj, L, *, br=256):
    N = pa.shape[0]
    lb = L // br
    row = pl.BlockSpec((br, D), lambda i: (i, 0))

    def body(pa_ref, pd_ref, ga_ref, gd_ref, y_ref):
        y_ref[...] = _merge_fn(pa_ref[...], pd_ref[...], ga_ref[...], gd_ref[...]).astype(BF16)

    return _call(body, name="merge_fwd", out_shape=_sds((N, D), BF16), grid=(N // br,),
                 in_specs=[row, row, pl.BlockSpec((br, D), lambda i: (i + lb, C_GATE // D)),
                           pl.BlockSpec((br, D), lambda i: (i + lb, C_GATE // D + 1))],
                 out_specs=row, sem=("parallel",))(pa, pd, proj, proj)


def _merge_bwd(pa, pd, proj, dy, L, *, br=256):
    N = pa.shape[0]
    T = N + L
    lb = L // br
    lrow = pl.BlockSpec((br, D), lambda i: (jnp.maximum(i - lb, 0), 0))

    def body(pa_ref, pd_ref, ga_ref, gd_ref, dy_ref, dpa_ref, dpd_ref, dg_ref):
        lat = pl.program_id(0) >= lb
        _, vjp = jax.vjp(_merge_fn, pa_ref[...], pd_ref[...], ga_ref[...], gd_ref[...])
        gpa, gpd, gga, ggd = vjp(dy_ref[...])
        dpa_ref[...] = gpa.astype(BF16)
        dpd_ref[...] = gpd.astype(BF16)
        dg_ref[:, :D] = jnp.where(lat, gga, 0.0).astype(BF16)
        dg_ref[:, D:] = jnp.where(lat, ggd, 0.0).astype(BF16)

    return _call(body, name="merge_bwd", out_shape=(_sds((N, D), BF16), _sds((N, D), BF16), _sds((T, C_END), BF16)),
                 grid=(T // br,),
                 in_specs=[lrow, lrow, pl.BlockSpec((br, D), lambda i: (i, C_GATE // D)),
                           pl.BlockSpec((br, D), lambda i: (i, C_GATE // D + 1)), lrow],
                 out_specs=(lrow, lrow, pl.BlockSpec((br, 2 * D), lambda i: (i, C_GATE // (2 * D)))),
                 sem=("arbitrary",))(pa, pd, proj, proj, dy)


def _resid_fwd(x, m, mod, i_g, *, name, br=256):
    R = x.shape[0]
    row = pl.BlockSpec((br, D), lambda i: (i, 0))

    def body(x_ref, m_ref, mod_ref, o_ref):
        o_ref[...] = x_ref[...] + mod_ref[i_g:i_g + 1, :] * m_ref[...]

    return _call(body, name=name, out_shape=_sds((R, D)), grid=(R // br,),
                 in_specs=[row, row, pl.BlockSpec((6, D), lambda i: (0, 0))], out_specs=row,
                 sem=("parallel",))(x, m, mod)


def _resid_bwd(dx, m, mod, i_g, *, name, br=256):
    R = dx.shape[0]
    row = pl.BlockSpec((br, D), lambda i: (i, 0))
    vec = pl.BlockSpec((1, D), lambda i: (0, 0))

    def body(dx_ref, m_ref, mod_ref, dm_ref, dg_ref):
        dxv = dx_ref[...]
        dm_ref[...] = (dxv * mod_ref[i_g:i_g + 1, :]).astype(BF16)

        @pl.when(pl.program_id(0) == 0)
        def _():
            dg_ref[...] = jnp.zeros_like(dg_ref)

        dg_ref[...] += jnp.sum(dxv * m_ref[...], axis=0, keepdims=True)

    return _call(body, name=name, out_shape=(_sds((R, D), BF16), _sds((1, D))), grid=(R // br,),
                 in_specs=[row, row, pl.BlockSpec((6, D), lambda i: (0, 0))], out_specs=(row, vec),
                 sem=("arbitrary",))(dx, m, mod)


def _ffn_fn(shifts, ug, uv, wg, wv, bg, bv):
    down, up = shifts

    def conv(x, w, b):
        return down(x) * w[0:1, :] + x * w[1:2, :] + up(x) * w[2:3, :] + b

    return _silu(conv(ug, wg, bg)) * conv(uv, wv, bv)


def _ffn_fwd(up, cw, cb, *, bw=256):
    N = up.shape[0]
    shifts = _make_shift(((0, N),))
    nb = DFF // bw

    def body(ug, uv, wg, wv, bg, bv, a_ref):
        a_ref[...] = _ffn_fn(shifts, ug[...], uv[...], wg[...], wv[...], bg[...], bv[...]).astype(BF16)

    def col(rows, off):
        return pl.BlockSpec((rows, bw), lambda j: (0, j + off))

    return _call(body, name="ffn_fwd", out_shape=_sds((N, DFF), BF16), grid=(nb,),
                 in_specs=[col(N, 0), col(N, nb), col(3, 0), col(3, nb), col(1, 0), col(1, nb)],
                 out_specs=col(N, 0), sem=("parallel",), vmem=VMEM_BIG)(up, up, cw, cw, cb, cb)


def _ffn_bwd(up, cw, cb, da, *, bw=256):
    N = up.shape[0]
    shifts = _make_shift(((0, N),))
    nb = DFF // bw

    def body(ug, uv, wg, wv, bg, bv, da_ref, dug, duv, dwg, dwv, dbg, dbv):
        _, vjp = jax.vjp(functools.partial(_ffn_fn, shifts), ug[...], uv[...], wg[...], wv[...], bg[...], bv[...])
        g = vjp(da_ref[...])
        dug[...] = g[0].astype(BF16)
        duv[...] = g[1].astype(BF16)
        dwg[...], dwv[...], dbg[...], dbv[...] = g[2], g[3], g[4], g[5]

    def col(rows, off):
        return pl.BlockSpec((rows, bw), lambda j: (0, j + off))

    half = (_sds((N, DFF), BF16), _sds((N, DFF), BF16), _sds((3, DFF)), _sds((3, DFF)), _sds((1, DFF)), _sds((1, DFF)))
    dug, duv, dwg, dwv, dbg, dbv = _call(
        body, name="ffn_bwd", out_shape=half, grid=(nb,),
        in_specs=[col(N, 0), col(N, nb), col(3, 0), col(3, nb), col(1, 0), col(1, nb), col(N, 0)],
        out_specs=(col(N, 0), col(N, 0), col(3, 0), col(3, 0), col(1, 0), col(1, 0)),
        sem=("parallel",), vmem=VMEM_BIG)(up, up, cw, cw, cb, cb, da)
    return (jnp.concatenate([dug, duv], axis=1), jnp.concatenate([dwg, dwv], axis=1),
            jnp.concatenate([dbg, dbv], axis=1))


def _head_fn(x1, dn, g2, fw, tgt):
    y = _rms(x1 + g2 * dn) * fw
    err = y - tgt
    return 0.5 * jnp.sum(jnp.mean(err * err, axis=-1))


def _head(x1, dn, mod, fw, tgt, *, br=256):
    N = x1.shape[0]
    row = pl.BlockSpec((br, D), lambda i: (i, 0))
    vec = pl.BlockSpec((1, D), lambda i: (0, 0))
    one = pl.BlockSpec((1, HD), lambda i: (0, 0))

    def body(x1_ref, dn_ref, mod_ref, fw_ref, tgt_ref, loss_ref, dx_ref, ddn_ref, dg_ref, dfw_ref):
        loss, (gx, gdn, gg, gfw) = jax.value_and_grad(_head_fn, argnums=(0, 1, 2, 3))(
            x1_ref[...], dn_ref[...], mod_ref[5:6, :], fw_ref[...], tgt_ref[...])
        dx_ref[...] = gx
        ddn_ref[...] = gdn.astype(BF16)

        @pl.when(pl.program_id(0) == 0)
        def _():
            loss_ref[...] = jnp.zeros_like(loss_ref)
            dg_ref[...] = jnp.zeros_like(dg_ref)
            dfw_ref[...] = jnp.zeros_like(dfw_ref)

        loss_ref[...] += jnp.broadcast_to(loss, (1, HD))
        dg_ref[...] += gg
        dfw_ref[...] += gfw

    return _call(body, name="head", out_shape=(_sds((1, HD)), _sds((N, D)), _sds((N, D), BF16), _sds((1, D)), _sds((1, D))),
                 grid=(N // br,), in_specs=[row, row, pl.BlockSpec((6, D), lambda i: (0, 0)), vec, row],
                 out_specs=(one, row, row, vec, vec), sem=("arbitrary",))(x1, dn, mod, fw, tgt)


def _adamw(w, g, m, v, *, name):
    shape = w.shape
    cols = shape[-1]
    rows = max(1, math.prod(shape[:-1]))
    w2, g2, m2, v2 = (t.reshape(rows, cols) for t in (w, g, m, v))
    br = 256 if rows % 256 == 0 else rows
    c1 = 1.0 - B1 ** STEP
    c2 = 1.0 - B2 ** STEP

    def body(w_ref, g_ref, m_ref, v_ref, d_ref, nm_ref, nv_ref):
        gv = g_ref[...]
        nm = B1 * m_ref[...] + (1.0 - B1) * gv
        nv = B2 * v_ref[...] + (1.0 - B2) * (gv * gv)
        d_ref[...] = -LR * ((nm / c1) / (jnp.sqrt(nv / c2) + AEPS) + WD * w_ref[...])
        nm_ref[...] = nm
        nv_ref[...] = nv

    blk = pl.BlockSpec((br, cols), lambda i: (i, 0))
    outs = _call(body, name=name, out_shape=(_sds((rows, cols)),) * 3, grid=(rows // br,),
                 in_specs=[blk] * 4, out_specs=(blk,) * 3, sem=("parallel",))(w2, g2, m2, v2)
    return tuple(t.reshape(shape) for t in outs)


def _adamw_many(items, *, name):
    k = len(items)
    shapes = [w.shape for w, _, _, _ in items]
    flat = [t.reshape(max(1, math.prod(t.shape[:-1])), t.shape[-1]) for it in items for t in it]
    c1 = 1.0 - B1 ** STEP
    c2 = 1.0 - B2 ** STEP

    def body(*refs):
        ins, outs = refs[:4 * k], refs[4 * k:]
        for i in range(k):
            w_ref, g_ref, m_ref, v_ref = ins[4 * i:4 * i + 4]
            gv = g_ref[...]
            nm = B1 * m_ref[...] + (1.0 - B1) * gv
            nv = B2 * v_ref[...] + (1.0 - B2) * (gv * gv)
            outs[3 * i][...] = -LR * ((nm / c1) / (jnp.sqrt(nv / c2) + AEPS) + WD * w_ref[...])
            outs[3 * i + 1][...] = nm
            outs[3 * i + 2][...] = nv

    res = _call(body, name=name, out_shape=tuple(_sds(flat[4 * i].shape) for i in range(k) for _ in range(3)))(*flat)
    return [tuple(res[3 * i + j].reshape(shapes[i]) for j in range(3)) for i in range(k)]


def _rope_tables(N, L):
    t = jnp.arange(N)
    pos = jnp.stack([(t // GRID_W).astype(F32), (t % GRID_W).astype(F32)], axis=1)
    inv = ROPE_THETA ** (-jnp.arange(0, HD // 2, 2, dtype=F32) / (HD // 2))
    ang = pos[:, :, None] * inv[None, None, :]
    cos = jnp.broadcast_to(jnp.cos(ang)[:, :, None, :], (N, 2, 2, HD // 4)).reshape(N, HD)
    sin = jnp.broadcast_to(jnp.sin(ang)[:, :, None, :], (N, 2, 2, HD // 4))
    sin = (sin * jnp.array([-1.0, 1.0], F32)[None, None, :, None]).reshape(N, HD)
    cos = jnp.concatenate([jnp.ones((L, HD), F32), cos], axis=0)
    sin = jnp.concatenate([jnp.zeros((L, HD), F32), sin], axis=0)
    return cos, sin


def _pad_lanes(v, off=0):
    return jnp.zeros((1, HD), F32).at[0, off:off + v.shape[0]].set(v)


def _local_step(x, ctx, tgt, mod_lat, mod_ctx, w_in, shards, small):
    N, L = x.shape[0], ctx.shape[0]
    T = N + L
    bounds = ((0, L), (L, T))
    qw, kw, gw = small["q_norm_w"], small["k_norm_w"], small["gdn_norm_w"]
    conv_w, ffn_w, ffn_b, fnw = small["conv_qkv_w"], small["ffn_conv_w"], small["ffn_conv_b"], small["final_norm_w"]
    alog = _pad_lanes(small["a_log"].reshape(-1), 2 * GH)
    dtb = _pad_lanes(small["dt_bias"].reshape(-1), 2 * GH)
    cos, sin = _rope_tables(N, L)
    bt = T
    bnl = 256 if N % 1024 else 1024

    hc = _normmod_fwd(ctx, mod_ctx, 0, 1, name="normmod_ctx")
    hx = _normmod_fwd(x, mod_lat, 0, 1, name="normmod_x")
    h1 = jnp.concatenate([hc, hx], axis=0)
    proj = _mm(h1, w_in, name="mm_in", M=T, N=C_END, K=D, tb=True, bm=bt, bn=1024)
    aq, ak, av = _aprep_fwd(proj, cos, sin, qw, kw)
    (attn, attn32, lse), (up_g,) = _attn_fwd(aq, ak, av, L, _GatherTwoLevel([shards["w_up"]]))
    gq = _gprep_fwd(proj, conv_w, 0, bounds)
    gk = _gprep_fwd(proj, conv_w, 1, bounds)
    gv = _gprep_fwd(proj, conv_w, 2, bounds)
    bl = _bl_fwd(proj, alog, dtb)
    intra, (down_g, pa_g, pd_g, out_g) = _intra_fwd(
        gq, gk, gv, bl, L, _GatherTwoLevel([shards[n] for n in ("w_down", "w_pa", "w_pd", "w_out")]))
    w_up, w_down = up_g.reshape(2 * DFF, D), down_g.reshape(DFF, D)
    w_pa, w_pd, w_out = pa_g.reshape(D, D), pd_g.reshape(D, D), out_g.reshape(D, D)
    xinv, intra = intra[6], intra[:6]
    o, states = _scan_fwd(*intra, L)
    gdn = _gout_fwd(o, proj, gw, L)
    pa = _mm(attn, w_pa, name="mm_pa", M=N, N=D, K=D, bm=bnl)
    pd = _mm(gdn, w_pd, name="mm_pd", M=N, N=D, K=D, bm=bnl)
    y = _merge_fwd(pa, pd, proj, L)
    m = _mm(y, w_out, name="mm_out", M=N, N=D, K=D, bm=bnl)
    x1 = _resid_fwd(x, m, mod_lat, 2, name="resid1")
    h2 = _normmod_fwd(x1, mod_lat, 3, 4, name="normmod_x1")
    up = _mm(h2, w_up, name="mm_up", M=N, N=2 * DFF, K=D, tb=True, bm=bnl, bn=2 * DFF // 4)
    a = _ffn_fwd(up, ffn_w, ffn_b)
    dn = _mm(a, w_down, name="mm_down", M=N, N=D, K=DFF, bm=bnl)
    loss, dx2, ddn, dg2, dfnw = _head(x1, dn, mod_lat, fnw, tgt)

    da = _mm(ddn, w_down, name="mm_down_dx", M=N, N=DFF, K=D, tb=True, bm=bnl, bn=DFF // 2)
    g_down = _mm(a, ddn, name="mm_down_dw", M=DFF, N=D, K=N, ta=True, bm=DFF // 2, out_dtype=BF16)
    dup, d_ffn_w, d_ffn_b = _ffn_bwd(up, ffn_w, ffn_b, da)
    dh2 = _mm(dup, w_up, name="mm_up_dx", M=N, N=D, K=2 * DFF, bm=bnl, bk=2 * DFF // 4)
    g_up = _mm(dup, h2, name="mm_up_dw", M=2 * DFF, N=D, K=N, ta=True, bm=2 * DFF // 4, out_dtype=BF16)
    dx1, dsh2, dsc2 = _normmod_bwd(x1, mod_lat, 3, 4, dh2, 0, dx2, name="normmod_x1_bwd")
    dm, dg1 = _resid_bwd(dx1, m, mod_lat, 2, name="resid1_bwd")
    dy = _mm(dm, w_out, name="mm_out_dx", M=N, N=D, K=D, tb=True, bm=bnl)
    g_out = _mm(y, dm, name="mm_out_dw", M=D, N=D, K=N, ta=True, out_dtype=BF16)
    dpa, dpd, dproj = _merge_bwd(pa, pd, proj, dy, L)
    dattn = _mm(dpa, w_pa, name="mm_pa_dx", M=N, N=D, K=D, tb=True, bm=bnl)
    g_pa = _mm(attn, dpa, name="mm_pa_dw", M=D, N=D, K=N, ta=True, out_dtype=BF16)
    dgdn = _mm(dpd, w_pd, name="mm_pd_dx", M=N, N=D, K=D, tb=True, bm=bnl)
    g_pd = _mm(gdn, dpd, name="mm_pd_dw", M=D, N=D, K=N, ta=True, out_dtype=BF16)
    do, dproj, dgw = _gout_bwd(o, proj, gw, dgdn, dproj, L)
    cts, recv_a = _scan_bwd(*intra, states, do, L, _Exchange([g_out.reshape(NDEV, D // NDEV, D)], True))
    (dgq, dgk, dgv, dbl), recv_b = _intra_bwd(gq, gk, gv, bl, xinv, cts, L, _Exchange(
        [g_pa.reshape(NDEV, D // NDEV, D), g_pd.reshape(NDEV, D // NDEV, D), g_up.reshape(NDEV, 2 * DFF // NDEV, D)], True))
    dproj, dwq = _gprep_bwd(proj, conv_w, 0, bounds, dgq, dproj)
    dproj, dwk = _gprep_bwd(proj, conv_w, 1, bounds, dgk, dproj)
    dproj, dwv = _gprep_bwd(proj, conv_w, 2, bounds, dgv, dproj)
    dproj, dalog, ddtb = _bl_bwd(proj, alog, dtb, dbl, dproj)
    (daq_h, dak_h, dav_h), recv_c = _attn_bwd(aq, ak, av, attn32, lse, dattn, L, _Exchange(
        [g_down.reshape(NDEV, DFF // NDEV, D)], True))
    recv = dict(zip(("w_out", "w_pa", "w_pd", "w_up", "w_down"), recv_a + recv_b + recv_c))
    dproj, dqw, dkw = _aprep_bwd(proj, cos, sin, qw, kw, daq_h, dak_h, dav_h, dproj, L)
    g_in = _mm(dproj, h1, name="mm_in_dw", M=C_END, N=D, K=T, ta=True, bm=1024, out_dtype=BF16)
    g_in = _unpad_columns(g_in).reshape(NDEV, W_END // NDEV, D)
    own_in = lax.dynamic_index_in_dim(g_in, _position()[3], axis=0, keepdims=False)
    *pending, token = _scatter_start(g_in, None, (0, D // 2), (), name="scatter_g_in_a_start")
    dh1 = _mm(dproj, w_in, name="mm_in_dx", M=T, N=D, K=C_END, bm=bt, bk=1024, after=(token,))
    grad_x, dsh1, dsc1 = _normmod_bwd(x, mod_lat, 0, 1, dh1, L, dx1, name="normmod_x_bwd")
    _, dcsh1, dcsc1 = _normmod_bwd(ctx, mod_ctx, 0, 1, dh1, 0, None, name="normmod_ctx_bwd")

    z1 = jnp.zeros((1, D), F32)
    dmod_lat = jnp.concatenate([dsh1, dsc1, dg1, dsh2, dsc2, dg2], axis=0)
    dmod_ctx = jnp.concatenate([dcsh1, dcsc1, z1, z1, z1, z1], axis=0)
    gsmall = {
        "q_norm_w": dqw, "k_norm_w": dkw, "gdn_norm_w": dgw,
        "conv_qkv_w": jnp.concatenate([dwq, dwk, dwv], axis=1),
        "a_log": dalog[0, 2 * GH:4 * GH], "dt_bias": ddtb[0, 2 * GH:4 * GH],
        "ffn_conv_w": d_ffn_w, "ffn_conv_b": d_ffn_b, "final_norm_w": dfnw,
    }
    return loss[0, 0], grad_x, (pending, own_in), recv, dmod_lat, dmod_ctx, gsmall


HBM = pl.BlockSpec(memory_space=pltpu.HBM)
ANYSPEC = pl.BlockSpec(memory_space=pl.ANY)


def _position():
    x, y, c = lax.axis_index("x"), lax.axis_index("y"), lax.axis_index("c")
    return x, y, c, 4 * x + 2 * y + c


def _peer(x, y, c, k):
    px = 1 - x if k & 4 else x
    py = 1 - y if k & 2 else y
    pc = 1 - c if k & 1 else c
    return (px, py, pc), 4 * px + 2 * py + pc


def _exchange(arrs, *, name, scatter):
    exch = _Exchange(arrs, scatter)
    n = exch.n

    def body(*refs):
        ins, outs, sems = refs[:n], refs[n:2 * n], refs[2 * n:]
        exch.start(ins, outs, sems)
        exch.finish(ins, outs, sems)

    outs = pl.pallas_call(body, name=name, out_shape=exch.out_shape, in_specs=[HBM] * n, out_specs=(HBM,) * n,
                          scratch_shapes=exch.scratch,
                          compiler_params=pltpu.CompilerParams(has_side_effects=True))(*arrs)
    return list(outs)


class _Exchange:
    def __init__(self, arrs, scatter):
        self.arrs, self.scatter, self.n = list(arrs), scatter, len(arrs)
        self.out_shape = tuple(_sds(a.shape if scatter else (NDEV,) + a.shape, a.dtype) for a in arrs)
        self.scratch = [pltpu.SemaphoreType.DMA((self.n, NDEV - 1)), pltpu.SemaphoreType.DMA((self.n, NDEV - 1)),
                        pltpu.SemaphoreType.DMA((self.n,))]

    def _copies(self, ins, outs, sems):
        send, recv, loc = sems
        x, y, c, me = _position()
        local = [pltpu.make_async_copy(ins[a].at[me] if self.scatter else ins[a], outs[a].at[me], loc.at[a])
                 for a in range(self.n)]
        remote = []
        for k in range(1, NDEV):
            peer, pid = _peer(x, y, c, k)
            for a in range(self.n):
                src = ins[a].at[pid] if self.scatter else ins[a]
                remote.append(pltpu.make_async_remote_copy(
                    src_ref=src, dst_ref=outs[a].at[me], send_sem=send.at[a, k - 1], recv_sem=recv.at[a, k - 1],
                    device_id=peer, device_id_type=MESH))
        return local, remote

    def start(self, ins, outs, sems):
        local, remote = self._copies(ins, outs, sems)
        for cp in local + remote:
            cp.start()

    def finish(self, ins, outs, sems):
        local, remote = self._copies(ins, outs, sems)
        for cp in remote:
            cp.wait()
        for cp in local:
            cp.wait()


class _GatherTwoLevel:
    scatter = False

    def __init__(self, arrs):
        self.arrs, self.n = list(arrs), len(arrs)
        self.out_shape = tuple(_sds((NDEV,) + a.shape, a.dtype) for a in arrs)
        self.scratch = [pltpu.SemaphoreType.DMA((self.n, NDEV - 1)), pltpu.SemaphoreType.DMA((self.n, NDEV - 1)),
                        pltpu.SemaphoreType.DMA((self.n,))]

    def _parts(self, ins, outs, sems):
        send, recv, loc = sems
        x, y, c, _ = _position()
        me, sibling = (x, y, c), (x, y, 1 - c)
        chips = [(1 - x, y), (x, 1 - y), (1 - x, 1 - y)]
        parts = []
        for a in range(self.n):
            slot = lambda px, py, pc, a=a: outs[a].at[4 * px + 2 * py + pc]

            def copy(k, owner, to, src=None, a=a, slot=slot):
                return pltpu.make_async_remote_copy(
                    src_ref=slot(*owner) if src is None else src, dst_ref=slot(*owner), send_sem=send.at[a, k],
                    recv_sem=recv.at[a, k], device_id=to, device_id_type=MESH)

            parts.append(dict(
                mine=pltpu.make_async_copy(ins[a], slot(*me), loc.at[a]),
                first=[copy(0, me, sibling, src=ins[a])] + [copy(1 + j, me, (*ch, c), src=ins[a]) for j, ch in enumerate(chips)],
                arrive=[copy(1 + j, (*ch, c), me) for j, ch in enumerate(chips)],
                passed=[copy(4 + j, (*ch, c), sibling) for j, ch in enumerate(chips)],
                rest=[copy(0, sibling, me)] + [copy(4 + j, (*ch, 1 - c), me) for j, ch in enumerate(chips)]))
        return parts

    def start(self, ins, outs, sems):
        for p in self._parts(ins, outs, sems):
            p["mine"].start()
            for cp in p["first"]:
                cp.start()

    def middle(self, ins, outs, sems):
        for p in self._parts(ins, outs, sems):
            for got, fwd in zip(p["arrive"], p["passed"]):
                got.wait_recv()
                fwd.start()

    def finish(self, ins, outs, sems):
        for p in self._parts(ins, outs, sems):
            for cp in p["rest"]:
                cp.wait_recv()
            for cp in p["first"] + p["passed"]:
                cp.wait_send()
            p["mine"].wait()


def _gather_two_level(blocks, *, name):
    exch = _GatherTwoLevel(blocks)
    n = exch.n

    def body(*refs):
        ins, outs, sems = refs[:n], refs[n:2 * n], refs[2 * n:]
        exch.start(ins, outs, sems)
        exch.middle(ins, outs, sems)
        exch.finish(ins, outs, sems)

    outs = pl.pallas_call(body, name=name, out_shape=exch.out_shape, in_specs=[HBM] * n, out_specs=(HBM,) * n,
                          scratch_shapes=exch.scratch,
                          compiler_params=pltpu.CompilerParams(has_side_effects=True))(*blocks)
    return list(outs)


SEM = pl.BlockSpec(memory_space=pltpu.SEMAPHORE)


def _scatter_copies(src_ref, land_ref, send_sems, recv_sems, cols):
    x, y, c, me = _position()
    span = (slice(None), pl.ds(*cols))
    copies = []
    for k in range(1, NDEV):
        peer, pid = _peer(x, y, c, k)
        copies.append(pltpu.make_async_remote_copy(
            src_ref=src_ref.at[pid].at[span], dst_ref=land_ref.at[me].at[span], send_sem=send_sems.at[k - 1],
            recv_sem=recv_sems.at[k - 1], device_id=peer, device_id_type=MESH))
    return copies


SPLIT_EFFECT = pltpu.SideEffectType.DATAFLOW_SIDE_EFFECTING


def _scatter_start(parts, land, cols, after, *, name):
    na = len(after)
    if land is None:
        land = lax.empty(parts.shape, parts.dtype)

    def body(src_ref, land_ref, *rest):
        send_sems, recv_sems, _, _, token = rest[na:]
        for cp in _scatter_copies(src_ref, land_ref, send_sems, recv_sems, cols):
            cp.start()
        token[...] = jnp.zeros_like(token)

    return pl.pallas_call(
        body, name=name,
        out_shape=(pltpu.SemaphoreType.DMA((NDEV - 1,)), pltpu.SemaphoreType.DMA((NDEV - 1,)),
                   pltpu.HBM(parts.shape, parts.dtype), pltpu.HBM(parts.shape, parts.dtype), _sds((8, HD))),
        in_specs=(HBM, HBM) + (pl.BlockSpec(memory_space=pl.ANY),) * na,
        out_specs=(SEM, SEM, HBM, HBM, pl.BlockSpec(memory_space=pltpu.VMEM)),
        input_output_aliases={0: 2, 1: 3}, compiler_params=pltpu.CompilerParams(has_side_effects=SPLIT_EFFECT),
    )(pltpu.with_memory_space_constraint(parts, pltpu.HBM), pltpu.with_memory_space_constraint(land, pltpu.HBM), *after)


def _scatter_wait(send_sems, recv_sems, src_thru, land_thru, cols, after, *, name):
    na = len(after)

    def body(src_ref, land_ref, send_sems, recv_sems, *rest):
        for cp in _scatter_copies(src_ref, land_ref, send_sems, recv_sems, cols):
            cp.wait_send()
            cp.wait_recv()

    return pl.pallas_call(
        body, name=name,
        out_shape=(pltpu.HBM(src_thru.shape, src_thru.dtype), pltpu.HBM(land_thru.shape, land_thru.dtype)),
        in_specs=(HBM, HBM, SEM, SEM) + (pl.BlockSpec(memory_space=pl.ANY),) * na, out_specs=(HBM, HBM),
        input_output_aliases={0: 0, 1: 1}, compiler_params=pltpu.CompilerParams(has_side_effects=SPLIT_EFFECT),
    )(src_thru, land_thru, send_sems, recv_sems, *after)


def _cast_bf16(w, *, name):
    rows, cols = w.shape
    br = 128 if rows % 128 == 0 else rows

    def body(w_ref, o_ref):
        o_ref[...] = w_ref[...].astype(BF16)

    blk = pl.BlockSpec((br, cols), lambda i: (i, 0))
    return _call(body, name=name, out_shape=_sds((rows, cols), BF16), grid=(rows // br,), in_specs=[blk],
                 out_specs=blk, sem=("parallel",))(w)


def _sum_slots(a, *, name):
    _, R, C = a.shape

    def body(a_ref, o_ref):
        s = a_ref[0]
        for d in range(1, NDEV):
            s = s + a_ref[d]
        o_ref[...] = s

    return _call(body, name=name, out_shape=_sds((R, C)))(a)


MODROWS = 16


def _mod_fwd(c9, w, b):
    cols = w.shape[1]

    def body(c_ref, w_ref, b_ref, o_ref):
        o_ref[...] = _nn(_silu(c_ref[...]), w_ref[...]) + b_ref[...]

    return _call(body, name="mod_fwd", out_shape=_sds((MODROWS, cols)))(c9, w, b)


def _mod_bwd(c9, dmy, dall, w):
    cols = w.shape[1]

    def body(c_ref, dmy_ref, dall_ref, w_ref, gw_ref, gb_ref, cp_ref):
        sc = _silu(c_ref[...])
        rows = lax.broadcasted_iota(jnp.int32, (MODROWS, 1), 0)
        d = dmy_ref[...]
        d_ctx = jnp.where(rows == NDEV, d, 0.0)
        sc_ctx = jnp.where(rows == NDEV, sc, 0.0)
        outer = lax.dot_general(sc_ctx, d_ctx, (((0,), (0,)), ((), ())), precision=HI, preferred_element_type=F32)
        gw_ref[...] = _tn(jnp.where(rows < NDEV, sc, 0.0), jnp.where(rows < NDEV, d, 0.0)) + outer
        gb_ref[...] = jnp.sum(dall_ref[...], axis=0, keepdims=True)
        cp_ref[...] = jnp.sum(_nt(d_ctx, w_ref[...]), axis=0, keepdims=True)

    return _call(body, name="mod_bwd", out_shape=(_sds((D, cols)), _sds((1, 6 * D)), _sds((1, D))),
                 vmem=VMEM_BIG)(c9, dmy, dall, w)


def _cctx_finish(parts, c_ctx, after):
    VM = pl.BlockSpec(memory_space=pltpu.VMEM)

    def body(p_ref, c_ref, *rest):
        o_ref = rest[-1]
        s = p_ref[0]
        for d in range(1, NDEV):
            s = s + p_ref[d]
        _, vjp = jax.vjp(_silu, c_ref[...])
        o_ref[...] = vjp(s)[0]

    return _call(body, name="cctx_finish", out_shape=_sds((1, D)),
                 in_specs=[VM, VM] + [pl.BlockSpec(memory_space=pl.ANY)] * len(after))(parts, c_ctx, *after)


def _adamw_recv(w, recv, m, v, *, name, own=None):
    rows, cols = w.shape
    bc = 256
    c1 = 1.0 - B1 ** STEP
    c2 = 1.0 - B2 ** STEP
    has_own = own is not None

    def body(w_ref, r_ref, m_ref, v_ref, *rest):
        g_ref, d_ref, nm_ref, nv_ref = rest[-4:]
        me = _position()[3]

        def slot(d):
            return jnp.where(me == d, rest[0][...], r_ref[d]) if has_own else r_ref[d]

        gv = slot(0).astype(F32)
        for d in range(1, NDEV):
            gv = gv + slot(d).astype(F32)
        nm = B1 * m_ref[...] + (1.0 - B1) * gv
        nv = B2 * v_ref[...] + (1.0 - B2) * (gv * gv)
        g_ref[...] = gv
        d_ref[...] = -LR * ((nm / c1) / (jnp.sqrt(nv / c2) + AEPS) + WD * w_ref[...])
        nm_ref[...] = nm
        nv_ref[...] = nv

    blk = pl.BlockSpec((rows, bc), lambda j: (0, j))
    return _call(body, name=name, out_shape=(_sds((rows, cols)),) * 4, grid=(cols // bc,),
                 in_specs=[blk, pl.BlockSpec((NDEV, rows, bc), lambda j: (0, 0, j)), blk, blk] + [blk] * has_own,
                 out_specs=(blk,) * 4, sem=("parallel",), vmem=VMEM_BIG)(w, recv, m, v, *([own] if has_own else []))


P_LAT, P_CTX, P_FNW, P_FFNB, P_CONV, P_FFNW, P_MISC, P_ROWS = 0, 8, 16, 24, 32, 48, 72, 80


def _rows_of(v, nrows):
    flat = v.reshape(-1)
    return jnp.pad(flat, (0, nrows * D - flat.shape[0])).reshape(nrows, D)


def _by_columns(g):
    n, r, c = g.shape
    return jnp.transpose(g, (1, 0, 2)).reshape(r, n * c)


def kernel(x, c, ctx, c_ctx, w_mod, b_mod, w_in, q_norm_w, k_norm_w, conv_qkv_w, a_log, dt_bias, gdn_norm_w, w_pa, w_pd, w_out, w_up, ffn_conv_w, ffn_conv_b, w_down, final_norm_w, loss_target, m_c_ctx, m_w_mod, m_b_mod, m_w_in, m_q_norm_w, m_k_norm_w, m_conv_qkv_w, m_a_log, m_dt_bias, m_gdn_norm_w, m_w_pa, m_w_pd, m_w_out, m_w_up, m_ffn_conv_w, m_ffn_conv_b, m_w_down, m_final_norm_w, v_c_ctx, v_w_mod, v_b_mod, v_w_in, v_q_norm_w, v_k_norm_w, v_conv_qkv_w, v_a_log, v_dt_bias, v_gdn_norm_w, v_w_pa, v_w_pd, v_w_out, v_w_up, v_ffn_conv_w, v_ffn_conv_b, v_w_down, v_final_norm_w):
    _, _, _, me = _position()
    mcols = w_mod.shape[2]

    transposed = ("w_in", "w_up")
    big = {"w_in": w_in[0].T, "w_pa": w_pa[0], "w_pd": w_pd[0], "w_out": w_out[0], "w_up": w_up[0].T, "w_down": w_down[0]}
    names = list(big)
    shards = {n: _cast_bf16(big[n], name="cast_" + n) for n in names}
    w_in_g, c_all, conv_g, ffnw_g = _gather_two_level([shards["w_in"], c, conv_qkv_w[0], ffn_conv_w[0]],
                                                      name="gather_w_in")
    w_in_full = w_in_g.reshape(W_END, D)
    w_in_pad = _pad_columns(w_in_full)

    c9 = jnp.concatenate([c_all.reshape(NDEV, D), jnp.pad(c_ctx[None], ((0, MODROWS - NDEV - 1), (0, 0)))], axis=0)
    b_loc = lax.dynamic_slice(b_mod, (0, me * mcols), (1, mcols))
    mod_all, = _exchange([_mod_fwd(c9, w_mod[0], b_loc)], name="gather_mod", scatter=False)
    mod_lat = lax.dynamic_index_in_dim(mod_all, me, axis=1, keepdims=False).reshape(6, D)
    mod_ctx = mod_all[:, NDEV, :].reshape(6, D)

    small = {"q_norm_w": q_norm_w, "k_norm_w": k_norm_w, "gdn_norm_w": gdn_norm_w, "a_log": a_log, "dt_bias": dt_bias,
             "conv_qkv_w": _by_columns(conv_g), "ffn_conv_w": _by_columns(ffnw_g), "ffn_conv_b": ffn_conv_b,
             "final_norm_w": final_norm_w[None]}
    loss_me, grad_x, (pending_in, own_in), recv, dmod_lat, dmod_ctx, gs = _local_step(
        x[0], ctx[0], loss_target[0], mod_lat, mod_ctx, w_in_pad, shards, small)

    moments = {"w_in": (m_w_in, v_w_in), "w_pa": (m_w_pa, v_w_pa), "w_pd": (m_w_pd, v_w_pd),
               "w_out": (m_w_out, v_w_out), "w_up": (m_w_up, v_w_up), "w_down": (m_w_down, v_w_down)}
    res = {}
    def finish(n, outs):
        return tuple((t.T if n in transposed else t)[None] for t in outs)

    def moment(t, n):
        return t[0].T if n in transposed else t[0]

    for n in recv:
        res[n] = finish(n, _adamw_recv(big[n], recv[n], moment(moments[n][0], n), moment(moments[n][1], n),
                                       name="adamw_" + n))

    misc = jnp.concatenate([gs["q_norm_w"][0], gs["k_norm_w"][0], gs["gdn_norm_w"][0], gs["a_log"], gs["dt_bias"],
                            loss_me[None]])
    pack = jnp.concatenate([_rows_of(dmod_lat, P_CTX - P_LAT), _rows_of(dmod_ctx, P_FNW - P_CTX),
                            _rows_of(gs["final_norm_w"], P_FFNB - P_FNW), _rows_of(gs["ffn_conv_b"], P_CONV - P_FFNB),
                            _rows_of(gs["conv_qkv_w"], P_FFNW - P_CONV), _rows_of(gs["ffn_conv_w"], P_MISC - P_FFNW),
                            _rows_of(misc, P_ROWS - P_MISC)], axis=0)
    pack_all, = _exchange([pack], name="gather_pack", scatter=False)
    tot = _sum_slots(pack_all, name="sum_pack")
    dall = jnp.concatenate([pack_all[:, P_LAT:P_LAT + 6, :].reshape(NDEV, 6 * D),
                            jnp.pad(tot[P_CTX:P_CTX + 6].reshape(1, 6 * D), ((0, MODROWS - NDEV - 1), (0, 0)))], axis=0)
    dmy = lax.dynamic_slice(dall, (0, me * mcols), (MODROWS, mcols))
    g_w_mod, g_b_mod, cpart = _mod_bwd(c9, dmy, dall, w_mod[0])
    cparts, = _exchange([cpart], name="gather_cctx", scatter=False)
    sems_a, land = pending_in[:2], pending_in[3]
    *sems_b, g_in_thru, land, token_b = _scatter_start(pending_in[2], land, (D // 2, D // 2), (cparts,),
                                                       name="scatter_g_in_b_start")
    g_c_ctx = _cctx_finish(cparts, c_ctx[None], (token_b,))[0]

    nconv, nffn = 3 * GH * HD, 2 * DFF
    conv_tot = tot[P_CONV:P_FFNW].reshape(-1)[:3 * nconv].reshape(3, nconv)
    ffnw_tot = tot[P_FFNW:P_MISC].reshape(-1)[:3 * nffn].reshape(3, nffn)
    mrow = tot[P_MISC]
    grads = {
        "c_ctx": g_c_ctx, "w_mod": g_w_mod[None], "b_mod": g_b_mod,
        "q_norm_w": mrow[None, 0:HD], "k_norm_w": mrow[None, HD:2 * HD], "gdn_norm_w": mrow[None, 2 * HD:3 * HD],
        "conv_qkv_w": lax.dynamic_slice(conv_tot, (0, me * (nconv // NDEV)), (3, nconv // NDEV))[None],
        "a_log": mrow[3 * HD:3 * HD + 2 * GH].reshape(1, 2, GH),
        "dt_bias": mrow[3 * HD + 2 * GH:3 * HD + 4 * GH].reshape(1, 2, GH),
        "ffn_conv_w": lax.dynamic_slice(ffnw_tot, (0, me * (nffn // NDEV)), (3, nffn // NDEV))[None],
        "ffn_conv_b": tot[P_FFNB:P_CONV].reshape(-1)[:nffn][None],
        "final_norm_w": tot[P_FNW],
    }
    loss = mrow[3 * HD + 4 * GH]
    given = {"c_ctx": (c_ctx, m_c_ctx, v_c_ctx), "w_mod": (w_mod, m_w_mod, v_w_mod), "b_mod": (b_mod, m_b_mod, v_b_mod),
             "q_norm_w": (q_norm_w, m_q_norm_w, v_q_norm_w), "k_norm_w": (k_norm_w, m_k_norm_w, v_k_norm_w),
             "conv_qkv_w": (conv_qkv_w, m_conv_qkv_w, v_conv_qkv_w), "a_log": (a_log, m_a_log, v_a_log),
             "dt_bias": (dt_bias, m_dt_bias, v_dt_bias), "gdn_norm_w": (gdn_norm_w, m_gdn_norm_w, v_gdn_norm_w),
             "ffn_conv_w": (ffn_conv_w, m_ffn_conv_w, v_ffn_conv_w), "ffn_conv_b": (ffn_conv_b, m_ffn_conv_b, v_ffn_conv_b),
             "final_norm_w": (final_norm_w, m_final_norm_w, v_final_norm_w)}
    res["w_mod"] = (grads["w_mod"],) + _adamw(w_mod, grads["w_mod"], m_w_mod, v_w_mod, name="adamw_w_mod")
    small_names = [n for n in given if n != "w_mod"]
    updates = _adamw_many([(given[n][0], grads[n], given[n][1], given[n][2]) for n in small_names], name="adamw_small")
    for n, upd in zip(small_names, updates):
        res[n] = (grads[n],) + upd

    g_in_thru, land = _scatter_wait(*sems_a, g_in_thru, land, (0, D // 2), [res[n][1] for n in res],
                                    name="scatter_g_in_a_wait")
    _, land = _scatter_wait(*sems_b, g_in_thru, land, (D // 2, D // 2), (), name="scatter_g_in_b_wait")
    res["w_in"] = finish("w_in", _adamw_recv(big["w_in"], land, moment(m_w_in, "w_in"), moment(v_w_in, "w_in"),
                                             name="adamw_w_in", own=own_in))

    order = ["c_ctx", "w_mod", "b_mod", "w_in", "q_norm_w", "k_norm_w", "conv_qkv_w", "a_log", "dt_bias", "gdn_norm_w",
             "w_pa", "w_pd", "w_out", "w_up", "ffn_conv_w", "ffn_conv_b", "w_down", "final_norm_w"]
    return (loss, grad_x[None], *[res[n][0] for n in order], *[res[n][1] for n in order],
            *[res[n][2] for n in order], *[res[n][3] for n in order])
```

```python
import functools
import math

import jax
import jax.numpy as jnp
from jax import lax
from jax.experimental import pallas as pl
from jax.experimental.pallas import tpu as pltpu

F32 = jnp.float32
BF16 = jnp.bfloat16
HI = lax.Precision.HIGHEST
MESH = pl.DeviceIdType.MESH

NDEV = 8
D = 1024
HD = 128
AH, AKV, GRP = 8, 2, 4
GH = 8
CH = 64
DFF = 2816
GRID_W = 64
EPS = 1e-6
ROPE_THETA = 10000.0
LOG2E = math.log2(math.e)
C_KV, C_AQ, C_QKV, C_BL, C_Z, C_GATE, C_END = 0, 512, 1536, 4608, 5120, 6144, 8192
W_QKV, W_AQ, W_Z, W_END = 512, 3616, 4640, 7712


def _pad_columns(w):
    zeros = jnp.zeros((C_Z - C_QKV - (W_AQ - W_QKV), D), w.dtype)
    return jnp.concatenate([w[:W_QKV], w[W_AQ:W_Z], w[W_QKV:W_AQ], zeros, w[W_Z:]], axis=0)


def _unpad_columns(g):
    return jnp.concatenate([g[:C_AQ], g[C_QKV:C_QKV + W_AQ - W_QKV], g[C_AQ:C_QKV], g[C_Z:]], axis=0)
LR, B1, B2, AEPS, WD, STEP = 0.001, 0.9, 0.999, 1e-08, 0.01, 10
VMEM_BIG = 56 * 1024 * 1024
INTRA_FWD_CHUNKS = 36
INTRA_BWD_CHUNKS = 36


def _call(body, *, name, out_shape, grid=None, in_specs=None, out_specs=None, scratch=(), sem=None,
          vmem=None, aliases=None):
    params = {}
    if sem is not None:
        params["dimension_semantics"] = sem
    if vmem is not None:
        params["vmem_limit_bytes"] = vmem
    kw = {}
    if grid is not None:
        kw["grid"] = grid
    if in_specs is not None:
        kw["in_specs"] = in_specs
    if out_specs is not None:
        kw["out_specs"] = out_specs
    if aliases:
        kw["input_output_aliases"] = aliases
    return pl.pallas_call(body, name=name, out_shape=out_shape, scratch_shapes=list(scratch),
                          compiler_params=pltpu.CompilerParams(**params), **kw)


def _call_carrying(body, exch, *, name, out_shape, grid, in_specs, out_specs, scratch=(), vmem=None):
    n, nin, nout, nscr = exch.n, len(in_specs), len(out_shape), len(scratch)
    steps = math.prod(grid)
    mid = (2 * steps) // 3

    def wrapped(*refs):
        ins, cins = refs[:nin], refs[nin:nin + n]
        outs, couts = refs[nin + n:nin + n + nout], refs[nin + n + nout:nin + 2 * n + nout]
        scr, sems = refs[nin + 2 * n + nout:nin + 2 * n + nout + nscr], refs[nin + 2 * n + nout + nscr:]
        ids = [pl.program_id(i) for i in range(len(grid))]
        first = functools.reduce(jnp.logical_and, [i == 0 for i in ids])
        last = functools.reduce(jnp.logical_and, [i == g - 1 for i, g in zip(ids, grid)])

        @pl.when(first)
        def _():
            exch.start(cins, couts, sems)

        if hasattr(exch, "middle"):
            linear = functools.reduce(lambda acc, ig: acc * ig[1] + ig[0], zip(ids, grid), 0)

            @pl.when(linear == mid)
            def _():
                exch.middle(cins, couts, sems)

        body(*ins, *outs, *scr)

        @pl.when(last)
        def _():
            exch.finish(cins, couts, sems)

    params = {"dimension_semantics": ("arbitrary",) * len(grid)}
    if vmem is not None:
        params["vmem_limit_bytes"] = vmem
    fn = pl.pallas_call(wrapped, name=name, out_shape=tuple(out_shape) + exch.out_shape, grid=grid,
                        in_specs=list(in_specs) + [HBM] * n, out_specs=tuple(out_specs) + (HBM,) * n,
                        scratch_shapes=list(scratch) + exch.scratch, compiler_params=pltpu.CompilerParams(**params))

    def run(*args):
        res = fn(*args, *exch.arrs)
        return res[:nout], list(res[nout:])

    return run


def _sds(shape, dtype=F32):
    return jax.ShapeDtypeStruct(tuple(shape), dtype)


def _dot(a, b, ca, cb):
    return lax.dot_general(a.astype(BF16), b.astype(BF16), (((ca,), (cb,)), ((), ())),
                           preferred_element_type=F32)


@jax.custom_vjp
def _nn(a, b):
    return _dot(a, b, 1, 0)


@jax.custom_vjp
def _nt(a, b):
    return _dot(a, b, 1, 1)


@jax.custom_vjp
def _tn(a, b):
    return _dot(a, b, 0, 0)


_nn.defvjp(lambda a, b: (_nn(a, b), (a, b)), lambda r, g: (_nt(g, r[1]), _tn(r[0], g)))
_nt.defvjp(lambda a, b: (_nt(a, b), (a, b)), lambda r, g: (_nn(g, r[1]), _tn(g, r[0])))
_tn.defvjp(lambda a, b: (_tn(a, b), (a, b)), lambda r, g: (_nt(r[1], g), _nn(r[0], g)))


def _mdot(a, b):
    return jnp.dot(a, b, precision=lax.Precision.HIGH, preferred_element_type=F32)


def _maskdot(mask, a, cm):
    hi = a.astype(BF16)
    r = a - hi.astype(F32)
    mid = r.astype(BF16)
    lo = (r - mid.astype(F32)).astype(BF16)
    mb = mask.astype(BF16)
    dims = (((cm,), (0,)), ((), ()))
    return (lax.dot_general(mb, hi, dims, preferred_element_type=F32)
            + lax.dot_general(mb, mid, dims, preferred_element_type=F32)
            + lax.dot_general(mb, lo, dims, preferred_element_type=F32))


@jax.custom_vjp
def _mask_nn(mask, a):
    return _maskdot(mask, a, 1)


_mask_nn.defvjp(lambda mask, a: (_maskdot(mask, a, 1), mask),
                lambda mask, g: (jnp.zeros_like(mask), _maskdot(mask, g, 0)))


@jax.custom_vjp
def _saved_inverse(lmat, x):
    return x


def _saved_inverse_bwd(x, g):
    t = lax.dot_general(x, g, (((0,), (0,)), ((), ())), precision=lax.Precision.HIGH, preferred_element_type=F32)
    dl = lax.dot_general(t, x, (((1,), (1,)), ((), ())), precision=lax.Precision.HIGH, preferred_element_type=F32)
    return -dl, jnp.zeros_like(x)


_saved_inverse.defvjp(lambda lmat, x: (x, x), _saved_inverse_bwd)


def _row_ids(shape):
    return lax.broadcasted_iota(jnp.int32, shape, 0)


def _shift_rows(x, down, bounds):
    n = x.shape[0]
    rows = _row_ids(x.shape)
    y = pltpu.roll(x, 1 if down else n - 1, 0)
    edge = functools.reduce(jnp.logical_or, [rows == (s if down else e - 1) for s, e in bounds])
    return jnp.where(edge, 0.0, y)


def _make_shift(bounds):
    @jax.custom_vjp
    def down(x):
        return _shift_rows(x, True, bounds)

    @jax.custom_vjp
    def up(x):
        return _shift_rows(x, False, bounds)

    down.defvjp(lambda x: (down(x), None), lambda _, g: (up(g),))
    up.defvjp(lambda x: (up(x), None), lambda _, g: (down(g),))
    return down, up


@jax.custom_vjp
def _swap32(x):
    lane = lax.broadcasted_iota(jnp.int32, x.shape, x.ndim - 1)
    return jnp.where((lane % 64) < 32, pltpu.roll(x, HD - 32, x.ndim - 1), pltpu.roll(x, 32, x.ndim - 1))


_swap32.defvjp(lambda x: (_swap32(x), None), lambda _, g: (_swap32(g),))


def _rms(x):
    return x * lax.rsqrt(jnp.mean(x * x, axis=-1, keepdims=True) + EPS)


def _silu(x):
    return x * jax.nn.sigmoid(x)


def _mm(a, b, *, name, M, N, K, ta=False, tb=False, out_dtype=F32, bm=None, bn=None, bk=None, after=()):
    bm, bn, bk = bm or M, bn or N, bk or K
    assert M % bm == 0 and N % bn == 0 and K % bk == 0, (name, M, N, K, bm, bn, bk)
    nk = K // bk
    ca, cb = (0 if ta else 1), (1 if tb else 0)
    na = len(after)

    def body(a_ref, b_ref, *rest):
        o_ref, acc = rest[na], rest[na + 1:]
        r = _dot(a_ref[...], b_ref[...], ca, cb)
        if nk == 1:
            o_ref[...] = r.astype(out_dtype)
        else:
            acc_ref, = acc
            k = pl.program_id(2)

            @pl.when(k == 0)
            def _():
                acc_ref[...] = r

            @pl.when(k > 0)
            def _():
                acc_ref[...] += r

            @pl.when(k == nk - 1)
            def _():
                o_ref[...] = acc_ref[...].astype(out_dtype)

    a_spec = pl.BlockSpec((bk, bm), lambda i, j, k: (k, i)) if ta else pl.BlockSpec((bm, bk), lambda i, j, k: (i, k))
    b_spec = pl.BlockSpec((bn, bk), lambda i, j, k: (j, k)) if tb else pl.BlockSpec((bk, bn), lambda i, j, k: (k, j))
    return _call(body, name=name, out_shape=_sds((M, N), out_dtype), grid=(M // bm, N // bn, nk),
                 in_specs=[a_spec, b_spec] + [pl.BlockSpec(memory_space=pl.ANY)] * na,
                 out_specs=pl.BlockSpec((bm, bn), lambda i, j, k: (i, j)),
                 scratch=[pltpu.VMEM((bm, bn), F32)] if nk > 1 else [],
                 sem=("parallel", "parallel", "arbitrary"), vmem=VMEM_BIG)(a, b, *after)


def _normmod_fn(x, sh, sc):
    return _rms(x) * (1.0 + sc) + sh


def _normmod_fwd(x, mod, i_sh, i_sc, *, name, br=256):
    R = x.shape[0]

    def body(x_ref, mod_ref, o_ref):
        o_ref[...] = _normmod_fn(x_ref[...], mod_ref[i_sh:i_sh + 1, :], mod_ref[i_sc:i_sc + 1, :]).astype(BF16)

    return _call(body, name=name, out_shape=_sds((R, D), BF16), grid=(R // br,),
                 in_specs=[pl.BlockSpec((br, D), lambda i: (i, 0)), pl.BlockSpec((6, D), lambda i: (0, 0))],
                 out_specs=pl.BlockSpec((br, D), lambda i: (i, 0)), sem=("parallel",))(x, mod)


def _normmod_bwd(x, mod, i_sh, i_sc, dh, dh_off, res, *, name, br=256):
    R = x.shape[0]
    ob = dh_off // br
    has_res = res is not None

    def body(x_ref, mod_ref, dh_ref, *rest):
        if has_res:
            res_ref, dx_ref, dsh_ref, dsc_ref = rest
        else:
            dx_ref, dsh_ref, dsc_ref = rest
        sh, sc = mod_ref[i_sh:i_sh + 1, :], mod_ref[i_sc:i_sc + 1, :]
        _, vjp = jax.vjp(_normmod_fn, x_ref[...], sh, sc)
        dx, dsh, dsc = vjp(dh_ref[...])
        dx_ref[...] = dx + res_ref[...] if has_res else dx

        @pl.when(pl.program_id(0) == 0)
        def _():
            dsh_ref[...] = jnp.zeros_like(dsh_ref)
            dsc_ref[...] = jnp.zeros_like(dsc_ref)

        dsh_ref[...] += dsh
        dsc_ref[...] += dsc

    row = pl.BlockSpec((br, D), lambda i: (i, 0))
    vec = pl.BlockSpec((1, D), lambda i: (0, 0))
    ins = [row, pl.BlockSpec((6, D), lambda i: (0, 0)), pl.BlockSpec((br, D), lambda i: (i + ob, 0))]
    args = [x, mod, dh]
    if has_res:
        ins.append(row)
        args.append(res)
    return _call(body, name=name, out_shape=(_sds((R, D)), _sds((1, D)), _sds((1, D))), grid=(R // br,),
                 in_specs=ins, out_specs=(row, vec, vec), sem=("arbitrary",))(*args)


def _rope(x, cos, sin):
    return x * cos + _swap32(x) * sin


def _aprep_fn(qs, ks, cos, sin, qw, kw):
    return ([_rope(_rms(q) * qw, cos, sin) for q in qs], [_rope(_rms(k) * kw, cos, sin) for k in ks])


def _aprep_fwd(proj, cos, sin, qw, kw, *, br=256):
    T = proj.shape[0]

    def body(x_ref, cos_ref, sin_ref, qw_ref, kw_ref, q_ref, k_ref, v_ref):
        qs = [x_ref[:, C_AQ + h * HD:C_AQ + (h + 1) * HD] for h in range(AH)]
        ks = [x_ref[:, h * HD:(h + 1) * HD] for h in range(AKV)]
        qo, ko = _aprep_fn(qs, ks, cos_ref[...], sin_ref[...], qw_ref[...], kw_ref[...])
        for h in range(AH):
            q_ref[h] = qo[h].astype(BF16)
        for h in range(AKV):
            k_ref[h] = ko[h].astype(BF16)
            v_ref[h] = x_ref[:, (AKV + h) * HD:(AKV + h + 1) * HD].astype(BF16)

    tab = pl.BlockSpec((br, HD), lambda i: (i, 0))
    vec = pl.BlockSpec((1, HD), lambda i: (0, 0))
    return _call(body, name="aprep_fwd",
                 out_shape=(_sds((AH, T, HD), BF16), _sds((AKV, T, HD), BF16), _sds((AKV, T, HD), BF16)),
                 grid=(T // br,),
                 in_specs=[pl.BlockSpec((br, C_QKV), lambda i: (i, 0)), tab, tab, vec, vec],
                 out_specs=(pl.BlockSpec((AH, br, HD), lambda i: (0, i, 0)),
                            pl.BlockSpec((AKV, br, HD), lambda i: (0, i, 0)),
                            pl.BlockSpec((AKV, br, HD), lambda i: (0, i, 0))),
                 sem=("parallel",))(proj, cos, sin, qw, kw)


def _aprep_bwd(proj, cos, sin, qw, kw, dq, dk, dv, dproj, L, *, br=256):
    T = proj.shape[0]
    lb = L // br

    def body(x_ref, cos_ref, sin_ref, qw_ref, kw_ref, dq_ref, dk_ref, dv_ref, _, dx_ref, dqw_ref, dkw_ref):
        i = pl.program_id(0)
        qs = [x_ref[:, C_AQ + h * HD:C_AQ + (h + 1) * HD] for h in range(AH)]
        ks = [x_ref[:, h * HD:(h + 1) * HD] for h in range(AKV)]
        _, vjp = jax.vjp(_aprep_fn, qs, ks, cos_ref[...], sin_ref[...], qw_ref[...], kw_ref[...])
        is_lat = i >= lb
        dqs = [jnp.where(is_lat, dq_ref[h], 0.0) for h in range(AH)]
        dks = [dk_ref[h] for h in range(AKV)]
        gq, gk, _, _, gqw, gkw = vjp((dqs, dks))
        for h in range(AH):
            dx_ref[:, C_AQ + h * HD:C_AQ + (h + 1) * HD] = gq[h].astype(BF16)
        for h in range(AKV):
            dx_ref[:, h * HD:(h + 1) * HD] = gk[h].astype(BF16)
            dx_ref[:, (AKV + h) * HD:(AKV + h + 1) * HD] = dv_ref[h].astype(BF16)

        @pl.when(i == 0)
        def _():
            dqw_ref[...] = jnp.zeros_like(dqw_ref)
            dkw_ref[...] = jnp.zeros_like(dkw_ref)

        dqw_ref[...] += gqw
        dkw_ref[...] += gkw

    tab = pl.BlockSpec((br, HD), lambda i: (i, 0))
    vec = pl.BlockSpec((1, HD), lambda i: (0, 0))
    kvb = pl.BlockSpec((AKV, br, HD), lambda i: (0, i, 0))
    blk = pl.BlockSpec((br, C_QKV), lambda i: (i, 0))
    return _call(body, name="aprep_bwd", out_shape=(_sds(dproj.shape, BF16), _sds((1, HD)), _sds((1, HD))),
                 grid=(T // br,),
                 in_specs=[blk, tab, tab, vec, vec,
                           pl.BlockSpec((AH, br, HD), lambda i: (0, jnp.maximum(i - lb, 0), 0)), kvb, kvb, ANYSPEC],
                 out_specs=(blk, vec, vec), aliases={8: 0},
                 sem=("arbitrary",))(proj, cos, sin, qw, kw, dq, dk, dv, dproj)


def _attn_grad(q, k, v, o, lse2, do):
    scale = HD ** -0.5
    p = jnp.exp2(_dot(q, k, 1, 1) * (scale * LOG2E) - lse2)
    dp = _dot(do, v, 1, 1)
    ds = p * (dp - jnp.sum(do * o, axis=-1, keepdims=True)) * scale
    return _dot(ds, k, 1, 0), _dot(ds, q, 0, 0), _dot(p, do, 0, 0)


ATTN_KEYS = 256


def _attn_fwd(q, k, v, L, exch, *, bq=128):
    T = q.shape[1]
    N = T - L
    lb = L // bq
    assert T % ATTN_KEYS == 0
    scale = HD ** -0.5
    heads = range(GRP)

    def body(q_ref, k_ref, v_ref, o_ref, o32_ref, lse_ref):
        qs = [q_ref[g] for g in heads]
        m = [jnp.full((bq, 1), -jnp.inf, F32) for _ in heads]
        l = [jnp.zeros((bq, 1), F32) for _ in heads]
        acc = [jnp.zeros((bq, HD), F32) for _ in heads]
        for c in range(T // ATTN_KEYS):
            kc, vc = k_ref[c * ATTN_KEYS:(c + 1) * ATTN_KEYS, :], v_ref[c * ATTN_KEYS:(c + 1) * ATTN_KEYS, :]
            s = [_dot(qs[g], kc, 1, 1) * (scale * LOG2E) for g in heads]
            m_new = [jnp.maximum(m[g], jnp.max(s[g], axis=-1, keepdims=True)) for g in heads]
            alpha = [jnp.exp2(m[g] - m_new[g]) for g in heads]
            p = [jnp.exp2(s[g] - m_new[g]) for g in heads]
            l = [l[g] * alpha[g] + jnp.sum(p[g], axis=-1, keepdims=True) for g in heads]
            acc = [acc[g] * alpha[g] + _dot(p[g], vc, 1, 0) for g in heads]
            m = m_new
        for g in heads:
            o = acc[g] / l[g]
            o_ref[:, g * HD:(g + 1) * HD] = o.astype(BF16)
            o32_ref[:, g * HD:(g + 1) * HD] = o
            lse_ref[g] = jnp.broadcast_to(m[g] + jnp.log2(l[g]), (bq, HD))

    kvb = pl.BlockSpec((None, T, HD), lambda g, i: (g, 0, 0))
    ob = pl.BlockSpec((bq, GRP * HD), lambda g, i: (i, g))
    return _call_carrying(
        body, exch, name="attn_fwd",
        out_shape=(_sds((N, AH * HD), BF16), _sds((N, AH * HD)), _sds((AH, N, HD))), grid=(AKV, N // bq),
        in_specs=[pl.BlockSpec((GRP, bq, HD), lambda g, i: (g, i + lb, 0)), kvb, kvb],
        out_specs=(ob, ob, pl.BlockSpec((GRP, bq, HD), lambda g, i: (g, i, 0))), vmem=VMEM_BIG)(q, k, v)


def _attn_bwd(q, k, v, o32, lse, do, L, exch, *, bq=128):
    T = q.shape[1]
    N = T - L
    lb = L // bq

    def body(q_ref, k_ref, v_ref, o_ref, lse_ref, do_ref, dq_ref, dk_ref, dv_ref):
        rows = lambda r: jnp.concatenate([r[:, g * HD:(g + 1) * HD] for g in range(GRP)], axis=0)
        lse = jnp.max(lse_ref[...].reshape(GRP * bq, HD), axis=-1, keepdims=True)
        dq, dk, dv = _attn_grad(q_ref[...].reshape(GRP * bq, HD), k_ref[...], v_ref[...], rows(o_ref), lse, rows(do_ref))
        dq_ref[...] = dq.reshape(GRP, bq, HD)

        @pl.when(pl.program_id(1) == 0)
        def _():
            dk_ref[...] = jnp.zeros_like(dk_ref)
            dv_ref[...] = jnp.zeros_like(dv_ref)

        dk_ref[...] += dk
        dv_ref[...] += dv

    kvb = pl.BlockSpec((None, T, HD), lambda g, i: (g, 0, 0))
    qb = pl.BlockSpec((GRP, bq, HD), lambda g, i: (g, i + lb, 0))
    hb = pl.BlockSpec((GRP, bq, HD), lambda g, i: (g, i, 0))
    ob = pl.BlockSpec((bq, GRP * HD), lambda g, i: (i, g))
    return _call_carrying(body, exch, name="attn_bwd",
                          out_shape=(_sds((AH, N, HD)), _sds((AKV, T, HD)), _sds((AKV, T, HD))), grid=(AKV, N // bq),
                          in_specs=[qb, kvb, kvb, ob, hb, ob], out_specs=(hb, kvb, kvb),
                          vmem=VMEM_BIG)(q, k, v, o32, lse, do)


def _gprep_fn(kind, shifts, x, w):
    down, up = shifts
    y = down(x) * w[0:1, :] + x * w[1:2, :] + up(x) * w[2:3, :]
    a = _silu(y)
    if kind == 2:
        return a
    a = a * lax.rsqrt(jnp.sum(a * a, axis=-1, keepdims=True) + EPS)
    return a * (HD ** -0.5) if kind == 0 else a


def _gprep_fwd(proj, conv_w, kind, bounds):
    T = proj.shape[0]
    shifts = _make_shift(bounds)
    cb = C_QKV // HD + kind * GH

    def body(x_ref, w_ref, o_ref):
        o_ref[...] = _gprep_fn(kind, shifts, x_ref[...], w_ref[...])

    return _call(body, name=f"gprep_fwd{kind}", out_shape=_sds((GH, T, HD)), grid=(GH,),
                 in_specs=[pl.BlockSpec((T, HD), lambda h: (0, cb + h)),
                           pl.BlockSpec((3, HD), lambda h: (0, kind * GH + h))],
                 out_specs=pl.BlockSpec((None, T, HD), lambda h: (h, 0, 0)), sem=("parallel",))(proj, conv_w)


def _gprep_bwd(proj, conv_w, kind, bounds, dy, dproj):
    T = proj.shape[0]
    shifts = _make_shift(bounds)
    cb = C_QKV // HD + kind * GH

    def body(x_ref, w_ref, dy_ref, _, dx_ref, dw_ref):
        _, vjp = jax.vjp(functools.partial(_gprep_fn, kind, shifts), x_ref[...], w_ref[...])
        dx, dw = vjp(dy_ref[0] + dy_ref[1])
        dx_ref[...] = dx.astype(BF16)
        dw_ref[...] = dw

    return _call(body, name=f"gprep_bwd{kind}", out_shape=(_sds(dproj.shape, BF16), _sds((3, GH * HD))), grid=(GH,),
                 in_specs=[pl.BlockSpec((T, HD), lambda h: (0, cb + h)),
                           pl.BlockSpec((3, HD), lambda h: (0, kind * GH + h)),
                           pl.BlockSpec((2, None, T, HD), lambda h: (0, h, 0, 0)), ANYSPEC],
                 out_specs=(pl.BlockSpec((T, HD), lambda h: (0, cb + h)), pl.BlockSpec((3, HD), lambda h: (0, h))),
                 aliases={3: 0}, sem=("parallel",))(proj, conv_w, dy, dproj)


def _bl_fn(x, alog, dtb):
    lane = lax.broadcasted_iota(jnp.int32, x.shape, 1)
    beta = jax.nn.sigmoid(x)
    z = x + dtb
    sp = jnp.maximum(z, 0.0) + jnp.log1p(jnp.exp(-jnp.abs(z)))
    la = -jnp.exp(alog) * sp
    return jnp.where(lane < 2 * GH, beta, jnp.where(lane < 4 * GH, la, 0.0))


def _bl_fwd(proj, alog, dtb, *, br=256):
    T = proj.shape[0]

    def body(x_ref, a_ref, d_ref, o_ref):
        o_ref[...] = _bl_fn(x_ref[...], a_ref[...], d_ref[...])

    vec = pl.BlockSpec((1, HD), lambda i: (0, 0))
    return _call(body, name="bl_fwd", out_shape=_sds((T, HD)), grid=(T // br,),
                 in_specs=[pl.BlockSpec((br, HD), lambda i: (i, C_BL // HD)), vec, vec],
                 out_specs=pl.BlockSpec((br, HD), lambda i: (i, 0)), sem=("parallel",))(proj, alog, dtb)


def _bl_bwd(proj, alog, dtb, dbl, dproj, *, br=256):
    T = proj.shape[0]
    wide = C_Z - C_BL

    def body(x_ref, a_ref, d_ref, g_ref, _, dx_ref, da_ref, dd_ref):
        g = g_ref[0, 0]
        for d in range(2):
            for h in range(GH):
                if d or h:
                    g = g + g_ref[d, h]
        _, vjp = jax.vjp(_bl_fn, x_ref[...], a_ref[...], d_ref[...])
        dx, da, dd = vjp(g)
        dx_ref[:, :HD] = dx.astype(BF16)
        dx_ref[:, HD:] = jnp.zeros((br, wide - HD), BF16)

        @pl.when(pl.program_id(0) == 0)
        def _():
            da_ref[...] = jnp.zeros_like(da_ref)
            dd_ref[...] = jnp.zeros_like(dd_ref)

        da_ref[...] += da
        dd_ref[...] += dd

    vec = pl.BlockSpec((1, HD), lambda i: (0, 0))
    return _call(body, name="bl_bwd", out_shape=(_sds(dproj.shape, BF16), _sds((1, HD)), _sds((1, HD))), grid=(T // br,),
                 in_specs=[pl.BlockSpec((br, HD), lambda i: (i, C_BL // HD)), vec, vec,
                           pl.BlockSpec((2, GH, br, HD), lambda i: (0, 0, i, 0)), ANYSPEC],
                 out_specs=(pl.BlockSpec((br, wide), lambda i: (i, C_BL // wide)), vec, vec), aliases={4: 0},
                 sem=("arbitrary",))(proj, alog, dtb, dbl, dproj)


def _chunk_masks(d):
    ii = lax.broadcasted_iota(jnp.int32, (CH, CH), 0)
    jj = lax.broadcasted_iota(jnp.int32, (CH, CH), 1)
    eye = (ii == jj).astype(F32)
    before = jnp.where(d == 0, (jj < ii).astype(F32), (jj > ii).astype(F32))
    return before, before + eye, eye


def _same_block(b):
    ii = lax.broadcasted_iota(jnp.int32, (CH, CH), 0)
    jj = lax.broadcasted_iota(jnp.int32, (CH, CH), 1)
    shift = b.bit_length() - 1
    return (jnp.right_shift(ii, shift) == jnp.right_shift(jj, shift)).astype(F32)


def _intra_fn(masks, sel_b, sel_l, qs, ks, vs, bls, xs=None):
    before, ateq, eye = masks
    inc = ateq > 0.0
    each = lambda f, *ls: [f(*t) for t in zip(*ls)]
    beta = each(lambda bl: jnp.sum(bl * sel_b, axis=-1, keepdims=True), bls)
    la = each(lambda bl: jnp.sum(bl * sel_l, axis=-1, keepdims=True), bls)
    gam = each(lambda a: _mask_nn(ateq, jnp.broadcast_to(a, (CH, HD))), la)
    gi = each(lambda g: g[:, :CH], gam)
    gj = each(lambda g: jnp.transpose(g)[:CH, :], gam)
    kq = each(lambda k, q: _nt(jnp.concatenate([k, q], axis=0), k), ks, qs)
    kk = each(lambda t: t[:CH], kq)
    qk = each(lambda t: t[CH:], kq)
    dec = each(lambda a, b: jnp.where(inc, jnp.exp(jnp.where(inc, a - b, 0.0)), 0.0), gi, gj)
    lmat = each(lambda b, d, m: before * (b * d * m), beta, dec, kk)
    if xs is None:
        same = lambda b: _same_block(b)
        l8 = each(lambda m: m * same(8), lmat)
        x = each(lambda m: eye - m, l8)
        p2 = each(lambda m: _mdot(m, m), l8)
        y = each(lambda a, b: _mdot(jnp.concatenate([a, b], axis=0), b), x, p2)
        x = each(lambda a, t: a + t[:CH], x, y)
        x = each(lambda a, t: a + _mdot(a, t[CH:]), x, y)
        for b in (8, 16, 32):
            below = same(2 * b) - same(b)
            x = each(lambda a, m: a - _mdot(a, _mdot(m * below, a)), x, lmat)
    else:
        x = each(_saved_inverse, lmat, xs)
    eg = each(jnp.exp, gam)
    uw = each(lambda a, b, v, e, k: _mdot(a, jnp.concatenate([b * v, (b * e) * k], axis=1)), x, beta, vs, eg, ks)
    u = each(lambda t: t[:, :HD], uw)
    w = each(lambda t: t[:, HD:], uw)
    tot = each(lambda a: jnp.sum(a, axis=0, keepdims=True), la)
    kd = each(lambda k, t, g: k * jnp.exp(t - g), ks, tot, gam)
    gl = each(lambda t: jnp.broadcast_to(jnp.exp(t), (1, HD)), tot)
    qd = each(lambda q, e: q * e, qs, eg)
    p = each(lambda d, m: d * m, dec, qk)
    return (u, w, kd, qd, p, gl, x) if xs is None else (u, w, kd, qd, p, gl)


def _dir_head_sel(d, h):
    lane = lax.broadcasted_iota(jnp.int32, (1, HD), 1)
    return (lane == d * GH + h).astype(F32), (lane == 2 * GH + d * GH + h).astype(F32)


def _intra_specs(T, G):
    nc = T // CH
    assert nc % G == 0
    qkv = pl.BlockSpec((None, G * CH, HD), lambda d, h, c: (h, c, 0))
    bl = pl.BlockSpec((G * CH, HD), lambda d, h, c: (c, 0))
    big = pl.BlockSpec((None, None, G * CH, HD), lambda d, h, c: (d, h, c, 0))
    pm = pl.BlockSpec((None, None, G * CH, CH), lambda d, h, c: (d, h, c, 0))
    gl = pl.BlockSpec((None, None, G, 1, HD), lambda d, h, c: (d, h, c, 0, 0))
    shapes = (_sds((2, GH, T, HD)),) + (_sds((2, GH, T, HD), BF16),) * 3 + (
        _sds((2, GH, T, CH), BF16), _sds((2, GH, nc, 1, HD)), _sds((2, GH, T, CH)))
    return nc, qkv, bl, big, pm, gl, shapes


def _chunks_per_step(T, most):
    nc = T // CH
    return max(g for g in range(1, most + 1) if nc % g == 0)


def _chunk_at(g, d, nc, ncc):
    pos = _visit_pos(g, d, nc, ncc)
    return pos, pl.ds(pl.multiple_of(pos * CH, CH), CH)


def _intra_fwd(q, k, v, bl, L, exch):
    T = q.shape[1]
    G = _chunks_per_step(T, INTRA_FWD_CHUNKS)
    nc, qkv_s, bl_s, big, pm, gl_s, shapes = _intra_specs(T, G)
    assert G == nc
    ncc = L // CH

    def body(q_ref, k_ref, v_ref, bl_ref, u_ref, w_ref, kd_ref, qd_ref, p_ref, gl_ref, x_ref):
        d, h = pl.program_id(0), pl.program_id(1)
        sb, sl = _dir_head_sel(d, h)
        rows = [slice(g * CH, (g + 1) * CH) for g in range(G)]
        outs = _intra_fn(_chunk_masks(d), sb, sl, *[[r[s, :] for s in rows] for r in (q_ref, k_ref, v_ref, bl_ref)])
        for g in range(G):
            pos, at = _chunk_at(g, d, nc, ncc)
            for r, o in zip((u_ref, w_ref, kd_ref, qd_ref, p_ref, x_ref), outs[:5] + outs[6:]):
                r[at, :] = o[g].astype(r.dtype)
            gl_ref[pos] = outs[5][g]

    return _call_carrying(body, exch, name="gdn_intra_fwd", out_shape=shapes, grid=(2, GH, nc // G),
                          in_specs=[qkv_s, qkv_s, qkv_s, bl_s], out_specs=(big, big, big, big, pm, gl_s, pm))(q, k, v, bl)


def _intra_bwd(q, k, v, bl, xinv, cts, L, exch):
    T = q.shape[1]
    G = _chunks_per_step(T, INTRA_BWD_CHUNKS)
    nc, qkv_s, bl_s, big, pm, gl_s, _ = _intra_specs(T, G)
    assert G == nc
    ncc = L // CH

    def body(q_ref, k_ref, v_ref, bl_ref, x_ref, du, dw, dkd, dqd, dp, dgl, dq_ref, dk_ref, dv_ref, dbl_ref):
        d, h = pl.program_id(0), pl.program_id(1)
        sb, sl = _dir_head_sel(d, h)
        rows = [slice(g * CH, (g + 1) * CH) for g in range(G)]
        places = [_chunk_at(g, d, nc, ncc) for g in range(G)]
        fn = functools.partial(_intra_fn, _chunk_masks(d), sb, sl, xs=[x_ref[at, :] for _, at in places])
        _, vjp = jax.vjp(fn, *[[r[s, :] for s in rows] for r in (q_ref, k_ref, v_ref, bl_ref)])
        cts = tuple([r[at, :] for _, at in places] for r in (du, dw, dkd, dqd, dp)) + ([dgl[pos] for pos, _ in places],)
        grads = vjp(cts)
        for g in range(G):
            for r, o in zip((dq_ref, dk_ref, dv_ref, dbl_ref), grads):
                r[rows[g], :] = o[g]

    return _call_carrying(body, exch, name="gdn_intra_bwd", out_shape=(_sds((2, GH, T, HD)),) * 4,
                          grid=(2, GH, nc // G), in_specs=[qkv_s, qkv_s, qkv_s, bl_s, pm, big, big, big, big, pm, gl_s],
                          out_specs=(big,) * 4)(q, k, v, bl, xinv, *cts)


def _scan_fn(s, u, w, kd, qd, p, gl):
    each = lambda f, *ls: [f(*t) for t in zip(*ls)]
    ws = each(_nn, w, s)
    delta = each(lambda a, b: a - b, u, ws)
    kdd = each(_tn, kd, delta)
    s_new = each(lambda g, a, b: g * a + b, gl, s, kdd)
    qs = each(_nn, qd, s)
    pd = each(_nn, p, delta)
    return each(lambda a, b: a + b, qs, pd), s_new


SCAN_BLOCK = 4


def _visit_pos(c, d, nc, ncc):
    back = ncc - 1 - c if c < ncc else ncc + (nc - 1 - c)
    return jnp.where(d == 0, c, back)


def _scan_specs(T, L, back):
    tb = SCAN_BLOCK * CH
    assert T % tb == 0 and L % tb == 0
    nb, ncb = T // tb, L // tb
    at = (lambda t: nb - 1 - t) if back else (lambda t: t)
    big = pl.BlockSpec((2, GH, tb, HD), lambda t: (0, 0, at(t), 0))
    pm = pl.BlockSpec((2, GH, tb, CH), lambda t: (0, 0, at(t), 0))
    gl = pl.BlockSpec((2, GH, SCAN_BLOCK, 1, HD), lambda t: (0, 0, at(t), 0, 0))
    st = pl.BlockSpec((2, GH, SCAN_BLOCK, HD, HD), lambda t: (0, 0, at(t), 0, 0))

    def natural(b):
        return jnp.where(b < ncb, ncb - 1 - b, nb - 1 - (b - ncb))

    do_specs = (pl.BlockSpec((GH, tb, HD), lambda t: (0, at(t), 0)),
                pl.BlockSpec((GH, tb, HD), lambda t: (0, natural(at(t)), 0)))
    return nb, big, pm, gl, st, do_specs


SCAN_STREAMS = [(d, h) for d in (0, 1) for h in range(GH)]


def _scan_fwd(u, w, kd, qd, p, gl, L):
    T = u.shape[2]
    nb, big, pm, gl_s, st, _ = _scan_specs(T, L, False)

    def body(u_ref, w_ref, kd_ref, qd_ref, p_ref, gl_ref, o_ref, st_ref, s_scr):
        @pl.when(pl.program_id(0) == 0)
        def _():
            s_scr[...] = jnp.zeros_like(s_scr)

        s = [s_scr[d, h] for d, h in SCAN_STREAMS]
        for i in range(SCAN_BLOCK):
            rows = slice(i * CH, (i + 1) * CH)
            for (d, h), sv in zip(SCAN_STREAMS, s):
                st_ref[d, h, i] = sv
            o, s = _scan_fn(s, *[[r[d, h, rows, :].astype(F32) for d, h in SCAN_STREAMS]
                                 for r in (u_ref, w_ref, kd_ref, qd_ref, p_ref)],
                            [gl_ref[d, h, i] for d, h in SCAN_STREAMS])
            for (d, h), ov in zip(SCAN_STREAMS, o):
                o_ref[d, h, rows, :] = ov
        for (d, h), sv in zip(SCAN_STREAMS, s):
            s_scr[d, h] = sv

    return _call(body, name="gdn_scan_fwd", out_shape=(_sds((2, GH, T, HD)), _sds((2, GH, T // CH, HD, HD))),
                 grid=(nb,), in_specs=[big, big, big, big, pm, gl_s], out_specs=(big, st),
                 scratch=[pltpu.VMEM((2, GH, HD, HD), F32)], sem=("arbitrary",), vmem=VMEM_BIG)(u, w, kd, qd, p, gl)


def _scan_bwd(u, w, kd, qd, p, gl, states, do, L, exch):
    T = u.shape[2]
    nb, big, pm, gl_s, st, do_specs = _scan_specs(T, L, True)

    def body(u_ref, w_ref, kd_ref, qd_ref, p_ref, gl_ref, st_ref, do0_ref, do1_ref,
             du_ref, dw_ref, dkd_ref, dqd_ref, dp_ref, dgl_ref, ds_scr):
        @pl.when(pl.program_id(0) == 0)
        def _():
            ds_scr[...] = jnp.zeros_like(ds_scr)

        ds = [ds_scr[d, h] for d, h in SCAN_STREAMS]
        for i in reversed(range(SCAN_BLOCK)):
            rows = slice(i * CH, (i + 1) * CH)
            mirror = slice((SCAN_BLOCK - 1 - i) * CH, (SCAN_BLOCK - i) * CH)
            _, vjp = jax.vjp(_scan_fn, [st_ref[d, h, i] for d, h in SCAN_STREAMS],
                             *[[r[d, h, rows, :].astype(F32) for d, h in SCAN_STREAMS]
                               for r in (u_ref, w_ref, kd_ref, qd_ref, p_ref)],
                             [gl_ref[d, h, i] for d, h in SCAN_STREAMS])
            dos = [do0_ref[h, rows, :] if d == 0 else do1_ref[h, mirror, :] for d, h in SCAN_STREAMS]
            ds, gu, gw, gkd, gqd, gp, ggl = vjp((dos, ds))
            for n, (d, h) in enumerate(SCAN_STREAMS):
                du_ref[d, h, rows, :] = gu[n]
                dw_ref[d, h, rows, :] = gw[n]
                dkd_ref[d, h, rows, :] = gkd[n]
                dqd_ref[d, h, rows, :] = gqd[n]
                dp_ref[d, h, rows, :] = gp[n]
                dgl_ref[d, h, i] = ggl[n]
        for (d, h), dv in zip(SCAN_STREAMS, ds):
            ds_scr[d, h] = dv

    return _call_carrying(
        body, exch, name="gdn_scan_bwd",
        out_shape=(_sds((2, GH, T, HD)),) * 4 + (_sds((2, GH, T, CH)), _sds((2, GH, T // CH, 1, HD))),
        grid=(nb,), in_specs=[big, big, big, big, pm, gl_s, st, *do_specs], out_specs=(big, big, big, big, pm, gl_s),
        scratch=[pltpu.VMEM((2, GH, HD, HD), F32)], vmem=VMEM_BIG)(u, w, kd, qd, p, gl, states, do, do)


def _gout_fn(o0, o1, z, gw):
    return _rms(o0 + o1) * gw * _silu(z)


def _backward_latent(o_ref, L):
    nl = (o_ref.shape[1] - L) // CH
    return jnp.concatenate([o_ref[1, L + (nl - 1 - j) * CH:L + (nl - j) * CH, :] for j in range(nl)], axis=0)


def _gout_fwd(o, proj, gw, L):
    T = o.shape[2]
    N = T - L
    ob = pl.BlockSpec((2, None, T, HD), lambda h: (0, h, 0, 0))

    def body(o_ref, z_ref, gw_ref, y_ref):
        y_ref[...] = _gout_fn(o_ref[0, L:, :], _backward_latent(o_ref, L), z_ref[L:, :], gw_ref[...]).astype(BF16)

    return _call(body, name="gout_fwd", out_shape=_sds((N, GH * HD), BF16), grid=(GH,),
                 in_specs=[ob, pl.BlockSpec((T, HD), lambda h: (0, C_Z // HD + h)), pl.BlockSpec((1, HD), lambda h: (0, 0))],
                 out_specs=pl.BlockSpec((N, HD), lambda h: (0, h)), sem=("parallel",))(o, proj, gw)


def _gout_bwd(o, proj, gw, dy, dproj, L):
    T = o.shape[2]
    N = T - L
    ob = pl.BlockSpec((2, None, T, HD), lambda h: (0, h, 0, 0))

    def body(o_ref, z_ref, gw_ref, dy_ref, _, do_ref, dz_ref, dgw_ref):
        _, vjp = jax.vjp(_gout_fn, o_ref[0, L:, :], _backward_latent(o_ref, L), z_ref[L:, :], gw_ref[...])
        g0, _, gz, ggw = vjp(dy_ref[...])
        do_ref[:L, :] = jnp.zeros((L, HD), F32)
        do_ref[L:, :] = g0
        dz_ref[:L, :] = jnp.zeros((L, HD), BF16)
        dz_ref[L:, :] = gz.astype(BF16)

        @pl.when(pl.program_id(0) == 0)
        def _():
            dgw_ref[...] = jnp.zeros_like(dgw_ref)

        dgw_ref[...] += ggw

    zb = pl.BlockSpec((T, HD), lambda h: (0, C_Z // HD + h))
    return _call(body, name="gout_bwd", out_shape=(_sds((GH, T, HD)), _sds(dproj.shape, BF16), _sds((1, HD))),
                 grid=(GH,),
                 in_specs=[ob, zb, pl.BlockSpec((1, HD), lambda h: (0, 0)), pl.BlockSpec((N, HD), lambda h: (0, h)), ANYSPEC],
                 out_specs=(pl.BlockSpec((None, T, HD), lambda h: (h, 0, 0)), zb, pl.BlockSpec((1, HD), lambda h: (0, 0))),
                 aliases={4: 1}, sem=("arbitrary",))(o, proj, gw, dy, dproj)


def _merge_fn(pa, pd, ga, gd):
    return jax.nn.sigmoid(ga) * pa + jax.nn.sigmoid(gd) * pd


def _merge_fwd(pa, pd, proj, L, *, br=256):
    N = pa.shape[0]
    lb = L // br
    row = pl.BlockSpec((br, D), lambda i: (i, 0))

    def body(pa_ref, pd_ref, ga_ref, gd_ref, y_ref):
        y_ref[...] = _merge_fn(pa_ref[...], pd_ref[...], ga_ref[...], gd_ref[...]).astype(BF16)

    return _call(body, name="merge_fwd", out_shape=_sds((N, D), BF16), grid=(N // br,),
                 in_specs=[row, row, pl.BlockSpec((br, D), lambda i: (i + lb, C_GATE // D)),
                           pl.BlockSpec((br, D), lambda i: (i + lb, C_GATE // D + 1))],
                 out_specs=row, sem=("parallel",))(pa, pd, proj, proj)


def _merge_bwd(pa, pd, proj, dy, L, *, br=256):
    N = pa.shape[0]
    T = N + L
    lb = L // br
    lrow = pl.BlockSpec((br, D), lambda i: (jnp.maximum(i - lb, 0), 0))

    def body(pa_ref, pd_ref, ga_ref, gd_ref, dy_ref, dpa_ref, dpd_ref, dg_ref):
        lat = pl.program_id(0) >= lb
        _, vjp = jax.vjp(_merge_fn, pa_ref[...], pd_ref[...], ga_ref[...], gd_ref[...])
        gpa, gpd, gga, ggd = vjp(dy_ref[...])
        dpa_ref[...] = gpa.astype(BF16)
        dpd_ref[...] = gpd.astype(BF16)
        dg_ref[:, :D] = jnp.where(lat, gga, 0.0).astype(BF16)
        dg_ref[:, D:] = jnp.where(lat, ggd, 0.0).astype(BF16)

    return _call(body, name="merge_bwd", out_shape=(_sds((N, D), BF16), _sds((N, D), BF16), _sds((T, C_END), BF16)),
                 grid=(T // br,),
                 in_specs=[lrow, lrow, pl.BlockSpec((br, D), lambda i: (i, C_GATE // D)),
                           pl.BlockSpec((br, D), lambda i: (i, C_GATE // D + 1)), lrow],
                 out_specs=(lrow, lrow, pl.BlockSpec((br, 2 * D), lambda i: (i, C_GATE // (2 * D)))),
                 sem=("arbitrary",))(pa, pd, proj, proj, dy)


def _resid_fwd(x, m, mod, i_g, *, name, br=256):
    R = x.shape[0]
    row = pl.BlockSpec((br, D), lambda i: (i, 0))

    def body(x_ref, m_ref, mod_ref, o_ref):
        o_ref[...] = x_ref[...] + mod_ref[i_g:i_g + 1, :] * m_ref[...]

    return _call(body, name=name, out_shape=_sds((R, D)), grid=(R // br,),
                 in_specs=[row, row, pl.BlockSpec((6, D), lambda i: (0, 0))], out_specs=row,
                 sem=("parallel",))(x, m, mod)


def _resid_bwd(dx, m, mod, i_g, *, name, br=256):
    R = dx.shape[0]
    row = pl.BlockSpec((br, D), lambda i: (i, 0))
    vec = pl.BlockSpec((1, D), lambda i: (0, 0))

    def body(dx_ref, m_ref, mod_ref, dm_ref, dg_ref):
        dxv = dx_ref[...]
        dm_ref[...] = (dxv * mod_ref[i_g:i_g + 1, :]).astype(BF16)

        @pl.when(pl.program_id(0) == 0)
        def _():
            dg_ref[...] = jnp.zeros_like(dg_ref)

        dg_ref[...] += jnp.sum(dxv * m_ref[...], axis=0, keepdims=True)

    return _call(body, name=name, out_shape=(_sds((R, D), BF16), _sds((1, D))), grid=(R // br,),
                 in_specs=[row, row, pl.BlockSpec((6, D), lambda i: (0, 0))], out_specs=(row, vec),
                 sem=("arbitrary",))(dx, m, mod)


def _ffn_fn(shifts, ug, uv, wg, wv, bg, bv):
    down, up = shifts

    def conv(x, w, b):
        return down(x) * w[0:1, :] + x * w[1:2, :] + up(x) * w[2:3, :] + b

    return _silu(conv(ug, wg, bg)) * conv(uv, wv, bv)


def _ffn_fwd(up, cw, cb, *, bw=256):
    N = up.shape[0]
    shifts = _make_shift(((0, N),))
    nb = DFF // bw

    def body(ug, uv, wg, wv, bg, bv, a_ref):
        a_ref[...] = _ffn_fn(shifts, ug[...], uv[...], wg[...], wv[...], bg[...], bv[...]).astype(BF16)

    def col(rows, off):
        return pl.BlockSpec((rows, bw), lambda j: (0, j + off))

    return _call(body, name="ffn_fwd", out_shape=_sds((N, DFF), BF16), grid=(nb,),
                 in_specs=[col(N, 0), col(N, nb), col(3, 0), col(3, nb), col(1, 0), col(1, nb)],
                 out_specs=col(N, 0), sem=("parallel",), vmem=VMEM_BIG)(up, up, cw, cw, cb, cb)


def _ffn_bwd(up, cw, cb, da, *, bw=256):
    N = up.shape[0]
    shifts = _make_shift(((0, N),))
    nb = DFF // bw

    def body(ug, uv, wg, wv, bg, bv, da_ref, dug, duv, dwg, dwv, dbg, dbv):
        _, vjp = jax.vjp(functools.partial(_ffn_fn, shifts), ug[...], uv[...], wg[...], wv[...], bg[...], bv[...])
        g = vjp(da_ref[...])
        dug[...] = g[0].astype(BF16)
        duv[...] = g[1].astype(BF16)
        dwg[...], dwv[...], dbg[...], dbv[...] = g[2], g[3], g[4], g[5]

    def col(rows, off):
        return pl.BlockSpec((rows, bw), lambda j: (0, j + off))

    half = (_sds((N, DFF), BF16), _sds((N, DFF), BF16), _sds((3, DFF)), _sds((3, DFF)), _sds((1, DFF)), _sds((1, DFF)))
    dug, duv, dwg, dwv, dbg, dbv = _call(
        body, name="ffn_bwd", out_shape=half, grid=(nb,),
        in_specs=[col(N, 0), col(N, nb), col(3, 0), col(3, nb), col(1, 0), col(1, nb), col(N, 0)],
        out_specs=(col(N, 0), col(N, 0), col(3, 0), col(3, 0), col(1, 0), col(1, 0)),
        sem=("parallel",), vmem=VMEM_BIG)(up, up, cw, cw, cb, cb, da)
    return (jnp.concatenate([dug, duv], axis=1), jnp.concatenate([dwg, dwv], axis=1),
            jnp.concatenate([dbg, dbv], axis=1))


def _head_fn(x1, dn, g2, fw, tgt):
    y = _rms(x1 + g2 * dn) * fw
    err = y - tgt
    return 0.5 * jnp.sum(jnp.mean(err * err, axis=-1))


def _head(x1, dn, mod, fw, tgt, *, br=256):
    N = x1.shape[0]
    row = pl.BlockSpec((br, D), lambda i: (i, 0))
    vec = pl.BlockSpec((1, D), lambda i: (0, 0))
    one = pl.BlockSpec((1, HD), lambda i: (0, 0))

    def body(x1_ref, dn_ref, mod_ref, fw_ref, tgt_ref, loss_ref, dx_ref, ddn_ref, dg_ref, dfw_ref):
        loss, (gx, gdn, gg, gfw) = jax.value_and_grad(_head_fn, argnums=(0, 1, 2, 3))(
            x1_ref[...], dn_ref[...], mod_ref[5:6, :], fw_ref[...], tgt_ref[...])
        dx_ref[...] = gx
        ddn_ref[...] = gdn.astype(BF16)

        @pl.when(pl.program_id(0) == 0)
        def _():
            loss_ref[...] = jnp.zeros_like(loss_ref)
            dg_ref[...] = jnp.zeros_like(dg_ref)
            dfw_ref[...] = jnp.zeros_like(dfw_ref)

        loss_ref[...] += jnp.broadcast_to(loss, (1, HD))
        dg_ref[...] += gg
        dfw_ref[...] += gfw

    return _call(body, name="head", out_shape=(_sds((1, HD)), _sds((N, D)), _sds((N, D), BF16), _sds((1, D)), _sds((1, D))),
                 grid=(N // br,), in_specs=[row, row, pl.BlockSpec((6, D), lambda i: (0, 0)), vec, row],
                 out_specs=(one, row, row, vec, vec), sem=("arbitrary",))(x1, dn, mod, fw, tgt)


def _adamw(w, g, m, v, *, name):
    shape = w.shape
    cols = shape[-1]
    rows = max(1, math.prod(shape[:-1]))
    w2, g2, m2, v2 = (t.reshape(rows, cols) for t in (w, g, m, v))
    br = 256 if rows % 256 == 0 else rows
    c1 = 1.0 - B1 ** STEP
    c2 = 1.0 - B2 ** STEP

    def body(w_ref, g_ref, m_ref, v_ref, d_ref, nm_ref, nv_ref):
        gv = g_ref[...]
        nm = B1 * m_ref[...] + (1.0 - B1) * gv
        nv = B2 * v_ref[...] + (1.0 - B2) * (gv * gv)
        d_ref[...] = -LR * ((nm / c1) / (jnp.sqrt(nv / c2) + AEPS) + WD * w_ref[...])
        nm_ref[...] = nm
        nv_ref[...] = nv

    blk = pl.BlockSpec((br, cols), lambda i: (i, 0))
    outs = _call(body, name=name, out_shape=(_sds((rows, cols)),) * 3, grid=(rows // br,),
                 in_specs=[blk] * 4, out_specs=(blk,) * 3, sem=("parallel",))(w2, g2, m2, v2)
    return tuple(t.reshape(shape) for t in outs)


def _adamw_many(items, *, name):
    k = len(items)
    shapes = [w.shape for w, _, _, _ in items]
    flat = [t.reshape(max(1, math.prod(t.shape[:-1])), t.shape[-1]) for it in items for t in it]
    c1 = 1.0 - B1 ** STEP
    c2 = 1.0 - B2 ** STEP

    def body(*refs):
        ins, outs = refs[:4 * k], refs[4 * k:]
        for i in range(k):
            w_ref, g_ref, m_ref, v_ref = ins[4 * i:4 * i + 4]
            gv = g_ref[...]
            nm = B1 * m_ref[...] + (1.0 - B1) * gv
            nv = B2 * v_ref[...] + (1.0 - B2) * (gv * gv)
            outs[3 * i][...] = -LR * ((nm / c1) / (jnp.sqrt(nv / c2) + AEPS) + WD * w_ref[...])
            outs[3 * i + 1][...] = nm
            outs[3 * i + 2][...] = nv

    res = _call(body, name=name, out_shape=tuple(_sds(flat[4 * i].shape) for i in range(k) for _ in range(3)))(*flat)
    return [tuple(res[3 * i + j].reshape(shapes[i]) for j in range(3)) for i in range(k)]


def _rope_tables(N, L):
    t = jnp.arange(N)
    pos = jnp.stack([(t // GRID_W).astype(F32), (t % GRID_W).astype(F32)], axis=1)
    inv = ROPE_THETA ** (-jnp.arange(0, HD // 2, 2, dtype=F32) / (HD // 2))
    ang = pos[:, :, None] * inv[None, None, :]
    cos = jnp.broadcast_to(jnp.cos(ang)[:, :, None, :], (N, 2, 2, HD // 4)).reshape(N, HD)
    sin = jnp.broadcast_to(jnp.sin(ang)[:, :, None, :], (N, 2, 2, HD // 4))
    sin = (sin * jnp.array([-1.0, 1.0], F32)[None, None, :, None]).reshape(N, HD)
    cos = jnp.concatenate([jnp.ones((L, HD), F32), cos], axis=0)
    sin = jnp.concatenate([jnp.zeros((L, HD), F32), sin], axis=0)
    return cos, sin


def _pad_lanes(v, off=0):
    return jnp.zeros((1, HD), F32).at[0, off:off + v.shape[0]].set(v)


def _local_step(x, ctx, tgt, mod_lat, mod_ctx, w_in, shards, small):
    N, L = x.shape[0], ctx.shape[0]
    T = N + L
    bounds = ((0, L), (L, T))
    qw, kw, gw = small["q_norm_w"], small["k_norm_w"], small["gdn_norm_w"]
    conv_w, ffn_w, ffn_b, fnw = small["conv_qkv_w"], small["ffn_conv_w"], small["ffn_conv_b"], small["final_norm_w"]
    alog = _pad_lanes(small["a_log"].reshape(-1), 2 * GH)
    dtb = _pad_lanes(small["dt_bias"].reshape(-1), 2 * GH)
    cos, sin = _rope_tables(N, L)
    bt = T
    bnl = 256 if N % 1024 else 1024

    hc = _normmod_fwd(ctx, mod_ctx, 0, 1, name="normmod_ctx")
    hx = _normmod_fwd(x, mod_lat, 0, 1, name="normmod_x")
    h1 = jnp.concatenate([hc, hx], axis=0)
    proj = _mm(h1, w_in, name="mm_in", M=T, N=C_END, K=D, tb=True, bm=bt, bn=1024)
    aq, ak, av = _aprep_fwd(proj, cos, sin, qw, kw)
    (attn, attn32, lse), (up_g,) = _attn_fwd(aq, ak, av, L, _GatherTwoLevel([shards["w_up"]]))
    gq = _gprep_fwd(proj, conv_w, 0, bounds)
    gk = _gprep_fwd(proj, conv_w, 1, bounds)
    gv = _gprep_fwd(proj, conv_w, 2, bounds)
    bl = _bl_fwd(proj, alog, dtb)
    intra, (down_g, pa_g, pd_g, out_g) = _intra_fwd(
        gq, gk, gv, bl, L, _GatherTwoLevel([shards[n] for n in ("w_down", "w_pa", "w_pd", "w_out")]))
    w_up, w_down = up_g.reshape(2 * DFF, D), down_g.reshape(DFF, D)
    w_pa, w_pd, w_out = pa_g.reshape(D, D), pd_g.reshape(D, D), out_g.reshape(D, D)
    xinv, intra = intra[6], intra[:6]
    o, states = _scan_fwd(*intra, L)
    gdn = _gout_fwd(o, proj, gw, L)
    pa = _mm(attn, w_pa, name="mm_pa", M=N, N=D, K=D, bm=bnl)
    pd = _mm(gdn, w_pd, name="mm_pd", M=N, N=D, K=D, bm=bnl)
    y = _merge_fwd(pa, pd, proj, L)
    m = _mm(y, w_out, name="mm_out", M=N, N=D, K=D, bm=bnl)
    x1 = _resid_fwd(x, m, mod_lat, 2, name="resid1")
    h2 = _normmod_fwd(x1, mod_lat, 3, 4, name="normmod_x1")
    up = _mm(h2, w_up, name="mm_up", M=N, N=2 * DFF, K=D, tb=True, bm=bnl, bn=2 * DFF // 4)
    a = _ffn_fwd(up, ffn_w, ffn_b)
    dn = _mm(a, w_down, name="mm_down", M=N, N=D, K=DFF, bm=bnl)
    loss, dx2, ddn, dg2, dfnw = _head(x1, dn, mod_lat, fnw, tgt)

    da = _mm(ddn, w_down, name="mm_down_dx", M=N, N=DFF, K=D, tb=True, bm=bnl, bn=DFF // 2)
    g_down = _mm(a, ddn, name="mm_down_dw", M=DFF, N=D, K=N, ta=True, bm=DFF // 2, out_dtype=BF16)
    dup, d_ffn_w, d_ffn_b = _ffn_bwd(up, ffn_w, ffn_b, da)
    dh2 = _mm(dup, w_up, name="mm_up_dx", M=N, N=D, K=2 * DFF, bm=bnl, bk=2 * DFF // 4)
    g_up = _mm(dup, h2, name="mm_up_dw", M=2 * DFF, N=D, K=N, ta=True, bm=2 * DFF // 4, out_dtype=BF16)
    dx1, dsh2, dsc2 = _normmod_bwd(x1, mod_lat, 3, 4, dh2, 0, dx2, name="normmod_x1_bwd")
    dm, dg1 = _resid_bwd(dx1, m, mod_lat, 2, name="resid1_bwd")
    dy = _mm(dm, w_out, name="mm_out_dx", M=N, N=D, K=D, tb=True, bm=bnl)
    g_out = _mm(y, dm, name="mm_out_dw", M=D, N=D, K=N, ta=True, out_dtype=BF16)
    dpa, dpd, dproj = _merge_bwd(pa, pd, proj, dy, L)
    dattn = _mm(dpa, w_pa, name="mm_pa_dx", M=N, N=D, K=D, tb=True, bm=bnl)
    g_pa = _mm(attn, dpa, name="mm_pa_dw", M=D, N=D, K=N, ta=True, out_dtype=BF16)
    dgdn = _mm(dpd, w_pd, name="mm_pd_dx", M=N, N=D, K=D, tb=True, bm=bnl)
    g_pd = _mm(gdn, dpd, name="mm_pd_dw", M=D, N=D, K=N, ta=True, out_dtype=BF16)
    do, dproj, dgw = _gout_bwd(o, proj, gw, dgdn, dproj, L)
    cts, recv_a = _scan_bwd(*intra, states, do, L, _Exchange(
        [g_out.reshape(NDEV, D // NDEV, D), g_pa.reshape(NDEV, D // NDEV, D), g_pd.reshape(NDEV, D // NDEV, D)], True))
    (dgq, dgk, dgv, dbl), recv_b = _intra_bwd(gq, gk, gv, bl, xinv, cts, L, _Exchange(
        [g_up.reshape(NDEV, 2 * DFF // NDEV, D)], True))
    dproj, dwq = _gprep_bwd(proj, conv_w, 0, bounds, dgq, dproj)
    dproj, dwk = _gprep_bwd(proj, conv_w, 1, bounds, dgk, dproj)
    dproj, dwv = _gprep_bwd(proj, conv_w, 2, bounds, dgv, dproj)
    dproj, dalog, ddtb = _bl_bwd(proj, alog, dtb, dbl, dproj)
    (daq_h, dak_h, dav_h), recv_c = _attn_bwd(aq, ak, av, attn32, lse, dattn, L, _Exchange(
        [g_down.reshape(NDEV, DFF // NDEV, D)], True))
    recv = dict(zip(("w_out", "w_pa", "w_pd", "w_up", "w_down"), recv_a + recv_b + recv_c))
    dproj, dqw, dkw = _aprep_bwd(proj, cos, sin, qw, kw, daq_h, dak_h, dav_h, dproj, L)
    g_in = _mm(dproj, h1, name="mm_in_dw", M=C_END, N=D, K=T, ta=True, bm=1024, out_dtype=BF16)
    g_in = _unpad_columns(g_in).reshape(NDEV, W_END // NDEV, D)
    own_in = lax.dynamic_index_in_dim(g_in, _position()[3], axis=0, keepdims=False)
    *pending, token = _scatter_start(g_in, None, (0, D // 2), (), name="scatter_g_in_a_start")
    dh1 = _mm(dproj, w_in, name="mm_in_dx", M=T, N=D, K=C_END, bm=bt, bk=1024, after=(token,))
    grad_x, dsh1, dsc1 = _normmod_bwd(x, mod_lat, 0, 1, dh1, L, dx1, name="normmod_x_bwd")
    _, dcsh1, dcsc1 = _normmod_bwd(ctx, mod_ctx, 0, 1, dh1, 0, None, name="normmod_ctx_bwd")

    z1 = jnp.zeros((1, D), F32)
    dmod_lat = jnp.concatenate([dsh1, dsc1, dg1, dsh2, dsc2, dg2], axis=0)
    dmod_ctx = jnp.concatenate([dcsh1, dcsc1, z1, z1, z1, z1], axis=0)
    gsmall = {
        "q_norm_w": dqw, "k_norm_w": dkw, "gdn_norm_w": dgw,
        "conv_qkv_w": jnp.concatenate([dwq, dwk, dwv], axis=1),
        "a_log": dalog[0, 2 * GH:4 * GH], "dt_bias": ddtb[0, 2 * GH:4 * GH],
        "ffn_conv_w": d_ffn_w, "ffn_conv_b": d_ffn_b, "final_norm_w": dfnw,
    }
    return loss[0, 0], grad_x, (pending, own_in), recv, dmod_lat, dmod_ctx, gsmall


HBM = pl.BlockSpec(memory_space=pltpu.HBM)
ANYSPEC = pl.BlockSpec(memory_space=pl.ANY)


def _position():
    x, y, c = lax.axis_index("x"), lax.axis_index("y"), lax.axis_index("c")
    return x, y, c, 4 * x + 2 * y + c


def _peer(x, y, c, k):
    px = 1 - x if k & 4 else x
    py = 1 - y if k & 2 else y
    pc = 1 - c if k & 1 else c
    return (px, py, pc), 4 * px + 2 * py + pc


def _exchange(arrs, *, name, scatter):
    exch = _Exchange(arrs, scatter)
    n = exch.n

    def body(*refs):
        ins, outs, sems = refs[:n], refs[n:2 * n], refs[2 * n:]
        exch.start(ins, outs, sems)
        exch.finish(ins, outs, sems)

    outs = pl.pallas_call(body, name=name, out_shape=exch.out_shape, in_specs=[HBM] * n, out_specs=(HBM,) * n,
                          scratch_shapes=exch.scratch,
                          compiler_params=pltpu.CompilerParams(has_side_effects=True))(*arrs)
    return list(outs)


class _Exchange:
    def __init__(self, arrs, scatter):
        self.arrs, self.scatter, self.n = list(arrs), scatter, len(arrs)
        self.out_shape = tuple(_sds(a.shape if scatter else (NDEV,) + a.shape, a.dtype) for a in arrs)
        self.scratch = [pltpu.SemaphoreType.DMA((self.n, NDEV - 1)), pltpu.SemaphoreType.DMA((self.n, NDEV - 1)),
                        pltpu.SemaphoreType.DMA((self.n,))]

    def _copies(self, ins, outs, sems):
        send, recv, loc = sems
        x, y, c, me = _position()
        local = [pltpu.make_async_copy(ins[a].at[me] if self.scatter else ins[a], outs[a].at[me], loc.at[a])
                 for a in range(self.n)]
        remote = []
        for k in range(1, NDEV):
            peer, pid = _peer(x, y, c, k)
            for a in range(self.n):
                src = ins[a].at[pid] if self.scatter else ins[a]
                remote.append(pltpu.make_async_remote_copy(
                    src_ref=src, dst_ref=outs[a].at[me], send_sem=send.at[a, k - 1], recv_sem=recv.at[a, k - 1],
                    device_id=peer, device_id_type=MESH))
        return local, remote

    def start(self, ins, outs, sems):
        local, remote = self._copies(ins, outs, sems)
        for cp in local + remote:
            cp.start()

    def finish(self, ins, outs, sems):
        local, remote = self._copies(ins, outs, sems)
        for cp in remote:
            cp.wait()
        for cp in local:
            cp.wait()


class _GatherTwoLevel:
    scatter = False

    def __init__(self, arrs):
        self.arrs, self.n = list(arrs), len(arrs)
        self.out_shape = tuple(_sds((NDEV,) + a.shape, a.dtype) for a in arrs)
        self.scratch = [pltpu.SemaphoreType.DMA((self.n, NDEV - 1)), pltpu.SemaphoreType.DMA((self.n, NDEV - 1)),
                        pltpu.SemaphoreType.DMA((self.n,))]

    def _parts(self, ins, outs, sems):
        send, recv, loc = sems
        x, y, c, _ = _position()
        me, sibling = (x, y, c), (x, y, 1 - c)
        chips = [(1 - x, y), (x, 1 - y), (1 - x, 1 - y)]
        parts = []
        for a in range(self.n):
            slot = lambda px, py, pc, a=a: outs[a].at[4 * px + 2 * py + pc]

            def copy(k, owner, to, src=None, a=a, slot=slot):
                return pltpu.make_async_remote_copy(
                    src_ref=slot(*owner) if src is None else src, dst_ref=slot(*owner), send_sem=send.at[a, k],
                    recv_sem=recv.at[a, k], device_id=to, device_id_type=MESH)

            parts.append(dict(
                mine=pltpu.make_async_copy(ins[a], slot(*me), loc.at[a]),
                first=[copy(0, me, sibling, src=ins[a])] + [copy(1 + j, me, (*ch, c), src=ins[a]) for j, ch in enumerate(chips)],
                arrive=[copy(1 + j, (*ch, c), me) for j, ch in enumerate(chips)],
                passed=[copy(4 + j, (*ch, c), sibling) for j, ch in enumerate(chips)],
                rest=[copy(0, sibling, me)] + [copy(4 + j, (*ch, 1 - c), me) for j, ch in enumerate(chips)]))
        return parts

    def start(self, ins, outs, sems):
        for p in self._parts(ins, outs, sems):
            p["mine"].start()
            for cp in p["first"]:
                cp.start()

    def middle(self, ins, outs, sems):
        for p in self._parts(ins, outs, sems):
            for got, fwd in zip(p["arrive"], p["passed"]):
                got.wait_recv()
                fwd.start()

    def finish(self, ins, outs, sems):
        for p in self._parts(ins, outs, sems):
            for cp in p["rest"]:
                cp.wait_recv()
            for cp in p["first"] + p["passed"]:
                cp.wait_send()
            p["mine"].wait()


def _gather_two_level(blocks, *, name):
    exch = _GatherTwoLevel(blocks)
    n = exch.n

    def body(*refs):
        ins, outs, sems = refs[:n], refs[n:2 * n], refs[2 * n:]
        exch.start(ins, outs, sems)
        exch.middle(ins, outs, sems)
        exch.finish(ins, outs, sems)

    outs = pl.pallas_call(body, name=name, out_shape=exch.out_shape, in_specs=[HBM] * n, out_specs=(HBM,) * n,
                          scratch_shapes=exch.scratch,
                          compiler_params=pltpu.CompilerParams(has_side_effects=True))(*blocks)
    return list(outs)


SEM = pl.BlockSpec(memory_space=pltpu.SEMAPHORE)


def _scatter_copies(src_ref, land_ref, send_sems, recv_sems, cols):
    x, y, c, me = _position()
    span = (slice(None), pl.ds(*cols))
    copies = []
    for k in range(1, NDEV):
        peer, pid = _peer(x, y, c, k)
        copies.append(pltpu.make_async_remote_copy(
            src_ref=src_ref.at[pid].at[span], dst_ref=land_ref.at[me].at[span], send_sem=send_sems.at[k - 1],
            recv_sem=recv_sems.at[k - 1], device_id=peer, device_id_type=MESH))
    return copies


SPLIT_EFFECT = pltpu.SideEffectType.DATAFLOW_SIDE_EFFECTING


def _scatter_start(parts, land, cols, after, *, name):
    na = len(after)
    if land is None:
        land = lax.empty(parts.shape, parts.dtype)

    def body(src_ref, land_ref, *rest):
        send_sems, recv_sems, _, _, token = rest[na:]
        for cp in _scatter_copies(src_ref, land_ref, send_sems, recv_sems, cols):
            cp.start()
        token[...] = jnp.zeros_like(token)

    return pl.pallas_call(
        body, name=name,
        out_shape=(pltpu.SemaphoreType.DMA((NDEV - 1,)), pltpu.SemaphoreType.DMA((NDEV - 1,)),
                   pltpu.HBM(parts.shape, parts.dtype), pltpu.HBM(parts.shape, parts.dtype), _sds((8, HD))),
        in_specs=(HBM, HBM) + (pl.BlockSpec(memory_space=pl.ANY),) * na,
        out_specs=(SEM, SEM, HBM, HBM, pl.BlockSpec(memory_space=pltpu.VMEM)),
        input_output_aliases={0: 2, 1: 3}, compiler_params=pltpu.CompilerParams(has_side_effects=SPLIT_EFFECT),
    )(pltpu.with_memory_space_constraint(parts, pltpu.HBM), pltpu.with_memory_space_constraint(land, pltpu.HBM), *after)


def _scatter_wait(send_sems, recv_sems, src_thru, land_thru, cols, after, *, name):
    na = len(after)

    def body(src_ref, land_ref, send_sems, recv_sems, *rest):
        for cp in _scatter_copies(src_ref, land_ref, send_sems, recv_sems, cols):
            cp.wait_send()
            cp.wait_recv()

    return pl.pallas_call(
        body, name=name,
        out_shape=(pltpu.HBM(src_thru.shape, src_thru.dtype), pltpu.HBM(land_thru.shape, land_thru.dtype)),
        in_specs=(HBM, HBM, SEM, SEM) + (pl.BlockSpec(memory_space=pl.ANY),) * na, out_specs=(HBM, HBM),
        input_output_aliases={0: 0, 1: 1}, compiler_params=pltpu.CompilerParams(has_side_effects=SPLIT_EFFECT),
    )(src_thru, land_thru, send_sems, recv_sems, *after)


def _cast_bf16(w, *, name):
    rows, cols = w.shape
    br = 128 if rows % 128 == 0 else rows

    def body(w_ref, o_ref):
        o_ref[...] = w_ref[...].astype(BF16)

    blk = pl.BlockSpec((br, cols), lambda i: (i, 0))
    return _call(body, name=name, out_shape=_sds((rows, cols), BF16), grid=(rows // br,), in_specs=[blk],
                 out_specs=blk, sem=("parallel",))(w)


def _sum_slots(a, *, name):
    _, R, C = a.shape

    def body(a_ref, o_ref):
        s = a_ref[0]
        for d in range(1, NDEV):
            s = s + a_ref[d]
        o_ref[...] = s

    return _call(body, name=name, out_shape=_sds((R, C)))(a)


MODROWS = 16


def _mod_fwd(c9, w, b):
    cols = w.shape[1]

    def body(c_ref, w_ref, b_ref, o_ref):
        o_ref[...] = _nn(_silu(c_ref[...]), w_ref[...]) + b_ref[...]

    return _call(body, name="mod_fwd", out_shape=_sds((MODROWS, cols)))(c9, w, b)


def _mod_bwd(c9, dmy, dall, w):
    cols = w.shape[1]

    def body(c_ref, dmy_ref, dall_ref, w_ref, gw_ref, gb_ref, cp_ref):
        sc = _silu(c_ref[...])
        rows = lax.broadcasted_iota(jnp.int32, (MODROWS, 1), 0)
        d = dmy_ref[...]
        d_ctx = jnp.where(rows == NDEV, d, 0.0)
        sc_ctx = jnp.where(rows == NDEV, sc, 0.0)
        outer = lax.dot_general(sc_ctx, d_ctx, (((0,), (0,)), ((), ())), precision=HI, preferred_element_type=F32)
        gw_ref[...] = _tn(jnp.where(rows < NDEV, sc, 0.0), jnp.where(rows < NDEV, d, 0.0)) + outer
        gb_ref[...] = jnp.sum(dall_ref[...], axis=0, keepdims=True)
        cp_ref[...] = jnp.sum(_nt(d_ctx, w_ref[...]), axis=0, keepdims=True)

    return _call(body, name="mod_bwd", out_shape=(_sds((D, cols)), _sds((1, 6 * D)), _sds((1, D))),
                 vmem=VMEM_BIG)(c9, dmy, dall, w)


def _cctx_finish(parts, c_ctx, after):
    VM = pl.BlockSpec(memory_space=pltpu.VMEM)

    def body(p_ref, c_ref, *rest):
        o_ref = rest[-1]
        s = p_ref[0]
        for d in range(1, NDEV):
            s = s + p_ref[d]
        _, vjp = jax.vjp(_silu, c_ref[...])
        o_ref[...] = vjp(s)[0]

    return _call(body, name="cctx_finish", out_shape=_sds((1, D)),
                 in_specs=[VM, VM] + [pl.BlockSpec(memory_space=pl.ANY)] * len(after))(parts, c_ctx, *after)


def _adamw_recv(w, recv, m, v, *, name, own=None):
    rows, cols = w.shape
    bc = 256
    c1 = 1.0 - B1 ** STEP
    c2 = 1.0 - B2 ** STEP
    has_own = own is not None

    def body(w_ref, r_ref, m_ref, v_ref, *rest):
        g_ref, d_ref, nm_ref, nv_ref = rest[-4:]
        me = _position()[3]

        def slot(d):
            return jnp.where(me == d, rest[0][...], r_ref[d]) if has_own else r_ref[d]

        gv = slot(0).astype(F32)
        for d in range(1, NDEV):
            gv = gv + slot(d).astype(F32)
        nm = B1 * m_ref[...] + (1.0 - B1) * gv
        nv = B2 * v_ref[...] + (1.0 - B2) * (gv * gv)
        g_ref[...] = gv
        d_ref[...] = -LR * ((nm / c1) / (jnp.sqrt(nv / c2) + AEPS) + WD * w_ref[...])
        nm_ref[...] = nm
        nv_ref[...] = nv

    blk = pl.BlockSpec((rows, bc), lambda j: (0, j))
    return _call(body, name=name, out_shape=(_sds((rows, cols)),) * 4, grid=(cols // bc,),
                 in_specs=[blk, pl.BlockSpec((NDEV, rows, bc), lambda j: (0, 0, j)), blk, blk] + [blk] * has_own,
                 out_specs=(blk,) * 4, sem=("parallel",), vmem=VMEM_BIG)(w, recv, m, v, *([own] if has_own else []))


P_LAT, P_CTX, P_FNW, P_FFNB, P_CONV, P_FFNW, P_MISC, P_ROWS = 0, 8, 16, 24, 32, 48, 72, 80


def _rows_of(v, nrows):
    flat = v.reshape(-1)
    return jnp.pad(flat, (0, nrows * D - flat.shape[0])).reshape(nrows, D)


def _by_columns(g):
    n, r, c = g.shape
    return jnp.transpose(g, (1, 0, 2)).reshape(r, n * c)


def kernel(x, c, ctx, c_ctx, w_mod, b_mod, w_in, q_norm_w, k_norm_w, conv_qkv_w, a_log, dt_bias, gdn_norm_w, w_pa, w_pd, w_out, w_up, ffn_conv_w, ffn_conv_b, w_down, final_norm_w, loss_target, m_c_ctx, m_w_mod, m_b_mod, m_w_in, m_q_norm_w, m_k_norm_w, m_conv_qkv_w, m_a_log, m_dt_bias, m_gdn_norm_w, m_w_pa, m_w_pd, m_w_out, m_w_up, m_ffn_conv_w, m_ffn_conv_b, m_w_down, m_final_norm_w, v_c_ctx, v_w_mod, v_b_mod, v_w_in, v_q_norm_w, v_k_norm_w, v_conv_qkv_w, v_a_log, v_dt_bias, v_gdn_norm_w, v_w_pa, v_w_pd, v_w_out, v_w_up, v_ffn_conv_w, v_ffn_conv_b, v_w_down, v_final_norm_w):
    _, _, _, me = _position()
    mcols = w_mod.shape[2]

    transposed = ("w_in", "w_up")
    big = {"w_in": w_in[0].T, "w_pa": w_pa[0], "w_pd": w_pd[0], "w_out": w_out[0], "w_up": w_up[0].T, "w_down": w_down[0]}
    names = list(big)
    shards = {n: _cast_bf16(big[n], name="cast_" + n) for n in names}
    w_in_g, c_all, conv_g, ffnw_g = _gather_two_level([shards["w_in"], c, conv_qkv_w[0], ffn_conv_w[0]],
                                                      name="gather_w_in")
    w_in_full = w_in_g.reshape(W_END, D)
    w_in_pad = _pad_columns(w_in_full)

    c9 = jnp.concatenate([c_all.reshape(NDEV, D), jnp.pad(c_ctx[None], ((0, MODROWS - NDEV - 1), (0, 0)))], axis=0)
    b_loc = lax.dynamic_slice(b_mod, (0, me * mcols), (1, mcols))
    mod_all, = _exchange([_mod_fwd(c9, w_mod[0], b_loc)], name="gather_mod", scatter=False)
    mod_lat = lax.dynamic_index_in_dim(mod_all, me, axis=1, keepdims=False).reshape(6, D)
    mod_ctx = mod_all[:, NDEV, :].reshape(6, D)

    small = {"q_norm_w": q_norm_w, "k_norm_w": k_norm_w, "gdn_norm_w": gdn_norm_w, "a_log": a_log, "dt_bias": dt_bias,
             "conv_qkv_w": _by_columns(conv_g), "ffn_conv_w": _by_columns(ffnw_g), "ffn_conv_b": ffn_conv_b,
             "final_norm_w": final_norm_w[None]}
    loss_me, grad_x, (pending_in, own_in), recv, dmod_lat, dmod_ctx, gs = _local_step(
        x[0], ctx[0], loss_target[0], mod_lat, mod_ctx, w_in_pad, shards, small)

    moments = {"w_in": (m_w_in, v_w_in), "w_pa": (m_w_pa, v_w_pa), "w_pd": (m_w_pd, v_w_pd),
               "w_out": (m_w_out, v_w_out), "w_up": (m_w_up, v_w_up), "w_down": (m_w_down, v_w_down)}
    res = {}
    def finish(n, outs):
        return tuple((t.T if n in transposed else t)[None] for t in outs)

    def moment(t, n):
        return t[0].T if n in transposed else t[0]

    for n in recv:
        res[n] = finish(n, _adamw_recv(big[n], recv[n], moment(moments[n][0], n), moment(moments[n][1], n),
                                       name="adamw_" + n))

    misc = jnp.concatenate([gs["q_norm_w"][0], gs["k_norm_w"][0], gs["gdn_norm_w"][0], gs["a_log"], gs["dt_bias"],
                            loss_me[None]])
    pack = jnp.concatenate([_rows_of(dmod_lat, P_CTX - P_LAT), _rows_of(dmod_ctx, P_FNW - P_CTX),
                            _rows_of(gs["final_norm_w"], P_FFNB - P_FNW), _rows_of(gs["ffn_conv_b"], P_CONV - P_FFNB),
                            _rows_of(gs["conv_qkv_w"], P_FFNW - P_CONV), _rows_of(gs["ffn_conv_w"], P_MISC - P_FFNW),
                            _rows_of(misc, P_ROWS - P_MISC)], axis=0)
    pack_all, = _exchange([pack], name="gather_pack", scatter=False)
    tot = _sum_slots(pack_all, name="sum_pack")
    dall = jnp.concatenate([pack_all[:, P_LAT:P_LAT + 6, :].reshape(NDEV, 6 * D),
                            jnp.pad(tot[P_CTX:P_CTX + 6].reshape(1, 6 * D), ((0, MODROWS - NDEV - 1), (0, 0)))], axis=0)
    dmy = lax.dynamic_slice(dall, (0, me * mcols), (MODROWS, mcols))
    g_w_mod, g_b_mod, cpart = _mod_bwd(c9, dmy, dall, w_mod[0])
    cparts, = _exchange([cpart], name="gather_cctx", scatter=False)
    sems_a, land = pending_in[:2], pending_in[3]
    *sems_b, g_in_thru, land, token_b = _scatter_start(pending_in[2], land, (D // 2, D // 2), (cparts,),
                                                       name="scatter_g_in_b_start")
    g_c_ctx = _cctx_finish(cparts, c_ctx[None], (token_b,))[0]

    nconv, nffn = 3 * GH * HD, 2 * DFF
    conv_tot = tot[P_CONV:P_FFNW].reshape(-1)[:3 * nconv].reshape(3, nconv)
    ffnw_tot = tot[P_FFNW:P_MISC].reshape(-1)[:3 * nffn].reshape(3, nffn)
    mrow = tot[P_MISC]
    grads = {
        "c_ctx": g_c_ctx, "w_mod": g_w_mod[None], "b_mod": g_b_mod,
        "q_norm_w": mrow[None, 0:HD], "k_norm_w": mrow[None, HD:2 * HD], "gdn_norm_w": mrow[None, 2 * HD:3 * HD],
        "conv_qkv_w": lax.dynamic_slice(conv_tot, (0, me * (nconv // NDEV)), (3, nconv // NDEV))[None],
        "a_log": mrow[3 * HD:3 * HD + 2 * GH].reshape(1, 2, GH),
        "dt_bias": mrow[3 * HD + 2 * GH:3 * HD + 4 * GH].reshape(1, 2, GH),
        "ffn_conv_w": lax.dynamic_slice(ffnw_tot, (0, me * (nffn // NDEV)), (3, nffn // NDEV))[None],
        "ffn_conv_b": tot[P_FFNB:P_CONV].reshape(-1)[:nffn][None],
        "final_norm_w": tot[P_FNW],
    }
    loss = mrow[3 * HD + 4 * GH]
    given = {"c_ctx": (c_ctx, m_c_ctx, v_c_ctx), "w_mod": (w_mod, m_w_mod, v_w_mod), "b_mod": (b_mod, m_b_mod, v_b_mod),
             "q_norm_w": (q_norm_w, m_q_norm_w, v_q_norm_w), "k_norm_w": (k_norm_w, m_k_norm_w, v_k_norm_w),
             "conv_qkv_w": (conv_qkv_w, m_conv_qkv_w, v_conv_qkv_w), "a_log": (a_log, m_a_log, v_a_log),
             "dt_bias": (dt_bias, m_dt_bias, v_dt_bias), "gdn_norm_w": (gdn_norm_w, m_gdn_norm_w, v_gdn_norm_w),
             "ffn_conv_w": (ffn_conv_w, m_ffn_conv_w, v_ffn_conv_w), "ffn_conv_b": (ffn_conv_b, m_ffn_conv_b, v_ffn_conv_b),
             "final_norm_w": (final_norm_w, m_final_norm_w, v_final_norm_w)}
    res["w_mod"] = (grads["w_mod"],) + _adamw(w_mod, grads["w_mod"], m_w_mod, v_w_mod, name="adamw_w_mod")
    small_names = [n for n in given if n != "w_mod"]
    updates = _adamw_many([(given[n][0], grads[n], given[n][1], given[n][2]) for n in small_names], name="adamw_small")
    for n, upd in zip(small_names, updates):
        res[n] = (grads[n],) + upd

    g_in_thru, land = _scatter_wait(*sems_a, g_in_thru, land, (0, D // 2), [res[n][1] for n in res],
                                    name="scatter_g_in_a_wait")
    _, land = _scatter_wait(*sems_b, g_in_thru, land, (D // 2, D // 2), (), name="scatter_g_in_b_wait")
    res["w_in"] = finish("w_in", _adamw_recv(big["w_in"], land, moment(m_w_in, "w_in"), moment(v_w_in, "w_in"),
                                             name="adamw_w_in", own=own_in))

    order = ["c_ctx", "w_mod", "b_mod", "w_in", "q_norm_w", "k_norm_w", "conv_qkv_w", "a_log", "dt_bias", "gdn_norm_w",
             "w_pa", "w_pd", "w_out", "w_up", "ffn_conv_w", "ffn_conv_b", "w_down", "final_norm_w"]
    return (loss, grad_x[None], *[res[n][0] for n in order], *[res[n][1] for n in order],
            *[res[n][2] for n in order], *[res[n][3] for n in order])
```

```python
import functools
import math

import jax
import jax.numpy as jnp
from jax import lax
from jax.experimental import pallas as pl
from jax.experimental.pallas import tpu as pltpu

F32 = jnp.float32
BF16 = jnp.bfloat16
HI = lax.Precision.HIGHEST
MESH = pl.DeviceIdType.MESH

NDEV = 8
D = 1024
HD = 128
AH, AKV, GRP = 8, 2, 4
GH = 8
CH = 64
DFF = 2816
GRID_W = 64
EPS = 1e-6
ROPE_THETA = 10000.0
LOG2E = math.log2(math.e)
C_KV, C_AQ, C_QKV, C_BL, C_Z, C_GATE, C_END = 0, 512, 1536, 4608, 5120, 6144, 8192
W_QKV, W_AQ, W_Z, W_END = 512, 3616, 4640, 7712


def _pad_columns(w):
    zeros = jnp.zeros((C_Z - C_QKV - (W_AQ - W_QKV), D), w.dtype)
    return jnp.concatenate([w[:W_QKV], w[W_AQ:W_Z], w[W_QKV:W_AQ], zeros, w[W_Z:]], axis=0)


def _unpad_columns(g):
    return jnp.concatenate([g[:C_AQ], g[C_QKV:C_QKV + W_AQ - W_QKV], g[C_AQ:C_QKV], g[C_Z:]], axis=0)
LR, B1, B2, AEPS, WD, STEP = 0.001, 0.9, 0.999, 1e-08, 0.01, 10
VMEM_BIG = 56 * 1024 * 1024
INTRA_FWD_CHUNKS = 36
INTRA_BWD_CHUNKS = 36


def _call(body, *, name, out_shape, grid=None, in_specs=None, out_specs=None, scratch=(), sem=None,
          vmem=None, aliases=None):
    params = {}
    if sem is not None:
        params["dimension_semantics"] = sem
    if vmem is not None:
        params["vmem_limit_bytes"] = vmem
    kw = {}
    if grid is not None:
        kw["grid"] = grid
    if in_specs is not None:
        kw["in_specs"] = in_specs
    if out_specs is not None:
        kw["out_specs"] = out_specs
    if aliases:
        kw["input_output_aliases"] = aliases
    return pl.pallas_call(body, name=name, out_shape=out_shape, scratch_shapes=list(scratch),
                          compiler_params=pltpu.CompilerParams(**params), **kw)


def _call_carrying(body, exch, *, name, out_shape, grid, in_specs, out_specs, scratch=(), vmem=None):
    n, nin, nout, nscr = exch.n, len(in_specs), len(out_shape), len(scratch)
    steps = math.prod(grid)
    mid = (2 * steps) // 3

    def wrapped(*refs):
        ins, cins = refs[:nin], refs[nin:nin + n]
        outs, couts = refs[nin + n:nin + n + nout], refs[nin + n + nout:nin + 2 * n + nout]
        scr, sems = refs[nin + 2 * n + nout:nin + 2 * n + nout + nscr], refs[nin + 2 * n + nout + nscr:]
        ids = [pl.program_id(i) for i in range(len(grid))]
        first = functools.reduce(jnp.logical_and, [i == 0 for i in ids])
        last = functools.reduce(jnp.logical_and, [i == g - 1 for i, g in zip(ids, grid)])

        @pl.when(first)
        def _():
            exch.start(cins, couts, sems)

        if hasattr(exch, "middle"):
            linear = functools.reduce(lambda acc, ig: acc * ig[1] + ig[0], zip(ids, grid), 0)

            @pl.when(linear == mid)
            def _():
                exch.middle(cins, couts, sems)

        body(*ins, *outs, *scr)

        @pl.when(last)
        def _():
            exch.finish(cins, couts, sems)

    params = {"dimension_semantics": ("arbitrary",) * len(grid)}
    if vmem is not None:
        params["vmem_limit_bytes"] = vmem
    fn = pl.pallas_call(wrapped, name=name, out_shape=tuple(out_shape) + exch.out_shape, grid=grid,
                        in_specs=list(in_specs) + [HBM] * n, out_specs=tuple(out_specs) + (HBM,) * n,
                        scratch_shapes=list(scratch) + exch.scratch, compiler_params=pltpu.CompilerParams(**params))

    def run(*args):
        res = fn(*args, *exch.arrs)
        return res[:nout], list(res[nout:])

    return run


def _sds(shape, dtype=F32):
    return jax.ShapeDtypeStruct(tuple(shape), dtype)


def _dot(a, b, ca, cb):
    return lax.dot_general(a.astype(BF16), b.astype(BF16), (((ca,), (cb,)), ((), ())),
                           preferred_element_type=F32)


@jax.custom_vjp
def _nn(a, b):
    return _dot(a, b, 1, 0)


@jax.custom_vjp
def _nt(a, b):
    return _dot(a, b, 1, 1)


@jax.custom_vjp
def _tn(a, b):
    return _dot(a, b, 0, 0)


_nn.defvjp(lambda a, b: (_nn(a, b), (a, b)), lambda r, g: (_nt(g, r[1]), _tn(r[0], g)))
_nt.defvjp(lambda a, b: (_nt(a, b), (a, b)), lambda r, g: (_nn(g, r[1]), _tn(g, r[0])))
_tn.defvjp(lambda a, b: (_tn(a, b), (a, b)), lambda r, g: (_nt(r[1], g), _nn(r[0], g)))


def _mdot(a, b):
    return jnp.dot(a, b, precision=lax.Precision.HIGH, preferred_element_type=F32)


def _maskdot(mask, a, cm):
    hi = a.astype(BF16)
    r = a - hi.astype(F32)
    mid = r.astype(BF16)
    lo = (r - mid.astype(F32)).astype(BF16)
    mb = mask.astype(BF16)
    dims = (((cm,), (0,)), ((), ()))
    return (lax.dot_general(mb, hi, dims, preferred_element_type=F32)
            + lax.dot_general(mb, mid, dims, preferred_element_type=F32)
            + lax.dot_general(mb, lo, dims, preferred_element_type=F32))


@jax.custom_vjp
def _mask_nn(mask, a):
    return _maskdot(mask, a, 1)


_mask_nn.defvjp(lambda mask, a: (_maskdot(mask, a, 1), mask),
                lambda mask, g: (jnp.zeros_like(mask), _maskdot(mask, g, 0)))


@jax.custom_vjp
def _saved_inverse(lmat, x):
    return x


def _saved_inverse_bwd(x, g):
    t = lax.dot_general(x, g, (((0,), (0,)), ((), ())), precision=lax.Precision.HIGH, preferred_element_type=F32)
    dl = lax.dot_general(t, x, (((1,), (1,)), ((), ())), precision=lax.Precision.HIGH, preferred_element_type=F32)
    return -dl, jnp.zeros_like(x)


_saved_inverse.defvjp(lambda lmat, x: (x, x), _saved_inverse_bwd)


def _row_ids(shape):
    return lax.broadcasted_iota(jnp.int32, shape, 0)


def _shift_rows(x, down, bounds):
    n = x.shape[0]
    rows = _row_ids(x.shape)
    y = pltpu.roll(x, 1 if down else n - 1, 0)
    edge = functools.reduce(jnp.logical_or, [rows == (s if down else e - 1) for s, e in bounds])
    return jnp.where(edge, 0.0, y)


def _make_shift(bounds):
    @jax.custom_vjp
    def down(x):
        return _shift_rows(x, True, bounds)

    @jax.custom_vjp
    def up(x):
        return _shift_rows(x, False, bounds)

    down.defvjp(lambda x: (down(x), None), lambda _, g: (up(g),))
    up.defvjp(lambda x: (up(x), None), lambda _, g: (down(g),))
    return down, up


@jax.custom_vjp
def _swap32(x):
    lane = lax.broadcasted_iota(jnp.int32, x.shape, x.ndim - 1)
    return jnp.where((lane % 64) < 32, pltpu.roll(x, HD - 32, x.ndim - 1), pltpu.roll(x, 32, x.ndim - 1))


_swap32.defvjp(lambda x: (_swap32(x), None), lambda _, g: (_swap32(g),))


def _rms(x):
    return x * lax.rsqrt(jnp.mean(x * x, axis=-1, keepdims=True) + EPS)


def _silu(x):
    return x * jax.nn.sigmoid(x)


def _mm(a, b, *, name, M, N, K, ta=False, tb=False, out_dtype=F32, bm=None, bn=None, bk=None, after=()):
    bm, bn, bk = bm or M, bn or N, bk or K
    assert M % bm == 0 and N % bn == 0 and K % bk == 0, (name, M, N, K, bm, bn, bk)
    nk = K // bk
    ca, cb = (0 if ta else 1), (1 if tb else 0)
    na = len(after)

    def body(a_ref, b_ref, *rest):
        o_ref, acc = rest[na], rest[na + 1:]
        r = _dot(a_ref[...], b_ref[...], ca, cb)
        if nk == 1:
            o_ref[...] = r.astype(out_dtype)
        else:
            acc_ref, = acc
            k = pl.program_id(2)

            @pl.when(k == 0)
            def _():
                acc_ref[...] = r

            @pl.when(k > 0)
            def _():
                acc_ref[...] += r

            @pl.when(k == nk - 1)
            def _():
                o_ref[...] = acc_ref[...].astype(out_dtype)

    a_spec = pl.BlockSpec((bk, bm), lambda i, j, k: (k, i)) if ta else pl.BlockSpec((bm, bk), lambda i, j, k: (i, k))
    b_spec = pl.BlockSpec((bn, bk), lambda i, j, k: (j, k)) if tb else pl.BlockSpec((bk, bn), lambda i, j, k: (k, j))
    return _call(body, name=name, out_shape=_sds((M, N), out_dtype), grid=(M // bm, N // bn, nk),
                 in_specs=[a_spec, b_spec] + [pl.BlockSpec(memory_space=pl.ANY)] * na,
                 out_specs=pl.BlockSpec((bm, bn), lambda i, j, k: (i, j)),
                 scratch=[pltpu.VMEM((bm, bn), F32)] if nk > 1 else [],
                 sem=("parallel", "parallel", "arbitrary"), vmem=VMEM_BIG)(a, b, *after)


def _normmod_fn(x, sh, sc):
    return _rms(x) * (1.0 + sc) + sh


def _normmod_fwd(x, mod, i_sh, i_sc, *, name, br=256):
    R = x.shape[0]

    def body(x_ref, mod_ref, o_ref):
        o_ref[...] = _normmod_fn(x_ref[...], mod_ref[i_sh:i_sh + 1, :], mod_ref[i_sc:i_sc + 1, :]).astype(BF16)

    return _call(body, name=name, out_shape=_sds((R, D), BF16), grid=(R // br,),
                 in_specs=[pl.BlockSpec((br, D), lambda i: (i, 0)), pl.BlockSpec((6, D), lambda i: (0, 0))],
                 out_specs=pl.BlockSpec((br, D), lambda i: (i, 0)), sem=("parallel",))(x, mod)


def _normmod_bwd(x, mod, i_sh, i_sc, dh, dh_off, res, *, name, br=256):
    R = x.shape[0]
    ob = dh_off // br
    has_res = res is not None

    def body(x_ref, mod_ref, dh_ref, *rest):
        if has_res:
            res_ref, dx_ref, dsh_ref, dsc_ref = rest
        else:
            dx_ref, dsh_ref, dsc_ref = rest
        sh, sc = mod_ref[i_sh:i_sh + 1, :], mod_ref[i_sc:i_sc + 1, :]
        _, vjp = jax.vjp(_normmod_fn, x_ref[...], sh, sc)
        dx, dsh, dsc = vjp(dh_ref[...])
        dx_ref[...] = dx + res_ref[...] if has_res else dx

        @pl.when(pl.program_id(0) == 0)
        def _():
            dsh_ref[...] = jnp.zeros_like(dsh_ref)
            dsc_ref[...] = jnp.zeros_like(dsc_ref)

        dsh_ref[...] += dsh
        dsc_ref[...] += dsc

    row = pl.BlockSpec((br, D), lambda i: (i, 0))
    vec = pl.BlockSpec((1, D), lambda i: (0, 0))
    ins = [row, pl.BlockSpec((6, D), lambda i: (0, 0)), pl.BlockSpec((br, D), lambda i: (i + ob, 0))]
    args = [x, mod, dh]
    if has_res:
        ins.append(row)
        args.append(res)
    return _call(body, name=name, out_shape=(_sds((R, D)), _sds((1, D)), _sds((1, D))), grid=(R // br,),
                 in_specs=ins, out_specs=(row, vec, vec), sem=("arbitrary",))(*args)


def _rope(x, cos, sin):
    return x * cos + _swap32(x) * sin


def _aprep_fn(qs, ks, cos, sin, qw, kw):
    return ([_rope(_rms(q) * qw, cos, sin) for q in qs], [_rope(_rms(k) * kw, cos, sin) for k in ks])


def _aprep_fwd(proj, cos, sin, qw, kw, *, br=256):
    T = proj.shape[0]

    def body(x_ref, cos_ref, sin_ref, qw_ref, kw_ref, q_ref, k_ref, v_ref):
        qs = [x_ref[:, C_AQ + h * HD:C_AQ + (h + 1) * HD] for h in range(AH)]
        ks = [x_ref[:, h * HD:(h + 1) * HD] for h in range(AKV)]
        qo, ko = _aprep_fn(qs, ks, cos_ref[...], sin_ref[...], qw_ref[...], kw_ref[...])
        for h in range(AH):
            q_ref[h] = qo[h].astype(BF16)
        for h in range(AKV):
            k_ref[h] = ko[h].astype(BF16)
            v_ref[h] = x_ref[:, (AKV + h) * HD:(AKV + h + 1) * HD].astype(BF16)

    tab = pl.BlockSpec((br, HD), lambda i: (i, 0))
    vec = pl.BlockSpec((1, HD), lambda i: (0, 0))
    return _call(body, name="aprep_fwd",
                 out_shape=(_sds((AH, T, HD), BF16), _sds((AKV, T, HD), BF16), _sds((AKV, T, HD), BF16)),
                 grid=(T // br,),
                 in_specs=[pl.BlockSpec((br, C_QKV), lambda i: (i, 0)), tab, tab, vec, vec],
                 out_specs=(pl.BlockSpec((AH, br, HD), lambda i: (0, i, 0)),
                            pl.BlockSpec((AKV, br, HD), lambda i: (0, i, 0)),
                            pl.BlockSpec((AKV, br, HD), lambda i: (0, i, 0))),
                 sem=("parallel",))(proj, cos, sin, qw, kw)


def _aprep_bwd(proj, cos, sin, qw, kw, dq, dk, dv, dproj, L, *, br=256):
    T = proj.shape[0]
    lb = L // br

    def body(x_ref, cos_ref, sin_ref, qw_ref, kw_ref, dq_ref, dk_ref, dv_ref, _, dx_ref, dqw_ref, dkw_ref):
        i = pl.program_id(0)
        qs = [x_ref[:, C_AQ + h * HD:C_AQ + (h + 1) * HD] for h in range(AH)]
        ks = [x_ref[:, h * HD:(h + 1) * HD] for h in range(AKV)]
        _, vjp = jax.vjp(_aprep_fn, qs, ks, cos_ref[...], sin_ref[...], qw_ref[...], kw_ref[...])
        is_lat = i >= lb
        dqs = [jnp.where(is_lat, dq_ref[h], 0.0) for h in range(AH)]
        dks = [dk_ref[h] for h in range(AKV)]
        gq, gk, _, _, gqw, gkw = vjp((dqs, dks))
        for h in range(AH):
            dx_ref[:, C_AQ + h * HD:C_AQ + (h + 1) * HD] = gq[h].astype(BF16)
        for h in range(AKV):
            dx_ref[:, h * HD:(h + 1) * HD] = gk[h].astype(BF16)
            dx_ref[:, (AKV + h) * HD:(AKV + h + 1) * HD] = dv_ref[h].astype(BF16)

        @pl.when(i == 0)
        def _():
            dqw_ref[...] = jnp.zeros_like(dqw_ref)
            dkw_ref[...] = jnp.zeros_like(dkw_ref)

        dqw_ref[...] += gqw
        dkw_ref[...] += gkw

    tab = pl.BlockSpec((br, HD), lambda i: (i, 0))
    vec = pl.BlockSpec((1, HD), lambda i: (0, 0))
    kvb = pl.BlockSpec((AKV, br, HD), lambda i: (0, i, 0))
    blk = pl.BlockSpec((br, C_QKV), lambda i: (i, 0))
    return _call(body, name="aprep_bwd", out_shape=(_sds(dproj.shape, BF16), _sds((1, HD)), _sds((1, HD))),
                 grid=(T // br,),
                 in_specs=[blk, tab, tab, vec, vec,
                           pl.BlockSpec((AH, br, HD), lambda i: (0, jnp.maximum(i - lb, 0), 0)), kvb, kvb, ANYSPEC],
                 out_specs=(blk, vec, vec), aliases={8: 0},
                 sem=("arbitrary",))(proj, cos, sin, qw, kw, dq, dk, dv, dproj)


def _attn_grad(q, k, v, o, lse2, do):
    scale = HD ** -0.5
    p = jnp.exp2(_dot(q, k, 1, 1) * (scale * LOG2E) - lse2)
    dp = _dot(do, v, 1, 1)
    ds = p * (dp - jnp.sum(do * o, axis=-1, keepdims=True)) * scale
    return _dot(ds, k, 1, 0), _dot(ds, q, 0, 0), _dot(p, do, 0, 0)


ATTN_KEYS = 256


def _attn_fwd(q, k, v, L, exch, *, bq=128):
    T = q.shape[1]
    N = T - L
    lb = L // bq
    assert T % ATTN_KEYS == 0
    scale = HD ** -0.5
    heads = range(GRP)

    def body(q_ref, k_ref, v_ref, o_ref, o32_ref, lse_ref):
        qs = [q_ref[g] for g in heads]
        m = [jnp.full((bq, 1), -jnp.inf, F32) for _ in heads]
        l = [jnp.zeros((bq, 1), F32) for _ in heads]
        acc = [jnp.zeros((bq, HD), F32) for _ in heads]
        for c in range(T // ATTN_KEYS):
            kc, vc = k_ref[c * ATTN_KEYS:(c + 1) * ATTN_KEYS, :], v_ref[c * ATTN_KEYS:(c + 1) * ATTN_KEYS, :]
            s = [_dot(qs[g], kc, 1, 1) * (scale * LOG2E) for g in heads]
            m_new = [jnp.maximum(m[g], jnp.max(s[g], axis=-1, keepdims=True)) for g in heads]
            alpha = [jnp.exp2(m[g] - m_new[g]) for g in heads]
            p = [jnp.exp2(s[g] - m_new[g]) for g in heads]
            l = [l[g] * alpha[g] + jnp.sum(p[g], axis=-1, keepdims=True) for g in heads]
            acc = [acc[g] * alpha[g] + _dot(p[g], vc, 1, 0) for g in heads]
            m = m_new
        for g in heads:
            o = acc[g] / l[g]
            o_ref[:, g * HD:(g + 1) * HD] = o.astype(BF16)
            o32_ref[:, g * HD:(g + 1) * HD] = o
            lse_ref[g] = jnp.broadcast_to(m[g] + jnp.log2(l[g]), (bq, HD))

    kvb = pl.BlockSpec((None, T, HD), lambda g, i: (g, 0, 0))
    ob = pl.BlockSpec((bq, GRP * HD), lambda g, i: (i, g))
    return _call_carrying(
        body, exch, name="attn_fwd",
        out_shape=(_sds((N, AH * HD), BF16), _sds((N, AH * HD)), _sds((AH, N, HD))), grid=(AKV, N // bq),
        in_specs=[pl.BlockSpec((GRP, bq, HD), lambda g, i: (g, i + lb, 0)), kvb, kvb],
        out_specs=(ob, ob, pl.BlockSpec((GRP, bq, HD), lambda g, i: (g, i, 0))), vmem=VMEM_BIG)(q, k, v)


def _attn_bwd(q, k, v, o32, lse, do, L, exch, *, bq=128):
    T = q.shape[1]
    N = T - L
    lb = L // bq

    def body(q_ref, k_ref, v_ref, o_ref, lse_ref, do_ref, dq_ref, dk_ref, dv_ref):
        rows = lambda r: jnp.concatenate([r[:, g * HD:(g + 1) * HD] for g in range(GRP)], axis=0)
        lse = jnp.max(lse_ref[...].reshape(GRP * bq, HD), axis=-1, keepdims=True)
        dq, dk, dv = _attn_grad(q_ref[...].reshape(GRP * bq, HD), k_ref[...], v_ref[...], rows(o_ref), lse, rows(do_ref))
        dq_ref[...] = dq.reshape(GRP, bq, HD)

        @pl.when(pl.program_id(1) == 0)
        def _():
            dk_ref[...] = jnp.zeros_like(dk_ref)
            dv_ref[...] = jnp.zeros_like(dv_ref)

        dk_ref[...] += dk
        dv_ref[...] += dv

    kvb = pl.BlockSpec((None, T, HD), lambda g, i: (g, 0, 0))
    qb = pl.BlockSpec((GRP, bq, HD), lambda g, i: (g, i + lb, 0))
    hb = pl.BlockSpec((GRP, bq, HD), lambda g, i: (g, i, 0))
    ob = pl.BlockSpec((bq, GRP * HD), lambda g, i: (i, g))
    return _call_carrying(body, exch, name="attn_bwd",
                          out_shape=(_sds((AH, N, HD)), _sds((AKV, T, HD)), _sds((AKV, T, HD))), grid=(AKV, N // bq),
                          in_specs=[qb, kvb, kvb, ob, hb, ob], out_specs=(hb, kvb, kvb),
                          vmem=VMEM_BIG)(q, k, v, o32, lse, do)


def _gprep_fn(kind, shifts, x, w):
    down, up = shifts
    y = down(x) * w[0:1, :] + x * w[1:2, :] + up(x) * w[2:3, :]
    a = _silu(y)
    if kind == 2:
        return a
    a = a * lax.rsqrt(jnp.sum(a * a, axis=-1, keepdims=True) + EPS)
    return a * (HD ** -0.5) if kind == 0 else a


def _gprep_fwd(proj, conv_w, kind, bounds):
    T = proj.shape[0]
    shifts = _make_shift(bounds)
    cb = C_QKV // HD + kind * GH

    def body(x_ref, w_ref, o_ref):
        o_ref[...] = _gprep_fn(kind, shifts, x_ref[...], w_ref[...])

    return _call(body, name=f"gprep_fwd{kind}", out_shape=_sds((GH, T, HD)), grid=(GH,),
                 in_specs=[pl.BlockSpec((T, HD), lambda h: (0, cb + h)),
                           pl.BlockSpec((3, HD), lambda h: (0, kind * GH + h))],
                 out_specs=pl.BlockSpec((None, T, HD), lambda h: (h, 0, 0)), sem=("parallel",))(proj, conv_w)


def _gprep_bwd(proj, conv_w, kind, bounds, dy, dproj):
    T = proj.shape[0]
    shifts = _make_shift(bounds)
    cb = C_QKV // HD + kind * GH

    def body(x_ref, w_ref, dy_ref, _, dx_ref, dw_ref):
        _, vjp = jax.vjp(functools.partial(_gprep_fn, kind, shifts), x_ref[...], w_ref[...])
        dx, dw = vjp(dy_ref[0] + dy_ref[1])
        dx_ref[...] = dx.astype(BF16)
        dw_ref[...] = dw

    return _call(body, name=f"gprep_bwd{kind}", out_shape=(_sds(dproj.shape, BF16), _sds((3, GH * HD))), grid=(GH,),
                 in_specs=[pl.BlockSpec((T, HD), lambda h: (0, cb + h)),
                           pl.BlockSpec((3, HD), lambda h: (0, kind * GH + h)),
                           pl.BlockSpec((2, None, T, HD), lambda h: (0, h, 0, 0)), ANYSPEC],
                 out_specs=(pl.BlockSpec((T, HD), lambda h: (0, cb + h)), pl.BlockSpec((3, HD), lambda h: (0, h))),
                 aliases={3: 0}, sem=("parallel",))(proj, conv_w, dy, dproj)


def _bl_fn(x, alog, dtb):
    lane = lax.broadcasted_iota(jnp.int32, x.shape, 1)
    beta = jax.nn.sigmoid(x)
    z = x + dtb
    sp = jnp.maximum(z, 0.0) + jnp.log1p(jnp.exp(-jnp.abs(z)))
    la = -jnp.exp(alog) * sp
    return jnp.where(lane < 2 * GH, beta, jnp.where(lane < 4 * GH, la, 0.0))


def _bl_fwd(proj, alog, dtb, *, br=256):
    T = proj.shape[0]

    def body(x_ref, a_ref, d_ref, o_ref):
        o_ref[...] = _bl_fn(x_ref[...], a_ref[...], d_ref[...])

    vec = pl.BlockSpec((1, HD), lambda i: (0, 0))
    return _call(body, name="bl_fwd", out_shape=_sds((T, HD)), grid=(T // br,),
                 in_specs=[pl.BlockSpec((br, HD), lambda i: (i, C_BL // HD)), vec, vec],
                 out_specs=pl.BlockSpec((br, HD), lambda i: (i, 0)), sem=("parallel",))(proj, alog, dtb)


def _bl_bwd(proj, alog, dtb, dbl, dproj, *, br=256):
    T = proj.shape[0]
    wide = C_Z - C_BL

    def body(x_ref, a_ref, d_ref, g_ref, _, dx_ref, da_ref, dd_ref):
        g = g_ref[0, 0]
        for d in range(2):
            for h in range(GH):
                if d or h:
                    g = g + g_ref[d, h]
        _, vjp = jax.vjp(_bl_fn, x_ref[...], a_ref[...], d_ref[...])
        dx, da, dd = vjp(g)
        dx_ref[:, :HD] = dx.astype(BF16)
        dx_ref[:, HD:] = jnp.zeros((br, wide - HD), BF16)

        @pl.when(pl.program_id(0) == 0)
        def _():
            da_ref[...] = jnp.zeros_like(da_ref)
            dd_ref[...] = jnp.zeros_like(dd_ref)

        da_ref[...] += da
        dd_ref[...] += dd

    vec = pl.BlockSpec((1, HD), lambda i: (0, 0))
    return _call(body, name="bl_bwd", out_shape=(_sds(dproj.shape, BF16), _sds((1, HD)), _sds((1, HD))), grid=(T // br,),
                 in_specs=[pl.BlockSpec((br, HD), lambda i: (i, C_BL // HD)), vec, vec,
                           pl.BlockSpec((2, GH, br, HD), lambda i: (0, 0, i, 0)), ANYSPEC],
                 out_specs=(pl.BlockSpec((br, wide), lambda i: (i, C_BL // wide)), vec, vec), aliases={4: 0},
                 sem=("arbitrary",))(proj, alog, dtb, dbl, dproj)


def _chunk_masks(d):
    ii = lax.broadcasted_iota(jnp.int32, (CH, CH), 0)
    jj = lax.broadcasted_iota(jnp.int32, (CH, CH), 1)
    eye = (ii == jj).astype(F32)
    before = jnp.where(d == 0, (jj < ii).astype(F32), (jj > ii).astype(F32))
    return before, before + eye, eye


def _same_block(b):
    ii = lax.broadcasted_iota(jnp.int32, (CH, CH), 0)
    jj = lax.broadcasted_iota(jnp.int32, (CH, CH), 1)
    shift = b.bit_length() - 1
    return (jnp.right_shift(ii, shift) == jnp.right_shift(jj, shift)).astype(F32)


def _intra_fn(masks, sel_b, sel_l, qs, ks, vs, bls, xs=None):
    before, ateq, eye = masks
    inc = ateq > 0.0
    each = lambda f, *ls: [f(*t) for t in zip(*ls)]
    beta = each(lambda bl: jnp.sum(bl * sel_b, axis=-1, keepdims=True), bls)
    la = each(lambda bl: jnp.sum(bl * sel_l, axis=-1, keepdims=True), bls)
    gam = each(lambda a: _mask_nn(ateq, jnp.broadcast_to(a, (CH, HD))), la)
    gi = each(lambda g: g[:, :CH], gam)
    gj = each(lambda g: jnp.transpose(g)[:CH, :], gam)
    kq = each(lambda k, q: _nt(jnp.concatenate([k, q], axis=0), k), ks, qs)
    kk = each(lambda t: t[:CH], kq)
    qk = each(lambda t: t[CH:], kq)
    dec = each(lambda a, b: jnp.where(inc, jnp.exp(jnp.where(inc, a - b, 0.0)), 0.0), gi, gj)
    lmat = each(lambda b, d, m: before * (b * d * m), beta, dec, kk)
    if xs is None:
        same = lambda b: _same_block(b)
        l8 = each(lambda m: m * same(8), lmat)
        x = each(lambda m: eye - m, l8)
        p2 = each(lambda m: _mdot(m, m), l8)
        y = each(lambda a, b: _mdot(jnp.concatenate([a, b], axis=0), b), x, p2)
        x = each(lambda a, t: a + t[:CH], x, y)
        x = each(lambda a, t: a + _mdot(a, t[CH:]), x, y)
        for b in (8, 16, 32):
            below = same(2 * b) - same(b)
            x = each(lambda a, m: a - _mdot(a, _mdot(m * below, a)), x, lmat)
    else:
        x = each(_saved_inverse, lmat, xs)
    eg = each(jnp.exp, gam)
    uw = each(lambda a, b, v, e, k: _mdot(a, jnp.concatenate([b * v, (b * e) * k], axis=1)), x, beta, vs, eg, ks)
    u = each(lambda t: t[:, :HD], uw)
    w = each(lambda t: t[:, HD:], uw)
    tot = each(lambda a: jnp.sum(a, axis=0, keepdims=True), la)
    kd = each(lambda k, t, g: k * jnp.exp(t - g), ks, tot, gam)
    gl = each(lambda t: jnp.broadcast_to(jnp.exp(t), (1, HD)), tot)
    qd = each(lambda q, e: q * e, qs, eg)
    p = each(lambda d, m: d * m, dec, qk)
    return (u, w, kd, qd, p, gl, x) if xs is None else (u, w, kd, qd, p, gl)


def _dir_head_sel(d, h):
    lane = lax.broadcasted_iota(jnp.int32, (1, HD), 1)
    return (lane == d * GH + h).astype(F32), (lane == 2 * GH + d * GH + h).astype(F32)


def _intra_specs(T, G):
    nc = T // CH
    assert nc % G == 0
    qkv = pl.BlockSpec((None, G * CH, HD), lambda d, h, c: (h, c, 0))
    bl = pl.BlockSpec((G * CH, HD), lambda d, h, c: (c, 0))
    big = pl.BlockSpec((None, None, G * CH, HD), lambda d, h, c: (d, h, c, 0))
    pm = pl.BlockSpec((None, None, G * CH, CH), lambda d, h, c: (d, h, c, 0))
    gl = pl.BlockSpec((None, None, G, 1, HD), lambda d, h, c: (d, h, c, 0, 0))
    shapes = (_sds((2, GH, T, HD)),) + (_sds((2, GH, T, HD), BF16),) * 3 + (
        _sds((2, GH, T, CH), BF16), _sds((2, GH, nc, 1, HD)), _sds((2, GH, T, CH)))
    return nc, qkv, bl, big, pm, gl, shapes


def _chunks_per_step(T, most):
    nc = T // CH
    return max(g for g in range(1, most + 1) if nc % g == 0)


def _chunk_at(g, d, nc, ncc):
    pos = _visit_pos(g, d, nc, ncc)
    return pos, pl.ds(pl.multiple_of(pos * CH, CH), CH)


def _intra_fwd(q, k, v, bl, L, exch):
    T = q.shape[1]
    G = _chunks_per_step(T, INTRA_FWD_CHUNKS)
    nc, qkv_s, bl_s, big, pm, gl_s, shapes = _intra_specs(T, G)
    assert G == nc
    ncc = L // CH

    def body(q_ref, k_ref, v_ref, bl_ref, u_ref, w_ref, kd_ref, qd_ref, p_ref, gl_ref, x_ref):
        d, h = pl.program_id(0), pl.program_id(1)
        sb, sl = _dir_head_sel(d, h)
        rows = [slice(g * CH, (g + 1) * CH) for g in range(G)]
        outs = _intra_fn(_chunk_masks(d), sb, sl, *[[r[s, :] for s in rows] for r in (q_ref, k_ref, v_ref, bl_ref)])
        for g in range(G):
            pos, at = _chunk_at(g, d, nc, ncc)
            for r, o in zip((u_ref, w_ref, kd_ref, qd_ref, p_ref, x_ref), outs[:5] + outs[6:]):
                r[at, :] = o[g].astype(r.dtype)
            gl_ref[pos] = outs[5][g]

    return _call_carrying(body, exch, name="gdn_intra_fwd", out_shape=shapes, grid=(2, GH, nc // G),
                          in_specs=[qkv_s, qkv_s, qkv_s, bl_s], out_specs=(big, big, big, big, pm, gl_s, pm))(q, k, v, bl)


def _intra_bwd(q, k, v, bl, xinv, cts, L, exch):
    T = q.shape[1]
    G = _chunks_per_step(T, INTRA_BWD_CHUNKS)
    nc, qkv_s, bl_s, big, pm, gl_s, _ = _intra_specs(T, G)
    assert G == nc
    ncc = L // CH

    def body(q_ref, k_ref, v_ref, bl_ref, x_ref, du, dw, dkd, dqd, dp, dgl, dq_ref, dk_ref, dv_ref, dbl_ref):
        d, h = pl.program_id(0), pl.program_id(1)
        sb, sl = _dir_head_sel(d, h)
        rows = [slice(g * CH, (g + 1) * CH) for g in range(G)]
        places = [_chunk_at(g, d, nc, ncc) for g in range(G)]
        fn = functools.partial(_intra_fn, _chunk_masks(d), sb, sl, xs=[x_ref[at, :] for _, at in places])
        _, vjp = jax.vjp(fn, *[[r[s, :] for s in rows] for r in (q_ref, k_ref, v_ref, bl_ref)])
        cts = tuple([r[at, :] for _, at in places] for r in (du, dw, dkd, dqd, dp)) + ([dgl[pos] for pos, _ in places],)
        grads = vjp(cts)
        for g in range(G):
            for r, o in zip((dq_ref, dk_ref, dv_ref, dbl_ref), grads):
                r[rows[g], :] = o[g]

    return _call_carrying(body, exch, name="gdn_intra_bwd", out_shape=(_sds((2, GH, T, HD)),) * 4,
                          grid=(2, GH, nc // G), in_specs=[qkv_s, qkv_s, qkv_s, bl_s, pm, big, big, big, big, pm, gl_s],
                          out_specs=(big,) * 4)(q, k, v, bl, xinv, *cts)


def _scan_fn(s, u, w, kd, qd, p, gl):
    each = lambda f, *ls: [f(*t) for t in zip(*ls)]
    ws = each(_nn, w, s)
    delta = each(lambda a, b: a - b, u, ws)
    kdd = each(_tn, kd, delta)
    s_new = each(lambda g, a, b: g * a + b, gl, s, kdd)
    qs = each(_nn, qd, s)
    pd = each(_nn, p, delta)
    return each(lambda a, b: a + b, qs, pd), s_new


SCAN_BLOCK = 4


def _visit_pos(c, d, nc, ncc):
    back = ncc - 1 - c if c < ncc else ncc + (nc - 1 - c)
    return jnp.where(d == 0, c, back)


def _scan_specs(T, L, back):
    tb = SCAN_BLOCK * CH
    assert T % tb == 0 and L % tb == 0
    nb, ncb = T // tb, L // tb
    at = (lambda t: nb - 1 - t) if back else (lambda t: t)
    big = pl.BlockSpec((2, GH, tb, HD), lambda t: (0, 0, at(t), 0))
    pm = pl.BlockSpec((2, GH, tb, CH), lambda t: (0, 0, at(t), 0))
    gl = pl.BlockSpec((2, GH, SCAN_BLOCK, 1, HD), lambda t: (0, 0, at(t), 0, 0))
    st = pl.BlockSpec((2, GH, SCAN_BLOCK, HD, HD), lambda t: (0, 0, at(t), 0, 0))

    def natural(b):
        return jnp.where(b < ncb, ncb - 1 - b, nb - 1 - (b - ncb))

    do_specs = (pl.BlockSpec((GH, tb, HD), lambda t: (0, at(t), 0)),
                pl.BlockSpec((GH, tb, HD), lambda t: (0, natural(at(t)), 0)))
    return nb, big, pm, gl, st, do_specs


SCAN_STREAMS = [(d, h) for d in (0, 1) for h in range(GH)]


def _scan_fwd(u, w, kd, qd, p, gl, L):
    T = u.shape[2]
    nb, big, pm, gl_s, st, _ = _scan_specs(T, L, False)

    def body(u_ref, w_ref, kd_ref, qd_ref, p_ref, gl_ref, o_ref, st_ref, s_scr):
        @pl.when(pl.program_id(0) == 0)
        def _():
            s_scr[...] = jnp.zeros_like(s_scr)

        s = [s_scr[d, h] for d, h in SCAN_STREAMS]
        for i in range(SCAN_BLOCK):
            rows = slice(i * CH, (i + 1) * CH)
            for (d, h), sv in zip(SCAN_STREAMS, s):
                st_ref[d, h, i] = sv
            o, s = _scan_fn(s, *[[r[d, h, rows, :].astype(F32) for d, h in SCAN_STREAMS]
                                 for r in (u_ref, w_ref, kd_ref, qd_ref, p_ref)],
                            [gl_ref[d, h, i] for d, h in SCAN_STREAMS])
            for (d, h), ov in zip(SCAN_STREAMS, o):
                o_ref[d, h, rows, :] = ov
        for (d, h), sv in zip(SCAN_STREAMS, s):
            s_scr[d, h] = sv

    return _call(body, name="gdn_scan_fwd", out_shape=(_sds((2, GH, T, HD)), _sds((2, GH, T // CH, HD, HD))),
                 grid=(nb,), in_specs=[big, big, big, big, pm, gl_s], out_specs=(big, st),
                 scratch=[pltpu.VMEM((2, GH, HD, HD), F32)], sem=("arbitrary",), vmem=VMEM_BIG)(u, w, kd, qd, p, gl)


def _scan_bwd(u, w, kd, qd, p, gl, states, do, L, exch):
    T = u.shape[2]
    nb, big, pm, gl_s, st, do_specs = _scan_specs(T, L, True)

    def body(u_ref, w_ref, kd_ref, qd_ref, p_ref, gl_ref, st_ref, do0_ref, do1_ref,
             du_ref, dw_ref, dkd_ref, dqd_ref, dp_ref, dgl_ref, ds_scr):
        @pl.when(pl.program_id(0) == 0)
        def _():
            ds_scr[...] = jnp.zeros_like(ds_scr)

        ds = [ds_scr[d, h] for d, h in SCAN_STREAMS]
        for i in reversed(range(SCAN_BLOCK)):
            rows = slice(i * CH, (i + 1) * CH)
            mirror = slice((SCAN_BLOCK - 1 - i) * CH, (SCAN_BLOCK - i) * CH)
            _, vjp = jax.vjp(_scan_fn, [st_ref[d, h, i] for d, h in SCAN_STREAMS],
                             *[[r[d, h, rows, :].astype(F32) for d, h in SCAN_STREAMS]
                               for r in (u_ref, w_ref, kd_ref, qd_ref, p_ref)],
                             [gl_ref[d, h, i] for d, h in SCAN_STREAMS])
            dos = [do0_ref[h, rows, :] if d == 0 else do1_ref[h, mirror, :] for d, h in SCAN_STREAMS]
            ds, gu, gw, gkd, gqd, gp, ggl = vjp((dos, ds))
            for n, (d, h) in enumerate(SCAN_STREAMS):
                du_ref[d, h, rows, :] = gu[n]
                dw_ref[d, h, rows, :] = gw[n]
                dkd_ref[d, h, rows, :] = gkd[n]
                dqd_ref[d, h, rows, :] = gqd[n]
                dp_ref[d, h, rows, :] = gp[n]
                dgl_ref[d, h, i] = ggl[n]
        for (d, h), dv in zip(SCAN_STREAMS, ds):
            ds_scr[d, h] = dv

    return _call_carrying(
        body, exch, name="gdn_scan_bwd",
        out_shape=(_sds((2, GH, T, HD)),) * 4 + (_sds((2, GH, T, CH)), _sds((2, GH, T // CH, 1, HD))),
        grid=(nb,), in_specs=[big, big, big, big, pm, gl_s, st, *do_specs], out_specs=(big, big, big, big, pm, gl_s),
        scratch=[pltpu.VMEM((2, GH, HD, HD), F32)], vmem=VMEM_BIG)(u, w, kd, qd, p, gl, states, do, do)


def _gout_fn(o0, o1, z, gw):
    return _rms(o0 + o1) * gw * _silu(z)


def _backward_latent(o_ref, L):
    nl = (o_ref.shape[1] - L) // CH
    return jnp.concatenate([o_ref[1, L + (nl - 1 - j) * CH:L + (nl - j) * CH, :] for j in range(nl)], axis=0)


def _gout_fwd(o, proj, gw, L):
    T = o.shape[2]
    N = T - L
    ob = pl.BlockSpec((2, None, T, HD), lambda h: (0, h, 0, 0))

    def body(o_ref, z_ref, gw_ref, y_ref):
        y_ref[...] = _gout_fn(o_ref[0, L:, :], _backward_latent(o_ref, L), z_ref[L:, :], gw_ref[...]).astype(BF16)

    return _call(body, name="gout_fwd", out_shape=_sds((N, GH * HD), BF16), grid=(GH,),
                 in_specs=[ob, pl.BlockSpec((T, HD), lambda h: (0, C_Z // HD + h)), pl.BlockSpec((1, HD), lambda h: (0, 0))],
                 out_specs=pl.BlockSpec((N, HD), lambda h: (0, h)), sem=("parallel",))(o, proj, gw)


def _gout_bwd(o, proj, gw, dy, dproj, L):
    T = o.shape[2]
    N = T - L
    ob = pl.BlockSpec((2, None, T, HD), lambda h: (0, h, 0, 0))

    def body(o_ref, z_ref, gw_ref, dy_ref, _, do_ref, dz_ref, dgw_ref):
        _, vjp = jax.vjp(_gout_fn, o_ref[0, L:, :], _backward_latent(o_ref, L), z_ref[L:, :], gw_ref[...])
        g0, _, gz, ggw = vjp(dy_ref[...])
        do_ref[:L, :] = jnp.zeros((L, HD), F32)
        do_ref[L:, :] = g0
        dz_ref[:L, :] = jnp.zeros((L, HD), BF16)
        dz_ref[L:, :] = gz.astype(BF16)

        @pl.when(pl.program_id(0) == 0)
        def _():
            dgw_ref[...] = jnp.zeros_like(dgw_ref)

        dgw_ref[...] += ggw

    zb = pl.BlockSpec((T, HD), lambda h: (0, C_Z // HD + h))
    return _call(body, name="gout_bwd", out_shape=(_sds((GH, T, HD)), _sds(dproj.shape, BF16), _sds((1, HD))),
                 grid=(GH,),
                 in_specs=[ob, zb, pl.BlockSpec((1, HD), lambda h: (0, 0)), pl.BlockSpec((N, HD), lambda h: (0, h)), ANYSPEC],
                 out_specs=(pl.BlockSpec((None, T, HD), lambda h: (h, 0, 0)), zb, pl.BlockSpec((1, HD), lambda h: (0, 0))),
                 aliases={4: 1}, sem=("arbitrary",))(o, proj, gw, dy, dproj)


def _merge_fn(pa, pd, ga, gd):
    return jax.nn.sigmoid(ga) * pa + jax.nn.sigmoid(gd) * pd


def _merge_fwd(pa, pd, proj, L, *, br=256):
    N = pa.shape[0]
    lb = L // br
    row = pl.BlockSpec((br, D), lambda i: (i, 0))

    def body(pa_ref, pd_ref, ga_ref, gd_ref, y_ref):
        y_ref[...] = _merge_fn(pa_ref[...], pd_ref[...], ga_ref[...], gd_ref[...]).astype(BF16)

    return _call(body, name="merge_fwd", out_shape=_sds((N, D), BF16), grid=(N // br,),
                 in_specs=[row, row, pl.BlockSpec((br, D), lambda i: (i + lb, C_GATE // D)),
                           pl.BlockSpec((br, D), lambda i: (i + lb, C_GATE // D + 1))],
                 out_specs=row, sem=("parallel",))(pa, pd, proj, proj)


def _merge_bwd(pa, pd, proj, dy, L, *, br=256):
    N = pa.shape[0]
    T = N + L
    lb = L // br
    lrow = pl.BlockSpec((br, D), lambda i: (jnp.maximum(i - lb, 0), 0))

    def body(pa_ref, pd_ref, ga_ref, gd_ref, dy_ref, dpa_ref, dpd_ref, dg_ref):
        lat = pl.program_id(0) >= lb
        _, vjp = jax.vjp(_merge_fn, pa_ref[...], pd_ref[...], ga_ref[...], gd_ref[...])
        gpa, gpd, gga, ggd = vjp(dy_ref[...])
        dpa_ref[...] = gpa.astype(BF16)
        dpd_ref[...] = gpd.astype(BF16)
        dg_ref[:, :D] = jnp.where(lat, gga, 0.0).astype(BF16)
        dg_ref[:, D:] = jnp.where(lat, ggd, 0.0).astype(BF16)

    return _call(body, name="merge_bwd", out_shape=(_sds((N, D), BF16), _sds((N, D), BF16), _sds((T, C_END), BF16)),
                 grid=(T // br,),
                 in_specs=[lrow, lrow, pl.BlockSpec((br, D), lambda i: (i, C_GATE // D)),
                           pl.BlockSpec((br, D), lambda i: (i, C_GATE // D + 1)), lrow],
                 out_specs=(lrow, lrow, pl.BlockSpec((br, 2 * D), lambda i: (i, C_GATE // (2 * D)))),
                 sem=("arbitrary",))(pa, pd, proj, proj, dy)


def _resid_fwd(x, m, mod, i_g, *, name, br=256):
    R = x.shape[0]
    row = pl.BlockSpec((br, D), lambda i: (i, 0))

    def body(x_ref, m_ref, mod_ref, o_ref):
        o_ref[...] = x_ref[...] + mod_ref[i_g:i_g + 1, :] * m_ref[...]

    return _call(body, name=name, out_shape=_sds((R, D)), grid=(R // br,),
                 in_specs=[row, row, pl.BlockSpec((6, D), lambda i: (0, 0))], out_specs=row,
                 sem=("parallel",))(x, m, mod)


def _resid_bwd(dx, m, mod, i_g, *, name, br=256):
    R = dx.shape[0]
    row = pl.BlockSpec((br, D), lambda i: (i, 0))
    vec = pl.BlockSpec((1, D), lambda i: (0, 0))

    def body(dx_ref, m_ref, mod_ref, dm_ref, dg_ref):
        dxv = dx_ref[...]
        dm_ref[...] = (dxv * mod_ref[i_g:i_g + 1, :]).astype(BF16)

        @pl.when(pl.program_id(0) == 0)
        def _():
            dg_ref[...] = jnp.zeros_like(dg_ref)

        dg_ref[...] += jnp.sum(dxv * m_ref[...], axis=0, keepdims=True)

    return _call(body, name=name, out_shape=(_sds((R, D), BF16), _sds((1, D))), grid=(R // br,),
                 in_specs=[row, row, pl.BlockSpec((6, D), lambda i: (0, 0))], out_specs=(row, vec),
                 sem=("arbitrary",))(dx, m, mod)


def _ffn_fn(shifts, ug, uv, wg, wv, bg, bv):
    down, up = shifts

    def conv(x, w, b):
        return down(x) * w[0:1, :] + x * w[1:2, :] + up(x) * w[2:3, :] + b

    return _silu(conv(ug, wg, bg)) * conv(uv, wv, bv)


def _ffn_fwd(up, cw, cb, *, bw=256):
    N = up.shape[0]
    shifts = _make_shift(((0, N),))
    nb = DFF // bw

    def body(ug, uv, wg, wv, bg, bv, a_ref):
        a_ref[...] = _ffn_fn(shifts, ug[...], uv[...], wg[...], wv[...], bg[...], bv[...]).astype(BF16)

    def col(rows, off):
        return pl.BlockSpec((rows, bw), lambda j: (0, j + off))

    return _call(body, name="ffn_fwd", out_shape=_sds((N, DFF), BF16), grid=(nb,),
                 in_specs=[col(N, 0), col(N, nb), col(3, 0), col(3, nb), col(1, 0), col(1, nb)],
                 out_specs=col(N, 0), sem=("parallel",), vmem=VMEM_BIG)(up, up, cw, cw, cb, cb)


def _ffn_bwd(up, cw, cb, da, *, bw=256):
    N = up.shape[0]
    shifts = _make_shift(((0, N),))
    nb = DFF // bw

    def body(ug, uv, wg, wv, bg, bv, da_ref, dug, duv, dwg, dwv, dbg, dbv):
        _, vjp = jax.vjp(functools.partial(_ffn_fn, shifts), ug[...], uv[...], wg[...], wv[...], bg[...], bv[...])
        g = vjp(da_ref[...])
        dug[...] = g[0].astype(BF16)
        duv[...] = g[1].astype(BF16)
        dwg[...], dwv[...], dbg[...], dbv[...] = g[2], g[3], g[4], g[5]

    def col(rows, off):
        return pl.BlockSpec((rows, bw), lambda j: (0, j + off))

    half = (_sds((N, DFF), BF16), _sds((N, DFF), BF16), _sds((3, DFF)), _sds((3, DFF)), _sds((1, DFF)), _sds((1, DFF)))
    dug, duv, dwg, dwv, dbg, dbv = _call(
        body, name="ffn_bwd", out_shape=half, grid=(nb,),
        in_specs=[col(N, 0), col(N, nb), col(3, 0), col(3, nb), col(1, 0), col(1, nb), col(N, 0)],
        out_specs=(col(N, 0), col(N, 0), col(3, 0), col(3, 0), col(1, 0), col(1, 0)),
        sem=("parallel",), vmem=VMEM_BIG)(up, up, cw, cw, cb, cb, da)
    return (jnp.concatenate([dug, duv], axis=1), jnp.concatenate([dwg, dwv], axis=1),
            jnp.concatenate([dbg, dbv], axis=1))


def _head_fn(x1, dn, g2, fw, tgt):
    y = _rms(x1 + g2 * dn) * fw
    err = y - tgt
    return 0.5 * jnp.sum(jnp.mean(err * err, axis=-1))


def _head(x1, dn, mod, fw, tgt, *, br=256):
    N = x1.shape[0]
    row = pl.BlockSpec((br, D), lambda i: (i, 0))
    vec = pl.BlockSpec((1, D), lambda i: (0, 0))
    one = pl.BlockSpec((1, HD), lambda i: (0, 0))

    def body(x1_ref, dn_ref, mod_ref, fw_ref, tgt_ref, loss_ref, dx_ref, ddn_ref, dg_ref, dfw_ref):
        loss, (gx, gdn, gg, gfw) = jax.value_and_grad(_head_fn, argnums=(0, 1, 2, 3))(
            x1_ref[...], dn_ref[...], mod_ref[5:6, :], fw_ref[...], tgt_ref[...])
        dx_ref[...] = gx
        ddn_ref[...] = gdn.astype(BF16)

        @pl.when(pl.program_id(0) == 0)
        def _():
            loss_ref[...] = jnp.zeros_like(loss_ref)
            dg_ref[...] = jnp.zeros_like(dg_ref)
            dfw_ref[...] = jnp.zeros_like(dfw_ref)

        loss_ref[...] += jnp.broadcast_to(loss, (1, HD))
        dg_ref[...] += gg
        dfw_ref[...] += gfw

    return _call(body, name="head", out_shape=(_sds((1, HD)), _sds((N, D)), _sds((N, D), BF16), _sds((1, D)), _sds((1, D))),
                 grid=(N // br,), in_specs=[row, row, pl.BlockSpec((6, D), lambda i: (0, 0)), vec, row],
                 out_specs=(one, row, row, vec, vec), sem=("arbitrary",))(x1, dn, mod, fw, tgt)


def _adamw(w, g, m, v, *, name):
    shape = w.shape
    cols = shape[-1]
    rows = max(1, math.prod(shape[:-1]))
    w2, g2, m2, v2 = (t.reshape(rows, cols) for t in (w, g, m, v))
    br = 256 if rows % 256 == 0 else rows
    c1 = 1.0 - B1 ** STEP
    c2 = 1.0 - B2 ** STEP

    def body(w_ref, g_ref, m_ref, v_ref, d_ref, nm_ref, nv_ref):
        gv = g_ref[...]
        nm = B1 * m_ref[...] + (1.0 - B1) * gv
        nv = B2 * v_ref[...] + (1.0 - B2) * (gv * gv)
        d_ref[...] = -LR * ((nm / c1) / (jnp.sqrt(nv / c2) + AEPS) + WD * w_ref[...])
        nm_ref[...] = nm
        nv_ref[...] = nv

    blk = pl.BlockSpec((br, cols), lambda i: (i, 0))
    outs = _call(body, name=name, out_shape=(_sds((rows, cols)),) * 3, grid=(rows // br,),
                 in_specs=[blk] * 4, out_specs=(blk,) * 3, sem=("parallel",))(w2, g2, m2, v2)
    return tuple(t.reshape(shape) for t in outs)


def _adamw_many(items, *, name):
    k = len(items)
    shapes = [w.shape for w, _, _, _ in items]
    flat = [t.reshape(max(1, math.prod(t.shape[:-1])), t.shape[-1]) for it in items for t in it]
    c1 = 1.0 - B1 ** STEP
    c2 = 1.0 - B2 ** STEP

    def body(*refs):
        ins, outs = refs[:4 * k], refs[4 * k:]
        for i in range(k):
            w_ref, g_ref, m_ref, v_ref = ins[4 * i:4 * i + 4]
            gv = g_ref[...]
            nm = B1 * m_ref[...] + (1.0 - B1) * gv
            nv = B2 * v_ref[...] + (1.0 - B2) * (gv * gv)
            outs[3 * i][...] = -LR * ((nm / c1) / (jnp.sqrt(nv / c2) + AEPS) + WD * w_ref[...])
            outs[3 * i + 1][...] = nm
            outs[3 * i + 2][...] = nv

    res = _call(body, name=name, out_shape=tuple(_sds(flat[4 * i].shape) for i in range(k) for _ in range(3)))(*flat)
    return [tuple(res[3 * i + j].reshape(shapes[i]) for j in range(3)) for i in range(k)]


def _rope_tables(N, L):
    t = jnp.arange(N)
    pos = jnp.stack([(t // GRID_W).astype(F32), (t % GRID_W).astype(F32)], axis=1)
    inv = ROPE_THETA ** (-jnp.arange(0, HD // 2, 2, dtype=F32) / (HD // 2))
    ang = pos[:, :, None] * inv[None, None, :]
    cos = jnp.broadcast_to(jnp.cos(ang)[:, :, None, :], (N, 2, 2, HD // 4)).reshape(N, HD)
    sin = jnp.broadcast_to(jnp.sin(ang)[:, :, None, :], (N, 2, 2, HD // 4))
    sin = (sin * jnp.array([-1.0, 1.0], F32)[None, None, :, None]).reshape(N, HD)
    cos = jnp.concatenate([jnp.ones((L, HD), F32), cos], axis=0)
    sin = jnp.concatenate([jnp.zeros((L, HD), F32), sin], axis=0)
    return cos, sin


def _pad_lanes(v, off=0):
    return jnp.zeros((1, HD), F32).at[0, off:off + v.shape[0]].set(v)


def _local_step(x, ctx, tgt, mod_lat, mod_ctx, w_in, shards, small):
    N, L = x.shape[0], ctx.shape[0]
    T = N + L
    bounds = ((0, L), (L, T))
    qw, kw, gw = small["q_norm_w"], small["k_norm_w"], small["gdn_norm_w"]
    conv_w, ffn_w, ffn_b, fnw = small["conv_qkv_w"], small["ffn_conv_w"], small["ffn_conv_b"], small["final_norm_w"]
    alog = _pad_lanes(small["a_log"].reshape(-1), 2 * GH)
    dtb = _pad_lanes(small["dt_bias"].reshape(-1), 2 * GH)
    cos, sin = _rope_tables(N, L)
    bt = T
    bnl = 256 if N % 1024 else 1024

    hc = _normmod_fwd(ctx, mod_ctx, 0, 1, name="normmod_ctx")
    hx = _normmod_fwd(x, mod_lat, 0, 1, name="normmod_x")
    h1 = jnp.concatenate([hc, hx], axis=0)
    proj = _mm(h1, w_in, name="mm_in", M=T, N=C_END, K=D, tb=True, bm=bt, bn=1024)
    aq, ak, av = _aprep_fwd(proj, cos, sin, qw, kw)
    (attn, attn32, lse), (up_g,) = _attn_fwd(aq, ak, av, L, _GatherTwoLevel([shards["w_up"]]))
    gq = _gprep_fwd(proj, conv_w, 0, bounds)
    gk = _gprep_fwd(proj, conv_w, 1, bounds)
    gv = _gprep_fwd(proj, conv_w, 2, bounds)
    bl = _bl_fwd(proj, alog, dtb)
    intra, (down_g, pa_g, pd_g, out_g) = _intra_fwd(
        gq, gk, gv, bl, L, _GatherTwoLevel([shards[n] for n in ("w_down", "w_pa", "w_pd", "w_out")]))
    w_up, w_down = up_g.reshape(2 * DFF, D), down_g.reshape(DFF, D)
    w_pa, w_pd, w_out = pa_g.reshape(D, D), pd_g.reshape(D, D), out_g.reshape(D, D)
    xinv, intra = intra[6], intra[:6]
    o, states = _scan_fwd(*intra, L)
    gdn = _gout_fwd(o, proj, gw, L)
    pa = _mm(attn, w_pa, name="mm_pa", M=N, N=D, K=D, bm=bnl)
    pd = _mm(gdn, w_pd, name="mm_pd", M=N, N=D, K=D, bm=bnl)
    y = _merge_fwd(pa, pd, proj, L)
    m = _mm(y, w_out, name="mm_out", M=N, N=D, K=D, bm=bnl)
    x1 = _resid_fwd(x, m, mod_lat, 2, name="resid1")
    h2 = _normmod_fwd(x1, mod_lat, 3, 4, name="normmod_x1")
    up = _mm(h2, w_up, name="mm_up", M=N, N=2 * DFF, K=D, tb=True, bm=bnl, bn=2 * DFF // 4)
    a = _ffn_fwd(up, ffn_w, ffn_b)
    dn = _mm(a, w_down, name="mm_down", M=N, N=D, K=DFF, bm=bnl)
    loss, dx2, ddn, dg2, dfnw = _head(x1, dn, mod_lat, fnw, tgt)

    early = {}
    g_down = _mm(a, ddn, name="mm_down_dw", M=DFF, N=D, K=N, ta=True, bm=DFF // 2, out_dtype=BF16)
    early["w_down"], token = _send_early(g_down.reshape(NDEV, DFF // NDEV, D), name="scatter_g_down")
    da = _mm(ddn, w_down, name="mm_down_dx", M=N, N=DFF, K=D, tb=True, bm=bnl, bn=DFF // 2, after=(token,))
    dup, d_ffn_w, d_ffn_b = _ffn_bwd(up, ffn_w, ffn_b, da)
    g_up = _mm(dup, h2, name="mm_up_dw", M=2 * DFF, N=D, K=N, ta=True, bm=2 * DFF // 4, out_dtype=BF16)
    early["w_up"], token = _send_early(g_up.reshape(NDEV, 2 * DFF // NDEV, D), name="scatter_g_up")
    dh2 = _mm(dup, w_up, name="mm_up_dx", M=N, N=D, K=2 * DFF, bm=bnl, bk=2 * DFF // 4, after=(token,))
    dx1, dsh2, dsc2 = _normmod_bwd(x1, mod_lat, 3, 4, dh2, 0, dx2, name="normmod_x1_bwd")
    dm, dg1 = _resid_bwd(dx1, m, mod_lat, 2, name="resid1_bwd")
    dy = _mm(dm, w_out, name="mm_out_dx", M=N, N=D, K=D, tb=True, bm=bnl)
    g_out = _mm(y, dm, name="mm_out_dw", M=D, N=D, K=N, ta=True, out_dtype=BF16)
    dpa, dpd, dproj = _merge_bwd(pa, pd, proj, dy, L)
    dattn = _mm(dpa, w_pa, name="mm_pa_dx", M=N, N=D, K=D, tb=True, bm=bnl)
    g_pa = _mm(attn, dpa, name="mm_pa_dw", M=D, N=D, K=N, ta=True, out_dtype=BF16)
    dgdn = _mm(dpd, w_pd, name="mm_pd_dx", M=N, N=D, K=D, tb=True, bm=bnl)
    g_pd = _mm(gdn, dpd, name="mm_pd_dw", M=D, N=D, K=N, ta=True, out_dtype=BF16)
    do, dproj, dgw = _gout_bwd(o, proj, gw, dgdn, dproj, L)
    cts, recv = _scan_bwd(*intra, states, do, L, _Exchange(
        [g_out.reshape(NDEV, D // NDEV, D), g_pa.reshape(NDEV, D // NDEV, D), g_pd.reshape(NDEV, D // NDEV, D)], True))
    recv = dict(zip(("w_out", "w_pa", "w_pd"), recv))
    (dgq, dgk, dgv, dbl), _ = _intra_bwd(gq, gk, gv, bl, xinv, cts, L, _NoExchange())
    dproj, dwq = _gprep_bwd(proj, conv_w, 0, bounds, dgq, dproj)
    dproj, dwk = _gprep_bwd(proj, conv_w, 1, bounds, dgk, dproj)
    dproj, dwv = _gprep_bwd(proj, conv_w, 2, bounds, dgv, dproj)
    dproj, dalog, ddtb = _bl_bwd(proj, alog, dtb, dbl, dproj)
    (daq_h, dak_h, dav_h), _ = _attn_bwd(aq, ak, av, attn32, lse, dattn, L, _NoExchange())
    dproj, dqw, dkw = _aprep_bwd(proj, cos, sin, qw, kw, daq_h, dak_h, dav_h, dproj, L)
    g_in = _mm(dproj, h1, name="mm_in_dw", M=C_END, N=D, K=T, ta=True, bm=1024, out_dtype=BF16)
    g_in = _unpad_columns(g_in).reshape(NDEV, W_END // NDEV, D)
    own_in = lax.dynamic_index_in_dim(g_in, _position()[3], axis=0, keepdims=False)
    *pending, token = _scatter_start(g_in, None, (0, D // 2), (), name="scatter_g_in_a_start")
    dh1 = _mm(dproj, w_in, name="mm_in_dx", M=T, N=D, K=C_END, bm=bt, bk=1024, after=(token,))
    grad_x, dsh1, dsc1 = _normmod_bwd(x, mod_lat, 0, 1, dh1, L, dx1, name="normmod_x_bwd")
    _, dcsh1, dcsc1 = _normmod_bwd(ctx, mod_ctx, 0, 1, dh1, 0, None, name="normmod_ctx_bwd")

    z1 = jnp.zeros((1, D), F32)
    dmod_lat = jnp.concatenate([dsh1, dsc1, dg1, dsh2, dsc2, dg2], axis=0)
    dmod_ctx = jnp.concatenate([dcsh1, dcsc1, z1, z1, z1, z1], axis=0)
    gsmall = {
        "q_norm_w": dqw, "k_norm_w": dkw, "gdn_norm_w": dgw,
        "conv_qkv_w": jnp.concatenate([dwq, dwk, dwv], axis=1),
        "a_log": dalog[0, 2 * GH:4 * GH], "dt_bias": ddtb[0, 2 * GH:4 * GH],
        "ffn_conv_w": d_ffn_w, "ffn_conv_b": d_ffn_b, "final_norm_w": dfnw,
    }
    early["w_in"] = (pending, own_in)
    return loss[0, 0], grad_x, early, recv, dmod_lat, dmod_ctx, gsmall


HBM = pl.BlockSpec(memory_space=pltpu.HBM)
ANYSPEC = pl.BlockSpec(memory_space=pl.ANY)


def _position():
    x, y, c = lax.axis_index("x"), lax.axis_index("y"), lax.axis_index("c")
    return x, y, c, 4 * x + 2 * y + c


def _peer(x, y, c, k):
    px = 1 - x if k & 4 else x
    py = 1 - y if k & 2 else y
    pc = 1 - c if k & 1 else c
    return (px, py, pc), 4 * px + 2 * py + pc


def _exchange(arrs, *, name, scatter):
    exch = _Exchange(arrs, scatter)
    n = exch.n

    def body(*refs):
        ins, outs, sems = refs[:n], refs[n:2 * n], refs[2 * n:]
        exch.start(ins, outs, sems)
        exch.finish(ins, outs, sems)

    outs = pl.pallas_call(body, name=name, out_shape=exch.out_shape, in_specs=[HBM] * n, out_specs=(HBM,) * n,
                          scratch_shapes=exch.scratch,
                          compiler_params=pltpu.CompilerParams(has_side_effects=True))(*arrs)
    return list(outs)


class _Exchange:
    def __init__(self, arrs, scatter):
        self.arrs, self.scatter, self.n = list(arrs), scatter, len(arrs)
        self.out_shape = tuple(_sds(a.shape if scatter else (NDEV,) + a.shape, a.dtype) for a in arrs)
        self.scratch = [pltpu.SemaphoreType.DMA((self.n, NDEV - 1)), pltpu.SemaphoreType.DMA((self.n, NDEV - 1)),
                        pltpu.SemaphoreType.DMA((self.n,))]

    def _copies(self, ins, outs, sems):
        send, recv, loc = sems
        x, y, c, me = _position()
        local = [pltpu.make_async_copy(ins[a].at[me] if self.scatter else ins[a], outs[a].at[me], loc.at[a])
                 for a in range(self.n)]
        remote = []
        for k in range(1, NDEV):
            peer, pid = _peer(x, y, c, k)
            for a in range(self.n):
                src = ins[a].at[pid] if self.scatter else ins[a]
                remote.append(pltpu.make_async_remote_copy(
                    src_ref=src, dst_ref=outs[a].at[me], send_sem=send.at[a, k - 1], recv_sem=recv.at[a, k - 1],
                    device_id=peer, device_id_type=MESH))
        return local, remote

    def start(self, ins, outs, sems):
        local, remote = self._copies(ins, outs, sems)
        for cp in local + remote:
            cp.start()

    def finish(self, ins, outs, sems):
        local, remote = self._copies(ins, outs, sems)
        for cp in remote:
            cp.wait()
        for cp in local:
            cp.wait()


class _NoExchange:
    arrs, n, out_shape, scratch = [], 0, (), []

    def start(self, ins, outs, sems):
        pass

    def finish(self, ins, outs, sems):
        pass


def _send_early(parts, *, name):
    own = lax.dynamic_index_in_dim(parts, _position()[3], axis=0, keepdims=False)
    *pending, token = _scatter_start(parts, None, (0, parts.shape[-1]), (), name=name + "_start")
    return (pending, own), token


def _receive(pending, after, *, name):
    send_sems, recv_sems, src, land = pending
    return _scatter_wait(send_sems, recv_sems, src, land, (0, src.shape[-1]), after, name=name + "_wait")[1]


class _GatherTwoLevel:
    scatter = False

    def __init__(self, arrs):
        self.arrs, self.n = list(arrs), len(arrs)
        self.out_shape = tuple(_sds((NDEV,) + a.shape, a.dtype) for a in arrs)
        self.scratch = [pltpu.SemaphoreType.DMA((self.n, NDEV - 1)), pltpu.SemaphoreType.DMA((self.n, NDEV - 1)),
                        pltpu.SemaphoreType.DMA((self.n,))]

    def _parts(self, ins, outs, sems):
        send, recv, loc = sems
        x, y, c, _ = _position()
        me, sibling = (x, y, c), (x, y, 1 - c)
        chips = [(1 - x, y), (x, 1 - y), (1 - x, 1 - y)]
        parts = []
        for a in range(self.n):
            slot = lambda px, py, pc, a=a: outs[a].at[4 * px + 2 * py + pc]

            def copy(k, owner, to, src=None, a=a, slot=slot):
                return pltpu.make_async_remote_copy(
                    src_ref=slot(*owner) if src is None else src, dst_ref=slot(*owner), send_sem=send.at[a, k],
                    recv_sem=recv.at[a, k], device_id=to, device_id_type=MESH)

            parts.append(dict(
                mine=pltpu.make_async_copy(ins[a], slot(*me), loc.at[a]),
                first=[copy(0, me, sibling, src=ins[a])] + [copy(1 + j, me, (*ch, c), src=ins[a]) for j, ch in enumerate(chips)],
                arrive=[copy(1 + j, (*ch, c), me) for j, ch in enumerate(chips)],
                passed=[copy(4 + j, (*ch, c), sibling) for j, ch in enumerate(chips)],
                rest=[copy(0, sibling, me)] + [copy(4 + j, (*ch, 1 - c), me) for j, ch in enumerate(chips)]))
        return parts

    def start(self, ins, outs, sems):
        for p in self._parts(ins, outs, sems):
            p["mine"].start()
            for cp in p["first"]:
                cp.start()

    def middle(self, ins, outs, sems):
        for p in self._parts(ins, outs, sems):
            for got, fwd in zip(p["arrive"], p["passed"]):
                got.wait_recv()
                fwd.start()

    def finish(self, ins, outs, sems):
        for p in self._parts(ins, outs, sems):
            for cp in p["rest"]:
                cp.wait_recv()
            for cp in p["first"] + p["passed"]:
                cp.wait_send()
            p["mine"].wait()


def _gather_two_level(blocks, *, name):
    exch = _GatherTwoLevel(blocks)
    n = exch.n

    def body(*refs):
        ins, outs, sems = refs[:n], refs[n:2 * n], refs[2 * n:]
        exch.start(ins, outs, sems)
        exch.middle(ins, outs, sems)
        exch.finish(ins, outs, sems)

    outs = pl.pallas_call(body, name=name, out_shape=exch.out_shape, in_specs=[HBM] * n, out_specs=(HBM,) * n,
                          scratch_shapes=exch.scratch,
                          compiler_params=pltpu.CompilerParams(has_side_effects=True))(*blocks)
    return list(outs)


SEM = pl.BlockSpec(memory_space=pltpu.SEMAPHORE)


def _scatter_copies(src_ref, land_ref, send_sems, recv_sems, cols):
    x, y, c, me = _position()
    span = (slice(None), pl.ds(*cols))
    copies = []
    for k in range(1, NDEV):
        peer, pid = _peer(x, y, c, k)
        copies.append(pltpu.make_async_remote_copy(
            src_ref=src_ref.at[pid].at[span], dst_ref=land_ref.at[me].at[span], send_sem=send_sems.at[k - 1],
            recv_sem=recv_sems.at[k - 1], device_id=peer, device_id_type=MESH))
    return copies


SPLIT_EFFECT = pltpu.SideEffectType.DATAFLOW_SIDE_EFFECTING


def _scatter_start(parts, land, cols, after, *, name):
    na = len(after)
    if land is None:
        land = lax.empty(parts.shape, parts.dtype)

    def body(src_ref, land_ref, *rest):
        send_sems, recv_sems, _, _, token = rest[na:]
        for cp in _scatter_copies(src_ref, land_ref, send_sems, recv_sems, cols):
            cp.start()
        token[...] = jnp.zeros_like(token)

    return pl.pallas_call(
        body, name=name,
        out_shape=(pltpu.SemaphoreType.DMA((NDEV - 1,)), pltpu.SemaphoreType.DMA((NDEV - 1,)),
                   pltpu.HBM(parts.shape, parts.dtype), pltpu.HBM(parts.shape, parts.dtype), _sds((8, HD))),
        in_specs=(HBM, HBM) + (pl.BlockSpec(memory_space=pl.ANY),) * na,
        out_specs=(SEM, SEM, HBM, HBM, pl.BlockSpec(memory_space=pltpu.VMEM)),
        input_output_aliases={0: 2, 1: 3}, compiler_params=pltpu.CompilerParams(has_side_effects=SPLIT_EFFECT),
    )(pltpu.with_memory_space_constraint(parts, pltpu.HBM), pltpu.with_memory_space_constraint(land, pltpu.HBM), *after)


def _scatter_wait(send_sems, recv_sems, src_thru, land_thru, cols, after, *, name):
    na = len(after)

    def body(src_ref, land_ref, send_sems, recv_sems, *rest):
        for cp in _scatter_copies(src_ref, land_ref, send_sems, recv_sems, cols):
            cp.wait_send()
            cp.wait_recv()

    return pl.pallas_call(
        body, name=name,
        out_shape=(pltpu.HBM(src_thru.shape, src_thru.dtype), pltpu.HBM(land_thru.shape, land_thru.dtype)),
        in_specs=(HBM, HBM, SEM, SEM) + (pl.BlockSpec(memory_space=pl.ANY),) * na, out_specs=(HBM, HBM),
        input_output_aliases={0: 0, 1: 1}, compiler_params=pltpu.CompilerParams(has_side_effects=SPLIT_EFFECT),
    )(src_thru, land_thru, send_sems, recv_sems, *after)


def _cast_bf16(w, *, name):
    rows, cols = w.shape
    br = 128 if rows % 128 == 0 else rows

    def body(w_ref, o_ref):
        o_ref[...] = w_ref[...].astype(BF16)

    blk = pl.BlockSpec((br, cols), lambda i: (i, 0))
    return _call(body, name=name, out_shape=_sds((rows, cols), BF16), grid=(rows // br,), in_specs=[blk],
                 out_specs=blk, sem=("parallel",))(w)


def _sum_slots(a, *, name):
    _, R, C = a.shape

    def body(a_ref, o_ref):
        s = a_ref[0]
        for d in range(1, NDEV):
            s = s + a_ref[d]
        o_ref[...] = s

    return _call(body, name=name, out_shape=_sds((R, C)))(a)


MODROWS = 16


def _mod_fwd(c9, w, b):
    cols = w.shape[1]

    def body(c_ref, w_ref, b_ref, o_ref):
        o_ref[...] = _nn(_silu(c_ref[...]), w_ref[...]) + b_ref[...]

    return _call(body, name="mod_fwd", out_shape=_sds((MODROWS, cols)))(c9, w, b)


def _mod_bwd(c9, dmy, dall, w):
    cols = w.shape[1]

    def body(c_ref, dmy_ref, dall_ref, w_ref, gw_ref, gb_ref, cp_ref):
        sc = _silu(c_ref[...])
        rows = lax.broadcasted_iota(jnp.int32, (MODROWS, 1), 0)
        d = dmy_ref[...]
        d_ctx = jnp.where(rows == NDEV, d, 0.0)
        sc_ctx = jnp.where(rows == NDEV, sc, 0.0)
        outer = lax.dot_general(sc_ctx, d_ctx, (((0,), (0,)), ((), ())), precision=HI, preferred_element_type=F32)
        gw_ref[...] = _tn(jnp.where(rows < NDEV, sc, 0.0), jnp.where(rows < NDEV, d, 0.0)) + outer
        gb_ref[...] = jnp.sum(dall_ref[...], axis=0, keepdims=True)
        cp_ref[...] = jnp.sum(_nt(d_ctx, w_ref[...]), axis=0, keepdims=True)

    return _call(body, name="mod_bwd", out_shape=(_sds((D, cols)), _sds((1, 6 * D)), _sds((1, D))),
                 vmem=VMEM_BIG)(c9, dmy, dall, w)


def _cctx_finish(parts, c_ctx, after):
    VM = pl.BlockSpec(memory_space=pltpu.VMEM)

    def body(p_ref, c_ref, *rest):
        o_ref = rest[-1]
        s = p_ref[0]
        for d in range(1, NDEV):
            s = s + p_ref[d]
        _, vjp = jax.vjp(_silu, c_ref[...])
        o_ref[...] = vjp(s)[0]

    return _call(body, name="cctx_finish", out_shape=_sds((1, D)),
                 in_specs=[VM, VM] + [pl.BlockSpec(memory_space=pl.ANY)] * len(after))(parts, c_ctx, *after)


def _adamw_recv(w, recv, m, v, *, name, own=None):
    rows, cols = w.shape
    bc = 256
    c1 = 1.0 - B1 ** STEP
    c2 = 1.0 - B2 ** STEP
    has_own = own is not None

    def body(w_ref, r_ref, m_ref, v_ref, *rest):
        g_ref, d_ref, nm_ref, nv_ref = rest[-4:]
        me = _position()[3]

        def slot(d):
            return jnp.where(me == d, rest[0][...], r_ref[d]) if has_own else r_ref[d]

        gv = slot(0).astype(F32)
        for d in range(1, NDEV):
            gv = gv + slot(d).astype(F32)
        nm = B1 * m_ref[...] + (1.0 - B1) * gv
        nv = B2 * v_ref[...] + (1.0 - B2) * (gv * gv)
        g_ref[...] = gv
        d_ref[...] = -LR * ((nm / c1) / (jnp.sqrt(nv / c2) + AEPS) + WD * w_ref[...])
        nm_ref[...] = nm
        nv_ref[...] = nv

    blk = pl.BlockSpec((rows, bc), lambda j: (0, j))
    return _call(body, name=name, out_shape=(_sds((rows, cols)),) * 4, grid=(cols // bc,),
                 in_specs=[blk, pl.BlockSpec((NDEV, rows, bc), lambda j: (0, 0, j)), blk, blk] + [blk] * has_own,
                 out_specs=(blk,) * 4, sem=("parallel",), vmem=VMEM_BIG)(w, recv, m, v, *([own] if has_own else []))


P_LAT, P_CTX, P_FNW, P_FFNB, P_CONV, P_FFNW, P_MISC, P_ROWS = 0, 8, 16, 24, 32, 48, 72, 80


def _rows_of(v, nrows):
    flat = v.reshape(-1)
    return jnp.pad(flat, (0, nrows * D - flat.shape[0])).reshape(nrows, D)


def _by_columns(g):
    n, r, c = g.shape
    return jnp.transpose(g, (1, 0, 2)).reshape(r, n * c)


def kernel(x, c, ctx, c_ctx, w_mod, b_mod, w_in, q_norm_w, k_norm_w, conv_qkv_w, a_log, dt_bias, gdn_norm_w, w_pa, w_pd, w_out, w_up, ffn_conv_w, ffn_conv_b, w_down, final_norm_w, loss_target, m_c_ctx, m_w_mod, m_b_mod, m_w_in, m_q_norm_w, m_k_norm_w, m_conv_qkv_w, m_a_log, m_dt_bias, m_gdn_norm_w, m_w_pa, m_w_pd, m_w_out, m_w_up, m_ffn_conv_w, m_ffn_conv_b, m_w_down, m_final_norm_w, v_c_ctx, v_w_mod, v_b_mod, v_w_in, v_q_norm_w, v_k_norm_w, v_conv_qkv_w, v_a_log, v_dt_bias, v_gdn_norm_w, v_w_pa, v_w_pd, v_w_out, v_w_up, v_ffn_conv_w, v_ffn_conv_b, v_w_down, v_final_norm_w):
    _, _, _, me = _position()
    mcols = w_mod.shape[2]

    transposed = ("w_in", "w_up")
    big = {"w_in": w_in[0].T, "w_pa": w_pa[0], "w_pd": w_pd[0], "w_out": w_out[0], "w_up": w_up[0].T, "w_down": w_down[0]}
    names = list(big)
    shards = {n: _cast_bf16(big[n], name="cast_" + n) for n in names}
    w_in_g, c_all, conv_g, ffnw_g = _gather_two_level([shards["w_in"], c, conv_qkv_w[0], ffn_conv_w[0]],
                                                      name="gather_w_in")
    w_in_full = w_in_g.reshape(W_END, D)
    w_in_pad = _pad_columns(w_in_full)

    c9 = jnp.concatenate([c_all.reshape(NDEV, D), jnp.pad(c_ctx[None], ((0, MODROWS - NDEV - 1), (0, 0)))], axis=0)
    b_loc = lax.dynamic_slice(b_mod, (0, me * mcols), (1, mcols))
    mod_all, = _exchange([_mod_fwd(c9, w_mod[0], b_loc)], name="gather_mod", scatter=False)
    mod_lat = lax.dynamic_index_in_dim(mod_all, me, axis=1, keepdims=False).reshape(6, D)
    mod_ctx = mod_all[:, NDEV, :].reshape(6, D)

    small = {"q_norm_w": q_norm_w, "k_norm_w": k_norm_w, "gdn_norm_w": gdn_norm_w, "a_log": a_log, "dt_bias": dt_bias,
             "conv_qkv_w": _by_columns(conv_g), "ffn_conv_w": _by_columns(ffnw_g), "ffn_conv_b": ffn_conv_b,
             "final_norm_w": final_norm_w[None]}
    loss_me, grad_x, early, recv, dmod_lat, dmod_ctx, gs = _local_step(
        x[0], ctx[0], loss_target[0], mod_lat, mod_ctx, w_in_pad, shards, small)
    pending_in, own_in = early["w_in"]

    moments = {"w_in": (m_w_in, v_w_in), "w_pa": (m_w_pa, v_w_pa), "w_pd": (m_w_pd, v_w_pd),
               "w_out": (m_w_out, v_w_out), "w_up": (m_w_up, v_w_up), "w_down": (m_w_down, v_w_down)}
    res = {}
    def finish(n, outs):
        return tuple((t.T if n in transposed else t)[None] for t in outs)

    def moment(t, n):
        return t[0].T if n in transposed else t[0]

    for n in recv:
        res[n] = finish(n, _adamw_recv(big[n], recv[n], moment(moments[n][0], n), moment(moments[n][1], n),
                                       name="adamw_" + n))

    misc = jnp.concatenate([gs["q_norm_w"][0], gs["k_norm_w"][0], gs["gdn_norm_w"][0], gs["a_log"], gs["dt_bias"],
                            loss_me[None]])
    pack = jnp.concatenate([_rows_of(dmod_lat, P_CTX - P_LAT), _rows_of(dmod_ctx, P_FNW - P_CTX),
                            _rows_of(gs["final_norm_w"], P_FFNB - P_FNW), _rows_of(gs["ffn_conv_b"], P_CONV - P_FFNB),
                            _rows_of(gs["conv_qkv_w"], P_FFNW - P_CONV), _rows_of(gs["ffn_conv_w"], P_MISC - P_FFNW),
                            _rows_of(misc, P_ROWS - P_MISC)], axis=0)
    pack_all, = _exchange([pack], name="gather_pack", scatter=False)
    tot = _sum_slots(pack_all, name="sum_pack")
    dall = jnp.concatenate([pack_all[:, P_LAT:P_LAT + 6, :].reshape(NDEV, 6 * D),
                            jnp.pad(tot[P_CTX:P_CTX + 6].reshape(1, 6 * D), ((0, MODROWS - NDEV - 1), (0, 0)))], axis=0)
    dmy = lax.dynamic_slice(dall, (0, me * mcols), (MODROWS, mcols))
    g_w_mod, g_b_mod, cpart = _mod_bwd(c9, dmy, dall, w_mod[0])
    cparts, = _exchange([cpart], name="gather_cctx", scatter=False)
    sems_a, land = pending_in[:2], pending_in[3]
    *sems_b, g_in_thru, land, token_b = _scatter_start(pending_in[2], land, (D // 2, D // 2), (cparts,),
                                                       name="scatter_g_in_b_start")
    g_c_ctx = _cctx_finish(cparts, c_ctx[None], (token_b,))[0]

    nconv, nffn = 3 * GH * HD, 2 * DFF
    conv_tot = tot[P_CONV:P_FFNW].reshape(-1)[:3 * nconv].reshape(3, nconv)
    ffnw_tot = tot[P_FFNW:P_MISC].reshape(-1)[:3 * nffn].reshape(3, nffn)
    mrow = tot[P_MISC]
    grads = {
        "c_ctx": g_c_ctx, "w_mod": g_w_mod[None], "b_mod": g_b_mod,
        "q_norm_w": mrow[None, 0:HD], "k_norm_w": mrow[None, HD:2 * HD], "gdn_norm_w": mrow[None, 2 * HD:3 * HD],
        "conv_qkv_w": lax.dynamic_slice(conv_tot, (0, me * (nconv // NDEV)), (3, nconv // NDEV))[None],
        "a_log": mrow[3 * HD:3 * HD + 2 * GH].reshape(1, 2, GH),
        "dt_bias": mrow[3 * HD + 2 * GH:3 * HD + 4 * GH].reshape(1, 2, GH),
        "ffn_conv_w": lax.dynamic_slice(ffnw_tot, (0, me * (nffn // NDEV)), (3, nffn // NDEV))[None],
        "ffn_conv_b": tot[P_FFNB:P_CONV].reshape(-1)[:nffn][None],
        "final_norm_w": tot[P_FNW],
    }
    loss = mrow[3 * HD + 4 * GH]
    given = {"c_ctx": (c_ctx, m_c_ctx, v_c_ctx), "w_mod": (w_mod, m_w_mod, v_w_mod), "b_mod": (b_mod, m_b_mod, v_b_mod),
             "q_norm_w": (q_norm_w, m_q_norm_w, v_q_norm_w), "k_norm_w": (k_norm_w, m_k_norm_w, v_k_norm_w),
             "conv_qkv_w": (conv_qkv_w, m_conv_qkv_w, v_conv_qkv_w), "a_log": (a_log, m_a_log, v_a_log),
             "dt_bias": (dt_bias, m_dt_bias, v_dt_bias), "gdn_norm_w": (gdn_norm_w, m_gdn_norm_w, v_gdn_norm_w),
             "ffn_conv_w": (ffn_conv_w, m_ffn_conv_w, v_ffn_conv_w), "ffn_conv_b": (ffn_conv_b, m_ffn_conv_b, v_ffn_conv_b),
             "final_norm_w": (final_norm_w, m_final_norm_w, v_final_norm_w)}
    res["w_mod"] = (grads["w_mod"],) + _adamw(w_mod, grads["w_mod"], m_w_mod, v_w_mod, name="adamw_w_mod")
    small_names = [n for n in given if n != "w_mod"]
    updates = _adamw_many([(given[n][0], grads[n], given[n][1], given[n][2]) for n in small_names], name="adamw_small")
    for n, upd in zip(small_names, updates):
        res[n] = (grads[n],) + upd

    for n in ("w_down", "w_up"):
        pending, own = early[n]
        got = _receive(pending, [g_c_ctx], name="scatter_g_" + n[2:])
        res[n] = finish(n, _adamw_recv(big[n], got, moment(moments[n][0], n), moment(moments[n][1], n),
                                       name="adamw_" + n, own=own))

    g_in_thru, land = _scatter_wait(*sems_a, g_in_thru, land, (0, D // 2), [res[n][1] for n in res],
                                    name="scatter_g_in_a_wait")
    _, land = _scatter_wait(*sems_b, g_in_thru, land, (D // 2, D // 2), (), name="scatter_g_in_b_wait")
    res["w_in"] = finish("w_in", _adamw_recv(big["w_in"], land, moment(m_w_in, "w_in"), moment(v_w_in, "w_in"),
                                             name="adamw_w_in", own=own_in))

    order = ["c_ctx", "w_mod", "b_mod", "w_in", "q_norm_w", "k_norm_w", "conv_qkv_w", "a_log", "dt_bias", "gdn_norm_w",
             "w_pa", "w_pd", "w_out", "w_up", "ffn_conv_w", "ffn_conv_b", "w_down", "final_norm_w"]
    return (loss, grad_x[None], *[res[n][0] for n in order], *[res[n][1] for n in order],
            *[res[n][2] for n in order], *[res[n][3] for n in order])
```

```python
import functools
import math

import jax
import jax.numpy as jnp
from jax import lax
from jax.experimental import pallas as pl
from jax.experimental.pallas import tpu as pltpu

F32 = jnp.float32
BF16 = jnp.bfloat16
HI = lax.Precision.HIGHEST
MESH = pl.DeviceIdType.MESH

NDEV = 8
D = 1024
HD = 128
AH, AKV, GRP = 8, 2, 4
GH = 8
CH = 64
DFF = 2816
GRID_W = 64
EPS = 1e-6
ROPE_THETA = 10000.0
LOG2E = math.log2(math.e)
C_KV, C_AQ, C_QKV, C_BL, C_Z, C_GATE, C_END = 0, 512, 1536, 4608, 5120, 6144, 8192
W_QKV, W_AQ, W_Z, W_END = 512, 3616, 4640, 7712


def _pad_columns(w):
    zeros = jnp.zeros((C_Z - C_QKV - (W_AQ - W_QKV), D), w.dtype)
    return jnp.concatenate([w[:W_QKV], w[W_AQ:W_Z], w[W_QKV:W_AQ], zeros, w[W_Z:]], axis=0)


def _unpad_columns(g):
    return jnp.concatenate([g[:C_AQ], g[C_QKV:C_QKV + W_AQ - W_QKV], g[C_AQ:C_QKV], g[C_Z:]], axis=0)
LR, B1, B2, AEPS, WD, STEP = 0.001, 0.9, 0.999, 1e-08, 0.01, 10
VMEM_BIG = 56 * 1024 * 1024
INTRA_FWD_CHUNKS = 36
INTRA_BWD_CHUNKS = 36


def _call(body, *, name, out_shape, grid=None, in_specs=None, out_specs=None, scratch=(), sem=None,
          vmem=None, aliases=None):
    params = {}
    if sem is not None:
        params["dimension_semantics"] = sem
    if vmem is not None:
        params["vmem_limit_bytes"] = vmem
    kw = {}
    if grid is not None:
        kw["grid"] = grid
    if in_specs is not None:
        kw["in_specs"] = in_specs
    if out_specs is not None:
        kw["out_specs"] = out_specs
    if aliases:
        kw["input_output_aliases"] = aliases
    return pl.pallas_call(body, name=name, out_shape=out_shape, scratch_shapes=list(scratch),
                          compiler_params=pltpu.CompilerParams(**params), **kw)


def _call_carrying(body, exch, *, name, out_shape, grid, in_specs, out_specs, scratch=(), vmem=None):
    n, nin, nout, nscr = exch.n, len(in_specs), len(out_shape), len(scratch)
    steps = math.prod(grid)
    mid = (2 * steps) // 3

    def wrapped(*refs):
        ins, cins = refs[:nin], refs[nin:nin + n]
        outs, couts = refs[nin + n:nin + n + nout], refs[nin + n + nout:nin + 2 * n + nout]
        scr, sems = refs[nin + 2 * n + nout:nin + 2 * n + nout + nscr], refs[nin + 2 * n + nout + nscr:]
        ids = [pl.program_id(i) for i in range(len(grid))]
        first = functools.reduce(jnp.logical_and, [i == 0 for i in ids])
        last = functools.reduce(jnp.logical_and, [i == g - 1 for i, g in zip(ids, grid)])

        @pl.when(first)
        def _():
            exch.start(cins, couts, sems)

        if hasattr(exch, "middle"):
            linear = functools.reduce(lambda acc, ig: acc * ig[1] + ig[0], zip(ids, grid), 0)

            @pl.when(linear == mid)
            def _():
                exch.middle(cins, couts, sems)

        body(*ins, *outs, *scr)

        @pl.when(last)
        def _():
            exch.finish(cins, couts, sems)

    params = {"dimension_semantics": ("arbitrary",) * len(grid)}
    if vmem is not None:
        params["vmem_limit_bytes"] = vmem
    fn = pl.pallas_call(wrapped, name=name, out_shape=tuple(out_shape) + exch.out_shape, grid=grid,
                        in_specs=list(in_specs) + [HBM] * n, out_specs=tuple(out_specs) + (HBM,) * n,
                        scratch_shapes=list(scratch) + exch.scratch, compiler_params=pltpu.CompilerParams(**params))

    def run(*args):
        res = fn(*args, *exch.arrs)
        return res[:nout], list(res[nout:])

    return run


def _sds(shape, dtype=F32):
    return jax.ShapeDtypeStruct(tuple(shape), dtype)


def _dot(a, b, ca, cb):
    return lax.dot_general(a.astype(BF16), b.astype(BF16), (((ca,), (cb,)), ((), ())),
                           preferred_element_type=F32)


@jax.custom_vjp
def _nn(a, b):
    return _dot(a, b, 1, 0)


@jax.custom_vjp
def _nt(a, b):
    return _dot(a, b, 1, 1)


@jax.custom_vjp
def _tn(a, b):
    return _dot(a, b, 0, 0)


_nn.defvjp(lambda a, b: (_nn(a, b), (a, b)), lambda r, g: (_nt(g, r[1]), _tn(r[0], g)))
_nt.defvjp(lambda a, b: (_nt(a, b), (a, b)), lambda r, g: (_nn(g, r[1]), _tn(g, r[0])))
_tn.defvjp(lambda a, b: (_tn(a, b), (a, b)), lambda r, g: (_nt(r[1], g), _nn(r[0], g)))


def _mdot(a, b):
    return jnp.dot(a, b, precision=lax.Precision.HIGH, preferred_element_type=F32)


def _maskdot(mask, a, cm):
    hi = a.astype(BF16)
    r = a - hi.astype(F32)
    mid = r.astype(BF16)
    lo = (r - mid.astype(F32)).astype(BF16)
    mb = mask.astype(BF16)
    dims = (((cm,), (0,)), ((), ()))
    return (lax.dot_general(mb, hi, dims, preferred_element_type=F32)
            + lax.dot_general(mb, mid, dims, preferred_element_type=F32)
            + lax.dot_general(mb, lo, dims, preferred_element_type=F32))


@jax.custom_vjp
def _mask_nn(mask, a):
    return _maskdot(mask, a, 1)


_mask_nn.defvjp(lambda mask, a: (_maskdot(mask, a, 1), mask),
                lambda mask, g: (jnp.zeros_like(mask), _maskdot(mask, g, 0)))


@jax.custom_vjp
def _saved_inverse(lmat, x):
    return x


def _saved_inverse_bwd(x, g):
    t = lax.dot_general(x, g, (((0,), (0,)), ((), ())), precision=lax.Precision.HIGH, preferred_element_type=F32)
    dl = lax.dot_general(t, x, (((1,), (1,)), ((), ())), precision=lax.Precision.HIGH, preferred_element_type=F32)
    return -dl, jnp.zeros_like(x)


_saved_inverse.defvjp(lambda lmat, x: (x, x), _saved_inverse_bwd)


def _row_ids(shape):
    return lax.broadcasted_iota(jnp.int32, shape, 0)


def _shift_rows(x, down, bounds):
    n = x.shape[0]
    rows = _row_ids(x.shape)
    y = pltpu.roll(x, 1 if down else n - 1, 0)
    edge = functools.reduce(jnp.logical_or, [rows == (s if down else e - 1) for s, e in bounds])
    return jnp.where(edge, 0.0, y)


def _make_shift(bounds):
    @jax.custom_vjp
    def down(x):
        return _shift_rows(x, True, bounds)

    @jax.custom_vjp
    def up(x):
        return _shift_rows(x, False, bounds)

    down.defvjp(lambda x: (down(x), None), lambda _, g: (up(g),))
    up.defvjp(lambda x: (up(x), None), lambda _, g: (down(g),))
    return down, up


@jax.custom_vjp
def _swap32(x):
    lane = lax.broadcasted_iota(jnp.int32, x.shape, x.ndim - 1)
    return jnp.where((lane % 64) < 32, pltpu.roll(x, HD - 32, x.ndim - 1), pltpu.roll(x, 32, x.ndim - 1))


_swap32.defvjp(lambda x: (_swap32(x), None), lambda _, g: (_swap32(g),))


def _rms(x):
    return x * lax.rsqrt(jnp.mean(x * x, axis=-1, keepdims=True) + EPS)


def _silu(x):
    return x * jax.nn.sigmoid(x)


def _mm(a, b, *, name, M, N, K, ta=False, tb=False, out_dtype=F32, bm=None, bn=None, bk=None, after=()):
    bm, bn, bk = bm or M, bn or N, bk or K
    assert M % bm == 0 and N % bn == 0 and K % bk == 0, (name, M, N, K, bm, bn, bk)
    nk = K // bk
    ca, cb = (0 if ta else 1), (1 if tb else 0)
    na = len(after)

    def body(a_ref, b_ref, *rest):
        o_ref, acc = rest[na], rest[na + 1:]
        r = _dot(a_ref[...], b_ref[...], ca, cb)
        if nk == 1:
            o_ref[...] = r.astype(out_dtype)
        else:
            acc_ref, = acc
            k = pl.program_id(2)

            @pl.when(k == 0)
            def _():
                acc_ref[...] = r

            @pl.when(k > 0)
            def _():
                acc_ref[...] += r

            @pl.when(k == nk - 1)
            def _():
                o_ref[...] = acc_ref[...].astype(out_dtype)

    a_spec = pl.BlockSpec((bk, bm), lambda i, j, k: (k, i)) if ta else pl.BlockSpec((bm, bk), lambda i, j, k: (i, k))
    b_spec = pl.BlockSpec((bn, bk), lambda i, j, k: (j, k)) if tb else pl.BlockSpec((bk, bn), lambda i, j, k: (k, j))
    return _call(body, name=name, out_shape=_sds((M, N), out_dtype), grid=(M // bm, N // bn, nk),
                 in_specs=[a_spec, b_spec] + [pl.BlockSpec(memory_space=pl.ANY)] * na,
                 out_specs=pl.BlockSpec((bm, bn), lambda i, j, k: (i, j)),
                 scratch=[pltpu.VMEM((bm, bn), F32)] if nk > 1 else [],
                 sem=("parallel", "parallel", "arbitrary"), vmem=VMEM_BIG)(a, b, *after)


def _normmod_fn(x, sh, sc):
    return _rms(x) * (1.0 + sc) + sh


def _normmod_fwd(x, mod, i_sh, i_sc, *, name, br=256):
    R = x.shape[0]

    def body(x_ref, mod_ref, o_ref):
        o_ref[...] = _normmod_fn(x_ref[...], mod_ref[i_sh:i_sh + 1, :], mod_ref[i_sc:i_sc + 1, :]).astype(BF16)

    return _call(body, name=name, out_shape=_sds((R, D), BF16), grid=(R // br,),
                 in_specs=[pl.BlockSpec((br, D), lambda i: (i, 0)), pl.BlockSpec((6, D), lambda i: (0, 0))],
                 out_specs=pl.BlockSpec((br, D), lambda i: (i, 0)), sem=("parallel",))(x, mod)


def _normmod_bwd(x, mod, i_sh, i_sc, dh, dh_off, res, *, name, br=256):
    R = x.shape[0]
    ob = dh_off // br
    has_res = res is not None

    def body(x_ref, mod_ref, dh_ref, *rest):
        if has_res:
            res_ref, dx_ref, dsh_ref, dsc_ref = rest
        else:
            dx_ref, dsh_ref, dsc_ref = rest
        sh, sc = mod_ref[i_sh:i_sh + 1, :], mod_ref[i_sc:i_sc + 1, :]
        _, vjp = jax.vjp(_normmod_fn, x_ref[...], sh, sc)
        dx, dsh, dsc = vjp(dh_ref[...])
        dx_ref[...] = dx + res_ref[...] if has_res else dx

        @pl.when(pl.program_id(0) == 0)
        def _():
            dsh_ref[...] = jnp.zeros_like(dsh_ref)
            dsc_ref[...] = jnp.zeros_like(dsc_ref)

        dsh_ref[...] += dsh
        dsc_ref[...] += dsc

    row = pl.BlockSpec((br, D), lambda i: (i, 0))
    vec = pl.BlockSpec((1, D), lambda i: (0, 0))
    ins = [row, pl.BlockSpec((6, D), lambda i: (0, 0)), pl.BlockSpec((br, D), lambda i: (i + ob, 0))]
    args = [x, mod, dh]
    if has_res:
        ins.append(row)
        args.append(res)
    return _call(body, name=name, out_shape=(_sds((R, D)), _sds((1, D)), _sds((1, D))), grid=(R // br,),
                 in_specs=ins, out_specs=(row, vec, vec), sem=("arbitrary",))(*args)


def _rope(x, cos, sin):
    return x * cos + _swap32(x) * sin


def _aprep_fn(qs, ks, cos, sin, qw, kw):
    return ([_rope(_rms(q) * qw, cos, sin) for q in qs], [_rope(_rms(k) * kw, cos, sin) for k in ks])


def _aprep_fwd(proj, cos, sin, qw, kw, *, br=256):
    T = proj.shape[0]

    def body(x_ref, cos_ref, sin_ref, qw_ref, kw_ref, q_ref, k_ref, v_ref):
        qs = [x_ref[:, C_AQ + h * HD:C_AQ + (h + 1) * HD] for h in range(AH)]
        ks = [x_ref[:, h * HD:(h + 1) * HD] for h in range(AKV)]
        qo, ko = _aprep_fn(qs, ks, cos_ref[...], sin_ref[...], qw_ref[...], kw_ref[...])
        for h in range(AH):
            q_ref[h] = qo[h].astype(BF16)
        for h in range(AKV):
            k_ref[h] = ko[h].astype(BF16)
            v_ref[h] = x_ref[:, (AKV + h) * HD:(AKV + h + 1) * HD].astype(BF16)

    tab = pl.BlockSpec((br, HD), lambda i: (i, 0))
    vec = pl.BlockSpec((1, HD), lambda i: (0, 0))
    return _call(body, name="aprep_fwd",
                 out_shape=(_sds((AH, T, HD), BF16), _sds((AKV, T, HD), BF16), _sds((AKV, T, HD), BF16)),
                 grid=(T // br,),
                 in_specs=[pl.BlockSpec((br, C_QKV), lambda i: (i, 0)), tab, tab, vec, vec],
                 out_specs=(pl.BlockSpec((AH, br, HD), lambda i: (0, i, 0)),
                            pl.BlockSpec((AKV, br, HD), lambda i: (0, i, 0)),
                            pl.BlockSpec((AKV, br, HD), lambda i: (0, i, 0))),
                 sem=("parallel",))(proj, cos, sin, qw, kw)


def _aprep_bwd(proj, cos, sin, qw, kw, dq, dk, dv, dproj, L, *, br=256):
    T = proj.shape[0]
    lb = L // br

    def body(x_ref, cos_ref, sin_ref, qw_ref, kw_ref, dq_ref, dk_ref, dv_ref, _, dx_ref, dqw_ref, dkw_ref):
        i = pl.program_id(0)
        qs = [x_ref[:, C_AQ + h * HD:C_AQ + (h + 1) * HD] for h in range(AH)]
        ks = [x_ref[:, h * HD:(h + 1) * HD] for h in range(AKV)]
        _, vjp = jax.vjp(_aprep_fn, qs, ks, cos_ref[...], sin_ref[...], qw_ref[...], kw_ref[...])
        is_lat = i >= lb
        dqs = [jnp.where(is_lat, dq_ref[h], 0.0) for h in range(AH)]
        dks = [dk_ref[h] for h in range(AKV)]
        gq, gk, _, _, gqw, gkw = vjp((dqs, dks))
        for h in range(AH):
            dx_ref[:, C_AQ + h * HD:C_AQ + (h + 1) * HD] = gq[h].astype(BF16)
        for h in range(AKV):
            dx_ref[:, h * HD:(h + 1) * HD] = gk[h].astype(BF16)
            dx_ref[:, (AKV + h) * HD:(AKV + h + 1) * HD] = dv_ref[h].astype(BF16)

        @pl.when(i == 0)
        def _():
            dqw_ref[...] = jnp.zeros_like(dqw_ref)
            dkw_ref[...] = jnp.zeros_like(dkw_ref)

        dqw_ref[...] += gqw
        dkw_ref[...] += gkw

    tab = pl.BlockSpec((br, HD), lambda i: (i, 0))
    vec = pl.BlockSpec((1, HD), lambda i: (0, 0))
    kvb = pl.BlockSpec((AKV, br, HD), lambda i: (0, i, 0))
    blk = pl.BlockSpec((br, C_QKV), lambda i: (i, 0))
    return _call(body, name="aprep_bwd", out_shape=(_sds(dproj.shape, BF16), _sds((1, HD)), _sds((1, HD))),
                 grid=(T // br,),
                 in_specs=[blk, tab, tab, vec, vec,
                           pl.BlockSpec((AH, br, HD), lambda i: (0, jnp.maximum(i - lb, 0), 0)), kvb, kvb, ANYSPEC],
                 out_specs=(blk, vec, vec), aliases={8: 0},
                 sem=("arbitrary",))(proj, cos, sin, qw, kw, dq, dk, dv, dproj)


def _attn_grad(q, k, v, o, lse2, do):
    scale = HD ** -0.5
    p = jnp.exp2(_dot(q, k, 1, 1) * (scale * LOG2E) - lse2)
    dp = _dot(do, v, 1, 1)
    ds = p * (dp - jnp.sum(do * o, axis=-1, keepdims=True)) * scale
    return _dot(ds, k, 1, 0), _dot(ds, q, 0, 0), _dot(p, do, 0, 0)


ATTN_KEYS = 256


def _attn_fwd(q, k, v, L, exch, *, bq=128):
    T = q.shape[1]
    N = T - L
    lb = L // bq
    assert T % ATTN_KEYS == 0
    scale = HD ** -0.5
    heads = range(GRP)

    def body(q_ref, k_ref, v_ref, o_ref, o32_ref, lse_ref):
        qs = [q_ref[g] for g in heads]
        m = [jnp.full((bq, 1), -jnp.inf, F32) for _ in heads]
        l = [jnp.zeros((bq, 1), F32) for _ in heads]
        acc = [jnp.zeros((bq, HD), F32) for _ in heads]
        for c in range(T // ATTN_KEYS):
            kc, vc = k_ref[c * ATTN_KEYS:(c + 1) * ATTN_KEYS, :], v_ref[c * ATTN_KEYS:(c + 1) * ATTN_KEYS, :]
            s = [_dot(qs[g], kc, 1, 1) * (scale * LOG2E) for g in heads]
            m_new = [jnp.maximum(m[g], jnp.max(s[g], axis=-1, keepdims=True)) for g in heads]
            alpha = [jnp.exp2(m[g] - m_new[g]) for g in heads]
            p = [jnp.exp2(s[g] - m_new[g]) for g in heads]
            l = [l[g] * alpha[g] + jnp.sum(p[g], axis=-1, keepdims=True) for g in heads]
            acc = [acc[g] * alpha[g] + _dot(p[g], vc, 1, 0) for g in heads]
            m = m_new
        for g in heads:
            o = acc[g] / l[g]
            o_ref[:, g * HD:(g + 1) * HD] = o.astype(BF16)
            o32_ref[:, g * HD:(g + 1) * HD] = o
            lse_ref[g] = jnp.broadcast_to(m[g] + jnp.log2(l[g]), (bq, HD))

    kvb = pl.BlockSpec((None, T, HD), lambda g, i: (g, 0, 0))
    ob = pl.BlockSpec((bq, GRP * HD), lambda g, i: (i, g))
    return _call_carrying(
        body, exch, name="attn_fwd",
        out_shape=(_sds((N, AH * HD), BF16), _sds((N, AH * HD)), _sds((AH, N, HD))), grid=(AKV, N // bq),
        in_specs=[pl.BlockSpec((GRP, bq, HD), lambda g, i: (g, i + lb, 0)), kvb, kvb],
        out_specs=(ob, ob, pl.BlockSpec((GRP, bq, HD), lambda g, i: (g, i, 0))), vmem=VMEM_BIG)(q, k, v)


def _attn_bwd(q, k, v, o32, lse, do, L, exch, *, bq=128):
    T = q.shape[1]
    N = T - L
    lb = L // bq

    def body(q_ref, k_ref, v_ref, o_ref, lse_ref, do_ref, dq_ref, dk_ref, dv_ref):
        rows = lambda r: jnp.concatenate([r[:, g * HD:(g + 1) * HD] for g in range(GRP)], axis=0)
        lse = jnp.max(lse_ref[...].reshape(GRP * bq, HD), axis=-1, keepdims=True)
        dq, dk, dv = _attn_grad(q_ref[...].reshape(GRP * bq, HD), k_ref[...], v_ref[...], rows(o_ref), lse, rows(do_ref))
        dq_ref[...] = dq.reshape(GRP, bq, HD)

        @pl.when(pl.program_id(1) == 0)
        def _():
            dk_ref[...] = jnp.zeros_like(dk_ref)
            dv_ref[...] = jnp.zeros_like(dv_ref)

        dk_ref[...] += dk
        dv_ref[...] += dv

    kvb = pl.BlockSpec((None, T, HD), lambda g, i: (g, 0, 0))
    qb = pl.BlockSpec((GRP, bq, HD), lambda g, i: (g, i + lb, 0))
    hb = pl.BlockSpec((GRP, bq, HD), lambda g, i: (g, i, 0))
    ob = pl.BlockSpec((bq, GRP * HD), lambda g, i: (i, g))
    return _call_carrying(body, exch, name="attn_bwd",
                          out_shape=(_sds((AH, N, HD)), _sds((AKV, T, HD)), _sds((AKV, T, HD))), grid=(AKV, N // bq),
                          in_specs=[qb, kvb, kvb, ob, hb, ob], out_specs=(hb, kvb, kvb),
                          vmem=VMEM_BIG)(q, k, v, o32, lse, do)


def _gprep_fn(kind, shifts, x, w):
    down, up = shifts
    y = down(x) * w[0:1, :] + x * w[1:2, :] + up(x) * w[2:3, :]
    a = _silu(y)
    if kind == 2:
        return a
    a = a * lax.rsqrt(jnp.sum(a * a, axis=-1, keepdims=True) + EPS)
    return a * (HD ** -0.5) if kind == 0 else a


def _gprep_fwd(proj, conv_w, kind, bounds):
    T = proj.shape[0]
    shifts = _make_shift(bounds)
    cb = C_QKV // HD + kind * GH

    def body(x_ref, w_ref, o_ref):
        o_ref[...] = _gprep_fn(kind, shifts, x_ref[...], w_ref[...])

    return _call(body, name=f"gprep_fwd{kind}", out_shape=_sds((GH, T, HD)), grid=(GH,),
                 in_specs=[pl.BlockSpec((T, HD), lambda h: (0, cb + h)),
                           pl.BlockSpec((3, HD), lambda h: (0, kind * GH + h))],
                 out_specs=pl.BlockSpec((None, T, HD), lambda h: (h, 0, 0)), sem=("parallel",))(proj, conv_w)


def _gprep_bwd(proj, conv_w, kind, bounds, dy, dproj):
    T = proj.shape[0]
    shifts = _make_shift(bounds)
    cb = C_QKV // HD + kind * GH

    def body(x_ref, w_ref, dy_ref, _, dx_ref, dw_ref):
        _, vjp = jax.vjp(functools.partial(_gprep_fn, kind, shifts), x_ref[...], w_ref[...])
        dx, dw = vjp(dy_ref[0] + dy_ref[1])
        dx_ref[...] = dx.astype(BF16)
        dw_ref[...] = dw

    return _call(body, name=f"gprep_bwd{kind}", out_shape=(_sds(dproj.shape, BF16), _sds((3, GH * HD))), grid=(GH,),
                 in_specs=[pl.BlockSpec((T, HD), lambda h: (0, cb + h)),
                           pl.BlockSpec((3, HD), lambda h: (0, kind * GH + h)),
                           pl.BlockSpec((2, None, T, HD), lambda h: (0, h, 0, 0)), ANYSPEC],
                 out_specs=(pl.BlockSpec((T, HD), lambda h: (0, cb + h)), pl.BlockSpec((3, HD), lambda h: (0, h))),
                 aliases={3: 0}, sem=("parallel",))(proj, conv_w, dy, dproj)


def _bl_fn(x, alog, dtb):
    lane = lax.broadcasted_iota(jnp.int32, x.shape, 1)
    beta = jax.nn.sigmoid(x)
    z = x + dtb
    sp = jnp.maximum(z, 0.0) + jnp.log1p(jnp.exp(-jnp.abs(z)))
    la = -jnp.exp(alog) * sp
    return jnp.where(lane < 2 * GH, beta, jnp.where(lane < 4 * GH, la, 0.0))


def _bl_fwd(proj, alog, dtb, *, br=256):
    T = proj.shape[0]

    def body(x_ref, a_ref, d_ref, o_ref):
        o_ref[...] = _bl_fn(x_ref[...], a_ref[...], d_ref[...])

    vec = pl.BlockSpec((1, HD), lambda i: (0, 0))
    return _call(body, name="bl_fwd", out_shape=_sds((T, HD)), grid=(T // br,),
                 in_specs=[pl.BlockSpec((br, HD), lambda i: (i, C_BL // HD)), vec, vec],
                 out_specs=pl.BlockSpec((br, HD), lambda i: (i, 0)), sem=("parallel",))(proj, alog, dtb)


def _bl_bwd(proj, alog, dtb, dbl, dproj, *, br=256):
    T = proj.shape[0]
    wide = C_Z - C_BL

    def body(x_ref, a_ref, d_ref, g_ref, _, dx_ref, da_ref, dd_ref):
        g = g_ref[0, 0]
        for d in range(2):
            for h in range(GH):
                if d or h:
                    g = g + g_ref[d, h]
        _, vjp = jax.vjp(_bl_fn, x_ref[...], a_ref[...], d_ref[...])
        dx, da, dd = vjp(g)
        dx_ref[:, :HD] = dx.astype(BF16)
        dx_ref[:, HD:] = jnp.zeros((br, wide - HD), BF16)

        @pl.when(pl.program_id(0) == 0)
        def _():
            da_ref[...] = jnp.zeros_like(da_ref)
            dd_ref[...] = jnp.zeros_like(dd_ref)

        da_ref[...] += da
        dd_ref[...] += dd

    vec = pl.BlockSpec((1, HD), lambda i: (0, 0))
    return _call(body, name="bl_bwd", out_shape=(_sds(dproj.shape, BF16), _sds((1, HD)), _sds((1, HD))), grid=(T // br,),
                 in_specs=[pl.BlockSpec((br, HD), lambda i: (i, C_BL // HD)), vec, vec,
                           pl.BlockSpec((2, GH, br, HD), lambda i: (0, 0, i, 0)), ANYSPEC],
                 out_specs=(pl.BlockSpec((br, wide), lambda i: (i, C_BL // wide)), vec, vec), aliases={4: 0},
                 sem=("arbitrary",))(proj, alog, dtb, dbl, dproj)


def _chunk_masks(d):
    ii = lax.broadcasted_iota(jnp.int32, (CH, CH), 0)
    jj = lax.broadcasted_iota(jnp.int32, (CH, CH), 1)
    eye = (ii == jj).astype(F32)
    before = jnp.where(d == 0, (jj < ii).astype(F32), (jj > ii).astype(F32))
    return before, before + eye, eye


def _same_block(b):
    ii = lax.broadcasted_iota(jnp.int32, (CH, CH), 0)
    jj = lax.broadcasted_iota(jnp.int32, (CH, CH), 1)
    shift = b.bit_length() - 1
    return (jnp.right_shift(ii, shift) == jnp.right_shift(jj, shift)).astype(F32)


def _intra_fn(masks, sel_b, sel_l, qs, ks, vs, bls, xs=None):
    before, ateq, eye = masks
    inc = ateq > 0.0
    each = lambda f, *ls: [f(*t) for t in zip(*ls)]
    beta = each(lambda bl: jnp.sum(bl * sel_b, axis=-1, keepdims=True), bls)
    la = each(lambda bl: jnp.sum(bl * sel_l, axis=-1, keepdims=True), bls)
    gam = each(lambda a: _mask_nn(ateq, jnp.broadcast_to(a, (CH, HD))), la)
    gi = each(lambda g: g[:, :CH], gam)
    gj = each(lambda g: jnp.transpose(g)[:CH, :], gam)
    kq = each(lambda k, q: _nt(jnp.concatenate([k, q], axis=0), k), ks, qs)
    kk = each(lambda t: t[:CH], kq)
    qk = each(lambda t: t[CH:], kq)
    dec = each(lambda a, b: jnp.where(inc, jnp.exp(jnp.where(inc, a - b, 0.0)), 0.0), gi, gj)
    lmat = each(lambda b, d, m: before * (b * d * m), beta, dec, kk)
    if xs is None:
        same = lambda b: _same_block(b)
        l8 = each(lambda m: m * same(8), lmat)
        x = each(lambda m: eye - m, l8)
        p2 = each(lambda m: _mdot(m, m), l8)
        y = each(lambda a, b: _mdot(jnp.concatenate([a, b], axis=0), b), x, p2)
        x = each(lambda a, t: a + t[:CH], x, y)
        x = each(lambda a, t: a + _mdot(a, t[CH:]), x, y)
        for b in (8, 16, 32):
            below = same(2 * b) - same(b)
            x = each(lambda a, m: a - _mdot(a, _mdot(m * below, a)), x, lmat)
    else:
        x = each(_saved_inverse, lmat, xs)
    eg = each(jnp.exp, gam)
    uw = each(lambda a, b, v, e, k: _mdot(a, jnp.concatenate([b * v, (b * e) * k], axis=1)), x, beta, vs, eg, ks)
    u = each(lambda t: t[:, :HD], uw)
    w = each(lambda t: t[:, HD:], uw)
    tot = each(lambda a: jnp.sum(a, axis=0, keepdims=True), la)
    kd = each(lambda k, t, g: k * jnp.exp(t - g), ks, tot, gam)
    gl = each(lambda t: jnp.broadcast_to(jnp.exp(t), (1, HD)), tot)
    qd = each(lambda q, e: q * e, qs, eg)
    p = each(lambda d, m: d * m, dec, qk)
    return (u, w, kd, qd, p, gl, x) if xs is None else (u, w, kd, qd, p, gl)


def _dir_head_sel(d, h):
    lane = lax.broadcasted_iota(jnp.int32, (1, HD), 1)
    return (lane == d * GH + h).astype(F32), (lane == 2 * GH + d * GH + h).astype(F32)


def _intra_specs(T, G):
    nc = T // CH
    assert nc % G == 0
    qkv = pl.BlockSpec((None, G * CH, HD), lambda d, h, c: (h, c, 0))
    bl = pl.BlockSpec((G * CH, HD), lambda d, h, c: (c, 0))
    big = pl.BlockSpec((None, None, G * CH, HD), lambda d, h, c: (d, h, c, 0))
    pm = pl.BlockSpec((None, None, G * CH, CH), lambda d, h, c: (d, h, c, 0))
    gl = pl.BlockSpec((None, None, G, 1, HD), lambda d, h, c: (d, h, c, 0, 0))
    shapes = (_sds((2, GH, T, HD)),) + (_sds((2, GH, T, HD), BF16),) * 3 + (
        _sds((2, GH, T, CH), BF16), _sds((2, GH, nc, 1, HD)), _sds((2, GH, T, CH)))
    return nc, qkv, bl, big, pm, gl, shapes


def _chunks_per_step(T, most):
    nc = T // CH
    return max(g for g in range(1, most + 1) if nc % g == 0)


def _chunk_at(g, d, nc, ncc):
    pos = _visit_pos(g, d, nc, ncc)
    return pos, pl.ds(pl.multiple_of(pos * CH, CH), CH)


def _intra_fwd(q, k, v, bl, L, exch):
    T = q.shape[1]
    G = _chunks_per_step(T, INTRA_FWD_CHUNKS)
    nc, qkv_s, bl_s, big, pm, gl_s, shapes = _intra_specs(T, G)
    assert G == nc
    ncc = L // CH

    def body(q_ref, k_ref, v_ref, bl_ref, u_ref, w_ref, kd_ref, qd_ref, p_ref, gl_ref, x_ref):
        d, h = pl.program_id(0), pl.program_id(1)
        sb, sl = _dir_head_sel(d, h)
        rows = [slice(g * CH, (g + 1) * CH) for g in range(G)]
        outs = _intra_fn(_chunk_masks(d), sb, sl, *[[r[s, :] for s in rows] for r in (q_ref, k_ref, v_ref, bl_ref)])
        for g in range(G):
            pos, at = _chunk_at(g, d, nc, ncc)
            for r, o in zip((u_ref, w_ref, kd_ref, qd_ref, p_ref, x_ref), outs[:5] + outs[6:]):
                r[at, :] = o[g].astype(r.dtype)
            gl_ref[pos] = outs[5][g]

    return _call_carrying(body, exch, name="gdn_intra_fwd", out_shape=shapes, grid=(2, GH, nc // G),
                          in_specs=[qkv_s, qkv_s, qkv_s, bl_s], out_specs=(big, big, big, big, pm, gl_s, pm))(q, k, v, bl)


def _intra_bwd(q, k, v, bl, xinv, cts, L, exch):
    T = q.shape[1]
    G = _chunks_per_step(T, INTRA_BWD_CHUNKS)
    nc, qkv_s, bl_s, big, pm, gl_s, _ = _intra_specs(T, G)
    assert G == nc
    ncc = L // CH

    def body(q_ref, k_ref, v_ref, bl_ref, x_ref, du, dw, dkd, dqd, dp, dgl, dq_ref, dk_ref, dv_ref, dbl_ref):
        d, h = pl.program_id(0), pl.program_id(1)
        sb, sl = _dir_head_sel(d, h)
        rows = [slice(g * CH, (g + 1) * CH) for g in range(G)]
        places = [_chunk_at(g, d, nc, ncc) for g in range(G)]
        fn = functools.partial(_intra_fn, _chunk_masks(d), sb, sl, xs=[x_ref[at, :] for _, at in places])
        _, vjp = jax.vjp(fn, *[[r[s, :] for s in rows] for r in (q_ref, k_ref, v_ref, bl_ref)])
        cts = tuple([r[at, :] for _, at in places] for r in (du, dw, dkd, dqd, dp)) + ([dgl[pos] for pos, _ in places],)
        grads = vjp(cts)
        for g in range(G):
            for r, o in zip((dq_ref, dk_ref, dv_ref, dbl_ref), grads):
                r[rows[g], :] = o[g]

    return _call_carrying(body, exch, name="gdn_intra_bwd", out_shape=(_sds((2, GH, T, HD)),) * 4,
                          grid=(2, GH, nc // G), in_specs=[qkv_s, qkv_s, qkv_s, bl_s, pm, big, big, big, big, pm, gl_s],
                          out_specs=(big,) * 4)(q, k, v, bl, xinv, *cts)


def _scan_fn(s, u, w, kd, qd, p, gl):
    each = lambda f, *ls: [f(*t) for t in zip(*ls)]
    ws = each(_nn, w, s)
    delta = each(lambda a, b: a - b, u, ws)
    kdd = each(_tn, kd, delta)
    s_new = each(lambda g, a, b: g * a + b, gl, s, kdd)
    qs = each(_nn, qd, s)
    pd = each(_nn, p, delta)
    return each(lambda a, b: a + b, qs, pd), s_new


SCAN_BLOCK = 4


def _visit_pos(c, d, nc, ncc):
    back = ncc - 1 - c if c < ncc else ncc + (nc - 1 - c)
    return jnp.where(d == 0, c, back)


def _scan_specs(T, L, back):
    tb = SCAN_BLOCK * CH
    assert T % tb == 0 and L % tb == 0
    nb, ncb = T // tb, L // tb
    at = (lambda t: nb - 1 - t) if back else (lambda t: t)
    big = pl.BlockSpec((2, GH, tb, HD), lambda t: (0, 0, at(t), 0))
    pm = pl.BlockSpec((2, GH, tb, CH), lambda t: (0, 0, at(t), 0))
    gl = pl.BlockSpec((2, GH, SCAN_BLOCK, 1, HD), lambda t: (0, 0, at(t), 0, 0))
    st = pl.BlockSpec((2, GH, SCAN_BLOCK, HD, HD), lambda t: (0, 0, at(t), 0, 0))

    def natural(b):
        return jnp.where(b < ncb, ncb - 1 - b, nb - 1 - (b - ncb))

    do_specs = (pl.BlockSpec((GH, tb, HD), lambda t: (0, at(t), 0)),
                pl.BlockSpec((GH, tb, HD), lambda t: (0, natural(at(t)), 0)))
    return nb, big, pm, gl, st, do_specs


SCAN_STREAMS = [(d, h) for d in (0, 1) for h in range(GH)]


def _scan_fwd(u, w, kd, qd, p, gl, L):
    T = u.shape[2]
    nb, big, pm, gl_s, st, _ = _scan_specs(T, L, False)

    def body(u_ref, w_ref, kd_ref, qd_ref, p_ref, gl_ref, o_ref, st_ref, s_scr):
        @pl.when(pl.program_id(0) == 0)
        def _():
            s_scr[...] = jnp.zeros_like(s_scr)

        s = [s_scr[d, h] for d, h in SCAN_STREAMS]
        for i in range(SCAN_BLOCK):
            rows = slice(i * CH, (i + 1) * CH)
            for (d, h), sv in zip(SCAN_STREAMS, s):
                st_ref[d, h, i] = sv
            o, s = _scan_fn(s, *[[r[d, h, rows, :].astype(F32) for d, h in SCAN_STREAMS]
                                 for r in (u_ref, w_ref, kd_ref, qd_ref, p_ref)],
                            [gl_ref[d, h, i] for d, h in SCAN_STREAMS])
            for (d, h), ov in zip(SCAN_STREAMS, o):
                o_ref[d, h, rows, :] = ov
        for (d, h), sv in zip(SCAN_STREAMS, s):
            s_scr[d, h] = sv

    return _call(body, name="gdn_scan_fwd", out_shape=(_sds((2, GH, T, HD)), _sds((2, GH, T // CH, HD, HD))),
                 grid=(nb,), in_specs=[big, big, big, big, pm, gl_s], out_specs=(big, st),
                 scratch=[pltpu.VMEM((2, GH, HD, HD), F32)], sem=("arbitrary",), vmem=VMEM_BIG)(u, w, kd, qd, p, gl)


def _scan_bwd(u, w, kd, qd, p, gl, states, do, L, exch):
    T = u.shape[2]
    nb, big, pm, gl_s, st, do_specs = _scan_specs(T, L, True)

    def body(u_ref, w_ref, kd_ref, qd_ref, p_ref, gl_ref, st_ref, do0_ref, do1_ref,
             du_ref, dw_ref, dkd_ref, dqd_ref, dp_ref, dgl_ref, ds_scr):
        @pl.when(pl.program_id(0) == 0)
        def _():
            ds_scr[...] = jnp.zeros_like(ds_scr)

        ds = [ds_scr[d, h] for d, h in SCAN_STREAMS]
        for i in reversed(range(SCAN_BLOCK)):
            rows = slice(i * CH, (i + 1) * CH)
            mirror = slice((SCAN_BLOCK - 1 - i) * CH, (SCAN_BLOCK - i) * CH)
            _, vjp = jax.vjp(_scan_fn, [st_ref[d, h, i] for d, h in SCAN_STREAMS],
                             *[[r[d, h, rows, :].astype(F32) for d, h in SCAN_STREAMS]
                               for r in (u_ref, w_ref, kd_ref, qd_ref, p_ref)],
                             [gl_ref[d, h, i] for d, h in SCAN_STREAMS])
            dos = [do0_ref[h, rows, :] if d == 0 else do1_ref[h, mirror, :] for d, h in SCAN_STREAMS]
            ds, gu, gw, gkd, gqd, gp, ggl = vjp((dos, ds))
            for n, (d, h) in enumerate(SCAN_STREAMS):
                du_ref[d, h, rows, :] = gu[n]
                dw_ref[d, h, rows, :] = gw[n]
                dkd_ref[d, h, rows, :] = gkd[n]
                dqd_ref[d, h, rows, :] = gqd[n]
                dp_ref[d, h, rows, :] = gp[n]
                dgl_ref[d, h, i] = ggl[n]
        for (d, h), dv in zip(SCAN_STREAMS, ds):
            ds_scr[d, h] = dv

    return _call_carrying(
        body, exch, name="gdn_scan_bwd",
        out_shape=(_sds((2, GH, T, HD)),) * 4 + (_sds((2, GH, T, CH)), _sds((2, GH, T // CH, 1, HD))),
        grid=(nb,), in_specs=[big, big, big, big, pm, gl_s, st, *do_specs], out_specs=(big, big, big, big, pm, gl_s),
        scratch=[pltpu.VMEM((2, GH, HD, HD), F32)], vmem=VMEM_BIG)(u, w, kd, qd, p, gl, states, do, do)


def _gout_fn(o0, o1, z, gw):
    return _rms(o0 + o1) * gw * _silu(z)


def _backward_latent(o_ref, L):
    nl = (o_ref.shape[1] - L) // CH
    return jnp.concatenate([o_ref[1, L + (nl - 1 - j) * CH:L + (nl - j) * CH, :] for j in range(nl)], axis=0)


def _gout_fwd(o, proj, gw, L):
    T = o.shape[2]
    N = T - L
    ob = pl.BlockSpec((2, None, T, HD), lambda h: (0, h, 0, 0))

    def body(o_ref, z_ref, gw_ref, y_ref):
        y_ref[...] = _gout_fn(o_ref[0, L:, :], _backward_latent(o_ref, L), z_ref[L:, :], gw_ref[...]).astype(BF16)

    return _call(body, name="gout_fwd", out_shape=_sds((N, GH * HD), BF16), grid=(GH,),
                 in_specs=[ob, pl.BlockSpec((T, HD), lambda h: (0, C_Z // HD + h)), pl.BlockSpec((1, HD), lambda h: (0, 0))],
                 out_specs=pl.BlockSpec((N, HD), lambda h: (0, h)), sem=("parallel",))(o, proj, gw)


def _gout_bwd(o, proj, gw, dy, dproj, L):
    T = o.shape[2]
    N = T - L
    ob = pl.BlockSpec((2, None, T, HD), lambda h: (0, h, 0, 0))

    def body(o_ref, z_ref, gw_ref, dy_ref, _, do_ref, dz_ref, dgw_ref):
        _, vjp = jax.vjp(_gout_fn, o_ref[0, L:, :], _backward_latent(o_ref, L), z_ref[L:, :], gw_ref[...])
        g0, _, gz, ggw = vjp(dy_ref[...])
        do_ref[:L, :] = jnp.zeros((L, HD), F32)
        do_ref[L:, :] = g0
        dz_ref[:L, :] = jnp.zeros((L, HD), BF16)
        dz_ref[L:, :] = gz.astype(BF16)

        @pl.when(pl.program_id(0) == 0)
        def _():
            dgw_ref[...] = jnp.zeros_like(dgw_ref)

        dgw_ref[...] += ggw

    zb = pl.BlockSpec((T, HD), lambda h: (0, C_Z // HD + h))
    return _call(body, name="gout_bwd", out_shape=(_sds((GH, T, HD)), _sds(dproj.shape, BF16), _sds((1, HD))),
                 grid=(GH,),
                 in_specs=[ob, zb, pl.BlockSpec((1, HD), lambda h: (0, 0)), pl.BlockSpec((N, HD), lambda h: (0, h)), ANYSPEC],
                 out_specs=(pl.BlockSpec((None, T, HD), lambda h: (h, 0, 0)), zb, pl.BlockSpec((1, HD), lambda h: (0, 0))),
                 aliases={4: 1}, sem=("arbitrary",))(o, proj, gw, dy, dproj)


def _merge_fn(pa, pd, ga, gd):
    return jax.nn.sigmoid(ga) * pa + jax.nn.sigmoid(gd) * pd


def _merge_fwd(pa, pd, proj, L, *, br=256):
    N = pa.shape[0]
    lb = L // br
    row = pl.BlockSpec((br, D), lambda i: (i, 0))

    def body(pa_ref, pd_ref, ga_ref, gd_ref, y_ref):
        y_ref[...] = _merge_fn(pa_ref[...], pd_ref[...], ga_ref[...], gd_ref[...]).astype(BF16)

    return _call(body, name="merge_fwd", out_shape=_sds((N, D), BF16), grid=(N // br,),
                 in_specs=[row, row, pl.BlockSpec((br, D), lambda i: (i + lb, C_GATE // D)),
                           pl.BlockSpec((br, D), lambda i: (i + lb, C_GATE // D + 1))],
                 out_specs=row, sem=("parallel",))(pa, pd, proj, proj)


def _merge_bwd(pa, pd, proj, dy, L, *, br=256):
    N = pa.shape[0]
    T = N + L
    lb = L // br
    lrow = pl.BlockSpec((br, D), lambda i: (jnp.maximum(i - lb, 0), 0))

    def body(pa_ref, pd_ref, ga_ref, gd_ref, dy_ref, dpa_ref, dpd_ref, dg_ref):
        lat = pl.program_id(0) >= lb
        _, vjp = jax.vjp(_merge_fn, pa_ref[...], pd_ref[...], ga_ref[...], gd_ref[...])
        gpa, gpd, gga, ggd = vjp(dy_ref[...])
        dpa_ref[...] = gpa.astype(BF16)
        dpd_ref[...] = gpd.astype(BF16)
        dg_ref[:, :D] = jnp.where(lat, gga, 0.0).astype(BF16)
        dg_ref[:, D:] = jnp.where(lat, ggd, 0.0).astype(BF16)

    return _call(body, name="merge_bwd", out_shape=(_sds((N, D), BF16), _sds((N, D), BF16), _sds((T, C_END), BF16)),
                 grid=(T // br,),
                 in_specs=[lrow, lrow, pl.BlockSpec((br, D), lambda i: (i, C_GATE // D)),
                           pl.BlockSpec((br, D), lambda i: (i, C_GATE // D + 1)), lrow],
                 out_specs=(lrow, lrow, pl.BlockSpec((br, 2 * D), lambda i: (i, C_GATE // (2 * D)))),
                 sem=("arbitrary",))(pa, pd, proj, proj, dy)


def _resid_fwd(x, m, mod, i_g, *, name, br=256):
    R = x.shape[0]
    row = pl.BlockSpec((br, D), lambda i: (i, 0))

    def body(x_ref, m_ref, mod_ref, o_ref):
        o_ref[...] = x_ref[...] + mod_ref[i_g:i_g + 1, :] * m_ref[...]

    return _call(body, name=name, out_shape=_sds((R, D)), grid=(R // br,),
                 in_specs=[row, row, pl.BlockSpec((6, D), lambda i: (0, 0))], out_specs=row,
                 sem=("parallel",))(x, m, mod)


def _resid_bwd(dx, m, mod, i_g, *, name, br=256):
    R = dx.shape[0]
    row = pl.BlockSpec((br, D), lambda i: (i, 0))
    vec = pl.BlockSpec((1, D), lambda i: (0, 0))

    def body(dx_ref, m_ref, mod_ref, dm_ref, dg_ref):
        dxv = dx_ref[...]
        dm_ref[...] = (dxv * mod_ref[i_g:i_g + 1, :]).astype(BF16)

        @pl.when(pl.program_id(0) == 0)
        def _():
            dg_ref[...] = jnp.zeros_like(dg_ref)

        dg_ref[...] += jnp.sum(dxv * m_ref[...], axis=0, keepdims=True)

    return _call(body, name=name, out_shape=(_sds((R, D), BF16), _sds((1, D))), grid=(R // br,),
                 in_specs=[row, row, pl.BlockSpec((6, D), lambda i: (0, 0))], out_specs=(row, vec),
                 sem=("arbitrary",))(dx, m, mod)


def _ffn_fn(shifts, ug, uv, wg, wv, bg, bv):
    down, up = shifts

    def conv(x, w, b):
        return down(x) * w[0:1, :] + x * w[1:2, :] + up(x) * w[2:3, :] + b

    return _silu(conv(ug, wg, bg)) * conv(uv, wv, bv)


def _ffn_fwd(up, cw, cb, *, bw=256):
    N = up.shape[0]
    shifts = _make_shift(((0, N),))
    nb = DFF // bw

    def body(ug, uv, wg, wv, bg, bv, a_ref):
        a_ref[...] = _ffn_fn(shifts, ug[...], uv[...], wg[...], wv[...], bg[...], bv[...]).astype(BF16)

    def col(rows, off):
        return pl.BlockSpec((rows, bw), lambda j: (0, j + off))

    return _call(body, name="ffn_fwd", out_shape=_sds((N, DFF), BF16), grid=(nb,),
                 in_specs=[col(N, 0), col(N, nb), col(3, 0), col(3, nb), col(1, 0), col(1, nb)],
                 out_specs=col(N, 0), sem=("parallel",), vmem=VMEM_BIG)(up, up, cw, cw, cb, cb)


def _ffn_bwd(up, cw, cb, da, *, bw=256):
    N = up.shape[0]
    shifts = _make_shift(((0, N),))
    nb = DFF // bw

    def body(ug, uv, wg, wv, bg, bv, da_ref, dug, duv, dwg, dwv, dbg, dbv):
        _, vjp = jax.vjp(functools.partial(_ffn_fn, shifts), ug[...], uv[...], wg[...], wv[...], bg[...], bv[...])
        g = vjp(da_ref[...])
        dug[...] = g[0].astype(BF16)
        duv[...] = g[1].astype(BF16)
        dwg[...], dwv[...], dbg[...], dbv[...] = g[2], g[3], g[4], g[5]

    def col(rows, off):
        return pl.BlockSpec((rows, bw), lambda j: (0, j + off))

    half = (_sds((N, DFF), BF16), _sds((N, DFF), BF16), _sds((3, DFF)), _sds((3, DFF)), _sds((1, DFF)), _sds((1, DFF)))
    dug, duv, dwg, dwv, dbg, dbv = _call(
        body, name="ffn_bwd", out_shape=half, grid=(nb,),
        in_specs=[col(N, 0), col(N, nb), col(3, 0), col(3, nb), col(1, 0), col(1, nb), col(N, 0)],
        out_specs=(col(N, 0), col(N, 0), col(3, 0), col(3, 0), col(1, 0), col(1, 0)),
        sem=("parallel",), vmem=VMEM_BIG)(up, up, cw, cw, cb, cb, da)
    return (jnp.concatenate([dug, duv], axis=1), jnp.concatenate([dwg, dwv], axis=1),
            jnp.concatenate([dbg, dbv], axis=1))


def _head_fn(x1, dn, g2, fw, tgt):
    y = _rms(x1 + g2 * dn) * fw
    err = y - tgt
    return 0.5 * jnp.sum(jnp.mean(err * err, axis=-1))


def _head(x1, dn, mod, fw, tgt, *, br=256):
    N = x1.shape[0]
    row = pl.BlockSpec((br, D), lambda i: (i, 0))
    vec = pl.BlockSpec((1, D), lambda i: (0, 0))
    one = pl.BlockSpec((1, HD), lambda i: (0, 0))

    def body(x1_ref, dn_ref, mod_ref, fw_ref, tgt_ref, loss_ref, dx_ref, ddn_ref, dg_ref, dfw_ref):
        loss, (gx, gdn, gg, gfw) = jax.value_and_grad(_head_fn, argnums=(0, 1, 2, 3))(
            x1_ref[...], dn_ref[...], mod_ref[5:6, :], fw_ref[...], tgt_ref[...])
        dx_ref[...] = gx
        ddn_ref[...] = gdn.astype(BF16)

        @pl.when(pl.program_id(0) == 0)
        def _():
            loss_ref[...] = jnp.zeros_like(loss_ref)
            dg_ref[...] = jnp.zeros_like(dg_ref)
            dfw_ref[...] = jnp.zeros_like(dfw_ref)

        loss_ref[...] += jnp.broadcast_to(loss, (1, HD))
        dg_ref[...] += gg
        dfw_ref[...] += gfw

    return _call(body, name="head", out_shape=(_sds((1, HD)), _sds((N, D)), _sds((N, D), BF16), _sds((1, D)), _sds((1, D))),
                 grid=(N // br,), in_specs=[row, row, pl.BlockSpec((6, D), lambda i: (0, 0)), vec, row],
                 out_specs=(one, row, row, vec, vec), sem=("arbitrary",))(x1, dn, mod, fw, tgt)


def _adamw(w, g, m, v, *, name):
    shape = w.shape
    cols = shape[-1]
    rows = max(1, math.prod(shape[:-1]))
    w2, g2, m2, v2 = (t.reshape(rows, cols) for t in (w, g, m, v))
    br = 256 if rows % 256 == 0 else rows
    c1 = 1.0 - B1 ** STEP
    c2 = 1.0 - B2 ** STEP

    def body(w_ref, g_ref, m_ref, v_ref, d_ref, nm_ref, nv_ref):
        gv = g_ref[...]
        nm = B1 * m_ref[...] + (1.0 - B1) * gv
        nv = B2 * v_ref[...] + (1.0 - B2) * (gv * gv)
        d_ref[...] = -LR * ((nm / c1) / (jnp.sqrt(nv / c2) + AEPS) + WD * w_ref[...])
        nm_ref[...] = nm
        nv_ref[...] = nv

    blk = pl.BlockSpec((br, cols), lambda i: (i, 0))
    outs = _call(body, name=name, out_shape=(_sds((rows, cols)),) * 3, grid=(rows // br,),
                 in_specs=[blk] * 4, out_specs=(blk,) * 3, sem=("parallel",))(w2, g2, m2, v2)
    return tuple(t.reshape(shape) for t in outs)


def _adamw_many(items, *, name):
    k = len(items)
    shapes = [w.shape for w, _, _, _ in items]
    flat = [t.reshape(max(1, math.prod(t.shape[:-1])), t.shape[-1]) for it in items for t in it]
    c1 = 1.0 - B1 ** STEP
    c2 = 1.0 - B2 ** STEP

    def body(*refs):
        ins, outs = refs[:4 * k], refs[4 * k:]
        for i in range(k):
            w_ref, g_ref, m_ref, v_ref = ins[4 * i:4 * i + 4]
            gv = g_ref[...]
            nm = B1 * m_ref[...] + (1.0 - B1) * gv
            nv = B2 * v_ref[...] + (1.0 - B2) * (gv * gv)
            outs[3 * i][...] = -LR * ((nm / c1) / (jnp.sqrt(nv / c2) + AEPS) + WD * w_ref[...])
            outs[3 * i + 1][...] = nm
            outs[3 * i + 2][...] = nv

    res = _call(body, name=name, out_shape=tuple(_sds(flat[4 * i].shape) for i in range(k) for _ in range(3)))(*flat)
    return [tuple(res[3 * i + j].reshape(shapes[i]) for j in range(3)) for i in range(k)]


def _rope_tables(N, L):
    t = jnp.arange(N)
    pos = jnp.stack([(t // GRID_W).astype(F32), (t % GRID_W).astype(F32)], axis=1)
    inv = ROPE_THETA ** (-jnp.arange(0, HD // 2, 2, dtype=F32) / (HD // 2))
    ang = pos[:, :, None] * inv[None, None, :]
    cos = jnp.broadcast_to(jnp.cos(ang)[:, :, None, :], (N, 2, 2, HD // 4)).reshape(N, HD)
    sin = jnp.broadcast_to(jnp.sin(ang)[:, :, None, :], (N, 2, 2, HD // 4))
    sin = (sin * jnp.array([-1.0, 1.0], F32)[None, None, :, None]).reshape(N, HD)
    cos = jnp.concatenate([jnp.ones((L, HD), F32), cos], axis=0)
    sin = jnp.concatenate([jnp.zeros((L, HD), F32), sin], axis=0)
    return cos, sin


def _pad_lanes(v, off=0):
    return jnp.zeros((1, HD), F32).at[0, off:off + v.shape[0]].set(v)


def _local_step(x, ctx, tgt, mod_lat, mod_ctx, w_in, shards, small):
    N, L = x.shape[0], ctx.shape[0]
    T = N + L
    bounds = ((0, L), (L, T))
    qw, kw, gw = small["q_norm_w"], small["k_norm_w"], small["gdn_norm_w"]
    conv_w, ffn_w, ffn_b, fnw = small["conv_qkv_w"], small["ffn_conv_w"], small["ffn_conv_b"], small["final_norm_w"]
    alog = _pad_lanes(small["a_log"].reshape(-1), 2 * GH)
    dtb = _pad_lanes(small["dt_bias"].reshape(-1), 2 * GH)
    cos, sin = _rope_tables(N, L)
    bt = T
    bnl = N

    hc = _normmod_fwd(ctx, mod_ctx, 0, 1, name="normmod_ctx")
    hx = _normmod_fwd(x, mod_lat, 0, 1, name="normmod_x")
    h1 = jnp.concatenate([hc, hx], axis=0)
    proj = _mm(h1, w_in, name="mm_in", M=T, N=C_END, K=D, tb=True, bm=bt, bn=1024)
    aq, ak, av = _aprep_fwd(proj, cos, sin, qw, kw)
    (attn, attn32, lse), (up_g,) = _attn_fwd(aq, ak, av, L, _GatherTwoLevel([shards["w_up"]]))
    gq = _gprep_fwd(proj, conv_w, 0, bounds)
    gk = _gprep_fwd(proj, conv_w, 1, bounds)
    gv = _gprep_fwd(proj, conv_w, 2, bounds)
    bl = _bl_fwd(proj, alog, dtb)
    intra, (down_g, pa_g, pd_g, out_g) = _intra_fwd(
        gq, gk, gv, bl, L, _GatherTwoLevel([shards[n] for n in ("w_down", "w_pa", "w_pd", "w_out")]))
    w_up, w_down = up_g.reshape(2 * DFF, D), down_g.reshape(DFF, D)
    w_pa, w_pd, w_out = pa_g.reshape(D, D), pd_g.reshape(D, D), out_g.reshape(D, D)
    xinv, intra = intra[6], intra[:6]
    o, states = _scan_fwd(*intra, L)
    gdn = _gout_fwd(o, proj, gw, L)
    pa = _mm(attn, w_pa, name="mm_pa", M=N, N=D, K=D, bm=bnl)
    pd = _mm(gdn, w_pd, name="mm_pd", M=N, N=D, K=D, bm=bnl)
    y = _merge_fwd(pa, pd, proj, L)
    m = _mm(y, w_out, name="mm_out", M=N, N=D, K=D, bm=bnl)
    x1 = _resid_fwd(x, m, mod_lat, 2, name="resid1")
    h2 = _normmod_fwd(x1, mod_lat, 3, 4, name="normmod_x1")
    up = _mm(h2, w_up, name="mm_up", M=N, N=2 * DFF, K=D, tb=True, bm=bnl, bn=2 * DFF // 4)
    a = _ffn_fwd(up, ffn_w, ffn_b)
    dn = _mm(a, w_down, name="mm_down", M=N, N=D, K=DFF, bm=bnl)
    loss, dx2, ddn, dg2, dfnw = _head(x1, dn, mod_lat, fnw, tgt)

    da = _mm(ddn, w_down, name="mm_down_dx", M=N, N=DFF, K=D, tb=True, bm=bnl, bn=DFF // 2)
    g_down = _mm(a, ddn, name="mm_down_dw", M=DFF, N=D, K=N, ta=True, bm=DFF // 2, out_dtype=BF16)
    dup, d_ffn_w, d_ffn_b = _ffn_bwd(up, ffn_w, ffn_b, da)
    dh2 = _mm(dup, w_up, name="mm_up_dx", M=N, N=D, K=2 * DFF, bm=bnl, bk=2 * DFF // 4)
    g_up = _mm(dup, h2, name="mm_up_dw", M=2 * DFF, N=D, K=N, ta=True, bm=2 * DFF // 4, out_dtype=BF16)
    dx1, dsh2, dsc2 = _normmod_bwd(x1, mod_lat, 3, 4, dh2, 0, dx2, name="normmod_x1_bwd")
    dm, dg1 = _resid_bwd(dx1, m, mod_lat, 2, name="resid1_bwd")
    dy = _mm(dm, w_out, name="mm_out_dx", M=N, N=D, K=D, tb=True, bm=bnl)
    g_out = _mm(y, dm, name="mm_out_dw", M=D, N=D, K=N, ta=True, out_dtype=BF16)
    dpa, dpd, dproj = _merge_bwd(pa, pd, proj, dy, L)
    dattn = _mm(dpa, w_pa, name="mm_pa_dx", M=N, N=D, K=D, tb=True, bm=bnl)
    g_pa = _mm(attn, dpa, name="mm_pa_dw", M=D, N=D, K=N, ta=True, out_dtype=BF16)
    dgdn = _mm(dpd, w_pd, name="mm_pd_dx", M=N, N=D, K=D, tb=True, bm=bnl)
    g_pd = _mm(gdn, dpd, name="mm_pd_dw", M=D, N=D, K=N, ta=True, out_dtype=BF16)
    do, dproj, dgw = _gout_bwd(o, proj, gw, dgdn, dproj, L)
    cts, recv_a = _scan_bwd(*intra, states, do, L, _Exchange(
        [g_out.reshape(NDEV, D // NDEV, D), g_pa.reshape(NDEV, D // NDEV, D), g_pd.reshape(NDEV, D // NDEV, D)], True))
    (dgq, dgk, dgv, dbl), recv_b = _intra_bwd(gq, gk, gv, bl, xinv, cts, L, _Exchange(
        [g_up.reshape(NDEV, 2 * DFF // NDEV, D)], True))
    dproj, dwq = _gprep_bwd(proj, conv_w, 0, bounds, dgq, dproj)
    dproj, dwk = _gprep_bwd(proj, conv_w, 1, bounds, dgk, dproj)
    dproj, dwv = _gprep_bwd(proj, conv_w, 2, bounds, dgv, dproj)
    dproj, dalog, ddtb = _bl_bwd(proj, alog, dtb, dbl, dproj)
    (daq_h, dak_h, dav_h), recv_c = _attn_bwd(aq, ak, av, attn32, lse, dattn, L, _Exchange(
        [g_down.reshape(NDEV, DFF // NDEV, D)], True))
    recv = dict(zip(("w_out", "w_pa", "w_pd", "w_up", "w_down"), recv_a + recv_b + recv_c))
    dproj, dqw, dkw = _aprep_bwd(proj, cos, sin, qw, kw, daq_h, dak_h, dav_h, dproj, L)
    g_in = _mm(dproj, h1, name="mm_in_dw", M=C_END, N=D, K=T, ta=True, bm=1024, out_dtype=BF16)
    g_in = _unpad_columns(g_in).reshape(NDEV, W_END // NDEV, D)
    own_in = lax.dynamic_index_in_dim(g_in, _position()[3], axis=0, keepdims=False)
    *pending, token = _scatter_start(g_in, None, (0, D // 2), (), name="scatter_g_in_a_start")
    dh1 = _mm(dproj, w_in, name="mm_in_dx", M=T, N=D, K=C_END, bm=bt, bk=1024, after=(token,))
    grad_x, dsh1, dsc1 = _normmod_bwd(x, mod_lat, 0, 1, dh1, L, dx1, name="normmod_x_bwd")
    _, dcsh1, dcsc1 = _normmod_bwd(ctx, mod_ctx, 0, 1, dh1, 0, None, name="normmod_ctx_bwd")

    z1 = jnp.zeros((1, D), F32)
    dmod_lat = jnp.concatenate([dsh1, dsc1, dg1, dsh2, dsc2, dg2], axis=0)
    dmod_ctx = jnp.concatenate([dcsh1, dcsc1, z1, z1, z1, z1], axis=0)
    gsmall = {
        "q_norm_w": dqw, "k_norm_w": dkw, "gdn_norm_w": dgw,
        "conv_qkv_w": jnp.concatenate([dwq, dwk, dwv], axis=1),
        "a_log": dalog[0, 2 * GH:4 * GH], "dt_bias": ddtb[0, 2 * GH:4 * GH],
        "ffn_conv_w": d_ffn_w, "ffn_conv_b": d_ffn_b, "final_norm_w": dfnw,
    }
    return loss[0, 0], grad_x, (pending, own_in), recv, dmod_lat, dmod_ctx, gsmall


HBM = pl.BlockSpec(memory_space=pltpu.HBM)
ANYSPEC = pl.BlockSpec(memory_space=pl.ANY)


def _position():
    x, y, c = lax.axis_index("x"), lax.axis_index("y"), lax.axis_index("c")
    return x, y, c, 4 * x + 2 * y + c


def _peer(x, y, c, k):
    px = 1 - x if k & 4 else x
    py = 1 - y if k & 2 else y
    pc = 1 - c if k & 1 else c
    return (px, py, pc), 4 * px + 2 * py + pc


def _exchange(arrs, *, name, scatter):
    exch = _Exchange(arrs, scatter)
    n = exch.n

    def body(*refs):
        ins, outs, sems = refs[:n], refs[n:2 * n], refs[2 * n:]
        exch.start(ins, outs, sems)
        exch.finish(ins, outs, sems)

    outs = pl.pallas_call(body, name=name, out_shape=exch.out_shape, in_specs=[HBM] * n, out_specs=(HBM,) * n,
                          scratch_shapes=exch.scratch,
                          compiler_params=pltpu.CompilerParams(has_side_effects=True))(*arrs)
    return list(outs)


class _Exchange:
    def __init__(self, arrs, scatter):
        self.arrs, self.scatter, self.n = list(arrs), scatter, len(arrs)
        self.out_shape = tuple(_sds(a.shape if scatter else (NDEV,) + a.shape, a.dtype) for a in arrs)
        self.scratch = [pltpu.SemaphoreType.DMA((self.n, NDEV - 1)), pltpu.SemaphoreType.DMA((self.n, NDEV - 1)),
                        pltpu.SemaphoreType.DMA((self.n,))]

    def _copies(self, ins, outs, sems):
        send, recv, loc = sems
        x, y, c, me = _position()
        local = [pltpu.make_async_copy(ins[a].at[me] if self.scatter else ins[a], outs[a].at[me], loc.at[a])
                 for a in range(self.n)]
        remote = []
        for k in range(1, NDEV):
            peer, pid = _peer(x, y, c, k)
            for a in range(self.n):
                src = ins[a].at[pid] if self.scatter else ins[a]
                remote.append(pltpu.make_async_remote_copy(
                    src_ref=src, dst_ref=outs[a].at[me], send_sem=send.at[a, k - 1], recv_sem=recv.at[a, k - 1],
                    device_id=peer, device_id_type=MESH))
        return local, remote

    def start(self, ins, outs, sems):
        local, remote = self._copies(ins, outs, sems)
        for cp in local + remote:
            cp.start()

    def finish(self, ins, outs, sems):
        local, remote = self._copies(ins, outs, sems)
        for cp in remote:
            cp.wait()
        for cp in local:
            cp.wait()


class _GatherTwoLevel:
    scatter = False

    def __init__(self, arrs):
        self.arrs, self.n = list(arrs), len(arrs)
        self.out_shape = tuple(_sds((NDEV,) + a.shape, a.dtype) for a in arrs)
        self.scratch = [pltpu.SemaphoreType.DMA((self.n, NDEV - 1)), pltpu.SemaphoreType.DMA((self.n, NDEV - 1)),
                        pltpu.SemaphoreType.DMA((self.n,))]

    def _parts(self, ins, outs, sems):
        send, recv, loc = sems
        x, y, c, _ = _position()
        me, sibling = (x, y, c), (x, y, 1 - c)
        chips = [(1 - x, y), (x, 1 - y), (1 - x, 1 - y)]
        parts = []
        for a in range(self.n):
            slot = lambda px, py, pc, a=a: outs[a].at[4 * px + 2 * py + pc]

            def copy(k, owner, to, src=None, a=a, slot=slot):
                return pltpu.make_async_remote_copy(
                    src_ref=slot(*owner) if src is None else src, dst_ref=slot(*owner), send_sem=send.at[a, k],
                    recv_sem=recv.at[a, k], device_id=to, device_id_type=MESH)

            parts.append(dict(
                mine=pltpu.make_async_copy(ins[a], slot(*me), loc.at[a]),
                first=[copy(0, me, sibling, src=ins[a])] + [copy(1 + j, me, (*ch, c), src=ins[a]) for j, ch in enumerate(chips)],
                arrive=[copy(1 + j, (*ch, c), me) for j, ch in enumerate(chips)],
                passed=[copy(4 + j, (*ch, c), sibling) for j, ch in enumerate(chips)],
                rest=[copy(0, sibling, me)] + [copy(4 + j, (*ch, 1 - c), me) for j, ch in enumerate(chips)]))
        return parts

    def start(self, ins, outs, sems):
        for p in self._parts(ins, outs, sems):
            p["mine"].start()
            for cp in p["first"]:
                cp.start()

    def middle(self, ins, outs, sems):
        for p in self._parts(ins, outs, sems):
            for got, fwd in zip(p["arrive"], p["passed"]):
                got.wait_recv()
                fwd.start()

    def finish(self, ins, outs, sems):
        for p in self._parts(ins, outs, sems):
            for cp in p["rest"]:
                cp.wait_recv()
            for cp in p["first"] + p["passed"]:
                cp.wait_send()
            p["mine"].wait()


def _gather_two_level(blocks, *, name):
    exch = _GatherTwoLevel(blocks)
    n = exch.n

    def body(*refs):
        ins, outs, sems = refs[:n], refs[n:2 * n], refs[2 * n:]
        exch.start(ins, outs, sems)
        exch.middle(ins, outs, sems)
        exch.finish(ins, outs, sems)

    outs = pl.pallas_call(body, name=name, out_shape=exch.out_shape, in_specs=[HBM] * n, out_specs=(HBM,) * n,
                          scratch_shapes=exch.scratch,
                          compiler_params=pltpu.CompilerParams(has_side_effects=True))(*blocks)
    return list(outs)


SEM = pl.BlockSpec(memory_space=pltpu.SEMAPHORE)


def _scatter_copies(src_ref, land_ref, send_sems, recv_sems, cols):
    x, y, c, me = _position()
    span = (slice(None), pl.ds(*cols))
    copies = []
    for k in range(1, NDEV):
        peer, pid = _peer(x, y, c, k)
        copies.append(pltpu.make_async_remote_copy(
            src_ref=src_ref.at[pid].at[span], dst_ref=land_ref.at[me].at[span], send_sem=send_sems.at[k - 1],
            recv_sem=recv_sems.at[k - 1], device_id=peer, device_id_type=MESH))
    return copies


SPLIT_EFFECT = pltpu.SideEffectType.DATAFLOW_SIDE_EFFECTING


def _scatter_start(parts, land, cols, after, *, name):
    na = len(after)
    if land is None:
        land = lax.empty(parts.shape, parts.dtype)

    def body(src_ref, land_ref, *rest):
        send_sems, recv_sems, _, _, token = rest[na:]
        for cp in _scatter_copies(src_ref, land_ref, send_sems, recv_sems, cols):
            cp.start()
        token[...] = jnp.zeros_like(token)

    return pl.pallas_call(
        body, name=name,
        out_shape=(pltpu.SemaphoreType.DMA((NDEV - 1,)), pltpu.SemaphoreType.DMA((NDEV - 1,)),
                   pltpu.HBM(parts.shape, parts.dtype), pltpu.HBM(parts.shape, parts.dtype), _sds((8, HD))),
        in_specs=(HBM, HBM) + (pl.BlockSpec(memory_space=pl.ANY),) * na,
        out_specs=(SEM, SEM, HBM, HBM, pl.BlockSpec(memory_space=pltpu.VMEM)),
        input_output_aliases={0: 2, 1: 3}, compiler_params=pltpu.CompilerParams(has_side_effects=SPLIT_EFFECT),
    )(pltpu.with_memory_space_constraint(parts, pltpu.HBM), pltpu.with_memory_space_constraint(land, pltpu.HBM), *after)


def _scatter_wait(send_sems, recv_sems, src_thru, land_thru, cols, after, *, name):
    na = len(after)

    def body(src_ref, land_ref, send_sems, recv_sems, *rest):
        for cp in _scatter_copies(src_ref, land_ref, send_sems, recv_sems, cols):
            cp.wait_send()
            cp.wait_recv()

    return pl.pallas_call(
        body, name=name,
        out_shape=(pltpu.HBM(src_thru.shape, src_thru.dtype), pltpu.HBM(land_thru.shape, land_thru.dtype)),
        in_specs=(HBM, HBM, SEM, SEM) + (pl.BlockSpec(memory_space=pl.ANY),) * na, out_specs=(HBM, HBM),
        input_output_aliases={0: 0, 1: 1}, compiler_params=pltpu.CompilerParams(has_side_effects=SPLIT_EFFECT),
    )(src_thru, land_thru, send_sems, recv_sems, *after)


def _cast_bf16(w, *, name):
    rows, cols = w.shape
    br = 128 if rows % 128 == 0 else rows

    def body(w_ref, o_ref):
        o_ref[...] = w_ref[...].astype(BF16)

    blk = pl.BlockSpec((br, cols), lambda i: (i, 0))
    return _call(body, name=name, out_shape=_sds((rows, cols), BF16), grid=(rows // br,), in_specs=[blk],
                 out_specs=blk, sem=("parallel",))(w)


def _sum_slots(a, *, name):
    _, R, C = a.shape

    def body(a_ref, o_ref):
        s = a_ref[0]
        for d in range(1, NDEV):
            s = s + a_ref[d]
        o_ref[...] = s

    return _call(body, name=name, out_shape=_sds((R, C)))(a)


MODROWS = 16


def _mod_fwd(c9, w, b):
    cols = w.shape[1]

    def body(c_ref, w_ref, b_ref, o_ref):
        o_ref[...] = _nn(_silu(c_ref[...]), w_ref[...]) + b_ref[...]

    return _call(body, name="mod_fwd", out_shape=_sds((MODROWS, cols)))(c9, w, b)


def _mod_bwd(c9, dmy, dall, w):
    cols = w.shape[1]

    def body(c_ref, dmy_ref, dall_ref, w_ref, gw_ref, gb_ref, cp_ref):
        sc = _silu(c_ref[...])
        rows = lax.broadcasted_iota(jnp.int32, (MODROWS, 1), 0)
        d = dmy_ref[...]
        d_ctx = jnp.where(rows == NDEV, d, 0.0)
        sc_ctx = jnp.where(rows == NDEV, sc, 0.0)
        outer = lax.dot_general(sc_ctx, d_ctx, (((0,), (0,)), ((), ())), precision=HI, preferred_element_type=F32)
        gw_ref[...] = _tn(jnp.where(rows < NDEV, sc, 0.0), jnp.where(rows < NDEV, d, 0.0)) + outer
        gb_ref[...] = jnp.sum(dall_ref[...], axis=0, keepdims=True)
        cp_ref[...] = jnp.sum(_nt(d_ctx, w_ref[...]), axis=0, keepdims=True)

    return _call(body, name="mod_bwd", out_shape=(_sds((D, cols)), _sds((1, 6 * D)), _sds((1, D))),
                 vmem=VMEM_BIG)(c9, dmy, dall, w)


def _cctx_finish(parts, c_ctx, after):
    VM = pl.BlockSpec(memory_space=pltpu.VMEM)

    def body(p_ref, c_ref, *rest):
        o_ref = rest[-1]
        s = p_ref[0]
        for d in range(1, NDEV):
            s = s + p_ref[d]
        _, vjp = jax.vjp(_silu, c_ref[...])
        o_ref[...] = vjp(s)[0]

    return _call(body, name="cctx_finish", out_shape=_sds((1, D)),
                 in_specs=[VM, VM] + [pl.BlockSpec(memory_space=pl.ANY)] * len(after))(parts, c_ctx, *after)


def _adamw_recv(w, recv, m, v, *, name, own=None):
    rows, cols = w.shape
    bc = 256
    c1 = 1.0 - B1 ** STEP
    c2 = 1.0 - B2 ** STEP
    has_own = own is not None

    def body(w_ref, r_ref, m_ref, v_ref, *rest):
        g_ref, d_ref, nm_ref, nv_ref = rest[-4:]
        me = _position()[3]

        def slot(d):
            return jnp.where(me == d, rest[0][...], r_ref[d]) if has_own else r_ref[d]

        gv = slot(0).astype(F32)
        for d in range(1, NDEV):
            gv = gv + slot(d).astype(F32)
        nm = B1 * m_ref[...] + (1.0 - B1) * gv
        nv = B2 * v_ref[...] + (1.0 - B2) * (gv * gv)
        g_ref[...] = gv
        d_ref[...] = -LR * ((nm / c1) / (jnp.sqrt(nv / c2) + AEPS) + WD * w_ref[...])
        nm_ref[...] = nm
        nv_ref[...] = nv

    blk = pl.BlockSpec((rows, bc), lambda j: (0, j))
    return _call(body, name=name, out_shape=(_sds((rows, cols)),) * 4, grid=(cols // bc,),
                 in_specs=[blk, pl.BlockSpec((NDEV, rows, bc), lambda j: (0, 0, j)), blk, blk] + [blk] * has_own,
                 out_specs=(blk,) * 4, sem=("parallel",), vmem=VMEM_BIG)(w, recv, m, v, *([own] if has_own else []))


P_LAT, P_CTX, P_FNW, P_FFNB, P_CONV, P_FFNW, P_MISC, P_ROWS = 0, 8, 16, 24, 32, 48, 72, 80


def _rows_of(v, nrows):
    flat = v.reshape(-1)
    return jnp.pad(flat, (0, nrows * D - flat.shape[0])).reshape(nrows, D)


def _by_columns(g):
    n, r, c = g.shape
    return jnp.transpose(g, (1, 0, 2)).reshape(r, n * c)


def kernel(x, c, ctx, c_ctx, w_mod, b_mod, w_in, q_norm_w, k_norm_w, conv_qkv_w, a_log, dt_bias, gdn_norm_w, w_pa, w_pd, w_out, w_up, ffn_conv_w, ffn_conv_b, w_down, final_norm_w, loss_target, m_c_ctx, m_w_mod, m_b_mod, m_w_in, m_q_norm_w, m_k_norm_w, m_conv_qkv_w, m_a_log, m_dt_bias, m_gdn_norm_w, m_w_pa, m_w_pd, m_w_out, m_w_up, m_ffn_conv_w, m_ffn_conv_b, m_w_down, m_final_norm_w, v_c_ctx, v_w_mod, v_b_mod, v_w_in, v_q_norm_w, v_k_norm_w, v_conv_qkv_w, v_a_log, v_dt_bias, v_gdn_norm_w, v_w_pa, v_w_pd, v_w_out, v_w_up, v_ffn_conv_w, v_ffn_conv_b, v_w_down, v_final_norm_w):
    _, _, _, me = _position()
    mcols = w_mod.shape[2]

    transposed = ("w_in", "w_up")
    big = {"w_in": w_in[0].T, "w_pa": w_pa[0], "w_pd": w_pd[0], "w_out": w_out[0], "w_up": w_up[0].T, "w_down": w_down[0]}
    names = list(big)
    shards = {n: _cast_bf16(big[n], name="cast_" + n) for n in names}
    w_in_g, c_all, conv_g, ffnw_g = _gather_two_level([shards["w_in"], c, conv_qkv_w[0], ffn_conv_w[0]],
                                                      name="gather_w_in")
    w_in_full = w_in_g.reshape(W_END, D)
    w_in_pad = _pad_columns(w_in_full)

    c9 = jnp.concatenate([c_all.reshape(NDEV, D), jnp.pad(c_ctx[None], ((0, MODROWS - NDEV - 1), (0, 0)))], axis=0)
    b_loc = lax.dynamic_slice(b_mod, (0, me * mcols), (1, mcols))
    mod_all, = _exchange([_mod_fwd(c9, w_mod[0], b_loc)], name="gather_mod", scatter=False)
    mod_lat = lax.dynamic_index_in_dim(mod_all, me, axis=1, keepdims=False).reshape(6, D)
    mod_ctx = mod_all[:, NDEV, :].reshape(6, D)

    small = {"q_norm_w": q_norm_w, "k_norm_w": k_norm_w, "gdn_norm_w": gdn_norm_w, "a_log": a_log, "dt_bias": dt_bias,
             "conv_qkv_w": _by_columns(conv_g), "ffn_conv_w": _by_columns(ffnw_g), "ffn_conv_b": ffn_conv_b,
             "final_norm_w": final_norm_w[None]}
    loss_me, grad_x, (pending_in, own_in), recv, dmod_lat, dmod_ctx, gs = _local_step(
        x[0], ctx[0], loss_target[0], mod_lat, mod_ctx, w_in_pad, shards, small)

    moments = {"w_in": (m_w_in, v_w_in), "w_pa": (m_w_pa, v_w_pa), "w_pd": (m_w_pd, v_w_pd),
               "w_out": (m_w_out, v_w_out), "w_up": (m_w_up, v_w_up), "w_down": (m_w_down, v_w_down)}
    res = {}
    def finish(n, outs):
        return tuple((t.T if n in transposed else t)[None] for t in outs)

    def moment(t, n):
        return t[0].T if n in transposed else t[0]

    for n in recv:
        res[n] = finish(n, _adamw_recv(big[n], recv[n], moment(moments[n][0], n), moment(moments[n][1], n),
                                       name="adamw_" + n))

    misc = jnp.concatenate([gs["q_norm_w"][0], gs["k_norm_w"][0], gs["gdn_norm_w"][0], gs["a_log"], gs["dt_bias"],
                            loss_me[None]])
    pack = jnp.concatenate([_rows_of(dmod_lat, P_CTX - P_LAT), _rows_of(dmod_ctx, P_FNW - P_CTX),
                            _rows_of(gs["final_norm_w"], P_FFNB - P_FNW), _rows_of(gs["ffn_conv_b"], P_CONV - P_FFNB),
                            _rows_of(gs["conv_qkv_w"], P_FFNW - P_CONV), _rows_of(gs["ffn_conv_w"], P_MISC - P_FFNW),
                            _rows_of(misc, P_ROWS - P_MISC)], axis=0)
    pack_all, = _exchange([pack], name="gather_pack", scatter=False)
    tot = _sum_slots(pack_all, name="sum_pack")
    dall = jnp.concatenate([pack_all[:, P_LAT:P_LAT + 6, :].reshape(NDEV, 6 * D),
                            jnp.pad(tot[P_CTX:P_CTX + 6].reshape(1, 6 * D), ((0, MODROWS - NDEV - 1), (0, 0)))], axis=0)
    dmy = lax.dynamic_slice(dall, (0, me * mcols), (MODROWS, mcols))
    g_w_mod, g_b_mod, cpart = _mod_bwd(c9, dmy, dall, w_mod[0])
    cparts, = _exchange([cpart], name="gather_cctx", scatter=False)
    sems_a, land = pending_in[:2], pending_in[3]
    *sems_b, g_in_thru, land, token_b = _scatter_start(pending_in[2], land, (D // 2, D // 2), (cparts,),
                                                       name="scatter_g_in_b_start")
    g_c_ctx = _cctx_finish(cparts, c_ctx[None], (token_b,))[0]

    nconv, nffn = 3 * GH * HD, 2 * DFF
    conv_tot = tot[P_CONV:P_FFNW].reshape(-1)[:3 * nconv].reshape(3, nconv)
    ffnw_tot = tot[P_FFNW:P_MISC].reshape(-1)[:3 * nffn].reshape(3, nffn)
    mrow = tot[P_MISC]
    grads = {
        "c_ctx": g_c_ctx, "w_mod": g_w_mod[None], "b_mod": g_b_mod,
        "q_norm_w": mrow[None, 0:HD], "k_norm_w": mrow[None, HD:2 * HD], "gdn_norm_w": mrow[None, 2 * HD:3 * HD],
        "conv_qkv_w": lax.dynamic_slice(conv_tot, (0, me * (nconv // NDEV)), (3, nconv // NDEV))[None],
        "a_log": mrow[3 * HD:3 * HD + 2 * GH].reshape(1, 2, GH),
        "dt_bias": mrow[3 * HD + 2 * GH:3 * HD + 4 * GH].reshape(1, 2, GH),
        "ffn_conv_w": lax.dynamic_slice(ffnw_tot, (0, me * (nffn // NDEV)), (3, nffn // NDEV))[None],
        "ffn_conv_b": tot[P_FFNB:P_CONV].reshape(-1)[:nffn][None],
        "final_norm_w": tot[P_FNW],
    }
    loss = mrow[3 * HD + 4 * GH]
    given = {"c_ctx": (c_ctx, m_c_ctx, v_c_ctx), "w_mod": (w_mod, m_w_mod, v_w_mod), "b_mod": (b_mod, m_b_mod, v_b_mod),
             "q_norm_w": (q_norm_w, m_q_norm_w, v_q_norm_w), "k_norm_w": (k_norm_w, m_k_norm_w, v_k_norm_w),
             "conv_qkv_w": (conv_qkv_w, m_conv_qkv_w, v_conv_qkv_w), "a_log": (a_log, m_a_log, v_a_log),
             "dt_bias": (dt_bias, m_dt_bias, v_dt_bias), "gdn_norm_w": (gdn_norm_w, m_gdn_norm_w, v_gdn_norm_w),
             "ffn_conv_w": (ffn_conv_w, m_ffn_conv_w, v_ffn_conv_w), "ffn_conv_b": (ffn_conv_b, m_ffn_conv_b, v_ffn_conv_b),
             "final_norm_w": (final_norm_w, m_final_norm_w, v_final_norm_w)}
    res["w_mod"] = (grads["w_mod"],) + _adamw(w_mod, grads["w_mod"], m_w_mod, v_w_mod, name="adamw_w_mod")
    small_names = [n for n in given if n != "w_mod"]
    updates = _adamw_many([(given[n][0], grads[n], given[n][1], given[n][2]) for n in small_names], name="adamw_small")
    for n, upd in zip(small_names, updates):
        res[n] = (grads[n],) + upd

    g_in_thru, land = _scatter_wait(*sems_a, g_in_thru, land, (0, D // 2), [res[n][1] for n in res],
                                    name="scatter_g_in_a_wait")
    _, land = _scatter_wait(*sems_b, g_in_thru, land, (D // 2, D // 2), (), name="scatter_g_in_b_wait")
    res["w_in"] = finish("w_in", _adamw_recv(big["w_in"], land, moment(m_w_in, "w_in"), moment(v_w_in, "w_in"),
                                             name="adamw_w_in", own=own_in))

    order = ["c_ctx", "w_mod", "b_mod", "w_in", "q_norm_w", "k_norm_w", "conv_qkv_w", "a_log", "dt_bias", "gdn_norm_w",
             "w_pa", "w_pd", "w_out", "w_up", "ffn_conv_w", "ffn_conv_b", "w_down", "final_norm_w"]
    return (loss, grad_x[None], *[res[n][0] for n in order], *[res[n][1] for n in order],
            *[res[n][2] for n in order], *[res[n][3] for n in order])
```

```python
import functools
import math

import jax
import jax.numpy as jnp
from jax import lax
from jax.experimental import pallas as pl
from jax.experimental.pallas import tpu as pltpu

F32 = jnp.float32
BF16 = jnp.bfloat16
HI = lax.Precision.HIGHEST
MESH = pl.DeviceIdType.MESH

NDEV = 8
D = 1024
HD = 128
AH, AKV, GRP = 8, 2, 4
GH = 8
CH = 64
DFF = 2816
GRID_W = 64
EPS = 1e-6
ROPE_THETA = 10000.0
LOG2E = math.log2(math.e)
C_KV, C_AQ, C_QKV, C_BL, C_Z, C_GATE, C_END = 0, 512, 1536, 4608, 5120, 6144, 8192
W_QKV, W_AQ, W_Z, W_END = 512, 3616, 4640, 7712


def _pad_columns(w):
    zeros = jnp.zeros((C_Z - C_QKV - (W_AQ - W_QKV), D), w.dtype)
    return jnp.concatenate([w[:W_QKV], w[W_AQ:W_Z], w[W_QKV:W_AQ], zeros, w[W_Z:]], axis=0)


def _unpad_columns(g):
    return jnp.concatenate([g[:C_AQ], g[C_QKV:C_QKV + W_AQ - W_QKV], g[C_AQ:C_QKV], g[C_Z:]], axis=0)
LR, B1, B2, AEPS, WD, STEP = 0.001, 0.9, 0.999, 1e-08, 0.01, 10
VMEM_BIG = 56 * 1024 * 1024
INTRA_FWD_CHUNKS = 36
INTRA_BWD_CHUNKS = 36


def _call(body, *, name, out_shape, grid=None, in_specs=None, out_specs=None, scratch=(), sem=None,
          vmem=None, aliases=None):
    params = {}
    if sem is not None:
        params["dimension_semantics"] = sem
    if vmem is not None:
        params["vmem_limit_bytes"] = vmem
    kw = {}
    if grid is not None:
        kw["grid"] = grid
    if in_specs is not None:
        kw["in_specs"] = in_specs
    if out_specs is not None:
        kw["out_specs"] = out_specs
    if aliases:
        kw["input_output_aliases"] = aliases
    return pl.pallas_call(body, name=name, out_shape=out_shape, scratch_shapes=list(scratch),
                          compiler_params=pltpu.CompilerParams(**params), **kw)


def _call_carrying(body, exch, *, name, out_shape, grid, in_specs, out_specs, scratch=(), vmem=None):
    n, nin, nout, nscr = exch.n, len(in_specs), len(out_shape), len(scratch)
    steps = math.prod(grid)
    mid = (2 * steps) // 3

    def wrapped(*refs):
        ins, cins = refs[:nin], refs[nin:nin + n]
        outs, couts = refs[nin + n:nin + n + nout], refs[nin + n + nout:nin + 2 * n + nout]
        scr, sems = refs[nin + 2 * n + nout:nin + 2 * n + nout + nscr], refs[nin + 2 * n + nout + nscr:]
        ids = [pl.program_id(i) for i in range(len(grid))]
        first = functools.reduce(jnp.logical_and, [i == 0 for i in ids])
        last = functools.reduce(jnp.logical_and, [i == g - 1 for i, g in zip(ids, grid)])

        @pl.when(first)
        def _():
            exch.start(cins, couts, sems)

        if hasattr(exch, "middle"):
            linear = functools.reduce(lambda acc, ig: acc * ig[1] + ig[0], zip(ids, grid), 0)

            @pl.when(linear == mid)
            def _():
                exch.middle(cins, couts, sems)

        body(*ins, *outs, *scr)

        @pl.when(last)
        def _():
            exch.finish(cins, couts, sems)

    params = {"dimension_semantics": ("arbitrary",) * len(grid)}
    if vmem is not None:
        params["vmem_limit_bytes"] = vmem
    fn = pl.pallas_call(wrapped, name=name, out_shape=tuple(out_shape) + exch.out_shape, grid=grid,
                        in_specs=list(in_specs) + [HBM] * n, out_specs=tuple(out_specs) + (HBM,) * n,
                        scratch_shapes=list(scratch) + exch.scratch, compiler_params=pltpu.CompilerParams(**params))

    def run(*args):
        res = fn(*args, *exch.arrs)
        return res[:nout], list(res[nout:])

    return run


def _sds(shape, dtype=F32):
    return jax.ShapeDtypeStruct(tuple(shape), dtype)


def _dot(a, b, ca, cb):
    return lax.dot_general(a.astype(BF16), b.astype(BF16), (((ca,), (cb,)), ((), ())),
                           preferred_element_type=F32)


@jax.custom_vjp
def _nn(a, b):
    return _dot(a, b, 1, 0)


@jax.custom_vjp
def _nt(a, b):
    return _dot(a, b, 1, 1)


@jax.custom_vjp
def _tn(a, b):
    return _dot(a, b, 0, 0)


_nn.defvjp(lambda a, b: (_nn(a, b), (a, b)), lambda r, g: (_nt(g, r[1]), _tn(r[0], g)))
_nt.defvjp(lambda a, b: (_nt(a, b), (a, b)), lambda r, g: (_nn(g, r[1]), _tn(g, r[0])))
_tn.defvjp(lambda a, b: (_tn(a, b), (a, b)), lambda r, g: (_nt(r[1], g), _nn(r[0], g)))


def _mdot(a, b):
    return jnp.dot(a, b, precision=lax.Precision.HIGH, preferred_element_type=F32)


def _maskdot(mask, a, cm):
    hi = a.astype(BF16)
    r = a - hi.astype(F32)
    mid = r.astype(BF16)
    lo = (r - mid.astype(F32)).astype(BF16)
    mb = mask.astype(BF16)
    dims = (((cm,), (0,)), ((), ()))
    return (lax.dot_general(mb, hi, dims, preferred_element_type=F32)
            + lax.dot_general(mb, mid, dims, preferred_element_type=F32)
            + lax.dot_general(mb, lo, dims, preferred_element_type=F32))


@jax.custom_vjp
def _mask_nn(mask, a):
    return _maskdot(mask, a, 1)


_mask_nn.defvjp(lambda mask, a: (_maskdot(mask, a, 1), mask),
                lambda mask, g: (jnp.zeros_like(mask), _maskdot(mask, g, 0)))


@jax.custom_vjp
def _saved_inverse(lmat, x):
    return x


def _saved_inverse_bwd(x, g):
    t = lax.dot_general(x, g, (((0,), (0,)), ((), ())), precision=lax.Precision.HIGH, preferred_element_type=F32)
    dl = lax.dot_general(t, x, (((1,), (1,)), ((), ())), precision=lax.Precision.HIGH, preferred_element_type=F32)
    return -dl, jnp.zeros_like(x)


_saved_inverse.defvjp(lambda lmat, x: (x, x), _saved_inverse_bwd)


def _row_ids(shape):
    return lax.broadcasted_iota(jnp.int32, shape, 0)


def _shift_rows(x, down, bounds):
    n = x.shape[0]
    rows = _row_ids(x.shape)
    y = pltpu.roll(x, 1 if down else n - 1, 0)
    edge = functools.reduce(jnp.logical_or, [rows == (s if down else e - 1) for s, e in bounds])
    return jnp.where(edge, 0.0, y)


def _make_shift(bounds):
    @jax.custom_vjp
    def down(x):
        return _shift_rows(x, True, bounds)

    @jax.custom_vjp
    def up(x):
        return _shift_rows(x, False, bounds)

    down.defvjp(lambda x: (down(x), None), lambda _, g: (up(g),))
    up.defvjp(lambda x: (up(x), None), lambda _, g: (down(g),))
    return down, up


@jax.custom_vjp
def _swap32(x):
    lane = lax.broadcasted_iota(jnp.int32, x.shape, x.ndim - 1)
    return jnp.where((lane % 64) < 32, pltpu.roll(x, HD - 32, x.ndim - 1), pltpu.roll(x, 32, x.ndim - 1))


_swap32.defvjp(lambda x: (_swap32(x), None), lambda _, g: (_swap32(g),))


def _rms(x):
    return x * lax.rsqrt(jnp.mean(x * x, axis=-1, keepdims=True) + EPS)


def _silu(x):
    return x * jax.nn.sigmoid(x)


def _mm(a, b, *, name, M, N, K, ta=False, tb=False, out_dtype=F32, bm=None, bn=None, bk=None, after=()):
    bm, bn, bk = bm or M, bn or N, bk or K
    assert M % bm == 0 and N % bn == 0 and K % bk == 0, (name, M, N, K, bm, bn, bk)
    nk = K // bk
    ca, cb = (0 if ta else 1), (1 if tb else 0)
    na = len(after)

    def body(a_ref, b_ref, *rest):
        o_ref, acc = rest[na], rest[na + 1:]
        r = _dot(a_ref[...], b_ref[...], ca, cb)
        if nk == 1:
            o_ref[...] = r.astype(out_dtype)
        else:
            acc_ref, = acc
            k = pl.program_id(2)

            @pl.when(k == 0)
            def _():
                acc_ref[...] = r

            @pl.when(k > 0)
            def _():
                acc_ref[...] += r

            @pl.when(k == nk - 1)
            def _():
                o_ref[...] = acc_ref[...].astype(out_dtype)

    a_spec = pl.BlockSpec((bk, bm), lambda i, j, k: (k, i)) if ta else pl.BlockSpec((bm, bk), lambda i, j, k: (i, k))
    b_spec = pl.BlockSpec((bn, bk), lambda i, j, k: (j, k)) if tb else pl.BlockSpec((bk, bn), lambda i, j, k: (k, j))
    return _call(body, name=name, out_shape=_sds((M, N), out_dtype), grid=(M // bm, N // bn, nk),
                 in_specs=[a_spec, b_spec] + [pl.BlockSpec(memory_space=pl.ANY)] * na,
                 out_specs=pl.BlockSpec((bm, bn), lambda i, j, k: (i, j)),
                 scratch=[pltpu.VMEM((bm, bn), F32)] if nk > 1 else [],
                 sem=("parallel", "parallel", "arbitrary"), vmem=VMEM_BIG)(a, b, *after)


def _normmod_fn(x, sh, sc):
    return _rms(x) * (1.0 + sc) + sh


def _normmod_fwd(x, mod, i_sh, i_sc, *, name, br=256):
    R = x.shape[0]

    def body(x_ref, mod_ref, o_ref):
        o_ref[...] = _normmod_fn(x_ref[...], mod_ref[i_sh:i_sh + 1, :], mod_ref[i_sc:i_sc + 1, :]).astype(BF16)

    return _call(body, name=name, out_shape=_sds((R, D), BF16), grid=(R // br,),
                 in_specs=[pl.BlockSpec((br, D), lambda i: (i, 0)), pl.BlockSpec((6, D), lambda i: (0, 0))],
                 out_specs=pl.BlockSpec((br, D), lambda i: (i, 0)), sem=("parallel",))(x, mod)


def _normmod_bwd(x, mod, i_sh, i_sc, dh, dh_off, res, *, name, br=256):
    R = x.shape[0]
    ob = dh_off // br
    has_res = res is not None

    def body(x_ref, mod_ref, dh_ref, *rest):
        if has_res:
            res_ref, dx_ref, dsh_ref, dsc_ref = rest
        else:
            dx_ref, dsh_ref, dsc_ref = rest
        sh, sc = mod_ref[i_sh:i_sh + 1, :], mod_ref[i_sc:i_sc + 1, :]
        _, vjp = jax.vjp(_normmod_fn, x_ref[...], sh, sc)
        dx, dsh, dsc = vjp(dh_ref[...])
        dx_ref[...] = dx + res_ref[...] if has_res else dx

        @pl.when(pl.program_id(0) == 0)
        def _():
            dsh_ref[...] = jnp.zeros_like(dsh_ref)
            dsc_ref[...] = jnp.zeros_like(dsc_ref)

        dsh_ref[...] += dsh
        dsc_ref[...] += dsc

    row = pl.BlockSpec((br, D), lambda i: (i, 0))
    vec = pl.BlockSpec((1, D), lambda i: (0, 0))
    ins = [row, pl.BlockSpec((6, D), lambda i: (0, 0)), pl.BlockSpec((br, D), lambda i: (i + ob, 0))]
    args = [x, mod, dh]
    if has_res:
        ins.append(row)
        args.append(res)
    return _call(body, name=name, out_shape=(_sds((R, D)), _sds((1, D)), _sds((1, D))), grid=(R // br,),
                 in_specs=ins, out_specs=(row, vec, vec), sem=("arbitrary",))(*args)


def _rope(x, cos, sin):
    return x * cos + _swap32(x) * sin


def _aprep_fn(qs, ks, cos, sin, qw, kw):
    return ([_rope(_rms(q) * qw, cos, sin) for q in qs], [_rope(_rms(k) * kw, cos, sin) for k in ks])


def _aprep_fwd(proj, cos, sin, qw, kw, *, br=256):
    T = proj.shape[0]

    def body(x_ref, cos_ref, sin_ref, qw_ref, kw_ref, q_ref, k_ref, v_ref):
        qs = [x_ref[:, C_AQ + h * HD:C_AQ + (h + 1) * HD] for h in range(AH)]
        ks = [x_ref[:, h * HD:(h + 1) * HD] for h in range(AKV)]
        qo, ko = _aprep_fn(qs, ks, cos_ref[...], sin_ref[...], qw_ref[...], kw_ref[...])
        for h in range(AH):
            q_ref[h] = qo[h].astype(BF16)
        for h in range(AKV):
            k_ref[h] = ko[h].astype(BF16)
            v_ref[h] = x_ref[:, (AKV + h) * HD:(AKV + h + 1) * HD].astype(BF16)

    tab = pl.BlockSpec((br, HD), lambda i: (i, 0))
    vec = pl.BlockSpec((1, HD), lambda i: (0, 0))
    return _call(body, name="aprep_fwd",
                 out_shape=(_sds((AH, T, HD), BF16), _sds((AKV, T, HD), BF16), _sds((AKV, T, HD), BF16)),
                 grid=(T // br,),
                 in_specs=[pl.BlockSpec((br, C_QKV), lambda i: (i, 0)), tab, tab, vec, vec],
                 out_specs=(pl.BlockSpec((AH, br, HD), lambda i: (0, i, 0)),
                            pl.BlockSpec((AKV, br, HD), lambda i: (0, i, 0)),
                            pl.BlockSpec((AKV, br, HD), lambda i: (0, i, 0))),
                 sem=("parallel",))(proj, cos, sin, qw, kw)


def _aprep_bwd(proj, cos, sin, qw, kw, dq, dk, dv, dproj, L, *, br=256):
    T = proj.shape[0]
    lb = L // br

    def body(x_ref, cos_ref, sin_ref, qw_ref, kw_ref, dq_ref, dk_ref, dv_ref, _, dx_ref, dqw_ref, dkw_ref):
        i = pl.program_id(0)
        qs = [x_ref[:, C_AQ + h * HD:C_AQ + (h + 1) * HD] for h in range(AH)]
        ks = [x_ref[:, h * HD:(h + 1) * HD] for h in range(AKV)]
        _, vjp = jax.vjp(_aprep_fn, qs, ks, cos_ref[...], sin_ref[...], qw_ref[...], kw_ref[...])
        is_lat = i >= lb
        dqs = [jnp.where(is_lat, dq_ref[h], 0.0) for h in range(AH)]
        dks = [dk_ref[h] for h in range(AKV)]
        gq, gk, _, _, gqw, gkw = vjp((dqs, dks))
        for h in range(AH):
            dx_ref[:, C_AQ + h * HD:C_AQ + (h + 1) * HD] = gq[h].astype(BF16)
        for h in range(AKV):
            dx_ref[:, h * HD:(h + 1) * HD] = gk[h].astype(BF16)
            dx_ref[:, (AKV + h) * HD:(AKV + h + 1) * HD] = dv_ref[h].astype(BF16)

        @pl.when(i == 0)
        def _():
            dqw_ref[...] = jnp.zeros_like(dqw_ref)
            dkw_ref[...] = jnp.zeros_like(dkw_ref)

        dqw_ref[...] += gqw
        dkw_ref[...] += gkw

    tab = pl.BlockSpec((br, HD), lambda i: (i, 0))
    vec = pl.BlockSpec((1, HD), lambda i: (0, 0))
    kvb = pl.BlockSpec((AKV, br, HD), lambda i: (0, i, 0))
    blk = pl.BlockSpec((br, C_QKV), lambda i: (i, 0))
    return _call(body, name="aprep_bwd", out_shape=(_sds(dproj.shape, BF16), _sds((1, HD)), _sds((1, HD))),
                 grid=(T // br,),
                 in_specs=[blk, tab, tab, vec, vec,
                           pl.BlockSpec((AH, br, HD), lambda i: (0, jnp.maximum(i - lb, 0), 0)), kvb, kvb, ANYSPEC],
                 out_specs=(blk, vec, vec), aliases={8: 0},
                 sem=("arbitrary",))(proj, cos, sin, qw, kw, dq, dk, dv, dproj)


def _attn_grad(q, k, v, o, lse2, do):
    scale = HD ** -0.5
    p = jnp.exp2(_dot(q, k, 1, 1) * (scale * LOG2E) - lse2)
    dp = _dot(do, v, 1, 1)
    ds = p * (dp - jnp.sum(do * o, axis=-1, keepdims=True)) * scale
    return _dot(ds, k, 1, 0), _dot(ds, q, 0, 0), _dot(p, do, 0, 0)


ATTN_KEYS = 256


def _attn_fwd(q, k, v, L, exch, *, bq=128):
    T = q.shape[1]
    N = T - L
    lb = L // bq
    assert T % ATTN_KEYS == 0
    scale = HD ** -0.5
    heads = range(GRP)

    def body(q_ref, k_ref, v_ref, o_ref, o32_ref, lse_ref):
        qs = [q_ref[g] for g in heads]
        m = [jnp.full((bq, 1), -jnp.inf, F32) for _ in heads]
        l = [jnp.zeros((bq, 1), F32) for _ in heads]
        acc = [jnp.zeros((bq, HD), F32) for _ in heads]
        for c in range(T // ATTN_KEYS):
            kc, vc = k_ref[c * ATTN_KEYS:(c + 1) * ATTN_KEYS, :], v_ref[c * ATTN_KEYS:(c + 1) * ATTN_KEYS, :]
            s = [_dot(qs[g], kc, 1, 1) * (scale * LOG2E) for g in heads]
            m_new = [jnp.maximum(m[g], jnp.max(s[g], axis=-1, keepdims=True)) for g in heads]
            alpha = [jnp.exp2(m[g] - m_new[g]) for g in heads]
            p = [jnp.exp2(s[g] - m_new[g]) for g in heads]
            l = [l[g] * alpha[g] + jnp.sum(p[g], axis=-1, keepdims=True) for g in heads]
            acc = [acc[g] * alpha[g] + _dot(p[g], vc, 1, 0) for g in heads]
            m = m_new
        for g in heads:
            o = acc[g] / l[g]
            o_ref[:, g * HD:(g + 1) * HD] = o.astype(BF16)
            o32_ref[:, g * HD:(g + 1) * HD] = o
            lse_ref[g] = jnp.broadcast_to(m[g] + jnp.log2(l[g]), (bq, HD))

    kvb = pl.BlockSpec((None, T, HD), lambda g, i: (g, 0, 0))
    ob = pl.BlockSpec((bq, GRP * HD), lambda g, i: (i, g))
    return _call_carrying(
        body, exch, name="attn_fwd",
        out_shape=(_sds((N, AH * HD), BF16), _sds((N, AH * HD)), _sds((AH, N, HD))), grid=(AKV, N // bq),
        in_specs=[pl.BlockSpec((GRP, bq, HD), lambda g, i: (g, i + lb, 0)), kvb, kvb],
        out_specs=(ob, ob, pl.BlockSpec((GRP, bq, HD), lambda g, i: (g, i, 0))), vmem=VMEM_BIG)(q, k, v)


def _attn_bwd(q, k, v, o32, lse, do, L, exch, *, bq=128):
    T = q.shape[1]
    N = T - L
    lb = L // bq

    def body(q_ref, k_ref, v_ref, o_ref, lse_ref, do_ref, dq_ref, dk_ref, dv_ref):
        rows = lambda r: jnp.concatenate([r[:, g * HD:(g + 1) * HD] for g in range(GRP)], axis=0)
        lse = jnp.max(lse_ref[...].reshape(GRP * bq, HD), axis=-1, keepdims=True)
        dq, dk, dv = _attn_grad(q_ref[...].reshape(GRP * bq, HD), k_ref[...], v_ref[...], rows(o_ref), lse, rows(do_ref))
        dq_ref[...] = dq.reshape(GRP, bq, HD)

        @pl.when(pl.program_id(1) == 0)
        def _():
            dk_ref[...] = jnp.zeros_like(dk_ref)
            dv_ref[...] = jnp.zeros_like(dv_ref)

        dk_ref[...] += dk
        dv_ref[...] += dv

    kvb = pl.BlockSpec((None, T, HD), lambda g, i: (g, 0, 0))
    qb = pl.BlockSpec((GRP, bq, HD), lambda g, i: (g, i + lb, 0))
    hb = pl.BlockSpec((GRP, bq, HD), lambda g, i: (g, i, 0))
    ob = pl.BlockSpec((bq, GRP * HD), lambda g, i: (i, g))
    return _call_carrying(body, exch, name="attn_bwd",
                          out_shape=(_sds((AH, N, HD)), _sds((AKV, T, HD)), _sds((AKV, T, HD))), grid=(AKV, N // bq),
                          in_specs=[qb, kvb, kvb, ob, hb, ob], out_specs=(hb, kvb, kvb),
                          vmem=VMEM_BIG)(q, k, v, o32, lse, do)


def _gprep_fn(kind, shifts, x, w):
    down, up = shifts
    y = down(x) * w[0:1, :] + x * w[1:2, :] + up(x) * w[2:3, :]
    a = _silu(y)
    if kind == 2:
        return a
    a = a * lax.rsqrt(jnp.sum(a * a, axis=-1, keepdims=True) + EPS)
    return a * (HD ** -0.5) if kind == 0 else a


def _gprep_fwd(proj, conv_w, kind, bounds):
    T = proj.shape[0]
    shifts = _make_shift(bounds)
    cb = C_QKV // HD + kind * GH

    def body(x_ref, w_ref, o_ref):
        o_ref[...] = _gprep_fn(kind, shifts, x_ref[...], w_ref[...])

    return _call(body, name=f"gprep_fwd{kind}", out_shape=_sds((GH, T, HD)), grid=(GH,),
                 in_specs=[pl.BlockSpec((T, HD), lambda h: (0, cb + h)),
                           pl.BlockSpec((3, HD), lambda h: (0, kind * GH + h))],
                 out_specs=pl.BlockSpec((None, T, HD), lambda h: (h, 0, 0)), sem=("parallel",))(proj, conv_w)


def _gprep_bwd(proj, conv_w, kind, bounds, dy, dproj):
    T = proj.shape[0]
    shifts = _make_shift(bounds)
    cb = C_QKV // HD + kind * GH

    def body(x_ref, w_ref, dy_ref, _, dx_ref, dw_ref):
        _, vjp = jax.vjp(functools.partial(_gprep_fn, kind, shifts), x_ref[...], w_ref[...])
        dx, dw = vjp(dy_ref[0] + dy_ref[1])
        dx_ref[...] = dx.astype(BF16)
        dw_ref[...] = dw

    return _call(body, name=f"gprep_bwd{kind}", out_shape=(_sds(dproj.shape, BF16), _sds((3, GH * HD))), grid=(GH,),
                 in_specs=[pl.BlockSpec((T, HD), lambda h: (0, cb + h)),
                           pl.BlockSpec((3, HD), lambda h: (0, kind * GH + h)),
                           pl.BlockSpec((2, None, T, HD), lambda h: (0, h, 0, 0)), ANYSPEC],
                 out_specs=(pl.BlockSpec((T, HD), lambda h: (0, cb + h)), pl.BlockSpec((3, HD), lambda h: (0, h))),
                 aliases={3: 0}, sem=("parallel",))(proj, conv_w, dy, dproj)


def _bl_fn(x, alog, dtb):
    lane = lax.broadcasted_iota(jnp.int32, x.shape, 1)
    beta = jax.nn.sigmoid(x)
    z = x + dtb
    sp = jnp.maximum(z, 0.0) + jnp.log1p(jnp.exp(-jnp.abs(z)))
    la = -jnp.exp(alog) * sp
    return jnp.where(lane < 2 * GH, beta, jnp.where(lane < 4 * GH, la, 0.0))


def _bl_fwd(proj, alog, dtb, *, br=256):
    T = proj.shape[0]

    def body(x_ref, a_ref, d_ref, o_ref):
        o_ref[...] = _bl_fn(x_ref[...], a_ref[...], d_ref[...])

    vec = pl.BlockSpec((1, HD), lambda i: (0, 0))
    return _call(body, name="bl_fwd", out_shape=_sds((T, HD)), grid=(T // br,),
                 in_specs=[pl.BlockSpec((br, HD), lambda i: (i, C_BL // HD)), vec, vec],
                 out_specs=pl.BlockSpec((br, HD), lambda i: (i, 0)), sem=("parallel",))(proj, alog, dtb)


def _bl_bwd(proj, alog, dtb, dbl, dproj, *, br=256):
    T = proj.shape[0]
    wide = C_Z - C_BL

    def body(x_ref, a_ref, d_ref, g_ref, _, dx_ref, da_ref, dd_ref):
        g = g_ref[0, 0]
        for d in range(2):
            for h in range(GH):
                if d or h:
                    g = g + g_ref[d, h]
        _, vjp = jax.vjp(_bl_fn, x_ref[...], a_ref[...], d_ref[...])
        dx, da, dd = vjp(g)
        dx_ref[:, :HD] = dx.astype(BF16)
        dx_ref[:, HD:] = jnp.zeros((br, wide - HD), BF16)

        @pl.when(pl.program_id(0) == 0)
        def _():
            da_ref[...] = jnp.zeros_like(da_ref)
            dd_ref[...] = jnp.zeros_like(dd_ref)

        da_ref[...] += da
        dd_ref[...] += dd

    vec = pl.BlockSpec((1, HD), lambda i: (0, 0))
    return _call(body, name="bl_bwd", out_shape=(_sds(dproj.shape, BF16), _sds((1, HD)), _sds((1, HD))), grid=(T // br,),
                 in_specs=[pl.BlockSpec((br, HD), lambda i: (i, C_BL // HD)), vec, vec,
                           pl.BlockSpec((2, GH, br, HD), lambda i: (0, 0, i, 0)), ANYSPEC],
                 out_specs=(pl.BlockSpec((br, wide), lambda i: (i, C_BL // wide)), vec, vec), aliases={4: 0},
                 sem=("arbitrary",))(proj, alog, dtb, dbl, dproj)


def _chunk_masks(d):
    ii = lax.broadcasted_iota(jnp.int32, (CH, CH), 0)
    jj = lax.broadcasted_iota(jnp.int32, (CH, CH), 1)
    eye = (ii == jj).astype(F32)
    before = jnp.where(d == 0, (jj < ii).astype(F32), (jj > ii).astype(F32))
    return before, before + eye, eye


def _same_block(b):
    ii = lax.broadcasted_iota(jnp.int32, (CH, CH), 0)
    jj = lax.broadcasted_iota(jnp.int32, (CH, CH), 1)
    shift = b.bit_length() - 1
    return (jnp.right_shift(ii, shift) == jnp.right_shift(jj, shift)).astype(F32)


def _intra_fn(masks, sel_b, sel_l, qs, ks, vs, bls, xs=None):
    before, ateq, eye = masks
    inc = ateq > 0.0
    each = lambda f, *ls: [f(*t) for t in zip(*ls)]
    beta = each(lambda bl: jnp.sum(bl * sel_b, axis=-1, keepdims=True), bls)
    la = each(lambda bl: jnp.sum(bl * sel_l, axis=-1, keepdims=True), bls)
    gam = each(lambda a: _mask_nn(ateq, jnp.broadcast_to(a, (CH, HD))), la)
    gi = each(lambda g: g[:, :CH], gam)
    gj = each(lambda g: jnp.transpose(g)[:CH, :], gam)
    kq = each(lambda k, q: _nt(jnp.concatenate([k, q], axis=0), k), ks, qs)
    kk = each(lambda t: t[:CH], kq)
    qk = each(lambda t: t[CH:], kq)
    dec = each(lambda a, b: jnp.where(inc, jnp.exp(jnp.where(inc, a - b, 0.0)), 0.0), gi, gj)
    lmat = each(lambda b, d, m: before * (b * d * m), beta, dec, kk)
    if xs is None:
        same = lambda b: _same_block(b)
        l8 = each(lambda m: m * same(8), lmat)
        x = each(lambda m: eye - m, l8)
        p2 = each(lambda m: _mdot(m, m), l8)
        y = each(lambda a, b: _mdot(jnp.concatenate([a, b], axis=0), b), x, p2)
        x = each(lambda a, t: a + t[:CH], x, y)
        x = each(lambda a, t: a + _mdot(a, t[CH:]), x, y)
        for b in (8, 16, 32):
            below = same(2 * b) - same(b)
            x = each(lambda a, m: a - _mdot(a, _mdot(m * below, a)), x, lmat)
    else:
        x = each(_saved_inverse, lmat, xs)
    eg = each(jnp.exp, gam)
    uw = each(lambda a, b, v, e, k: _mdot(a, jnp.concatenate([b * v, (b * e) * k], axis=1)), x, beta, vs, eg, ks)
    u = each(lambda t: t[:, :HD], uw)
    w = each(lambda t: t[:, HD:], uw)
    tot = each(lambda a: jnp.sum(a, axis=0, keepdims=True), la)
    kd = each(lambda k, t, g: k * jnp.exp(t - g), ks, tot, gam)
    gl = each(lambda t: jnp.broadcast_to(jnp.exp(t), (1, HD)), tot)
    qd = each(lambda q, e: q * e, qs, eg)
    p = each(lambda d, m: d * m, dec, qk)
    return (u, w, kd, qd, p, gl, x) if xs is None else (u, w, kd, qd, p, gl)


def _dir_head_sel(d, h):
    lane = lax.broadcasted_iota(jnp.int32, (1, HD), 1)
    return (lane == d * GH + h).astype(F32), (lane == 2 * GH + d * GH + h).astype(F32)


def _intra_specs(T, G):
    nc = T // CH
    assert nc % G == 0
    qkv = pl.BlockSpec((None, G * CH, HD), lambda d, h, c: (h, c, 0))
    bl = pl.BlockSpec((G * CH, HD), lambda d, h, c: (c, 0))
    big = pl.BlockSpec((None, None, G * CH, HD), lambda d, h, c: (d, h, c, 0))
    pm = pl.BlockSpec((None, None, G * CH, CH), lambda d, h, c: (d, h, c, 0))
    gl = pl.BlockSpec((None, None, G, 1, HD), lambda d, h, c: (d, h, c, 0, 0))
    shapes = (_sds((2, GH, T, HD)),) + (_sds((2, GH, T, HD), BF16),) * 3 + (
        _sds((2, GH, T, CH), BF16), _sds((2, GH, nc, 1, HD)), _sds((2, GH, T, CH)))
    return nc, qkv, bl, big, pm, gl, shapes


def _chunks_per_step(T, most):
    nc = T // CH
    return max(g for g in range(1, most + 1) if nc % g == 0)


def _chunk_at(g, d, nc, ncc):
    pos = _visit_pos(g, d, nc, ncc)
    return pos, pl.ds(pl.multiple_of(pos * CH, CH), CH)


def _intra_fwd(q, k, v, bl, L, exch):
    T = q.shape[1]
    G = _chunks_per_step(T, INTRA_FWD_CHUNKS)
    nc, qkv_s, bl_s, big, pm, gl_s, shapes = _intra_specs(T, G)
    assert G == nc
    ncc = L // CH

    def body(q_ref, k_ref, v_ref, bl_ref, u_ref, w_ref, kd_ref, qd_ref, p_ref, gl_ref, x_ref):
        d, h = pl.program_id(0), pl.program_id(1)
        sb, sl = _dir_head_sel(d, h)
        rows = [slice(g * CH, (g + 1) * CH) for g in range(G)]
        outs = _intra_fn(_chunk_masks(d), sb, sl, *[[r[s, :] for s in rows] for r in (q_ref, k_ref, v_ref, bl_ref)])
        for g in range(G):
            pos, at = _chunk_at(g, d, nc, ncc)
            for r, o in zip((u_ref, w_ref, kd_ref, qd_ref, p_ref, x_ref), outs[:5] + outs[6:]):
                r[at, :] = o[g].astype(r.dtype)
            gl_ref[pos] = outs[5][g]

    return _call_carrying(body, exch, name="gdn_intra_fwd", out_shape=shapes, grid=(2, GH, nc // G),
                          in_specs=[qkv_s, qkv_s, qkv_s, bl_s], out_specs=(big, big, big, big, pm, gl_s, pm))(q, k, v, bl)


def _intra_bwd(q, k, v, bl, xinv, cts, L, exch):
    T = q.shape[1]
    G = _chunks_per_step(T, INTRA_BWD_CHUNKS)
    nc, qkv_s, bl_s, big, pm, gl_s, _ = _intra_specs(T, G)
    assert G == nc
    ncc = L // CH

    def body(q_ref, k_ref, v_ref, bl_ref, x_ref, du, dw, dkd, dqd, dp, dgl, dq_ref, dk_ref, dv_ref, dbl_ref):
        d, h = pl.program_id(0), pl.program_id(1)
        sb, sl = _dir_head_sel(d, h)
        rows = [slice(g * CH, (g + 1) * CH) for g in range(G)]
        places = [_chunk_at(g, d, nc, ncc) for g in range(G)]
        fn = functools.partial(_intra_fn, _chunk_masks(d), sb, sl, xs=[x_ref[at, :] for _, at in places])
        _, vjp = jax.vjp(fn, *[[r[s, :] for s in rows] for r in (q_ref, k_ref, v_ref, bl_ref)])
        cts = tuple([r[at, :] for _, at in places] for r in (du, dw, dkd, dqd, dp)) + ([dgl[pos] for pos, _ in places],)
        grads = vjp(cts)
        for g in range(G):
            for r, o in zip((dq_ref, dk_ref, dv_ref, dbl_ref), grads):
                r[rows[g], :] = o[g]

    return _call_carrying(body, exch, name="gdn_intra_bwd", out_shape=(_sds((2, GH, T, HD)),) * 4,
                          grid=(2, GH, nc // G), in_specs=[qkv_s, qkv_s, qkv_s, bl_s, pm, big, big, big, big, pm, gl_s],
                          out_specs=(big,) * 4)(q, k, v, bl, xinv, *cts)


def _scan_fn(s, u, w, kd, qd, p, gl):
    each = lambda f, *ls: [f(*t) for t in zip(*ls)]
    ws = each(_nn, w, s)
    delta = each(lambda a, b: a - b, u, ws)
    kdd = each(_tn, kd, delta)
    s_new = each(lambda g, a, b: g * a + b, gl, s, kdd)
    qs = each(_nn, qd, s)
    pd = each(_nn, p, delta)
    return each(lambda a, b: a + b, qs, pd), s_new


SCAN_BLOCK = 4


def _visit_pos(c, d, nc, ncc):
    back = ncc - 1 - c if c < ncc else ncc + (nc - 1 - c)
    return jnp.where(d == 0, c, back)


def _scan_specs(T, L, back):
    tb = SCAN_BLOCK * CH
    assert T % tb == 0 and L % tb == 0
    nb, ncb = T // tb, L // tb
    at = (lambda t: nb - 1 - t) if back else (lambda t: t)
    big = pl.BlockSpec((2, GH, tb, HD), lambda t: (0, 0, at(t), 0))
    pm = pl.BlockSpec((2, GH, tb, CH), lambda t: (0, 0, at(t), 0))
    gl = pl.BlockSpec((2, GH, SCAN_BLOCK, 1, HD), lambda t: (0, 0, at(t), 0, 0))
    st = pl.BlockSpec((2, GH, SCAN_BLOCK, HD, HD), lambda t: (0, 0, at(t), 0, 0))

    def natural(b):
        return jnp.where(b < ncb, ncb - 1 - b, nb - 1 - (b - ncb))

    do_specs = (pl.BlockSpec((GH, tb, HD), lambda t: (0, at(t), 0)),
                pl.BlockSpec((GH, tb, HD), lambda t: (0, natural(at(t)), 0)))
    return nb, big, pm, gl, st, do_specs


SCAN_STREAMS = [(d, h) for d in (0, 1) for h in range(GH)]


def _scan_fwd(u, w, kd, qd, p, gl, L):
    T = u.shape[2]
    nb, big, pm, gl_s, st, _ = _scan_specs(T, L, False)

    def body(u_ref, w_ref, kd_ref, qd_ref, p_ref, gl_ref, o_ref, st_ref, s_scr):
        @pl.when(pl.program_id(0) == 0)
        def _():
            s_scr[...] = jnp.zeros_like(s_scr)

        s = [s_scr[d, h] for d, h in SCAN_STREAMS]
        for i in range(SCAN_BLOCK):
            rows = slice(i * CH, (i + 1) * CH)
            for (d, h), sv in zip(SCAN_STREAMS, s):
                st_ref[d, h, i] = sv
            o, s = _scan_fn(s, *[[r[d, h, rows, :].astype(F32) for d, h in SCAN_STREAMS]
                                 for r in (u_ref, w_ref, kd_ref, qd_ref, p_ref)],
                            [gl_ref[d, h, i] for d, h in SCAN_STREAMS])
            for (d, h), ov in zip(SCAN_STREAMS, o):
                o_ref[d, h, rows, :] = ov
        for (d, h), sv in zip(SCAN_STREAMS, s):
            s_scr[d, h] = sv

    return _call(body, name="gdn_scan_fwd", out_shape=(_sds((2, GH, T, HD)), _sds((2, GH, T // CH, HD, HD))),
                 grid=(nb,), in_specs=[big, big, big, big, pm, gl_s], out_specs=(big, st),
                 scratch=[pltpu.VMEM((2, GH, HD, HD), F32)], sem=("arbitrary",), vmem=VMEM_BIG)(u, w, kd, qd, p, gl)


def _scan_bwd(u, w, kd, qd, p, gl, states, do, L, exch):
    T = u.shape[2]
    nb, big, pm, gl_s, st, do_specs = _scan_specs(T, L, True)

    def body(u_ref, w_ref, kd_ref, qd_ref, p_ref, gl_ref, st_ref, do0_ref, do1_ref,
             du_ref, dw_ref, dkd_ref, dqd_ref, dp_ref, dgl_ref, ds_scr):
        @pl.when(pl.program_id(0) == 0)
        def _():
            ds_scr[...] = jnp.zeros_like(ds_scr)

        ds = [ds_scr[d, h] for d, h in SCAN_STREAMS]
        for i in reversed(range(SCAN_BLOCK)):
            rows = slice(i * CH, (i + 1) * CH)
            mirror = slice((SCAN_BLOCK - 1 - i) * CH, (SCAN_BLOCK - i) * CH)
            _, vjp = jax.vjp(_scan_fn, [st_ref[d, h, i] for d, h in SCAN_STREAMS],
                             *[[r[d, h, rows, :].astype(F32) for d, h in SCAN_STREAMS]
                               for r in (u_ref, w_ref, kd_ref, qd_ref, p_ref)],
                             [gl_ref[d, h, i] for d, h in SCAN_STREAMS])
            dos = [do0_ref[h, rows, :] if d == 0 else do1_ref[h, mirror, :] for d, h in SCAN_STREAMS]
            ds, gu, gw, gkd, gqd, gp, ggl = vjp((dos, ds))
            for n, (d, h) in enumerate(SCAN_STREAMS):
                du_ref[d, h, rows, :] = gu[n]
                dw_ref[d, h, rows, :] = gw[n]
                dkd_ref[d, h, rows, :] = gkd[n]
                dqd_ref[d, h, rows, :] = gqd[n]
                dp_ref[d, h, rows, :] = gp[n]
                dgl_ref[d, h, i] = ggl[n]
        for (d, h), dv in zip(SCAN_STREAMS, ds):
            ds_scr[d, h] = dv

    return _call_carrying(
        body, exch, name="gdn_scan_bwd",
        out_shape=(_sds((2, GH, T, HD)),) * 4 + (_sds((2, GH, T, CH)), _sds((2, GH, T // CH, 1, HD))),
        grid=(nb,), in_specs=[big, big, big, big, pm, gl_s, st, *do_specs], out_specs=(big, big, big, big, pm, gl_s),
        scratch=[pltpu.VMEM((2, GH, HD, HD), F32)], vmem=VMEM_BIG)(u, w, kd, qd, p, gl, states, do, do)


def _gout_fn(o0, o1, z, gw):
    return _rms(o0 + o1) * gw * _silu(z)


def _backward_latent(o_ref, L):
    nl = (o_ref.shape[1] - L) // CH
    return jnp.concatenate([o_ref[1, L + (nl - 1 - j) * CH:L + (nl - j) * CH, :] for j in range(nl)], axis=0)


def _gout_fwd(o, proj, gw, L):
    T = o.shape[2]
    N = T - L
    ob = pl.BlockSpec((2, None, T, HD), lambda h: (0, h, 0, 0))

    def body(o_ref, z_ref, gw_ref, y_ref):
        y_ref[...] = _gout_fn(o_ref[0, L:, :], _backward_latent(o_ref, L), z_ref[L:, :], gw_ref[...]).astype(BF16)

    return _call(body, name="gout_fwd", out_shape=_sds((N, GH * HD), BF16), grid=(GH,),
                 in_specs=[ob, pl.BlockSpec((T, HD), lambda h: (0, C_Z // HD + h)), pl.BlockSpec((1, HD), lambda h: (0, 0))],
                 out_specs=pl.BlockSpec((N, HD), lambda h: (0, h)), sem=("parallel",))(o, proj, gw)


def _gout_bwd(o, proj, gw, dy, dproj, L):
    T = o.shape[2]
    N = T - L
    ob = pl.BlockSpec((2, None, T, HD), lambda h: (0, h, 0, 0))

    def body(o_ref, z_ref, gw_ref, dy_ref, _, do_ref, dz_ref, dgw_ref):
        _, vjp = jax.vjp(_gout_fn, o_ref[0, L:, :], _backward_latent(o_ref, L), z_ref[L:, :], gw_ref[...])
        g0, _, gz, ggw = vjp(dy_ref[...])
        do_ref[:L, :] = jnp.zeros((L, HD), F32)
        do_ref[L:, :] = g0
        dz_ref[:L, :] = jnp.zeros((L, HD), BF16)
        dz_ref[L:, :] = gz.astype(BF16)

        @pl.when(pl.program_id(0) == 0)
        def _():
            dgw_ref[...] = jnp.zeros_like(dgw_ref)

        dgw_ref[...] += ggw

    zb = pl.BlockSpec((T, HD), lambda h: (0, C_Z // HD + h))
    return _call(body, name="gout_bwd", out_shape=(_sds((GH, T, HD)), _sds(dproj.shape, BF16), _sds((1, HD))),
                 grid=(GH,),
                 in_specs=[ob, zb, pl.BlockSpec((1, HD), lambda h: (0, 0)), pl.BlockSpec((N, HD), lambda h: (0, h)), ANYSPEC],
                 out_specs=(pl.BlockSpec((None, T, HD), lambda h: (h, 0, 0)), zb, pl.BlockSpec((1, HD), lambda h: (0, 0))),
                 aliases={4: 1}, sem=("arbitrary",))(o, proj, gw, dy, dproj)


def _merge_fn(pa, pd, ga, gd):
    return jax.nn.sigmoid(ga) * pa + jax.nn.sigmoid(gd) * pd


def _merge_fwd(pa, pd, proj, L, *, br=256):
    N = pa.shape[0]
    lb = L // br
    row = pl.BlockSpec((br, D), lambda i: (i, 0))

    def body(pa_ref, pd_ref, ga_ref, gd_ref, y_ref):
        y_ref[...] = _merge_fn(pa_ref[...], pd_ref[...], ga_ref[...], gd_ref[...]).astype(BF16)

    return _call(body, name="merge_fwd", out_shape=_sds((N, D), BF16), grid=(N // br,),
                 in_specs=[row, row, pl.BlockSpec((br, D), lambda i: (i + lb, C_GATE // D)),
                           pl.BlockSpec((br, D), lambda i: (i + lb, C_GATE // D + 1))],
                 out_specs=row, sem=("parallel",))(pa, pd, proj, proj)


def _merge_bwd(pa, pd, proj, dy, L, *, br=256):
    N = pa.shape[0]
    T = N + L
    lb = L // br
    lrow = pl.BlockSpec((br, D), lambda i: (jnp.maximum(i - lb, 0), 0))

    def body(pa_ref, pd_ref, ga_ref, gd_ref, dy_ref, dpa_ref, dpd_ref, dg_ref):
        lat = pl.program_id(0) >= lb
        _, vjp = jax.vjp(_merge_fn, pa_ref[...], pd_ref[...], ga_ref[...], gd_ref[...])
        gpa, gpd, gga, ggd = vjp(dy_ref[...])
        dpa_ref[...] = gpa.astype(BF16)
        dpd_ref[...] = gpd.astype(BF16)
        dg_ref[:, :D] = jnp.where(lat, gga, 0.0).astype(BF16)
        dg_ref[:, D:] = jnp.where(lat, ggd, 0.0).astype(BF16)

    return _call(body, name="merge_bwd", out_shape=(_sds((N, D), BF16), _sds((N, D), BF16), _sds((T, C_END), BF16)),
                 grid=(T // br,),
                 in_specs=[lrow, lrow, pl.BlockSpec((br, D), lambda i: (i, C_GATE // D)),
                           pl.BlockSpec((br, D), lambda i: (i, C_GATE // D + 1)), lrow],
                 out_specs=(lrow, lrow, pl.BlockSpec((br, 2 * D), lambda i: (i, C_GATE // (2 * D)))),
                 sem=("arbitrary",))(pa, pd, proj, proj, dy)


def _resid_fwd(x, m, mod, i_g, *, name, br=256):
    R = x.shape[0]
    row = pl.BlockSpec((br, D), lambda i: (i, 0))

    def body(x_ref, m_ref, mod_ref, o_ref):
        o_ref[...] = x_ref[...] + mod_ref[i_g:i_g + 1, :] * m_ref[...]

    return _call(body, name=name, out_shape=_sds((R, D)), grid=(R // br,),
                 in_specs=[row, row, pl.BlockSpec((6, D), lambda i: (0, 0))], out_specs=row,
                 sem=("parallel",))(x, m, mod)


def _resid_bwd(dx, m, mod, i_g, *, name, br=256):
    R = dx.shape[0]
    row = pl.BlockSpec((br, D), lambda i: (i, 0))
    vec = pl.BlockSpec((1, D), lambda i: (0, 0))

    def body(dx_ref, m_ref, mod_ref, dm_ref, dg_ref):
        dxv = dx_ref[...]
        dm_ref[...] = (dxv * mod_ref[i_g:i_g + 1, :]).astype(BF16)

        @pl.when(pl.program_id(0) == 0)
        def _():
            dg_ref[...] = jnp.zeros_like(dg_ref)

        dg_ref[...] += jnp.sum(dxv * m_ref[...], axis=0, keepdims=True)

    return _call(body, name=name, out_shape=(_sds((R, D), BF16), _sds((1, D))), grid=(R // br,),
                 in_specs=[row, row, pl.BlockSpec((6, D), lambda i: (0, 0))], out_specs=(row, vec),
                 sem=("arbitrary",))(dx, m, mod)


def _ffn_fn(shifts, ug, uv, wg, wv, bg, bv):
    down, up = shifts

    def conv(x, w, b):
        return down(x) * w[0:1, :] + x * w[1:2, :] + up(x) * w[2:3, :] + b

    return _silu(conv(ug, wg, bg)) * conv(uv, wv, bv)


def _ffn_fwd(up, cw, cb, *, bw=256):
    N = up.shape[0]
    shifts = _make_shift(((0, N),))
    nb = DFF // bw

    def body(ug, uv, wg, wv, bg, bv, a_ref):
        a_ref[...] = _ffn_fn(shifts, ug[...], uv[...], wg[...], wv[...], bg[...], bv[...]).astype(BF16)

    def col(rows, off):
        return pl.BlockSpec((rows, bw), lambda j: (0, j + off))

    return _call(body, name="ffn_fwd", out_shape=_sds((N, DFF), BF16), grid=(nb,),
                 in_specs=[col(N, 0), col(N, nb), col(3, 0), col(3, nb), col(1, 0), col(1, nb)],
                 out_specs=col(N, 0), sem=("parallel",), vmem=VMEM_BIG)(up, up, cw, cw, cb, cb)


def _ffn_bwd(up, cw, cb, da, *, bw=256):
    N = up.shape[0]
    shifts = _make_shift(((0, N),))
    nb = DFF // bw

    def body(ug, uv, wg, wv, bg, bv, da_ref, dug, duv, dwg, dwv, dbg, dbv):
        _, vjp = jax.vjp(functools.partial(_ffn_fn, shifts), ug[...], uv[...], wg[...], wv[...], bg[...], bv[...])
        g = vjp(da_ref[...])
        dug[...] = g[0].astype(BF16)
        duv[...] = g[1].astype(BF16)
        dwg[...], dwv[...], dbg[...], dbv[...] = g[2], g[3], g[4], g[5]

    def col(rows, off):
        return pl.BlockSpec((rows, bw), lambda j: (0, j + off))

    half = (_sds((N, DFF), BF16), _sds((N, DFF), BF16), _sds((3, DFF)), _sds((3, DFF)), _sds((1, DFF)), _sds((1, DFF)))
    dug, duv, dwg, dwv, dbg, dbv = _call(
        body, name="ffn_bwd", out_shape=half, grid=(nb,),
        in_specs=[col(N, 0), col(N, nb), col(3, 0), col(3, nb), col(1, 0), col(1, nb), col(N, 0)],
        out_specs=(col(N, 0), col(N, 0), col(3, 0), col(3, 0), col(1, 0), col(1, 0)),
        sem=("parallel",), vmem=VMEM_BIG)(up, up, cw, cw, cb, cb, da)
    return (jnp.concatenate([dug, duv], axis=1), jnp.concatenate([dwg, dwv], axis=1),
            jnp.concatenate([dbg, dbv], axis=1))


def _head_fn(x1, dn, g2, fw, tgt):
    y = _rms(x1 + g2 * dn) * fw
    err = y - tgt
    return 0.5 * jnp.sum(jnp.mean(err * err, axis=-1))


def _head(x1, dn, mod, fw, tgt, *, br=256):
    N = x1.shape[0]
    row = pl.BlockSpec((br, D), lambda i: (i, 0))
    vec = pl.BlockSpec((1, D), lambda i: (0, 0))
    one = pl.BlockSpec((1, HD), lambda i: (0, 0))

    def body(x1_ref, dn_ref, mod_ref, fw_ref, tgt_ref, loss_ref, dx_ref, ddn_ref, dg_ref, dfw_ref):
        loss, (gx, gdn, gg, gfw) = jax.value_and_grad(_head_fn, argnums=(0, 1, 2, 3))(
            x1_ref[...], dn_ref[...], mod_ref[5:6, :], fw_ref[...], tgt_ref[...])
        dx_ref[...] = gx
        ddn_ref[...] = gdn.astype(BF16)

        @pl.when(pl.program_id(0) == 0)
        def _():
            loss_ref[...] = jnp.zeros_like(loss_ref)
            dg_ref[...] = jnp.zeros_like(dg_ref)
            dfw_ref[...] = jnp.zeros_like(dfw_ref)

        loss_ref[...] += jnp.broadcast_to(loss, (1, HD))
        dg_ref[...] += gg
        dfw_ref[...] += gfw

    return _call(body, name="head", out_shape=(_sds((1, HD)), _sds((N, D)), _sds((N, D), BF16), _sds((1, D)), _sds((1, D))),
                 grid=(N // br,), in_specs=[row, row, pl.BlockSpec((6, D), lambda i: (0, 0)), vec, row],
                 out_specs=(one, row, row, vec, vec), sem=("arbitrary",))(x1, dn, mod, fw, tgt)


def _adamw(w, g, m, v, *, name):
    shape = w.shape
    cols = shape[-1]
    rows = max(1, math.prod(shape[:-1]))
    w2, g2, m2, v2 = (t.reshape(rows, cols) for t in (w, g, m, v))
    br = 256 if rows % 256 == 0 else rows
    c1 = 1.0 - B1 ** STEP
    c2 = 1.0 - B2 ** STEP

    def body(w_ref, g_ref, m_ref, v_ref, d_ref, nm_ref, nv_ref):
        gv = g_ref[...]
        nm = B1 * m_ref[...] + (1.0 - B1) * gv
        nv = B2 * v_ref[...] + (1.0 - B2) * (gv * gv)
        d_ref[...] = -LR * ((nm / c1) / (jnp.sqrt(nv / c2) + AEPS) + WD * w_ref[...])
        nm_ref[...] = nm
        nv_ref[...] = nv

    blk = pl.BlockSpec((br, cols), lambda i: (i, 0))
    outs = _call(body, name=name, out_shape=(_sds((rows, cols)),) * 3, grid=(rows // br,),
                 in_specs=[blk] * 4, out_specs=(blk,) * 3, sem=("parallel",))(w2, g2, m2, v2)
    return tuple(t.reshape(shape) for t in outs)


def _adamw_many(items, *, name):
    k = len(items)
    shapes = [w.shape for w, _, _, _ in items]
    flat = [t.reshape(max(1, math.prod(t.shape[:-1])), t.shape[-1]) for it in items for t in it]
    c1 = 1.0 - B1 ** STEP
    c2 = 1.0 - B2 ** STEP

    def body(*refs):
        ins, outs = refs[:4 * k], refs[4 * k:]
        for i in range(k):
            w_ref, g_ref, m_ref, v_ref = ins[4 * i:4 * i + 4]
            gv = g_ref[...]
            nm = B1 * m_ref[...] + (1.0 - B1) * gv
            nv = B2 * v_ref[...] + (1.0 - B2) * (gv * gv)
            outs[3 * i][...] = -LR * ((nm / c1) / (jnp.sqrt(nv / c2) + AEPS) + WD * w_ref[...])
            outs[3 * i + 1][...] = nm
            outs[3 * i + 2][...] = nv

    res = _call(body, name=name, out_shape=tuple(_sds(flat[4 * i].shape) for i in range(k) for _ in range(3)))(*flat)
    return [tuple(res[3 * i + j].reshape(shapes[i]) for j in range(3)) for i in range(k)]


def _rope_tables(N, L):
    t = jnp.arange(N)
    pos = jnp.stack([(t // GRID_W).astype(F32), (t % GRID_W).astype(F32)], axis=1)
    inv = ROPE_THETA ** (-jnp.arange(0, HD // 2, 2, dtype=F32) / (HD // 2))
    ang = pos[:, :, None] * inv[None, None, :]
    cos = jnp.broadcast_to(jnp.cos(ang)[:, :, None, :], (N, 2, 2, HD // 4)).reshape(N, HD)
    sin = jnp.broadcast_to(jnp.sin(ang)[:, :, None, :], (N, 2, 2, HD // 4))
    sin = (sin * jnp.array([-1.0, 1.0], F32)[None, None, :, None]).reshape(N, HD)
    cos = jnp.concatenate([jnp.ones((L, HD), F32), cos], axis=0)
    sin = jnp.concatenate([jnp.zeros((L, HD), F32), sin], axis=0)
    return cos, sin


def _pad_lanes(v, off=0):
    return jnp.zeros((1, HD), F32).at[0, off:off + v.shape[0]].set(v)


def _local_step(x, ctx, tgt, mod_lat, mod_ctx, w_in, shards, small):
    N, L = x.shape[0], ctx.shape[0]
    T = N + L
    bounds = ((0, L), (L, T))
    qw, kw, gw = small["q_norm_w"], small["k_norm_w"], small["gdn_norm_w"]
    conv_w, ffn_w, ffn_b, fnw = small["conv_qkv_w"], small["ffn_conv_w"], small["ffn_conv_b"], small["final_norm_w"]
    alog = _pad_lanes(small["a_log"].reshape(-1), 2 * GH)
    dtb = _pad_lanes(small["dt_bias"].reshape(-1), 2 * GH)
    cos, sin = _rope_tables(N, L)
    bt = T
    bnl = 256 if N % 1024 else 1024
    bsq = 256 if N % 512 else 512

    hc = _normmod_fwd(ctx, mod_ctx, 0, 1, name="normmod_ctx")
    hx = _normmod_fwd(x, mod_lat, 0, 1, name="normmod_x")
    h1 = jnp.concatenate([hc, hx], axis=0)
    proj = _mm(h1, w_in, name="mm_in", M=T, N=C_END, K=D, tb=True, bm=bt, bn=1024)
    aq, ak, av = _aprep_fwd(proj, cos, sin, qw, kw)
    (attn, attn32, lse), (up_g,) = _attn_fwd(aq, ak, av, L, _GatherTwoLevel([shards["w_up"]]))
    gq = _gprep_fwd(proj, conv_w, 0, bounds)
    gk = _gprep_fwd(proj, conv_w, 1, bounds)
    gv = _gprep_fwd(proj, conv_w, 2, bounds)
    bl = _bl_fwd(proj, alog, dtb)
    intra, (down_g, pa_g, pd_g, out_g) = _intra_fwd(
        gq, gk, gv, bl, L, _GatherTwoLevel([shards[n] for n in ("w_down", "w_pa", "w_pd", "w_out")]))
    w_up, w_down = up_g.reshape(2 * DFF, D), down_g.reshape(DFF, D)
    w_pa, w_pd, w_out = pa_g.reshape(D, D), pd_g.reshape(D, D), out_g.reshape(D, D)
    xinv, intra = intra[6], intra[:6]
    o, states = _scan_fwd(*intra, L)
    gdn = _gout_fwd(o, proj, gw, L)
    pa = _mm(attn, w_pa, name="mm_pa", M=N, N=D, K=D, bm=bsq)
    pd = _mm(gdn, w_pd, name="mm_pd", M=N, N=D, K=D, bm=bsq)
    y = _merge_fwd(pa, pd, proj, L)
    m = _mm(y, w_out, name="mm_out", M=N, N=D, K=D, bm=bsq)
    x1 = _resid_fwd(x, m, mod_lat, 2, name="resid1")
    h2 = _normmod_fwd(x1, mod_lat, 3, 4, name="normmod_x1")
    up = _mm(h2, w_up, name="mm_up", M=N, N=2 * DFF, K=D, tb=True, bm=bnl, bn=2 * DFF // 4)
    a = _ffn_fwd(up, ffn_w, ffn_b)
    dn = _mm(a, w_down, name="mm_down", M=N, N=D, K=DFF, bm=bnl)
    loss, dx2, ddn, dg2, dfnw = _head(x1, dn, mod_lat, fnw, tgt)

    da = _mm(ddn, w_down, name="mm_down_dx", M=N, N=DFF, K=D, tb=True, bm=bnl, bn=DFF // 2)
    g_down = _mm(a, ddn, name="mm_down_dw", M=DFF, N=D, K=N, ta=True, bm=DFF // 2, out_dtype=BF16)
    dup, d_ffn_w, d_ffn_b = _ffn_bwd(up, ffn_w, ffn_b, da)
    dh2 = _mm(dup, w_up, name="mm_up_dx", M=N, N=D, K=2 * DFF, bm=bnl, bk=2 * DFF // 4)
    g_up = _mm(dup, h2, name="mm_up_dw", M=2 * DFF, N=D, K=N, ta=True, bm=2 * DFF // 4, out_dtype=BF16)
    dx1, dsh2, dsc2 = _normmod_bwd(x1, mod_lat, 3, 4, dh2, 0, dx2, name="normmod_x1_bwd")
    dm, dg1 = _resid_bwd(dx1, m, mod_lat, 2, name="resid1_bwd")
    dy = _mm(dm, w_out, name="mm_out_dx", M=N, N=D, K=D, tb=True, bm=bsq)
    g_out = _mm(y, dm, name="mm_out_dw", M=D, N=D, K=N, ta=True, bm=D // 4, out_dtype=BF16)
    dpa, dpd, dproj = _merge_bwd(pa, pd, proj, dy, L)
    dattn = _mm(dpa, w_pa, name="mm_pa_dx", M=N, N=D, K=D, tb=True, bm=bsq)
    g_pa = _mm(attn, dpa, name="mm_pa_dw", M=D, N=D, K=N, ta=True, bm=D // 4, out_dtype=BF16)
    dgdn = _mm(dpd, w_pd, name="mm_pd_dx", M=N, N=D, K=D, tb=True, bm=bsq)
    g_pd = _mm(gdn, dpd, name="mm_pd_dw", M=D, N=D, K=N, ta=True, bm=D // 4, out_dtype=BF16)
    do, dproj, dgw = _gout_bwd(o, proj, gw, dgdn, dproj, L)
    cts, recv_a = _scan_bwd(*intra, states, do, L, _Exchange(
        [g_out.reshape(NDEV, D // NDEV, D), g_pa.reshape(NDEV, D // NDEV, D), g_pd.reshape(NDEV, D // NDEV, D)], True))
    (dgq, dgk, dgv, dbl), recv_b = _intra_bwd(gq, gk, gv, bl, xinv, cts, L, _Exchange(
        [g_up.reshape(NDEV, 2 * DFF // NDEV, D)], True))
    dproj, dwq = _gprep_bwd(proj, conv_w, 0, bounds, dgq, dproj)
    dproj, dwk = _gprep_bwd(proj, conv_w, 1, bounds, dgk, dproj)
    dproj, dwv = _gprep_bwd(proj, conv_w, 2, bounds, dgv, dproj)
    dproj, dalog, ddtb = _bl_bwd(proj, alog, dtb, dbl, dproj)
    (daq_h, dak_h, dav_h), recv_c = _attn_bwd(aq, ak, av, attn32, lse, dattn, L, _Exchange(
        [g_down.reshape(NDEV, DFF // NDEV, D)], True))
    recv = dict(zip(("w_out", "w_pa", "w_pd", "w_up", "w_down"), recv_a + recv_b + recv_c))
    dproj, dqw, dkw = _aprep_bwd(proj, cos, sin, qw, kw, daq_h, dak_h, dav_h, dproj, L)
    g_in = _mm(dproj, h1, name="mm_in_dw", M=C_END, N=D, K=T, ta=True, bm=1024, out_dtype=BF16)
    g_in = _unpad_columns(g_in).reshape(NDEV, W_END // NDEV, D)
    own_in = lax.dynamic_index_in_dim(g_in, _position()[3], axis=0, keepdims=False)
    *pending, token = _scatter_start(g_in, None, (0, D // 2), (), name="scatter_g_in_a_start")
    dh1 = _mm(dproj, w_in, name="mm_in_dx", M=T, N=D, K=C_END, bm=bt, bk=1024, after=(token,))
    grad_x, dsh1, dsc1 = _normmod_bwd(x, mod_lat, 0, 1, dh1, L, dx1, name="normmod_x_bwd")
    _, dcsh1, dcsc1 = _normmod_bwd(ctx, mod_ctx, 0, 1, dh1, 0, None, name="normmod_ctx_bwd")

    z1 = jnp.zeros((1, D), F32)
    dmod_lat = jnp.concatenate([dsh1, dsc1, dg1, dsh2, dsc2, dg2], axis=0)
    dmod_ctx = jnp.concatenate([dcsh1, dcsc1, z1, z1, z1, z1], axis=0)
    gsmall = {
        "q_norm_w": dqw, "k_norm_w": dkw, "gdn_norm_w": dgw,
        "conv_qkv_w": jnp.concatenate([dwq, dwk, dwv], axis=1),
        "a_log": dalog[0, 2 * GH:4 * GH], "dt_bias": ddtb[0, 2 * GH:4 * GH],
        "ffn_conv_w": d_ffn_w, "ffn_conv_b": d_ffn_b, "final_norm_w": dfnw,
    }
    return loss[0, 0], grad_x, (pending, own_in), recv, dmod_lat, dmod_ctx, gsmall


HBM = pl.BlockSpec(memory_space=pltpu.HBM)
ANYSPEC = pl.BlockSpec(memory_space=pl.ANY)


def _position():
    x, y, c = lax.axis_index("x"), lax.axis_index("y"), lax.axis_index("c")
    return x, y, c, 4 * x + 2 * y + c


def _peer(x, y, c, k):
    px = 1 - x if k & 4 else x
    py = 1 - y if k & 2 else y
    pc = 1 - c if k & 1 else c
    return (px, py, pc), 4 * px + 2 * py + pc


def _exchange(arrs, *, name, scatter):
    exch = _Exchange(arrs, scatter)
    n = exch.n

    def body(*refs):
        ins, outs, sems = refs[:n], refs[n:2 * n], refs[2 * n:]
        exch.start(ins, outs, sems)
        exch.finish(ins, outs, sems)

    outs = pl.pallas_call(body, name=name, out_shape=exch.out_shape, in_specs=[HBM] * n, out_specs=(HBM,) * n,
                          scratch_shapes=exch.scratch,
                          compiler_params=pltpu.CompilerParams(has_side_effects=True))(*arrs)
    return list(outs)


class _Exchange:
    def __init__(self, arrs, scatter):
        self.arrs, self.scatter, self.n = list(arrs), scatter, len(arrs)
        self.out_shape = tuple(_sds(a.shape if scatter else (NDEV,) + a.shape, a.dtype) for a in arrs)
        self.scratch = [pltpu.SemaphoreType.DMA((self.n, NDEV - 1)), pltpu.SemaphoreType.DMA((self.n, NDEV - 1)),
                        pltpu.SemaphoreType.DMA((self.n,))]

    def _copies(self, ins, outs, sems):
        send, recv, loc = sems
        x, y, c, me = _position()
        local = [pltpu.make_async_copy(ins[a].at[me] if self.scatter else ins[a], outs[a].at[me], loc.at[a])
                 for a in range(self.n)]
        remote = []
        for k in range(1, NDEV):
            peer, pid = _peer(x, y, c, k)
            for a in range(self.n):
                src = ins[a].at[pid] if self.scatter else ins[a]
                remote.append(pltpu.make_async_remote_copy(
                    src_ref=src, dst_ref=outs[a].at[me], send_sem=send.at[a, k - 1], recv_sem=recv.at[a, k - 1],
                    device_id=peer, device_id_type=MESH))
        return local, remote

    def start(self, ins, outs, sems):
        local, remote = self._copies(ins, outs, sems)
        for cp in local + remote:
            cp.start()

    def finish(self, ins, outs, sems):
        local, remote = self._copies(ins, outs, sems)
        for cp in remote:
            cp.wait()
        for cp in local:
            cp.wait()


class _GatherTwoLevel:
    scatter = False

    def __init__(self, arrs):
        self.arrs, self.n = list(arrs), len(arrs)
        self.out_shape = tuple(_sds((NDEV,) + a.shape, a.dtype) for a in arrs)
        self.scratch = [pltpu.SemaphoreType.DMA((self.n, NDEV - 1)), pltpu.SemaphoreType.DMA((self.n, NDEV - 1)),
                        pltpu.SemaphoreType.DMA((self.n,))]

    def _parts(self, ins, outs, sems):
        send, recv, loc = sems
        x, y, c, _ = _position()
        me, sibling = (x, y, c), (x, y, 1 - c)
        chips = [(1 - x, y), (x, 1 - y), (1 - x, 1 - y)]
        parts = []
        for a in range(self.n):
            slot = lambda px, py, pc, a=a: outs[a].at[4 * px + 2 * py + pc]

            def copy(k, owner, to, src=None, a=a, slot=slot):
                return pltpu.make_async_remote_copy(
                    src_ref=slot(*owner) if src is None else src, dst_ref=slot(*owner), send_sem=send.at[a, k],
                    recv_sem=recv.at[a, k], device_id=to, device_id_type=MESH)

            parts.append(dict(
                mine=pltpu.make_async_copy(ins[a], slot(*me), loc.at[a]),
                first=[copy(0, me, sibling, src=ins[a])] + [copy(1 + j, me, (*ch, c), src=ins[a]) for j, ch in enumerate(chips)],
                arrive=[copy(1 + j, (*ch, c), me) for j, ch in enumerate(chips)],
                passed=[copy(4 + j, (*ch, c), sibling) for j, ch in enumerate(chips)],
                rest=[copy(0, sibling, me)] + [copy(4 + j, (*ch, 1 - c), me) for j, ch in enumerate(chips)]))
        return parts

    def start(self, ins, outs, sems):
        for p in self._parts(ins, outs, sems):
            p["mine"].start()
            for cp in p["first"]:
                cp.start()

    def middle(self, ins, outs, sems):
        for p in self._parts(ins, outs, sems):
            for got, fwd in zip(p["arrive"], p["passed"]):
                got.wait_recv()
                fwd.start()

    def finish(self, ins, outs, sems):
        for p in self._parts(ins, outs, sems):
            for cp in p["rest"]:
                cp.wait_recv()
            for cp in p["first"] + p["passed"]:
                cp.wait_send()
            p["mine"].wait()


def _gather_two_level(blocks, *, name):
    exch = _GatherTwoLevel(blocks)
    n = exch.n

    def body(*refs):
        ins, outs, sems = refs[:n], refs[n:2 * n], refs[2 * n:]
        exch.start(ins, outs, sems)
        exch.middle(ins, outs, sems)
        exch.finish(ins, outs, sems)

    outs = pl.pallas_call(body, name=name, out_shape=exch.out_shape, in_specs=[HBM] * n, out_specs=(HBM,) * n,
                          scratch_shapes=exch.scratch,
                          compiler_params=pltpu.CompilerParams(has_side_effects=True))(*blocks)
    return list(outs)


SEM = pl.BlockSpec(memory_space=pltpu.SEMAPHORE)


def _scatter_copies(src_ref, land_ref, send_sems, recv_sems, cols):
    x, y, c, me = _position()
    span = (slice(None), pl.ds(*cols))
    copies = []
    for k in range(1, NDEV):
        peer, pid = _peer(x, y, c, k)
        copies.append(pltpu.make_async_remote_copy(
            src_ref=src_ref.at[pid].at[span], dst_ref=land_ref.at[me].at[span], send_sem=send_sems.at[k - 1],
            recv_sem=recv_sems.at[k - 1], device_id=peer, device_id_type=MESH))
    return copies


SPLIT_EFFECT = pltpu.SideEffectType.DATAFLOW_SIDE_EFFECTING


def _scatter_start(parts, land, cols, after, *, name):
    na = len(after)
    if land is None:
        land = lax.empty(parts.shape, parts.dtype)

    def body(src_ref, land_ref, *rest):
        send_sems, recv_sems, _, _, token = rest[na:]
        for cp in _scatter_copies(src_ref, land_ref, send_sems, recv_sems, cols):
            cp.start()
        token[...] = jnp.zeros_like(token)

    return pl.pallas_call(
        body, name=name,
        out_shape=(pltpu.SemaphoreType.DMA((NDEV - 1,)), pltpu.SemaphoreType.DMA((NDEV - 1,)),
                   pltpu.HBM(parts.shape, parts.dtype), pltpu.HBM(parts.shape, parts.dtype), _sds((8, HD))),
        in_specs=(HBM, HBM) + (pl.BlockSpec(memory_space=pl.ANY),) * na,
        out_specs=(SEM, SEM, HBM, HBM, pl.BlockSpec(memory_space=pltpu.VMEM)),
        input_output_aliases={0: 2, 1: 3}, compiler_params=pltpu.CompilerParams(has_side_effects=SPLIT_EFFECT),
    )(pltpu.with_memory_space_constraint(parts, pltpu.HBM), pltpu.with_memory_space_constraint(land, pltpu.HBM), *after)


def _scatter_wait(send_sems, recv_sems, src_thru, land_thru, cols, after, *, name):
    na = len(after)

    def body(src_ref, land_ref, send_sems, recv_sems, *rest):
        for cp in _scatter_copies(src_ref, land_ref, send_sems, recv_sems, cols):
            cp.wait_send()
            cp.wait_recv()

    return pl.pallas_call(
        body, name=name,
        out_shape=(pltpu.HBM(src_thru.shape, src_thru.dtype), pltpu.HBM(land_thru.shape, land_thru.dtype)),
        in_specs=(HBM, HBM, SEM, SEM) + (pl.BlockSpec(memory_space=pl.ANY),) * na, out_specs=(HBM, HBM),
        input_output_aliases={0: 0, 1: 1}, compiler_params=pltpu.CompilerParams(has_side_effects=SPLIT_EFFECT),
    )(src_thru, land_thru, send_sems, recv_sems, *after)


def _cast_bf16(w, *, name):
    rows, cols = w.shape
    br = 128 if rows % 128 == 0 else rows

    def body(w_ref, o_ref):
        o_ref[...] = w_ref[...].astype(BF16)

    blk = pl.BlockSpec((br, cols), lambda i: (i, 0))
    return _call(body, name=name, out_shape=_sds((rows, cols), BF16), grid=(rows // br,), in_specs=[blk],
                 out_specs=blk, sem=("parallel",))(w)


def _sum_slots(a, *, name):
    _, R, C = a.shape

    def body(a_ref, o_ref):
        s = a_ref[0]
        for d in range(1, NDEV):
            s = s + a_ref[d]
        o_ref[...] = s

    return _call(body, name=name, out_shape=_sds((R, C)))(a)


MODROWS = 16


def _mod_fwd(c9, w, b):
    cols = w.shape[1]

    def body(c_ref, w_ref, b_ref, o_ref):
        o_ref[...] = _nn(_silu(c_ref[...]), w_ref[...]) + b_ref[...]

    return _call(body, name="mod_fwd", out_shape=_sds((MODROWS, cols)))(c9, w, b)


def _mod_bwd(c9, dmy, dall, w):
    cols = w.shape[1]

    def body(c_ref, dmy_ref, dall_ref, w_ref, gw_ref, gb_ref, cp_ref):
        sc = _silu(c_ref[...])
        rows = lax.broadcasted_iota(jnp.int32, (MODROWS, 1), 0)
        d = dmy_ref[...]
        d_ctx = jnp.where(rows == NDEV, d, 0.0)
        sc_ctx = jnp.where(rows == NDEV, sc, 0.0)
        outer = lax.dot_general(sc_ctx, d_ctx, (((0,), (0,)), ((), ())), precision=HI, preferred_element_type=F32)
        gw_ref[...] = _tn(jnp.where(rows < NDEV, sc, 0.0), jnp.where(rows < NDEV, d, 0.0)) + outer
        gb_ref[...] = jnp.sum(dall_ref[...], axis=0, keepdims=True)
        cp_ref[...] = jnp.sum(_nt(d_ctx, w_ref[...]), axis=0, keepdims=True)

    return _call(body, name="mod_bwd", out_shape=(_sds((D, cols)), _sds((1, 6 * D)), _sds((1, D))),
                 vmem=VMEM_BIG)(c9, dmy, dall, w)


def _cctx_finish(parts, c_ctx, after):
    VM = pl.BlockSpec(memory_space=pltpu.VMEM)

    def body(p_ref, c_ref, *rest):
        o_ref = rest[-1]
        s = p_ref[0]
        for d in range(1, NDEV):
            s = s + p_ref[d]
        _, vjp = jax.vjp(_silu, c_ref[...])
        o_ref[...] = vjp(s)[0]

    return _call(body, name="cctx_finish", out_shape=_sds((1, D)),
                 in_specs=[VM, VM] + [pl.BlockSpec(memory_space=pl.ANY)] * len(after))(parts, c_ctx, *after)


def _adamw_recv(w, recv, m, v, *, name, own=None):
    rows, cols = w.shape
    bc = 256
    c1 = 1.0 - B1 ** STEP
    c2 = 1.0 - B2 ** STEP
    has_own = own is not None

    def body(w_ref, r_ref, m_ref, v_ref, *rest):
        g_ref, d_ref, nm_ref, nv_ref = rest[-4:]
        me = _position()[3]

        def slot(d):
            return jnp.where(me == d, rest[0][...], r_ref[d]) if has_own else r_ref[d]

        gv = slot(0).astype(F32)
        for d in range(1, NDEV):
            gv = gv + slot(d).astype(F32)
        nm = B1 * m_ref[...] + (1.0 - B1) * gv
        nv = B2 * v_ref[...] + (1.0 - B2) * (gv * gv)
        g_ref[...] = gv
        d_ref[...] = -LR * ((nm / c1) / (jnp.sqrt(nv / c2) + AEPS) + WD * w_ref[...])
        nm_ref[...] = nm
        nv_ref[...] = nv

    blk = pl.BlockSpec((rows, bc), lambda j: (0, j))
    return _call(body, name=name, out_shape=(_sds((rows, cols)),) * 4, grid=(cols // bc,),
                 in_specs=[blk, pl.BlockSpec((NDEV, rows, bc), lambda j: (0, 0, j)), blk, blk] + [blk] * has_own,
                 out_specs=(blk,) * 4, sem=("parallel",), vmem=VMEM_BIG)(w, recv, m, v, *([own] if has_own else []))


P_LAT, P_CTX, P_FNW, P_FFNB, P_CONV, P_FFNW, P_MISC, P_ROWS = 0, 8, 16, 24, 32, 48, 72, 80


def _rows_of(v, nrows):
    flat = v.reshape(-1)
    return jnp.pad(flat, (0, nrows * D - flat.shape[0])).reshape(nrows, D)


def _by_columns(g):
    n, r, c = g.shape
    return jnp.transpose(g, (1, 0, 2)).reshape(r, n * c)


def kernel(x, c, ctx, c_ctx, w_mod, b_mod, w_in, q_norm_w, k_norm_w, conv_qkv_w, a_log, dt_bias, gdn_norm_w, w_pa, w_pd, w_out, w_up, ffn_conv_w, ffn_conv_b, w_down, final_norm_w, loss_target, m_c_ctx, m_w_mod, m_b_mod, m_w_in, m_q_norm_w, m_k_norm_w, m_conv_qkv_w, m_a_log, m_dt_bias, m_gdn_norm_w, m_w_pa, m_w_pd, m_w_out, m_w_up, m_ffn_conv_w, m_ffn_conv_b, m_w_down, m_final_norm_w, v_c_ctx, v_w_mod, v_b_mod, v_w_in, v_q_norm_w, v_k_norm_w, v_conv_qkv_w, v_a_log, v_dt_bias, v_gdn_norm_w, v_w_pa, v_w_pd, v_w_out, v_w_up, v_ffn_conv_w, v_ffn_conv_b, v_w_down, v_final_norm_w):
    _, _, _, me = _position()
    mcols = w_mod.shape[2]

    transposed = ("w_in", "w_up")
    big = {"w_in": w_in[0].T, "w_pa": w_pa[0], "w_pd": w_pd[0], "w_out": w_out[0], "w_up": w_up[0].T, "w_down": w_down[0]}
    names = list(big)
    shards = {n: _cast_bf16(big[n], name="cast_" + n) for n in names}
    w_in_g, c_all, conv_g, ffnw_g = _gather_two_level([shards["w_in"], c, conv_qkv_w[0], ffn_conv_w[0]],
                                                      name="gather_w_in")
    w_in_full = w_in_g.reshape(W_END, D)
    w_in_pad = _pad_columns(w_in_full)

    c9 = jnp.concatenate([c_all.reshape(NDEV, D), jnp.pad(c_ctx[None], ((0, MODROWS - NDEV - 1), (0, 0)))], axis=0)
    b_loc = lax.dynamic_slice(b_mod, (0, me * mcols), (1, mcols))
    mod_all, = _exchange([_mod_fwd(c9, w_mod[0], b_loc)], name="gather_mod", scatter=False)
    mod_lat = lax.dynamic_index_in_dim(mod_all, me, axis=1, keepdims=False).reshape(6, D)
    mod_ctx = mod_all[:, NDEV, :].reshape(6, D)

    small = {"q_norm_w": q_norm_w, "k_norm_w": k_norm_w, "gdn_norm_w": gdn_norm_w, "a_log": a_log, "dt_bias": dt_bias,
             "conv_qkv_w": _by_columns(conv_g), "ffn_conv_w": _by_columns(ffnw_g), "ffn_conv_b": ffn_conv_b,
             "final_norm_w": final_norm_w[None]}
    loss_me, grad_x, (pending_in, own_in), recv, dmod_lat, dmod_ctx, gs = _local_step(
        x[0], ctx[0], loss_target[0], mod_lat, mod_ctx, w_in_pad, shards, small)

    moments = {"w_in": (m_w_in, v_w_in), "w_pa": (m_w_pa, v_w_pa), "w_pd": (m_w_pd, v_w_pd),
               "w_out": (m_w_out, v_w_out), "w_up": (m_w_up, v_w_up), "w_down": (m_w_down, v_w_down)}
    res = {}
    def finish(n, outs):
        return tuple((t.T if n in transposed else t)[None] for t in outs)

    def moment(t, n):
        return t[0].T if n in transposed else t[0]

    for n in recv:
        res[n] = finish(n, _adamw_recv(big[n], recv[n], moment(moments[n][0], n), moment(moments[n][1], n),
                                       name="adamw_" + n))

    misc = jnp.concatenate([gs["q_norm_w"][0], gs["k_norm_w"][0], gs["gdn_norm_w"][0], gs["a_log"], gs["dt_bias"],
                            loss_me[None]])
    pack = jnp.concatenate([_rows_of(dmod_lat, P_CTX - P_LAT), _rows_of(dmod_ctx, P_FNW - P_CTX),
                            _rows_of(gs["final_norm_w"], P_FFNB - P_FNW), _rows_of(gs["ffn_conv_b"], P_CONV - P_FFNB),
                            _rows_of(gs["conv_qkv_w"], P_FFNW - P_CONV), _rows_of(gs["ffn_conv_w"], P_MISC - P_FFNW),
                            _rows_of(misc, P_ROWS - P_MISC)], axis=0)
    pack_all, = _exchange([pack], name="gather_pack", scatter=False)
    tot = _sum_slots(pack_all, name="sum_pack")
    dall = jnp.concatenate([pack_all[:, P_LAT:P_LAT + 6, :].reshape(NDEV, 6 * D),
                            jnp.pad(tot[P_CTX:P_CTX + 6].reshape(1, 6 * D), ((0, MODROWS - NDEV - 1), (0, 0)))], axis=0)
    dmy = lax.dynamic_slice(dall, (0, me * mcols), (MODROWS, mcols))
    g_w_mod, g_b_mod, cpart = _mod_bwd(c9, dmy, dall, w_mod[0])
    cparts, = _exchange([cpart], name="gather_cctx", scatter=False)
    sems_a, land = pending_in[:2], pending_in[3]
    *sems_b, g_in_thru, land, token_b = _scatter_start(pending_in[2], land, (D // 2, D // 2), (cparts,),
                                                       name="scatter_g_in_b_start")
    g_c_ctx = _cctx_finish(cparts, c_ctx[None], (token_b,))[0]

    nconv, nffn = 3 * GH * HD, 2 * DFF
    conv_tot = tot[P_CONV:P_FFNW].reshape(-1)[:3 * nconv].reshape(3, nconv)
    ffnw_tot = tot[P_FFNW:P_MISC].reshape(-1)[:3 * nffn].reshape(3, nffn)
    mrow = tot[P_MISC]
    grads = {
        "c_ctx": g_c_ctx, "w_mod": g_w_mod[None], "b_mod": g_b_mod,
        "q_norm_w": mrow[None, 0:HD], "k_norm_w": mrow[None, HD:2 * HD], "gdn_norm_w": mrow[None, 2 * HD:3 * HD],
        "conv_qkv_w": lax.dynamic_slice(conv_tot, (0, me * (nconv // NDEV)), (3, nconv // NDEV))[None],
        "a_log": mrow[3 * HD:3 * HD + 2 * GH].reshape(1, 2, GH),
        "dt_bias": mrow[3 * HD + 2 * GH:3 * HD + 4 * GH].reshape(1, 2, GH),
        "ffn_conv_w": lax.dynamic_slice(ffnw_tot, (0, me * (nffn // NDEV)), (3, nffn // NDEV))[None],
        "ffn_conv_b": tot[P_FFNB:P_CONV].reshape(-1)[:nffn][None],
        "final_norm_w": tot[P_FNW],
    }
    loss = mrow[3 * HD + 4 * GH]
    given = {"c_ctx": (c_ctx, m_c_ctx, v_c_ctx), "w_mod": (w_mod, m_w_mod, v_w_mod), "b_mod": (b_mod, m_b_mod, v_b_mod),
             "q_norm_w": (q_norm_w, m_q_norm_w, v_q_norm_w), "k_norm_w": (k_norm_w, m_k_norm_w, v_k_norm_w),
             "conv_qkv_w": (conv_qkv_w, m_conv_qkv_w, v_conv_qkv_w), "a_log": (a_log, m_a_log, v_a_log),
             "dt_bias": (dt_bias, m_dt_bias, v_dt_bias), "gdn_norm_w": (gdn_norm_w, m_gdn_norm_w, v_gdn_norm_w),
             "ffn_conv_w": (ffn_conv_w, m_ffn_conv_w, v_ffn_conv_w), "ffn_conv_b": (ffn_conv_b, m_ffn_conv_b, v_ffn_conv_b),
             "final_norm_w": (final_norm_w, m_final_norm_w, v_final_norm_w)}
    res["w_mod"] = (grads["w_mod"],) + _adamw(w_mod, grads["w_mod"], m_w_mod, v_w_mod, name="adamw_w_mod")
    small_names = [n for n in given if n != "w_mod"]
    updates = _adamw_many([(given[n][0], grads[n], given[n][1], given[n][2]) for n in small_names], name="adamw_small")
    for n, upd in zip(small_names, updates):
        res[n] = (grads[n],) + upd

    g_in_thru, land = _scatter_wait(*sems_a, g_in_thru, land, (0, D // 2), [res[n][1] for n in res],
                                    name="scatter_g_in_a_wait")
    _, land = _scatter_wait(*sems_b, g_in_thru, land, (D // 2, D // 2), (), name="scatter_g_in_b_wait")
    res["w_in"] = finish("w_in", _adamw_recv(big["w_in"], land, moment(m_w_in, "w_in"), moment(v_w_in, "w_in"),
                                             name="adamw_w_in", own=own_in))

    order = ["c_ctx", "w_mod", "b_mod", "w_in", "q_norm_w", "k_norm_w", "conv_qkv_w", "a_log", "dt_bias", "gdn_norm_w",
             "w_pa", "w_pd", "w_out", "w_up", "ffn_conv_w", "ffn_conv_b", "w_down", "final_norm_w"]
    return (loss, grad_x[None], *[res[n][0] for n in order], *[res[n][1] for n in order],
            *[res[n][2] for n in order], *[res[n][3] for n in order])
```

```python
import functools
import math

import jax
import jax.numpy as jnp
from jax import lax
from jax.experimental import pallas as pl
from jax.experimental.pallas import tpu as pltpu

F32 = jnp.float32
BF16 = jnp.bfloat16
HI = lax.Precision.HIGHEST
MESH = pl.DeviceIdType.MESH

NDEV = 8
D = 1024
HD = 128
AH, AKV, GRP = 8, 2, 4
GH = 8
CH = 64
DFF = 2816
GRID_W = 64
EPS = 1e-6
ROPE_THETA = 10000.0
LOG2E = math.log2(math.e)
C_KV, C_AQ, C_QKV, C_BL, C_Z, C_GATE, C_END = 0, 512, 1536, 4608, 5120, 6144, 8192
W_QKV, W_AQ, W_Z, W_END = 512, 3616, 4640, 7712


def _pad_columns(w):
    zeros = jnp.zeros((C_Z - C_QKV - (W_AQ - W_QKV), D), w.dtype)
    return jnp.concatenate([w[:W_QKV], w[W_AQ:W_Z], w[W_QKV:W_AQ], zeros, w[W_Z:]], axis=0)


def _unpad_columns(g):
    return jnp.concatenate([g[:C_AQ], g[C_QKV:C_QKV + W_AQ - W_QKV], g[C_AQ:C_QKV], g[C_Z:]], axis=0)
LR, B1, B2, AEPS, WD, STEP = 0.001, 0.9, 0.999, 1e-08, 0.01, 10
VMEM_BIG = 56 * 1024 * 1024
INTRA_FWD_CHUNKS = 36
INTRA_BWD_CHUNKS = 36


def _call(body, *, name, out_shape, grid=None, in_specs=None, out_specs=None, scratch=(), sem=None,
          vmem=None, aliases=None):
    params = {}
    if sem is not None:
        params["dimension_semantics"] = sem
    if vmem is not None:
        params["vmem_limit_bytes"] = vmem
    kw = {}
    if grid is not None:
        kw["grid"] = grid
    if in_specs is not None:
        kw["in_specs"] = in_specs
    if out_specs is not None:
        kw["out_specs"] = out_specs
    if aliases:
        kw["input_output_aliases"] = aliases
    return pl.pallas_call(body, name=name, out_shape=out_shape, scratch_shapes=list(scratch),
                          compiler_params=pltpu.CompilerParams(**params), **kw)


def _call_carrying(body, exch, *, name, out_shape, grid, in_specs, out_specs, scratch=(), vmem=None):
    n, nin, nout, nscr = exch.n, len(in_specs), len(out_shape), len(scratch)
    steps = math.prod(grid)
    mid = (2 * steps) // 3

    def wrapped(*refs):
        ins, cins = refs[:nin], refs[nin:nin + n]
        outs, couts = refs[nin + n:nin + n + nout], refs[nin + n + nout:nin + 2 * n + nout]
        scr, sems = refs[nin + 2 * n + nout:nin + 2 * n + nout + nscr], refs[nin + 2 * n + nout + nscr:]
        ids = [pl.program_id(i) for i in range(len(grid))]
        first = functools.reduce(jnp.logical_and, [i == 0 for i in ids])
        last = functools.reduce(jnp.logical_and, [i == g - 1 for i, g in zip(ids, grid)])

        @pl.when(first)
        def _():
            exch.start(cins, couts, sems)

        if hasattr(exch, "middle"):
            linear = functools.reduce(lambda acc, ig: acc * ig[1] + ig[0], zip(ids, grid), 0)

            @pl.when(linear == mid)
            def _():
                exch.middle(cins, couts, sems)

        body(*ins, *outs, *scr)

        @pl.when(last)
        def _():
            exch.finish(cins, couts, sems)

    params = {"dimension_semantics": ("arbitrary",) * len(grid)}
    if vmem is not None:
        params["vmem_limit_bytes"] = vmem
    fn = pl.pallas_call(wrapped, name=name, out_shape=tuple(out_shape) + exch.out_shape, grid=grid,
                        in_specs=list(in_specs) + [HBM] * n, out_specs=tuple(out_specs) + (HBM,) * n,
                        scratch_shapes=list(scratch) + exch.scratch, compiler_params=pltpu.CompilerParams(**params))

    def run(*args):
        res = fn(*args, *exch.arrs)
        return res[:nout], list(res[nout:])

    return run


def _sds(shape, dtype=F32):
    return jax.ShapeDtypeStruct(tuple(shape), dtype)


def _dot(a, b, ca, cb):
    return lax.dot_general(a.astype(BF16), b.astype(BF16), (((ca,), (cb,)), ((), ())),
                           preferred_element_type=F32)


@jax.custom_vjp
def _nn(a, b):
    return _dot(a, b, 1, 0)


@jax.custom_vjp
def _nt(a, b):
    return _dot(a, b, 1, 1)


@jax.custom_vjp
def _tn(a, b):
    return _dot(a, b, 0, 0)


_nn.defvjp(lambda a, b: (_nn(a, b), (a, b)), lambda r, g: (_nt(g, r[1]), _tn(r[0], g)))
_nt.defvjp(lambda a, b: (_nt(a, b), (a, b)), lambda r, g: (_nn(g, r[1]), _tn(g, r[0])))
_tn.defvjp(lambda a, b: (_tn(a, b), (a, b)), lambda r, g: (_nt(r[1], g), _nn(r[0], g)))


def _mdot(a, b):
    return jnp.dot(a, b, precision=lax.Precision.HIGH, preferred_element_type=F32)


def _maskdot(mask, a, cm):
    hi = a.astype(BF16)
    r = a - hi.astype(F32)
    mid = r.astype(BF16)
    lo = (r - mid.astype(F32)).astype(BF16)
    mb = mask.astype(BF16)
    dims = (((cm,), (0,)), ((), ()))
    return (lax.dot_general(mb, hi, dims, preferred_element_type=F32)
            + lax.dot_general(mb, mid, dims, preferred_element_type=F32)
            + lax.dot_general(mb, lo, dims, preferred_element_type=F32))


@jax.custom_vjp
def _mask_nn(mask, a):
    return _maskdot(mask, a, 1)


_mask_nn.defvjp(lambda mask, a: (_maskdot(mask, a, 1), mask),
                lambda mask, g: (jnp.zeros_like(mask), _maskdot(mask, g, 0)))


@jax.custom_vjp
def _saved_inverse(lmat, x):
    return x


def _saved_inverse_bwd(x, g):
    t = lax.dot_general(x, g, (((0,), (0,)), ((), ())), precision=lax.Precision.HIGH, preferred_element_type=F32)
    dl = lax.dot_general(t, x, (((1,), (1,)), ((), ())), precision=lax.Precision.HIGH, preferred_element_type=F32)
    return -dl, jnp.zeros_like(x)


_saved_inverse.defvjp(lambda lmat, x: (x, x), _saved_inverse_bwd)


def _row_ids(shape):
    return lax.broadcasted_iota(jnp.int32, shape, 0)


def _shift_rows(x, down, bounds):
    n = x.shape[0]
    rows = _row_ids(x.shape)
    y = pltpu.roll(x, 1 if down else n - 1, 0)
    edge = functools.reduce(jnp.logical_or, [rows == (s if down else e - 1) for s, e in bounds])
    return jnp.where(edge, 0.0, y)


def _make_shift(bounds):
    @jax.custom_vjp
    def down(x):
        return _shift_rows(x, True, bounds)

    @jax.custom_vjp
    def up(x):
        return _shift_rows(x, False, bounds)

    down.defvjp(lambda x: (down(x), None), lambda _, g: (up(g),))
    up.defvjp(lambda x: (up(x), None), lambda _, g: (down(g),))
    return down, up


@jax.custom_vjp
def _swap32(x):
    lane = lax.broadcasted_iota(jnp.int32, x.shape, x.ndim - 1)
    return jnp.where((lane % 64) < 32, pltpu.roll(x, HD - 32, x.ndim - 1), pltpu.roll(x, 32, x.ndim - 1))


_swap32.defvjp(lambda x: (_swap32(x), None), lambda _, g: (_swap32(g),))


def _rms(x):
    return x * lax.rsqrt(jnp.mean(x * x, axis=-1, keepdims=True) + EPS)


def _silu(x):
    return x * jax.nn.sigmoid(x)


def _mm(a, b, *, name, M, N, K, ta=False, tb=False, out_dtype=F32, bm=None, bn=None, bk=None, after=()):
    bm, bn, bk = bm or M, bn or N, bk or K
    assert M % bm == 0 and N % bn == 0 and K % bk == 0, (name, M, N, K, bm, bn, bk)
    nk = K // bk
    ca, cb = (0 if ta else 1), (1 if tb else 0)
    na = len(after)

    def body(a_ref, b_ref, *rest):
        o_ref, acc = rest[na], rest[na + 1:]
        r = _dot(a_ref[...], b_ref[...], ca, cb)
        if nk == 1:
            o_ref[...] = r.astype(out_dtype)
        else:
            acc_ref, = acc
            k = pl.program_id(2)

            @pl.when(k == 0)
            def _():
                acc_ref[...] = r

            @pl.when(k > 0)
            def _():
                acc_ref[...] += r

            @pl.when(k == nk - 1)
            def _():
                o_ref[...] = acc_ref[...].astype(out_dtype)

    a_spec = pl.BlockSpec((bk, bm), lambda i, j, k: (k, i)) if ta else pl.BlockSpec((bm, bk), lambda i, j, k: (i, k))
    b_spec = pl.BlockSpec((bn, bk), lambda i, j, k: (j, k)) if tb else pl.BlockSpec((bk, bn), lambda i, j, k: (k, j))
    return _call(body, name=name, out_shape=_sds((M, N), out_dtype), grid=(M // bm, N // bn, nk),
                 in_specs=[a_spec, b_spec] + [pl.BlockSpec(memory_space=pl.ANY)] * na,
                 out_specs=pl.BlockSpec((bm, bn), lambda i, j, k: (i, j)),
                 scratch=[pltpu.VMEM((bm, bn), F32)] if nk > 1 else [],
                 sem=("parallel", "parallel", "arbitrary"), vmem=VMEM_BIG)(a, b, *after)


def _normmod_fn(x, sh, sc):
    return _rms(x) * (1.0 + sc) + sh


def _normmod_fwd(x, mod, i_sh, i_sc, *, name, br=256):
    R = x.shape[0]

    def body(x_ref, mod_ref, o_ref):
        o_ref[...] = _normmod_fn(x_ref[...], mod_ref[i_sh:i_sh + 1, :], mod_ref[i_sc:i_sc + 1, :]).astype(BF16)

    return _call(body, name=name, out_shape=_sds((R, D), BF16), grid=(R // br,),
                 in_specs=[pl.BlockSpec((br, D), lambda i: (i, 0)), pl.BlockSpec((6, D), lambda i: (0, 0))],
                 out_specs=pl.BlockSpec((br, D), lambda i: (i, 0)), sem=("parallel",))(x, mod)


def _normmod_bwd(x, mod, i_sh, i_sc, dh, dh_off, res, *, name, br=256):
    R = x.shape[0]
    ob = dh_off // br
    has_res = res is not None

    def body(x_ref, mod_ref, dh_ref, *rest):
        if has_res:
            res_ref, dx_ref, dsh_ref, dsc_ref = rest
        else:
            dx_ref, dsh_ref, dsc_ref = rest
        sh, sc = mod_ref[i_sh:i_sh + 1, :], mod_ref[i_sc:i_sc + 1, :]
        _, vjp = jax.vjp(_normmod_fn, x_ref[...], sh, sc)
        dx, dsh, dsc = vjp(dh_ref[...])
        dx_ref[...] = dx + res_ref[...] if has_res else dx

        @pl.when(pl.program_id(0) == 0)
        def _():
            dsh_ref[...] = jnp.zeros_like(dsh_ref)
            dsc_ref[...] = jnp.zeros_like(dsc_ref)

        dsh_ref[...] += dsh
        dsc_ref[...] += dsc

    row = pl.BlockSpec((br, D), lambda i: (i, 0))
    vec = pl.BlockSpec((1, D), lambda i: (0, 0))
    ins = [row, pl.BlockSpec((6, D), lambda i: (0, 0)), pl.BlockSpec((br, D), lambda i: (i + ob, 0))]
    args = [x, mod, dh]
    if has_res:
        ins.append(row)
        args.append(res)
    return _call(body, name=name, out_shape=(_sds((R, D)), _sds((1, D)), _sds((1, D))), grid=(R // br,),
                 in_specs=ins, out_specs=(row, vec, vec), sem=("arbitrary",))(*args)


def _rope(x, cos, sin):
    return x * cos + _swap32(x) * sin


def _aprep_fn(qs, ks, cos, sin, qw, kw):
    return ([_rope(_rms(q) * qw, cos, sin) for q in qs], [_rope(_rms(k) * kw, cos, sin) for k in ks])


def _aprep_fwd(proj, cos, sin, qw, kw, *, br=256):
    T = proj.shape[0]

    def body(x_ref, cos_ref, sin_ref, qw_ref, kw_ref, q_ref, k_ref, v_ref):
        qs = [x_ref[:, C_AQ + h * HD:C_AQ + (h + 1) * HD] for h in range(AH)]
        ks = [x_ref[:, h * HD:(h + 1) * HD] for h in range(AKV)]
        qo, ko = _aprep_fn(qs, ks, cos_ref[...], sin_ref[...], qw_ref[...], kw_ref[...])
        for h in range(AH):
            q_ref[h] = qo[h].astype(BF16)
        for h in range(AKV):
            k_ref[h] = ko[h].astype(BF16)
            v_ref[h] = x_ref[:, (AKV + h) * HD:(AKV + h + 1) * HD].astype(BF16)

    tab = pl.BlockSpec((br, HD), lambda i: (i, 0))
    vec = pl.BlockSpec((1, HD), lambda i: (0, 0))
    return _call(body, name="aprep_fwd",
                 out_shape=(_sds((AH, T, HD), BF16), _sds((AKV, T, HD), BF16), _sds((AKV, T, HD), BF16)),
                 grid=(T // br,),
                 in_specs=[pl.BlockSpec((br, C_QKV), lambda i: (i, 0)), tab, tab, vec, vec],
                 out_specs=(pl.BlockSpec((AH, br, HD), lambda i: (0, i, 0)),
                            pl.BlockSpec((AKV, br, HD), lambda i: (0, i, 0)),
                            pl.BlockSpec((AKV, br, HD), lambda i: (0, i, 0))),
                 sem=("parallel",))(proj, cos, sin, qw, kw)


def _aprep_bwd(proj, cos, sin, qw, kw, dq, dk, dv, dproj, L, *, br=256):
    T = proj.shape[0]
    lb = L // br

    def body(x_ref, cos_ref, sin_ref, qw_ref, kw_ref, dq_ref, dk_ref, dv_ref, _, dx_ref, dqw_ref, dkw_ref):
        i = pl.program_id(0)
        qs = [x_ref[:, C_AQ + h * HD:C_AQ + (h + 1) * HD] for h in range(AH)]
        ks = [x_ref[:, h * HD:(h + 1) * HD] for h in range(AKV)]
        _, vjp = jax.vjp(_aprep_fn, qs, ks, cos_ref[...], sin_ref[...], qw_ref[...], kw_ref[...])
        is_lat = i >= lb
        dqs = [jnp.where(is_lat, dq_ref[h], 0.0) for h in range(AH)]
        dks = [dk_ref[h] for h in range(AKV)]
        gq, gk, _, _, gqw, gkw = vjp((dqs, dks))
        for h in range(AH):
            dx_ref[:, C_AQ + h * HD:C_AQ + (h + 1) * HD] = gq[h].astype(BF16)
        for h in range(AKV):
            dx_ref[:, h * HD:(h + 1) * HD] = gk[h].astype(BF16)
            dx_ref[:, (AKV + h) * HD:(AKV + h + 1) * HD] = dv_ref[h].astype(BF16)

        @pl.when(i == 0)
        def _():
            dqw_ref[...] = jnp.zeros_like(dqw_ref)
            dkw_ref[...] = jnp.zeros_like(dkw_ref)

        dqw_ref[...] += gqw
        dkw_ref[...] += gkw

    tab = pl.BlockSpec((br, HD), lambda i: (i, 0))
    vec = pl.BlockSpec((1, HD), lambda i: (0, 0))
    kvb = pl.BlockSpec((AKV, br, HD), lambda i: (0, i, 0))
    blk = pl.BlockSpec((br, C_QKV), lambda i: (i, 0))
    return _call(body, name="aprep_bwd", out_shape=(_sds(dproj.shape, BF16), _sds((1, HD)), _sds((1, HD))),
                 grid=(T // br,),
                 in_specs=[blk, tab, tab, vec, vec,
                           pl.BlockSpec((AH, br, HD), lambda i: (0, jnp.maximum(i - lb, 0), 0)), kvb, kvb, ANYSPEC],
                 out_specs=(blk, vec, vec), aliases={8: 0},
                 sem=("arbitrary",))(proj, cos, sin, qw, kw, dq, dk, dv, dproj)


def _attn_grad(q, k, v, o, lse2, do):
    scale = HD ** -0.5
    p = jnp.exp2(_dot(q, k, 1, 1) * (scale * LOG2E) - lse2)
    dp = _dot(do, v, 1, 1)
    ds = p * (dp - jnp.sum(do * o, axis=-1, keepdims=True)) * scale
    return _dot(ds, k, 1, 0), _dot(ds, q, 0, 0), _dot(p, do, 0, 0)


ATTN_KEYS = 256


def _attn_fwd(q, k, v, L, exch, *, bq=128):
    T = q.shape[1]
    N = T - L
    lb = L // bq
    assert T % ATTN_KEYS == 0
    scale = HD ** -0.5
    heads = range(GRP)

    def body(q_ref, k_ref, v_ref, o_ref, o32_ref, lse_ref):
        qs = [q_ref[g] for g in heads]
        m = [jnp.full((bq, 1), -jnp.inf, F32) for _ in heads]
        l = [jnp.zeros((bq, 1), F32) for _ in heads]
        acc = [jnp.zeros((bq, HD), F32) for _ in heads]
        for c in range(T // ATTN_KEYS):
            kc, vc = k_ref[c * ATTN_KEYS:(c + 1) * ATTN_KEYS, :], v_ref[c * ATTN_KEYS:(c + 1) * ATTN_KEYS, :]
            s = [_dot(qs[g], kc, 1, 1) * (scale * LOG2E) for g in heads]
            m_new = [jnp.maximum(m[g], jnp.max(s[g], axis=-1, keepdims=True)) for g in heads]
            alpha = [jnp.exp2(m[g] - m_new[g]) for g in heads]
            p = [jnp.exp2(s[g] - m_new[g]) for g in heads]
            l = [l[g] * alpha[g] + jnp.sum(p[g], axis=-1, keepdims=True) for g in heads]
            acc = [acc[g] * alpha[g] + _dot(p[g], vc, 1, 0) for g in heads]
            m = m_new
        for g in heads:
            o = acc[g] / l[g]
            o_ref[:, g * HD:(g + 1) * HD] = o.astype(BF16)
            o32_ref[:, g * HD:(g + 1) * HD] = o
            lse_ref[g] = jnp.broadcast_to(m[g] + jnp.log2(l[g]), (bq, HD))

    kvb = pl.BlockSpec((None, T, HD), lambda g, i: (g, 0, 0))
    ob = pl.BlockSpec((bq, GRP * HD), lambda g, i: (i, g))
    return _call_carrying(
        body, exch, name="attn_fwd",
        out_shape=(_sds((N, AH * HD), BF16), _sds((N, AH * HD)), _sds((AH, N, HD))), grid=(AKV, N // bq),
        in_specs=[pl.BlockSpec((GRP, bq, HD), lambda g, i: (g, i + lb, 0)), kvb, kvb],
        out_specs=(ob, ob, pl.BlockSpec((GRP, bq, HD), lambda g, i: (g, i, 0))), vmem=VMEM_BIG)(q, k, v)


def _attn_bwd(q, k, v, o32, lse, do, L, exch, *, bq=128):
    T = q.shape[1]
    N = T - L
    lb = L // bq

    def body(q_ref, k_ref, v_ref, o_ref, lse_ref, do_ref, dq_ref, dk_ref, dv_ref):
        rows = lambda r: jnp.concatenate([r[:, g * HD:(g + 1) * HD] for g in range(GRP)], axis=0)
        lse = jnp.max(lse_ref[...].reshape(GRP * bq, HD), axis=-1, keepdims=True)
        dq, dk, dv = _attn_grad(q_ref[...].reshape(GRP * bq, HD), k_ref[...], v_ref[...], rows(o_ref), lse, rows(do_ref))
        dq_ref[...] = dq.reshape(GRP, bq, HD)

        @pl.when(pl.program_id(1) == 0)
        def _():
            dk_ref[...] = jnp.zeros_like(dk_ref)
            dv_ref[...] = jnp.zeros_like(dv_ref)

        dk_ref[...] += dk
        dv_ref[...] += dv

    kvb = pl.BlockSpec((None, T, HD), lambda g, i: (g, 0, 0))
    qb = pl.BlockSpec((GRP, bq, HD), lambda g, i: (g, i + lb, 0))
    hb = pl.BlockSpec((GRP, bq, HD), lambda g, i: (g, i, 0))
    ob = pl.BlockSpec((bq, GRP * HD), lambda g, i: (i, g))
    return _call_carrying(body, exch, name="attn_bwd",
                          out_shape=(_sds((AH, N, HD)), _sds((AKV, T, HD)), _sds((AKV, T, HD))), grid=(AKV, N // bq),
                          in_specs=[qb, kvb, kvb, ob, hb, ob], out_specs=(hb, kvb, kvb),
                          vmem=VMEM_BIG)(q, k, v, o32, lse, do)


def _gprep_fn(kind, shifts, x, w):
    down, up = shifts
    y = down(x) * w[0:1, :] + x * w[1:2, :] + up(x) * w[2:3, :]
    a = _silu(y)
    if kind == 2:
        return a
    a = a * lax.rsqrt(jnp.sum(a * a, axis=-1, keepdims=True) + EPS)
    return a * (HD ** -0.5) if kind == 0 else a


def _gprep_fwd(proj, conv_w, kind, bounds):
    T = proj.shape[0]
    shifts = _make_shift(bounds)
    cb = C_QKV // HD + kind * GH

    def body(x_ref, w_ref, o_ref):
        o_ref[...] = _gprep_fn(kind, shifts, x_ref[...], w_ref[...])

    return _call(body, name=f"gprep_fwd{kind}", out_shape=_sds((GH, T, HD)), grid=(GH,),
                 in_specs=[pl.BlockSpec((T, HD), lambda h: (0, cb + h)),
                           pl.BlockSpec((3, HD), lambda h: (0, kind * GH + h))],
                 out_specs=pl.BlockSpec((None, T, HD), lambda h: (h, 0, 0)), sem=("parallel",))(proj, conv_w)


def _gprep_bwd(proj, conv_w, kind, bounds, dy, dproj):
    T = proj.shape[0]
    shifts = _make_shift(bounds)
    cb = C_QKV // HD + kind * GH

    def body(x_ref, w_ref, dy_ref, _, dx_ref, dw_ref):
        _, vjp = jax.vjp(functools.partial(_gprep_fn, kind, shifts), x_ref[...], w_ref[...])
        dx, dw = vjp(dy_ref[0] + dy_ref[1])
        dx_ref[...] = dx.astype(BF16)
        dw_ref[...] = dw

    return _call(body, name=f"gprep_bwd{kind}", out_shape=(_sds(dproj.shape, BF16), _sds((3, GH * HD))), grid=(GH,),
                 in_specs=[pl.BlockSpec((T, HD), lambda h: (0, cb + h)),
                           pl.BlockSpec((3, HD), lambda h: (0, kind * GH + h)),
                           pl.BlockSpec((2, None, T, HD), lambda h: (0, h, 0, 0)), ANYSPEC],
                 out_specs=(pl.BlockSpec((T, HD), lambda h: (0, cb + h)), pl.BlockSpec((3, HD), lambda h: (0, h))),
                 aliases={3: 0}, sem=("parallel",))(proj, conv_w, dy, dproj)


def _bl_fn(x, alog, dtb):
    lane = lax.broadcasted_iota(jnp.int32, x.shape, 1)
    beta = jax.nn.sigmoid(x)
    z = x + dtb
    sp = jnp.maximum(z, 0.0) + jnp.log1p(jnp.exp(-jnp.abs(z)))
    la = -jnp.exp(alog) * sp
    return jnp.where(lane < 2 * GH, beta, jnp.where(lane < 4 * GH, la, 0.0))


def _bl_fwd(proj, alog, dtb, *, br=256):
    T = proj.shape[0]

    def body(x_ref, a_ref, d_ref, o_ref):
        o_ref[...] = _bl_fn(x_ref[...], a_ref[...], d_ref[...])

    vec = pl.BlockSpec((1, HD), lambda i: (0, 0))
    return _call(body, name="bl_fwd", out_shape=_sds((T, HD)), grid=(T // br,),
                 in_specs=[pl.BlockSpec((br, HD), lambda i: (i, C_BL // HD)), vec, vec],
                 out_specs=pl.BlockSpec((br, HD), lambda i: (i, 0)), sem=("parallel",))(proj, alog, dtb)


def _bl_bwd(proj, alog, dtb, dbl, dproj, *, br=256):
    T = proj.shape[0]
    wide = C_Z - C_BL

    def body(x_ref, a_ref, d_ref, g_ref, _, dx_ref, da_ref, dd_ref):
        g = g_ref[0, 0]
        for d in range(2):
            for h in range(GH):
                if d or h:
                    g = g + g_ref[d, h]
        _, vjp = jax.vjp(_bl_fn, x_ref[...], a_ref[...], d_ref[...])
        dx, da, dd = vjp(g)
        dx_ref[:, :HD] = dx.astype(BF16)
        dx_ref[:, HD:] = jnp.zeros((br, wide - HD), BF16)

        @pl.when(pl.program_id(0) == 0)
        def _():
            da_ref[...] = jnp.zeros_like(da_ref)
            dd_ref[...] = jnp.zeros_like(dd_ref)

        da_ref[...] += da
        dd_ref[...] += dd

    vec = pl.BlockSpec((1, HD), lambda i: (0, 0))
    return _call(body, name="bl_bwd", out_shape=(_sds(dproj.shape, BF16), _sds((1, HD)), _sds((1, HD))), grid=(T // br,),
                 in_specs=[pl.BlockSpec((br, HD), lambda i: (i, C_BL // HD)), vec, vec,
                           pl.BlockSpec((2, GH, br, HD), lambda i: (0, 0, i, 0)), ANYSPEC],
                 out_specs=(pl.BlockSpec((br, wide), lambda i: (i, C_BL // wide)), vec, vec), aliases={4: 0},
                 sem=("arbitrary",))(proj, alog, dtb, dbl, dproj)


def _chunk_masks(d):
    ii = lax.broadcasted_iota(jnp.int32, (CH, CH), 0)
    jj = lax.broadcasted_iota(jnp.int32, (CH, CH), 1)
    eye = (ii == jj).astype(F32)
    before = jnp.where(d == 0, (jj < ii).astype(F32), (jj > ii).astype(F32))
    return before, before + eye, eye


def _same_block(b):
    ii = lax.broadcasted_iota(jnp.int32, (CH, CH), 0)
    jj = lax.broadcasted_iota(jnp.int32, (CH, CH), 1)
    shift = b.bit_length() - 1
    return (jnp.right_shift(ii, shift) == jnp.right_shift(jj, shift)).astype(F32)


def _intra_fn(masks, sel_b, sel_l, qs, ks, vs, bls, xs=None):
    before, ateq, eye = masks
    inc = ateq > 0.0
    each = lambda f, *ls: [f(*t) for t in zip(*ls)]
    beta = each(lambda bl: jnp.sum(bl * sel_b, axis=-1, keepdims=True), bls)
    la = each(lambda bl: jnp.sum(bl * sel_l, axis=-1, keepdims=True), bls)
    gam = each(lambda a: _mask_nn(ateq, jnp.broadcast_to(a, (CH, HD))), la)
    gi = each(lambda g: g[:, :CH], gam)
    gj = each(lambda g: jnp.transpose(g)[:CH, :], gam)
    kq = each(lambda k, q: _nt(jnp.concatenate([k, q], axis=0), k), ks, qs)
    kk = each(lambda t: t[:CH], kq)
    qk = each(lambda t: t[CH:], kq)
    dec = each(lambda a, b: jnp.where(inc, jnp.exp(jnp.where(inc, a - b, 0.0)), 0.0), gi, gj)
    lmat = each(lambda b, d, m: before * (b * d * m), beta, dec, kk)
    if xs is None:
        same = lambda b: _same_block(b)
        l8 = each(lambda m: m * same(8), lmat)
        x = each(lambda m: eye - m, l8)
        p2 = each(lambda m: _mdot(m, m), l8)
        y = each(lambda a, b: _mdot(jnp.concatenate([a, b], axis=0), b), x, p2)
        x = each(lambda a, t: a + t[:CH], x, y)
        x = each(lambda a, t: a + _mdot(a, t[CH:]), x, y)
        for b in (8, 16, 32):
            below = same(2 * b) - same(b)
            x = each(lambda a, m: a - _mdot(a, _mdot(m * below, a)), x, lmat)
    else:
        x = each(_saved_inverse, lmat, xs)
    eg = each(jnp.exp, gam)
    uw = each(lambda a, b, v, e, k: _mdot(a, jnp.concatenate([b * v, (b * e) * k], axis=1)), x, beta, vs, eg, ks)
    u = each(lambda t: t[:, :HD], uw)
    w = each(lambda t: t[:, HD:], uw)
    tot = each(lambda a: jnp.sum(a, axis=0, keepdims=True), la)
    kd = each(lambda k, t, g: k * jnp.exp(t - g), ks, tot, gam)
    gl = each(lambda t: jnp.broadcast_to(jnp.exp(t), (1, HD)), tot)
    qd = each(lambda q, e: q * e, qs, eg)
    p = each(lambda d, m: d * m, dec, qk)
    return (u, w, kd, qd, p, gl, x) if xs is None else (u, w, kd, qd, p, gl)


def _dir_head_sel(d, h):
    lane = lax.broadcasted_iota(jnp.int32, (1, HD), 1)
    return (lane == d * GH + h).astype(F32), (lane == 2 * GH + d * GH + h).astype(F32)


def _intra_specs(T, G):
    nc = T // CH
    assert nc % G == 0
    qkv = pl.BlockSpec((None, G * CH, HD), lambda d, h, c: (h, c, 0))
    bl = pl.BlockSpec((G * CH, HD), lambda d, h, c: (c, 0))
    big = pl.BlockSpec((None, None, G * CH, HD), lambda d, h, c: (d, h, c, 0))
    pm = pl.BlockSpec((None, None, G * CH, CH), lambda d, h, c: (d, h, c, 0))
    gl = pl.BlockSpec((None, None, G, 1, HD), lambda d, h, c: (d, h, c, 0, 0))
    shapes = (_sds((2, GH, T, HD)),) + (_sds((2, GH, T, HD), BF16),) * 3 + (
        _sds((2, GH, T, CH), BF16), _sds((2, GH, nc, 1, HD)), _sds((2, GH, T, CH)))
    return nc, qkv, bl, big, pm, gl, shapes


def _chunks_per_step(T, most):
    nc = T // CH
    return max(g for g in range(1, most + 1) if nc % g == 0)


def _chunk_at(g, d, nc, ncc):
    pos = _visit_pos(g, d, nc, ncc)
    return pos, pl.ds(pl.multiple_of(pos * CH, CH), CH)


def _intra_fwd(q, k, v, bl, L, exch):
    T = q.shape[1]
    G = _chunks_per_step(T, INTRA_FWD_CHUNKS)
    nc, qkv_s, bl_s, big, pm, gl_s, shapes = _intra_specs(T, G)
    assert G == nc
    ncc = L // CH

    def body(q_ref, k_ref, v_ref, bl_ref, u_ref, w_ref, kd_ref, qd_ref, p_ref, gl_ref, x_ref):
        d, h = pl.program_id(0), pl.program_id(1)
        sb, sl = _dir_head_sel(d, h)
        rows = [slice(g * CH, (g + 1) * CH) for g in range(G)]
        outs = _intra_fn(_chunk_masks(d), sb, sl, *[[r[s, :] for s in rows] for r in (q_ref, k_ref, v_ref, bl_ref)])
        for g in range(G):
            pos, at = _chunk_at(g, d, nc, ncc)
            for r, o in zip((u_ref, w_ref, kd_ref, qd_ref, p_ref, x_ref), outs[:5] + outs[6:]):
                r[at, :] = o[g].astype(r.dtype)
            gl_ref[pos] = outs[5][g]

    return _call_carrying(body, exch, name="gdn_intra_fwd", out_shape=shapes, grid=(2, GH, nc // G),
                          in_specs=[qkv_s, qkv_s, qkv_s, bl_s], out_specs=(big, big, big, big, pm, gl_s, pm))(q, k, v, bl)


def _intra_bwd(q, k, v, bl, xinv, cts, L, exch):
    T = q.shape[1]
    G = _chunks_per_step(T, INTRA_BWD_CHUNKS)
    nc, qkv_s, bl_s, big, pm, gl_s, _ = _intra_specs(T, G)
    assert G == nc
    ncc = L // CH

    def body(q_ref, k_ref, v_ref, bl_ref, x_ref, du, dw, dkd, dqd, dp, dgl, dq_ref, dk_ref, dv_ref, dbl_ref):
        d, h = pl.program_id(0), pl.program_id(1)
        sb, sl = _dir_head_sel(d, h)
        rows = [slice(g * CH, (g + 1) * CH) for g in range(G)]
        places = [_chunk_at(g, d, nc, ncc) for g in range(G)]
        fn = functools.partial(_intra_fn, _chunk_masks(d), sb, sl, xs=[x_ref[at, :] for _, at in places])
        _, vjp = jax.vjp(fn, *[[r[s, :] for s in rows] for r in (q_ref, k_ref, v_ref, bl_ref)])
        cts = tuple([r[at, :] for _, at in places] for r in (du, dw, dkd, dqd, dp)) + ([dgl[pos] for pos, _ in places],)
        grads = vjp(cts)
        for g in range(G):
            for r, o in zip((dq_ref, dk_ref, dv_ref, dbl_ref), grads):
                r[rows[g], :] = o[g]

    return _call_carrying(body, exch, name="gdn_intra_bwd", out_shape=(_sds((2, GH, T, HD)),) * 4,
                          grid=(2, GH, nc // G), in_specs=[qkv_s, qkv_s, qkv_s, bl_s, pm, big, big, big, big, pm, gl_s],
                          out_specs=(big,) * 4)(q, k, v, bl, xinv, *cts)


def _scan_fn(s, u, w, kd, qd, p, gl):
    each = lambda f, *ls: [f(*t) for t in zip(*ls)]
    ws = each(_nn, w, s)
    delta = each(lambda a, b: a - b, u, ws)
    kdd = each(_tn, kd, delta)
    s_new = each(lambda g, a, b: g * a + b, gl, s, kdd)
    qs = each(_nn, qd, s)
    pd = each(_nn, p, delta)
    return each(lambda a, b: a + b, qs, pd), s_new


SCAN_BLOCK = 4


def _visit_pos(c, d, nc, ncc):
    back = ncc - 1 - c if c < ncc else ncc + (nc - 1 - c)
    return jnp.where(d == 0, c, back)


def _scan_specs(T, L, back):
    tb = SCAN_BLOCK * CH
    assert T % tb == 0 and L % tb == 0
    nb, ncb = T // tb, L // tb
    at = (lambda t: nb - 1 - t) if back else (lambda t: t)
    big = pl.BlockSpec((2, GH, tb, HD), lambda t: (0, 0, at(t), 0))
    pm = pl.BlockSpec((2, GH, tb, CH), lambda t: (0, 0, at(t), 0))
    gl = pl.BlockSpec((2, GH, SCAN_BLOCK, 1, HD), lambda t: (0, 0, at(t), 0, 0))
    st = pl.BlockSpec((2, GH, SCAN_BLOCK, HD, HD), lambda t: (0, 0, at(t), 0, 0))

    def natural(b):
        return jnp.where(b < ncb, ncb - 1 - b, nb - 1 - (b - ncb))

    do_specs = (pl.BlockSpec((GH, tb, HD), lambda t: (0, at(t), 0)),
                pl.BlockSpec((GH, tb, HD), lambda t: (0, natural(at(t)), 0)))
    return nb, big, pm, gl, st, do_specs


SCAN_STREAMS = [(d, h) for d in (0, 1) for h in range(GH)]


def _scan_fwd(u, w, kd, qd, p, gl, L):
    T = u.shape[2]
    nb, big, pm, gl_s, st, _ = _scan_specs(T, L, False)

    def body(u_ref, w_ref, kd_ref, qd_ref, p_ref, gl_ref, o_ref, st_ref, s_scr):
        @pl.when(pl.program_id(0) == 0)
        def _():
            s_scr[...] = jnp.zeros_like(s_scr)

        s = [s_scr[d, h] for d, h in SCAN_STREAMS]
        for i in range(SCAN_BLOCK):
            rows = slice(i * CH, (i + 1) * CH)
            for (d, h), sv in zip(SCAN_STREAMS, s):
                st_ref[d, h, i] = sv
            o, s = _scan_fn(s, *[[r[d, h, rows, :].astype(F32) for d, h in SCAN_STREAMS]
                                 for r in (u_ref, w_ref, kd_ref, qd_ref, p_ref)],
                            [gl_ref[d, h, i] for d, h in SCAN_STREAMS])
            for (d, h), ov in zip(SCAN_STREAMS, o):
                o_ref[d, h, rows, :] = ov
        for (d, h), sv in zip(SCAN_STREAMS, s):
            s_scr[d, h] = sv

    return _call(body, name="gdn_scan_fwd", out_shape=(_sds((2, GH, T, HD)), _sds((2, GH, T // CH, HD, HD))),
                 grid=(nb,), in_specs=[big, big, big, big, pm, gl_s], out_specs=(big, st),
                 scratch=[pltpu.VMEM((2, GH, HD, HD), F32)], sem=("arbitrary",), vmem=VMEM_BIG)(u, w, kd, qd, p, gl)


def _scan_bwd(u, w, kd, qd, p, gl, states, do, L, exch):
    T = u.shape[2]
    nb, big, pm, gl_s, st, do_specs = _scan_specs(T, L, True)

    def body(u_ref, w_ref, kd_ref, qd_ref, p_ref, gl_ref, st_ref, do0_ref, do1_ref,
             du_ref, dw_ref, dkd_ref, dqd_ref, dp_ref, dgl_ref, ds_scr):
        @pl.when(pl.program_id(0) == 0)
        def _():
            ds_scr[...] = jnp.zeros_like(ds_scr)

        ds = [ds_scr[d, h] for d, h in SCAN_STREAMS]
        for i in reversed(range(SCAN_BLOCK)):
            rows = slice(i * CH, (i + 1) * CH)
            mirror = slice((SCAN_BLOCK - 1 - i) * CH, (SCAN_BLOCK - i) * CH)
            _, vjp = jax.vjp(_scan_fn, [st_ref[d, h, i] for d, h in SCAN_STREAMS],
                             *[[r[d, h, rows, :].astype(F32) for d, h in SCAN_STREAMS]
                               for r in (u_ref, w_ref, kd_ref, qd_ref, p_ref)],
                             [gl_ref[d, h, i] for d, h in SCAN_STREAMS])
            dos = [do0_ref[h, rows, :] if d == 0 else do1_ref[h, mirror, :] for d, h in SCAN_STREAMS]
            ds, gu, gw, gkd, gqd, gp, ggl = vjp((dos, ds))
            for n, (d, h) in enumerate(SCAN_STREAMS):
                du_ref[d, h, rows, :] = gu[n]
                dw_ref[d, h, rows, :] = gw[n]
                dkd_ref[d, h, rows, :] = gkd[n]
                dqd_ref[d, h, rows, :] = gqd[n]
                dp_ref[d, h, rows, :] = gp[n]
                dgl_ref[d, h, i] = ggl[n]
        for (d, h), dv in zip(SCAN_STREAMS, ds):
            ds_scr[d, h] = dv

    return _call_carrying(
        body, exch, name="gdn_scan_bwd",
        out_shape=(_sds((2, GH, T, HD)),) * 4 + (_sds((2, GH, T, CH)), _sds((2, GH, T // CH, 1, HD))),
        grid=(nb,), in_specs=[big, big, big, big, pm, gl_s, st, *do_specs], out_specs=(big, big, big, big, pm, gl_s),
        scratch=[pltpu.VMEM((2, GH, HD, HD), F32)], vmem=VMEM_BIG)(u, w, kd, qd, p, gl, states, do, do)


def _gout_fn(o0, o1, z, gw):
    return _rms(o0 + o1) * gw * _silu(z)


def _backward_latent(o_ref, L):
    nl = (o_ref.shape[1] - L) // CH
    return jnp.concatenate([o_ref[1, L + (nl - 1 - j) * CH:L + (nl - j) * CH, :] for j in range(nl)], axis=0)


def _gout_fwd(o, proj, gw, L):
    T = o.shape[2]
    N = T - L
    ob = pl.BlockSpec((2, None, T, HD), lambda h: (0, h, 0, 0))

    def body(o_ref, z_ref, gw_ref, y_ref):
        y_ref[...] = _gout_fn(o_ref[0, L:, :], _backward_latent(o_ref, L), z_ref[L:, :], gw_ref[...]).astype(BF16)

    return _call(body, name="gout_fwd", out_shape=_sds((N, GH * HD), BF16), grid=(GH,),
                 in_specs=[ob, pl.BlockSpec((T, HD), lambda h: (0, C_Z // HD + h)), pl.BlockSpec((1, HD), lambda h: (0, 0))],
                 out_specs=pl.BlockSpec((N, HD), lambda h: (0, h)), sem=("parallel",))(o, proj, gw)


def _gout_bwd(o, proj, gw, dy, dproj, L):
    T = o.shape[2]
    N = T - L
    ob = pl.BlockSpec((2, None, T, HD), lambda h: (0, h, 0, 0))

    def body(o_ref, z_ref, gw_ref, dy_ref, _, do_ref, dz_ref, dgw_ref):
        _, vjp = jax.vjp(_gout_fn, o_ref[0, L:, :], _backward_latent(o_ref, L), z_ref[L:, :], gw_ref[...])
        g0, _, gz, ggw = vjp(dy_ref[...])
        do_ref[:L, :] = jnp.zeros((L, HD), F32)
        do_ref[L:, :] = g0
        dz_ref[:L, :] = jnp.zeros((L, HD), BF16)
        dz_ref[L:, :] = gz.astype(BF16)

        @pl.when(pl.program_id(0) == 0)
        def _():
            dgw_ref[...] = jnp.zeros_like(dgw_ref)

        dgw_ref[...] += ggw

    zb = pl.BlockSpec((T, HD), lambda h: (0, C_Z // HD + h))
    return _call(body, name="gout_bwd", out_shape=(_sds((GH, T, HD)), _sds(dproj.shape, BF16), _sds((1, HD))),
                 grid=(GH,),
                 in_specs=[ob, zb, pl.BlockSpec((1, HD), lambda h: (0, 0)), pl.BlockSpec((N, HD), lambda h: (0, h)), ANYSPEC],
                 out_specs=(pl.BlockSpec((None, T, HD), lambda h: (h, 0, 0)), zb, pl.BlockSpec((1, HD), lambda h: (0, 0))),
                 aliases={4: 1}, sem=("arbitrary",))(o, proj, gw, dy, dproj)


def _merge_fn(pa, pd, ga, gd):
    return jax.nn.sigmoid(ga) * pa + jax.nn.sigmoid(gd) * pd


def _merge_fwd(pa, pd, proj, L, *, br=256):
    N = pa.shape[0]
    lb = L // br
    row = pl.BlockSpec((br, D), lambda i: (i, 0))

    def body(pa_ref, pd_ref, ga_ref, gd_ref, y_ref):
        y_ref[...] = _merge_fn(pa_ref[...], pd_ref[...], ga_ref[...], gd_ref[...]).astype(BF16)

    return _call(body, name="merge_fwd", out_shape=_sds((N, D), BF16), grid=(N // br,),
                 in_specs=[row, row, pl.BlockSpec((br, D), lambda i: (i + lb, C_GATE // D)),
                           pl.BlockSpec((br, D), lambda i: (i + lb, C_GATE // D + 1))],
                 out_specs=row, sem=("parallel",))(pa, pd, proj, proj)


def _merge_bwd(pa, pd, proj, dy, L, *, br=256):
    N = pa.shape[0]
    T = N + L
    lb = L // br
    lrow = pl.BlockSpec((br, D), lambda i: (jnp.maximum(i - lb, 0), 0))

    def body(pa_ref, pd_ref, ga_ref, gd_ref, dy_ref, dpa_ref, dpd_ref, dg_ref):
        lat = pl.program_id(0) >= lb
        _, vjp = jax.vjp(_merge_fn, pa_ref[...], pd_ref[...], ga_ref[...], gd_ref[...])
        gpa, gpd, gga, ggd = vjp(dy_ref[...])
        dpa_ref[...] = gpa.astype(BF16)
        dpd_ref[...] = gpd.astype(BF16)
        dg_ref[:, :D] = jnp.where(lat, gga, 0.0).astype(BF16)
        dg_ref[:, D:] = jnp.where(lat, ggd, 0.0).astype(BF16)

    return _call(body, name="merge_bwd", out_shape=(_sds((N, D), BF16), _sds((N, D), BF16), _sds((T, C_END), BF16)),
                 grid=(T // br,),
                 in_specs=[lrow, lrow, pl.BlockSpec((br, D), lambda i: (i, C_GATE // D)),
                           pl.BlockSpec((br, D), lambda i: (i, C_GATE // D + 1)), lrow],
                 out_specs=(lrow, lrow, pl.BlockSpec((br, 2 * D), lambda i: (i, C_GATE // (2 * D)))),
                 sem=("arbitrary",))(pa, pd, proj, proj, dy)


def _resid_fwd(x, m, mod, i_g, *, name, br=256):
    R = x.shape[0]
    row = pl.BlockSpec((br, D), lambda i: (i, 0))

    def body(x_ref, m_ref, mod_ref, o_ref):
        o_ref[...] = x_ref[...] + mod_ref[i_g:i_g + 1, :] * m_ref[...]

    return _call(body, name=name, out_shape=_sds((R, D)), grid=(R // br,),
                 in_specs=[row, row, pl.BlockSpec((6, D), lambda i: (0, 0))], out_specs=row,
                 sem=("parallel",))(x, m, mod)


def _resid_bwd(dx, m, mod, i_g, *, name, br=256):
    R = dx.shape[0]
    row = pl.BlockSpec((br, D), lambda i: (i, 0))
    vec = pl.BlockSpec((1, D), lambda i: (0, 0))

    def body(dx_ref, m_ref, mod_ref, dm_ref, dg_ref):
        dxv = dx_ref[...]
        dm_ref[...] = (dxv * mod_ref[i_g:i_g + 1, :]).astype(BF16)

        @pl.when(pl.program_id(0) == 0)
        def _():
            dg_ref[...] = jnp.zeros_like(dg_ref)

        dg_ref[...] += jnp.sum(dxv * m_ref[...], axis=0, keepdims=True)

    return _call(body, name=name, out_shape=(_sds((R, D), BF16), _sds((1, D))), grid=(R // br,),
                 in_specs=[row, row, pl.BlockSpec((6, D), lambda i: (0, 0))], out_specs=(row, vec),
                 sem=("arbitrary",))(dx, m, mod)


def _ffn_fn(shifts, ug, uv, wg, wv, bg, bv):
    down, up = shifts

    def conv(x, w, b):
        return down(x) * w[0:1, :] + x * w[1:2, :] + up(x) * w[2:3, :] + b

    return _silu(conv(ug, wg, bg)) * conv(uv, wv, bv)


def _ffn_fwd(up, cw, cb, *, bw=256):
    N = up.shape[0]
    shifts = _make_shift(((0, N),))
    nb = DFF // bw

    def body(ug, uv, wg, wv, bg, bv, a_ref):
        a_ref[...] = _ffn_fn(shifts, ug[...], uv[...], wg[...], wv[...], bg[...], bv[...]).astype(BF16)

    def col(rows, off):
        return pl.BlockSpec((rows, bw), lambda j: (0, j + off))

    return _call(body, name="ffn_fwd", out_shape=_sds((N, DFF), BF16), grid=(nb,),
                 in_specs=[col(N, 0), col(N, nb), col(3, 0), col(3, nb), col(1, 0), col(1, nb)],
                 out_specs=col(N, 0), sem=("parallel",), vmem=VMEM_BIG)(up, up, cw, cw, cb, cb)


def _ffn_bwd(up, cw, cb, da, *, bw=256):
    N = up.shape[0]
    shifts = _make_shift(((0, N),))
    nb = DFF // bw

    def body(ug, uv, wg, wv, bg, bv, da_ref, dug, duv, dwg, dwv, dbg, dbv):
        _, vjp = jax.vjp(functools.partial(_ffn_fn, shifts), ug[...], uv[...], wg[...], wv[...], bg[...], bv[...])
        g = vjp(da_ref[...])
        dug[...] = g[0].astype(BF16)
        duv[...] = g[1].astype(BF16)
        dwg[...], dwv[...], dbg[...], dbv[...] = g[2], g[3], g[4], g[5]

    def col(rows, off):
        return pl.BlockSpec((rows, bw), lambda j: (0, j + off))

    half = (_sds((N, DFF), BF16), _sds((N, DFF), BF16), _sds((3, DFF)), _sds((3, DFF)), _sds((1, DFF)), _sds((1, DFF)))
    dug, duv, dwg, dwv, dbg, dbv = _call(
        body, name="ffn_bwd", out_shape=half, grid=(nb,),
        in_specs=[col(N, 0), col(N, nb), col(3, 0), col(3, nb), col(1, 0), col(1, nb), col(N, 0)],
        out_specs=(col(N, 0), col(N, 0), col(3, 0), col(3, 0), col(1, 0), col(1, 0)),
        sem=("parallel",), vmem=VMEM_BIG)(up, up, cw, cw, cb, cb, da)
    return (jnp.concatenate([dug, duv], axis=1), jnp.concatenate([dwg, dwv], axis=1),
            jnp.concatenate([dbg, dbv], axis=1))


def _head_fn(x1, dn, g2, fw, tgt):
    y = _rms(x1 + g2 * dn) * fw
    err = y - tgt
    return 0.5 * jnp.sum(jnp.mean(err * err, axis=-1))


def _head(x1, dn, mod, fw, tgt, *, br=256):
    N = x1.shape[0]
    row = pl.BlockSpec((br, D), lambda i: (i, 0))
    vec = pl.BlockSpec((1, D), lambda i: (0, 0))
    one = pl.BlockSpec((1, HD), lambda i: (0, 0))

    def body(x1_ref, dn_ref, mod_ref, fw_ref, tgt_ref, loss_ref, dx_ref, ddn_ref, dg_ref, dfw_ref):
        loss, (gx, gdn, gg, gfw) = jax.value_and_grad(_head_fn, argnums=(0, 1, 2, 3))(
            x1_ref[...], dn_ref[...], mod_ref[5:6, :], fw_ref[...], tgt_ref[...])
        dx_ref[...] = gx
        ddn_ref[...] = gdn.astype(BF16)

        @pl.when(pl.program_id(0) == 0)
        def _():
            loss_ref[...] = jnp.zeros_like(loss_ref)
            dg_ref[...] = jnp.zeros_like(dg_ref)
            dfw_ref[...] = jnp.zeros_like(dfw_ref)

        loss_ref[...] += jnp.broadcast_to(loss, (1, HD))
        dg_ref[...] += gg
        dfw_ref[...] += gfw

    return _call(body, name="head", out_shape=(_sds((1, HD)), _sds((N, D)), _sds((N, D), BF16), _sds((1, D)), _sds((1, D))),
                 grid=(N // br,), in_specs=[row, row, pl.BlockSpec((6, D), lambda i: (0, 0)), vec, row],
                 out_specs=(one, row, row, vec, vec), sem=("arbitrary",))(x1, dn, mod, fw, tgt)


def _adamw(w, g, m, v, *, name):
    shape = w.shape
    cols = shape[-1]
    rows = max(1, math.prod(shape[:-1]))
    w2, g2, m2, v2 = (t.reshape(rows, cols) for t in (w, g, m, v))
    br = 256 if rows % 256 == 0 else rows
    c1 = 1.0 - B1 ** STEP
    c2 = 1.0 - B2 ** STEP

    def body(w_ref, g_ref, m_ref, v_ref, d_ref, nm_ref, nv_ref):
        gv = g_ref[...]
        nm = B1 * m_ref[...] + (1.0 - B1) * gv
        nv = B2 * v_ref[...] + (1.0 - B2) * (gv * gv)
        d_ref[...] = -LR * ((nm / c1) / (jnp.sqrt(nv / c2) + AEPS) + WD * w_ref[...])
        nm_ref[...] = nm
        nv_ref[...] = nv

    blk = pl.BlockSpec((br, cols), lambda i: (i, 0))
    outs = _call(body, name=name, out_shape=(_sds((rows, cols)),) * 3, grid=(rows // br,),
                 in_specs=[blk] * 4, out_specs=(blk,) * 3, sem=("parallel",))(w2, g2, m2, v2)
    return tuple(t.reshape(shape) for t in outs)


def _adamw_many(items, *, name):
    k = len(items)
    shapes = [w.shape for w, _, _, _ in items]
    flat = [t.reshape(max(1, math.prod(t.shape[:-1])), t.shape[-1]) for it in items for t in it]
    c1 = 1.0 - B1 ** STEP
    c2 = 1.0 - B2 ** STEP

    def body(*refs):
        ins, outs = refs[:4 * k], refs[4 * k:]
        for i in range(k):
            w_ref, g_ref, m_ref, v_ref = ins[4 * i:4 * i + 4]
            gv = g_ref[...]
            nm = B1 * m_ref[...] + (1.0 - B1) * gv
            nv = B2 * v_ref[...] + (1.0 - B2) * (gv * gv)
            outs[3 * i][...] = -LR * ((nm / c1) / (jnp.sqrt(nv / c2) + AEPS) + WD * w_ref[...])
            outs[3 * i + 1][...] = nm
            outs[3 * i + 2][...] = nv

    res = _call(body, name=name, out_shape=tuple(_sds(flat[4 * i].shape) for i in range(k) for _ in range(3)))(*flat)
    return [tuple(res[3 * i + j].reshape(shapes[i]) for j in range(3)) for i in range(k)]


def _rope_tables(N, L):
    t = jnp.arange(N)
    pos = jnp.stack([(t // GRID_W).astype(F32), (t % GRID_W).astype(F32)], axis=1)
    inv = ROPE_THETA ** (-jnp.arange(0, HD // 2, 2, dtype=F32) / (HD // 2))
    ang = pos[:, :, None] * inv[None, None, :]
    cos = jnp.broadcast_to(jnp.cos(ang)[:, :, None, :], (N, 2, 2, HD // 4)).reshape(N, HD)
    sin = jnp.broadcast_to(jnp.sin(ang)[:, :, None, :], (N, 2, 2, HD // 4))
    sin = (sin * jnp.array([-1.0, 1.0], F32)[None, None, :, None]).reshape(N, HD)
    cos = jnp.concatenate([jnp.ones((L, HD), F32), cos], axis=0)
    sin = jnp.concatenate([jnp.zeros((L, HD), F32), sin], axis=0)
    return cos, sin


def _pad_lanes(v, off=0):
    return jnp.zeros((1, HD), F32).at[0, off:off + v.shape[0]].set(v)


def _local_step(x, ctx, tgt, mod_lat, mod_ctx, w_in, shards, small):
    N, L = x.shape[0], ctx.shape[0]
    T = N + L
    bounds = ((0, L), (L, T))
    qw, kw, gw = small["q_norm_w"], small["k_norm_w"], small["gdn_norm_w"]
    conv_w, ffn_w, ffn_b, fnw = small["conv_qkv_w"], small["ffn_conv_w"], small["ffn_conv_b"], small["final_norm_w"]
    alog = _pad_lanes(small["a_log"].reshape(-1), 2 * GH)
    dtb = _pad_lanes(small["dt_bias"].reshape(-1), 2 * GH)
    cos, sin = _rope_tables(N, L)
    bt = T
    bnl = 256 if N % 1024 else 1024

    hc = _normmod_fwd(ctx, mod_ctx, 0, 1, name="normmod_ctx")
    hx = _normmod_fwd(x, mod_lat, 0, 1, name="normmod_x")
    h1 = jnp.concatenate([hc, hx], axis=0)
    proj = _mm(h1, w_in, name="mm_in", M=T, N=C_END, K=D, tb=True, bm=bt, bn=1024)
    aq, ak, av = _aprep_fwd(proj, cos, sin, qw, kw)
    (attn, attn32, lse), (up_g,) = _attn_fwd(aq, ak, av, L, _GatherTwoLevel([shards["w_up"]]))
    gq = _gprep_fwd(proj, conv_w, 0, bounds)
    gk = _gprep_fwd(proj, conv_w, 1, bounds)
    gv = _gprep_fwd(proj, conv_w, 2, bounds)
    bl = _bl_fwd(proj, alog, dtb)
    intra, (down_g, pa_g, pd_g, out_g) = _intra_fwd(
        gq, gk, gv, bl, L, _GatherTwoLevel([shards[n] for n in ("w_down", "w_pa", "w_pd", "w_out")]))
    w_up, w_down = up_g.reshape(2 * DFF, D), down_g.reshape(DFF, D)
    w_pa, w_pd, w_out = pa_g.reshape(D, D), pd_g.reshape(D, D), out_g.reshape(D, D)
    xinv, intra = intra[6], intra[:6]
    o, states = _scan_fwd(*intra, L)
    gdn = _gout_fwd(o, proj, gw, L)
    pa = _mm(attn, w_pa, name="mm_pa", M=N, N=D, K=D, bm=bnl)
    pd = _mm(gdn, w_pd, name="mm_pd", M=N, N=D, K=D, bm=bnl)
    y = _merge_fwd(pa, pd, proj, L)
    m = _mm(y, w_out, name="mm_out", M=N, N=D, K=D, bm=bnl)
    x1 = _resid_fwd(x, m, mod_lat, 2, name="resid1")
    h2 = _normmod_fwd(x1, mod_lat, 3, 4, name="normmod_x1")
    up = _mm(h2, w_up, name="mm_up", M=N, N=2 * DFF, K=D, tb=True, bm=bnl, bn=2 * DFF // 4)
    a = _ffn_fwd(up, ffn_w, ffn_b)
    dn = _mm(a, w_down, name="mm_down", M=N, N=D, K=DFF, bm=bnl)
    loss, dx2, ddn, dg2, dfnw = _head(x1, dn, mod_lat, fnw, tgt)

    da = _mm(ddn, w_down, name="mm_down_dx", M=N, N=DFF, K=D, tb=True, bm=bnl, bn=DFF // 2)
    g_down = _mm(a, ddn, name="mm_down_dw", M=DFF, N=D, K=N, ta=True, bm=DFF // 2, out_dtype=BF16)
    dup, d_ffn_w, d_ffn_b = _ffn_bwd(up, ffn_w, ffn_b, da)
    dh2 = _mm(dup, w_up, name="mm_up_dx", M=N, N=D, K=2 * DFF, bm=bnl, bk=2 * DFF // 4)
    g_up = _mm(dup, h2, name="mm_up_dw", M=2 * DFF, N=D, K=N, ta=True, bm=2 * DFF // 4, out_dtype=BF16)
    dx1, dsh2, dsc2 = _normmod_bwd(x1, mod_lat, 3, 4, dh2, 0, dx2, name="normmod_x1_bwd")
    dm, dg1 = _resid_bwd(dx1, m, mod_lat, 2, name="resid1_bwd")
    dy = _mm(dm, w_out, name="mm_out_dx", M=N, N=D, K=D, tb=True, bm=bnl)
    g_out = _mm(y, dm, name="mm_out_dw", M=D, N=D, K=N, ta=True, out_dtype=BF16)
    dpa, dpd, dproj = _merge_bwd(pa, pd, proj, dy, L)
    dattn = _mm(dpa, w_pa, name="mm_pa_dx", M=N, N=D, K=D, tb=True, bm=bnl)
    g_pa = _mm(attn, dpa, name="mm_pa_dw", M=D, N=D, K=N, ta=True, out_dtype=BF16)
    dgdn = _mm(dpd, w_pd, name="mm_pd_dx", M=N, N=D, K=D, tb=True, bm=bnl)
    g_pd = _mm(gdn, dpd, name="mm_pd_dw", M=D, N=D, K=N, ta=True, out_dtype=BF16)
    do, dproj, dgw = _gout_bwd(o, proj, gw, dgdn, dproj, L)
    cts, recv_a = _scan_bwd(*intra, states, do, L, _Exchange(
        [g_out.reshape(NDEV, D // NDEV, D), g_pa.reshape(NDEV, D // NDEV, D), g_pd.reshape(NDEV, D // NDEV, D)], True))
    (dgq, dgk, dgv, dbl), recv_b = _intra_bwd(gq, gk, gv, bl, xinv, cts, L, _Exchange(
        [g_up.reshape(NDEV, 2 * DFF // NDEV, D)], True))
    dproj, dwq = _gprep_bwd(proj, conv_w, 0, bounds, dgq, dproj)
    dproj, dwk = _gprep_bwd(proj, conv_w, 1, bounds, dgk, dproj)
    dproj, dwv = _gprep_bwd(proj, conv_w, 2, bounds, dgv, dproj)
    dproj, dalog, ddtb = _bl_bwd(proj, alog, dtb, dbl, dproj)
    (daq_h, dak_h, dav_h), recv_c = _attn_bwd(aq, ak, av, attn32, lse, dattn, L, _Exchange(
        [g_down.reshape(NDEV, DFF // NDEV, D)], True))
    recv = dict(zip(("w_out", "w_pa", "w_pd", "w_up", "w_down"), recv_a + recv_b + recv_c))
    dproj, dqw, dkw = _aprep_bwd(proj, cos, sin, qw, kw, daq_h, dak_h, dav_h, dproj, L)
    g_in = _mm(dproj, h1, name="mm_in_dw", M=C_END, N=D, K=T, ta=True, bm=1024, out_dtype=BF16)
    g_in = _unpad_columns(g_in).reshape(NDEV, W_END // NDEV, D)
    own_in = lax.dynamic_index_in_dim(g_in, _position()[3], axis=0, keepdims=False)
    *pending, token = _scatter_start(g_in, None, (0, D // 2), (), name="scatter_g_in_a_start")
    dh1 = _mm(dproj, w_in, name="mm_in_dx", M=T, N=D, K=C_END, bm=bt, bk=1024, after=(token,))
    grad_x, dsh1, dsc1 = _normmod_bwd(x, mod_lat, 0, 1, dh1, L, dx1, name="normmod_x_bwd")
    _, dcsh1, dcsc1 = _normmod_bwd(ctx, mod_ctx, 0, 1, dh1, 0, None, name="normmod_ctx_bwd")

    z1 = jnp.zeros((1, D), F32)
    dmod_lat = jnp.concatenate([dsh1, dsc1, dg1, dsh2, dsc2, dg2], axis=0)
    dmod_ctx = jnp.concatenate([dcsh1, dcsc1, z1, z1, z1, z1], axis=0)
    gsmall = {
        "q_norm_w": dqw, "k_norm_w": dkw, "gdn_norm_w": dgw,
        "conv_qkv_w": jnp.concatenate([dwq, dwk, dwv], axis=1),
        "a_log": dalog[0, 2 * GH:4 * GH], "dt_bias": ddtb[0, 2 * GH:4 * GH],
        "ffn_conv_w": d_ffn_w, "ffn_conv_b": d_ffn_b, "final_norm_w": dfnw,
    }
    return loss[0, 0], grad_x, (pending, own_in), recv, dmod_lat, dmod_ctx, gsmall


HBM = pl.BlockSpec(memory_space=pltpu.HBM)
ANYSPEC = pl.BlockSpec(memory_space=pl.ANY)


def _position():
    x, y, c = lax.axis_index("x"), lax.axis_index("y"), lax.axis_index("c")
    return x, y, c, 4 * x + 2 * y + c


def _peer(x, y, c, k):
    px = 1 - x if k & 4 else x
    py = 1 - y if k & 2 else y
    pc = 1 - c if k & 1 else c
    return (px, py, pc), 4 * px + 2 * py + pc


def _exchange(arrs, *, name, scatter):
    exch = _Exchange(arrs, scatter)
    n = exch.n

    def body(*refs):
        ins, outs, sems = refs[:n], refs[n:2 * n], refs[2 * n:]
        exch.start(ins, outs, sems)
        exch.finish(ins, outs, sems)

    outs = pl.pallas_call(body, name=name, out_shape=exch.out_shape, in_specs=[HBM] * n, out_specs=(HBM,) * n,
                          scratch_shapes=exch.scratch,
                          compiler_params=pltpu.CompilerParams(has_side_effects=True))(*arrs)
    return list(outs)


class _Exchange:
    def __init__(self, arrs, scatter):
        self.arrs, self.scatter, self.n = list(arrs), scatter, len(arrs)
        self.out_shape = tuple(_sds(a.shape if scatter else (NDEV,) + a.shape, a.dtype) for a in arrs)
        self.scratch = [pltpu.SemaphoreType.DMA((self.n, NDEV - 1)), pltpu.SemaphoreType.DMA((self.n, NDEV - 1)),
                        pltpu.SemaphoreType.DMA((self.n,))]

    def _copies(self, ins, outs, sems):
        send, recv, loc = sems
        x, y, c, me = _position()
        local = [pltpu.make_async_copy(ins[a].at[me] if self.scatter else ins[a], outs[a].at[me], loc.at[a])
                 for a in range(self.n)]
        remote = []
        for k in range(1, NDEV):
            peer, pid = _peer(x, y, c, k)
            for a in range(self.n):
                src = ins[a].at[pid] if self.scatter else ins[a]
                remote.append(pltpu.make_async_remote_copy(
                    src_ref=src, dst_ref=outs[a].at[me], send_sem=send.at[a, k - 1], recv_sem=recv.at[a, k - 1],
                    device_id=peer, device_id_type=MESH))
        return local, remote

    def start(self, ins, outs, sems):
        local, remote = self._copies(ins, outs, sems)
        for cp in local + remote:
            cp.start()

    def finish(self, ins, outs, sems):
        local, remote = self._copies(ins, outs, sems)
        for cp in remote:
            cp.wait()
        for cp in local:
            cp.wait()


class _GatherTwoLevel:
    scatter = False

    def __init__(self, arrs):
        self.arrs, self.n = list(arrs), len(arrs)
        self.out_shape = tuple(_sds((NDEV,) + a.shape, a.dtype) for a in arrs)
        self.scratch = [pltpu.SemaphoreType.DMA((self.n, NDEV - 1)), pltpu.SemaphoreType.DMA((self.n, NDEV - 1)),
                        pltpu.SemaphoreType.DMA((self.n,))]

    def _parts(self, ins, outs, sems):
        send, recv, loc = sems
        x, y, c, _ = _position()
        me, sibling = (x, y, c), (x, y, 1 - c)
        chips = [(1 - x, y), (x, 1 - y), (1 - x, 1 - y)]
        parts = []
        for a in range(self.n):
            slot = lambda px, py, pc, a=a: outs[a].at[4 * px + 2 * py + pc]

            def copy(k, owner, to, src=None, a=a, slot=slot):
                return pltpu.make_async_remote_copy(
                    src_ref=slot(*owner) if src is None else src, dst_ref=slot(*owner), send_sem=send.at[a, k],
                    recv_sem=recv.at[a, k], device_id=to, device_id_type=MESH)

            parts.append(dict(
                mine=pltpu.make_async_copy(ins[a], slot(*me), loc.at[a]),
                first=[copy(0, me, sibling, src=ins[a])] + [copy(1 + j, me, (*ch, c), src=ins[a]) for j, ch in enumerate(chips)],
                arrive=[copy(1 + j, (*ch, c), me) for j, ch in enumerate(chips)],
                passed=[copy(4 + j, (*ch, c), sibling) for j, ch in enumerate(chips)],
                rest=[copy(0, sibling, me)] + [copy(4 + j, (*ch, 1 - c), me) for j, ch in enumerate(chips)]))
        return parts

    def start(self, ins, outs, sems):
        for p in self._parts(ins, outs, sems):
            p["mine"].start()
            for cp in p["first"]:
                cp.start()

    def middle(self, ins, outs, sems):
        for p in self._parts(ins, outs, sems):
            for got, fwd in zip(p["arrive"], p["passed"]):
                got.wait_recv()
                fwd.start()

    def finish(self, ins, outs, sems):
        for p in self._parts(ins, outs, sems):
            for cp in p["rest"]:
                cp.wait_recv()
            for cp in p["first"] + p["passed"]:
                cp.wait_send()
            p["mine"].wait()


def _gather_two_level(blocks, *, name):
    exch = _GatherTwoLevel(blocks)
    n = exch.n

    def body(*refs):
        ins, outs, sems = refs[:n], refs[n:2 * n], refs[2 * n:]
        exch.start(ins, outs, sems)
        exch.middle(ins, outs, sems)
        exch.finish(ins, outs, sems)

    outs = pl.pallas_call(body, name=name, out_shape=exch.out_shape, in_specs=[HBM] * n, out_specs=(HBM,) * n,
                          scratch_shapes=exch.scratch,
                          compiler_params=pltpu.CompilerParams(has_side_effects=True))(*blocks)
    return list(outs)


SEM = pl.BlockSpec(memory_space=pltpu.SEMAPHORE)


def _scatter_copies(src_ref, land_ref, send_sems, recv_sems, cols):
    x, y, c, me = _position()
    span = (slice(None), pl.ds(*cols))
    copies = []
    for k in range(1, NDEV):
        peer, pid = _peer(x, y, c, k)
        copies.append(pltpu.make_async_remote_copy(
            src_ref=src_ref.at[pid].at[span], dst_ref=land_ref.at[me].at[span], send_sem=send_sems.at[k - 1],
            recv_sem=recv_sems.at[k - 1], device_id=peer, device_id_type=MESH))
    return copies


SPLIT_EFFECT = pltpu.SideEffectType.DATAFLOW_SIDE_EFFECTING


def _scatter_start(parts, land, cols, after, *, name):
    na = len(after)
    if land is None:
        land = lax.empty(parts.shape, parts.dtype)

    def body(src_ref, land_ref, *rest):
        send_sems, recv_sems, _, _, token = rest[na:]
        for cp in _scatter_copies(src_ref, land_ref, send_sems, recv_sems, cols):
            cp.start()
        token[...] = jnp.zeros_like(token)

    return pl.pallas_call(
        body, name=name,
        out_shape=(pltpu.SemaphoreType.DMA((NDEV - 1,)), pltpu.SemaphoreType.DMA((NDEV - 1,)),
                   pltpu.HBM(parts.shape, parts.dtype), pltpu.HBM(parts.shape, parts.dtype), _sds((8, HD))),
        in_specs=(HBM, HBM) + (pl.BlockSpec(memory_space=pl.ANY),) * na,
        out_specs=(SEM, SEM, HBM, HBM, pl.BlockSpec(memory_space=pltpu.VMEM)),
        input_output_aliases={0: 2, 1: 3}, compiler_params=pltpu.CompilerParams(has_side_effects=SPLIT_EFFECT),
    )(pltpu.with_memory_space_constraint(parts, pltpu.HBM), pltpu.with_memory_space_constraint(land, pltpu.HBM), *after)


def _scatter_wait(send_sems, recv_sems, src_thru, land_thru, cols, after, *, name):
    na = len(after)

    def body(src_ref, land_ref, send_sems, recv_sems, *rest):
        for cp in _scatter_copies(src_ref, land_ref, send_sems, recv_sems, cols):
            cp.wait_send()
            cp.wait_recv()

    return pl.pallas_call(
        body, name=name,
        out_shape=(pltpu.HBM(src_thru.shape, src_thru.dtype), pltpu.HBM(land_thru.shape, land_thru.dtype)),
        in_specs=(HBM, HBM, SEM, SEM) + (pl.BlockSpec(memory_space=pl.ANY),) * na, out_specs=(HBM, HBM),
        input_output_aliases={0: 0, 1: 1}, compiler_params=pltpu.CompilerParams(has_side_effects=SPLIT_EFFECT),
    )(src_thru, land_thru, send_sems, recv_sems, *after)


def _cast_bf16(ws, *, name):
    k = len(ws)

    def body(*refs):
        for w_ref, o_ref in zip(refs[:k], refs[k:]):
            o_ref[...] = w_ref[...].astype(BF16)

    return _call(body, name=name, out_shape=tuple(_sds(w.shape, BF16) for w in ws), vmem=VMEM_BIG)(*ws)


def _sum_slots(a, *, name):
    _, R, C = a.shape

    def body(a_ref, o_ref):
        s = a_ref[0]
        for d in range(1, NDEV):
            s = s + a_ref[d]
        o_ref[...] = s

    return _call(body, name=name, out_shape=_sds((R, C)))(a)


MODROWS = 16


def _mod_fwd(c9, w, b):
    cols = w.shape[1]

    def body(c_ref, w_ref, b_ref, o_ref):
        o_ref[...] = _nn(_silu(c_ref[...]), w_ref[...]) + b_ref[...]

    return _call(body, name="mod_fwd", out_shape=_sds((MODROWS, cols)))(c9, w, b)


def _mod_bwd(c9, dmy, dall, w):
    cols = w.shape[1]

    def body(c_ref, dmy_ref, dall_ref, w_ref, gw_ref, gb_ref, cp_ref):
        sc = _silu(c_ref[...])
        rows = lax.broadcasted_iota(jnp.int32, (MODROWS, 1), 0)
        d = dmy_ref[...]
        d_ctx = jnp.where(rows == NDEV, d, 0.0)
        sc_ctx = jnp.where(rows == NDEV, sc, 0.0)
        outer = lax.dot_general(sc_ctx, d_ctx, (((0,), (0,)), ((), ())), precision=HI, preferred_element_type=F32)
        gw_ref[...] = _tn(jnp.where(rows < NDEV, sc, 0.0), jnp.where(rows < NDEV, d, 0.0)) + outer
        gb_ref[...] = jnp.sum(dall_ref[...], axis=0, keepdims=True)
        cp_ref[...] = jnp.sum(_nt(d_ctx, w_ref[...]), axis=0, keepdims=True)

    return _call(body, name="mod_bwd", out_shape=(_sds((D, cols)), _sds((1, 6 * D)), _sds((1, D))),
                 vmem=VMEM_BIG)(c9, dmy, dall, w)


def _cctx_finish(parts, c_ctx, after):
    VM = pl.BlockSpec(memory_space=pltpu.VMEM)

    def body(p_ref, c_ref, *rest):
        o_ref = rest[-1]
        s = p_ref[0]
        for d in range(1, NDEV):
            s = s + p_ref[d]
        _, vjp = jax.vjp(_silu, c_ref[...])
        o_ref[...] = vjp(s)[0]

    return _call(body, name="cctx_finish", out_shape=_sds((1, D)),
                 in_specs=[VM, VM] + [pl.BlockSpec(memory_space=pl.ANY)] * len(after))(parts, c_ctx, *after)


def _adamw_recv(w, recv, m, v, *, name, own=None):
    rows, cols = w.shape
    bc = 256
    c1 = 1.0 - B1 ** STEP
    c2 = 1.0 - B2 ** STEP
    has_own = own is not None

    def body(w_ref, r_ref, m_ref, v_ref, *rest):
        g_ref, d_ref, nm_ref, nv_ref = rest[-4:]
        me = _position()[3]

        def slot(d):
            return jnp.where(me == d, rest[0][...], r_ref[d]) if has_own else r_ref[d]

        gv = slot(0).astype(F32)
        for d in range(1, NDEV):
            gv = gv + slot(d).astype(F32)
        nm = B1 * m_ref[...] + (1.0 - B1) * gv
        nv = B2 * v_ref[...] + (1.0 - B2) * (gv * gv)
        g_ref[...] = gv
        d_ref[...] = -LR * ((nm / c1) / (jnp.sqrt(nv / c2) + AEPS) + WD * w_ref[...])
        nm_ref[...] = nm
        nv_ref[...] = nv

    blk = pl.BlockSpec((rows, bc), lambda j: (0, j))
    return _call(body, name=name, out_shape=(_sds((rows, cols)),) * 4, grid=(cols // bc,),
                 in_specs=[blk, pl.BlockSpec((NDEV, rows, bc), lambda j: (0, 0, j)), blk, blk] + [blk] * has_own,
                 out_specs=(blk,) * 4, sem=("parallel",), vmem=VMEM_BIG)(w, recv, m, v, *([own] if has_own else []))


P_LAT, P_CTX, P_FNW, P_FFNB, P_CONV, P_FFNW, P_MISC, P_ROWS = 0, 8, 16, 24, 32, 48, 72, 80


def _rows_of(v, nrows):
    flat = v.reshape(-1)
    return jnp.pad(flat, (0, nrows * D - flat.shape[0])).reshape(nrows, D)


def _by_columns(g):
    n, r, c = g.shape
    return jnp.transpose(g, (1, 0, 2)).reshape(r, n * c)


def kernel(x, c, ctx, c_ctx, w_mod, b_mod, w_in, q_norm_w, k_norm_w, conv_qkv_w, a_log, dt_bias, gdn_norm_w, w_pa, w_pd, w_out, w_up, ffn_conv_w, ffn_conv_b, w_down, final_norm_w, loss_target, m_c_ctx, m_w_mod, m_b_mod, m_w_in, m_q_norm_w, m_k_norm_w, m_conv_qkv_w, m_a_log, m_dt_bias, m_gdn_norm_w, m_w_pa, m_w_pd, m_w_out, m_w_up, m_ffn_conv_w, m_ffn_conv_b, m_w_down, m_final_norm_w, v_c_ctx, v_w_mod, v_b_mod, v_w_in, v_q_norm_w, v_k_norm_w, v_conv_qkv_w, v_a_log, v_dt_bias, v_gdn_norm_w, v_w_pa, v_w_pd, v_w_out, v_w_up, v_ffn_conv_w, v_ffn_conv_b, v_w_down, v_final_norm_w):
    _, _, _, me = _position()
    mcols = w_mod.shape[2]

    transposed = ("w_in", "w_up")
    big = {"w_in": w_in[0].T, "w_pa": w_pa[0], "w_pd": w_pd[0], "w_out": w_out[0], "w_up": w_up[0].T, "w_down": w_down[0]}
    names = list(big)
    shards = dict(zip(names, _cast_bf16([big[n] for n in names], name="cast_weights")))
    w_in_g, c_all, conv_g, ffnw_g = _gather_two_level([shards["w_in"], c, conv_qkv_w[0], ffn_conv_w[0]],
                                                      name="gather_w_in")
    w_in_full = w_in_g.reshape(W_END, D)
    w_in_pad = _pad_columns(w_in_full)

    c9 = jnp.concatenate([c_all.reshape(NDEV, D), jnp.pad(c_ctx[None], ((0, MODROWS - NDEV - 1), (0, 0)))], axis=0)
    b_loc = lax.dynamic_slice(b_mod, (0, me * mcols), (1, mcols))
    mod_all, = _exchange([_mod_fwd(c9, w_mod[0], b_loc)], name="gather_mod", scatter=False)
    mod_lat = lax.dynamic_index_in_dim(mod_all, me, axis=1, keepdims=False).reshape(6, D)
    mod_ctx = mod_all[:, NDEV, :].reshape(6, D)

    small = {"q_norm_w": q_norm_w, "k_norm_w": k_norm_w, "gdn_norm_w": gdn_norm_w, "a_log": a_log, "dt_bias": dt_bias,
             "conv_qkv_w": _by_columns(conv_g), "ffn_conv_w": _by_columns(ffnw_g), "ffn_conv_b": ffn_conv_b,
             "final_norm_w": final_norm_w[None]}
    loss_me, grad_x, (pending_in, own_in), recv, dmod_lat, dmod_ctx, gs = _local_step(
        x[0], ctx[0], loss_target[0], mod_lat, mod_ctx, w_in_pad, shards, small)

    moments = {"w_in": (m_w_in, v_w_in), "w_pa": (m_w_pa, v_w_pa), "w_pd": (m_w_pd, v_w_pd),
               "w_out": (m_w_out, v_w_out), "w_up": (m_w_up, v_w_up), "w_down": (m_w_down, v_w_down)}
    res = {}
    def finish(n, outs):
        return tuple((t.T if n in transposed else t)[None] for t in outs)

    def moment(t, n):
        return t[0].T if n in transposed else t[0]

    for n in recv:
        res[n] = finish(n, _adamw_recv(big[n], recv[n], moment(moments[n][0], n), moment(moments[n][1], n),
                                       name="adamw_" + n))

    misc = jnp.concatenate([gs["q_norm_w"][0], gs["k_norm_w"][0], gs["gdn_norm_w"][0], gs["a_log"], gs["dt_bias"],
                            loss_me[None]])
    pack = jnp.concatenate([_rows_of(dmod_lat, P_CTX - P_LAT), _rows_of(dmod_ctx, P_FNW - P_CTX),
                            _rows_of(gs["final_norm_w"], P_FFNB - P_FNW), _rows_of(gs["ffn_conv_b"], P_CONV - P_FFNB),
                            _rows_of(gs["conv_qkv_w"], P_FFNW - P_CONV), _rows_of(gs["ffn_conv_w"], P_MISC - P_FFNW),
                            _rows_of(misc, P_ROWS - P_MISC)], axis=0)
    pack_all, = _exchange([pack], name="gather_pack", scatter=False)
    tot = _sum_slots(pack_all, name="sum_pack")
    dall = jnp.concatenate([pack_all[:, P_LAT:P_LAT + 6, :].reshape(NDEV, 6 * D),
                            jnp.pad(tot[P_CTX:P_CTX + 6].reshape(1, 6 * D), ((0, MODROWS - NDEV - 1), (0, 0)))], axis=0)
    dmy = lax.dynamic_slice(dall, (0, me * mcols), (MODROWS, mcols))
    g_w_mod, g_b_mod, cpart = _mod_bwd(c9, dmy, dall, w_mod[0])
    cparts, = _exchange([cpart], name="gather_cctx", scatter=False)
    sems_a, land = pending_in[:2], pending_in[3]
    *sems_b, g_in_thru, land, token_b = _scatter_start(pending_in[2], land, (D // 2, D // 2), (cparts,),
                                                       name="scatter_g_in_b_start")
    g_c_ctx = _cctx_finish(cparts, c_ctx[None], (token_b,))[0]

    nconv, nffn = 3 * GH * HD, 2 * DFF
    conv_tot = tot[P_CONV:P_FFNW].reshape(-1)[:3 * nconv].reshape(3, nconv)
    ffnw_tot = tot[P_FFNW:P_MISC].reshape(-1)[:3 * nffn].reshape(3, nffn)
    mrow = tot[P_MISC]
    grads = {
        "c_ctx": g_c_ctx, "w_mod": g_w_mod[None], "b_mod": g_b_mod,
        "q_norm_w": mrow[None, 0:HD], "k_norm_w": mrow[None, HD:2 * HD], "gdn_norm_w": mrow[None, 2 * HD:3 * HD],
        "conv_qkv_w": lax.dynamic_slice(conv_tot, (0, me * (nconv // NDEV)), (3, nconv // NDEV))[None],
        "a_log": mrow[3 * HD:3 * HD + 2 * GH].reshape(1, 2, GH),
        "dt_bias": mrow[3 * HD + 2 * GH:3 * HD + 4 * GH].reshape(1, 2, GH),
        "ffn_conv_w": lax.dynamic_slice(ffnw_tot, (0, me * (nffn // NDEV)), (3, nffn // NDEV))[None],
        "ffn_conv_b": tot[P_FFNB:P_CONV].reshape(-1)[:nffn][None],
        "final_norm_w": tot[P_FNW],
    }
    loss = mrow[3 * HD + 4 * GH]
    given = {"c_ctx": (c_ctx, m_c_ctx, v_c_ctx), "w_mod": (w_mod, m_w_mod, v_w_mod), "b_mod": (b_mod, m_b_mod, v_b_mod),
             "q_norm_w": (q_norm_w, m_q_norm_w, v_q_norm_w), "k_norm_w": (k_norm_w, m_k_norm_w, v_k_norm_w),
             "conv_qkv_w": (conv_qkv_w, m_conv_qkv_w, v_conv_qkv_w), "a_log": (a_log, m_a_log, v_a_log),
             "dt_bias": (dt_bias, m_dt_bias, v_dt_bias), "gdn_norm_w": (gdn_norm_w, m_gdn_norm_w, v_gdn_norm_w),
             "ffn_conv_w": (ffn_conv_w, m_ffn_conv_w, v_ffn_conv_w), "ffn_conv_b": (ffn_conv_b, m_ffn_conv_b, v_ffn_conv_b),
             "final_norm_w": (final_norm_w, m_final_norm_w, v_final_norm_w)}
    res["w_mod"] = (grads["w_mod"],) + _adamw(w_mod, grads["w_mod"], m_w_mod, v_w_mod, name="adamw_w_mod")
    small_names = [n for n in given if n != "w_mod"]
    updates = _adamw_many([(given[n][0], grads[n], given[n][1], given[n][2]) for n in small_names], name="adamw_small")
    for n, upd in zip(small_names, updates):
        res[n] = (grads[n],) + upd

    g_in_thru, land = _scatter_wait(*sems_a, g_in_thru, land, (0, D // 2), [res[n][1] for n in res],
                                    name="scatter_g_in_a_wait")
    _, land = _scatter_wait(*sems_b, g_in_thru, land, (D // 2, D // 2), (), name="scatter_g_in_b_wait")
    res["w_in"] = finish("w_in", _adamw_recv(big["w_in"], land, moment(m_w_in, "w_in"), moment(v_w_in, "w_in"),
                                             name="adamw_w_in", own=own_in))

    order = ["c_ctx", "w_mod", "b_mod", "w_in", "q_norm_w", "k_norm_w", "conv_qkv_w", "a_log", "dt_bias", "gdn_norm_w",
             "w_pa", "w_pd", "w_out", "w_up", "ffn_conv_w", "ffn_conv_b", "w_down", "final_norm_w"]
    return (loss, grad_x[None], *[res[n][0] for n in order], *[res[n][1] for n in order],
            *[res[n][2] for n in order], *[res[n][3] for n in order])
```

```python
import functools
import math

import jax
import jax.numpy as jnp
from jax import lax
from jax.experimental import pallas as pl
from jax.experimental.pallas import tpu as pltpu

F32 = jnp.float32
BF16 = jnp.bfloat16
HI = lax.Precision.HIGHEST
MESH = pl.DeviceIdType.MESH

NDEV = 8
D = 1024
HD = 128
AH, AKV, GRP = 8, 2, 4
GH = 8
CH = 64
DFF = 2816
GRID_W = 64
EPS = 1e-6
ROPE_THETA = 10000.0
LOG2E = math.log2(math.e)
C_KV, C_AQ, C_QKV, C_BL, C_Z, C_GATE, C_END = 0, 512, 1536, 4608, 5120, 6144, 8192
W_QKV, W_AQ, W_Z, W_END = 512, 3616, 4640, 7712


def _pad_columns(w):
    zeros = jnp.zeros((C_Z - C_QKV - (W_AQ - W_QKV), D), w.dtype)
    return jnp.concatenate([w[:W_QKV], w[W_AQ:W_Z], w[W_QKV:W_AQ], zeros, w[W_Z:]], axis=0)


def _unpad_columns(g):
    return jnp.concatenate([g[:C_AQ], g[C_QKV:C_QKV + W_AQ - W_QKV], g[C_AQ:C_QKV], g[C_Z:]], axis=0)
LR, B1, B2, AEPS, WD, STEP = 0.001, 0.9, 0.999, 1e-08, 0.01, 10
VMEM_BIG = 56 * 1024 * 1024
INTRA_FWD_CHUNKS = 36
INTRA_BWD_CHUNKS = 36


def _call(body, *, name, out_shape, grid=None, in_specs=None, out_specs=None, scratch=(), sem=None,
          vmem=None, aliases=None):
    params = {}
    if sem is not None:
        params["dimension_semantics"] = sem
    if vmem is not None:
        params["vmem_limit_bytes"] = vmem
    kw = {}
    if grid is not None:
        kw["grid"] = grid
    if in_specs is not None:
        kw["in_specs"] = in_specs
    if out_specs is not None:
        kw["out_specs"] = out_specs
    if aliases:
        kw["input_output_aliases"] = aliases
    return pl.pallas_call(body, name=name, out_shape=out_shape, scratch_shapes=list(scratch),
                          compiler_params=pltpu.CompilerParams(**params), **kw)


def _call_carrying(body, exch, *, name, out_shape, grid, in_specs, out_specs, scratch=(), vmem=None):
    n, nin, nout, nscr = exch.n, len(in_specs), len(out_shape), len(scratch)
    steps = math.prod(grid)
    mid = (2 * steps) // 3

    def wrapped(*refs):
        ins, cins = refs[:nin], refs[nin:nin + n]
        outs, couts = refs[nin + n:nin + n + nout], refs[nin + n + nout:nin + 2 * n + nout]
        scr, sems = refs[nin + 2 * n + nout:nin + 2 * n + nout + nscr], refs[nin + 2 * n + nout + nscr:]
        ids = [pl.program_id(i) for i in range(len(grid))]
        first = functools.reduce(jnp.logical_and, [i == 0 for i in ids])
        last = functools.reduce(jnp.logical_and, [i == g - 1 for i, g in zip(ids, grid)])

        @pl.when(first)
        def _():
            exch.start(cins, couts, sems)

        if hasattr(exch, "middle"):
            linear = functools.reduce(lambda acc, ig: acc * ig[1] + ig[0], zip(ids, grid), 0)

            @pl.when(linear == mid)
            def _():
                exch.middle(cins, couts, sems)

        body(*ins, *outs, *scr)

        @pl.when(last)
        def _():
            exch.finish(cins, couts, sems)

    params = {"dimension_semantics": ("arbitrary",) * len(grid)}
    if vmem is not None:
        params["vmem_limit_bytes"] = vmem
    fn = pl.pallas_call(wrapped, name=name, out_shape=tuple(out_shape) + exch.out_shape, grid=grid,
                        in_specs=list(in_specs) + [HBM] * n, out_specs=tuple(out_specs) + (HBM,) * n,
                        scratch_shapes=list(scratch) + exch.scratch, compiler_params=pltpu.CompilerParams(**params))

    def run(*args):
        res = fn(*args, *exch.arrs)
        return res[:nout], list(res[nout:])

    return run


def _sds(shape, dtype=F32):
    return jax.ShapeDtypeStruct(tuple(shape), dtype)


def _dot(a, b, ca, cb):
    return lax.dot_general(a.astype(BF16), b.astype(BF16), (((ca,), (cb,)), ((), ())),
                           preferred_element_type=F32)


@jax.custom_vjp
def _nn(a, b):
    return _dot(a, b, 1, 0)


@jax.custom_vjp
def _nt(a, b):
    return _dot(a, b, 1, 1)


@jax.custom_vjp
def _tn(a, b):
    return _dot(a, b, 0, 0)


_nn.defvjp(lambda a, b: (_nn(a, b), (a, b)), lambda r, g: (_nt(g, r[1]), _tn(r[0], g)))
_nt.defvjp(lambda a, b: (_nt(a, b), (a, b)), lambda r, g: (_nn(g, r[1]), _tn(g, r[0])))
_tn.defvjp(lambda a, b: (_tn(a, b), (a, b)), lambda r, g: (_nt(r[1], g), _nn(r[0], g)))


def _mdot(a, b):
    return jnp.dot(a, b, precision=lax.Precision.HIGH, preferred_element_type=F32)


def _maskdot(mask, a, cm):
    hi = a.astype(BF16)
    r = a - hi.astype(F32)
    mid = r.astype(BF16)
    lo = (r - mid.astype(F32)).astype(BF16)
    mb = mask.astype(BF16)
    dims = (((cm,), (0,)), ((), ()))
    return (lax.dot_general(mb, hi, dims, preferred_element_type=F32)
            + lax.dot_general(mb, mid, dims, preferred_element_type=F32)
            + lax.dot_general(mb, lo, dims, preferred_element_type=F32))


@jax.custom_vjp
def _mask_nn(mask, a):
    return _maskdot(mask, a, 1)


_mask_nn.defvjp(lambda mask, a: (_maskdot(mask, a, 1), mask),
                lambda mask, g: (jnp.zeros_like(mask), _maskdot(mask, g, 0)))


@jax.custom_vjp
def _saved_inverse(lmat, x):
    return x


def _saved_inverse_bwd(x, g):
    t = lax.dot_general(x, g, (((0,), (0,)), ((), ())), precision=lax.Precision.HIGH, preferred_element_type=F32)
    dl = lax.dot_general(t, x, (((1,), (1,)), ((), ())), precision=lax.Precision.HIGH, preferred_element_type=F32)
    return -dl, jnp.zeros_like(x)


_saved_inverse.defvjp(lambda lmat, x: (x, x), _saved_inverse_bwd)


def _row_ids(shape):
    return lax.broadcasted_iota(jnp.int32, shape, 0)


def _shift_rows(x, down, bounds):
    n = x.shape[0]
    rows = _row_ids(x.shape)
    y = pltpu.roll(x, 1 if down else n - 1, 0)
    edge = functools.reduce(jnp.logical_or, [rows == (s if down else e - 1) for s, e in bounds])
    return jnp.where(edge, 0.0, y)


def _make_shift(bounds):
    @jax.custom_vjp
    def down(x):
        return _shift_rows(x, True, bounds)

    @jax.custom_vjp
    def up(x):
        return _shift_rows(x, False, bounds)

    down.defvjp(lambda x: (down(x), None), lambda _, g: (up(g),))
    up.defvjp(lambda x: (up(x), None), lambda _, g: (down(g),))
    return down, up


@jax.custom_vjp
def _swap32(x):
    lane = lax.broadcasted_iota(jnp.int32, x.shape, x.ndim - 1)
    return jnp.where((lane % 64) < 32, pltpu.roll(x, HD - 32, x.ndim - 1), pltpu.roll(x, 32, x.ndim - 1))


_swap32.defvjp(lambda x: (_swap32(x), None), lambda _, g: (_swap32(g),))


def _rms(x):
    return x * lax.rsqrt(jnp.mean(x * x, axis=-1, keepdims=True) + EPS)


def _silu(x):
    return x * jax.nn.sigmoid(x)


def _mm(a, b, *, name, M, N, K, ta=False, tb=False, out_dtype=F32, bm=None, bn=None, bk=None, after=()):
    bm, bn, bk = bm or M, bn or N, bk or K
    assert M % bm == 0 and N % bn == 0 and K % bk == 0, (name, M, N, K, bm, bn, bk)
    nk = K // bk
    ca, cb = (0 if ta else 1), (1 if tb else 0)
    na = len(after)

    def body(a_ref, b_ref, *rest):
        o_ref, acc = rest[na], rest[na + 1:]
        r = _dot(a_ref[...], b_ref[...], ca, cb)
        if nk == 1:
            o_ref[...] = r.astype(out_dtype)
        else:
            acc_ref, = acc
            k = pl.program_id(2)

            @pl.when(k == 0)
            def _():
                acc_ref[...] = r

            @pl.when(k > 0)
            def _():
                acc_ref[...] += r

            @pl.when(k == nk - 1)
            def _():
                o_ref[...] = acc_ref[...].astype(out_dtype)

    a_spec = pl.BlockSpec((bk, bm), lambda i, j, k: (k, i)) if ta else pl.BlockSpec((bm, bk), lambda i, j, k: (i, k))
    b_spec = pl.BlockSpec((bn, bk), lambda i, j, k: (j, k)) if tb else pl.BlockSpec((bk, bn), lambda i, j, k: (k, j))
    return _call(body, name=name, out_shape=_sds((M, N), out_dtype), grid=(M // bm, N // bn, nk),
                 in_specs=[a_spec, b_spec] + [pl.BlockSpec(memory_space=pl.ANY)] * na,
                 out_specs=pl.BlockSpec((bm, bn), lambda i, j, k: (i, j)),
                 scratch=[pltpu.VMEM((bm, bn), F32)] if nk > 1 else [],
                 sem=("parallel", "parallel", "arbitrary"), vmem=VMEM_BIG)(a, b, *after)


def _normmod_fn(x, sh, sc):
    return _rms(x) * (1.0 + sc) + sh


def _normmod_fwd(x, mod, i_sh, i_sc, *, name, br=256):
    R = x.shape[0]

    def body(x_ref, mod_ref, o_ref):
        o_ref[...] = _normmod_fn(x_ref[...], mod_ref[i_sh:i_sh + 1, :], mod_ref[i_sc:i_sc + 1, :]).astype(BF16)

    return _call(body, name=name, out_shape=_sds((R, D), BF16), grid=(R // br,),
                 in_specs=[pl.BlockSpec((br, D), lambda i: (i, 0)), pl.BlockSpec((6, D), lambda i: (0, 0))],
                 out_specs=pl.BlockSpec((br, D), lambda i: (i, 0)), sem=("parallel",))(x, mod)


def _normmod_bwd(x, mod, i_sh, i_sc, dh, dh_off, res, *, name, br=256):
    R = x.shape[0]
    ob = dh_off // br
    has_res = res is not None

    def body(x_ref, mod_ref, dh_ref, *rest):
        if has_res:
            res_ref, dx_ref, dsh_ref, dsc_ref = rest
        else:
            dx_ref, dsh_ref, dsc_ref = rest
        sh, sc = mod_ref[i_sh:i_sh + 1, :], mod_ref[i_sc:i_sc + 1, :]
        _, vjp = jax.vjp(_normmod_fn, x_ref[...], sh, sc)
        dx, dsh, dsc = vjp(dh_ref[...])
        dx_ref[...] = dx + res_ref[...] if has_res else dx

        @pl.when(pl.program_id(0) == 0)
        def _():
            dsh_ref[...] = jnp.zeros_like(dsh_ref)
            dsc_ref[...] = jnp.zeros_like(dsc_ref)

        dsh_ref[...] += dsh
        dsc_ref[...] += dsc

    row = pl.BlockSpec((br, D), lambda i: (i, 0))
    vec = pl.BlockSpec((1, D), lambda i: (0, 0))
    ins = [row, pl.BlockSpec((6, D), lambda i: (0, 0)), pl.BlockSpec((br, D), lambda i: (i + ob, 0))]
    args = [x, mod, dh]
    if has_res:
        ins.append(row)
        args.append(res)
    return _call(body, name=name, out_shape=(_sds((R, D)), _sds((1, D)), _sds((1, D))), grid=(R // br,),
                 in_specs=ins, out_specs=(row, vec, vec), sem=("arbitrary",))(*args)


def _rope(x, cos, sin):
    return x * cos + _swap32(x) * sin


def _aprep_fn(qs, ks, cos, sin, qw, kw):
    return ([_rope(_rms(q) * qw, cos, sin) for q in qs], [_rope(_rms(k) * kw, cos, sin) for k in ks])


def _aprep_fwd(proj, cos, sin, qw, kw, *, br=256):
    T = proj.shape[0]

    def body(x_ref, cos_ref, sin_ref, qw_ref, kw_ref, q_ref, k_ref, v_ref):
        qs = [x_ref[:, C_AQ + h * HD:C_AQ + (h + 1) * HD] for h in range(AH)]
        ks = [x_ref[:, h * HD:(h + 1) * HD] for h in range(AKV)]
        qo, ko = _aprep_fn(qs, ks, cos_ref[...], sin_ref[...], qw_ref[...], kw_ref[...])
        for h in range(AH):
            q_ref[h] = qo[h].astype(BF16)
        for h in range(AKV):
            k_ref[h] = ko[h].astype(BF16)
            v_ref[h] = x_ref[:, (AKV + h) * HD:(AKV + h + 1) * HD].astype(BF16)

    tab = pl.BlockSpec((br, HD), lambda i: (i, 0))
    vec = pl.BlockSpec((1, HD), lambda i: (0, 0))
    return _call(body, name="aprep_fwd",
                 out_shape=(_sds((AH, T, HD), BF16), _sds((AKV, T, HD), BF16), _sds((AKV, T, HD), BF16)),
                 grid=(T // br,),
                 in_specs=[pl.BlockSpec((br, C_QKV), lambda i: (i, 0)), tab, tab, vec, vec],
                 out_specs=(pl.BlockSpec((AH, br, HD), lambda i: (0, i, 0)),
                            pl.BlockSpec((AKV, br, HD), lambda i: (0, i, 0)),
                            pl.BlockSpec((AKV, br, HD), lambda i: (0, i, 0))),
                 sem=("parallel",))(proj, cos, sin, qw, kw)


def _aprep_bwd(proj, cos, sin, qw, kw, dq, dk, dv, dproj, L, *, br=256):
    T = proj.shape[0]
    lb = L // br

    def body(x_ref, cos_ref, sin_ref, qw_ref, kw_ref, dq_ref, dk_ref, dv_ref, _, dx_ref, dqw_ref, dkw_ref):
        i = pl.program_id(0)
        qs = [x_ref[:, C_AQ + h * HD:C_AQ + (h + 1) * HD] for h in range(AH)]
        ks = [x_ref[:, h * HD:(h + 1) * HD] for h in range(AKV)]
        _, vjp = jax.vjp(_aprep_fn, qs, ks, cos_ref[...], sin_ref[...], qw_ref[...], kw_ref[...])
        is_lat = i >= lb
        dqs = [jnp.where(is_lat, dq_ref[h], 0.0) for h in range(AH)]
        dks = [dk_ref[h] for h in range(AKV)]
        gq, gk, _, _, gqw, gkw = vjp((dqs, dks))
        for h in range(AH):
            dx_ref[:, C_AQ + h * HD:C_AQ + (h + 1) * HD] = gq[h].astype(BF16)
        for h in range(AKV):
            dx_ref[:, h * HD:(h + 1) * HD] = gk[h].astype(BF16)
            dx_ref[:, (AKV + h) * HD:(AKV + h + 1) * HD] = dv_ref[h].astype(BF16)

        @pl.when(i == 0)
        def _():
            dqw_ref[...] = jnp.zeros_like(dqw_ref)
            dkw_ref[...] = jnp.zeros_like(dkw_ref)

        dqw_ref[...] += gqw
        dkw_ref[...] += gkw

    tab = pl.BlockSpec((br, HD), lambda i: (i, 0))
    vec = pl.BlockSpec((1, HD), lambda i: (0, 0))
    kvb = pl.BlockSpec((AKV, br, HD), lambda i: (0, i, 0))
    blk = pl.BlockSpec((br, C_QKV), lambda i: (i, 0))
    return _call(body, name="aprep_bwd", out_shape=(_sds(dproj.shape, BF16), _sds((1, HD)), _sds((1, HD))),
                 grid=(T // br,),
                 in_specs=[blk, tab, tab, vec, vec,
                           pl.BlockSpec((AH, br, HD), lambda i: (0, jnp.maximum(i - lb, 0), 0)), kvb, kvb, ANYSPEC],
                 out_specs=(blk, vec, vec), aliases={8: 0},
                 sem=("arbitrary",))(proj, cos, sin, qw, kw, dq, dk, dv, dproj)


def _attn_grad(q, k, v, o, lse2, do):
    scale = HD ** -0.5
    p = jnp.exp2(_dot(q, k, 1, 1) * (scale * LOG2E) - lse2)
    dp = _dot(do, v, 1, 1)
    ds = p * (dp - jnp.sum(do * o, axis=-1, keepdims=True)) * scale
    return _dot(ds, k, 1, 0), _dot(ds, q, 0, 0), _dot(p, do, 0, 0)


ATTN_KEYS = 256


def _attn_fwd(q, k, v, L, exch, *, bq=128):
    T = q.shape[1]
    N = T - L
    lb = L // bq
    assert T % ATTN_KEYS == 0
    scale = HD ** -0.5
    heads = range(GRP)

    def body(q_ref, k_ref, v_ref, o_ref, o32_ref, lse_ref):
        qs = [q_ref[g] for g in heads]
        m = [jnp.full((bq, 1), -jnp.inf, F32) for _ in heads]
        l = [jnp.zeros((bq, 1), F32) for _ in heads]
        acc = [jnp.zeros((bq, HD), F32) for _ in heads]
        for c in range(T // ATTN_KEYS):
            kc, vc = k_ref[c * ATTN_KEYS:(c + 1) * ATTN_KEYS, :], v_ref[c * ATTN_KEYS:(c + 1) * ATTN_KEYS, :]
            s = [_dot(qs[g], kc, 1, 1) * (scale * LOG2E) for g in heads]
            m_new = [jnp.maximum(m[g], jnp.max(s[g], axis=-1, keepdims=True)) for g in heads]
            alpha = [jnp.exp2(m[g] - m_new[g]) for g in heads]
            p = [jnp.exp2(s[g] - m_new[g]) for g in heads]
            l = [l[g] * alpha[g] + jnp.sum(p[g], axis=-1, keepdims=True) for g in heads]
            acc = [acc[g] * alpha[g] + _dot(p[g], vc, 1, 0) for g in heads]
            m = m_new
        for g in heads:
            o = acc[g] / l[g]
            o_ref[:, g * HD:(g + 1) * HD] = o.astype(BF16)
            o32_ref[:, g * HD:(g + 1) * HD] = o
            lse_ref[g] = jnp.broadcast_to(m[g] + jnp.log2(l[g]), (bq, HD))

    kvb = pl.BlockSpec((None, T, HD), lambda g, i: (g, 0, 0))
    ob = pl.BlockSpec((bq, GRP * HD), lambda g, i: (i, g))
    return _call_carrying(
        body, exch, name="attn_fwd",
        out_shape=(_sds((N, AH * HD), BF16), _sds((N, AH * HD)), _sds((AH, N, HD))), grid=(AKV, N // bq),
        in_specs=[pl.BlockSpec((GRP, bq, HD), lambda g, i: (g, i + lb, 0)), kvb, kvb],
        out_specs=(ob, ob, pl.BlockSpec((GRP, bq, HD), lambda g, i: (g, i, 0))), vmem=VMEM_BIG)(q, k, v)


def _attn_bwd(q, k, v, o32, lse, do, L, exch, *, bq=128):
    T = q.shape[1]
    N = T - L
    lb = L // bq

    def body(q_ref, k_ref, v_ref, o_ref, lse_ref, do_ref, dq_ref, dk_ref, dv_ref):
        rows = lambda r: jnp.concatenate([r[:, g * HD:(g + 1) * HD] for g in range(GRP)], axis=0)
        lse = jnp.max(lse_ref[...].reshape(GRP * bq, HD), axis=-1, keepdims=True)
        dq, dk, dv = _attn_grad(q_ref[...].reshape(GRP * bq, HD), k_ref[...], v_ref[...], rows(o_ref), lse, rows(do_ref))
        dq_ref[...] = dq.reshape(GRP, bq, HD)

        @pl.when(pl.program_id(1) == 0)
        def _():
            dk_ref[...] = jnp.zeros_like(dk_ref)
            dv_ref[...] = jnp.zeros_like(dv_ref)

        dk_ref[...] += dk
        dv_ref[...] += dv

    kvb = pl.BlockSpec((None, T, HD), lambda g, i: (g, 0, 0))
    qb = pl.BlockSpec((GRP, bq, HD), lambda g, i: (g, i + lb, 0))
    hb = pl.BlockSpec((GRP, bq, HD), lambda g, i: (g, i, 0))
    ob = pl.BlockSpec((bq, GRP * HD), lambda g, i: (i, g))
    return _call_carrying(body, exch, name="attn_bwd",
                          out_shape=(_sds((AH, N, HD)), _sds((AKV, T, HD)), _sds((AKV, T, HD))), grid=(AKV, N // bq),
                          in_specs=[qb, kvb, kvb, ob, hb, ob], out_specs=(hb, kvb, kvb),
                          vmem=VMEM_BIG)(q, k, v, o32, lse, do)


def _gprep_fn(kind, shifts, x, w):
    down, up = shifts
    y = down(x) * w[0:1, :] + x * w[1:2, :] + up(x) * w[2:3, :]
    a = _silu(y)
    if kind == 2:
        return a
    a = a * lax.rsqrt(jnp.sum(a * a, axis=-1, keepdims=True) + EPS)
    return a * (HD ** -0.5) if kind == 0 else a


def _gprep_fwd(proj, conv_w, kind, bounds):
    T = proj.shape[0]
    shifts = _make_shift(bounds)
    cb = C_QKV // HD + kind * GH

    def body(x_ref, w_ref, o_ref):
        o_ref[...] = _gprep_fn(kind, shifts, x_ref[...], w_ref[...])

    return _call(body, name=f"gprep_fwd{kind}", out_shape=_sds((GH, T, HD)), grid=(GH,),
                 in_specs=[pl.BlockSpec((T, HD), lambda h: (0, cb + h)),
                           pl.BlockSpec((3, HD), lambda h: (0, kind * GH + h))],
                 out_specs=pl.BlockSpec((None, T, HD), lambda h: (h, 0, 0)), sem=("parallel",))(proj, conv_w)


def _gprep_bwd(proj, conv_w, kind, bounds, dy, dproj):
    T = proj.shape[0]
    shifts = _make_shift(bounds)
    cb = C_QKV // HD + kind * GH

    def body(x_ref, w_ref, dy_ref, _, dx_ref, dw_ref):
        _, vjp = jax.vjp(functools.partial(_gprep_fn, kind, shifts), x_ref[...], w_ref[...])
        dx, dw = vjp(dy_ref[0] + dy_ref[1])
        dx_ref[...] = dx.astype(BF16)
        dw_ref[...] = dw

    return _call(body, name=f"gprep_bwd{kind}", out_shape=(_sds(dproj.shape, BF16), _sds((3, GH * HD))), grid=(GH,),
                 in_specs=[pl.BlockSpec((T, HD), lambda h: (0, cb + h)),
                           pl.BlockSpec((3, HD), lambda h: (0, kind * GH + h)),
                           pl.BlockSpec((2, None, T, HD), lambda h: (0, h, 0, 0)), ANYSPEC],
                 out_specs=(pl.BlockSpec((T, HD), lambda h: (0, cb + h)), pl.BlockSpec((3, HD), lambda h: (0, h))),
                 aliases={3: 0}, sem=("parallel",))(proj, conv_w, dy, dproj)


def _bl_fn(x, alog, dtb):
    lane = lax.broadcasted_iota(jnp.int32, x.shape, 1)
    beta = jax.nn.sigmoid(x)
    z = x + dtb
    sp = jnp.maximum(z, 0.0) + jnp.log1p(jnp.exp(-jnp.abs(z)))
    la = -jnp.exp(alog) * sp
    return jnp.where(lane < 2 * GH, beta, jnp.where(lane < 4 * GH, la, 0.0))


def _bl_fwd(proj, alog, dtb, *, br=256):
    T = proj.shape[0]

    def body(x_ref, a_ref, d_ref, o_ref):
        o_ref[...] = _bl_fn(x_ref[...], a_ref[...], d_ref[...])

    vec = pl.BlockSpec((1, HD), lambda i: (0, 0))
    return _call(body, name="bl_fwd", out_shape=_sds((T, HD)), grid=(T // br,),
                 in_specs=[pl.BlockSpec((br, HD), lambda i: (i, C_BL // HD)), vec, vec],
                 out_specs=pl.BlockSpec((br, HD), lambda i: (i, 0)), sem=("parallel",))(proj, alog, dtb)


def _bl_bwd(proj, alog, dtb, dbl, dproj, *, br=256):
    T = proj.shape[0]
    wide = C_Z - C_BL

    def body(x_ref, a_ref, d_ref, g_ref, _, dx_ref, da_ref, dd_ref):
        g = g_ref[0, 0]
        for d in range(2):
            for h in range(GH):
                if d or h:
                    g = g + g_ref[d, h]
        _, vjp = jax.vjp(_bl_fn, x_ref[...], a_ref[...], d_ref[...])
        dx, da, dd = vjp(g)
        dx_ref[:, :HD] = dx.astype(BF16)
        dx_ref[:, HD:] = jnp.zeros((br, wide - HD), BF16)

        @pl.when(pl.program_id(0) == 0)
        def _():
            da_ref[...] = jnp.zeros_like(da_ref)
            dd_ref[...] = jnp.zeros_like(dd_ref)

        da_ref[...] += da
        dd_ref[...] += dd

    vec = pl.BlockSpec((1, HD), lambda i: (0, 0))
    return _call(body, name="bl_bwd", out_shape=(_sds(dproj.shape, BF16), _sds((1, HD)), _sds((1, HD))), grid=(T // br,),
                 in_specs=[pl.BlockSpec((br, HD), lambda i: (i, C_BL // HD)), vec, vec,
                           pl.BlockSpec((2, GH, br, HD), lambda i: (0, 0, i, 0)), ANYSPEC],
                 out_specs=(pl.BlockSpec((br, wide), lambda i: (i, C_BL // wide)), vec, vec), aliases={4: 0},
                 sem=("arbitrary",))(proj, alog, dtb, dbl, dproj)


def _chunk_masks(d):
    ii = lax.broadcasted_iota(jnp.int32, (CH, CH), 0)
    jj = lax.broadcasted_iota(jnp.int32, (CH, CH), 1)
    eye = (ii == jj).astype(F32)
    before = jnp.where(d == 0, (jj < ii).astype(F32), (jj > ii).astype(F32))
    return before, before + eye, eye


def _same_block(b):
    ii = lax.broadcasted_iota(jnp.int32, (CH, CH), 0)
    jj = lax.broadcasted_iota(jnp.int32, (CH, CH), 1)
    shift = b.bit_length() - 1
    return (jnp.right_shift(ii, shift) == jnp.right_shift(jj, shift)).astype(F32)


def _intra_fn(masks, sel_b, sel_l, qs, ks, vs, bls, xs=None):
    before, ateq, eye = masks
    inc = ateq > 0.0
    each = lambda f, *ls: [f(*t) for t in zip(*ls)]
    beta = each(lambda bl: jnp.sum(bl * sel_b, axis=-1, keepdims=True), bls)
    la = each(lambda bl: jnp.sum(bl * sel_l, axis=-1, keepdims=True), bls)
    gam = each(lambda a: _mask_nn(ateq, jnp.broadcast_to(a, (CH, HD))), la)
    gi = each(lambda g: g[:, :CH], gam)
    gj = each(lambda g: jnp.transpose(g)[:CH, :], gam)
    kq = each(lambda k, q: _nt(jnp.concatenate([k, q], axis=0), k), ks, qs)
    kk = each(lambda t: t[:CH], kq)
    qk = each(lambda t: t[CH:], kq)
    dec = each(lambda a, b: jnp.where(inc, jnp.exp(jnp.where(inc, a - b, 0.0)), 0.0), gi, gj)
    lmat = each(lambda b, d, m: before * (b * d * m), beta, dec, kk)
    if xs is None:
        same = lambda b: _same_block(b)
        l8 = each(lambda m: m * same(8), lmat)
        x = each(lambda m: eye - m, l8)
        p2 = each(lambda m: _mdot(m, m), l8)
        y = each(lambda a, b: _mdot(jnp.concatenate([a, b], axis=0), b), x, p2)
        x = each(lambda a, t: a + t[:CH], x, y)
        x = each(lambda a, t: a + _mdot(a, t[CH:]), x, y)
        for b in (8, 16, 32):
            below = same(2 * b) - same(b)
            x = each(lambda a, m: a - _mdot(a, _mdot(m * below, a)), x, lmat)
    else:
        x = each(_saved_inverse, lmat, xs)
    eg = each(jnp.exp, gam)
    uw = each(lambda a, b, v, e, k: _mdot(a, jnp.concatenate([b * v, (b * e) * k], axis=1)), x, beta, vs, eg, ks)
    u = each(lambda t: t[:, :HD], uw)
    w = each(lambda t: t[:, HD:], uw)
    tot = each(lambda a: jnp.sum(a, axis=0, keepdims=True), la)
    kd = each(lambda k, t, g: k * jnp.exp(t - g), ks, tot, gam)
    gl = each(lambda t: jnp.broadcast_to(jnp.exp(t), (1, HD)), tot)
    qd = each(lambda q, e: q * e, qs, eg)
    p = each(lambda d, m: d * m, dec, qk)
    return (u, w, kd, qd, p, gl, x) if xs is None else (u, w, kd, qd, p, gl)


def _dir_head_sel(d, h):
    lane = lax.broadcasted_iota(jnp.int32, (1, HD), 1)
    return (lane == d * GH + h).astype(F32), (lane == 2 * GH + d * GH + h).astype(F32)


def _intra_specs(T, G):
    nc = T // CH
    assert nc % G == 0
    qkv = pl.BlockSpec((None, G * CH, HD), lambda d, h, c: (h, c, 0))
    bl = pl.BlockSpec((G * CH, HD), lambda d, h, c: (c, 0))
    big = pl.BlockSpec((None, None, G * CH, HD), lambda d, h, c: (d, h, c, 0))
    pm = pl.BlockSpec((None, None, G * CH, CH), lambda d, h, c: (d, h, c, 0))
    gl = pl.BlockSpec((None, None, G, 1, HD), lambda d, h, c: (d, h, c, 0, 0))
    shapes = (_sds((2, GH, T, HD)),) + (_sds((2, GH, T, HD), BF16),) * 3 + (
        _sds((2, GH, T, CH), BF16), _sds((2, GH, nc, 1, HD)), _sds((2, GH, T, CH)))
    return nc, qkv, bl, big, pm, gl, shapes


def _chunks_per_step(T, most):
    nc = T // CH
    return max(g for g in range(1, most + 1) if nc % g == 0)


def _chunk_at(g, d, nc, ncc):
    pos = _visit_pos(g, d, nc, ncc)
    return pos, pl.ds(pl.multiple_of(pos * CH, CH), CH)


def _intra_fwd(q, k, v, bl, L, exch):
    T = q.shape[1]
    G = _chunks_per_step(T, INTRA_FWD_CHUNKS)
    nc, qkv_s, bl_s, big, pm, gl_s, shapes = _intra_specs(T, G)
    assert G == nc
    ncc = L // CH

    def body(q_ref, k_ref, v_ref, bl_ref, u_ref, w_ref, kd_ref, qd_ref, p_ref, gl_ref, x_ref):
        d, h = pl.program_id(0), pl.program_id(1)
        sb, sl = _dir_head_sel(d, h)
        rows = [slice(g * CH, (g + 1) * CH) for g in range(G)]
        outs = _intra_fn(_chunk_masks(d), sb, sl, *[[r[s, :] for s in rows] for r in (q_ref, k_ref, v_ref, bl_ref)])
        for g in range(G):
            pos, at = _chunk_at(g, d, nc, ncc)
            for r, o in zip((u_ref, w_ref, kd_ref, qd_ref, p_ref, x_ref), outs[:5] + outs[6:]):
                r[at, :] = o[g].astype(r.dtype)
            gl_ref[pos] = outs[5][g]

    return _call_carrying(body, exch, name="gdn_intra_fwd", out_shape=shapes, grid=(2, GH, nc // G),
                          in_specs=[qkv_s, qkv_s, qkv_s, bl_s], out_specs=(big, big, big, big, pm, gl_s, pm))(q, k, v, bl)


def _intra_bwd(q, k, v, bl, xinv, cts, L, exch):
    T = q.shape[1]
    G = _chunks_per_step(T, INTRA_BWD_CHUNKS)
    nc, qkv_s, bl_s, big, pm, gl_s, _ = _intra_specs(T, G)
    assert G == nc
    ncc = L // CH

    def body(q_ref, k_ref, v_ref, bl_ref, x_ref, du, dw, dkd, dqd, dp, dgl, dq_ref, dk_ref, dv_ref, dbl_ref):
        d, h = pl.program_id(0), pl.program_id(1)
        sb, sl = _dir_head_sel(d, h)
        rows = [slice(g * CH, (g + 1) * CH) for g in range(G)]
        places = [_chunk_at(g, d, nc, ncc) for g in range(G)]
        fn = functools.partial(_intra_fn, _chunk_masks(d), sb, sl, xs=[x_ref[at, :] for _, at in places])
        _, vjp = jax.vjp(fn, *[[r[s, :] for s in rows] for r in (q_ref, k_ref, v_ref, bl_ref)])
        cts = tuple([r[at, :] for _, at in places] for r in (du, dw, dkd, dqd, dp)) + ([dgl[pos] for pos, _ in places],)
        grads = vjp(cts)
        for g in range(G):
            for r, o in zip((dq_ref, dk_ref, dv_ref, dbl_ref), grads):
                r[rows[g], :] = o[g]

    return _call_carrying(body, exch, name="gdn_intra_bwd", out_shape=(_sds((2, GH, T, HD)),) * 4,
                          grid=(2, GH, nc // G), in_specs=[qkv_s, qkv_s, qkv_s, bl_s, pm, big, big, big, big, pm, gl_s],
                          out_specs=(big,) * 4)(q, k, v, bl, xinv, *cts)


def _scan_fn(s, u, w, kd, qd, p, gl):
    each = lambda f, *ls: [f(*t) for t in zip(*ls)]
    ws = each(_nn, w, s)
    delta = each(lambda a, b: a - b, u, ws)
    kdd = each(_tn, kd, delta)
    s_new = each(lambda g, a, b: g * a + b, gl, s, kdd)
    qs = each(_nn, qd, s)
    pd = each(_nn, p, delta)
    return each(lambda a, b: a + b, qs, pd), s_new


SCAN_BLOCK = 4


def _visit_pos(c, d, nc, ncc):
    back = ncc - 1 - c if c < ncc else ncc + (nc - 1 - c)
    return jnp.where(d == 0, c, back)


def _scan_specs(T, L, back):
    tb = SCAN_BLOCK * CH
    assert T % tb == 0 and L % tb == 0
    nb, ncb = T // tb, L // tb
    at = (lambda t: nb - 1 - t) if back else (lambda t: t)
    big = pl.BlockSpec((2, GH, tb, HD), lambda t: (0, 0, at(t), 0))
    pm = pl.BlockSpec((2, GH, tb, CH), lambda t: (0, 0, at(t), 0))
    gl = pl.BlockSpec((2, GH, SCAN_BLOCK, 1, HD), lambda t: (0, 0, at(t), 0, 0))
    st = pl.BlockSpec((2, GH, SCAN_BLOCK, HD, HD), lambda t: (0, 0, at(t), 0, 0))

    def natural(b):
        return jnp.where(b < ncb, ncb - 1 - b, nb - 1 - (b - ncb))

    do_specs = (pl.BlockSpec((GH, tb, HD), lambda t: (0, at(t), 0)),
                pl.BlockSpec((GH, tb, HD), lambda t: (0, natural(at(t)), 0)))
    return nb, big, pm, gl, st, do_specs


SCAN_STREAMS = [(d, h) for d in (0, 1) for h in range(GH)]


def _scan_fwd(u, w, kd, qd, p, gl, L):
    T = u.shape[2]
    nb, big, pm, gl_s, st, _ = _scan_specs(T, L, False)

    def body(u_ref, w_ref, kd_ref, qd_ref, p_ref, gl_ref, o_ref, st_ref, s_scr):
        @pl.when(pl.program_id(0) == 0)
        def _():
            s_scr[...] = jnp.zeros_like(s_scr)

        s = [s_scr[d, h] for d, h in SCAN_STREAMS]
        for i in range(SCAN_BLOCK):
            rows = slice(i * CH, (i + 1) * CH)
            for (d, h), sv in zip(SCAN_STREAMS, s):
                st_ref[d, h, i] = sv
            o, s = _scan_fn(s, *[[r[d, h, rows, :].astype(F32) for d, h in SCAN_STREAMS]
                                 for r in (u_ref, w_ref, kd_ref, qd_ref, p_ref)],
                            [gl_ref[d, h, i] for d, h in SCAN_STREAMS])
            for (d, h), ov in zip(SCAN_STREAMS, o):
                o_ref[d, h, rows, :] = ov
        for (d, h), sv in zip(SCAN_STREAMS, s):
            s_scr[d, h] = sv

    return _call(body, name="gdn_scan_fwd", out_shape=(_sds((2, GH, T, HD)), _sds((2, GH, T // CH, HD, HD))),
                 grid=(nb,), in_specs=[big, big, big, big, pm, gl_s], out_specs=(big, st),
                 scratch=[pltpu.VMEM((2, GH, HD, HD), F32)], sem=("arbitrary",), vmem=VMEM_BIG)(u, w, kd, qd, p, gl)


def _scan_bwd(u, w, kd, qd, p, gl, states, do, L, exch):
    T = u.shape[2]
    nb, big, pm, gl_s, st, do_specs = _scan_specs(T, L, True)

    def body(u_ref, w_ref, kd_ref, qd_ref, p_ref, gl_ref, st_ref, do0_ref, do1_ref,
             du_ref, dw_ref, dkd_ref, dqd_ref, dp_ref, dgl_ref, ds_scr):
        @pl.when(pl.program_id(0) == 0)
        def _():
            ds_scr[...] = jnp.zeros_like(ds_scr)

        ds = [ds_scr[d, h] for d, h in SCAN_STREAMS]
        for i in reversed(range(SCAN_BLOCK)):
            rows = slice(i * CH, (i + 1) * CH)
            mirror = slice((SCAN_BLOCK - 1 - i) * CH, (SCAN_BLOCK - i) * CH)
            _, vjp = jax.vjp(_scan_fn, [st_ref[d, h, i] for d, h in SCAN_STREAMS],
                             *[[r[d, h, rows, :].astype(F32) for d, h in SCAN_STREAMS]
                               for r in (u_ref, w_ref, kd_ref, qd_ref, p_ref)],
                             [gl_ref[d, h, i] for d, h in SCAN_STREAMS])
            dos = [do0_ref[h, rows, :] if d == 0 else do1_ref[h, mirror, :] for d, h in SCAN_STREAMS]
            ds, gu, gw, gkd, gqd, gp, ggl = vjp((dos, ds))
            for n, (d, h) in enumerate(SCAN_STREAMS):
                du_ref[d, h, rows, :] = gu[n]
                dw_ref[d, h, rows, :] = gw[n]
                dkd_ref[d, h, rows, :] = gkd[n]
                dqd_ref[d, h, rows, :] = gqd[n]
                dp_ref[d, h, rows, :] = gp[n]
                dgl_ref[d, h, i] = ggl[n]
        for (d, h), dv in zip(SCAN_STREAMS, ds):
            ds_scr[d, h] = dv

    return _call_carrying(
        body, exch, name="gdn_scan_bwd",
        out_shape=(_sds((2, GH, T, HD)),) * 4 + (_sds((2, GH, T, CH)), _sds((2, GH, T // CH, 1, HD))),
        grid=(nb,), in_specs=[big, big, big, big, pm, gl_s, st, *do_specs], out_specs=(big, big, big, big, pm, gl_s),
        scratch=[pltpu.VMEM((2, GH, HD, HD), F32)], vmem=VMEM_BIG)(u, w, kd, qd, p, gl, states, do, do)


def _gout_fn(o0, o1, z, gw):
    return _rms(o0 + o1) * gw * _silu(z)


def _backward_latent(o_ref, L):
    nl = (o_ref.shape[1] - L) // CH
    return jnp.concatenate([o_ref[1, L + (nl - 1 - j) * CH:L + (nl - j) * CH, :] for j in range(nl)], axis=0)


def _gout_fwd(o, proj, gw, L):
    T = o.shape[2]
    N = T - L
    ob = pl.BlockSpec((2, None, T, HD), lambda h: (0, h, 0, 0))

    def body(o_ref, z_ref, gw_ref, y_ref):
        y_ref[...] = _gout_fn(o_ref[0, L:, :], _backward_latent(o_ref, L), z_ref[L:, :], gw_ref[...]).astype(BF16)

    return _call(body, name="gout_fwd", out_shape=_sds((N, GH * HD), BF16), grid=(GH,),
                 in_specs=[ob, pl.BlockSpec((T, HD), lambda h: (0, C_Z // HD + h)), pl.BlockSpec((1, HD), lambda h: (0, 0))],
                 out_specs=pl.BlockSpec((N, HD), lambda h: (0, h)), sem=("parallel",))(o, proj, gw)


def _gout_bwd(o, proj, gw, dy, dproj, L):
    T = o.shape[2]
    N = T - L
    ob = pl.BlockSpec((2, None, T, HD), lambda h: (0, h, 0, 0))

    def body(o_ref, z_ref, gw_ref, dy_ref, _, do_ref, dz_ref, dgw_ref):
        _, vjp = jax.vjp(_gout_fn, o_ref[0, L:, :], _backward_latent(o_ref, L), z_ref[L:, :], gw_ref[...])
        g0, _, gz, ggw = vjp(dy_ref[...])
        do_ref[:L, :] = jnp.zeros((L, HD), F32)
        do_ref[L:, :] = g0
        dz_ref[:L, :] = jnp.zeros((L, HD), BF16)
        dz_ref[L:, :] = gz.astype(BF16)

        @pl.when(pl.program_id(0) == 0)
        def _():
            dgw_ref[...] = jnp.zeros_like(dgw_ref)

        dgw_ref[...] += ggw

    zb = pl.BlockSpec((T, HD), lambda h: (0, C_Z // HD + h))
    return _call(body, name="gout_bwd", out_shape=(_sds((GH, T, HD)), _sds(dproj.shape, BF16), _sds((1, HD))),
                 grid=(GH,),
                 in_specs=[ob, zb, pl.BlockSpec((1, HD), lambda h: (0, 0)), pl.BlockSpec((N, HD), lambda h: (0, h)), ANYSPEC],
                 out_specs=(pl.BlockSpec((None, T, HD), lambda h: (h, 0, 0)), zb, pl.BlockSpec((1, HD), lambda h: (0, 0))),
                 aliases={4: 1}, sem=("arbitrary",))(o, proj, gw, dy, dproj)


def _merge_fn(pa, pd, ga, gd):
    return jax.nn.sigmoid(ga) * pa + jax.nn.sigmoid(gd) * pd


def _merge_fwd(pa, pd, proj, L, *, br=256):
    N = pa.shape[0]
    lb = L // br
    row = pl.BlockSpec((br, D), lambda i: (i, 0))

    def body(pa_ref, pd_ref, ga_ref, gd_ref, y_ref):
        y_ref[...] = _merge_fn(pa_ref[...], pd_ref[...], ga_ref[...], gd_ref[...]).astype(BF16)

    return _call(body, name="merge_fwd", out_shape=_sds((N, D), BF16), grid=(N // br,),
                 in_specs=[row, row, pl.BlockSpec((br, D), lambda i: (i + lb, C_GATE // D)),
                           pl.BlockSpec((br, D), lambda i: (i + lb, C_GATE // D + 1))],
                 out_specs=row, sem=("parallel",))(pa, pd, proj, proj)


def _merge_bwd(pa, pd, proj, dy, L, *, br=256):
    N = pa.shape[0]
    T = N + L
    lb = L // br
    lrow = pl.BlockSpec((br, D), lambda i: (jnp.maximum(i - lb, 0), 0))

    def body(pa_ref, pd_ref, ga_ref, gd_ref, dy_ref, dpa_ref, dpd_ref, dg_ref):
        lat = pl.program_id(0) >= lb
        _, vjp = jax.vjp(_merge_fn, pa_ref[...], pd_ref[...], ga_ref[...], gd_ref[...])
        gpa, gpd, gga, ggd = vjp(dy_ref[...])
        dpa_ref[...] = gpa.astype(BF16)
        dpd_ref[...] = gpd.astype(BF16)
        dg_ref[:, :D] = jnp.where(lat, gga, 0.0).astype(BF16)
        dg_ref[:, D:] = jnp.where(lat, ggd, 0.0).astype(BF16)

    return _call(body, name="merge_bwd", out_shape=(_sds((N, D), BF16), _sds((N, D), BF16), _sds((T, C_END), BF16)),
                 grid=(T // br,),
                 in_specs=[lrow, lrow, pl.BlockSpec((br, D), lambda i: (i, C_GATE // D)),
                           pl.BlockSpec((br, D), lambda i: (i, C_GATE // D + 1)), lrow],
                 out_specs=(lrow, lrow, pl.BlockSpec((br, 2 * D), lambda i: (i, C_GATE // (2 * D)))),
                 sem=("arbitrary",))(pa, pd, proj, proj, dy)


def _resid_fwd(x, m, mod, i_g, *, name, br=256):
    R = x.shape[0]
    row = pl.BlockSpec((br, D), lambda i: (i, 0))

    def body(x_ref, m_ref, mod_ref, o_ref):
        o_ref[...] = x_ref[...] + mod_ref[i_g:i_g + 1, :] * m_ref[...]

    return _call(body, name=name, out_shape=_sds((R, D)), grid=(R // br,),
                 in_specs=[row, row, pl.BlockSpec((6, D), lambda i: (0, 0))], out_specs=row,
                 sem=("parallel",))(x, m, mod)


def _resid_bwd(dx, m, mod, i_g, *, name, br=256):
    R = dx.shape[0]
    row = pl.BlockSpec((br, D), lambda i: (i, 0))
    vec = pl.BlockSpec((1, D), lambda i: (0, 0))

    def body(dx_ref, m_ref, mod_ref, dm_ref, dg_ref):
        dxv = dx_ref[...]
        dm_ref[...] = (dxv * mod_ref[i_g:i_g + 1, :]).astype(BF16)

        @pl.when(pl.program_id(0) == 0)
        def _():
            dg_ref[...] = jnp.zeros_like(dg_ref)

        dg_ref[...] += jnp.sum(dxv * m_ref[...], axis=0, keepdims=True)

    return _call(body, name=name, out_shape=(_sds((R, D), BF16), _sds((1, D))), grid=(R // br,),
                 in_specs=[row, row, pl.BlockSpec((6, D), lambda i: (0, 0))], out_specs=(row, vec),
                 sem=("arbitrary",))(dx, m, mod)


def _ffn_fn(shifts, ug, uv, wg, wv, bg, bv):
    down, up = shifts

    def conv(x, w, b):
        return down(x) * w[0:1, :] + x * w[1:2, :] + up(x) * w[2:3, :] + b

    return _silu(conv(ug, wg, bg)) * conv(uv, wv, bv)


def _ffn_fwd(up, cw, cb, *, bw=256):
    N = up.shape[0]
    shifts = _make_shift(((0, N),))
    nb = DFF // bw

    def body(ug, uv, wg, wv, bg, bv, a_ref):
        a_ref[...] = _ffn_fn(shifts, ug[...], uv[...], wg[...], wv[...], bg[...], bv[...]).astype(BF16)

    def col(rows, off):
        return pl.BlockSpec((rows, bw), lambda j: (0, j + off))

    return _call(body, name="ffn_fwd", out_shape=_sds((N, DFF), BF16), grid=(nb,),
                 in_specs=[col(N, 0), col(N, nb), col(3, 0), col(3, nb), col(1, 0), col(1, nb)],
                 out_specs=col(N, 0), sem=("parallel",), vmem=VMEM_BIG)(up, up, cw, cw, cb, cb)


def _ffn_bwd(up, cw, cb, da, *, bw=256):
    N = up.shape[0]
    shifts = _make_shift(((0, N),))
    nb = DFF // bw

    def body(ug, uv, wg, wv, bg, bv, da_ref, dug, duv, dwg, dwv, dbg, dbv):
        _, vjp = jax.vjp(functools.partial(_ffn_fn, shifts), ug[...], uv[...], wg[...], wv[...], bg[...], bv[...])
        g = vjp(da_ref[...])
        dug[...] = g[0].astype(BF16)
        duv[...] = g[1].astype(BF16)
        dwg[...], dwv[...], dbg[...], dbv[...] = g[2], g[3], g[4], g[5]

    def col(rows, off):
        return pl.BlockSpec((rows, bw), lambda j: (0, j + off))

    half = (_sds((N, DFF), BF16), _sds((N, DFF), BF16), _sds((3, DFF)), _sds((3, DFF)), _sds((1, DFF)), _sds((1, DFF)))
    dug, duv, dwg, dwv, dbg, dbv = _call(
        body, name="ffn_bwd", out_shape=half, grid=(nb,),
        in_specs=[col(N, 0), col(N, nb), col(3, 0), col(3, nb), col(1, 0), col(1, nb), col(N, 0)],
        out_specs=(col(N, 0), col(N, 0), col(3, 0), col(3, 0), col(1, 0), col(1, 0)),
        sem=("parallel",), vmem=VMEM_BIG)(up, up, cw, cw, cb, cb, da)
    return (jnp.concatenate([dug, duv], axis=1), jnp.concatenate([dwg, dwv], axis=1),
            jnp.concatenate([dbg, dbv], axis=1))


def _head_fn(x1, dn, g2, fw, tgt):
    y = _rms(x1 + g2 * dn) * fw
    err = y - tgt
    return 0.5 * jnp.sum(jnp.mean(err * err, axis=-1))


def _head(x1, dn, mod, fw, tgt, *, br=256):
    N = x1.shape[0]
    row = pl.BlockSpec((br, D), lambda i: (i, 0))
    vec = pl.BlockSpec((1, D), lambda i: (0, 0))
    one = pl.BlockSpec((1, HD), lambda i: (0, 0))

    def body(x1_ref, dn_ref, mod_ref, fw_ref, tgt_ref, loss_ref, dx_ref, ddn_ref, dg_ref, dfw_ref):
        loss, (gx, gdn, gg, gfw) = jax.value_and_grad(_head_fn, argnums=(0, 1, 2, 3))(
            x1_ref[...], dn_ref[...], mod_ref[5:6, :], fw_ref[...], tgt_ref[...])
        dx_ref[...] = gx
        ddn_ref[...] = gdn.astype(BF16)

        @pl.when(pl.program_id(0) == 0)
        def _():
            loss_ref[...] = jnp.zeros_like(loss_ref)
            dg_ref[...] = jnp.zeros_like(dg_ref)
            dfw_ref[...] = jnp.zeros_like(dfw_ref)

        loss_ref[...] += jnp.broadcast_to(loss, (1, HD))
        dg_ref[...] += gg
        dfw_ref[...] += gfw

    return _call(body, name="head", out_shape=(_sds((1, HD)), _sds((N, D)), _sds((N, D), BF16), _sds((1, D)), _sds((1, D))),
                 grid=(N // br,), in_specs=[row, row, pl.BlockSpec((6, D), lambda i: (0, 0)), vec, row],
                 out_specs=(one, row, row, vec, vec), sem=("arbitrary",))(x1, dn, mod, fw, tgt)


def _adamw(w, g, m, v, *, name):
    shape = w.shape
    cols = shape[-1]
    rows = max(1, math.prod(shape[:-1]))
    w2, g2, m2, v2 = (t.reshape(rows, cols) for t in (w, g, m, v))
    br = 256 if rows % 256 == 0 else rows
    c1 = 1.0 - B1 ** STEP
    c2 = 1.0 - B2 ** STEP

    def body(w_ref, g_ref, m_ref, v_ref, d_ref, nm_ref, nv_ref):
        gv = g_ref[...]
        nm = B1 * m_ref[...] + (1.0 - B1) * gv
        nv = B2 * v_ref[...] + (1.0 - B2) * (gv * gv)
        d_ref[...] = -LR * ((nm / c1) / (jnp.sqrt(nv / c2) + AEPS) + WD * w_ref[...])
        nm_ref[...] = nm
        nv_ref[...] = nv

    blk = pl.BlockSpec((br, cols), lambda i: (i, 0))
    outs = _call(body, name=name, out_shape=(_sds((rows, cols)),) * 3, grid=(rows // br,),
                 in_specs=[blk] * 4, out_specs=(blk,) * 3, sem=("parallel",))(w2, g2, m2, v2)
    return tuple(t.reshape(shape) for t in outs)


def _adamw_many(items, *, name):
    k = len(items)
    shapes = [w.shape for w, _, _, _ in items]
    flat = [t.reshape(max(1, math.prod(t.shape[:-1])), t.shape[-1]) for it in items for t in it]
    c1 = 1.0 - B1 ** STEP
    c2 = 1.0 - B2 ** STEP

    def body(*refs):
        ins, outs = refs[:4 * k], refs[4 * k:]
        for i in range(k):
            w_ref, g_ref, m_ref, v_ref = ins[4 * i:4 * i + 4]
            gv = g_ref[...]
            nm = B1 * m_ref[...] + (1.0 - B1) * gv
            nv = B2 * v_ref[...] + (1.0 - B2) * (gv * gv)
            outs[3 * i][...] = -LR * ((nm / c1) / (jnp.sqrt(nv / c2) + AEPS) + WD * w_ref[...])
            outs[3 * i + 1][...] = nm
            outs[3 * i + 2][...] = nv

    res = _call(body, name=name, out_shape=tuple(_sds(flat[4 * i].shape) for i in range(k) for _ in range(3)))(*flat)
    return [tuple(res[3 * i + j].reshape(shapes[i]) for j in range(3)) for i in range(k)]


def _rope_tables(N, L):
    t = jnp.arange(N)
    pos = jnp.stack([(t // GRID_W).astype(F32), (t % GRID_W).astype(F32)], axis=1)
    inv = ROPE_THETA ** (-jnp.arange(0, HD // 2, 2, dtype=F32) / (HD // 2))
    ang = pos[:, :, None] * inv[None, None, :]
    cos = jnp.broadcast_to(jnp.cos(ang)[:, :, None, :], (N, 2, 2, HD // 4)).reshape(N, HD)
    sin = jnp.broadcast_to(jnp.sin(ang)[:, :, None, :], (N, 2, 2, HD // 4))
    sin = (sin * jnp.array([-1.0, 1.0], F32)[None, None, :, None]).reshape(N, HD)
    cos = jnp.concatenate([jnp.ones((L, HD), F32), cos], axis=0)
    sin = jnp.concatenate([jnp.zeros((L, HD), F32), sin], axis=0)
    return cos, sin


def _pad_lanes(v, off=0):
    return jnp.zeros((1, HD), F32).at[0, off:off + v.shape[0]].set(v)


def _local_step(x, ctx, tgt, mod_lat, mod_ctx, w_in, shards, small):
    N, L = x.shape[0], ctx.shape[0]
    T = N + L
    bounds = ((0, L), (L, T))
    qw, kw, gw = small["q_norm_w"], small["k_norm_w"], small["gdn_norm_w"]
    conv_w, ffn_w, ffn_b, fnw = small["conv_qkv_w"], small["ffn_conv_w"], small["ffn_conv_b"], small["final_norm_w"]
    alog = _pad_lanes(small["a_log"].reshape(-1), 2 * GH)
    dtb = _pad_lanes(small["dt_bias"].reshape(-1), 2 * GH)
    cos, sin = _rope_tables(N, L)
    bt = T
    bnl = 256 if N % 1024 else 1024

    hc = _normmod_fwd(ctx, mod_ctx, 0, 1, name="normmod_ctx")
    hx = _normmod_fwd(x, mod_lat, 0, 1, name="normmod_x")
    h1 = jnp.concatenate([hc, hx], axis=0)
    proj = _mm(h1, w_in, name="mm_in", M=T, N=C_END, K=D, tb=True, bm=bt, bn=1024)
    aq, ak, av = _aprep_fwd(proj, cos, sin, qw, kw)
    (attn, attn32, lse), (up_g,) = _attn_fwd(aq, ak, av, L, _GatherTwoLevel([shards["w_up"]]))
    gq = _gprep_fwd(proj, conv_w, 0, bounds)
    gk = _gprep_fwd(proj, conv_w, 1, bounds)
    gv = _gprep_fwd(proj, conv_w, 2, bounds)
    bl = _bl_fwd(proj, alog, dtb)
    intra, (down_g, pa_g, pd_g, out_g) = _intra_fwd(
        gq, gk, gv, bl, L, _GatherTwoLevel([shards[n] for n in ("w_down", "w_pa", "w_pd", "w_out")]))
    w_up, w_down = up_g.reshape(2 * DFF, D), down_g.reshape(DFF, D)
    w_pa, w_pd, w_out = pa_g.reshape(D, D), pd_g.reshape(D, D), out_g.reshape(D, D)
    xinv, intra = intra[6], intra[:6]
    o, states = _scan_fwd(*intra, L)
    gdn = _gout_fwd(o, proj, gw, L)
    pa = _mm(attn, w_pa, name="mm_pa", M=N, N=D, K=D, bm=bnl)
    pd = _mm(gdn, w_pd, name="mm_pd", M=N, N=D, K=D, bm=bnl)
    y = _merge_fwd(pa, pd, proj, L)
    m = _mm(y, w_out, name="mm_out", M=N, N=D, K=D, bm=bnl)
    x1 = _resid_fwd(x, m, mod_lat, 2, name="resid1")
    h2 = _normmod_fwd(x1, mod_lat, 3, 4, name="normmod_x1")
    up = _mm(h2, w_up, name="mm_up", M=N, N=2 * DFF, K=D, tb=True, bm=bnl, bn=2 * DFF // 4)
    a = _ffn_fwd(up, ffn_w, ffn_b)
    dn = _mm(a, w_down, name="mm_down", M=N, N=D, K=DFF, bm=bnl)
    loss, dx2, ddn, dg2, dfnw = _head(x1, dn, mod_lat, fnw, tgt)

    da = _mm(ddn, w_down, name="mm_down_dx", M=N, N=DFF, K=D, tb=True, bm=bnl, bn=DFF // 2)
    g_down = _mm(a, ddn, name="mm_down_dw", M=DFF, N=D, K=N, ta=True, bm=DFF // 2, out_dtype=BF16)
    dup, d_ffn_w, d_ffn_b = _ffn_bwd(up, ffn_w, ffn_b, da)
    dh2 = _mm(dup, w_up, name="mm_up_dx", M=N, N=D, K=2 * DFF, bm=bnl, bk=2 * DFF // 4)
    g_up = _mm(dup, h2, name="mm_up_dw", M=2 * DFF, N=D, K=N, ta=True, bm=2 * DFF // 4, out_dtype=BF16)
    dx1, dsh2, dsc2 = _normmod_bwd(x1, mod_lat, 3, 4, dh2, 0, dx2, name="normmod_x1_bwd")
    dm, dg1 = _resid_bwd(dx1, m, mod_lat, 2, name="resid1_bwd")
    dy = _mm(dm, w_out, name="mm_out_dx", M=N, N=D, K=D, tb=True, bm=bnl)
    g_out = _mm(y, dm, name="mm_out_dw", M=D, N=D, K=N, ta=True, out_dtype=BF16)
    dpa, dpd, dproj = _merge_bwd(pa, pd, proj, dy, L)
    dattn = _mm(dpa, w_pa, name="mm_pa_dx", M=N, N=D, K=D, tb=True, bm=bnl)
    g_pa = _mm(attn, dpa, name="mm_pa_dw", M=D, N=D, K=N, ta=True, out_dtype=BF16)
    dgdn = _mm(dpd, w_pd, name="mm_pd_dx", M=N, N=D, K=D, tb=True, bm=bnl)
    g_pd = _mm(gdn, dpd, name="mm_pd_dw", M=D, N=D, K=N, ta=True, out_dtype=BF16)
    do, dproj, dgw = _gout_bwd(o, proj, gw, dgdn, dproj, L)
    cts, recv_a = _scan_bwd(*intra, states, do, L, _Exchange(
        [g_out.reshape(NDEV, D // NDEV, D), g_pa.reshape(NDEV, D // NDEV, D), g_pd.reshape(NDEV, D // NDEV, D)], True))
    (dgq, dgk, dgv, dbl), recv_b = _intra_bwd(gq, gk, gv, bl, xinv, cts, L, _Exchange(
        [g_up.reshape(NDEV, 2 * DFF // NDEV, D)], True))
    dproj, dwq = _gprep_bwd(proj, conv_w, 0, bounds, dgq, dproj)
    dproj, dwk = _gprep_bwd(proj, conv_w, 1, bounds, dgk, dproj)
    dproj, dwv = _gprep_bwd(proj, conv_w, 2, bounds, dgv, dproj)
    dproj, dalog, ddtb = _bl_bwd(proj, alog, dtb, dbl, dproj)
    (daq_h, dak_h, dav_h), recv_c = _attn_bwd(aq, ak, av, attn32, lse, dattn, L, _Exchange(
        [g_down.reshape(NDEV, DFF // NDEV, D)], True))
    recv = dict(zip(("w_out", "w_pa", "w_pd", "w_up", "w_down"), recv_a + recv_b + recv_c))
    dproj, dqw, dkw = _aprep_bwd(proj, cos, sin, qw, kw, daq_h, dak_h, dav_h, dproj, L)
    g_in = _mm(dproj, h1, name="mm_in_dw", M=C_END, N=D, K=T, ta=True, bm=1024, out_dtype=BF16)
    first = _position()[3] * SHARD_ROWS
    own_in = lax.dynamic_slice(_unpad_columns(g_in), (first - first % ROW_TILE, 0), (SLOT_ROWS, D))
    *pending, token = _scatter_start(g_in, None, (0, D // 2), (), name="scatter_g_in_a_start")
    dh1 = _mm(dproj, w_in, name="mm_in_dx", M=T, N=D, K=C_END, bm=bt, bk=1024, after=(token,))
    grad_x, dsh1, dsc1 = _normmod_bwd(x, mod_lat, 0, 1, dh1, L, dx1, name="normmod_x_bwd")
    _, dcsh1, dcsc1 = _normmod_bwd(ctx, mod_ctx, 0, 1, dh1, 0, None, name="normmod_ctx_bwd")

    z1 = jnp.zeros((1, D), F32)
    dmod_lat = jnp.concatenate([dsh1, dsc1, dg1, dsh2, dsc2, dg2], axis=0)
    dmod_ctx = jnp.concatenate([dcsh1, dcsc1, z1, z1, z1, z1], axis=0)
    gsmall = {
        "q_norm_w": dqw, "k_norm_w": dkw, "gdn_norm_w": dgw,
        "conv_qkv_w": jnp.concatenate([dwq, dwk, dwv], axis=1),
        "a_log": dalog[0, 2 * GH:4 * GH], "dt_bias": ddtb[0, 2 * GH:4 * GH],
        "ffn_conv_w": d_ffn_w, "ffn_conv_b": d_ffn_b, "final_norm_w": dfnw,
    }
    return loss[0, 0], grad_x, (pending, own_in), recv, dmod_lat, dmod_ctx, gsmall


HBM = pl.BlockSpec(memory_space=pltpu.HBM)
ANYSPEC = pl.BlockSpec(memory_space=pl.ANY)


def _position():
    x, y, c = lax.axis_index("x"), lax.axis_index("y"), lax.axis_index("c")
    return x, y, c, 4 * x + 2 * y + c


def _peer(x, y, c, k):
    px = 1 - x if k & 4 else x
    py = 1 - y if k & 2 else y
    pc = 1 - c if k & 1 else c
    return (px, py, pc), 4 * px + 2 * py + pc


def _exchange(arrs, *, name, scatter):
    exch = _Exchange(arrs, scatter)
    n = exch.n

    def body(*refs):
        ins, outs, sems = refs[:n], refs[n:2 * n], refs[2 * n:]
        exch.start(ins, outs, sems)
        exch.finish(ins, outs, sems)

    outs = pl.pallas_call(body, name=name, out_shape=exch.out_shape, in_specs=[HBM] * n, out_specs=(HBM,) * n,
                          scratch_shapes=exch.scratch,
                          compiler_params=pltpu.CompilerParams(has_side_effects=True))(*arrs)
    return list(outs)


class _Exchange:
    def __init__(self, arrs, scatter):
        self.arrs, self.scatter, self.n = list(arrs), scatter, len(arrs)
        self.out_shape = tuple(_sds(a.shape if scatter else (NDEV,) + a.shape, a.dtype) for a in arrs)
        self.scratch = [pltpu.SemaphoreType.DMA((self.n, NDEV - 1)), pltpu.SemaphoreType.DMA((self.n, NDEV - 1)),
                        pltpu.SemaphoreType.DMA((self.n,))]

    def _copies(self, ins, outs, sems):
        send, recv, loc = sems
        x, y, c, me = _position()
        local = [pltpu.make_async_copy(ins[a].at[me] if self.scatter else ins[a], outs[a].at[me], loc.at[a])
                 for a in range(self.n)]
        remote = []
        for k in range(1, NDEV):
            peer, pid = _peer(x, y, c, k)
            for a in range(self.n):
                src = ins[a].at[pid] if self.scatter else ins[a]
                remote.append(pltpu.make_async_remote_copy(
                    src_ref=src, dst_ref=outs[a].at[me], send_sem=send.at[a, k - 1], recv_sem=recv.at[a, k - 1],
                    device_id=peer, device_id_type=MESH))
        return local, remote

    def start(self, ins, outs, sems):
        local, remote = self._copies(ins, outs, sems)
        for cp in local + remote:
            cp.start()

    def finish(self, ins, outs, sems):
        local, remote = self._copies(ins, outs, sems)
        for cp in remote:
            cp.wait()
        for cp in local:
            cp.wait()


class _GatherTwoLevel:
    scatter = False

    def __init__(self, arrs):
        self.arrs, self.n = list(arrs), len(arrs)
        self.out_shape = tuple(_sds((NDEV,) + a.shape, a.dtype) for a in arrs)
        self.scratch = [pltpu.SemaphoreType.DMA((self.n, NDEV - 1)), pltpu.SemaphoreType.DMA((self.n, NDEV - 1)),
                        pltpu.SemaphoreType.DMA((self.n,))]

    def _parts(self, ins, outs, sems):
        send, recv, loc = sems
        x, y, c, _ = _position()
        me, sibling = (x, y, c), (x, y, 1 - c)
        chips = [(1 - x, y), (x, 1 - y), (1 - x, 1 - y)]
        parts = []
        for a in range(self.n):
            slot = lambda px, py, pc, a=a: outs[a].at[4 * px + 2 * py + pc]

            def copy(k, owner, to, src=None, a=a, slot=slot):
                return pltpu.make_async_remote_copy(
                    src_ref=slot(*owner) if src is None else src, dst_ref=slot(*owner), send_sem=send.at[a, k],
                    recv_sem=recv.at[a, k], device_id=to, device_id_type=MESH)

            parts.append(dict(
                mine=pltpu.make_async_copy(ins[a], slot(*me), loc.at[a]),
                first=[copy(0, me, sibling, src=ins[a])] + [copy(1 + j, me, (*ch, c), src=ins[a]) for j, ch in enumerate(chips)],
                arrive=[copy(1 + j, (*ch, c), me) for j, ch in enumerate(chips)],
                passed=[copy(4 + j, (*ch, c), sibling) for j, ch in enumerate(chips)],
                rest=[copy(0, sibling, me)] + [copy(4 + j, (*ch, 1 - c), me) for j, ch in enumerate(chips)]))
        return parts

    def start(self, ins, outs, sems):
        for p in self._parts(ins, outs, sems):
            p["mine"].start()
            for cp in p["first"]:
                cp.start()

    def middle(self, ins, outs, sems):
        for p in self._parts(ins, outs, sems):
            for got, fwd in zip(p["arrive"], p["passed"]):
                got.wait_recv()
                fwd.start()

    def finish(self, ins, outs, sems):
        for p in self._parts(ins, outs, sems):
            for cp in p["rest"]:
                cp.wait_recv()
            for cp in p["first"] + p["passed"]:
                cp.wait_send()
            p["mine"].wait()


def _gather_two_level(blocks, *, name):
    exch = _GatherTwoLevel(blocks)
    n = exch.n

    def body(*refs):
        ins, outs, sems = refs[:n], refs[n:2 * n], refs[2 * n:]
        exch.start(ins, outs, sems)
        exch.middle(ins, outs, sems)
        exch.finish(ins, outs, sems)

    outs = pl.pallas_call(body, name=name, out_shape=exch.out_shape, in_specs=[HBM] * n, out_specs=(HBM,) * n,
                          scratch_shapes=exch.scratch,
                          compiler_params=pltpu.CompilerParams(has_side_effects=True))(*blocks)
    return list(outs)


SEM = pl.BlockSpec(memory_space=pltpu.SEMAPHORE)


SHARD_ROWS = W_END // NDEV
RUNS = ((0, W_QKV, C_KV), (W_QKV, W_AQ - W_QKV, C_QKV), (W_AQ, W_Z - W_AQ, C_AQ), (W_Z, W_END - W_Z, C_Z))


ROW_TILE = 8
SLOT_ROWS = -(-SHARD_ROWS // ROW_TILE) * ROW_TILE


def _shard_pieces(d):
    lo, hi = d * SHARD_ROWS, (d + 1) * SHARD_ROWS
    lead = lo % ROW_TILE
    pieces = []
    for first, rows, padded in RUNS:
        a, b = max(lo, first), min(hi, first + rows)
        if a < b:
            pieces.append([a - lo + lead, b - a, padded + a - first])
    pieces[0] = [0, pieces[0][1] + lead, pieces[0][2] - lead]
    pieces[-1][1] = SLOT_ROWS - pieces[-1][0]
    assert all(v % ROW_TILE == 0 for p in pieces for v in p) and all(p[2] + p[1] <= C_END for p in pieces)
    return pieces


def _scatter_send(src_ref, land_ref, send_sems, recv_sems, cols):
    _, _, _, me = _position()
    for d in range(NDEV):
        @pl.when(me != d)
        def _():
            k = jnp.bitwise_xor(me, d)
            peer = tuple(jnp.int32((d >> s) & 1) for s in (2, 1, 0))
            for off, rows, padded in _shard_pieces(d):
                pltpu.make_async_remote_copy(
                    src_ref=src_ref.at[pl.ds(padded, rows), pl.ds(*cols)],
                    dst_ref=land_ref.at[me].at[pl.ds(off, rows), pl.ds(*cols)], send_sem=send_sems.at[k - 1],
                    recv_sem=recv_sems.at[k - 1], device_id=peer, device_id_type=MESH).start()


def _scatter_whole(src_ref, land_ref, send_sems, recv_sems, cols):
    x, y, c, me = _position()
    span = (slice(None), pl.ds(*cols))
    copies = []
    for k in range(1, NDEV):
        peer, _ = _peer(x, y, c, k)
        copies.append(pltpu.make_async_remote_copy(
            src_ref=src_ref.at[pl.ds(0, SLOT_ROWS)].at[span], dst_ref=land_ref.at[me].at[span],
            send_sem=send_sems.at[k - 1], recv_sem=recv_sems.at[k - 1], device_id=peer, device_id_type=MESH))
    return copies


SPLIT_EFFECT = pltpu.SideEffectType.DATAFLOW_SIDE_EFFECTING


def _scatter_start(parts, land, cols, after, *, name):
    na = len(after)
    if land is None:
        land = lax.empty((NDEV, SLOT_ROWS, D), parts.dtype)

    def body(src_ref, land_ref, *rest):
        send_sems, recv_sems, _, _, token = rest[na:]
        _scatter_send(src_ref, land_ref, send_sems, recv_sems, cols)
        token[...] = jnp.zeros_like(token)

    return pl.pallas_call(
        body, name=name,
        out_shape=(pltpu.SemaphoreType.DMA((NDEV - 1,)), pltpu.SemaphoreType.DMA((NDEV - 1,)),
                   pltpu.HBM(parts.shape, parts.dtype), pltpu.HBM(land.shape, land.dtype), _sds((8, HD))),
        in_specs=(HBM, HBM) + (pl.BlockSpec(memory_space=pl.ANY),) * na,
        out_specs=(SEM, SEM, HBM, HBM, pl.BlockSpec(memory_space=pltpu.VMEM)),
        input_output_aliases={0: 2, 1: 3}, compiler_params=pltpu.CompilerParams(has_side_effects=SPLIT_EFFECT),
    )(pltpu.with_memory_space_constraint(parts, pltpu.HBM), pltpu.with_memory_space_constraint(land, pltpu.HBM), *after)


def _scatter_wait(send_sems, recv_sems, src_thru, land_thru, cols, after, *, name):
    na = len(after)

    def body(src_ref, land_ref, send_sems, recv_sems, *rest):
        for cp in _scatter_whole(src_ref, land_ref, send_sems, recv_sems, cols):
            cp.wait_send()
            cp.wait_recv()

    return pl.pallas_call(
        body, name=name,
        out_shape=(pltpu.HBM(src_thru.shape, src_thru.dtype), pltpu.HBM(land_thru.shape, land_thru.dtype)),
        in_specs=(HBM, HBM, SEM, SEM) + (pl.BlockSpec(memory_space=pl.ANY),) * na, out_specs=(HBM, HBM),
        input_output_aliases={0: 0, 1: 1}, compiler_params=pltpu.CompilerParams(has_side_effects=SPLIT_EFFECT),
    )(src_thru, land_thru, send_sems, recv_sems, *after)


def _cast_bf16(ws, *, name):
    k = len(ws)

    def body(*refs):
        for w_ref, o_ref in zip(refs[:k], refs[k:]):
            o_ref[...] = w_ref[...].astype(BF16)

    return _call(body, name=name, out_shape=tuple(_sds(w.shape, BF16) for w in ws), vmem=VMEM_BIG)(*ws)


def _sum_slots(a, *, name):
    _, R, C = a.shape

    def body(a_ref, o_ref):
        s = a_ref[0]
        for d in range(1, NDEV):
            s = s + a_ref[d]
        o_ref[...] = s

    return _call(body, name=name, out_shape=_sds((R, C)))(a)


MODROWS = 16


def _mod_fwd(c9, w, b):
    cols = w.shape[1]

    def body(c_ref, w_ref, b_ref, o_ref):
        o_ref[...] = _nn(_silu(c_ref[...]), w_ref[...]) + b_ref[...]

    return _call(body, name="mod_fwd", out_shape=_sds((MODROWS, cols)))(c9, w, b)


def _mod_bwd(c9, dmy, dall, w):
    cols = w.shape[1]

    def body(c_ref, dmy_ref, dall_ref, w_ref, gw_ref, gb_ref, cp_ref):
        sc = _silu(c_ref[...])
        rows = lax.broadcasted_iota(jnp.int32, (MODROWS, 1), 0)
        d = dmy_ref[...]
        d_ctx = jnp.where(rows == NDEV, d, 0.0)
        sc_ctx = jnp.where(rows == NDEV, sc, 0.0)
        outer = lax.dot_general(sc_ctx, d_ctx, (((0,), (0,)), ((), ())), precision=HI, preferred_element_type=F32)
        gw_ref[...] = _tn(jnp.where(rows < NDEV, sc, 0.0), jnp.where(rows < NDEV, d, 0.0)) + outer
        gb_ref[...] = jnp.sum(dall_ref[...], axis=0, keepdims=True)
        cp_ref[...] = jnp.sum(_nt(d_ctx, w_ref[...]), axis=0, keepdims=True)

    return _call(body, name="mod_bwd", out_shape=(_sds((D, cols)), _sds((1, 6 * D)), _sds((1, D))),
                 vmem=VMEM_BIG)(c9, dmy, dall, w)


def _cctx_finish(parts, c_ctx, after):
    VM = pl.BlockSpec(memory_space=pltpu.VMEM)

    def body(p_ref, c_ref, *rest):
        o_ref = rest[-1]
        s = p_ref[0]
        for d in range(1, NDEV):
            s = s + p_ref[d]
        _, vjp = jax.vjp(_silu, c_ref[...])
        o_ref[...] = vjp(s)[0]

    return _call(body, name="cctx_finish", out_shape=_sds((1, D)),
                 in_specs=[VM, VM] + [pl.BlockSpec(memory_space=pl.ANY)] * len(after))(parts, c_ctx, *after)


def _adamw_recv(w, recv, m, v, *, name, own=None):
    rows, cols = w.shape
    slot_rows = recv.shape[1]
    lead = slot_rows - rows
    assert rows % ROW_TILE in (0, lead)
    bc = 256
    c1 = 1.0 - B1 ** STEP
    c2 = 1.0 - B2 ** STEP
    has_own = own is not None

    def body(w_ref, r_ref, m_ref, v_ref, *rest):
        g_ref, d_ref, nm_ref, nv_ref = rest[-4:]
        me = _position()[3]

        def slot(d):
            return jnp.where(me == d, rest[0][...], r_ref[d]) if has_own else r_ref[d]

        gv = slot(0).astype(F32)
        for d in range(1, NDEV):
            gv = gv + slot(d).astype(F32)
        if lead:
            gv = jnp.where((me * rows) % ROW_TILE == 0, gv[:rows], gv[lead:])
        nm = B1 * m_ref[...] + (1.0 - B1) * gv
        nv = B2 * v_ref[...] + (1.0 - B2) * (gv * gv)
        g_ref[...] = gv
        d_ref[...] = -LR * ((nm / c1) / (jnp.sqrt(nv / c2) + AEPS) + WD * w_ref[...])
        nm_ref[...] = nm
        nv_ref[...] = nv

    blk = pl.BlockSpec((rows, bc), lambda j: (0, j))
    return _call(body, name=name, out_shape=(_sds((rows, cols)),) * 4, grid=(cols // bc,),
                 in_specs=[blk, pl.BlockSpec((NDEV, slot_rows, bc), lambda j: (0, 0, j)), blk, blk]
                 + [pl.BlockSpec((slot_rows, bc), lambda j: (0, j))] * has_own,
                 out_specs=(blk,) * 4, sem=("parallel",), vmem=VMEM_BIG)(w, recv, m, v, *([own] if has_own else []))


P_LAT, P_CTX, P_FNW, P_FFNB, P_CONV, P_FFNW, P_MISC, P_ROWS = 0, 8, 16, 24, 32, 48, 72, 80


def _rows_of(v, nrows):
    flat = v.reshape(-1)
    return jnp.pad(flat, (0, nrows * D - flat.shape[0])).reshape(nrows, D)


def _by_columns(g):
    n, r, c = g.shape
    return jnp.transpose(g, (1, 0, 2)).reshape(r, n * c)


def kernel(x, c, ctx, c_ctx, w_mod, b_mod, w_in, q_norm_w, k_norm_w, conv_qkv_w, a_log, dt_bias, gdn_norm_w, w_pa, w_pd, w_out, w_up, ffn_conv_w, ffn_conv_b, w_down, final_norm_w, loss_target, m_c_ctx, m_w_mod, m_b_mod, m_w_in, m_q_norm_w, m_k_norm_w, m_conv_qkv_w, m_a_log, m_dt_bias, m_gdn_norm_w, m_w_pa, m_w_pd, m_w_out, m_w_up, m_ffn_conv_w, m_ffn_conv_b, m_w_down, m_final_norm_w, v_c_ctx, v_w_mod, v_b_mod, v_w_in, v_q_norm_w, v_k_norm_w, v_conv_qkv_w, v_a_log, v_dt_bias, v_gdn_norm_w, v_w_pa, v_w_pd, v_w_out, v_w_up, v_ffn_conv_w, v_ffn_conv_b, v_w_down, v_final_norm_w):
    _, _, _, me = _position()
    mcols = w_mod.shape[2]

    transposed = ("w_in", "w_up")
    big = {"w_in": w_in[0].T, "w_pa": w_pa[0], "w_pd": w_pd[0], "w_out": w_out[0], "w_up": w_up[0].T, "w_down": w_down[0]}
    names = list(big)
    shards = dict(zip(names, _cast_bf16([big[n] for n in names], name="cast_weights")))
    w_in_g, c_all, conv_g, ffnw_g = _gather_two_level([shards["w_in"], c, conv_qkv_w[0], ffn_conv_w[0]],
                                                      name="gather_w_in")
    w_in_full = w_in_g.reshape(W_END, D)
    w_in_pad = _pad_columns(w_in_full)

    c9 = jnp.concatenate([c_all.reshape(NDEV, D), jnp.pad(c_ctx[None], ((0, MODROWS - NDEV - 1), (0, 0)))], axis=0)
    b_loc = lax.dynamic_slice(b_mod, (0, me * mcols), (1, mcols))
    mod_all, = _exchange([_mod_fwd(c9, w_mod[0], b_loc)], name="gather_mod", scatter=False)
    mod_lat = lax.dynamic_index_in_dim(mod_all, me, axis=1, keepdims=False).reshape(6, D)
    mod_ctx = mod_all[:, NDEV, :].reshape(6, D)

    small = {"q_norm_w": q_norm_w, "k_norm_w": k_norm_w, "gdn_norm_w": gdn_norm_w, "a_log": a_log, "dt_bias": dt_bias,
             "conv_qkv_w": _by_columns(conv_g), "ffn_conv_w": _by_columns(ffnw_g), "ffn_conv_b": ffn_conv_b,
             "final_norm_w": final_norm_w[None]}
    loss_me, grad_x, (pending_in, own_in), recv, dmod_lat, dmod_ctx, gs = _local_step(
        x[0], ctx[0], loss_target[0], mod_lat, mod_ctx, w_in_pad, shards, small)

    moments = {"w_in": (m_w_in, v_w_in), "w_pa": (m_w_pa, v_w_pa), "w_pd": (m_w_pd, v_w_pd),
               "w_out": (m_w_out, v_w_out), "w_up": (m_w_up, v_w_up), "w_down": (m_w_down, v_w_down)}
    res = {}
    def finish(n, outs):
        return tuple((t.T if n in transposed else t)[None] for t in outs)

    def moment(t, n):
        return t[0].T if n in transposed else t[0]

    for n in recv:
        res[n] = finish(n, _adamw_recv(big[n], recv[n], moment(moments[n][0], n), moment(moments[n][1], n),
                                       name="adamw_" + n))

    misc = jnp.concatenate([gs["q_norm_w"][0], gs["k_norm_w"][0], gs["gdn_norm_w"][0], gs["a_log"], gs["dt_bias"],
                            loss_me[None]])
    pack = jnp.concatenate([_rows_of(dmod_lat, P_CTX - P_LAT), _rows_of(dmod_ctx, P_FNW - P_CTX),
                            _rows_of(gs["final_norm_w"], P_FFNB - P_FNW), _rows_of(gs["ffn_conv_b"], P_CONV - P_FFNB),
                            _rows_of(gs["conv_qkv_w"], P_FFNW - P_CONV), _rows_of(gs["ffn_conv_w"], P_MISC - P_FFNW),
                            _rows_of(misc, P_ROWS - P_MISC)], axis=0)
    pack_all, = _exchange([pack], name="gather_pack", scatter=False)
    tot = _sum_slots(pack_all, name="sum_pack")
    dall = jnp.concatenate([pack_all[:, P_LAT:P_LAT + 6, :].reshape(NDEV, 6 * D),
                            jnp.pad(tot[P_CTX:P_CTX + 6].reshape(1, 6 * D), ((0, MODROWS - NDEV - 1), (0, 0)))], axis=0)
    dmy = lax.dynamic_slice(dall, (0, me * mcols), (MODROWS, mcols))
    g_w_mod, g_b_mod, cpart = _mod_bwd(c9, dmy, dall, w_mod[0])
    cparts, = _exchange([cpart], name="gather_cctx", scatter=False)
    sems_a, land = pending_in[:2], pending_in[3]
    *sems_b, g_in_thru, land, token_b = _scatter_start(pending_in[2], land, (D // 2, D // 2), (cparts,),
                                                       name="scatter_g_in_b_start")
    g_c_ctx = _cctx_finish(cparts, c_ctx[None], (token_b,))[0]

    nconv, nffn = 3 * GH * HD, 2 * DFF
    conv_tot = tot[P_CONV:P_FFNW].reshape(-1)[:3 * nconv].reshape(3, nconv)
    ffnw_tot = tot[P_FFNW:P_MISC].reshape(-1)[:3 * nffn].reshape(3, nffn)
    mrow = tot[P_MISC]
    grads = {
        "c_ctx": g_c_ctx, "w_mod": g_w_mod[None], "b_mod": g_b_mod,
        "q_norm_w": mrow[None, 0:HD], "k_norm_w": mrow[None, HD:2 * HD], "gdn_norm_w": mrow[None, 2 * HD:3 * HD],
        "conv_qkv_w": lax.dynamic_slice(conv_tot, (0, me * (nconv // NDEV)), (3, nconv // NDEV))[None],
        "a_log": mrow[3 * HD:3 * HD + 2 * GH].reshape(1, 2, GH),
        "dt_bias": mrow[3 * HD + 2 * GH:3 * HD + 4 * GH].reshape(1, 2, GH),
        "ffn_conv_w": lax.dynamic_slice(ffnw_tot, (0, me * (nffn // NDEV)), (3, nffn // NDEV))[None],
        "ffn_conv_b": tot[P_FFNB:P_CONV].reshape(-1)[:nffn][None],
        "final_norm_w": tot[P_FNW],
    }
    loss = mrow[3 * HD + 4 * GH]
    given = {"c_ctx": (c_ctx, m_c_ctx, v_c_ctx), "w_mod": (w_mod, m_w_mod, v_w_mod), "b_mod": (b_mod, m_b_mod, v_b_mod),
             "q_norm_w": (q_norm_w, m_q_norm_w, v_q_norm_w), "k_norm_w": (k_norm_w, m_k_norm_w, v_k_norm_w),
             "conv_qkv_w": (conv_qkv_w, m_conv_qkv_w, v_conv_qkv_w), "a_log": (a_log, m_a_log, v_a_log),
             "dt_bias": (dt_bias, m_dt_bias, v_dt_bias), "gdn_norm_w": (gdn_norm_w, m_gdn_norm_w, v_gdn_norm_w),
             "ffn_conv_w": (ffn_conv_w, m_ffn_conv_w, v_ffn_conv_w), "ffn_conv_b": (ffn_conv_b, m_ffn_conv_b, v_ffn_conv_b),
             "final_norm_w": (final_norm_w, m_final_norm_w, v_final_norm_w)}
    res["w_mod"] = (grads["w_mod"],) + _adamw(w_mod, grads["w_mod"], m_w_mod, v_w_mod, name="adamw_w_mod")
    small_names = [n for n in given if n != "w_mod"]
    updates = _adamw_many([(given[n][0], grads[n], given[n][1], given[n][2]) for n in small_names], name="adamw_small")
    for n, upd in zip(small_names, updates):
        res[n] = (grads[n],) + upd

    g_in_thru, land = _scatter_wait(*sems_a, g_in_thru, land, (0, D // 2), [res[n][1] for n in res],
                                    name="scatter_g_in_a_wait")
    _, land = _scatter_wait(*sems_b, g_in_thru, land, (D // 2, D // 2), (), name="scatter_g_in_b_wait")
    res["w_in"] = finish("w_in", _adamw_recv(big["w_in"], land, moment(m_w_in, "w_in"), moment(v_w_in, "w_in"),
                                             name="adamw_w_in", own=own_in))

    order = ["c_ctx", "w_mod", "b_mod", "w_in", "q_norm_w", "k_norm_w", "conv_qkv_w", "a_log", "dt_bias", "gdn_norm_w",
             "w_pa", "w_pd", "w_out", "w_up", "ffn_conv_w", "ffn_conv_b", "w_down", "final_norm_w"]
    return (loss, grad_x[None], *[res[n][0] for n in order], *[res[n][1] for n in order],
            *[res[n][2] for n in order], *[res[n][3] for n in order])
```

```python
import functools
import math

import jax
import jax.numpy as jnp
from jax import lax
from jax.experimental import pallas as pl
from jax.experimental.pallas import tpu as pltpu

F32 = jnp.float32
BF16 = jnp.bfloat16
HI = lax.Precision.HIGHEST
MESH = pl.DeviceIdType.MESH

NDEV = 8
D = 1024
HD = 128
AH, AKV, GRP = 8, 2, 4
GH = 8
CH = 64
DFF = 2816
GRID_W = 64
EPS = 1e-6
ROPE_THETA = 10000.0
LOG2E = math.log2(math.e)
C_KV, C_AQ, C_QKV, C_BL, C_Z, C_GATE, C_END = 0, 512, 1536, 4608, 5120, 6144, 8192
W_QKV, W_AQ, W_Z, W_END = 512, 3616, 4640, 7712


def _pad_columns(w):
    zeros = jnp.zeros((C_Z - C_QKV - (W_AQ - W_QKV), D), w.dtype)
    return jnp.concatenate([w[:W_QKV], w[W_AQ:W_Z], w[W_QKV:W_AQ], zeros, w[W_Z:]], axis=0)


def _unpad_columns(g):
    return jnp.concatenate([g[:C_AQ], g[C_QKV:C_QKV + W_AQ - W_QKV], g[C_AQ:C_QKV], g[C_Z:]], axis=0)
LR, B1, B2, AEPS, WD, STEP = 0.001, 0.9, 0.999, 1e-08, 0.01, 10
VMEM_BIG = 56 * 1024 * 1024
INTRA_FWD_CHUNKS = 36
INTRA_BWD_CHUNKS = 36


def _call(body, *, name, out_shape, grid=None, in_specs=None, out_specs=None, scratch=(), sem=None,
          vmem=None, aliases=None):
    params = {}
    if sem is not None:
        params["dimension_semantics"] = sem
    if vmem is not None:
        params["vmem_limit_bytes"] = vmem
    kw = {}
    if grid is not None:
        kw["grid"] = grid
    if in_specs is not None:
        kw["in_specs"] = in_specs
    if out_specs is not None:
        kw["out_specs"] = out_specs
    if aliases:
        kw["input_output_aliases"] = aliases
    return pl.pallas_call(body, name=name, out_shape=out_shape, scratch_shapes=list(scratch),
                          compiler_params=pltpu.CompilerParams(**params), **kw)


def _call_carrying(body, exch, *, name, out_shape, grid, in_specs, out_specs, scratch=(), vmem=None):
    n, nin, nout, nscr = exch.n, len(in_specs), len(out_shape), len(scratch)
    steps = math.prod(grid)
    mid = (2 * steps) // 3

    def wrapped(*refs):
        ins, cins = refs[:nin], refs[nin:nin + n]
        outs, couts = refs[nin + n:nin + n + nout], refs[nin + n + nout:nin + 2 * n + nout]
        scr, sems = refs[nin + 2 * n + nout:nin + 2 * n + nout + nscr], refs[nin + 2 * n + nout + nscr:]
        ids = [pl.program_id(i) for i in range(len(grid))]
        first = functools.reduce(jnp.logical_and, [i == 0 for i in ids])
        last = functools.reduce(jnp.logical_and, [i == g - 1 for i, g in zip(ids, grid)])

        @pl.when(first)
        def _():
            exch.start(cins, couts, sems)

        if hasattr(exch, "middle"):
            linear = functools.reduce(lambda acc, ig: acc * ig[1] + ig[0], zip(ids, grid), 0)

            @pl.when(linear == mid)
            def _():
                exch.middle(cins, couts, sems)

        body(*ins, *outs, *scr)

        @pl.when(last)
        def _():
            exch.finish(cins, couts, sems)

    params = {"dimension_semantics": ("arbitrary",) * len(grid)}
    if vmem is not None:
        params["vmem_limit_bytes"] = vmem
    fn = pl.pallas_call(wrapped, name=name, out_shape=tuple(out_shape) + exch.out_shape, grid=grid,
                        in_specs=list(in_specs) + [HBM] * n, out_specs=tuple(out_specs) + (HBM,) * n,
                        scratch_shapes=list(scratch) + exch.scratch, compiler_params=pltpu.CompilerParams(**params))

    def run(*args):
        res = fn(*args, *exch.arrs)
        return res[:nout], list(res[nout:])

    return run


def _sds(shape, dtype=F32):
    return jax.ShapeDtypeStruct(tuple(shape), dtype)


def _dot(a, b, ca, cb):
    return lax.dot_general(a.astype(BF16), b.astype(BF16), (((ca,), (cb,)), ((), ())),
                           preferred_element_type=F32)


@jax.custom_vjp
def _nn(a, b):
    return _dot(a, b, 1, 0)


@jax.custom_vjp
def _nt(a, b):
    return _dot(a, b, 1, 1)


@jax.custom_vjp
def _tn(a, b):
    return _dot(a, b, 0, 0)


_nn.defvjp(lambda a, b: (_nn(a, b), (a, b)), lambda r, g: (_nt(g, r[1]), _tn(r[0], g)))
_nt.defvjp(lambda a, b: (_nt(a, b), (a, b)), lambda r, g: (_nn(g, r[1]), _tn(g, r[0])))
_tn.defvjp(lambda a, b: (_tn(a, b), (a, b)), lambda r, g: (_nt(r[1], g), _nn(r[0], g)))


def _mdot(a, b):
    return jnp.dot(a, b, precision=lax.Precision.HIGH, preferred_element_type=F32)


def _maskdot(mask, a, cm):
    hi = a.astype(BF16)
    r = a - hi.astype(F32)
    mid = r.astype(BF16)
    lo = (r - mid.astype(F32)).astype(BF16)
    mb = mask.astype(BF16)
    dims = (((cm,), (0,)), ((), ()))
    return (lax.dot_general(mb, hi, dims, preferred_element_type=F32)
            + lax.dot_general(mb, mid, dims, preferred_element_type=F32)
            + lax.dot_general(mb, lo, dims, preferred_element_type=F32))


@jax.custom_vjp
def _mask_nn(mask, a):
    return _maskdot(mask, a, 1)


_mask_nn.defvjp(lambda mask, a: (_maskdot(mask, a, 1), mask),
                lambda mask, g: (jnp.zeros_like(mask), _maskdot(mask, g, 0)))


@jax.custom_vjp
def _saved_inverse(lmat, x):
    return x


def _saved_inverse_bwd(x, g):
    t = lax.dot_general(x, g, (((0,), (0,)), ((), ())), precision=lax.Precision.HIGH, preferred_element_type=F32)
    dl = lax.dot_general(t, x, (((1,), (1,)), ((), ())), precision=lax.Precision.HIGH, preferred_element_type=F32)
    return -dl, jnp.zeros_like(x)


_saved_inverse.defvjp(lambda lmat, x: (x, x), _saved_inverse_bwd)


def _row_ids(shape):
    return lax.broadcasted_iota(jnp.int32, shape, 0)


def _shift_rows(x, down, bounds):
    n = x.shape[0]
    rows = _row_ids(x.shape)
    y = pltpu.roll(x, 1 if down else n - 1, 0)
    edge = functools.reduce(jnp.logical_or, [rows == (s if down else e - 1) for s, e in bounds])
    return jnp.where(edge, 0.0, y)


def _make_shift(bounds):
    @jax.custom_vjp
    def down(x):
        return _shift_rows(x, True, bounds)

    @jax.custom_vjp
    def up(x):
        return _shift_rows(x, False, bounds)

    down.defvjp(lambda x: (down(x), None), lambda _, g: (up(g),))
    up.defvjp(lambda x: (up(x), None), lambda _, g: (down(g),))
    return down, up


@jax.custom_vjp
def _swap32(x):
    lane = lax.broadcasted_iota(jnp.int32, x.shape, x.ndim - 1)
    return jnp.where((lane % 64) < 32, pltpu.roll(x, HD - 32, x.ndim - 1), pltpu.roll(x, 32, x.ndim - 1))


_swap32.defvjp(lambda x: (_swap32(x), None), lambda _, g: (_swap32(g),))


def _rms(x):
    return x * lax.rsqrt(jnp.mean(x * x, axis=-1, keepdims=True) + EPS)


def _silu(x):
    return x * jax.nn.sigmoid(x)


def _mm(a, b, *, name, M, N, K, ta=False, tb=False, out_dtype=F32, bm=None, bn=None, bk=None, after=()):
    bm, bn, bk = bm or M, bn or N, bk or K
    assert M % bm == 0 and N % bn == 0 and K % bk == 0, (name, M, N, K, bm, bn, bk)
    nk = K // bk
    ca, cb = (0 if ta else 1), (1 if tb else 0)
    na = len(after)

    def body(a_ref, b_ref, *rest):
        o_ref, acc = rest[na], rest[na + 1:]
        r = _dot(a_ref[...], b_ref[...], ca, cb)
        if nk == 1:
            o_ref[...] = r.astype(out_dtype)
        else:
            acc_ref, = acc
            k = pl.program_id(2)

            @pl.when(k == 0)
            def _():
                acc_ref[...] = r

            @pl.when(k > 0)
            def _():
                acc_ref[...] += r

            @pl.when(k == nk - 1)
            def _():
                o_ref[...] = acc_ref[...].astype(out_dtype)

    a_spec = pl.BlockSpec((bk, bm), lambda i, j, k: (k, i)) if ta else pl.BlockSpec((bm, bk), lambda i, j, k: (i, k))
    b_spec = pl.BlockSpec((bn, bk), lambda i, j, k: (j, k)) if tb else pl.BlockSpec((bk, bn), lambda i, j, k: (k, j))
    return _call(body, name=name, out_shape=_sds((M, N), out_dtype), grid=(M // bm, N // bn, nk),
                 in_specs=[a_spec, b_spec] + [pl.BlockSpec(memory_space=pl.ANY)] * na,
                 out_specs=pl.BlockSpec((bm, bn), lambda i, j, k: (i, j)),
                 scratch=[pltpu.VMEM((bm, bn), F32)] if nk > 1 else [],
                 sem=("parallel", "parallel", "arbitrary"), vmem=VMEM_BIG)(a, b, *after)


def _normmod_fn(x, sh, sc):
    return _rms(x) * (1.0 + sc) + sh


def _normmod_fwd(x, mod, i_sh, i_sc, *, name, br=256):
    R = x.shape[0]

    def body(x_ref, mod_ref, o_ref):
        o_ref[...] = _normmod_fn(x_ref[...], mod_ref[i_sh:i_sh + 1, :], mod_ref[i_sc:i_sc + 1, :]).astype(BF16)

    return _call(body, name=name, out_shape=_sds((R, D), BF16), grid=(R // br,),
                 in_specs=[pl.BlockSpec((br, D), lambda i: (i, 0)), pl.BlockSpec((6, D), lambda i: (0, 0))],
                 out_specs=pl.BlockSpec((br, D), lambda i: (i, 0)), sem=("parallel",))(x, mod)


def _normmod_bwd(x, mod, i_sh, i_sc, dh, dh_off, res, *, name, br=256):
    R = x.shape[0]
    ob = dh_off // br
    has_res = res is not None

    def body(x_ref, mod_ref, dh_ref, *rest):
        if has_res:
            res_ref, dx_ref, dsh_ref, dsc_ref = rest
        else:
            dx_ref, dsh_ref, dsc_ref = rest
        sh, sc = mod_ref[i_sh:i_sh + 1, :], mod_ref[i_sc:i_sc + 1, :]
        _, vjp = jax.vjp(_normmod_fn, x_ref[...], sh, sc)
        dx, dsh, dsc = vjp(dh_ref[...])
        dx_ref[...] = dx + res_ref[...] if has_res else dx

        @pl.when(pl.program_id(0) == 0)
        def _():
            dsh_ref[...] = jnp.zeros_like(dsh_ref)
            dsc_ref[...] = jnp.zeros_like(dsc_ref)

        dsh_ref[...] += dsh
        dsc_ref[...] += dsc

    row = pl.BlockSpec((br, D), lambda i: (i, 0))
    vec = pl.BlockSpec((1, D), lambda i: (0, 0))
    ins = [row, pl.BlockSpec((6, D), lambda i: (0, 0)), pl.BlockSpec((br, D), lambda i: (i + ob, 0))]
    args = [x, mod, dh]
    if has_res:
        ins.append(row)
        args.append(res)
    return _call(body, name=name, out_shape=(_sds((R, D)), _sds((1, D)), _sds((1, D))), grid=(R // br,),
                 in_specs=ins, out_specs=(row, vec, vec), sem=("arbitrary",))(*args)


def _rope(x, cos, sin):
    return x * cos + _swap32(x) * sin


def _aprep_fn(qs, ks, cos, sin, qw, kw):
    return ([_rope(_rms(q) * qw, cos, sin) for q in qs], [_rope(_rms(k) * kw, cos, sin) for k in ks])


def _aprep_fwd(proj, cos, sin, qw, kw, *, br=256):
    T = proj.shape[0]

    def body(x_ref, cos_ref, sin_ref, qw_ref, kw_ref, q_ref, k_ref, v_ref):
        qs = [x_ref[:, C_AQ + h * HD:C_AQ + (h + 1) * HD] for h in range(AH)]
        ks = [x_ref[:, h * HD:(h + 1) * HD] for h in range(AKV)]
        qo, ko = _aprep_fn(qs, ks, cos_ref[...], sin_ref[...], qw_ref[...], kw_ref[...])
        for h in range(AH):
            q_ref[h] = qo[h].astype(BF16)
        for h in range(AKV):
            k_ref[h] = ko[h].astype(BF16)
            v_ref[h] = x_ref[:, (AKV + h) * HD:(AKV + h + 1) * HD].astype(BF16)

    tab = pl.BlockSpec((br, HD), lambda i: (i, 0))
    vec = pl.BlockSpec((1, HD), lambda i: (0, 0))
    return _call(body, name="aprep_fwd",
                 out_shape=(_sds((AH, T, HD), BF16), _sds((AKV, T, HD), BF16), _sds((AKV, T, HD), BF16)),
                 grid=(T // br,),
                 in_specs=[pl.BlockSpec((br, C_QKV), lambda i: (i, 0)), tab, tab, vec, vec],
                 out_specs=(pl.BlockSpec((AH, br, HD), lambda i: (0, i, 0)),
                            pl.BlockSpec((AKV, br, HD), lambda i: (0, i, 0)),
                            pl.BlockSpec((AKV, br, HD), lambda i: (0, i, 0))),
                 sem=("parallel",))(proj, cos, sin, qw, kw)


def _aprep_bwd(proj, cos, sin, qw, kw, dq, dk, dv, dproj, L, *, br=256):
    T = proj.shape[0]
    lb = L // br

    def body(x_ref, cos_ref, sin_ref, qw_ref, kw_ref, dq_ref, dk_ref, dv_ref, _, dx_ref, dqw_ref, dkw_ref):
        i = pl.program_id(0)
        qs = [x_ref[:, C_AQ + h * HD:C_AQ + (h + 1) * HD] for h in range(AH)]
        ks = [x_ref[:, h * HD:(h + 1) * HD] for h in range(AKV)]
        _, vjp = jax.vjp(_aprep_fn, qs, ks, cos_ref[...], sin_ref[...], qw_ref[...], kw_ref[...])
        is_lat = i >= lb
        dqs = [jnp.where(is_lat, dq_ref[h], 0.0) for h in range(AH)]
        dks = [dk_ref[h] for h in range(AKV)]
        gq, gk, _, _, gqw, gkw = vjp((dqs, dks))
        for h in range(AH):
            dx_ref[:, C_AQ + h * HD:C_AQ + (h + 1) * HD] = gq[h].astype(BF16)
        for h in range(AKV):
            dx_ref[:, h * HD:(h + 1) * HD] = gk[h].astype(BF16)
            dx_ref[:, (AKV + h) * HD:(AKV + h + 1) * HD] = dv_ref[h].astype(BF16)

        @pl.when(i == 0)
        def _():
            dqw_ref[...] = jnp.zeros_like(dqw_ref)
            dkw_ref[...] = jnp.zeros_like(dkw_ref)

        dqw_ref[...] += gqw
        dkw_ref[...] += gkw

    tab = pl.BlockSpec((br, HD), lambda i: (i, 0))
    vec = pl.BlockSpec((1, HD), lambda i: (0, 0))
    kvb = pl.BlockSpec((AKV, br, HD), lambda i: (0, i, 0))
    blk = pl.BlockSpec((br, C_QKV), lambda i: (i, 0))
    return _call(body, name="aprep_bwd", out_shape=(_sds(dproj.shape, BF16), _sds((1, HD)), _sds((1, HD))),
                 grid=(T // br,),
                 in_specs=[blk, tab, tab, vec, vec,
                           pl.BlockSpec((AH, br, HD), lambda i: (0, jnp.maximum(i - lb, 0), 0)), kvb, kvb, ANYSPEC],
                 out_specs=(blk, vec, vec), aliases={8: 0},
                 sem=("arbitrary",))(proj, cos, sin, qw, kw, dq, dk, dv, dproj)


def _attn_grad(q, k, v, o, lse2, do):
    scale = HD ** -0.5
    p = jnp.exp2(_dot(q, k, 1, 1) * (scale * LOG2E) - lse2)
    dp = _dot(do, v, 1, 1)
    ds = p * (dp - jnp.sum(do * o, axis=-1, keepdims=True)) * scale
    return _dot(ds, k, 1, 0), _dot(ds, q, 0, 0), _dot(p, do, 0, 0)


ATTN_KEYS = 256


def _attn_fwd(q, k, v, L, exch, *, bq=128):
    T = q.shape[1]
    N = T - L
    lb = L // bq
    assert T % ATTN_KEYS == 0
    scale = HD ** -0.5
    heads = range(GRP)

    def body(q_ref, k_ref, v_ref, o_ref, o32_ref, lse_ref):
        qs = [q_ref[g] for g in heads]
        m = [jnp.full((bq, 1), -jnp.inf, F32) for _ in heads]
        l = [jnp.zeros((bq, 1), F32) for _ in heads]
        acc = [jnp.zeros((bq, HD), F32) for _ in heads]
        for c in range(T // ATTN_KEYS):
            kc, vc = k_ref[c * ATTN_KEYS:(c + 1) * ATTN_KEYS, :], v_ref[c * ATTN_KEYS:(c + 1) * ATTN_KEYS, :]
            s = [_dot(qs[g], kc, 1, 1) * (scale * LOG2E) for g in heads]
            m_new = [jnp.maximum(m[g], jnp.max(s[g], axis=-1, keepdims=True)) for g in heads]
            alpha = [jnp.exp2(m[g] - m_new[g]) for g in heads]
            p = [jnp.exp2(s[g] - m_new[g]) for g in heads]
            l = [l[g] * alpha[g] + jnp.sum(p[g], axis=-1, keepdims=True) for g in heads]
            acc = [acc[g] * alpha[g] + _dot(p[g], vc, 1, 0) for g in heads]
            m = m_new
        for g in heads:
            o = acc[g] / l[g]
            o_ref[:, g * HD:(g + 1) * HD] = o.astype(BF16)
            o32_ref[:, g * HD:(g + 1) * HD] = o
            lse_ref[g] = jnp.broadcast_to(m[g] + jnp.log2(l[g]), (bq, HD))

    kvb = pl.BlockSpec((None, T, HD), lambda g, i: (g, 0, 0))
    ob = pl.BlockSpec((bq, GRP * HD), lambda g, i: (i, g))
    return _call_carrying(
        body, exch, name="attn_fwd",
        out_shape=(_sds((N, AH * HD), BF16), _sds((N, AH * HD)), _sds((AH, N, HD))), grid=(AKV, N // bq),
        in_specs=[pl.BlockSpec((GRP, bq, HD), lambda g, i: (g, i + lb, 0)), kvb, kvb],
        out_specs=(ob, ob, pl.BlockSpec((GRP, bq, HD), lambda g, i: (g, i, 0))), vmem=VMEM_BIG)(q, k, v)


def _attn_bwd(q, k, v, o32, lse, do, L, exch, *, bq=128):
    T = q.shape[1]
    N = T - L
    lb = L // bq

    def body(q_ref, k_ref, v_ref, o_ref, lse_ref, do_ref, dq_ref, dk_ref, dv_ref):
        rows = lambda r: jnp.concatenate([r[:, g * HD:(g + 1) * HD] for g in range(GRP)], axis=0)
        lse = jnp.max(lse_ref[...].reshape(GRP * bq, HD), axis=-1, keepdims=True)
        dq, dk, dv = _attn_grad(q_ref[...].reshape(GRP * bq, HD), k_ref[...], v_ref[...], rows(o_ref), lse, rows(do_ref))
        dq_ref[...] = dq.reshape(GRP, bq, HD)

        @pl.when(pl.program_id(1) == 0)
        def _():
            dk_ref[...] = jnp.zeros_like(dk_ref)
            dv_ref[...] = jnp.zeros_like(dv_ref)

        dk_ref[...] += dk
        dv_ref[...] += dv

    kvb = pl.BlockSpec((None, T, HD), lambda g, i: (g, 0, 0))
    qb = pl.BlockSpec((GRP, bq, HD), lambda g, i: (g, i + lb, 0))
    hb = pl.BlockSpec((GRP, bq, HD), lambda g, i: (g, i, 0))
    ob = pl.BlockSpec((bq, GRP * HD), lambda g, i: (i, g))
    return _call_carrying(body, exch, name="attn_bwd",
                          out_shape=(_sds((AH, N, HD)), _sds((AKV, T, HD)), _sds((AKV, T, HD))), grid=(AKV, N // bq),
                          in_specs=[qb, kvb, kvb, ob, hb, ob], out_specs=(hb, kvb, kvb),
                          vmem=VMEM_BIG)(q, k, v, o32, lse, do)


def _gprep_fn(kind, shifts, x, w):
    down, up = shifts
    y = down(x) * w[0:1, :] + x * w[1:2, :] + up(x) * w[2:3, :]
    a = _silu(y)
    if kind == 2:
        return a
    a = a * lax.rsqrt(jnp.sum(a * a, axis=-1, keepdims=True) + EPS)
    return a * (HD ** -0.5) if kind == 0 else a


def _gprep_fwd(proj, conv_w, kind, bounds):
    T = proj.shape[0]
    shifts = _make_shift(bounds)
    cb = C_QKV // HD + kind * GH

    def body(x_ref, w_ref, o_ref):
        o_ref[...] = _gprep_fn(kind, shifts, x_ref[...], w_ref[...])

    return _call(body, name=f"gprep_fwd{kind}", out_shape=_sds((GH, T, HD)), grid=(GH,),
                 in_specs=[pl.BlockSpec((T, HD), lambda h: (0, cb + h)),
                           pl.BlockSpec((3, HD), lambda h: (0, kind * GH + h))],
                 out_specs=pl.BlockSpec((None, T, HD), lambda h: (h, 0, 0)), sem=("parallel",))(proj, conv_w)


def _gprep_bwd(proj, conv_w, kind, bounds, dy, dproj):
    T = proj.shape[0]
    shifts = _make_shift(bounds)
    cb = C_QKV // HD + kind * GH

    def body(x_ref, w_ref, dy_ref, _, dx_ref, dw_ref):
        _, vjp = jax.vjp(functools.partial(_gprep_fn, kind, shifts), x_ref[...], w_ref[...])
        dx, dw = vjp(dy_ref[0] + dy_ref[1])
        dx_ref[...] = dx.astype(BF16)
        dw_ref[...] = dw

    return _call(body, name=f"gprep_bwd{kind}", out_shape=(_sds(dproj.shape, BF16), _sds((3, GH * HD))), grid=(GH,),
                 in_specs=[pl.BlockSpec((T, HD), lambda h: (0, cb + h)),
                           pl.BlockSpec((3, HD), lambda h: (0, kind * GH + h)),
                           pl.BlockSpec((2, None, T, HD), lambda h: (0, h, 0, 0)), ANYSPEC],
                 out_specs=(pl.BlockSpec((T, HD), lambda h: (0, cb + h)), pl.BlockSpec((3, HD), lambda h: (0, h))),
                 aliases={3: 0}, sem=("parallel",))(proj, conv_w, dy, dproj)


def _bl_fn(x, alog, dtb):
    lane = lax.broadcasted_iota(jnp.int32, x.shape, 1)
    beta = jax.nn.sigmoid(x)
    z = x + dtb
    sp = jnp.maximum(z, 0.0) + jnp.log1p(jnp.exp(-jnp.abs(z)))
    la = -jnp.exp(alog) * sp
    return jnp.where(lane < 2 * GH, beta, jnp.where(lane < 4 * GH, la, 0.0))


def _bl_fwd(proj, alog, dtb, *, br=256):
    T = proj.shape[0]

    def body(x_ref, a_ref, d_ref, o_ref):
        o_ref[...] = _bl_fn(x_ref[...], a_ref[...], d_ref[...])

    vec = pl.BlockSpec((1, HD), lambda i: (0, 0))
    return _call(body, name="bl_fwd", out_shape=_sds((T, HD)), grid=(T // br,),
                 in_specs=[pl.BlockSpec((br, HD), lambda i: (i, C_BL // HD)), vec, vec],
                 out_specs=pl.BlockSpec((br, HD), lambda i: (i, 0)), sem=("parallel",))(proj, alog, dtb)


def _bl_bwd(proj, alog, dtb, dbl, dproj, *, br=256):
    T = proj.shape[0]
    wide = C_Z - C_BL

    def body(x_ref, a_ref, d_ref, g_ref, _, dx_ref, da_ref, dd_ref):
        g = g_ref[0, 0]
        for d in range(2):
            for h in range(GH):
                if d or h:
                    g = g + g_ref[d, h]
        _, vjp = jax.vjp(_bl_fn, x_ref[...], a_ref[...], d_ref[...])
        dx, da, dd = vjp(g)
        dx_ref[:, :HD] = dx.astype(BF16)
        dx_ref[:, HD:] = jnp.zeros((br, wide - HD), BF16)

        @pl.when(pl.program_id(0) == 0)
        def _():
            da_ref[...] = jnp.zeros_like(da_ref)
            dd_ref[...] = jnp.zeros_like(dd_ref)

        da_ref[...] += da
        dd_ref[...] += dd

    vec = pl.BlockSpec((1, HD), lambda i: (0, 0))
    return _call(body, name="bl_bwd", out_shape=(_sds(dproj.shape, BF16), _sds((1, HD)), _sds((1, HD))), grid=(T // br,),
                 in_specs=[pl.BlockSpec((br, HD), lambda i: (i, C_BL // HD)), vec, vec,
                           pl.BlockSpec((2, GH, br, HD), lambda i: (0, 0, i, 0)), ANYSPEC],
                 out_specs=(pl.BlockSpec((br, wide), lambda i: (i, C_BL // wide)), vec, vec), aliases={4: 0},
                 sem=("arbitrary",))(proj, alog, dtb, dbl, dproj)


def _chunk_masks(d):
    ii = lax.broadcasted_iota(jnp.int32, (CH, CH), 0)
    jj = lax.broadcasted_iota(jnp.int32, (CH, CH), 1)
    eye = (ii == jj).astype(F32)
    before = jnp.where(d == 0, (jj < ii).astype(F32), (jj > ii).astype(F32))
    return before, before + eye, eye


def _same_block(b):
    ii = lax.broadcasted_iota(jnp.int32, (CH, CH), 0)
    jj = lax.broadcasted_iota(jnp.int32, (CH, CH), 1)
    shift = b.bit_length() - 1
    return (jnp.right_shift(ii, shift) == jnp.right_shift(jj, shift)).astype(F32)


def _intra_fn(masks, sel_b, sel_l, qs, ks, vs, bls, xs=None):
    before, ateq, eye = masks
    inc = ateq > 0.0
    each = lambda f, *ls: [f(*t) for t in zip(*ls)]
    beta = each(lambda bl: jnp.sum(bl * sel_b, axis=-1, keepdims=True), bls)
    la = each(lambda bl: jnp.sum(bl * sel_l, axis=-1, keepdims=True), bls)
    gam = each(lambda a: _mask_nn(ateq, jnp.broadcast_to(a, (CH, HD))), la)
    gi = each(lambda g: g[:, :CH], gam)
    gj = each(lambda g: jnp.transpose(g)[:CH, :], gam)
    kq = each(lambda k, q: _nt(jnp.concatenate([k, q], axis=0), k), ks, qs)
    kk = each(lambda t: t[:CH], kq)
    qk = each(lambda t: t[CH:], kq)
    dec = each(lambda a, b: jnp.where(inc, jnp.exp(jnp.where(inc, a - b, 0.0)), 0.0), gi, gj)
    lmat = each(lambda b, d, m: before * (b * d * m), beta, dec, kk)
    if xs is None:
        same = lambda b: _same_block(b)
        l8 = each(lambda m: m * same(8), lmat)
        x = each(lambda m: eye - m, l8)
        p2 = each(lambda m: _mdot(m, m), l8)
        y = each(lambda a, b: _mdot(jnp.concatenate([a, b], axis=0), b), x, p2)
        x = each(lambda a, t: a + t[:CH], x, y)
        x = each(lambda a, t: a + _mdot(a, t[CH:]), x, y)
        for b in (8, 16, 32):
            below = same(2 * b) - same(b)
            x = each(lambda a, m: a - _mdot(a, _mdot(m * below, a)), x, lmat)
    else:
        x = each(_saved_inverse, lmat, xs)
    eg = each(jnp.exp, gam)
    uw = each(lambda a, b, v, e, k: _mdot(a, jnp.concatenate([b * v, (b * e) * k], axis=1)), x, beta, vs, eg, ks)
    u = each(lambda t: t[:, :HD], uw)
    w = each(lambda t: t[:, HD:], uw)
    tot = each(lambda a: jnp.sum(a, axis=0, keepdims=True), la)
    kd = each(lambda k, t, g: k * jnp.exp(t - g), ks, tot, gam)
    gl = each(lambda t: jnp.broadcast_to(jnp.exp(t), (1, HD)), tot)
    qd = each(lambda q, e: q * e, qs, eg)
    p = each(lambda d, m: d * m, dec, qk)
    return (u, w, kd, qd, p, gl, x) if xs is None else (u, w, kd, qd, p, gl)


def _dir_head_sel(d, h):
    lane = lax.broadcasted_iota(jnp.int32, (1, HD), 1)
    return (lane == d * GH + h).astype(F32), (lane == 2 * GH + d * GH + h).astype(F32)


def _intra_specs(T, G):
    nc = T // CH
    assert nc % G == 0
    qkv = pl.BlockSpec((None, G * CH, HD), lambda d, h, c: (h, c, 0))
    bl = pl.BlockSpec((G * CH, HD), lambda d, h, c: (c, 0))
    big = pl.BlockSpec((None, None, G * CH, HD), lambda d, h, c: (d, h, c, 0))
    pm = pl.BlockSpec((None, None, G * CH, CH), lambda d, h, c: (d, h, c, 0))
    gl = pl.BlockSpec((None, None, G, 1, HD), lambda d, h, c: (d, h, c, 0, 0))
    shapes = (_sds((2, GH, T, HD)),) + (_sds((2, GH, T, HD), BF16),) * 3 + (
        _sds((2, GH, T, CH), BF16), _sds((2, GH, nc, 1, HD)), _sds((2, GH, T, CH)))
    return nc, qkv, bl, big, pm, gl, shapes


def _chunks_per_step(T, most):
    nc = T // CH
    return max(g for g in range(1, most + 1) if nc % g == 0)


def _chunk_at(g, d, nc, ncc):
    pos = _visit_pos(g, d, nc, ncc)
    return pos, pl.ds(pl.multiple_of(pos * CH, CH), CH)


def _intra_fwd(q, k, v, bl, L, exch):
    T = q.shape[1]
    G = _chunks_per_step(T, INTRA_FWD_CHUNKS)
    nc, qkv_s, bl_s, big, pm, gl_s, shapes = _intra_specs(T, G)
    assert G == nc
    ncc = L // CH

    def body(q_ref, k_ref, v_ref, bl_ref, u_ref, w_ref, kd_ref, qd_ref, p_ref, gl_ref, x_ref):
        d, h = pl.program_id(0), pl.program_id(1)
        sb, sl = _dir_head_sel(d, h)
        rows = [slice(g * CH, (g + 1) * CH) for g in range(G)]
        outs = _intra_fn(_chunk_masks(d), sb, sl, *[[r[s, :] for s in rows] for r in (q_ref, k_ref, v_ref, bl_ref)])
        for g in range(G):
            pos, at = _chunk_at(g, d, nc, ncc)
            for r, o in zip((u_ref, w_ref, kd_ref, qd_ref, p_ref, x_ref), outs[:5] + outs[6:]):
                r[at, :] = o[g].astype(r.dtype)
            gl_ref[pos] = outs[5][g]

    return _call_carrying(body, exch, name="gdn_intra_fwd", out_shape=shapes, grid=(2, GH, nc // G),
                          in_specs=[qkv_s, qkv_s, qkv_s, bl_s], out_specs=(big, big, big, big, pm, gl_s, pm))(q, k, v, bl)


def _intra_bwd(q, k, v, bl, xinv, cts, L, exch):
    T = q.shape[1]
    G = _chunks_per_step(T, INTRA_BWD_CHUNKS)
    nc, qkv_s, bl_s, big, pm, gl_s, _ = _intra_specs(T, G)
    assert G == nc
    ncc = L // CH

    def body(q_ref, k_ref, v_ref, bl_ref, x_ref, du, dw, dkd, dqd, dp, dgl, dq_ref, dk_ref, dv_ref, dbl_ref):
        d, h = pl.program_id(0), pl.program_id(1)
        sb, sl = _dir_head_sel(d, h)
        rows = [slice(g * CH, (g + 1) * CH) for g in range(G)]
        places = [_chunk_at(g, d, nc, ncc) for g in range(G)]
        fn = functools.partial(_intra_fn, _chunk_masks(d), sb, sl, xs=[x_ref[at, :] for _, at in places])
        _, vjp = jax.vjp(fn, *[[r[s, :] for s in rows] for r in (q_ref, k_ref, v_ref, bl_ref)])
        cts = tuple([r[at, :] for _, at in places] for r in (du, dw, dkd, dqd, dp)) + ([dgl[pos] for pos, _ in places],)
        grads = vjp(cts)
        for g in range(G):
            for r, o in zip((dq_ref, dk_ref, dv_ref, dbl_ref), grads):
                r[rows[g], :] = o[g]

    return _call_carrying(body, exch, name="gdn_intra_bwd", out_shape=(_sds((2, GH, T, HD)),) * 4,
                          grid=(2, GH, nc // G), in_specs=[qkv_s, qkv_s, qkv_s, bl_s, pm, big, big, big, big, pm, gl_s],
                          out_specs=(big,) * 4)(q, k, v, bl, xinv, *cts)


def _scan_fn(s, u, w, kd, qd, p, gl):
    each = lambda f, *ls: [f(*t) for t in zip(*ls)]
    ws = each(_nn, w, s)
    delta = each(lambda a, b: a - b, u, ws)
    kdd = each(_tn, kd, delta)
    s_new = each(lambda g, a, b: g * a + b, gl, s, kdd)
    qs = each(_nn, qd, s)
    pd = each(_nn, p, delta)
    return each(lambda a, b: a + b, qs, pd), s_new


SCAN_BLOCK = 4


def _visit_pos(c, d, nc, ncc):
    back = ncc - 1 - c if c < ncc else ncc + (nc - 1 - c)
    return jnp.where(d == 0, c, back)


def _scan_specs(T, L, back):
    tb = SCAN_BLOCK * CH
    assert T % tb == 0 and L % tb == 0
    nb, ncb = T // tb, L // tb
    at = (lambda t: nb - 1 - t) if back else (lambda t: t)
    big = pl.BlockSpec((2, GH, tb, HD), lambda t: (0, 0, at(t), 0))
    pm = pl.BlockSpec((2, GH, tb, CH), lambda t: (0, 0, at(t), 0))
    gl = pl.BlockSpec((2, GH, SCAN_BLOCK, 1, HD), lambda t: (0, 0, at(t), 0, 0))
    st = pl.BlockSpec((2, GH, SCAN_BLOCK, HD, HD), lambda t: (0, 0, at(t), 0, 0))

    def natural(b):
        return jnp.where(b < ncb, ncb - 1 - b, nb - 1 - (b - ncb))

    do_specs = (pl.BlockSpec((GH, tb, HD), lambda t: (0, at(t), 0)),
                pl.BlockSpec((GH, tb, HD), lambda t: (0, natural(at(t)), 0)))
    return nb, big, pm, gl, st, do_specs


SCAN_STREAMS = [(d, h) for d in (0, 1) for h in range(GH)]


def _scan_fwd(u, w, kd, qd, p, gl, L):
    T = u.shape[2]
    nb, big, pm, gl_s, st, _ = _scan_specs(T, L, False)

    def body(u_ref, w_ref, kd_ref, qd_ref, p_ref, gl_ref, o_ref, st_ref, s_scr):
        @pl.when(pl.program_id(0) == 0)
        def _():
            s_scr[...] = jnp.zeros_like(s_scr)

        s = [s_scr[d, h] for d, h in SCAN_STREAMS]
        for i in range(SCAN_BLOCK):
            rows = slice(i * CH, (i + 1) * CH)
            for (d, h), sv in zip(SCAN_STREAMS, s):
                st_ref[d, h, i] = sv
            o, s = _scan_fn(s, *[[r[d, h, rows, :].astype(F32) for d, h in SCAN_STREAMS]
                                 for r in (u_ref, w_ref, kd_ref, qd_ref, p_ref)],
                            [gl_ref[d, h, i] for d, h in SCAN_STREAMS])
            for (d, h), ov in zip(SCAN_STREAMS, o):
                o_ref[d, h, rows, :] = ov
        for (d, h), sv in zip(SCAN_STREAMS, s):
            s_scr[d, h] = sv

    return _call(body, name="gdn_scan_fwd", out_shape=(_sds((2, GH, T, HD)), _sds((2, GH, T // CH, HD, HD))),
                 grid=(nb,), in_specs=[big, big, big, big, pm, gl_s], out_specs=(big, st),
                 scratch=[pltpu.VMEM((2, GH, HD, HD), F32)], sem=("arbitrary",), vmem=VMEM_BIG)(u, w, kd, qd, p, gl)


def _scan_bwd(u, w, kd, qd, p, gl, states, do, L, exch):
    T = u.shape[2]
    nb, big, pm, gl_s, st, do_specs = _scan_specs(T, L, True)

    def body(u_ref, w_ref, kd_ref, qd_ref, p_ref, gl_ref, st_ref, do0_ref, do1_ref,
             du_ref, dw_ref, dkd_ref, dqd_ref, dp_ref, dgl_ref, ds_scr):
        @pl.when(pl.program_id(0) == 0)
        def _():
            ds_scr[...] = jnp.zeros_like(ds_scr)

        ds = [ds_scr[d, h] for d, h in SCAN_STREAMS]
        for i in reversed(range(SCAN_BLOCK)):
            rows = slice(i * CH, (i + 1) * CH)
            mirror = slice((SCAN_BLOCK - 1 - i) * CH, (SCAN_BLOCK - i) * CH)
            _, vjp = jax.vjp(_scan_fn, [st_ref[d, h, i] for d, h in SCAN_STREAMS],
                             *[[r[d, h, rows, :].astype(F32) for d, h in SCAN_STREAMS]
                               for r in (u_ref, w_ref, kd_ref, qd_ref, p_ref)],
                             [gl_ref[d, h, i] for d, h in SCAN_STREAMS])
            dos = [do0_ref[h, rows, :] if d == 0 else do1_ref[h, mirror, :] for d, h in SCAN_STREAMS]
            ds, gu, gw, gkd, gqd, gp, ggl = vjp((dos, ds))
            for n, (d, h) in enumerate(SCAN_STREAMS):
                du_ref[d, h, rows, :] = gu[n]
                dw_ref[d, h, rows, :] = gw[n]
                dkd_ref[d, h, rows, :] = gkd[n]
                dqd_ref[d, h, rows, :] = gqd[n]
                dp_ref[d, h, rows, :] = gp[n]
                dgl_ref[d, h, i] = ggl[n]
        for (d, h), dv in zip(SCAN_STREAMS, ds):
            ds_scr[d, h] = dv

    return _call_carrying(
        body, exch, name="gdn_scan_bwd",
        out_shape=(_sds((2, GH, T, HD)),) * 4 + (_sds((2, GH, T, CH)), _sds((2, GH, T // CH, 1, HD))),
        grid=(nb,), in_specs=[big, big, big, big, pm, gl_s, st, *do_specs], out_specs=(big, big, big, big, pm, gl_s),
        scratch=[pltpu.VMEM((2, GH, HD, HD), F32)], vmem=VMEM_BIG)(u, w, kd, qd, p, gl, states, do, do)


def _gout_fn(o0, o1, z, gw):
    return _rms(o0 + o1) * gw * _silu(z)


def _backward_latent(o_ref, L):
    nl = (o_ref.shape[1] - L) // CH
    return jnp.concatenate([o_ref[1, L + (nl - 1 - j) * CH:L + (nl - j) * CH, :] for j in range(nl)], axis=0)


def _gout_fwd(o, proj, gw, L):
    T = o.shape[2]
    N = T - L
    ob = pl.BlockSpec((2, None, T, HD), lambda h: (0, h, 0, 0))

    def body(o_ref, z_ref, gw_ref, y_ref):
        y_ref[...] = _gout_fn(o_ref[0, L:, :], _backward_latent(o_ref, L), z_ref[L:, :], gw_ref[...]).astype(BF16)

    return _call(body, name="gout_fwd", out_shape=_sds((N, GH * HD), BF16), grid=(GH,),
                 in_specs=[ob, pl.BlockSpec((T, HD), lambda h: (0, C_Z // HD + h)), pl.BlockSpec((1, HD), lambda h: (0, 0))],
                 out_specs=pl.BlockSpec((N, HD), lambda h: (0, h)), sem=("parallel",))(o, proj, gw)


def _gout_bwd(o, proj, gw, dy, dproj, L):
    T = o.shape[2]
    N = T - L
    ob = pl.BlockSpec((2, None, T, HD), lambda h: (0, h, 0, 0))

    def body(o_ref, z_ref, gw_ref, dy_ref, _, do_ref, dz_ref, dgw_ref):
        _, vjp = jax.vjp(_gout_fn, o_ref[0, L:, :], _backward_latent(o_ref, L), z_ref[L:, :], gw_ref[...])
        g0, _, gz, ggw = vjp(dy_ref[...])
        do_ref[:L, :] = jnp.zeros((L, HD), F32)
        do_ref[L:, :] = g0
        dz_ref[:L, :] = jnp.zeros((L, HD), BF16)
        dz_ref[L:, :] = gz.astype(BF16)

        @pl.when(pl.program_id(0) == 0)
        def _():
            dgw_ref[...] = jnp.zeros_like(dgw_ref)

        dgw_ref[...] += ggw

    zb = pl.BlockSpec((T, HD), lambda h: (0, C_Z // HD + h))
    return _call(body, name="gout_bwd", out_shape=(_sds((GH, T, HD)), _sds(dproj.shape, BF16), _sds((1, HD))),
                 grid=(GH,),
                 in_specs=[ob, zb, pl.BlockSpec((1, HD), lambda h: (0, 0)), pl.BlockSpec((N, HD), lambda h: (0, h)), ANYSPEC],
                 out_specs=(pl.BlockSpec((None, T, HD), lambda h: (h, 0, 0)), zb, pl.BlockSpec((1, HD), lambda h: (0, 0))),
                 aliases={4: 1}, sem=("arbitrary",))(o, proj, gw, dy, dproj)


def _merge_fn(pa, pd, ga, gd):
    return jax.nn.sigmoid(ga) * pa + jax.nn.sigmoid(gd) * pd


def _merge_fwd(pa, pd, proj, L, *, br=256):
    N = pa.shape[0]
    lb = L // br
    row = pl.BlockSpec((br, D), lambda i: (i, 0))

    def body(pa_ref, pd_ref, ga_ref, gd_ref, y_ref):
        y_ref[...] = _merge_fn(pa_ref[...], pd_ref[...], ga_ref[...], gd_ref[...]).astype(BF16)

    return _call(body, name="merge_fwd", out_shape=_sds((N, D), BF16), grid=(N // br,),
                 in_specs=[row, row, pl.BlockSpec((br, D), lambda i: (i + lb, C_GATE // D)),
                           pl.BlockSpec((br, D), lambda i: (i + lb, C_GATE // D + 1))],
                 out_specs=row, sem=("parallel",))(pa, pd, proj, proj)


def _merge_bwd(pa, pd, proj, dy, L, *, br=256):
    N = pa.shape[0]
    T = N + L
    lb = L // br
    lrow = pl.BlockSpec((br, D), lambda i: (jnp.maximum(i - lb, 0), 0))

    def body(pa_ref, pd_ref, ga_ref, gd_ref, dy_ref, dpa_ref, dpd_ref, dg_ref):
        lat = pl.program_id(0) >= lb
        _, vjp = jax.vjp(_merge_fn, pa_ref[...], pd_ref[...], ga_ref[...], gd_ref[...])
        gpa, gpd, gga, ggd = vjp(dy_ref[...])
        dpa_ref[...] = gpa.astype(BF16)
        dpd_ref[...] = gpd.astype(BF16)
        dg_ref[:, :D] = jnp.where(lat, gga, 0.0).astype(BF16)
        dg_ref[:, D:] = jnp.where(lat, ggd, 0.0).astype(BF16)

    return _call(body, name="merge_bwd", out_shape=(_sds((N, D), BF16), _sds((N, D), BF16), _sds((T, C_END), BF16)),
                 grid=(T // br,),
                 in_specs=[lrow, lrow, pl.BlockSpec((br, D), lambda i: (i, C_GATE // D)),
                           pl.BlockSpec((br, D), lambda i: (i, C_GATE // D + 1)), lrow],
                 out_specs=(lrow, lrow, pl.BlockSpec((br, 2 * D), lambda i: (i, C_GATE // (2 * D)))),
                 sem=("arbitrary",))(pa, pd, proj, proj, dy)


def _resid_fwd(x, m, mod, i_g, *, name, br=256):
    R = x.shape[0]
    row = pl.BlockSpec((br, D), lambda i: (i, 0))

    def body(x_ref, m_ref, mod_ref, o_ref):
        o_ref[...] = x_ref[...] + mod_ref[i_g:i_g + 1, :] * m_ref[...]

    return _call(body, name=name, out_shape=_sds((R, D)), grid=(R // br,),
                 in_specs=[row, row, pl.BlockSpec((6, D), lambda i: (0, 0))], out_specs=row,
                 sem=("parallel",))(x, m, mod)


def _resid_bwd(dx, m, mod, i_g, *, name, br=256):
    R = dx.shape[0]
    row = pl.BlockSpec((br, D), lambda i: (i, 0))
    vec = pl.BlockSpec((1, D), lambda i: (0, 0))

    def body(dx_ref, m_ref, mod_ref, dm_ref, dg_ref):
        dxv = dx_ref[...]
        dm_ref[...] = (dxv * mod_ref[i_g:i_g + 1, :]).astype(BF16)

        @pl.when(pl.program_id(0) == 0)
        def _():
            dg_ref[...] = jnp.zeros_like(dg_ref)

        dg_ref[...] += jnp.sum(dxv * m_ref[...], axis=0, keepdims=True)

    return _call(body, name=name, out_shape=(_sds((R, D), BF16), _sds((1, D))), grid=(R // br,),
                 in_specs=[row, row, pl.BlockSpec((6, D), lambda i: (0, 0))], out_specs=(row, vec),
                 sem=("arbitrary",))(dx, m, mod)


def _ffn_fn(shifts, ug, uv, wg, wv, bg, bv):
    down, up = shifts

    def conv(x, w, b):
        return down(x) * w[0:1, :] + x * w[1:2, :] + up(x) * w[2:3, :] + b

    return _silu(conv(ug, wg, bg)) * conv(uv, wv, bv)


def _ffn_fwd(up, cw, cb, *, bw=256):
    N = up.shape[0]
    shifts = _make_shift(((0, N),))
    nb = DFF // bw

    def body(ug, uv, wg, wv, bg, bv, a_ref):
        a_ref[...] = _ffn_fn(shifts, ug[...], uv[...], wg[...], wv[...], bg[...], bv[...]).astype(BF16)

    def col(rows, off):
        return pl.BlockSpec((rows, bw), lambda j: (0, j + off))

    return _call(body, name="ffn_fwd", out_shape=_sds((N, DFF), BF16), grid=(nb,),
                 in_specs=[col(N, 0), col(N, nb), col(3, 0), col(3, nb), col(1, 0), col(1, nb)],
                 out_specs=col(N, 0), sem=("parallel",), vmem=VMEM_BIG)(up, up, cw, cw, cb, cb)


def _ffn_bwd(up, cw, cb, da, *, bw=256):
    N = up.shape[0]
    shifts = _make_shift(((0, N),))
    nb = DFF // bw

    def body(ug, uv, wg, wv, bg, bv, da_ref, dug, duv, dwg, dwv, dbg, dbv):
        _, vjp = jax.vjp(functools.partial(_ffn_fn, shifts), ug[...], uv[...], wg[...], wv[...], bg[...], bv[...])
        g = vjp(da_ref[...])
        dug[...] = g[0].astype(BF16)
        duv[...] = g[1].astype(BF16)
        dwg[...], dwv[...], dbg[...], dbv[...] = g[2], g[3], g[4], g[5]

    def col(rows, off):
        return pl.BlockSpec((rows, bw), lambda j: (0, j + off))

    half = (_sds((N, DFF), BF16), _sds((N, DFF), BF16), _sds((3, DFF)), _sds((3, DFF)), _sds((1, DFF)), _sds((1, DFF)))
    dug, duv, dwg, dwv, dbg, dbv = _call(
        body, name="ffn_bwd", out_shape=half, grid=(nb,),
        in_specs=[col(N, 0), col(N, nb), col(3, 0), col(3, nb), col(1, 0), col(1, nb), col(N, 0)],
        out_specs=(col(N, 0), col(N, 0), col(3, 0), col(3, 0), col(1, 0), col(1, 0)),
        sem=("parallel",), vmem=VMEM_BIG)(up, up, cw, cw, cb, cb, da)
    return (jnp.concatenate([dug, duv], axis=1), jnp.concatenate([dwg, dwv], axis=1),
            jnp.concatenate([dbg, dbv], axis=1))


def _head_fn(x1, dn, g2, fw, tgt):
    y = _rms(x1 + g2 * dn) * fw
    err = y - tgt
    return 0.5 * jnp.sum(jnp.mean(err * err, axis=-1))


def _head(x1, dn, mod, fw, tgt, *, br=256):
    N = x1.shape[0]
    row = pl.BlockSpec((br, D), lambda i: (i, 0))
    vec = pl.BlockSpec((1, D), lambda i: (0, 0))
    one = pl.BlockSpec((1, HD), lambda i: (0, 0))

    def body(x1_ref, dn_ref, mod_ref, fw_ref, tgt_ref, loss_ref, dx_ref, ddn_ref, dg_ref, dfw_ref):
        loss, (gx, gdn, gg, gfw) = jax.value_and_grad(_head_fn, argnums=(0, 1, 2, 3))(
            x1_ref[...], dn_ref[...], mod_ref[5:6, :], fw_ref[...], tgt_ref[...])
        dx_ref[...] = gx
        ddn_ref[...] = gdn.astype(BF16)

        @pl.when(pl.program_id(0) == 0)
        def _():
            loss_ref[...] = jnp.zeros_like(loss_ref)
            dg_ref[...] = jnp.zeros_like(dg_ref)
            dfw_ref[...] = jnp.zeros_like(dfw_ref)

        loss_ref[...] += jnp.broadcast_to(loss, (1, HD))
        dg_ref[...] += gg
        dfw_ref[...] += gfw

    return _call(body, name="head", out_shape=(_sds((1, HD)), _sds((N, D)), _sds((N, D), BF16), _sds((1, D)), _sds((1, D))),
                 grid=(N // br,), in_specs=[row, row, pl.BlockSpec((6, D), lambda i: (0, 0)), vec, row],
                 out_specs=(one, row, row, vec, vec), sem=("arbitrary",))(x1, dn, mod, fw, tgt)


def _adamw(w, g, m, v, *, name):
    shape = w.shape
    cols = shape[-1]
    rows = max(1, math.prod(shape[:-1]))
    w2, g2, m2, v2 = (t.reshape(rows, cols) for t in (w, g, m, v))
    br = 256 if rows % 256 == 0 else rows
    c1 = 1.0 - B1 ** STEP
    c2 = 1.0 - B2 ** STEP

    def body(w_ref, g_ref, m_ref, v_ref, d_ref, nm_ref, nv_ref):
        gv = g_ref[...]
        nm = B1 * m_ref[...] + (1.0 - B1) * gv
        nv = B2 * v_ref[...] + (1.0 - B2) * (gv * gv)
        d_ref[...] = -LR * ((nm / c1) / (jnp.sqrt(nv / c2) + AEPS) + WD * w_ref[...])
        nm_ref[...] = nm
        nv_ref[...] = nv

    blk = pl.BlockSpec((br, cols), lambda i: (i, 0))
    outs = _call(body, name=name, out_shape=(_sds((rows, cols)),) * 3, grid=(rows // br,),
                 in_specs=[blk] * 4, out_specs=(blk,) * 3, sem=("parallel",))(w2, g2, m2, v2)
    return tuple(t.reshape(shape) for t in outs)


def _adamw_many(items, *, name):
    k = len(items)
    shapes = [w.shape for w, _, _, _ in items]
    flat = [t.reshape(max(1, math.prod(t.shape[:-1])), t.shape[-1]) for it in items for t in it]
    c1 = 1.0 - B1 ** STEP
    c2 = 1.0 - B2 ** STEP

    def body(*refs):
        ins, outs = refs[:4 * k], refs[4 * k:]
        for i in range(k):
            w_ref, g_ref, m_ref, v_ref = ins[4 * i:4 * i + 4]
            gv = g_ref[...]
            nm = B1 * m_ref[...] + (1.0 - B1) * gv
            nv = B2 * v_ref[...] + (1.0 - B2) * (gv * gv)
            outs[3 * i][...] = -LR * ((nm / c1) / (jnp.sqrt(nv / c2) + AEPS) + WD * w_ref[...])
            outs[3 * i + 1][...] = nm
            outs[3 * i + 2][...] = nv

    res = _call(body, name=name, out_shape=tuple(_sds(flat[4 * i].shape) for i in range(k) for _ in range(3)))(*flat)
    return [tuple(res[3 * i + j].reshape(shapes[i]) for j in range(3)) for i in range(k)]


def _rope_tables(N, L):
    t = jnp.arange(N)
    pos = jnp.stack([(t // GRID_W).astype(F32), (t % GRID_W).astype(F32)], axis=1)
    inv = ROPE_THETA ** (-jnp.arange(0, HD // 2, 2, dtype=F32) / (HD // 2))
    ang = pos[:, :, None] * inv[None, None, :]
    cos = jnp.broadcast_to(jnp.cos(ang)[:, :, None, :], (N, 2, 2, HD // 4)).reshape(N, HD)
    sin = jnp.broadcast_to(jnp.sin(ang)[:, :, None, :], (N, 2, 2, HD // 4))
    sin = (sin * jnp.array([-1.0, 1.0], F32)[None, None, :, None]).reshape(N, HD)
    cos = jnp.concatenate([jnp.ones((L, HD), F32), cos], axis=0)
    sin = jnp.concatenate([jnp.zeros((L, HD), F32), sin], axis=0)
    return cos, sin


def _pad_lanes(v, off=0):
    return jnp.zeros((1, HD), F32).at[0, off:off + v.shape[0]].set(v)


def _local_step(x, ctx, tgt, mod_lat, mod_ctx, w_in, shards, small):
    N, L = x.shape[0], ctx.shape[0]
    T = N + L
    bounds = ((0, L), (L, T))
    qw, kw, gw = small["q_norm_w"], small["k_norm_w"], small["gdn_norm_w"]
    conv_w, ffn_w, ffn_b, fnw = small["conv_qkv_w"], small["ffn_conv_w"], small["ffn_conv_b"], small["final_norm_w"]
    alog = _pad_lanes(small["a_log"].reshape(-1), 2 * GH)
    dtb = _pad_lanes(small["dt_bias"].reshape(-1), 2 * GH)
    cos, sin = _rope_tables(N, L)
    bt = T
    bnl = 256 if N % 1024 else 1024

    hc = _normmod_fwd(ctx, mod_ctx, 0, 1, name="normmod_ctx")
    hx = _normmod_fwd(x, mod_lat, 0, 1, name="normmod_x")
    h1 = jnp.concatenate([hc, hx], axis=0)
    proj = _mm(h1, w_in, name="mm_in", M=T, N=C_END, K=D, tb=True, bm=bt, bn=1024)
    aq, ak, av = _aprep_fwd(proj, cos, sin, qw, kw)
    (attn, attn32, lse), (up_g,) = _attn_fwd(aq, ak, av, L, _GatherTwoLevel([shards["w_up"]]))
    gq = _gprep_fwd(proj, conv_w, 0, bounds)
    gk = _gprep_fwd(proj, conv_w, 1, bounds)
    gv = _gprep_fwd(proj, conv_w, 2, bounds)
    bl = _bl_fwd(proj, alog, dtb)
    intra, (down_g, pa_g, pd_g, out_g) = _intra_fwd(
        gq, gk, gv, bl, L, _GatherTwoLevel([shards[n] for n in ("w_down", "w_pa", "w_pd", "w_out")]))
    w_up, w_down = up_g.reshape(2 * DFF, D), down_g.reshape(DFF, D)
    w_pa, w_pd, w_out = pa_g.reshape(D, D), pd_g.reshape(D, D), out_g.reshape(D, D)
    xinv, intra = intra[6], intra[:6]
    o, states = _scan_fwd(*intra, L)
    gdn = _gout_fwd(o, proj, gw, L)
    pa = _mm(attn, w_pa, name="mm_pa", M=N, N=D, K=D, bm=bnl)
    pd = _mm(gdn, w_pd, name="mm_pd", M=N, N=D, K=D, bm=bnl)
    y = _merge_fwd(pa, pd, proj, L)
    m = _mm(y, w_out, name="mm_out", M=N, N=D, K=D, bm=bnl)
    x1 = _resid_fwd(x, m, mod_lat, 2, name="resid1")
    h2 = _normmod_fwd(x1, mod_lat, 3, 4, name="normmod_x1")
    up = _mm(h2, w_up, name="mm_up", M=N, N=2 * DFF, K=D, tb=True, bm=bnl, bn=2 * DFF // 4)
    a = _ffn_fwd(up, ffn_w, ffn_b)
    dn = _mm(a, w_down, name="mm_down", M=N, N=D, K=DFF, bm=bnl)
    loss, dx2, ddn, dg2, dfnw = _head(x1, dn, mod_lat, fnw, tgt)

    da = _mm(ddn, w_down, name="mm_down_dx", M=N, N=DFF, K=D, tb=True, bm=bnl, bn=DFF // 2)
    g_down = _mm(a, ddn, name="mm_down_dw", M=DFF, N=D, K=N, ta=True, bm=DFF // 2, out_dtype=BF16)
    dup, d_ffn_w, d_ffn_b = _ffn_bwd(up, ffn_w, ffn_b, da)
    dh2 = _mm(dup, w_up, name="mm_up_dx", M=N, N=D, K=2 * DFF, bm=bnl, bk=2 * DFF // 4)
    g_up = _mm(dup, h2, name="mm_up_dw", M=2 * DFF, N=D, K=N, ta=True, bm=2 * DFF // 4, out_dtype=BF16)
    dx1, dsh2, dsc2 = _normmod_bwd(x1, mod_lat, 3, 4, dh2, 0, dx2, name="normmod_x1_bwd")
    dm, dg1 = _resid_bwd(dx1, m, mod_lat, 2, name="resid1_bwd")
    dy = _mm(dm, w_out, name="mm_out_dx", M=N, N=D, K=D, tb=True, bm=bnl)
    g_out = _mm(y, dm, name="mm_out_dw", M=D, N=D, K=N, ta=True, out_dtype=BF16)
    dpa, dpd, dproj = _merge_bwd(pa, pd, proj, dy, L)
    dattn = _mm(dpa, w_pa, name="mm_pa_dx", M=N, N=D, K=D, tb=True, bm=bnl)
    g_pa = _mm(attn, dpa, name="mm_pa_dw", M=D, N=D, K=N, ta=True, out_dtype=BF16)
    dgdn = _mm(dpd, w_pd, name="mm_pd_dx", M=N, N=D, K=D, tb=True, bm=bnl)
    g_pd = _mm(gdn, dpd, name="mm_pd_dw", M=D, N=D, K=N, ta=True, out_dtype=BF16)
    do, dproj, dgw = _gout_bwd(o, proj, gw, dgdn, dproj, L)
    cts, recv_a = _scan_bwd(*intra, states, do, L, _Exchange(
        [g_out.reshape(NDEV, D // NDEV, D), g_pa.reshape(NDEV, D // NDEV, D), g_pd.reshape(NDEV, D // NDEV, D)], True))
    (dgq, dgk, dgv, dbl), recv_b = _intra_bwd(gq, gk, gv, bl, xinv, cts, L, _Exchange(
        [g_up.reshape(NDEV, 2 * DFF // NDEV, D)], True))
    dproj, dwq = _gprep_bwd(proj, conv_w, 0, bounds, dgq, dproj)
    dproj, dwk = _gprep_bwd(proj, conv_w, 1, bounds, dgk, dproj)
    dproj, dwv = _gprep_bwd(proj, conv_w, 2, bounds, dgv, dproj)
    dproj, dalog, ddtb = _bl_bwd(proj, alog, dtb, dbl, dproj)
    (daq_h, dak_h, dav_h), recv_c = _attn_bwd(aq, ak, av, attn32, lse, dattn, L, _Exchange(
        [g_down.reshape(NDEV, DFF // NDEV, D)], True))
    recv = dict(zip(("w_out", "w_pa", "w_pd", "w_up", "w_down"), recv_a + recv_b + recv_c))
    dproj, dqw, dkw = _aprep_bwd(proj, cos, sin, qw, kw, daq_h, dak_h, dav_h, dproj, L)
    g_in = _mm(dproj, h1, name="mm_in_dw", M=C_END, N=D, K=T, ta=True, bm=1024, out_dtype=BF16)
    *pending, token = _scatter_start(g_in, None, (0, D // 2), (), name="scatter_g_in_a_start")
    dh1 = _mm(dproj, w_in, name="mm_in_dx", M=T, N=D, K=C_END, bm=bt, bk=1024, after=(token,))
    grad_x, dsh1, dsc1 = _normmod_bwd(x, mod_lat, 0, 1, dh1, L, dx1, name="normmod_x_bwd")
    _, dcsh1, dcsc1 = _normmod_bwd(ctx, mod_ctx, 0, 1, dh1, 0, None, name="normmod_ctx_bwd")

    z1 = jnp.zeros((1, D), F32)
    dmod_lat = jnp.concatenate([dsh1, dsc1, dg1, dsh2, dsc2, dg2], axis=0)
    dmod_ctx = jnp.concatenate([dcsh1, dcsc1, z1, z1, z1, z1], axis=0)
    gsmall = {
        "q_norm_w": dqw, "k_norm_w": dkw, "gdn_norm_w": dgw,
        "conv_qkv_w": jnp.concatenate([dwq, dwk, dwv], axis=1),
        "a_log": dalog[0, 2 * GH:4 * GH], "dt_bias": ddtb[0, 2 * GH:4 * GH],
        "ffn_conv_w": d_ffn_w, "ffn_conv_b": d_ffn_b, "final_norm_w": dfnw,
    }
    return loss[0, 0], grad_x, pending, recv, dmod_lat, dmod_ctx, gsmall


HBM = pl.BlockSpec(memory_space=pltpu.HBM)
ANYSPEC = pl.BlockSpec(memory_space=pl.ANY)


def _position():
    x, y, c = lax.axis_index("x"), lax.axis_index("y"), lax.axis_index("c")
    return x, y, c, 4 * x + 2 * y + c


def _peer(x, y, c, k):
    px = 1 - x if k & 4 else x
    py = 1 - y if k & 2 else y
    pc = 1 - c if k & 1 else c
    return (px, py, pc), 4 * px + 2 * py + pc


def _exchange(arrs, *, name, scatter):
    exch = _Exchange(arrs, scatter)
    n = exch.n

    def body(*refs):
        ins, outs, sems = refs[:n], refs[n:2 * n], refs[2 * n:]
        exch.start(ins, outs, sems)
        exch.finish(ins, outs, sems)

    outs = pl.pallas_call(body, name=name, out_shape=exch.out_shape, in_specs=[HBM] * n, out_specs=(HBM,) * n,
                          scratch_shapes=exch.scratch,
                          compiler_params=pltpu.CompilerParams(has_side_effects=True))(*arrs)
    return list(outs)


class _Exchange:
    def __init__(self, arrs, scatter):
        self.arrs, self.scatter, self.n = list(arrs), scatter, len(arrs)
        self.out_shape = tuple(_sds(a.shape if scatter else (NDEV,) + a.shape, a.dtype) for a in arrs)
        self.scratch = [pltpu.SemaphoreType.DMA((self.n, NDEV - 1)), pltpu.SemaphoreType.DMA((self.n, NDEV - 1)),
                        pltpu.SemaphoreType.DMA((self.n,))]

    def _copies(self, ins, outs, sems):
        send, recv, loc = sems
        x, y, c, me = _position()
        local = [pltpu.make_async_copy(ins[a].at[me] if self.scatter else ins[a], outs[a].at[me], loc.at[a])
                 for a in range(self.n)]
        remote = []
        for k in range(1, NDEV):
            peer, pid = _peer(x, y, c, k)
            for a in range(self.n):
                src = ins[a].at[pid] if self.scatter else ins[a]
                remote.append(pltpu.make_async_remote_copy(
                    src_ref=src, dst_ref=outs[a].at[me], send_sem=send.at[a, k - 1], recv_sem=recv.at[a, k - 1],
                    device_id=peer, device_id_type=MESH))
        return local, remote

    def start(self, ins, outs, sems):
        local, remote = self._copies(ins, outs, sems)
        for cp in local + remote:
            cp.start()

    def finish(self, ins, outs, sems):
        local, remote = self._copies(ins, outs, sems)
        for cp in remote:
            cp.wait()
        for cp in local:
            cp.wait()


class _GatherTwoLevel:
    scatter = False

    def __init__(self, arrs):
        self.arrs, self.n = list(arrs), len(arrs)
        self.out_shape = tuple(_sds((NDEV,) + a.shape, a.dtype) for a in arrs)
        self.scratch = [pltpu.SemaphoreType.DMA((self.n, NDEV - 1)), pltpu.SemaphoreType.DMA((self.n, NDEV - 1)),
                        pltpu.SemaphoreType.DMA((self.n,))]

    def _parts(self, ins, outs, sems):
        send, recv, loc = sems
        x, y, c, _ = _position()
        me, sibling = (x, y, c), (x, y, 1 - c)
        chips = [(1 - x, y), (x, 1 - y), (1 - x, 1 - y)]
        parts = []
        for a in range(self.n):
            slot = lambda px, py, pc, a=a: outs[a].at[4 * px + 2 * py + pc]

            def copy(k, owner, to, src=None, a=a, slot=slot):
                return pltpu.make_async_remote_copy(
                    src_ref=slot(*owner) if src is None else src, dst_ref=slot(*owner), send_sem=send.at[a, k],
                    recv_sem=recv.at[a, k], device_id=to, device_id_type=MESH)

            parts.append(dict(
                mine=pltpu.make_async_copy(ins[a], slot(*me), loc.at[a]),
                first=[copy(0, me, sibling, src=ins[a])] + [copy(1 + j, me, (*ch, c), src=ins[a]) for j, ch in enumerate(chips)],
                arrive=[copy(1 + j, (*ch, c), me) for j, ch in enumerate(chips)],
                passed=[copy(4 + j, (*ch, c), sibling) for j, ch in enumerate(chips)],
                rest=[copy(0, sibling, me)] + [copy(4 + j, (*ch, 1 - c), me) for j, ch in enumerate(chips)]))
        return parts

    def start(self, ins, outs, sems):
        for p in self._parts(ins, outs, sems):
            p["mine"].start()
            for cp in p["first"]:
                cp.start()

    def middle(self, ins, outs, sems):
        for p in self._parts(ins, outs, sems):
            for got, fwd in zip(p["arrive"], p["passed"]):
                got.wait_recv()
                fwd.start()

    def finish(self, ins, outs, sems):
        for p in self._parts(ins, outs, sems):
            for cp in p["rest"]:
                cp.wait_recv()
            for cp in p["first"] + p["passed"]:
                cp.wait_send()
            p["mine"].wait()


def _gather_two_level(blocks, *, name):
    exch = _GatherTwoLevel(blocks)
    n = exch.n

    def body(*refs):
        ins, outs, sems = refs[:n], refs[n:2 * n], refs[2 * n:]
        exch.start(ins, outs, sems)
        exch.middle(ins, outs, sems)
        exch.finish(ins, outs, sems)

    outs = pl.pallas_call(body, name=name, out_shape=exch.out_shape, in_specs=[HBM] * n, out_specs=(HBM,) * n,
                          scratch_shapes=exch.scratch,
                          compiler_params=pltpu.CompilerParams(has_side_effects=True))(*blocks)
    return list(outs)


SEM = pl.BlockSpec(memory_space=pltpu.SEMAPHORE)


SHARD_ROWS = W_END // NDEV
RUNS = ((0, W_QKV, C_KV), (W_QKV, W_AQ - W_QKV, C_QKV), (W_AQ, W_Z - W_AQ, C_AQ), (W_Z, W_END - W_Z, C_Z))


ROW_TILE = 8
SLOT_ROWS = -(-SHARD_ROWS // ROW_TILE) * ROW_TILE


def _shard_pieces(d):
    lo, hi = d * SHARD_ROWS, (d + 1) * SHARD_ROWS
    lead = lo % ROW_TILE
    pieces = []
    for first, rows, padded in RUNS:
        a, b = max(lo, first), min(hi, first + rows)
        if a < b:
            pieces.append([a - lo + lead, b - a, padded + a - first])
    pieces[0] = [0, pieces[0][1] + lead, pieces[0][2] - lead]
    pieces[-1][1] = SLOT_ROWS - pieces[-1][0]
    assert all(v % ROW_TILE == 0 for p in pieces for v in p) and all(p[2] + p[1] <= C_END for p in pieces)
    return pieces


def _scatter_send(src_ref, land_ref, send_sems, recv_sems, cols):
    _, _, _, me = _position()
    for d in range(NDEV):
        @pl.when(me != d)
        def _():
            k = jnp.bitwise_xor(me, d)
            peer = tuple(jnp.int32((d >> s) & 1) for s in (2, 1, 0))
            for off, rows, padded in _shard_pieces(d):
                pltpu.make_async_remote_copy(
                    src_ref=src_ref.at[pl.ds(padded, rows), pl.ds(*cols)],
                    dst_ref=land_ref.at[me].at[pl.ds(off, rows), pl.ds(*cols)], send_sem=send_sems.at[k - 1],
                    recv_sem=recv_sems.at[k - 1], device_id=peer, device_id_type=MESH).start()


def _scatter_whole(src_ref, land_ref, send_sems, recv_sems, cols):
    x, y, c, me = _position()
    span = (slice(None), pl.ds(*cols))
    copies = []
    for k in range(1, NDEV):
        peer, _ = _peer(x, y, c, k)
        copies.append(pltpu.make_async_remote_copy(
            src_ref=src_ref.at[pl.ds(0, SLOT_ROWS)].at[span], dst_ref=land_ref.at[me].at[span],
            send_sem=send_sems.at[k - 1], recv_sem=recv_sems.at[k - 1], device_id=peer, device_id_type=MESH))
    return copies


SPLIT_EFFECT = pltpu.SideEffectType.DATAFLOW_SIDE_EFFECTING


def _scatter_start(parts, land, cols, after, *, name):
    na = len(after)
    if land is None:
        land = lax.empty((NDEV, SLOT_ROWS, D), parts.dtype)

    def body(src_ref, land_ref, *rest):
        send_sems, recv_sems, _, _, token = rest[na:]
        _scatter_send(src_ref, land_ref, send_sems, recv_sems, cols)
        token[...] = jnp.zeros_like(token)

    return pl.pallas_call(
        body, name=name,
        out_shape=(pltpu.SemaphoreType.DMA((NDEV - 1,)), pltpu.SemaphoreType.DMA((NDEV - 1,)),
                   pltpu.HBM(parts.shape, parts.dtype), pltpu.HBM(land.shape, land.dtype), _sds((8, HD))),
        in_specs=(HBM, HBM) + (pl.BlockSpec(memory_space=pl.ANY),) * na,
        out_specs=(SEM, SEM, HBM, HBM, pl.BlockSpec(memory_space=pltpu.VMEM)),
        input_output_aliases={0: 2, 1: 3}, compiler_params=pltpu.CompilerParams(has_side_effects=SPLIT_EFFECT),
    )(pltpu.with_memory_space_constraint(parts, pltpu.HBM), pltpu.with_memory_space_constraint(land, pltpu.HBM), *after)


def _scatter_wait(send_sems, recv_sems, src_thru, land_thru, cols, after, *, name):
    na = len(after)

    def body(src_ref, land_ref, send_sems, recv_sems, *rest):
        for cp in _scatter_whole(src_ref, land_ref, send_sems, recv_sems, cols):
            cp.wait_send()
            cp.wait_recv()

    return pl.pallas_call(
        body, name=name,
        out_shape=(pltpu.HBM(src_thru.shape, src_thru.dtype), pltpu.HBM(land_thru.shape, land_thru.dtype)),
        in_specs=(HBM, HBM, SEM, SEM) + (pl.BlockSpec(memory_space=pl.ANY),) * na, out_specs=(HBM, HBM),
        input_output_aliases={0: 0, 1: 1}, compiler_params=pltpu.CompilerParams(has_side_effects=SPLIT_EFFECT),
    )(src_thru, land_thru, send_sems, recv_sems, *after)


def _cast_bf16(ws, *, name):
    k = len(ws)

    def body(*refs):
        for w_ref, o_ref in zip(refs[:k], refs[k:]):
            o_ref[...] = w_ref[...].astype(BF16)

    return _call(body, name=name, out_shape=tuple(_sds(w.shape, BF16) for w in ws), vmem=VMEM_BIG)(*ws)


def _sum_slots(a, *, name):
    _, R, C = a.shape

    def body(a_ref, o_ref):
        s = a_ref[0]
        for d in range(1, NDEV):
            s = s + a_ref[d]
        o_ref[...] = s

    return _call(body, name=name, out_shape=_sds((R, C)))(a)


MODROWS = 16


def _mod_fwd(c9, w, b):
    cols = w.shape[1]

    def body(c_ref, w_ref, b_ref, o_ref):
        o_ref[...] = _nn(_silu(c_ref[...]), w_ref[...]) + b_ref[...]

    return _call(body, name="mod_fwd", out_shape=_sds((MODROWS, cols)))(c9, w, b)


def _mod_bwd(c9, dmy, dall, w):
    cols = w.shape[1]

    def body(c_ref, dmy_ref, dall_ref, w_ref, gw_ref, gb_ref, cp_ref):
        sc = _silu(c_ref[...])
        rows = lax.broadcasted_iota(jnp.int32, (MODROWS, 1), 0)
        d = dmy_ref[...]
        d_ctx = jnp.where(rows == NDEV, d, 0.0)
        sc_ctx = jnp.where(rows == NDEV, sc, 0.0)
        outer = lax.dot_general(sc_ctx, d_ctx, (((0,), (0,)), ((), ())), precision=HI, preferred_element_type=F32)
        gw_ref[...] = _tn(jnp.where(rows < NDEV, sc, 0.0), jnp.where(rows < NDEV, d, 0.0)) + outer
        gb_ref[...] = jnp.sum(dall_ref[...], axis=0, keepdims=True)
        cp_ref[...] = jnp.sum(_nt(d_ctx, w_ref[...]), axis=0, keepdims=True)

    return _call(body, name="mod_bwd", out_shape=(_sds((D, cols)), _sds((1, 6 * D)), _sds((1, D))),
                 vmem=VMEM_BIG)(c9, dmy, dall, w)


def _cctx_finish(parts, c_ctx, after):
    VM = pl.BlockSpec(memory_space=pltpu.VMEM)

    def body(p_ref, c_ref, *rest):
        o_ref = rest[-1]
        s = p_ref[0]
        for d in range(1, NDEV):
            s = s + p_ref[d]
        _, vjp = jax.vjp(_silu, c_ref[...])
        o_ref[...] = vjp(s)[0]

    return _call(body, name="cctx_finish", out_shape=_sds((1, D)),
                 in_specs=[VM, VM] + [pl.BlockSpec(memory_space=pl.ANY)] * len(after))(parts, c_ctx, *after)


def _adamw_recv(w, recv, m, v, *, name, own=None):
    rows, cols = w.shape
    slot_rows = recv.shape[1]
    lead = slot_rows - rows
    assert rows % ROW_TILE in (0, lead)
    bc = 256
    c1 = 1.0 - B1 ** STEP
    c2 = 1.0 - B2 ** STEP
    has_own = own is not None

    def body(w_ref, r_ref, m_ref, v_ref, *rest):
        g_ref, d_ref, nm_ref, nv_ref = rest[-4:]
        me = _position()[3]

        def slot(d):
            return jnp.where(me == d, rest[0][...], r_ref[d]) if has_own else r_ref[d]

        gv = slot(0).astype(F32)
        for d in range(1, NDEV):
            gv = gv + slot(d).astype(F32)
        if lead:
            gv = jnp.where((me * rows) % ROW_TILE == 0, gv[:rows], gv[lead:])
        nm = B1 * m_ref[...] + (1.0 - B1) * gv
        nv = B2 * v_ref[...] + (1.0 - B2) * (gv * gv)
        g_ref[...] = gv
        d_ref[...] = -LR * ((nm / c1) / (jnp.sqrt(nv / c2) + AEPS) + WD * w_ref[...])
        nm_ref[...] = nm
        nv_ref[...] = nv

    blk = pl.BlockSpec((rows, bc), lambda j: (0, j))
    return _call(body, name=name, out_shape=(_sds((rows, cols)),) * 4, grid=(cols // bc,),
                 in_specs=[blk, pl.BlockSpec((NDEV, slot_rows, bc), lambda j: (0, 0, j)), blk, blk]
                 + [pl.BlockSpec((slot_rows, bc), lambda j: (0, j))] * has_own,
                 out_specs=(blk,) * 4, sem=("parallel",), vmem=VMEM_BIG)(w, recv, m, v, *([own] if has_own else []))


P_LAT, P_CTX, P_FNW, P_FFNB, P_CONV, P_FFNW, P_MISC, P_ROWS = 0, 8, 16, 24, 32, 48, 72, 80


def _rows_of(v, nrows):
    flat = v.reshape(-1)
    return jnp.pad(flat, (0, nrows * D - flat.shape[0])).reshape(nrows, D)


def _by_columns(g):
    n, r, c = g.shape
    return jnp.transpose(g, (1, 0, 2)).reshape(r, n * c)


def kernel(x, c, ctx, c_ctx, w_mod, b_mod, w_in, q_norm_w, k_norm_w, conv_qkv_w, a_log, dt_bias, gdn_norm_w, w_pa, w_pd, w_out, w_up, ffn_conv_w, ffn_conv_b, w_down, final_norm_w, loss_target, m_c_ctx, m_w_mod, m_b_mod, m_w_in, m_q_norm_w, m_k_norm_w, m_conv_qkv_w, m_a_log, m_dt_bias, m_gdn_norm_w, m_w_pa, m_w_pd, m_w_out, m_w_up, m_ffn_conv_w, m_ffn_conv_b, m_w_down, m_final_norm_w, v_c_ctx, v_w_mod, v_b_mod, v_w_in, v_q_norm_w, v_k_norm_w, v_conv_qkv_w, v_a_log, v_dt_bias, v_gdn_norm_w, v_w_pa, v_w_pd, v_w_out, v_w_up, v_ffn_conv_w, v_ffn_conv_b, v_w_down, v_final_norm_w):
    _, _, _, me = _position()
    mcols = w_mod.shape[2]

    transposed = ("w_in", "w_up")
    big = {"w_in": w_in[0].T, "w_pa": w_pa[0], "w_pd": w_pd[0], "w_out": w_out[0], "w_up": w_up[0].T, "w_down": w_down[0]}
    names = list(big)
    shards = dict(zip(names, _cast_bf16([big[n] for n in names], name="cast_weights")))
    w_in_g, c_all, conv_g, ffnw_g = _gather_two_level([shards["w_in"], c, conv_qkv_w[0], ffn_conv_w[0]],
                                                      name="gather_w_in")
    w_in_full = w_in_g.reshape(W_END, D)
    w_in_pad = _pad_columns(w_in_full)

    c9 = jnp.concatenate([c_all.reshape(NDEV, D), jnp.pad(c_ctx[None], ((0, MODROWS - NDEV - 1), (0, 0)))], axis=0)
    b_loc = lax.dynamic_slice(b_mod, (0, me * mcols), (1, mcols))
    mod_all, = _exchange([_mod_fwd(c9, w_mod[0], b_loc)], name="gather_mod", scatter=False)
    mod_lat = lax.dynamic_index_in_dim(mod_all, me, axis=1, keepdims=False).reshape(6, D)
    mod_ctx = mod_all[:, NDEV, :].reshape(6, D)

    small = {"q_norm_w": q_norm_w, "k_norm_w": k_norm_w, "gdn_norm_w": gdn_norm_w, "a_log": a_log, "dt_bias": dt_bias,
             "conv_qkv_w": _by_columns(conv_g), "ffn_conv_w": _by_columns(ffnw_g), "ffn_conv_b": ffn_conv_b,
             "final_norm_w": final_norm_w[None]}
    loss_me, grad_x, pending_in, recv, dmod_lat, dmod_ctx, gs = _local_step(
        x[0], ctx[0], loss_target[0], mod_lat, mod_ctx, w_in_pad, shards, small)

    moments = {"w_in": (m_w_in, v_w_in), "w_pa": (m_w_pa, v_w_pa), "w_pd": (m_w_pd, v_w_pd),
               "w_out": (m_w_out, v_w_out), "w_up": (m_w_up, v_w_up), "w_down": (m_w_down, v_w_down)}
    res = {}
    def finish(n, outs):
        return tuple((t.T if n in transposed else t)[None] for t in outs)

    def moment(t, n):
        return t[0].T if n in transposed else t[0]

    for n in recv:
        res[n] = finish(n, _adamw_recv(big[n], recv[n], moment(moments[n][0], n), moment(moments[n][1], n),
                                       name="adamw_" + n))

    misc = jnp.concatenate([gs["q_norm_w"][0], gs["k_norm_w"][0], gs["gdn_norm_w"][0], gs["a_log"], gs["dt_bias"],
                            loss_me[None]])
    pack = jnp.concatenate([_rows_of(dmod_lat, P_CTX - P_LAT), _rows_of(dmod_ctx, P_FNW - P_CTX),
                            _rows_of(gs["final_norm_w"], P_FFNB - P_FNW), _rows_of(gs["ffn_conv_b"], P_CONV - P_FFNB),
                            _rows_of(gs["conv_qkv_w"], P_FFNW - P_CONV), _rows_of(gs["ffn_conv_w"], P_MISC - P_FFNW),
                            _rows_of(misc, P_ROWS - P_MISC)], axis=0)
    pack_all, = _exchange([pack], name="gather_pack", scatter=False)
    tot = _sum_slots(pack_all, name="sum_pack")
    dall = jnp.concatenate([pack_all[:, P_LAT:P_LAT + 6, :].reshape(NDEV, 6 * D),
                            jnp.pad(tot[P_CTX:P_CTX + 6].reshape(1, 6 * D), ((0, MODROWS - NDEV - 1), (0, 0)))], axis=0)
    dmy = lax.dynamic_slice(dall, (0, me * mcols), (MODROWS, mcols))
    g_w_mod, g_b_mod, cpart = _mod_bwd(c9, dmy, dall, w_mod[0])
    cparts, = _exchange([cpart], name="gather_cctx", scatter=False)
    sems_a, land = pending_in[:2], pending_in[3]
    *sems_b, g_in_thru, land, token_b = _scatter_start(pending_in[2], land, (D // 2, D // 2), (cparts,),
                                                       name="scatter_g_in_b_start")
    g_c_ctx = _cctx_finish(cparts, c_ctx[None], (token_b,))[0]

    nconv, nffn = 3 * GH * HD, 2 * DFF
    conv_tot = tot[P_CONV:P_FFNW].reshape(-1)[:3 * nconv].reshape(3, nconv)
    ffnw_tot = tot[P_FFNW:P_MISC].reshape(-1)[:3 * nffn].reshape(3, nffn)
    mrow = tot[P_MISC]
    grads = {
        "c_ctx": g_c_ctx, "w_mod": g_w_mod[None], "b_mod": g_b_mod,
        "q_norm_w": mrow[None, 0:HD], "k_norm_w": mrow[None, HD:2 * HD], "gdn_norm_w": mrow[None, 2 * HD:3 * HD],
        "conv_qkv_w": lax.dynamic_slice(conv_tot, (0, me * (nconv // NDEV)), (3, nconv // NDEV))[None],
        "a_log": mrow[3 * HD:3 * HD + 2 * GH].reshape(1, 2, GH),
        "dt_bias": mrow[3 * HD + 2 * GH:3 * HD + 4 * GH].reshape(1, 2, GH),
        "ffn_conv_w": lax.dynamic_slice(ffnw_tot, (0, me * (nffn // NDEV)), (3, nffn // NDEV))[None],
        "ffn_conv_b": tot[P_FFNB:P_CONV].reshape(-1)[:nffn][None],
        "final_norm_w": tot[P_FNW],
    }
    loss = mrow[3 * HD + 4 * GH]
    given = {"c_ctx": (c_ctx, m_c_ctx, v_c_ctx), "w_mod": (w_mod, m_w_mod, v_w_mod), "b_mod": (b_mod, m_b_mod, v_b_mod),
             "q_norm_w": (q_norm_w, m_q_norm_w, v_q_norm_w), "k_norm_w": (k_norm_w, m_k_norm_w, v_k_norm_w),
             "conv_qkv_w": (conv_qkv_w, m_conv_qkv_w, v_conv_qkv_w), "a_log": (a_log, m_a_log, v_a_log),
             "dt_bias": (dt_bias, m_dt_bias, v_dt_bias), "gdn_norm_w": (gdn_norm_w, m_gdn_norm_w, v_gdn_norm_w),
             "ffn_conv_w": (ffn_conv_w, m_ffn_conv_w, v_ffn_conv_w), "ffn_conv_b": (ffn_conv_b, m_ffn_conv_b, v_ffn_conv_b),
             "final_norm_w": (final_norm_w, m_final_norm_w, v_final_norm_w)}
    res["w_mod"] = (grads["w_mod"],) + _adamw(w_mod, grads["w_mod"], m_w_mod, v_w_mod, name="adamw_w_mod")
    small_names = [n for n in given if n != "w_mod"]
    updates = _adamw_many([(given[n][0], grads[n], given[n][1], given[n][2]) for n in small_names], name="adamw_small")
    for n, upd in zip(small_names, updates):
        res[n] = (grads[n],) + upd

    first = me * SHARD_ROWS
    own_in = lax.dynamic_slice(_unpad_columns(g_in_thru), (first - first % ROW_TILE, 0), (SLOT_ROWS, D))
    mine = (big["w_in"], moment(m_w_in, "w_in"), moment(v_w_in, "w_in"))
    g_in_thru, land = _scatter_wait(*sems_a, g_in_thru, land, (0, D // 2), [res[n][1] for n in res] + [own_in, *mine],
                                    name="scatter_g_in_a_wait")
    _, land = _scatter_wait(*sems_b, g_in_thru, land, (D // 2, D // 2), (), name="scatter_g_in_b_wait")
    res["w_in"] = finish("w_in", _adamw_recv(mine[0], land, mine[1], mine[2], name="adamw_w_in", own=own_in))

    order = ["c_ctx", "w_mod", "b_mod", "w_in", "q_norm_w", "k_norm_w", "conv_qkv_w", "a_log", "dt_bias", "gdn_norm_w",
             "w_pa", "w_pd", "w_out", "w_up", "ffn_conv_w", "ffn_conv_b", "w_down", "final_norm_w"]
    return (loss, grad_x[None], *[res[n][0] for n in order], *[res[n][1] for n in order],
            *[res[n][2] for n in order], *[res[n][3] for n in order])
```

```python
import functools
import math

import jax
import jax.numpy as jnp
from jax import lax
from jax.experimental import pallas as pl
from jax.experimental.pallas import tpu as pltpu

F32 = jnp.float32
BF16 = jnp.bfloat16
HI = lax.Precision.HIGHEST
MESH = pl.DeviceIdType.MESH

NDEV = 8
D = 1024
HD = 128
AH, AKV, GRP = 8, 2, 4
GH = 8
CH = 64
DFF = 2816
GRID_W = 64
EPS = 1e-6
ROPE_THETA = 10000.0
LOG2E = math.log2(math.e)
C_KV, C_AQ, C_QKV, C_BL, C_Z, C_GATE, C_END = 0, 512, 1536, 4608, 5120, 6144, 8192
W_QKV, W_AQ, W_Z, W_END = 512, 3616, 4640, 7712


def _pad_columns(w):
    zeros = jnp.zeros((C_Z - C_QKV - (W_AQ - W_QKV), D), w.dtype)
    return jnp.concatenate([w[:W_QKV], w[W_AQ:W_Z], w[W_QKV:W_AQ], zeros, w[W_Z:]], axis=0)


def _unpad_columns(g):
    return jnp.concatenate([g[:C_AQ], g[C_QKV:C_QKV + W_AQ - W_QKV], g[C_AQ:C_QKV], g[C_Z:]], axis=0)
LR, B1, B2, AEPS, WD, STEP = 0.001, 0.9, 0.999, 1e-08, 0.01, 10
VMEM_BIG = 56 * 1024 * 1024
INTRA_FWD_CHUNKS = 36
INTRA_BWD_CHUNKS = 36


def _call(body, *, name, out_shape, grid=None, in_specs=None, out_specs=None, scratch=(), sem=None,
          vmem=None, aliases=None):
    params = {}
    if sem is not None:
        params["dimension_semantics"] = sem
    if vmem is not None:
        params["vmem_limit_bytes"] = vmem
    kw = {}
    if grid is not None:
        kw["grid"] = grid
    if in_specs is not None:
        kw["in_specs"] = in_specs
    if out_specs is not None:
        kw["out_specs"] = out_specs
    if aliases:
        kw["input_output_aliases"] = aliases
    return pl.pallas_call(body, name=name, out_shape=out_shape, scratch_shapes=list(scratch),
                          compiler_params=pltpu.CompilerParams(**params), **kw)


def _call_carrying(body, exch, *, name, out_shape, grid, in_specs, out_specs, scratch=(), vmem=None):
    n, nin, nout, nscr = exch.n, len(in_specs), len(out_shape), len(scratch)
    steps = math.prod(grid)
    mid = (2 * steps) // 3

    def wrapped(*refs):
        ins, cins = refs[:nin], refs[nin:nin + n]
        outs, couts = refs[nin + n:nin + n + nout], refs[nin + n + nout:nin + 2 * n + nout]
        scr, sems = refs[nin + 2 * n + nout:nin + 2 * n + nout + nscr], refs[nin + 2 * n + nout + nscr:]
        ids = [pl.program_id(i) for i in range(len(grid))]
        first = functools.reduce(jnp.logical_and, [i == 0 for i in ids])
        last = functools.reduce(jnp.logical_and, [i == g - 1 for i, g in zip(ids, grid)])

        @pl.when(first)
        def _():
            exch.start(cins, couts, sems)

        if hasattr(exch, "middle"):
            linear = functools.reduce(lambda acc, ig: acc * ig[1] + ig[0], zip(ids, grid), 0)

            @pl.when(linear == mid)
            def _():
                exch.middle(cins, couts, sems)

        body(*ins, *outs, *scr)

        @pl.when(last)
        def _():
            exch.finish(cins, couts, sems)

    params = {"dimension_semantics": ("arbitrary",) * len(grid)}
    if vmem is not None:
        params["vmem_limit_bytes"] = vmem
    fn = pl.pallas_call(wrapped, name=name, out_shape=tuple(out_shape) + exch.out_shape, grid=grid,
                        in_specs=list(in_specs) + [HBM] * n, out_specs=tuple(out_specs) + (HBM,) * n,
                        scratch_shapes=list(scratch) + exch.scratch, compiler_params=pltpu.CompilerParams(**params))

    def run(*args):
        res = fn(*args, *exch.arrs)
        return res[:nout], list(res[nout:])

    return run


def _sds(shape, dtype=F32):
    return jax.ShapeDtypeStruct(tuple(shape), dtype)


def _dot(a, b, ca, cb):
    return lax.dot_general(a.astype(BF16), b.astype(BF16), (((ca,), (cb,)), ((), ())),
                           preferred_element_type=F32)


@jax.custom_vjp
def _nn(a, b):
    return _dot(a, b, 1, 0)


@jax.custom_vjp
def _nt(a, b):
    return _dot(a, b, 1, 1)


@jax.custom_vjp
def _tn(a, b):
    return _dot(a, b, 0, 0)


_nn.defvjp(lambda a, b: (_nn(a, b), (a, b)), lambda r, g: (_nt(g, r[1]), _tn(r[0], g)))
_nt.defvjp(lambda a, b: (_nt(a, b), (a, b)), lambda r, g: (_nn(g, r[1]), _tn(g, r[0])))
_tn.defvjp(lambda a, b: (_tn(a, b), (a, b)), lambda r, g: (_nt(r[1], g), _nn(r[0], g)))


def _mdot(a, b):
    return jnp.dot(a, b, precision=lax.Precision.HIGH, preferred_element_type=F32)


def _maskdot(mask, a, cm):
    hi = a.astype(BF16)
    r = a - hi.astype(F32)
    mid = r.astype(BF16)
    lo = (r - mid.astype(F32)).astype(BF16)
    mb = mask.astype(BF16)
    dims = (((cm,), (0,)), ((), ()))
    return (lax.dot_general(mb, hi, dims, preferred_element_type=F32)
            + lax.dot_general(mb, mid, dims, preferred_element_type=F32)
            + lax.dot_general(mb, lo, dims, preferred_element_type=F32))


@jax.custom_vjp
def _mask_nn(mask, a):
    return _maskdot(mask, a, 1)


_mask_nn.defvjp(lambda mask, a: (_maskdot(mask, a, 1), mask),
                lambda mask, g: (jnp.zeros_like(mask), _maskdot(mask, g, 0)))


@jax.custom_vjp
def _saved_inverse(lmat, x):
    return x


def _saved_inverse_bwd(x, g):
    t = lax.dot_general(x, g, (((0,), (0,)), ((), ())), precision=lax.Precision.HIGH, preferred_element_type=F32)
    dl = lax.dot_general(t, x, (((1,), (1,)), ((), ())), precision=lax.Precision.HIGH, preferred_element_type=F32)
    return -dl, jnp.zeros_like(x)


_saved_inverse.defvjp(lambda lmat, x: (x, x), _saved_inverse_bwd)


def _row_ids(shape):
    return lax.broadcasted_iota(jnp.int32, shape, 0)


def _shift_rows(x, down, bounds):
    n = x.shape[0]
    rows = _row_ids(x.shape)
    y = pltpu.roll(x, 1 if down else n - 1, 0)
    edge = functools.reduce(jnp.logical_or, [rows == (s if down else e - 1) for s, e in bounds])
    return jnp.where(edge, 0.0, y)


def _make_shift(bounds):
    @jax.custom_vjp
    def down(x):
        return _shift_rows(x, True, bounds)

    @jax.custom_vjp
    def up(x):
        return _shift_rows(x, False, bounds)

    down.defvjp(lambda x: (down(x), None), lambda _, g: (up(g),))
    up.defvjp(lambda x: (up(x), None), lambda _, g: (down(g),))
    return down, up


@jax.custom_vjp
def _swap32(x):
    lane = lax.broadcasted_iota(jnp.int32, x.shape, x.ndim - 1)
    return jnp.where((lane % 64) < 32, pltpu.roll(x, HD - 32, x.ndim - 1), pltpu.roll(x, 32, x.ndim - 1))


_swap32.defvjp(lambda x: (_swap32(x), None), lambda _, g: (_swap32(g),))


def _rms(x):
    return x * lax.rsqrt(jnp.mean(x * x, axis=-1, keepdims=True) + EPS)


def _silu(x):
    return x * jax.nn.sigmoid(x)


def _mm(a, b, *, name, M, N, K, ta=False, tb=False, out_dtype=F32, bm=None, bn=None, bk=None, after=()):
    bm, bn, bk = bm or M, bn or N, bk or K
    assert M % bm == 0 and N % bn == 0 and K % bk == 0, (name, M, N, K, bm, bn, bk)
    nk = K // bk
    ca, cb = (0 if ta else 1), (1 if tb else 0)
    na = len(after)

    def body(a_ref, b_ref, *rest):
        o_ref, acc = rest[na], rest[na + 1:]
        r = _dot(a_ref[...], b_ref[...], ca, cb)
        if nk == 1:
            o_ref[...] = r.astype(out_dtype)
        else:
            acc_ref, = acc
            k = pl.program_id(2)

            @pl.when(k == 0)
            def _():
                acc_ref[...] = r

            @pl.when(k > 0)
            def _():
                acc_ref[...] += r

            @pl.when(k == nk - 1)
            def _():
                o_ref[...] = acc_ref[...].astype(out_dtype)

    a_spec = pl.BlockSpec((bk, bm), lambda i, j, k: (k, i)) if ta else pl.BlockSpec((bm, bk), lambda i, j, k: (i, k))
    b_spec = pl.BlockSpec((bn, bk), lambda i, j, k: (j, k)) if tb else pl.BlockSpec((bk, bn), lambda i, j, k: (k, j))
    return _call(body, name=name, out_shape=_sds((M, N), out_dtype), grid=(M // bm, N // bn, nk),
                 in_specs=[a_spec, b_spec] + [pl.BlockSpec(memory_space=pl.ANY)] * na,
                 out_specs=pl.BlockSpec((bm, bn), lambda i, j, k: (i, j)),
                 scratch=[pltpu.VMEM((bm, bn), F32)] if nk > 1 else [],
                 sem=("parallel", "parallel", "arbitrary"), vmem=VMEM_BIG)(a, b, *after)


def _normmod_fn(x, sh, sc):
    return _rms(x) * (1.0 + sc) + sh


def _normmod_fwd(x, mod, i_sh, i_sc, *, name, br=256):
    R = x.shape[0]

    def body(x_ref, mod_ref, o_ref):
        o_ref[...] = _normmod_fn(x_ref[...], mod_ref[i_sh:i_sh + 1, :], mod_ref[i_sc:i_sc + 1, :]).astype(BF16)

    return _call(body, name=name, out_shape=_sds((R, D), BF16), grid=(R // br,),
                 in_specs=[pl.BlockSpec((br, D), lambda i: (i, 0)), pl.BlockSpec((6, D), lambda i: (0, 0))],
                 out_specs=pl.BlockSpec((br, D), lambda i: (i, 0)), sem=("parallel",))(x, mod)


def _normmod_bwd(x, mod, i_sh, i_sc, dh, dh_off, res, *, name, br=256):
    R = x.shape[0]
    ob = dh_off // br
    has_res = res is not None

    def body(x_ref, mod_ref, dh_ref, *rest):
        if has_res:
            res_ref, dx_ref, dsh_ref, dsc_ref = rest
        else:
            dx_ref, dsh_ref, dsc_ref = rest
        sh, sc = mod_ref[i_sh:i_sh + 1, :], mod_ref[i_sc:i_sc + 1, :]
        _, vjp = jax.vjp(_normmod_fn, x_ref[...], sh, sc)
        dx, dsh, dsc = vjp(dh_ref[...])
        dx_ref[...] = dx + res_ref[...] if has_res else dx

        @pl.when(pl.program_id(0) == 0)
        def _():
            dsh_ref[...] = jnp.zeros_like(dsh_ref)
            dsc_ref[...] = jnp.zeros_like(dsc_ref)

        dsh_ref[...] += dsh
        dsc_ref[...] += dsc

    row = pl.BlockSpec((br, D), lambda i: (i, 0))
    vec = pl.BlockSpec((1, D), lambda i: (0, 0))
    ins = [row, pl.BlockSpec((6, D), lambda i: (0, 0)), pl.BlockSpec((br, D), lambda i: (i + ob, 0))]
    args = [x, mod, dh]
    if has_res:
        ins.append(row)
        args.append(res)
    return _call(body, name=name, out_shape=(_sds((R, D)), _sds((1, D)), _sds((1, D))), grid=(R // br,),
                 in_specs=ins, out_specs=(row, vec, vec), sem=("arbitrary",))(*args)


def _rope(x, cos, sin):
    return x * cos + _swap32(x) * sin


def _aprep_fn(qs, ks, cos, sin, qw, kw):
    return ([_rope(_rms(q) * qw, cos, sin) for q in qs], [_rope(_rms(k) * kw, cos, sin) for k in ks])


def _aprep_fwd(proj, cos, sin, qw, kw, *, br=256):
    T = proj.shape[0]

    def body(x_ref, cos_ref, sin_ref, qw_ref, kw_ref, q_ref, k_ref, v_ref):
        qs = [x_ref[:, C_AQ + h * HD:C_AQ + (h + 1) * HD] for h in range(AH)]
        ks = [x_ref[:, h * HD:(h + 1) * HD] for h in range(AKV)]
        qo, ko = _aprep_fn(qs, ks, cos_ref[...], sin_ref[...], qw_ref[...], kw_ref[...])
        for h in range(AH):
            q_ref[h] = qo[h].astype(BF16)
        for h in range(AKV):
            k_ref[h] = ko[h].astype(BF16)
            v_ref[h] = x_ref[:, (AKV + h) * HD:(AKV + h + 1) * HD].astype(BF16)

    tab = pl.BlockSpec((br, HD), lambda i: (i, 0))
    vec = pl.BlockSpec((1, HD), lambda i: (0, 0))
    return _call(body, name="aprep_fwd",
                 out_shape=(_sds((AH, T, HD), BF16), _sds((AKV, T, HD), BF16), _sds((AKV, T, HD), BF16)),
                 grid=(T // br,),
                 in_specs=[pl.BlockSpec((br, C_QKV), lambda i: (i, 0)), tab, tab, vec, vec],
                 out_specs=(pl.BlockSpec((AH, br, HD), lambda i: (0, i, 0)),
                            pl.BlockSpec((AKV, br, HD), lambda i: (0, i, 0)),
                            pl.BlockSpec((AKV, br, HD), lambda i: (0, i, 0))),
                 sem=("parallel",))(proj, cos, sin, qw, kw)


def _aprep_bwd(proj, cos, sin, qw, kw, dq, dk, dv, dproj, L, *, br=256):
    T = proj.shape[0]
    lb = L // br

    def body(x_ref, cos_ref, sin_ref, qw_ref, kw_ref, dq_ref, dk_ref, dv_ref, _, dx_ref, dqw_ref, dkw_ref):
        i = pl.program_id(0)
        qs = [x_ref[:, C_AQ + h * HD:C_AQ + (h + 1) * HD] for h in range(AH)]
        ks = [x_ref[:, h * HD:(h + 1) * HD] for h in range(AKV)]
        _, vjp = jax.vjp(_aprep_fn, qs, ks, cos_ref[...], sin_ref[...], qw_ref[...], kw_ref[...])
        is_lat = i >= lb
        dqs = [jnp.where(is_lat, dq_ref[h], 0.0) for h in range(AH)]
        dks = [dk_ref[h] for h in range(AKV)]
        gq, gk, _, _, gqw, gkw = vjp((dqs, dks))
        for h in range(AH):
            dx_ref[:, C_AQ + h * HD:C_AQ + (h + 1) * HD] = gq[h].astype(BF16)
        for h in range(AKV):
            dx_ref[:, h * HD:(h + 1) * HD] = gk[h].astype(BF16)
            dx_ref[:, (AKV + h) * HD:(AKV + h + 1) * HD] = dv_ref[h].astype(BF16)

        @pl.when(i == 0)
        def _():
            dqw_ref[...] = jnp.zeros_like(dqw_ref)
            dkw_ref[...] = jnp.zeros_like(dkw_ref)

        dqw_ref[...] += gqw
        dkw_ref[...] += gkw

    tab = pl.BlockSpec((br, HD), lambda i: (i, 0))
    vec = pl.BlockSpec((1, HD), lambda i: (0, 0))
    kvb = pl.BlockSpec((AKV, br, HD), lambda i: (0, i, 0))
    blk = pl.BlockSpec((br, C_QKV), lambda i: (i, 0))
    return _call(body, name="aprep_bwd", out_shape=(_sds(dproj.shape, BF16), _sds((1, HD)), _sds((1, HD))),
                 grid=(T // br,),
                 in_specs=[blk, tab, tab, vec, vec,
                           pl.BlockSpec((AH, br, HD), lambda i: (0, jnp.maximum(i - lb, 0), 0)), kvb, kvb, ANYSPEC],
                 out_specs=(blk, vec, vec), aliases={8: 0},
                 sem=("arbitrary",))(proj, cos, sin, qw, kw, dq, dk, dv, dproj)


def _attn_grad(q, k, v, o, lse2, do):
    scale = HD ** -0.5
    p = jnp.exp2(_dot(q, k, 1, 1) * (scale * LOG2E) - lse2)
    dp = _dot(do, v, 1, 1)
    ds = p * (dp - jnp.sum(do * o, axis=-1, keepdims=True)) * scale
    return _dot(ds, k, 1, 0), _dot(ds, q, 0, 0), _dot(p, do, 0, 0)


ATTN_KEYS = 256


def _attn_fwd(q, k, v, L, exch, *, bq=128):
    T = q.shape[1]
    N = T - L
    lb = L // bq
    assert T % ATTN_KEYS == 0
    scale = HD ** -0.5
    heads = range(GRP)

    def body(q_ref, k_ref, v_ref, o_ref, o32_ref, lse_ref):
        qs = [q_ref[g] for g in heads]
        m = [jnp.full((bq, 1), -jnp.inf, F32) for _ in heads]
        l = [jnp.zeros((bq, 1), F32) for _ in heads]
        acc = [jnp.zeros((bq, HD), F32) for _ in heads]
        for c in range(T // ATTN_KEYS):
            kc, vc = k_ref[c * ATTN_KEYS:(c + 1) * ATTN_KEYS, :], v_ref[c * ATTN_KEYS:(c + 1) * ATTN_KEYS, :]
            s = [_dot(qs[g], kc, 1, 1) * (scale * LOG2E) for g in heads]
            m_new = [jnp.maximum(m[g], jnp.max(s[g], axis=-1, keepdims=True)) for g in heads]
            alpha = [jnp.exp2(m[g] - m_new[g]) for g in heads]
            p = [jnp.exp2(s[g] - m_new[g]) for g in heads]
            l = [l[g] * alpha[g] + jnp.sum(p[g], axis=-1, keepdims=True) for g in heads]
            acc = [acc[g] * alpha[g] + _dot(p[g], vc, 1, 0) for g in heads]
            m = m_new
        for g in heads:
            o = acc[g] / l[g]
            o_ref[:, g * HD:(g + 1) * HD] = o.astype(BF16)
            o32_ref[:, g * HD:(g + 1) * HD] = o
            lse_ref[g] = jnp.broadcast_to(m[g] + jnp.log2(l[g]), (bq, HD))

    kvb = pl.BlockSpec((None, T, HD), lambda g, i: (g, 0, 0))
    ob = pl.BlockSpec((bq, GRP * HD), lambda g, i: (i, g))
    return _call_carrying(
        body, exch, name="attn_fwd",
        out_shape=(_sds((N, AH * HD), BF16), _sds((N, AH * HD)), _sds((AH, N, HD))), grid=(AKV, N // bq),
        in_specs=[pl.BlockSpec((GRP, bq, HD), lambda g, i: (g, i + lb, 0)), kvb, kvb],
        out_specs=(ob, ob, pl.BlockSpec((GRP, bq, HD), lambda g, i: (g, i, 0))), vmem=VMEM_BIG)(q, k, v)


def _attn_bwd(q, k, v, o32, lse, do, L, exch, *, bq=128):
    T = q.shape[1]
    N = T - L
    lb = L // bq

    def body(q_ref, k_ref, v_ref, o_ref, lse_ref, do_ref, dq_ref, dk_ref, dv_ref):
        rows = lambda r: jnp.concatenate([r[:, g * HD:(g + 1) * HD] for g in range(GRP)], axis=0)
        lse = jnp.max(lse_ref[...].reshape(GRP * bq, HD), axis=-1, keepdims=True)
        dq, dk, dv = _attn_grad(q_ref[...].reshape(GRP * bq, HD), k_ref[...], v_ref[...], rows(o_ref), lse, rows(do_ref))
        dq_ref[...] = dq.reshape(GRP, bq, HD)

        @pl.when(pl.program_id(1) == 0)
        def _():
            dk_ref[...] = jnp.zeros_like(dk_ref)
            dv_ref[...] = jnp.zeros_like(dv_ref)

        dk_ref[...] += dk
        dv_ref[...] += dv

    kvb = pl.BlockSpec((None, T, HD), lambda g, i: (g, 0, 0))
    qb = pl.BlockSpec((GRP, bq, HD), lambda g, i: (g, i + lb, 0))
    hb = pl.BlockSpec((GRP, bq, HD), lambda g, i: (g, i, 0))
    ob = pl.BlockSpec((bq, GRP * HD), lambda g, i: (i, g))
    return _call_carrying(body, exch, name="attn_bwd",
                          out_shape=(_sds((AH, N, HD)), _sds((AKV, T, HD)), _sds((AKV, T, HD))), grid=(AKV, N // bq),
                          in_specs=[qb, kvb, kvb, ob, hb, ob], out_specs=(hb, kvb, kvb),
                          vmem=VMEM_BIG)(q, k, v, o32, lse, do)


def _gprep_fn(kind, shifts, x, w):
    down, up = shifts
    y = down(x) * w[0:1, :] + x * w[1:2, :] + up(x) * w[2:3, :]
    a = _silu(y)
    if kind == 2:
        return a
    a = a * lax.rsqrt(jnp.sum(a * a, axis=-1, keepdims=True) + EPS)
    return a * (HD ** -0.5) if kind == 0 else a


def _gprep_fwd(proj, conv_w, kind, bounds):
    T = proj.shape[0]
    shifts = _make_shift(bounds)
    cb = C_QKV // HD + kind * GH

    def body(x_ref, w_ref, o_ref):
        o_ref[...] = _gprep_fn(kind, shifts, x_ref[...], w_ref[...])

    return _call(body, name=f"gprep_fwd{kind}", out_shape=_sds((GH, T, HD)), grid=(GH,),
                 in_specs=[pl.BlockSpec((T, HD), lambda h: (0, cb + h)),
                           pl.BlockSpec((3, HD), lambda h: (0, kind * GH + h))],
                 out_specs=pl.BlockSpec((None, T, HD), lambda h: (h, 0, 0)), sem=("parallel",))(proj, conv_w)


def _gprep_bwd(proj, conv_w, kind, bounds, dy, dproj):
    T = proj.shape[0]
    shifts = _make_shift(bounds)
    cb = C_QKV // HD + kind * GH

    def body(x_ref, w_ref, dy_ref, _, dx_ref, dw_ref):
        _, vjp = jax.vjp(functools.partial(_gprep_fn, kind, shifts), x_ref[...], w_ref[...])
        dx, dw = vjp(dy_ref[0] + dy_ref[1])
        dx_ref[...] = dx.astype(BF16)
        dw_ref[...] = dw

    return _call(body, name=f"gprep_bwd{kind}", out_shape=(_sds(dproj.shape, BF16), _sds((3, GH * HD))), grid=(GH,),
                 in_specs=[pl.BlockSpec((T, HD), lambda h: (0, cb + h)),
                           pl.BlockSpec((3, HD), lambda h: (0, kind * GH + h)),
                           pl.BlockSpec((2, None, T, HD), lambda h: (0, h, 0, 0)), ANYSPEC],
                 out_specs=(pl.BlockSpec((T, HD), lambda h: (0, cb + h)), pl.BlockSpec((3, HD), lambda h: (0, h))),
                 aliases={3: 0}, sem=("parallel",))(proj, conv_w, dy, dproj)


def _bl_fn(x, alog, dtb):
    lane = lax.broadcasted_iota(jnp.int32, x.shape, 1)
    beta = jax.nn.sigmoid(x)
    z = x + dtb
    sp = jnp.maximum(z, 0.0) + jnp.log1p(jnp.exp(-jnp.abs(z)))
    la = -jnp.exp(alog) * sp
    return jnp.where(lane < 2 * GH, beta, jnp.where(lane < 4 * GH, la, 0.0))


def _bl_fwd(proj, alog, dtb, *, br=256):
    T = proj.shape[0]

    def body(x_ref, a_ref, d_ref, o_ref):
        o_ref[...] = _bl_fn(x_ref[...], a_ref[...], d_ref[...])

    vec = pl.BlockSpec((1, HD), lambda i: (0, 0))
    return _call(body, name="bl_fwd", out_shape=_sds((T, HD)), grid=(T // br,),
                 in_specs=[pl.BlockSpec((br, HD), lambda i: (i, C_BL // HD)), vec, vec],
                 out_specs=pl.BlockSpec((br, HD), lambda i: (i, 0)), sem=("parallel",))(proj, alog, dtb)


def _bl_bwd(proj, alog, dtb, dbl, dproj, *, br=256):
    T = proj.shape[0]
    wide = C_Z - C_BL

    def body(x_ref, a_ref, d_ref, g_ref, _, dx_ref, da_ref, dd_ref):
        g = g_ref[0, 0]
        for d in range(2):
            for h in range(GH):
                if d or h:
                    g = g + g_ref[d, h]
        _, vjp = jax.vjp(_bl_fn, x_ref[...], a_ref[...], d_ref[...])
        dx, da, dd = vjp(g)
        dx_ref[:, :HD] = dx.astype(BF16)
        dx_ref[:, HD:] = jnp.zeros((br, wide - HD), BF16)

        @pl.when(pl.program_id(0) == 0)
        def _():
            da_ref[...] = jnp.zeros_like(da_ref)
            dd_ref[...] = jnp.zeros_like(dd_ref)

        da_ref[...] += da
        dd_ref[...] += dd

    vec = pl.BlockSpec((1, HD), lambda i: (0, 0))
    return _call(body, name="bl_bwd", out_shape=(_sds(dproj.shape, BF16), _sds((1, HD)), _sds((1, HD))), grid=(T // br,),
                 in_specs=[pl.BlockSpec((br, HD), lambda i: (i, C_BL // HD)), vec, vec,
                           pl.BlockSpec((2, GH, br, HD), lambda i: (0, 0, i, 0)), ANYSPEC],
                 out_specs=(pl.BlockSpec((br, wide), lambda i: (i, C_BL // wide)), vec, vec), aliases={4: 0},
                 sem=("arbitrary",))(proj, alog, dtb, dbl, dproj)


def _chunk_masks(d):
    ii = lax.broadcasted_iota(jnp.int32, (CH, CH), 0)
    jj = lax.broadcasted_iota(jnp.int32, (CH, CH), 1)
    eye = (ii == jj).astype(F32)
    before = jnp.where(d == 0, (jj < ii).astype(F32), (jj > ii).astype(F32))
    return before, before + eye, eye


def _same_block(b):
    ii = lax.broadcasted_iota(jnp.int32, (CH, CH), 0)
    jj = lax.broadcasted_iota(jnp.int32, (CH, CH), 1)
    shift = b.bit_length() - 1
    return (jnp.right_shift(ii, shift) == jnp.right_shift(jj, shift)).astype(F32)


def _intra_fn(masks, sel_b, sel_l, qs, ks, vs, bls, xs=None):
    before, ateq, eye = masks
    inc = ateq > 0.0
    each = lambda f, *ls: [f(*t) for t in zip(*ls)]
    beta = each(lambda bl: jnp.sum(bl * sel_b, axis=-1, keepdims=True), bls)
    la = each(lambda bl: jnp.sum(bl * sel_l, axis=-1, keepdims=True), bls)
    gam = each(lambda a: _mask_nn(ateq, jnp.broadcast_to(a, (CH, HD))), la)
    gi = each(lambda g: g[:, :CH], gam)
    gj = each(lambda g: jnp.transpose(g)[:CH, :], gam)
    kq = each(lambda k, q: _nt(jnp.concatenate([k, q], axis=0), k), ks, qs)
    kk = each(lambda t: t[:CH], kq)
    qk = each(lambda t: t[CH:], kq)
    dec = each(lambda a, b: jnp.where(inc, jnp.exp(jnp.where(inc, a - b, 0.0)), 0.0), gi, gj)
    lmat = each(lambda b, d, m: before * (b * d * m), beta, dec, kk)
    if xs is None:
        same = lambda b: _same_block(b)
        l8 = each(lambda m: m * same(8), lmat)
        x = each(lambda m: eye - m, l8)
        p2 = each(lambda m: _mdot(m, m), l8)
        y = each(lambda a, b: _mdot(jnp.concatenate([a, b], axis=0), b), x, p2)
        x = each(lambda a, t: a + t[:CH], x, y)
        x = each(lambda a, t: a + _mdot(a, t[CH:]), x, y)
        for b in (8, 16, 32):
            below = same(2 * b) - same(b)
            x = each(lambda a, m: a - _mdot(a, _mdot(m * below, a)), x, lmat)
    else:
        x = each(_saved_inverse, lmat, xs)
    eg = each(jnp.exp, gam)
    uw = each(lambda a, b, v, e, k: _mdot(a, jnp.concatenate([b * v, (b * e) * k], axis=1)), x, beta, vs, eg, ks)
    u = each(lambda t: t[:, :HD], uw)
    w = each(lambda t: t[:, HD:], uw)
    tot = each(lambda a: jnp.sum(a, axis=0, keepdims=True), la)
    kd = each(lambda k, t, g: k * jnp.exp(t - g), ks, tot, gam)
    gl = each(lambda t: jnp.broadcast_to(jnp.exp(t), (1, HD)), tot)
    qd = each(lambda q, e: q * e, qs, eg)
    p = each(lambda d, m: d * m, dec, qk)
    return (u, w, kd, qd, p, gl, x) if xs is None else (u, w, kd, qd, p, gl)


def _dir_head_sel(d, h):
    lane = lax.broadcasted_iota(jnp.int32, (1, HD), 1)
    return (lane == d * GH + h).astype(F32), (lane == 2 * GH + d * GH + h).astype(F32)


def _intra_specs(T, G):
    nc = T // CH
    assert nc % G == 0
    qkv = pl.BlockSpec((None, G * CH, HD), lambda d, h, c: (h, c, 0))
    bl = pl.BlockSpec((G * CH, HD), lambda d, h, c: (c, 0))
    big = pl.BlockSpec((None, None, G * CH, HD), lambda d, h, c: (d, h, c, 0))
    pm = pl.BlockSpec((None, None, G * CH, CH), lambda d, h, c: (d, h, c, 0))
    gl = pl.BlockSpec((None, None, G, 1, HD), lambda d, h, c: (d, h, c, 0, 0))
    shapes = (_sds((2, GH, T, HD)),) + (_sds((2, GH, T, HD), BF16),) * 3 + (
        _sds((2, GH, T, CH), BF16), _sds((2, GH, nc, 1, HD)), _sds((2, GH, T, CH)))
    return nc, qkv, bl, big, pm, gl, shapes


def _chunks_per_step(T, most):
    nc = T // CH
    return max(g for g in range(1, most + 1) if nc % g == 0)


def _chunk_at(g, d, nc, ncc):
    pos = _visit_pos(g, d, nc, ncc)
    return pos, pl.ds(pl.multiple_of(pos * CH, CH), CH)


def _intra_fwd(q, k, v, bl, L, exch):
    T = q.shape[1]
    G = _chunks_per_step(T, INTRA_FWD_CHUNKS)
    nc, qkv_s, bl_s, big, pm, gl_s, shapes = _intra_specs(T, G)
    assert G == nc
    ncc = L // CH

    def body(q_ref, k_ref, v_ref, bl_ref, u_ref, w_ref, kd_ref, qd_ref, p_ref, gl_ref, x_ref):
        d, h = pl.program_id(0), pl.program_id(1)
        sb, sl = _dir_head_sel(d, h)
        rows = [slice(g * CH, (g + 1) * CH) for g in range(G)]
        outs = _intra_fn(_chunk_masks(d), sb, sl, *[[r[s, :] for s in rows] for r in (q_ref, k_ref, v_ref, bl_ref)])
        for g in range(G):
            pos, at = _chunk_at(g, d, nc, ncc)
            for r, o in zip((u_ref, w_ref, kd_ref, qd_ref, p_ref, x_ref), outs[:5] + outs[6:]):
                r[at, :] = o[g].astype(r.dtype)
            gl_ref[pos] = outs[5][g]

    return _call_carrying(body, exch, name="gdn_intra_fwd", out_shape=shapes, grid=(2, GH, nc // G),
                          in_specs=[qkv_s, qkv_s, qkv_s, bl_s], out_specs=(big, big, big, big, pm, gl_s, pm))(q, k, v, bl)


def _intra_bwd(q, k, v, bl, xinv, cts, L, exch):
    T = q.shape[1]
    G = _chunks_per_step(T, INTRA_BWD_CHUNKS)
    nc, qkv_s, bl_s, big, pm, gl_s, _ = _intra_specs(T, G)
    assert G == nc
    ncc = L // CH

    def body(q_ref, k_ref, v_ref, bl_ref, x_ref, du, dw, dkd, dqd, dp, dgl, dq_ref, dk_ref, dv_ref, dbl_ref):
        d, h = pl.program_id(0), pl.program_id(1)
        sb, sl = _dir_head_sel(d, h)
        rows = [slice(g * CH, (g + 1) * CH) for g in range(G)]
        places = [_chunk_at(g, d, nc, ncc) for g in range(G)]
        fn = functools.partial(_intra_fn, _chunk_masks(d), sb, sl, xs=[x_ref[at, :] for _, at in places])
        _, vjp = jax.vjp(fn, *[[r[s, :] for s in rows] for r in (q_ref, k_ref, v_ref, bl_ref)])
        cts = tuple([r[at, :] for _, at in places] for r in (du, dw, dkd, dqd, dp)) + ([dgl[pos] for pos, _ in places],)
        grads = vjp(cts)
        for g in range(G):
            for r, o in zip((dq_ref, dk_ref, dv_ref, dbl_ref), grads):
                r[rows[g], :] = o[g]

    return _call_carrying(body, exch, name="gdn_intra_bwd", out_shape=(_sds((2, GH, T, HD)),) * 4,
                          grid=(2, GH, nc // G), in_specs=[qkv_s, qkv_s, qkv_s, bl_s, pm, big, big, big, big, pm, gl_s],
                          out_specs=(big,) * 4)(q, k, v, bl, xinv, *cts)


def _scan_fn(s, u, w, kd, qd, p, gl):
    each = lambda f, *ls: [f(*t) for t in zip(*ls)]
    ws = each(_nn, w, s)
    delta = each(lambda a, b: a - b, u, ws)
    kdd = each(_tn, kd, delta)
    s_new = each(lambda g, a, b: g * a + b, gl, s, kdd)
    qs = each(_nn, qd, s)
    pd = each(_nn, p, delta)
    return each(lambda a, b: a + b, qs, pd), s_new


SCAN_BLOCK = 4


def _visit_pos(c, d, nc, ncc):
    back = ncc - 1 - c if c < ncc else ncc + (nc - 1 - c)
    return jnp.where(d == 0, c, back)


def _scan_specs(T, L, back):
    tb = SCAN_BLOCK * CH
    assert T % tb == 0 and L % tb == 0
    nb, ncb = T // tb, L // tb
    at = (lambda t: nb - 1 - t) if back else (lambda t: t)
    big = pl.BlockSpec((2, GH, tb, HD), lambda t: (0, 0, at(t), 0))
    pm = pl.BlockSpec((2, GH, tb, CH), lambda t: (0, 0, at(t), 0))
    gl = pl.BlockSpec((2, GH, SCAN_BLOCK, 1, HD), lambda t: (0, 0, at(t), 0, 0))
    st = pl.BlockSpec((2, GH, SCAN_BLOCK, HD, HD), lambda t: (0, 0, at(t), 0, 0))

    def natural(b):
        return jnp.where(b < ncb, ncb - 1 - b, nb - 1 - (b - ncb))

    do_specs = (pl.BlockSpec((GH, tb, HD), lambda t: (0, at(t), 0)),
                pl.BlockSpec((GH, tb, HD), lambda t: (0, natural(at(t)), 0)))
    return nb, big, pm, gl, st, do_specs


SCAN_STREAMS = [(d, h) for d in (0, 1) for h in range(GH)]


def _scan_fwd(u, w, kd, qd, p, gl, L):
    T = u.shape[2]
    nb, big, pm, gl_s, st, _ = _scan_specs(T, L, False)

    def body(u_ref, w_ref, kd_ref, qd_ref, p_ref, gl_ref, o_ref, st_ref, s_scr):
        @pl.when(pl.program_id(0) == 0)
        def _():
            s_scr[...] = jnp.zeros_like(s_scr)

        s = [s_scr[d, h] for d, h in SCAN_STREAMS]
        for i in range(SCAN_BLOCK):
            rows = slice(i * CH, (i + 1) * CH)
            for (d, h), sv in zip(SCAN_STREAMS, s):
                st_ref[d, h, i] = sv
            o, s = _scan_fn(s, *[[r[d, h, rows, :].astype(F32) for d, h in SCAN_STREAMS]
                                 for r in (u_ref, w_ref, kd_ref, qd_ref, p_ref)],
                            [gl_ref[d, h, i] for d, h in SCAN_STREAMS])
            for (d, h), ov in zip(SCAN_STREAMS, o):
                o_ref[d, h, rows, :] = ov
        for (d, h), sv in zip(SCAN_STREAMS, s):
            s_scr[d, h] = sv

    return _call(body, name="gdn_scan_fwd", out_shape=(_sds((2, GH, T, HD)), _sds((2, GH, T // CH, HD, HD))),
                 grid=(nb,), in_specs=[big, big, big, big, pm, gl_s], out_specs=(big, st),
                 scratch=[pltpu.VMEM((2, GH, HD, HD), F32)], sem=("arbitrary",), vmem=VMEM_BIG)(u, w, kd, qd, p, gl)


def _scan_bwd(u, w, kd, qd, p, gl, states, do, L, exch):
    T = u.shape[2]
    nb, big, pm, gl_s, st, do_specs = _scan_specs(T, L, True)

    def body(u_ref, w_ref, kd_ref, qd_ref, p_ref, gl_ref, st_ref, do0_ref, do1_ref,
             du_ref, dw_ref, dkd_ref, dqd_ref, dp_ref, dgl_ref, ds_scr):
        @pl.when(pl.program_id(0) == 0)
        def _():
            ds_scr[...] = jnp.zeros_like(ds_scr)

        ds = [ds_scr[d, h] for d, h in SCAN_STREAMS]
        for i in reversed(range(SCAN_BLOCK)):
            rows = slice(i * CH, (i + 1) * CH)
            mirror = slice((SCAN_BLOCK - 1 - i) * CH, (SCAN_BLOCK - i) * CH)
            _, vjp = jax.vjp(_scan_fn, [st_ref[d, h, i] for d, h in SCAN_STREAMS],
                             *[[r[d, h, rows, :].astype(F32) for d, h in SCAN_STREAMS]
                               for r in (u_ref, w_ref, kd_ref, qd_ref, p_ref)],
                             [gl_ref[d, h, i] for d, h in SCAN_STREAMS])
            dos = [do0_ref[h, rows, :] if d == 0 else do1_ref[h, mirror, :] for d, h in SCAN_STREAMS]
            ds, gu, gw, gkd, gqd, gp, ggl = vjp((dos, ds))
            for n, (d, h) in enumerate(SCAN_STREAMS):
                du_ref[d, h, rows, :] = gu[n]
                dw_ref[d, h, rows, :] = gw[n]
                dkd_ref[d, h, rows, :] = gkd[n]
                dqd_ref[d, h, rows, :] = gqd[n]
                dp_ref[d, h, rows, :] = gp[n]
                dgl_ref[d, h, i] = ggl[n]
        for (d, h), dv in zip(SCAN_STREAMS, ds):
            ds_scr[d, h] = dv

    return _call_carrying(
        body, exch, name="gdn_scan_bwd",
        out_shape=(_sds((2, GH, T, HD)),) * 4 + (_sds((2, GH, T, CH)), _sds((2, GH, T // CH, 1, HD))),
        grid=(nb,), in_specs=[big, big, big, big, pm, gl_s, st, *do_specs], out_specs=(big, big, big, big, pm, gl_s),
        scratch=[pltpu.VMEM((2, GH, HD, HD), F32)], vmem=VMEM_BIG)(u, w, kd, qd, p, gl, states, do, do)


def _gout_fn(o0, o1, z, gw):
    return _rms(o0 + o1) * gw * _silu(z)


def _backward_latent(o_ref, L):
    nl = (o_ref.shape[1] - L) // CH
    return jnp.concatenate([o_ref[1, L + (nl - 1 - j) * CH:L + (nl - j) * CH, :] for j in range(nl)], axis=0)


def _gout_fwd(o, proj, gw, L):
    T = o.shape[2]
    N = T - L
    ob = pl.BlockSpec((2, None, T, HD), lambda h: (0, h, 0, 0))

    def body(o_ref, z_ref, gw_ref, y_ref):
        y_ref[...] = _gout_fn(o_ref[0, L:, :], _backward_latent(o_ref, L), z_ref[L:, :], gw_ref[...]).astype(BF16)

    return _call(body, name="gout_fwd", out_shape=_sds((N, GH * HD), BF16), grid=(GH,),
                 in_specs=[ob, pl.BlockSpec((T, HD), lambda h: (0, C_Z // HD + h)), pl.BlockSpec((1, HD), lambda h: (0, 0))],
                 out_specs=pl.BlockSpec((N, HD), lambda h: (0, h)), sem=("parallel",))(o, proj, gw)


def _gout_bwd(o, proj, gw, dy, dproj, L):
    T = o.shape[2]
    N = T - L
    ob = pl.BlockSpec((2, None, T, HD), lambda h: (0, h, 0, 0))

    def body(o_ref, z_ref, gw_ref, dy_ref, _, do_ref, dz_ref, dgw_ref):
        _, vjp = jax.vjp(_gout_fn, o_ref[0, L:, :], _backward_latent(o_ref, L), z_ref[L:, :], gw_ref[...])
        g0, _, gz, ggw = vjp(dy_ref[...])
        do_ref[:L, :] = jnp.zeros((L, HD), F32)
        do_ref[L:, :] = g0
        dz_ref[:L, :] = jnp.zeros((L, HD), BF16)
        dz_ref[L:, :] = gz.astype(BF16)

        @pl.when(pl.program_id(0) == 0)
        def _():
            dgw_ref[...] = jnp.zeros_like(dgw_ref)

        dgw_ref[...] += ggw

    zb = pl.BlockSpec((T, HD), lambda h: (0, C_Z // HD + h))
    return _call(body, name="gout_bwd", out_shape=(_sds((GH, T, HD)), _sds(dproj.shape, BF16), _sds((1, HD))),
                 grid=(GH,),
                 in_specs=[ob, zb, pl.BlockSpec((1, HD), lambda h: (0, 0)), pl.BlockSpec((N, HD), lambda h: (0, h)), ANYSPEC],
                 out_specs=(pl.BlockSpec((None, T, HD), lambda h: (h, 0, 0)), zb, pl.BlockSpec((1, HD), lambda h: (0, 0))),
                 aliases={4: 1}, sem=("arbitrary",))(o, proj, gw, dy, dproj)


def _merge_fn(pa, pd, ga, gd):
    return jax.nn.sigmoid(ga) * pa + jax.nn.sigmoid(gd) * pd


def _merge_fwd(pa, pd, proj, L, *, br=256):
    N = pa.shape[0]
    lb = L // br
    row = pl.BlockSpec((br, D), lambda i: (i, 0))

    def body(pa_ref, pd_ref, ga_ref, gd_ref, y_ref):
        y_ref[...] = _merge_fn(pa_ref[...], pd_ref[...], ga_ref[...], gd_ref[...]).astype(BF16)

    return _call(body, name="merge_fwd", out_shape=_sds((N, D), BF16), grid=(N // br,),
                 in_specs=[row, row, pl.BlockSpec((br, D), lambda i: (i + lb, C_GATE // D)),
                           pl.BlockSpec((br, D), lambda i: (i + lb, C_GATE // D + 1))],
                 out_specs=row, sem=("parallel",))(pa, pd, proj, proj)


def _merge_bwd(pa, pd, proj, dy, L, *, br=256):
    N = pa.shape[0]
    T = N + L
    lb = L // br
    lrow = pl.BlockSpec((br, D), lambda i: (jnp.maximum(i - lb, 0), 0))

    def body(pa_ref, pd_ref, ga_ref, gd_ref, dy_ref, dpa_ref, dpd_ref, dg_ref):
        lat = pl.program_id(0) >= lb
        _, vjp = jax.vjp(_merge_fn, pa_ref[...], pd_ref[...], ga_ref[...], gd_ref[...])
        gpa, gpd, gga, ggd = vjp(dy_ref[...])
        dpa_ref[...] = gpa.astype(BF16)
        dpd_ref[...] = gpd.astype(BF16)
        dg_ref[:, :D] = jnp.where(lat, gga, 0.0).astype(BF16)
        dg_ref[:, D:] = jnp.where(lat, ggd, 0.0).astype(BF16)

    return _call(body, name="merge_bwd", out_shape=(_sds((N, D), BF16), _sds((N, D), BF16), _sds((T, C_END), BF16)),
                 grid=(T // br,),
                 in_specs=[lrow, lrow, pl.BlockSpec((br, D), lambda i: (i, C_GATE // D)),
                           pl.BlockSpec((br, D), lambda i: (i, C_GATE // D + 1)), lrow],
                 out_specs=(lrow, lrow, pl.BlockSpec((br, 2 * D), lambda i: (i, C_GATE // (2 * D)))),
                 sem=("arbitrary",))(pa, pd, proj, proj, dy)


def _resid_fwd(x, m, mod, i_g, *, name, br=256):
    R = x.shape[0]
    row = pl.BlockSpec((br, D), lambda i: (i, 0))

    def body(x_ref, m_ref, mod_ref, o_ref):
        o_ref[...] = x_ref[...] + mod_ref[i_g:i_g + 1, :] * m_ref[...]

    return _call(body, name=name, out_shape=_sds((R, D)), grid=(R // br,),
                 in_specs=[row, row, pl.BlockSpec((6, D), lambda i: (0, 0))], out_specs=row,
                 sem=("parallel",))(x, m, mod)


def _resid_bwd(dx, m, mod, i_g, *, name, br=256):
    R = dx.shape[0]
    row = pl.BlockSpec((br, D), lambda i: (i, 0))
    vec = pl.BlockSpec((1, D), lambda i: (0, 0))

    def body(dx_ref, m_ref, mod_ref, dm_ref, dg_ref):
        dxv = dx_ref[...]
        dm_ref[...] = (dxv * mod_ref[i_g:i_g + 1, :]).astype(BF16)

        @pl.when(pl.program_id(0) == 0)
        def _():
            dg_ref[...] = jnp.zeros_like(dg_ref)

        dg_ref[...] += jnp.sum(dxv * m_ref[...], axis=0, keepdims=True)

    return _call(body, name=name, out_shape=(_sds((R, D), BF16), _sds((1, D))), grid=(R // br,),
                 in_specs=[row, row, pl.BlockSpec((6, D), lambda i: (0, 0))], out_specs=(row, vec),
                 sem=("arbitrary",))(dx, m, mod)


def _ffn_fn(shifts, ug, uv, wg, wv, bg, bv):
    down, up = shifts

    def conv(x, w, b):
        return down(x) * w[0:1, :] + x * w[1:2, :] + up(x) * w[2:3, :] + b

    return _silu(conv(ug, wg, bg)) * conv(uv, wv, bv)


def _ffn_fwd(up, cw, cb, *, bw=256):
    N = up.shape[0]
    shifts = _make_shift(((0, N),))
    nb = DFF // bw

    def body(ug, uv, wg, wv, bg, bv, a_ref):
        a_ref[...] = _ffn_fn(shifts, ug[...], uv[...], wg[...], wv[...], bg[...], bv[...]).astype(BF16)

    def col(rows, off):
        return pl.BlockSpec((rows, bw), lambda j: (0, j + off))

    return _call(body, name="ffn_fwd", out_shape=_sds((N, DFF), BF16), grid=(nb,),
                 in_specs=[col(N, 0), col(N, nb), col(3, 0), col(3, nb), col(1, 0), col(1, nb)],
                 out_specs=col(N, 0), sem=("parallel",), vmem=VMEM_BIG)(up, up, cw, cw, cb, cb)


def _ffn_bwd(up, cw, cb, da, *, bw=256):
    N = up.shape[0]
    shifts = _make_shift(((0, N),))
    nb = DFF // bw

    def body(ug, uv, wg, wv, bg, bv, da_ref, dug, duv, dwg, dwv, dbg, dbv):
        _, vjp = jax.vjp(functools.partial(_ffn_fn, shifts), ug[...], uv[...], wg[...], wv[...], bg[...], bv[...])
        g = vjp(da_ref[...])
        dug[...] = g[0].astype(BF16)
        duv[...] = g[1].astype(BF16)
        dwg[...], dwv[...], dbg[...], dbv[...] = g[2], g[3], g[4], g[5]

    def col(rows, off):
        return pl.BlockSpec((rows, bw), lambda j: (0, j + off))

    half = (_sds((N, DFF), BF16), _sds((N, DFF), BF16), _sds((3, DFF)), _sds((3, DFF)), _sds((1, DFF)), _sds((1, DFF)))
    dug, duv, dwg, dwv, dbg, dbv = _call(
        body, name="ffn_bwd", out_shape=half, grid=(nb,),
        in_specs=[col(N, 0), col(N, nb), col(3, 0), col(3, nb), col(1, 0), col(1, nb), col(N, 0)],
        out_specs=(col(N, 0), col(N, 0), col(3, 0), col(3, 0), col(1, 0), col(1, 0)),
        sem=("parallel",), vmem=VMEM_BIG)(up, up, cw, cw, cb, cb, da)
    return (jnp.concatenate([dug, duv], axis=1), jnp.concatenate([dwg, dwv], axis=1),
            jnp.concatenate([dbg, dbv], axis=1))


def _head_fn(x1, dn, g2, fw, tgt):
    y = _rms(x1 + g2 * dn) * fw
    err = y - tgt
    return 0.5 * jnp.sum(jnp.mean(err * err, axis=-1))


def _head(x1, dn, mod, fw, tgt, *, br=256):
    N = x1.shape[0]
    row = pl.BlockSpec((br, D), lambda i: (i, 0))
    vec = pl.BlockSpec((1, D), lambda i: (0, 0))
    one = pl.BlockSpec((1, HD), lambda i: (0, 0))

    def body(x1_ref, dn_ref, mod_ref, fw_ref, tgt_ref, loss_ref, dx_ref, ddn_ref, dg_ref, dfw_ref):
        loss, (gx, gdn, gg, gfw) = jax.value_and_grad(_head_fn, argnums=(0, 1, 2, 3))(
            x1_ref[...], dn_ref[...], mod_ref[5:6, :], fw_ref[...], tgt_ref[...])
        dx_ref[...] = gx
        ddn_ref[...] = gdn.astype(BF16)

        @pl.when(pl.program_id(0) == 0)
        def _():
            loss_ref[...] = jnp.zeros_like(loss_ref)
            dg_ref[...] = jnp.zeros_like(dg_ref)
            dfw_ref[...] = jnp.zeros_like(dfw_ref)

        loss_ref[...] += jnp.broadcast_to(loss, (1, HD))
        dg_ref[...] += gg
        dfw_ref[...] += gfw

    return _call(body, name="head", out_shape=(_sds((1, HD)), _sds((N, D)), _sds((N, D), BF16), _sds((1, D)), _sds((1, D))),
                 grid=(N // br,), in_specs=[row, row, pl.BlockSpec((6, D), lambda i: (0, 0)), vec, row],
                 out_specs=(one, row, row, vec, vec), sem=("arbitrary",))(x1, dn, mod, fw, tgt)


def _adamw(w, g, m, v, *, name):
    shape = w.shape
    cols = shape[-1]
    rows = max(1, math.prod(shape[:-1]))
    w2, g2, m2, v2 = (t.reshape(rows, cols) for t in (w, g, m, v))
    br = 256 if rows % 256 == 0 else rows
    c1 = 1.0 - B1 ** STEP
    c2 = 1.0 - B2 ** STEP

    def body(w_ref, g_ref, m_ref, v_ref, d_ref, nm_ref, nv_ref):
        gv = g_ref[...]
        nm = B1 * m_ref[...] + (1.0 - B1) * gv
        nv = B2 * v_ref[...] + (1.0 - B2) * (gv * gv)
        d_ref[...] = -LR * ((nm / c1) / (jnp.sqrt(nv / c2) + AEPS) + WD * w_ref[...])
        nm_ref[...] = nm
        nv_ref[...] = nv

    blk = pl.BlockSpec((br, cols), lambda i: (i, 0))
    outs = _call(body, name=name, out_shape=(_sds((rows, cols)),) * 3, grid=(rows // br,),
                 in_specs=[blk] * 4, out_specs=(blk,) * 3, sem=("parallel",))(w2, g2, m2, v2)
    return tuple(t.reshape(shape) for t in outs)


def _adamw_many(items, *, name):
    k = len(items)
    shapes = [w.shape for w, _, _, _ in items]
    flat = [t.reshape(max(1, math.prod(t.shape[:-1])), t.shape[-1]) for it in items for t in it]
    c1 = 1.0 - B1 ** STEP
    c2 = 1.0 - B2 ** STEP

    def body(*refs):
        ins, outs = refs[:4 * k], refs[4 * k:]
        for i in range(k):
            w_ref, g_ref, m_ref, v_ref = ins[4 * i:4 * i + 4]
            gv = g_ref[...]
            nm = B1 * m_ref[...] + (1.0 - B1) * gv
            nv = B2 * v_ref[...] + (1.0 - B2) * (gv * gv)
            outs[3 * i][...] = -LR * ((nm / c1) / (jnp.sqrt(nv / c2) + AEPS) + WD * w_ref[...])
            outs[3 * i + 1][...] = nm
            outs[3 * i + 2][...] = nv

    res = _call(body, name=name, out_shape=tuple(_sds(flat[4 * i].shape) for i in range(k) for _ in range(3)))(*flat)
    return [tuple(res[3 * i + j].reshape(shapes[i]) for j in range(3)) for i in range(k)]


def _rope_tables(N, L):
    t = jnp.arange(N)
    pos = jnp.stack([(t // GRID_W).astype(F32), (t % GRID_W).astype(F32)], axis=1)
    inv = ROPE_THETA ** (-jnp.arange(0, HD // 2, 2, dtype=F32) / (HD // 2))
    ang = pos[:, :, None] * inv[None, None, :]
    cos = jnp.broadcast_to(jnp.cos(ang)[:, :, None, :], (N, 2, 2, HD // 4)).reshape(N, HD)
    sin = jnp.broadcast_to(jnp.sin(ang)[:, :, None, :], (N, 2, 2, HD // 4))
    sin = (sin * jnp.array([-1.0, 1.0], F32)[None, None, :, None]).reshape(N, HD)
    cos = jnp.concatenate([jnp.ones((L, HD), F32), cos], axis=0)
    sin = jnp.concatenate([jnp.zeros((L, HD), F32), sin], axis=0)
    return cos, sin


def _pad_lanes(v, off=0):
    return jnp.zeros((1, HD), F32).at[0, off:off + v.shape[0]].set(v)


def _local_step(x, ctx, tgt, mod_lat, mod_ctx, w_in, shards, small):
    N, L = x.shape[0], ctx.shape[0]
    T = N + L
    bounds = ((0, L), (L, T))
    qw, kw, gw = small["q_norm_w"], small["k_norm_w"], small["gdn_norm_w"]
    conv_w, ffn_w, ffn_b, fnw = small["conv_qkv_w"], small["ffn_conv_w"], small["ffn_conv_b"], small["final_norm_w"]
    alog = _pad_lanes(small["a_log"].reshape(-1), 2 * GH)
    dtb = _pad_lanes(small["dt_bias"].reshape(-1), 2 * GH)
    cos, sin = _rope_tables(N, L)
    bt = T
    bnl = 256 if N % 1024 else 1024

    hc = _normmod_fwd(ctx, mod_ctx, 0, 1, name="normmod_ctx")
    hx = _normmod_fwd(x, mod_lat, 0, 1, name="normmod_x")
    h1 = jnp.concatenate([hc, hx], axis=0)
    proj = _mm(h1, w_in, name="mm_in", M=T, N=C_END, K=D, tb=True, bm=bt, bn=1024)
    aq, ak, av = _aprep_fwd(proj, cos, sin, qw, kw)
    (attn, attn32, lse), (up_g,) = _attn_fwd(aq, ak, av, L, _GatherTwoLevel([shards["w_up"]]))
    gq = _gprep_fwd(proj, conv_w, 0, bounds)
    gk = _gprep_fwd(proj, conv_w, 1, bounds)
    gv = _gprep_fwd(proj, conv_w, 2, bounds)
    bl = _bl_fwd(proj, alog, dtb)
    intra, (down_g, pa_g, pd_g, out_g) = _intra_fwd(
        gq, gk, gv, bl, L, _GatherTwoLevel([shards[n] for n in ("w_down", "w_pa", "w_pd", "w_out")]))
    w_up, w_down = up_g.reshape(2 * DFF, D), down_g.reshape(DFF, D)
    w_pa, w_pd, w_out = pa_g.reshape(D, D), pd_g.reshape(D, D), out_g.reshape(D, D)
    xinv, intra = intra[6], intra[:6]
    o, states = _scan_fwd(*intra, L)
    gdn = _gout_fwd(o, proj, gw, L)
    pa = _mm(attn, w_pa, name="mm_pa", M=N, N=D, K=D, bm=bnl)
    pd = _mm(gdn, w_pd, name="mm_pd", M=N, N=D, K=D, bm=bnl)
    y = _merge_fwd(pa, pd, proj, L)
    m = _mm(y, w_out, name="mm_out", M=N, N=D, K=D, bm=bnl)
    x1 = _resid_fwd(x, m, mod_lat, 2, name="resid1")
    h2 = _normmod_fwd(x1, mod_lat, 3, 4, name="normmod_x1")
    up = _mm(h2, w_up, name="mm_up", M=N, N=2 * DFF, K=D, tb=True, bm=bnl, bn=2 * DFF // 4)
    a = _ffn_fwd(up, ffn_w, ffn_b)
    dn = _mm(a, w_down, name="mm_down", M=N, N=D, K=DFF, bm=bnl)
    loss, dx2, ddn, dg2, dfnw = _head(x1, dn, mod_lat, fnw, tgt)

    da = _mm(ddn, w_down, name="mm_down_dx", M=N, N=DFF, K=D, tb=True, bm=bnl, bn=DFF // 2)
    g_down = _mm(a, ddn, name="mm_down_dw", M=DFF, N=D, K=N, ta=True, bm=DFF // 2, out_dtype=BF16)
    dup, d_ffn_w, d_ffn_b = _ffn_bwd(up, ffn_w, ffn_b, da)
    dh2 = _mm(dup, w_up, name="mm_up_dx", M=N, N=D, K=2 * DFF, bm=bnl, bk=2 * DFF // 4)
    g_up = _mm(dup, h2, name="mm_up_dw", M=2 * DFF, N=D, K=N, ta=True, bm=2 * DFF // 4, out_dtype=BF16)
    dx1, dsh2, dsc2 = _normmod_bwd(x1, mod_lat, 3, 4, dh2, 0, dx2, name="normmod_x1_bwd")
    dm, dg1 = _resid_bwd(dx1, m, mod_lat, 2, name="resid1_bwd")
    dy = _mm(dm, w_out, name="mm_out_dx", M=N, N=D, K=D, tb=True, bm=bnl)
    g_out = _mm(y, dm, name="mm_out_dw", M=D, N=D, K=N, ta=True, out_dtype=BF16)
    dpa, dpd, dproj = _merge_bwd(pa, pd, proj, dy, L)
    dattn = _mm(dpa, w_pa, name="mm_pa_dx", M=N, N=D, K=D, tb=True, bm=bnl)
    g_pa = _mm(attn, dpa, name="mm_pa_dw", M=D, N=D, K=N, ta=True, out_dtype=BF16)
    dgdn = _mm(dpd, w_pd, name="mm_pd_dx", M=N, N=D, K=D, tb=True, bm=bnl)
    g_pd = _mm(gdn, dpd, name="mm_pd_dw", M=D, N=D, K=N, ta=True, out_dtype=BF16)
    do, dproj, dgw = _gout_bwd(o, proj, gw, dgdn, dproj, L)
    cts, recv_a = _scan_bwd(*intra, states, do, L, _Exchange(
        [g_out.reshape(NDEV, D // NDEV, D), g_pa.reshape(NDEV, D // NDEV, D), g_pd.reshape(NDEV, D // NDEV, D)], True))
    (dgq, dgk, dgv, dbl), recv_b = _intra_bwd(gq, gk, gv, bl, xinv, cts, L, _Exchange(
        [g_up.reshape(NDEV, 2 * DFF // NDEV, D)], True))
    dproj, dwq = _gprep_bwd(proj, conv_w, 0, bounds, dgq, dproj)
    dproj, dwk = _gprep_bwd(proj, conv_w, 1, bounds, dgk, dproj)
    dproj, dwv = _gprep_bwd(proj, conv_w, 2, bounds, dgv, dproj)
    dproj, dalog, ddtb = _bl_bwd(proj, alog, dtb, dbl, dproj)
    (daq_h, dak_h, dav_h), recv_c = _attn_bwd(aq, ak, av, attn32, lse, dattn, L, _Exchange(
        [g_down.reshape(NDEV, DFF // NDEV, D)], True))
    recv = dict(zip(("w_out", "w_pa", "w_pd", "w_up", "w_down"), recv_a + recv_b + recv_c))
    dproj, dqw, dkw = _aprep_bwd(proj, cos, sin, qw, kw, daq_h, dak_h, dav_h, dproj, L)
    g_in = _mm(dproj, h1, name="mm_in_dw", M=C_END, N=D, K=T, ta=True, bm=1024, out_dtype=BF16)
    *pending, token = _scatter_start(g_in, None, (0, D // 2), (), name="scatter_g_in_a_start")
    dh1 = _mm(dproj, w_in, name="mm_in_dx", M=T, N=D, K=C_END, bm=bt, bk=1024, after=(token,))
    grad_x, dsh1, dsc1 = _normmod_bwd(x, mod_lat, 0, 1, dh1, L, dx1, name="normmod_x_bwd")
    _, dcsh1, dcsc1 = _normmod_bwd(ctx, mod_ctx, 0, 1, dh1, 0, None, name="normmod_ctx_bwd")

    z1 = jnp.zeros((1, D), F32)
    dmod_lat = jnp.concatenate([dsh1, dsc1, dg1, dsh2, dsc2, dg2], axis=0)
    dmod_ctx = jnp.concatenate([dcsh1, dcsc1, z1, z1, z1, z1], axis=0)
    gsmall = {
        "q_norm_w": dqw, "k_norm_w": dkw, "gdn_norm_w": dgw,
        "conv_qkv_w": jnp.concatenate([dwq, dwk, dwv], axis=1),
        "a_log": dalog[0, 2 * GH:4 * GH], "dt_bias": ddtb[0, 2 * GH:4 * GH],
        "ffn_conv_w": d_ffn_w, "ffn_conv_b": d_ffn_b, "final_norm_w": dfnw,
    }
    return loss[0, 0], grad_x, pending, recv, dmod_lat, dmod_ctx, gsmall


HBM = pl.BlockSpec(memory_space=pltpu.HBM)
ANYSPEC = pl.BlockSpec(memory_space=pl.ANY)


def _position():
    x, y, c = lax.axis_index("x"), lax.axis_index("y"), lax.axis_index("c")
    return x, y, c, 4 * x + 2 * y + c


def _peer(x, y, c, k):
    px = 1 - x if k & 4 else x
    py = 1 - y if k & 2 else y
    pc = 1 - c if k & 1 else c
    return (px, py, pc), 4 * px + 2 * py + pc


def _exchange(arrs, *, name, scatter):
    exch = _Exchange(arrs, scatter)
    n = exch.n

    def body(*refs):
        ins, outs, sems = refs[:n], refs[n:2 * n], refs[2 * n:]
        exch.start(ins, outs, sems)
        exch.finish(ins, outs, sems)

    outs = pl.pallas_call(body, name=name, out_shape=exch.out_shape, in_specs=[HBM] * n, out_specs=(HBM,) * n,
                          scratch_shapes=exch.scratch,
                          compiler_params=pltpu.CompilerParams(has_side_effects=True))(*arrs)
    return list(outs)


class _Exchange:
    def __init__(self, arrs, scatter):
        self.arrs, self.scatter, self.n = list(arrs), scatter, len(arrs)
        self.out_shape = tuple(_sds(a.shape if scatter else (NDEV,) + a.shape, a.dtype) for a in arrs)
        self.scratch = [pltpu.SemaphoreType.DMA((self.n, NDEV - 1)), pltpu.SemaphoreType.DMA((self.n, NDEV - 1)),
                        pltpu.SemaphoreType.DMA((self.n,))]

    def _copies(self, ins, outs, sems):
        send, recv, loc = sems
        x, y, c, me = _position()
        local = [pltpu.make_async_copy(ins[a].at[me] if self.scatter else ins[a], outs[a].at[me], loc.at[a])
                 for a in range(self.n)]
        remote = []
        for k in range(1, NDEV):
            peer, pid = _peer(x, y, c, k)
            for a in range(self.n):
                src = ins[a].at[pid] if self.scatter else ins[a]
                remote.append(pltpu.make_async_remote_copy(
                    src_ref=src, dst_ref=outs[a].at[me], send_sem=send.at[a, k - 1], recv_sem=recv.at[a, k - 1],
                    device_id=peer, device_id_type=MESH))
        return local, remote

    def start(self, ins, outs, sems):
        local, remote = self._copies(ins, outs, sems)
        for cp in local + remote:
            cp.start()

    def finish(self, ins, outs, sems):
        local, remote = self._copies(ins, outs, sems)
        for cp in remote:
            cp.wait()
        for cp in local:
            cp.wait()


class _GatherTwoLevel:
    scatter = False

    def __init__(self, arrs):
        self.arrs, self.n = list(arrs), len(arrs)
        self.out_shape = tuple(_sds((NDEV,) + a.shape, a.dtype) for a in arrs)
        self.scratch = [pltpu.SemaphoreType.DMA((self.n, NDEV - 1)), pltpu.SemaphoreType.DMA((self.n, NDEV - 1)),
                        pltpu.SemaphoreType.DMA((self.n,))]

    def _parts(self, ins, outs, sems):
        send, recv, loc = sems
        x, y, c, _ = _position()
        me, sibling = (x, y, c), (x, y, 1 - c)
        chips = [(1 - x, y), (x, 1 - y), (1 - x, 1 - y)]
        parts = []
        for a in range(self.n):
            slot = lambda px, py, pc, a=a: outs[a].at[4 * px + 2 * py + pc]

            def copy(k, owner, to, src=None, a=a, slot=slot):
                return pltpu.make_async_remote_copy(
                    src_ref=slot(*owner) if src is None else src, dst_ref=slot(*owner), send_sem=send.at[a, k],
                    recv_sem=recv.at[a, k], device_id=to, device_id_type=MESH)

            parts.append(dict(
                mine=pltpu.make_async_copy(ins[a], slot(*me), loc.at[a]),
                first=[copy(0, me, sibling, src=ins[a])] + [copy(1 + j, me, (*ch, c), src=ins[a]) for j, ch in enumerate(chips)],
                arrive=[copy(1 + j, (*ch, c), me) for j, ch in enumerate(chips)],
                passed=[copy(4 + j, (*ch, c), sibling) for j, ch in enumerate(chips)],
                rest=[copy(0, sibling, me)] + [copy(4 + j, (*ch, 1 - c), me) for j, ch in enumerate(chips)]))
        return parts

    def start(self, ins, outs, sems):
        for p in self._parts(ins, outs, sems):
            p["mine"].start()
            for cp in p["first"]:
                cp.start()

    def middle(self, ins, outs, sems):
        for p in self._parts(ins, outs, sems):
            for got, fwd in zip(p["arrive"], p["passed"]):
                got.wait_recv()
                fwd.start()

    def finish(self, ins, outs, sems):
        for p in self._parts(ins, outs, sems):
            for cp in p["rest"]:
                cp.wait_recv()
            for cp in p["first"] + p["passed"]:
                cp.wait_send()
            p["mine"].wait()


def _gather_two_level(blocks, *, name):
    exch = _GatherTwoLevel(blocks)
    n = exch.n

    def body(*refs):
        ins, outs, sems = refs[:n], refs[n:2 * n], refs[2 * n:]
        exch.start(ins, outs, sems)
        exch.middle(ins, outs, sems)
        exch.finish(ins, outs, sems)

    outs = pl.pallas_call(body, name=name, out_shape=exch.out_shape, in_specs=[HBM] * n, out_specs=(HBM,) * n,
                          scratch_shapes=exch.scratch,
                          compiler_params=pltpu.CompilerParams(has_side_effects=True))(*blocks)
    return list(outs)


SEM = pl.BlockSpec(memory_space=pltpu.SEMAPHORE)


SHARD_ROWS = W_END // NDEV
RUNS = ((0, W_QKV, C_KV), (W_QKV, W_AQ - W_QKV, C_QKV), (W_AQ, W_Z - W_AQ, C_AQ), (W_Z, W_END - W_Z, C_Z))


ROW_TILE = 8
LANES = 128
SLOT_ROWS = -(-SHARD_ROWS // ROW_TILE) * ROW_TILE


def _shard_pieces(d):
    lo, hi = d * SHARD_ROWS, (d + 1) * SHARD_ROWS
    lead = lo % ROW_TILE
    pieces = []
    for first, rows, padded in RUNS:
        a, b = max(lo, first), min(hi, first + rows)
        if a < b:
            pieces.append([a - lo + lead, b - a, padded + a - first])
    pieces[0] = [0, pieces[0][1] + lead, pieces[0][2] - lead]
    pieces[-1][1] = SLOT_ROWS - pieces[-1][0]
    assert all(v % ROW_TILE == 0 for p in pieces for v in p) and all(p[2] + p[1] <= C_END for p in pieces)
    return pieces


def _scatter_send(src_ref, land_ref, send_sems, recv_sems, cols):
    _, _, _, me = _position()
    for d in range(NDEV):
        @pl.when(me != d)
        def _():
            k = jnp.bitwise_xor(me, d)
            peer = tuple(jnp.int32((d >> s) & 1) for s in (2, 1, 0))
            for off, rows, padded in _shard_pieces(d):
                pltpu.make_async_remote_copy(
                    src_ref=src_ref.at[pl.ds(padded, rows), pl.ds(*cols)],
                    dst_ref=land_ref.at[me].at[pl.ds(off, rows), pl.ds(*cols)], send_sem=send_sems.at[k - 1],
                    recv_sem=recv_sems.at[k - 1], device_id=peer, device_id_type=MESH).start()


def _scatter_whole(src_ref, land_ref, send_sems, recv_sems, cols):
    x, y, c, me = _position()
    span = (slice(None), pl.ds(*cols))
    copies = []
    for k in range(1, NDEV):
        peer, _ = _peer(x, y, c, k)
        copies.append(pltpu.make_async_remote_copy(
            src_ref=src_ref.at[pl.ds(0, SLOT_ROWS)].at[span], dst_ref=land_ref.at[me].at[span],
            send_sem=send_sems.at[k - 1], recv_sem=recv_sems.at[k - 1], device_id=peer, device_id_type=MESH))
    return copies


SPLIT_EFFECT = pltpu.SideEffectType.DATAFLOW_SIDE_EFFECTING


def _scatter_start(parts, land, cols, after, *, name):
    na = len(after)
    if land is None:
        land = lax.empty((NDEV, SLOT_ROWS, D), parts.dtype)

    def body(src_ref, land_ref, *rest):
        send_sems, recv_sems, _, _, token = rest[na:]
        _scatter_send(src_ref, land_ref, send_sems, recv_sems, cols)
        token[...] = jnp.zeros_like(token)

    return pl.pallas_call(
        body, name=name,
        out_shape=(pltpu.SemaphoreType.DMA((NDEV - 1,)), pltpu.SemaphoreType.DMA((NDEV - 1,)),
                   pltpu.HBM(parts.shape, parts.dtype), pltpu.HBM(land.shape, land.dtype), _sds((8, HD))),
        in_specs=(HBM, HBM) + (pl.BlockSpec(memory_space=pl.ANY),) * na,
        out_specs=(SEM, SEM, HBM, HBM, pl.BlockSpec(memory_space=pltpu.VMEM)),
        input_output_aliases={0: 2, 1: 3}, compiler_params=pltpu.CompilerParams(has_side_effects=SPLIT_EFFECT),
    )(pltpu.with_memory_space_constraint(parts, pltpu.HBM), pltpu.with_memory_space_constraint(land, pltpu.HBM), *after)


def _scatter_wait(send_sems, recv_sems, src_thru, land_thru, cols, after, *, name):
    na = len(after)

    def body(src_ref, land_ref, send_sems, recv_sems, *rest):
        for cp in _scatter_whole(src_ref, land_ref, send_sems, recv_sems, cols):
            cp.wait_send()
            cp.wait_recv()

    return pl.pallas_call(
        body, name=name,
        out_shape=(pltpu.HBM(src_thru.shape, src_thru.dtype), pltpu.HBM(land_thru.shape, land_thru.dtype)),
        in_specs=(HBM, HBM, SEM, SEM) + (pl.BlockSpec(memory_space=pl.ANY),) * na, out_specs=(HBM, HBM),
        input_output_aliases={0: 0, 1: 1}, compiler_params=pltpu.CompilerParams(has_side_effects=SPLIT_EFFECT),
    )(src_thru, land_thru, send_sems, recv_sems, *after)


def _cast_bf16(ws, *, name):
    k = len(ws)

    def body(*refs):
        for w_ref, o_ref in zip(refs[:k], refs[k:]):
            o_ref[...] = w_ref[...].astype(BF16)

    return _call(body, name=name, out_shape=tuple(_sds(w.shape, BF16) for w in ws), vmem=VMEM_BIG)(*ws)


def _sum_slots(a, *, name):
    _, R, C = a.shape

    def body(a_ref, o_ref):
        s = a_ref[0]
        for d in range(1, NDEV):
            s = s + a_ref[d]
        o_ref[...] = s

    return _call(body, name=name, out_shape=_sds((R, C)))(a)


MODROWS = 16


def _mod_fwd(c9, w, b):
    cols = w.shape[1]

    def body(c_ref, w_ref, b_ref, o_ref):
        o_ref[...] = _nn(_silu(c_ref[...]), w_ref[...]) + b_ref[...]

    return _call(body, name="mod_fwd", out_shape=_sds((MODROWS, cols)))(c9, w, b)


def _mod_bwd(c9, dmy, dall, w):
    cols = w.shape[1]

    def body(c_ref, dmy_ref, dall_ref, w_ref, gw_ref, gb_ref, cp_ref):
        sc = _silu(c_ref[...])
        rows = lax.broadcasted_iota(jnp.int32, (MODROWS, 1), 0)
        d = dmy_ref[...]
        d_ctx = jnp.where(rows == NDEV, d, 0.0)
        sc_ctx = jnp.where(rows == NDEV, sc, 0.0)
        outer = lax.dot_general(sc_ctx, d_ctx, (((0,), (0,)), ((), ())), precision=HI, preferred_element_type=F32)
        gw_ref[...] = _tn(jnp.where(rows < NDEV, sc, 0.0), jnp.where(rows < NDEV, d, 0.0)) + outer
        gb_ref[...] = jnp.sum(dall_ref[...], axis=0, keepdims=True)
        cp_ref[...] = jnp.sum(_nt(d_ctx, w_ref[...]), axis=0, keepdims=True)

    return _call(body, name="mod_bwd", out_shape=(_sds((D, cols)), _sds((1, 6 * D)), _sds((1, D))),
                 vmem=VMEM_BIG)(c9, dmy, dall, w)


def _cctx_finish(parts, c_ctx, after):
    VM = pl.BlockSpec(memory_space=pltpu.VMEM)

    def body(p_ref, c_ref, *rest):
        o_ref = rest[-1]
        s = p_ref[0]
        for d in range(1, NDEV):
            s = s + p_ref[d]
        _, vjp = jax.vjp(_silu, c_ref[...])
        o_ref[...] = vjp(s)[0]

    return _call(body, name="cctx_finish", out_shape=_sds((1, D)),
                 in_specs=[VM, VM] + [pl.BlockSpec(memory_space=pl.ANY)] * len(after))(parts, c_ctx, *after)


def _adamw_recv(w, recv, m, v, *, name, own=None, transposed=False):
    rows, cols = w.shape[::-1] if transposed else w.shape
    slot_rows = recv.shape[1]
    lead = slot_rows - rows
    assert rows % ROW_TILE in (0, lead)
    bc = 256
    c1 = 1.0 - B1 ** STEP
    c2 = 1.0 - B2 ** STEP
    has_own = own is not None

    def body(w_ref, r_ref, m_ref, v_ref, *rest):
        g_ref, d_ref, nm_ref, nv_ref = rest[-4:]
        me = _position()[3]

        def slot(d):
            return jnp.where(me == d, rest[0][...], r_ref[d]) if has_own else r_ref[d]

        gv = slot(0).astype(F32)
        for d in range(1, NDEV):
            gv = gv + slot(d).astype(F32)
        aligned = (me * rows) % ROW_TILE == 0
        if transposed:
            full = -(-rows // LANES) * LANES
            more = full + (ROW_TILE if lead else 0) - slot_rows
            if more:
                gv = jnp.concatenate([gv, jnp.zeros((more, bc), F32)], axis=0)
            if lead:
                gv = jnp.where(aligned, gv[:full], gv[lead:lead + full])
            gv = gv.T[:, :rows]
        elif lead:
            gv = jnp.where(aligned, gv[:rows], gv[lead:])
        nm = B1 * m_ref[...] + (1.0 - B1) * gv
        nv = B2 * v_ref[...] + (1.0 - B2) * (gv * gv)
        g_ref[...] = gv
        d_ref[...] = -LR * ((nm / c1) / (jnp.sqrt(nv / c2) + AEPS) + WD * w_ref[...])
        nm_ref[...] = nm
        nv_ref[...] = nv

    blk = pl.BlockSpec((bc, rows), lambda j: (j, 0)) if transposed else pl.BlockSpec((rows, bc), lambda j: (0, j))
    return _call(body, name=name, out_shape=(_sds(w.shape),) * 4, grid=(cols // bc,),
                 in_specs=[blk, pl.BlockSpec((NDEV, slot_rows, bc), lambda j: (0, 0, j)), blk, blk]
                 + [pl.BlockSpec((slot_rows, bc), lambda j: (0, j))] * has_own,
                 out_specs=(blk,) * 4, sem=("parallel",), vmem=VMEM_BIG)(w, recv, m, v, *([own] if has_own else []))


P_LAT, P_CTX, P_FNW, P_FFNB, P_CONV, P_FFNW, P_MISC, P_ROWS = 0, 8, 16, 24, 32, 48, 72, 80


def _rows_of(v, nrows):
    flat = v.reshape(-1)
    return jnp.pad(flat, (0, nrows * D - flat.shape[0])).reshape(nrows, D)


def _by_columns(g):
    n, r, c = g.shape
    return jnp.transpose(g, (1, 0, 2)).reshape(r, n * c)


def kernel(x, c, ctx, c_ctx, w_mod, b_mod, w_in, q_norm_w, k_norm_w, conv_qkv_w, a_log, dt_bias, gdn_norm_w, w_pa, w_pd, w_out, w_up, ffn_conv_w, ffn_conv_b, w_down, final_norm_w, loss_target, m_c_ctx, m_w_mod, m_b_mod, m_w_in, m_q_norm_w, m_k_norm_w, m_conv_qkv_w, m_a_log, m_dt_bias, m_gdn_norm_w, m_w_pa, m_w_pd, m_w_out, m_w_up, m_ffn_conv_w, m_ffn_conv_b, m_w_down, m_final_norm_w, v_c_ctx, v_w_mod, v_b_mod, v_w_in, v_q_norm_w, v_k_norm_w, v_conv_qkv_w, v_a_log, v_dt_bias, v_gdn_norm_w, v_w_pa, v_w_pd, v_w_out, v_w_up, v_ffn_conv_w, v_ffn_conv_b, v_w_down, v_final_norm_w):
    _, _, _, me = _position()
    mcols = w_mod.shape[2]

    transposed = ("w_in", "w_up")
    big = {"w_in": w_in[0].T, "w_pa": w_pa[0], "w_pd": w_pd[0], "w_out": w_out[0], "w_up": w_up[0].T, "w_down": w_down[0]}
    names = list(big)
    shards = dict(zip(names, _cast_bf16([big[n] for n in names], name="cast_weights")))
    w_in_g, c_all, conv_g, ffnw_g = _gather_two_level([shards["w_in"], c, conv_qkv_w[0], ffn_conv_w[0]],
                                                      name="gather_w_in")
    w_in_full = w_in_g.reshape(W_END, D)
    w_in_pad = _pad_columns(w_in_full)

    c9 = jnp.concatenate([c_all.reshape(NDEV, D), jnp.pad(c_ctx[None], ((0, MODROWS - NDEV - 1), (0, 0)))], axis=0)
    b_loc = lax.dynamic_slice(b_mod, (0, me * mcols), (1, mcols))
    mod_all, = _exchange([_mod_fwd(c9, w_mod[0], b_loc)], name="gather_mod", scatter=False)
    mod_lat = lax.dynamic_index_in_dim(mod_all, me, axis=1, keepdims=False).reshape(6, D)
    mod_ctx = mod_all[:, NDEV, :].reshape(6, D)

    small = {"q_norm_w": q_norm_w, "k_norm_w": k_norm_w, "gdn_norm_w": gdn_norm_w, "a_log": a_log, "dt_bias": dt_bias,
             "conv_qkv_w": _by_columns(conv_g), "ffn_conv_w": _by_columns(ffnw_g), "ffn_conv_b": ffn_conv_b,
             "final_norm_w": final_norm_w[None]}
    loss_me, grad_x, pending_in, recv, dmod_lat, dmod_ctx, gs = _local_step(
        x[0], ctx[0], loss_target[0], mod_lat, mod_ctx, w_in_pad, shards, small)

    moments = {"w_in": (m_w_in, v_w_in), "w_pa": (m_w_pa, v_w_pa), "w_pd": (m_w_pd, v_w_pd),
               "w_out": (m_w_out, v_w_out), "w_up": (m_w_up, v_w_up), "w_down": (m_w_down, v_w_down)}
    res = {}
    given_big = {"w_in": w_in, "w_pa": w_pa, "w_pd": w_pd, "w_out": w_out, "w_up": w_up, "w_down": w_down}

    def update(n, got, own=None):
        outs = _adamw_recv(given_big[n][0], got, moments[n][0][0], moments[n][1][0], name="adamw_" + n, own=own,
                           transposed=n in transposed)
        return tuple(t[None] for t in outs)

    for n in recv:
        res[n] = update(n, recv[n])

    misc = jnp.concatenate([gs["q_norm_w"][0], gs["k_norm_w"][0], gs["gdn_norm_w"][0], gs["a_log"], gs["dt_bias"],
                            loss_me[None]])
    pack = jnp.concatenate([_rows_of(dmod_lat, P_CTX - P_LAT), _rows_of(dmod_ctx, P_FNW - P_CTX),
                            _rows_of(gs["final_norm_w"], P_FFNB - P_FNW), _rows_of(gs["ffn_conv_b"], P_CONV - P_FFNB),
                            _rows_of(gs["conv_qkv_w"], P_FFNW - P_CONV), _rows_of(gs["ffn_conv_w"], P_MISC - P_FFNW),
                            _rows_of(misc, P_ROWS - P_MISC)], axis=0)
    pack_all, = _exchange([pack], name="gather_pack", scatter=False)
    tot = _sum_slots(pack_all, name="sum_pack")
    dall = jnp.concatenate([pack_all[:, P_LAT:P_LAT + 6, :].reshape(NDEV, 6 * D),
                            jnp.pad(tot[P_CTX:P_CTX + 6].reshape(1, 6 * D), ((0, MODROWS - NDEV - 1), (0, 0)))], axis=0)
    dmy = lax.dynamic_slice(dall, (0, me * mcols), (MODROWS, mcols))
    g_w_mod, g_b_mod, cpart = _mod_bwd(c9, dmy, dall, w_mod[0])
    cparts, = _exchange([cpart], name="gather_cctx", scatter=False)
    sems_a, land = pending_in[:2], pending_in[3]
    *sems_b, g_in_thru, land, token_b = _scatter_start(pending_in[2], land, (D // 2, D // 2), (cparts,),
                                                       name="scatter_g_in_b_start")
    g_c_ctx = _cctx_finish(cparts, c_ctx[None], (token_b,))[0]

    nconv, nffn = 3 * GH * HD, 2 * DFF
    conv_tot = tot[P_CONV:P_FFNW].reshape(-1)[:3 * nconv].reshape(3, nconv)
    ffnw_tot = tot[P_FFNW:P_MISC].reshape(-1)[:3 * nffn].reshape(3, nffn)
    mrow = tot[P_MISC]
    grads = {
        "c_ctx": g_c_ctx, "w_mod": g_w_mod[None], "b_mod": g_b_mod,
        "q_norm_w": mrow[None, 0:HD], "k_norm_w": mrow[None, HD:2 * HD], "gdn_norm_w": mrow[None, 2 * HD:3 * HD],
        "conv_qkv_w": lax.dynamic_slice(conv_tot, (0, me * (nconv // NDEV)), (3, nconv // NDEV))[None],
        "a_log": mrow[3 * HD:3 * HD + 2 * GH].reshape(1, 2, GH),
        "dt_bias": mrow[3 * HD + 2 * GH:3 * HD + 4 * GH].reshape(1, 2, GH),
        "ffn_conv_w": lax.dynamic_slice(ffnw_tot, (0, me * (nffn // NDEV)), (3, nffn // NDEV))[None],
        "ffn_conv_b": tot[P_FFNB:P_CONV].reshape(-1)[:nffn][None],
        "final_norm_w": tot[P_FNW],
    }
    loss = mrow[3 * HD + 4 * GH]
    given = {"c_ctx": (c_ctx, m_c_ctx, v_c_ctx), "w_mod": (w_mod, m_w_mod, v_w_mod), "b_mod": (b_mod, m_b_mod, v_b_mod),
             "q_norm_w": (q_norm_w, m_q_norm_w, v_q_norm_w), "k_norm_w": (k_norm_w, m_k_norm_w, v_k_norm_w),
             "conv_qkv_w": (conv_qkv_w, m_conv_qkv_w, v_conv_qkv_w), "a_log": (a_log, m_a_log, v_a_log),
             "dt_bias": (dt_bias, m_dt_bias, v_dt_bias), "gdn_norm_w": (gdn_norm_w, m_gdn_norm_w, v_gdn_norm_w),
             "ffn_conv_w": (ffn_conv_w, m_ffn_conv_w, v_ffn_conv_w), "ffn_conv_b": (ffn_conv_b, m_ffn_conv_b, v_ffn_conv_b),
             "final_norm_w": (final_norm_w, m_final_norm_w, v_final_norm_w)}
    res["w_mod"] = (grads["w_mod"],) + _adamw(w_mod, grads["w_mod"], m_w_mod, v_w_mod, name="adamw_w_mod")
    small_names = [n for n in given if n != "w_mod"]
    updates = _adamw_many([(given[n][0], grads[n], given[n][1], given[n][2]) for n in small_names], name="adamw_small")
    for n, upd in zip(small_names, updates):
        res[n] = (grads[n],) + upd

    first = me * SHARD_ROWS
    own_in = lax.dynamic_slice(_unpad_columns(g_in_thru), (first - first % ROW_TILE, 0), (SLOT_ROWS, D))
    g_in_thru, land = _scatter_wait(*sems_a, g_in_thru, land, (0, D // 2), [res[n][1] for n in res] + [own_in],
                                    name="scatter_g_in_a_wait")
    _, land = _scatter_wait(*sems_b, g_in_thru, land, (D // 2, D // 2), (), name="scatter_g_in_b_wait")
    res["w_in"] = update("w_in", land, own_in)

    order = ["c_ctx", "w_mod", "b_mod", "w_in", "q_norm_w", "k_norm_w", "conv_qkv_w", "a_log", "dt_bias", "gdn_norm_w",
             "w_pa", "w_pd", "w_out", "w_up", "ffn_conv_w", "ffn_conv_b", "w_down", "final_norm_w"]
    return (loss, grad_x[None], *[res[n][0] for n in order], *[res[n][1] for n in order],
            *[res[n][2] for n in order], *[res[n][3] for n in order])
```

```python
import functools
import math

import jax
import jax.numpy as jnp
from jax import lax
from jax.experimental import pallas as pl
from jax.experimental.pallas import tpu as pltpu

F32 = jnp.float32
BF16 = jnp.bfloat16
HI = lax.Precision.HIGHEST
MESH = pl.DeviceIdType.MESH

NDEV = 8
D = 1024
HD = 128
AH, AKV, GRP = 8, 2, 4
GH = 8
CH = 64
DFF = 2816
GRID_W = 64
EPS = 1e-6
ROPE_THETA = 10000.0
LOG2E = math.log2(math.e)
C_KV, C_AQ, C_QKV, C_BL, C_Z, C_GATE, C_END = 0, 512, 1536, 4608, 5120, 6144, 8192
W_QKV, W_AQ, W_Z, W_END = 512, 3616, 4640, 7712


def _pad_columns(w):
    zeros = jnp.zeros((C_Z - C_QKV - (W_AQ - W_QKV), D), w.dtype)
    return jnp.concatenate([w[:W_QKV], w[W_AQ:W_Z], w[W_QKV:W_AQ], zeros, w[W_Z:]], axis=0)


def _unpad_columns(g):
    return jnp.concatenate([g[:C_AQ], g[C_QKV:C_QKV + W_AQ - W_QKV], g[C_AQ:C_QKV], g[C_Z:]], axis=0)
LR, B1, B2, AEPS, WD, STEP = 0.001, 0.9, 0.999, 1e-08, 0.01, 10
VMEM_BIG = 56 * 1024 * 1024
INTRA_FWD_CHUNKS = 36
INTRA_BWD_CHUNKS = 36


def _call(body, *, name, out_shape, grid=None, in_specs=None, out_specs=None, scratch=(), sem=None,
          vmem=None, aliases=None):
    params = {}
    if sem is not None:
        params["dimension_semantics"] = sem
    if vmem is not None:
        params["vmem_limit_bytes"] = vmem
    kw = {}
    if grid is not None:
        kw["grid"] = grid
    if in_specs is not None:
        kw["in_specs"] = in_specs
    if out_specs is not None:
        kw["out_specs"] = out_specs
    if aliases:
        kw["input_output_aliases"] = aliases
    return pl.pallas_call(body, name=name, out_shape=out_shape, scratch_shapes=list(scratch),
                          compiler_params=pltpu.CompilerParams(**params), **kw)


def _call_carrying(body, exch, *, name, out_shape, grid, in_specs, out_specs, scratch=(), vmem=None):
    n, nin, nout, nscr = exch.n, len(in_specs), len(out_shape), len(scratch)
    steps = math.prod(grid)
    mid = (2 * steps) // 3

    def wrapped(*refs):
        ins, cins = refs[:nin], refs[nin:nin + n]
        outs, couts = refs[nin + n:nin + n + nout], refs[nin + n + nout:nin + 2 * n + nout]
        scr, sems = refs[nin + 2 * n + nout:nin + 2 * n + nout + nscr], refs[nin + 2 * n + nout + nscr:]
        ids = [pl.program_id(i) for i in range(len(grid))]
        first = functools.reduce(jnp.logical_and, [i == 0 for i in ids])
        last = functools.reduce(jnp.logical_and, [i == g - 1 for i, g in zip(ids, grid)])

        @pl.when(first)
        def _():
            exch.start(cins, couts, sems)

        if hasattr(exch, "middle"):
            linear = functools.reduce(lambda acc, ig: acc * ig[1] + ig[0], zip(ids, grid), 0)

            @pl.when(linear == mid)
            def _():
                exch.middle(cins, couts, sems)

        body(*ins, *outs, *scr)

        @pl.when(last)
        def _():
            exch.finish(cins, couts, sems)

    params = {"dimension_semantics": ("arbitrary",) * len(grid)}
    if vmem is not None:
        params["vmem_limit_bytes"] = vmem
    fn = pl.pallas_call(wrapped, name=name, out_shape=tuple(out_shape) + exch.out_shape, grid=grid,
                        in_specs=list(in_specs) + [HBM] * n, out_specs=tuple(out_specs) + (HBM,) * n,
                        scratch_shapes=list(scratch) + exch.scratch, compiler_params=pltpu.CompilerParams(**params))

    def run(*args):
        res = fn(*args, *exch.arrs)
        return res[:nout], list(res[nout:])

    return run


def _sds(shape, dtype=F32):
    return jax.ShapeDtypeStruct(tuple(shape), dtype)


def _dot(a, b, ca, cb):
    return lax.dot_general(a.astype(BF16), b.astype(BF16), (((ca,), (cb,)), ((), ())),
                           preferred_element_type=F32)


@jax.custom_vjp
def _nn(a, b):
    return _dot(a, b, 1, 0)


@jax.custom_vjp
def _nt(a, b):
    return _dot(a, b, 1, 1)


@jax.custom_vjp
def _tn(a, b):
    return _dot(a, b, 0, 0)


_nn.defvjp(lambda a, b: (_nn(a, b), (a, b)), lambda r, g: (_nt(g, r[1]), _tn(r[0], g)))
_nt.defvjp(lambda a, b: (_nt(a, b), (a, b)), lambda r, g: (_nn(g, r[1]), _tn(g, r[0])))
_tn.defvjp(lambda a, b: (_tn(a, b), (a, b)), lambda r, g: (_nt(r[1], g), _nn(r[0], g)))


def _mdot(a, b):
    return jnp.dot(a, b, precision=lax.Precision.HIGH, preferred_element_type=F32)


def _maskdot(mask, a, cm):
    hi = a.astype(BF16)
    r = a - hi.astype(F32)
    mid = r.astype(BF16)
    lo = (r - mid.astype(F32)).astype(BF16)
    mb = mask.astype(BF16)
    dims = (((cm,), (0,)), ((), ()))
    return (lax.dot_general(mb, hi, dims, preferred_element_type=F32)
            + lax.dot_general(mb, mid, dims, preferred_element_type=F32)
            + lax.dot_general(mb, lo, dims, preferred_element_type=F32))


@jax.custom_vjp
def _mask_nn(mask, a):
    return _maskdot(mask, a, 1)


_mask_nn.defvjp(lambda mask, a: (_maskdot(mask, a, 1), mask),
                lambda mask, g: (jnp.zeros_like(mask), _maskdot(mask, g, 0)))


@jax.custom_vjp
def _saved_inverse(lmat, x):
    return x


def _saved_inverse_bwd(x, g):
    t = lax.dot_general(x, g, (((0,), (0,)), ((), ())), precision=lax.Precision.HIGH, preferred_element_type=F32)
    dl = lax.dot_general(t, x, (((1,), (1,)), ((), ())), precision=lax.Precision.HIGH, preferred_element_type=F32)
    return -dl, jnp.zeros_like(x)


_saved_inverse.defvjp(lambda lmat, x: (x, x), _saved_inverse_bwd)


def _row_ids(shape):
    return lax.broadcasted_iota(jnp.int32, shape, 0)


def _shift_rows(x, down, bounds):
    n = x.shape[0]
    rows = _row_ids(x.shape)
    y = pltpu.roll(x, 1 if down else n - 1, 0)
    edge = functools.reduce(jnp.logical_or, [rows == (s if down else e - 1) for s, e in bounds])
    return jnp.where(edge, 0.0, y)


def _make_shift(bounds):
    @jax.custom_vjp
    def down(x):
        return _shift_rows(x, True, bounds)

    @jax.custom_vjp
    def up(x):
        return _shift_rows(x, False, bounds)

    down.defvjp(lambda x: (down(x), None), lambda _, g: (up(g),))
    up.defvjp(lambda x: (up(x), None), lambda _, g: (down(g),))
    return down, up


@jax.custom_vjp
def _swap32(x):
    lane = lax.broadcasted_iota(jnp.int32, x.shape, x.ndim - 1)
    return jnp.where((lane % 64) < 32, pltpu.roll(x, HD - 32, x.ndim - 1), pltpu.roll(x, 32, x.ndim - 1))


_swap32.defvjp(lambda x: (_swap32(x), None), lambda _, g: (_swap32(g),))


def _rms(x):
    return x * lax.rsqrt(jnp.mean(x * x, axis=-1, keepdims=True) + EPS)


def _silu(x):
    return x * jax.nn.sigmoid(x)


def _mm(a, b, *, name, M, N, K, ta=False, tb=False, out_dtype=F32, bm=None, bn=None, bk=None, after=()):
    bm, bn, bk = bm or M, bn or N, bk or K
    assert M % bm == 0 and N % bn == 0 and K % bk == 0, (name, M, N, K, bm, bn, bk)
    nk = K // bk
    ca, cb = (0 if ta else 1), (1 if tb else 0)
    na = len(after)

    def body(a_ref, b_ref, *rest):
        o_ref, acc = rest[na], rest[na + 1:]
        r = _dot(a_ref[...], b_ref[...], ca, cb)
        if nk == 1:
            o_ref[...] = r.astype(out_dtype)
        else:
            acc_ref, = acc
            k = pl.program_id(2)

            @pl.when(k == 0)
            def _():
                acc_ref[...] = r

            @pl.when(k > 0)
            def _():
                acc_ref[...] += r

            @pl.when(k == nk - 1)
            def _():
                o_ref[...] = acc_ref[...].astype(out_dtype)

    a_spec = pl.BlockSpec((bk, bm), lambda i, j, k: (k, i)) if ta else pl.BlockSpec((bm, bk), lambda i, j, k: (i, k))
    b_spec = pl.BlockSpec((bn, bk), lambda i, j, k: (j, k)) if tb else pl.BlockSpec((bk, bn), lambda i, j, k: (k, j))
    return _call(body, name=name, out_shape=_sds((M, N), out_dtype), grid=(M // bm, N // bn, nk),
                 in_specs=[a_spec, b_spec] + [pl.BlockSpec(memory_space=pl.ANY)] * na,
                 out_specs=pl.BlockSpec((bm, bn), lambda i, j, k: (i, j)),
                 scratch=[pltpu.VMEM((bm, bn), F32)] if nk > 1 else [],
                 sem=("parallel", "parallel", "arbitrary"), vmem=VMEM_BIG)(a, b, *after)


def _normmod_fn(x, sh, sc):
    return _rms(x) * (1.0 + sc) + sh


def _normmod_fwd(x, mod, i_sh, i_sc, *, name, br=256):
    R = x.shape[0]

    def body(x_ref, mod_ref, o_ref):
        o_ref[...] = _normmod_fn(x_ref[...], mod_ref[i_sh:i_sh + 1, :], mod_ref[i_sc:i_sc + 1, :]).astype(BF16)

    return _call(body, name=name, out_shape=_sds((R, D), BF16), grid=(R // br,),
                 in_specs=[pl.BlockSpec((br, D), lambda i: (i, 0)), pl.BlockSpec((6, D), lambda i: (0, 0))],
                 out_specs=pl.BlockSpec((br, D), lambda i: (i, 0)), sem=("parallel",))(x, mod)


def _normmod_bwd(x, mod, i_sh, i_sc, dh, dh_off, res, *, name, br=256):
    R = x.shape[0]
    ob = dh_off // br
    has_res = res is not None

    def body(x_ref, mod_ref, dh_ref, *rest):
        if has_res:
            res_ref, dx_ref, dsh_ref, dsc_ref = rest
        else:
            dx_ref, dsh_ref, dsc_ref = rest
        sh, sc = mod_ref[i_sh:i_sh + 1, :], mod_ref[i_sc:i_sc + 1, :]
        _, vjp = jax.vjp(_normmod_fn, x_ref[...], sh, sc)
        dx, dsh, dsc = vjp(dh_ref[...])
        dx_ref[...] = dx + res_ref[...] if has_res else dx

        @pl.when(pl.program_id(0) == 0)
        def _():
            dsh_ref[...] = jnp.zeros_like(dsh_ref)
            dsc_ref[...] = jnp.zeros_like(dsc_ref)

        dsh_ref[...] += dsh
        dsc_ref[...] += dsc

    row = pl.BlockSpec((br, D), lambda i: (i, 0))
    vec = pl.BlockSpec((1, D), lambda i: (0, 0))
    ins = [row, pl.BlockSpec((6, D), lambda i: (0, 0)), pl.BlockSpec((br, D), lambda i: (i + ob, 0))]
    args = [x, mod, dh]
    if has_res:
        ins.append(row)
        args.append(res)
    return _call(body, name=name, out_shape=(_sds((R, D)), _sds((1, D)), _sds((1, D))), grid=(R // br,),
                 in_specs=ins, out_specs=(row, vec, vec), sem=("arbitrary",))(*args)


def _rope(x, cos, sin):
    return x * cos + _swap32(x) * sin


def _aprep_fn(qs, ks, cos, sin, qw, kw):
    return ([_rope(_rms(q) * qw, cos, sin) for q in qs], [_rope(_rms(k) * kw, cos, sin) for k in ks])


def _aprep_fwd(proj, cos, sin, qw, kw, *, br=256):
    T = proj.shape[0]

    def body(x_ref, cos_ref, sin_ref, qw_ref, kw_ref, q_ref, k_ref, v_ref):
        qs = [x_ref[:, C_AQ + h * HD:C_AQ + (h + 1) * HD] for h in range(AH)]
        ks = [x_ref[:, h * HD:(h + 1) * HD] for h in range(AKV)]
        qo, ko = _aprep_fn(qs, ks, cos_ref[...], sin_ref[...], qw_ref[...], kw_ref[...])
        for h in range(AH):
            q_ref[h] = qo[h].astype(BF16)
        for h in range(AKV):
            k_ref[h] = ko[h].astype(BF16)
            v_ref[h] = x_ref[:, (AKV + h) * HD:(AKV + h + 1) * HD].astype(BF16)

    tab = pl.BlockSpec((br, HD), lambda i: (i, 0))
    vec = pl.BlockSpec((1, HD), lambda i: (0, 0))
    return _call(body, name="aprep_fwd",
                 out_shape=(_sds((AH, T, HD), BF16), _sds((AKV, T, HD), BF16), _sds((AKV, T, HD), BF16)),
                 grid=(T // br,),
                 in_specs=[pl.BlockSpec((br, C_QKV), lambda i: (i, 0)), tab, tab, vec, vec],
                 out_specs=(pl.BlockSpec((AH, br, HD), lambda i: (0, i, 0)),
                            pl.BlockSpec((AKV, br, HD), lambda i: (0, i, 0)),
                            pl.BlockSpec((AKV, br, HD), lambda i: (0, i, 0))),
                 sem=("parallel",))(proj, cos, sin, qw, kw)


def _aprep_bwd(proj, cos, sin, qw, kw, dq, dk, dv, dproj, L, *, br=256):
    T = proj.shape[0]
    lb = L // br

    def body(x_ref, cos_ref, sin_ref, qw_ref, kw_ref, dq_ref, dk_ref, dv_ref, _, dx_ref, dqw_ref, dkw_ref):
        i = pl.program_id(0)
        qs = [x_ref[:, C_AQ + h * HD:C_AQ + (h + 1) * HD] for h in range(AH)]
        ks = [x_ref[:, h * HD:(h + 1) * HD] for h in range(AKV)]
        _, vjp = jax.vjp(_aprep_fn, qs, ks, cos_ref[...], sin_ref[...], qw_ref[...], kw_ref[...])
        is_lat = i >= lb
        dqs = [jnp.where(is_lat, dq_ref[h], 0.0) for h in range(AH)]
        dks = [dk_ref[h] for h in range(AKV)]
        gq, gk, _, _, gqw, gkw = vjp((dqs, dks))
        for h in range(AH):
            dx_ref[:, C_AQ + h * HD:C_AQ + (h + 1) * HD] = gq[h].astype(BF16)
        for h in range(AKV):
            dx_ref[:, h * HD:(h + 1) * HD] = gk[h].astype(BF16)
            dx_ref[:, (AKV + h) * HD:(AKV + h + 1) * HD] = dv_ref[h].astype(BF16)

        @pl.when(i == 0)
        def _():
            dqw_ref[...] = jnp.zeros_like(dqw_ref)
            dkw_ref[...] = jnp.zeros_like(dkw_ref)

        dqw_ref[...] += gqw
        dkw_ref[...] += gkw

    tab = pl.BlockSpec((br, HD), lambda i: (i, 0))
    vec = pl.BlockSpec((1, HD), lambda i: (0, 0))
    kvb = pl.BlockSpec((AKV, br, HD), lambda i: (0, i, 0))
    blk = pl.BlockSpec((br, C_QKV), lambda i: (i, 0))
    return _call(body, name="aprep_bwd", out_shape=(_sds(dproj.shape, BF16), _sds((1, HD)), _sds((1, HD))),
                 grid=(T // br,),
                 in_specs=[blk, tab, tab, vec, vec,
                           pl.BlockSpec((AH, br, HD), lambda i: (0, jnp.maximum(i - lb, 0), 0)), kvb, kvb, ANYSPEC],
                 out_specs=(blk, vec, vec), aliases={8: 0},
                 sem=("arbitrary",))(proj, cos, sin, qw, kw, dq, dk, dv, dproj)


def _attn_grad(q, k, v, o, lse2, do):
    scale = HD ** -0.5
    p = jnp.exp2(_dot(q, k, 1, 1) * (scale * LOG2E) - lse2)
    dp = _dot(do, v, 1, 1)
    ds = p * (dp - jnp.sum(do * o, axis=-1, keepdims=True)) * scale
    return _dot(ds, k, 1, 0), _dot(ds, q, 0, 0), _dot(p, do, 0, 0)


ATTN_KEYS = 256


def _attn_fwd(q, k, v, L, exch, *, bq=128):
    T = q.shape[1]
    N = T - L
    lb = L // bq
    assert T % ATTN_KEYS == 0
    scale = HD ** -0.5
    heads = range(GRP)

    def body(q_ref, k_ref, v_ref, o_ref, o32_ref, lse_ref):
        qs = [q_ref[g] for g in heads]
        m = [jnp.full((bq, 1), -jnp.inf, F32) for _ in heads]
        l = [jnp.zeros((bq, 1), F32) for _ in heads]
        acc = [jnp.zeros((bq, HD), F32) for _ in heads]
        for c in range(T // ATTN_KEYS):
            kc, vc = k_ref[c * ATTN_KEYS:(c + 1) * ATTN_KEYS, :], v_ref[c * ATTN_KEYS:(c + 1) * ATTN_KEYS, :]
            s = [_dot(qs[g], kc, 1, 1) * (scale * LOG2E) for g in heads]
            m_new = [jnp.maximum(m[g], jnp.max(s[g], axis=-1, keepdims=True)) for g in heads]
            alpha = [jnp.exp2(m[g] - m_new[g]) for g in heads]
            p = [jnp.exp2(s[g] - m_new[g]) for g in heads]
            l = [l[g] * alpha[g] + jnp.sum(p[g], axis=-1, keepdims=True) for g in heads]
            acc = [acc[g] * alpha[g] + _dot(p[g], vc, 1, 0) for g in heads]
            m = m_new
        for g in heads:
            o = acc[g] / l[g]
            o_ref[:, g * HD:(g + 1) * HD] = o.astype(BF16)
            o32_ref[:, g * HD:(g + 1) * HD] = o
            lse_ref[g] = jnp.broadcast_to(m[g] + jnp.log2(l[g]), (bq, HD))

    kvb = pl.BlockSpec((None, T, HD), lambda g, i: (g, 0, 0))
    ob = pl.BlockSpec((bq, GRP * HD), lambda g, i: (i, g))
    return _call_carrying(
        body, exch, name="attn_fwd",
        out_shape=(_sds((N, AH * HD), BF16), _sds((N, AH * HD)), _sds((AH, N, HD))), grid=(AKV, N // bq),
        in_specs=[pl.BlockSpec((GRP, bq, HD), lambda g, i: (g, i + lb, 0)), kvb, kvb],
        out_specs=(ob, ob, pl.BlockSpec((GRP, bq, HD), lambda g, i: (g, i, 0))), vmem=VMEM_BIG)(q, k, v)


def _attn_bwd(q, k, v, o32, lse, do, L, exch, *, bq=128):
    T = q.shape[1]
    N = T - L
    lb = L // bq

    def body(q_ref, k_ref, v_ref, o_ref, lse_ref, do_ref, dq_ref, dk_ref, dv_ref):
        rows = lambda r: jnp.concatenate([r[:, g * HD:(g + 1) * HD] for g in range(GRP)], axis=0)
        lse = jnp.max(lse_ref[...].reshape(GRP * bq, HD), axis=-1, keepdims=True)
        dq, dk, dv = _attn_grad(q_ref[...].reshape(GRP * bq, HD), k_ref[...], v_ref[...], rows(o_ref), lse, rows(do_ref))
        dq_ref[...] = dq.reshape(GRP, bq, HD)

        @pl.when(pl.program_id(1) == 0)
        def _():
            dk_ref[...] = jnp.zeros_like(dk_ref)
            dv_ref[...] = jnp.zeros_like(dv_ref)

        dk_ref[...] += dk
        dv_ref[...] += dv

    kvb = pl.BlockSpec((None, T, HD), lambda g, i: (g, 0, 0))
    qb = pl.BlockSpec((GRP, bq, HD), lambda g, i: (g, i + lb, 0))
    hb = pl.BlockSpec((GRP, bq, HD), lambda g, i: (g, i, 0))
    ob = pl.BlockSpec((bq, GRP * HD), lambda g, i: (i, g))
    return _call_carrying(body, exch, name="attn_bwd",
                          out_shape=(_sds((AH, N, HD)), _sds((AKV, T, HD)), _sds((AKV, T, HD))), grid=(AKV, N // bq),
                          in_specs=[qb, kvb, kvb, ob, hb, ob], out_specs=(hb, kvb, kvb),
                          vmem=VMEM_BIG)(q, k, v, o32, lse, do)


def _gprep_fn(kind, shifts, x, w):
    down, up = shifts
    y = down(x) * w[0:1, :] + x * w[1:2, :] + up(x) * w[2:3, :]
    a = _silu(y)
    if kind == 2:
        return a
    a = a * lax.rsqrt(jnp.sum(a * a, axis=-1, keepdims=True) + EPS)
    return a * (HD ** -0.5) if kind == 0 else a


def _gprep_fwd(proj, conv_w, kind, bounds):
    T = proj.shape[0]
    shifts = _make_shift(bounds)
    cb = C_QKV // HD + kind * GH

    def body(x_ref, w_ref, o_ref):
        o_ref[...] = _gprep_fn(kind, shifts, x_ref[...], w_ref[...])

    return _call(body, name=f"gprep_fwd{kind}", out_shape=_sds((GH, T, HD)), grid=(GH,),
                 in_specs=[pl.BlockSpec((T, HD), lambda h: (0, cb + h)),
                           pl.BlockSpec((3, HD), lambda h: (0, kind * GH + h))],
                 out_specs=pl.BlockSpec((None, T, HD), lambda h: (h, 0, 0)), sem=("parallel",))(proj, conv_w)


def _gprep_bwd(proj, conv_w, kind, bounds, dy, dproj):
    T = proj.shape[0]
    shifts = _make_shift(bounds)
    cb = C_QKV // HD + kind * GH

    def body(x_ref, w_ref, dy_ref, _, dx_ref, dw_ref):
        _, vjp = jax.vjp(functools.partial(_gprep_fn, kind, shifts), x_ref[...], w_ref[...])
        dx, dw = vjp(dy_ref[0] + dy_ref[1])
        dx_ref[...] = dx.astype(BF16)
        dw_ref[...] = dw

    return _call(body, name=f"gprep_bwd{kind}", out_shape=(_sds(dproj.shape, BF16), _sds((3, GH * HD))), grid=(GH,),
                 in_specs=[pl.BlockSpec((T, HD), lambda h: (0, cb + h)),
                           pl.BlockSpec((3, HD), lambda h: (0, kind * GH + h)),
                           pl.BlockSpec((2, None, T, HD), lambda h: (0, h, 0, 0)), ANYSPEC],
                 out_specs=(pl.BlockSpec((T, HD), lambda h: (0, cb + h)), pl.BlockSpec((3, HD), lambda h: (0, h))),
                 aliases={3: 0}, sem=("parallel",))(proj, conv_w, dy, dproj)


def _bl_fn(x, alog, dtb):
    lane = lax.broadcasted_iota(jnp.int32, x.shape, 1)
    beta = jax.nn.sigmoid(x)
    z = x + dtb
    sp = jnp.maximum(z, 0.0) + jnp.log1p(jnp.exp(-jnp.abs(z)))
    la = -jnp.exp(alog) * sp
    return jnp.where(lane < 2 * GH, beta, jnp.where(lane < 4 * GH, la, 0.0))


def _bl_fwd(proj, alog, dtb, *, br=256):
    T = proj.shape[0]

    def body(x_ref, a_ref, d_ref, o_ref):
        o_ref[...] = _bl_fn(x_ref[...], a_ref[...], d_ref[...])

    vec = pl.BlockSpec((1, HD), lambda i: (0, 0))
    return _call(body, name="bl_fwd", out_shape=_sds((T, HD)), grid=(T // br,),
                 in_specs=[pl.BlockSpec((br, HD), lambda i: (i, C_BL // HD)), vec, vec],
                 out_specs=pl.BlockSpec((br, HD), lambda i: (i, 0)), sem=("parallel",))(proj, alog, dtb)


def _bl_bwd(proj, alog, dtb, dbl, dproj, *, br=256):
    T = proj.shape[0]
    wide = C_Z - C_BL

    def body(x_ref, a_ref, d_ref, g_ref, _, dx_ref, da_ref, dd_ref):
        g = g_ref[0, 0]
        for d in range(2):
            for h in range(GH):
                if d or h:
                    g = g + g_ref[d, h]
        _, vjp = jax.vjp(_bl_fn, x_ref[...], a_ref[...], d_ref[...])
        dx, da, dd = vjp(g)
        dx_ref[:, :HD] = dx.astype(BF16)
        dx_ref[:, HD:] = jnp.zeros((br, wide - HD), BF16)

        @pl.when(pl.program_id(0) == 0)
        def _():
            da_ref[...] = jnp.zeros_like(da_ref)
            dd_ref[...] = jnp.zeros_like(dd_ref)

        da_ref[...] += da
        dd_ref[...] += dd

    vec = pl.BlockSpec((1, HD), lambda i: (0, 0))
    return _call(body, name="bl_bwd", out_shape=(_sds(dproj.shape, BF16), _sds((1, HD)), _sds((1, HD))), grid=(T // br,),
                 in_specs=[pl.BlockSpec((br, HD), lambda i: (i, C_BL // HD)), vec, vec,
                           pl.BlockSpec((2, GH, br, HD), lambda i: (0, 0, i, 0)), ANYSPEC],
                 out_specs=(pl.BlockSpec((br, wide), lambda i: (i, C_BL // wide)), vec, vec), aliases={4: 0},
                 sem=("arbitrary",))(proj, alog, dtb, dbl, dproj)


def _chunk_masks(d):
    ii = lax.broadcasted_iota(jnp.int32, (CH, CH), 0)
    jj = lax.broadcasted_iota(jnp.int32, (CH, CH), 1)
    eye = (ii == jj).astype(F32)
    before = jnp.where(d == 0, (jj < ii).astype(F32), (jj > ii).astype(F32))
    return before, before + eye, eye


def _same_block(b):
    ii = lax.broadcasted_iota(jnp.int32, (CH, CH), 0)
    jj = lax.broadcasted_iota(jnp.int32, (CH, CH), 1)
    shift = b.bit_length() - 1
    return (jnp.right_shift(ii, shift) == jnp.right_shift(jj, shift)).astype(F32)


def _intra_fn(masks, sel_b, sel_l, qs, ks, vs, bls, xs=None):
    before, ateq, eye = masks
    inc = ateq > 0.0
    each = lambda f, *ls: [f(*t) for t in zip(*ls)]
    beta = each(lambda bl: jnp.sum(bl * sel_b, axis=-1, keepdims=True), bls)
    la = each(lambda bl: jnp.sum(bl * sel_l, axis=-1, keepdims=True), bls)
    gam = each(lambda a: _mask_nn(ateq, jnp.broadcast_to(a, (CH, HD))), la)
    gi = each(lambda g: g[:, :CH], gam)
    gj = each(lambda g: jnp.transpose(g)[:CH, :], gam)
    kq = each(lambda k, q: _nt(jnp.concatenate([k, q], axis=0), k), ks, qs)
    kk = each(lambda t: t[:CH], kq)
    qk = each(lambda t: t[CH:], kq)
    dec = each(lambda a, b: jnp.where(inc, jnp.exp(jnp.where(inc, a - b, 0.0)), 0.0), gi, gj)
    lmat = each(lambda b, d, m: before * (b * d * m), beta, dec, kk)
    if xs is None:
        same = lambda b: _same_block(b)
        l8 = each(lambda m: m * same(8), lmat)
        x = each(lambda m: eye - m, l8)
        p2 = each(lambda m: _mdot(m, m), l8)
        y = each(lambda a, b: _mdot(jnp.concatenate([a, b], axis=0), b), x, p2)
        x = each(lambda a, t: a + t[:CH], x, y)
        x = each(lambda a, t: a + _mdot(a, t[CH:]), x, y)
        for b in (8, 16, 32):
            below = same(2 * b) - same(b)
            x = each(lambda a, m: a - _mdot(a, _mdot(m * below, a)), x, lmat)
    else:
        x = each(_saved_inverse, lmat, xs)
    eg = each(jnp.exp, gam)
    uw = each(lambda a, b, v, e, k: _mdot(a, jnp.concatenate([b * v, (b * e) * k], axis=1)), x, beta, vs, eg, ks)
    u = each(lambda t: t[:, :HD], uw)
    w = each(lambda t: t[:, HD:], uw)
    tot = each(lambda a: jnp.sum(a, axis=0, keepdims=True), la)
    kd = each(lambda k, t, g: k * jnp.exp(t - g), ks, tot, gam)
    gl = each(lambda t: jnp.broadcast_to(jnp.exp(t), (1, HD)), tot)
    qd = each(lambda q, e: q * e, qs, eg)
    p = each(lambda d, m: d * m, dec, qk)
    return (u, w, kd, qd, p, gl, x) if xs is None else (u, w, kd, qd, p, gl)


def _dir_head_sel(d, h):
    lane = lax.broadcasted_iota(jnp.int32, (1, HD), 1)
    return (lane == d * GH + h).astype(F32), (lane == 2 * GH + d * GH + h).astype(F32)


def _intra_specs(T, G):
    nc = T // CH
    assert nc % G == 0
    qkv = pl.BlockSpec((None, G * CH, HD), lambda d, h, c: (h, c, 0))
    bl = pl.BlockSpec((G * CH, HD), lambda d, h, c: (c, 0))
    big = pl.BlockSpec((None, None, G * CH, HD), lambda d, h, c: (d, h, c, 0))
    pm = pl.BlockSpec((None, None, G * CH, CH), lambda d, h, c: (d, h, c, 0))
    gl = pl.BlockSpec((None, None, G, 1, HD), lambda d, h, c: (d, h, c, 0, 0))
    shapes = (_sds((2, GH, T, HD)),) + (_sds((2, GH, T, HD), BF16),) * 3 + (
        _sds((2, GH, T, CH), BF16), _sds((2, GH, nc, 1, HD)), _sds((2, GH, T, CH)))
    return nc, qkv, bl, big, pm, gl, shapes


def _chunks_per_step(T, most):
    nc = T // CH
    return max(g for g in range(1, most + 1) if nc % g == 0)


def _chunk_at(g, d, nc, ncc):
    pos = _visit_pos(g, d, nc, ncc)
    return pos, pl.ds(pl.multiple_of(pos * CH, CH), CH)


def _intra_fwd(q, k, v, bl, L, exch):
    T = q.shape[1]
    G = _chunks_per_step(T, INTRA_FWD_CHUNKS)
    nc, qkv_s, bl_s, big, pm, gl_s, shapes = _intra_specs(T, G)
    assert G == nc
    ncc = L // CH

    def body(q_ref, k_ref, v_ref, bl_ref, u_ref, w_ref, kd_ref, qd_ref, p_ref, gl_ref, x_ref):
        d, h = pl.program_id(0), pl.program_id(1)
        sb, sl = _dir_head_sel(d, h)
        rows = [slice(g * CH, (g + 1) * CH) for g in range(G)]
        outs = _intra_fn(_chunk_masks(d), sb, sl, *[[r[s, :] for s in rows] for r in (q_ref, k_ref, v_ref, bl_ref)])
        for g in range(G):
            pos, at = _chunk_at(g, d, nc, ncc)
            for r, o in zip((u_ref, w_ref, kd_ref, qd_ref, p_ref, x_ref), outs[:5] + outs[6:]):
                r[at, :] = o[g].astype(r.dtype)
            gl_ref[pos] = outs[5][g]

    return _call_carrying(body, exch, name="gdn_intra_fwd", out_shape=shapes, grid=(2, GH, nc // G),
                          in_specs=[qkv_s, qkv_s, qkv_s, bl_s], out_specs=(big, big, big, big, pm, gl_s, pm))(q, k, v, bl)


def _intra_bwd(q, k, v, bl, xinv, cts, L, exch):
    T = q.shape[1]
    G = _chunks_per_step(T, INTRA_BWD_CHUNKS)
    nc, qkv_s, bl_s, big, pm, gl_s, _ = _intra_specs(T, G)
    assert G == nc
    ncc = L // CH

    def body(q_ref, k_ref, v_ref, bl_ref, x_ref, du, dw, dkd, dqd, dp, dgl, dq_ref, dk_ref, dv_ref, dbl_ref):
        d, h = pl.program_id(0), pl.program_id(1)
        sb, sl = _dir_head_sel(d, h)
        rows = [slice(g * CH, (g + 1) * CH) for g in range(G)]
        places = [_chunk_at(g, d, nc, ncc) for g in range(G)]
        fn = functools.partial(_intra_fn, _chunk_masks(d), sb, sl, xs=[x_ref[at, :] for _, at in places])
        _, vjp = jax.vjp(fn, *[[r[s, :] for s in rows] for r in (q_ref, k_ref, v_ref, bl_ref)])
        cts = tuple([r[at, :] for _, at in places] for r in (du, dw, dkd, dqd, dp)) + ([dgl[pos] for pos, _ in places],)
        grads = vjp(cts)
        for g in range(G):
            for r, o in zip((dq_ref, dk_ref, dv_ref, dbl_ref), grads):
                r[rows[g], :] = o[g]

    return _call_carrying(body, exch, name="gdn_intra_bwd", out_shape=(_sds((2, GH, T, HD)),) * 4,
                          grid=(2, GH, nc // G), in_specs=[qkv_s, qkv_s, qkv_s, bl_s, pm, big, big, big, big, pm, gl_s],
                          out_specs=(big,) * 4)(q, k, v, bl, xinv, *cts)


def _scan_fn(s, u, w, kd, qd, p, gl):
    each = lambda f, *ls: [f(*t) for t in zip(*ls)]
    ws = each(_nn, w, s)
    delta = each(lambda a, b: a - b, u, ws)
    kdd = each(_tn, kd, delta)
    s_new = each(lambda g, a, b: g * a + b, gl, s, kdd)
    qs = each(_nn, qd, s)
    pd = each(_nn, p, delta)
    return each(lambda a, b: a + b, qs, pd), s_new


SCAN_BLOCK = 4


def _visit_pos(c, d, nc, ncc):
    back = ncc - 1 - c if c < ncc else ncc + (nc - 1 - c)
    return jnp.where(d == 0, c, back)


def _scan_specs(T, L, back):
    tb = SCAN_BLOCK * CH
    assert T % tb == 0 and L % tb == 0
    nb, ncb = T // tb, L // tb
    at = (lambda t: nb - 1 - t) if back else (lambda t: t)
    big = pl.BlockSpec((2, GH, tb, HD), lambda t: (0, 0, at(t), 0))
    pm = pl.BlockSpec((2, GH, tb, CH), lambda t: (0, 0, at(t), 0))
    gl = pl.BlockSpec((2, GH, SCAN_BLOCK, 1, HD), lambda t: (0, 0, at(t), 0, 0))
    st = pl.BlockSpec((2, GH, SCAN_BLOCK, HD, HD), lambda t: (0, 0, at(t), 0, 0))

    def natural(b):
        return jnp.where(b < ncb, ncb - 1 - b, nb - 1 - (b - ncb))

    do_specs = (pl.BlockSpec((GH, tb, HD), lambda t: (0, at(t), 0)),
                pl.BlockSpec((GH, tb, HD), lambda t: (0, natural(at(t)), 0)))
    return nb, big, pm, gl, st, do_specs


SCAN_STREAMS = [(d, h) for d in (0, 1) for h in range(GH)]


def _scan_fwd(u, w, kd, qd, p, gl, L):
    T = u.shape[2]
    nb, big, pm, gl_s, st, _ = _scan_specs(T, L, False)

    def body(u_ref, w_ref, kd_ref, qd_ref, p_ref, gl_ref, o_ref, st_ref, s_scr):
        @pl.when(pl.program_id(0) == 0)
        def _():
            s_scr[...] = jnp.zeros_like(s_scr)

        s = [s_scr[d, h] for d, h in SCAN_STREAMS]
        for i in range(SCAN_BLOCK):
            rows = slice(i * CH, (i + 1) * CH)
            for (d, h), sv in zip(SCAN_STREAMS, s):
                st_ref[d, h, i] = sv
            o, s = _scan_fn(s, *[[r[d, h, rows, :].astype(F32) for d, h in SCAN_STREAMS]
                                 for r in (u_ref, w_ref, kd_ref, qd_ref, p_ref)],
                            [gl_ref[d, h, i] for d, h in SCAN_STREAMS])
            for (d, h), ov in zip(SCAN_STREAMS, o):
                o_ref[d, h, rows, :] = ov
        for (d, h), sv in zip(SCAN_STREAMS, s):
            s_scr[d, h] = sv

    return _call(body, name="gdn_scan_fwd", out_shape=(_sds((2, GH, T, HD)), _sds((2, GH, T // CH, HD, HD))),
                 grid=(nb,), in_specs=[big, big, big, big, pm, gl_s], out_specs=(big, st),
                 scratch=[pltpu.VMEM((2, GH, HD, HD), F32)], sem=("arbitrary",), vmem=VMEM_BIG)(u, w, kd, qd, p, gl)


def _scan_bwd(u, w, kd, qd, p, gl, states, do, L, exch):
    T = u.shape[2]
    nb, big, pm, gl_s, st, do_specs = _scan_specs(T, L, True)

    def body(u_ref, w_ref, kd_ref, qd_ref, p_ref, gl_ref, st_ref, do0_ref, do1_ref,
             du_ref, dw_ref, dkd_ref, dqd_ref, dp_ref, dgl_ref, ds_scr):
        @pl.when(pl.program_id(0) == 0)
        def _():
            ds_scr[...] = jnp.zeros_like(ds_scr)

        ds = [ds_scr[d, h] for d, h in SCAN_STREAMS]
        for i in reversed(range(SCAN_BLOCK)):
            rows = slice(i * CH, (i + 1) * CH)
            mirror = slice((SCAN_BLOCK - 1 - i) * CH, (SCAN_BLOCK - i) * CH)
            _, vjp = jax.vjp(_scan_fn, [st_ref[d, h, i] for d, h in SCAN_STREAMS],
                             *[[r[d, h, rows, :].astype(F32) for d, h in SCAN_STREAMS]
                               for r in (u_ref, w_ref, kd_ref, qd_ref, p_ref)],
                             [gl_ref[d, h, i] for d, h in SCAN_STREAMS])
            dos = [do0_ref[h, rows, :] if d == 0 else do1_ref[h, mirror, :] for d, h in SCAN_STREAMS]
            ds, gu, gw, gkd, gqd, gp, ggl = vjp((dos, ds))
            for n, (d, h) in enumerate(SCAN_STREAMS):
                du_ref[d, h, rows, :] = gu[n]
                dw_ref[d, h, rows, :] = gw[n]
                dkd_ref[d, h, rows, :] = gkd[n]
                dqd_ref[d, h, rows, :] = gqd[n]
                dp_ref[d, h, rows, :] = gp[n]
                dgl_ref[d, h, i] = ggl[n]
        for (d, h), dv in zip(SCAN_STREAMS, ds):
            ds_scr[d, h] = dv

    return _call_carrying(
        body, exch, name="gdn_scan_bwd",
        out_shape=(_sds((2, GH, T, HD)),) * 4 + (_sds((2, GH, T, CH)), _sds((2, GH, T // CH, 1, HD))),
        grid=(nb,), in_specs=[big, big, big, big, pm, gl_s, st, *do_specs], out_specs=(big, big, big, big, pm, gl_s),
        scratch=[pltpu.VMEM((2, GH, HD, HD), F32)], vmem=VMEM_BIG)(u, w, kd, qd, p, gl, states, do, do)


def _gout_fn(o0, o1, z, gw):
    return _rms(o0 + o1) * gw * _silu(z)


def _backward_latent(o_ref, L):
    nl = (o_ref.shape[1] - L) // CH
    return jnp.concatenate([o_ref[1, L + (nl - 1 - j) * CH:L + (nl - j) * CH, :] for j in range(nl)], axis=0)


def _gout_fwd(o, proj, gw, L):
    T = o.shape[2]
    N = T - L
    ob = pl.BlockSpec((2, None, T, HD), lambda h: (0, h, 0, 0))

    def body(o_ref, z_ref, gw_ref, y_ref):
        y_ref[...] = _gout_fn(o_ref[0, L:, :], _backward_latent(o_ref, L), z_ref[L:, :], gw_ref[...]).astype(BF16)

    return _call(body, name="gout_fwd", out_shape=_sds((N, GH * HD), BF16), grid=(GH,),
                 in_specs=[ob, pl.BlockSpec((T, HD), lambda h: (0, C_Z // HD + h)), pl.BlockSpec((1, HD), lambda h: (0, 0))],
                 out_specs=pl.BlockSpec((N, HD), lambda h: (0, h)), sem=("parallel",))(o, proj, gw)


def _gout_bwd(o, proj, gw, dy, dproj, L):
    T = o.shape[2]
    N = T - L
    ob = pl.BlockSpec((2, None, T, HD), lambda h: (0, h, 0, 0))

    def body(o_ref, z_ref, gw_ref, dy_ref, _, do_ref, dz_ref, dgw_ref):
        _, vjp = jax.vjp(_gout_fn, o_ref[0, L:, :], _backward_latent(o_ref, L), z_ref[L:, :], gw_ref[...])
        g0, _, gz, ggw = vjp(dy_ref[...])
        do_ref[:L, :] = jnp.zeros((L, HD), F32)
        do_ref[L:, :] = g0
        dz_ref[:L, :] = jnp.zeros((L, HD), BF16)
        dz_ref[L:, :] = gz.astype(BF16)

        @pl.when(pl.program_id(0) == 0)
        def _():
            dgw_ref[...] = jnp.zeros_like(dgw_ref)

        dgw_ref[...] += ggw

    zb = pl.BlockSpec((T, HD), lambda h: (0, C_Z // HD + h))
    return _call(body, name="gout_bwd", out_shape=(_sds((GH, T, HD)), _sds(dproj.shape, BF16), _sds((1, HD))),
                 grid=(GH,),
                 in_specs=[ob, zb, pl.BlockSpec((1, HD), lambda h: (0, 0)), pl.BlockSpec((N, HD), lambda h: (0, h)), ANYSPEC],
                 out_specs=(pl.BlockSpec((None, T, HD), lambda h: (h, 0, 0)), zb, pl.BlockSpec((1, HD), lambda h: (0, 0))),
                 aliases={4: 1}, sem=("arbitrary",))(o, proj, gw, dy, dproj)


def _merge_fn(pa, pd, ga, gd):
    return jax.nn.sigmoid(ga) * pa + jax.nn.sigmoid(gd) * pd


def _merge_fwd(pa, pd, proj, L, *, br=256):
    N = pa.shape[0]
    lb = L // br
    row = pl.BlockSpec((br, D), lambda i: (i, 0))

    def body(pa_ref, pd_ref, ga_ref, gd_ref, y_ref):
        y_ref[...] = _merge_fn(pa_ref[...], pd_ref[...], ga_ref[...], gd_ref[...]).astype(BF16)

    return _call(body, name="merge_fwd", out_shape=_sds((N, D), BF16), grid=(N // br,),
                 in_specs=[row, row, pl.BlockSpec((br, D), lambda i: (i + lb, C_GATE // D)),
                           pl.BlockSpec((br, D), lambda i: (i + lb, C_GATE // D + 1))],
                 out_specs=row, sem=("parallel",))(pa, pd, proj, proj)


def _merge_bwd(pa, pd, proj, dy, L, *, br=256):
    N = pa.shape[0]
    T = N + L
    lb = L // br
    lrow = pl.BlockSpec((br, D), lambda i: (jnp.maximum(i - lb, 0), 0))

    def body(pa_ref, pd_ref, ga_ref, gd_ref, dy_ref, dpa_ref, dpd_ref, dg_ref):
        lat = pl.program_id(0) >= lb
        _, vjp = jax.vjp(_merge_fn, pa_ref[...], pd_ref[...], ga_ref[...], gd_ref[...])
        gpa, gpd, gga, ggd = vjp(dy_ref[...])
        dpa_ref[...] = gpa.astype(BF16)
        dpd_ref[...] = gpd.astype(BF16)
        dg_ref[:, :D] = jnp.where(lat, gga, 0.0).astype(BF16)
        dg_ref[:, D:] = jnp.where(lat, ggd, 0.0).astype(BF16)

    return _call(body, name="merge_bwd", out_shape=(_sds((N, D), BF16), _sds((N, D), BF16), _sds((T, C_END), BF16)),
                 grid=(T // br,),
                 in_specs=[lrow, lrow, pl.BlockSpec((br, D), lambda i: (i, C_GATE // D)),
                           pl.BlockSpec((br, D), lambda i: (i, C_GATE // D + 1)), lrow],
                 out_specs=(lrow, lrow, pl.BlockSpec((br, 2 * D), lambda i: (i, C_GATE // (2 * D)))),
                 sem=("arbitrary",))(pa, pd, proj, proj, dy)


def _resid_fwd(x, m, mod, i_g, *, name, br=256):
    R = x.shape[0]
    row = pl.BlockSpec((br, D), lambda i: (i, 0))

    def body(x_ref, m_ref, mod_ref, o_ref):
        o_ref[...] = x_ref[...] + mod_ref[i_g:i_g + 1, :] * m_ref[...]

    return _call(body, name=name, out_shape=_sds((R, D)), grid=(R // br,),
                 in_specs=[row, row, pl.BlockSpec((6, D), lambda i: (0, 0))], out_specs=row,
                 sem=("parallel",))(x, m, mod)


def _resid_bwd(dx, m, mod, i_g, *, name, br=256):
    R = dx.shape[0]
    row = pl.BlockSpec((br, D), lambda i: (i, 0))
    vec = pl.BlockSpec((1, D), lambda i: (0, 0))

    def body(dx_ref, m_ref, mod_ref, dm_ref, dg_ref):
        dxv = dx_ref[...]
        dm_ref[...] = (dxv * mod_ref[i_g:i_g + 1, :]).astype(BF16)

        @pl.when(pl.program_id(0) == 0)
        def _():
            dg_ref[...] = jnp.zeros_like(dg_ref)

        dg_ref[...] += jnp.sum(dxv * m_ref[...], axis=0, keepdims=True)

    return _call(body, name=name, out_shape=(_sds((R, D), BF16), _sds((1, D))), grid=(R // br,),
                 in_specs=[row, row, pl.BlockSpec((6, D), lambda i: (0, 0))], out_specs=(row, vec),
                 sem=("arbitrary",))(dx, m, mod)


def _ffn_fn(shifts, ug, uv, wg, wv, bg, bv):
    down, up = shifts

    def conv(x, w, b):
        return down(x) * w[0:1, :] + x * w[1:2, :] + up(x) * w[2:3, :] + b

    return _silu(conv(ug, wg, bg)) * conv(uv, wv, bv)


def _ffn_fwd(up, cw, cb, *, bw=256):
    N = up.shape[0]
    shifts = _make_shift(((0, N),))
    nb = DFF // bw

    def body(ug, uv, wg, wv, bg, bv, a_ref):
        a_ref[...] = _ffn_fn(shifts, ug[...], uv[...], wg[...], wv[...], bg[...], bv[...]).astype(BF16)

    def col(rows, off):
        return pl.BlockSpec((rows, bw), lambda j: (0, j + off))

    return _call(body, name="ffn_fwd", out_shape=_sds((N, DFF), BF16), grid=(nb,),
                 in_specs=[col(N, 0), col(N, nb), col(3, 0), col(3, nb), col(1, 0), col(1, nb)],
                 out_specs=col(N, 0), sem=("parallel",), vmem=VMEM_BIG)(up, up, cw, cw, cb, cb)


def _ffn_bwd(up, cw, cb, da, *, bw=256):
    N = up.shape[0]
    shifts = _make_shift(((0, N),))
    nb = DFF // bw

    def body(ug, uv, wg, wv, bg, bv, da_ref, dug, duv, dwg, dwv, dbg, dbv):
        _, vjp = jax.vjp(functools.partial(_ffn_fn, shifts), ug[...], uv[...], wg[...], wv[...], bg[...], bv[...])
        g = vjp(da_ref[...])
        dug[...] = g[0].astype(BF16)
        duv[...] = g[1].astype(BF16)
        dwg[...], dwv[...], dbg[...], dbv[...] = g[2], g[3], g[4], g[5]

    def col(rows, off):
        return pl.BlockSpec((rows, bw), lambda j: (0, j + off))

    half = (_sds((N, DFF), BF16), _sds((N, DFF), BF16), _sds((3, DFF)), _sds((3, DFF)), _sds((1, DFF)), _sds((1, DFF)))
    dug, duv, dwg, dwv, dbg, dbv = _call(
        body, name="ffn_bwd", out_shape=half, grid=(nb,),
        in_specs=[col(N, 0), col(N, nb), col(3, 0), col(3, nb), col(1, 0), col(1, nb), col(N, 0)],
        out_specs=(col(N, 0), col(N, 0), col(3, 0), col(3, 0), col(1, 0), col(1, 0)),
        sem=("parallel",), vmem=VMEM_BIG)(up, up, cw, cw, cb, cb, da)
    return (jnp.concatenate([dug, duv], axis=1), jnp.concatenate([dwg, dwv], axis=1),
            jnp.concatenate([dbg, dbv], axis=1))


def _head_fn(x1, dn, g2, fw, tgt):
    y = _rms(x1 + g2 * dn) * fw
    err = y - tgt
    return 0.5 * jnp.sum(jnp.mean(err * err, axis=-1))


def _head(x1, dn, mod, fw, tgt, *, br=256):
    N = x1.shape[0]
    row = pl.BlockSpec((br, D), lambda i: (i, 0))
    vec = pl.BlockSpec((1, D), lambda i: (0, 0))
    one = pl.BlockSpec((1, HD), lambda i: (0, 0))

    def body(x1_ref, dn_ref, mod_ref, fw_ref, tgt_ref, loss_ref, dx_ref, ddn_ref, dg_ref, dfw_ref):
        loss, (gx, gdn, gg, gfw) = jax.value_and_grad(_head_fn, argnums=(0, 1, 2, 3))(
            x1_ref[...], dn_ref[...], mod_ref[5:6, :], fw_ref[...], tgt_ref[...])
        dx_ref[...] = gx
        ddn_ref[...] = gdn.astype(BF16)

        @pl.when(pl.program_id(0) == 0)
        def _():
            loss_ref[...] = jnp.zeros_like(loss_ref)
            dg_ref[...] = jnp.zeros_like(dg_ref)
            dfw_ref[...] = jnp.zeros_like(dfw_ref)

        loss_ref[...] += jnp.broadcast_to(loss, (1, HD))
        dg_ref[...] += gg
        dfw_ref[...] += gfw

    return _call(body, name="head", out_shape=(_sds((1, HD)), _sds((N, D)), _sds((N, D), BF16), _sds((1, D)), _sds((1, D))),
                 grid=(N // br,), in_specs=[row, row, pl.BlockSpec((6, D), lambda i: (0, 0)), vec, row],
                 out_specs=(one, row, row, vec, vec), sem=("arbitrary",))(x1, dn, mod, fw, tgt)


def _adamw(w, g, m, v, *, name):
    shape = w.shape
    cols = shape[-1]
    rows = max(1, math.prod(shape[:-1]))
    w2, g2, m2, v2 = (t.reshape(rows, cols) for t in (w, g, m, v))
    br = 256 if rows % 256 == 0 else rows
    c1 = 1.0 - B1 ** STEP
    c2 = 1.0 - B2 ** STEP

    def body(w_ref, g_ref, m_ref, v_ref, d_ref, nm_ref, nv_ref):
        gv = g_ref[...]
        nm = B1 * m_ref[...] + (1.0 - B1) * gv
        nv = B2 * v_ref[...] + (1.0 - B2) * (gv * gv)
        d_ref[...] = -LR * ((nm / c1) / (jnp.sqrt(nv / c2) + AEPS) + WD * w_ref[...])
        nm_ref[...] = nm
        nv_ref[...] = nv

    blk = pl.BlockSpec((br, cols), lambda i: (i, 0))
    outs = _call(body, name=name, out_shape=(_sds((rows, cols)),) * 3, grid=(rows // br,),
                 in_specs=[blk] * 4, out_specs=(blk,) * 3, sem=("parallel",))(w2, g2, m2, v2)
    return tuple(t.reshape(shape) for t in outs)


def _adamw_many(items, *, name):
    k = len(items)
    shapes = [w.shape for w, _, _, _ in items]
    flat = [t.reshape(max(1, math.prod(t.shape[:-1])), t.shape[-1]) for it in items for t in it]
    c1 = 1.0 - B1 ** STEP
    c2 = 1.0 - B2 ** STEP

    def body(*refs):
        ins, outs = refs[:4 * k], refs[4 * k:]
        for i in range(k):
            w_ref, g_ref, m_ref, v_ref = ins[4 * i:4 * i + 4]
            gv = g_ref[...]
            nm = B1 * m_ref[...] + (1.0 - B1) * gv
            nv = B2 * v_ref[...] + (1.0 - B2) * (gv * gv)
            outs[3 * i][...] = -LR * ((nm / c1) / (jnp.sqrt(nv / c2) + AEPS) + WD * w_ref[...])
            outs[3 * i + 1][...] = nm
            outs[3 * i + 2][...] = nv

    res = _call(body, name=name, out_shape=tuple(_sds(flat[4 * i].shape) for i in range(k) for _ in range(3)))(*flat)
    return [tuple(res[3 * i + j].reshape(shapes[i]) for j in range(3)) for i in range(k)]


def _rope_tables(N, L):
    t = jnp.arange(N)
    pos = jnp.stack([(t // GRID_W).astype(F32), (t % GRID_W).astype(F32)], axis=1)
    inv = ROPE_THETA ** (-jnp.arange(0, HD // 2, 2, dtype=F32) / (HD // 2))
    ang = pos[:, :, None] * inv[None, None, :]
    cos = jnp.broadcast_to(jnp.cos(ang)[:, :, None, :], (N, 2, 2, HD // 4)).reshape(N, HD)
    sin = jnp.broadcast_to(jnp.sin(ang)[:, :, None, :], (N, 2, 2, HD // 4))
    sin = (sin * jnp.array([-1.0, 1.0], F32)[None, None, :, None]).reshape(N, HD)
    cos = jnp.concatenate([jnp.ones((L, HD), F32), cos], axis=0)
    sin = jnp.concatenate([jnp.zeros((L, HD), F32), sin], axis=0)
    return cos, sin


def _pad_lanes(v, off=0):
    return jnp.zeros((1, HD), F32).at[0, off:off + v.shape[0]].set(v)


def _local_step(x, ctx, tgt, mod_lat, mod_ctx, w_in, shards, small):
    N, L = x.shape[0], ctx.shape[0]
    T = N + L
    bounds = ((0, L), (L, T))
    qw, kw, gw = small["q_norm_w"], small["k_norm_w"], small["gdn_norm_w"]
    conv_w, ffn_w, ffn_b, fnw = small["conv_qkv_w"], small["ffn_conv_w"], small["ffn_conv_b"], small["final_norm_w"]
    alog = _pad_lanes(small["a_log"].reshape(-1), 2 * GH)
    dtb = _pad_lanes(small["dt_bias"].reshape(-1), 2 * GH)
    cos, sin = _rope_tables(N, L)
    bt = T
    bnl = 256 if N % 1024 else 1024

    hc = _normmod_fwd(ctx, mod_ctx, 0, 1, name="normmod_ctx")
    hx = _normmod_fwd(x, mod_lat, 0, 1, name="normmod_x")
    h1 = jnp.concatenate([hc, hx], axis=0)
    proj = _mm(h1, w_in, name="mm_in", M=T, N=C_END, K=D, tb=True, bm=bt, bn=1024)
    aq, ak, av = _aprep_fwd(proj, cos, sin, qw, kw)
    (attn, attn32, lse), (up_g,) = _attn_fwd(aq, ak, av, L, _GatherTwoLevel([shards["w_up"]]))
    gq = _gprep_fwd(proj, conv_w, 0, bounds)
    gk = _gprep_fwd(proj, conv_w, 1, bounds)
    gv = _gprep_fwd(proj, conv_w, 2, bounds)
    bl = _bl_fwd(proj, alog, dtb)
    intra, (down_g, pa_g, pd_g, out_g) = _intra_fwd(
        gq, gk, gv, bl, L, _GatherTwoLevel([shards[n] for n in ("w_down", "w_pa", "w_pd", "w_out")]))
    w_up, w_down = up_g.reshape(2 * DFF, D), down_g.reshape(DFF, D)
    w_pa, w_pd, w_out = pa_g.reshape(D, D), pd_g.reshape(D, D), out_g.reshape(D, D)
    xinv, intra = intra[6], intra[:6]
    o, states = _scan_fwd(*intra, L)
    gdn = _gout_fwd(o, proj, gw, L)
    pa = _mm(attn, w_pa, name="mm_pa", M=N, N=D, K=D, bm=bnl)
    pd = _mm(gdn, w_pd, name="mm_pd", M=N, N=D, K=D, bm=bnl)
    y = _merge_fwd(pa, pd, proj, L)
    m = _mm(y, w_out, name="mm_out", M=N, N=D, K=D, bm=bnl)
    x1 = _resid_fwd(x, m, mod_lat, 2, name="resid1")
    h2 = _normmod_fwd(x1, mod_lat, 3, 4, name="normmod_x1")
    up = _mm(h2, w_up, name="mm_up", M=N, N=2 * DFF, K=D, tb=True, bm=bnl, bn=2 * DFF // 4)
    a = _ffn_fwd(up, ffn_w, ffn_b)
    dn = _mm(a, w_down, name="mm_down", M=N, N=D, K=DFF, bm=bnl)
    loss, dx2, ddn, dg2, dfnw = _head(x1, dn, mod_lat, fnw, tgt)

    da = _mm(ddn, w_down, name="mm_down_dx", M=N, N=DFF, K=D, tb=True, bm=bnl, bn=DFF // 2)
    g_down = _mm(a, ddn, name="mm_down_dw", M=DFF, N=D, K=N, ta=True, bm=DFF // 2, out_dtype=BF16)
    dup, d_ffn_w, d_ffn_b = _ffn_bwd(up, ffn_w, ffn_b, da)
    dh2 = _mm(dup, w_up, name="mm_up_dx", M=N, N=D, K=2 * DFF, bm=bnl, bk=2 * DFF // 4)
    g_up = _mm(dup, h2, name="mm_up_dw", M=2 * DFF, N=D, K=N, ta=True, bm=2 * DFF // 4, out_dtype=BF16)
    dx1, dsh2, dsc2 = _normmod_bwd(x1, mod_lat, 3, 4, dh2, 0, dx2, name="normmod_x1_bwd")
    dm, dg1 = _resid_bwd(dx1, m, mod_lat, 2, name="resid1_bwd")
    dy = _mm(dm, w_out, name="mm_out_dx", M=N, N=D, K=D, tb=True, bm=bnl)
    g_out = _mm(y, dm, name="mm_out_dw", M=D, N=D, K=N, ta=True, out_dtype=BF16)
    dpa, dpd, dproj = _merge_bwd(pa, pd, proj, dy, L)
    dattn = _mm(dpa, w_pa, name="mm_pa_dx", M=N, N=D, K=D, tb=True, bm=bnl)
    g_pa = _mm(attn, dpa, name="mm_pa_dw", M=D, N=D, K=N, ta=True, out_dtype=BF16)
    dgdn = _mm(dpd, w_pd, name="mm_pd_dx", M=N, N=D, K=D, tb=True, bm=bnl)
    g_pd = _mm(gdn, dpd, name="mm_pd_dw", M=D, N=D, K=N, ta=True, out_dtype=BF16)
    do, dproj, dgw = _gout_bwd(o, proj, gw, dgdn, dproj, L)
    cts, recv_a = _scan_bwd(*intra, states, do, L, _Exchange(
        [g_out.reshape(NDEV, D // NDEV, D), g_pa.reshape(NDEV, D // NDEV, D), g_pd.reshape(NDEV, D // NDEV, D)], True))
    (dgq, dgk, dgv, dbl), recv_b = _intra_bwd(gq, gk, gv, bl, xinv, cts, L, _Exchange(
        [g_up.reshape(NDEV, 2 * DFF // NDEV, D)], True))
    dproj, dwq = _gprep_bwd(proj, conv_w, 0, bounds, dgq, dproj)
    dproj, dwk = _gprep_bwd(proj, conv_w, 1, bounds, dgk, dproj)
    dproj, dwv = _gprep_bwd(proj, conv_w, 2, bounds, dgv, dproj)
    dproj, dalog, ddtb = _bl_bwd(proj, alog, dtb, dbl, dproj)
    (daq_h, dak_h, dav_h), recv_c = _attn_bwd(aq, ak, av, attn32, lse, dattn, L, _Exchange(
        [g_down.reshape(NDEV, DFF // NDEV, D)], True))
    recv = dict(zip(("w_out", "w_pa", "w_pd", "w_up", "w_down"), recv_a + recv_b + recv_c))
    dproj, dqw, dkw = _aprep_bwd(proj, cos, sin, qw, kw, daq_h, dak_h, dav_h, dproj, L)
    g_in = _mm(dproj, h1, name="mm_in_dw", M=C_END, N=D, K=T, ta=True, bm=1024, out_dtype=BF16)
    *pending, token = _scatter_start(g_in, None, (0, D // 2), (), name="scatter_g_in_a_start")
    dh1 = _mm(dproj, w_in, name="mm_in_dx", M=T, N=D, K=C_END, bm=bt, bk=1024, after=(token,))
    grad_x, dsh1, dsc1 = _normmod_bwd(x, mod_lat, 0, 1, dh1, L, dx1, name="normmod_x_bwd")
    _, dcsh1, dcsc1 = _normmod_bwd(ctx, mod_ctx, 0, 1, dh1, 0, None, name="normmod_ctx_bwd")

    z1 = jnp.zeros((1, D), F32)
    dmod_lat = jnp.concatenate([dsh1, dsc1, dg1, dsh2, dsc2, dg2], axis=0)
    dmod_ctx = jnp.concatenate([dcsh1, dcsc1, z1, z1, z1, z1], axis=0)
    gsmall = {
        "q_norm_w": dqw, "k_norm_w": dkw, "gdn_norm_w": dgw,
        "conv_qkv_w": jnp.concatenate([dwq, dwk, dwv], axis=1),
        "a_log": dalog[0, 2 * GH:4 * GH], "dt_bias": ddtb[0, 2 * GH:4 * GH],
        "ffn_conv_w": d_ffn_w, "ffn_conv_b": d_ffn_b, "final_norm_w": dfnw,
    }
    return loss[0, 0], grad_x, pending, recv, dmod_lat, dmod_ctx, gsmall


HBM = pl.BlockSpec(memory_space=pltpu.HBM)
ANYSPEC = pl.BlockSpec(memory_space=pl.ANY)


def _position():
    x, y, c = lax.axis_index("x"), lax.axis_index("y"), lax.axis_index("c")
    return x, y, c, 4 * x + 2 * y + c


def _peer(x, y, c, k):
    px = 1 - x if k & 4 else x
    py = 1 - y if k & 2 else y
    pc = 1 - c if k & 1 else c
    return (px, py, pc), 4 * px + 2 * py + pc


def _exchange(arrs, *, name, scatter):
    exch = _Exchange(arrs, scatter)
    n = exch.n

    def body(*refs):
        ins, outs, sems = refs[:n], refs[n:2 * n], refs[2 * n:]
        exch.start(ins, outs, sems)
        exch.finish(ins, outs, sems)

    outs = pl.pallas_call(body, name=name, out_shape=exch.out_shape, in_specs=[HBM] * n, out_specs=(HBM,) * n,
                          scratch_shapes=exch.scratch,
                          compiler_params=pltpu.CompilerParams(has_side_effects=True))(*arrs)
    return list(outs)


class _Exchange:
    def __init__(self, arrs, scatter):
        self.arrs, self.scatter, self.n = list(arrs), scatter, len(arrs)
        self.out_shape = tuple(_sds(a.shape if scatter else (NDEV,) + a.shape, a.dtype) for a in arrs)
        self.scratch = [pltpu.SemaphoreType.DMA((self.n, NDEV - 1)), pltpu.SemaphoreType.DMA((self.n, NDEV - 1)),
                        pltpu.SemaphoreType.DMA((self.n,))]

    def _copies(self, ins, outs, sems):
        send, recv, loc = sems
        x, y, c, me = _position()
        local = [pltpu.make_async_copy(ins[a].at[me] if self.scatter else ins[a], outs[a].at[me], loc.at[a])
                 for a in range(self.n)]
        remote = []
        for k in range(1, NDEV):
            peer, pid = _peer(x, y, c, k)
            for a in range(self.n):
                src = ins[a].at[pid] if self.scatter else ins[a]
                remote.append(pltpu.make_async_remote_copy(
                    src_ref=src, dst_ref=outs[a].at[me], send_sem=send.at[a, k - 1], recv_sem=recv.at[a, k - 1],
                    device_id=peer, device_id_type=MESH))
        return local, remote

    def start(self, ins, outs, sems):
        local, remote = self._copies(ins, outs, sems)
        for cp in local + remote:
            cp.start()

    def finish(self, ins, outs, sems):
        local, remote = self._copies(ins, outs, sems)
        for cp in remote:
            cp.wait()
        for cp in local:
            cp.wait()


class _GatherTwoLevel:
    scatter = False

    def __init__(self, arrs):
        self.arrs, self.n = list(arrs), len(arrs)
        self.out_shape = tuple(_sds((NDEV,) + a.shape, a.dtype) for a in arrs)
        self.scratch = [pltpu.SemaphoreType.DMA((self.n, NDEV - 1)), pltpu.SemaphoreType.DMA((self.n, NDEV - 1)),
                        pltpu.SemaphoreType.DMA((self.n,))]

    def _parts(self, ins, outs, sems):
        send, recv, loc = sems
        x, y, c, _ = _position()
        me, sibling = (x, y, c), (x, y, 1 - c)
        chips = [(1 - x, y), (x, 1 - y), (1 - x, 1 - y)]
        parts = []
        for a in range(self.n):
            slot = lambda px, py, pc, a=a: outs[a].at[4 * px + 2 * py + pc]

            def copy(k, owner, to, src=None, a=a, slot=slot):
                return pltpu.make_async_remote_copy(
                    src_ref=slot(*owner) if src is None else src, dst_ref=slot(*owner), send_sem=send.at[a, k],
                    recv_sem=recv.at[a, k], device_id=to, device_id_type=MESH)

            parts.append(dict(
                mine=pltpu.make_async_copy(ins[a], slot(*me), loc.at[a]),
                first=[copy(0, me, sibling, src=ins[a])] + [copy(1 + j, me, (*ch, c), src=ins[a]) for j, ch in enumerate(chips)],
                arrive=[copy(1 + j, (*ch, c), me) for j, ch in enumerate(chips)],
                passed=[copy(4 + j, (*ch, c), sibling) for j, ch in enumerate(chips)],
                rest=[copy(0, sibling, me)] + [copy(4 + j, (*ch, 1 - c), me) for j, ch in enumerate(chips)]))
        return parts

    def start(self, ins, outs, sems):
        for p in self._parts(ins, outs, sems):
            p["mine"].start()
            for cp in p["first"]:
                cp.start()

    def middle(self, ins, outs, sems):
        for p in self._parts(ins, outs, sems):
            for got, fwd in zip(p["arrive"], p["passed"]):
                got.wait_recv()
                fwd.start()

    def finish(self, ins, outs, sems):
        for p in self._parts(ins, outs, sems):
            for cp in p["rest"]:
                cp.wait_recv()
            for cp in p["first"] + p["passed"]:
                cp.wait_send()
            p["mine"].wait()


def _gather_two_level(blocks, *, name):
    exch = _GatherTwoLevel(blocks)
    n = exch.n

    def body(*refs):
        ins, outs, sems = refs[:n], refs[n:2 * n], refs[2 * n:]
        exch.start(ins, outs, sems)
        exch.middle(ins, outs, sems)
        exch.finish(ins, outs, sems)

    outs = pl.pallas_call(body, name=name, out_shape=exch.out_shape, in_specs=[HBM] * n, out_specs=(HBM,) * n,
                          scratch_shapes=exch.scratch,
                          compiler_params=pltpu.CompilerParams(has_side_effects=True))(*blocks)
    return list(outs)


SEM = pl.BlockSpec(memory_space=pltpu.SEMAPHORE)


SHARD_ROWS = W_END // NDEV
RUNS = ((0, W_QKV, C_KV), (W_QKV, W_AQ - W_QKV, C_QKV), (W_AQ, W_Z - W_AQ, C_AQ), (W_Z, W_END - W_Z, C_Z))


ROW_TILE = 8
SLOT_ROWS = -(-SHARD_ROWS // ROW_TILE) * ROW_TILE


def _shard_pieces(d):
    lo, hi = d * SHARD_ROWS, (d + 1) * SHARD_ROWS
    lead = lo % ROW_TILE
    pieces = []
    for first, rows, padded in RUNS:
        a, b = max(lo, first), min(hi, first + rows)
        if a < b:
            pieces.append([a - lo + lead, b - a, padded + a - first])
    pieces[0] = [0, pieces[0][1] + lead, pieces[0][2] - lead]
    pieces[-1][1] = SLOT_ROWS - pieces[-1][0]
    assert all(v % ROW_TILE == 0 for p in pieces for v in p) and all(p[2] + p[1] <= C_END for p in pieces)
    return pieces


def _pad_rows(w):
    gap = C_QKV + W_AQ - W_QKV
    assert w.shape == (W_END, D) and (gap, C_Z) == (RUNS[1][2] + RUNS[1][1], RUNS[3][2])

    def body(w_ref, o_ref, zeros, sems):
        zeros[...] = jnp.zeros_like(zeros)
        copies = [pltpu.make_async_copy(w_ref.at[pl.ds(first, rows)], o_ref.at[pl.ds(padded, rows)], sems.at[i])
                  for i, (first, rows, padded) in enumerate(RUNS)]
        copies.append(pltpu.make_async_copy(zeros, o_ref.at[pl.ds(gap, C_Z - gap)], sems.at[len(RUNS)]))
        for cp in copies:
            cp.start()
        for cp in copies:
            cp.wait()

    return _call(body, name="pad_w_in", out_shape=_sds((C_END, D), w.dtype), in_specs=[HBM], out_specs=HBM,
                 scratch=[pltpu.VMEM((C_Z - gap, D), w.dtype), pltpu.SemaphoreType.DMA((len(RUNS) + 1,))])(w)


def _scatter_send(src_ref, land_ref, send_sems, recv_sems, cols):
    _, _, _, me = _position()
    for d in range(NDEV):
        @pl.when(me != d)
        def _():
            k = jnp.bitwise_xor(me, d)
            peer = tuple(jnp.int32((d >> s) & 1) for s in (2, 1, 0))
            for off, rows, padded in _shard_pieces(d):
                pltpu.make_async_remote_copy(
                    src_ref=src_ref.at[pl.ds(padded, rows), pl.ds(*cols)],
                    dst_ref=land_ref.at[me].at[pl.ds(off, rows), pl.ds(*cols)], send_sem=send_sems.at[k - 1],
                    recv_sem=recv_sems.at[k - 1], device_id=peer, device_id_type=MESH).start()


def _scatter_whole(src_ref, land_ref, send_sems, recv_sems, cols):
    x, y, c, me = _position()
    span = (slice(None), pl.ds(*cols))
    copies = []
    for k in range(1, NDEV):
        peer, _ = _peer(x, y, c, k)
        copies.append(pltpu.make_async_remote_copy(
            src_ref=src_ref.at[pl.ds(0, SLOT_ROWS)].at[span], dst_ref=land_ref.at[me].at[span],
            send_sem=send_sems.at[k - 1], recv_sem=recv_sems.at[k - 1], device_id=peer, device_id_type=MESH))
    return copies


SPLIT_EFFECT = pltpu.SideEffectType.DATAFLOW_SIDE_EFFECTING


def _scatter_start(parts, land, cols, after, *, name):
    na = len(after)
    if land is None:
        land = lax.empty((NDEV, SLOT_ROWS, D), parts.dtype)

    def body(src_ref, land_ref, *rest):
        send_sems, recv_sems, _, _, token = rest[na:]
        _scatter_send(src_ref, land_ref, send_sems, recv_sems, cols)
        token[...] = jnp.zeros_like(token)

    return pl.pallas_call(
        body, name=name,
        out_shape=(pltpu.SemaphoreType.DMA((NDEV - 1,)), pltpu.SemaphoreType.DMA((NDEV - 1,)),
                   pltpu.HBM(parts.shape, parts.dtype), pltpu.HBM(land.shape, land.dtype), _sds((8, HD))),
        in_specs=(HBM, HBM) + (pl.BlockSpec(memory_space=pl.ANY),) * na,
        out_specs=(SEM, SEM, HBM, HBM, pl.BlockSpec(memory_space=pltpu.VMEM)),
        input_output_aliases={0: 2, 1: 3}, compiler_params=pltpu.CompilerParams(has_side_effects=SPLIT_EFFECT),
    )(pltpu.with_memory_space_constraint(parts, pltpu.HBM), pltpu.with_memory_space_constraint(land, pltpu.HBM), *after)


def _scatter_wait(send_sems, recv_sems, src_thru, land_thru, cols, after, *, name):
    na = len(after)

    def body(src_ref, land_ref, send_sems, recv_sems, *rest):
        for cp in _scatter_whole(src_ref, land_ref, send_sems, recv_sems, cols):
            cp.wait_send()
            cp.wait_recv()

    return pl.pallas_call(
        body, name=name,
        out_shape=(pltpu.HBM(src_thru.shape, src_thru.dtype), pltpu.HBM(land_thru.shape, land_thru.dtype)),
        in_specs=(HBM, HBM, SEM, SEM) + (pl.BlockSpec(memory_space=pl.ANY),) * na, out_specs=(HBM, HBM),
        input_output_aliases={0: 0, 1: 1}, compiler_params=pltpu.CompilerParams(has_side_effects=SPLIT_EFFECT),
    )(src_thru, land_thru, send_sems, recv_sems, *after)


def _cast_bf16(ws, *, name):
    k = len(ws)

    def body(*refs):
        for w_ref, o_ref in zip(refs[:k], refs[k:]):
            o_ref[...] = w_ref[...].astype(BF16)

    return _call(body, name=name, out_shape=tuple(_sds(w.shape, BF16) for w in ws), vmem=VMEM_BIG)(*ws)


def _sum_slots(a, *, name):
    _, R, C = a.shape

    def body(a_ref, o_ref):
        s = a_ref[0]
        for d in range(1, NDEV):
            s = s + a_ref[d]
        o_ref[...] = s

    return _call(body, name=name, out_shape=_sds((R, C)))(a)


MODROWS = 16


def _mod_fwd(c9, w, b):
    cols = w.shape[1]

    def body(c_ref, w_ref, b_ref, o_ref):
        o_ref[...] = _nn(_silu(c_ref[...]), w_ref[...]) + b_ref[...]

    return _call(body, name="mod_fwd", out_shape=_sds((MODROWS, cols)))(c9, w, b)


def _mod_bwd(c9, dmy, dall, w):
    cols = w.shape[1]

    def body(c_ref, dmy_ref, dall_ref, w_ref, gw_ref, gb_ref, cp_ref):
        sc = _silu(c_ref[...])
        rows = lax.broadcasted_iota(jnp.int32, (MODROWS, 1), 0)
        d = dmy_ref[...]
        d_ctx = jnp.where(rows == NDEV, d, 0.0)
        sc_ctx = jnp.where(rows == NDEV, sc, 0.0)
        outer = lax.dot_general(sc_ctx, d_ctx, (((0,), (0,)), ((), ())), precision=HI, preferred_element_type=F32)
        gw_ref[...] = _tn(jnp.where(rows < NDEV, sc, 0.0), jnp.where(rows < NDEV, d, 0.0)) + outer
        gb_ref[...] = jnp.sum(dall_ref[...], axis=0, keepdims=True)
        cp_ref[...] = jnp.sum(_nt(d_ctx, w_ref[...]), axis=0, keepdims=True)

    return _call(body, name="mod_bwd", out_shape=(_sds((D, cols)), _sds((1, 6 * D)), _sds((1, D))),
                 vmem=VMEM_BIG)(c9, dmy, dall, w)


def _cctx_finish(parts, c_ctx, after):
    VM = pl.BlockSpec(memory_space=pltpu.VMEM)

    def body(p_ref, c_ref, *rest):
        o_ref = rest[-1]
        s = p_ref[0]
        for d in range(1, NDEV):
            s = s + p_ref[d]
        _, vjp = jax.vjp(_silu, c_ref[...])
        o_ref[...] = vjp(s)[0]

    return _call(body, name="cctx_finish", out_shape=_sds((1, D)),
                 in_specs=[VM, VM] + [pl.BlockSpec(memory_space=pl.ANY)] * len(after))(parts, c_ctx, *after)


def _adamw_recv(w, recv, m, v, *, name, own=None):
    rows, cols = w.shape
    slot_rows = recv.shape[1]
    lead = slot_rows - rows
    assert rows % ROW_TILE in (0, lead)
    bc = 256
    c1 = 1.0 - B1 ** STEP
    c2 = 1.0 - B2 ** STEP
    has_own = own is not None

    def body(w_ref, r_ref, m_ref, v_ref, *rest):
        g_ref, d_ref, nm_ref, nv_ref = rest[-4:]
        me = _position()[3]

        def slot(d):
            return jnp.where(me == d, rest[0][...], r_ref[d]) if has_own else r_ref[d]

        gv = slot(0).astype(F32)
        for d in range(1, NDEV):
            gv = gv + slot(d).astype(F32)
        if lead:
            gv = jnp.where((me * rows) % ROW_TILE == 0, gv[:rows], gv[lead:])
        nm = B1 * m_ref[...] + (1.0 - B1) * gv
        nv = B2 * v_ref[...] + (1.0 - B2) * (gv * gv)
        g_ref[...] = gv
        d_ref[...] = -LR * ((nm / c1) / (jnp.sqrt(nv / c2) + AEPS) + WD * w_ref[...])
        nm_ref[...] = nm
        nv_ref[...] = nv

    blk = pl.BlockSpec((rows, bc), lambda j: (0, j))
    return _call(body, name=name, out_shape=(_sds((rows, cols)),) * 4, grid=(cols // bc,),
                 in_specs=[blk, pl.BlockSpec((NDEV, slot_rows, bc), lambda j: (0, 0, j)), blk, blk]
                 + [pl.BlockSpec((slot_rows, bc), lambda j: (0, j))] * has_own,
                 out_specs=(blk,) * 4, sem=("parallel",), vmem=VMEM_BIG)(w, recv, m, v, *([own] if has_own else []))


P_LAT, P_CTX, P_FNW, P_FFNB, P_CONV, P_FFNW, P_MISC, P_ROWS = 0, 8, 16, 24, 32, 48, 72, 80


def _rows_of(v, nrows):
    flat = v.reshape(-1)
    return jnp.pad(flat, (0, nrows * D - flat.shape[0])).reshape(nrows, D)


def _by_columns(g):
    n, r, c = g.shape
    return jnp.transpose(g, (1, 0, 2)).reshape(r, n * c)


def kernel(x, c, ctx, c_ctx, w_mod, b_mod, w_in, q_norm_w, k_norm_w, conv_qkv_w, a_log, dt_bias, gdn_norm_w, w_pa, w_pd, w_out, w_up, ffn_conv_w, ffn_conv_b, w_down, final_norm_w, loss_target, m_c_ctx, m_w_mod, m_b_mod, m_w_in, m_q_norm_w, m_k_norm_w, m_conv_qkv_w, m_a_log, m_dt_bias, m_gdn_norm_w, m_w_pa, m_w_pd, m_w_out, m_w_up, m_ffn_conv_w, m_ffn_conv_b, m_w_down, m_final_norm_w, v_c_ctx, v_w_mod, v_b_mod, v_w_in, v_q_norm_w, v_k_norm_w, v_conv_qkv_w, v_a_log, v_dt_bias, v_gdn_norm_w, v_w_pa, v_w_pd, v_w_out, v_w_up, v_ffn_conv_w, v_ffn_conv_b, v_w_down, v_final_norm_w):
    _, _, _, me = _position()
    mcols = w_mod.shape[2]

    transposed = ("w_in", "w_up")
    big = {"w_in": w_in[0].T, "w_pa": w_pa[0], "w_pd": w_pd[0], "w_out": w_out[0], "w_up": w_up[0].T, "w_down": w_down[0]}
    names = list(big)
    shards = dict(zip(names, _cast_bf16([big[n] for n in names], name="cast_weights")))
    w_in_g, c_all, conv_g, ffnw_g = _gather_two_level([shards["w_in"], c, conv_qkv_w[0], ffn_conv_w[0]],
                                                      name="gather_w_in")
    w_in_full = w_in_g.reshape(W_END, D)
    w_in_pad = _pad_rows(w_in_full)

    c9 = jnp.concatenate([c_all.reshape(NDEV, D), jnp.pad(c_ctx[None], ((0, MODROWS - NDEV - 1), (0, 0)))], axis=0)
    b_loc = lax.dynamic_slice(b_mod, (0, me * mcols), (1, mcols))
    mod_all, = _exchange([_mod_fwd(c9, w_mod[0], b_loc)], name="gather_mod", scatter=False)
    mod_lat = lax.dynamic_index_in_dim(mod_all, me, axis=1, keepdims=False).reshape(6, D)
    mod_ctx = mod_all[:, NDEV, :].reshape(6, D)

    small = {"q_norm_w": q_norm_w, "k_norm_w": k_norm_w, "gdn_norm_w": gdn_norm_w, "a_log": a_log, "dt_bias": dt_bias,
             "conv_qkv_w": _by_columns(conv_g), "ffn_conv_w": _by_columns(ffnw_g), "ffn_conv_b": ffn_conv_b,
             "final_norm_w": final_norm_w[None]}
    loss_me, grad_x, pending_in, recv, dmod_lat, dmod_ctx, gs = _local_step(
        x[0], ctx[0], loss_target[0], mod_lat, mod_ctx, w_in_pad, shards, small)

    moments = {"w_in": (m_w_in, v_w_in), "w_pa": (m_w_pa, v_w_pa), "w_pd": (m_w_pd, v_w_pd),
               "w_out": (m_w_out, v_w_out), "w_up": (m_w_up, v_w_up), "w_down": (m_w_down, v_w_down)}
    res = {}
    def finish(n, outs):
        return tuple((t.T if n in transposed else t)[None] for t in outs)

    def moment(t, n):
        return t[0].T if n in transposed else t[0]

    for n in recv:
        res[n] = finish(n, _adamw_recv(big[n], recv[n], moment(moments[n][0], n), moment(moments[n][1], n),
                                       name="adamw_" + n))

    misc = jnp.concatenate([gs["q_norm_w"][0], gs["k_norm_w"][0], gs["gdn_norm_w"][0], gs["a_log"], gs["dt_bias"],
                            loss_me[None]])
    pack = jnp.concatenate([_rows_of(dmod_lat, P_CTX - P_LAT), _rows_of(dmod_ctx, P_FNW - P_CTX),
                            _rows_of(gs["final_norm_w"], P_FFNB - P_FNW), _rows_of(gs["ffn_conv_b"], P_CONV - P_FFNB),
                            _rows_of(gs["conv_qkv_w"], P_FFNW - P_CONV), _rows_of(gs["ffn_conv_w"], P_MISC - P_FFNW),
                            _rows_of(misc, P_ROWS - P_MISC)], axis=0)
    pack_all, = _exchange([pack], name="gather_pack", scatter=False)
    tot = _sum_slots(pack_all, name="sum_pack")
    dall = jnp.concatenate([pack_all[:, P_LAT:P_LAT + 6, :].reshape(NDEV, 6 * D),
                            jnp.pad(tot[P_CTX:P_CTX + 6].reshape(1, 6 * D), ((0, MODROWS - NDEV - 1), (0, 0)))], axis=0)
    dmy = lax.dynamic_slice(dall, (0, me * mcols), (MODROWS, mcols))
    g_w_mod, g_b_mod, cpart = _mod_bwd(c9, dmy, dall, w_mod[0])
    cparts, = _exchange([cpart], name="gather_cctx", scatter=False)
    sems_a, land = pending_in[:2], pending_in[3]
    *sems_b, g_in_thru, land, token_b = _scatter_start(pending_in[2], land, (D // 2, D // 2), (cparts,),
                                                       name="scatter_g_in_b_start")
    g_c_ctx = _cctx_finish(cparts, c_ctx[None], (token_b,))[0]

    nconv, nffn = 3 * GH * HD, 2 * DFF
    conv_tot = tot[P_CONV:P_FFNW].reshape(-1)[:3 * nconv].reshape(3, nconv)
    ffnw_tot = tot[P_FFNW:P_MISC].reshape(-1)[:3 * nffn].reshape(3, nffn)
    mrow = tot[P_MISC]
    grads = {
        "c_ctx": g_c_ctx, "w_mod": g_w_mod[None], "b_mod": g_b_mod,
        "q_norm_w": mrow[None, 0:HD], "k_norm_w": mrow[None, HD:2 * HD], "gdn_norm_w": mrow[None, 2 * HD:3 * HD],
        "conv_qkv_w": lax.dynamic_slice(conv_tot, (0, me * (nconv // NDEV)), (3, nconv // NDEV))[None],
        "a_log": mrow[3 * HD:3 * HD + 2 * GH].reshape(1, 2, GH),
        "dt_bias": mrow[3 * HD + 2 * GH:3 * HD + 4 * GH].reshape(1, 2, GH),
        "ffn_conv_w": lax.dynamic_slice(ffnw_tot, (0, me * (nffn // NDEV)), (3, nffn // NDEV))[None],
        "ffn_conv_b": tot[P_FFNB:P_CONV].reshape(-1)[:nffn][None],
        "final_norm_w": tot[P_FNW],
    }
    loss = mrow[3 * HD + 4 * GH]
    given = {"c_ctx": (c_ctx, m_c_ctx, v_c_ctx), "w_mod": (w_mod, m_w_mod, v_w_mod), "b_mod": (b_mod, m_b_mod, v_b_mod),
             "q_norm_w": (q_norm_w, m_q_norm_w, v_q_norm_w), "k_norm_w": (k_norm_w, m_k_norm_w, v_k_norm_w),
             "conv_qkv_w": (conv_qkv_w, m_conv_qkv_w, v_conv_qkv_w), "a_log": (a_log, m_a_log, v_a_log),
             "dt_bias": (dt_bias, m_dt_bias, v_dt_bias), "gdn_norm_w": (gdn_norm_w, m_gdn_norm_w, v_gdn_norm_w),
             "ffn_conv_w": (ffn_conv_w, m_ffn_conv_w, v_ffn_conv_w), "ffn_conv_b": (ffn_conv_b, m_ffn_conv_b, v_ffn_conv_b),
             "final_norm_w": (final_norm_w, m_final_norm_w, v_final_norm_w)}
    res["w_mod"] = (grads["w_mod"],) + _adamw(w_mod, grads["w_mod"], m_w_mod, v_w_mod, name="adamw_w_mod")
    small_names = [n for n in given if n != "w_mod"]
    updates = _adamw_many([(given[n][0], grads[n], given[n][1], given[n][2]) for n in small_names], name="adamw_small")
    for n, upd in zip(small_names, updates):
        res[n] = (grads[n],) + upd

    first = me * SHARD_ROWS
    own_in = lax.dynamic_slice(_unpad_columns(g_in_thru), (first - first % ROW_TILE, 0), (SLOT_ROWS, D))
    mine = (big["w_in"], moment(m_w_in, "w_in"), moment(v_w_in, "w_in"))
    g_in_thru, land = _scatter_wait(*sems_a, g_in_thru, land, (0, D // 2), [res[n][1] for n in res] + [own_in, *mine],
                                    name="scatter_g_in_a_wait")
    _, land = _scatter_wait(*sems_b, g_in_thru, land, (D // 2, D // 2), (), name="scatter_g_in_b_wait")
    res["w_in"] = finish("w_in", _adamw_recv(mine[0], land, mine[1], mine[2], name="adamw_w_in", own=own_in))

    order = ["c_ctx", "w_mod", "b_mod", "w_in", "q_norm_w", "k_norm_w", "conv_qkv_w", "a_log", "dt_bias", "gdn_norm_w",
             "w_pa", "w_pd", "w_out", "w_up", "ffn_conv_w", "ffn_conv_b", "w_down", "final_norm_w"]
    return (loss, grad_x[None], *[res[n][0] for n in order], *[res[n][1] for n in order],
            *[res[n][2] for n in order], *[res[n][3] for n in order])
```

```python
import functools
import math

import jax
import jax.numpy as jnp
from jax import lax
from jax.experimental import pallas as pl
from jax.experimental.pallas import tpu as pltpu

F32 = jnp.float32
BF16 = jnp.bfloat16
HI = lax.Precision.HIGHEST
MESH = pl.DeviceIdType.MESH

NDEV = 8
D = 1024
HD = 128
AH, AKV, GRP = 8, 2, 4
GH = 8
CH = 64
DFF = 2816
GRID_W = 64
EPS = 1e-6
ROPE_THETA = 10000.0
LOG2E = math.log2(math.e)
C_KV, C_AQ, C_QKV, C_BL, C_Z, C_GATE, C_END = 0, 512, 1536, 4608, 5120, 6144, 8192
W_QKV, W_AQ, W_Z, W_END = 512, 3616, 4640, 7712


def _pad_columns(w):
    zeros = jnp.zeros((C_Z - C_QKV - (W_AQ - W_QKV), D), w.dtype)
    return jnp.concatenate([w[:W_QKV], w[W_AQ:W_Z], w[W_QKV:W_AQ], zeros, w[W_Z:]], axis=0)


def _unpad_columns(g):
    return jnp.concatenate([g[:C_AQ], g[C_QKV:C_QKV + W_AQ - W_QKV], g[C_AQ:C_QKV], g[C_Z:]], axis=0)
LR, B1, B2, AEPS, WD, STEP = 0.001, 0.9, 0.999, 1e-08, 0.01, 10
VMEM_BIG = 56 * 1024 * 1024
INTRA_FWD_CHUNKS = 36
INTRA_BWD_CHUNKS = 36


def _call(body, *, name, out_shape, grid=None, in_specs=None, out_specs=None, scratch=(), sem=None,
          vmem=None, aliases=None):
    params = {}
    if sem is not None:
        params["dimension_semantics"] = sem
    if vmem is not None:
        params["vmem_limit_bytes"] = vmem
    kw = {}
    if grid is not None:
        kw["grid"] = grid
    if in_specs is not None:
        kw["in_specs"] = in_specs
    if out_specs is not None:
        kw["out_specs"] = out_specs
    if aliases:
        kw["input_output_aliases"] = aliases
    return pl.pallas_call(body, name=name, out_shape=out_shape, scratch_shapes=list(scratch),
                          compiler_params=pltpu.CompilerParams(**params), **kw)


def _call_carrying(body, exch, *, name, out_shape, grid, in_specs, out_specs, scratch=(), vmem=None):
    n, nin, nout, nscr = exch.n, len(in_specs), len(out_shape), len(scratch)
    steps = math.prod(grid)
    mid = (2 * steps) // 3

    def wrapped(*refs):
        ins, cins = refs[:nin], refs[nin:nin + n]
        outs, couts = refs[nin + n:nin + n + nout], refs[nin + n + nout:nin + 2 * n + nout]
        scr, sems = refs[nin + 2 * n + nout:nin + 2 * n + nout + nscr], refs[nin + 2 * n + nout + nscr:]
        ids = [pl.program_id(i) for i in range(len(grid))]
        first = functools.reduce(jnp.logical_and, [i == 0 for i in ids])
        last = functools.reduce(jnp.logical_and, [i == g - 1 for i, g in zip(ids, grid)])

        @pl.when(first)
        def _():
            exch.start(cins, couts, sems)

        if hasattr(exch, "middle"):
            linear = functools.reduce(lambda acc, ig: acc * ig[1] + ig[0], zip(ids, grid), 0)

            @pl.when(linear == mid)
            def _():
                exch.middle(cins, couts, sems)

        body(*ins, *outs, *scr)

        @pl.when(last)
        def _():
            exch.finish(cins, couts, sems)

    params = {"dimension_semantics": ("arbitrary",) * len(grid)}
    if vmem is not None:
        params["vmem_limit_bytes"] = vmem
    fn = pl.pallas_call(wrapped, name=name, out_shape=tuple(out_shape) + exch.out_shape, grid=grid,
                        in_specs=list(in_specs) + [HBM] * n, out_specs=tuple(out_specs) + (HBM,) * n,
                        scratch_shapes=list(scratch) + exch.scratch, compiler_params=pltpu.CompilerParams(**params))

    def run(*args):
        res = fn(*args, *exch.arrs)
        return res[:nout], list(res[nout:])

    return run


def _sds(shape, dtype=F32):
    return jax.ShapeDtypeStruct(tuple(shape), dtype)


def _dot(a, b, ca, cb):
    return lax.dot_general(a.astype(BF16), b.astype(BF16), (((ca,), (cb,)), ((), ())),
                           preferred_element_type=F32)


@jax.custom_vjp
def _nn(a, b):
    return _dot(a, b, 1, 0)


@jax.custom_vjp
def _nt(a, b):
    return _dot(a, b, 1, 1)


@jax.custom_vjp
def _tn(a, b):
    return _dot(a, b, 0, 0)


_nn.defvjp(lambda a, b: (_nn(a, b), (a, b)), lambda r, g: (_nt(g, r[1]), _tn(r[0], g)))
_nt.defvjp(lambda a, b: (_nt(a, b), (a, b)), lambda r, g: (_nn(g, r[1]), _tn(g, r[0])))
_tn.defvjp(lambda a, b: (_tn(a, b), (a, b)), lambda r, g: (_nt(r[1], g), _nn(r[0], g)))


def _mdot(a, b):
    return jnp.dot(a, b, precision=lax.Precision.HIGH, preferred_element_type=F32)


def _maskdot(mask, a, cm):
    hi = a.astype(BF16)
    r = a - hi.astype(F32)
    mid = r.astype(BF16)
    lo = (r - mid.astype(F32)).astype(BF16)
    mb = mask.astype(BF16)
    dims = (((cm,), (0,)), ((), ()))
    return (lax.dot_general(mb, hi, dims, preferred_element_type=F32)
            + lax.dot_general(mb, mid, dims, preferred_element_type=F32)
            + lax.dot_general(mb, lo, dims, preferred_element_type=F32))


@jax.custom_vjp
def _mask_nn(mask, a):
    return _maskdot(mask, a, 1)


_mask_nn.defvjp(lambda mask, a: (_maskdot(mask, a, 1), mask),
                lambda mask, g: (jnp.zeros_like(mask), _maskdot(mask, g, 0)))


@jax.custom_vjp
def _saved_inverse(lmat, x):
    return x


def _saved_inverse_bwd(x, g):
    t = lax.dot_general(x, g, (((0,), (0,)), ((), ())), precision=lax.Precision.HIGH, preferred_element_type=F32)
    dl = lax.dot_general(t, x, (((1,), (1,)), ((), ())), precision=lax.Precision.HIGH, preferred_element_type=F32)
    return -dl, jnp.zeros_like(x)


_saved_inverse.defvjp(lambda lmat, x: (x, x), _saved_inverse_bwd)


def _row_ids(shape):
    return lax.broadcasted_iota(jnp.int32, shape, 0)


def _shift_rows(x, down, bounds):
    n = x.shape[0]
    rows = _row_ids(x.shape)
    y = pltpu.roll(x, 1 if down else n - 1, 0)
    edge = functools.reduce(jnp.logical_or, [rows == (s if down else e - 1) for s, e in bounds])
    return jnp.where(edge, 0.0, y)


def _make_shift(bounds):
    @jax.custom_vjp
    def down(x):
        return _shift_rows(x, True, bounds)

    @jax.custom_vjp
    def up(x):
        return _shift_rows(x, False, bounds)

    down.defvjp(lambda x: (down(x), None), lambda _, g: (up(g),))
    up.defvjp(lambda x: (up(x), None), lambda _, g: (down(g),))
    return down, up


@jax.custom_vjp
def _swap32(x):
    lane = lax.broadcasted_iota(jnp.int32, x.shape, x.ndim - 1)
    return jnp.where((lane % 64) < 32, pltpu.roll(x, HD - 32, x.ndim - 1), pltpu.roll(x, 32, x.ndim - 1))


_swap32.defvjp(lambda x: (_swap32(x), None), lambda _, g: (_swap32(g),))


def _rms(x):
    return x * lax.rsqrt(jnp.mean(x * x, axis=-1, keepdims=True) + EPS)


def _silu(x):
    return x * jax.nn.sigmoid(x)


def _mm(a, b, *, name, M, N, K, ta=False, tb=False, out_dtype=F32, bm=None, bn=None, bk=None, after=()):
    bm, bn, bk = bm or M, bn or N, bk or K
    assert M % bm == 0 and N % bn == 0 and K % bk == 0, (name, M, N, K, bm, bn, bk)
    nk = K // bk
    ca, cb = (0 if ta else 1), (1 if tb else 0)
    na = len(after)

    def body(a_ref, b_ref, *rest):
        o_ref, acc = rest[na], rest[na + 1:]
        r = _dot(a_ref[...], b_ref[...], ca, cb)
        if nk == 1:
            o_ref[...] = r.astype(out_dtype)
        else:
            acc_ref, = acc
            k = pl.program_id(2)

            @pl.when(k == 0)
            def _():
                acc_ref[...] = r

            @pl.when(k > 0)
            def _():
                acc_ref[...] += r

            @pl.when(k == nk - 1)
            def _():
                o_ref[...] = acc_ref[...].astype(out_dtype)

    a_spec = pl.BlockSpec((bk, bm), lambda i, j, k: (k, i)) if ta else pl.BlockSpec((bm, bk), lambda i, j, k: (i, k))
    b_spec = pl.BlockSpec((bn, bk), lambda i, j, k: (j, k)) if tb else pl.BlockSpec((bk, bn), lambda i, j, k: (k, j))
    return _call(body, name=name, out_shape=_sds((M, N), out_dtype), grid=(M // bm, N // bn, nk),
                 in_specs=[a_spec, b_spec] + [pl.BlockSpec(memory_space=pl.ANY)] * na,
                 out_specs=pl.BlockSpec((bm, bn), lambda i, j, k: (i, j)),
                 scratch=[pltpu.VMEM((bm, bn), F32)] if nk > 1 else [],
                 sem=("parallel", "parallel", "arbitrary"), vmem=VMEM_BIG)(a, b, *after)


def _normmod_fn(x, sh, sc):
    return _rms(x) * (1.0 + sc) + sh


def _normmod_fwd(x, mod, i_sh, i_sc, *, name, br=256, out_rows=None, off=0, into=None):
    R = x.shape[0]
    ob = off // br
    given = [] if into is None else [into]
    if given:
        out_rows = into.shape[0]

    def body(x_ref, mod_ref, *rest):
        rest[-1][...] = _normmod_fn(x_ref[...], mod_ref[i_sh:i_sh + 1, :], mod_ref[i_sc:i_sc + 1, :]).astype(BF16)

    return _call(body, name=name, out_shape=_sds((out_rows or R, D), BF16), grid=(R // br,),
                 in_specs=[pl.BlockSpec((br, D), lambda i: (i, 0)), pl.BlockSpec((6, D), lambda i: (0, 0))]
                 + [ANYSPEC] * len(given), out_specs=pl.BlockSpec((br, D), lambda i: (i + ob, 0)),
                 aliases={2: 0} if given else None, sem=("parallel",))(x, mod, *given)


def _normmod_bwd(x, mod, i_sh, i_sc, dh, dh_off, res, *, name, br=256):
    R = x.shape[0]
    ob = dh_off // br
    has_res = res is not None

    def body(x_ref, mod_ref, dh_ref, *rest):
        if has_res:
            res_ref, dx_ref, dsh_ref, dsc_ref = rest
        else:
            dx_ref, dsh_ref, dsc_ref = rest
        sh, sc = mod_ref[i_sh:i_sh + 1, :], mod_ref[i_sc:i_sc + 1, :]
        _, vjp = jax.vjp(_normmod_fn, x_ref[...], sh, sc)
        dx, dsh, dsc = vjp(dh_ref[...])
        dx_ref[...] = dx + res_ref[...] if has_res else dx

        @pl.when(pl.program_id(0) == 0)
        def _():
            dsh_ref[...] = jnp.zeros_like(dsh_ref)
            dsc_ref[...] = jnp.zeros_like(dsc_ref)

        dsh_ref[...] += dsh
        dsc_ref[...] += dsc

    row = pl.BlockSpec((br, D), lambda i: (i, 0))
    vec = pl.BlockSpec((1, D), lambda i: (0, 0))
    ins = [row, pl.BlockSpec((6, D), lambda i: (0, 0)), pl.BlockSpec((br, D), lambda i: (i + ob, 0))]
    args = [x, mod, dh]
    if has_res:
        ins.append(row)
        args.append(res)
    return _call(body, name=name, out_shape=(_sds((R, D)), _sds((1, D)), _sds((1, D))), grid=(R // br,),
                 in_specs=ins, out_specs=(row, vec, vec), sem=("arbitrary",))(*args)


def _rope(x, cos, sin):
    return x * cos + _swap32(x) * sin


def _aprep_fn(qs, ks, cos, sin, qw, kw):
    return ([_rope(_rms(q) * qw, cos, sin) for q in qs], [_rope(_rms(k) * kw, cos, sin) for k in ks])


def _aprep_fwd(proj, cos, sin, qw, kw, *, br=256):
    T = proj.shape[0]

    def body(x_ref, cos_ref, sin_ref, qw_ref, kw_ref, q_ref, k_ref, v_ref):
        qs = [x_ref[:, C_AQ + h * HD:C_AQ + (h + 1) * HD] for h in range(AH)]
        ks = [x_ref[:, h * HD:(h + 1) * HD] for h in range(AKV)]
        qo, ko = _aprep_fn(qs, ks, cos_ref[...], sin_ref[...], qw_ref[...], kw_ref[...])
        for h in range(AH):
            q_ref[h] = qo[h].astype(BF16)
        for h in range(AKV):
            k_ref[h] = ko[h].astype(BF16)
            v_ref[h] = x_ref[:, (AKV + h) * HD:(AKV + h + 1) * HD].astype(BF16)

    tab = pl.BlockSpec((br, HD), lambda i: (i, 0))
    vec = pl.BlockSpec((1, HD), lambda i: (0, 0))
    return _call(body, name="aprep_fwd",
                 out_shape=(_sds((AH, T, HD), BF16), _sds((AKV, T, HD), BF16), _sds((AKV, T, HD), BF16)),
                 grid=(T // br,),
                 in_specs=[pl.BlockSpec((br, C_QKV), lambda i: (i, 0)), tab, tab, vec, vec],
                 out_specs=(pl.BlockSpec((AH, br, HD), lambda i: (0, i, 0)),
                            pl.BlockSpec((AKV, br, HD), lambda i: (0, i, 0)),
                            pl.BlockSpec((AKV, br, HD), lambda i: (0, i, 0))),
                 sem=("parallel",))(proj, cos, sin, qw, kw)


def _aprep_bwd(proj, cos, sin, qw, kw, dq, dk, dv, dproj, L, *, br=256):
    T = proj.shape[0]
    lb = L // br

    def body(x_ref, cos_ref, sin_ref, qw_ref, kw_ref, dq_ref, dk_ref, dv_ref, _, dx_ref, dqw_ref, dkw_ref):
        i = pl.program_id(0)
        qs = [x_ref[:, C_AQ + h * HD:C_AQ + (h + 1) * HD] for h in range(AH)]
        ks = [x_ref[:, h * HD:(h + 1) * HD] for h in range(AKV)]
        _, vjp = jax.vjp(_aprep_fn, qs, ks, cos_ref[...], sin_ref[...], qw_ref[...], kw_ref[...])
        is_lat = i >= lb
        dqs = [jnp.where(is_lat, dq_ref[h], 0.0) for h in range(AH)]
        dks = [dk_ref[h] for h in range(AKV)]
        gq, gk, _, _, gqw, gkw = vjp((dqs, dks))
        for h in range(AH):
            dx_ref[:, C_AQ + h * HD:C_AQ + (h + 1) * HD] = gq[h].astype(BF16)
        for h in range(AKV):
            dx_ref[:, h * HD:(h + 1) * HD] = gk[h].astype(BF16)
            dx_ref[:, (AKV + h) * HD:(AKV + h + 1) * HD] = dv_ref[h].astype(BF16)

        @pl.when(i == 0)
        def _():
            dqw_ref[...] = jnp.zeros_like(dqw_ref)
            dkw_ref[...] = jnp.zeros_like(dkw_ref)

        dqw_ref[...] += gqw
        dkw_ref[...] += gkw

    tab = pl.BlockSpec((br, HD), lambda i: (i, 0))
    vec = pl.BlockSpec((1, HD), lambda i: (0, 0))
    kvb = pl.BlockSpec((AKV, br, HD), lambda i: (0, i, 0))
    blk = pl.BlockSpec((br, C_QKV), lambda i: (i, 0))
    return _call(body, name="aprep_bwd", out_shape=(_sds(dproj.shape, BF16), _sds((1, HD)), _sds((1, HD))),
                 grid=(T // br,),
                 in_specs=[blk, tab, tab, vec, vec,
                           pl.BlockSpec((AH, br, HD), lambda i: (0, jnp.maximum(i - lb, 0), 0)), kvb, kvb, ANYSPEC],
                 out_specs=(blk, vec, vec), aliases={8: 0},
                 sem=("arbitrary",))(proj, cos, sin, qw, kw, dq, dk, dv, dproj)


def _attn_grad(q, k, v, o, lse2, do):
    scale = HD ** -0.5
    p = jnp.exp2(_dot(q, k, 1, 1) * (scale * LOG2E) - lse2)
    dp = _dot(do, v, 1, 1)
    ds = p * (dp - jnp.sum(do * o, axis=-1, keepdims=True)) * scale
    return _dot(ds, k, 1, 0), _dot(ds, q, 0, 0), _dot(p, do, 0, 0)


ATTN_KEYS = 256


def _attn_fwd(q, k, v, L, exch, *, bq=128):
    T = q.shape[1]
    N = T - L
    lb = L // bq
    assert T % ATTN_KEYS == 0
    scale = HD ** -0.5
    heads = range(GRP)

    def body(q_ref, k_ref, v_ref, o_ref, o32_ref, lse_ref):
        qs = [q_ref[g] for g in heads]
        m = [jnp.full((bq, 1), -jnp.inf, F32) for _ in heads]
        l = [jnp.zeros((bq, 1), F32) for _ in heads]
        acc = [jnp.zeros((bq, HD), F32) for _ in heads]
        for c in range(T // ATTN_KEYS):
            kc, vc = k_ref[c * ATTN_KEYS:(c + 1) * ATTN_KEYS, :], v_ref[c * ATTN_KEYS:(c + 1) * ATTN_KEYS, :]
            s = [_dot(qs[g], kc, 1, 1) * (scale * LOG2E) for g in heads]
            m_new = [jnp.maximum(m[g], jnp.max(s[g], axis=-1, keepdims=True)) for g in heads]
            alpha = [jnp.exp2(m[g] - m_new[g]) for g in heads]
            p = [jnp.exp2(s[g] - m_new[g]) for g in heads]
            l = [l[g] * alpha[g] + jnp.sum(p[g], axis=-1, keepdims=True) for g in heads]
            acc = [acc[g] * alpha[g] + _dot(p[g], vc, 1, 0) for g in heads]
            m = m_new
        for g in heads:
            o = acc[g] / l[g]
            o_ref[:, g * HD:(g + 1) * HD] = o.astype(BF16)
            o32_ref[:, g * HD:(g + 1) * HD] = o
            lse_ref[g] = jnp.broadcast_to(m[g] + jnp.log2(l[g]), (bq, HD))

    kvb = pl.BlockSpec((None, T, HD), lambda g, i: (g, 0, 0))
    ob = pl.BlockSpec((bq, GRP * HD), lambda g, i: (i, g))
    return _call_carrying(
        body, exch, name="attn_fwd",
        out_shape=(_sds((N, AH * HD), BF16), _sds((N, AH * HD)), _sds((AH, N, HD))), grid=(AKV, N // bq),
        in_specs=[pl.BlockSpec((GRP, bq, HD), lambda g, i: (g, i + lb, 0)), kvb, kvb],
        out_specs=(ob, ob, pl.BlockSpec((GRP, bq, HD), lambda g, i: (g, i, 0))), vmem=VMEM_BIG)(q, k, v)


def _attn_bwd(q, k, v, o32, lse, do, L, exch, *, bq=128):
    T = q.shape[1]
    N = T - L
    lb = L // bq

    def body(q_ref, k_ref, v_ref, o_ref, lse_ref, do_ref, dq_ref, dk_ref, dv_ref):
        rows = lambda r: jnp.concatenate([r[:, g * HD:(g + 1) * HD] for g in range(GRP)], axis=0)
        lse = jnp.max(lse_ref[...].reshape(GRP * bq, HD), axis=-1, keepdims=True)
        dq, dk, dv = _attn_grad(q_ref[...].reshape(GRP * bq, HD), k_ref[...], v_ref[...], rows(o_ref), lse, rows(do_ref))
        dq_ref[...] = dq.reshape(GRP, bq, HD)

        @pl.when(pl.program_id(1) == 0)
        def _():
            dk_ref[...] = jnp.zeros_like(dk_ref)
            dv_ref[...] = jnp.zeros_like(dv_ref)

        dk_ref[...] += dk
        dv_ref[...] += dv

    kvb = pl.BlockSpec((None, T, HD), lambda g, i: (g, 0, 0))
    qb = pl.BlockSpec((GRP, bq, HD), lambda g, i: (g, i + lb, 0))
    hb = pl.BlockSpec((GRP, bq, HD), lambda g, i: (g, i, 0))
    ob = pl.BlockSpec((bq, GRP * HD), lambda g, i: (i, g))
    return _call_carrying(body, exch, name="attn_bwd",
                          out_shape=(_sds((AH, N, HD)), _sds((AKV, T, HD)), _sds((AKV, T, HD))), grid=(AKV, N // bq),
                          in_specs=[qb, kvb, kvb, ob, hb, ob], out_specs=(hb, kvb, kvb),
                          vmem=VMEM_BIG)(q, k, v, o32, lse, do)


def _gprep_fn(kind, shifts, x, w):
    down, up = shifts
    y = down(x) * w[0:1, :] + x * w[1:2, :] + up(x) * w[2:3, :]
    a = _silu(y)
    if kind == 2:
        return a
    a = a * lax.rsqrt(jnp.sum(a * a, axis=-1, keepdims=True) + EPS)
    return a * (HD ** -0.5) if kind == 0 else a


def _gprep_fwd(proj, conv_w, kind, bounds):
    T = proj.shape[0]
    shifts = _make_shift(bounds)
    cb = C_QKV // HD + kind * GH

    def body(x_ref, w_ref, o_ref):
        o_ref[...] = _gprep_fn(kind, shifts, x_ref[...], w_ref[...])

    return _call(body, name=f"gprep_fwd{kind}", out_shape=_sds((GH, T, HD)), grid=(GH,),
                 in_specs=[pl.BlockSpec((T, HD), lambda h: (0, cb + h)),
                           pl.BlockSpec((3, HD), lambda h: (0, kind * GH + h))],
                 out_specs=pl.BlockSpec((None, T, HD), lambda h: (h, 0, 0)), sem=("parallel",))(proj, conv_w)


def _gprep_bwd(proj, conv_w, kind, bounds, dy, dproj):
    T = proj.shape[0]
    shifts = _make_shift(bounds)
    cb = C_QKV // HD + kind * GH

    def body(x_ref, w_ref, dy_ref, _, dx_ref, dw_ref):
        _, vjp = jax.vjp(functools.partial(_gprep_fn, kind, shifts), x_ref[...], w_ref[...])
        dx, dw = vjp(dy_ref[0] + dy_ref[1])
        dx_ref[...] = dx.astype(BF16)
        dw_ref[...] = dw

    return _call(body, name=f"gprep_bwd{kind}", out_shape=(_sds(dproj.shape, BF16), _sds((3, GH * HD))), grid=(GH,),
                 in_specs=[pl.BlockSpec((T, HD), lambda h: (0, cb + h)),
                           pl.BlockSpec((3, HD), lambda h: (0, kind * GH + h)),
                           pl.BlockSpec((2, None, T, HD), lambda h: (0, h, 0, 0)), ANYSPEC],
                 out_specs=(pl.BlockSpec((T, HD), lambda h: (0, cb + h)), pl.BlockSpec((3, HD), lambda h: (0, h))),
                 aliases={3: 0}, sem=("parallel",))(proj, conv_w, dy, dproj)


def _bl_fn(x, alog, dtb):
    lane = lax.broadcasted_iota(jnp.int32, x.shape, 1)
    beta = jax.nn.sigmoid(x)
    z = x + dtb
    sp = jnp.maximum(z, 0.0) + jnp.log1p(jnp.exp(-jnp.abs(z)))
    la = -jnp.exp(alog) * sp
    return jnp.where(lane < 2 * GH, beta, jnp.where(lane < 4 * GH, la, 0.0))


def _bl_fwd(proj, alog, dtb, *, br=256):
    T = proj.shape[0]

    def body(x_ref, a_ref, d_ref, o_ref):
        o_ref[...] = _bl_fn(x_ref[...], a_ref[...], d_ref[...])

    vec = pl.BlockSpec((1, HD), lambda i: (0, 0))
    return _call(body, name="bl_fwd", out_shape=_sds((T, HD)), grid=(T // br,),
                 in_specs=[pl.BlockSpec((br, HD), lambda i: (i, C_BL // HD)), vec, vec],
                 out_specs=pl.BlockSpec((br, HD), lambda i: (i, 0)), sem=("parallel",))(proj, alog, dtb)


def _bl_bwd(proj, alog, dtb, dbl, dproj, *, br=256):
    T = proj.shape[0]
    wide = C_Z - C_BL

    def body(x_ref, a_ref, d_ref, g_ref, _, dx_ref, da_ref, dd_ref):
        g = g_ref[0, 0]
        for d in range(2):
            for h in range(GH):
                if d or h:
                    g = g + g_ref[d, h]
        _, vjp = jax.vjp(_bl_fn, x_ref[...], a_ref[...], d_ref[...])
        dx, da, dd = vjp(g)
        dx_ref[:, :HD] = dx.astype(BF16)
        dx_ref[:, HD:] = jnp.zeros((br, wide - HD), BF16)

        @pl.when(pl.program_id(0) == 0)
        def _():
            da_ref[...] = jnp.zeros_like(da_ref)
            dd_ref[...] = jnp.zeros_like(dd_ref)

        da_ref[...] += da
        dd_ref[...] += dd

    vec = pl.BlockSpec((1, HD), lambda i: (0, 0))
    return _call(body, name="bl_bwd", out_shape=(_sds(dproj.shape, BF16), _sds((1, HD)), _sds((1, HD))), grid=(T // br,),
                 in_specs=[pl.BlockSpec((br, HD), lambda i: (i, C_BL // HD)), vec, vec,
                           pl.BlockSpec((2, GH, br, HD), lambda i: (0, 0, i, 0)), ANYSPEC],
                 out_specs=(pl.BlockSpec((br, wide), lambda i: (i, C_BL // wide)), vec, vec), aliases={4: 0},
                 sem=("arbitrary",))(proj, alog, dtb, dbl, dproj)


def _chunk_masks(d):
    ii = lax.broadcasted_iota(jnp.int32, (CH, CH), 0)
    jj = lax.broadcasted_iota(jnp.int32, (CH, CH), 1)
    eye = (ii == jj).astype(F32)
    before = jnp.where(d == 0, (jj < ii).astype(F32), (jj > ii).astype(F32))
    return before, before + eye, eye


def _same_block(b):
    ii = lax.broadcasted_iota(jnp.int32, (CH, CH), 0)
    jj = lax.broadcasted_iota(jnp.int32, (CH, CH), 1)
    shift = b.bit_length() - 1
    return (jnp.right_shift(ii, shift) == jnp.right_shift(jj, shift)).astype(F32)


def _intra_fn(masks, sel_b, sel_l, qs, ks, vs, bls, xs=None):
    before, ateq, eye = masks
    inc = ateq > 0.0
    each = lambda f, *ls: [f(*t) for t in zip(*ls)]
    beta = each(lambda bl: jnp.sum(bl * sel_b, axis=-1, keepdims=True), bls)
    la = each(lambda bl: jnp.sum(bl * sel_l, axis=-1, keepdims=True), bls)
    gam = each(lambda a: _mask_nn(ateq, jnp.broadcast_to(a, (CH, HD))), la)
    gi = each(lambda g: g[:, :CH], gam)
    gj = each(lambda g: jnp.transpose(g)[:CH, :], gam)
    kq = each(lambda k, q: _nt(jnp.concatenate([k, q], axis=0), k), ks, qs)
    kk = each(lambda t: t[:CH], kq)
    qk = each(lambda t: t[CH:], kq)
    dec = each(lambda a, b: jnp.where(inc, jnp.exp(jnp.where(inc, a - b, 0.0)), 0.0), gi, gj)
    lmat = each(lambda b, d, m: before * (b * d * m), beta, dec, kk)
    if xs is None:
        same = lambda b: _same_block(b)
        l8 = each(lambda m: m * same(8), lmat)
        x = each(lambda m: eye - m, l8)
        p2 = each(lambda m: _mdot(m, m), l8)
        y = each(lambda a, b: _mdot(jnp.concatenate([a, b], axis=0), b), x, p2)
        x = each(lambda a, t: a + t[:CH], x, y)
        x = each(lambda a, t: a + _mdot(a, t[CH:]), x, y)
        for b in (8, 16, 32):
            below = same(2 * b) - same(b)
            x = each(lambda a, m: a - _mdot(a, _mdot(m * below, a)), x, lmat)
    else:
        x = each(_saved_inverse, lmat, xs)
    eg = each(jnp.exp, gam)
    uw = each(lambda a, b, v, e, k: _mdot(a, jnp.concatenate([b * v, (b * e) * k], axis=1)), x, beta, vs, eg, ks)
    u = each(lambda t: t[:, :HD], uw)
    w = each(lambda t: t[:, HD:], uw)
    tot = each(lambda a: jnp.sum(a, axis=0, keepdims=True), la)
    kd = each(lambda k, t, g: k * jnp.exp(t - g), ks, tot, gam)
    gl = each(lambda t: jnp.broadcast_to(jnp.exp(t), (1, HD)), tot)
    qd = each(lambda q, e: q * e, qs, eg)
    p = each(lambda d, m: d * m, dec, qk)
    return (u, w, kd, qd, p, gl, x) if xs is None else (u, w, kd, qd, p, gl)


def _dir_head_sel(d, h):
    lane = lax.broadcasted_iota(jnp.int32, (1, HD), 1)
    return (lane == d * GH + h).astype(F32), (lane == 2 * GH + d * GH + h).astype(F32)


def _intra_specs(T, G):
    nc = T // CH
    assert nc % G == 0
    qkv = pl.BlockSpec((None, G * CH, HD), lambda d, h, c: (h, c, 0))
    bl = pl.BlockSpec((G * CH, HD), lambda d, h, c: (c, 0))
    big = pl.BlockSpec((None, None, G * CH, HD), lambda d, h, c: (d, h, c, 0))
    pm = pl.BlockSpec((None, None, G * CH, CH), lambda d, h, c: (d, h, c, 0))
    gl = pl.BlockSpec((None, None, G, 1, HD), lambda d, h, c: (d, h, c, 0, 0))
    shapes = (_sds((2, GH, T, HD)),) + (_sds((2, GH, T, HD), BF16),) * 3 + (
        _sds((2, GH, T, CH), BF16), _sds((2, GH, nc, 1, HD)), _sds((2, GH, T, CH)))
    return nc, qkv, bl, big, pm, gl, shapes


def _chunks_per_step(T, most):
    nc = T // CH
    return max(g for g in range(1, most + 1) if nc % g == 0)


def _chunk_at(g, d, nc, ncc):
    pos = _visit_pos(g, d, nc, ncc)
    return pos, pl.ds(pl.multiple_of(pos * CH, CH), CH)


def _intra_fwd(q, k, v, bl, L, exch):
    T = q.shape[1]
    G = _chunks_per_step(T, INTRA_FWD_CHUNKS)
    nc, qkv_s, bl_s, big, pm, gl_s, shapes = _intra_specs(T, G)
    assert G == nc
    ncc = L // CH

    def body(q_ref, k_ref, v_ref, bl_ref, u_ref, w_ref, kd_ref, qd_ref, p_ref, gl_ref, x_ref):
        d, h = pl.program_id(0), pl.program_id(1)
        sb, sl = _dir_head_sel(d, h)
        rows = [slice(g * CH, (g + 1) * CH) for g in range(G)]
        outs = _intra_fn(_chunk_masks(d), sb, sl, *[[r[s, :] for s in rows] for r in (q_ref, k_ref, v_ref, bl_ref)])
        for g in range(G):
            pos, at = _chunk_at(g, d, nc, ncc)
            for r, o in zip((u_ref, w_ref, kd_ref, qd_ref, p_ref, x_ref), outs[:5] + outs[6:]):
                r[at, :] = o[g].astype(r.dtype)
            gl_ref[pos] = outs[5][g]

    return _call_carrying(body, exch, name="gdn_intra_fwd", out_shape=shapes, grid=(2, GH, nc // G),
                          in_specs=[qkv_s, qkv_s, qkv_s, bl_s], out_specs=(big, big, big, big, pm, gl_s, pm))(q, k, v, bl)


def _intra_bwd(q, k, v, bl, xinv, cts, L, exch):
    T = q.shape[1]
    G = _chunks_per_step(T, INTRA_BWD_CHUNKS)
    nc, qkv_s, bl_s, big, pm, gl_s, _ = _intra_specs(T, G)
    assert G == nc
    ncc = L // CH

    def body(q_ref, k_ref, v_ref, bl_ref, x_ref, du, dw, dkd, dqd, dp, dgl, dq_ref, dk_ref, dv_ref, dbl_ref):
        d, h = pl.program_id(0), pl.program_id(1)
        sb, sl = _dir_head_sel(d, h)
        rows = [slice(g * CH, (g + 1) * CH) for g in range(G)]
        places = [_chunk_at(g, d, nc, ncc) for g in range(G)]
        fn = functools.partial(_intra_fn, _chunk_masks(d), sb, sl, xs=[x_ref[at, :] for _, at in places])
        _, vjp = jax.vjp(fn, *[[r[s, :] for s in rows] for r in (q_ref, k_ref, v_ref, bl_ref)])
        cts = tuple([r[at, :] for _, at in places] for r in (du, dw, dkd, dqd, dp)) + ([dgl[pos] for pos, _ in places],)
        grads = vjp(cts)
        for g in range(G):
            for r, o in zip((dq_ref, dk_ref, dv_ref, dbl_ref), grads):
                r[rows[g], :] = o[g]

    return _call_carrying(body, exch, name="gdn_intra_bwd", out_shape=(_sds((2, GH, T, HD)),) * 4,
                          grid=(2, GH, nc // G), in_specs=[qkv_s, qkv_s, qkv_s, bl_s, pm, big, big, big, big, pm, gl_s],
                          out_specs=(big,) * 4)(q, k, v, bl, xinv, *cts)


def _scan_fn(s, u, w, kd, qd, p, gl):
    each = lambda f, *ls: [f(*t) for t in zip(*ls)]
    ws = each(_nn, w, s)
    delta = each(lambda a, b: a - b, u, ws)
    kdd = each(_tn, kd, delta)
    s_new = each(lambda g, a, b: g * a + b, gl, s, kdd)
    qs = each(_nn, qd, s)
    pd = each(_nn, p, delta)
    return each(lambda a, b: a + b, qs, pd), s_new


SCAN_BLOCK = 4


def _visit_pos(c, d, nc, ncc):
    back = ncc - 1 - c if c < ncc else ncc + (nc - 1 - c)
    return jnp.where(d == 0, c, back)


def _scan_specs(T, L, back):
    tb = SCAN_BLOCK * CH
    assert T % tb == 0 and L % tb == 0
    nb, ncb = T // tb, L // tb
    at = (lambda t: nb - 1 - t) if back else (lambda t: t)
    big = pl.BlockSpec((2, GH, tb, HD), lambda t: (0, 0, at(t), 0))
    pm = pl.BlockSpec((2, GH, tb, CH), lambda t: (0, 0, at(t), 0))
    gl = pl.BlockSpec((2, GH, SCAN_BLOCK, 1, HD), lambda t: (0, 0, at(t), 0, 0))
    st = pl.BlockSpec((2, GH, SCAN_BLOCK, HD, HD), lambda t: (0, 0, at(t), 0, 0))

    def natural(b):
        return jnp.where(b < ncb, ncb - 1 - b, nb - 1 - (b - ncb))

    do_specs = (pl.BlockSpec((GH, tb, HD), lambda t: (0, at(t), 0)),
                pl.BlockSpec((GH, tb, HD), lambda t: (0, natural(at(t)), 0)))
    return nb, big, pm, gl, st, do_specs


SCAN_STREAMS = [(d, h) for d in (0, 1) for h in range(GH)]


def _scan_fwd(u, w, kd, qd, p, gl, L):
    T = u.shape[2]
    nb, big, pm, gl_s, st, _ = _scan_specs(T, L, False)

    def body(u_ref, w_ref, kd_ref, qd_ref, p_ref, gl_ref, o_ref, st_ref, s_scr):
        @pl.when(pl.program_id(0) == 0)
        def _():
            s_scr[...] = jnp.zeros_like(s_scr)

        s = [s_scr[d, h] for d, h in SCAN_STREAMS]
        for i in range(SCAN_BLOCK):
            rows = slice(i * CH, (i + 1) * CH)
            for (d, h), sv in zip(SCAN_STREAMS, s):
                st_ref[d, h, i] = sv
            o, s = _scan_fn(s, *[[r[d, h, rows, :].astype(F32) for d, h in SCAN_STREAMS]
                                 for r in (u_ref, w_ref, kd_ref, qd_ref, p_ref)],
                            [gl_ref[d, h, i] for d, h in SCAN_STREAMS])
            for (d, h), ov in zip(SCAN_STREAMS, o):
                o_ref[d, h, rows, :] = ov
        for (d, h), sv in zip(SCAN_STREAMS, s):
            s_scr[d, h] = sv

    return _call(body, name="gdn_scan_fwd", out_shape=(_sds((2, GH, T, HD)), _sds((2, GH, T // CH, HD, HD))),
                 grid=(nb,), in_specs=[big, big, big, big, pm, gl_s], out_specs=(big, st),
                 scratch=[pltpu.VMEM((2, GH, HD, HD), F32)], sem=("arbitrary",), vmem=VMEM_BIG)(u, w, kd, qd, p, gl)


def _scan_bwd(u, w, kd, qd, p, gl, states, do, L, exch):
    T = u.shape[2]
    nb, big, pm, gl_s, st, do_specs = _scan_specs(T, L, True)

    def body(u_ref, w_ref, kd_ref, qd_ref, p_ref, gl_ref, st_ref, do0_ref, do1_ref,
             du_ref, dw_ref, dkd_ref, dqd_ref, dp_ref, dgl_ref, ds_scr):
        @pl.when(pl.program_id(0) == 0)
        def _():
            ds_scr[...] = jnp.zeros_like(ds_scr)

        ds = [ds_scr[d, h] for d, h in SCAN_STREAMS]
        for i in reversed(range(SCAN_BLOCK)):
            rows = slice(i * CH, (i + 1) * CH)
            mirror = slice((SCAN_BLOCK - 1 - i) * CH, (SCAN_BLOCK - i) * CH)
            _, vjp = jax.vjp(_scan_fn, [st_ref[d, h, i] for d, h in SCAN_STREAMS],
                             *[[r[d, h, rows, :].astype(F32) for d, h in SCAN_STREAMS]
                               for r in (u_ref, w_ref, kd_ref, qd_ref, p_ref)],
                             [gl_ref[d, h, i] for d, h in SCAN_STREAMS])
            dos = [do0_ref[h, rows, :] if d == 0 else do1_ref[h, mirror, :] for d, h in SCAN_STREAMS]
            ds, gu, gw, gkd, gqd, gp, ggl = vjp((dos, ds))
            for n, (d, h) in enumerate(SCAN_STREAMS):
                du_ref[d, h, rows, :] = gu[n]
                dw_ref[d, h, rows, :] = gw[n]
                dkd_ref[d, h, rows, :] = gkd[n]
                dqd_ref[d, h, rows, :] = gqd[n]
                dp_ref[d, h, rows, :] = gp[n]
                dgl_ref[d, h, i] = ggl[n]
        for (d, h), dv in zip(SCAN_STREAMS, ds):
            ds_scr[d, h] = dv

    return _call_carrying(
        body, exch, name="gdn_scan_bwd",
        out_shape=(_sds((2, GH, T, HD)),) * 4 + (_sds((2, GH, T, CH)), _sds((2, GH, T // CH, 1, HD))),
        grid=(nb,), in_specs=[big, big, big, big, pm, gl_s, st, *do_specs], out_specs=(big, big, big, big, pm, gl_s),
        scratch=[pltpu.VMEM((2, GH, HD, HD), F32)], vmem=VMEM_BIG)(u, w, kd, qd, p, gl, states, do, do)


def _gout_fn(o0, o1, z, gw):
    return _rms(o0 + o1) * gw * _silu(z)


def _backward_latent(o_ref, L):
    nl = (o_ref.shape[1] - L) // CH
    return jnp.concatenate([o_ref[1, L + (nl - 1 - j) * CH:L + (nl - j) * CH, :] for j in range(nl)], axis=0)


def _gout_fwd(o, proj, gw, L):
    T = o.shape[2]
    N = T - L
    ob = pl.BlockSpec((2, None, T, HD), lambda h: (0, h, 0, 0))

    def body(o_ref, z_ref, gw_ref, y_ref):
        y_ref[...] = _gout_fn(o_ref[0, L:, :], _backward_latent(o_ref, L), z_ref[L:, :], gw_ref[...]).astype(BF16)

    return _call(body, name="gout_fwd", out_shape=_sds((N, GH * HD), BF16), grid=(GH,),
                 in_specs=[ob, pl.BlockSpec((T, HD), lambda h: (0, C_Z // HD + h)), pl.BlockSpec((1, HD), lambda h: (0, 0))],
                 out_specs=pl.BlockSpec((N, HD), lambda h: (0, h)), sem=("parallel",))(o, proj, gw)


def _gout_bwd(o, proj, gw, dy, dproj, L):
    T = o.shape[2]
    N = T - L
    ob = pl.BlockSpec((2, None, T, HD), lambda h: (0, h, 0, 0))

    def body(o_ref, z_ref, gw_ref, dy_ref, _, do_ref, dz_ref, dgw_ref):
        _, vjp = jax.vjp(_gout_fn, o_ref[0, L:, :], _backward_latent(o_ref, L), z_ref[L:, :], gw_ref[...])
        g0, _, gz, ggw = vjp(dy_ref[...])
        do_ref[:L, :] = jnp.zeros((L, HD), F32)
        do_ref[L:, :] = g0
        dz_ref[:L, :] = jnp.zeros((L, HD), BF16)
        dz_ref[L:, :] = gz.astype(BF16)

        @pl.when(pl.program_id(0) == 0)
        def _():
            dgw_ref[...] = jnp.zeros_like(dgw_ref)

        dgw_ref[...] += ggw

    zb = pl.BlockSpec((T, HD), lambda h: (0, C_Z // HD + h))
    return _call(body, name="gout_bwd", out_shape=(_sds((GH, T, HD)), _sds(dproj.shape, BF16), _sds((1, HD))),
                 grid=(GH,),
                 in_specs=[ob, zb, pl.BlockSpec((1, HD), lambda h: (0, 0)), pl.BlockSpec((N, HD), lambda h: (0, h)), ANYSPEC],
                 out_specs=(pl.BlockSpec((None, T, HD), lambda h: (h, 0, 0)), zb, pl.BlockSpec((1, HD), lambda h: (0, 0))),
                 aliases={4: 1}, sem=("arbitrary",))(o, proj, gw, dy, dproj)


def _merge_fn(pa, pd, ga, gd):
    return jax.nn.sigmoid(ga) * pa + jax.nn.sigmoid(gd) * pd


def _merge_fwd(pa, pd, proj, L, *, br=256):
    N = pa.shape[0]
    lb = L // br
    row = pl.BlockSpec((br, D), lambda i: (i, 0))

    def body(pa_ref, pd_ref, ga_ref, gd_ref, y_ref):
        y_ref[...] = _merge_fn(pa_ref[...], pd_ref[...], ga_ref[...], gd_ref[...]).astype(BF16)

    return _call(body, name="merge_fwd", out_shape=_sds((N, D), BF16), grid=(N // br,),
                 in_specs=[row, row, pl.BlockSpec((br, D), lambda i: (i + lb, C_GATE // D)),
                           pl.BlockSpec((br, D), lambda i: (i + lb, C_GATE // D + 1))],
                 out_specs=row, sem=("parallel",))(pa, pd, proj, proj)


def _merge_bwd(pa, pd, proj, dy, L, *, br=256):
    N = pa.shape[0]
    T = N + L
    lb = L // br
    lrow = pl.BlockSpec((br, D), lambda i: (jnp.maximum(i - lb, 0), 0))

    def body(pa_ref, pd_ref, ga_ref, gd_ref, dy_ref, dpa_ref, dpd_ref, dg_ref):
        lat = pl.program_id(0) >= lb
        _, vjp = jax.vjp(_merge_fn, pa_ref[...], pd_ref[...], ga_ref[...], gd_ref[...])
        gpa, gpd, gga, ggd = vjp(dy_ref[...])
        dpa_ref[...] = gpa.astype(BF16)
        dpd_ref[...] = gpd.astype(BF16)
        dg_ref[:, :D] = jnp.where(lat, gga, 0.0).astype(BF16)
        dg_ref[:, D:] = jnp.where(lat, ggd, 0.0).astype(BF16)

    return _call(body, name="merge_bwd", out_shape=(_sds((N, D), BF16), _sds((N, D), BF16), _sds((T, C_END), BF16)),
                 grid=(T // br,),
                 in_specs=[lrow, lrow, pl.BlockSpec((br, D), lambda i: (i, C_GATE // D)),
                           pl.BlockSpec((br, D), lambda i: (i, C_GATE // D + 1)), lrow],
                 out_specs=(lrow, lrow, pl.BlockSpec((br, 2 * D), lambda i: (i, C_GATE // (2 * D)))),
                 sem=("arbitrary",))(pa, pd, proj, proj, dy)


def _resid_fwd(x, m, mod, i_g, *, name, br=256):
    R = x.shape[0]
    row = pl.BlockSpec((br, D), lambda i: (i, 0))

    def body(x_ref, m_ref, mod_ref, o_ref):
        o_ref[...] = x_ref[...] + mod_ref[i_g:i_g + 1, :] * m_ref[...]

    return _call(body, name=name, out_shape=_sds((R, D)), grid=(R // br,),
                 in_specs=[row, row, pl.BlockSpec((6, D), lambda i: (0, 0))], out_specs=row,
                 sem=("parallel",))(x, m, mod)


def _resid_bwd(dx, m, mod, i_g, *, name, br=256):
    R = dx.shape[0]
    row = pl.BlockSpec((br, D), lambda i: (i, 0))
    vec = pl.BlockSpec((1, D), lambda i: (0, 0))

    def body(dx_ref, m_ref, mod_ref, dm_ref, dg_ref):
        dxv = dx_ref[...]
        dm_ref[...] = (dxv * mod_ref[i_g:i_g + 1, :]).astype(BF16)

        @pl.when(pl.program_id(0) == 0)
        def _():
            dg_ref[...] = jnp.zeros_like(dg_ref)

        dg_ref[...] += jnp.sum(dxv * m_ref[...], axis=0, keepdims=True)

    return _call(body, name=name, out_shape=(_sds((R, D), BF16), _sds((1, D))), grid=(R // br,),
                 in_specs=[row, row, pl.BlockSpec((6, D), lambda i: (0, 0))], out_specs=(row, vec),
                 sem=("arbitrary",))(dx, m, mod)


def _ffn_fn(shifts, ug, uv, wg, wv, bg, bv):
    down, up = shifts

    def conv(x, w, b):
        return down(x) * w[0:1, :] + x * w[1:2, :] + up(x) * w[2:3, :] + b

    return _silu(conv(ug, wg, bg)) * conv(uv, wv, bv)


def _ffn_fwd(up, cw, cb, *, bw=256):
    N = up.shape[0]
    shifts = _make_shift(((0, N),))
    nb = DFF // bw

    def body(ug, uv, wg, wv, bg, bv, a_ref):
        a_ref[...] = _ffn_fn(shifts, ug[...], uv[...], wg[...], wv[...], bg[...], bv[...]).astype(BF16)

    def col(rows, off):
        return pl.BlockSpec((rows, bw), lambda j: (0, j + off))

    return _call(body, name="ffn_fwd", out_shape=_sds((N, DFF), BF16), grid=(nb,),
                 in_specs=[col(N, 0), col(N, nb), col(3, 0), col(3, nb), col(1, 0), col(1, nb)],
                 out_specs=col(N, 0), sem=("parallel",), vmem=VMEM_BIG)(up, up, cw, cw, cb, cb)


def _ffn_bwd(up, cw, cb, da, *, bw=256):
    N = up.shape[0]
    shifts = _make_shift(((0, N),))
    nb = DFF // bw

    def body(ug, uv, wg, wv, bg, bv, da_ref, dug, duv, dwg, dwv, dbg, dbv):
        _, vjp = jax.vjp(functools.partial(_ffn_fn, shifts), ug[...], uv[...], wg[...], wv[...], bg[...], bv[...])
        g = vjp(da_ref[...])
        dug[...] = g[0].astype(BF16)
        duv[...] = g[1].astype(BF16)
        dwg[...], dwv[...], dbg[...], dbv[...] = g[2], g[3], g[4], g[5]

    def col(rows, off):
        return pl.BlockSpec((rows, bw), lambda j: (0, j + off))

    half = (_sds((N, DFF), BF16), _sds((N, DFF), BF16), _sds((3, DFF)), _sds((3, DFF)), _sds((1, DFF)), _sds((1, DFF)))
    dug, duv, dwg, dwv, dbg, dbv = _call(
        body, name="ffn_bwd", out_shape=half, grid=(nb,),
        in_specs=[col(N, 0), col(N, nb), col(3, 0), col(3, nb), col(1, 0), col(1, nb), col(N, 0)],
        out_specs=(col(N, 0), col(N, 0), col(3, 0), col(3, 0), col(1, 0), col(1, 0)),
        sem=("parallel",), vmem=VMEM_BIG)(up, up, cw, cw, cb, cb, da)
    return (jnp.concatenate([dug, duv], axis=1), jnp.concatenate([dwg, dwv], axis=1),
            jnp.concatenate([dbg, dbv], axis=1))


def _head_fn(x1, dn, g2, fw, tgt):
    y = _rms(x1 + g2 * dn) * fw
    err = y - tgt
    return 0.5 * jnp.sum(jnp.mean(err * err, axis=-1))


def _head(x1, dn, mod, fw, tgt, *, br=256):
    N = x1.shape[0]
    row = pl.BlockSpec((br, D), lambda i: (i, 0))
    vec = pl.BlockSpec((1, D), lambda i: (0, 0))
    one = pl.BlockSpec((1, HD), lambda i: (0, 0))

    def body(x1_ref, dn_ref, mod_ref, fw_ref, tgt_ref, loss_ref, dx_ref, ddn_ref, dg_ref, dfw_ref):
        loss, (gx, gdn, gg, gfw) = jax.value_and_grad(_head_fn, argnums=(0, 1, 2, 3))(
            x1_ref[...], dn_ref[...], mod_ref[5:6, :], fw_ref[...], tgt_ref[...])
        dx_ref[...] = gx
        ddn_ref[...] = gdn.astype(BF16)

        @pl.when(pl.program_id(0) == 0)
        def _():
            loss_ref[...] = jnp.zeros_like(loss_ref)
            dg_ref[...] = jnp.zeros_like(dg_ref)
            dfw_ref[...] = jnp.zeros_like(dfw_ref)

        loss_ref[...] += jnp.broadcast_to(loss, (1, HD))
        dg_ref[...] += gg
        dfw_ref[...] += gfw

    return _call(body, name="head", out_shape=(_sds((1, HD)), _sds((N, D)), _sds((N, D), BF16), _sds((1, D)), _sds((1, D))),
                 grid=(N // br,), in_specs=[row, row, pl.BlockSpec((6, D), lambda i: (0, 0)), vec, row],
                 out_specs=(one, row, row, vec, vec), sem=("arbitrary",))(x1, dn, mod, fw, tgt)


def _adamw(w, g, m, v, *, name):
    shape = w.shape
    cols = shape[-1]
    rows = max(1, math.prod(shape[:-1]))
    w2, g2, m2, v2 = (t.reshape(rows, cols) for t in (w, g, m, v))
    br = 256 if rows % 256 == 0 else rows
    c1 = 1.0 - B1 ** STEP
    c2 = 1.0 - B2 ** STEP

    def body(w_ref, g_ref, m_ref, v_ref, d_ref, nm_ref, nv_ref):
        gv = g_ref[...]
        nm = B1 * m_ref[...] + (1.0 - B1) * gv
        nv = B2 * v_ref[...] + (1.0 - B2) * (gv * gv)
        d_ref[...] = -LR * ((nm / c1) / (jnp.sqrt(nv / c2) + AEPS) + WD * w_ref[...])
        nm_ref[...] = nm
        nv_ref[...] = nv

    blk = pl.BlockSpec((br, cols), lambda i: (i, 0))
    outs = _call(body, name=name, out_shape=(_sds((rows, cols)),) * 3, grid=(rows // br,),
                 in_specs=[blk] * 4, out_specs=(blk,) * 3, sem=("parallel",))(w2, g2, m2, v2)
    return tuple(t.reshape(shape) for t in outs)


def _adamw_many(items, *, name):
    k = len(items)
    shapes = [w.shape for w, _, _, _ in items]
    flat = [t.reshape(max(1, math.prod(t.shape[:-1])), t.shape[-1]) for it in items for t in it]
    c1 = 1.0 - B1 ** STEP
    c2 = 1.0 - B2 ** STEP

    def body(*refs):
        ins, outs = refs[:4 * k], refs[4 * k:]
        for i in range(k):
            w_ref, g_ref, m_ref, v_ref = ins[4 * i:4 * i + 4]
            gv = g_ref[...]
            nm = B1 * m_ref[...] + (1.0 - B1) * gv
            nv = B2 * v_ref[...] + (1.0 - B2) * (gv * gv)
            outs[3 * i][...] = -LR * ((nm / c1) / (jnp.sqrt(nv / c2) + AEPS) + WD * w_ref[...])
            outs[3 * i + 1][...] = nm
            outs[3 * i + 2][...] = nv

    res = _call(body, name=name, out_shape=tuple(_sds(flat[4 * i].shape) for i in range(k) for _ in range(3)))(*flat)
    return [tuple(res[3 * i + j].reshape(shapes[i]) for j in range(3)) for i in range(k)]


def _rope_tables(N, L):
    t = jnp.arange(N)
    pos = jnp.stack([(t // GRID_W).astype(F32), (t % GRID_W).astype(F32)], axis=1)
    inv = ROPE_THETA ** (-jnp.arange(0, HD // 2, 2, dtype=F32) / (HD // 2))
    ang = pos[:, :, None] * inv[None, None, :]
    cos = jnp.broadcast_to(jnp.cos(ang)[:, :, None, :], (N, 2, 2, HD // 4)).reshape(N, HD)
    sin = jnp.broadcast_to(jnp.sin(ang)[:, :, None, :], (N, 2, 2, HD // 4))
    sin = (sin * jnp.array([-1.0, 1.0], F32)[None, None, :, None]).reshape(N, HD)
    cos = jnp.concatenate([jnp.ones((L, HD), F32), cos], axis=0)
    sin = jnp.concatenate([jnp.zeros((L, HD), F32), sin], axis=0)
    return cos, sin


def _pad_lanes(v, off=0):
    return jnp.zeros((1, HD), F32).at[0, off:off + v.shape[0]].set(v)


def _local_step(x, ctx, tgt, mod_lat, mod_ctx, w_in, shards, small):
    N, L = x.shape[0], ctx.shape[0]
    T = N + L
    bounds = ((0, L), (L, T))
    qw, kw, gw = small["q_norm_w"], small["k_norm_w"], small["gdn_norm_w"]
    conv_w, ffn_w, ffn_b, fnw = small["conv_qkv_w"], small["ffn_conv_w"], small["ffn_conv_b"], small["final_norm_w"]
    alog = _pad_lanes(small["a_log"].reshape(-1), 2 * GH)
    dtb = _pad_lanes(small["dt_bias"].reshape(-1), 2 * GH)
    cos, sin = _rope_tables(N, L)
    bt = T
    bnl = 256 if N % 1024 else 1024

    h1 = _normmod_fwd(ctx, mod_ctx, 0, 1, name="normmod_ctx", out_rows=T)
    h1 = _normmod_fwd(x, mod_lat, 0, 1, name="normmod_x", off=L, into=h1)
    proj = _mm(h1, w_in, name="mm_in", M=T, N=C_END, K=D, tb=True, bm=bt, bn=1024)
    aq, ak, av = _aprep_fwd(proj, cos, sin, qw, kw)
    (attn, attn32, lse), (up_g,) = _attn_fwd(aq, ak, av, L, _GatherTwoLevel([shards["w_up"]]))
    gq = _gprep_fwd(proj, conv_w, 0, bounds)
    gk = _gprep_fwd(proj, conv_w, 1, bounds)
    gv = _gprep_fwd(proj, conv_w, 2, bounds)
    bl = _bl_fwd(proj, alog, dtb)
    intra, (down_g, pa_g, pd_g, out_g) = _intra_fwd(
        gq, gk, gv, bl, L, _GatherTwoLevel([shards[n] for n in ("w_down", "w_pa", "w_pd", "w_out")]))
    w_up, w_down = up_g.reshape(2 * DFF, D), down_g.reshape(DFF, D)
    w_pa, w_pd, w_out = pa_g.reshape(D, D), pd_g.reshape(D, D), out_g.reshape(D, D)
    xinv, intra = intra[6], intra[:6]
    o, states = _scan_fwd(*intra, L)
    gdn = _gout_fwd(o, proj, gw, L)
    pa = _mm(attn, w_pa, name="mm_pa", M=N, N=D, K=D, bm=bnl)
    pd = _mm(gdn, w_pd, name="mm_pd", M=N, N=D, K=D, bm=bnl)
    y = _merge_fwd(pa, pd, proj, L)
    m = _mm(y, w_out, name="mm_out", M=N, N=D, K=D, bm=bnl)
    x1 = _resid_fwd(x, m, mod_lat, 2, name="resid1")
    h2 = _normmod_fwd(x1, mod_lat, 3, 4, name="normmod_x1")
    up = _mm(h2, w_up, name="mm_up", M=N, N=2 * DFF, K=D, tb=True, bm=bnl, bn=2 * DFF // 4)
    a = _ffn_fwd(up, ffn_w, ffn_b)
    dn = _mm(a, w_down, name="mm_down", M=N, N=D, K=DFF, bm=bnl)
    loss, dx2, ddn, dg2, dfnw = _head(x1, dn, mod_lat, fnw, tgt)

    da = _mm(ddn, w_down, name="mm_down_dx", M=N, N=DFF, K=D, tb=True, bm=bnl, bn=DFF // 2)
    g_down = _mm(a, ddn, name="mm_down_dw", M=DFF, N=D, K=N, ta=True, bm=DFF // 2, out_dtype=BF16)
    dup, d_ffn_w, d_ffn_b = _ffn_bwd(up, ffn_w, ffn_b, da)
    dh2 = _mm(dup, w_up, name="mm_up_dx", M=N, N=D, K=2 * DFF, bm=bnl, bk=2 * DFF // 4)
    g_up = _mm(dup, h2, name="mm_up_dw", M=2 * DFF, N=D, K=N, ta=True, bm=2 * DFF // 4, out_dtype=BF16)
    dx1, dsh2, dsc2 = _normmod_bwd(x1, mod_lat, 3, 4, dh2, 0, dx2, name="normmod_x1_bwd")
    dm, dg1 = _resid_bwd(dx1, m, mod_lat, 2, name="resid1_bwd")
    dy = _mm(dm, w_out, name="mm_out_dx", M=N, N=D, K=D, tb=True, bm=bnl)
    g_out = _mm(y, dm, name="mm_out_dw", M=D, N=D, K=N, ta=True, out_dtype=BF16)
    dpa, dpd, dproj = _merge_bwd(pa, pd, proj, dy, L)
    dattn = _mm(dpa, w_pa, name="mm_pa_dx", M=N, N=D, K=D, tb=True, bm=bnl)
    g_pa = _mm(attn, dpa, name="mm_pa_dw", M=D, N=D, K=N, ta=True, out_dtype=BF16)
    dgdn = _mm(dpd, w_pd, name="mm_pd_dx", M=N, N=D, K=D, tb=True, bm=bnl)
    g_pd = _mm(gdn, dpd, name="mm_pd_dw", M=D, N=D, K=N, ta=True, out_dtype=BF16)
    do, dproj, dgw = _gout_bwd(o, proj, gw, dgdn, dproj, L)
    cts, recv_a = _scan_bwd(*intra, states, do, L, _Exchange(
        [g_out.reshape(NDEV, D // NDEV, D), g_pa.reshape(NDEV, D // NDEV, D), g_pd.reshape(NDEV, D // NDEV, D)], True))
    (dgq, dgk, dgv, dbl), recv_b = _intra_bwd(gq, gk, gv, bl, xinv, cts, L, _Exchange(
        [g_up.reshape(NDEV, 2 * DFF // NDEV, D)], True))
    dproj, dwq = _gprep_bwd(proj, conv_w, 0, bounds, dgq, dproj)
    dproj, dwk = _gprep_bwd(proj, conv_w, 1, bounds, dgk, dproj)
    dproj, dwv = _gprep_bwd(proj, conv_w, 2, bounds, dgv, dproj)
    dproj, dalog, ddtb = _bl_bwd(proj, alog, dtb, dbl, dproj)
    (daq_h, dak_h, dav_h), recv_c = _attn_bwd(aq, ak, av, attn32, lse, dattn, L, _Exchange(
        [g_down.reshape(NDEV, DFF // NDEV, D)], True))
    recv = dict(zip(("w_out", "w_pa", "w_pd", "w_up", "w_down"), recv_a + recv_b + recv_c))
    dproj, dqw, dkw = _aprep_bwd(proj, cos, sin, qw, kw, daq_h, dak_h, dav_h, dproj, L)
    g_in = _mm(dproj, h1, name="mm_in_dw", M=C_END, N=D, K=T, ta=True, bm=1024, out_dtype=BF16)
    *pending, token = _scatter_start(g_in, None, (0, D // 2), (), name="scatter_g_in_a_start")
    dh1 = _mm(dproj, w_in, name="mm_in_dx", M=T, N=D, K=C_END, bm=bt, bk=1024, after=(token,))
    grad_x, dsh1, dsc1 = _normmod_bwd(x, mod_lat, 0, 1, dh1, L, dx1, name="normmod_x_bwd")
    _, dcsh1, dcsc1 = _normmod_bwd(ctx, mod_ctx, 0, 1, dh1, 0, None, name="normmod_ctx_bwd")

    z1 = jnp.zeros((1, D), F32)
    dmod_lat = jnp.concatenate([dsh1, dsc1, dg1, dsh2, dsc2, dg2], axis=0)
    dmod_ctx = jnp.concatenate([dcsh1, dcsc1, z1, z1, z1, z1], axis=0)
    gsmall = {
        "q_norm_w": dqw, "k_norm_w": dkw, "gdn_norm_w": dgw,
        "conv_qkv_w": jnp.concatenate([dwq, dwk, dwv], axis=1),
        "a_log": dalog[0, 2 * GH:4 * GH], "dt_bias": ddtb[0, 2 * GH:4 * GH],
        "ffn_conv_w": d_ffn_w, "ffn_conv_b": d_ffn_b, "final_norm_w": dfnw,
    }
    return loss[0, 0], grad_x, pending, recv, dmod_lat, dmod_ctx, gsmall


HBM = pl.BlockSpec(memory_space=pltpu.HBM)
ANYSPEC = pl.BlockSpec(memory_space=pl.ANY)


def _position():
    x, y, c = lax.axis_index("x"), lax.axis_index("y"), lax.axis_index("c")
    return x, y, c, 4 * x + 2 * y + c


def _peer(x, y, c, k):
    px = 1 - x if k & 4 else x
    py = 1 - y if k & 2 else y
    pc = 1 - c if k & 1 else c
    return (px, py, pc), 4 * px + 2 * py + pc


def _exchange(arrs, *, name, scatter):
    exch = _Exchange(arrs, scatter)
    n = exch.n

    def body(*refs):
        ins, outs, sems = refs[:n], refs[n:2 * n], refs[2 * n:]
        exch.start(ins, outs, sems)
        exch.finish(ins, outs, sems)

    outs = pl.pallas_call(body, name=name, out_shape=exch.out_shape, in_specs=[HBM] * n, out_specs=(HBM,) * n,
                          scratch_shapes=exch.scratch,
                          compiler_params=pltpu.CompilerParams(has_side_effects=True))(*arrs)
    return list(outs)


class _Exchange:
    def __init__(self, arrs, scatter):
        self.arrs, self.scatter, self.n = list(arrs), scatter, len(arrs)
        self.out_shape = tuple(_sds(a.shape if scatter else (NDEV,) + a.shape, a.dtype) for a in arrs)
        self.scratch = [pltpu.SemaphoreType.DMA((self.n, NDEV - 1)), pltpu.SemaphoreType.DMA((self.n, NDEV - 1)),
                        pltpu.SemaphoreType.DMA((self.n,))]

    def _copies(self, ins, outs, sems):
        send, recv, loc = sems
        x, y, c, me = _position()
        local = [pltpu.make_async_copy(ins[a].at[me] if self.scatter else ins[a], outs[a].at[me], loc.at[a])
                 for a in range(self.n)]
        remote = []
        for k in range(1, NDEV):
            peer, pid = _peer(x, y, c, k)
            for a in range(self.n):
                src = ins[a].at[pid] if self.scatter else ins[a]
                remote.append(pltpu.make_async_remote_copy(
                    src_ref=src, dst_ref=outs[a].at[me], send_sem=send.at[a, k - 1], recv_sem=recv.at[a, k - 1],
                    device_id=peer, device_id_type=MESH))
        return local, remote

    def start(self, ins, outs, sems):
        local, remote = self._copies(ins, outs, sems)
        for cp in local + remote:
            cp.start()

    def finish(self, ins, outs, sems):
        local, remote = self._copies(ins, outs, sems)
        for cp in remote:
            cp.wait()
        for cp in local:
            cp.wait()


class _GatherTwoLevel:
    scatter = False

    def __init__(self, arrs):
        self.arrs, self.n = list(arrs), len(arrs)
        self.out_shape = tuple(_sds((NDEV,) + a.shape, a.dtype) for a in arrs)
        self.scratch = [pltpu.SemaphoreType.DMA((self.n, NDEV - 1)), pltpu.SemaphoreType.DMA((self.n, NDEV - 1)),
                        pltpu.SemaphoreType.DMA((self.n,))]

    def _parts(self, ins, outs, sems):
        send, recv, loc = sems
        x, y, c, _ = _position()
        me, sibling = (x, y, c), (x, y, 1 - c)
        chips = [(1 - x, y), (x, 1 - y), (1 - x, 1 - y)]
        parts = []
        for a in range(self.n):
            slot = lambda px, py, pc, a=a: outs[a].at[4 * px + 2 * py + pc]

            def copy(k, owner, to, src=None, a=a, slot=slot):
                return pltpu.make_async_remote_copy(
                    src_ref=slot(*owner) if src is None else src, dst_ref=slot(*owner), send_sem=send.at[a, k],
                    recv_sem=recv.at[a, k], device_id=to, device_id_type=MESH)

            parts.append(dict(
                mine=pltpu.make_async_copy(ins[a], slot(*me), loc.at[a]),
                first=[copy(0, me, sibling, src=ins[a])] + [copy(1 + j, me, (*ch, c), src=ins[a]) for j, ch in enumerate(chips)],
                arrive=[copy(1 + j, (*ch, c), me) for j, ch in enumerate(chips)],
                passed=[copy(4 + j, (*ch, c), sibling) for j, ch in enumerate(chips)],
                rest=[copy(0, sibling, me)] + [copy(4 + j, (*ch, 1 - c), me) for j, ch in enumerate(chips)]))
        return parts

    def start(self, ins, outs, sems):
        for p in self._parts(ins, outs, sems):
            p["mine"].start()
            for cp in p["first"]:
                cp.start()

    def middle(self, ins, outs, sems):
        for p in self._parts(ins, outs, sems):
            for got, fwd in zip(p["arrive"], p["passed"]):
                got.wait_recv()
                fwd.start()

    def finish(self, ins, outs, sems):
        for p in self._parts(ins, outs, sems):
            for cp in p["rest"]:
                cp.wait_recv()
            for cp in p["first"] + p["passed"]:
                cp.wait_send()
            p["mine"].wait()


def _gather_two_level(blocks, *, name):
    exch = _GatherTwoLevel(blocks)
    n = exch.n

    def body(*refs):
        ins, outs, sems = refs[:n], refs[n:2 * n], refs[2 * n:]
        exch.start(ins, outs, sems)
        exch.middle(ins, outs, sems)
        exch.finish(ins, outs, sems)

    outs = pl.pallas_call(body, name=name, out_shape=exch.out_shape, in_specs=[HBM] * n, out_specs=(HBM,) * n,
                          scratch_shapes=exch.scratch,
                          compiler_params=pltpu.CompilerParams(has_side_effects=True))(*blocks)
    return list(outs)


SEM = pl.BlockSpec(memory_space=pltpu.SEMAPHORE)


SHARD_ROWS = W_END // NDEV
RUNS = ((0, W_QKV, C_KV), (W_QKV, W_AQ - W_QKV, C_QKV), (W_AQ, W_Z - W_AQ, C_AQ), (W_Z, W_END - W_Z, C_Z))


ROW_TILE = 8
SLOT_ROWS = -(-SHARD_ROWS // ROW_TILE) * ROW_TILE


def _shard_pieces(d):
    lo, hi = d * SHARD_ROWS, (d + 1) * SHARD_ROWS
    lead = lo % ROW_TILE
    pieces = []
    for first, rows, padded in RUNS:
        a, b = max(lo, first), min(hi, first + rows)
        if a < b:
            pieces.append([a - lo + lead, b - a, padded + a - first])
    pieces[0] = [0, pieces[0][1] + lead, pieces[0][2] - lead]
    pieces[-1][1] = SLOT_ROWS - pieces[-1][0]
    assert all(v % ROW_TILE == 0 for p in pieces for v in p) and all(p[2] + p[1] <= C_END for p in pieces)
    return pieces


def _scatter_send(src_ref, land_ref, send_sems, recv_sems, cols):
    _, _, _, me = _position()
    for d in range(NDEV):
        @pl.when(me != d)
        def _():
            k = jnp.bitwise_xor(me, d)
            peer = tuple(jnp.int32((d >> s) & 1) for s in (2, 1, 0))
            for off, rows, padded in _shard_pieces(d):
                pltpu.make_async_remote_copy(
                    src_ref=src_ref.at[pl.ds(padded, rows), pl.ds(*cols)],
                    dst_ref=land_ref.at[me].at[pl.ds(off, rows), pl.ds(*cols)], send_sem=send_sems.at[k - 1],
                    recv_sem=recv_sems.at[k - 1], device_id=peer, device_id_type=MESH).start()


def _scatter_whole(src_ref, land_ref, send_sems, recv_sems, cols):
    x, y, c, me = _position()
    span = (slice(None), pl.ds(*cols))
    copies = []
    for k in range(1, NDEV):
        peer, _ = _peer(x, y, c, k)
        copies.append(pltpu.make_async_remote_copy(
            src_ref=src_ref.at[pl.ds(0, SLOT_ROWS)].at[span], dst_ref=land_ref.at[me].at[span],
            send_sem=send_sems.at[k - 1], recv_sem=recv_sems.at[k - 1], device_id=peer, device_id_type=MESH))
    return copies


SPLIT_EFFECT = pltpu.SideEffectType.DATAFLOW_SIDE_EFFECTING


def _scatter_start(parts, land, cols, after, *, name):
    na = len(after)
    if land is None:
        land = lax.empty((NDEV, SLOT_ROWS, D), parts.dtype)

    def body(src_ref, land_ref, *rest):
        send_sems, recv_sems, _, _, token = rest[na:]
        _scatter_send(src_ref, land_ref, send_sems, recv_sems, cols)
        token[...] = jnp.zeros_like(token)

    return pl.pallas_call(
        body, name=name,
        out_shape=(pltpu.SemaphoreType.DMA((NDEV - 1,)), pltpu.SemaphoreType.DMA((NDEV - 1,)),
                   pltpu.HBM(parts.shape, parts.dtype), pltpu.HBM(land.shape, land.dtype), _sds((8, HD))),
        in_specs=(HBM, HBM) + (pl.BlockSpec(memory_space=pl.ANY),) * na,
        out_specs=(SEM, SEM, HBM, HBM, pl.BlockSpec(memory_space=pltpu.VMEM)),
        input_output_aliases={0: 2, 1: 3}, compiler_params=pltpu.CompilerParams(has_side_effects=SPLIT_EFFECT),
    )(pltpu.with_memory_space_constraint(parts, pltpu.HBM), pltpu.with_memory_space_constraint(land, pltpu.HBM), *after)


def _scatter_wait(send_sems, recv_sems, src_thru, land_thru, cols, after, *, name):
    na = len(after)

    def body(src_ref, land_ref, send_sems, recv_sems, *rest):
        for cp in _scatter_whole(src_ref, land_ref, send_sems, recv_sems, cols):
            cp.wait_send()
            cp.wait_recv()

    return pl.pallas_call(
        body, name=name,
        out_shape=(pltpu.HBM(src_thru.shape, src_thru.dtype), pltpu.HBM(land_thru.shape, land_thru.dtype)),
        in_specs=(HBM, HBM, SEM, SEM) + (pl.BlockSpec(memory_space=pl.ANY),) * na, out_specs=(HBM, HBM),
        input_output_aliases={0: 0, 1: 1}, compiler_params=pltpu.CompilerParams(has_side_effects=SPLIT_EFFECT),
    )(src_thru, land_thru, send_sems, recv_sems, *after)


def _cast_bf16(ws, *, name):
    k = len(ws)

    def body(*refs):
        for w_ref, o_ref in zip(refs[:k], refs[k:]):
            o_ref[...] = w_ref[...].astype(BF16)

    return _call(body, name=name, out_shape=tuple(_sds(w.shape, BF16) for w in ws), vmem=VMEM_BIG)(*ws)


def _sum_slots(a, *, name):
    _, R, C = a.shape

    def body(a_ref, o_ref):
        s = a_ref[0]
        for d in range(1, NDEV):
            s = s + a_ref[d]
        o_ref[...] = s

    return _call(body, name=name, out_shape=_sds((R, C)))(a)


MODROWS = 16


def _mod_fwd(c9, w, b):
    cols = w.shape[1]

    def body(c_ref, w_ref, b_ref, o_ref):
        o_ref[...] = _nn(_silu(c_ref[...]), w_ref[...]) + b_ref[...]

    return _call(body, name="mod_fwd", out_shape=_sds((MODROWS, cols)))(c9, w, b)


def _mod_bwd(c9, dmy, dall, w):
    cols = w.shape[1]

    def body(c_ref, dmy_ref, dall_ref, w_ref, gw_ref, gb_ref, cp_ref):
        sc = _silu(c_ref[...])
        rows = lax.broadcasted_iota(jnp.int32, (MODROWS, 1), 0)
        d = dmy_ref[...]
        d_ctx = jnp.where(rows == NDEV, d, 0.0)
        sc_ctx = jnp.where(rows == NDEV, sc, 0.0)
        outer = lax.dot_general(sc_ctx, d_ctx, (((0,), (0,)), ((), ())), precision=HI, preferred_element_type=F32)
        gw_ref[...] = _tn(jnp.where(rows < NDEV, sc, 0.0), jnp.where(rows < NDEV, d, 0.0)) + outer
        gb_ref[...] = jnp.sum(dall_ref[...], axis=0, keepdims=True)
        cp_ref[...] = jnp.sum(_nt(d_ctx, w_ref[...]), axis=0, keepdims=True)

    return _call(body, name="mod_bwd", out_shape=(_sds((D, cols)), _sds((1, 6 * D)), _sds((1, D))),
                 vmem=VMEM_BIG)(c9, dmy, dall, w)


def _cctx_finish(parts, c_ctx, after):
    VM = pl.BlockSpec(memory_space=pltpu.VMEM)

    def body(p_ref, c_ref, *rest):
        o_ref = rest[-1]
        s = p_ref[0]
        for d in range(1, NDEV):
            s = s + p_ref[d]
        _, vjp = jax.vjp(_silu, c_ref[...])
        o_ref[...] = vjp(s)[0]

    return _call(body, name="cctx_finish", out_shape=_sds((1, D)),
                 in_specs=[VM, VM] + [pl.BlockSpec(memory_space=pl.ANY)] * len(after))(parts, c_ctx, *after)


def _adamw_recv(w, recv, m, v, *, name, own=None):
    rows, cols = w.shape
    slot_rows = recv.shape[1]
    lead = slot_rows - rows
    assert rows % ROW_TILE in (0, lead)
    bc = 256
    c1 = 1.0 - B1 ** STEP
    c2 = 1.0 - B2 ** STEP
    has_own = own is not None

    def body(w_ref, r_ref, m_ref, v_ref, *rest):
        g_ref, d_ref, nm_ref, nv_ref = rest[-4:]
        me = _position()[3]

        def slot(d):
            return jnp.where(me == d, rest[0][...], r_ref[d]) if has_own else r_ref[d]

        gv = slot(0).astype(F32)
        for d in range(1, NDEV):
            gv = gv + slot(d).astype(F32)
        if lead:
            gv = jnp.where((me * rows) % ROW_TILE == 0, gv[:rows], gv[lead:])
        nm = B1 * m_ref[...] + (1.0 - B1) * gv
        nv = B2 * v_ref[...] + (1.0 - B2) * (gv * gv)
        g_ref[...] = gv
        d_ref[...] = -LR * ((nm / c1) / (jnp.sqrt(nv / c2) + AEPS) + WD * w_ref[...])
        nm_ref[...] = nm
        nv_ref[...] = nv

    blk = pl.BlockSpec((rows, bc), lambda j: (0, j))
    return _call(body, name=name, out_shape=(_sds((rows, cols)),) * 4, grid=(cols // bc,),
                 in_specs=[blk, pl.BlockSpec((NDEV, slot_rows, bc), lambda j: (0, 0, j)), blk, blk]
                 + [pl.BlockSpec((slot_rows, bc), lambda j: (0, j))] * has_own,
                 out_specs=(blk,) * 4, sem=("parallel",), vmem=VMEM_BIG)(w, recv, m, v, *([own] if has_own else []))


P_LAT, P_CTX, P_FNW, P_FFNB, P_CONV, P_FFNW, P_MISC, P_ROWS = 0, 8, 16, 24, 32, 48, 72, 80


def _rows_of(v, nrows):
    flat = v.reshape(-1)
    return jnp.pad(flat, (0, nrows * D - flat.shape[0])).reshape(nrows, D)


def _by_columns(g):
    n, r, c = g.shape
    return jnp.transpose(g, (1, 0, 2)).reshape(r, n * c)


def kernel(x, c, ctx, c_ctx, w_mod, b_mod, w_in, q_norm_w, k_norm_w, conv_qkv_w, a_log, dt_bias, gdn_norm_w, w_pa, w_pd, w_out, w_up, ffn_conv_w, ffn_conv_b, w_down, final_norm_w, loss_target, m_c_ctx, m_w_mod, m_b_mod, m_w_in, m_q_norm_w, m_k_norm_w, m_conv_qkv_w, m_a_log, m_dt_bias, m_gdn_norm_w, m_w_pa, m_w_pd, m_w_out, m_w_up, m_ffn_conv_w, m_ffn_conv_b, m_w_down, m_final_norm_w, v_c_ctx, v_w_mod, v_b_mod, v_w_in, v_q_norm_w, v_k_norm_w, v_conv_qkv_w, v_a_log, v_dt_bias, v_gdn_norm_w, v_w_pa, v_w_pd, v_w_out, v_w_up, v_ffn_conv_w, v_ffn_conv_b, v_w_down, v_final_norm_w):
    _, _, _, me = _position()
    mcols = w_mod.shape[2]

    transposed = ("w_in", "w_up")
    big = {"w_in": w_in[0].T, "w_pa": w_pa[0], "w_pd": w_pd[0], "w_out": w_out[0], "w_up": w_up[0].T, "w_down": w_down[0]}
    names = list(big)
    shards = dict(zip(names, _cast_bf16([big[n] for n in names], name="cast_weights")))
    w_in_g, c_all, conv_g, ffnw_g = _gather_two_level([shards["w_in"], c, conv_qkv_w[0], ffn_conv_w[0]],
                                                      name="gather_w_in")
    w_in_full = w_in_g.reshape(W_END, D)
    w_in_pad = _pad_columns(w_in_full)

    c9 = jnp.concatenate([c_all.reshape(NDEV, D), jnp.pad(c_ctx[None], ((0, MODROWS - NDEV - 1), (0, 0)))], axis=0)
    b_loc = lax.dynamic_slice(b_mod, (0, me * mcols), (1, mcols))
    mod_all, = _exchange([_mod_fwd(c9, w_mod[0], b_loc)], name="gather_mod", scatter=False)
    mod_lat = lax.dynamic_index_in_dim(mod_all, me, axis=1, keepdims=False).reshape(6, D)
    mod_ctx = mod_all[:, NDEV, :].reshape(6, D)

    small = {"q_norm_w": q_norm_w, "k_norm_w": k_norm_w, "gdn_norm_w": gdn_norm_w, "a_log": a_log, "dt_bias": dt_bias,
             "conv_qkv_w": _by_columns(conv_g), "ffn_conv_w": _by_columns(ffnw_g), "ffn_conv_b": ffn_conv_b,
             "final_norm_w": final_norm_w[None]}
    loss_me, grad_x, pending_in, recv, dmod_lat, dmod_ctx, gs = _local_step(
        x[0], ctx[0], loss_target[0], mod_lat, mod_ctx, w_in_pad, shards, small)

    moments = {"w_in": (m_w_in, v_w_in), "w_pa": (m_w_pa, v_w_pa), "w_pd": (m_w_pd, v_w_pd),
               "w_out": (m_w_out, v_w_out), "w_up": (m_w_up, v_w_up), "w_down": (m_w_down, v_w_down)}
    res = {}
    def finish(n, outs):
        return tuple((t.T if n in transposed else t)[None] for t in outs)

    def moment(t, n):
        return t[0].T if n in transposed else t[0]

    for n in recv:
        res[n] = finish(n, _adamw_recv(big[n], recv[n], moment(moments[n][0], n), moment(moments[n][1], n),
                                       name="adamw_" + n))

    misc = jnp.concatenate([gs["q_norm_w"][0], gs["k_norm_w"][0], gs["gdn_norm_w"][0], gs["a_log"], gs["dt_bias"],
                            loss_me[None]])
    pack = jnp.concatenate([_rows_of(dmod_lat, P_CTX - P_LAT), _rows_of(dmod_ctx, P_FNW - P_CTX),
                            _rows_of(gs["final_norm_w"], P_FFNB - P_FNW), _rows_of(gs["ffn_conv_b"], P_CONV - P_FFNB),
                            _rows_of(gs["conv_qkv_w"], P_FFNW - P_CONV), _rows_of(gs["ffn_conv_w"], P_MISC - P_FFNW),
                            _rows_of(misc, P_ROWS - P_MISC)], axis=0)
    pack_all, = _exchange([pack], name="gather_pack", scatter=False)
    tot = _sum_slots(pack_all, name="sum_pack")
    dall = jnp.concatenate([pack_all[:, P_LAT:P_LAT + 6, :].reshape(NDEV, 6 * D),
                            jnp.pad(tot[P_CTX:P_CTX + 6].reshape(1, 6 * D), ((0, MODROWS - NDEV - 1), (0, 0)))], axis=0)
    dmy = lax.dynamic_slice(dall, (0, me * mcols), (MODROWS, mcols))
    g_w_mod, g_b_mod, cpart = _mod_bwd(c9, dmy, dall, w_mod[0])
    cparts, = _exchange([cpart], name="gather_cctx", scatter=False)
    sems_a, land = pending_in[:2], pending_in[3]
    *sems_b, g_in_thru, land, token_b = _scatter_start(pending_in[2], land, (D // 2, D // 2), (cparts,),
                                                       name="scatter_g_in_b_start")
    g_c_ctx = _cctx_finish(cparts, c_ctx[None], (token_b,))[0]

    nconv, nffn = 3 * GH * HD, 2 * DFF
    conv_tot = tot[P_CONV:P_FFNW].reshape(-1)[:3 * nconv].reshape(3, nconv)
    ffnw_tot = tot[P_FFNW:P_MISC].reshape(-1)[:3 * nffn].reshape(3, nffn)
    mrow = tot[P_MISC]
    grads = {
        "c_ctx": g_c_ctx, "w_mod": g_w_mod[None], "b_mod": g_b_mod,
        "q_norm_w": mrow[None, 0:HD], "k_norm_w": mrow[None, HD:2 * HD], "gdn_norm_w": mrow[None, 2 * HD:3 * HD],
        "conv_qkv_w": lax.dynamic_slice(conv_tot, (0, me * (nconv // NDEV)), (3, nconv // NDEV))[None],
        "a_log": mrow[3 * HD:3 * HD + 2 * GH].reshape(1, 2, GH),
        "dt_bias": mrow[3 * HD + 2 * GH:3 * HD + 4 * GH].reshape(1, 2, GH),
        "ffn_conv_w": lax.dynamic_slice(ffnw_tot, (0, me * (nffn // NDEV)), (3, nffn // NDEV))[None],
        "ffn_conv_b": tot[P_FFNB:P_CONV].reshape(-1)[:nffn][None],
        "final_norm_w": tot[P_FNW],
    }
    loss = mrow[3 * HD + 4 * GH]
    given = {"c_ctx": (c_ctx, m_c_ctx, v_c_ctx), "w_mod": (w_mod, m_w_mod, v_w_mod), "b_mod": (b_mod, m_b_mod, v_b_mod),
             "q_norm_w": (q_norm_w, m_q_norm_w, v_q_norm_w), "k_norm_w": (k_norm_w, m_k_norm_w, v_k_norm_w),
             "conv_qkv_w": (conv_qkv_w, m_conv_qkv_w, v_conv_qkv_w), "a_log": (a_log, m_a_log, v_a_log),
             "dt_bias": (dt_bias, m_dt_bias, v_dt_bias), "gdn_norm_w": (gdn_norm_w, m_gdn_norm_w, v_gdn_norm_w),
             "ffn_conv_w": (ffn_conv_w, m_ffn_conv_w, v_ffn_conv_w), "ffn_conv_b": (ffn_conv_b, m_ffn_conv_b, v_ffn_conv_b),
             "final_norm_w": (final_norm_w, m_final_norm_w, v_final_norm_w)}
    res["w_mod"] = (grads["w_mod"],) + _adamw(w_mod, grads["w_mod"], m_w_mod, v_w_mod, name="adamw_w_mod")
    small_names = [n for n in given if n != "w_mod"]
    updates = _adamw_many([(given[n][0], grads[n], given[n][1], given[n][2]) for n in small_names], name="adamw_small")
    for n, upd in zip(small_names, updates):
        res[n] = (grads[n],) + upd

    first = me * SHARD_ROWS
    own_in = lax.dynamic_slice(_unpad_columns(g_in_thru), (first - first % ROW_TILE, 0), (SLOT_ROWS, D))
    mine = (big["w_in"], moment(m_w_in, "w_in"), moment(v_w_in, "w_in"))
    g_in_thru, land = _scatter_wait(*sems_a, g_in_thru, land, (0, D // 2), [res[n][1] for n in res] + [own_in, *mine],
                                    name="scatter_g_in_a_wait")
    _, land = _scatter_wait(*sems_b, g_in_thru, land, (D // 2, D // 2), (), name="scatter_g_in_b_wait")
    res["w_in"] = finish("w_in", _adamw_recv(mine[0], land, mine[1], mine[2], name="adamw_w_in", own=own_in))

    order = ["c_ctx", "w_mod", "b_mod", "w_in", "q_norm_w", "k_norm_w", "conv_qkv_w", "a_log", "dt_bias", "gdn_norm_w",
             "w_pa", "w_pd", "w_out", "w_up", "ffn_conv_w", "ffn_conv_b", "w_down", "final_norm_w"]
    return (loss, grad_x[None], *[res[n][0] for n in order], *[res[n][1] for n in order],
            *[res[n][2] for n in order], *[res[n][3] for n in order])
```

```python
import functools
import math

import jax
import jax.numpy as jnp
from jax import lax
from jax.experimental import pallas as pl
from jax.experimental.pallas import tpu as pltpu

F32 = jnp.float32
BF16 = jnp.bfloat16
HI = lax.Precision.HIGHEST
MESH = pl.DeviceIdType.MESH

NDEV = 8
D = 1024
HD = 128
AH, AKV, GRP = 8, 2, 4
GH = 8
CH = 64
DFF = 2816
GRID_W = 64
EPS = 1e-6
ROPE_THETA = 10000.0
LOG2E = math.log2(math.e)
C_KV, C_AQ, C_QKV, C_BL, C_Z, C_GATE, C_END = 0, 512, 1536, 4608, 5120, 6144, 8192
W_QKV, W_AQ, W_Z, W_END = 512, 3616, 4640, 7712


def _pad_columns(w):
    zeros = jnp.zeros((C_Z - C_QKV - (W_AQ - W_QKV), D), w.dtype)
    return jnp.concatenate([w[:W_QKV], w[W_AQ:W_Z], w[W_QKV:W_AQ], zeros, w[W_Z:]], axis=0)


def _unpad_columns(g):
    return jnp.concatenate([g[:C_AQ], g[C_QKV:C_QKV + W_AQ - W_QKV], g[C_AQ:C_QKV], g[C_Z:]], axis=0)
LR, B1, B2, AEPS, WD, STEP = 0.001, 0.9, 0.999, 1e-08, 0.01, 10
VMEM_BIG = 56 * 1024 * 1024
INTRA_FWD_CHUNKS = 36
INTRA_BWD_CHUNKS = 36


def _call(body, *, name, out_shape, grid=None, in_specs=None, out_specs=None, scratch=(), sem=None,
          vmem=None, aliases=None):
    params = {}
    if sem is not None:
        params["dimension_semantics"] = sem
    if vmem is not None:
        params["vmem_limit_bytes"] = vmem
    kw = {}
    if grid is not None:
        kw["grid"] = grid
    if in_specs is not None:
        kw["in_specs"] = in_specs
    if out_specs is not None:
        kw["out_specs"] = out_specs
    if aliases:
        kw["input_output_aliases"] = aliases
    return pl.pallas_call(body, name=name, out_shape=out_shape, scratch_shapes=list(scratch),
                          compiler_params=pltpu.CompilerParams(**params), **kw)


def _call_carrying(body, exch, *, name, out_shape, grid, in_specs, out_specs, scratch=(), vmem=None):
    n, nin, nout, nscr = exch.n, len(in_specs), len(out_shape), len(scratch)
    steps = math.prod(grid)
    mid = (2 * steps) // 3

    def wrapped(*refs):
        ins, cins = refs[:nin], refs[nin:nin + n]
        outs, couts = refs[nin + n:nin + n + nout], refs[nin + n + nout:nin + 2 * n + nout]
        scr, sems = refs[nin + 2 * n + nout:nin + 2 * n + nout + nscr], refs[nin + 2 * n + nout + nscr:]
        ids = [pl.program_id(i) for i in range(len(grid))]
        first = functools.reduce(jnp.logical_and, [i == 0 for i in ids])
        last = functools.reduce(jnp.logical_and, [i == g - 1 for i, g in zip(ids, grid)])

        @pl.when(first)
        def _():
            exch.start(cins, couts, sems)

        if hasattr(exch, "middle"):
            linear = functools.reduce(lambda acc, ig: acc * ig[1] + ig[0], zip(ids, grid), 0)

            @pl.when(linear == mid)
            def _():
                exch.middle(cins, couts, sems)

        body(*ins, *outs, *scr)

        @pl.when(last)
        def _():
            exch.finish(cins, couts, sems)

    params = {"dimension_semantics": ("arbitrary",) * len(grid)}
    if vmem is not None:
        params["vmem_limit_bytes"] = vmem
    fn = pl.pallas_call(wrapped, name=name, out_shape=tuple(out_shape) + exch.out_shape, grid=grid,
                        in_specs=list(in_specs) + [HBM] * n, out_specs=tuple(out_specs) + (HBM,) * n,
                        scratch_shapes=list(scratch) + exch.scratch, compiler_params=pltpu.CompilerParams(**params))

    def run(*args):
        res = fn(*args, *exch.arrs)
        return res[:nout], list(res[nout:])

    return run


def _sds(shape, dtype=F32):
    return jax.ShapeDtypeStruct(tuple(shape), dtype)


def _dot(a, b, ca, cb):
    return lax.dot_general(a.astype(BF16), b.astype(BF16), (((ca,), (cb,)), ((), ())),
                           preferred_element_type=F32)


@jax.custom_vjp
def _nn(a, b):
    return _dot(a, b, 1, 0)


@jax.custom_vjp
def _nt(a, b):
    return _dot(a, b, 1, 1)


@jax.custom_vjp
def _tn(a, b):
    return _dot(a, b, 0, 0)


_nn.defvjp(lambda a, b: (_nn(a, b), (a, b)), lambda r, g: (_nt(g, r[1]), _tn(r[0], g)))
_nt.defvjp(lambda a, b: (_nt(a, b), (a, b)), lambda r, g: (_nn(g, r[1]), _tn(g, r[0])))
_tn.defvjp(lambda a, b: (_tn(a, b), (a, b)), lambda r, g: (_nt(r[1], g), _nn(r[0], g)))


def _mdot(a, b):
    return jnp.dot(a, b, precision=lax.Precision.HIGH, preferred_element_type=F32)


def _maskdot(mask, a, cm):
    hi = a.astype(BF16)
    r = a - hi.astype(F32)
    mid = r.astype(BF16)
    lo = (r - mid.astype(F32)).astype(BF16)
    mb = mask.astype(BF16)
    dims = (((cm,), (0,)), ((), ()))
    return (lax.dot_general(mb, hi, dims, preferred_element_type=F32)
            + lax.dot_general(mb, mid, dims, preferred_element_type=F32)
            + lax.dot_general(mb, lo, dims, preferred_element_type=F32))


@jax.custom_vjp
def _mask_nn(mask, a):
    return _maskdot(mask, a, 1)


_mask_nn.defvjp(lambda mask, a: (_maskdot(mask, a, 1), mask),
                lambda mask, g: (jnp.zeros_like(mask), _maskdot(mask, g, 0)))


@jax.custom_vjp
def _saved_inverse(lmat, x):
    return x


def _saved_inverse_bwd(x, g):
    t = lax.dot_general(x, g, (((0,), (0,)), ((), ())), precision=lax.Precision.HIGH, preferred_element_type=F32)
    dl = lax.dot_general(t, x, (((1,), (1,)), ((), ())), precision=lax.Precision.HIGH, preferred_element_type=F32)
    return -dl, jnp.zeros_like(x)


_saved_inverse.defvjp(lambda lmat, x: (x, x), _saved_inverse_bwd)


def _row_ids(shape):
    return lax.broadcasted_iota(jnp.int32, shape, 0)


def _shift_rows(x, down, bounds):
    n = x.shape[0]
    rows = _row_ids(x.shape)
    y = pltpu.roll(x, 1 if down else n - 1, 0)
    edge = functools.reduce(jnp.logical_or, [rows == (s if down else e - 1) for s, e in bounds])
    return jnp.where(edge, 0.0, y)


def _make_shift(bounds):
    @jax.custom_vjp
    def down(x):
        return _shift_rows(x, True, bounds)

    @jax.custom_vjp
    def up(x):
        return _shift_rows(x, False, bounds)

    down.defvjp(lambda x: (down(x), None), lambda _, g: (up(g),))
    up.defvjp(lambda x: (up(x), None), lambda _, g: (down(g),))
    return down, up


@jax.custom_vjp
def _swap32(x):
    lane = lax.broadcasted_iota(jnp.int32, x.shape, x.ndim - 1)
    return jnp.where((lane % 64) < 32, pltpu.roll(x, HD - 32, x.ndim - 1), pltpu.roll(x, 32, x.ndim - 1))


_swap32.defvjp(lambda x: (_swap32(x), None), lambda _, g: (_swap32(g),))


def _rms(x):
    return x * lax.rsqrt(jnp.mean(x * x, axis=-1, keepdims=True) + EPS)


def _silu(x):
    return x * jax.nn.sigmoid(x)


def _mm(a, b, *, name, M, N, K, ta=False, tb=False, out_dtype=F32, bm=None, bn=None, bk=None, after=()):
    bm, bn, bk = bm or M, bn or N, bk or K
    assert M % bm == 0 and N % bn == 0 and K % bk == 0, (name, M, N, K, bm, bn, bk)
    nk = K // bk
    ca, cb = (0 if ta else 1), (1 if tb else 0)
    na = len(after)

    def body(a_ref, b_ref, *rest):
        o_ref, acc = rest[na], rest[na + 1:]
        r = _dot(a_ref[...], b_ref[...], ca, cb)
        if nk == 1:
            o_ref[...] = r.astype(out_dtype)
        else:
            acc_ref, = acc
            k = pl.program_id(2)

            @pl.when(k == 0)
            def _():
                acc_ref[...] = r

            @pl.when(k > 0)
            def _():
                acc_ref[...] += r

            @pl.when(k == nk - 1)
            def _():
                o_ref[...] = acc_ref[...].astype(out_dtype)

    a_spec = pl.BlockSpec((bk, bm), lambda i, j, k: (k, i)) if ta else pl.BlockSpec((bm, bk), lambda i, j, k: (i, k))
    b_spec = pl.BlockSpec((bn, bk), lambda i, j, k: (j, k)) if tb else pl.BlockSpec((bk, bn), lambda i, j, k: (k, j))
    return _call(body, name=name, out_shape=_sds((M, N), out_dtype), grid=(M // bm, N // bn, nk),
                 in_specs=[a_spec, b_spec] + [pl.BlockSpec(memory_space=pl.ANY)] * na,
                 out_specs=pl.BlockSpec((bm, bn), lambda i, j, k: (i, j)),
                 scratch=[pltpu.VMEM((bm, bn), F32)] if nk > 1 else [],
                 sem=("parallel", "parallel", "arbitrary"), vmem=VMEM_BIG)(a, b, *after)


def _normmod_fn(x, sh, sc):
    return _rms(x) * (1.0 + sc) + sh


def _normmod_fwd(x, mod, i_sh, i_sc, *, name, br=256, out_rows=None, off=0, into=None):
    R = x.shape[0]
    ob = off // br
    given = [] if into is None else [into]
    if given:
        out_rows = into.shape[0]

    def body(x_ref, mod_ref, *rest):
        rest[-1][...] = _normmod_fn(x_ref[...], mod_ref[i_sh:i_sh + 1, :], mod_ref[i_sc:i_sc + 1, :]).astype(BF16)

    return _call(body, name=name, out_shape=_sds((out_rows or R, D), BF16), grid=(R // br,),
                 in_specs=[pl.BlockSpec((br, D), lambda i: (i, 0)), pl.BlockSpec((6, D), lambda i: (0, 0))]
                 + [ANYSPEC] * len(given), out_specs=pl.BlockSpec((br, D), lambda i: (i + ob, 0)),
                 aliases={2: 0} if given else None, sem=("parallel",))(x, mod, *given)


def _normmod_bwd(x, mod, i_sh, i_sc, dh, dh_off, res, *, name, br=256):
    R = x.shape[0]
    ob = dh_off // br
    has_res = res is not None

    def body(x_ref, mod_ref, dh_ref, *rest):
        if has_res:
            res_ref, dx_ref, dsh_ref, dsc_ref = rest
        else:
            dx_ref, dsh_ref, dsc_ref = rest
        sh, sc = mod_ref[i_sh:i_sh + 1, :], mod_ref[i_sc:i_sc + 1, :]
        _, vjp = jax.vjp(_normmod_fn, x_ref[...], sh, sc)
        dx, dsh, dsc = vjp(dh_ref[...])
        dx_ref[...] = dx + res_ref[...] if has_res else dx

        @pl.when(pl.program_id(0) == 0)
        def _():
            dsh_ref[...] = jnp.zeros_like(dsh_ref)
            dsc_ref[...] = jnp.zeros_like(dsc_ref)

        dsh_ref[...] += dsh
        dsc_ref[...] += dsc

    row = pl.BlockSpec((br, D), lambda i: (i, 0))
    vec = pl.BlockSpec((1, D), lambda i: (0, 0))
    ins = [row, pl.BlockSpec((6, D), lambda i: (0, 0)), pl.BlockSpec((br, D), lambda i: (i + ob, 0))]
    args = [x, mod, dh]
    if has_res:
        ins.append(row)
        args.append(res)
    return _call(body, name=name, out_shape=(_sds((R, D)), _sds((1, D)), _sds((1, D))), grid=(R // br,),
                 in_specs=ins, out_specs=(row, vec, vec), sem=("arbitrary",))(*args)


def _rope(x, cos, sin):
    return x * cos + _swap32(x) * sin


def _aprep_fn(qs, ks, cos, sin, qw, kw):
    return ([_rope(_rms(q) * qw, cos, sin) for q in qs], [_rope(_rms(k) * kw, cos, sin) for k in ks])


def _aprep_fwd(proj, cos, sin, qw, kw, *, br=256):
    T = proj.shape[0]

    def body(x_ref, cos_ref, sin_ref, qw_ref, kw_ref, q_ref, k_ref, v_ref):
        qs = [x_ref[:, C_AQ + h * HD:C_AQ + (h + 1) * HD] for h in range(AH)]
        ks = [x_ref[:, h * HD:(h + 1) * HD] for h in range(AKV)]
        qo, ko = _aprep_fn(qs, ks, cos_ref[...], sin_ref[...], qw_ref[...], kw_ref[...])
        for h in range(AH):
            q_ref[h] = qo[h].astype(BF16)
        for h in range(AKV):
            k_ref[h] = ko[h].astype(BF16)
            v_ref[h] = x_ref[:, (AKV + h) * HD:(AKV + h + 1) * HD].astype(BF16)

    tab = pl.BlockSpec((br, HD), lambda i: (i, 0))
    vec = pl.BlockSpec((1, HD), lambda i: (0, 0))
    return _call(body, name="aprep_fwd",
                 out_shape=(_sds((AH, T, HD), BF16), _sds((AKV, T, HD), BF16), _sds((AKV, T, HD), BF16)),
                 grid=(T // br,),
                 in_specs=[pl.BlockSpec((br, C_QKV), lambda i: (i, 0)), tab, tab, vec, vec],
                 out_specs=(pl.BlockSpec((AH, br, HD), lambda i: (0, i, 0)),
                            pl.BlockSpec((AKV, br, HD), lambda i: (0, i, 0)),
                            pl.BlockSpec((AKV, br, HD), lambda i: (0, i, 0))),
                 sem=("parallel",))(proj, cos, sin, qw, kw)


def _aprep_bwd(proj, cos, sin, qw, kw, dq, dk, dv, dproj, L, *, br=256):
    T = proj.shape[0]
    lb = L // br

    def body(x_ref, cos_ref, sin_ref, qw_ref, kw_ref, dq_ref, dk_ref, dv_ref, _, dx_ref, dqw_ref, dkw_ref):
        i = pl.program_id(0)
        qs = [x_ref[:, C_AQ + h * HD:C_AQ + (h + 1) * HD] for h in range(AH)]
        ks = [x_ref[:, h * HD:(h + 1) * HD] for h in range(AKV)]
        _, vjp = jax.vjp(_aprep_fn, qs, ks, cos_ref[...], sin_ref[...], qw_ref[...], kw_ref[...])
        is_lat = i >= lb
        dqs = [jnp.where(is_lat, dq_ref[h], 0.0) for h in range(AH)]
        dks = [dk_ref[h] for h in range(AKV)]
        gq, gk, _, _, gqw, gkw = vjp((dqs, dks))
        for h in range(AH):
            dx_ref[:, C_AQ + h * HD:C_AQ + (h + 1) * HD] = gq[h].astype(BF16)
        for h in range(AKV):
            dx_ref[:, h * HD:(h + 1) * HD] = gk[h].astype(BF16)
            dx_ref[:, (AKV + h) * HD:(AKV + h + 1) * HD] = dv_ref[h].astype(BF16)

        @pl.when(i == 0)
        def _():
            dqw_ref[...] = jnp.zeros_like(dqw_ref)
            dkw_ref[...] = jnp.zeros_like(dkw_ref)

        dqw_ref[...] += gqw
        dkw_ref[...] += gkw

    tab = pl.BlockSpec((br, HD), lambda i: (i, 0))
    vec = pl.BlockSpec((1, HD), lambda i: (0, 0))
    kvb = pl.BlockSpec((AKV, br, HD), lambda i: (0, i, 0))
    blk = pl.BlockSpec((br, C_QKV), lambda i: (i, 0))
    return _call(body, name="aprep_bwd", out_shape=(_sds(dproj.shape, BF16), _sds((1, HD)), _sds((1, HD))),
                 grid=(T // br,),
                 in_specs=[blk, tab, tab, vec, vec,
                           pl.BlockSpec((AH, br, HD), lambda i: (0, jnp.maximum(i - lb, 0), 0)), kvb, kvb, ANYSPEC],
                 out_specs=(blk, vec, vec), aliases={8: 0},
                 sem=("arbitrary",))(proj, cos, sin, qw, kw, dq, dk, dv, dproj)


def _attn_grad(q, k, v, o, lse2, do):
    scale = HD ** -0.5
    p = jnp.exp2(_dot(q, k, 1, 1) * (scale * LOG2E) - lse2)
    dp = _dot(do, v, 1, 1)
    ds = p * (dp - jnp.sum(do * o, axis=-1, keepdims=True)) * scale
    return _dot(ds, k, 1, 0), _dot(ds, q, 0, 0), _dot(p, do, 0, 0)


ATTN_KEYS = 256


def _attn_fwd(q, k, v, L, exch, *, bq=128):
    T = q.shape[1]
    N = T - L
    lb = L // bq
    assert T % ATTN_KEYS == 0
    scale = HD ** -0.5
    heads = range(GRP)

    def body(q_ref, k_ref, v_ref, o_ref, o32_ref, lse_ref):
        qs = [q_ref[g] for g in heads]
        m = [jnp.full((bq, 1), -jnp.inf, F32) for _ in heads]
        l = [jnp.zeros((bq, 1), F32) for _ in heads]
        acc = [jnp.zeros((bq, HD), F32) for _ in heads]
        for c in range(T // ATTN_KEYS):
            kc, vc = k_ref[c * ATTN_KEYS:(c + 1) * ATTN_KEYS, :], v_ref[c * ATTN_KEYS:(c + 1) * ATTN_KEYS, :]
            s = [_dot(qs[g], kc, 1, 1) * (scale * LOG2E) for g in heads]
            m_new = [jnp.maximum(m[g], jnp.max(s[g], axis=-1, keepdims=True)) for g in heads]
            alpha = [jnp.exp2(m[g] - m_new[g]) for g in heads]
            p = [jnp.exp2(s[g] - m_new[g]) for g in heads]
            l = [l[g] * alpha[g] + jnp.sum(p[g], axis=-1, keepdims=True) for g in heads]
            acc = [acc[g] * alpha[g] + _dot(p[g], vc, 1, 0) for g in heads]
            m = m_new
        for g in heads:
            o = acc[g] / l[g]
            o_ref[:, g * HD:(g + 1) * HD] = o.astype(BF16)
            o32_ref[:, g * HD:(g + 1) * HD] = o
            lse_ref[g] = jnp.broadcast_to(m[g] + jnp.log2(l[g]), (bq, HD))

    kvb = pl.BlockSpec((None, T, HD), lambda g, i: (g, 0, 0))
    ob = pl.BlockSpec((bq, GRP * HD), lambda g, i: (i, g))
    return _call_carrying(
        body, exch, name="attn_fwd",
        out_shape=(_sds((N, AH * HD), BF16), _sds((N, AH * HD)), _sds((AH, N, HD))), grid=(AKV, N // bq),
        in_specs=[pl.BlockSpec((GRP, bq, HD), lambda g, i: (g, i + lb, 0)), kvb, kvb],
        out_specs=(ob, ob, pl.BlockSpec((GRP, bq, HD), lambda g, i: (g, i, 0))), vmem=VMEM_BIG)(q, k, v)


def _attn_bwd(q, k, v, o32, lse, do, L, exch, *, bq=128):
    T = q.shape[1]
    N = T - L
    lb = L // bq

    def body(q_ref, k_ref, v_ref, o_ref, lse_ref, do_ref, dq_ref, dk_ref, dv_ref):
        rows = lambda r: jnp.concatenate([r[:, g * HD:(g + 1) * HD] for g in range(GRP)], axis=0)
        lse = jnp.max(lse_ref[...].reshape(GRP * bq, HD), axis=-1, keepdims=True)
        dq, dk, dv = _attn_grad(q_ref[...].reshape(GRP * bq, HD), k_ref[...], v_ref[...], rows(o_ref), lse, rows(do_ref))
        dq_ref[...] = dq.reshape(GRP, bq, HD)

        @pl.when(pl.program_id(1) == 0)
        def _():
            dk_ref[...] = jnp.zeros_like(dk_ref)
            dv_ref[...] = jnp.zeros_like(dv_ref)

        dk_ref[...] += dk
        dv_ref[...] += dv

    kvb = pl.BlockSpec((None, T, HD), lambda g, i: (g, 0, 0))
    qb = pl.BlockSpec((GRP, bq, HD), lambda g, i: (g, i + lb, 0))
    hb = pl.BlockSpec((GRP, bq, HD), lambda g, i: (g, i, 0))
    ob = pl.BlockSpec((bq, GRP * HD), lambda g, i: (i, g))
    return _call_carrying(body, exch, name="attn_bwd",
                          out_shape=(_sds((AH, N, HD)), _sds((AKV, T, HD)), _sds((AKV, T, HD))), grid=(AKV, N // bq),
                          in_specs=[qb, kvb, kvb, ob, hb, ob], out_specs=(hb, kvb, kvb),
                          vmem=VMEM_BIG)(q, k, v, o32, lse, do)


def _gprep_fn(kind, shifts, x, w):
    down, up = shifts
    y = down(x) * w[0:1, :] + x * w[1:2, :] + up(x) * w[2:3, :]
    a = _silu(y)
    if kind == 2:
        return a
    a = a * lax.rsqrt(jnp.sum(a * a, axis=-1, keepdims=True) + EPS)
    return a * (HD ** -0.5) if kind == 0 else a


def _gprep_fwd(proj, conv_w, kind, bounds):
    T = proj.shape[0]
    shifts = _make_shift(bounds)
    cb = C_QKV // HD + kind * GH

    def body(x_ref, w_ref, o_ref):
        o_ref[...] = _gprep_fn(kind, shifts, x_ref[...], w_ref[...])

    return _call(body, name=f"gprep_fwd{kind}", out_shape=_sds((GH, T, HD)), grid=(GH,),
                 in_specs=[pl.BlockSpec((T, HD), lambda h: (0, cb + h)),
                           pl.BlockSpec((3, HD), lambda h: (0, kind * GH + h))],
                 out_specs=pl.BlockSpec((None, T, HD), lambda h: (h, 0, 0)), sem=("parallel",))(proj, conv_w)


def _gprep_bwd(proj, conv_w, kind, bounds, dy, dproj):
    T = proj.shape[0]
    shifts = _make_shift(bounds)
    cb = C_QKV // HD + kind * GH

    def body(x_ref, w_ref, dy_ref, _, dx_ref, dw_ref):
        _, vjp = jax.vjp(functools.partial(_gprep_fn, kind, shifts), x_ref[...], w_ref[...])
        dx, dw = vjp(dy_ref[0] + dy_ref[1])
        dx_ref[...] = dx.astype(BF16)
        dw_ref[...] = dw

    return _call(body, name=f"gprep_bwd{kind}", out_shape=(_sds(dproj.shape, BF16), _sds((3, GH * HD))), grid=(GH,),
                 in_specs=[pl.BlockSpec((T, HD), lambda h: (0, cb + h)),
                           pl.BlockSpec((3, HD), lambda h: (0, kind * GH + h)),
                           pl.BlockSpec((2, None, T, HD), lambda h: (0, h, 0, 0)), ANYSPEC],
                 out_specs=(pl.BlockSpec((T, HD), lambda h: (0, cb + h)), pl.BlockSpec((3, HD), lambda h: (0, h))),
                 aliases={3: 0}, sem=("parallel",))(proj, conv_w, dy, dproj)


def _bl_fn(x, alog, dtb):
    lane = lax.broadcasted_iota(jnp.int32, x.shape, 1)
    beta = jax.nn.sigmoid(x)
    z = x + dtb
    sp = jnp.maximum(z, 0.0) + jnp.log1p(jnp.exp(-jnp.abs(z)))
    la = -jnp.exp(alog) * sp
    return jnp.where(lane < 2 * GH, beta, jnp.where(lane < 4 * GH, la, 0.0))


def _bl_fwd(proj, alog, dtb, *, br=256):
    T = proj.shape[0]

    def body(x_ref, a_ref, d_ref, o_ref):
        o_ref[...] = _bl_fn(x_ref[...], a_ref[...], d_ref[...])

    vec = pl.BlockSpec((1, HD), lambda i: (0, 0))
    return _call(body, name="bl_fwd", out_shape=_sds((T, HD)), grid=(T // br,),
                 in_specs=[pl.BlockSpec((br, HD), lambda i: (i, C_BL // HD)), vec, vec],
                 out_specs=pl.BlockSpec((br, HD), lambda i: (i, 0)), sem=("parallel",))(proj, alog, dtb)


def _bl_bwd(proj, alog, dtb, dbl, dproj, *, br=256):
    T = proj.shape[0]
    wide = C_Z - C_BL

    def body(x_ref, a_ref, d_ref, g_ref, _, dx_ref, da_ref, dd_ref):
        g = g_ref[0, 0]
        for d in range(2):
            for h in range(GH):
                if d or h:
                    g = g + g_ref[d, h]
        _, vjp = jax.vjp(_bl_fn, x_ref[...], a_ref[...], d_ref[...])
        dx, da, dd = vjp(g)
        dx_ref[:, :HD] = dx.astype(BF16)
        dx_ref[:, HD:] = jnp.zeros((br, wide - HD), BF16)

        @pl.when(pl.program_id(0) == 0)
        def _():
            da_ref[...] = jnp.zeros_like(da_ref)
            dd_ref[...] = jnp.zeros_like(dd_ref)

        da_ref[...] += da
        dd_ref[...] += dd

    vec = pl.BlockSpec((1, HD), lambda i: (0, 0))
    return _call(body, name="bl_bwd", out_shape=(_sds(dproj.shape, BF16), _sds((1, HD)), _sds((1, HD))), grid=(T // br,),
                 in_specs=[pl.BlockSpec((br, HD), lambda i: (i, C_BL // HD)), vec, vec,
                           pl.BlockSpec((2, GH, br, HD), lambda i: (0, 0, i, 0)), ANYSPEC],
                 out_specs=(pl.BlockSpec((br, wide), lambda i: (i, C_BL // wide)), vec, vec), aliases={4: 0},
                 sem=("arbitrary",))(proj, alog, dtb, dbl, dproj)


def _chunk_masks(d):
    ii = lax.broadcasted_iota(jnp.int32, (CH, CH), 0)
    jj = lax.broadcasted_iota(jnp.int32, (CH, CH), 1)
    eye = (ii == jj).astype(F32)
    before = jnp.where(d == 0, (jj < ii).astype(F32), (jj > ii).astype(F32))
    return before, before + eye, eye


def _same_block(b):
    ii = lax.broadcasted_iota(jnp.int32, (CH, CH), 0)
    jj = lax.broadcasted_iota(jnp.int32, (CH, CH), 1)
    shift = b.bit_length() - 1
    return (jnp.right_shift(ii, shift) == jnp.right_shift(jj, shift)).astype(F32)


def _intra_fn(masks, sel_b, sel_l, qs, ks, vs, bls, xs=None):
    before, ateq, eye = masks
    inc = ateq > 0.0
    each = lambda f, *ls: [f(*t) for t in zip(*ls)]
    beta = each(lambda bl: jnp.sum(bl * sel_b, axis=-1, keepdims=True), bls)
    la = each(lambda bl: jnp.sum(bl * sel_l, axis=-1, keepdims=True), bls)
    gam = each(lambda a: _mask_nn(ateq, jnp.broadcast_to(a, (CH, HD))), la)
    gi = each(lambda g: g[:, :CH], gam)
    gj = each(lambda g: jnp.transpose(g)[:CH, :], gam)
    kq = each(lambda k, q: _nt(jnp.concatenate([k, q], axis=0), k), ks, qs)
    kk = each(lambda t: t[:CH], kq)
    qk = each(lambda t: t[CH:], kq)
    dec = each(lambda a, b: jnp.where(inc, jnp.exp(jnp.where(inc, a - b, 0.0)), 0.0), gi, gj)
    lmat = each(lambda b, d, m: before * (b * d * m), beta, dec, kk)
    if xs is None:
        same = lambda b: _same_block(b)
        l8 = each(lambda m: m * same(8), lmat)
        x = each(lambda m: eye - m, l8)
        p2 = each(lambda m: _mdot(m, m), l8)
        y = each(lambda a, b: _mdot(jnp.concatenate([a, b], axis=0), b), x, p2)
        x = each(lambda a, t: a + t[:CH], x, y)
        x = each(lambda a, t: a + _mdot(a, t[CH:]), x, y)
        for b in (8, 16, 32):
            below = same(2 * b) - same(b)
            x = each(lambda a, m: a - _mdot(a, _mdot(m * below, a)), x, lmat)
    else:
        x = each(_saved_inverse, lmat, xs)
    eg = each(jnp.exp, gam)
    uw = each(lambda a, b, v, e, k: _mdot(a, jnp.concatenate([b * v, (b * e) * k], axis=1)), x, beta, vs, eg, ks)
    u = each(lambda t: t[:, :HD], uw)
    w = each(lambda t: t[:, HD:], uw)
    tot = each(lambda a: jnp.sum(a, axis=0, keepdims=True), la)
    kd = each(lambda k, t, g: k * jnp.exp(t - g), ks, tot, gam)
    gl = each(lambda t: jnp.broadcast_to(jnp.exp(t), (1, HD)), tot)
    qd = each(lambda q, e: q * e, qs, eg)
    p = each(lambda d, m: d * m, dec, qk)
    return (u, w, kd, qd, p, gl, x) if xs is None else (u, w, kd, qd, p, gl)


def _dir_head_sel(d, h):
    lane = lax.broadcasted_iota(jnp.int32, (1, HD), 1)
    return (lane == d * GH + h).astype(F32), (lane == 2 * GH + d * GH + h).astype(F32)


def _intra_specs(T, G):
    nc = T // CH
    assert nc % G == 0
    qkv = pl.BlockSpec((None, G * CH, HD), lambda d, h, c: (h, c, 0))
    bl = pl.BlockSpec((G * CH, HD), lambda d, h, c: (c, 0))
    big = pl.BlockSpec((None, None, G * CH, HD), lambda d, h, c: (d, h, c, 0))
    pm = pl.BlockSpec((None, None, G * CH, CH), lambda d, h, c: (d, h, c, 0))
    gl = pl.BlockSpec((None, None, G, 1, HD), lambda d, h, c: (d, h, c, 0, 0))
    shapes = (_sds((2, GH, T, HD)),) + (_sds((2, GH, T, HD), BF16),) * 3 + (
        _sds((2, GH, T, CH), BF16), _sds((2, GH, nc, 1, HD)), _sds((2, GH, T, CH)))
    return nc, qkv, bl, big, pm, gl, shapes


def _chunks_per_step(T, most):
    nc = T // CH
    return max(g for g in range(1, most + 1) if nc % g == 0)


def _chunk_at(g, d, nc, ncc):
    pos = _visit_pos(g, d, nc, ncc)
    return pos, pl.ds(pl.multiple_of(pos * CH, CH), CH)


def _intra_fwd(q, k, v, bl, L, exch):
    T = q.shape[1]
    G = _chunks_per_step(T, INTRA_FWD_CHUNKS)
    nc, qkv_s, bl_s, big, pm, gl_s, shapes = _intra_specs(T, G)
    assert G == nc
    ncc = L // CH

    def body(q_ref, k_ref, v_ref, bl_ref, u_ref, w_ref, kd_ref, qd_ref, p_ref, gl_ref, x_ref):
        d, h = pl.program_id(0), pl.program_id(1)
        sb, sl = _dir_head_sel(d, h)
        rows = [slice(g * CH, (g + 1) * CH) for g in range(G)]
        outs = _intra_fn(_chunk_masks(d), sb, sl, *[[r[s, :] for s in rows] for r in (q_ref, k_ref, v_ref, bl_ref)])
        for g in range(G):
            pos, at = _chunk_at(g, d, nc, ncc)
            for r, o in zip((u_ref, w_ref, kd_ref, qd_ref, p_ref, x_ref), outs[:5] + outs[6:]):
                r[at, :] = o[g].astype(r.dtype)
            gl_ref[pos] = outs[5][g]

    return _call_carrying(body, exch, name="gdn_intra_fwd", out_shape=shapes, grid=(2, GH, nc // G),
                          in_specs=[qkv_s, qkv_s, qkv_s, bl_s], out_specs=(big, big, big, big, pm, gl_s, pm))(q, k, v, bl)


def _intra_bwd(q, k, v, bl, xinv, cts, L, exch):
    T = q.shape[1]
    G = _chunks_per_step(T, INTRA_BWD_CHUNKS)
    nc, qkv_s, bl_s, big, pm, gl_s, _ = _intra_specs(T, G)
    assert G == nc
    ncc = L // CH

    def body(q_ref, k_ref, v_ref, bl_ref, x_ref, du, dw, dkd, dqd, dp, dgl, dq_ref, dk_ref, dv_ref, dbl_ref):
        d, h = pl.program_id(0), pl.program_id(1)
        sb, sl = _dir_head_sel(d, h)
        rows = [slice(g * CH, (g + 1) * CH) for g in range(G)]
        places = [_chunk_at(g, d, nc, ncc) for g in range(G)]
        fn = functools.partial(_intra_fn, _chunk_masks(d), sb, sl, xs=[x_ref[at, :] for _, at in places])
        _, vjp = jax.vjp(fn, *[[r[s, :] for s in rows] for r in (q_ref, k_ref, v_ref, bl_ref)])
        cts = tuple([r[at, :] for _, at in places] for r in (du, dw, dkd, dqd, dp)) + ([dgl[pos] for pos, _ in places],)
        grads = vjp(cts)
        for g in range(G):
            for r, o in zip((dq_ref, dk_ref, dv_ref, dbl_ref), grads):
                r[rows[g], :] = o[g]

    return _call_carrying(body, exch, name="gdn_intra_bwd", out_shape=(_sds((2, GH, T, HD)),) * 4,
                          grid=(2, GH, nc // G), in_specs=[qkv_s, qkv_s, qkv_s, bl_s, pm, big, big, big, big, pm, gl_s],
                          out_specs=(big,) * 4)(q, k, v, bl, xinv, *cts)


def _scan_fn(s, u, w, kd, qd, p, gl):
    each = lambda f, *ls: [f(*t) for t in zip(*ls)]
    ws = each(_nn, w, s)
    delta = each(lambda a, b: a - b, u, ws)
    kdd = each(_tn, kd, delta)
    s_new = each(lambda g, a, b: g * a + b, gl, s, kdd)
    qs = each(_nn, qd, s)
    pd = each(_nn, p, delta)
    return each(lambda a, b: a + b, qs, pd), s_new


SCAN_BLOCK = 4


def _visit_pos(c, d, nc, ncc):
    back = ncc - 1 - c if c < ncc else ncc + (nc - 1 - c)
    return jnp.where(d == 0, c, back)


def _scan_specs(T, L, back):
    tb = SCAN_BLOCK * CH
    assert T % tb == 0 and L % tb == 0
    nb, ncb = T // tb, L // tb
    at = (lambda t: nb - 1 - t) if back else (lambda t: t)
    big = pl.BlockSpec((2, GH, tb, HD), lambda t: (0, 0, at(t), 0))
    pm = pl.BlockSpec((2, GH, tb, CH), lambda t: (0, 0, at(t), 0))
    gl = pl.BlockSpec((2, GH, SCAN_BLOCK, 1, HD), lambda t: (0, 0, at(t), 0, 0))
    st = pl.BlockSpec((2, GH, SCAN_BLOCK, HD, HD), lambda t: (0, 0, at(t), 0, 0))

    def natural(b):
        return jnp.where(b < ncb, ncb - 1 - b, nb - 1 - (b - ncb))

    do_specs = (pl.BlockSpec((GH, tb, HD), lambda t: (0, at(t), 0)),
                pl.BlockSpec((GH, tb, HD), lambda t: (0, natural(at(t)), 0)))
    return nb, big, pm, gl, st, do_specs


SCAN_STREAMS = [(d, h) for d in (0, 1) for h in range(GH)]


def _scan_fwd(u, w, kd, qd, p, gl, L):
    T = u.shape[2]
    nb, big, pm, gl_s, st, _ = _scan_specs(T, L, False)

    def body(u_ref, w_ref, kd_ref, qd_ref, p_ref, gl_ref, o_ref, st_ref, s_scr):
        @pl.when(pl.program_id(0) == 0)
        def _():
            s_scr[...] = jnp.zeros_like(s_scr)

        s = [s_scr[d, h] for d, h in SCAN_STREAMS]
        for i in range(SCAN_BLOCK):
            rows = slice(i * CH, (i + 1) * CH)
            for (d, h), sv in zip(SCAN_STREAMS, s):
                st_ref[d, h, i] = sv
            o, s = _scan_fn(s, *[[r[d, h, rows, :].astype(F32) for d, h in SCAN_STREAMS]
                                 for r in (u_ref, w_ref, kd_ref, qd_ref, p_ref)],
                            [gl_ref[d, h, i] for d, h in SCAN_STREAMS])
            for (d, h), ov in zip(SCAN_STREAMS, o):
                o_ref[d, h, rows, :] = ov
        for (d, h), sv in zip(SCAN_STREAMS, s):
            s_scr[d, h] = sv

    return _call(body, name="gdn_scan_fwd", out_shape=(_sds((2, GH, T, HD)), _sds((2, GH, T // CH, HD, HD))),
                 grid=(nb,), in_specs=[big, big, big, big, pm, gl_s], out_specs=(big, st),
                 scratch=[pltpu.VMEM((2, GH, HD, HD), F32)], sem=("arbitrary",), vmem=VMEM_BIG)(u, w, kd, qd, p, gl)


def _scan_bwd(u, w, kd, qd, p, gl, states, do, L, exch):
    T = u.shape[2]
    nb, big, pm, gl_s, st, do_specs = _scan_specs(T, L, True)

    def body(u_ref, w_ref, kd_ref, qd_ref, p_ref, gl_ref, st_ref, do0_ref, do1_ref,
             du_ref, dw_ref, dkd_ref, dqd_ref, dp_ref, dgl_ref, ds_scr):
        @pl.when(pl.program_id(0) == 0)
        def _():
            ds_scr[...] = jnp.zeros_like(ds_scr)

        ds = [ds_scr[d, h] for d, h in SCAN_STREAMS]
        for i in reversed(range(SCAN_BLOCK)):
            rows = slice(i * CH, (i + 1) * CH)
            mirror = slice((SCAN_BLOCK - 1 - i) * CH, (SCAN_BLOCK - i) * CH)
            _, vjp = jax.vjp(_scan_fn, [st_ref[d, h, i] for d, h in SCAN_STREAMS],
                             *[[r[d, h, rows, :].astype(F32) for d, h in SCAN_STREAMS]
                               for r in (u_ref, w_ref, kd_ref, qd_ref, p_ref)],
                             [gl_ref[d, h, i] for d, h in SCAN_STREAMS])
            dos = [do0_ref[h, rows, :] if d == 0 else do1_ref[h, mirror, :] for d, h in SCAN_STREAMS]
            ds, gu, gw, gkd, gqd, gp, ggl = vjp((dos, ds))
            for n, (d, h) in enumerate(SCAN_STREAMS):
                du_ref[d, h, rows, :] = gu[n]
                dw_ref[d, h, rows, :] = gw[n]
                dkd_ref[d, h, rows, :] = gkd[n]
                dqd_ref[d, h, rows, :] = gqd[n]
                dp_ref[d, h, rows, :] = gp[n]
                dgl_ref[d, h, i] = ggl[n]
        for (d, h), dv in zip(SCAN_STREAMS, ds):
            ds_scr[d, h] = dv

    return _call_carrying(
        body, exch, name="gdn_scan_bwd",
        out_shape=(_sds((2, GH, T, HD)),) * 4 + (_sds((2, GH, T, CH)), _sds((2, GH, T // CH, 1, HD))),
        grid=(nb,), in_specs=[big, big, big, big, pm, gl_s, st, *do_specs], out_specs=(big, big, big, big, pm, gl_s),
        scratch=[pltpu.VMEM((2, GH, HD, HD), F32)], vmem=VMEM_BIG)(u, w, kd, qd, p, gl, states, do, do)


def _gout_fn(o0, o1, z, gw):
    return _rms(o0 + o1) * gw * _silu(z)


def _backward_latent(o_ref, L):
    nl = (o_ref.shape[1] - L) // CH
    return jnp.concatenate([o_ref[1, L + (nl - 1 - j) * CH:L + (nl - j) * CH, :] for j in range(nl)], axis=0)


def _gout_fwd(o, proj, gw, L):
    T = o.shape[2]
    N = T - L
    ob = pl.BlockSpec((2, None, T, HD), lambda h: (0, h, 0, 0))

    def body(o_ref, z_ref, gw_ref, y_ref):
        y_ref[...] = _gout_fn(o_ref[0, L:, :], _backward_latent(o_ref, L), z_ref[L:, :], gw_ref[...]).astype(BF16)

    return _call(body, name="gout_fwd", out_shape=_sds((N, GH * HD), BF16), grid=(GH,),
                 in_specs=[ob, pl.BlockSpec((T, HD), lambda h: (0, C_Z // HD + h)), pl.BlockSpec((1, HD), lambda h: (0, 0))],
                 out_specs=pl.BlockSpec((N, HD), lambda h: (0, h)), sem=("parallel",))(o, proj, gw)


def _gout_bwd(o, proj, gw, dy, dproj, L):
    T = o.shape[2]
    N = T - L
    ob = pl.BlockSpec((2, None, T, HD), lambda h: (0, h, 0, 0))

    def body(o_ref, z_ref, gw_ref, dy_ref, _, do_ref, dz_ref, dgw_ref):
        _, vjp = jax.vjp(_gout_fn, o_ref[0, L:, :], _backward_latent(o_ref, L), z_ref[L:, :], gw_ref[...])
        g0, _, gz, ggw = vjp(dy_ref[...])
        do_ref[:L, :] = jnp.zeros((L, HD), F32)
        do_ref[L:, :] = g0
        dz_ref[:L, :] = jnp.zeros((L, HD), BF16)
        dz_ref[L:, :] = gz.astype(BF16)

        @pl.when(pl.program_id(0) == 0)
        def _():
            dgw_ref[...] = jnp.zeros_like(dgw_ref)

        dgw_ref[...] += ggw

    zb = pl.BlockSpec((T, HD), lambda h: (0, C_Z // HD + h))
    return _call(body, name="gout_bwd", out_shape=(_sds((GH, T, HD)), _sds(dproj.shape, BF16), _sds((1, HD))),
                 grid=(GH,),
                 in_specs=[ob, zb, pl.BlockSpec((1, HD), lambda h: (0, 0)), pl.BlockSpec((N, HD), lambda h: (0, h)), ANYSPEC],
                 out_specs=(pl.BlockSpec((None, T, HD), lambda h: (h, 0, 0)), zb, pl.BlockSpec((1, HD), lambda h: (0, 0))),
                 aliases={4: 1}, sem=("arbitrary",))(o, proj, gw, dy, dproj)


def _merge_fn(pa, pd, ga, gd):
    return jax.nn.sigmoid(ga) * pa + jax.nn.sigmoid(gd) * pd


def _merge_fwd(pa, pd, proj, L, *, br=256):
    N = pa.shape[0]
    lb = L // br
    row = pl.BlockSpec((br, D), lambda i: (i, 0))

    def body(pa_ref, pd_ref, ga_ref, gd_ref, y_ref):
        y_ref[...] = _merge_fn(pa_ref[...], pd_ref[...], ga_ref[...], gd_ref[...]).astype(BF16)

    return _call(body, name="merge_fwd", out_shape=_sds((N, D), BF16), grid=(N // br,),
                 in_specs=[row, row, pl.BlockSpec((br, D), lambda i: (i + lb, C_GATE // D)),
                           pl.BlockSpec((br, D), lambda i: (i + lb, C_GATE // D + 1))],
                 out_specs=row, sem=("parallel",))(pa, pd, proj, proj)


def _merge_bwd(pa, pd, proj, dy, L, *, br=256):
    N = pa.shape[0]
    T = N + L
    lb = L // br
    lrow = pl.BlockSpec((br, D), lambda i: (jnp.maximum(i - lb, 0), 0))

    def body(pa_ref, pd_ref, ga_ref, gd_ref, dy_ref, dpa_ref, dpd_ref, dg_ref):
        lat = pl.program_id(0) >= lb
        _, vjp = jax.vjp(_merge_fn, pa_ref[...], pd_ref[...], ga_ref[...], gd_ref[...])
        gpa, gpd, gga, ggd = vjp(dy_ref[...])
        dpa_ref[...] = gpa.astype(BF16)
        dpd_ref[...] = gpd.astype(BF16)
        dg_ref[:, :D] = jnp.where(lat, gga, 0.0).astype(BF16)
        dg_ref[:, D:] = jnp.where(lat, ggd, 0.0).astype(BF16)

    return _call(body, name="merge_bwd", out_shape=(_sds((N, D), BF16), _sds((N, D), BF16), _sds((T, C_END), BF16)),
                 grid=(T // br,),
                 in_specs=[lrow, lrow, pl.BlockSpec((br, D), lambda i: (i, C_GATE // D)),
                           pl.BlockSpec((br, D), lambda i: (i, C_GATE // D + 1)), lrow],
                 out_specs=(lrow, lrow, pl.BlockSpec((br, 2 * D), lambda i: (i, C_GATE // (2 * D)))),
                 sem=("arbitrary",))(pa, pd, proj, proj, dy)


def _resid_fwd(x, m, mod, i_g, *, name, br=256):
    R = x.shape[0]
    row = pl.BlockSpec((br, D), lambda i: (i, 0))

    def body(x_ref, m_ref, mod_ref, o_ref):
        o_ref[...] = x_ref[...] + mod_ref[i_g:i_g + 1, :] * m_ref[...]

    return _call(body, name=name, out_shape=_sds((R, D)), grid=(R // br,),
                 in_specs=[row, row, pl.BlockSpec((6, D), lambda i: (0, 0))], out_specs=row,
                 sem=("parallel",))(x, m, mod)


def _resid_bwd(dx, m, mod, i_g, *, name, br=256):
    R = dx.shape[0]
    row = pl.BlockSpec((br, D), lambda i: (i, 0))
    vec = pl.BlockSpec((1, D), lambda i: (0, 0))

    def body(dx_ref, m_ref, mod_ref, dm_ref, dg_ref):
        dxv = dx_ref[...]
        dm_ref[...] = (dxv * mod_ref[i_g:i_g + 1, :]).astype(BF16)

        @pl.when(pl.program_id(0) == 0)
        def _():
            dg_ref[...] = jnp.zeros_like(dg_ref)

        dg_ref[...] += jnp.sum(dxv * m_ref[...], axis=0, keepdims=True)

    return _call(body, name=name, out_shape=(_sds((R, D), BF16), _sds((1, D))), grid=(R // br,),
                 in_specs=[row, row, pl.BlockSpec((6, D), lambda i: (0, 0))], out_specs=(row, vec),
                 sem=("arbitrary",))(dx, m, mod)


def _ffn_fn(shifts, ug, uv, wg, wv, bg, bv):
    down, up = shifts

    def conv(x, w, b):
        return down(x) * w[0:1, :] + x * w[1:2, :] + up(x) * w[2:3, :] + b

    return _silu(conv(ug, wg, bg)) * conv(uv, wv, bv)


def _ffn_fwd(up, cw, cb, *, bw=256):
    N = up.shape[0]
    shifts = _make_shift(((0, N),))
    nb = DFF // bw

    def body(ug, uv, wg, wv, bg, bv, a_ref):
        a_ref[...] = _ffn_fn(shifts, ug[...], uv[...], wg[...], wv[...], bg[...], bv[...]).astype(BF16)

    def col(rows, off):
        return pl.BlockSpec((rows, bw), lambda j: (0, j + off))

    return _call(body, name="ffn_fwd", out_shape=_sds((N, DFF), BF16), grid=(nb,),
                 in_specs=[col(N, 0), col(N, nb), col(3, 0), col(3, nb), col(1, 0), col(1, nb)],
                 out_specs=col(N, 0), sem=("parallel",), vmem=VMEM_BIG)(up, up, cw, cw, cb, cb)


def _ffn_bwd(up, cw, cb, da, *, bw=256):
    N = up.shape[0]
    shifts = _make_shift(((0, N),))
    nb = DFF // bw

    def body(ug, uv, wg, wv, bg, bv, da_ref, dug, duv, dwg, dwv, dbg, dbv):
        _, vjp = jax.vjp(functools.partial(_ffn_fn, shifts), ug[...], uv[...], wg[...], wv[...], bg[...], bv[...])
        g = vjp(da_ref[...])
        dug[...] = g[0].astype(BF16)
        duv[...] = g[1].astype(BF16)
        dwg[...], dwv[...], dbg[...], dbv[...] = g[2], g[3], g[4], g[5]

    def col(rows, off):
        return pl.BlockSpec((rows, bw), lambda j: (0, j + off))

    half = (_sds((N, DFF), BF16), _sds((N, DFF), BF16), _sds((3, DFF)), _sds((3, DFF)), _sds((1, DFF)), _sds((1, DFF)))
    dug, duv, dwg, dwv, dbg, dbv = _call(
        body, name="ffn_bwd", out_shape=half, grid=(nb,),
        in_specs=[col(N, 0), col(N, nb), col(3, 0), col(3, nb), col(1, 0), col(1, nb), col(N, 0)],
        out_specs=(col(N, 0), col(N, 0), col(3, 0), col(3, 0), col(1, 0), col(1, 0)),
        sem=("parallel",), vmem=VMEM_BIG)(up, up, cw, cw, cb, cb, da)
    return (jnp.concatenate([dug, duv], axis=1), jnp.concatenate([dwg, dwv], axis=1),
            jnp.concatenate([dbg, dbv], axis=1))


def _head_fn(x1, dn, g2, fw, tgt):
    y = _rms(x1 + g2 * dn) * fw
    err = y - tgt
    return 0.5 * jnp.sum(jnp.mean(err * err, axis=-1))


def _head(x1, dn, mod, fw, tgt, *, br=256):
    N = x1.shape[0]
    row = pl.BlockSpec((br, D), lambda i: (i, 0))
    vec = pl.BlockSpec((1, D), lambda i: (0, 0))
    one = pl.BlockSpec((1, HD), lambda i: (0, 0))

    def body(x1_ref, dn_ref, mod_ref, fw_ref, tgt_ref, loss_ref, dx_ref, ddn_ref, dg_ref, dfw_ref):
        loss, (gx, gdn, gg, gfw) = jax.value_and_grad(_head_fn, argnums=(0, 1, 2, 3))(
            x1_ref[...], dn_ref[...], mod_ref[5:6, :], fw_ref[...], tgt_ref[...])
        dx_ref[...] = gx
        ddn_ref[...] = gdn.astype(BF16)

        @pl.when(pl.program_id(0) == 0)
        def _():
            loss_ref[...] = jnp.zeros_like(loss_ref)
            dg_ref[...] = jnp.zeros_like(dg_ref)
            dfw_ref[...] = jnp.zeros_like(dfw_ref)

        loss_ref[...] += jnp.broadcast_to(loss, (1, HD))
        dg_ref[...] += gg
        dfw_ref[...] += gfw

    return _call(body, name="head", out_shape=(_sds((1, HD)), _sds((N, D)), _sds((N, D), BF16), _sds((1, D)), _sds((1, D))),
                 grid=(N // br,), in_specs=[row, row, pl.BlockSpec((6, D), lambda i: (0, 0)), vec, row],
                 out_specs=(one, row, row, vec, vec), sem=("arbitrary",))(x1, dn, mod, fw, tgt)


def _adamw(w, g, m, v, *, name):
    shape = w.shape
    cols = shape[-1]
    rows = max(1, math.prod(shape[:-1]))
    w2, g2, m2, v2 = (t.reshape(rows, cols) for t in (w, g, m, v))
    br = 256 if rows % 256 == 0 else rows
    c1 = 1.0 - B1 ** STEP
    c2 = 1.0 - B2 ** STEP

    def body(w_ref, g_ref, m_ref, v_ref, d_ref, nm_ref, nv_ref):
        gv = g_ref[...]
        nm = B1 * m_ref[...] + (1.0 - B1) * gv
        nv = B2 * v_ref[...] + (1.0 - B2) * (gv * gv)
        d_ref[...] = -LR * ((nm / c1) / (jnp.sqrt(nv / c2) + AEPS) + WD * w_ref[...])
        nm_ref[...] = nm
        nv_ref[...] = nv

    blk = pl.BlockSpec((br, cols), lambda i: (i, 0))
    outs = _call(body, name=name, out_shape=(_sds((rows, cols)),) * 3, grid=(rows // br,),
                 in_specs=[blk] * 4, out_specs=(blk,) * 3, sem=("parallel",))(w2, g2, m2, v2)
    return tuple(t.reshape(shape) for t in outs)


def _adamw_many(items, *, name):
    k = len(items)
    shapes = [w.shape for w, _, _, _ in items]
    flat = [t.reshape(max(1, math.prod(t.shape[:-1])), t.shape[-1]) for it in items for t in it]
    c1 = 1.0 - B1 ** STEP
    c2 = 1.0 - B2 ** STEP

    def body(*refs):
        ins, outs = refs[:4 * k], refs[4 * k:]
        for i in range(k):
            w_ref, g_ref, m_ref, v_ref = ins[4 * i:4 * i + 4]
            gv = g_ref[...]
            nm = B1 * m_ref[...] + (1.0 - B1) * gv
            nv = B2 * v_ref[...] + (1.0 - B2) * (gv * gv)
            outs[3 * i][...] = -LR * ((nm / c1) / (jnp.sqrt(nv / c2) + AEPS) + WD * w_ref[...])
            outs[3 * i + 1][...] = nm
            outs[3 * i + 2][...] = nv

    res = _call(body, name=name, out_shape=tuple(_sds(flat[4 * i].shape) for i in range(k) for _ in range(3)))(*flat)
    return [tuple(res[3 * i + j].reshape(shapes[i]) for j in range(3)) for i in range(k)]


def _rope_tables(N, L):
    t = jnp.arange(N)
    pos = jnp.stack([(t // GRID_W).astype(F32), (t % GRID_W).astype(F32)], axis=1)
    inv = ROPE_THETA ** (-jnp.arange(0, HD // 2, 2, dtype=F32) / (HD // 2))
    ang = pos[:, :, None] * inv[None, None, :]
    cos = jnp.broadcast_to(jnp.cos(ang)[:, :, None, :], (N, 2, 2, HD // 4)).reshape(N, HD)
    sin = jnp.broadcast_to(jnp.sin(ang)[:, :, None, :], (N, 2, 2, HD // 4))
    sin = (sin * jnp.array([-1.0, 1.0], F32)[None, None, :, None]).reshape(N, HD)
    cos = jnp.concatenate([jnp.ones((L, HD), F32), cos], axis=0)
    sin = jnp.concatenate([jnp.zeros((L, HD), F32), sin], axis=0)
    return cos, sin


def _pad_lanes(v, off=0):
    return jnp.zeros((1, HD), F32).at[0, off:off + v.shape[0]].set(v)


def _local_step(x, ctx, tgt, mod_lat, mod_ctx, w_in, shards, small):
    N, L = x.shape[0], ctx.shape[0]
    T = N + L
    bounds = ((0, L), (L, T))
    qw, kw, gw = small["q_norm_w"], small["k_norm_w"], small["gdn_norm_w"]
    conv_w, ffn_w, ffn_b, fnw = small["conv_qkv_w"], small["ffn_conv_w"], small["ffn_conv_b"], small["final_norm_w"]
    alog = _pad_lanes(small["a_log"].reshape(-1), 2 * GH)
    dtb = _pad_lanes(small["dt_bias"].reshape(-1), 2 * GH)
    cos, sin = _rope_tables(N, L)
    bt = T
    bnl = 256 if N % 1024 else 1024

    h1 = _normmod_fwd(ctx, mod_ctx, 0, 1, name="normmod_ctx", out_rows=T)
    h1 = _normmod_fwd(x, mod_lat, 0, 1, name="normmod_x", off=L, into=h1)
    proj = _mm(h1, w_in, name="mm_in", M=T, N=C_END, K=D, tb=True, bm=bt, bn=1024)
    aq, ak, av = _aprep_fwd(proj, cos, sin, qw, kw)
    (attn, attn32, lse), (up_g,) = _attn_fwd(aq, ak, av, L, _GatherTwoLevel([shards["w_up"]]))
    gq = _gprep_fwd(proj, conv_w, 0, bounds)
    gk = _gprep_fwd(proj, conv_w, 1, bounds)
    gv = _gprep_fwd(proj, conv_w, 2, bounds)
    bl = _bl_fwd(proj, alog, dtb)
    intra, (down_g, pa_g, pd_g, out_g) = _intra_fwd(
        gq, gk, gv, bl, L, _GatherTwoLevel([shards[n] for n in ("w_down", "w_pa", "w_pd", "w_out")]))
    w_up, w_down = up_g.reshape(2 * DFF, D), down_g.reshape(DFF, D)
    w_pa, w_pd, w_out = pa_g.reshape(D, D), pd_g.reshape(D, D), out_g.reshape(D, D)
    xinv, intra = intra[6], intra[:6]
    o, states = _scan_fwd(*intra, L)
    gdn = _gout_fwd(o, proj, gw, L)
    pa = _mm(attn, w_pa, name="mm_pa", M=N, N=D, K=D, bm=bnl)
    pd = _mm(gdn, w_pd, name="mm_pd", M=N, N=D, K=D, bm=bnl)
    y = _merge_fwd(pa, pd, proj, L)
    m = _mm(y, w_out, name="mm_out", M=N, N=D, K=D, bm=bnl)
    x1 = _resid_fwd(x, m, mod_lat, 2, name="resid1")
    h2 = _normmod_fwd(x1, mod_lat, 3, 4, name="normmod_x1")
    up = _mm(h2, w_up, name="mm_up", M=N, N=2 * DFF, K=D, tb=True, bm=bnl, bn=2 * DFF // 4)
    a = _ffn_fwd(up, ffn_w, ffn_b)
    dn = _mm(a, w_down, name="mm_down", M=N, N=D, K=DFF, bm=bnl)
    loss, dx2, ddn, dg2, dfnw = _head(x1, dn, mod_lat, fnw, tgt)

    da = _mm(ddn, w_down, name="mm_down_dx", M=N, N=DFF, K=D, tb=True, bm=bnl, bn=DFF // 2)
    g_down = _mm(a, ddn, name="mm_down_dw", M=DFF, N=D, K=N, ta=True, bm=DFF // 2, out_dtype=BF16)
    dup, d_ffn_w, d_ffn_b = _ffn_bwd(up, ffn_w, ffn_b, da)
    dh2 = _mm(dup, w_up, name="mm_up_dx", M=N, N=D, K=2 * DFF, bm=bnl, bk=2 * DFF // 4)
    g_up = _mm(dup, h2, name="mm_up_dw", M=2 * DFF, N=D, K=N, ta=True, bm=2 * DFF // 4, out_dtype=BF16)
    dx1, dsh2, dsc2 = _normmod_bwd(x1, mod_lat, 3, 4, dh2, 0, dx2, name="normmod_x1_bwd")
    dm, dg1 = _resid_bwd(dx1, m, mod_lat, 2, name="resid1_bwd")
    dy = _mm(dm, w_out, name="mm_out_dx", M=N, N=D, K=D, tb=True, bm=bnl)
    g_out = _mm(y, dm, name="mm_out_dw", M=D, N=D, K=N, ta=True, out_dtype=BF16)
    dpa, dpd, dproj = _merge_bwd(pa, pd, proj, dy, L)
    dattn = _mm(dpa, w_pa, name="mm_pa_dx", M=N, N=D, K=D, tb=True, bm=bnl)
    g_pa = _mm(attn, dpa, name="mm_pa_dw", M=D, N=D, K=N, ta=True, out_dtype=BF16)
    dgdn = _mm(dpd, w_pd, name="mm_pd_dx", M=N, N=D, K=D, tb=True, bm=bnl)
    g_pd = _mm(gdn, dpd, name="mm_pd_dw", M=D, N=D, K=N, ta=True, out_dtype=BF16)
    do, dproj, dgw = _gout_bwd(o, proj, gw, dgdn, dproj, L)
    cts, recv_a = _scan_bwd(*intra, states, do, L, _Exchange(
        [g_out.reshape(NDEV, D // NDEV, D), g_pa.reshape(NDEV, D // NDEV, D), g_pd.reshape(NDEV, D // NDEV, D)], True))
    (dgq, dgk, dgv, dbl), recv_b = _intra_bwd(gq, gk, gv, bl, xinv, cts, L, _Exchange(
        [g_up.reshape(NDEV, 2 * DFF // NDEV, D)], True))
    dproj, dwq = _gprep_bwd(proj, conv_w, 0, bounds, dgq, dproj)
    dproj, dwk = _gprep_bwd(proj, conv_w, 1, bounds, dgk, dproj)
    dproj, dwv = _gprep_bwd(proj, conv_w, 2, bounds, dgv, dproj)
    dproj, dalog, ddtb = _bl_bwd(proj, alog, dtb, dbl, dproj)
    (daq_h, dak_h, dav_h), recv_c = _attn_bwd(aq, ak, av, attn32, lse, dattn, L, _Exchange(
        [g_down.reshape(NDEV, DFF // NDEV, D)], True))
    recv = dict(zip(("w_out", "w_pa", "w_pd", "w_up", "w_down"), recv_a + recv_b + recv_c))
    dproj, dqw, dkw = _aprep_bwd(proj, cos, sin, qw, kw, daq_h, dak_h, dav_h, dproj, L)
    g_in = _mm(dproj, h1, name="mm_in_dw", M=C_END, N=D, K=T, ta=True, bm=1024, out_dtype=BF16)
    *pending, token = _scatter_start(g_in, None, (0, D // 2), (), name="scatter_g_in_a_start")
    dh1 = _mm(dproj, w_in, name="mm_in_dx", M=T, N=D, K=C_END, bm=bt, bk=1024, after=(token,))
    grad_x, dsh1, dsc1 = _normmod_bwd(x, mod_lat, 0, 1, dh1, L, dx1, name="normmod_x_bwd")
    _, dcsh1, dcsc1 = _normmod_bwd(ctx, mod_ctx, 0, 1, dh1, 0, None, name="normmod_ctx_bwd")

    z1 = jnp.zeros((1, D), F32)
    dmod_lat = jnp.concatenate([dsh1, dsc1, dg1, dsh2, dsc2, dg2], axis=0)
    dmod_ctx = jnp.concatenate([dcsh1, dcsc1, z1, z1, z1, z1], axis=0)
    gsmall = {
        "q_norm_w": dqw, "k_norm_w": dkw, "gdn_norm_w": dgw,
        "conv_qkv_w": jnp.concatenate([dwq, dwk, dwv], axis=1),
        "a_log": dalog[0, 2 * GH:4 * GH], "dt_bias": ddtb[0, 2 * GH:4 * GH],
        "ffn_conv_w": d_ffn_w, "ffn_conv_b": d_ffn_b, "final_norm_w": dfnw,
    }
    return loss[0, 0], grad_x, pending, recv, dmod_lat, dmod_ctx, gsmall


HBM = pl.BlockSpec(memory_space=pltpu.HBM)
ANYSPEC = pl.BlockSpec(memory_space=pl.ANY)


def _position():
    x, y, c = lax.axis_index("x"), lax.axis_index("y"), lax.axis_index("c")
    return x, y, c, 4 * x + 2 * y + c


def _peer(x, y, c, k):
    px = 1 - x if k & 4 else x
    py = 1 - y if k & 2 else y
    pc = 1 - c if k & 1 else c
    return (px, py, pc), 4 * px + 2 * py + pc


def _exchange(arrs, *, name, scatter):
    exch = _Exchange(arrs, scatter)
    n = exch.n

    def body(*refs):
        ins, outs, sems = refs[:n], refs[n:2 * n], refs[2 * n:]
        exch.start(ins, outs, sems)
        exch.finish(ins, outs, sems)

    outs = pl.pallas_call(body, name=name, out_shape=exch.out_shape, in_specs=[HBM] * n, out_specs=(HBM,) * n,
                          scratch_shapes=exch.scratch,
                          compiler_params=pltpu.CompilerParams(has_side_effects=True))(*arrs)
    return list(outs)


class _Exchange:
    def __init__(self, arrs, scatter):
        self.arrs, self.scatter, self.n = list(arrs), scatter, len(arrs)
        self.out_shape = tuple(_sds(a.shape if scatter else (NDEV,) + a.shape, a.dtype) for a in arrs)
        self.scratch = [pltpu.SemaphoreType.DMA((self.n, NDEV - 1)), pltpu.SemaphoreType.DMA((self.n, NDEV - 1)),
                        pltpu.SemaphoreType.DMA((self.n,))]

    def _copies(self, ins, outs, sems):
        send, recv, loc = sems
        x, y, c, me = _position()
        local = [pltpu.make_async_copy(ins[a].at[me] if self.scatter else ins[a], outs[a].at[me], loc.at[a])
                 for a in range(self.n)]
        remote = []
        for k in range(1, NDEV):
            peer, pid = _peer(x, y, c, k)
            for a in range(self.n):
                src = ins[a].at[pid] if self.scatter else ins[a]
                remote.append(pltpu.make_async_remote_copy(
                    src_ref=src, dst_ref=outs[a].at[me], send_sem=send.at[a, k - 1], recv_sem=recv.at[a, k - 1],
                    device_id=peer, device_id_type=MESH))
        return local, remote

    def start(self, ins, outs, sems):
        local, remote = self._copies(ins, outs, sems)
        for cp in local + remote:
            cp.start()

    def finish(self, ins, outs, sems):
        local, remote = self._copies(ins, outs, sems)
        for cp in remote:
            cp.wait()
        for cp in local:
            cp.wait()


class _GatherTwoLevel:
    scatter = False

    def __init__(self, arrs):
        self.arrs, self.n = list(arrs), len(arrs)
        self.out_shape = tuple(_sds((NDEV,) + a.shape, a.dtype) for a in arrs)
        self.scratch = [pltpu.SemaphoreType.DMA((self.n, NDEV - 1)), pltpu.SemaphoreType.DMA((self.n, NDEV - 1)),
                        pltpu.SemaphoreType.DMA((self.n,))]

    def _parts(self, ins, outs, sems):
        send, recv, loc = sems
        x, y, c, _ = _position()
        me, sibling = (x, y, c), (x, y, 1 - c)
        chips = [(1 - x, y), (x, 1 - y), (1 - x, 1 - y)]
        parts = []
        for a in range(self.n):
            slot = lambda px, py, pc, a=a: outs[a].at[4 * px + 2 * py + pc]

            def copy(k, owner, to, src=None, a=a, slot=slot):
                return pltpu.make_async_remote_copy(
                    src_ref=slot(*owner) if src is None else src, dst_ref=slot(*owner), send_sem=send.at[a, k],
                    recv_sem=recv.at[a, k], device_id=to, device_id_type=MESH)

            parts.append(dict(
                mine=pltpu.make_async_copy(ins[a], slot(*me), loc.at[a]),
                first=[copy(0, me, sibling, src=ins[a])] + [copy(1 + j, me, (*ch, c), src=ins[a]) for j, ch in enumerate(chips)],
                arrive=[copy(1 + j, (*ch, c), me) for j, ch in enumerate(chips)],
                passed=[copy(4 + j, (*ch, c), sibling) for j, ch in enumerate(chips)],
                rest=[copy(0, sibling, me)] + [copy(4 + j, (*ch, 1 - c), me) for j, ch in enumerate(chips)]))
        return parts

    def start(self, ins, outs, sems):
        for p in self._parts(ins, outs, sems):
            p["mine"].start()
            for cp in p["first"]:
                cp.start()

    def middle(self, ins, outs, sems):
        for p in self._parts(ins, outs, sems):
            for got, fwd in zip(p["arrive"], p["passed"]):
                got.wait_recv()
                fwd.start()

    def finish(self, ins, outs, sems):
        for p in self._parts(ins, outs, sems):
            for cp in p["rest"]:
                cp.wait_recv()
            for cp in p["first"] + p["passed"]:
                cp.wait_send()
            p["mine"].wait()


def _gather_two_level(blocks, *, name):
    exch = _GatherTwoLevel(blocks)
    n = exch.n

    def body(*refs):
        ins, outs, sems = refs[:n], refs[n:2 * n], refs[2 * n:]
        exch.start(ins, outs, sems)
        exch.middle(ins, outs, sems)
        exch.finish(ins, outs, sems)

    outs = pl.pallas_call(body, name=name, out_shape=exch.out_shape, in_specs=[HBM] * n, out_specs=(HBM,) * n,
                          scratch_shapes=exch.scratch,
                          compiler_params=pltpu.CompilerParams(has_side_effects=True))(*blocks)
    return list(outs)


SEM = pl.BlockSpec(memory_space=pltpu.SEMAPHORE)


SHARD_ROWS = W_END // NDEV
RUNS = ((0, W_QKV, C_KV), (W_QKV, W_AQ - W_QKV, C_QKV), (W_AQ, W_Z - W_AQ, C_AQ), (W_Z, W_END - W_Z, C_Z))


ROW_TILE = 8
SLOT_ROWS = -(-SHARD_ROWS // ROW_TILE) * ROW_TILE


def _shard_pieces(d):
    lo, hi = d * SHARD_ROWS, (d + 1) * SHARD_ROWS
    lead = lo % ROW_TILE
    pieces = []
    for first, rows, padded in RUNS:
        a, b = max(lo, first), min(hi, first + rows)
        if a < b:
            pieces.append([a - lo + lead, b - a, padded + a - first])
    pieces[0] = [0, pieces[0][1] + lead, pieces[0][2] - lead]
    pieces[-1][1] = SLOT_ROWS - pieces[-1][0]
    assert all(v % ROW_TILE == 0 for p in pieces for v in p) and all(p[2] + p[1] <= C_END for p in pieces)
    return pieces


def _pad_shards(g):
    gap = C_QKV + W_AQ - W_QKV
    parts = [(gap, jnp.zeros((C_Z - gap, D), g.dtype))]
    for d in range(NDEV):
        lo = d * SHARD_ROWS
        for first, rows, padded in RUNS:
            a, b = max(lo, first), min(lo + SHARD_ROWS, first + rows)
            if a < b:
                parts.append((padded + a - first, g[d, a - lo:b - lo]))
    parts.sort(key=lambda p: p[0])
    assert sum(p[1].shape[0] for p in parts) == C_END
    return jnp.concatenate([p[1] for p in parts], axis=0)


def _scatter_send(src_ref, land_ref, send_sems, recv_sems, cols):
    _, _, _, me = _position()
    for d in range(NDEV):
        @pl.when(me != d)
        def _():
            k = jnp.bitwise_xor(me, d)
            peer = tuple(jnp.int32((d >> s) & 1) for s in (2, 1, 0))
            for off, rows, padded in _shard_pieces(d):
                pltpu.make_async_remote_copy(
                    src_ref=src_ref.at[pl.ds(padded, rows), pl.ds(*cols)],
                    dst_ref=land_ref.at[me].at[pl.ds(off, rows), pl.ds(*cols)], send_sem=send_sems.at[k - 1],
                    recv_sem=recv_sems.at[k - 1], device_id=peer, device_id_type=MESH).start()


def _scatter_whole(src_ref, land_ref, send_sems, recv_sems, cols):
    x, y, c, me = _position()
    span = (slice(None), pl.ds(*cols))
    copies = []
    for k in range(1, NDEV):
        peer, _ = _peer(x, y, c, k)
        copies.append(pltpu.make_async_remote_copy(
            src_ref=src_ref.at[pl.ds(0, SLOT_ROWS)].at[span], dst_ref=land_ref.at[me].at[span],
            send_sem=send_sems.at[k - 1], recv_sem=recv_sems.at[k - 1], device_id=peer, device_id_type=MESH))
    return copies


SPLIT_EFFECT = pltpu.SideEffectType.DATAFLOW_SIDE_EFFECTING


def _scatter_start(parts, land, cols, after, *, name):
    na = len(after)
    if land is None:
        land = lax.empty((NDEV, SLOT_ROWS, D), parts.dtype)

    def body(src_ref, land_ref, *rest):
        send_sems, recv_sems, _, _, token = rest[na:]
        _scatter_send(src_ref, land_ref, send_sems, recv_sems, cols)
        token[...] = jnp.zeros_like(token)

    return pl.pallas_call(
        body, name=name,
        out_shape=(pltpu.SemaphoreType.DMA((NDEV - 1,)), pltpu.SemaphoreType.DMA((NDEV - 1,)),
                   pltpu.HBM(parts.shape, parts.dtype), pltpu.HBM(land.shape, land.dtype), _sds((8, HD))),
        in_specs=(HBM, HBM) + (pl.BlockSpec(memory_space=pl.ANY),) * na,
        out_specs=(SEM, SEM, HBM, HBM, pl.BlockSpec(memory_space=pltpu.VMEM)),
        input_output_aliases={0: 2, 1: 3}, compiler_params=pltpu.CompilerParams(has_side_effects=SPLIT_EFFECT),
    )(pltpu.with_memory_space_constraint(parts, pltpu.HBM), pltpu.with_memory_space_constraint(land, pltpu.HBM), *after)


def _scatter_wait(send_sems, recv_sems, src_thru, land_thru, cols, after, *, name):
    na = len(after)

    def body(src_ref, land_ref, send_sems, recv_sems, *rest):
        for cp in _scatter_whole(src_ref, land_ref, send_sems, recv_sems, cols):
            cp.wait_send()
            cp.wait_recv()

    return pl.pallas_call(
        body, name=name,
        out_shape=(pltpu.HBM(src_thru.shape, src_thru.dtype), pltpu.HBM(land_thru.shape, land_thru.dtype)),
        in_specs=(HBM, HBM, SEM, SEM) + (pl.BlockSpec(memory_space=pl.ANY),) * na, out_specs=(HBM, HBM),
        input_output_aliases={0: 0, 1: 1}, compiler_params=pltpu.CompilerParams(has_side_effects=SPLIT_EFFECT),
    )(src_thru, land_thru, send_sems, recv_sems, *after)


def _cast_bf16(ws, *, name):
    k = len(ws)

    def body(*refs):
        for w_ref, o_ref in zip(refs[:k], refs[k:]):
            o_ref[...] = w_ref[...].astype(BF16)

    return _call(body, name=name, out_shape=tuple(_sds(w.shape, BF16) for w in ws), vmem=VMEM_BIG)(*ws)


def _sum_slots(a, *, name):
    _, R, C = a.shape

    def body(a_ref, o_ref):
        s = a_ref[0]
        for d in range(1, NDEV):
            s = s + a_ref[d]
        o_ref[...] = s

    return _call(body, name=name, out_shape=_sds((R, C)))(a)


MODROWS = 16


def _mod_fwd(c9, w, b):
    cols = w.shape[1]

    def body(c_ref, w_ref, b_ref, o_ref):
        o_ref[...] = _nn(_silu(c_ref[...]), w_ref[...]) + b_ref[...]

    return _call(body, name="mod_fwd", out_shape=_sds((MODROWS, cols)))(c9, w, b)


def _mod_bwd(c9, dmy, dall, w):
    cols = w.shape[1]

    def body(c_ref, dmy_ref, dall_ref, w_ref, gw_ref, gb_ref, cp_ref):
        sc = _silu(c_ref[...])
        rows = lax.broadcasted_iota(jnp.int32, (MODROWS, 1), 0)
        d = dmy_ref[...]
        d_ctx = jnp.where(rows == NDEV, d, 0.0)
        sc_ctx = jnp.where(rows == NDEV, sc, 0.0)
        outer = lax.dot_general(sc_ctx, d_ctx, (((0,), (0,)), ((), ())), precision=HI, preferred_element_type=F32)
        gw_ref[...] = _tn(jnp.where(rows < NDEV, sc, 0.0), jnp.where(rows < NDEV, d, 0.0)) + outer
        gb_ref[...] = jnp.sum(dall_ref[...], axis=0, keepdims=True)
        cp_ref[...] = jnp.sum(_nt(d_ctx, w_ref[...]), axis=0, keepdims=True)

    return _call(body, name="mod_bwd", out_shape=(_sds((D, cols)), _sds((1, 6 * D)), _sds((1, D))),
                 vmem=VMEM_BIG)(c9, dmy, dall, w)


def _cctx_finish(parts, c_ctx, after):
    VM = pl.BlockSpec(memory_space=pltpu.VMEM)

    def body(p_ref, c_ref, *rest):
        o_ref = rest[-1]
        s = p_ref[0]
        for d in range(1, NDEV):
            s = s + p_ref[d]
        _, vjp = jax.vjp(_silu, c_ref[...])
        o_ref[...] = vjp(s)[0]

    return _call(body, name="cctx_finish", out_shape=_sds((1, D)),
                 in_specs=[VM, VM] + [pl.BlockSpec(memory_space=pl.ANY)] * len(after))(parts, c_ctx, *after)


def _adamw_recv(w, recv, m, v, *, name, own=None):
    rows, cols = w.shape
    slot_rows = recv.shape[1]
    lead = slot_rows - rows
    assert rows % ROW_TILE in (0, lead)
    bc = 256
    c1 = 1.0 - B1 ** STEP
    c2 = 1.0 - B2 ** STEP
    has_own = own is not None

    def body(w_ref, r_ref, m_ref, v_ref, *rest):
        g_ref, d_ref, nm_ref, nv_ref = rest[-4:]
        me = _position()[3]

        def slot(d):
            return jnp.where(me == d, rest[0][...], r_ref[d]) if has_own else r_ref[d]

        gv = slot(0).astype(F32)
        for d in range(1, NDEV):
            gv = gv + slot(d).astype(F32)
        if lead:
            gv = jnp.where((me * rows) % ROW_TILE == 0, gv[:rows], gv[lead:])
        nm = B1 * m_ref[...] + (1.0 - B1) * gv
        nv = B2 * v_ref[...] + (1.0 - B2) * (gv * gv)
        g_ref[...] = gv
        d_ref[...] = -LR * ((nm / c1) / (jnp.sqrt(nv / c2) + AEPS) + WD * w_ref[...])
        nm_ref[...] = nm
        nv_ref[...] = nv

    blk = pl.BlockSpec((rows, bc), lambda j: (0, j))
    return _call(body, name=name, out_shape=(_sds((rows, cols)),) * 4, grid=(cols // bc,),
                 in_specs=[blk, pl.BlockSpec((NDEV, slot_rows, bc), lambda j: (0, 0, j)), blk, blk]
                 + [pl.BlockSpec((slot_rows, bc), lambda j: (0, j))] * has_own,
                 out_specs=(blk,) * 4, sem=("parallel",), vmem=VMEM_BIG)(w, recv, m, v, *([own] if has_own else []))


P_LAT, P_CTX, P_FNW, P_FFNB, P_CONV, P_FFNW, P_MISC, P_ROWS = 0, 8, 16, 24, 32, 48, 72, 80


def _rows_of(v, nrows):
    flat = v.reshape(-1)
    return jnp.pad(flat, (0, nrows * D - flat.shape[0])).reshape(nrows, D)


def _by_columns(g):
    n, r, c = g.shape
    return jnp.transpose(g, (1, 0, 2)).reshape(r, n * c)


def kernel(x, c, ctx, c_ctx, w_mod, b_mod, w_in, q_norm_w, k_norm_w, conv_qkv_w, a_log, dt_bias, gdn_norm_w, w_pa, w_pd, w_out, w_up, ffn_conv_w, ffn_conv_b, w_down, final_norm_w, loss_target, m_c_ctx, m_w_mod, m_b_mod, m_w_in, m_q_norm_w, m_k_norm_w, m_conv_qkv_w, m_a_log, m_dt_bias, m_gdn_norm_w, m_w_pa, m_w_pd, m_w_out, m_w_up, m_ffn_conv_w, m_ffn_conv_b, m_w_down, m_final_norm_w, v_c_ctx, v_w_mod, v_b_mod, v_w_in, v_q_norm_w, v_k_norm_w, v_conv_qkv_w, v_a_log, v_dt_bias, v_gdn_norm_w, v_w_pa, v_w_pd, v_w_out, v_w_up, v_ffn_conv_w, v_ffn_conv_b, v_w_down, v_final_norm_w):
    _, _, _, me = _position()
    mcols = w_mod.shape[2]

    transposed = ("w_in", "w_up")
    big = {"w_in": w_in[0].T, "w_pa": w_pa[0], "w_pd": w_pd[0], "w_out": w_out[0], "w_up": w_up[0].T, "w_down": w_down[0]}
    names = list(big)
    shards = dict(zip(names, _cast_bf16([big[n] for n in names], name="cast_weights")))
    w_in_g, c_all, conv_g, ffnw_g = _gather_two_level([shards["w_in"], c, conv_qkv_w[0], ffn_conv_w[0]],
                                                      name="gather_w_in")
    w_in_pad = _pad_shards(w_in_g)

    c9 = jnp.concatenate([c_all.reshape(NDEV, D), jnp.pad(c_ctx[None], ((0, MODROWS - NDEV - 1), (0, 0)))], axis=0)
    b_loc = lax.dynamic_slice(b_mod, (0, me * mcols), (1, mcols))
    mod_all, = _exchange([_mod_fwd(c9, w_mod[0], b_loc)], name="gather_mod", scatter=False)
    mod_lat = lax.dynamic_index_in_dim(mod_all, me, axis=1, keepdims=False).reshape(6, D)
    mod_ctx = mod_all[:, NDEV, :].reshape(6, D)

    small = {"q_norm_w": q_norm_w, "k_norm_w": k_norm_w, "gdn_norm_w": gdn_norm_w, "a_log": a_log, "dt_bias": dt_bias,
             "conv_qkv_w": _by_columns(conv_g), "ffn_conv_w": _by_columns(ffnw_g), "ffn_conv_b": ffn_conv_b,
             "final_norm_w": final_norm_w[None]}
    loss_me, grad_x, pending_in, recv, dmod_lat, dmod_ctx, gs = _local_step(
        x[0], ctx[0], loss_target[0], mod_lat, mod_ctx, w_in_pad, shards, small)

    moments = {"w_in": (m_w_in, v_w_in), "w_pa": (m_w_pa, v_w_pa), "w_pd": (m_w_pd, v_w_pd),
               "w_out": (m_w_out, v_w_out), "w_up": (m_w_up, v_w_up), "w_down": (m_w_down, v_w_down)}
    res = {}
    def finish(n, outs):
        return tuple((t.T if n in transposed else t)[None] for t in outs)

    def moment(t, n):
        return t[0].T if n in transposed else t[0]

    for n in recv:
        res[n] = finish(n, _adamw_recv(big[n], recv[n], moment(moments[n][0], n), moment(moments[n][1], n),
                                       name="adamw_" + n))

    misc = jnp.concatenate([gs["q_norm_w"][0], gs["k_norm_w"][0], gs["gdn_norm_w"][0], gs["a_log"], gs["dt_bias"],
                            loss_me[None]])
    pack = jnp.concatenate([_rows_of(dmod_lat, P_CTX - P_LAT), _rows_of(dmod_ctx, P_FNW - P_CTX),
                            _rows_of(gs["final_norm_w"], P_FFNB - P_FNW), _rows_of(gs["ffn_conv_b"], P_CONV - P_FFNB),
                            _rows_of(gs["conv_qkv_w"], P_FFNW - P_CONV), _rows_of(gs["ffn_conv_w"], P_MISC - P_FFNW),
                            _rows_of(misc, P_ROWS - P_MISC)], axis=0)
    pack_all, = _exchange([pack], name="gather_pack", scatter=False)
    tot = _sum_slots(pack_all, name="sum_pack")
    dall = jnp.concatenate([pack_all[:, P_LAT:P_LAT + 6, :].reshape(NDEV, 6 * D),
                            jnp.pad(tot[P_CTX:P_CTX + 6].reshape(1, 6 * D), ((0, MODROWS - NDEV - 1), (0, 0)))], axis=0)
    dmy = lax.dynamic_slice(dall, (0, me * mcols), (MODROWS, mcols))
    g_w_mod, g_b_mod, cpart = _mod_bwd(c9, dmy, dall, w_mod[0])
    cparts, = _exchange([cpart], name="gather_cctx", scatter=False)
    sems_a, land = pending_in[:2], pending_in[3]
    *sems_b, g_in_thru, land, token_b = _scatter_start(pending_in[2], land, (D // 2, D // 2), (cparts,),
                                                       name="scatter_g_in_b_start")
    g_c_ctx = _cctx_finish(cparts, c_ctx[None], (token_b,))[0]

    nconv, nffn = 3 * GH * HD, 2 * DFF
    conv_tot = tot[P_CONV:P_FFNW].reshape(-1)[:3 * nconv].reshape(3, nconv)
    ffnw_tot = tot[P_FFNW:P_MISC].reshape(-1)[:3 * nffn].reshape(3, nffn)
    mrow = tot[P_MISC]
    grads = {
        "c_ctx": g_c_ctx, "w_mod": g_w_mod[None], "b_mod": g_b_mod,
        "q_norm_w": mrow[None, 0:HD], "k_norm_w": mrow[None, HD:2 * HD], "gdn_norm_w": mrow[None, 2 * HD:3 * HD],
        "conv_qkv_w": lax.dynamic_slice(conv_tot, (0, me * (nconv // NDEV)), (3, nconv // NDEV))[None],
        "a_log": mrow[3 * HD:3 * HD + 2 * GH].reshape(1, 2, GH),
        "dt_bias": mrow[3 * HD + 2 * GH:3 * HD + 4 * GH].reshape(1, 2, GH),
        "ffn_conv_w": lax.dynamic_slice(ffnw_tot, (0, me * (nffn // NDEV)), (3, nffn // NDEV))[None],
        "ffn_conv_b": tot[P_FFNB:P_CONV].reshape(-1)[:nffn][None],
        "final_norm_w": tot[P_FNW],
    }
    loss = mrow[3 * HD + 4 * GH]
    given = {"c_ctx": (c_ctx, m_c_ctx, v_c_ctx), "w_mod": (w_mod, m_w_mod, v_w_mod), "b_mod": (b_mod, m_b_mod, v_b_mod),
             "q_norm_w": (q_norm_w, m_q_norm_w, v_q_norm_w), "k_norm_w": (k_norm_w, m_k_norm_w, v_k_norm_w),
             "conv_qkv_w": (conv_qkv_w, m_conv_qkv_w, v_conv_qkv_w), "a_log": (a_log, m_a_log, v_a_log),
             "dt_bias": (dt_bias, m_dt_bias, v_dt_bias), "gdn_norm_w": (gdn_norm_w, m_gdn_norm_w, v_gdn_norm_w),
             "ffn_conv_w": (ffn_conv_w, m_ffn_conv_w, v_ffn_conv_w), "ffn_conv_b": (ffn_conv_b, m_ffn_conv_b, v_ffn_conv_b),
             "final_norm_w": (final_norm_w, m_final_norm_w, v_final_norm_w)}
    res["w_mod"] = (grads["w_mod"],) + _adamw(w_mod, grads["w_mod"], m_w_mod, v_w_mod, name="adamw_w_mod")
    small_names = [n for n in given if n != "w_mod"]
    updates = _adamw_many([(given[n][0], grads[n], given[n][1], given[n][2]) for n in small_names], name="adamw_small")
    for n, upd in zip(small_names, updates):
        res[n] = (grads[n],) + upd

    first = me * SHARD_ROWS
    own_in = lax.dynamic_slice(_unpad_columns(g_in_thru), (first - first % ROW_TILE, 0), (SLOT_ROWS, D))
    mine = (big["w_in"], moment(m_w_in, "w_in"), moment(v_w_in, "w_in"))
    g_in_thru, land = _scatter_wait(*sems_a, g_in_thru, land, (0, D // 2), [res[n][1] for n in res] + [own_in, *mine],
                                    name="scatter_g_in_a_wait")
    _, land = _scatter_wait(*sems_b, g_in_thru, land, (D // 2, D // 2), (), name="scatter_g_in_b_wait")
    res["w_in"] = finish("w_in", _adamw_recv(mine[0], land, mine[1], mine[2], name="adamw_w_in", own=own_in))

    order = ["c_ctx", "w_mod", "b_mod", "w_in", "q_norm_w", "k_norm_w", "conv_qkv_w", "a_log", "dt_bias", "gdn_norm_w",
             "w_pa", "w_pd", "w_out", "w_up", "ffn_conv_w", "ffn_conv_b", "w_down", "final_norm_w"]
    return (loss, grad_x[None], *[res[n][0] for n in order], *[res[n][1] for n in order],
            *[res[n][2] for n in order], *[res[n][3] for n in order])
```

```python
import functools
import math

import jax
import jax.numpy as jnp
from jax import lax
from jax.experimental import pallas as pl
from jax.experimental.pallas import tpu as pltpu

F32 = jnp.float32
BF16 = jnp.bfloat16
HI = lax.Precision.HIGHEST
MESH = pl.DeviceIdType.MESH

NDEV = 8
D = 1024
HD = 128
AH, AKV, GRP = 8, 2, 4
GH = 8
CH = 64
DFF = 2816
GRID_W = 64
EPS = 1e-6
ROPE_THETA = 10000.0
LOG2E = math.log2(math.e)
C_KV, C_AQ, C_QKV, C_BL, C_Z, C_GATE, C_END = 0, 512, 1536, 4608, 5120, 6144, 8192
W_QKV, W_AQ, W_Z, W_END = 512, 3616, 4640, 7712


def _pad_columns(w):
    zeros = jnp.zeros((C_Z - C_QKV - (W_AQ - W_QKV), D), w.dtype)
    return jnp.concatenate([w[:W_QKV], w[W_AQ:W_Z], w[W_QKV:W_AQ], zeros, w[W_Z:]], axis=0)


def _unpad_columns(g):
    return jnp.concatenate([g[:C_AQ], g[C_QKV:C_QKV + W_AQ - W_QKV], g[C_AQ:C_QKV], g[C_Z:]], axis=0)
LR, B1, B2, AEPS, WD, STEP = 0.001, 0.9, 0.999, 1e-08, 0.01, 10
VMEM_BIG = 56 * 1024 * 1024
INTRA_FWD_CHUNKS = 36
INTRA_BWD_CHUNKS = 36


def _call(body, *, name, out_shape, grid=None, in_specs=None, out_specs=None, scratch=(), sem=None,
          vmem=None, aliases=None):
    params = {}
    if sem is not None:
        params["dimension_semantics"] = sem
    if vmem is not None:
        params["vmem_limit_bytes"] = vmem
    kw = {}
    if grid is not None:
        kw["grid"] = grid
    if in_specs is not None:
        kw["in_specs"] = in_specs
    if out_specs is not None:
        kw["out_specs"] = out_specs
    if aliases:
        kw["input_output_aliases"] = aliases
    return pl.pallas_call(body, name=name, out_shape=out_shape, scratch_shapes=list(scratch),
                          compiler_params=pltpu.CompilerParams(**params), **kw)


def _call_carrying(body, exch, *, name, out_shape, grid, in_specs, out_specs, scratch=(), vmem=None):
    n, nin, nout, nscr = exch.n, len(in_specs), len(out_shape), len(scratch)
    steps = math.prod(grid)
    mid = (2 * steps) // 3

    def wrapped(*refs):
        ins, cins = refs[:nin], refs[nin:nin + n]
        outs, couts = refs[nin + n:nin + n + nout], refs[nin + n + nout:nin + 2 * n + nout]
        scr, sems = refs[nin + 2 * n + nout:nin + 2 * n + nout + nscr], refs[nin + 2 * n + nout + nscr:]
        ids = [pl.program_id(i) for i in range(len(grid))]
        first = functools.reduce(jnp.logical_and, [i == 0 for i in ids])
        last = functools.reduce(jnp.logical_and, [i == g - 1 for i, g in zip(ids, grid)])

        @pl.when(first)
        def _():
            exch.start(cins, couts, sems)

        if hasattr(exch, "middle"):
            linear = functools.reduce(lambda acc, ig: acc * ig[1] + ig[0], zip(ids, grid), 0)

            @pl.when(linear == mid)
            def _():
                exch.middle(cins, couts, sems)

        body(*ins, *outs, *scr)

        @pl.when(last)
        def _():
            exch.finish(cins, couts, sems)

    params = {"dimension_semantics": ("arbitrary",) * len(grid)}
    if vmem is not None:
        params["vmem_limit_bytes"] = vmem
    fn = pl.pallas_call(wrapped, name=name, out_shape=tuple(out_shape) + exch.out_shape, grid=grid,
                        in_specs=list(in_specs) + [HBM] * n, out_specs=tuple(out_specs) + (HBM,) * n,
                        scratch_shapes=list(scratch) + exch.scratch, compiler_params=pltpu.CompilerParams(**params))

    def run(*args):
        res = fn(*args, *exch.arrs)
        return res[:nout], list(res[nout:])

    return run


def _sds(shape, dtype=F32):
    return jax.ShapeDtypeStruct(tuple(shape), dtype)


def _dot(a, b, ca, cb):
    return lax.dot_general(a.astype(BF16), b.astype(BF16), (((ca,), (cb,)), ((), ())),
                           preferred_element_type=F32)


@jax.custom_vjp
def _nn(a, b):
    return _dot(a, b, 1, 0)


@jax.custom_vjp
def _nt(a, b):
    return _dot(a, b, 1, 1)


@jax.custom_vjp
def _tn(a, b):
    return _dot(a, b, 0, 0)


_nn.defvjp(lambda a, b: (_nn(a, b), (a, b)), lambda r, g: (_nt(g, r[1]), _tn(r[0], g)))
_nt.defvjp(lambda a, b: (_nt(a, b), (a, b)), lambda r, g: (_nn(g, r[1]), _tn(g, r[0])))
_tn.defvjp(lambda a, b: (_tn(a, b), (a, b)), lambda r, g: (_nt(r[1], g), _nn(r[0], g)))


def _mdot(a, b):
    return jnp.dot(a, b, precision=lax.Precision.HIGH, preferred_element_type=F32)


def _maskdot(mask, a, cm):
    hi = a.astype(BF16)
    r = a - hi.astype(F32)
    mid = r.astype(BF16)
    lo = (r - mid.astype(F32)).astype(BF16)
    mb = mask.astype(BF16)
    dims = (((cm,), (0,)), ((), ()))
    return (lax.dot_general(mb, hi, dims, preferred_element_type=F32)
            + lax.dot_general(mb, mid, dims, preferred_element_type=F32)
            + lax.dot_general(mb, lo, dims, preferred_element_type=F32))


@jax.custom_vjp
def _mask_nn(mask, a):
    return _maskdot(mask, a, 1)


_mask_nn.defvjp(lambda mask, a: (_maskdot(mask, a, 1), mask),
                lambda mask, g: (jnp.zeros_like(mask), _maskdot(mask, g, 0)))


@jax.custom_vjp
def _saved_inverse(lmat, x):
    return x


def _saved_inverse_bwd(x, g):
    t = lax.dot_general(x, g, (((0,), (0,)), ((), ())), precision=lax.Precision.HIGH, preferred_element_type=F32)
    dl = lax.dot_general(t, x, (((1,), (1,)), ((), ())), precision=lax.Precision.HIGH, preferred_element_type=F32)
    return -dl, jnp.zeros_like(x)


_saved_inverse.defvjp(lambda lmat, x: (x, x), _saved_inverse_bwd)


def _row_ids(shape):
    return lax.broadcasted_iota(jnp.int32, shape, 0)


def _shift_rows(x, down, bounds):
    n = x.shape[0]
    rows = _row_ids(x.shape)
    y = pltpu.roll(x, 1 if down else n - 1, 0)
    edge = functools.reduce(jnp.logical_or, [rows == (s if down else e - 1) for s, e in bounds])
    return jnp.where(edge, 0.0, y)


def _make_shift(bounds):
    @jax.custom_vjp
    def down(x):
        return _shift_rows(x, True, bounds)

    @jax.custom_vjp
    def up(x):
        return _shift_rows(x, False, bounds)

    down.defvjp(lambda x: (down(x), None), lambda _, g: (up(g),))
    up.defvjp(lambda x: (up(x), None), lambda _, g: (down(g),))
    return down, up


@jax.custom_vjp
def _swap32(x):
    lane = lax.broadcasted_iota(jnp.int32, x.shape, x.ndim - 1)
    return jnp.where((lane % 64) < 32, pltpu.roll(x, HD - 32, x.ndim - 1), pltpu.roll(x, 32, x.ndim - 1))


_swap32.defvjp(lambda x: (_swap32(x), None), lambda _, g: (_swap32(g),))


def _rms(x):
    return x * lax.rsqrt(jnp.mean(x * x, axis=-1, keepdims=True) + EPS)


def _silu(x):
    return x * jax.nn.sigmoid(x)


def _mm(a, b, *, name, M, N, K, ta=False, tb=False, out_dtype=F32, bm=None, bn=None, bk=None, after=()):
    bm, bn, bk = bm or M, bn or N, bk or K
    assert M % bm == 0 and N % bn == 0 and K % bk == 0, (name, M, N, K, bm, bn, bk)
    nk = K // bk
    ca, cb = (0 if ta else 1), (1 if tb else 0)
    na = len(after)

    def body(a_ref, b_ref, *rest):
        o_ref, acc = rest[na], rest[na + 1:]
        r = _dot(a_ref[...], b_ref[...], ca, cb)
        if nk == 1:
            o_ref[...] = r.astype(out_dtype)
        else:
            acc_ref, = acc
            k = pl.program_id(2)

            @pl.when(k == 0)
            def _():
                acc_ref[...] = r

            @pl.when(k > 0)
            def _():
                acc_ref[...] += r

            @pl.when(k == nk - 1)
            def _():
                o_ref[...] = acc_ref[...].astype(out_dtype)

    a_spec = pl.BlockSpec((bk, bm), lambda i, j, k: (k, i)) if ta else pl.BlockSpec((bm, bk), lambda i, j, k: (i, k))
    b_spec = pl.BlockSpec((bn, bk), lambda i, j, k: (j, k)) if tb else pl.BlockSpec((bk, bn), lambda i, j, k: (k, j))
    return _call(body, name=name, out_shape=_sds((M, N), out_dtype), grid=(M // bm, N // bn, nk),
                 in_specs=[a_spec, b_spec] + [pl.BlockSpec(memory_space=pl.ANY)] * na,
                 out_specs=pl.BlockSpec((bm, bn), lambda i, j, k: (i, j)),
                 scratch=[pltpu.VMEM((bm, bn), F32)] if nk > 1 else [],
                 sem=("parallel", "parallel", "arbitrary"), vmem=VMEM_BIG)(a, b, *after)


def _mm_deep(a, b, *, name, M, N, K, bk, after=()):
    assert K % bk == 0
    na = len(after)

    def body(a_hbm, b_hbm, *rest):
        o_ref = rest[na]
        o_ref[...] = jnp.zeros_like(o_ref)

        def step(a_ref, b_ref):
            o_ref[...] += _dot(a_ref[...], b_ref[...], 1, 0)

        deep = pl.Buffered(3)
        pltpu.emit_pipeline(step, grid=(K // bk,),
                            in_specs=[pl.BlockSpec((M, bk), lambda k: (0, k), pipeline_mode=deep),
                                      pl.BlockSpec((bk, N), lambda k: (k, 0), pipeline_mode=deep)])(a_hbm, b_hbm)

    return _call(body, name=name, out_shape=_sds((M, N), F32), in_specs=[ANYSPEC] * (2 + na),
                 out_specs=pl.BlockSpec(memory_space=pltpu.VMEM), vmem=VMEM_BIG)(a, b, *after)


def _normmod_fn(x, sh, sc):
    return _rms(x) * (1.0 + sc) + sh


def _normmod_fwd(x, mod, i_sh, i_sc, *, name, br=256, out_rows=None, off=0, into=None):
    R = x.shape[0]
    ob = off // br
    given = [] if into is None else [into]
    if given:
        out_rows = into.shape[0]

    def body(x_ref, mod_ref, *rest):
        rest[-1][...] = _normmod_fn(x_ref[...], mod_ref[i_sh:i_sh + 1, :], mod_ref[i_sc:i_sc + 1, :]).astype(BF16)

    return _call(body, name=name, out_shape=_sds((out_rows or R, D), BF16), grid=(R // br,),
                 in_specs=[pl.BlockSpec((br, D), lambda i: (i, 0)), pl.BlockSpec((6, D), lambda i: (0, 0))]
                 + [ANYSPEC] * len(given), out_specs=pl.BlockSpec((br, D), lambda i: (i + ob, 0)),
                 aliases={2: 0} if given else None, sem=("parallel",))(x, mod, *given)


def _normmod_bwd(x, mod, i_sh, i_sc, dh, dh_off, res, *, name, br=256):
    R = x.shape[0]
    ob = dh_off // br
    has_res = res is not None

    def body(x_ref, mod_ref, dh_ref, *rest):
        if has_res:
            res_ref, dx_ref, dsh_ref, dsc_ref = rest
        else:
            dx_ref, dsh_ref, dsc_ref = rest
        sh, sc = mod_ref[i_sh:i_sh + 1, :], mod_ref[i_sc:i_sc + 1, :]
        _, vjp = jax.vjp(_normmod_fn, x_ref[...], sh, sc)
        dx, dsh, dsc = vjp(dh_ref[...])
        dx_ref[...] = dx + res_ref[...] if has_res else dx

        @pl.when(pl.program_id(0) == 0)
        def _():
            dsh_ref[...] = jnp.zeros_like(dsh_ref)
            dsc_ref[...] = jnp.zeros_like(dsc_ref)

        dsh_ref[...] += dsh
        dsc_ref[...] += dsc

    row = pl.BlockSpec((br, D), lambda i: (i, 0))
    vec = pl.BlockSpec((1, D), lambda i: (0, 0))
    ins = [row, pl.BlockSpec((6, D), lambda i: (0, 0)), pl.BlockSpec((br, D), lambda i: (i + ob, 0))]
    args = [x, mod, dh]
    if has_res:
        ins.append(row)
        args.append(res)
    return _call(body, name=name, out_shape=(_sds((R, D)), _sds((1, D)), _sds((1, D))), grid=(R // br,),
                 in_specs=ins, out_specs=(row, vec, vec), sem=("arbitrary",))(*args)


def _rope(x, cos, sin):
    return x * cos + _swap32(x) * sin


def _aprep_fn(qs, ks, cos, sin, qw, kw):
    return ([_rope(_rms(q) * qw, cos, sin) for q in qs], [_rope(_rms(k) * kw, cos, sin) for k in ks])


def _aprep_fwd(proj, cos, sin, qw, kw, *, br=256):
    T = proj.shape[0]

    def body(x_ref, cos_ref, sin_ref, qw_ref, kw_ref, q_ref, k_ref, v_ref):
        qs = [x_ref[:, C_AQ + h * HD:C_AQ + (h + 1) * HD] for h in range(AH)]
        ks = [x_ref[:, h * HD:(h + 1) * HD] for h in range(AKV)]
        qo, ko = _aprep_fn(qs, ks, cos_ref[...], sin_ref[...], qw_ref[...], kw_ref[...])
        for h in range(AH):
            q_ref[h] = qo[h].astype(BF16)
        for h in range(AKV):
            k_ref[h] = ko[h].astype(BF16)
            v_ref[h] = x_ref[:, (AKV + h) * HD:(AKV + h + 1) * HD].astype(BF16)

    tab = pl.BlockSpec((br, HD), lambda i: (i, 0))
    vec = pl.BlockSpec((1, HD), lambda i: (0, 0))
    return _call(body, name="aprep_fwd",
                 out_shape=(_sds((AH, T, HD), BF16), _sds((AKV, T, HD), BF16), _sds((AKV, T, HD), BF16)),
                 grid=(T // br,),
                 in_specs=[pl.BlockSpec((br, C_QKV), lambda i: (i, 0)), tab, tab, vec, vec],
                 out_specs=(pl.BlockSpec((AH, br, HD), lambda i: (0, i, 0)),
                            pl.BlockSpec((AKV, br, HD), lambda i: (0, i, 0)),
                            pl.BlockSpec((AKV, br, HD), lambda i: (0, i, 0))),
                 sem=("parallel",))(proj, cos, sin, qw, kw)


def _aprep_bwd(proj, cos, sin, qw, kw, dq, dk, dv, dproj, L, *, br=256):
    T = proj.shape[0]
    lb = L // br

    def body(x_ref, cos_ref, sin_ref, qw_ref, kw_ref, dq_ref, dk_ref, dv_ref, _, dx_ref, dqw_ref, dkw_ref):
        i = pl.program_id(0)
        qs = [x_ref[:, C_AQ + h * HD:C_AQ + (h + 1) * HD] for h in range(AH)]
        ks = [x_ref[:, h * HD:(h + 1) * HD] for h in range(AKV)]
        _, vjp = jax.vjp(_aprep_fn, qs, ks, cos_ref[...], sin_ref[...], qw_ref[...], kw_ref[...])
        is_lat = i >= lb
        dqs = [jnp.where(is_lat, dq_ref[h], 0.0) for h in range(AH)]
        dks = [dk_ref[h] for h in range(AKV)]
        gq, gk, _, _, gqw, gkw = vjp((dqs, dks))
        for h in range(AH):
            dx_ref[:, C_AQ + h * HD:C_AQ + (h + 1) * HD] = gq[h].astype(BF16)
        for h in range(AKV):
            dx_ref[:, h * HD:(h + 1) * HD] = gk[h].astype(BF16)
            dx_ref[:, (AKV + h) * HD:(AKV + h + 1) * HD] = dv_ref[h].astype(BF16)

        @pl.when(i == 0)
        def _():
            dqw_ref[...] = jnp.zeros_like(dqw_ref)
            dkw_ref[...] = jnp.zeros_like(dkw_ref)

        dqw_ref[...] += gqw
        dkw_ref[...] += gkw

    tab = pl.BlockSpec((br, HD), lambda i: (i, 0))
    vec = pl.BlockSpec((1, HD), lambda i: (0, 0))
    kvb = pl.BlockSpec((AKV, br, HD), lambda i: (0, i, 0))
    blk = pl.BlockSpec((br, C_QKV), lambda i: (i, 0))
    return _call(body, name="aprep_bwd", out_shape=(_sds(dproj.shape, BF16), _sds((1, HD)), _sds((1, HD))),
                 grid=(T // br,),
                 in_specs=[blk, tab, tab, vec, vec,
                           pl.BlockSpec((AH, br, HD), lambda i: (0, jnp.maximum(i - lb, 0), 0)), kvb, kvb, ANYSPEC],
                 out_specs=(blk, vec, vec), aliases={8: 0},
                 sem=("arbitrary",))(proj, cos, sin, qw, kw, dq, dk, dv, dproj)


def _attn_grad(q, k, v, o, lse2, do):
    scale = HD ** -0.5
    p = jnp.exp2(_dot(q, k, 1, 1) * (scale * LOG2E) - lse2)
    dp = _dot(do, v, 1, 1)
    ds = p * (dp - jnp.sum(do * o, axis=-1, keepdims=True)) * scale
    return _dot(ds, k, 1, 0), _dot(ds, q, 0, 0), _dot(p, do, 0, 0)


ATTN_KEYS = 256


def _attn_fwd(q, k, v, L, exch, *, bq=128):
    T = q.shape[1]
    N = T - L
    lb = L // bq
    assert T % ATTN_KEYS == 0
    scale = HD ** -0.5
    heads = range(GRP)

    def body(q_ref, k_ref, v_ref, o_ref, o32_ref, lse_ref):
        qs = [q_ref[g] for g in heads]
        m = [jnp.full((bq, 1), -jnp.inf, F32) for _ in heads]
        l = [jnp.zeros((bq, 1), F32) for _ in heads]
        acc = [jnp.zeros((bq, HD), F32) for _ in heads]
        for c in range(T // ATTN_KEYS):
            kc, vc = k_ref[c * ATTN_KEYS:(c + 1) * ATTN_KEYS, :], v_ref[c * ATTN_KEYS:(c + 1) * ATTN_KEYS, :]
            s = [_dot(qs[g], kc, 1, 1) * (scale * LOG2E) for g in heads]
            m_new = [jnp.maximum(m[g], jnp.max(s[g], axis=-1, keepdims=True)) for g in heads]
            alpha = [jnp.exp2(m[g] - m_new[g]) for g in heads]
            p = [jnp.exp2(s[g] - m_new[g]) for g in heads]
            l = [l[g] * alpha[g] + jnp.sum(p[g], axis=-1, keepdims=True) for g in heads]
            acc = [acc[g] * alpha[g] + _dot(p[g], vc, 1, 0) for g in heads]
            m = m_new
        for g in heads:
            o = acc[g] / l[g]
            o_ref[:, g * HD:(g + 1) * HD] = o.astype(BF16)
            o32_ref[:, g * HD:(g + 1) * HD] = o
            lse_ref[g] = jnp.broadcast_to(m[g] + jnp.log2(l[g]), (bq, HD))

    kvb = pl.BlockSpec((None, T, HD), lambda g, i: (g, 0, 0))
    ob = pl.BlockSpec((bq, GRP * HD), lambda g, i: (i, g))
    return _call_carrying(
        body, exch, name="attn_fwd",
        out_shape=(_sds((N, AH * HD), BF16), _sds((N, AH * HD)), _sds((AH, N, HD))), grid=(AKV, N // bq),
        in_specs=[pl.BlockSpec((GRP, bq, HD), lambda g, i: (g, i + lb, 0)), kvb, kvb],
        out_specs=(ob, ob, pl.BlockSpec((GRP, bq, HD), lambda g, i: (g, i, 0))), vmem=VMEM_BIG)(q, k, v)


def _attn_bwd(q, k, v, o32, lse, do, L, exch, *, bq=128):
    T = q.shape[1]
    N = T - L
    lb = L // bq

    def body(q_ref, k_ref, v_ref, o_ref, lse_ref, do_ref, dq_ref, dk_ref, dv_ref):
        rows = lambda r: jnp.concatenate([r[:, g * HD:(g + 1) * HD] for g in range(GRP)], axis=0)
        lse = jnp.max(lse_ref[...].reshape(GRP * bq, HD), axis=-1, keepdims=True)
        dq, dk, dv = _attn_grad(q_ref[...].reshape(GRP * bq, HD), k_ref[...], v_ref[...], rows(o_ref), lse, rows(do_ref))
        dq_ref[...] = dq.reshape(GRP, bq, HD)

        @pl.when(pl.program_id(1) == 0)
        def _():
            dk_ref[...] = jnp.zeros_like(dk_ref)
            dv_ref[...] = jnp.zeros_like(dv_ref)

        dk_ref[...] += dk
        dv_ref[...] += dv

    kvb = pl.BlockSpec((None, T, HD), lambda g, i: (g, 0, 0))
    qb = pl.BlockSpec((GRP, bq, HD), lambda g, i: (g, i + lb, 0))
    hb = pl.BlockSpec((GRP, bq, HD), lambda g, i: (g, i, 0))
    ob = pl.BlockSpec((bq, GRP * HD), lambda g, i: (i, g))
    return _call_carrying(body, exch, name="attn_bwd",
                          out_shape=(_sds((AH, N, HD)), _sds((AKV, T, HD)), _sds((AKV, T, HD))), grid=(AKV, N // bq),
                          in_specs=[qb, kvb, kvb, ob, hb, ob], out_specs=(hb, kvb, kvb),
                          vmem=VMEM_BIG)(q, k, v, o32, lse, do)


def _gprep_fn(kind, shifts, x, w):
    down, up = shifts
    y = down(x) * w[0:1, :] + x * w[1:2, :] + up(x) * w[2:3, :]
    a = _silu(y)
    if kind == 2:
        return a
    a = a * lax.rsqrt(jnp.sum(a * a, axis=-1, keepdims=True) + EPS)
    return a * (HD ** -0.5) if kind == 0 else a


def _gprep_fwd(proj, conv_w, kind, bounds):
    T = proj.shape[0]
    shifts = _make_shift(bounds)
    cb = C_QKV // HD + kind * GH

    def body(x_ref, w_ref, o_ref):
        o_ref[...] = _gprep_fn(kind, shifts, x_ref[...], w_ref[...])

    return _call(body, name=f"gprep_fwd{kind}", out_shape=_sds((GH, T, HD)), grid=(GH,),
                 in_specs=[pl.BlockSpec((T, HD), lambda h: (0, cb + h)),
                           pl.BlockSpec((3, HD), lambda h: (0, kind * GH + h))],
                 out_specs=pl.BlockSpec((None, T, HD), lambda h: (h, 0, 0)), sem=("parallel",))(proj, conv_w)


def _gprep_bwd(proj, conv_w, kind, bounds, dy, dproj):
    T = proj.shape[0]
    shifts = _make_shift(bounds)
    cb = C_QKV // HD + kind * GH

    def body(x_ref, w_ref, dy_ref, _, dx_ref, dw_ref):
        _, vjp = jax.vjp(functools.partial(_gprep_fn, kind, shifts), x_ref[...], w_ref[...])
        dx, dw = vjp(dy_ref[0] + dy_ref[1])
        dx_ref[...] = dx.astype(BF16)
        dw_ref[...] = dw

    return _call(body, name=f"gprep_bwd{kind}", out_shape=(_sds(dproj.shape, BF16), _sds((3, GH * HD))), grid=(GH,),
                 in_specs=[pl.BlockSpec((T, HD), lambda h: (0, cb + h)),
                           pl.BlockSpec((3, HD), lambda h: (0, kind * GH + h)),
                           pl.BlockSpec((2, None, T, HD), lambda h: (0, h, 0, 0)), ANYSPEC],
                 out_specs=(pl.BlockSpec((T, HD), lambda h: (0, cb + h)), pl.BlockSpec((3, HD), lambda h: (0, h))),
                 aliases={3: 0}, sem=("parallel",))(proj, conv_w, dy, dproj)


def _bl_fn(x, alog, dtb):
    lane = lax.broadcasted_iota(jnp.int32, x.shape, 1)
    beta = jax.nn.sigmoid(x)
    z = x + dtb
    sp = jnp.maximum(z, 0.0) + jnp.log1p(jnp.exp(-jnp.abs(z)))
    la = -jnp.exp(alog) * sp
    return jnp.where(lane < 2 * GH, beta, jnp.where(lane < 4 * GH, la, 0.0))


def _bl_fwd(proj, alog, dtb, *, br=256):
    T = proj.shape[0]

    def body(x_ref, a_ref, d_ref, o_ref):
        o_ref[...] = _bl_fn(x_ref[...], a_ref[...], d_ref[...])

    vec = pl.BlockSpec((1, HD), lambda i: (0, 0))
    return _call(body, name="bl_fwd", out_shape=_sds((T, HD)), grid=(T // br,),
                 in_specs=[pl.BlockSpec((br, HD), lambda i: (i, C_BL // HD)), vec, vec],
                 out_specs=pl.BlockSpec((br, HD), lambda i: (i, 0)), sem=("parallel",))(proj, alog, dtb)


def _bl_bwd(proj, alog, dtb, dbl, dproj, *, br=256):
    T = proj.shape[0]
    wide = C_Z - C_BL

    def body(x_ref, a_ref, d_ref, g_ref, _, dx_ref, da_ref, dd_ref):
        g = g_ref[0, 0]
        for d in range(2):
            for h in range(GH):
                if d or h:
                    g = g + g_ref[d, h]
        _, vjp = jax.vjp(_bl_fn, x_ref[...], a_ref[...], d_ref[...])
        dx, da, dd = vjp(g)
        dx_ref[:, :HD] = dx.astype(BF16)
        dx_ref[:, HD:] = jnp.zeros((br, wide - HD), BF16)

        @pl.when(pl.program_id(0) == 0)
        def _():
            da_ref[...] = jnp.zeros_like(da_ref)
            dd_ref[...] = jnp.zeros_like(dd_ref)

        da_ref[...] += da
        dd_ref[...] += dd

    vec = pl.BlockSpec((1, HD), lambda i: (0, 0))
    return _call(body, name="bl_bwd", out_shape=(_sds(dproj.shape, BF16), _sds((1, HD)), _sds((1, HD))), grid=(T // br,),
                 in_specs=[pl.BlockSpec((br, HD), lambda i: (i, C_BL // HD)), vec, vec,
                           pl.BlockSpec((2, GH, br, HD), lambda i: (0, 0, i, 0)), ANYSPEC],
                 out_specs=(pl.BlockSpec((br, wide), lambda i: (i, C_BL // wide)), vec, vec), aliases={4: 0},
                 sem=("arbitrary",))(proj, alog, dtb, dbl, dproj)


def _chunk_masks(d):
    ii = lax.broadcasted_iota(jnp.int32, (CH, CH), 0)
    jj = lax.broadcasted_iota(jnp.int32, (CH, CH), 1)
    eye = (ii == jj).astype(F32)
    before = jnp.where(d == 0, (jj < ii).astype(F32), (jj > ii).astype(F32))
    return before, before + eye, eye


def _same_block(b):
    ii = lax.broadcasted_iota(jnp.int32, (CH, CH), 0)
    jj = lax.broadcasted_iota(jnp.int32, (CH, CH), 1)
    shift = b.bit_length() - 1
    return (jnp.right_shift(ii, shift) == jnp.right_shift(jj, shift)).astype(F32)


def _intra_fn(masks, sel_b, sel_l, qs, ks, vs, bls, xs=None):
    before, ateq, eye = masks
    inc = ateq > 0.0
    each = lambda f, *ls: [f(*t) for t in zip(*ls)]
    beta = each(lambda bl: jnp.sum(bl * sel_b, axis=-1, keepdims=True), bls)
    la = each(lambda bl: jnp.sum(bl * sel_l, axis=-1, keepdims=True), bls)
    gam = each(lambda a: _mask_nn(ateq, jnp.broadcast_to(a, (CH, HD))), la)
    gi = each(lambda g: g[:, :CH], gam)
    gj = each(lambda g: jnp.transpose(g)[:CH, :], gam)
    kq = each(lambda k, q: _nt(jnp.concatenate([k, q], axis=0), k), ks, qs)
    kk = each(lambda t: t[:CH], kq)
    qk = each(lambda t: t[CH:], kq)
    dec = each(lambda a, b: jnp.where(inc, jnp.exp(jnp.where(inc, a - b, 0.0)), 0.0), gi, gj)
    lmat = each(lambda b, d, m: before * (b * d * m), beta, dec, kk)
    if xs is None:
        same = lambda b: _same_block(b)
        l8 = each(lambda m: m * same(8), lmat)
        x = each(lambda m: eye - m, l8)
        p2 = each(lambda m: _mdot(m, m), l8)
        y = each(lambda a, b: _mdot(jnp.concatenate([a, b], axis=0), b), x, p2)
        x = each(lambda a, t: a + t[:CH], x, y)
        x = each(lambda a, t: a + _mdot(a, t[CH:]), x, y)
        for b in (8, 16, 32):
            below = same(2 * b) - same(b)
            x = each(lambda a, m: a - _mdot(a, _mdot(m * below, a)), x, lmat)
    else:
        x = each(_saved_inverse, lmat, xs)
    eg = each(jnp.exp, gam)
    uw = each(lambda a, b, v, e, k: _mdot(a, jnp.concatenate([b * v, (b * e) * k], axis=1)), x, beta, vs, eg, ks)
    u = each(lambda t: t[:, :HD], uw)
    w = each(lambda t: t[:, HD:], uw)
    tot = each(lambda a: jnp.sum(a, axis=0, keepdims=True), la)
    kd = each(lambda k, t, g: k * jnp.exp(t - g), ks, tot, gam)
    gl = each(lambda t: jnp.broadcast_to(jnp.exp(t), (1, HD)), tot)
    qd = each(lambda q, e: q * e, qs, eg)
    p = each(lambda d, m: d * m, dec, qk)
    return (u, w, kd, qd, p, gl, x) if xs is None else (u, w, kd, qd, p, gl)


def _dir_head_sel(d, h):
    lane = lax.broadcasted_iota(jnp.int32, (1, HD), 1)
    return (lane == d * GH + h).astype(F32), (lane == 2 * GH + d * GH + h).astype(F32)


def _intra_specs(T, G):
    nc = T // CH
    assert nc % G == 0
    qkv = pl.BlockSpec((None, G * CH, HD), lambda d, h, c: (h, c, 0))
    bl = pl.BlockSpec((G * CH, HD), lambda d, h, c: (c, 0))
    big = pl.BlockSpec((None, None, G * CH, HD), lambda d, h, c: (d, h, c, 0))
    pm = pl.BlockSpec((None, None, G * CH, CH), lambda d, h, c: (d, h, c, 0))
    gl = pl.BlockSpec((None, None, G, 1, HD), lambda d, h, c: (d, h, c, 0, 0))
    shapes = (_sds((2, GH, T, HD)),) + (_sds((2, GH, T, HD), BF16),) * 3 + (
        _sds((2, GH, T, CH), BF16), _sds((2, GH, nc, 1, HD)), _sds((2, GH, T, CH)))
    return nc, qkv, bl, big, pm, gl, shapes


def _chunks_per_step(T, most):
    nc = T // CH
    return max(g for g in range(1, most + 1) if nc % g == 0)


def _chunk_at(g, d, nc, ncc):
    pos = _visit_pos(g, d, nc, ncc)
    return pos, pl.ds(pl.multiple_of(pos * CH, CH), CH)


def _intra_fwd(q, k, v, bl, L, exch):
    T = q.shape[1]
    G = _chunks_per_step(T, INTRA_FWD_CHUNKS)
    nc, qkv_s, bl_s, big, pm, gl_s, shapes = _intra_specs(T, G)
    assert G == nc
    ncc = L // CH

    def body(q_ref, k_ref, v_ref, bl_ref, u_ref, w_ref, kd_ref, qd_ref, p_ref, gl_ref, x_ref):
        d, h = pl.program_id(0), pl.program_id(1)
        sb, sl = _dir_head_sel(d, h)
        rows = [slice(g * CH, (g + 1) * CH) for g in range(G)]
        outs = _intra_fn(_chunk_masks(d), sb, sl, *[[r[s, :] for s in rows] for r in (q_ref, k_ref, v_ref, bl_ref)])
        for g in range(G):
            pos, at = _chunk_at(g, d, nc, ncc)
            for r, o in zip((u_ref, w_ref, kd_ref, qd_ref, p_ref, x_ref), outs[:5] + outs[6:]):
                r[at, :] = o[g].astype(r.dtype)
            gl_ref[pos] = outs[5][g]

    return _call_carrying(body, exch, name="gdn_intra_fwd", out_shape=shapes, grid=(2, GH, nc // G),
                          in_specs=[qkv_s, qkv_s, qkv_s, bl_s], out_specs=(big, big, big, big, pm, gl_s, pm))(q, k, v, bl)


def _intra_bwd(q, k, v, bl, xinv, cts, L, exch):
    T = q.shape[1]
    G = _chunks_per_step(T, INTRA_BWD_CHUNKS)
    nc, qkv_s, bl_s, big, pm, gl_s, _ = _intra_specs(T, G)
    assert G == nc
    ncc = L // CH

    def body(q_ref, k_ref, v_ref, bl_ref, x_ref, du, dw, dkd, dqd, dp, dgl, dq_ref, dk_ref, dv_ref, dbl_ref):
        d, h = pl.program_id(0), pl.program_id(1)
        sb, sl = _dir_head_sel(d, h)
        rows = [slice(g * CH, (g + 1) * CH) for g in range(G)]
        places = [_chunk_at(g, d, nc, ncc) for g in range(G)]
        fn = functools.partial(_intra_fn, _chunk_masks(d), sb, sl, xs=[x_ref[at, :] for _, at in places])
        _, vjp = jax.vjp(fn, *[[r[s, :] for s in rows] for r in (q_ref, k_ref, v_ref, bl_ref)])
        cts = tuple([r[at, :] for _, at in places] for r in (du, dw, dkd, dqd, dp)) + ([dgl[pos] for pos, _ in places],)
        grads = vjp(cts)
        for g in range(G):
            for r, o in zip((dq_ref, dk_ref, dv_ref, dbl_ref), grads):
                r[rows[g], :] = o[g]

    return _call_carrying(body, exch, name="gdn_intra_bwd", out_shape=(_sds((2, GH, T, HD)),) * 4,
                          grid=(2, GH, nc // G), in_specs=[qkv_s, qkv_s, qkv_s, bl_s, pm, big, big, big, big, pm, gl_s],
                          out_specs=(big,) * 4)(q, k, v, bl, xinv, *cts)


def _scan_fn(s, u, w, kd, qd, p, gl):
    each = lambda f, *ls: [f(*t) for t in zip(*ls)]
    ws = each(_nn, w, s)
    delta = each(lambda a, b: a - b, u, ws)
    kdd = each(_tn, kd, delta)
    s_new = each(lambda g, a, b: g * a + b, gl, s, kdd)
    qs = each(_nn, qd, s)
    pd = each(_nn, p, delta)
    return each(lambda a, b: a + b, qs, pd), s_new


SCAN_BLOCK = 4


def _visit_pos(c, d, nc, ncc):
    back = ncc - 1 - c if c < ncc else ncc + (nc - 1 - c)
    return jnp.where(d == 0, c, back)


def _scan_specs(T, L, back):
    tb = SCAN_BLOCK * CH
    assert T % tb == 0 and L % tb == 0
    nb, ncb = T // tb, L // tb
    at = (lambda t: nb - 1 - t) if back else (lambda t: t)
    big = pl.BlockSpec((2, GH, tb, HD), lambda t: (0, 0, at(t), 0))
    pm = pl.BlockSpec((2, GH, tb, CH), lambda t: (0, 0, at(t), 0))
    gl = pl.BlockSpec((2, GH, SCAN_BLOCK, 1, HD), lambda t: (0, 0, at(t), 0, 0))
    st = pl.BlockSpec((2, GH, SCAN_BLOCK, HD, HD), lambda t: (0, 0, at(t), 0, 0))

    def natural(b):
        return jnp.where(b < ncb, ncb - 1 - b, nb - 1 - (b - ncb))

    do_specs = (pl.BlockSpec((GH, tb, HD), lambda t: (0, at(t), 0)),
                pl.BlockSpec((GH, tb, HD), lambda t: (0, natural(at(t)), 0)))
    return nb, big, pm, gl, st, do_specs


SCAN_STREAMS = [(d, h) for d in (0, 1) for h in range(GH)]


def _scan_fwd(u, w, kd, qd, p, gl, L):
    T = u.shape[2]
    nb, big, pm, gl_s, st, _ = _scan_specs(T, L, False)

    def body(u_ref, w_ref, kd_ref, qd_ref, p_ref, gl_ref, o_ref, st_ref, s_scr):
        @pl.when(pl.program_id(0) == 0)
        def _():
            s_scr[...] = jnp.zeros_like(s_scr)

        s = [s_scr[d, h] for d, h in SCAN_STREAMS]
        for i in range(SCAN_BLOCK):
            rows = slice(i * CH, (i + 1) * CH)
            for (d, h), sv in zip(SCAN_STREAMS, s):
                st_ref[d, h, i] = sv
            o, s = _scan_fn(s, *[[r[d, h, rows, :].astype(F32) for d, h in SCAN_STREAMS]
                                 for r in (u_ref, w_ref, kd_ref, qd_ref, p_ref)],
                            [gl_ref[d, h, i] for d, h in SCAN_STREAMS])
            for (d, h), ov in zip(SCAN_STREAMS, o):
                o_ref[d, h, rows, :] = ov
        for (d, h), sv in zip(SCAN_STREAMS, s):
            s_scr[d, h] = sv

    return _call(body, name="gdn_scan_fwd", out_shape=(_sds((2, GH, T, HD)), _sds((2, GH, T // CH, HD, HD))),
                 grid=(nb,), in_specs=[big, big, big, big, pm, gl_s], out_specs=(big, st),
                 scratch=[pltpu.VMEM((2, GH, HD, HD), F32)], sem=("arbitrary",), vmem=VMEM_BIG)(u, w, kd, qd, p, gl)


def _scan_bwd(u, w, kd, qd, p, gl, states, do, L, exch):
    T = u.shape[2]
    nb, big, pm, gl_s, st, do_specs = _scan_specs(T, L, True)

    def body(u_ref, w_ref, kd_ref, qd_ref, p_ref, gl_ref, st_ref, do0_ref, do1_ref,
             du_ref, dw_ref, dkd_ref, dqd_ref, dp_ref, dgl_ref, ds_scr):
        @pl.when(pl.program_id(0) == 0)
        def _():
            ds_scr[...] = jnp.zeros_like(ds_scr)

        ds = [ds_scr[d, h] for d, h in SCAN_STREAMS]
        for i in reversed(range(SCAN_BLOCK)):
            rows = slice(i * CH, (i + 1) * CH)
            mirror = slice((SCAN_BLOCK - 1 - i) * CH, (SCAN_BLOCK - i) * CH)
            _, vjp = jax.vjp(_scan_fn, [st_ref[d, h, i] for d, h in SCAN_STREAMS],
                             *[[r[d, h, rows, :].astype(F32) for d, h in SCAN_STREAMS]
                               for r in (u_ref, w_ref, kd_ref, qd_ref, p_ref)],
                             [gl_ref[d, h, i] for d, h in SCAN_STREAMS])
            dos = [do0_ref[h, rows, :] if d == 0 else do1_ref[h, mirror, :] for d, h in SCAN_STREAMS]
            ds, gu, gw, gkd, gqd, gp, ggl = vjp((dos, ds))
            for n, (d, h) in enumerate(SCAN_STREAMS):
                du_ref[d, h, rows, :] = gu[n]
                dw_ref[d, h, rows, :] = gw[n]
                dkd_ref[d, h, rows, :] = gkd[n]
                dqd_ref[d, h, rows, :] = gqd[n]
                dp_ref[d, h, rows, :] = gp[n]
                dgl_ref[d, h, i] = ggl[n]
        for (d, h), dv in zip(SCAN_STREAMS, ds):
            ds_scr[d, h] = dv

    return _call_carrying(
        body, exch, name="gdn_scan_bwd",
        out_shape=(_sds((2, GH, T, HD)),) * 4 + (_sds((2, GH, T, CH)), _sds((2, GH, T // CH, 1, HD))),
        grid=(nb,), in_specs=[big, big, big, big, pm, gl_s, st, *do_specs], out_specs=(big, big, big, big, pm, gl_s),
        scratch=[pltpu.VMEM((2, GH, HD, HD), F32)], vmem=VMEM_BIG)(u, w, kd, qd, p, gl, states, do, do)


def _gout_fn(o0, o1, z, gw):
    return _rms(o0 + o1) * gw * _silu(z)


def _backward_latent(o_ref, L):
    nl = (o_ref.shape[1] - L) // CH
    return jnp.concatenate([o_ref[1, L + (nl - 1 - j) * CH:L + (nl - j) * CH, :] for j in range(nl)], axis=0)


def _gout_fwd(o, proj, gw, L):
    T = o.shape[2]
    N = T - L
    ob = pl.BlockSpec((2, None, T, HD), lambda h: (0, h, 0, 0))

    def body(o_ref, z_ref, gw_ref, y_ref):
        y_ref[...] = _gout_fn(o_ref[0, L:, :], _backward_latent(o_ref, L), z_ref[L:, :], gw_ref[...]).astype(BF16)

    return _call(body, name="gout_fwd", out_shape=_sds((N, GH * HD), BF16), grid=(GH,),
                 in_specs=[ob, pl.BlockSpec((T, HD), lambda h: (0, C_Z // HD + h)), pl.BlockSpec((1, HD), lambda h: (0, 0))],
                 out_specs=pl.BlockSpec((N, HD), lambda h: (0, h)), sem=("parallel",))(o, proj, gw)


def _gout_bwd(o, proj, gw, dy, dproj, L):
    T = o.shape[2]
    N = T - L
    ob = pl.BlockSpec((2, None, T, HD), lambda h: (0, h, 0, 0))

    def body(o_ref, z_ref, gw_ref, dy_ref, _, do_ref, dz_ref, dgw_ref):
        _, vjp = jax.vjp(_gout_fn, o_ref[0, L:, :], _backward_latent(o_ref, L), z_ref[L:, :], gw_ref[...])
        g0, _, gz, ggw = vjp(dy_ref[...])
        do_ref[:L, :] = jnp.zeros((L, HD), F32)
        do_ref[L:, :] = g0
        dz_ref[:L, :] = jnp.zeros((L, HD), BF16)
        dz_ref[L:, :] = gz.astype(BF16)

        @pl.when(pl.program_id(0) == 0)
        def _():
            dgw_ref[...] = jnp.zeros_like(dgw_ref)

        dgw_ref[...] += ggw

    zb = pl.BlockSpec((T, HD), lambda h: (0, C_Z // HD + h))
    return _call(body, name="gout_bwd", out_shape=(_sds((GH, T, HD)), _sds(dproj.shape, BF16), _sds((1, HD))),
                 grid=(GH,),
                 in_specs=[ob, zb, pl.BlockSpec((1, HD), lambda h: (0, 0)), pl.BlockSpec((N, HD), lambda h: (0, h)), ANYSPEC],
                 out_specs=(pl.BlockSpec((None, T, HD), lambda h: (h, 0, 0)), zb, pl.BlockSpec((1, HD), lambda h: (0, 0))),
                 aliases={4: 1}, sem=("arbitrary",))(o, proj, gw, dy, dproj)


def _merge_fn(pa, pd, ga, gd):
    return jax.nn.sigmoid(ga) * pa + jax.nn.sigmoid(gd) * pd


def _merge_fwd(pa, pd, proj, L, *, br=256):
    N = pa.shape[0]
    lb = L // br
    row = pl.BlockSpec((br, D), lambda i: (i, 0))

    def body(pa_ref, pd_ref, ga_ref, gd_ref, y_ref):
        y_ref[...] = _merge_fn(pa_ref[...], pd_ref[...], ga_ref[...], gd_ref[...]).astype(BF16)

    return _call(body, name="merge_fwd", out_shape=_sds((N, D), BF16), grid=(N // br,),
                 in_specs=[row, row, pl.BlockSpec((br, D), lambda i: (i + lb, C_GATE // D)),
                           pl.BlockSpec((br, D), lambda i: (i + lb, C_GATE // D + 1))],
                 out_specs=row, sem=("parallel",))(pa, pd, proj, proj)


def _merge_bwd(pa, pd, proj, dy, L, *, br=256):
    N = pa.shape[0]
    T = N + L
    lb = L // br
    lrow = pl.BlockSpec((br, D), lambda i: (jnp.maximum(i - lb, 0), 0))

    def body(pa_ref, pd_ref, ga_ref, gd_ref, dy_ref, dpa_ref, dpd_ref, dg_ref):
        lat = pl.program_id(0) >= lb
        _, vjp = jax.vjp(_merge_fn, pa_ref[...], pd_ref[...], ga_ref[...], gd_ref[...])
        gpa, gpd, gga, ggd = vjp(dy_ref[...])
        dpa_ref[...] = gpa.astype(BF16)
        dpd_ref[...] = gpd.astype(BF16)
        dg_ref[:, :D] = jnp.where(lat, gga, 0.0).astype(BF16)
        dg_ref[:, D:] = jnp.where(lat, ggd, 0.0).astype(BF16)

    return _call(body, name="merge_bwd", out_shape=(_sds((N, D), BF16), _sds((N, D), BF16), _sds((T, C_END), BF16)),
                 grid=(T // br,),
                 in_specs=[lrow, lrow, pl.BlockSpec((br, D), lambda i: (i, C_GATE // D)),
                           pl.BlockSpec((br, D), lambda i: (i, C_GATE // D + 1)), lrow],
                 out_specs=(lrow, lrow, pl.BlockSpec((br, 2 * D), lambda i: (i, C_GATE // (2 * D)))),
                 sem=("arbitrary",))(pa, pd, proj, proj, dy)


def _resid_fwd(x, m, mod, i_g, *, name, br=256):
    R = x.shape[0]
    row = pl.BlockSpec((br, D), lambda i: (i, 0))

    def body(x_ref, m_ref, mod_ref, o_ref):
        o_ref[...] = x_ref[...] + mod_ref[i_g:i_g + 1, :] * m_ref[...]

    return _call(body, name=name, out_shape=_sds((R, D)), grid=(R // br,),
                 in_specs=[row, row, pl.BlockSpec((6, D), lambda i: (0, 0))], out_specs=row,
                 sem=("parallel",))(x, m, mod)


def _resid_bwd(dx, m, mod, i_g, *, name, br=256):
    R = dx.shape[0]
    row = pl.BlockSpec((br, D), lambda i: (i, 0))
    vec = pl.BlockSpec((1, D), lambda i: (0, 0))

    def body(dx_ref, m_ref, mod_ref, dm_ref, dg_ref):
        dxv = dx_ref[...]
        dm_ref[...] = (dxv * mod_ref[i_g:i_g + 1, :]).astype(BF16)

        @pl.when(pl.program_id(0) == 0)
        def _():
            dg_ref[...] = jnp.zeros_like(dg_ref)

        dg_ref[...] += jnp.sum(dxv * m_ref[...], axis=0, keepdims=True)

    return _call(body, name=name, out_shape=(_sds((R, D), BF16), _sds((1, D))), grid=(R // br,),
                 in_specs=[row, row, pl.BlockSpec((6, D), lambda i: (0, 0))], out_specs=(row, vec),
                 sem=("arbitrary",))(dx, m, mod)


def _ffn_fn(shifts, ug, uv, wg, wv, bg, bv):
    down, up = shifts

    def conv(x, w, b):
        return down(x) * w[0:1, :] + x * w[1:2, :] + up(x) * w[2:3, :] + b

    return _silu(conv(ug, wg, bg)) * conv(uv, wv, bv)


def _ffn_fwd(up, cw, cb, *, bw=256):
    N = up.shape[0]
    shifts = _make_shift(((0, N),))
    nb = DFF // bw

    def body(ug, uv, wg, wv, bg, bv, a_ref):
        a_ref[...] = _ffn_fn(shifts, ug[...], uv[...], wg[...], wv[...], bg[...], bv[...]).astype(BF16)

    def col(rows, off):
        return pl.BlockSpec((rows, bw), lambda j: (0, j + off))

    return _call(body, name="ffn_fwd", out_shape=_sds((N, DFF), BF16), grid=(nb,),
                 in_specs=[col(N, 0), col(N, nb), col(3, 0), col(3, nb), col(1, 0), col(1, nb)],
                 out_specs=col(N, 0), sem=("parallel",), vmem=VMEM_BIG)(up, up, cw, cw, cb, cb)


def _ffn_bwd(up, cw, cb, da, *, bw=256):
    N = up.shape[0]
    shifts = _make_shift(((0, N),))
    nb = DFF // bw

    def body(ug, uv, wg, wv, bg, bv, da_ref, dug, duv, dwg, dwv, dbg, dbv):
        _, vjp = jax.vjp(functools.partial(_ffn_fn, shifts), ug[...], uv[...], wg[...], wv[...], bg[...], bv[...])
        g = vjp(da_ref[...])
        dug[...] = g[0].astype(BF16)
        duv[...] = g[1].astype(BF16)
        dwg[...], dwv[...], dbg[...], dbv[...] = g[2], g[3], g[4], g[5]

    def col(rows, off):
        return pl.BlockSpec((rows, bw), lambda j: (0, j + off))

    half = (_sds((N, DFF), BF16), _sds((N, DFF), BF16), _sds((3, DFF)), _sds((3, DFF)), _sds((1, DFF)), _sds((1, DFF)))
    dug, duv, dwg, dwv, dbg, dbv = _call(
        body, name="ffn_bwd", out_shape=half, grid=(nb,),
        in_specs=[col(N, 0), col(N, nb), col(3, 0), col(3, nb), col(1, 0), col(1, nb), col(N, 0)],
        out_specs=(col(N, 0), col(N, 0), col(3, 0), col(3, 0), col(1, 0), col(1, 0)),
        sem=("parallel",), vmem=VMEM_BIG)(up, up, cw, cw, cb, cb, da)
    return (jnp.concatenate([dug, duv], axis=1), jnp.concatenate([dwg, dwv], axis=1),
            jnp.concatenate([dbg, dbv], axis=1))


def _head_fn(x1, dn, g2, fw, tgt):
    y = _rms(x1 + g2 * dn) * fw
    err = y - tgt
    return 0.5 * jnp.sum(jnp.mean(err * err, axis=-1))


def _head(x1, dn, mod, fw, tgt, *, br=256):
    N = x1.shape[0]
    row = pl.BlockSpec((br, D), lambda i: (i, 0))
    vec = pl.BlockSpec((1, D), lambda i: (0, 0))
    one = pl.BlockSpec((1, HD), lambda i: (0, 0))

    def body(x1_ref, dn_ref, mod_ref, fw_ref, tgt_ref, loss_ref, dx_ref, ddn_ref, dg_ref, dfw_ref):
        loss, (gx, gdn, gg, gfw) = jax.value_and_grad(_head_fn, argnums=(0, 1, 2, 3))(
            x1_ref[...], dn_ref[...], mod_ref[5:6, :], fw_ref[...], tgt_ref[...])
        dx_ref[...] = gx
        ddn_ref[...] = gdn.astype(BF16)

        @pl.when(pl.program_id(0) == 0)
        def _():
            loss_ref[...] = jnp.zeros_like(loss_ref)
            dg_ref[...] = jnp.zeros_like(dg_ref)
            dfw_ref[...] = jnp.zeros_like(dfw_ref)

        loss_ref[...] += jnp.broadcast_to(loss, (1, HD))
        dg_ref[...] += gg
        dfw_ref[...] += gfw

    return _call(body, name="head", out_shape=(_sds((1, HD)), _sds((N, D)), _sds((N, D), BF16), _sds((1, D)), _sds((1, D))),
                 grid=(N // br,), in_specs=[row, row, pl.BlockSpec((6, D), lambda i: (0, 0)), vec, row],
                 out_specs=(one, row, row, vec, vec), sem=("arbitrary",))(x1, dn, mod, fw, tgt)


def _adamw(w, g, m, v, *, name):
    shape = w.shape
    cols = shape[-1]
    rows = max(1, math.prod(shape[:-1]))
    w2, g2, m2, v2 = (t.reshape(rows, cols) for t in (w, g, m, v))
    br = 256 if rows % 256 == 0 else rows
    c1 = 1.0 - B1 ** STEP
    c2 = 1.0 - B2 ** STEP

    def body(w_ref, g_ref, m_ref, v_ref, d_ref, nm_ref, nv_ref):
        gv = g_ref[...]
        nm = B1 * m_ref[...] + (1.0 - B1) * gv
        nv = B2 * v_ref[...] + (1.0 - B2) * (gv * gv)
        d_ref[...] = -LR * ((nm / c1) / (jnp.sqrt(nv / c2) + AEPS) + WD * w_ref[...])
        nm_ref[...] = nm
        nv_ref[...] = nv

    blk = pl.BlockSpec((br, cols), lambda i: (i, 0))
    outs = _call(body, name=name, out_shape=(_sds((rows, cols)),) * 3, grid=(rows // br,),
                 in_specs=[blk] * 4, out_specs=(blk,) * 3, sem=("parallel",))(w2, g2, m2, v2)
    return tuple(t.reshape(shape) for t in outs)


def _adamw_many(items, *, name):
    k = len(items)
    shapes = [w.shape for w, _, _, _ in items]
    flat = [t.reshape(max(1, math.prod(t.shape[:-1])), t.shape[-1]) for it in items for t in it]
    c1 = 1.0 - B1 ** STEP
    c2 = 1.0 - B2 ** STEP

    def body(*refs):
        ins, outs = refs[:4 * k], refs[4 * k:]
        for i in range(k):
            w_ref, g_ref, m_ref, v_ref = ins[4 * i:4 * i + 4]
            gv = g_ref[...]
            nm = B1 * m_ref[...] + (1.0 - B1) * gv
            nv = B2 * v_ref[...] + (1.0 - B2) * (gv * gv)
            outs[3 * i][...] = -LR * ((nm / c1) / (jnp.sqrt(nv / c2) + AEPS) + WD * w_ref[...])
            outs[3 * i + 1][...] = nm
            outs[3 * i + 2][...] = nv

    res = _call(body, name=name, out_shape=tuple(_sds(flat[4 * i].shape) for i in range(k) for _ in range(3)))(*flat)
    return [tuple(res[3 * i + j].reshape(shapes[i]) for j in range(3)) for i in range(k)]


def _rope_tables(N, L):
    t = jnp.arange(N)
    pos = jnp.stack([(t // GRID_W).astype(F32), (t % GRID_W).astype(F32)], axis=1)
    inv = ROPE_THETA ** (-jnp.arange(0, HD // 2, 2, dtype=F32) / (HD // 2))
    ang = pos[:, :, None] * inv[None, None, :]
    cos = jnp.broadcast_to(jnp.cos(ang)[:, :, None, :], (N, 2, 2, HD // 4)).reshape(N, HD)
    sin = jnp.broadcast_to(jnp.sin(ang)[:, :, None, :], (N, 2, 2, HD // 4))
    sin = (sin * jnp.array([-1.0, 1.0], F32)[None, None, :, None]).reshape(N, HD)
    cos = jnp.concatenate([jnp.ones((L, HD), F32), cos], axis=0)
    sin = jnp.concatenate([jnp.zeros((L, HD), F32), sin], axis=0)
    return cos, sin


def _pad_lanes(v, off=0):
    return jnp.zeros((1, HD), F32).at[0, off:off + v.shape[0]].set(v)


def _local_step(x, ctx, tgt, mod_lat, mod_ctx, w_in, shards, small):
    N, L = x.shape[0], ctx.shape[0]
    T = N + L
    bounds = ((0, L), (L, T))
    qw, kw, gw = small["q_norm_w"], small["k_norm_w"], small["gdn_norm_w"]
    conv_w, ffn_w, ffn_b, fnw = small["conv_qkv_w"], small["ffn_conv_w"], small["ffn_conv_b"], small["final_norm_w"]
    alog = _pad_lanes(small["a_log"].reshape(-1), 2 * GH)
    dtb = _pad_lanes(small["dt_bias"].reshape(-1), 2 * GH)
    cos, sin = _rope_tables(N, L)
    bt = T
    bnl = 256 if N % 1024 else 1024

    h1 = _normmod_fwd(ctx, mod_ctx, 0, 1, name="normmod_ctx", out_rows=T)
    h1 = _normmod_fwd(x, mod_lat, 0, 1, name="normmod_x", off=L, into=h1)
    proj = _mm(h1, w_in, name="mm_in", M=T, N=C_END, K=D, tb=True, bm=bt, bn=1024)
    aq, ak, av = _aprep_fwd(proj, cos, sin, qw, kw)
    (attn, attn32, lse), (up_g,) = _attn_fwd(aq, ak, av, L, _GatherTwoLevel([shards["w_up"]]))
    gq = _gprep_fwd(proj, conv_w, 0, bounds)
    gk = _gprep_fwd(proj, conv_w, 1, bounds)
    gv = _gprep_fwd(proj, conv_w, 2, bounds)
    bl = _bl_fwd(proj, alog, dtb)
    intra, (down_g, pa_g, pd_g, out_g) = _intra_fwd(
        gq, gk, gv, bl, L, _GatherTwoLevel([shards[n] for n in ("w_down", "w_pa", "w_pd", "w_out")]))
    w_up, w_down = up_g.reshape(2 * DFF, D), down_g.reshape(DFF, D)
    w_pa, w_pd, w_out = pa_g.reshape(D, D), pd_g.reshape(D, D), out_g.reshape(D, D)
    xinv, intra = intra[6], intra[:6]
    o, states = _scan_fwd(*intra, L)
    gdn = _gout_fwd(o, proj, gw, L)
    pa = _mm(attn, w_pa, name="mm_pa", M=N, N=D, K=D, bm=bnl)
    pd = _mm(gdn, w_pd, name="mm_pd", M=N, N=D, K=D, bm=bnl)
    y = _merge_fwd(pa, pd, proj, L)
    m = _mm(y, w_out, name="mm_out", M=N, N=D, K=D, bm=bnl)
    x1 = _resid_fwd(x, m, mod_lat, 2, name="resid1")
    h2 = _normmod_fwd(x1, mod_lat, 3, 4, name="normmod_x1")
    up = _mm(h2, w_up, name="mm_up", M=N, N=2 * DFF, K=D, tb=True, bm=bnl, bn=2 * DFF // 4)
    a = _ffn_fwd(up, ffn_w, ffn_b)
    dn = _mm(a, w_down, name="mm_down", M=N, N=D, K=DFF, bm=bnl)
    loss, dx2, ddn, dg2, dfnw = _head(x1, dn, mod_lat, fnw, tgt)

    da = _mm(ddn, w_down, name="mm_down_dx", M=N, N=DFF, K=D, tb=True, bm=bnl, bn=DFF // 2)
    g_down = _mm(a, ddn, name="mm_down_dw", M=DFF, N=D, K=N, ta=True, bm=DFF // 2, out_dtype=BF16)
    dup, d_ffn_w, d_ffn_b = _ffn_bwd(up, ffn_w, ffn_b, da)
    dh2 = _mm(dup, w_up, name="mm_up_dx", M=N, N=D, K=2 * DFF, bm=bnl, bk=2 * DFF // 4)
    g_up = _mm(dup, h2, name="mm_up_dw", M=2 * DFF, N=D, K=N, ta=True, bm=2 * DFF // 4, out_dtype=BF16)
    dx1, dsh2, dsc2 = _normmod_bwd(x1, mod_lat, 3, 4, dh2, 0, dx2, name="normmod_x1_bwd")
    dm, dg1 = _resid_bwd(dx1, m, mod_lat, 2, name="resid1_bwd")
    dy = _mm(dm, w_out, name="mm_out_dx", M=N, N=D, K=D, tb=True, bm=bnl)
    g_out = _mm(y, dm, name="mm_out_dw", M=D, N=D, K=N, ta=True, out_dtype=BF16)
    dpa, dpd, dproj = _merge_bwd(pa, pd, proj, dy, L)
    dattn = _mm(dpa, w_pa, name="mm_pa_dx", M=N, N=D, K=D, tb=True, bm=bnl)
    g_pa = _mm(attn, dpa, name="mm_pa_dw", M=D, N=D, K=N, ta=True, out_dtype=BF16)
    dgdn = _mm(dpd, w_pd, name="mm_pd_dx", M=N, N=D, K=D, tb=True, bm=bnl)
    g_pd = _mm(gdn, dpd, name="mm_pd_dw", M=D, N=D, K=N, ta=True, out_dtype=BF16)
    do, dproj, dgw = _gout_bwd(o, proj, gw, dgdn, dproj, L)
    cts, recv_a = _scan_bwd(*intra, states, do, L, _Exchange(
        [g_out.reshape(NDEV, D // NDEV, D), g_pa.reshape(NDEV, D // NDEV, D), g_pd.reshape(NDEV, D // NDEV, D)], True))
    (dgq, dgk, dgv, dbl), recv_b = _intra_bwd(gq, gk, gv, bl, xinv, cts, L, _Exchange(
        [g_up.reshape(NDEV, 2 * DFF // NDEV, D)], True))
    dproj, dwq = _gprep_bwd(proj, conv_w, 0, bounds, dgq, dproj)
    dproj, dwk = _gprep_bwd(proj, conv_w, 1, bounds, dgk, dproj)
    dproj, dwv = _gprep_bwd(proj, conv_w, 2, bounds, dgv, dproj)
    dproj, dalog, ddtb = _bl_bwd(proj, alog, dtb, dbl, dproj)
    (daq_h, dak_h, dav_h), recv_c = _attn_bwd(aq, ak, av, attn32, lse, dattn, L, _Exchange(
        [g_down.reshape(NDEV, DFF // NDEV, D)], True))
    recv = dict(zip(("w_out", "w_pa", "w_pd", "w_up", "w_down"), recv_a + recv_b + recv_c))
    dproj, dqw, dkw = _aprep_bwd(proj, cos, sin, qw, kw, daq_h, dak_h, dav_h, dproj, L)
    g_in = _mm(dproj, h1, name="mm_in_dw", M=C_END, N=D, K=T, ta=True, bm=1024, out_dtype=BF16)
    *pending, token = _scatter_start(g_in, None, (0, D // 2), (), name="scatter_g_in_a_start")
    dh1 = _mm_deep(dproj, w_in, name="mm_in_dx", M=T, N=D, K=C_END, bk=1024, after=(token,))
    grad_x, dsh1, dsc1 = _normmod_bwd(x, mod_lat, 0, 1, dh1, L, dx1, name="normmod_x_bwd")
    _, dcsh1, dcsc1 = _normmod_bwd(ctx, mod_ctx, 0, 1, dh1, 0, None, name="normmod_ctx_bwd")

    z1 = jnp.zeros((1, D), F32)
    dmod_lat = jnp.concatenate([dsh1, dsc1, dg1, dsh2, dsc2, dg2], axis=0)
    dmod_ctx = jnp.concatenate([dcsh1, dcsc1, z1, z1, z1, z1], axis=0)
    gsmall = {
        "q_norm_w": dqw, "k_norm_w": dkw, "gdn_norm_w": dgw,
        "conv_qkv_w": jnp.concatenate([dwq, dwk, dwv], axis=1),
        "a_log": dalog[0, 2 * GH:4 * GH], "dt_bias": ddtb[0, 2 * GH:4 * GH],
        "ffn_conv_w": d_ffn_w, "ffn_conv_b": d_ffn_b, "final_norm_w": dfnw,
    }
    return loss[0, 0], grad_x, pending, recv, dmod_lat, dmod_ctx, gsmall


HBM = pl.BlockSpec(memory_space=pltpu.HBM)
ANYSPEC = pl.BlockSpec(memory_space=pl.ANY)


def _position():
    x, y, c = lax.axis_index("x"), lax.axis_index("y"), lax.axis_index("c")
    return x, y, c, 4 * x + 2 * y + c


def _peer(x, y, c, k):
    px = 1 - x if k & 4 else x
    py = 1 - y if k & 2 else y
    pc = 1 - c if k & 1 else c
    return (px, py, pc), 4 * px + 2 * py + pc


def _exchange(arrs, *, name, scatter):
    exch = _Exchange(arrs, scatter)
    n = exch.n

    def body(*refs):
        ins, outs, sems = refs[:n], refs[n:2 * n], refs[2 * n:]
        exch.start(ins, outs, sems)
        exch.finish(ins, outs, sems)

    outs = pl.pallas_call(body, name=name, out_shape=exch.out_shape, in_specs=[HBM] * n, out_specs=(HBM,) * n,
                          scratch_shapes=exch.scratch,
                          compiler_params=pltpu.CompilerParams(has_side_effects=True))(*arrs)
    return list(outs)


class _Exchange:
    def __init__(self, arrs, scatter):
        self.arrs, self.scatter, self.n = list(arrs), scatter, len(arrs)
        self.out_shape = tuple(_sds(a.shape if scatter else (NDEV,) + a.shape, a.dtype) for a in arrs)
        self.scratch = [pltpu.SemaphoreType.DMA((self.n, NDEV - 1)), pltpu.SemaphoreType.DMA((self.n, NDEV - 1)),
                        pltpu.SemaphoreType.DMA((self.n,))]

    def _copies(self, ins, outs, sems):
        send, recv, loc = sems
        x, y, c, me = _position()
        local = [pltpu.make_async_copy(ins[a].at[me] if self.scatter else ins[a], outs[a].at[me], loc.at[a])
                 for a in range(self.n)]
        remote = []
        for k in range(1, NDEV):
            peer, pid = _peer(x, y, c, k)
            for a in range(self.n):
                src = ins[a].at[pid] if self.scatter else ins[a]
                remote.append(pltpu.make_async_remote_copy(
                    src_ref=src, dst_ref=outs[a].at[me], send_sem=send.at[a, k - 1], recv_sem=recv.at[a, k - 1],
                    device_id=peer, device_id_type=MESH))
        return local, remote

    def start(self, ins, outs, sems):
        local, remote = self._copies(ins, outs, sems)
        for cp in local + remote:
            cp.start()

    def finish(self, ins, outs, sems):
        local, remote = self._copies(ins, outs, sems)
        for cp in remote:
            cp.wait()
        for cp in local:
            cp.wait()


class _GatherTwoLevel:
    scatter = False

    def __init__(self, arrs):
        self.arrs, self.n = list(arrs), len(arrs)
        self.out_shape = tuple(_sds((NDEV,) + a.shape, a.dtype) for a in arrs)
        self.scratch = [pltpu.SemaphoreType.DMA((self.n, NDEV - 1)), pltpu.SemaphoreType.DMA((self.n, NDEV - 1)),
                        pltpu.SemaphoreType.DMA((self.n,))]

    def _parts(self, ins, outs, sems):
        send, recv, loc = sems
        x, y, c, _ = _position()
        me, sibling = (x, y, c), (x, y, 1 - c)
        chips = [(1 - x, y), (x, 1 - y), (1 - x, 1 - y)]
        parts = []
        for a in range(self.n):
            slot = lambda px, py, pc, a=a: outs[a].at[4 * px + 2 * py + pc]

            def copy(k, owner, to, src=None, a=a, slot=slot):
                return pltpu.make_async_remote_copy(
                    src_ref=slot(*owner) if src is None else src, dst_ref=slot(*owner), send_sem=send.at[a, k],
                    recv_sem=recv.at[a, k], device_id=to, device_id_type=MESH)

            parts.append(dict(
                mine=pltpu.make_async_copy(ins[a], slot(*me), loc.at[a]),
                first=[copy(0, me, sibling, src=ins[a])] + [copy(1 + j, me, (*ch, c), src=ins[a]) for j, ch in enumerate(chips)],
                arrive=[copy(1 + j, (*ch, c), me) for j, ch in enumerate(chips)],
                passed=[copy(4 + j, (*ch, c), sibling) for j, ch in enumerate(chips)],
                rest=[copy(0, sibling, me)] + [copy(4 + j, (*ch, 1 - c), me) for j, ch in enumerate(chips)]))
        return parts

    def start(self, ins, outs, sems):
        for p in self._parts(ins, outs, sems):
            p["mine"].start()
            for cp in p["first"]:
                cp.start()

    def middle(self, ins, outs, sems):
        for p in self._parts(ins, outs, sems):
            for got, fwd in zip(p["arrive"], p["passed"]):
                got.wait_recv()
                fwd.start()

    def finish(self, ins, outs, sems):
        for p in self._parts(ins, outs, sems):
            for cp in p["rest"]:
                cp.wait_recv()
            for cp in p["first"] + p["passed"]:
                cp.wait_send()
            p["mine"].wait()


def _gather_two_level(blocks, *, name):
    exch = _GatherTwoLevel(blocks)
    n = exch.n

    def body(*refs):
        ins, outs, sems = refs[:n], refs[n:2 * n], refs[2 * n:]
        exch.start(ins, outs, sems)
        exch.middle(ins, outs, sems)
        exch.finish(ins, outs, sems)

    outs = pl.pallas_call(body, name=name, out_shape=exch.out_shape, in_specs=[HBM] * n, out_specs=(HBM,) * n,
                          scratch_shapes=exch.scratch,
                          compiler_params=pltpu.CompilerParams(has_side_effects=True))(*blocks)
    return list(outs)


SEM = pl.BlockSpec(memory_space=pltpu.SEMAPHORE)


SHARD_ROWS = W_END // NDEV
RUNS = ((0, W_QKV, C_KV), (W_QKV, W_AQ - W_QKV, C_QKV), (W_AQ, W_Z - W_AQ, C_AQ), (W_Z, W_END - W_Z, C_Z))


ROW_TILE = 8
SLOT_ROWS = -(-SHARD_ROWS // ROW_TILE) * ROW_TILE


def _shard_pieces(d):
    lo, hi = d * SHARD_ROWS, (d + 1) * SHARD_ROWS
    lead = lo % ROW_TILE
    pieces = []
    for first, rows, padded in RUNS:
        a, b = max(lo, first), min(hi, first + rows)
        if a < b:
            pieces.append([a - lo + lead, b - a, padded + a - first])
    pieces[0] = [0, pieces[0][1] + lead, pieces[0][2] - lead]
    pieces[-1][1] = SLOT_ROWS - pieces[-1][0]
    assert all(v % ROW_TILE == 0 for p in pieces for v in p) and all(p[2] + p[1] <= C_END for p in pieces)
    return pieces


def _scatter_send(src_ref, land_ref, send_sems, recv_sems, cols):
    _, _, _, me = _position()
    for d in range(NDEV):
        @pl.when(me != d)
        def _():
            k = jnp.bitwise_xor(me, d)
            peer = tuple(jnp.int32((d >> s) & 1) for s in (2, 1, 0))
            for off, rows, padded in _shard_pieces(d):
                pltpu.make_async_remote_copy(
                    src_ref=src_ref.at[pl.ds(padded, rows), pl.ds(*cols)],
                    dst_ref=land_ref.at[me].at[pl.ds(off, rows), pl.ds(*cols)], send_sem=send_sems.at[k - 1],
                    recv_sem=recv_sems.at[k - 1], device_id=peer, device_id_type=MESH).start()


def _scatter_whole(src_ref, land_ref, send_sems, recv_sems, cols):
    x, y, c, me = _position()
    span = (slice(None), pl.ds(*cols))
    copies = []
    for k in range(1, NDEV):
        peer, _ = _peer(x, y, c, k)
        copies.append(pltpu.make_async_remote_copy(
            src_ref=src_ref.at[pl.ds(0, SLOT_ROWS)].at[span], dst_ref=land_ref.at[me].at[span],
            send_sem=send_sems.at[k - 1], recv_sem=recv_sems.at[k - 1], device_id=peer, device_id_type=MESH))
    return copies


SPLIT_EFFECT = pltpu.SideEffectType.DATAFLOW_SIDE_EFFECTING


def _scatter_start(parts, land, cols, after, *, name):
    na = len(after)
    if land is None:
        land = lax.empty((NDEV, SLOT_ROWS, D), parts.dtype)

    def body(src_ref, land_ref, *rest):
        send_sems, recv_sems, _, _, token = rest[na:]
        _scatter_send(src_ref, land_ref, send_sems, recv_sems, cols)
        token[...] = jnp.zeros_like(token)

    return pl.pallas_call(
        body, name=name,
        out_shape=(pltpu.SemaphoreType.DMA((NDEV - 1,)), pltpu.SemaphoreType.DMA((NDEV - 1,)),
                   pltpu.HBM(parts.shape, parts.dtype), pltpu.HBM(land.shape, land.dtype), _sds((8, HD))),
        in_specs=(HBM, HBM) + (pl.BlockSpec(memory_space=pl.ANY),) * na,
        out_specs=(SEM, SEM, HBM, HBM, pl.BlockSpec(memory_space=pltpu.VMEM)),
        input_output_aliases={0: 2, 1: 3}, compiler_params=pltpu.CompilerParams(has_side_effects=SPLIT_EFFECT),
    )(pltpu.with_memory_space_constraint(parts, pltpu.HBM), pltpu.with_memory_space_constraint(land, pltpu.HBM), *after)


def _scatter_wait(send_sems, recv_sems, src_thru, land_thru, cols, after, *, name):
    na = len(after)

    def body(src_ref, land_ref, send_sems, recv_sems, *rest):
        for cp in _scatter_whole(src_ref, land_ref, send_sems, recv_sems, cols):
            cp.wait_send()
            cp.wait_recv()

    return pl.pallas_call(
        body, name=name,
        out_shape=(pltpu.HBM(src_thru.shape, src_thru.dtype), pltpu.HBM(land_thru.shape, land_thru.dtype)),
        in_specs=(HBM, HBM, SEM, SEM) + (pl.BlockSpec(memory_space=pl.ANY),) * na, out_specs=(HBM, HBM),
        input_output_aliases={0: 0, 1: 1}, compiler_params=pltpu.CompilerParams(has_side_effects=SPLIT_EFFECT),
    )(src_thru, land_thru, send_sems, recv_sems, *after)


def _cast_bf16(ws, *, name):
    k = len(ws)

    def body(*refs):
        for w_ref, o_ref in zip(refs[:k], refs[k:]):
            o_ref[...] = w_ref[...].astype(BF16)

    return _call(body, name=name, out_shape=tuple(_sds(w.shape, BF16) for w in ws), vmem=VMEM_BIG)(*ws)


def _sum_slots(a, *, name):
    _, R, C = a.shape

    def body(a_ref, o_ref):
        s = a_ref[0]
        for d in range(1, NDEV):
            s = s + a_ref[d]
        o_ref[...] = s

    return _call(body, name=name, out_shape=_sds((R, C)))(a)


MODROWS = 16


def _mod_fwd(c9, w, b):
    cols = w.shape[1]

    def body(c_ref, w_ref, b_ref, o_ref):
        o_ref[...] = _nn(_silu(c_ref[...]), w_ref[...]) + b_ref[...]

    return _call(body, name="mod_fwd", out_shape=_sds((MODROWS, cols)))(c9, w, b)


def _mod_bwd(c9, dmy, dall, w):
    cols = w.shape[1]

    def body(c_ref, dmy_ref, dall_ref, w_ref, gw_ref, gb_ref, cp_ref):
        sc = _silu(c_ref[...])
        rows = lax.broadcasted_iota(jnp.int32, (MODROWS, 1), 0)
        d = dmy_ref[...]
        d_ctx = jnp.where(rows == NDEV, d, 0.0)
        sc_ctx = jnp.where(rows == NDEV, sc, 0.0)
        outer = lax.dot_general(sc_ctx, d_ctx, (((0,), (0,)), ((), ())), precision=HI, preferred_element_type=F32)
        gw_ref[...] = _tn(jnp.where(rows < NDEV, sc, 0.0), jnp.where(rows < NDEV, d, 0.0)) + outer
        gb_ref[...] = jnp.sum(dall_ref[...], axis=0, keepdims=True)
        cp_ref[...] = jnp.sum(_nt(d_ctx, w_ref[...]), axis=0, keepdims=True)

    return _call(body, name="mod_bwd", out_shape=(_sds((D, cols)), _sds((1, 6 * D)), _sds((1, D))),
                 vmem=VMEM_BIG)(c9, dmy, dall, w)


def _cctx_finish(parts, c_ctx, after):
    VM = pl.BlockSpec(memory_space=pltpu.VMEM)

    def body(p_ref, c_ref, *rest):
        o_ref = rest[-1]
        s = p_ref[0]
        for d in range(1, NDEV):
            s = s + p_ref[d]
        _, vjp = jax.vjp(_silu, c_ref[...])
        o_ref[...] = vjp(s)[0]

    return _call(body, name="cctx_finish", out_shape=_sds((1, D)),
                 in_specs=[VM, VM] + [pl.BlockSpec(memory_space=pl.ANY)] * len(after))(parts, c_ctx, *after)


def _adamw_recv(w, recv, m, v, *, name, own=None):
    rows, cols = w.shape
    slot_rows = recv.shape[1]
    lead = slot_rows - rows
    assert rows % ROW_TILE in (0, lead)
    bc = 256
    c1 = 1.0 - B1 ** STEP
    c2 = 1.0 - B2 ** STEP
    has_own = own is not None

    def body(w_ref, r_ref, m_ref, v_ref, *rest):
        g_ref, d_ref, nm_ref, nv_ref = rest[-4:]
        me = _position()[3]

        def slot(d):
            return jnp.where(me == d, rest[0][...], r_ref[d]) if has_own else r_ref[d]

        gv = slot(0).astype(F32)
        for d in range(1, NDEV):
            gv = gv + slot(d).astype(F32)
        if lead:
            gv = jnp.where((me * rows) % ROW_TILE == 0, gv[:rows], gv[lead:])
        nm = B1 * m_ref[...] + (1.0 - B1) * gv
        nv = B2 * v_ref[...] + (1.0 - B2) * (gv * gv)
        g_ref[...] = gv
        d_ref[...] = -LR * ((nm / c1) / (jnp.sqrt(nv / c2) + AEPS) + WD * w_ref[...])
        nm_ref[...] = nm
        nv_ref[...] = nv

    blk = pl.BlockSpec((rows, bc), lambda j: (0, j))
    return _call(body, name=name, out_shape=(_sds((rows, cols)),) * 4, grid=(cols // bc,),
                 in_specs=[blk, pl.BlockSpec((NDEV, slot_rows, bc), lambda j: (0, 0, j)), blk, blk]
                 + [pl.BlockSpec((slot_rows, bc), lambda j: (0, j))] * has_own,
                 out_specs=(blk,) * 4, sem=("parallel",), vmem=VMEM_BIG)(w, recv, m, v, *([own] if has_own else []))


P_LAT, P_CTX, P_FNW, P_FFNB, P_CONV, P_FFNW, P_MISC, P_ROWS = 0, 8, 16, 24, 32, 48, 72, 80


def _rows_of(v, nrows):
    flat = v.reshape(-1)
    return jnp.pad(flat, (0, nrows * D - flat.shape[0])).reshape(nrows, D)


def _by_columns(g):
    n, r, c = g.shape
    return jnp.transpose(g, (1, 0, 2)).reshape(r, n * c)


def kernel(x, c, ctx, c_ctx, w_mod, b_mod, w_in, q_norm_w, k_norm_w, conv_qkv_w, a_log, dt_bias, gdn_norm_w, w_pa, w_pd, w_out, w_up, ffn_conv_w, ffn_conv_b, w_down, final_norm_w, loss_target, m_c_ctx, m_w_mod, m_b_mod, m_w_in, m_q_norm_w, m_k_norm_w, m_conv_qkv_w, m_a_log, m_dt_bias, m_gdn_norm_w, m_w_pa, m_w_pd, m_w_out, m_w_up, m_ffn_conv_w, m_ffn_conv_b, m_w_down, m_final_norm_w, v_c_ctx, v_w_mod, v_b_mod, v_w_in, v_q_norm_w, v_k_norm_w, v_conv_qkv_w, v_a_log, v_dt_bias, v_gdn_norm_w, v_w_pa, v_w_pd, v_w_out, v_w_up, v_ffn_conv_w, v_ffn_conv_b, v_w_down, v_final_norm_w):
    _, _, _, me = _position()
    mcols = w_mod.shape[2]

    transposed = ("w_in", "w_up")
    big = {"w_in": w_in[0].T, "w_pa": w_pa[0], "w_pd": w_pd[0], "w_out": w_out[0], "w_up": w_up[0].T, "w_down": w_down[0]}
    names = list(big)
    shards = dict(zip(names, _cast_bf16([big[n] for n in names], name="cast_weights")))
    w_in_g, c_all, conv_g, ffnw_g = _gather_two_level([shards["w_in"], c, conv_qkv_w[0], ffn_conv_w[0]],
                                                      name="gather_w_in")
    w_in_full = w_in_g.reshape(W_END, D)
    w_in_pad = _pad_columns(w_in_full)

    c9 = jnp.concatenate([c_all.reshape(NDEV, D), jnp.pad(c_ctx[None], ((0, MODROWS - NDEV - 1), (0, 0)))], axis=0)
    b_loc = lax.dynamic_slice(b_mod, (0, me * mcols), (1, mcols))
    mod_all, = _exchange([_mod_fwd(c9, w_mod[0], b_loc)], name="gather_mod", scatter=False)
    mod_lat = lax.dynamic_index_in_dim(mod_all, me, axis=1, keepdims=False).reshape(6, D)
    mod_ctx = mod_all[:, NDEV, :].reshape(6, D)

    small = {"q_norm_w": q_norm_w, "k_norm_w": k_norm_w, "gdn_norm_w": gdn_norm_w, "a_log": a_log, "dt_bias": dt_bias,
             "conv_qkv_w": _by_columns(conv_g), "ffn_conv_w": _by_columns(ffnw_g), "ffn_conv_b": ffn_conv_b,
             "final_norm_w": final_norm_w[None]}
    loss_me, grad_x, pending_in, recv, dmod_lat, dmod_ctx, gs = _local_step(
        x[0], ctx[0], loss_target[0], mod_lat, mod_ctx, w_in_pad, shards, small)

    moments = {"w_in": (m_w_in, v_w_in), "w_pa": (m_w_pa, v_w_pa), "w_pd": (m_w_pd, v_w_pd),
               "w_out": (m_w_out, v_w_out), "w_up": (m_w_up, v_w_up), "w_down": (m_w_down, v_w_down)}
    res = {}
    def finish(n, outs):
        return tuple((t.T if n in transposed else t)[None] for t in outs)

    def moment(t, n):
        return t[0].T if n in transposed else t[0]

    for n in recv:
        res[n] = finish(n, _adamw_recv(big[n], recv[n], moment(moments[n][0], n), moment(moments[n][1], n),
                                       name="adamw_" + n))

    misc = jnp.concatenate([gs["q_norm_w"][0], gs["k_norm_w"][0], gs["gdn_norm_w"][0], gs["a_log"], gs["dt_bias"],
                            loss_me[None]])
    pack = jnp.concatenate([_rows_of(dmod_lat, P_CTX - P_LAT), _rows_of(dmod_ctx, P_FNW - P_CTX),
                            _rows_of(gs["final_norm_w"], P_FFNB - P_FNW), _rows_of(gs["ffn_conv_b"], P_CONV - P_FFNB),
                            _rows_of(gs["conv_qkv_w"], P_FFNW - P_CONV), _rows_of(gs["ffn_conv_w"], P_MISC - P_FFNW),
                            _rows_of(misc, P_ROWS - P_MISC)], axis=0)
    pack_all, = _exchange([pack], name="gather_pack", scatter=False)
    tot = _sum_slots(pack_all, name="sum_pack")
    dall = jnp.concatenate([pack_all[:, P_LAT:P_LAT + 6, :].reshape(NDEV, 6 * D),
                            jnp.pad(tot[P_CTX:P_CTX + 6].reshape(1, 6 * D), ((0, MODROWS - NDEV - 1), (0, 0)))], axis=0)
    dmy = lax.dynamic_slice(dall, (0, me * mcols), (MODROWS, mcols))
    g_w_mod, g_b_mod, cpart = _mod_bwd(c9, dmy, dall, w_mod[0])
    cparts, = _exchange([cpart], name="gather_cctx", scatter=False)
    sems_a, land = pending_in[:2], pending_in[3]
    *sems_b, g_in_thru, land, token_b = _scatter_start(pending_in[2], land, (D // 2, D // 2), (cparts,),
                                                       name="scatter_g_in_b_start")
    g_c_ctx = _cctx_finish(cparts, c_ctx[None], (token_b,))[0]

    nconv, nffn = 3 * GH * HD, 2 * DFF
    conv_tot = tot[P_CONV:P_FFNW].reshape(-1)[:3 * nconv].reshape(3, nconv)
    ffnw_tot = tot[P_FFNW:P_MISC].reshape(-1)[:3 * nffn].reshape(3, nffn)
    mrow = tot[P_MISC]
    grads = {
        "c_ctx": g_c_ctx, "w_mod": g_w_mod[None], "b_mod": g_b_mod,
        "q_norm_w": mrow[None, 0:HD], "k_norm_w": mrow[None, HD:2 * HD], "gdn_norm_w": mrow[None, 2 * HD:3 * HD],
        "conv_qkv_w": lax.dynamic_slice(conv_tot, (0, me * (nconv // NDEV)), (3, nconv // NDEV))[None],
        "a_log": mrow[3 * HD:3 * HD + 2 * GH].reshape(1, 2, GH),
        "dt_bias": mrow[3 * HD + 2 * GH:3 * HD + 4 * GH].reshape(1, 2, GH),
        "ffn_conv_w": lax.dynamic_slice(ffnw_tot, (0, me * (nffn // NDEV)), (3, nffn // NDEV))[None],
        "ffn_conv_b": tot[P_FFNB:P_CONV].reshape(-1)[:nffn][None],
        "final_norm_w": tot[P_FNW],
    }
    loss = mrow[3 * HD + 4 * GH]
    given = {"c_ctx": (c_ctx, m_c_ctx, v_c_ctx), "w_mod": (w_mod, m_w_mod, v_w_mod), "b_mod": (b_mod, m_b_mod, v_b_mod),
             "q_norm_w": (q_norm_w, m_q_norm_w, v_q_norm_w), "k_norm_w": (k_norm_w, m_k_norm_w, v_k_norm_w),
             "conv_qkv_w": (conv_qkv_w, m_conv_qkv_w, v_conv_qkv_w), "a_log": (a_log, m_a_log, v_a_log),
             "dt_bias": (dt_bias, m_dt_bias, v_dt_bias), "gdn_norm_w": (gdn_norm_w, m_gdn_norm_w, v_gdn_norm_w),
             "ffn_conv_w": (ffn_conv_w, m_ffn_conv_w, v_ffn_conv_w), "ffn_conv_b": (ffn_conv_b, m_ffn_conv_b, v_ffn_conv_b),
             "final_norm_w": (final_norm_w, m_final_norm_w, v_final_norm_w)}
    res["w_mod"] = (grads["w_mod"],) + _adamw(w_mod, grads["w_mod"], m_w_mod, v_w_mod, name="adamw_w_mod")
    small_names = [n for n in given if n != "w_mod"]
    updates = _adamw_many([(given[n][0], grads[n], given[n][1], given[n][2]) for n in small_names], name="adamw_small")
    for n, upd in zip(small_names, updates):
        res[n] = (grads[n],) + upd

    first = me * SHARD_ROWS
    own_in = lax.dynamic_slice(_unpad_columns(g_in_thru), (first - first % ROW_TILE, 0), (SLOT_ROWS, D))
    mine = (big["w_in"], moment(m_w_in, "w_in"), moment(v_w_in, "w_in"))
    g_in_thru, land = _scatter_wait(*sems_a, g_in_thru, land, (0, D // 2), [res[n][1] for n in res] + [own_in, *mine],
                                    name="scatter_g_in_a_wait")
    _, land = _scatter_wait(*sems_b, g_in_thru, land, (D // 2, D // 2), (), name="scatter_g_in_b_wait")
    res["w_in"] = finish("w_in", _adamw_recv(mine[0], land, mine[1], mine[2], name="adamw_w_in", own=own_in))

    order = ["c_ctx", "w_mod", "b_mod", "w_in", "q_norm_w", "k_norm_w", "conv_qkv_w", "a_log", "dt_bias", "gdn_norm_w",
             "w_pa", "w_pd", "w_out", "w_up", "ffn_conv_w", "ffn_conv_b", "w_down", "final_norm_w"]
    return (loss, grad_x[None], *[res[n][0] for n in order], *[res[n][1] for n in order],
            *[res[n][2] for n in order], *[res[n][3] for n in order])
```

```python
import functools
import math

import jax
import jax.numpy as jnp
from jax import lax
from jax.experimental import pallas as pl
from jax.experimental.pallas import tpu as pltpu

F32 = jnp.float32
BF16 = jnp.bfloat16
HI = lax.Precision.HIGHEST
MESH = pl.DeviceIdType.MESH

NDEV = 8
D = 1024
HD = 128
AH, AKV, GRP = 8, 2, 4
GH = 8
CH = 64
DFF = 2816
GRID_W = 64
EPS = 1e-6
ROPE_THETA = 10000.0
LOG2E = math.log2(math.e)
C_KV, C_AQ, C_QKV, C_BL, C_Z, C_GATE, C_END = 0, 512, 1536, 4608, 5120, 6144, 8192
W_QKV, W_AQ, W_Z, W_END = 512, 3616, 4640, 7712


def _pad_columns(w):
    zeros = jnp.zeros((C_Z - C_QKV - (W_AQ - W_QKV), D), w.dtype)
    return jnp.concatenate([w[:W_QKV], w[W_AQ:W_Z], w[W_QKV:W_AQ], zeros, w[W_Z:]], axis=0)


def _unpad_columns(g):
    return jnp.concatenate([g[:C_AQ], g[C_QKV:C_QKV + W_AQ - W_QKV], g[C_AQ:C_QKV], g[C_Z:]], axis=0)
LR, B1, B2, AEPS, WD, STEP = 0.001, 0.9, 0.999, 1e-08, 0.01, 10
VMEM_BIG = 56 * 1024 * 1024
INTRA_FWD_CHUNKS = 36
INTRA_BWD_CHUNKS = 36


def _call(body, *, name, out_shape, grid=None, in_specs=None, out_specs=None, scratch=(), sem=None,
          vmem=None, aliases=None):
    params = {}
    if sem is not None:
        params["dimension_semantics"] = sem
    if vmem is not None:
        params["vmem_limit_bytes"] = vmem
    kw = {}
    if grid is not None:
        kw["grid"] = grid
    if in_specs is not None:
        kw["in_specs"] = in_specs
    if out_specs is not None:
        kw["out_specs"] = out_specs
    if aliases:
        kw["input_output_aliases"] = aliases
    return pl.pallas_call(body, name=name, out_shape=out_shape, scratch_shapes=list(scratch),
                          compiler_params=pltpu.CompilerParams(**params), **kw)


def _call_carrying(body, exch, *, name, out_shape, grid, in_specs, out_specs, scratch=(), vmem=None):
    n, nin, nout, nscr = exch.n, len(in_specs), len(out_shape), len(scratch)
    steps = math.prod(grid)
    mid = (2 * steps) // 3

    def wrapped(*refs):
        ins, cins = refs[:nin], refs[nin:nin + n]
        outs, couts = refs[nin + n:nin + n + nout], refs[nin + n + nout:nin + 2 * n + nout]
        scr, sems = refs[nin + 2 * n + nout:nin + 2 * n + nout + nscr], refs[nin + 2 * n + nout + nscr:]
        ids = [pl.program_id(i) for i in range(len(grid))]
        first = functools.reduce(jnp.logical_and, [i == 0 for i in ids])
        last = functools.reduce(jnp.logical_and, [i == g - 1 for i, g in zip(ids, grid)])

        @pl.when(first)
        def _():
            exch.start(cins, couts, sems)

        if hasattr(exch, "middle"):
            linear = functools.reduce(lambda acc, ig: acc * ig[1] + ig[0], zip(ids, grid), 0)

            @pl.when(linear == mid)
            def _():
                exch.middle(cins, couts, sems)

        body(*ins, *outs, *scr)

        @pl.when(last)
        def _():
            exch.finish(cins, couts, sems)

    params = {"dimension_semantics": ("arbitrary",) * len(grid)}
    if vmem is not None:
        params["vmem_limit_bytes"] = vmem
    fn = pl.pallas_call(wrapped, name=name, out_shape=tuple(out_shape) + exch.out_shape, grid=grid,
                        in_specs=list(in_specs) + [HBM] * n, out_specs=tuple(out_specs) + (HBM,) * n,
                        scratch_shapes=list(scratch) + exch.scratch, compiler_params=pltpu.CompilerParams(**params))

    def run(*args):
        res = fn(*args, *exch.arrs)
        return res[:nout], list(res[nout:])

    return run


def _sds(shape, dtype=F32):
    return jax.ShapeDtypeStruct(tuple(shape), dtype)


def _dot(a, b, ca, cb):
    return lax.dot_general(a.astype(BF16), b.astype(BF16), (((ca,), (cb,)), ((), ())),
                           preferred_element_type=F32)


@jax.custom_vjp
def _nn(a, b):
    return _dot(a, b, 1, 0)


@jax.custom_vjp
def _nt(a, b):
    return _dot(a, b, 1, 1)


@jax.custom_vjp
def _tn(a, b):
    return _dot(a, b, 0, 0)


_nn.defvjp(lambda a, b: (_nn(a, b), (a, b)), lambda r, g: (_nt(g, r[1]), _tn(r[0], g)))
_nt.defvjp(lambda a, b: (_nt(a, b), (a, b)), lambda r, g: (_nn(g, r[1]), _tn(g, r[0])))
_tn.defvjp(lambda a, b: (_tn(a, b), (a, b)), lambda r, g: (_nt(r[1], g), _nn(r[0], g)))


def _mdot(a, b):
    return jnp.dot(a, b, precision=lax.Precision.HIGH, preferred_element_type=F32)


def _maskdot(mask, a, cm):
    hi = a.astype(BF16)
    r = a - hi.astype(F32)
    mid = r.astype(BF16)
    lo = (r - mid.astype(F32)).astype(BF16)
    mb = mask.astype(BF16)
    dims = (((cm,), (0,)), ((), ()))
    return (lax.dot_general(mb, hi, dims, preferred_element_type=F32)
            + lax.dot_general(mb, mid, dims, preferred_element_type=F32)
            + lax.dot_general(mb, lo, dims, preferred_element_type=F32))


@jax.custom_vjp
def _mask_nn(mask, a):
    return _maskdot(mask, a, 1)


_mask_nn.defvjp(lambda mask, a: (_maskdot(mask, a, 1), mask),
                lambda mask, g: (jnp.zeros_like(mask), _maskdot(mask, g, 0)))


@jax.custom_vjp
def _saved_inverse(lmat, x):
    return x


def _saved_inverse_bwd(x, g):
    t = lax.dot_general(x, g, (((0,), (0,)), ((), ())), precision=lax.Precision.HIGH, preferred_element_type=F32)
    dl = lax.dot_general(t, x, (((1,), (1,)), ((), ())), precision=lax.Precision.HIGH, preferred_element_type=F32)
    return -dl, jnp.zeros_like(x)


_saved_inverse.defvjp(lambda lmat, x: (x, x), _saved_inverse_bwd)


def _row_ids(shape):
    return lax.broadcasted_iota(jnp.int32, shape, 0)


def _shift_rows(x, down, bounds):
    n = x.shape[0]
    rows = _row_ids(x.shape)
    y = pltpu.roll(x, 1 if down else n - 1, 0)
    edge = functools.reduce(jnp.logical_or, [rows == (s if down else e - 1) for s, e in bounds])
    return jnp.where(edge, 0.0, y)


def _make_shift(bounds):
    @jax.custom_vjp
    def down(x):
        return _shift_rows(x, True, bounds)

    @jax.custom_vjp
    def up(x):
        return _shift_rows(x, False, bounds)

    down.defvjp(lambda x: (down(x), None), lambda _, g: (up(g),))
    up.defvjp(lambda x: (up(x), None), lambda _, g: (down(g),))
    return down, up


@jax.custom_vjp
def _swap32(x):
    lane = lax.broadcasted_iota(jnp.int32, x.shape, x.ndim - 1)
    return jnp.where((lane % 64) < 32, pltpu.roll(x, HD - 32, x.ndim - 1), pltpu.roll(x, 32, x.ndim - 1))


_swap32.defvjp(lambda x: (_swap32(x), None), lambda _, g: (_swap32(g),))


def _rms(x):
    return x * lax.rsqrt(jnp.mean(x * x, axis=-1, keepdims=True) + EPS)


def _silu(x):
    return x * jax.nn.sigmoid(x)


def _mm(a, b, *, name, M, N, K, ta=False, tb=False, out_dtype=F32, bm=None, bn=None, bk=None, after=()):
    bm, bn, bk = bm or M, bn or N, bk or K
    assert M % bm == 0 and N % bn == 0 and K % bk == 0, (name, M, N, K, bm, bn, bk)
    nk = K // bk
    ca, cb = (0 if ta else 1), (1 if tb else 0)
    na = len(after)

    def body(a_ref, b_ref, *rest):
        o_ref, acc = rest[na], rest[na + 1:]
        r = _dot(a_ref[...], b_ref[...], ca, cb)
        if nk == 1:
            o_ref[...] = r.astype(out_dtype)
        else:
            acc_ref, = acc
            k = pl.program_id(2)

            @pl.when(k == 0)
            def _():
                acc_ref[...] = r

            @pl.when(k > 0)
            def _():
                acc_ref[...] += r

            @pl.when(k == nk - 1)
            def _():
                o_ref[...] = acc_ref[...].astype(out_dtype)

    a_spec = pl.BlockSpec((bk, bm), lambda i, j, k: (k, i)) if ta else pl.BlockSpec((bm, bk), lambda i, j, k: (i, k))
    b_spec = pl.BlockSpec((bn, bk), lambda i, j, k: (j, k)) if tb else pl.BlockSpec((bk, bn), lambda i, j, k: (k, j))
    return _call(body, name=name, out_shape=_sds((M, N), out_dtype), grid=(M // bm, N // bn, nk),
                 in_specs=[a_spec, b_spec] + [pl.BlockSpec(memory_space=pl.ANY)] * na,
                 out_specs=pl.BlockSpec((bm, bn), lambda i, j, k: (i, j)),
                 scratch=[pltpu.VMEM((bm, bn), F32)] if nk > 1 else [],
                 sem=("parallel", "parallel", "arbitrary"), vmem=VMEM_BIG)(a, b, *after)


def _mm_deep(a, b, *, name, M, N, K, bk, after=()):
    assert K % bk == 0
    na = len(after)

    def body(a_hbm, b_hbm, *rest):
        o_ref = rest[na]
        o_ref[...] = jnp.zeros_like(o_ref)

        def step(a_ref, b_ref):
            o_ref[...] += _dot(a_ref[...], b_ref[...], 1, 0)

        deep = pl.Buffered(3)
        pltpu.emit_pipeline(step, grid=(K // bk,),
                            in_specs=[pl.BlockSpec((M, bk), lambda k: (0, k), pipeline_mode=deep),
                                      pl.BlockSpec((bk, N), lambda k: (k, 0), pipeline_mode=deep)])(a_hbm, b_hbm)

    return _call(body, name=name, out_shape=_sds((M, N), F32), in_specs=[ANYSPEC] * (2 + na),
                 out_specs=pl.BlockSpec(memory_space=pltpu.VMEM), vmem=VMEM_BIG)(a, b, *after)


def _mm_deep_dw(a, b, *, name, M, N, K, bm, out_dtype):
    assert M % bm == 0

    def body(a_hbm, b_ref, o_hbm):
        def step(a_ref, o_ref):
            o_ref[...] = _dot(a_ref[...], b_ref[...], 0, 0).astype(out_dtype)

        pltpu.emit_pipeline(step, grid=(M // bm,),
                            in_specs=[pl.BlockSpec((K, bm), lambda i: (0, i), pipeline_mode=pl.Buffered(3))],
                            out_specs=[pl.BlockSpec((bm, N), lambda i: (i, 0))])(a_hbm, o_hbm)

    return _call(body, name=name, out_shape=_sds((M, N), out_dtype),
                 in_specs=[ANYSPEC, pl.BlockSpec(memory_space=pltpu.VMEM)], out_specs=ANYSPEC, vmem=VMEM_BIG)(a, b)


def _normmod_fn(x, sh, sc):
    return _rms(x) * (1.0 + sc) + sh


def _normmod_fwd(x, mod, i_sh, i_sc, *, name, br=256, out_rows=None, off=0, into=None):
    R = x.shape[0]
    ob = off // br
    given = [] if into is None else [into]
    if given:
        out_rows = into.shape[0]

    def body(x_ref, mod_ref, *rest):
        rest[-1][...] = _normmod_fn(x_ref[...], mod_ref[i_sh:i_sh + 1, :], mod_ref[i_sc:i_sc + 1, :]).astype(BF16)

    return _call(body, name=name, out_shape=_sds((out_rows or R, D), BF16), grid=(R // br,),
                 in_specs=[pl.BlockSpec((br, D), lambda i: (i, 0)), pl.BlockSpec((6, D), lambda i: (0, 0))]
                 + [ANYSPEC] * len(given), out_specs=pl.BlockSpec((br, D), lambda i: (i + ob, 0)),
                 aliases={2: 0} if given else None, sem=("parallel",))(x, mod, *given)


def _normmod_bwd(x, mod, i_sh, i_sc, dh, dh_off, res, *, name, br=256):
    R = x.shape[0]
    ob = dh_off // br
    has_res = res is not None

    def body(x_ref, mod_ref, dh_ref, *rest):
        if has_res:
            res_ref, dx_ref, dsh_ref, dsc_ref = rest
        else:
            dx_ref, dsh_ref, dsc_ref = rest
        sh, sc = mod_ref[i_sh:i_sh + 1, :], mod_ref[i_sc:i_sc + 1, :]
        _, vjp = jax.vjp(_normmod_fn, x_ref[...], sh, sc)
        dx, dsh, dsc = vjp(dh_ref[...])
        dx_ref[...] = dx + res_ref[...] if has_res else dx

        @pl.when(pl.program_id(0) == 0)
        def _():
            dsh_ref[...] = jnp.zeros_like(dsh_ref)
            dsc_ref[...] = jnp.zeros_like(dsc_ref)

        dsh_ref[...] += dsh
        dsc_ref[...] += dsc

    row = pl.BlockSpec((br, D), lambda i: (i, 0))
    vec = pl.BlockSpec((1, D), lambda i: (0, 0))
    ins = [row, pl.BlockSpec((6, D), lambda i: (0, 0)), pl.BlockSpec((br, D), lambda i: (i + ob, 0))]
    args = [x, mod, dh]
    if has_res:
        ins.append(row)
        args.append(res)
    return _call(body, name=name, out_shape=(_sds((R, D)), _sds((1, D)), _sds((1, D))), grid=(R // br,),
                 in_specs=ins, out_specs=(row, vec, vec), sem=("arbitrary",))(*args)


def _rope(x, cos, sin):
    return x * cos + _swap32(x) * sin


def _aprep_fn(qs, ks, cos, sin, qw, kw):
    return ([_rope(_rms(q) * qw, cos, sin) for q in qs], [_rope(_rms(k) * kw, cos, sin) for k in ks])


def _aprep_fwd(proj, cos, sin, qw, kw, *, br=256):
    T = proj.shape[0]

    def body(x_ref, cos_ref, sin_ref, qw_ref, kw_ref, q_ref, k_ref, v_ref):
        qs = [x_ref[:, C_AQ + h * HD:C_AQ + (h + 1) * HD] for h in range(AH)]
        ks = [x_ref[:, h * HD:(h + 1) * HD] for h in range(AKV)]
        qo, ko = _aprep_fn(qs, ks, cos_ref[...], sin_ref[...], qw_ref[...], kw_ref[...])
        for h in range(AH):
            q_ref[h] = qo[h].astype(BF16)
        for h in range(AKV):
            k_ref[h] = ko[h].astype(BF16)
            v_ref[h] = x_ref[:, (AKV + h) * HD:(AKV + h + 1) * HD].astype(BF16)

    tab = pl.BlockSpec((br, HD), lambda i: (i, 0))
    vec = pl.BlockSpec((1, HD), lambda i: (0, 0))
    return _call(body, name="aprep_fwd",
                 out_shape=(_sds((AH, T, HD), BF16), _sds((AKV, T, HD), BF16), _sds((AKV, T, HD), BF16)),
                 grid=(T // br,),
                 in_specs=[pl.BlockSpec((br, C_QKV), lambda i: (i, 0)), tab, tab, vec, vec],
                 out_specs=(pl.BlockSpec((AH, br, HD), lambda i: (0, i, 0)),
                            pl.BlockSpec((AKV, br, HD), lambda i: (0, i, 0)),
                            pl.BlockSpec((AKV, br, HD), lambda i: (0, i, 0))),
                 sem=("parallel",))(proj, cos, sin, qw, kw)


def _aprep_bwd(proj, cos, sin, qw, kw, dq, dk, dv, dproj, L, *, br=256):
    T = proj.shape[0]
    lb = L // br

    def body(x_ref, cos_ref, sin_ref, qw_ref, kw_ref, dq_ref, dk_ref, dv_ref, _, dx_ref, dqw_ref, dkw_ref):
        i = pl.program_id(0)
        qs = [x_ref[:, C_AQ + h * HD:C_AQ + (h + 1) * HD] for h in range(AH)]
        ks = [x_ref[:, h * HD:(h + 1) * HD] for h in range(AKV)]
        _, vjp = jax.vjp(_aprep_fn, qs, ks, cos_ref[...], sin_ref[...], qw_ref[...], kw_ref[...])
        is_lat = i >= lb
        dqs = [jnp.where(is_lat, dq_ref[h], 0.0) for h in range(AH)]
        dks = [dk_ref[h] for h in range(AKV)]
        gq, gk, _, _, gqw, gkw = vjp((dqs, dks))
        for h in range(AH):
            dx_ref[:, C_AQ + h * HD:C_AQ + (h + 1) * HD] = gq[h].astype(BF16)
        for h in range(AKV):
            dx_ref[:, h * HD:(h + 1) * HD] = gk[h].astype(BF16)
            dx_ref[:, (AKV + h) * HD:(AKV + h + 1) * HD] = dv_ref[h].astype(BF16)

        @pl.when(i == 0)
        def _():
            dqw_ref[...] = jnp.zeros_like(dqw_ref)
            dkw_ref[...] = jnp.zeros_like(dkw_ref)

        dqw_ref[...] += gqw
        dkw_ref[...] += gkw

    tab = pl.BlockSpec((br, HD), lambda i: (i, 0))
    vec = pl.BlockSpec((1, HD), lambda i: (0, 0))
    kvb = pl.BlockSpec((AKV, br, HD), lambda i: (0, i, 0))
    blk = pl.BlockSpec((br, C_QKV), lambda i: (i, 0))
    return _call(body, name="aprep_bwd", out_shape=(_sds(dproj.shape, BF16), _sds((1, HD)), _sds((1, HD))),
                 grid=(T // br,),
                 in_specs=[blk, tab, tab, vec, vec,
                           pl.BlockSpec((AH, br, HD), lambda i: (0, jnp.maximum(i - lb, 0), 0)), kvb, kvb, ANYSPEC],
                 out_specs=(blk, vec, vec), aliases={8: 0},
                 sem=("arbitrary",))(proj, cos, sin, qw, kw, dq, dk, dv, dproj)


def _attn_grad(q, k, v, o, lse2, do):
    scale = HD ** -0.5
    p = jnp.exp2(_dot(q, k, 1, 1) * (scale * LOG2E) - lse2)
    dp = _dot(do, v, 1, 1)
    ds = p * (dp - jnp.sum(do * o, axis=-1, keepdims=True)) * scale
    return _dot(ds, k, 1, 0), _dot(ds, q, 0, 0), _dot(p, do, 0, 0)


ATTN_KEYS = 256


def _attn_fwd(q, k, v, L, exch, *, bq=128):
    T = q.shape[1]
    N = T - L
    lb = L // bq
    assert T % ATTN_KEYS == 0
    scale = HD ** -0.5
    heads = range(GRP)

    def body(q_ref, k_ref, v_ref, o_ref, o32_ref, lse_ref):
        qs = [q_ref[g] for g in heads]
        m = [jnp.full((bq, 1), -jnp.inf, F32) for _ in heads]
        l = [jnp.zeros((bq, 1), F32) for _ in heads]
        acc = [jnp.zeros((bq, HD), F32) for _ in heads]
        for c in range(T // ATTN_KEYS):
            kc, vc = k_ref[c * ATTN_KEYS:(c + 1) * ATTN_KEYS, :], v_ref[c * ATTN_KEYS:(c + 1) * ATTN_KEYS, :]
            s = [_dot(qs[g], kc, 1, 1) * (scale * LOG2E) for g in heads]
            m_new = [jnp.maximum(m[g], jnp.max(s[g], axis=-1, keepdims=True)) for g in heads]
            alpha = [jnp.exp2(m[g] - m_new[g]) for g in heads]
            p = [jnp.exp2(s[g] - m_new[g]) for g in heads]
            l = [l[g] * alpha[g] + jnp.sum(p[g], axis=-1, keepdims=True) for g in heads]
            acc = [acc[g] * alpha[g] + _dot(p[g], vc, 1, 0) for g in heads]
            m = m_new
        for g in heads:
            o = acc[g] / l[g]
            o_ref[:, g * HD:(g + 1) * HD] = o.astype(BF16)
            o32_ref[:, g * HD:(g + 1) * HD] = o
            lse_ref[g] = jnp.broadcast_to(m[g] + jnp.log2(l[g]), (bq, HD))

    kvb = pl.BlockSpec((None, T, HD), lambda g, i: (g, 0, 0))
    ob = pl.BlockSpec((bq, GRP * HD), lambda g, i: (i, g))
    return _call_carrying(
        body, exch, name="attn_fwd",
        out_shape=(_sds((N, AH * HD), BF16), _sds((N, AH * HD)), _sds((AH, N, HD))), grid=(AKV, N // bq),
        in_specs=[pl.BlockSpec((GRP, bq, HD), lambda g, i: (g, i + lb, 0)), kvb, kvb],
        out_specs=(ob, ob, pl.BlockSpec((GRP, bq, HD), lambda g, i: (g, i, 0))), vmem=VMEM_BIG)(q, k, v)


def _attn_bwd(q, k, v, o32, lse, do, L, exch, *, bq=128):
    T = q.shape[1]
    N = T - L
    lb = L // bq

    def body(q_ref, k_ref, v_ref, o_ref, lse_ref, do_ref, dq_ref, dk_ref, dv_ref):
        rows = lambda r: jnp.concatenate([r[:, g * HD:(g + 1) * HD] for g in range(GRP)], axis=0)
        lse = jnp.max(lse_ref[...].reshape(GRP * bq, HD), axis=-1, keepdims=True)
        dq, dk, dv = _attn_grad(q_ref[...].reshape(GRP * bq, HD), k_ref[...], v_ref[...], rows(o_ref), lse, rows(do_ref))
        dq_ref[...] = dq.reshape(GRP, bq, HD)

        @pl.when(pl.program_id(1) == 0)
        def _():
            dk_ref[...] = jnp.zeros_like(dk_ref)
            dv_ref[...] = jnp.zeros_like(dv_ref)

        dk_ref[...] += dk
        dv_ref[...] += dv

    kvb = pl.BlockSpec((None, T, HD), lambda g, i: (g, 0, 0))
    qb = pl.BlockSpec((GRP, bq, HD), lambda g, i: (g, i + lb, 0))
    hb = pl.BlockSpec((GRP, bq, HD), lambda g, i: (g, i, 0))
    ob = pl.BlockSpec((bq, GRP * HD), lambda g, i: (i, g))
    return _call_carrying(body, exch, name="attn_bwd",
                          out_shape=(_sds((AH, N, HD)), _sds((AKV, T, HD)), _sds((AKV, T, HD))), grid=(AKV, N // bq),
                          in_specs=[qb, kvb, kvb, ob, hb, ob], out_specs=(hb, kvb, kvb),
                          vmem=VMEM_BIG)(q, k, v, o32, lse, do)


def _gprep_fn(kind, shifts, x, w):
    down, up = shifts
    y = down(x) * w[0:1, :] + x * w[1:2, :] + up(x) * w[2:3, :]
    a = _silu(y)
    if kind == 2:
        return a
    a = a * lax.rsqrt(jnp.sum(a * a, axis=-1, keepdims=True) + EPS)
    return a * (HD ** -0.5) if kind == 0 else a


def _gprep_fwd(proj, conv_w, kind, bounds):
    T = proj.shape[0]
    shifts = _make_shift(bounds)
    cb = C_QKV // HD + kind * GH

    def body(x_ref, w_ref, o_ref):
        o_ref[...] = _gprep_fn(kind, shifts, x_ref[...], w_ref[...])

    return _call(body, name=f"gprep_fwd{kind}", out_shape=_sds((GH, T, HD)), grid=(GH,),
                 in_specs=[pl.BlockSpec((T, HD), lambda h: (0, cb + h)),
                           pl.BlockSpec((3, HD), lambda h: (0, kind * GH + h))],
                 out_specs=pl.BlockSpec((None, T, HD), lambda h: (h, 0, 0)), sem=("parallel",))(proj, conv_w)


def _gprep_bwd(proj, conv_w, kind, bounds, dy, dproj):
    T = proj.shape[0]
    shifts = _make_shift(bounds)
    cb = C_QKV // HD + kind * GH

    def body(x_ref, w_ref, dy_ref, _, dx_ref, dw_ref):
        _, vjp = jax.vjp(functools.partial(_gprep_fn, kind, shifts), x_ref[...], w_ref[...])
        dx, dw = vjp(dy_ref[0] + dy_ref[1])
        dx_ref[...] = dx.astype(BF16)
        dw_ref[...] = dw

    return _call(body, name=f"gprep_bwd{kind}", out_shape=(_sds(dproj.shape, BF16), _sds((3, GH * HD))), grid=(GH,),
                 in_specs=[pl.BlockSpec((T, HD), lambda h: (0, cb + h)),
                           pl.BlockSpec((3, HD), lambda h: (0, kind * GH + h)),
                           pl.BlockSpec((2, None, T, HD), lambda h: (0, h, 0, 0)), ANYSPEC],
                 out_specs=(pl.BlockSpec((T, HD), lambda h: (0, cb + h)), pl.BlockSpec((3, HD), lambda h: (0, h))),
                 aliases={3: 0}, sem=("parallel",))(proj, conv_w, dy, dproj)


def _bl_fn(x, alog, dtb):
    lane = lax.broadcasted_iota(jnp.int32, x.shape, 1)
    beta = jax.nn.sigmoid(x)
    z = x + dtb
    sp = jnp.maximum(z, 0.0) + jnp.log1p(jnp.exp(-jnp.abs(z)))
    la = -jnp.exp(alog) * sp
    return jnp.where(lane < 2 * GH, beta, jnp.where(lane < 4 * GH, la, 0.0))


def _bl_fwd(proj, alog, dtb, *, br=256):
    T = proj.shape[0]

    def body(x_ref, a_ref, d_ref, o_ref):
        o_ref[...] = _bl_fn(x_ref[...], a_ref[...], d_ref[...])

    vec = pl.BlockSpec((1, HD), lambda i: (0, 0))
    return _call(body, name="bl_fwd", out_shape=_sds((T, HD)), grid=(T // br,),
                 in_specs=[pl.BlockSpec((br, HD), lambda i: (i, C_BL // HD)), vec, vec],
                 out_specs=pl.BlockSpec((br, HD), lambda i: (i, 0)), sem=("parallel",))(proj, alog, dtb)


def _bl_bwd(proj, alog, dtb, dbl, dproj, *, br=256):
    T = proj.shape[0]
    wide = C_Z - C_BL

    def body(x_ref, a_ref, d_ref, g_ref, _, dx_ref, da_ref, dd_ref):
        g = g_ref[0, 0]
        for d in range(2):
            for h in range(GH):
                if d or h:
                    g = g + g_ref[d, h]
        _, vjp = jax.vjp(_bl_fn, x_ref[...], a_ref[...], d_ref[...])
        dx, da, dd = vjp(g)
        dx_ref[:, :HD] = dx.astype(BF16)
        dx_ref[:, HD:] = jnp.zeros((br, wide - HD), BF16)

        @pl.when(pl.program_id(0) == 0)
        def _():
            da_ref[...] = jnp.zeros_like(da_ref)
            dd_ref[...] = jnp.zeros_like(dd_ref)

        da_ref[...] += da
        dd_ref[...] += dd

    vec = pl.BlockSpec((1, HD), lambda i: (0, 0))
    return _call(body, name="bl_bwd", out_shape=(_sds(dproj.shape, BF16), _sds((1, HD)), _sds((1, HD))), grid=(T // br,),
                 in_specs=[pl.BlockSpec((br, HD), lambda i: (i, C_BL // HD)), vec, vec,
                           pl.BlockSpec((2, GH, br, HD), lambda i: (0, 0, i, 0)), ANYSPEC],
                 out_specs=(pl.BlockSpec((br, wide), lambda i: (i, C_BL // wide)), vec, vec), aliases={4: 0},
                 sem=("arbitrary",))(proj, alog, dtb, dbl, dproj)


def _chunk_masks(d):
    ii = lax.broadcasted_iota(jnp.int32, (CH, CH), 0)
    jj = lax.broadcasted_iota(jnp.int32, (CH, CH), 1)
    eye = (ii == jj).astype(F32)
    before = jnp.where(d == 0, (jj < ii).astype(F32), (jj > ii).astype(F32))
    return before, before + eye, eye


def _same_block(b):
    ii = lax.broadcasted_iota(jnp.int32, (CH, CH), 0)
    jj = lax.broadcasted_iota(jnp.int32, (CH, CH), 1)
    shift = b.bit_length() - 1
    return (jnp.right_shift(ii, shift) == jnp.right_shift(jj, shift)).astype(F32)


def _intra_fn(masks, sel_b, sel_l, qs, ks, vs, bls, xs=None):
    before, ateq, eye = masks
    inc = ateq > 0.0
    each = lambda f, *ls: [f(*t) for t in zip(*ls)]
    beta = each(lambda bl: jnp.sum(bl * sel_b, axis=-1, keepdims=True), bls)
    la = each(lambda bl: jnp.sum(bl * sel_l, axis=-1, keepdims=True), bls)
    gam = each(lambda a: _mask_nn(ateq, jnp.broadcast_to(a, (CH, HD))), la)
    gi = each(lambda g: g[:, :CH], gam)
    gj = each(lambda g: jnp.transpose(g)[:CH, :], gam)
    kq = each(lambda k, q: _nt(jnp.concatenate([k, q], axis=0), k), ks, qs)
    kk = each(lambda t: t[:CH], kq)
    qk = each(lambda t: t[CH:], kq)
    dec = each(lambda a, b: jnp.where(inc, jnp.exp(jnp.where(inc, a - b, 0.0)), 0.0), gi, gj)
    lmat = each(lambda b, d, m: before * (b * d * m), beta, dec, kk)
    if xs is None:
        same = lambda b: _same_block(b)
        l8 = each(lambda m: m * same(8), lmat)
        x = each(lambda m: eye - m, l8)
        p2 = each(lambda m: _mdot(m, m), l8)
        y = each(lambda a, b: _mdot(jnp.concatenate([a, b], axis=0), b), x, p2)
        x = each(lambda a, t: a + t[:CH], x, y)
        x = each(lambda a, t: a + _mdot(a, t[CH:]), x, y)
        for b in (8, 16, 32):
            below = same(2 * b) - same(b)
            x = each(lambda a, m: a - _mdot(a, _mdot(m * below, a)), x, lmat)
    else:
        x = each(_saved_inverse, lmat, xs)
    eg = each(jnp.exp, gam)
    uw = each(lambda a, b, v, e, k: _mdot(a, jnp.concatenate([b * v, (b * e) * k], axis=1)), x, beta, vs, eg, ks)
    u = each(lambda t: t[:, :HD], uw)
    w = each(lambda t: t[:, HD:], uw)
    tot = each(lambda a: jnp.sum(a, axis=0, keepdims=True), la)
    kd = each(lambda k, t, g: k * jnp.exp(t - g), ks, tot, gam)
    gl = each(lambda t: jnp.broadcast_to(jnp.exp(t), (1, HD)), tot)
    qd = each(lambda q, e: q * e, qs, eg)
    p = each(lambda d, m: d * m, dec, qk)
    return (u, w, kd, qd, p, gl, x) if xs is None else (u, w, kd, qd, p, gl)


def _dir_head_sel(d, h):
    lane = lax.broadcasted_iota(jnp.int32, (1, HD), 1)
    return (lane == d * GH + h).astype(F32), (lane == 2 * GH + d * GH + h).astype(F32)


def _intra_specs(T, G):
    nc = T // CH
    assert nc % G == 0
    qkv = pl.BlockSpec((None, G * CH, HD), lambda d, h, c: (h, c, 0))
    bl = pl.BlockSpec((G * CH, HD), lambda d, h, c: (c, 0))
    big = pl.BlockSpec((None, None, G * CH, HD), lambda d, h, c: (d, h, c, 0))
    pm = pl.BlockSpec((None, None, G * CH, CH), lambda d, h, c: (d, h, c, 0))
    gl = pl.BlockSpec((None, None, G, 1, HD), lambda d, h, c: (d, h, c, 0, 0))
    shapes = (_sds((2, GH, T, HD)),) + (_sds((2, GH, T, HD), BF16),) * 3 + (
        _sds((2, GH, T, CH), BF16), _sds((2, GH, nc, 1, HD)), _sds((2, GH, T, CH)))
    return nc, qkv, bl, big, pm, gl, shapes


def _chunks_per_step(T, most):
    nc = T // CH
    return max(g for g in range(1, most + 1) if nc % g == 0)


def _chunk_at(g, d, nc, ncc):
    pos = _visit_pos(g, d, nc, ncc)
    return pos, pl.ds(pl.multiple_of(pos * CH, CH), CH)


def _intra_fwd(q, k, v, bl, L, exch):
    T = q.shape[1]
    G = _chunks_per_step(T, INTRA_FWD_CHUNKS)
    nc, qkv_s, bl_s, big, pm, gl_s, shapes = _intra_specs(T, G)
    assert G == nc
    ncc = L // CH

    def body(q_ref, k_ref, v_ref, bl_ref, u_ref, w_ref, kd_ref, qd_ref, p_ref, gl_ref, x_ref):
        d, h = pl.program_id(0), pl.program_id(1)
        sb, sl = _dir_head_sel(d, h)
        rows = [slice(g * CH, (g + 1) * CH) for g in range(G)]
        outs = _intra_fn(_chunk_masks(d), sb, sl, *[[r[s, :] for s in rows] for r in (q_ref, k_ref, v_ref, bl_ref)])
        for g in range(G):
            pos, at = _chunk_at(g, d, nc, ncc)
            for r, o in zip((u_ref, w_ref, kd_ref, qd_ref, p_ref, x_ref), outs[:5] + outs[6:]):
                r[at, :] = o[g].astype(r.dtype)
            gl_ref[pos] = outs[5][g]

    return _call_carrying(body, exch, name="gdn_intra_fwd", out_shape=shapes, grid=(2, GH, nc // G),
                          in_specs=[qkv_s, qkv_s, qkv_s, bl_s], out_specs=(big, big, big, big, pm, gl_s, pm))(q, k, v, bl)


def _intra_bwd(q, k, v, bl, xinv, cts, L, exch):
    T = q.shape[1]
    G = _chunks_per_step(T, INTRA_BWD_CHUNKS)
    nc, qkv_s, bl_s, big, pm, gl_s, _ = _intra_specs(T, G)
    assert G == nc
    ncc = L // CH

    def body(q_ref, k_ref, v_ref, bl_ref, x_ref, du, dw, dkd, dqd, dp, dgl, dq_ref, dk_ref, dv_ref, dbl_ref):
        d, h = pl.program_id(0), pl.program_id(1)
        sb, sl = _dir_head_sel(d, h)
        rows = [slice(g * CH, (g + 1) * CH) for g in range(G)]
        places = [_chunk_at(g, d, nc, ncc) for g in range(G)]
        fn = functools.partial(_intra_fn, _chunk_masks(d), sb, sl, xs=[x_ref[at, :] for _, at in places])
        _, vjp = jax.vjp(fn, *[[r[s, :] for s in rows] for r in (q_ref, k_ref, v_ref, bl_ref)])
        cts = tuple([r[at, :] for _, at in places] for r in (du, dw, dkd, dqd, dp)) + ([dgl[pos] for pos, _ in places],)
        grads = vjp(cts)
        for g in range(G):
            for r, o in zip((dq_ref, dk_ref, dv_ref, dbl_ref), grads):
                r[rows[g], :] = o[g]

    return _call_carrying(body, exch, name="gdn_intra_bwd", out_shape=(_sds((2, GH, T, HD)),) * 4,
                          grid=(2, GH, nc // G), in_specs=[qkv_s, qkv_s, qkv_s, bl_s, pm, big, big, big, big, pm, gl_s],
                          out_specs=(big,) * 4)(q, k, v, bl, xinv, *cts)


def _scan_fn(s, u, w, kd, qd, p, gl):
    each = lambda f, *ls: [f(*t) for t in zip(*ls)]
    ws = each(_nn, w, s)
    delta = each(lambda a, b: a - b, u, ws)
    kdd = each(_tn, kd, delta)
    s_new = each(lambda g, a, b: g * a + b, gl, s, kdd)
    qs = each(_nn, qd, s)
    pd = each(_nn, p, delta)
    return each(lambda a, b: a + b, qs, pd), s_new


SCAN_BLOCK = 4


def _visit_pos(c, d, nc, ncc):
    back = ncc - 1 - c if c < ncc else ncc + (nc - 1 - c)
    return jnp.where(d == 0, c, back)


def _scan_specs(T, L, back):
    tb = SCAN_BLOCK * CH
    assert T % tb == 0 and L % tb == 0
    nb, ncb = T // tb, L // tb
    at = (lambda t: nb - 1 - t) if back else (lambda t: t)
    big = pl.BlockSpec((2, GH, tb, HD), lambda t: (0, 0, at(t), 0))
    pm = pl.BlockSpec((2, GH, tb, CH), lambda t: (0, 0, at(t), 0))
    gl = pl.BlockSpec((2, GH, SCAN_BLOCK, 1, HD), lambda t: (0, 0, at(t), 0, 0))
    st = pl.BlockSpec((2, GH, SCAN_BLOCK, HD, HD), lambda t: (0, 0, at(t), 0, 0))

    def natural(b):
        return jnp.where(b < ncb, ncb - 1 - b, nb - 1 - (b - ncb))

    do_specs = (pl.BlockSpec((GH, tb, HD), lambda t: (0, at(t), 0)),
                pl.BlockSpec((GH, tb, HD), lambda t: (0, natural(at(t)), 0)))
    return nb, big, pm, gl, st, do_specs


SCAN_STREAMS = [(d, h) for d in (0, 1) for h in range(GH)]


def _scan_fwd(u, w, kd, qd, p, gl, L):
    T = u.shape[2]
    nb, big, pm, gl_s, st, _ = _scan_specs(T, L, False)

    def body(u_ref, w_ref, kd_ref, qd_ref, p_ref, gl_ref, o_ref, st_ref, s_scr):
        @pl.when(pl.program_id(0) == 0)
        def _():
            s_scr[...] = jnp.zeros_like(s_scr)

        s = [s_scr[d, h] for d, h in SCAN_STREAMS]
        for i in range(SCAN_BLOCK):
            rows = slice(i * CH, (i + 1) * CH)
            for (d, h), sv in zip(SCAN_STREAMS, s):
                st_ref[d, h, i] = sv
            o, s = _scan_fn(s, *[[r[d, h, rows, :].astype(F32) for d, h in SCAN_STREAMS]
                                 for r in (u_ref, w_ref, kd_ref, qd_ref, p_ref)],
                            [gl_ref[d, h, i] for d, h in SCAN_STREAMS])
            for (d, h), ov in zip(SCAN_STREAMS, o):
                o_ref[d, h, rows, :] = ov
        for (d, h), sv in zip(SCAN_STREAMS, s):
            s_scr[d, h] = sv

    return _call(body, name="gdn_scan_fwd", out_shape=(_sds((2, GH, T, HD)), _sds((2, GH, T // CH, HD, HD))),
                 grid=(nb,), in_specs=[big, big, big, big, pm, gl_s], out_specs=(big, st),
                 scratch=[pltpu.VMEM((2, GH, HD, HD), F32)], sem=("arbitrary",), vmem=VMEM_BIG)(u, w, kd, qd, p, gl)


def _scan_bwd(u, w, kd, qd, p, gl, states, do, L, exch):
    T = u.shape[2]
    nb, big, pm, gl_s, st, do_specs = _scan_specs(T, L, True)

    def body(u_ref, w_ref, kd_ref, qd_ref, p_ref, gl_ref, st_ref, do0_ref, do1_ref,
             du_ref, dw_ref, dkd_ref, dqd_ref, dp_ref, dgl_ref, ds_scr):
        @pl.when(pl.program_id(0) == 0)
        def _():
            ds_scr[...] = jnp.zeros_like(ds_scr)

        ds = [ds_scr[d, h] for d, h in SCAN_STREAMS]
        for i in reversed(range(SCAN_BLOCK)):
            rows = slice(i * CH, (i + 1) * CH)
            mirror = slice((SCAN_BLOCK - 1 - i) * CH, (SCAN_BLOCK - i) * CH)
            _, vjp = jax.vjp(_scan_fn, [st_ref[d, h, i] for d, h in SCAN_STREAMS],
                             *[[r[d, h, rows, :].astype(F32) for d, h in SCAN_STREAMS]
                               for r in (u_ref, w_ref, kd_ref, qd_ref, p_ref)],
                             [gl_ref[d, h, i] for d, h in SCAN_STREAMS])
            dos = [do0_ref[h, rows, :] if d == 0 else do1_ref[h, mirror, :] for d, h in SCAN_STREAMS]
            ds, gu, gw, gkd, gqd, gp, ggl = vjp((dos, ds))
            for n, (d, h) in enumerate(SCAN_STREAMS):
                du_ref[d, h, rows, :] = gu[n]
                dw_ref[d, h, rows, :] = gw[n]
                dkd_ref[d, h, rows, :] = gkd[n]
                dqd_ref[d, h, rows, :] = gqd[n]
                dp_ref[d, h, rows, :] = gp[n]
                dgl_ref[d, h, i] = ggl[n]
        for (d, h), dv in zip(SCAN_STREAMS, ds):
            ds_scr[d, h] = dv

    return _call_carrying(
        body, exch, name="gdn_scan_bwd",
        out_shape=(_sds((2, GH, T, HD)),) * 4 + (_sds((2, GH, T, CH)), _sds((2, GH, T // CH, 1, HD))),
        grid=(nb,), in_specs=[big, big, big, big, pm, gl_s, st, *do_specs], out_specs=(big, big, big, big, pm, gl_s),
        scratch=[pltpu.VMEM((2, GH, HD, HD), F32)], vmem=VMEM_BIG)(u, w, kd, qd, p, gl, states, do, do)


def _gout_fn(o0, o1, z, gw):
    return _rms(o0 + o1) * gw * _silu(z)


def _backward_latent(o_ref, L):
    nl = (o_ref.shape[1] - L) // CH
    return jnp.concatenate([o_ref[1, L + (nl - 1 - j) * CH:L + (nl - j) * CH, :] for j in range(nl)], axis=0)


def _gout_fwd(o, proj, gw, L):
    T = o.shape[2]
    N = T - L
    ob = pl.BlockSpec((2, None, T, HD), lambda h: (0, h, 0, 0))

    def body(o_ref, z_ref, gw_ref, y_ref):
        y_ref[...] = _gout_fn(o_ref[0, L:, :], _backward_latent(o_ref, L), z_ref[L:, :], gw_ref[...]).astype(BF16)

    return _call(body, name="gout_fwd", out_shape=_sds((N, GH * HD), BF16), grid=(GH,),
                 in_specs=[ob, pl.BlockSpec((T, HD), lambda h: (0, C_Z // HD + h)), pl.BlockSpec((1, HD), lambda h: (0, 0))],
                 out_specs=pl.BlockSpec((N, HD), lambda h: (0, h)), sem=("parallel",))(o, proj, gw)


def _gout_bwd(o, proj, gw, dy, dproj, L):
    T = o.shape[2]
    N = T - L
    ob = pl.BlockSpec((2, None, T, HD), lambda h: (0, h, 0, 0))

    def body(o_ref, z_ref, gw_ref, dy_ref, _, do_ref, dz_ref, dgw_ref):
        _, vjp = jax.vjp(_gout_fn, o_ref[0, L:, :], _backward_latent(o_ref, L), z_ref[L:, :], gw_ref[...])
        g0, _, gz, ggw = vjp(dy_ref[...])
        do_ref[:L, :] = jnp.zeros((L, HD), F32)
        do_ref[L:, :] = g0
        dz_ref[:L, :] = jnp.zeros((L, HD), BF16)
        dz_ref[L:, :] = gz.astype(BF16)

        @pl.when(pl.program_id(0) == 0)
        def _():
            dgw_ref[...] = jnp.zeros_like(dgw_ref)

        dgw_ref[...] += ggw

    zb = pl.BlockSpec((T, HD), lambda h: (0, C_Z // HD + h))
    return _call(body, name="gout_bwd", out_shape=(_sds((GH, T, HD)), _sds(dproj.shape, BF16), _sds((1, HD))),
                 grid=(GH,),
                 in_specs=[ob, zb, pl.BlockSpec((1, HD), lambda h: (0, 0)), pl.BlockSpec((N, HD), lambda h: (0, h)), ANYSPEC],
                 out_specs=(pl.BlockSpec((None, T, HD), lambda h: (h, 0, 0)), zb, pl.BlockSpec((1, HD), lambda h: (0, 0))),
                 aliases={4: 1}, sem=("arbitrary",))(o, proj, gw, dy, dproj)


def _merge_fn(pa, pd, ga, gd):
    return jax.nn.sigmoid(ga) * pa + jax.nn.sigmoid(gd) * pd


def _merge_fwd(pa, pd, proj, L, *, br=256):
    N = pa.shape[0]
    lb = L // br
    row = pl.BlockSpec((br, D), lambda i: (i, 0))

    def body(pa_ref, pd_ref, ga_ref, gd_ref, y_ref):
        y_ref[...] = _merge_fn(pa_ref[...], pd_ref[...], ga_ref[...], gd_ref[...]).astype(BF16)

    return _call(body, name="merge_fwd", out_shape=_sds((N, D), BF16), grid=(N // br,),
                 in_specs=[row, row, pl.BlockSpec((br, D), lambda i: (i + lb, C_GATE // D)),
                           pl.BlockSpec((br, D), lambda i: (i + lb, C_GATE // D + 1))],
                 out_specs=row, sem=("parallel",))(pa, pd, proj, proj)


def _merge_bwd(pa, pd, proj, dy, L, *, br=256):
    N = pa.shape[0]
    T = N + L
    lb = L // br
    lrow = pl.BlockSpec((br, D), lambda i: (jnp.maximum(i - lb, 0), 0))

    def body(pa_ref, pd_ref, ga_ref, gd_ref, dy_ref, dpa_ref, dpd_ref, dg_ref):
        lat = pl.program_id(0) >= lb
        _, vjp = jax.vjp(_merge_fn, pa_ref[...], pd_ref[...], ga_ref[...], gd_ref[...])
        gpa, gpd, gga, ggd = vjp(dy_ref[...])
        dpa_ref[...] = gpa.astype(BF16)
        dpd_ref[...] = gpd.astype(BF16)
        dg_ref[:, :D] = jnp.where(lat, gga, 0.0).astype(BF16)
        dg_ref[:, D:] = jnp.where(lat, ggd, 0.0).astype(BF16)

    return _call(body, name="merge_bwd", out_shape=(_sds((N, D), BF16), _sds((N, D), BF16), _sds((T, C_END), BF16)),
                 grid=(T // br,),
                 in_specs=[lrow, lrow, pl.BlockSpec((br, D), lambda i: (i, C_GATE // D)),
                           pl.BlockSpec((br, D), lambda i: (i, C_GATE // D + 1)), lrow],
                 out_specs=(lrow, lrow, pl.BlockSpec((br, 2 * D), lambda i: (i, C_GATE // (2 * D)))),
                 sem=("arbitrary",))(pa, pd, proj, proj, dy)


def _resid_fwd(x, m, mod, i_g, *, name, br=256):
    R = x.shape[0]
    row = pl.BlockSpec((br, D), lambda i: (i, 0))

    def body(x_ref, m_ref, mod_ref, o_ref):
        o_ref[...] = x_ref[...] + mod_ref[i_g:i_g + 1, :] * m_ref[...]

    return _call(body, name=name, out_shape=_sds((R, D)), grid=(R // br,),
                 in_specs=[row, row, pl.BlockSpec((6, D), lambda i: (0, 0))], out_specs=row,
                 sem=("parallel",))(x, m, mod)


def _resid_bwd(dx, m, mod, i_g, *, name, br=256):
    R = dx.shape[0]
    row = pl.BlockSpec((br, D), lambda i: (i, 0))
    vec = pl.BlockSpec((1, D), lambda i: (0, 0))

    def body(dx_ref, m_ref, mod_ref, dm_ref, dg_ref):
        dxv = dx_ref[...]
        dm_ref[...] = (dxv * mod_ref[i_g:i_g + 1, :]).astype(BF16)

        @pl.when(pl.program_id(0) == 0)
        def _():
            dg_ref[...] = jnp.zeros_like(dg_ref)

        dg_ref[...] += jnp.sum(dxv * m_ref[...], axis=0, keepdims=True)

    return _call(body, name=name, out_shape=(_sds((R, D), BF16), _sds((1, D))), grid=(R // br,),
                 in_specs=[row, row, pl.BlockSpec((6, D), lambda i: (0, 0))], out_specs=(row, vec),
                 sem=("arbitrary",))(dx, m, mod)


def _ffn_fn(shifts, ug, uv, wg, wv, bg, bv):
    down, up = shifts

    def conv(x, w, b):
        return down(x) * w[0:1, :] + x * w[1:2, :] + up(x) * w[2:3, :] + b

    return _silu(conv(ug, wg, bg)) * conv(uv, wv, bv)


def _ffn_fwd(up, cw, cb, *, bw=256):
    N = up.shape[0]
    shifts = _make_shift(((0, N),))
    nb = DFF // bw

    def body(ug, uv, wg, wv, bg, bv, a_ref):
        a_ref[...] = _ffn_fn(shifts, ug[...], uv[...], wg[...], wv[...], bg[...], bv[...]).astype(BF16)

    def col(rows, off):
        return pl.BlockSpec((rows, bw), lambda j: (0, j + off))

    return _call(body, name="ffn_fwd", out_shape=_sds((N, DFF), BF16), grid=(nb,),
                 in_specs=[col(N, 0), col(N, nb), col(3, 0), col(3, nb), col(1, 0), col(1, nb)],
                 out_specs=col(N, 0), sem=("parallel",), vmem=VMEM_BIG)(up, up, cw, cw, cb, cb)


def _ffn_bwd(up, cw, cb, da, *, bw=256):
    N = up.shape[0]
    shifts = _make_shift(((0, N),))
    nb = DFF // bw

    def body(ug, uv, wg, wv, bg, bv, da_ref, dug, duv, dwg, dwv, dbg, dbv):
        _, vjp = jax.vjp(functools.partial(_ffn_fn, shifts), ug[...], uv[...], wg[...], wv[...], bg[...], bv[...])
        g = vjp(da_ref[...])
        dug[...] = g[0].astype(BF16)
        duv[...] = g[1].astype(BF16)
        dwg[...], dwv[...], dbg[...], dbv[...] = g[2], g[3], g[4], g[5]

    def col(rows, off):
        return pl.BlockSpec((rows, bw), lambda j: (0, j + off))

    half = (_sds((N, DFF), BF16), _sds((N, DFF), BF16), _sds((3, DFF)), _sds((3, DFF)), _sds((1, DFF)), _sds((1, DFF)))
    dug, duv, dwg, dwv, dbg, dbv = _call(
        body, name="ffn_bwd", out_shape=half, grid=(nb,),
        in_specs=[col(N, 0), col(N, nb), col(3, 0), col(3, nb), col(1, 0), col(1, nb), col(N, 0)],
        out_specs=(col(N, 0), col(N, 0), col(3, 0), col(3, 0), col(1, 0), col(1, 0)),
        sem=("parallel",), vmem=VMEM_BIG)(up, up, cw, cw, cb, cb, da)
    return (jnp.concatenate([dug, duv], axis=1), jnp.concatenate([dwg, dwv], axis=1),
            jnp.concatenate([dbg, dbv], axis=1))


def _head_fn(x1, dn, g2, fw, tgt):
    y = _rms(x1 + g2 * dn) * fw
    err = y - tgt
    return 0.5 * jnp.sum(jnp.mean(err * err, axis=-1))


def _head(x1, dn, mod, fw, tgt, *, br=256):
    N = x1.shape[0]
    row = pl.BlockSpec((br, D), lambda i: (i, 0))
    vec = pl.BlockSpec((1, D), lambda i: (0, 0))
    one = pl.BlockSpec((1, HD), lambda i: (0, 0))

    def body(x1_ref, dn_ref, mod_ref, fw_ref, tgt_ref, loss_ref, dx_ref, ddn_ref, dg_ref, dfw_ref):
        loss, (gx, gdn, gg, gfw) = jax.value_and_grad(_head_fn, argnums=(0, 1, 2, 3))(
            x1_ref[...], dn_ref[...], mod_ref[5:6, :], fw_ref[...], tgt_ref[...])
        dx_ref[...] = gx
        ddn_ref[...] = gdn.astype(BF16)

        @pl.when(pl.program_id(0) == 0)
        def _():
            loss_ref[...] = jnp.zeros_like(loss_ref)
            dg_ref[...] = jnp.zeros_like(dg_ref)
            dfw_ref[...] = jnp.zeros_like(dfw_ref)

        loss_ref[...] += jnp.broadcast_to(loss, (1, HD))
        dg_ref[...] += gg
        dfw_ref[...] += gfw

    return _call(body, name="head", out_shape=(_sds((1, HD)), _sds((N, D)), _sds((N, D), BF16), _sds((1, D)), _sds((1, D))),
                 grid=(N // br,), in_specs=[row, row, pl.BlockSpec((6, D), lambda i: (0, 0)), vec, row],
                 out_specs=(one, row, row, vec, vec), sem=("arbitrary",))(x1, dn, mod, fw, tgt)


def _adamw(w, g, m, v, *, name):
    shape = w.shape
    cols = shape[-1]
    rows = max(1, math.prod(shape[:-1]))
    w2, g2, m2, v2 = (t.reshape(rows, cols) for t in (w, g, m, v))
    br = 256 if rows % 256 == 0 else rows
    c1 = 1.0 - B1 ** STEP
    c2 = 1.0 - B2 ** STEP

    def body(w_ref, g_ref, m_ref, v_ref, d_ref, nm_ref, nv_ref):
        gv = g_ref[...]
        nm = B1 * m_ref[...] + (1.0 - B1) * gv
        nv = B2 * v_ref[...] + (1.0 - B2) * (gv * gv)
        d_ref[...] = -LR * ((nm / c1) / (jnp.sqrt(nv / c2) + AEPS) + WD * w_ref[...])
        nm_ref[...] = nm
        nv_ref[...] = nv

    blk = pl.BlockSpec((br, cols), lambda i: (i, 0))
    outs = _call(body, name=name, out_shape=(_sds((rows, cols)),) * 3, grid=(rows // br,),
                 in_specs=[blk] * 4, out_specs=(blk,) * 3, sem=("parallel",))(w2, g2, m2, v2)
    return tuple(t.reshape(shape) for t in outs)


def _adamw_many(items, *, name):
    k = len(items)
    shapes = [w.shape for w, _, _, _ in items]
    flat = [t.reshape(max(1, math.prod(t.shape[:-1])), t.shape[-1]) for it in items for t in it]
    c1 = 1.0 - B1 ** STEP
    c2 = 1.0 - B2 ** STEP

    def body(*refs):
        ins, outs = refs[:4 * k], refs[4 * k:]
        for i in range(k):
            w_ref, g_ref, m_ref, v_ref = ins[4 * i:4 * i + 4]
            gv = g_ref[...]
            nm = B1 * m_ref[...] + (1.0 - B1) * gv
            nv = B2 * v_ref[...] + (1.0 - B2) * (gv * gv)
            outs[3 * i][...] = -LR * ((nm / c1) / (jnp.sqrt(nv / c2) + AEPS) + WD * w_ref[...])
            outs[3 * i + 1][...] = nm
            outs[3 * i + 2][...] = nv

    res = _call(body, name=name, out_shape=tuple(_sds(flat[4 * i].shape) for i in range(k) for _ in range(3)))(*flat)
    return [tuple(res[3 * i + j].reshape(shapes[i]) for j in range(3)) for i in range(k)]


def _rope_tables(N, L):
    t = jnp.arange(N)
    pos = jnp.stack([(t // GRID_W).astype(F32), (t % GRID_W).astype(F32)], axis=1)
    inv = ROPE_THETA ** (-jnp.arange(0, HD // 2, 2, dtype=F32) / (HD // 2))
    ang = pos[:, :, None] * inv[None, None, :]
    cos = jnp.broadcast_to(jnp.cos(ang)[:, :, None, :], (N, 2, 2, HD // 4)).reshape(N, HD)
    sin = jnp.broadcast_to(jnp.sin(ang)[:, :, None, :], (N, 2, 2, HD // 4))
    sin = (sin * jnp.array([-1.0, 1.0], F32)[None, None, :, None]).reshape(N, HD)
    cos = jnp.concatenate([jnp.ones((L, HD), F32), cos], axis=0)
    sin = jnp.concatenate([jnp.zeros((L, HD), F32), sin], axis=0)
    return cos, sin


def _pad_lanes(v, off=0):
    return jnp.zeros((1, HD), F32).at[0, off:off + v.shape[0]].set(v)


def _local_step(x, ctx, tgt, mod_lat, mod_ctx, w_in, shards, small):
    N, L = x.shape[0], ctx.shape[0]
    T = N + L
    bounds = ((0, L), (L, T))
    qw, kw, gw = small["q_norm_w"], small["k_norm_w"], small["gdn_norm_w"]
    conv_w, ffn_w, ffn_b, fnw = small["conv_qkv_w"], small["ffn_conv_w"], small["ffn_conv_b"], small["final_norm_w"]
    alog = _pad_lanes(small["a_log"].reshape(-1), 2 * GH)
    dtb = _pad_lanes(small["dt_bias"].reshape(-1), 2 * GH)
    cos, sin = _rope_tables(N, L)
    bt = T
    bnl = 256 if N % 1024 else 1024

    h1 = _normmod_fwd(ctx, mod_ctx, 0, 1, name="normmod_ctx", out_rows=T)
    h1 = _normmod_fwd(x, mod_lat, 0, 1, name="normmod_x", off=L, into=h1)
    proj = _mm(h1, w_in, name="mm_in", M=T, N=C_END, K=D, tb=True, bm=bt, bn=1024)
    aq, ak, av = _aprep_fwd(proj, cos, sin, qw, kw)
    (attn, attn32, lse), (up_g,) = _attn_fwd(aq, ak, av, L, _GatherTwoLevel([shards["w_up"]]))
    gq = _gprep_fwd(proj, conv_w, 0, bounds)
    gk = _gprep_fwd(proj, conv_w, 1, bounds)
    gv = _gprep_fwd(proj, conv_w, 2, bounds)
    bl = _bl_fwd(proj, alog, dtb)
    intra, (down_g, pa_g, pd_g, out_g) = _intra_fwd(
        gq, gk, gv, bl, L, _GatherTwoLevel([shards[n] for n in ("w_down", "w_pa", "w_pd", "w_out")]))
    w_up, w_down = up_g.reshape(2 * DFF, D), down_g.reshape(DFF, D)
    w_pa, w_pd, w_out = pa_g.reshape(D, D), pd_g.reshape(D, D), out_g.reshape(D, D)
    xinv, intra = intra[6], intra[:6]
    o, states = _scan_fwd(*intra, L)
    gdn = _gout_fwd(o, proj, gw, L)
    pa = _mm(attn, w_pa, name="mm_pa", M=N, N=D, K=D, bm=bnl)
    pd = _mm(gdn, w_pd, name="mm_pd", M=N, N=D, K=D, bm=bnl)
    y = _merge_fwd(pa, pd, proj, L)
    m = _mm(y, w_out, name="mm_out", M=N, N=D, K=D, bm=bnl)
    x1 = _resid_fwd(x, m, mod_lat, 2, name="resid1")
    h2 = _normmod_fwd(x1, mod_lat, 3, 4, name="normmod_x1")
    up = _mm(h2, w_up, name="mm_up", M=N, N=2 * DFF, K=D, tb=True, bm=bnl, bn=2 * DFF // 4)
    a = _ffn_fwd(up, ffn_w, ffn_b)
    dn = _mm(a, w_down, name="mm_down", M=N, N=D, K=DFF, bm=bnl)
    loss, dx2, ddn, dg2, dfnw = _head(x1, dn, mod_lat, fnw, tgt)

    da = _mm(ddn, w_down, name="mm_down_dx", M=N, N=DFF, K=D, tb=True, bm=bnl, bn=DFF // 2)
    g_down = _mm(a, ddn, name="mm_down_dw", M=DFF, N=D, K=N, ta=True, bm=DFF // 2, out_dtype=BF16)
    dup, d_ffn_w, d_ffn_b = _ffn_bwd(up, ffn_w, ffn_b, da)
    dh2 = _mm(dup, w_up, name="mm_up_dx", M=N, N=D, K=2 * DFF, bm=bnl, bk=2 * DFF // 4)
    g_up = _mm_deep_dw(dup, h2, name="mm_up_dw", M=2 * DFF, N=D, K=N, bm=2 * DFF // 4, out_dtype=BF16)
    dx1, dsh2, dsc2 = _normmod_bwd(x1, mod_lat, 3, 4, dh2, 0, dx2, name="normmod_x1_bwd")
    dm, dg1 = _resid_bwd(dx1, m, mod_lat, 2, name="resid1_bwd")
    dy = _mm(dm, w_out, name="mm_out_dx", M=N, N=D, K=D, tb=True, bm=bnl)
    g_out = _mm(y, dm, name="mm_out_dw", M=D, N=D, K=N, ta=True, out_dtype=BF16)
    dpa, dpd, dproj = _merge_bwd(pa, pd, proj, dy, L)
    dattn = _mm(dpa, w_pa, name="mm_pa_dx", M=N, N=D, K=D, tb=True, bm=bnl)
    g_pa = _mm(attn, dpa, name="mm_pa_dw", M=D, N=D, K=N, ta=True, out_dtype=BF16)
    dgdn = _mm(dpd, w_pd, name="mm_pd_dx", M=N, N=D, K=D, tb=True, bm=bnl)
    g_pd = _mm(gdn, dpd, name="mm_pd_dw", M=D, N=D, K=N, ta=True, out_dtype=BF16)
    do, dproj, dgw = _gout_bwd(o, proj, gw, dgdn, dproj, L)
    cts, recv_a = _scan_bwd(*intra, states, do, L, _Exchange(
        [g_out.reshape(NDEV, D // NDEV, D), g_pa.reshape(NDEV, D // NDEV, D), g_pd.reshape(NDEV, D // NDEV, D)], True))
    (dgq, dgk, dgv, dbl), recv_b = _intra_bwd(gq, gk, gv, bl, xinv, cts, L, _Exchange(
        [g_up.reshape(NDEV, 2 * DFF // NDEV, D)], True))
    dproj, dwq = _gprep_bwd(proj, conv_w, 0, bounds, dgq, dproj)
    dproj, dwk = _gprep_bwd(proj, conv_w, 1, bounds, dgk, dproj)
    dproj, dwv = _gprep_bwd(proj, conv_w, 2, bounds, dgv, dproj)
    dproj, dalog, ddtb = _bl_bwd(proj, alog, dtb, dbl, dproj)
    (daq_h, dak_h, dav_h), recv_c = _attn_bwd(aq, ak, av, attn32, lse, dattn, L, _Exchange(
        [g_down.reshape(NDEV, DFF // NDEV, D)], True))
    recv = dict(zip(("w_out", "w_pa", "w_pd", "w_up", "w_down"), recv_a + recv_b + recv_c))
    dproj, dqw, dkw = _aprep_bwd(proj, cos, sin, qw, kw, daq_h, dak_h, dav_h, dproj, L)
    g_in = _mm_deep_dw(dproj, h1, name="mm_in_dw", M=C_END, N=D, K=T, bm=1024, out_dtype=BF16)
    *pending, token = _scatter_start(g_in, None, (0, D // 2), (), name="scatter_g_in_a_start")
    dh1 = _mm_deep(dproj, w_in, name="mm_in_dx", M=T, N=D, K=C_END, bk=1024, after=(token,))
    grad_x, dsh1, dsc1 = _normmod_bwd(x, mod_lat, 0, 1, dh1, L, dx1, name="normmod_x_bwd")
    _, dcsh1, dcsc1 = _normmod_bwd(ctx, mod_ctx, 0, 1, dh1, 0, None, name="normmod_ctx_bwd")

    z1 = jnp.zeros((1, D), F32)
    dmod_lat = jnp.concatenate([dsh1, dsc1, dg1, dsh2, dsc2, dg2], axis=0)
    dmod_ctx = jnp.concatenate([dcsh1, dcsc1, z1, z1, z1, z1], axis=0)
    gsmall = {
        "q_norm_w": dqw, "k_norm_w": dkw, "gdn_norm_w": dgw,
        "conv_qkv_w": jnp.concatenate([dwq, dwk, dwv], axis=1),
        "a_log": dalog[0, 2 * GH:4 * GH], "dt_bias": ddtb[0, 2 * GH:4 * GH],
        "ffn_conv_w": d_ffn_w, "ffn_conv_b": d_ffn_b, "final_norm_w": dfnw,
    }
    return loss[0, 0], grad_x, pending, recv, dmod_lat, dmod_ctx, gsmall


HBM = pl.BlockSpec(memory_space=pltpu.HBM)
ANYSPEC = pl.BlockSpec(memory_space=pl.ANY)


def _position():
    x, y, c = lax.axis_index("x"), lax.axis_index("y"), lax.axis_index("c")
    return x, y, c, 4 * x + 2 * y + c


def _peer(x, y, c, k):
    px = 1 - x if k & 4 else x
    py = 1 - y if k & 2 else y
    pc = 1 - c if k & 1 else c
    return (px, py, pc), 4 * px + 2 * py + pc


def _exchange(arrs, *, name, scatter):
    exch = _Exchange(arrs, scatter)
    n = exch.n

    def body(*refs):
        ins, outs, sems = refs[:n], refs[n:2 * n], refs[2 * n:]
        exch.start(ins, outs, sems)
        exch.finish(ins, outs, sems)

    outs = pl.pallas_call(body, name=name, out_shape=exch.out_shape, in_specs=[HBM] * n, out_specs=(HBM,) * n,
                          scratch_shapes=exch.scratch,
                          compiler_params=pltpu.CompilerParams(has_side_effects=True))(*arrs)
    return list(outs)


class _Exchange:
    def __init__(self, arrs, scatter):
        self.arrs, self.scatter, self.n = list(arrs), scatter, len(arrs)
        self.out_shape = tuple(_sds(a.shape if scatter else (NDEV,) + a.shape, a.dtype) for a in arrs)
        self.scratch = [pltpu.SemaphoreType.DMA((self.n, NDEV - 1)), pltpu.SemaphoreType.DMA((self.n, NDEV - 1)),
                        pltpu.SemaphoreType.DMA((self.n,))]

    def _copies(self, ins, outs, sems):
        send, recv, loc = sems
        x, y, c, me = _position()
        local = [pltpu.make_async_copy(ins[a].at[me] if self.scatter else ins[a], outs[a].at[me], loc.at[a])
                 for a in range(self.n)]
        remote = []
        for k in range(1, NDEV):
            peer, pid = _peer(x, y, c, k)
            for a in range(self.n):
                src = ins[a].at[pid] if self.scatter else ins[a]
                remote.append(pltpu.make_async_remote_copy(
                    src_ref=src, dst_ref=outs[a].at[me], send_sem=send.at[a, k - 1], recv_sem=recv.at[a, k - 1],
                    device_id=peer, device_id_type=MESH))
        return local, remote

    def start(self, ins, outs, sems):
        local, remote = self._copies(ins, outs, sems)
        for cp in local + remote:
            cp.start()

    def finish(self, ins, outs, sems):
        local, remote = self._copies(ins, outs, sems)
        for cp in remote:
            cp.wait()
        for cp in local:
            cp.wait()


class _GatherTwoLevel:
    scatter = False

    def __init__(self, arrs):
        self.arrs, self.n = list(arrs), len(arrs)
        self.out_shape = tuple(_sds((NDEV,) + a.shape, a.dtype) for a in arrs)
        self.scratch = [pltpu.SemaphoreType.DMA((self.n, NDEV - 1)), pltpu.SemaphoreType.DMA((self.n, NDEV - 1)),
                        pltpu.SemaphoreType.DMA((self.n,))]

    def _parts(self, ins, outs, sems):
        send, recv, loc = sems
        x, y, c, _ = _position()
        me, sibling = (x, y, c), (x, y, 1 - c)
        chips = [(1 - x, y), (x, 1 - y), (1 - x, 1 - y)]
        parts = []
        for a in range(self.n):
            slot = lambda px, py, pc, a=a: outs[a].at[4 * px + 2 * py + pc]

            def copy(k, owner, to, src=None, a=a, slot=slot):
                return pltpu.make_async_remote_copy(
                    src_ref=slot(*owner) if src is None else src, dst_ref=slot(*owner), send_sem=send.at[a, k],
                    recv_sem=recv.at[a, k], device_id=to, device_id_type=MESH)

            parts.append(dict(
                mine=pltpu.make_async_copy(ins[a], slot(*me), loc.at[a]),
                first=[copy(0, me, sibling, src=ins[a])] + [copy(1 + j, me, (*ch, c), src=ins[a]) for j, ch in enumerate(chips)],
                arrive=[copy(1 + j, (*ch, c), me) for j, ch in enumerate(chips)],
                passed=[copy(4 + j, (*ch, c), sibling) for j, ch in enumerate(chips)],
                rest=[copy(0, sibling, me)] + [copy(4 + j, (*ch, 1 - c), me) for j, ch in enumerate(chips)]))
        return parts

    def start(self, ins, outs, sems):
        for p in self._parts(ins, outs, sems):
            p["mine"].start()
            for cp in p["first"]:
                cp.start()

    def middle(self, ins, outs, sems):
        for p in self._parts(ins, outs, sems):
            for got, fwd in zip(p["arrive"], p["passed"]):
                got.wait_recv()
                fwd.start()

    def finish(self, ins, outs, sems):
        for p in self._parts(ins, outs, sems):
            for cp in p["rest"]:
                cp.wait_recv()
            for cp in p["first"] + p["passed"]:
                cp.wait_send()
            p["mine"].wait()


def _gather_two_level(blocks, *, name):
    exch = _GatherTwoLevel(blocks)
    n = exch.n

    def body(*refs):
        ins, outs, sems = refs[:n], refs[n:2 * n], refs[2 * n:]
        exch.start(ins, outs, sems)
        exch.middle(ins, outs, sems)
        exch.finish(ins, outs, sems)

    outs = pl.pallas_call(body, name=name, out_shape=exch.out_shape, in_specs=[HBM] * n, out_specs=(HBM,) * n,
                          scratch_shapes=exch.scratch,
                          compiler_params=pltpu.CompilerParams(has_side_effects=True))(*blocks)
    return list(outs)


SEM = pl.BlockSpec(memory_space=pltpu.SEMAPHORE)


SHARD_ROWS = W_END // NDEV
RUNS = ((0, W_QKV, C_KV), (W_QKV, W_AQ - W_QKV, C_QKV), (W_AQ, W_Z - W_AQ, C_AQ), (W_Z, W_END - W_Z, C_Z))


ROW_TILE = 8
SLOT_ROWS = -(-SHARD_ROWS // ROW_TILE) * ROW_TILE


def _shard_pieces(d):
    lo, hi = d * SHARD_ROWS, (d + 1) * SHARD_ROWS
    lead = lo % ROW_TILE
    pieces = []
    for first, rows, padded in RUNS:
        a, b = max(lo, first), min(hi, first + rows)
        if a < b:
            pieces.append([a - lo + lead, b - a, padded + a - first])
    pieces[0] = [0, pieces[0][1] + lead, pieces[0][2] - lead]
    pieces[-1][1] = SLOT_ROWS - pieces[-1][0]
    assert all(v % ROW_TILE == 0 for p in pieces for v in p) and all(p[2] + p[1] <= C_END for p in pieces)
    return pieces


def _scatter_send(src_ref, land_ref, send_sems, recv_sems, cols):
    _, _, _, me = _position()
    for d in range(NDEV):
        @pl.when(me != d)
        def _():
            k = jnp.bitwise_xor(me, d)
            peer = tuple(jnp.int32((d >> s) & 1) for s in (2, 1, 0))
            for off, rows, padded in _shard_pieces(d):
                pltpu.make_async_remote_copy(
                    src_ref=src_ref.at[pl.ds(padded, rows), pl.ds(*cols)],
                    dst_ref=land_ref.at[me].at[pl.ds(off, rows), pl.ds(*cols)], send_sem=send_sems.at[k - 1],
                    recv_sem=recv_sems.at[k - 1], device_id=peer, device_id_type=MESH).start()


def _scatter_whole(src_ref, land_ref, send_sems, recv_sems, cols):
    x, y, c, me = _position()
    span = (slice(None), pl.ds(*cols))
    copies = []
    for k in range(1, NDEV):
        peer, _ = _peer(x, y, c, k)
        copies.append(pltpu.make_async_remote_copy(
            src_ref=src_ref.at[pl.ds(0, SLOT_ROWS)].at[span], dst_ref=land_ref.at[me].at[span],
            send_sem=send_sems.at[k - 1], recv_sem=recv_sems.at[k - 1], device_id=peer, device_id_type=MESH))
    return copies


SPLIT_EFFECT = pltpu.SideEffectType.DATAFLOW_SIDE_EFFECTING


def _scatter_start(parts, land, cols, after, *, name):
    na = len(after)
    if land is None:
        land = lax.empty((NDEV, SLOT_ROWS, D), parts.dtype)

    def body(src_ref, land_ref, *rest):
        send_sems, recv_sems, _, _, token = rest[na:]
        _scatter_send(src_ref, land_ref, send_sems, recv_sems, cols)
        token[...] = jnp.zeros_like(token)

    return pl.pallas_call(
        body, name=name,
        out_shape=(pltpu.SemaphoreType.DMA((NDEV - 1,)), pltpu.SemaphoreType.DMA((NDEV - 1,)),
                   pltpu.HBM(parts.shape, parts.dtype), pltpu.HBM(land.shape, land.dtype), _sds((8, HD))),
        in_specs=(HBM, HBM) + (pl.BlockSpec(memory_space=pl.ANY),) * na,
        out_specs=(SEM, SEM, HBM, HBM, pl.BlockSpec(memory_space=pltpu.VMEM)),
        input_output_aliases={0: 2, 1: 3}, compiler_params=pltpu.CompilerParams(has_side_effects=SPLIT_EFFECT),
    )(pltpu.with_memory_space_constraint(parts, pltpu.HBM), pltpu.with_memory_space_constraint(land, pltpu.HBM), *after)


def _scatter_wait(send_sems, recv_sems, src_thru, land_thru, cols, after, *, name):
    na = len(after)

    def body(src_ref, land_ref, send_sems, recv_sems, *rest):
        for cp in _scatter_whole(src_ref, land_ref, send_sems, recv_sems, cols):
            cp.wait_send()
            cp.wait_recv()

    return pl.pallas_call(
        body, name=name,
        out_shape=(pltpu.HBM(src_thru.shape, src_thru.dtype), pltpu.HBM(land_thru.shape, land_thru.dtype)),
        in_specs=(HBM, HBM, SEM, SEM) + (pl.BlockSpec(memory_space=pl.ANY),) * na, out_specs=(HBM, HBM),
        input_output_aliases={0: 0, 1: 1}, compiler_params=pltpu.CompilerParams(has_side_effects=SPLIT_EFFECT),
    )(src_thru, land_thru, send_sems, recv_sems, *after)


def _cast_bf16(ws, *, name):
    k = len(ws)

    def body(*refs):
        for w_ref, o_ref in zip(refs[:k], refs[k:]):
            o_ref[...] = w_ref[...].astype(BF16)

    return _call(body, name=name, out_shape=tuple(_sds(w.shape, BF16) for w in ws), vmem=VMEM_BIG)(*ws)


def _sum_slots(a, *, name):
    _, R, C = a.shape

    def body(a_ref, o_ref):
        s = a_ref[0]
        for d in range(1, NDEV):
            s = s + a_ref[d]
        o_ref[...] = s

    return _call(body, name=name, out_shape=_sds((R, C)))(a)


MODROWS = 16


def _mod_fwd(c9, w, b):
    cols = w.shape[1]

    def body(c_ref, w_ref, b_ref, o_ref):
        o_ref[...] = _nn(_silu(c_ref[...]), w_ref[...]) + b_ref[...]

    return _call(body, name="mod_fwd", out_shape=_sds((MODROWS, cols)))(c9, w, b)


def _mod_bwd(c9, dmy, dall, w):
    cols = w.shape[1]

    def body(c_ref, dmy_ref, dall_ref, w_ref, gw_ref, gb_ref, cp_ref):
        sc = _silu(c_ref[...])
        rows = lax.broadcasted_iota(jnp.int32, (MODROWS, 1), 0)
        d = dmy_ref[...]
        d_ctx = jnp.where(rows == NDEV, d, 0.0)
        sc_ctx = jnp.where(rows == NDEV, sc, 0.0)
        outer = lax.dot_general(sc_ctx, d_ctx, (((0,), (0,)), ((), ())), precision=HI, preferred_element_type=F32)
        gw_ref[...] = _tn(jnp.where(rows < NDEV, sc, 0.0), jnp.where(rows < NDEV, d, 0.0)) + outer
        gb_ref[...] = jnp.sum(dall_ref[...], axis=0, keepdims=True)
        cp_ref[...] = jnp.sum(_nt(d_ctx, w_ref[...]), axis=0, keepdims=True)

    return _call(body, name="mod_bwd", out_shape=(_sds((D, cols)), _sds((1, 6 * D)), _sds((1, D))),
                 vmem=VMEM_BIG)(c9, dmy, dall, w)


def _cctx_finish(parts, c_ctx, after):
    VM = pl.BlockSpec(memory_space=pltpu.VMEM)

    def body(p_ref, c_ref, *rest):
        o_ref = rest[-1]
        s = p_ref[0]
        for d in range(1, NDEV):
            s = s + p_ref[d]
        _, vjp = jax.vjp(_silu, c_ref[...])
        o_ref[...] = vjp(s)[0]

    return _call(body, name="cctx_finish", out_shape=_sds((1, D)),
                 in_specs=[VM, VM] + [pl.BlockSpec(memory_space=pl.ANY)] * len(after))(parts, c_ctx, *after)


def _adamw_recv(w, recv, m, v, *, name, own=None):
    rows, cols = w.shape
    slot_rows = recv.shape[1]
    lead = slot_rows - rows
    assert rows % ROW_TILE in (0, lead)
    bc = 256
    c1 = 1.0 - B1 ** STEP
    c2 = 1.0 - B2 ** STEP
    has_own = own is not None

    def body(w_ref, r_ref, m_ref, v_ref, *rest):
        g_ref, d_ref, nm_ref, nv_ref = rest[-4:]
        me = _position()[3]

        def slot(d):
            return jnp.where(me == d, rest[0][...], r_ref[d]) if has_own else r_ref[d]

        gv = slot(0).astype(F32)
        for d in range(1, NDEV):
            gv = gv + slot(d).astype(F32)
        if lead:
            gv = jnp.where((me * rows) % ROW_TILE == 0, gv[:rows], gv[lead:])
        nm = B1 * m_ref[...] + (1.0 - B1) * gv
        nv = B2 * v_ref[...] + (1.0 - B2) * (gv * gv)
        g_ref[...] = gv
        d_ref[...] = -LR * ((nm / c1) / (jnp.sqrt(nv / c2) + AEPS) + WD * w_ref[...])
        nm_ref[...] = nm
        nv_ref[...] = nv

    blk = pl.BlockSpec((rows, bc), lambda j: (0, j))
    return _call(body, name=name, out_shape=(_sds((rows, cols)),) * 4, grid=(cols // bc,),
                 in_specs=[blk, pl.BlockSpec((NDEV, slot_rows, bc), lambda j: (0, 0, j)), blk, blk]
                 + [pl.BlockSpec((slot_rows, bc), lambda j: (0, j))] * has_own,
                 out_specs=(blk,) * 4, sem=("parallel",), vmem=VMEM_BIG)(w, recv, m, v, *([own] if has_own else []))


P_LAT, P_CTX, P_FNW, P_FFNB, P_CONV, P_FFNW, P_MISC, P_ROWS = 0, 8, 16, 24, 32, 48, 72, 80


def _rows_of(v, nrows):
    flat = v.reshape(-1)
    return jnp.pad(flat, (0, nrows * D - flat.shape[0])).reshape(nrows, D)


def _by_columns(g):
    n, r, c = g.shape
    return jnp.transpose(g, (1, 0, 2)).reshape(r, n * c)


def kernel(x, c, ctx, c_ctx, w_mod, b_mod, w_in, q_norm_w, k_norm_w, conv_qkv_w, a_log, dt_bias, gdn_norm_w, w_pa, w_pd, w_out, w_up, ffn_conv_w, ffn_conv_b, w_down, final_norm_w, loss_target, m_c_ctx, m_w_mod, m_b_mod, m_w_in, m_q_norm_w, m_k_norm_w, m_conv_qkv_w, m_a_log, m_dt_bias, m_gdn_norm_w, m_w_pa, m_w_pd, m_w_out, m_w_up, m_ffn_conv_w, m_ffn_conv_b, m_w_down, m_final_norm_w, v_c_ctx, v_w_mod, v_b_mod, v_w_in, v_q_norm_w, v_k_norm_w, v_conv_qkv_w, v_a_log, v_dt_bias, v_gdn_norm_w, v_w_pa, v_w_pd, v_w_out, v_w_up, v_ffn_conv_w, v_ffn_conv_b, v_w_down, v_final_norm_w):
    _, _, _, me = _position()
    mcols = w_mod.shape[2]

    transposed = ("w_in", "w_up")
    big = {"w_in": w_in[0].T, "w_pa": w_pa[0], "w_pd": w_pd[0], "w_out": w_out[0], "w_up": w_up[0].T, "w_down": w_down[0]}
    names = list(big)
    shards = dict(zip(names, _cast_bf16([big[n] for n in names], name="cast_weights")))
    w_in_g, c_all, conv_g, ffnw_g = _gather_two_level([shards["w_in"], c, conv_qkv_w[0], ffn_conv_w[0]],
                                                      name="gather_w_in")
    w_in_full = w_in_g.reshape(W_END, D)
    w_in_pad = _pad_columns(w_in_full)

    c9 = jnp.concatenate([c_all.reshape(NDEV, D), jnp.pad(c_ctx[None], ((0, MODROWS - NDEV - 1), (0, 0)))], axis=0)
    b_loc = lax.dynamic_slice(b_mod, (0, me * mcols), (1, mcols))
    mod_all, = _exchange([_mod_fwd(c9, w_mod[0], b_loc)], name="gather_mod", scatter=False)
    mod_lat = lax.dynamic_index_in_dim(mod_all, me, axis=1, keepdims=False).reshape(6, D)
    mod_ctx = mod_all[:, NDEV, :].reshape(6, D)

    small = {"q_norm_w": q_norm_w, "k_norm_w": k_norm_w, "gdn_norm_w": gdn_norm_w, "a_log": a_log, "dt_bias": dt_bias,
             "conv_qkv_w": _by_columns(conv_g), "ffn_conv_w": _by_columns(ffnw_g), "ffn_conv_b": ffn_conv_b,
             "final_norm_w": final_norm_w[None]}
    loss_me, grad_x, pending_in, recv, dmod_lat, dmod_ctx, gs = _local_step(
        x[0], ctx[0], loss_target[0], mod_lat, mod_ctx, w_in_pad, shards, small)

    moments = {"w_in": (m_w_in, v_w_in), "w_pa": (m_w_pa, v_w_pa), "w_pd": (m_w_pd, v_w_pd),
               "w_out": (m_w_out, v_w_out), "w_up": (m_w_up, v_w_up), "w_down": (m_w_down, v_w_down)}
    res = {}
    def finish(n, outs):
        return tuple((t.T if n in transposed else t)[None] for t in outs)

    def moment(t, n):
        return t[0].T if n in transposed else t[0]

    for n in recv:
        res[n] = finish(n, _adamw_recv(big[n], recv[n], moment(moments[n][0], n), moment(moments[n][1], n),
                                       name="adamw_" + n))

    misc = jnp.concatenate([gs["q_norm_w"][0], gs["k_norm_w"][0], gs["gdn_norm_w"][0], gs["a_log"], gs["dt_bias"],
                            loss_me[None]])
    pack = jnp.concatenate([_rows_of(dmod_lat, P_CTX - P_LAT), _rows_of(dmod_ctx, P_FNW - P_CTX),
                            _rows_of(gs["final_norm_w"], P_FFNB - P_FNW), _rows_of(gs["ffn_conv_b"], P_CONV - P_FFNB),
                            _rows_of(gs["conv_qkv_w"], P_FFNW - P_CONV), _rows_of(gs["ffn_conv_w"], P_MISC - P_FFNW),
                            _rows_of(misc, P_ROWS - P_MISC)], axis=0)
    pack_all, = _exchange([pack], name="gather_pack", scatter=False)
    tot = _sum_slots(pack_all, name="sum_pack")
    dall = jnp.concatenate([pack_all[:, P_LAT:P_LAT + 6, :].reshape(NDEV, 6 * D),
                            jnp.pad(tot[P_CTX:P_CTX + 6].reshape(1, 6 * D), ((0, MODROWS - NDEV - 1), (0, 0)))], axis=0)
    dmy = lax.dynamic_slice(dall, (0, me * mcols), (MODROWS, mcols))
    g_w_mod, g_b_mod, cpart = _mod_bwd(c9, dmy, dall, w_mod[0])
    cparts, = _exchange([cpart], name="gather_cctx", scatter=False)
    sems_a, land = pending_in[:2], pending_in[3]
    *sems_b, g_in_thru, land, token_b = _scatter_start(pending_in[2], land, (D // 2, D // 2), (cparts,),
                                                       name="scatter_g_in_b_start")
    g_c_ctx = _cctx_finish(cparts, c_ctx[None], (token_b,))[0]

    nconv, nffn = 3 * GH * HD, 2 * DFF
    conv_tot = tot[P_CONV:P_FFNW].reshape(-1)[:3 * nconv].reshape(3, nconv)
    ffnw_tot = tot[P_FFNW:P_MISC].reshape(-1)[:3 * nffn].reshape(3, nffn)
    mrow = tot[P_MISC]
    grads = {
        "c_ctx": g_c_ctx, "w_mod": g_w_mod[None], "b_mod": g_b_mod,
        "q_norm_w": mrow[None, 0:HD], "k_norm_w": mrow[None, HD:2 * HD], "gdn_norm_w": mrow[None, 2 * HD:3 * HD],
        "conv_qkv_w": lax.dynamic_slice(conv_tot, (0, me * (nconv // NDEV)), (3, nconv // NDEV))[None],
        "a_log": mrow[3 * HD:3 * HD + 2 * GH].reshape(1, 2, GH),
        "dt_bias": mrow[3 * HD + 2 * GH:3 * HD + 4 * GH].reshape(1, 2, GH),
        "ffn_conv_w": lax.dynamic_slice(ffnw_tot, (0, me * (nffn // NDEV)), (3, nffn // NDEV))[None],
        "ffn_conv_b": tot[P_FFNB:P_CONV].reshape(-1)[:nffn][None],
        "final_norm_w": tot[P_FNW],
    }
    loss = mrow[3 * HD + 4 * GH]
    given = {"c_ctx": (c_ctx, m_c_ctx, v_c_ctx), "w_mod": (w_mod, m_w_mod, v_w_mod), "b_mod": (b_mod, m_b_mod, v_b_mod),
             "q_norm_w": (q_norm_w, m_q_norm_w, v_q_norm_w), "k_norm_w": (k_norm_w, m_k_norm_w, v_k_norm_w),
             "conv_qkv_w": (conv_qkv_w, m_conv_qkv_w, v_conv_qkv_w), "a_log": (a_log, m_a_log, v_a_log),
             "dt_bias": (dt_bias, m_dt_bias, v_dt_bias), "gdn_norm_w": (gdn_norm_w, m_gdn_norm_w, v_gdn_norm_w),
             "ffn_conv_w": (ffn_conv_w, m_ffn_conv_w, v_ffn_conv_w), "ffn_conv_b": (ffn_conv_b, m_ffn_conv_b, v_ffn_conv_b),
             "final_norm_w": (final_norm_w, m_final_norm_w, v_final_norm_w)}
    res["w_mod"] = (grads["w_mod"],) + _adamw(w_mod, grads["w_mod"], m_w_mod, v_w_mod, name="adamw_w_mod")
    small_names = [n for n in given if n != "w_mod"]
    updates = _adamw_many([(given[n][0], grads[n], given[n][1], given[n][2]) for n in small_names], name="adamw_small")
    for n, upd in zip(small_names, updates):
        res[n] = (grads[n],) + upd

    first = me * SHARD_ROWS
    own_in = lax.dynamic_slice(_unpad_columns(g_in_thru), (first - first % ROW_TILE, 0), (SLOT_ROWS, D))
    mine = (big["w_in"], moment(m_w_in, "w_in"), moment(v_w_in, "w_in"))
    g_in_thru, land = _scatter_wait(*sems_a, g_in_thru, land, (0, D // 2), [res[n][1] for n in res] + [own_in, *mine],
                                    name="scatter_g_in_a_wait")
    _, land = _scatter_wait(*sems_b, g_in_thru, land, (D // 2, D // 2), (), name="scatter_g_in_b_wait")
    res["w_in"] = finish("w_in", _adamw_recv(mine[0], land, mine[1], mine[2], name="adamw_w_in", own=own_in))

    order = ["c_ctx", "w_mod", "b_mod", "w_in", "q_norm_w", "k_norm_w", "conv_qkv_w", "a_log", "dt_bias", "gdn_norm_w",
             "w_pa", "w_pd", "w_out", "w_up", "ffn_conv_w", "ffn_conv_b", "w_down", "final_norm_w"]
    return (loss, grad_x[None], *[res[n][0] for n in order], *[res[n][1] for n in order],
            *[res[n][2] for n in order], *[res[n][3] for n in order])
```
